```python
import jax, jax.numpy as jnp
from jax import lax
import numpy as np

D_MODEL = 2048
BATCH = 8
SEQ = 2048
DEPTH = 1

CTX_LEN = 256
GRID_W = 64
Q_BLOCK = 128
ROPE_THETA = 10000.0
NORM_EPS = 1e-6

MLA_HEADS = 8
MLA_Q_LORA = 768
MLA_KV_LORA = 512
MLA_NOPE = 128
MLA_ROPE = 64
MLA_V = 128
GQA_HEADS = 8
GQA_KV_HEADS = 2
GQA_HEAD_DIM = 128
D_FF = 5632
CONV_W = 3
N_BRANCH = 2

KV_COLS = MLA_KV_LORA + MLA_ROPE + 2 * GQA_KV_HEADS * GQA_HEAD_DIM
Q_COLS = MLA_Q_LORA + GQA_HEADS * GQA_HEAD_DIM
GATE_COLS = N_BRANCH * D_MODEL
IN_COLS = KV_COLS + Q_COLS + GATE_COLS
KV_SPLITS = [MLA_KV_LORA, MLA_KV_LORA + MLA_ROPE, MLA_KV_LORA + MLA_ROPE + GQA_KV_HEADS * GQA_HEAD_DIM]

kernel_name = "hybrid_mla_gqa_convffn_dit_prefix"


def rms_norm(x, g):
    xf = x.astype(jnp.float32)
    y = xf * lax.rsqrt(jnp.mean(xf * xf, axis=-1, keepdims=True) + NORM_EPS)
    return (y * g.astype(jnp.float32)).astype(x.dtype)


def modulate(h, shift, scale):
    return h * (1 + scale) + shift


def ada_terms(cond, w_ada, b_ada):
    return jnp.split(jax.nn.silu(cond) @ w_ada + b_ada, 6, axis=-1)


def grid_rope_tables(n_rows, rot_dim):
    row = jnp.repeat(jnp.arange(n_rows, dtype=jnp.float32), GRID_W)
    col = jnp.tile(jnp.arange(GRID_W, dtype=jnp.float32), n_rows)
    half = rot_dim // 2
    inv_freq = ROPE_THETA ** (-jnp.arange(0, half, 2, dtype=jnp.float32) / half)
    ang = jnp.concatenate([row[:, None] * inv_freq, col[:, None] * inv_freq], axis=-1)
    return jnp.cos(ang), jnp.sin(ang)


def apply_grid_rope(x, cos, sin):
    b, t, h, r = x.shape
    q = r // 4
    xs = x.reshape(b, t, h, 2, 2, q)
    x1, x2 = xs[..., 0, :], xs[..., 1, :]
    c = cos.reshape(t, 1, 2, q).astype(x.dtype)
    s = sin.reshape(t, 1, 2, q).astype(x.dtype)
    out = jnp.stack([x1 * c - x2 * s, x1 * s + x2 * c], axis=-2)
    return out.reshape(b, t, h, r)


def block_attention(q, k, v):
    b, tq, hk, g, dk = q.shape
    dv = v.shape[-1]
    scale = dk ** -0.5
    kf = k.astype(jnp.float32)
    qb = jnp.moveaxis(q.reshape(b, tq // Q_BLOCK, Q_BLOCK, hk, g, dk), 1, 0)

    def one_block(q_blk):
        s = jnp.einsum("bqhgd,bkhd->bhgqk", q_blk.astype(jnp.float32), kf) * scale
        p = jax.nn.softmax(s, axis=-1)
        return jnp.einsum("bhgqk,bkhd->bqhgd", p.astype(v.dtype), v)

    o = lax.map(one_block, qb)
    return jnp.moveaxis(o, 0, 1).reshape(b, tq, hk * g * dv)


def mixer_keys(kv, p, rope):
    b, t, _ = kv.shape
    c_kv, k_pe, k_b, v_b = jnp.split(kv, KV_SPLITS, axis=-1)
    kv_up = (rms_norm(c_kv, p["mla_kv_norm_g"]) @ p["w_kv_up"]).reshape(b, t, MLA_HEADS, MLA_NOPE + MLA_V)
    k_nope, v_a = jnp.split(kv_up, [MLA_NOPE], axis=-1)
    k_pe = k_pe.reshape(b, t, 1, MLA_ROPE)
    k_b = rms_norm(k_b.reshape(b, t, GQA_KV_HEADS, GQA_HEAD_DIM), p["gqa_k_norm_g"])
    v_b = v_b.reshape(b, t, GQA_KV_HEADS, GQA_HEAD_DIM)
    if rope is not None:
        cos_a, sin_a, cos_b, sin_b = rope
        k_pe = apply_grid_rope(k_pe, cos_a, sin_a)
        k_b = apply_grid_rope(k_b, cos_b, sin_b)
    k_a = jnp.concatenate([k_nope, jnp.broadcast_to(k_pe, (b, t, MLA_HEADS, MLA_ROPE))], axis=-1)
    return (k_a, v_a, k_b, v_b)


def mixer_queries(qp, p, rope):
    b, t, _ = qp.shape
    c_q, q_b = jnp.split(qp, [MLA_Q_LORA], axis=-1)
    q_a = (rms_norm(c_q, p["mla_q_norm_g"]) @ p["w_q_up"]).reshape(b, t, MLA_HEADS, MLA_NOPE + MLA_ROPE)
    q_nope, q_pe = jnp.split(q_a, [MLA_NOPE], axis=-1)
    q_b = rms_norm(q_b.reshape(b, t, GQA_HEADS, GQA_HEAD_DIM), p["gqa_q_norm_g"])
    if rope is not None:
        cos_a, sin_a, cos_b, sin_b = rope
        q_pe = apply_grid_rope(q_pe, cos_a, sin_a)
        q_b = apply_grid_rope(q_b, cos_b, sin_b)
    q_a = jnp.concatenate([q_nope, q_pe], axis=-1)[:, :, :, None, :]
    q_b = q_b.reshape(b, t, GQA_KV_HEADS, GQA_HEADS // GQA_KV_HEADS, GQA_HEAD_DIM)
    return q_a, q_b


def attend_and_merge(proj, keys, p, rope):
    q_a, q_b = mixer_queries(proj[..., KV_COLS:KV_COLS + Q_COLS], p, rope)
    g_a, g_b = jnp.split(jax.nn.sigmoid(proj[..., KV_COLS + Q_COLS:]), N_BRANCH, axis=-1)
    k_a, v_a, k_b, v_b = keys
    o_a = block_attention(q_a, k_a, v_a)
    o_b = block_attention(q_b, k_b, v_b)
    merged = g_a * (o_a @ p["w_br_a"]) + g_b * (o_b @ p["w_br_b"])
    return merged @ p["w_out"]


def conv_ffn(z, p):
    t = z.shape[1]
    u = z @ p["w_up"]
    pad = CONV_W // 2
    up = jnp.pad(u, ((0, 0), (pad, pad), (0, 0)))
    uc = p["conv_b"] + sum(p["conv_w"][j] * up[:, j:j + t] for j in range(CONV_W))
    a, bb = jnp.split(uc, 2, axis=-1)
    return (jax.nn.silu(a) * bb) @ p["w_down"]


def _fwd_setup_inputs(seed: int = 0) -> dict:
    key = jax.random.key(seed)
    ks = jax.random.split(key, 24)
    f32 = jnp.float32

    def nrm(k, shape, scale):
        return jax.random.normal(k, shape, f32) * scale

    def gain(k, shape):
        return 1.0 + 0.01 * jax.random.normal(k, shape, f32)

    L, D = DEPTH, D_MODEL
    return {
        "x": nrm(ks[0], (BATCH, SEQ, D), 1.0),
        "c": nrm(ks[1], (BATCH, D), 1.0),
        "ctx": nrm(ks[2], (BATCH, CTX_LEN, D), 1.0),
        "c_ctx": nrm(ks[3], (D,), 0.5),
        "w_ada": nrm(ks[4], (L, D, 6 * D), 0.5 * D ** -0.5),
        "b_ada": nrm(ks[5], (L, 6 * D), 0.01),
        "norm1_g": gain(ks[6], (L, D)),
        "w_in": nrm(ks[7], (L, D, IN_COLS), D ** -0.5),
        "mla_q_norm_g": gain(ks[8], (L, MLA_Q_LORA)),
        "w_q_up": nrm(ks[9], (L, MLA_Q_LORA, MLA_HEADS * (MLA_NOPE + MLA_ROPE)), MLA_Q_LORA ** -0.5),
        "mla_kv_norm_g": gain(ks[10], (L, MLA_KV_LORA)),
        "w_kv_up": nrm(ks[11], (L, MLA_KV_LORA, MLA_HEADS * (MLA_NOPE + MLA_V)), MLA_KV_LORA ** -0.5),
        "gqa_q_norm_g": gain(ks[12], (L, GQA_HEAD_DIM)),
        "gqa_k_norm_g": gain(ks[13], (L, GQA_HEAD_DIM)),
        "w_br_a": nrm(ks[14], (L, MLA_HEADS * MLA_V, D), (MLA_HEADS * MLA_V) ** -0.5),
        "w_br_b": nrm(ks[15], (L, GQA_HEADS * GQA_HEAD_DIM, D), (GQA_HEADS * GQA_HEAD_DIM) ** -0.5),
        "w_out": nrm(ks[16], (L, D, D), D ** -0.5),
        "norm2_g": gain(ks[17], (L, D)),
        "w_up": nrm(ks[18], (L, D, 2 * D_FF), D ** -0.5),
        "conv_w": nrm(ks[19], (L, CONV_W, 2 * D_FF), CONV_W ** -0.5),
        "conv_b": nrm(ks[20], (L, 2 * D_FF), 0.01),
        "w_down": nrm(ks[21], (L, D_FF, D), D_FF ** -0.5),
        "final_norm_g": gain(ks[22], (D,)),
    }


def _fwd_reference(x, c, ctx, c_ctx, w_ada, b_ada, norm1_g, w_in, mla_q_norm_g, w_q_up, mla_kv_norm_g,
              w_kv_up, gqa_q_norm_g, gqa_k_norm_g, w_br_a, w_br_b, w_out, norm2_g, w_up, conv_w,
              conv_b, w_down, final_norm_g):
    n_lat = x.shape[1]
    ROWS = n_lat // GRID_W
    rope = (*grid_rope_tables(ROWS, MLA_ROPE), *grid_rope_tables(ROWS, GQA_HEAD_DIM))
    cond_lat = c[:, None, :]
    cond_ctx = c_ctx[None, None, :]

    for l in range(DEPTH):
        p = {
            "w_in": w_in[l], "mla_q_norm_g": mla_q_norm_g[l], "w_q_up": w_q_up[l],
            "mla_kv_norm_g": mla_kv_norm_g[l], "w_kv_up": w_kv_up[l],
            "gqa_q_norm_g": gqa_q_norm_g[l], "gqa_k_norm_g": gqa_k_norm_g[l],
            "w_br_a": w_br_a[l], "w_br_b": w_br_b[l], "w_out": w_out[l],
            "w_up": w_up[l], "conv_w": conv_w[l], "conv_b": conv_b[l], "w_down": w_down[l],
        }
        last = l == DEPTH - 1
        sh1, sc1, g1, sh2, sc2, g2 = ada_terms(cond_lat, w_ada[l], b_ada[l])
        ctx_terms = ada_terms(cond_ctx, w_ada[l], b_ada[l])

        z_ctx = modulate(rms_norm(ctx, norm1_g[l]), ctx_terms[0], ctx_terms[1])
        ctx_proj = z_ctx @ (p["w_in"][:, :KV_COLS] if last else p["w_in"])
        ctx_keys = mixer_keys(ctx_proj[..., :KV_COLS], p, None)

        z_lat = modulate(rms_norm(x, norm1_g[l]), sh1, sc1)
        lat_proj = z_lat @ p["w_in"]
        lat_keys = mixer_keys(lat_proj[..., :KV_COLS], p, rope)
        keys = tuple(jnp.concatenate([ck, lk], axis=1) for ck, lk in zip(ctx_keys, lat_keys))
        x = x + g1 * attend_and_merge(lat_proj, keys, p, rope)
        x = x + g2 * conv_ffn(modulate(rms_norm(x, norm2_g[l]), sh2, sc2), p)

        if not last:
            ctx = ctx + ctx_terms[2] * attend_and_merge(ctx_proj, ctx_keys, p, None)
            z2 = modulate(rms_norm(ctx, norm2_g[l]), ctx_terms[3], ctx_terms[4])
            ctx = ctx + ctx_terms[5] * conv_ffn(z2, p)

    return rms_norm(x, final_norm_g)


import jax as _jax
import jax.numpy as _jnp

TWIN_FORMAT = 'train_step'
FWD_PARAMS = ['x', 'c', 'ctx', 'c_ctx', 'w_ada', 'b_ada', 'norm1_g', 'w_in', 'mla_q_norm_g', 'w_q_up', 'mla_kv_norm_g', 'w_kv_up', 'gqa_q_norm_g', 'gqa_k_norm_g', 'w_br_a', 'w_br_b', 'w_out', 'norm2_g', 'w_up', 'conv_w', 'conv_b', 'w_down', 'final_norm_g']
TWIN_WEIGHTS = ['c_ctx', 'w_ada', 'b_ada', 'norm1_g', 'w_in', 'mla_q_norm_g', 'w_q_up', 'mla_kv_norm_g', 'w_kv_up', 'gqa_q_norm_g', 'gqa_k_norm_g', 'w_br_a', 'w_br_b', 'w_out', 'norm2_g', 'w_up', 'conv_w', 'conv_b', 'w_down', 'final_norm_g']
TWIN_DIFF_INPUT = 'x'
TWIN_INPUTS = ['x', 'c', 'ctx', 'c_ctx', 'w_ada', 'b_ada', 'norm1_g', 'w_in', 'mla_q_norm_g', 'w_q_up', 'mla_kv_norm_g', 'w_kv_up', 'gqa_q_norm_g', 'gqa_k_norm_g', 'w_br_a', 'w_br_b', 'w_out', 'norm2_g', 'w_up', 'conv_w', 'conv_b', 'w_down', 'final_norm_g', 'loss_target', 'm_c_ctx', 'm_w_ada', 'm_b_ada', 'm_norm1_g', 'm_w_in', 'm_mla_q_norm_g', 'm_w_q_up', 'm_mla_kv_norm_g', 'm_w_kv_up', 'm_gqa_q_norm_g', 'm_gqa_k_norm_g', 'm_w_br_a', 'm_w_br_b', 'm_w_out', 'm_norm2_g', 'm_w_up', 'm_conv_w', 'm_conv_b', 'm_w_down', 'm_final_norm_g', 'v_c_ctx', 'v_w_ada', 'v_b_ada', 'v_norm1_g', 'v_w_in', 'v_mla_q_norm_g', 'v_w_q_up', 'v_mla_kv_norm_g', 'v_w_kv_up', 'v_gqa_q_norm_g', 'v_gqa_k_norm_g', 'v_w_br_a', 'v_w_br_b', 'v_w_out', 'v_norm2_g', 'v_w_up', 'v_conv_w', 'v_conv_b', 'v_w_down', 'v_final_norm_g']
TWIN_OUTPUTS = ['loss', 'grad_x', 'grad_c_ctx', 'grad_w_ada', 'grad_b_ada', 'grad_norm1_g', 'grad_w_in', 'grad_mla_q_norm_g', 'grad_w_q_up', 'grad_mla_kv_norm_g', 'grad_w_kv_up', 'grad_gqa_q_norm_g', 'grad_gqa_k_norm_g', 'grad_w_br_a', 'grad_w_br_b', 'grad_w_out', 'grad_norm2_g', 'grad_w_up', 'grad_conv_w', 'grad_conv_b', 'grad_w_down', 'grad_final_norm_g', 'delta_c_ctx', 'delta_w_ada', 'delta_b_ada', 'delta_norm1_g', 'delta_w_in', 'delta_mla_q_norm_g', 'delta_w_q_up', 'delta_mla_kv_norm_g', 'delta_w_kv_up', 'delta_gqa_q_norm_g', 'delta_gqa_k_norm_g', 'delta_w_br_a', 'delta_w_br_b', 'delta_w_out', 'delta_norm2_g', 'delta_w_up', 'delta_conv_w', 'delta_conv_b', 'delta_w_down', 'delta_final_norm_g', 'new_m_c_ctx', 'new_m_w_ada', 'new_m_b_ada', 'new_m_norm1_g', 'new_m_w_in', 'new_m_mla_q_norm_g', 'new_m_w_q_up', 'new_m_mla_kv_norm_g', 'new_m_w_kv_up', 'new_m_gqa_q_norm_g', 'new_m_gqa_k_norm_g', 'new_m_w_br_a', 'new_m_w_br_b', 'new_m_w_out', 'new_m_norm2_g', 'new_m_w_up', 'new_m_conv_w', 'new_m_conv_b', 'new_m_w_down', 'new_m_final_norm_g', 'new_v_c_ctx', 'new_v_w_ada', 'new_v_b_ada', 'new_v_norm1_g', 'new_v_w_in', 'new_v_mla_q_norm_g', 'new_v_w_q_up', 'new_v_mla_kv_norm_g', 'new_v_w_kv_up', 'new_v_gqa_q_norm_g', 'new_v_gqa_k_norm_g', 'new_v_w_br_a', 'new_v_w_br_b', 'new_v_w_out', 'new_v_norm2_g', 'new_v_w_up', 'new_v_conv_w', 'new_v_conv_b', 'new_v_w_down', 'new_v_final_norm_g']
TWIN_LEAF_KINDS = {'loss': 'loss', 'grad_x': 'grad_x', 'grad_c_ctx': 'grad_w', 'grad_w_ada': 'grad_w', 'grad_b_ada': 'grad_w', 'grad_norm1_g': 'grad_w', 'grad_w_in': 'grad_w', 'grad_mla_q_norm_g': 'grad_w', 'grad_w_q_up': 'grad_w', 'grad_mla_kv_norm_g': 'grad_w', 'grad_w_kv_up': 'grad_w', 'grad_gqa_q_norm_g': 'grad_w', 'grad_gqa_k_norm_g': 'grad_w', 'grad_w_br_a': 'grad_w', 'grad_w_br_b': 'grad_w', 'grad_w_out': 'grad_w', 'grad_norm2_g': 'grad_w', 'grad_w_up': 'grad_w', 'grad_conv_w': 'grad_w', 'grad_conv_b': 'grad_w', 'grad_w_down': 'grad_w', 'grad_final_norm_g': 'grad_w', 'delta_c_ctx': 'delta_w', 'delta_w_ada': 'delta_w', 'delta_b_ada': 'delta_w', 'delta_norm1_g': 'delta_w', 'delta_w_in': 'delta_w', 'delta_mla_q_norm_g': 'delta_w', 'delta_w_q_up': 'delta_w', 'delta_mla_kv_norm_g': 'delta_w', 'delta_w_kv_up': 'delta_w', 'delta_gqa_q_norm_g': 'delta_w', 'delta_gqa_k_norm_g': 'delta_w', 'delta_w_br_a': 'delta_w', 'delta_w_br_b': 'delta_w', 'delta_w_out': 'delta_w', 'delta_norm2_g': 'delta_w', 'delta_w_up': 'delta_w', 'delta_conv_w': 'delta_w', 'delta_conv_b': 'delta_w', 'delta_w_down': 'delta_w', 'delta_final_norm_g': 'delta_w', 'new_m_c_ctx': 'new_m', 'new_m_w_ada': 'new_m', 'new_m_b_ada': 'new_m', 'new_m_norm1_g': 'new_m', 'new_m_w_in': 'new_m', 'new_m_mla_q_norm_g': 'new_m', 'new_m_w_q_up': 'new_m', 'new_m_mla_kv_norm_g': 'new_m', 'new_m_w_kv_up': 'new_m', 'new_m_gqa_q_norm_g': 'new_m', 'new_m_gqa_k_norm_g': 'new_m', 'new_m_w_br_a': 'new_m', 'new_m_w_br_b': 'new_m', 'new_m_w_out': 'new_m', 'new_m_norm2_g': 'new_m', 'new_m_w_up': 'new_m', 'new_m_conv_w': 'new_m', 'new_m_conv_b': 'new_m', 'new_m_w_down': 'new_m', 'new_m_final_norm_g': 'new_m', 'new_v_c_ctx': 'new_v', 'new_v_w_ada': 'new_v', 'new_v_b_ada': 'new_v', 'new_v_norm1_g': 'new_v', 'new_v_w_in': 'new_v', 'new_v_mla_q_norm_g': 'new_v', 'new_v_w_q_up': 'new_v', 'new_v_mla_kv_norm_g': 'new_v', 'new_v_w_kv_up': 'new_v', 'new_v_gqa_q_norm_g': 'new_v', 'new_v_gqa_k_norm_g': 'new_v', 'new_v_w_br_a': 'new_v', 'new_v_w_br_b': 'new_v', 'new_v_w_out': 'new_v', 'new_v_norm2_g': 'new_v', 'new_v_w_up': 'new_v', 'new_v_conv_w': 'new_v', 'new_v_conv_b': 'new_v', 'new_v_w_down': 'new_v', 'new_v_final_norm_g': 'new_v'}


def _forward(args):
    return _fwd_reference(*[args[k] for k in FWD_PARAMS])


def _output_shape():
    out = _jax.eval_shape(lambda: _forward(_fwd_setup_inputs(0)))
    return out.shape, out.dtype

N_MICROBATCH = 1
ADAM_LR = 0.001
ADAM_B1 = 0.9
ADAM_B2 = 0.999
ADAM_EPS = 1e-08
ADAM_WD = 0.01
ADAM_STEP = 10
PER_EXAMPLE_BATCH_AXIS = {'x': 0, 'c': 0, 'ctx': 0, 'loss_target': 0}
SHARED_INPUTS = []
_WEIGHT_DTYPES = {'c_ctx': _jnp.float32, 'w_ada': _jnp.float32, 'b_ada': _jnp.float32, 'norm1_g': _jnp.float32, 'w_in': _jnp.float32, 'mla_q_norm_g': _jnp.float32, 'w_q_up': _jnp.float32, 'mla_kv_norm_g': _jnp.float32, 'w_kv_up': _jnp.float32, 'gqa_q_norm_g': _jnp.float32, 'gqa_k_norm_g': _jnp.float32, 'w_br_a': _jnp.float32, 'w_br_b': _jnp.float32, 'w_out': _jnp.float32, 'norm2_g': _jnp.float32, 'w_up': _jnp.float32, 'conv_w': _jnp.float32, 'conv_b': _jnp.float32, 'w_down': _jnp.float32, 'final_norm_g': _jnp.float32}
MOMENT_SCALE = {'c_ctx': 2.177741e-03, 'w_ada': 1.369611e-02, 'b_ada': 2.289797e-02, 'norm1_g': 2.238279e-03, 'w_in': 2.696996e-03, 'mla_q_norm_g': 1.466088e-03, 'w_q_up': 1.038700e-03, 'mla_kv_norm_g': 7.298588e-03, 'w_kv_up': 3.170317e-03, 'gqa_q_norm_g': 3.438935e-03, 'gqa_k_norm_g': 3.414433e-03, 'w_br_a': 3.236398e-03, 'w_br_b': 3.403276e-03, 'w_out': 4.611253e-03, 'norm2_g': 1.868102e-02, 'w_up': 8.240908e-03, 'conv_w': 8.293807e-03, 'conv_b': 7.514312e-03, 'w_down': 1.344172e-02, 'final_norm_g': 7.997542e+00}


def _to_microbatches(a, axis):
    t = _jnp.moveaxis(a, axis, 0)
    t = t.reshape((N_MICROBATCH, t.shape[0] // N_MICROBATCH) + t.shape[1:])
    return _jnp.moveaxis(t, 1, axis + 1)


def setup_inputs(seed: int = 0) -> dict:
    inp = _fwd_setup_inputs(seed)
    key = _jax.random.fold_in(_jax.random.key(seed), 7919)
    shape, _ = _output_shape()
    out = dict(inp)
    out["loss_target"] = _jax.random.normal(_jax.random.fold_in(key, 0), shape, _jnp.float32)
    for i, name in enumerate(TWIN_WEIGHTS):
        w = inp[name].astype(_jnp.float32)
        if MOMENT_SCALE is None:
            s = _jnp.sqrt(_jnp.mean(_jnp.square(w)) + 1e-30)
        else:
            s = MOMENT_SCALE[name]
        km, kv = _jax.random.split(_jax.random.fold_in(key, i + 1))
        out[name] = w
        out["m_" + name] = s * _jax.random.normal(km, w.shape, _jnp.float32)
        out["v_" + name] = (s * s) * _jax.random.uniform(kv, w.shape, _jnp.float32, 0.5, 1.5)
    if N_MICROBATCH > 1:
        for name, axis in PER_EXAMPLE_BATCH_AXIS.items():
            out[name] = _to_microbatches(out[name], axis)
    return {'x': out['x'], 'c': out['c'], 'ctx': out['ctx'], 'c_ctx': out['c_ctx'], 'w_ada': out['w_ada'], 'b_ada': out['b_ada'], 'norm1_g': out['norm1_g'], 'w_in': out['w_in'], 'mla_q_norm_g': out['mla_q_norm_g'], 'w_q_up': out['w_q_up'], 'mla_kv_norm_g': out['mla_kv_norm_g'], 'w_kv_up': out['w_kv_up'], 'gqa_q_norm_g': out['gqa_q_norm_g'], 'gqa_k_norm_g': out['gqa_k_norm_g'], 'w_br_a': out['w_br_a'], 'w_br_b': out['w_br_b'], 'w_out': out['w_out'], 'norm2_g': out['norm2_g'], 'w_up': out['w_up'], 'conv_w': out['conv_w'], 'conv_b': out['conv_b'], 'w_down': out['w_down'], 'final_norm_g': out['final_norm_g'], 'loss_target': out['loss_target'], 'm_c_ctx': out['m_c_ctx'], 'm_w_ada': out['m_w_ada'], 'm_b_ada': out['m_b_ada'], 'm_norm1_g': out['m_norm1_g'], 'm_w_in': out['m_w_in'], 'm_mla_q_norm_g': out['m_mla_q_norm_g'], 'm_w_q_up': out['m_w_q_up'], 'm_mla_kv_norm_g': out['m_mla_kv_norm_g'], 'm_w_kv_up': out['m_w_kv_up'], 'm_gqa_q_norm_g': out['m_gqa_q_norm_g'], 'm_gqa_k_norm_g': out['m_gqa_k_norm_g'], 'm_w_br_a': out['m_w_br_a'], 'm_w_br_b': out['m_w_br_b'], 'm_w_out': out['m_w_out'], 'm_norm2_g': out['m_norm2_g'], 'm_w_up': out['m_w_up'], 'm_conv_w': out['m_conv_w'], 'm_conv_b': out['m_conv_b'], 'm_w_down': out['m_w_down'], 'm_final_norm_g': out['m_final_norm_g'], 'v_c_ctx': out['v_c_ctx'], 'v_w_ada': out['v_w_ada'], 'v_b_ada': out['v_b_ada'], 'v_norm1_g': out['v_norm1_g'], 'v_w_in': out['v_w_in'], 'v_mla_q_norm_g': out['v_mla_q_norm_g'], 'v_w_q_up': out['v_w_q_up'], 'v_mla_kv_norm_g': out['v_mla_kv_norm_g'], 'v_w_kv_up': out['v_w_kv_up'], 'v_gqa_q_norm_g': out['v_gqa_q_norm_g'], 'v_gqa_k_norm_g': out['v_gqa_k_norm_g'], 'v_w_br_a': out['v_w_br_a'], 'v_w_br_b': out['v_w_br_b'], 'v_w_out': out['v_w_out'], 'v_norm2_g': out['v_norm2_g'], 'v_w_up': out['v_w_up'], 'v_conv_w': out['v_conv_w'], 'v_conv_b': out['v_conv_b'], 'v_w_down': out['v_w_down'], 'v_final_norm_g': out['v_final_norm_g']}


def _loss(weights, diff, rest, loss_target):
    with _jax.named_scope("forward"):
        args = {**rest, TWIN_DIFF_INPUT: diff, **{k: w.astype(_WEIGHT_DTYPES[k]) for k, w in weights.items()}}
        y = _forward(args)
    with _jax.named_scope("loss_head"):
        err = _jnp.square(y.astype(_jnp.float32) - loss_target)
        return 0.5 * _jnp.sum(_jnp.mean(err, axis=-1)) if err.ndim else 0.5 * err


def _adamw(w, g, m, v):
    m = ADAM_B1 * m + (1.0 - ADAM_B1) * g
    v = ADAM_B2 * v + (1.0 - ADAM_B2) * _jnp.square(g)
    m_hat = m / (1.0 - ADAM_B1 ** ADAM_STEP)
    v_hat = v / (1.0 - ADAM_B2 ** ADAM_STEP)
    delta = -ADAM_LR * (m_hat / (_jnp.sqrt(v_hat) + ADAM_EPS) + ADAM_WD * w)
    return delta, m, v


def reference(x, c, ctx, c_ctx, w_ada, b_ada, norm1_g, w_in, mla_q_norm_g, w_q_up, mla_kv_norm_g, w_kv_up, gqa_q_norm_g, gqa_k_norm_g, w_br_a, w_br_b, w_out, norm2_g, w_up, conv_w, conv_b, w_down, final_norm_g, loss_target, m_c_ctx, m_w_ada, m_b_ada, m_norm1_g, m_w_in, m_mla_q_norm_g, m_w_q_up, m_mla_kv_norm_g, m_w_kv_up, m_gqa_q_norm_g, m_gqa_k_norm_g, m_w_br_a, m_w_br_b, m_w_out, m_norm2_g, m_w_up, m_conv_w, m_conv_b, m_w_down, m_final_norm_g, v_c_ctx, v_w_ada, v_b_ada, v_norm1_g, v_w_in, v_mla_q_norm_g, v_w_q_up, v_mla_kv_norm_g, v_w_kv_up, v_gqa_q_norm_g, v_gqa_k_norm_g, v_w_br_a, v_w_br_b, v_w_out, v_norm2_g, v_w_up, v_conv_w, v_conv_b, v_w_down, v_final_norm_g):
    given = dict(x=x, c=c, ctx=ctx, c_ctx=c_ctx, w_ada=w_ada, b_ada=b_ada, norm1_g=norm1_g, w_in=w_in, mla_q_norm_g=mla_q_norm_g, w_q_up=w_q_up, mla_kv_norm_g=mla_kv_norm_g, w_kv_up=w_kv_up, gqa_q_norm_g=gqa_q_norm_g, gqa_k_norm_g=gqa_k_norm_g, w_br_a=w_br_a, w_br_b=w_br_b, w_out=w_out, norm2_g=norm2_g, w_up=w_up, conv_w=conv_w, conv_b=conv_b, w_down=w_down, final_norm_g=final_norm_g, loss_target=loss_target, m_c_ctx=m_c_ctx, m_w_ada=m_w_ada, m_b_ada=m_b_ada, m_norm1_g=m_norm1_g, m_w_in=m_w_in, m_mla_q_norm_g=m_mla_q_norm_g, m_w_q_up=m_w_q_up, m_mla_kv_norm_g=m_mla_kv_norm_g, m_w_kv_up=m_w_kv_up, m_gqa_q_norm_g=m_gqa_q_norm_g, m_gqa_k_norm_g=m_gqa_k_norm_g, m_w_br_a=m_w_br_a, m_w_br_b=m_w_br_b, m_w_out=m_w_out, m_norm2_g=m_norm2_g, m_w_up=m_w_up, m_conv_w=m_conv_w, m_conv_b=m_conv_b, m_w_down=m_w_down, m_final_norm_g=m_final_norm_g, v_c_ctx=v_c_ctx, v_w_ada=v_w_ada, v_b_ada=v_b_ada, v_norm1_g=v_norm1_g, v_w_in=v_w_in, v_mla_q_norm_g=v_mla_q_norm_g, v_w_q_up=v_w_q_up, v_mla_kv_norm_g=v_mla_kv_norm_g, v_w_kv_up=v_w_kv_up, v_gqa_q_norm_g=v_gqa_q_norm_g, v_gqa_k_norm_g=v_gqa_k_norm_g, v_w_br_a=v_w_br_a, v_w_br_b=v_w_br_b, v_w_out=v_w_out, v_norm2_g=v_norm2_g, v_w_up=v_w_up, v_conv_w=v_conv_w, v_conv_b=v_conv_b, v_w_down=v_w_down, v_final_norm_g=v_final_norm_g)
    weights = {n: given[n] for n in TWIN_WEIGHTS}
    shared = {n: given[n] for n in SHARED_INPUTS}
    per_example = {n: given[n] for n in ['x', 'c', 'ctx']}
    grad_fn = _jax.value_and_grad(_loss, argnums=(0, 1))

    def one_microbatch(ex, loss_target):
        ex = dict(ex)
        diff = ex.pop(TWIN_DIFF_INPUT)
        return grad_fn(weights, diff, {**shared, **ex}, loss_target)

    if N_MICROBATCH == 1:
        loss, (grad_w, grad_x) = one_microbatch(per_example, given["loss_target"])
    else:
        def body(carry, xs):
            loss_sum, grad_sum = carry
            l_k, (gw_k, gx_k) = one_microbatch(xs[0], xs[1])
            with _jax.named_scope("update"):
                return (loss_sum + l_k, _jax.tree.map(_jnp.add, grad_sum, gw_k)), gx_k

        init = (_jnp.zeros((), _jnp.float32), _jax.tree.map(_jnp.zeros_like, weights))
        (loss, grad_w), grad_x = _jax.lax.scan(body, init, (per_example, given["loss_target"]))
    with _jax.named_scope("update"):
        delta_w, new_m, new_v = {}, {}, {}
        for n in TWIN_WEIGHTS:
            delta_w[n], new_m[n], new_v[n] = _adamw(weights[n], grad_w[n], given["m_" + n], given["v_" + n])
    return (loss, grad_x, *[grad_w[n] for n in TWIN_WEIGHTS], *[delta_w[n] for n in TWIN_WEIGHTS],
            *[new_m[n] for n in TWIN_WEIGHTS], *[new_v[n] for n in TWIN_WEIGHTS])
```

```python
import functools

import jax
import jax.numpy as jnp
from jax import lax
from jax.experimental import pallas as pl
from jax.experimental.pallas import tpu as pltpu

F32 = jnp.float32
BF16 = jnp.bfloat16
MESH = pl.DeviceIdType.MESH

NORM_EPS = 1e-6
ROPE_THETA = 10000.0
GRID_W = 64
MLA_HEADS = 8
MLA_Q_LORA = 768
MLA_KV_LORA = 512
MLA_NOPE = 128
MLA_ROPE = 64
MLA_V = 128
GQA_HEADS = 8
GQA_KV_HEADS = 2
GQA_HEAD_DIM = 128
GQA_GROUP = GQA_HEADS // GQA_KV_HEADS
LANES = 128
KVP = MLA_KV_LORA + 2 * GQA_KV_HEADS * GQA_HEAD_DIM + LANES
QC = MLA_Q_LORA + GQA_HEADS * GQA_HEAD_DIM

ADAM_LR = 0.001
ADAM_B1 = 0.9
ADAM_B2 = 0.999
ADAM_EPS = 1e-08
ADAM_WD = 0.01
ADAM_STEP = 10

VMEM_LIMIT = 48 * 1024 * 1024


def _pick(dim, target, mult=LANES):
    t = (min(target, dim) // mult) * mult
    while t >= mult:
        if dim % t == 0:
            return t
        t -= mult
    return dim


def _params(sem):
    return pltpu.CompilerParams(dimension_semantics=sem, vmem_limit_bytes=VMEM_LIMIT)


_DIMS = {"NN": (((1,), (0,)), ((), ())), "NT": (((1,), (1,)), ((), ())), "TN": (((0,), (0,)), ((), ()))}


def _mm(a, b, mode, out_dtype, name, tm=1024, tn=1024, tk=512, act=None, bias=None):
    if mode == "NN":
        (M, K), (K2, N) = a.shape, b.shape
    elif mode == "NT":
        (M, K), (N, K2) = a.shape, b.shape
    else:
        (K, M), (K2, N) = a.shape, b.shape
    assert K == K2, (name, a.shape, b.shape)
    tm, tn, tk = _pick(M, tm), _pick(N, tn), _pick(K, tk)
    nk = K // tk
    dims = _DIMS[mode]

    def body(*refs):
        if bias is None:
            a_ref, b_ref, o_ref, acc = refs
        else:
            a_ref, b_ref, bias_ref, o_ref, acc = refs
        k = pl.program_id(2)

        @pl.when(k == 0)
        def _():
            acc[...] = jnp.zeros_like(acc)

        av = a_ref[...]
        if act == "silu":
            av = av * jax.nn.sigmoid(av)
        acc[...] += lax.dot_general(av.astype(BF16), b_ref[...].astype(BF16), dims, preferred_element_type=F32)

        @pl.when(k == nk - 1)
        def _():
            r = acc[...]
            if bias is not None:
                r = r + bias_ref[...]
            o_ref[...] = r.astype(out_dtype)

    a_spec = pl.BlockSpec((tk, tm), lambda i, j, k: (k, i)) if mode == "TN" else pl.BlockSpec((tm, tk), lambda i, j, k: (i, k))
    b_spec = pl.BlockSpec((tn, tk), lambda i, j, k: (j, k)) if mode == "NT" else pl.BlockSpec((tk, tn), lambda i, j, k: (k, j))
    in_specs, args = [a_spec, b_spec], [a, b]
    if bias is not None:
        in_specs.append(pl.BlockSpec((1, tn), lambda i, j, k: (0, j)))
        args.append(bias)
    return pl.pallas_call(
        body,
        name=name,
        grid=(M // tm, N // tn, nk),
        in_specs=in_specs,
        out_specs=pl.BlockSpec((tm, tn), lambda i, j, k: (i, j)),
        out_shape=jax.ShapeDtypeStruct((M, N), out_dtype),
        scratch_shapes=[pltpu.VMEM((tm, tn), F32)],
        compiler_params=_params(("parallel", "parallel", "arbitrary")),
    )(*args)


def _rms(x):
    r = lax.rsqrt(jnp.mean(x * x, axis=-1, keepdims=True) + NORM_EPS)
    return x * r, r


def _rms_bwd(xh, r, dxh):
    return r * (dxh - xh * jnp.mean(dxh * xh, axis=-1, keepdims=True))


def _swap(x, q):
    lane = lax.broadcasted_iota(jnp.int32, x.shape, 1)
    even = ((lane // q) % 2) == 0
    return jnp.where(even, pltpu.roll(x, LANES - q, 1), pltpu.roll(x, q, 1))


def _rope(x, cos, ss, q):
    return x * cos + _swap(x, q) * ss


def _rope_t(d, cos, ss, q):
    return d * cos + _swap(d * ss, q)


def _csum(x):
    return jnp.sum(x, axis=0, keepdims=True)


def _rows(tr, w, off=0):
    return pl.BlockSpec((tr, w), lambda i: (i + off, 0))


def _bcast(w):
    return pl.BlockSpec((1, w), lambda i: (0, 0))


def _acc_init(i, refs):
    @pl.when(i == 0)
    def _():
        for r in refs:
            r[...] = jnp.zeros_like(r)


def _rope_tables(n_ctx, n_lat, rot_dim):
    rows = n_lat // GRID_W
    row = jnp.repeat(jnp.arange(rows, dtype=F32), GRID_W)
    col = jnp.tile(jnp.arange(GRID_W, dtype=F32), rows)
    half = rot_dim // 2
    inv_freq = ROPE_THETA ** (-jnp.arange(0, half, 2, dtype=F32) / half)
    ar, ac = row[:, None] * inv_freq, col[:, None] * inv_freq
    cos = jnp.concatenate([jnp.cos(ar), jnp.cos(ar), jnp.cos(ac), jnp.cos(ac)], axis=-1)
    ss = jnp.concatenate([-jnp.sin(ar), jnp.sin(ar), -jnp.sin(ac), jnp.sin(ac)], axis=-1)
    cos = jnp.tile(cos, (1, LANES // rot_dim))
    ss = jnp.tile(ss, (1, LANES // rot_dim))
    cos = jnp.concatenate([jnp.ones((n_ctx, LANES), F32), cos], axis=0)
    ss = jnp.concatenate([jnp.zeros((n_ctx, LANES), F32), ss], axis=0)
    return cos, ss


def _norm_mod_fwd(x2d, g, sh, sc, name, tr):
    n, d = x2d.shape

    def body(x_ref, g_ref, sh_ref, sc_ref, z_ref):
        xh, _ = _rms(x_ref[...])
        z_ref[...] = ((xh * g_ref[...]) * (1.0 + sc_ref[...]) + sh_ref[...]).astype(BF16)

    return pl.pallas_call(
        body,
        name=name,
        grid=(n // tr,),
        in_specs=[_rows(tr, d), _bcast(d), _bcast(d), _bcast(d)],
        out_specs=_rows(tr, d),
        out_shape=jax.ShapeDtypeStruct((n, d), BF16),
        compiler_params=_params(("parallel",)),
    )(x2d, g, sh, sc)


def _norm_mod_bwd(dz, dz_off, x2d, g, sc, dres, name, tr):
    n, d = x2d.shape
    want_dx = dres is not None

    def body(*refs):
        if want_dx:
            dz_ref, x_ref, g_ref, sc_ref, dres_ref, dx_ref, dg_ref, dsh_ref, dsc_ref = refs
        else:
            dz_ref, x_ref, g_ref, sc_ref, dg_ref, dsh_ref, dsc_ref = refs
        _acc_init(pl.program_id(0), [dg_ref, dsh_ref, dsc_ref])
        xh, r = _rms(x_ref[...])
        dzv = dz_ref[...]
        gv = g_ref[...]
        dsc_ref[...] += _csum(dzv * (xh * gv))
        dsh_ref[...] += _csum(dzv)
        dh = dzv * (1.0 + sc_ref[...])
        dg_ref[...] += _csum(dh * xh)
        if want_dx:
            dx_ref[...] = _rms_bwd(xh, r, dh * gv) + dres_ref[...]

    in_specs = [_rows(tr, d, dz_off), _rows(tr, d), _bcast(d), _bcast(d)]
    args = [dz, x2d, g, sc]
    out_specs = [_bcast(d)] * 3
    out_shape = [jax.ShapeDtypeStruct((1, d), F32)] * 3
    if want_dx:
        in_specs.append(_rows(tr, d))
        args.append(dres)
        out_specs = [_rows(tr, d)] + out_specs
        out_shape = [jax.ShapeDtypeStruct((n, d), F32)] + out_shape
    res = pl.pallas_call(
        body,
        name=name,
        grid=(n // tr,),
        in_specs=in_specs,
        out_specs=out_specs,
        out_shape=out_shape,
        compiler_params=_params(("arbitrary",)),
    )(*args)
    return res if want_dx else (None, *res)


_QA, _QB = MLA_ROPE // 4, GQA_HEAD_DIM // 4


def _kprep_fwd(pkv, kvg, kg, cos_a, ss_a, cos_b, ss_b, tr):
    n = pkv.shape[0]
    nb = GQA_KV_HEADS * GQA_HEAD_DIM

    def body(p_ref, kvg_ref, kg_ref, ca, sa, cb, sb, ckv_ref, kb_ref, vb_ref, kpe_ref):
        p = p_ref[...]
        xh, _ = _rms(p[:, :MLA_KV_LORA])
        ckv_ref[...] = (xh * kvg_ref[...]).astype(BF16)
        for e in range(GQA_KV_HEADS):
            lo = MLA_KV_LORA + e * GQA_HEAD_DIM
            kh, _ = _rms(p[:, lo : lo + GQA_HEAD_DIM])
            kb_ref[:, e * GQA_HEAD_DIM : (e + 1) * GQA_HEAD_DIM] = _rope(kh * kg_ref[...], cb[...], sb[...], _QB).astype(BF16)
        vb_ref[...] = p[:, MLA_KV_LORA + nb : MLA_KV_LORA + 2 * nb].astype(BF16)
        kr = _rope(p[:, MLA_KV_LORA + 2 * nb :], ca[...], sa[...], _QA)
        kpe_ref[:, :LANES] = kr.astype(BF16)
        kpe_ref[:, LANES:] = pltpu.roll(kr, MLA_ROPE, 1).astype(BF16)

    return pl.pallas_call(
        body,
        name="kprep_fwd",
        grid=(n // tr,),
        in_specs=[_rows(tr, KVP), _bcast(MLA_KV_LORA), _bcast(GQA_HEAD_DIM)] + [_rows(tr, LANES)] * 4,
        out_specs=[_rows(tr, MLA_KV_LORA), _rows(tr, nb), _rows(tr, nb), _rows(tr, 2 * LANES)],
        out_shape=[jax.ShapeDtypeStruct((n, w), BF16) for w in (MLA_KV_LORA, nb, nb, 2 * LANES)],
        compiler_params=_params(("parallel",)),
    )(pkv, kvg, kg, cos_a, ss_a, cos_b, ss_b)


def _kprep_bwd(pkv, dckv, dkb, dvb, dkpe, kvg, kg, cos_b, ss_b, tr):
    n = pkv.shape[0]
    nb = GQA_KV_HEADS * GQA_HEAD_DIM

    def body(p_ref, dckv_ref, dkb_ref, dvb_ref, dkpe_ref, kvg_ref, kg_ref, cb, sb, dp_ref, dkvg_ref, dkg_ref):
        _acc_init(pl.program_id(0), [dkvg_ref, dkg_ref])
        p = p_ref[...]
        xh, r = _rms(p[:, :MLA_KV_LORA])
        dn = dckv_ref[...]
        dkvg_ref[...] += _csum(dn * xh)
        dp_ref[:, :MLA_KV_LORA] = _rms_bwd(xh, r, dn * kvg_ref[...]).astype(BF16)
        for e in range(GQA_KV_HEADS):
            lo = MLA_KV_LORA + e * GQA_HEAD_DIM
            kh, rk = _rms(p[:, lo : lo + GQA_HEAD_DIM])
            dk = _rope_t(dkb_ref[:, e * GQA_HEAD_DIM : (e + 1) * GQA_HEAD_DIM], cb[...], sb[...], _QB)
            dkg_ref[...] += _csum(dk * kh)
            dp_ref[:, lo : lo + GQA_HEAD_DIM] = _rms_bwd(kh, rk, dk * kg_ref[...]).astype(BF16)
        dp_ref[:, MLA_KV_LORA + nb : MLA_KV_LORA + 2 * nb] = dvb_ref[...].astype(BF16)
        dp_ref[:, MLA_KV_LORA + 2 * nb :] = dkpe_ref[...].astype(BF16)

    return pl.pallas_call(
        body,
        name="kprep_bwd",
        grid=(n // tr,),
        in_specs=[_rows(tr, KVP), _rows(tr, MLA_KV_LORA), _rows(tr, nb), _rows(tr, nb), _rows(tr, LANES),
                  _bcast(MLA_KV_LORA), _bcast(GQA_HEAD_DIM), _rows(tr, LANES), _rows(tr, LANES)],
        out_specs=[_rows(tr, KVP), _bcast(MLA_KV_LORA), _bcast(GQA_HEAD_DIM)],
        out_shape=[jax.ShapeDtypeStruct((n, KVP), BF16), jax.ShapeDtypeStruct((1, MLA_KV_LORA), F32),
                   jax.ShapeDtypeStruct((1, GQA_HEAD_DIM), F32)],
        compiler_params=_params(("arbitrary",)),
    )(pkv, dckv, dkb, dvb, dkpe, kvg, kg, cos_b, ss_b)


def _kgrad_split(dka, dva, cos_a, ss_a, tr):
    n = dka.shape[0]
    wk = MLA_HEADS * 2 * LANES

    def body(dk_ref, dv_ref, ca, sa, dkv_ref, dkpe_ref):
        even = jnp.zeros((tr, LANES), F32)
        odd = jnp.zeros((tr, LANES), F32)
        for h in range(MLA_HEADS):
            dkv_ref[:, 2 * h * LANES : (2 * h + 1) * LANES] = dk_ref[:, 2 * h * LANES : (2 * h + 1) * LANES].astype(BF16)
            dkv_ref[:, (2 * h + 1) * LANES : (2 * h + 2) * LANES] = dv_ref[:, h * MLA_V : (h + 1) * MLA_V].astype(BF16)
            part = dk_ref[:, (2 * h + 1) * LANES : (2 * h + 2) * LANES]
            if h % 2 == 0:
                even = even + part
            else:
                odd = odd + part
        lane = lax.broadcasted_iota(jnp.int32, (tr, LANES), 1)
        low = lane < MLA_ROPE
        both = jnp.where(low, even, odd)
        tot = jnp.where(low, both + pltpu.roll(both, MLA_ROPE, 1), 0.0)
        dkpe_ref[...] = _rope_t(tot, ca[...], sa[...], _QA)

    return pl.pallas_call(
        body,
        name="kgrad_split",
        grid=(n // tr,),
        in_specs=[_rows(tr, wk), _rows(tr, MLA_HEADS * MLA_V), _rows(tr, LANES), _rows(tr, LANES)],
        out_specs=[_rows(tr, wk), _rows(tr, LANES)],
        out_shape=[jax.ShapeDtypeStruct((n, wk), BF16), jax.ShapeDtypeStruct((n, LANES), F32)],
        compiler_params=_params(("parallel",)),
    )(dka, dva, cos_a, ss_a)


def _qprep_fwd(pq, qg, gq, cos_b, ss_b, tr):
    n = pq.shape[0]
    nq = GQA_HEADS * GQA_HEAD_DIM

    def body(p_ref, qg_ref, gq_ref, cb, sb, cq_ref, qb_ref):
        xh, _ = _rms(p_ref[:, :MLA_Q_LORA])
        cq_ref[...] = (xh * qg_ref[...]).astype(BF16)
        for h in range(GQA_HEADS):
            lo = MLA_Q_LORA + h * GQA_HEAD_DIM
            qh, _ = _rms(p_ref[:, lo : lo + GQA_HEAD_DIM])
            qb_ref[:, h * GQA_HEAD_DIM : (h + 1) * GQA_HEAD_DIM] = _rope(qh * gq_ref[...], cb[...], sb[...], _QB).astype(BF16)

    return pl.pallas_call(
        body,
        name="qprep_fwd",
        grid=(n // tr,),
        in_specs=[_rows(tr, QC), _bcast(MLA_Q_LORA), _bcast(GQA_HEAD_DIM), _rows(tr, LANES), _rows(tr, LANES)],
        out_specs=[_rows(tr, MLA_Q_LORA), _rows(tr, nq)],
        out_shape=[jax.ShapeDtypeStruct((n, MLA_Q_LORA), BF16), jax.ShapeDtypeStruct((n, nq), BF16)],
        compiler_params=_params(("parallel",)),
    )(pq, qg, gq, cos_b, ss_b)


def _qprep_bwd(pq, dcq, dqb, qg, gq, cos_b, ss_b, tr):
    n = pq.shape[0]
    nq = GQA_HEADS * GQA_HEAD_DIM

    def body(p_ref, dcq_ref, dqb_ref, qg_ref, gq_ref, cb, sb, dp_ref, dqg_ref, dgq_ref):
        _acc_init(pl.program_id(0), [dqg_ref, dgq_ref])
        xh, r = _rms(p_ref[:, :MLA_Q_LORA])
        dn = dcq_ref[...]
        dqg_ref[...] += _csum(dn * xh)
        dp_ref[:, :MLA_Q_LORA] = _rms_bwd(xh, r, dn * qg_ref[...]).astype(BF16)
        for h in range(GQA_HEADS):
            lo = MLA_Q_LORA + h * GQA_HEAD_DIM
            qh, rq = _rms(p_ref[:, lo : lo + GQA_HEAD_DIM])
            dq = _rope_t(dqb_ref[:, h * GQA_HEAD_DIM : (h + 1) * GQA_HEAD_DIM], cb[...], sb[...], _QB)
            dgq_ref[...] += _csum(dq * qh)
            dp_ref[:, lo : lo + GQA_HEAD_DIM] = _rms_bwd(qh, rq, dq * gq_ref[...]).astype(BF16)

    return pl.pallas_call(
        body,
        name="qprep_bwd",
        grid=(n // tr,),
        in_specs=[_rows(tr, QC), _rows(tr, MLA_Q_LORA), _rows(tr, nq), _bcast(MLA_Q_LORA), _bcast(GQA_HEAD_DIM),
                  _rows(tr, LANES), _rows(tr, LANES)],
        out_specs=[_rows(tr, QC), _bcast(MLA_Q_LORA), _bcast(GQA_HEAD_DIM)],
        out_shape=[jax.ShapeDtypeStruct((n, QC), BF16), jax.ShapeDtypeStruct((1, MLA_Q_LORA), F32),
                   jax.ShapeDtypeStruct((1, GQA_HEAD_DIM), F32)],
        compiler_params=_params(("arbitrary",)),
    )(pq, dcq, dqb, qg, gq, cos_b, ss_b)


_QA_COLS = MLA_HEADS * (MLA_NOPE + MLA_ROPE)


def _qrope_fwd(qa, cos_a, ss_a, tr):
    n = qa.shape[0]

    def body(q_ref, ca, sa, o_ref):
        for j in range(MLA_HEADS // 2):
            lo = 3 * j * LANES
            o_ref[:, lo : lo + 2 * LANES] = q_ref[:, lo : lo + 2 * LANES].astype(BF16)
            o_ref[:, lo + 2 * LANES : lo + 3 * LANES] = _rope(q_ref[:, lo + 2 * LANES : lo + 3 * LANES], ca[...], sa[...], _QA).astype(BF16)

    return pl.pallas_call(
        body,
        name="qrope_fwd",
        grid=(n // tr,),
        in_specs=[_rows(tr, _QA_COLS), _rows(tr, LANES), _rows(tr, LANES)],
        out_specs=_rows(tr, _QA_COLS),
        out_shape=jax.ShapeDtypeStruct((n, _QA_COLS), BF16),
        compiler_params=_params(("parallel",)),
    )(qa, cos_a, ss_a)


def _qrope_bwd(dq2, cos_a, ss_a, tr):
    n = dq2.shape[0]

    def body(d_ref, ca, sa, o_ref):
        for j in range(MLA_HEADS // 2):
            lo = 3 * j * LANES
            h0, h1 = 2 * j, 2 * j + 1
            o_ref[:, lo : lo + LANES] = d_ref[:, 2 * h0 * LANES : (2 * h0 + 1) * LANES].astype(BF16)
            o_ref[:, lo + LANES : lo + 2 * LANES] = d_ref[:, 2 * h1 * LANES : (2 * h1 + 1) * LANES].astype(BF16)
            pe = d_ref[:, (2 * h0 + 1) * LANES : (2 * h0 + 2) * LANES] + d_ref[:, (2 * h1 + 1) * LANES : (2 * h1 + 2) * LANES]
            o_ref[:, lo + 2 * LANES : lo + 3 * LANES] = _rope_t(pe, ca[...], sa[...], _QA).astype(BF16)

    return pl.pallas_call(
        body,
        name="qrope_bwd",
        grid=(n // tr,),
        in_specs=[_rows(tr, MLA_HEADS * 2 * LANES), _rows(tr, LANES), _rows(tr, LANES)],
        out_specs=_rows(tr, _QA_COLS),
        out_shape=jax.ShapeDtypeStruct((n, _QA_COLS), BF16),
        compiler_params=_params(("parallel",)),
    )(dq2, cos_a, ss_a)


def _cat(refs):
    vals = [r[...] for r in refs]
    return vals[0] if len(vals) == 1 else jnp.concatenate(vals, axis=-1)


def _attn_fwd(qparts, kparts, vpart, n_heads, group, dv, scale, name, tq):
    T, Tk = qparts[0][0].shape[0], kparts[0][0].shape[0]
    nq_, nk_ = len(qparts), len(kparts)

    def body(*refs):
        q = _cat(refs[:nq_])
        k = _cat(refs[nq_ : nq_ + nk_])
        v_ref, o_ref, lse_ref = refs[nq_ + nk_ :]
        s = lax.dot_general(q, k, _DIMS["NT"], preferred_element_type=F32) * scale
        m = jnp.max(s, axis=-1, keepdims=True)
        p = jnp.exp(s - m)
        l = jnp.sum(p, axis=-1, keepdims=True)
        pn = (p * (1.0 / l)).astype(BF16)
        o_ref[...] = jnp.dot(pn, v_ref[...], preferred_element_type=F32).astype(BF16)
        lse_ref[...] = m + jnp.log(l)

    in_specs = [pl.BlockSpec((tq, LANES), lambda h, i, f=f: (i, f(h))) for _, f in qparts]
    in_specs += [pl.BlockSpec((Tk, LANES), lambda h, i, f=f: (0, f(h // group))) for _, f in kparts]
    fv = vpart[1]
    in_specs.append(pl.BlockSpec((Tk, dv), lambda h, i: (0, fv(h // group))))
    return pl.pallas_call(
        body,
        name=name,
        grid=(n_heads, T // tq),
        in_specs=in_specs,
        out_specs=[pl.BlockSpec((tq, dv), lambda h, i: (i, h)), pl.BlockSpec((None, tq, 1), lambda h, i: (h, i, 0))],
        out_shape=[jax.ShapeDtypeStruct((T, n_heads * dv), BF16), jax.ShapeDtypeStruct((n_heads, T, 1), F32)],
        compiler_params=_params(("parallel", "parallel")),
    )(*[a for a, _ in qparts], *[a for a, _ in kparts], vpart[0])


def _attn_bwd(qparts, kparts, vpart, o, do, lse, n_heads, group, dv, scale, name, tq):
    T, Tk = qparts[0][0].shape[0], kparts[0][0].shape[0]
    nq_, nk_ = len(qparts), len(kparts)
    dk_ = LANES * nq_
    n_kv = n_heads // group
    nblk = T // tq

    def head(hk, i):
        return hk * group + i // nblk

    def body(*refs):
        q = _cat(refs[:nq_])
        k = _cat(refs[nq_ : nq_ + nk_])
        v_ref, o_ref, do_ref, lse_ref, dq_ref, dk_ref, dv_ref = refs[nq_ + nk_ :]
        _acc_init(pl.program_id(1), [dk_ref, dv_ref])
        s = lax.dot_general(q, k, _DIMS["NT"], preferred_element_type=F32) * scale
        p = jnp.exp(s - lse_ref[...])
        dov = do_ref[...]
        dp = lax.dot_general(dov, v_ref[...], _DIMS["NT"], preferred_element_type=F32)
        delta = jnp.sum(dov.astype(F32) * o_ref[...].astype(F32), axis=-1, keepdims=True)
        ds = (p * (dp - delta) * scale).astype(BF16)
        dq_ref[...] = jnp.dot(ds, k, preferred_element_type=F32)
        dk_ref[...] += lax.dot_general(ds, q, _DIMS["TN"], preferred_element_type=F32)
        dv_ref[...] += lax.dot_general(p.astype(BF16), dov, _DIMS["TN"], preferred_element_type=F32)

    in_specs = [pl.BlockSpec((tq, LANES), lambda hk, i, f=f: (i % nblk, f(head(hk, i)))) for _, f in qparts]
    in_specs += [pl.BlockSpec((Tk, LANES), lambda hk, i, f=f: (0, f(hk))) for _, f in kparts]
    fv = vpart[1]
    in_specs.append(pl.BlockSpec((Tk, dv), lambda hk, i: (0, fv(hk))))
    in_specs += [pl.BlockSpec((tq, dv), lambda hk, i: (i % nblk, head(hk, i)))] * 2
    in_specs.append(pl.BlockSpec((None, tq, 1), lambda hk, i: (head(hk, i), i % nblk, 0)))
    return pl.pallas_call(
        body,
        name=name,
        grid=(n_kv, group * nblk),
        in_specs=in_specs,
        out_specs=[pl.BlockSpec((tq, dk_), lambda hk, i: (i % nblk, head(hk, i))),
                   pl.BlockSpec((Tk, dk_), lambda hk, i: (0, hk)),
                   pl.BlockSpec((Tk, dv), lambda hk, i: (0, hk))],
        out_shape=[jax.ShapeDtypeStruct((T, n_heads * dk_), F32), jax.ShapeDtypeStruct((Tk, n_kv * dk_), F32),
                   jax.ShapeDtypeStruct((Tk, n_kv * dv), F32)],
        compiler_params=_params(("parallel", "arbitrary")),
    )(*[a for a, _ in qparts], *[a for a, _ in kparts], vpart[0], o, do, lse)


def _gates_fwd(pg, ya, yb, tr):
    n, d = ya.shape

    def body(pg_ref, ya_ref, yb_ref, o_ref):
        ga = jax.nn.sigmoid(pg_ref[:, :d])
        gb = jax.nn.sigmoid(pg_ref[:, d:])
        o_ref[...] = (ga * ya_ref[...] + gb * yb_ref[...]).astype(BF16)

    return pl.pallas_call(
        body,
        name="gates_fwd",
        grid=(n // tr,),
        in_specs=[_rows(tr, 2 * d), _rows(tr, d), _rows(tr, d)],
        out_specs=_rows(tr, d),
        out_shape=jax.ShapeDtypeStruct((n, d), BF16),
        compiler_params=_params(("parallel",)),
    )(pg, ya, yb)


def _gates_bwd(dm, pg, ya, yb, tr):
    n, d = ya.shape

    def body(dm_ref, pg_ref, ya_ref, yb_ref, dya_ref, dyb_ref, dpg_ref):
        dmv = dm_ref[...]
        ga = jax.nn.sigmoid(pg_ref[:, :d])
        gb = jax.nn.sigmoid(pg_ref[:, d:])
        dya_ref[...] = (dmv * ga).astype(BF16)
        dyb_ref[...] = (dmv * gb).astype(BF16)
        dpg_ref[:, :d] = (dmv * ya_ref[...] * ga * (1.0 - ga)).astype(BF16)
        dpg_ref[:, d:] = (dmv * yb_ref[...] * gb * (1.0 - gb)).astype(BF16)

    return pl.pallas_call(
        body,
        name="gates_bwd",
        grid=(n // tr,),
        in_specs=[_rows(tr, d), _rows(tr, 2 * d), _rows(tr, d), _rows(tr, d)],
        out_specs=[_rows(tr, d), _rows(tr, d), _rows(tr, 2 * d)],
        out_shape=[jax.ShapeDtypeStruct((n, d), BF16), jax.ShapeDtypeStruct((n, d), BF16), jax.ShapeDtypeStruct((n, 2 * d), BF16)],
        compiler_params=_params(("parallel",)),
    )(dm, pg, ya, yb)


def _resid_norm2_fwd(x2d, att, g1, n2g, sh2, sc2, tr):
    n, d = x2d.shape

    def body(x_ref, a_ref, g1_ref, g_ref, sh_ref, sc_ref, x1_ref, z_ref):
        x1 = x_ref[...] + g1_ref[...] * a_ref[...]
        x1_ref[...] = x1
        xh, _ = _rms(x1)
        z_ref[...] = ((xh * g_ref[...]) * (1.0 + sc_ref[...]) + sh_ref[...]).astype(BF16)

    return pl.pallas_call(
        body,
        name="resid_norm2_fwd",
        grid=(n // tr,),
        in_specs=[_rows(tr, d), _rows(tr, d)] + [_bcast(d)] * 4,
        out_specs=[_rows(tr, d), _rows(tr, d)],
        out_shape=[jax.ShapeDtypeStruct((n, d), F32), jax.ShapeDtypeStruct((n, d), BF16)],
        compiler_params=_params(("parallel",)),
    )(x2d, att, g1, n2g, sh2, sc2)


def _resid_norm2_bwd(dz2, x1, dx2, att, n2g, sc2, g1, tr):
    n, d = x1.shape

    def body(dz_ref, x1_ref, dx2_ref, a_ref, g_ref, sc_ref, g1_ref, dx1_ref, da_ref, dg_ref, dsh_ref, dsc_ref, dg1_ref):
        _acc_init(pl.program_id(0), [dg_ref, dsh_ref, dsc_ref, dg1_ref])
        xh, r = _rms(x1_ref[...])
        dzv = dz_ref[...]
        gv = g_ref[...]
        dsc_ref[...] += _csum(dzv * (xh * gv))
        dsh_ref[...] += _csum(dzv)
        dh = dzv * (1.0 + sc_ref[...])
        dg_ref[...] += _csum(dh * xh)
        dx1 = _rms_bwd(xh, r, dh * gv) + dx2_ref[...]
        dx1_ref[...] = dx1
        dg1_ref[...] += _csum(dx1 * a_ref[...])
        da_ref[...] = (dx1 * g1_ref[...]).astype(BF16)

    return pl.pallas_call(
        body,
        name="resid_norm2_bwd",
        grid=(n // tr,),
        in_specs=[_rows(tr, d)] * 4 + [_bcast(d)] * 3,
        out_specs=[_rows(tr, d), _rows(tr, d)] + [_bcast(d)] * 4,
        out_shape=[jax.ShapeDtypeStruct((n, d), F32), jax.ShapeDtypeStruct((n, d), BF16)] + [jax.ShapeDtypeStruct((1, d), F32)] * 4,
        compiler_params=_params(("arbitrary",)),
    )(dz2, x1, dx2, att, n2g, sc2, g1)


def _shift_prev(u):
    row = lax.broadcasted_iota(jnp.int32, u.shape, 0)
    return jnp.where(row == 0, 0.0, pltpu.roll(u, 1, 0))


def _shift_next(u):
    n = u.shape[0]
    row = lax.broadcasted_iota(jnp.int32, u.shape, 0)
    return jnp.where(row == n - 1, 0.0, pltpu.roll(u, n - 1, 0))


def _conv3(u, w_ref, b_ref):
    return b_ref[...] + w_ref[0:1, :] * _shift_prev(u) + w_ref[1:2, :] * u + w_ref[2:3, :] * _shift_next(u)


def _conv_fwd(u, cw, cb, tc):
    n, two_f = u.shape
    f = two_f // 2
    nb = f // tc

    def body(ua_ref, ub_ref, wa_ref, wb_ref, ba_ref, bb_ref, h_ref):
        a = _conv3(ua_ref[...].astype(F32), wa_ref, ba_ref)
        b = _conv3(ub_ref[...].astype(F32), wb_ref, bb_ref)
        h_ref[...] = (a * jax.nn.sigmoid(a) * b).astype(BF16)

    col = lambda rows, off: pl.BlockSpec((rows, tc), lambda i: (0, i + off))
    return pl.pallas_call(
        body,
        name="conv_fwd",
        grid=(nb,),
        in_specs=[col(n, 0), col(n, nb), col(3, 0), col(3, nb), col(1, 0), col(1, nb)],
        out_specs=col(n, 0),
        out_shape=jax.ShapeDtypeStruct((n, f), BF16),
        compiler_params=_params(("parallel",)),
    )(u, u, cw, cw, cb, cb)


def _conv_bwd(u, dh, cw, cb, tc):
    n, two_f = u.shape
    f = two_f // 2
    nb = f // tc

    def part(uv, duc, w_ref, du_ref, dw_ref, db_ref):
        db_ref[...] = _csum(duc)
        dw_ref[0:1, :] = _csum(duc * _shift_prev(uv))
        dw_ref[1:2, :] = _csum(duc * uv)
        dw_ref[2:3, :] = _csum(duc * _shift_next(uv))
        du_ref[...] = (w_ref[0:1, :] * _shift_next(duc) + w_ref[1:2, :] * duc + w_ref[2:3, :] * _shift_prev(duc)).astype(BF16)

    def body(ua_ref, ub_ref, dh_ref, wa_ref, wb_ref, ba_ref, bb_ref, dua_ref, dub_ref, dwa_ref, dwb_ref, dba_ref, dbb_ref):
        ua = ua_ref[...].astype(F32)
        ub = ub_ref[...].astype(F32)
        a = _conv3(ua, wa_ref, ba_ref)
        b = _conv3(ub, wb_ref, bb_ref)
        dhv = dh_ref[...].astype(F32)
        sg = jax.nn.sigmoid(a)
        da = dhv * b * (sg * (1.0 + a * (1.0 - sg)))
        db = dhv * (a * sg)
        part(ua, da, wa_ref, dua_ref, dwa_ref, dba_ref)
        part(ub, db, wb_ref, dub_ref, dwb_ref, dbb_ref)

    col = lambda rows, off: pl.BlockSpec((rows, tc), lambda i: (0, i + off))
    return pl.pallas_call(
        body,
        name="conv_bwd",
        grid=(nb,),
        in_specs=[col(n, 0), col(n, nb), col(n, 0), col(3, 0), col(3, nb), col(1, 0), col(1, nb)],
        out_specs=[col(n, 0), col(n, 0), col(3, 0), col(3, 0), col(1, 0), col(1, 0)],
        out_shape=[jax.ShapeDtypeStruct((n, f), BF16)] * 2 + [jax.ShapeDtypeStruct((3, f), F32)] * 2 + [jax.ShapeDtypeStruct((1, f), F32)] * 2,
        compiler_params=_params(("parallel",)),
    )(u, u, dh, cw, cw, cb, cb)


def _loss_head(x1, f, g2, fg, tgt, tr):
    n, d = x1.shape

    def body(x1_ref, f_ref, g2_ref, fg_ref, t_ref, sq_ref, dx2_ref, dfg_ref, dg2_ref, df_ref):
        _acc_init(pl.program_id(0), [sq_ref, dfg_ref, dg2_ref])
        fv = f_ref[...]
        xh, r = _rms(x1_ref[...] + g2_ref[...] * fv)
        err = xh * fg_ref[...] - t_ref[...]
        sq_ref[...] += _csum(err * err)
        dy = err * (1.0 / d)
        dfg_ref[...] += _csum(dy * xh)
        dx2 = _rms_bwd(xh, r, dy * fg_ref[...])
        dx2_ref[...] = dx2
        dg2_ref[...] += _csum(dx2 * fv)
        df_ref[...] = (dx2 * g2_ref[...]).astype(BF16)

    return pl.pallas_call(
        body,
        name="loss_head",
        grid=(n // tr,),
        in_specs=[_rows(tr, d), _rows(tr, d), _bcast(d), _bcast(d), _rows(tr, d)],
        out_specs=[_bcast(d), _rows(tr, d), _bcast(d), _bcast(d), _rows(tr, d)],
        out_shape=[jax.ShapeDtypeStruct((1, d), F32), jax.ShapeDtypeStruct((n, d), F32), jax.ShapeDtypeStruct((1, d), F32),
                   jax.ShapeDtypeStruct((1, d), F32), jax.ShapeDtypeStruct((n, d), BF16)],
        compiler_params=_params(("arbitrary",)),
    )(x1, f, g2, fg, tgt)


def _sum_slots(g, name):
    s, r, w = g.shape

    def body(g_ref, o_ref):
        acc = g_ref[0]
        for k in range(1, s):
            acc = acc + g_ref[k]
        o_ref[...] = acc

    return pl.pallas_call(body, name=name, out_shape=jax.ShapeDtypeStruct((r, w), F32))(g)


def _silu_grad_mul(ds, cvec):
    def body(d_ref, c_ref, o_ref):
        cv = c_ref[...]
        sg = jax.nn.sigmoid(cv)
        o_ref[...] = d_ref[...] * (sg * (1.0 + cv * (1.0 - sg)))

    return pl.pallas_call(body, name="silu_grad_mul", out_shape=jax.ShapeDtypeStruct(ds.shape, F32))(ds, cvec)


def _adamw(w, g, m, v, name):
    r, cdim = w.shape
    tr = _pick(r, max(8, (1 << 18) // cdim), 8)
    b1c = 1.0 - ADAM_B1**ADAM_STEP
    b2c = 1.0 - ADAM_B2**ADAM_STEP

    def body(w_ref, g_ref, m_ref, v_ref, d_ref, mo_ref, vo_ref):
        gv = g_ref[...]
        mn = ADAM_B1 * m_ref[...] + (1.0 - ADAM_B1) * gv
        vn = ADAM_B2 * v_ref[...] + (1.0 - ADAM_B2) * (gv * gv)
        mo_ref[...] = mn
        vo_ref[...] = vn
        d_ref[...] = -ADAM_LR * ((mn / b1c) / (jnp.sqrt(vn / b2c) + ADAM_EPS) + ADAM_WD * w_ref[...])

    spec = pl.BlockSpec((tr, cdim), lambda i: (i, 0))
    return pl.pallas_call(
        body,
        name=name,
        grid=(r // tr,),
        in_specs=[spec] * 4,
        out_specs=[spec] * 3,
        out_shape=[jax.ShapeDtypeStruct((r, cdim), F32)] * 3,
        compiler_params=_params(("parallel",)),
    )(w, g, m, v)


def _place():
    return lax.axis_index("x"), lax.axis_index("y"), lax.axis_index("c")


def _remote(src, dst, send_sem, recv_sem, dev):
    return pltpu.make_async_remote_copy(src_ref=src, dst_ref=dst, send_sem=send_sem, recv_sem=recv_sem, device_id=dev, device_id_type=MESH)


ANY = pl.BlockSpec(memory_space=pl.ANY)


def _all_gather_small(v, name):
    r, w = v.shape

    def body(v_ref, o_ref, send, recv, lsem):
        x, y, c = _place()
        me = 4 * x + 2 * y + c
        mine = pltpu.make_async_copy(v_ref, o_ref.at[me], lsem)
        mine.start()
        sent = []
        for k in range(1, 8):
            px, py, pc = x ^ (k >> 2), y ^ ((k >> 1) & 1), c ^ (k & 1)
            cp = _remote(v_ref, o_ref.at[me], send.at[k - 1], recv.at[k - 1], (px, py, pc))
            cp.start()
            sent.append(cp)
        for k in range(1, 8):
            px, py, pc = x ^ (k >> 2), y ^ ((k >> 1) & 1), c ^ (k & 1)
            slot = o_ref.at[4 * px + 2 * py + pc]
            _remote(slot, slot, send.at[k - 1], recv.at[k - 1], (x, y, c)).wait_recv()
        for cp in sent:
            cp.wait_send()
        mine.wait()

    return pl.pallas_call(
        body,
        name=name,
        out_shape=jax.ShapeDtypeStruct((8, r, w), F32),
        in_specs=[pl.BlockSpec(memory_space=pltpu.VMEM)],
        out_specs=pl.BlockSpec(memory_space=pltpu.VMEM),
        scratch_shapes=[pltpu.SemaphoreType.DMA((7,)), pltpu.SemaphoreType.DMA((7,)), pltpu.SemaphoreType.DMA],
        compiler_params=pltpu.CompilerParams(vmem_limit_bytes=VMEM_LIMIT),
    )(v)


def _all_gather_weights(shards):
    n = len(shards)

    def body(*refs):
        ins, outs = refs[:n], refs[n : 2 * n]
        send, recv, lsem = refs[2 * n :]
        x, y, c = _place()
        j = 2 * x + y
        chips = [(1 - x, y), (x, 1 - y), (1 - x, 1 - y)]
        local = []
        for a in range(n):
            cp = pltpu.make_async_copy(ins[a], outs[a].at[j], lsem.at[a])
            cp.start()
            local.append(cp)
        started = []
        for a in range(n):
            for k, (px, py) in enumerate(chips):
                cp = _remote(ins[a].at[c], outs[a].at[j, c], send.at[6 * a + k], recv.at[6 * a + k], (px, py, c))
                cp.start()
                started.append(cp)
        for a in range(n):
            for k, (px, py) in enumerate(chips):
                blk = outs[a].at[2 * px + py, c]
                _remote(blk, blk, send.at[6 * a + k], recv.at[6 * a + k], (x, y, c)).wait_recv()
                cp = _remote(blk, blk, send.at[6 * a + 3 + k], recv.at[6 * a + 3 + k], (x, y, 1 - c))
                cp.start()
                started.append(cp)
        for a in range(n):
            for k, (px, py) in enumerate(chips):
                blk = outs[a].at[2 * px + py, 1 - c]
                _remote(blk, blk, send.at[6 * a + 3 + k], recv.at[6 * a + 3 + k], (x, y, c)).wait_recv()
        for cp in started:
            cp.wait_send()
        for cp in local:
            cp.wait()

    return pl.pallas_call(
        body,
        name="all_gather_weights",
        out_shape=[jax.ShapeDtypeStruct((4,) + s.shape, s.dtype) for s in shards],
        in_specs=[ANY] * n,
        out_specs=[ANY] * n,
        scratch_shapes=[pltpu.SemaphoreType.DMA((6 * n,)), pltpu.SemaphoreType.DMA((6 * n,)), pltpu.SemaphoreType.DMA((n,))],
    )(*shards)


def _swap_halves(grads):
    n = len(grads)

    def body(*refs):
        ins, outs = refs[:n], refs[n : 2 * n]
        send, recv = refs[2 * n :]
        x, y, c = _place()
        started = []
        for a in range(n):
            for s in range(4):
                cp = _remote(ins[a].at[s, 1 - c], outs[a].at[s], send.at[4 * a + s], recv.at[4 * a + s], (x, y, 1 - c))
                cp.start()
                started.append(cp)
        for cp in started:
            cp.wait_recv()
        for cp in started:
            cp.wait_send()

    return pl.pallas_call(
        body,
        name="swap_halves",
        out_shape=[jax.ShapeDtypeStruct((4,) + g.shape[2:], g.dtype) for g in grads],
        in_specs=[ANY] * n,
        out_specs=[ANY] * n,
        scratch_shapes=[pltpu.SemaphoreType.DMA((4 * n,)), pltpu.SemaphoreType.DMA((4 * n,))],
    )(*grads)


def _add_halves(grads, others):
    outs = []
    for a, (g, o) in enumerate(zip(grads, others)):
        _, _, rh, cdim = g.shape
        tr = _pick(rh, 512, 16)

        def body(g_ref, o_ref, p_ref):
            c = lax.axis_index("c")
            own = jnp.where(c == 0, g_ref[0].astype(F32), g_ref[1].astype(F32))
            p_ref[...] = (own + o_ref[...].astype(F32)).astype(BF16)

        outs.append(
            pl.pallas_call(
                body,
                name=f"add_halves_{a}",
                grid=(4, rh // tr),
                in_specs=[pl.BlockSpec((None, 2, tr, cdim), lambda s, i: (s, 0, i, 0)), pl.BlockSpec((None, tr, cdim), lambda s, i: (s, i, 0))],
                out_specs=pl.BlockSpec((None, tr, cdim), lambda s, i: (s, i, 0)),
                out_shape=jax.ShapeDtypeStruct((4, rh, cdim), BF16),
                compiler_params=_params(("parallel", "parallel")),
            )(g, o)
        )
    return outs


def _exchange_shards(parts):
    n = len(parts)

    def body(*refs):
        ins, outs = refs[:n], refs[n : 2 * n]
        send, recv, lsem = refs[2 * n :]
        x, y, c = _place()
        j = 2 * x + y
        chips = [(1 - x, y), (x, 1 - y), (1 - x, 1 - y)]
        local, started = [], []
        for a in range(n):
            cp = pltpu.make_async_copy(ins[a].at[j], outs[a].at[j], lsem.at[a])
            cp.start()
            local.append(cp)
            for k, (px, py) in enumerate(chips):
                cp = _remote(ins[a].at[2 * px + py], outs[a].at[j], send.at[3 * a + k], recv.at[3 * a + k], (px, py, c))
                cp.start()
                started.append(cp)
        for a in range(n):
            for k, (px, py) in enumerate(chips):
                slot = outs[a].at[2 * px + py]
                _remote(slot, slot, send.at[3 * a + k], recv.at[3 * a + k], (x, y, c)).wait_recv()
        for cp in started:
            cp.wait_send()
        for cp in local:
            cp.wait()

    return pl.pallas_call(
        body,
        name="exchange_shards",
        out_shape=[jax.ShapeDtypeStruct(p.shape, p.dtype) for p in parts],
        in_specs=[ANY] * n,
        out_specs=[ANY] * n,
        scratch_shapes=[pltpu.SemaphoreType.DMA((3 * n,)), pltpu.SemaphoreType.DMA((3 * n,)), pltpu.SemaphoreType.DMA((n,))],
    )(*parts)


def _sum_chips(recvd):
    outs = []
    for a, g in enumerate(recvd):
        _, rh, cdim = g.shape
        tr = _pick(rh, 512, 16)

        def body(g_ref, o_ref):
            o_ref[...] = ((g_ref[0].astype(F32) + g_ref[1].astype(F32)) + g_ref[2].astype(F32)) + g_ref[3].astype(F32)

        outs.append(
            pl.pallas_call(
                body,
                name=f"sum_chips_{a}",
                grid=(rh // tr,),
                in_specs=[pl.BlockSpec((4, tr, cdim), lambda i: (0, i, 0))],
                out_specs=pl.BlockSpec((tr, cdim), lambda i: (i, 0)),
                out_shape=jax.ShapeDtypeStruct((rh, cdim), F32),
                compiler_params=_params(("parallel",)),
            )(g)
        )
    return outs


def _join_halves(halves):
    n = len(halves)

    def body(*refs):
        ins, outs = refs[:n], refs[n : 2 * n]
        send, recv, lsem = refs[2 * n :]
        x, y, c = _place()
        local, started = [], []
        for a in range(n):
            cp = pltpu.make_async_copy(ins[a], outs[a].at[c], lsem.at[a])
            cp.start()
            local.append(cp)
            cp = _remote(ins[a], outs[a].at[c], send.at[a], recv.at[a], (x, y, 1 - c))
            cp.start()
            started.append(cp)
        for a in range(n):
            slot = outs[a].at[1 - c]
            _remote(slot, slot, send.at[a], recv.at[a], (x, y, c)).wait_recv()
        for cp in started:
            cp.wait_send()
        for cp in local:
            cp.wait()

    return pl.pallas_call(
        body,
        name="join_halves",
        out_shape=[jax.ShapeDtypeStruct((2,) + h.shape, h.dtype) for h in halves],
        in_specs=[ANY] * n,
        out_specs=[ANY] * n,
        scratch_shapes=[pltpu.SemaphoreType.DMA((n,)), pltpu.SemaphoreType.DMA((n,)), pltpu.SemaphoreType.DMA((n,))],
    )(*halves)


def _reduce_scatter(grads):
    views = [g.reshape(4, 2, g.shape[0] // 8, g.shape[1]) for g in grads]
    mine = _add_halves(views, _swap_halves(views))
    halves = _sum_chips(_exchange_shards(mine))
    return [f.reshape(2 * f.shape[1], f.shape[2]) for f in _join_halves(halves)]


def _t_bf16(w):
    return w.T.astype(BF16)


def kernel(x, c, ctx, c_ctx, w_ada, b_ada, norm1_g, w_in, mla_q_norm_g, w_q_up, mla_kv_norm_g, w_kv_up, gqa_q_norm_g, gqa_k_norm_g, w_br_a, w_br_b, w_out, norm2_g, w_up, conv_w, conv_b, w_down, final_norm_g, loss_target, m_c_ctx, m_w_ada, m_b_ada, m_norm1_g, m_w_in, m_mla_q_norm_g, m_w_q_up, m_mla_kv_norm_g, m_w_kv_up, m_gqa_q_norm_g, m_gqa_k_norm_g, m_w_br_a, m_w_br_b, m_w_out, m_norm2_g, m_w_up, m_conv_w, m_conv_b, m_w_down, m_final_norm_g, v_c_ctx, v_w_ada, v_b_ada, v_norm1_g, v_w_in, v_mla_q_norm_g, v_w_q_up, v_mla_kv_norm_g, v_w_kv_up, v_gqa_q_norm_g, v_gqa_k_norm_g, v_w_br_a, v_w_br_b, v_w_out, v_norm2_g, v_w_up, v_conv_w, v_conv_b, v_w_down, v_final_norm_g):
    T, D = x.shape[1], x.shape[2]
    C = ctx.shape[1]
    NA = w_ada.shape[2]
    NW = w_up.shape[2]
    F2 = 4 * NW
    FF = F2 // 2
    xi, yi, ci = _place()
    j = 2 * xi + yi
    me = 4 * xi + 2 * yi + ci
    tr = _pick(C, 128, 8)
    tq = _pick(T, 256)

    x2d, tgt, ctx2d = x[0], loss_target[0], ctx[0]
    fg = final_norm_g.reshape(1, D)
    cc = c_ctx.reshape(1, D)

    w0 = max(D, NW)
    pay = jnp.zeros((8, w0), F32).at[0:1, :D].set(c).at[1:4, :NW].set(conv_w[0])
    got = _all_gather_small(pay, "gather_cond")
    c_all = got[:, 0, :D]
    cw = jnp.concatenate([got[2 * s, 1:4, :NW] for s in range(4)], axis=1)
    s16 = jnp.concatenate([c_all, cc, jnp.zeros((7, D), F32)], axis=0)
    b_cols = lax.dynamic_slice(b_ada, (0, j * NA), (1, NA))
    ada_part = _mm(s16, w_ada[0], "NN", F32, "ada_fwd", act="silu", bias=b_cols)
    got = _all_gather_small(ada_part, "gather_ada")
    ada = jnp.concatenate([got[2 * s] for s in range(4)], axis=1)
    lat = lax.dynamic_slice(ada, (me, 0), (1, 6 * D))
    sh1, sc1, g1, sh2, sc2, g2 = [lat[:, k * D : (k + 1) * D] for k in range(6)]
    csh, csc = ada[8:9, :D], ada[8:9, D : 2 * D]

    wq3 = w_q_up[0].reshape(MLA_Q_LORA, 2, MLA_NOPE + MLA_ROPE)
    wq_perm = jnp.concatenate([wq3[:, :, :MLA_NOPE].reshape(MLA_Q_LORA, -1), wq3[:, :, MLA_NOPE:].reshape(MLA_Q_LORA, -1)], axis=1)
    shards = [_t_bf16(w_in[0]), _t_bf16(wq_perm), _t_bf16(w_kv_up[0]), _t_bf16(w_br_a[0]), _t_bf16(w_br_b[0]),
              w_out[0].astype(BF16), _t_bf16(w_up[0]), w_down[0].astype(BF16)]
    full = _all_gather_weights([s.reshape(2, s.shape[0] // 2, s.shape[1]) for s in shards])
    win_t, wq_t, wkv_t, wbra_t, wbrb_t, wout, wup_t, wdown = [f.reshape(4 * s.shape[0], s.shape[1]) for f, s in zip(full, shards)]
    kv_cols = KVP - LANES + MLA_ROPE
    e_kpe = MLA_KV_LORA + MLA_ROPE
    w_kvp = jnp.concatenate([win_t[:MLA_KV_LORA], win_t[e_kpe:kv_cols], win_t[MLA_KV_LORA:e_kpe], jnp.zeros((LANES - MLA_ROPE, D), BF16)], axis=0)
    w_q = win_t[kv_cols : kv_cols + QC]
    w_g = win_t[kv_cols + QC :]
    w_inp = jnp.concatenate([w_kvp, w_q, w_g], axis=0)

    cos_a, ss_a = _rope_tables(C, T, MLA_ROPE)
    cos_b, ss_b = _rope_tables(C, T, GQA_HEAD_DIM)
    lcos_a, lss_a, lcos_b, lss_b = cos_a[C:], ss_a[C:], cos_b[C:], ss_b[C:]

    z_ctx = _norm_mod_fwd(ctx2d, norm1_g, csh, csc, "norm1_ctx_fwd", tr)
    z_lat = _norm_mod_fwd(x2d, norm1_g, sh1, sc1, "norm1_lat_fwd", tr)
    z_all = jnp.concatenate([z_ctx, z_lat], axis=0)
    pkv = _mm(z_all, w_kvp, "NT", F32, "proj_kv")
    pq = _mm(z_lat, w_q, "NT", F32, "proj_q")
    pg = _mm(z_lat, w_g, "NT", F32, "proj_g")
    ckv_n, kb2, vb2, kpe2 = _kprep_fwd(pkv, mla_kv_norm_g, gqa_k_norm_g, cos_a, ss_a, cos_b, ss_b, tr)
    kv_up = _mm(ckv_n, wkv_t, "NT", BF16, "kv_up")
    cq_n, qb2 = _qprep_fwd(pq, mla_q_norm_g, gqa_q_norm_g, lcos_b, lss_b, tr)
    q_a = _mm(cq_n, wq_t, "NT", F32, "q_up")
    qar = _qrope_fwd(q_a, lcos_a, lss_a, tr)

    a_q = [(qar, lambda h: 3 * (h // 2) + h % 2), (qar, lambda h: 3 * (h // 2) + 2)]
    a_k = [(kv_up, lambda h: 2 * h), (kpe2, lambda h: h % 2)]
    a_v = (kv_up, lambda h: 2 * h + 1)
    a_scale = float(MLA_NOPE + MLA_ROPE) ** -0.5
    b_q = [(qb2, lambda h: h)]
    b_k = [(kb2, lambda h: h)]
    b_v = (vb2, lambda h: h)
    b_scale = float(GQA_HEAD_DIM) ** -0.5
    o_a, lse_a = _attn_fwd(a_q, a_k, a_v, MLA_HEADS, 1, MLA_V, a_scale, "attn_a_fwd", tq)
    o_b, lse_b = _attn_fwd(b_q, b_k, b_v, GQA_HEADS, GQA_GROUP, GQA_HEAD_DIM, b_scale, "attn_b_fwd", tq)
    ya = _mm(o_a, wbra_t, "NT", F32, "br_a")
    yb = _mm(o_b, wbrb_t, "NT", F32, "br_b")
    merged = _gates_fwd(pg, ya, yb, tr)
    att = _mm(merged, wout, "NN", F32, "out_proj")
    x1, z2 = _resid_norm2_fwd(x2d, att, g1, norm2_g, sh2, sc2, tr)
    u = _mm(z2, wup_t, "NT", BF16, "ffn_up")
    tc = _pick(FF, 128)
    hg = _conv_fwd(u, cw, conv_b, tc)
    f = _mm(hg, wdown, "NN", F32, "ffn_down")
    sq, dx2, d_fg, d_g2, df = _loss_head(x1, f, g2, fg, tgt, tr)
    loss = lax.psum(0.5 * jnp.sum(sq) / D, ("x", "y", "c"))

    dhg = _mm(df, wdown, "NT", BF16, "ffn_down_dx")
    g_wdown = _mm(hg, df, "TN", BF16, "ffn_down_dw")
    du_a, du_b, dcw_a, dcw_b, dcb_a, dcb_b = _conv_bwd(u, dhg, cw, conv_b, tc)
    du = jnp.concatenate([du_a, du_b], axis=1)
    dz2 = _mm(du, wup_t, "NN", F32, "ffn_up_dx")
    g_wup_t = _mm(du, z2, "TN", BF16, "ffn_up_dw")
    dx1, datt, d_n2g, d_sh2, d_sc2, d_g1 = _resid_norm2_bwd(dz2, x1, dx2, att, norm2_g, sc2, g1, tr)

    dmerged = _mm(datt, wout, "NT", F32, "out_proj_dx")
    g_wout = _mm(merged, datt, "TN", BF16, "out_proj_dw")
    dya, dyb, dpg = _gates_bwd(dmerged, pg, ya, yb, tr)
    do_a = _mm(dya, wbra_t, "NN", BF16, "br_a_dx")
    g_wbra_t = _mm(dya, o_a, "TN", BF16, "br_a_dw")
    do_b = _mm(dyb, wbrb_t, "NN", BF16, "br_b_dx")
    g_wbrb_t = _mm(dyb, o_b, "TN", BF16, "br_b_dw")
    dqa2, dka2, dva2 = _attn_bwd(a_q, a_k, a_v, o_a, do_a, lse_a, MLA_HEADS, 1, MLA_V, a_scale, "attn_a_bwd", tq)
    dqb2, dkb2, dvb2 = _attn_bwd(b_q, b_k, b_v, o_b, do_b, lse_b, GQA_HEADS, GQA_GROUP, GQA_HEAD_DIM, b_scale, "attn_b_bwd", tq)
    dq_a = _qrope_bwd(dqa2, lcos_a, lss_a, tr)
    dcq_n = _mm(dq_a, wq_t, "NN", F32, "q_up_dx")
    g_wq_t = _mm(dq_a, cq_n, "TN", BF16, "q_up_dw")
    dpq, d_qg, d_gq = _qprep_bwd(pq, dcq_n, dqb2, mla_q_norm_g, gqa_q_norm_g, lcos_b, lss_b, tr)
    dkv_up, dkpe = _kgrad_split(dka2, dva2, cos_a, ss_a, tr)
    dckv_n = _mm(dkv_up, wkv_t, "NN", F32, "kv_up_dx")
    g_wkv_t = _mm(dkv_up, ckv_n, "TN", BF16, "kv_up_dw")
    dpkv, d_kvg, d_kg = _kprep_bwd(pkv, dckv_n, dkb2, dvb2, dkpe, mla_kv_norm_g, gqa_k_norm_g, cos_b, ss_b, tr)
    dproj = jnp.concatenate([dpkv, jnp.concatenate([jnp.zeros((C, QC + 2 * D), BF16), jnp.concatenate([dpq, dpg], axis=1)], axis=0)], axis=1)
    dz_all = _mm(dproj, w_inp, "NN", F32, "proj_dx")
    g_winp = _mm(dproj, z_all, "TN", BF16, "proj_dw")
    nk = MLA_KV_LORA + 2 * GQA_KV_HEADS * GQA_HEAD_DIM
    g_win_t = jnp.concatenate([g_winp[:MLA_KV_LORA], g_winp[nk : nk + MLA_ROPE], g_winp[MLA_KV_LORA:nk], g_winp[KVP:]], axis=0)
    _, d_n1g_c, d_csh, d_csc = _norm_mod_bwd(dz_all, 0, ctx2d, norm1_g, csc, None, "norm1_ctx_bwd", tr)
    grad_x, d_n1g_l, d_sh1, d_sc1 = _norm_mod_bwd(dz_all, C // tr, x2d, norm1_g, sc1, dx1, "norm1_lat_bwd", tr)

    zeros_d = jnp.zeros((1, D), F32)
    d_lat = jnp.concatenate([d_sh1, d_sc1, d_g1, d_sh2, d_sc2, d_g2], axis=1)
    d_ctx_part = jnp.concatenate([d_csh, d_csc], axis=1)
    flat = jnp.concatenate(
        [d_n1g_c + d_n1g_l, d_qg, d_kvg, d_gq, d_kg, d_n2g, dcb_a, dcb_b, d_fg,
         dcw_a.reshape(1, -1), dcw_b.reshape(1, -1), d_ctx_part, d_lat], axis=1)
    n_flat = flat.shape[1]
    n_rows = -(-n_flat // (8 * LANES)) * 8
    flat = jnp.pad(flat, ((0, 0), (0, n_rows * LANES - n_flat))).reshape(n_rows, LANES)
    got = _all_gather_small(flat, "gather_small_grads")
    tot = _sum_slots(got, "sum_small_grads").reshape(1, -1)
    sizes = [D, MLA_Q_LORA, MLA_KV_LORA, GQA_HEAD_DIM, GQA_HEAD_DIM, D, F2, D, 3 * FF, 3 * FF, 2 * D]
    offs = [0]
    for s in sizes:
        offs.append(offs[-1] + s)
    t_n1g, t_qg, t_kvg, t_gq, t_kg, t_n2g, t_cb, t_fg, t_cwa, t_cwb, t_ctx = [tot[:, offs[k] : offs[k + 1]] for k in range(len(sizes))]
    g_cw_full = jnp.concatenate([t_cwa.reshape(3, FF), t_cwb.reshape(3, FF)], axis=1)
    g_cw = lax.dynamic_slice(g_cw_full, (0, j * NW), (3, NW))
    d_lat_all = got.reshape(8, -1)[:, offs[-1] : offs[-1] + 6 * D]
    g16 = jnp.concatenate([d_lat_all, jnp.pad(t_ctx, ((0, 0), (0, 4 * D))), jnp.zeros((7, 6 * D), F32)], axis=0)
    g_b_ada = _sum_slots(g16.reshape(16, 1, 6 * D), "sum_b_ada")
    g16_cols = lax.dynamic_slice(g16, (0, j * NA), (16, NA))
    g_w_ada = _mm(s16, g16_cols, "TN", F32, "ada_dw", act="silu")
    ds_part = _mm(g16_cols, w_ada[0], "NT", F32, "ada_dx")
    got = _all_gather_small(ds_part[8:16], "gather_ada_dx")
    ds_ctx = _sum_slots(jnp.stack([got[2 * s] for s in range(4)]), "sum_ada_dx")[0:1]
    g_c_ctx = _silu_grad_mul(ds_ctx, cc)

    r_win, r_wq, r_wkv, r_wbra, r_wbrb, r_wout, r_wup, r_wdown = _reduce_scatter(
        [g_win_t, g_wq_t, g_wkv_t, g_wbra_t, g_wbrb_t, g_wout, g_wup_t, g_wdown])
    gq_p = r_wq.T
    gq = jnp.concatenate([gq_p[:, : 2 * MLA_NOPE].reshape(MLA_Q_LORA, 2, MLA_NOPE), gq_p[:, 2 * MLA_NOPE :].reshape(MLA_Q_LORA, 2, MLA_ROPE)], axis=2)
    grads = {
        "c_ctx": g_c_ctx.reshape(D), "w_ada": g_w_ada[None], "b_ada": g_b_ada, "norm1_g": t_n1g, "w_in": r_win.T[None],
        "mla_q_norm_g": t_qg, "w_q_up": gq.reshape(1, MLA_Q_LORA, -1), "mla_kv_norm_g": t_kvg, "w_kv_up": r_wkv.T[None],
        "gqa_q_norm_g": t_gq, "gqa_k_norm_g": t_kg, "w_br_a": r_wbra.T[None], "w_br_b": r_wbrb.T[None], "w_out": r_wout[None],
        "norm2_g": t_n2g, "w_up": r_wup.T[None], "conv_w": g_cw[None], "conv_b": t_cb, "w_down": r_wdown[None],
        "final_norm_g": t_fg.reshape(D),
    }
    weights = dict(c_ctx=c_ctx, w_ada=w_ada, b_ada=b_ada, norm1_g=norm1_g, w_in=w_in, mla_q_norm_g=mla_q_norm_g, w_q_up=w_q_up,
                   mla_kv_norm_g=mla_kv_norm_g, w_kv_up=w_kv_up, gqa_q_norm_g=gqa_q_norm_g, gqa_k_norm_g=gqa_k_norm_g, w_br_a=w_br_a,
                   w_br_b=w_br_b, w_out=w_out, norm2_g=norm2_g, w_up=w_up, conv_w=conv_w, conv_b=conv_b, w_down=w_down,
                   final_norm_g=final_norm_g)
    m_in = dict(c_ctx=m_c_ctx, w_ada=m_w_ada, b_ada=m_b_ada, norm1_g=m_norm1_g, w_in=m_w_in, mla_q_norm_g=m_mla_q_norm_g,
                w_q_up=m_w_q_up, mla_kv_norm_g=m_mla_kv_norm_g, w_kv_up=m_w_kv_up, gqa_q_norm_g=m_gqa_q_norm_g,
                gqa_k_norm_g=m_gqa_k_norm_g, w_br_a=m_w_br_a, w_br_b=m_w_br_b, w_out=m_w_out, norm2_g=m_norm2_g, w_up=m_w_up,
                conv_w=m_conv_w, conv_b=m_conv_b, w_down=m_w_down, final_norm_g=m_final_norm_g)
    v_in = dict(c_ctx=v_c_ctx, w_ada=v_w_ada, b_ada=v_b_ada, norm1_g=v_norm1_g, w_in=v_w_in, mla_q_norm_g=v_mla_q_norm_g,
                w_q_up=v_w_q_up, mla_kv_norm_g=v_mla_kv_norm_g, w_kv_up=v_w_kv_up, gqa_q_norm_g=v_gqa_q_norm_g,
                gqa_k_norm_g=v_gqa_k_norm_g, w_br_a=v_w_br_a, w_br_b=v_w_br_b, w_out=v_w_out, norm2_g=v_norm2_g, w_up=v_w_up,
                conv_w=v_conv_w, conv_b=v_conv_b, w_down=v_w_down, final_norm_g=v_final_norm_g)
    names = list(weights)
    big = [n for n in names if weights[n].ndim == 3 and weights[n].shape[1] >= 8]
    small = [n for n in names if n not in big]
    grads = {n: grads[n].reshape(weights[n].shape).astype(F32) for n in names}
    delta, new_m, new_v = {}, {}, {}
    for n in big:
        shp = weights[n].shape
        two_d = lambda a: a.reshape(shp[1], shp[2])
        d_, m_, v_ = _adamw(two_d(weights[n]), two_d(grads[n]), two_d(m_in[n]), two_d(v_in[n]), "adamw_" + n)
        delta[n], new_m[n], new_v[n] = d_.reshape(shp), m_.reshape(shp), v_.reshape(shp)

    def pack(tree):
        flat_ = jnp.concatenate([tree[n].reshape(-1) for n in small])
        rows = -(-flat_.shape[0] // (8 * LANES)) * 8
        return jnp.pad(flat_, (0, rows * LANES - flat_.shape[0])).reshape(rows, LANES)

    d_, m_, v_ = _adamw(pack(weights), pack(grads), pack(m_in) , pack(v_in), "adamw_small")
    off = 0
    for n in small:
        size = weights[n].size
        shp = weights[n].shape
        delta[n] = d_.reshape(-1)[off : off + size].reshape(shp)
        new_m[n] = m_.reshape(-1)[off : off + size].reshape(shp)
        new_v[n] = v_.reshape(-1)[off : off + size].reshape(shp)
        off += size

    return (loss, grad_x[None], *[grads[n] for n in names], *[delta[n] for n in names], *[new_m[n] for n in names],
            *[new_v[n] for n in names])
```

```python
import functools

import jax
import jax.numpy as jnp
from jax import lax
from jax.experimental import pallas as pl
from jax.experimental.pallas import tpu as pltpu

F32 = jnp.float32
BF16 = jnp.bfloat16
MESH = pl.DeviceIdType.MESH

NORM_EPS = 1e-6
ROPE_THETA = 10000.0
GRID_W = 64
MLA_HEADS = 8
MLA_Q_LORA = 768
MLA_KV_LORA = 512
MLA_NOPE = 128
MLA_ROPE = 64
MLA_V = 128
GQA_HEADS = 8
GQA_KV_HEADS = 2
GQA_HEAD_DIM = 128
GQA_GROUP = GQA_HEADS // GQA_KV_HEADS
LANES = 128
KVP = MLA_KV_LORA + 2 * GQA_KV_HEADS * GQA_HEAD_DIM + LANES
QC = MLA_Q_LORA + GQA_HEADS * GQA_HEAD_DIM

ADAM_LR = 0.001
ADAM_B1 = 0.9
ADAM_B2 = 0.999
ADAM_EPS = 1e-08
ADAM_WD = 0.01
ADAM_STEP = 10

VMEM_LIMIT = 48 * 1024 * 1024


def _pick(dim, target, mult=LANES):
    t = (min(target, dim) // mult) * mult
    while t >= mult:
        if dim % t == 0:
            return t
        t -= mult
    return dim


def _params(sem):
    return pltpu.CompilerParams(dimension_semantics=sem, vmem_limit_bytes=VMEM_LIMIT)


_DIMS = {"NN": (((1,), (0,)), ((), ())), "NT": (((1,), (1,)), ((), ())), "TN": (((0,), (0,)), ((), ()))}


def _mm(a, b, mode, out_dtype, name, tm=1024, tn=1024, tk=512, act=None, bias=None):
    if mode == "NN":
        (M, K), (K2, N) = a.shape, b.shape
    elif mode == "NT":
        (M, K), (N, K2) = a.shape, b.shape
    else:
        (K, M), (K2, N) = a.shape, b.shape
    assert K == K2, (name, a.shape, b.shape)
    tm, tn, tk = _pick(M, tm), _pick(N, tn), _pick(K, tk)
    nk = K // tk
    dims = _DIMS[mode]

    def body(*refs):
        if bias is None:
            a_ref, b_ref, o_ref, acc = refs
        else:
            a_ref, b_ref, bias_ref, o_ref, acc = refs
        k = pl.program_id(2)

        @pl.when(k == 0)
        def _():
            acc[...] = jnp.zeros_like(acc)

        av = a_ref[...]
        if act == "silu":
            av = av * jax.nn.sigmoid(av)
        acc[...] += lax.dot_general(av.astype(BF16), b_ref[...].astype(BF16), dims, preferred_element_type=F32)

        @pl.when(k == nk - 1)
        def _():
            r = acc[...]
            if bias is not None:
                r = r + bias_ref[...]
            o_ref[...] = r.astype(out_dtype)

    a_spec = pl.BlockSpec((tk, tm), lambda i, j, k: (k, i)) if mode == "TN" else pl.BlockSpec((tm, tk), lambda i, j, k: (i, k))
    b_spec = pl.BlockSpec((tn, tk), lambda i, j, k: (j, k)) if mode == "NT" else pl.BlockSpec((tk, tn), lambda i, j, k: (k, j))
    in_specs, args = [a_spec, b_spec], [a, b]
    if bias is not None:
        in_specs.append(pl.BlockSpec((1, tn), lambda i, j, k: (0, j)))
        args.append(bias)
    return pl.pallas_call(
        body,
        name=name,
        grid=(M // tm, N // tn, nk),
        in_specs=in_specs,
        out_specs=pl.BlockSpec((tm, tn), lambda i, j, k: (i, j)),
        out_shape=jax.ShapeDtypeStruct((M, N), out_dtype),
        scratch_shapes=[pltpu.VMEM((tm, tn), F32)],
        compiler_params=_params(("parallel", "parallel", "arbitrary")),
    )(*args)


def _rms(x):
    r = lax.rsqrt(jnp.mean(x * x, axis=-1, keepdims=True) + NORM_EPS)
    return x * r, r


def _rms_bwd(xh, r, dxh):
    return r * (dxh - xh * jnp.mean(dxh * xh, axis=-1, keepdims=True))


def _swap(x, q):
    lane = lax.broadcasted_iota(jnp.int32, x.shape, 1)
    even = ((lane // q) % 2) == 0
    return jnp.where(even, pltpu.roll(x, LANES - q, 1), pltpu.roll(x, q, 1))


def _rope(x, cos, ss, q):
    return x * cos + _swap(x, q) * ss


def _rope_t(d, cos, ss, q):
    return d * cos + _swap(d * ss, q)


def _csum(x):
    return jnp.sum(x, axis=0, keepdims=True)


def _rows(tr, w, off=0):
    return pl.BlockSpec((tr, w), lambda i: (i + off, 0))


def _bcast(w):
    return pl.BlockSpec((1, w), lambda i: (0, 0))


def _acc_init(i, refs):
    @pl.when(i == 0)
    def _():
        for r in refs:
            r[...] = jnp.zeros_like(r)


def _rope_tables(n_ctx, n_lat, rot_dim):
    rows = n_lat // GRID_W
    row = jnp.repeat(jnp.arange(rows, dtype=F32), GRID_W)
    col = jnp.tile(jnp.arange(GRID_W, dtype=F32), rows)
    half = rot_dim // 2
    inv_freq = ROPE_THETA ** (-jnp.arange(0, half, 2, dtype=F32) / half)
    ar, ac = row[:, None] * inv_freq, col[:, None] * inv_freq
    cos = jnp.concatenate([jnp.cos(ar), jnp.cos(ar), jnp.cos(ac), jnp.cos(ac)], axis=-1)
    ss = jnp.concatenate([-jnp.sin(ar), jnp.sin(ar), -jnp.sin(ac), jnp.sin(ac)], axis=-1)
    cos = jnp.tile(cos, (1, LANES // rot_dim))
    ss = jnp.tile(ss, (1, LANES // rot_dim))
    cos = jnp.concatenate([jnp.ones((n_ctx, LANES), F32), cos], axis=0)
    ss = jnp.concatenate([jnp.zeros((n_ctx, LANES), F32), ss], axis=0)
    return cos, ss


def _norm_mod_fwd(x2d, g, sh, sc, name, tr):
    n, d = x2d.shape

    def body(x_ref, g_ref, sh_ref, sc_ref, z_ref):
        xh, _ = _rms(x_ref[...])
        z_ref[...] = ((xh * g_ref[...]) * (1.0 + sc_ref[...]) + sh_ref[...]).astype(BF16)

    return pl.pallas_call(
        body,
        name=name,
        grid=(n // tr,),
        in_specs=[_rows(tr, d), _bcast(d), _bcast(d), _bcast(d)],
        out_specs=_rows(tr, d),
        out_shape=jax.ShapeDtypeStruct((n, d), BF16),
        compiler_params=_params(("parallel",)),
    )(x2d, g, sh, sc)


def _norm_mod_bwd(dz, dz_off, x2d, g, sc, dres, name, tr):
    n, d = x2d.shape
    want_dx = dres is not None

    def body(*refs):
        if want_dx:
            dz_ref, x_ref, g_ref, sc_ref, dres_ref, dx_ref, dg_ref, dsh_ref, dsc_ref = refs
        else:
            dz_ref, x_ref, g_ref, sc_ref, dg_ref, dsh_ref, dsc_ref = refs
        _acc_init(pl.program_id(0), [dg_ref, dsh_ref, dsc_ref])
        xh, r = _rms(x_ref[...])
        dzv = dz_ref[...]
        gv = g_ref[...]
        dsc_ref[...] += _csum(dzv * (xh * gv))
        dsh_ref[...] += _csum(dzv)
        dh = dzv * (1.0 + sc_ref[...])
        dg_ref[...] += _csum(dh * xh)
        if want_dx:
            dx_ref[...] = _rms_bwd(xh, r, dh * gv) + dres_ref[...]

    in_specs = [_rows(tr, d, dz_off), _rows(tr, d), _bcast(d), _bcast(d)]
    args = [dz, x2d, g, sc]
    out_specs = [_bcast(d)] * 3
    out_shape = [jax.ShapeDtypeStruct((1, d), F32)] * 3
    if want_dx:
        in_specs.append(_rows(tr, d))
        args.append(dres)
        out_specs = [_rows(tr, d)] + out_specs
        out_shape = [jax.ShapeDtypeStruct((n, d), F32)] + out_shape
    res = pl.pallas_call(
        body,
        name=name,
        grid=(n // tr,),
        in_specs=in_specs,
        out_specs=out_specs,
        out_shape=out_shape,
        compiler_params=_params(("arbitrary",)),
    )(*args)
    return res if want_dx else (None, *res)


_QA, _QB = MLA_ROPE // 4, GQA_HEAD_DIM // 4


def _kprep_fwd(pkv, kvg, kg, cos_a, ss_a, cos_b, ss_b, tr):
    n = pkv.shape[0]
    nb = GQA_KV_HEADS * GQA_HEAD_DIM

    def body(p_ref, kvg_ref, kg_ref, ca, sa, cb, sb, ckv_ref, kb_ref, vb_ref, kpe_ref):
        p = p_ref[...]
        xh, _ = _rms(p[:, :MLA_KV_LORA])
        ckv_ref[...] = (xh * kvg_ref[...]).astype(BF16)
        for e in range(GQA_KV_HEADS):
            lo = MLA_KV_LORA + e * GQA_HEAD_DIM
            kh, _ = _rms(p[:, lo : lo + GQA_HEAD_DIM])
            kb_ref[:, e * GQA_HEAD_DIM : (e + 1) * GQA_HEAD_DIM] = _rope(kh * kg_ref[...], cb[...], sb[...], _QB).astype(BF16)
        vb_ref[...] = p[:, MLA_KV_LORA + nb : MLA_KV_LORA + 2 * nb].astype(BF16)
        kr = _rope(p[:, MLA_KV_LORA + 2 * nb :], ca[...], sa[...], _QA)
        kpe_ref[:, :LANES] = kr.astype(BF16)
        kpe_ref[:, LANES:] = pltpu.roll(kr, MLA_ROPE, 1).astype(BF16)

    return pl.pallas_call(
        body,
        name="kprep_fwd",
        grid=(n // tr,),
        in_specs=[_rows(tr, KVP), _bcast(MLA_KV_LORA), _bcast(GQA_HEAD_DIM)] + [_rows(tr, LANES)] * 4,
        out_specs=[_rows(tr, MLA_KV_LORA), _rows(tr, nb), _rows(tr, nb), _rows(tr, 2 * LANES)],
        out_shape=[jax.ShapeDtypeStruct((n, w), BF16) for w in (MLA_KV_LORA, nb, nb, 2 * LANES)],
        compiler_params=_params(("parallel",)),
    )(pkv, kvg, kg, cos_a, ss_a, cos_b, ss_b)


def _kprep_bwd(pkv, dckv, dkb, dvb, dkpe, kvg, kg, cos_b, ss_b, tr):
    n = pkv.shape[0]
    nb = GQA_KV_HEADS * GQA_HEAD_DIM

    def body(p_ref, dckv_ref, dkb_ref, dvb_ref, dkpe_ref, kvg_ref, kg_ref, cb, sb, dp_ref, dkvg_ref, dkg_ref):
        _acc_init(pl.program_id(0), [dkvg_ref, dkg_ref])
        p = p_ref[...]
        xh, r = _rms(p[:, :MLA_KV_LORA])
        dn = dckv_ref[...]
        dkvg_ref[...] += _csum(dn * xh)
        dp_ref[:, :MLA_KV_LORA] = _rms_bwd(xh, r, dn * kvg_ref[...]).astype(BF16)
        for e in range(GQA_KV_HEADS):
            lo = MLA_KV_LORA + e * GQA_HEAD_DIM
            kh, rk = _rms(p[:, lo : lo + GQA_HEAD_DIM])
            dk = _rope_t(dkb_ref[:, e * GQA_HEAD_DIM : (e + 1) * GQA_HEAD_DIM], cb[...], sb[...], _QB)
            dkg_ref[...] += _csum(dk * kh)
            dp_ref[:, lo : lo + GQA_HEAD_DIM] = _rms_bwd(kh, rk, dk * kg_ref[...]).astype(BF16)
        dp_ref[:, MLA_KV_LORA + nb : MLA_KV_LORA + 2 * nb] = dvb_ref[...].astype(BF16)
        dp_ref[:, MLA_KV_LORA + 2 * nb :] = dkpe_ref[...].astype(BF16)

    return pl.pallas_call(
        body,
        name="kprep_bwd",
        grid=(n // tr,),
        in_specs=[_rows(tr, KVP), _rows(tr, MLA_KV_LORA), _rows(tr, nb), _rows(tr, nb), _rows(tr, LANES),
                  _bcast(MLA_KV_LORA), _bcast(GQA_HEAD_DIM), _rows(tr, LANES), _rows(tr, LANES)],
        out_specs=[_rows(tr, KVP), _bcast(MLA_KV_LORA), _bcast(GQA_HEAD_DIM)],
        out_shape=[jax.ShapeDtypeStruct((n, KVP), BF16), jax.ShapeDtypeStruct((1, MLA_KV_LORA), F32),
                   jax.ShapeDtypeStruct((1, GQA_HEAD_DIM), F32)],
        compiler_params=_params(("arbitrary",)),
    )(pkv, dckv, dkb, dvb, dkpe, kvg, kg, cos_b, ss_b)


def _kgrad_split(dka, dva, cos_a, ss_a, tr):
    n = dka.shape[0]
    wk = MLA_HEADS * 2 * LANES

    def body(dk_ref, dv_ref, ca, sa, dkv_ref, dkpe_ref):
        even = jnp.zeros((tr, LANES), F32)
        odd = jnp.zeros((tr, LANES), F32)
        for h in range(MLA_HEADS):
            dkv_ref[:, 2 * h * LANES : (2 * h + 1) * LANES] = dk_ref[:, 2 * h * LANES : (2 * h + 1) * LANES].astype(BF16)
            dkv_ref[:, (2 * h + 1) * LANES : (2 * h + 2) * LANES] = dv_ref[:, h * MLA_V : (h + 1) * MLA_V].astype(BF16)
            part = dk_ref[:, (2 * h + 1) * LANES : (2 * h + 2) * LANES]
            if h % 2 == 0:
                even = even + part
            else:
                odd = odd + part
        lane = lax.broadcasted_iota(jnp.int32, (tr, LANES), 1)
        low = lane < MLA_ROPE
        both = jnp.where(low, even, odd)
        tot = jnp.where(low, both + pltpu.roll(both, MLA_ROPE, 1), 0.0)
        dkpe_ref[...] = _rope_t(tot, ca[...], sa[...], _QA)

    return pl.pallas_call(
        body,
        name="kgrad_split",
        grid=(n // tr,),
        in_specs=[_rows(tr, wk), _rows(tr, MLA_HEADS * MLA_V), _rows(tr, LANES), _rows(tr, LANES)],
        out_specs=[_rows(tr, wk), _rows(tr, LANES)],
        out_shape=[jax.ShapeDtypeStruct((n, wk), BF16), jax.ShapeDtypeStruct((n, LANES), F32)],
        compiler_params=_params(("parallel",)),
    )(dka, dva, cos_a, ss_a)


def _qprep_fwd(pq, qg, gq, cos_b, ss_b, tr):
    n = pq.shape[0]
    nq = GQA_HEADS * GQA_HEAD_DIM

    def body(p_ref, qg_ref, gq_ref, cb, sb, cq_ref, qb_ref):
        xh, _ = _rms(p_ref[:, :MLA_Q_LORA])
        cq_ref[...] = (xh * qg_ref[...]).astype(BF16)
        for h in range(GQA_HEADS):
            lo = MLA_Q_LORA + h * GQA_HEAD_DIM
            qh, _ = _rms(p_ref[:, lo : lo + GQA_HEAD_DIM])
            qb_ref[:, h * GQA_HEAD_DIM : (h + 1) * GQA_HEAD_DIM] = _rope(qh * gq_ref[...], cb[...], sb[...], _QB).astype(BF16)

    return pl.pallas_call(
        body,
        name="qprep_fwd",
        grid=(n // tr,),
        in_specs=[_rows(tr, QC), _bcast(MLA_Q_LORA), _bcast(GQA_HEAD_DIM), _rows(tr, LANES), _rows(tr, LANES)],
        out_specs=[_rows(tr, MLA_Q_LORA), _rows(tr, nq)],
        out_shape=[jax.ShapeDtypeStruct((n, MLA_Q_LORA), BF16), jax.ShapeDtypeStruct((n, nq), BF16)],
        compiler_params=_params(("parallel",)),
    )(pq, qg, gq, cos_b, ss_b)


def _qprep_bwd(pq, dcq, dqb, qg, gq, cos_b, ss_b, tr):
    n = pq.shape[0]
    nq = GQA_HEADS * GQA_HEAD_DIM

    def body(p_ref, dcq_ref, dqb_ref, qg_ref, gq_ref, cb, sb, dp_ref, dqg_ref, dgq_ref):
        _acc_init(pl.program_id(0), [dqg_ref, dgq_ref])
        xh, r = _rms(p_ref[:, :MLA_Q_LORA])
        dn = dcq_ref[...]
        dqg_ref[...] += _csum(dn * xh)
        dp_ref[:, :MLA_Q_LORA] = _rms_bwd(xh, r, dn * qg_ref[...]).astype(BF16)
        for h in range(GQA_HEADS):
            lo = MLA_Q_LORA + h * GQA_HEAD_DIM
            qh, rq = _rms(p_ref[:, lo : lo + GQA_HEAD_DIM])
            dq = _rope_t(dqb_ref[:, h * GQA_HEAD_DIM : (h + 1) * GQA_HEAD_DIM], cb[...], sb[...], _QB)
            dgq_ref[...] += _csum(dq * qh)
            dp_ref[:, lo : lo + GQA_HEAD_DIM] = _rms_bwd(qh, rq, dq * gq_ref[...]).astype(BF16)

    return pl.pallas_call(
        body,
        name="qprep_bwd",
        grid=(n // tr,),
        in_specs=[_rows(tr, QC), _rows(tr, MLA_Q_LORA), _rows(tr, nq), _bcast(MLA_Q_LORA), _bcast(GQA_HEAD_DIM),
                  _rows(tr, LANES), _rows(tr, LANES)],
        out_specs=[_rows(tr, QC), _bcast(MLA_Q_LORA), _bcast(GQA_HEAD_DIM)],
        out_shape=[jax.ShapeDtypeStruct((n, QC), BF16), jax.ShapeDtypeStruct((1, MLA_Q_LORA), F32),
                   jax.ShapeDtypeStruct((1, GQA_HEAD_DIM), F32)],
        compiler_params=_params(("arbitrary",)),
    )(pq, dcq, dqb, qg, gq, cos_b, ss_b)


_QA_COLS = MLA_HEADS * (MLA_NOPE + MLA_ROPE)


def _qrope_fwd(qa, cos_a, ss_a, tr):
    n = qa.shape[0]

    def body(q_ref, ca, sa, o_ref):
        for j in range(MLA_HEADS // 2):
            lo = 3 * j * LANES
            o_ref[:, lo : lo + 2 * LANES] = q_ref[:, lo : lo + 2 * LANES].astype(BF16)
            o_ref[:, lo + 2 * LANES : lo + 3 * LANES] = _rope(q_ref[:, lo + 2 * LANES : lo + 3 * LANES], ca[...], sa[...], _QA).astype(BF16)

    return pl.pallas_call(
        body,
        name="qrope_fwd",
        grid=(n // tr,),
        in_specs=[_rows(tr, _QA_COLS), _rows(tr, LANES), _rows(tr, LANES)],
        out_specs=_rows(tr, _QA_COLS),
        out_shape=jax.ShapeDtypeStruct((n, _QA_COLS), BF16),
        compiler_params=_params(("parallel",)),
    )(qa, cos_a, ss_a)


def _qrope_bwd(dq2, cos_a, ss_a, tr):
    n = dq2.shape[0]

    def body(d_ref, ca, sa, o_ref):
        for j in range(MLA_HEADS // 2):
            lo = 3 * j * LANES
            h0, h1 = 2 * j, 2 * j + 1
            o_ref[:, lo : lo + LANES] = d_ref[:, 2 * h0 * LANES : (2 * h0 + 1) * LANES].astype(BF16)
            o_ref[:, lo + LANES : lo + 2 * LANES] = d_ref[:, 2 * h1 * LANES : (2 * h1 + 1) * LANES].astype(BF16)
            pe = d_ref[:, (2 * h0 + 1) * LANES : (2 * h0 + 2) * LANES] + d_ref[:, (2 * h1 + 1) * LANES : (2 * h1 + 2) * LANES]
            o_ref[:, lo + 2 * LANES : lo + 3 * LANES] = _rope_t(pe, ca[...], sa[...], _QA).astype(BF16)

    return pl.pallas_call(
        body,
        name="qrope_bwd",
        grid=(n // tr,),
        in_specs=[_rows(tr, MLA_HEADS * 2 * LANES), _rows(tr, LANES), _rows(tr, LANES)],
        out_specs=_rows(tr, _QA_COLS),
        out_shape=jax.ShapeDtypeStruct((n, _QA_COLS), BF16),
        compiler_params=_params(("parallel",)),
    )(dq2, cos_a, ss_a)


def _cat(refs):
    vals = [r[...] for r in refs]
    return vals[0] if len(vals) == 1 else jnp.concatenate(vals, axis=-1)


def _attn_fwd(qparts, kparts, vpart, n_heads, group, dv, scale, name, tq):
    T, Tk = qparts[0][0].shape[0], kparts[0][0].shape[0]
    nq_, nk_ = len(qparts), len(kparts)

    def body(*refs):
        q = _cat(refs[:nq_])
        k = _cat(refs[nq_ : nq_ + nk_])
        v_ref, o_ref, lse_ref = refs[nq_ + nk_ :]
        s = lax.dot_general(q, k, _DIMS["NT"], preferred_element_type=F32) * scale
        m = jnp.max(s, axis=-1, keepdims=True)
        p = jnp.exp(s - m)
        l = jnp.sum(p, axis=-1, keepdims=True)
        pn = (p * (1.0 / l)).astype(BF16)
        o_ref[...] = jnp.dot(pn, v_ref[...], preferred_element_type=F32).astype(BF16)
        lse_ref[...] = m + jnp.log(l)

    in_specs = [pl.BlockSpec((tq, LANES), lambda h, i, f=f: (i, f(h))) for _, f in qparts]
    in_specs += [pl.BlockSpec((Tk, LANES), lambda h, i, f=f: (0, f(h // group))) for _, f in kparts]
    fv = vpart[1]
    in_specs.append(pl.BlockSpec((Tk, dv), lambda h, i: (0, fv(h // group))))
    return pl.pallas_call(
        body,
        name=name,
        grid=(n_heads, T // tq),
        in_specs=in_specs,
        out_specs=[pl.BlockSpec((tq, dv), lambda h, i: (i, h)), pl.BlockSpec((None, tq, 1), lambda h, i: (h, i, 0))],
        out_shape=[jax.ShapeDtypeStruct((T, n_heads * dv), BF16), jax.ShapeDtypeStruct((n_heads, T, 1), F32)],
        compiler_params=_params(("parallel", "parallel")),
    )(*[a for a, _ in qparts], *[a for a, _ in kparts], vpart[0])


def _attn_bwd(qparts, kparts, vpart, o, do, lse, n_heads, group, dv, scale, name, tq):
    T, Tk = qparts[0][0].shape[0], kparts[0][0].shape[0]
    nq_, nk_ = len(qparts), len(kparts)
    dk_ = LANES * nq_
    n_kv = n_heads // group
    nblk = T // tq

    def head(hk, i):
        return hk * group + i // nblk

    def body(*refs):
        q = _cat(refs[:nq_])
        k = _cat(refs[nq_ : nq_ + nk_])
        v_ref, o_ref, do_ref, lse_ref, dq_ref, dk_ref, dv_ref = refs[nq_ + nk_ :]
        _acc_init(pl.program_id(1), [dk_ref, dv_ref])
        s = lax.dot_general(q, k, _DIMS["NT"], preferred_element_type=F32) * scale
        p = jnp.exp(s - lse_ref[...])
        dov = do_ref[...]
        dp = lax.dot_general(dov, v_ref[...], _DIMS["NT"], preferred_element_type=F32)
        delta = jnp.sum(dov.astype(F32) * o_ref[...].astype(F32), axis=-1, keepdims=True)
        ds = (p * (dp - delta) * scale).astype(BF16)
        dq_ref[...] = jnp.dot(ds, k, preferred_element_type=F32)
        dk_ref[...] += lax.dot_general(ds, q, _DIMS["TN"], preferred_element_type=F32)
        dv_ref[...] += lax.dot_general(p.astype(BF16), dov, _DIMS["TN"], preferred_element_type=F32)

    in_specs = [pl.BlockSpec((tq, LANES), lambda hk, i, f=f: (i % nblk, f(head(hk, i)))) for _, f in qparts]
    in_specs += [pl.BlockSpec((Tk, LANES), lambda hk, i, f=f: (0, f(hk))) for _, f in kparts]
    fv = vpart[1]
    in_specs.append(pl.BlockSpec((Tk, dv), lambda hk, i: (0, fv(hk))))
    in_specs += [pl.BlockSpec((tq, dv), lambda hk, i: (i % nblk, head(hk, i)))] * 2
    in_specs.append(pl.BlockSpec((None, tq, 1), lambda hk, i: (head(hk, i), i % nblk, 0)))
    return pl.pallas_call(
        body,
        name=name,
        grid=(n_kv, group * nblk),
        in_specs=in_specs,
        out_specs=[pl.BlockSpec((tq, dk_), lambda hk, i: (i % nblk, head(hk, i))),
                   pl.BlockSpec((Tk, dk_), lambda hk, i: (0, hk)),
                   pl.BlockSpec((Tk, dv), lambda hk, i: (0, hk))],
        out_shape=[jax.ShapeDtypeStruct((T, n_heads * dk_), F32), jax.ShapeDtypeStruct((Tk, n_kv * dk_), F32),
                   jax.ShapeDtypeStruct((Tk, n_kv * dv), F32)],
        compiler_params=_params(("parallel", "arbitrary")),
    )(*[a for a, _ in qparts], *[a for a, _ in kparts], vpart[0], o, do, lse)


def _gates_fwd(pg, ya, yb, tr):
    n, d = ya.shape

    def body(pg_ref, ya_ref, yb_ref, o_ref):
        ga = jax.nn.sigmoid(pg_ref[:, :d])
        gb = jax.nn.sigmoid(pg_ref[:, d:])
        o_ref[...] = (ga * ya_ref[...] + gb * yb_ref[...]).astype(BF16)

    return pl.pallas_call(
        body,
        name="gates_fwd",
        grid=(n // tr,),
        in_specs=[_rows(tr, 2 * d), _rows(tr, d), _rows(tr, d)],
        out_specs=_rows(tr, d),
        out_shape=jax.ShapeDtypeStruct((n, d), BF16),
        compiler_params=_params(("parallel",)),
    )(pg, ya, yb)


def _gates_bwd(dm, pg, ya, yb, tr):
    n, d = ya.shape

    def body(dm_ref, pg_ref, ya_ref, yb_ref, dya_ref, dyb_ref, dpg_ref):
        dmv = dm_ref[...]
        ga = jax.nn.sigmoid(pg_ref[:, :d])
        gb = jax.nn.sigmoid(pg_ref[:, d:])
        dya_ref[...] = (dmv * ga).astype(BF16)
        dyb_ref[...] = (dmv * gb).astype(BF16)
        dpg_ref[:, :d] = (dmv * ya_ref[...] * ga * (1.0 - ga)).astype(BF16)
        dpg_ref[:, d:] = (dmv * yb_ref[...] * gb * (1.0 - gb)).astype(BF16)

    return pl.pallas_call(
        body,
        name="gates_bwd",
        grid=(n // tr,),
        in_specs=[_rows(tr, d), _rows(tr, 2 * d), _rows(tr, d), _rows(tr, d)],
        out_specs=[_rows(tr, d), _rows(tr, d), _rows(tr, 2 * d)],
        out_shape=[jax.ShapeDtypeStruct((n, d), BF16), jax.ShapeDtypeStruct((n, d), BF16), jax.ShapeDtypeStruct((n, 2 * d), BF16)],
        compiler_params=_params(("parallel",)),
    )(dm, pg, ya, yb)


def _resid_norm2_fwd(x2d, att, g1, n2g, sh2, sc2, tr):
    n, d = x2d.shape

    def body(x_ref, a_ref, g1_ref, g_ref, sh_ref, sc_ref, x1_ref, z_ref):
        x1 = x_ref[...] + g1_ref[...] * a_ref[...]
        x1_ref[...] = x1
        xh, _ = _rms(x1)
        z_ref[...] = ((xh * g_ref[...]) * (1.0 + sc_ref[...]) + sh_ref[...]).astype(BF16)

    return pl.pallas_call(
        body,
        name="resid_norm2_fwd",
        grid=(n // tr,),
        in_specs=[_rows(tr, d), _rows(tr, d)] + [_bcast(d)] * 4,
        out_specs=[_rows(tr, d), _rows(tr, d)],
        out_shape=[jax.ShapeDtypeStruct((n, d), F32), jax.ShapeDtypeStruct((n, d), BF16)],
        compiler_params=_params(("parallel",)),
    )(x2d, att, g1, n2g, sh2, sc2)


def _resid_norm2_bwd(dz2, x1, dx2, att, n2g, sc2, g1, tr):
    n, d = x1.shape

    def body(dz_ref, x1_ref, dx2_ref, a_ref, g_ref, sc_ref, g1_ref, dx1_ref, da_ref, dg_ref, dsh_ref, dsc_ref, dg1_ref):
        _acc_init(pl.program_id(0), [dg_ref, dsh_ref, dsc_ref, dg1_ref])
        xh, r = _rms(x1_ref[...])
        dzv = dz_ref[...]
        gv = g_ref[...]
        dsc_ref[...] += _csum(dzv * (xh * gv))
        dsh_ref[...] += _csum(dzv)
        dh = dzv * (1.0 + sc_ref[...])
        dg_ref[...] += _csum(dh * xh)
        dx1 = _rms_bwd(xh, r, dh * gv) + dx2_ref[...]
        dx1_ref[...] = dx1
        dg1_ref[...] += _csum(dx1 * a_ref[...])
        da_ref[...] = (dx1 * g1_ref[...]).astype(BF16)

    return pl.pallas_call(
        body,
        name="resid_norm2_bwd",
        grid=(n // tr,),
        in_specs=[_rows(tr, d)] * 4 + [_bcast(d)] * 3,
        out_specs=[_rows(tr, d), _rows(tr, d)] + [_bcast(d)] * 4,
        out_shape=[jax.ShapeDtypeStruct((n, d), F32), jax.ShapeDtypeStruct((n, d), BF16)] + [jax.ShapeDtypeStruct((1, d), F32)] * 4,
        compiler_params=_params(("arbitrary",)),
    )(dz2, x1, dx2, att, n2g, sc2, g1)


def _shift_prev(u):
    row = lax.broadcasted_iota(jnp.int32, u.shape, 0)
    return jnp.where(row == 0, 0.0, pltpu.roll(u, 1, 0))


def _shift_next(u):
    n = u.shape[0]
    row = lax.broadcasted_iota(jnp.int32, u.shape, 0)
    return jnp.where(row == n - 1, 0.0, pltpu.roll(u, n - 1, 0))


def _conv3(u, w_ref, b_ref):
    return b_ref[...] + w_ref[0:1, :] * _shift_prev(u) + w_ref[1:2, :] * u + w_ref[2:3, :] * _shift_next(u)


def _conv_fwd(u, cw, cb, tc):
    n, two_f = u.shape
    f = two_f // 2
    nb = f // tc

    def body(ua_ref, ub_ref, wa_ref, wb_ref, ba_ref, bb_ref, h_ref):
        a = _conv3(ua_ref[...].astype(F32), wa_ref, ba_ref)
        b = _conv3(ub_ref[...].astype(F32), wb_ref, bb_ref)
        h_ref[...] = (a * jax.nn.sigmoid(a) * b).astype(BF16)

    col = lambda rows, off: pl.BlockSpec((rows, tc), lambda i: (0, i + off))
    return pl.pallas_call(
        body,
        name="conv_fwd",
        grid=(nb,),
        in_specs=[col(n, 0), col(n, nb), col(3, 0), col(3, nb), col(1, 0), col(1, nb)],
        out_specs=col(n, 0),
        out_shape=jax.ShapeDtypeStruct((n, f), BF16),
        compiler_params=_params(("parallel",)),
    )(u, u, cw, cw, cb, cb)


def _conv_bwd(u, dh, cw, cb, tc):
    n, two_f = u.shape
    f = two_f // 2
    nb = f // tc

    def part(uv, duc, w_ref, du_ref, dw_ref, db_ref):
        db_ref[...] = _csum(duc)
        dw_ref[0:1, :] = _csum(duc * _shift_prev(uv))
        dw_ref[1:2, :] = _csum(duc * uv)
        dw_ref[2:3, :] = _csum(duc * _shift_next(uv))
        du_ref[...] = (w_ref[0:1, :] * _shift_next(duc) + w_ref[1:2, :] * duc + w_ref[2:3, :] * _shift_prev(duc)).astype(BF16)

    def body(ua_ref, ub_ref, dh_ref, wa_ref, wb_ref, ba_ref, bb_ref, dua_ref, dub_ref, dwa_ref, dwb_ref, dba_ref, dbb_ref):
        ua = ua_ref[...].astype(F32)
        ub = ub_ref[...].astype(F32)
        a = _conv3(ua, wa_ref, ba_ref)
        b = _conv3(ub, wb_ref, bb_ref)
        dhv = dh_ref[...].astype(F32)
        sg = jax.nn.sigmoid(a)
        da = dhv * b * (sg * (1.0 + a * (1.0 - sg)))
        db = dhv * (a * sg)
        part(ua, da, wa_ref, dua_ref, dwa_ref, dba_ref)
        part(ub, db, wb_ref, dub_ref, dwb_ref, dbb_ref)

    col = lambda rows, off: pl.BlockSpec((rows, tc), lambda i: (0, i + off))
    return pl.pallas_call(
        body,
        name="conv_bwd",
        grid=(nb,),
        in_specs=[col(n, 0), col(n, nb), col(n, 0), col(3, 0), col(3, nb), col(1, 0), col(1, nb)],
        out_specs=[col(n, 0), col(n, 0), col(3, 0), col(3, 0), col(1, 0), col(1, 0)],
        out_shape=[jax.ShapeDtypeStruct((n, f), BF16)] * 2 + [jax.ShapeDtypeStruct((3, f), F32)] * 2 + [jax.ShapeDtypeStruct((1, f), F32)] * 2,
        compiler_params=_params(("parallel",)),
    )(u, u, dh, cw, cw, cb, cb)


def _loss_head(x1, f, g2, fg, tgt, tr):
    n, d = x1.shape

    def body(x1_ref, f_ref, g2_ref, fg_ref, t_ref, sq_ref, dx2_ref, dfg_ref, dg2_ref, df_ref):
        _acc_init(pl.program_id(0), [sq_ref, dfg_ref, dg2_ref])
        fv = f_ref[...]
        xh, r = _rms(x1_ref[...] + g2_ref[...] * fv)
        err = xh * fg_ref[...] - t_ref[...]
        sq_ref[...] += _csum(err * err)
        dy = err * (1.0 / d)
        dfg_ref[...] += _csum(dy * xh)
        dx2 = _rms_bwd(xh, r, dy * fg_ref[...])
        dx2_ref[...] = dx2
        dg2_ref[...] += _csum(dx2 * fv)
        df_ref[...] = (dx2 * g2_ref[...]).astype(BF16)

    return pl.pallas_call(
        body,
        name="loss_head",
        grid=(n // tr,),
        in_specs=[_rows(tr, d), _rows(tr, d), _bcast(d), _bcast(d), _rows(tr, d)],
        out_specs=[_bcast(d), _rows(tr, d), _bcast(d), _bcast(d), _rows(tr, d)],
        out_shape=[jax.ShapeDtypeStruct((1, d), F32), jax.ShapeDtypeStruct((n, d), F32), jax.ShapeDtypeStruct((1, d), F32),
                   jax.ShapeDtypeStruct((1, d), F32), jax.ShapeDtypeStruct((n, d), BF16)],
        compiler_params=_params(("arbitrary",)),
    )(x1, f, g2, fg, tgt)


def _sum_slots(g, name):
    s, r, w = g.shape

    def body(g_ref, o_ref):
        acc = g_ref[0]
        for k in range(1, s):
            acc = acc + g_ref[k]
        o_ref[...] = acc

    return pl.pallas_call(body, name=name, out_shape=jax.ShapeDtypeStruct((r, w), F32))(g)


def _silu_grad_mul(ds, cvec):
    def body(d_ref, c_ref, o_ref):
        cv = c_ref[...]
        sg = jax.nn.sigmoid(cv)
        o_ref[...] = d_ref[...] * (sg * (1.0 + cv * (1.0 - sg)))

    return pl.pallas_call(body, name="silu_grad_mul", out_shape=jax.ShapeDtypeStruct(ds.shape, F32))(ds, cvec)


def _adamw(w, g, m, v, name):
    r, cdim = w.shape
    tr = _pick(r, max(8, (1 << 18) // cdim), 8)
    b1c = 1.0 - ADAM_B1**ADAM_STEP
    b2c = 1.0 - ADAM_B2**ADAM_STEP

    def body(w_ref, g_ref, m_ref, v_ref, d_ref, mo_ref, vo_ref):
        gv = g_ref[...]
        mn = ADAM_B1 * m_ref[...] + (1.0 - ADAM_B1) * gv
        vn = ADAM_B2 * v_ref[...] + (1.0 - ADAM_B2) * (gv * gv)
        mo_ref[...] = mn
        vo_ref[...] = vn
        d_ref[...] = -ADAM_LR * ((mn / b1c) / (jnp.sqrt(vn / b2c) + ADAM_EPS) + ADAM_WD * w_ref[...])

    spec = pl.BlockSpec((tr, cdim), lambda i: (i, 0))
    return pl.pallas_call(
        body,
        name=name,
        grid=(r // tr,),
        in_specs=[spec] * 4,
        out_specs=[spec] * 3,
        out_shape=[jax.ShapeDtypeStruct((r, cdim), F32)] * 3,
        compiler_params=_params(("parallel",)),
    )(w, g, m, v)


def _place():
    return lax.axis_index("x"), lax.axis_index("y"), lax.axis_index("c")


def _remote(src, dst, send_sem, recv_sem, dev):
    return pltpu.make_async_remote_copy(src_ref=src, dst_ref=dst, send_sem=send_sem, recv_sem=recv_sem, device_id=dev, device_id_type=MESH)


ANY = pl.BlockSpec(memory_space=pl.ANY)


def _all_gather_small(v, name):
    r, w = v.shape

    def body(v_ref, o_ref, send, recv, lsem):
        x, y, c = _place()
        me = 4 * x + 2 * y + c
        mine = pltpu.make_async_copy(v_ref, o_ref.at[me], lsem)
        mine.start()
        sent = []
        for k in range(1, 8):
            px, py, pc = x ^ (k >> 2), y ^ ((k >> 1) & 1), c ^ (k & 1)
            cp = _remote(v_ref, o_ref.at[me], send.at[k - 1], recv.at[k - 1], (px, py, pc))
            cp.start()
            sent.append(cp)
        for k in range(1, 8):
            px, py, pc = x ^ (k >> 2), y ^ ((k >> 1) & 1), c ^ (k & 1)
            slot = o_ref.at[4 * px + 2 * py + pc]
            _remote(slot, slot, send.at[k - 1], recv.at[k - 1], (x, y, c)).wait_recv()
        for cp in sent:
            cp.wait_send()
        mine.wait()

    return pl.pallas_call(
        body,
        name=name,
        out_shape=jax.ShapeDtypeStruct((8, r, w), F32),
        in_specs=[pl.BlockSpec(memory_space=pltpu.VMEM)],
        out_specs=pl.BlockSpec(memory_space=pltpu.VMEM),
        scratch_shapes=[pltpu.SemaphoreType.DMA((7,)), pltpu.SemaphoreType.DMA((7,)), pltpu.SemaphoreType.DMA],
        compiler_params=pltpu.CompilerParams(vmem_limit_bytes=VMEM_LIMIT),
    )(v)


HBM = pl.BlockSpec(memory_space=pltpu.HBM)
SEM = pl.BlockSpec(memory_space=pltpu.SEMAPHORE)
EFFECT = pltpu.SideEffectType.DATAFLOW_SIDE_EFFECTING


def _other_chips(x, y):
    return [(1 - x, y), (x, 1 - y), (1 - x, 1 - y)]


def _bulk_start(name, srcs, land_shapes, n_copies, copies, after):
    n, m = len(srcs), len(land_shapes)

    def body(*refs):
        src_refs, land_refs = refs[:n], refs[n : n + m]
        send, recv = refs[n + m + 1], refs[n + m + 2]
        token = refs[-1]
        for k, (s, d, dev) in enumerate(copies(src_refs, land_refs)):
            _remote(s, d, send.at[k], recv.at[k], dev).start()
        token[...] = jnp.zeros_like(token)

    lands = [pltpu.with_memory_space_constraint(lax.empty(s.shape, s.dtype), pltpu.HBM) for s in land_shapes]
    out = pl.pallas_call(
        body,
        name=name,
        out_shape=(pltpu.SemaphoreType.DMA((n_copies,)), pltpu.SemaphoreType.DMA((n_copies,)),
                   *[pltpu.HBM(s.shape, s.dtype) for s in srcs], *[pltpu.HBM(s.shape, s.dtype) for s in land_shapes],
                   jax.ShapeDtypeStruct((8, LANES), F32)),
        in_specs=[HBM] * (n + m) + [ANY],
        out_specs=(SEM, SEM, *[HBM] * (n + m), pl.BlockSpec(memory_space=pltpu.VMEM)),
        input_output_aliases={i: 2 + i for i in range(n + m)},
        compiler_params=pltpu.CompilerParams(has_side_effects=EFFECT),
    )(*[pltpu.with_memory_space_constraint(s, pltpu.HBM) for s in srcs], *lands, after)
    return out[0], out[1], list(out[2 : 2 + n]), list(out[2 + n : 2 + n + m]), out[-1][0:1, 0:1]


def _bulk_wait(name, send, recv, srcs, lands, after, waits):
    n, m = len(srcs), len(lands)

    def body(*refs):
        src_refs, land_refs = refs[:n], refs[n : n + m]
        send_sem, recv_sem = refs[n + m], refs[n + m + 1]
        x, y, c = _place()
        for k, (s, d) in enumerate(waits(src_refs, land_refs)):
            cp = _remote(s, d, send_sem.at[k], recv_sem.at[k], (x, y, c))
            cp.wait_send()
            cp.wait_recv()

    out = pl.pallas_call(
        body,
        name=name,
        out_shape=tuple(pltpu.HBM(s.shape, s.dtype) for s in (*srcs, *lands)),
        in_specs=[HBM] * (n + m) + [SEM, SEM, ANY],
        out_specs=tuple([HBM] * (n + m)),
        input_output_aliases={i: i for i in range(n + m)},
        compiler_params=pltpu.CompilerParams(has_side_effects=EFFECT),
    )(*srcs, *lands, send, recv, after)
    return list(out[:n]), list(out[n:])


def _gather_start(shards, after, name):
    def copies(src, land):
        x, y, c = _place()
        j = 2 * x + y
        return [(src[a].at[c], land[a].at[j, c], (px, py, c)) for a in range(len(shards)) for px, py in _other_chips(x, y)]

    shapes = [jax.ShapeDtypeStruct((4,) + s.shape, s.dtype) for s in shards]
    return _bulk_start(name, shards, shapes, 3 * len(shards), copies, after)


def _gather_wait(started, after, name):
    send, recv, srcs, lands, _ = started

    def waits(src, land):
        x, y, c = _place()
        return [(src[a].at[c], land[a].at[2 * px + py, c]) for a in range(len(srcs)) for px, py in _other_chips(x, y)]

    return _bulk_wait(name, send, recv, srcs, lands, after, waits)


def _forward_halves(lands, name):
    n = len(lands)

    def body(*refs):
        bufs = refs[n : 2 * n]
        send, recv = refs[2 * n :]
        x, y, c = _place()
        started = []
        for a in range(n):
            for k, (px, py) in enumerate(_other_chips(x, y)):
                blk = bufs[a].at[2 * px + py, c]
                cp = _remote(blk, blk, send.at[3 * a + k], recv.at[3 * a + k], (x, y, 1 - c))
                cp.start()
                started.append(cp)
        for a in range(n):
            for k, (px, py) in enumerate(_other_chips(x, y)):
                blk = bufs[a].at[2 * px + py, 1 - c]
                _remote(blk, blk, send.at[3 * a + k], recv.at[3 * a + k], (x, y, c)).wait_recv()
        for cp in started:
            cp.wait_send()

    return pl.pallas_call(
        body,
        name=name,
        out_shape=[jax.ShapeDtypeStruct(b.shape, b.dtype) for b in lands],
        in_specs=[ANY] * n,
        out_specs=[ANY] * n,
        input_output_aliases={i: i for i in range(n)},
        scratch_shapes=[pltpu.SemaphoreType.DMA((3 * n,)), pltpu.SemaphoreType.DMA((3 * n,))],
    )(*lands)


def _gather_finish(started, after, tag):
    shards, lands = _gather_wait(started, after, "gather_wait_" + tag)
    lands = _forward_halves(lands, "gather_forward_" + tag)
    j = 2 * lax.axis_index("x") + lax.axis_index("y")
    full = [lax.dynamic_update_slice(b, s[None], (j, 0, 0, 0)) for b, s in zip(lands, shards)]
    return [f.reshape(4 * f.shape[2] * 2, f.shape[3]) for f in full]


def _swap_halves(grads, name):
    n = len(grads)

    def body(*refs):
        ins, outs = refs[:n], refs[n : 2 * n]
        send, recv = refs[2 * n :]
        x, y, c = _place()
        started = []
        for a in range(n):
            for s in range(4):
                cp = _remote(ins[a].at[s, 1 - c], outs[a].at[s], send.at[4 * a + s], recv.at[4 * a + s], (x, y, 1 - c))
                cp.start()
                started.append(cp)
        for cp in started:
            cp.wait_recv()
        for cp in started:
            cp.wait_send()

    return pl.pallas_call(
        body,
        name=name,
        out_shape=[jax.ShapeDtypeStruct((4,) + g.shape[2:], g.dtype) for g in grads],
        in_specs=[ANY] * n,
        out_specs=[ANY] * n,
        scratch_shapes=[pltpu.SemaphoreType.DMA((4 * n,)), pltpu.SemaphoreType.DMA((4 * n,))],
    )(*grads)


def _add_halves(grads, others, tag):
    outs = []
    for a, (g, o) in enumerate(zip(grads, others)):
        _, _, rh, cdim = g.shape
        tr = _pick(rh, 512, 16)

        def body(g_ref, o_ref, p_ref):
            c = lax.axis_index("c")
            own = jnp.where(c == 0, g_ref[0].astype(F32), g_ref[1].astype(F32))
            p_ref[...] = (own + o_ref[...].astype(F32)).astype(BF16)

        outs.append(
            pl.pallas_call(
                body,
                name=f"add_halves_{tag}{a}",
                grid=(4, rh // tr),
                in_specs=[pl.BlockSpec((None, 2, tr, cdim), lambda s, i: (s, 0, i, 0)), pl.BlockSpec((None, tr, cdim), lambda s, i: (s, i, 0))],
                out_specs=pl.BlockSpec((None, tr, cdim), lambda s, i: (s, i, 0)),
                out_shape=jax.ShapeDtypeStruct((4, rh, cdim), BF16),
                compiler_params=_params(("parallel", "parallel")),
            )(g, o)
        )
    return outs


def _exchange_start(parts, after, name):
    def copies(src, land):
        x, y, c = _place()
        j = 2 * x + y
        return [(src[a].at[2 * px + py], land[a].at[j], (px, py, c)) for a in range(len(parts)) for px, py in _other_chips(x, y)]

    return _bulk_start(name, parts, [jax.ShapeDtypeStruct(p.shape, p.dtype) for p in parts], 3 * len(parts), copies, after)


def _exchange_finish(started, after, name):
    send, recv, srcs, lands, _ = started

    def waits(src, land):
        x, y, _ = _place()
        return [(src[a].at[2 * px + py], land[a].at[2 * px + py]) for a in range(len(srcs)) for px, py in _other_chips(x, y)]

    srcs, lands = _bulk_wait(name, send, recv, srcs, lands, after, waits)
    j = 2 * lax.axis_index("x") + lax.axis_index("y")
    return [lax.dynamic_update_slice(b, lax.dynamic_slice(p, (j, 0, 0), (1,) + p.shape[1:]), (j, 0, 0)) for b, p in zip(lands, srcs)]


def _sum_chips(recvd, tag):
    outs = []
    for a, g in enumerate(recvd):
        _, rh, cdim = g.shape
        tr = _pick(rh, 512, 16)

        def body(g_ref, o_ref):
            o_ref[...] = ((g_ref[0].astype(F32) + g_ref[1].astype(F32)) + g_ref[2].astype(F32)) + g_ref[3].astype(F32)

        outs.append(
            pl.pallas_call(
                body,
                name=f"sum_chips_{tag}{a}",
                grid=(rh // tr,),
                in_specs=[pl.BlockSpec((4, tr, cdim), lambda i: (0, i, 0))],
                out_specs=pl.BlockSpec((tr, cdim), lambda i: (i, 0)),
                out_shape=jax.ShapeDtypeStruct((rh, cdim), F32),
                compiler_params=_params(("parallel",)),
            )(g)
        )
    return outs


def _join_halves(halves, name):
    n = len(halves)

    def body(*refs):
        ins, outs = refs[:n], refs[n : 2 * n]
        send, recv = refs[2 * n :]
        x, y, c = _place()
        started = []
        for a in range(n):
            cp = _remote(ins[a], outs[a], send.at[a], recv.at[a], (x, y, 1 - c))
            cp.start()
            started.append(cp)
        for cp in started:
            cp.wait_recv()
        for cp in started:
            cp.wait_send()

    others = pl.pallas_call(
        body,
        name=name,
        out_shape=[jax.ShapeDtypeStruct(h.shape, h.dtype) for h in halves],
        in_specs=[ANY] * n,
        out_specs=[ANY] * n,
        scratch_shapes=[pltpu.SemaphoreType.DMA((n,)), pltpu.SemaphoreType.DMA((n,))],
    )(*halves)
    first = lax.axis_index("c") == 0
    return [jnp.concatenate([jnp.where(first, h, o), jnp.where(first, o, h)], axis=0) for h, o in zip(halves, others)]


def _scatter_start(grads, tag):
    views = [g.reshape(4, 2, g.shape[0] // 8, g.shape[1]) for g in grads]
    mine = _add_halves(views, _swap_halves(views, "swap_halves_" + tag), tag)
    return _exchange_start(mine, mine[-1], "exchange_start_" + tag)


def _scatter_finish(started, after, tag):
    halves = _sum_chips(_exchange_finish(started, after, "exchange_wait_" + tag), tag)
    return _join_halves(halves, "join_halves_" + tag)


def _t_bf16(w):
    return w.T.astype(BF16)


def kernel(x, c, ctx, c_ctx, w_ada, b_ada, norm1_g, w_in, mla_q_norm_g, w_q_up, mla_kv_norm_g, w_kv_up, gqa_q_norm_g, gqa_k_norm_g, w_br_a, w_br_b, w_out, norm2_g, w_up, conv_w, conv_b, w_down, final_norm_g, loss_target, m_c_ctx, m_w_ada, m_b_ada, m_norm1_g, m_w_in, m_mla_q_norm_g, m_w_q_up, m_mla_kv_norm_g, m_w_kv_up, m_gqa_q_norm_g, m_gqa_k_norm_g, m_w_br_a, m_w_br_b, m_w_out, m_norm2_g, m_w_up, m_conv_w, m_conv_b, m_w_down, m_final_norm_g, v_c_ctx, v_w_ada, v_b_ada, v_norm1_g, v_w_in, v_mla_q_norm_g, v_w_q_up, v_mla_kv_norm_g, v_w_kv_up, v_gqa_q_norm_g, v_gqa_k_norm_g, v_w_br_a, v_w_br_b, v_w_out, v_norm2_g, v_w_up, v_conv_w, v_conv_b, v_w_down, v_final_norm_g):
    T, D = x.shape[1], x.shape[2]
    C = ctx.shape[1]
    NA = w_ada.shape[2]
    NW = w_up.shape[2]
    F2 = 4 * NW
    FF = F2 // 2
    xi, yi, ci = _place()
    j = 2 * xi + yi
    me = 4 * xi + 2 * yi + ci
    tr = _pick(C, 128, 8)
    tq = _pick(T, 256)

    x2d, tgt, ctx2d = x[0], loss_target[0], ctx[0]
    fg = final_norm_g.reshape(1, D)
    cc = c_ctx.reshape(1, D)

    w0 = max(D, NW)
    pay = jnp.zeros((8, w0), F32).at[0:1, :D].set(c).at[1:4, :NW].set(conv_w[0])
    got = _all_gather_small(pay, "gather_cond")
    c_all = got[:, 0, :D]
    cw = jnp.concatenate([got[2 * s, 1:4, :NW] for s in range(4)], axis=1)
    s16 = jnp.concatenate([c_all, cc, jnp.zeros((7, D), F32)], axis=0)
    b_cols = lax.dynamic_slice(b_ada, (0, j * NA), (1, NA))
    ada_part = _mm(s16, w_ada[0], "NN", F32, "ada_fwd", act="silu", bias=b_cols)
    got = _all_gather_small(ada_part, "gather_ada")
    ada = jnp.concatenate([got[2 * s] for s in range(4)], axis=1)
    lat = lax.dynamic_slice(ada, (me, 0), (1, 6 * D))
    sh1, sc1, g1, sh2, sc2, g2 = [lat[:, k * D : (k + 1) * D] for k in range(6)]
    csh, csc = ada[8:9, :D], ada[8:9, D : 2 * D]

    wq3 = w_q_up[0].reshape(MLA_Q_LORA, 2, MLA_NOPE + MLA_ROPE)
    wq_perm = jnp.concatenate([wq3[:, :, :MLA_NOPE].reshape(MLA_Q_LORA, -1), wq3[:, :, MLA_NOPE:].reshape(MLA_Q_LORA, -1)], axis=1)
    shards = [_t_bf16(w_in[0]), _t_bf16(wq_perm), _t_bf16(w_kv_up[0]), _t_bf16(w_br_a[0]), _t_bf16(w_br_b[0]),
              w_out[0].astype(BF16), _t_bf16(w_up[0]), w_down[0].astype(BF16)]
    halves = [s.reshape(2, s.shape[0] // 2, s.shape[1]) for s in shards]
    ag_in = _gather_start(halves[0:1], got, "gather_start_in")
    ag_mix = _gather_start(halves[1:6], ag_in[4], "gather_start_mix")
    ag_ffn = _gather_start(halves[6:8], ag_mix[4], "gather_start_ffn")
    sh1 = sh1 + ag_ffn[4]

    cos_a, ss_a = _rope_tables(C, T, MLA_ROPE)
    cos_b, ss_b = _rope_tables(C, T, GQA_HEAD_DIM)
    lcos_a, lss_a, lcos_b, lss_b = cos_a[C:], ss_a[C:], cos_b[C:], ss_b[C:]

    z_ctx = _norm_mod_fwd(ctx2d, norm1_g, csh, csc, "norm1_ctx_fwd", tr)
    z_lat = _norm_mod_fwd(x2d, norm1_g, sh1, sc1, "norm1_lat_fwd", tr)
    z_all = jnp.concatenate([z_ctx, z_lat], axis=0)
    (win_t,) = _gather_finish(ag_in, z_all, "in")
    kv_cols = KVP - LANES + MLA_ROPE
    e_kpe = MLA_KV_LORA + MLA_ROPE
    w_kvp = jnp.concatenate([win_t[:MLA_KV_LORA], win_t[e_kpe:kv_cols], win_t[MLA_KV_LORA:e_kpe], jnp.zeros((LANES - MLA_ROPE, D), BF16)], axis=0)
    w_q = win_t[kv_cols : kv_cols + QC]
    w_g = win_t[kv_cols + QC :]
    w_inp = jnp.concatenate([w_kvp, w_q, w_g], axis=0)

    pkv = _mm(z_all, w_kvp, "NT", F32, "proj_kv")
    pq = _mm(z_lat, w_q, "NT", F32, "proj_q")
    pg = _mm(z_lat, w_g, "NT", F32, "proj_g")
    wq_t, wkv_t, wbra_t, wbrb_t, wout = _gather_finish(ag_mix, pg, "mix")
    ckv_n, kb2, vb2, kpe2 = _kprep_fwd(pkv, mla_kv_norm_g, gqa_k_norm_g, cos_a, ss_a, cos_b, ss_b, tr)
    kv_up = _mm(ckv_n, wkv_t, "NT", BF16, "kv_up")
    cq_n, qb2 = _qprep_fwd(pq, mla_q_norm_g, gqa_q_norm_g, lcos_b, lss_b, tr)
    q_a = _mm(cq_n, wq_t, "NT", F32, "q_up")
    qar = _qrope_fwd(q_a, lcos_a, lss_a, tr)

    a_q = [(qar, lambda h: 3 * (h // 2) + h % 2), (qar, lambda h: 3 * (h // 2) + 2)]
    a_k = [(kv_up, lambda h: 2 * h), (kpe2, lambda h: h % 2)]
    a_v = (kv_up, lambda h: 2 * h + 1)
    a_scale = float(MLA_NOPE + MLA_ROPE) ** -0.5
    b_q = [(qb2, lambda h: h)]
    b_k = [(kb2, lambda h: h)]
    b_v = (vb2, lambda h: h)
    b_scale = float(GQA_HEAD_DIM) ** -0.5
    o_a, lse_a = _attn_fwd(a_q, a_k, a_v, MLA_HEADS, 1, MLA_V, a_scale, "attn_a_fwd", tq)
    o_b, lse_b = _attn_fwd(b_q, b_k, b_v, GQA_HEADS, GQA_GROUP, GQA_HEAD_DIM, b_scale, "attn_b_fwd", tq)
    ya = _mm(o_a, wbra_t, "NT", F32, "br_a")
    yb = _mm(o_b, wbrb_t, "NT", F32, "br_b")
    merged = _gates_fwd(pg, ya, yb, tr)
    att = _mm(merged, wout, "NN", F32, "out_proj")
    x1, z2 = _resid_norm2_fwd(x2d, att, g1, norm2_g, sh2, sc2, tr)
    wup_t, wdown = _gather_finish(ag_ffn, o_b, "ffn")
    u = _mm(z2, wup_t, "NT", BF16, "ffn_up")
    tc = _pick(FF, 128)
    hg = _conv_fwd(u, cw, conv_b, tc)
    f = _mm(hg, wdown, "NN", F32, "ffn_down")
    sq, dx2, d_fg, d_g2, df = _loss_head(x1, f, g2, fg, tgt, tr)
    loss = lax.psum(0.5 * jnp.sum(sq) / D, ("x", "y", "c"))

    dhg = _mm(df, wdown, "NT", BF16, "ffn_down_dx")
    g_wdown = _mm(hg, df, "TN", BF16, "ffn_down_dw")
    du_a, du_b, dcw_a, dcw_b, dcb_a, dcb_b = _conv_bwd(u, dhg, cw, conv_b, tc)
    du = jnp.concatenate([du_a, du_b], axis=1)
    dz2 = _mm(du, wup_t, "NN", F32, "ffn_up_dx")
    g_wup_t = _mm(du, z2, "TN", BF16, "ffn_up_dw")
    rs_ffn = _scatter_start([g_wdown, g_wup_t], "ffn")
    sc2 = sc2 + rs_ffn[4]
    dx1, datt, d_n2g, d_sh2, d_sc2, d_g1 = _resid_norm2_bwd(dz2, x1, dx2, att, norm2_g, sc2, g1, tr)

    dmerged = _mm(datt, wout, "NT", F32, "out_proj_dx")
    g_wout = _mm(merged, datt, "TN", BF16, "out_proj_dw")
    dya, dyb, dpg = _gates_bwd(dmerged, pg, ya, yb, tr)
    do_a = _mm(dya, wbra_t, "NN", BF16, "br_a_dx")
    g_wbra_t = _mm(dya, o_a, "TN", BF16, "br_a_dw")
    do_b = _mm(dyb, wbrb_t, "NN", BF16, "br_b_dx")
    g_wbrb_t = _mm(dyb, o_b, "TN", BF16, "br_b_dw")
    dqa2, dka2, dva2 = _attn_bwd(a_q, a_k, a_v, o_a, do_a, lse_a, MLA_HEADS, 1, MLA_V, a_scale, "attn_a_bwd", tq)
    dqb2, dkb2, dvb2 = _attn_bwd(b_q, b_k, b_v, o_b, do_b, lse_b, GQA_HEADS, GQA_GROUP, GQA_HEAD_DIM, b_scale, "attn_b_bwd", tq)
    dq_a = _qrope_bwd(dqa2, lcos_a, lss_a, tr)
    dcq_n = _mm(dq_a, wq_t, "NN", F32, "q_up_dx")
    g_wq_t = _mm(dq_a, cq_n, "TN", BF16, "q_up_dw")
    dpq, d_qg, d_gq = _qprep_bwd(pq, dcq_n, dqb2, mla_q_norm_g, gqa_q_norm_g, lcos_b, lss_b, tr)
    dkv_up, dkpe = _kgrad_split(dka2, dva2, cos_a, ss_a, tr)
    dckv_n = _mm(dkv_up, wkv_t, "NN", F32, "kv_up_dx")
    g_wkv_t = _mm(dkv_up, ckv_n, "TN", BF16, "kv_up_dw")
    rs_mix = _scatter_start([g_wq_t, g_wkv_t, g_wbra_t, g_wbrb_t, g_wout], "mix")
    dpkv, d_kvg, d_kg = _kprep_bwd(pkv, dckv_n, dkb2, dvb2, dkpe, mla_kv_norm_g + rs_mix[4], gqa_k_norm_g, cos_b, ss_b, tr)
    dproj = jnp.concatenate([dpkv, jnp.concatenate([jnp.zeros((C, QC + 2 * D), BF16), jnp.concatenate([dpq, dpg], axis=1)], axis=0)], axis=1)
    dz_all = _mm(dproj, w_inp, "NN", F32, "proj_dx")
    g_winp = _mm(dproj, z_all, "TN", BF16, "proj_dw")
    nk = MLA_KV_LORA + 2 * GQA_KV_HEADS * GQA_HEAD_DIM
    g_win_t = jnp.concatenate([g_winp[:MLA_KV_LORA], g_winp[nk : nk + MLA_ROPE], g_winp[MLA_KV_LORA:nk], g_winp[KVP:]], axis=0)
    rs_in = _scatter_start([g_win_t], "in")
    csc, sc1 = csc + rs_in[4], sc1 + rs_in[4]
    _, d_n1g_c, d_csh, d_csc = _norm_mod_bwd(dz_all, 0, ctx2d, norm1_g, csc, None, "norm1_ctx_bwd", tr)
    grad_x, d_n1g_l, d_sh1, d_sc1 = _norm_mod_bwd(dz_all, C // tr, x2d, norm1_g, sc1, dx1, "norm1_lat_bwd", tr)

    zeros_d = jnp.zeros((1, D), F32)
    d_lat = jnp.concatenate([d_sh1, d_sc1, d_g1, d_sh2, d_sc2, d_g2], axis=1)
    d_ctx_part = jnp.concatenate([d_csh, d_csc], axis=1)
    flat = jnp.concatenate(
        [d_n1g_c + d_n1g_l, d_qg, d_kvg, d_gq, d_kg, d_n2g, dcb_a, dcb_b, d_fg,
         dcw_a.reshape(1, -1), dcw_b.reshape(1, -1), d_ctx_part, d_lat], axis=1)
    n_flat = flat.shape[1]
    n_rows = -(-n_flat // (8 * LANES)) * 8
    flat = jnp.pad(flat, ((0, 0), (0, n_rows * LANES - n_flat))).reshape(n_rows, LANES)
    got = _all_gather_small(flat, "gather_small_grads")
    tot = _sum_slots(got, "sum_small_grads").reshape(1, -1)
    sizes = [D, MLA_Q_LORA, MLA_KV_LORA, GQA_HEAD_DIM, GQA_HEAD_DIM, D, F2, D, 3 * FF, 3 * FF, 2 * D]
    offs = [0]
    for s in sizes:
        offs.append(offs[-1] + s)
    t_n1g, t_qg, t_kvg, t_gq, t_kg, t_n2g, t_cb, t_fg, t_cwa, t_cwb, t_ctx = [tot[:, offs[k] : offs[k + 1]] for k in range(len(sizes))]
    g_cw_full = jnp.concatenate([t_cwa.reshape(3, FF), t_cwb.reshape(3, FF)], axis=1)
    g_cw = lax.dynamic_slice(g_cw_full, (0, j * NW), (3, NW))
    d_lat_all = got.reshape(8, -1)[:, offs[-1] : offs[-1] + 6 * D]
    g16 = jnp.concatenate([d_lat_all, jnp.pad(t_ctx, ((0, 0), (0, 4 * D))), jnp.zeros((7, 6 * D), F32)], axis=0)
    g_b_ada = _sum_slots(g16.reshape(16, 1, 6 * D), "sum_b_ada")
    g16_cols = lax.dynamic_slice(g16, (0, j * NA), (16, NA))
    g_w_ada = _mm(s16, g16_cols, "TN", F32, "ada_dw", act="silu")
    ds_part = _mm(g16_cols, w_ada[0], "NT", F32, "ada_dx")
    got = _all_gather_small(ds_part[8:16], "gather_ada_dx")
    ds_ctx = _sum_slots(jnp.stack([got[2 * s] for s in range(4)]), "sum_ada_dx")[0:1]
    g_c_ctx = _silu_grad_mul(ds_ctx, cc)

    r_wdown, r_wup = _scatter_finish(rs_ffn, grad_x, "ffn")
    r_wq, r_wkv, r_wbra, r_wbrb, r_wout = _scatter_finish(rs_mix, r_wup, "mix")
    gq_p = r_wq.T
    gq = jnp.concatenate([gq_p[:, : 2 * MLA_NOPE].reshape(MLA_Q_LORA, 2, MLA_NOPE), gq_p[:, 2 * MLA_NOPE :].reshape(MLA_Q_LORA, 2, MLA_ROPE)], axis=2)
    grads = {
        "c_ctx": g_c_ctx.reshape(D), "w_ada": g_w_ada[None], "b_ada": g_b_ada, "norm1_g": t_n1g,
        "mla_q_norm_g": t_qg, "w_q_up": gq.reshape(1, MLA_Q_LORA, -1), "mla_kv_norm_g": t_kvg, "w_kv_up": r_wkv.T[None],
        "gqa_q_norm_g": t_gq, "gqa_k_norm_g": t_kg, "w_br_a": r_wbra.T[None], "w_br_b": r_wbrb.T[None], "w_out": r_wout[None],
        "norm2_g": t_n2g, "w_up": r_wup.T[None], "conv_w": g_cw[None], "conv_b": t_cb, "w_down": r_wdown[None],
        "final_norm_g": t_fg.reshape(D),
    }
    weights = dict(c_ctx=c_ctx, w_ada=w_ada, b_ada=b_ada, norm1_g=norm1_g, w_in=w_in, mla_q_norm_g=mla_q_norm_g, w_q_up=w_q_up,
                   mla_kv_norm_g=mla_kv_norm_g, w_kv_up=w_kv_up, gqa_q_norm_g=gqa_q_norm_g, gqa_k_norm_g=gqa_k_norm_g, w_br_a=w_br_a,
                   w_br_b=w_br_b, w_out=w_out, norm2_g=norm2_g, w_up=w_up, conv_w=conv_w, conv_b=conv_b, w_down=w_down,
                   final_norm_g=final_norm_g)
    m_in = dict(c_ctx=m_c_ctx, w_ada=m_w_ada, b_ada=m_b_ada, norm1_g=m_norm1_g, w_in=m_w_in, mla_q_norm_g=m_mla_q_norm_g,
                w_q_up=m_w_q_up, mla_kv_norm_g=m_mla_kv_norm_g, w_kv_up=m_w_kv_up, gqa_q_norm_g=m_gqa_q_norm_g,
                gqa_k_norm_g=m_gqa_k_norm_g, w_br_a=m_w_br_a, w_br_b=m_w_br_b, w_out=m_w_out, norm2_g=m_norm2_g, w_up=m_w_up,
                conv_w=m_conv_w, conv_b=m_conv_b, w_down=m_w_down, final_norm_g=m_final_norm_g)
    v_in = dict(c_ctx=v_c_ctx, w_ada=v_w_ada, b_ada=v_b_ada, norm1_g=v_norm1_g, w_in=v_w_in, mla_q_norm_g=v_mla_q_norm_g,
                w_q_up=v_w_q_up, mla_kv_norm_g=v_mla_kv_norm_g, w_kv_up=v_w_kv_up, gqa_q_norm_g=v_gqa_q_norm_g,
                gqa_k_norm_g=v_gqa_k_norm_g, w_br_a=v_w_br_a, w_br_b=v_w_br_b, w_out=v_w_out, norm2_g=v_norm2_g, w_up=v_w_up,
                conv_w=v_conv_w, conv_b=v_conv_b, w_down=v_w_down, final_norm_g=v_final_norm_g)
    names = list(weights)
    big = [n for n in names if weights[n].ndim == 3 and weights[n].shape[1] >= 8]
    small = [n for n in names if n not in big]
    delta, new_m, new_v = {}, {}, {}

    def update(n):
        shp = weights[n].shape
        grads[n] = grads[n].reshape(shp).astype(F32)
        two_d = lambda a: a.reshape(shp[1], shp[2])
        d_, m_, v_ = _adamw(two_d(weights[n]), two_d(grads[n]), two_d(m_in[n]), two_d(v_in[n]), "adamw_" + n)
        delta[n], new_m[n], new_v[n] = d_.reshape(shp), m_.reshape(shp), v_.reshape(shp)

    early = [n for n in big if n != "w_in"]
    for n in early:
        update(n)
    done = sum(delta[n][0, 0:1, 0:1] for n in early)
    (r_win,) = _scatter_finish(rs_in, done, "in")
    grads["w_in"] = r_win.T[None]
    update("w_in")
    grads = {n: grads[n].reshape(weights[n].shape).astype(F32) for n in names}

    def pack(tree):
        flat_ = jnp.concatenate([tree[n].reshape(-1) for n in small])
        rows = -(-flat_.shape[0] // (8 * LANES)) * 8
        return jnp.pad(flat_, (0, rows * LANES - flat_.shape[0])).reshape(rows, LANES)

    d_, m_, v_ = _adamw(pack(weights), pack(grads), pack(m_in) , pack(v_in), "adamw_small")
    off = 0
    for n in small:
        size = weights[n].size
        shp = weights[n].shape
        delta[n] = d_.reshape(-1)[off : off + size].reshape(shp)
        new_m[n] = m_.reshape(-1)[off : off + size].reshape(shp)
        new_v[n] = v_.reshape(-1)[off : off + size].reshape(shp)
        off += size

    return (loss, grad_x[None], *[grads[n] for n in names], *[delta[n] for n in names], *[new_m[n] for n in names],
            *[new_v[n] for n in names])
```

```python
import functools

import jax
import jax.numpy as jnp
from jax import lax
from jax.experimental import pallas as pl
from jax.experimental.pallas import tpu as pltpu

F32 = jnp.float32
BF16 = jnp.bfloat16
MESH = pl.DeviceIdType.MESH

NORM_EPS = 1e-6
ROPE_THETA = 10000.0
GRID_W = 64
MLA_HEADS = 8
MLA_Q_LORA = 768
MLA_KV_LORA = 512
MLA_NOPE = 128
MLA_ROPE = 64
MLA_V = 128
GQA_HEADS = 8
GQA_KV_HEADS = 2
GQA_HEAD_DIM = 128
GQA_GROUP = GQA_HEADS // GQA_KV_HEADS
LANES = 128
KVP = MLA_KV_LORA + 2 * GQA_KV_HEADS * GQA_HEAD_DIM + LANES
QC = MLA_Q_LORA + GQA_HEADS * GQA_HEAD_DIM

ADAM_LR = 0.001
ADAM_B1 = 0.9
ADAM_B2 = 0.999
ADAM_EPS = 1e-08
ADAM_WD = 0.01
ADAM_STEP = 10

VMEM_LIMIT = 56 * 1024 * 1024


def _pick(dim, target, mult=LANES):
    t = (min(target, dim) // mult) * mult
    while t >= mult:
        if dim % t == 0:
            return t
        t -= mult
    return dim


def _params(sem):
    return pltpu.CompilerParams(dimension_semantics=sem, vmem_limit_bytes=VMEM_LIMIT)


_DIMS = {"NN": (((1,), (0,)), ((), ())), "NT": (((1,), (1,)), ((), ())), "TN": (((0,), (0,)), ((), ()))}


MM_VMEM_BUDGET = 36 * 1024 * 1024


def _mm_tiles(M, N, K, sa, sb, so, tm, tn, tk):
    tm, tn, tk = _pick(M, tm), _pick(N, tn), _pick(K, tk)

    def need(t):
        return 2 * (tm * t * sa + t * tn * sb) + 2 * tm * tn * so + (tm * tn * 4 if t < K else 0)

    while need(tk) > MM_VMEM_BUDGET and tk > LANES:
        smaller = _pick(K, tk - LANES)
        if smaller >= tk:
            break
        tk = smaller
    return tm, tn, tk


def _mm(a, b, mode, out_dtype, name, tm=1024, tn=1024, tk=2304, act=None, bias=None):
    if mode == "NN":
        (M, K), (K2, N) = a.shape, b.shape
    elif mode == "NT":
        (M, K), (N, K2) = a.shape, b.shape
    else:
        (K, M), (K2, N) = a.shape, b.shape
    assert K == K2, (name, a.shape, b.shape)
    tm, tn, tk = _mm_tiles(M, N, K, a.dtype.itemsize, b.dtype.itemsize, jnp.dtype(out_dtype).itemsize, tm, tn, tk)
    nk = K // tk
    dims = _DIMS[mode]

    def body(*refs):
        a_ref, b_ref = refs[:2]
        bias_ref = refs[2] if bias is not None else None
        o_ref = refs[2 + (bias is not None)]
        av = a_ref[...]
        if act == "silu":
            av = av * jax.nn.sigmoid(av)
        part = lax.dot_general(av.astype(BF16), b_ref[...].astype(BF16), dims, preferred_element_type=F32)

        def finish(r):
            if bias is not None:
                r = r + bias_ref[...]
            o_ref[...] = r.astype(out_dtype)

        if nk == 1:
            finish(part)
            return
        acc = refs[-1]
        k = pl.program_id(2)

        @pl.when(k == 0)
        def _():
            acc[...] = part

        @pl.when(jnp.logical_and(k > 0, k < nk - 1))
        def _():
            acc[...] += part

        @pl.when(k == nk - 1)
        def _():
            finish(acc[...] + part)

    a_spec = pl.BlockSpec((tk, tm), lambda i, j, k: (k, i)) if mode == "TN" else pl.BlockSpec((tm, tk), lambda i, j, k: (i, k))
    b_spec = pl.BlockSpec((tn, tk), lambda i, j, k: (j, k)) if mode == "NT" else pl.BlockSpec((tk, tn), lambda i, j, k: (k, j))
    in_specs, args = [a_spec, b_spec], [a, b]
    if bias is not None:
        in_specs.append(pl.BlockSpec((1, tn), lambda i, j, k: (0, j)))
        args.append(bias)
    return pl.pallas_call(
        body,
        name=name,
        grid=(M // tm, N // tn, nk),
        in_specs=in_specs,
        out_specs=pl.BlockSpec((tm, tn), lambda i, j, k: (i, j)),
        out_shape=jax.ShapeDtypeStruct((M, N), out_dtype),
        scratch_shapes=[pltpu.VMEM((tm, tn), F32)] if nk > 1 else [],
        compiler_params=_params(("parallel", "parallel", "arbitrary")),
    )(*args)


def _rms(x):
    r = lax.rsqrt(jnp.mean(x * x, axis=-1, keepdims=True) + NORM_EPS)
    return x * r, r


def _rms_bwd(xh, r, dxh):
    return r * (dxh - xh * jnp.mean(dxh * xh, axis=-1, keepdims=True))


def _swap(x, q):
    lane = lax.broadcasted_iota(jnp.int32, x.shape, 1)
    even = ((lane // q) % 2) == 0
    return jnp.where(even, pltpu.roll(x, LANES - q, 1), pltpu.roll(x, q, 1))


def _rope(x, cos, ss, q):
    return x * cos + _swap(x, q) * ss


def _rope_t(d, cos, ss, q):
    return d * cos + _swap(d * ss, q)


def _csum(x):
    return jnp.sum(x, axis=0, keepdims=True)


def _rows(tr, w, off=0):
    return pl.BlockSpec((tr, w), lambda i: (i + off, 0))


def _bcast(w):
    return pl.BlockSpec((1, w), lambda i: (0, 0))


def _acc_init(i, refs):
    @pl.when(i == 0)
    def _():
        for r in refs:
            r[...] = jnp.zeros_like(r)


def _rope_tables(n_ctx, n_lat, rot_dim):
    rows = n_lat // GRID_W
    row = jnp.repeat(jnp.arange(rows, dtype=F32), GRID_W)
    col = jnp.tile(jnp.arange(GRID_W, dtype=F32), rows)
    half = rot_dim // 2
    inv_freq = ROPE_THETA ** (-jnp.arange(0, half, 2, dtype=F32) / half)
    ar, ac = row[:, None] * inv_freq, col[:, None] * inv_freq
    cos = jnp.concatenate([jnp.cos(ar), jnp.cos(ar), jnp.cos(ac), jnp.cos(ac)], axis=-1)
    ss = jnp.concatenate([-jnp.sin(ar), jnp.sin(ar), -jnp.sin(ac), jnp.sin(ac)], axis=-1)
    cos = jnp.tile(cos, (1, LANES // rot_dim))
    ss = jnp.tile(ss, (1, LANES // rot_dim))
    cos = jnp.concatenate([jnp.ones((n_ctx, LANES), F32), cos], axis=0)
    ss = jnp.concatenate([jnp.zeros((n_ctx, LANES), F32), ss], axis=0)
    return cos, ss


def _norm_mod_fwd(x2d, g, sh, sc, name, tr):
    n, d = x2d.shape

    def body(x_ref, g_ref, sh_ref, sc_ref, z_ref):
        xh, _ = _rms(x_ref[...])
        z_ref[...] = ((xh * g_ref[...]) * (1.0 + sc_ref[...]) + sh_ref[...]).astype(BF16)

    return pl.pallas_call(
        body,
        name=name,
        grid=(n // tr,),
        in_specs=[_rows(tr, d), _bcast(d), _bcast(d), _bcast(d)],
        out_specs=_rows(tr, d),
        out_shape=jax.ShapeDtypeStruct((n, d), BF16),
        compiler_params=_params(("parallel",)),
    )(x2d, g, sh, sc)


def _norm_mod_bwd(dz, dz_off, x2d, g, sc, dres, name, tr):
    n, d = x2d.shape
    want_dx = dres is not None

    def body(*refs):
        if want_dx:
            dz_ref, x_ref, g_ref, sc_ref, dres_ref, dx_ref, dg_ref, dsh_ref, dsc_ref = refs
        else:
            dz_ref, x_ref, g_ref, sc_ref, dg_ref, dsh_ref, dsc_ref = refs
        _acc_init(pl.program_id(0), [dg_ref, dsh_ref, dsc_ref])
        xh, r = _rms(x_ref[...])
        dzv = dz_ref[...]
        gv = g_ref[...]
        dsc_ref[...] += _csum(dzv * (xh * gv))
        dsh_ref[...] += _csum(dzv)
        dh = dzv * (1.0 + sc_ref[...])
        dg_ref[...] += _csum(dh * xh)
        if want_dx:
            dx_ref[...] = _rms_bwd(xh, r, dh * gv) + dres_ref[...]

    in_specs = [_rows(tr, d, dz_off), _rows(tr, d), _bcast(d), _bcast(d)]
    args = [dz, x2d, g, sc]
    out_specs = [_bcast(d)] * 3
    out_shape = [jax.ShapeDtypeStruct((1, d), F32)] * 3
    if want_dx:
        in_specs.append(_rows(tr, d))
        args.append(dres)
        out_specs = [_rows(tr, d)] + out_specs
        out_shape = [jax.ShapeDtypeStruct((n, d), F32)] + out_shape
    res = pl.pallas_call(
        body,
        name=name,
        grid=(n // tr,),
        in_specs=in_specs,
        out_specs=out_specs,
        out_shape=out_shape,
        compiler_params=_params(("arbitrary",)),
    )(*args)
    return res if want_dx else (None, *res)


_QA, _QB = MLA_ROPE // 4, GQA_HEAD_DIM // 4


def _kprep_fwd(pkv, kvg, kg, cos_a, ss_a, cos_b, ss_b, tr):
    n = pkv.shape[0]
    nb = GQA_KV_HEADS * GQA_HEAD_DIM

    def body(p_ref, kvg_ref, kg_ref, ca, sa, cb, sb, ckv_ref, kb_ref, vb_ref, kpe_ref):
        p = p_ref[...]
        xh, _ = _rms(p[:, :MLA_KV_LORA])
        ckv_ref[...] = (xh * kvg_ref[...]).astype(BF16)
        for e in range(GQA_KV_HEADS):
            lo = MLA_KV_LORA + e * GQA_HEAD_DIM
            kh, _ = _rms(p[:, lo : lo + GQA_HEAD_DIM])
            kb_ref[:, e * GQA_HEAD_DIM : (e + 1) * GQA_HEAD_DIM] = _rope(kh * kg_ref[...], cb[...], sb[...], _QB).astype(BF16)
        vb_ref[...] = p[:, MLA_KV_LORA + nb : MLA_KV_LORA + 2 * nb].astype(BF16)
        kr = _rope(p[:, MLA_KV_LORA + 2 * nb :], ca[...], sa[...], _QA)
        kpe_ref[:, :LANES] = kr.astype(BF16)
        kpe_ref[:, LANES:] = pltpu.roll(kr, MLA_ROPE, 1).astype(BF16)

    return pl.pallas_call(
        body,
        name="kprep_fwd",
        grid=(n // tr,),
        in_specs=[_rows(tr, KVP), _bcast(MLA_KV_LORA), _bcast(GQA_HEAD_DIM)] + [_rows(tr, LANES)] * 4,
        out_specs=[_rows(tr, MLA_KV_LORA), _rows(tr, nb), _rows(tr, nb), _rows(tr, 2 * LANES)],
        out_shape=[jax.ShapeDtypeStruct((n, w), BF16) for w in (MLA_KV_LORA, nb, nb, 2 * LANES)],
        compiler_params=_params(("parallel",)),
    )(pkv, kvg, kg, cos_a, ss_a, cos_b, ss_b)


def _kprep_bwd(pkv, dckv, dkb, dvb, dkpe, kvg, kg, cos_b, ss_b, tr):
    n = pkv.shape[0]
    nb = GQA_KV_HEADS * GQA_HEAD_DIM

    def body(p_ref, dckv_ref, dkb_ref, dvb_ref, dkpe_ref, kvg_ref, kg_ref, cb, sb, dp_ref, dkvg_ref, dkg_ref):
        _acc_init(pl.program_id(0), [dkvg_ref, dkg_ref])
        p = p_ref[...]
        xh, r = _rms(p[:, :MLA_KV_LORA])
        dn = dckv_ref[...]
        dkvg_ref[...] += _csum(dn * xh)
        dp_ref[:, :MLA_KV_LORA] = _rms_bwd(xh, r, dn * kvg_ref[...]).astype(BF16)
        for e in range(GQA_KV_HEADS):
            lo = MLA_KV_LORA + e * GQA_HEAD_DIM
            kh, rk = _rms(p[:, lo : lo + GQA_HEAD_DIM])
            dk = _rope_t(dkb_ref[:, e * GQA_HEAD_DIM : (e + 1) * GQA_HEAD_DIM], cb[...], sb[...], _QB)
            dkg_ref[...] += _csum(dk * kh)
            dp_ref[:, lo : lo + GQA_HEAD_DIM] = _rms_bwd(kh, rk, dk * kg_ref[...]).astype(BF16)
        dp_ref[:, MLA_KV_LORA + nb : MLA_KV_LORA + 2 * nb] = dvb_ref[...].astype(BF16)
        dp_ref[:, MLA_KV_LORA + 2 * nb :] = dkpe_ref[...].astype(BF16)

    return pl.pallas_call(
        body,
        name="kprep_bwd",
        grid=(n // tr,),
        in_specs=[_rows(tr, KVP), _rows(tr, MLA_KV_LORA), _rows(tr, nb), _rows(tr, nb), _rows(tr, LANES),
                  _bcast(MLA_KV_LORA), _bcast(GQA_HEAD_DIM), _rows(tr, LANES), _rows(tr, LANES)],
        out_specs=[_rows(tr, KVP), _bcast(MLA_KV_LORA), _bcast(GQA_HEAD_DIM)],
        out_shape=[jax.ShapeDtypeStruct((n, KVP), BF16), jax.ShapeDtypeStruct((1, MLA_KV_LORA), F32),
                   jax.ShapeDtypeStruct((1, GQA_HEAD_DIM), F32)],
        compiler_params=_params(("arbitrary",)),
    )(pkv, dckv, dkb, dvb, dkpe, kvg, kg, cos_b, ss_b)


def _kgrad_split(dka, dva, cos_a, ss_a, tr):
    n = dka.shape[0]
    wk = MLA_HEADS * 2 * LANES

    def body(dk_ref, dv_ref, ca, sa, dkv_ref, dkpe_ref):
        even = jnp.zeros((tr, LANES), F32)
        odd = jnp.zeros((tr, LANES), F32)
        for h in range(MLA_HEADS):
            dkv_ref[:, 2 * h * LANES : (2 * h + 1) * LANES] = dk_ref[:, 2 * h * LANES : (2 * h + 1) * LANES].astype(BF16)
            dkv_ref[:, (2 * h + 1) * LANES : (2 * h + 2) * LANES] = dv_ref[:, h * MLA_V : (h + 1) * MLA_V].astype(BF16)
            part = dk_ref[:, (2 * h + 1) * LANES : (2 * h + 2) * LANES]
            if h % 2 == 0:
                even = even + part
            else:
                odd = odd + part
        lane = lax.broadcasted_iota(jnp.int32, (tr, LANES), 1)
        low = lane < MLA_ROPE
        both = jnp.where(low, even, odd)
        tot = jnp.where(low, both + pltpu.roll(both, MLA_ROPE, 1), 0.0)
        dkpe_ref[...] = _rope_t(tot, ca[...], sa[...], _QA)

    return pl.pallas_call(
        body,
        name="kgrad_split",
        grid=(n // tr,),
        in_specs=[_rows(tr, wk), _rows(tr, MLA_HEADS * MLA_V), _rows(tr, LANES), _rows(tr, LANES)],
        out_specs=[_rows(tr, wk), _rows(tr, LANES)],
        out_shape=[jax.ShapeDtypeStruct((n, wk), BF16), jax.ShapeDtypeStruct((n, LANES), F32)],
        compiler_params=_params(("parallel",)),
    )(dka, dva, cos_a, ss_a)


def _qprep_fwd(pq, qg, gq, cos_b, ss_b, tr):
    n = pq.shape[0]
    nq = GQA_HEADS * GQA_HEAD_DIM

    def body(p_ref, qg_ref, gq_ref, cb, sb, cq_ref, qb_ref):
        xh, _ = _rms(p_ref[:, :MLA_Q_LORA])
        cq_ref[...] = (xh * qg_ref[...]).astype(BF16)
        for h in range(GQA_HEADS):
            lo = MLA_Q_LORA + h * GQA_HEAD_DIM
            qh, _ = _rms(p_ref[:, lo : lo + GQA_HEAD_DIM])
            qb_ref[:, h * GQA_HEAD_DIM : (h + 1) * GQA_HEAD_DIM] = _rope(qh * gq_ref[...], cb[...], sb[...], _QB).astype(BF16)

    return pl.pallas_call(
        body,
        name="qprep_fwd",
        grid=(n // tr,),
        in_specs=[_rows(tr, QC), _bcast(MLA_Q_LORA), _bcast(GQA_HEAD_DIM), _rows(tr, LANES), _rows(tr, LANES)],
        out_specs=[_rows(tr, MLA_Q_LORA), _rows(tr, nq)],
        out_shape=[jax.ShapeDtypeStruct((n, MLA_Q_LORA), BF16), jax.ShapeDtypeStruct((n, nq), BF16)],
        compiler_params=_params(("parallel",)),
    )(pq, qg, gq, cos_b, ss_b)


def _qprep_bwd(pq, dcq, dqb, qg, gq, cos_b, ss_b, tr):
    n = pq.shape[0]
    nq = GQA_HEADS * GQA_HEAD_DIM

    def body(p_ref, dcq_ref, dqb_ref, qg_ref, gq_ref, cb, sb, dp_ref, dqg_ref, dgq_ref):
        _acc_init(pl.program_id(0), [dqg_ref, dgq_ref])
        xh, r = _rms(p_ref[:, :MLA_Q_LORA])
        dn = dcq_ref[...]
        dqg_ref[...] += _csum(dn * xh)
        dp_ref[:, :MLA_Q_LORA] = _rms_bwd(xh, r, dn * qg_ref[...]).astype(BF16)
        for h in range(GQA_HEADS):
            lo = MLA_Q_LORA + h * GQA_HEAD_DIM
            qh, rq = _rms(p_ref[:, lo : lo + GQA_HEAD_DIM])
            dq = _rope_t(dqb_ref[:, h * GQA_HEAD_DIM : (h + 1) * GQA_HEAD_DIM], cb[...], sb[...], _QB)
            dgq_ref[...] += _csum(dq * qh)
            dp_ref[:, lo : lo + GQA_HEAD_DIM] = _rms_bwd(qh, rq, dq * gq_ref[...]).astype(BF16)

    return pl.pallas_call(
        body,
        name="qprep_bwd",
        grid=(n // tr,),
        in_specs=[_rows(tr, QC), _rows(tr, MLA_Q_LORA), _rows(tr, nq), _bcast(MLA_Q_LORA), _bcast(GQA_HEAD_DIM),
                  _rows(tr, LANES), _rows(tr, LANES)],
        out_specs=[_rows(tr, QC), _bcast(MLA_Q_LORA), _bcast(GQA_HEAD_DIM)],
        out_shape=[jax.ShapeDtypeStruct((n, QC), BF16), jax.ShapeDtypeStruct((1, MLA_Q_LORA), F32),
                   jax.ShapeDtypeStruct((1, GQA_HEAD_DIM), F32)],
        compiler_params=_params(("arbitrary",)),
    )(pq, dcq, dqb, qg, gq, cos_b, ss_b)


_QA_COLS = MLA_HEADS * (MLA_NOPE + MLA_ROPE)


def _qrope_fwd(qa, cos_a, ss_a, tr):
    n = qa.shape[0]

    def body(q_ref, ca, sa, o_ref):
        for j in range(MLA_HEADS // 2):
            lo = 3 * j * LANES
            o_ref[:, lo : lo + 2 * LANES] = q_ref[:, lo : lo + 2 * LANES].astype(BF16)
            o_ref[:, lo + 2 * LANES : lo + 3 * LANES] = _rope(q_ref[:, lo + 2 * LANES : lo + 3 * LANES], ca[...], sa[...], _QA).astype(BF16)

    return pl.pallas_call(
        body,
        name="qrope_fwd",
        grid=(n // tr,),
        in_specs=[_rows(tr, _QA_COLS), _rows(tr, LANES), _rows(tr, LANES)],
        out_specs=_rows(tr, _QA_COLS),
        out_shape=jax.ShapeDtypeStruct((n, _QA_COLS), BF16),
        compiler_params=_params(("parallel",)),
    )(qa, cos_a, ss_a)


def _qrope_bwd(dq2, cos_a, ss_a, tr):
    n = dq2.shape[0]

    def body(d_ref, ca, sa, o_ref):
        for j in range(MLA_HEADS // 2):
            lo = 3 * j * LANES
            h0, h1 = 2 * j, 2 * j + 1
            o_ref[:, lo : lo + LANES] = d_ref[:, 2 * h0 * LANES : (2 * h0 + 1) * LANES].astype(BF16)
            o_ref[:, lo + LANES : lo + 2 * LANES] = d_ref[:, 2 * h1 * LANES : (2 * h1 + 1) * LANES].astype(BF16)
            pe = d_ref[:, (2 * h0 + 1) * LANES : (2 * h0 + 2) * LANES] + d_ref[:, (2 * h1 + 1) * LANES : (2 * h1 + 2) * LANES]
            o_ref[:, lo + 2 * LANES : lo + 3 * LANES] = _rope_t(pe, ca[...], sa[...], _QA).astype(BF16)

    return pl.pallas_call(
        body,
        name="qrope_bwd",
        grid=(n // tr,),
        in_specs=[_rows(tr, MLA_HEADS * 2 * LANES), _rows(tr, LANES), _rows(tr, LANES)],
        out_specs=_rows(tr, _QA_COLS),
        out_shape=jax.ShapeDtypeStruct((n, _QA_COLS), BF16),
        compiler_params=_params(("parallel",)),
    )(dq2, cos_a, ss_a)


def _cat(refs):
    vals = [r[...] for r in refs]
    return vals[0] if len(vals) == 1 else jnp.concatenate(vals, axis=-1)


def _attn_fwd(qparts, kparts, vpart, n_heads, group, dv, scale, name, tq):
    T, Tk = qparts[0][0].shape[0], kparts[0][0].shape[0]
    nq_, nk_ = len(qparts), len(kparts)

    def body(*refs):
        q = _cat(refs[:nq_])
        k = _cat(refs[nq_ : nq_ + nk_])
        v_ref, o_ref, lse_ref = refs[nq_ + nk_ :]
        s = lax.dot_general(q, k, _DIMS["NT"], preferred_element_type=F32) * scale
        m = jnp.max(s, axis=-1, keepdims=True)
        p = jnp.exp(s - m)
        l = jnp.sum(p, axis=-1, keepdims=True)
        pn = (p * (1.0 / l)).astype(BF16)
        o_ref[...] = jnp.dot(pn, v_ref[...], preferred_element_type=F32).astype(BF16)
        lse_ref[...] = m + jnp.log(l)

    in_specs = [pl.BlockSpec((tq, LANES), lambda h, i, f=f: (i, f(h))) for _, f in qparts]
    in_specs += [pl.BlockSpec((Tk, LANES), lambda h, i, f=f: (0, f(h // group))) for _, f in kparts]
    fv = vpart[1]
    in_specs.append(pl.BlockSpec((Tk, dv), lambda h, i: (0, fv(h // group))))
    return pl.pallas_call(
        body,
        name=name,
        grid=(n_heads, T // tq),
        in_specs=in_specs,
        out_specs=[pl.BlockSpec((tq, dv), lambda h, i: (i, h)), pl.BlockSpec((None, tq, 1), lambda h, i: (h, i, 0))],
        out_shape=[jax.ShapeDtypeStruct((T, n_heads * dv), BF16), jax.ShapeDtypeStruct((n_heads, T, 1), F32)],
        compiler_params=_params(("parallel", "parallel")),
    )(*[a for a, _ in qparts], *[a for a, _ in kparts], vpart[0])


def _attn_bwd(qparts, kparts, vpart, o, do, lse, n_heads, group, dv, scale, name, tq):
    T, Tk = qparts[0][0].shape[0], kparts[0][0].shape[0]
    nq_, nk_ = len(qparts), len(kparts)
    dk_ = LANES * nq_
    n_kv = n_heads // group
    nblk = T // tq

    def head(hk, i):
        return hk * group + i // nblk

    def body(*refs):
        q = _cat(refs[:nq_])
        k = _cat(refs[nq_ : nq_ + nk_])
        v_ref, o_ref, do_ref, lse_ref, dq_ref, dk_ref, dv_ref = refs[nq_ + nk_ :]
        _acc_init(pl.program_id(1), [dk_ref, dv_ref])
        s = lax.dot_general(q, k, _DIMS["NT"], preferred_element_type=F32) * scale
        p = jnp.exp(s - lse_ref[...])
        dov = do_ref[...]
        dp = lax.dot_general(dov, v_ref[...], _DIMS["NT"], preferred_element_type=F32)
        delta = jnp.sum(dov.astype(F32) * o_ref[...].astype(F32), axis=-1, keepdims=True)
        ds = (p * (dp - delta) * scale).astype(BF16)
        dq_ref[...] = jnp.dot(ds, k, preferred_element_type=F32)
        dk_ref[...] += lax.dot_general(ds, q, _DIMS["TN"], preferred_element_type=F32)
        dv_ref[...] += lax.dot_general(p.astype(BF16), dov, _DIMS["TN"], preferred_element_type=F32)

    in_specs = [pl.BlockSpec((tq, LANES), lambda hk, i, f=f: (i % nblk, f(head(hk, i)))) for _, f in qparts]
    in_specs += [pl.BlockSpec((Tk, LANES), lambda hk, i, f=f: (0, f(hk))) for _, f in kparts]
    fv = vpart[1]
    in_specs.append(pl.BlockSpec((Tk, dv), lambda hk, i: (0, fv(hk))))
    in_specs += [pl.BlockSpec((tq, dv), lambda hk, i: (i % nblk, head(hk, i)))] * 2
    in_specs.append(pl.BlockSpec((None, tq, 1), lambda hk, i: (head(hk, i), i % nblk, 0)))
    return pl.pallas_call(
        body,
        name=name,
        grid=(n_kv, group * nblk),
        in_specs=in_specs,
        out_specs=[pl.BlockSpec((tq, dk_), lambda hk, i: (i % nblk, head(hk, i))),
                   pl.BlockSpec((Tk, dk_), lambda hk, i: (0, hk)),
                   pl.BlockSpec((Tk, dv), lambda hk, i: (0, hk))],
        out_shape=[jax.ShapeDtypeStruct((T, n_heads * dk_), F32), jax.ShapeDtypeStruct((Tk, n_kv * dk_), F32),
                   jax.ShapeDtypeStruct((Tk, n_kv * dv), F32)],
        compiler_params=_params(("parallel", "arbitrary")),
    )(*[a for a, _ in qparts], *[a for a, _ in kparts], vpart[0], o, do, lse)


def _gates_fwd(pg, ya, yb, tr):
    n, d = ya.shape

    def body(pg_ref, ya_ref, yb_ref, o_ref):
        ga = jax.nn.sigmoid(pg_ref[:, :d])
        gb = jax.nn.sigmoid(pg_ref[:, d:])
        o_ref[...] = (ga * ya_ref[...] + gb * yb_ref[...]).astype(BF16)

    return pl.pallas_call(
        body,
        name="gates_fwd",
        grid=(n // tr,),
        in_specs=[_rows(tr, 2 * d), _rows(tr, d), _rows(tr, d)],
        out_specs=_rows(tr, d),
        out_shape=jax.ShapeDtypeStruct((n, d), BF16),
        compiler_params=_params(("parallel",)),
    )(pg, ya, yb)


def _gates_bwd(dm, pg, ya, yb, tr):
    n, d = ya.shape

    def body(dm_ref, pg_ref, ya_ref, yb_ref, dya_ref, dyb_ref, dpg_ref):
        dmv = dm_ref[...]
        ga = jax.nn.sigmoid(pg_ref[:, :d])
        gb = jax.nn.sigmoid(pg_ref[:, d:])
        dya_ref[...] = (dmv * ga).astype(BF16)
        dyb_ref[...] = (dmv * gb).astype(BF16)
        dpg_ref[:, :d] = (dmv * ya_ref[...] * ga * (1.0 - ga)).astype(BF16)
        dpg_ref[:, d:] = (dmv * yb_ref[...] * gb * (1.0 - gb)).astype(BF16)

    return pl.pallas_call(
        body,
        name="gates_bwd",
        grid=(n // tr,),
        in_specs=[_rows(tr, d), _rows(tr, 2 * d), _rows(tr, d), _rows(tr, d)],
        out_specs=[_rows(tr, d), _rows(tr, d), _rows(tr, 2 * d)],
        out_shape=[jax.ShapeDtypeStruct((n, d), BF16), jax.ShapeDtypeStruct((n, d), BF16), jax.ShapeDtypeStruct((n, 2 * d), BF16)],
        compiler_params=_params(("parallel",)),
    )(dm, pg, ya, yb)


def _resid_norm2_fwd(x2d, att, g1, n2g, sh2, sc2, tr):
    n, d = x2d.shape

    def body(x_ref, a_ref, g1_ref, g_ref, sh_ref, sc_ref, x1_ref, z_ref):
        x1 = x_ref[...] + g1_ref[...] * a_ref[...]
        x1_ref[...] = x1
        xh, _ = _rms(x1)
        z_ref[...] = ((xh * g_ref[...]) * (1.0 + sc_ref[...]) + sh_ref[...]).astype(BF16)

    return pl.pallas_call(
        body,
        name="resid_norm2_fwd",
        grid=(n // tr,),
        in_specs=[_rows(tr, d), _rows(tr, d)] + [_bcast(d)] * 4,
        out_specs=[_rows(tr, d), _rows(tr, d)],
        out_shape=[jax.ShapeDtypeStruct((n, d), F32), jax.ShapeDtypeStruct((n, d), BF16)],
        compiler_params=_params(("parallel",)),
    )(x2d, att, g1, n2g, sh2, sc2)


def _resid_norm2_bwd(dz2, x1, dx2, att, n2g, sc2, g1, tr):
    n, d = x1.shape

    def body(dz_ref, x1_ref, dx2_ref, a_ref, g_ref, sc_ref, g1_ref, dx1_ref, da_ref, dg_ref, dsh_ref, dsc_ref, dg1_ref):
        _acc_init(pl.program_id(0), [dg_ref, dsh_ref, dsc_ref, dg1_ref])
        xh, r = _rms(x1_ref[...])
        dzv = dz_ref[...]
        gv = g_ref[...]
        dsc_ref[...] += _csum(dzv * (xh * gv))
        dsh_ref[...] += _csum(dzv)
        dh = dzv * (1.0 + sc_ref[...])
        dg_ref[...] += _csum(dh * xh)
        dx1 = _rms_bwd(xh, r, dh * gv) + dx2_ref[...]
        dx1_ref[...] = dx1
        dg1_ref[...] += _csum(dx1 * a_ref[...])
        da_ref[...] = (dx1 * g1_ref[...]).astype(BF16)

    return pl.pallas_call(
        body,
        name="resid_norm2_bwd",
        grid=(n // tr,),
        in_specs=[_rows(tr, d)] * 4 + [_bcast(d)] * 3,
        out_specs=[_rows(tr, d), _rows(tr, d)] + [_bcast(d)] * 4,
        out_shape=[jax.ShapeDtypeStruct((n, d), F32), jax.ShapeDtypeStruct((n, d), BF16)] + [jax.ShapeDtypeStruct((1, d), F32)] * 4,
        compiler_params=_params(("arbitrary",)),
    )(dz2, x1, dx2, att, n2g, sc2, g1)


def _shift_prev(u):
    row = lax.broadcasted_iota(jnp.int32, u.shape, 0)
    return jnp.where(row == 0, 0.0, pltpu.roll(u, 1, 0))


def _shift_next(u):
    n = u.shape[0]
    row = lax.broadcasted_iota(jnp.int32, u.shape, 0)
    return jnp.where(row == n - 1, 0.0, pltpu.roll(u, n - 1, 0))


def _conv3(u, w_ref, b_ref):
    return b_ref[...] + w_ref[0:1, :] * _shift_prev(u) + w_ref[1:2, :] * u + w_ref[2:3, :] * _shift_next(u)


def _conv_fwd(u, cw, cb, tc):
    n, two_f = u.shape
    f = two_f // 2
    nb = f // tc

    def body(ua_ref, ub_ref, wa_ref, wb_ref, ba_ref, bb_ref, h_ref):
        a = _conv3(ua_ref[...].astype(F32), wa_ref, ba_ref)
        b = _conv3(ub_ref[...].astype(F32), wb_ref, bb_ref)
        h_ref[...] = (a * jax.nn.sigmoid(a) * b).astype(BF16)

    col = lambda rows, off: pl.BlockSpec((rows, tc), lambda i: (0, i + off))
    return pl.pallas_call(
        body,
        name="conv_fwd",
        grid=(nb,),
        in_specs=[col(n, 0), col(n, nb), col(3, 0), col(3, nb), col(1, 0), col(1, nb)],
        out_specs=col(n, 0),
        out_shape=jax.ShapeDtypeStruct((n, f), BF16),
        compiler_params=_params(("parallel",)),
    )(u, u, cw, cw, cb, cb)


def _conv_bwd(u, dh, cw, cb, tc):
    n, two_f = u.shape
    f = two_f // 2
    nb = f // tc

    def part(uv, duc, w_ref, du_ref, dw_ref, db_ref):
        db_ref[...] = _csum(duc)
        dw_ref[0:1, :] = _csum(duc * _shift_prev(uv))
        dw_ref[1:2, :] = _csum(duc * uv)
        dw_ref[2:3, :] = _csum(duc * _shift_next(uv))
        du_ref[...] = (w_ref[0:1, :] * _shift_next(duc) + w_ref[1:2, :] * duc + w_ref[2:3, :] * _shift_prev(duc)).astype(BF16)

    def body(ua_ref, ub_ref, dh_ref, wa_ref, wb_ref, ba_ref, bb_ref, dua_ref, dub_ref, dwa_ref, dwb_ref, dba_ref, dbb_ref):
        ua = ua_ref[...].astype(F32)
        ub = ub_ref[...].astype(F32)
        a = _conv3(ua, wa_ref, ba_ref)
        b = _conv3(ub, wb_ref, bb_ref)
        dhv = dh_ref[...].astype(F32)
        sg = jax.nn.sigmoid(a)
        da = dhv * b * (sg * (1.0 + a * (1.0 - sg)))
        db = dhv * (a * sg)
        part(ua, da, wa_ref, dua_ref, dwa_ref, dba_ref)
        part(ub, db, wb_ref, dub_ref, dwb_ref, dbb_ref)

    col = lambda rows, off: pl.BlockSpec((rows, tc), lambda i: (0, i + off))
    return pl.pallas_call(
        body,
        name="conv_bwd",
        grid=(nb,),
        in_specs=[col(n, 0), col(n, nb), col(n, 0), col(3, 0), col(3, nb), col(1, 0), col(1, nb)],
        out_specs=[col(n, 0), col(n, 0), col(3, 0), col(3, 0), col(1, 0), col(1, 0)],
        out_shape=[jax.ShapeDtypeStruct((n, f), BF16)] * 2 + [jax.ShapeDtypeStruct((3, f), F32)] * 2 + [jax.ShapeDtypeStruct((1, f), F32)] * 2,
        compiler_params=_params(("parallel",)),
    )(u, u, dh, cw, cw, cb, cb)


def _loss_head(x1, f, g2, fg, tgt, tr):
    n, d = x1.shape

    def body(x1_ref, f_ref, g2_ref, fg_ref, t_ref, sq_ref, dx2_ref, dfg_ref, dg2_ref, df_ref):
        _acc_init(pl.program_id(0), [sq_ref, dfg_ref, dg2_ref])
        fv = f_ref[...]
        xh, r = _rms(x1_ref[...] + g2_ref[...] * fv)
        err = xh * fg_ref[...] - t_ref[...]
        sq_ref[...] += _csum(err * err)
        dy = err * (1.0 / d)
        dfg_ref[...] += _csum(dy * xh)
        dx2 = _rms_bwd(xh, r, dy * fg_ref[...])
        dx2_ref[...] = dx2
        dg2_ref[...] += _csum(dx2 * fv)
        df_ref[...] = (dx2 * g2_ref[...]).astype(BF16)

    return pl.pallas_call(
        body,
        name="loss_head",
        grid=(n // tr,),
        in_specs=[_rows(tr, d), _rows(tr, d), _bcast(d), _bcast(d), _rows(tr, d)],
        out_specs=[_bcast(d), _rows(tr, d), _bcast(d), _bcast(d), _rows(tr, d)],
        out_shape=[jax.ShapeDtypeStruct((1, d), F32), jax.ShapeDtypeStruct((n, d), F32), jax.ShapeDtypeStruct((1, d), F32),
                   jax.ShapeDtypeStruct((1, d), F32), jax.ShapeDtypeStruct((n, d), BF16)],
        compiler_params=_params(("arbitrary",)),
    )(x1, f, g2, fg, tgt)


def _sum_slots(g, name):
    s, r, w = g.shape

    def body(g_ref, o_ref):
        acc = g_ref[0]
        for k in range(1, s):
            acc = acc + g_ref[k]
        o_ref[...] = acc

    return pl.pallas_call(body, name=name, out_shape=jax.ShapeDtypeStruct((r, w), F32))(g)


def _silu_grad_mul(ds, cvec):
    def body(d_ref, c_ref, o_ref):
        cv = c_ref[...]
        sg = jax.nn.sigmoid(cv)
        o_ref[...] = d_ref[...] * (sg * (1.0 + cv * (1.0 - sg)))

    return pl.pallas_call(body, name="silu_grad_mul", out_shape=jax.ShapeDtypeStruct(ds.shape, F32))(ds, cvec)


def _adamw(w, g, m, v, name, g_transposed=False):
    r, cdim = w.shape
    tr = _pick(r, 1024, LANES if g_transposed else 8)
    tc = _pick(cdim, max(LANES, (1 << 19) // tr))
    b1c = 1.0 - ADAM_B1**ADAM_STEP
    b2c = 1.0 - ADAM_B2**ADAM_STEP

    def body(w_ref, g_ref, m_ref, v_ref, *outs):
        d_ref, mo_ref, vo_ref = outs[-3:]
        gv = g_ref[...]
        if g_transposed:
            gv = gv.T
            outs[0][...] = gv
        mn = ADAM_B1 * m_ref[...] + (1.0 - ADAM_B1) * gv
        vn = ADAM_B2 * v_ref[...] + (1.0 - ADAM_B2) * (gv * gv)
        mo_ref[...] = mn
        vo_ref[...] = vn
        d_ref[...] = -ADAM_LR * ((mn / b1c) / (jnp.sqrt(vn / b2c) + ADAM_EPS) + ADAM_WD * w_ref[...])

    spec = pl.BlockSpec((tr, tc), lambda i, j: (i, j))
    g_spec = pl.BlockSpec((tc, tr), lambda i, j: (j, i)) if g_transposed else spec
    n_out = 4 if g_transposed else 3
    res = pl.pallas_call(
        body,
        name=name,
        grid=(r // tr, cdim // tc),
        in_specs=[spec, g_spec, spec, spec],
        out_specs=[spec] * n_out,
        out_shape=[jax.ShapeDtypeStruct((r, cdim), F32)] * n_out,
        compiler_params=_params(("parallel", "parallel")),
    )(w, g, m, v)
    return res if g_transposed else [g, *res]


def _place():
    return lax.axis_index("x"), lax.axis_index("y"), lax.axis_index("c")


def _remote(src, dst, send_sem, recv_sem, dev):
    return pltpu.make_async_remote_copy(src_ref=src, dst_ref=dst, send_sem=send_sem, recv_sem=recv_sem, device_id=dev, device_id_type=MESH)


ANY = pl.BlockSpec(memory_space=pl.ANY)


def _all_gather_small(v, name):
    r, w = v.shape

    def body(v_ref, o_ref, send, recv, lsem):
        x, y, c = _place()
        me = 4 * x + 2 * y + c
        mine = pltpu.make_async_copy(v_ref, o_ref.at[me], lsem)
        mine.start()
        sent = []
        for k in range(1, 8):
            px, py, pc = x ^ (k >> 2), y ^ ((k >> 1) & 1), c ^ (k & 1)
            cp = _remote(v_ref, o_ref.at[me], send.at[k - 1], recv.at[k - 1], (px, py, pc))
            cp.start()
            sent.append(cp)
        for k in range(1, 8):
            px, py, pc = x ^ (k >> 2), y ^ ((k >> 1) & 1), c ^ (k & 1)
            slot = o_ref.at[4 * px + 2 * py + pc]
            _remote(slot, slot, send.at[k - 1], recv.at[k - 1], (x, y, c)).wait_recv()
        for cp in sent:
            cp.wait_send()
        mine.wait()

    return pl.pallas_call(
        body,
        name=name,
        out_shape=jax.ShapeDtypeStruct((8, r, w), F32),
        in_specs=[pl.BlockSpec(memory_space=pltpu.VMEM)],
        out_specs=pl.BlockSpec(memory_space=pltpu.VMEM),
        scratch_shapes=[pltpu.SemaphoreType.DMA((7,)), pltpu.SemaphoreType.DMA((7,)), pltpu.SemaphoreType.DMA],
        compiler_params=pltpu.CompilerParams(vmem_limit_bytes=VMEM_LIMIT),
    )(v)


HBM = pl.BlockSpec(memory_space=pltpu.HBM)
SEM = pl.BlockSpec(memory_space=pltpu.SEMAPHORE)
EFFECT = pltpu.SideEffectType.DATAFLOW_SIDE_EFFECTING


def _other_chips(x, y):
    return [(1 - x, y), (x, 1 - y), (1 - x, 1 - y)]


def _bulk_start(name, srcs, land_shapes, n_copies, copies, after):
    n, m = len(srcs), len(land_shapes)

    def body(*refs):
        src_refs, land_refs = refs[:n], refs[n : n + m]
        send, recv = refs[n + m + 1], refs[n + m + 2]
        token = refs[-1]
        for k, (s, d, dev) in enumerate(copies(src_refs, land_refs)):
            _remote(s, d, send.at[k], recv.at[k], dev).start()
        token[...] = jnp.zeros_like(token)

    lands = [pltpu.with_memory_space_constraint(lax.empty(s.shape, s.dtype), pltpu.HBM) for s in land_shapes]
    out = pl.pallas_call(
        body,
        name=name,
        out_shape=(pltpu.SemaphoreType.DMA((n_copies,)), pltpu.SemaphoreType.DMA((n_copies,)),
                   *[pltpu.HBM(s.shape, s.dtype) for s in srcs], *[pltpu.HBM(s.shape, s.dtype) for s in land_shapes],
                   jax.ShapeDtypeStruct((8, LANES), F32)),
        in_specs=[HBM] * (n + m) + [ANY],
        out_specs=(SEM, SEM, *[HBM] * (n + m), pl.BlockSpec(memory_space=pltpu.VMEM)),
        input_output_aliases={i: 2 + i for i in range(n + m)},
        compiler_params=pltpu.CompilerParams(has_side_effects=EFFECT),
    )(*[pltpu.with_memory_space_constraint(s, pltpu.HBM) for s in srcs], *lands, after)
    return out[0], out[1], list(out[2 : 2 + n]), list(out[2 + n : 2 + n + m]), out[-1][0:1, 0:1]


def _bulk_wait(name, send, recv, srcs, lands, after, waits):
    n, m = len(srcs), len(lands)

    def body(*refs):
        src_refs, land_refs = refs[:n], refs[n : n + m]
        send_sem, recv_sem = refs[n + m], refs[n + m + 1]
        x, y, c = _place()
        for k, (s, d) in enumerate(waits(src_refs, land_refs)):
            cp = _remote(s, d, send_sem.at[k], recv_sem.at[k], (x, y, c))
            cp.wait_send()
            cp.wait_recv()

    out = pl.pallas_call(
        body,
        name=name,
        out_shape=tuple(pltpu.HBM(s.shape, s.dtype) for s in (*srcs, *lands)),
        in_specs=[HBM] * (n + m) + [SEM, SEM, ANY],
        out_specs=tuple([HBM] * (n + m)),
        input_output_aliases={i: i for i in range(n + m)},
        compiler_params=pltpu.CompilerParams(has_side_effects=EFFECT),
    )(*srcs, *lands, send, recv, after)
    return list(out[:n]), list(out[n:])


def _gather_start(shards, after, name):
    def copies(src, land):
        x, y, c = _place()
        j = 2 * x + y
        return [(src[a].at[c], land[a].at[j, c], (px, py, c)) for a in range(len(shards)) for px, py in _other_chips(x, y)]

    shapes = [jax.ShapeDtypeStruct((4,) + s.shape, s.dtype) for s in shards]
    return _bulk_start(name, shards, shapes, 3 * len(shards), copies, after)


def _gather_wait(started, after, name):
    send, recv, srcs, lands, _ = started

    def waits(src, land):
        x, y, c = _place()
        return [(src[a].at[c], land[a].at[2 * px + py, c]) for a in range(len(srcs)) for px, py in _other_chips(x, y)]

    return _bulk_wait(name, send, recv, srcs, lands, after, waits)


def _forward_halves(lands, name):
    n = len(lands)

    def body(*refs):
        bufs = refs[n : 2 * n]
        send, recv = refs[2 * n :]
        x, y, c = _place()
        started = []
        for a in range(n):
            for k, (px, py) in enumerate(_other_chips(x, y)):
                blk = bufs[a].at[2 * px + py, c]
                cp = _remote(blk, blk, send.at[3 * a + k], recv.at[3 * a + k], (x, y, 1 - c))
                cp.start()
                started.append(cp)
        for a in range(n):
            for k, (px, py) in enumerate(_other_chips(x, y)):
                blk = bufs[a].at[2 * px + py, 1 - c]
                _remote(blk, blk, send.at[3 * a + k], recv.at[3 * a + k], (x, y, c)).wait_recv()
        for cp in started:
            cp.wait_send()

    return pl.pallas_call(
        body,
        name=name,
        out_shape=[jax.ShapeDtypeStruct(b.shape, b.dtype) for b in lands],
        in_specs=[ANY] * n,
        out_specs=[ANY] * n,
        input_output_aliases={i: i for i in range(n)},
        scratch_shapes=[pltpu.SemaphoreType.DMA((3 * n,)), pltpu.SemaphoreType.DMA((3 * n,))],
    )(*lands)


def _gather_finish(started, after, tag):
    shards, lands = _gather_wait(started, after, "gather_wait_" + tag)
    lands = _forward_halves(lands, "gather_forward_" + tag)
    j = 2 * lax.axis_index("x") + lax.axis_index("y")
    full = [lax.dynamic_update_slice(b, s[None], (j, 0, 0, 0)) for b, s in zip(lands, shards)]
    return [f.reshape(4 * f.shape[2] * 2, f.shape[3]) for f in full]


def _swap_halves(grads, name):
    n = len(grads)

    def body(*refs):
        ins, outs = refs[:n], refs[n : 2 * n]
        send, recv = refs[2 * n :]
        x, y, c = _place()
        started = []
        for a in range(n):
            for s in range(4):
                cp = _remote(ins[a].at[s, 1 - c], outs[a].at[s], send.at[4 * a + s], recv.at[4 * a + s], (x, y, 1 - c))
                cp.start()
                started.append(cp)
        for cp in started:
            cp.wait_recv()
        for cp in started:
            cp.wait_send()

    return pl.pallas_call(
        body,
        name=name,
        out_shape=[jax.ShapeDtypeStruct((4,) + g.shape[2:], g.dtype) for g in grads],
        in_specs=[ANY] * n,
        out_specs=[ANY] * n,
        scratch_shapes=[pltpu.SemaphoreType.DMA((4 * n,)), pltpu.SemaphoreType.DMA((4 * n,))],
    )(*grads)


def _add_halves(grads, others, tag):
    outs = []
    for a, (g, o) in enumerate(zip(grads, others)):
        _, _, rh, cdim = g.shape
        tr = _pick(rh, 512, 16)

        def body(g_ref, o_ref, p_ref):
            c = lax.axis_index("c")
            own = jnp.where(c == 0, g_ref[0].astype(F32), g_ref[1].astype(F32))
            p_ref[...] = (own + o_ref[...].astype(F32)).astype(BF16)

        outs.append(
            pl.pallas_call(
                body,
                name=f"add_halves_{tag}{a}",
                grid=(4, rh // tr),
                in_specs=[pl.BlockSpec((None, 2, tr, cdim), lambda s, i: (s, 0, i, 0)), pl.BlockSpec((None, tr, cdim), lambda s, i: (s, i, 0))],
                out_specs=pl.BlockSpec((None, tr, cdim), lambda s, i: (s, i, 0)),
                out_shape=jax.ShapeDtypeStruct((4, rh, cdim), BF16),
                compiler_params=_params(("parallel", "parallel")),
            )(g, o)
        )
    return outs


def _exchange_start(parts, after, name):
    def copies(src, land):
        x, y, c = _place()
        j = 2 * x + y
        return [(src[a].at[2 * px + py], land[a].at[j], (px, py, c)) for a in range(len(parts)) for px, py in _other_chips(x, y)]

    return _bulk_start(name, parts, [jax.ShapeDtypeStruct(p.shape, p.dtype) for p in parts], 3 * len(parts), copies, after)


def _exchange_finish(started, after, name):
    send, recv, srcs, lands, _ = started

    def waits(src, land):
        x, y, _ = _place()
        return [(src[a].at[2 * px + py], land[a].at[2 * px + py]) for a in range(len(srcs)) for px, py in _other_chips(x, y)]

    srcs, lands = _bulk_wait(name, send, recv, srcs, lands, after, waits)
    j = 2 * lax.axis_index("x") + lax.axis_index("y")
    return [lax.dynamic_update_slice(b, lax.dynamic_slice(p, (j, 0, 0), (1,) + p.shape[1:]), (j, 0, 0)) for b, p in zip(lands, srcs)]


def _sum_chips(recvd, tag):
    outs = []
    for a, g in enumerate(recvd):
        _, rh, cdim = g.shape
        tr = _pick(rh, 512, 16)

        def body(g_ref, o_ref):
            o_ref[...] = ((g_ref[0].astype(F32) + g_ref[1].astype(F32)) + g_ref[2].astype(F32)) + g_ref[3].astype(F32)

        outs.append(
            pl.pallas_call(
                body,
                name=f"sum_chips_{tag}{a}",
                grid=(rh // tr,),
                in_specs=[pl.BlockSpec((4, tr, cdim), lambda i: (0, i, 0))],
                out_specs=pl.BlockSpec((tr, cdim), lambda i: (i, 0)),
                out_shape=jax.ShapeDtypeStruct((rh, cdim), F32),
                compiler_params=_params(("parallel",)),
            )(g)
        )
    return outs


def _join_halves(halves, name):
    n = len(halves)

    def body(*refs):
        ins, outs = refs[:n], refs[n : 2 * n]
        send, recv = refs[2 * n :]
        x, y, c = _place()
        started = []
        for a in range(n):
            cp = _remote(ins[a], outs[a], send.at[a], recv.at[a], (x, y, 1 - c))
            cp.start()
            started.append(cp)
        for cp in started:
            cp.wait_recv()
        for cp in started:
            cp.wait_send()

    others = pl.pallas_call(
        body,
        name=name,
        out_shape=[jax.ShapeDtypeStruct(h.shape, h.dtype) for h in halves],
        in_specs=[ANY] * n,
        out_specs=[ANY] * n,
        scratch_shapes=[pltpu.SemaphoreType.DMA((n,)), pltpu.SemaphoreType.DMA((n,))],
    )(*halves)
    first = lax.axis_index("c") == 0
    return [jnp.concatenate([jnp.where(first, h, o), jnp.where(first, o, h)], axis=0) for h, o in zip(halves, others)]


def _scatter_start(grads, tag):
    views = [g.reshape(4, 2, g.shape[0] // 8, g.shape[1]) for g in grads]
    mine = _add_halves(views, _swap_halves(views, "swap_halves_" + tag), tag)
    return _exchange_start(mine, mine[-1], "exchange_start_" + tag)


def _scatter_finish(started, after, tag):
    halves = _sum_chips(_exchange_finish(started, after, "exchange_wait_" + tag), tag)
    return _join_halves(halves, "join_halves_" + tag)


def _t_bf16(w):
    return w.T.astype(BF16)


def kernel(x, c, ctx, c_ctx, w_ada, b_ada, norm1_g, w_in, mla_q_norm_g, w_q_up, mla_kv_norm_g, w_kv_up, gqa_q_norm_g, gqa_k_norm_g, w_br_a, w_br_b, w_out, norm2_g, w_up, conv_w, conv_b, w_down, final_norm_g, loss_target, m_c_ctx, m_w_ada, m_b_ada, m_norm1_g, m_w_in, m_mla_q_norm_g, m_w_q_up, m_mla_kv_norm_g, m_w_kv_up, m_gqa_q_norm_g, m_gqa_k_norm_g, m_w_br_a, m_w_br_b, m_w_out, m_norm2_g, m_w_up, m_conv_w, m_conv_b, m_w_down, m_final_norm_g, v_c_ctx, v_w_ada, v_b_ada, v_norm1_g, v_w_in, v_mla_q_norm_g, v_w_q_up, v_mla_kv_norm_g, v_w_kv_up, v_gqa_q_norm_g, v_gqa_k_norm_g, v_w_br_a, v_w_br_b, v_w_out, v_norm2_g, v_w_up, v_conv_w, v_conv_b, v_w_down, v_final_norm_g):
    T, D = x.shape[1], x.shape[2]
    C = ctx.shape[1]
    NA = w_ada.shape[2]
    NW = w_up.shape[2]
    F2 = 4 * NW
    FF = F2 // 2
    xi, yi, ci = _place()
    j = 2 * xi + yi
    me = 4 * xi + 2 * yi + ci
    tr = _pick(C, 128, 8)
    tq = _pick(T, 256)

    x2d, tgt, ctx2d = x[0], loss_target[0], ctx[0]
    fg = final_norm_g.reshape(1, D)
    cc = c_ctx.reshape(1, D)

    w0 = max(D, NW)
    pay = jnp.zeros((8, w0), F32).at[0:1, :D].set(c).at[1:4, :NW].set(conv_w[0])
    got = _all_gather_small(pay, "gather_cond")
    c_all = got[:, 0, :D]
    cw = jnp.concatenate([got[2 * s, 1:4, :NW] for s in range(4)], axis=1)
    s16 = jnp.concatenate([c_all, cc, jnp.zeros((7, D), F32)], axis=0)
    b_cols = lax.dynamic_slice(b_ada, (0, j * NA), (1, NA))
    ada_part = _mm(s16, w_ada[0], "NN", F32, "ada_fwd", act="silu", bias=b_cols)
    got = _all_gather_small(ada_part, "gather_ada")
    ada = jnp.concatenate([got[2 * s] for s in range(4)], axis=1)
    lat = lax.dynamic_slice(ada, (me, 0), (1, 6 * D))
    sh1, sc1, g1, sh2, sc2, g2 = [lat[:, k * D : (k + 1) * D] for k in range(6)]
    csh, csc = ada[8:9, :D], ada[8:9, D : 2 * D]

    wq3 = w_q_up[0].reshape(MLA_Q_LORA, 2, MLA_NOPE + MLA_ROPE)
    wq_perm = jnp.concatenate([wq3[:, :, :MLA_NOPE].reshape(MLA_Q_LORA, -1), wq3[:, :, MLA_NOPE:].reshape(MLA_Q_LORA, -1)], axis=1)
    shards = [_t_bf16(w_in[0]), _t_bf16(wq_perm), _t_bf16(w_kv_up[0]), _t_bf16(w_br_a[0]), _t_bf16(w_br_b[0]),
              w_out[0].astype(BF16), _t_bf16(w_up[0]), w_down[0].astype(BF16)]
    halves = [s.reshape(2, s.shape[0] // 2, s.shape[1]) for s in shards]
    ag_in = _gather_start(halves[0:1], got, "gather_start_in")
    ag_mix = _gather_start(halves[1:6], ag_in[4], "gather_start_mix")
    ag_ffn = _gather_start(halves[6:8], ag_mix[4], "gather_start_ffn")
    sh1 = sh1 + ag_ffn[4]

    cos_a, ss_a = _rope_tables(C, T, MLA_ROPE)
    cos_b, ss_b = _rope_tables(C, T, GQA_HEAD_DIM)
    lcos_a, lss_a, lcos_b, lss_b = cos_a[C:], ss_a[C:], cos_b[C:], ss_b[C:]

    z_ctx = _norm_mod_fwd(ctx2d, norm1_g, csh, csc, "norm1_ctx_fwd", tr)
    z_lat = _norm_mod_fwd(x2d, norm1_g, sh1, sc1, "norm1_lat_fwd", tr)
    z_all = jnp.concatenate([z_ctx, z_lat], axis=0)
    (win_t,) = _gather_finish(ag_in, z_all, "in")
    kv_cols = KVP - LANES + MLA_ROPE
    e_kpe = MLA_KV_LORA + MLA_ROPE
    w_kvp = jnp.concatenate([win_t[:MLA_KV_LORA], win_t[e_kpe:kv_cols], win_t[MLA_KV_LORA:e_kpe], jnp.zeros((LANES - MLA_ROPE, D), BF16)], axis=0)
    w_q = win_t[kv_cols : kv_cols + QC]
    w_g = win_t[kv_cols + QC :]
    w_inp = jnp.concatenate([w_kvp, w_q, w_g], axis=0)

    pkv = _mm(z_all, w_kvp, "NT", F32, "proj_kv")
    pq = _mm(z_lat, w_q, "NT", F32, "proj_q")
    pg = _mm(z_lat, w_g, "NT", F32, "proj_g")
    wq_t, wkv_t, wbra_t, wbrb_t, wout = _gather_finish(ag_mix, pg, "mix")
    ckv_n, kb2, vb2, kpe2 = _kprep_fwd(pkv, mla_kv_norm_g, gqa_k_norm_g, cos_a, ss_a, cos_b, ss_b, tr)
    kv_up = _mm(ckv_n, wkv_t, "NT", BF16, "kv_up")
    cq_n, qb2 = _qprep_fwd(pq, mla_q_norm_g, gqa_q_norm_g, lcos_b, lss_b, tr)
    q_a = _mm(cq_n, wq_t, "NT", F32, "q_up")
    qar = _qrope_fwd(q_a, lcos_a, lss_a, tr)

    a_q = [(qar, lambda h: 3 * (h // 2) + h % 2), (qar, lambda h: 3 * (h // 2) + 2)]
    a_k = [(kv_up, lambda h: 2 * h), (kpe2, lambda h: h % 2)]
    a_v = (kv_up, lambda h: 2 * h + 1)
    a_scale = float(MLA_NOPE + MLA_ROPE) ** -0.5
    b_q = [(qb2, lambda h: h)]
    b_k = [(kb2, lambda h: h)]
    b_v = (vb2, lambda h: h)
    b_scale = float(GQA_HEAD_DIM) ** -0.5
    o_a, lse_a = _attn_fwd(a_q, a_k, a_v, MLA_HEADS, 1, MLA_V, a_scale, "attn_a_fwd", tq)
    o_b, lse_b = _attn_fwd(b_q, b_k, b_v, GQA_HEADS, GQA_GROUP, GQA_HEAD_DIM, b_scale, "attn_b_fwd", tq)
    ya = _mm(o_a, wbra_t, "NT", F32, "br_a")
    yb = _mm(o_b, wbrb_t, "NT", F32, "br_b")
    merged = _gates_fwd(pg, ya, yb, tr)
    att = _mm(merged, wout, "NN", F32, "out_proj")
    x1, z2 = _resid_norm2_fwd(x2d, att, g1, norm2_g, sh2, sc2, tr)
    wup_t, wdown = _gather_finish(ag_ffn, o_b, "ffn")
    u = _mm(z2, wup_t, "NT", BF16, "ffn_up")
    tc = _pick(FF, 128)
    hg = _conv_fwd(u, cw, conv_b, tc)
    f = _mm(hg, wdown, "NN", F32, "ffn_down")
    sq, dx2, d_fg, d_g2, df = _loss_head(x1, f, g2, fg, tgt, tr)
    loss = lax.psum(0.5 * jnp.sum(sq) / D, ("x", "y", "c"))

    dhg = _mm(df, wdown, "NT", BF16, "ffn_down_dx")
    g_wdown = _mm(hg, df, "TN", BF16, "ffn_down_dw")
    du_a, du_b, dcw_a, dcw_b, dcb_a, dcb_b = _conv_bwd(u, dhg, cw, conv_b, tc)
    du = jnp.concatenate([du_a, du_b], axis=1)
    dz2 = _mm(du, wup_t, "NN", F32, "ffn_up_dx")
    g_wup_t = _mm(du, z2, "TN", BF16, "ffn_up_dw")
    rs_ffn = _scatter_start([g_wdown, g_wup_t], "ffn")
    sc2 = sc2 + rs_ffn[4]
    dx1, datt, d_n2g, d_sh2, d_sc2, d_g1 = _resid_norm2_bwd(dz2, x1, dx2, att, norm2_g, sc2, g1, tr)

    dmerged = _mm(datt, wout, "NT", F32, "out_proj_dx")
    g_wout = _mm(merged, datt, "TN", BF16, "out_proj_dw")
    dya, dyb, dpg = _gates_bwd(dmerged, pg, ya, yb, tr)
    do_a = _mm(dya, wbra_t, "NN", BF16, "br_a_dx")
    g_wbra_t = _mm(dya, o_a, "TN", BF16, "br_a_dw")
    do_b = _mm(dyb, wbrb_t, "NN", BF16, "br_b_dx")
    g_wbrb_t = _mm(dyb, o_b, "TN", BF16, "br_b_dw")
    dqa2, dka2, dva2 = _attn_bwd(a_q, a_k, a_v, o_a, do_a, lse_a, MLA_HEADS, 1, MLA_V, a_scale, "attn_a_bwd", tq)
    dqb2, dkb2, dvb2 = _attn_bwd(b_q, b_k, b_v, o_b, do_b, lse_b, GQA_HEADS, GQA_GROUP, GQA_HEAD_DIM, b_scale, "attn_b_bwd", tq)
    dq_a = _qrope_bwd(dqa2, lcos_a, lss_a, tr)
    dcq_n = _mm(dq_a, wq_t, "NN", F32, "q_up_dx")
    g_wq_t = _mm(dq_a, cq_n, "TN", BF16, "q_up_dw")
    dpq, d_qg, d_gq = _qprep_bwd(pq, dcq_n, dqb2, mla_q_norm_g, gqa_q_norm_g, lcos_b, lss_b, tr)
    dkv_up, dkpe = _kgrad_split(dka2, dva2, cos_a, ss_a, tr)
    dckv_n = _mm(dkv_up, wkv_t, "NN", F32, "kv_up_dx")
    g_wkv_t = _mm(dkv_up, ckv_n, "TN", BF16, "kv_up_dw")
    rs_mix = _scatter_start([g_wq_t, g_wkv_t, g_wbra_t, g_wbrb_t, g_wout], "mix")
    dpkv, d_kvg, d_kg = _kprep_bwd(pkv, dckv_n, dkb2, dvb2, dkpe, mla_kv_norm_g + rs_mix[4], gqa_k_norm_g, cos_b, ss_b, tr)
    dproj = jnp.concatenate([dpkv, jnp.concatenate([jnp.zeros((C, QC + 2 * D), BF16), jnp.concatenate([dpq, dpg], axis=1)], axis=0)], axis=1)
    dz_all = _mm(dproj, w_inp, "NN", F32, "proj_dx")
    g_winp = _mm(dproj, z_all, "TN", BF16, "proj_dw")
    nk = MLA_KV_LORA + 2 * GQA_KV_HEADS * GQA_HEAD_DIM
    g_win_t = jnp.concatenate([g_winp[:MLA_KV_LORA], g_winp[nk : nk + MLA_ROPE], g_winp[MLA_KV_LORA:nk], g_winp[KVP:]], axis=0)
    rs_in = _scatter_start([g_win_t], "in")
    csc, sc1 = csc + rs_in[4], sc1 + rs_in[4]
    _, d_n1g_c, d_csh, d_csc = _norm_mod_bwd(dz_all, 0, ctx2d, norm1_g, csc, None, "norm1_ctx_bwd", tr)
    grad_x, d_n1g_l, d_sh1, d_sc1 = _norm_mod_bwd(dz_all, C // tr, x2d, norm1_g, sc1, dx1, "norm1_lat_bwd", tr)

    zeros_d = jnp.zeros((1, D), F32)
    d_lat = jnp.concatenate([d_sh1, d_sc1, d_g1, d_sh2, d_sc2, d_g2], axis=1)
    d_ctx_part = jnp.concatenate([d_csh, d_csc], axis=1)
    flat = jnp.concatenate(
        [d_n1g_c + d_n1g_l, d_qg, d_kvg, d_gq, d_kg, d_n2g, dcb_a, dcb_b, d_fg,
         dcw_a.reshape(1, -1), dcw_b.reshape(1, -1), d_ctx_part, d_lat], axis=1)
    n_flat = flat.shape[1]
    n_rows = -(-n_flat // (8 * LANES)) * 8
    flat = jnp.pad(flat, ((0, 0), (0, n_rows * LANES - n_flat))).reshape(n_rows, LANES)
    got = _all_gather_small(flat, "gather_small_grads")
    tot = _sum_slots(got, "sum_small_grads").reshape(1, -1)
    sizes = [D, MLA_Q_LORA, MLA_KV_LORA, GQA_HEAD_DIM, GQA_HEAD_DIM, D, F2, D, 3 * FF, 3 * FF, 2 * D]
    offs = [0]
    for s in sizes:
        offs.append(offs[-1] + s)
    t_n1g, t_qg, t_kvg, t_gq, t_kg, t_n2g, t_cb, t_fg, t_cwa, t_cwb, t_ctx = [tot[:, offs[k] : offs[k + 1]] for k in range(len(sizes))]
    g_cw_full = jnp.concatenate([t_cwa.reshape(3, FF), t_cwb.reshape(3, FF)], axis=1)
    g_cw = lax.dynamic_slice(g_cw_full, (0, j * NW), (3, NW))
    d_lat_all = got.reshape(8, -1)[:, offs[-1] : offs[-1] + 6 * D]
    g16 = jnp.concatenate([d_lat_all, jnp.pad(t_ctx, ((0, 0), (0, 4 * D))), jnp.zeros((7, 6 * D), F32)], axis=0)
    g_b_ada = _sum_slots(g16.reshape(16, 1, 6 * D), "sum_b_ada")
    g16_cols = lax.dynamic_slice(g16, (0, j * NA), (16, NA))
    g_w_ada = _mm(s16, g16_cols, "TN", F32, "ada_dw", act="silu")
    ds_part = _mm(g16_cols, w_ada[0], "NT", F32, "ada_dx")
    got = _all_gather_small(ds_part[8:16], "gather_ada_dx")
    ds_ctx = _sum_slots(jnp.stack([got[2 * s] for s in range(4)]), "sum_ada_dx")[0:1]
    g_c_ctx = _silu_grad_mul(ds_ctx, cc)

    r_wdown, r_wup = _scatter_finish(rs_ffn, grad_x, "ffn")
    r_wq, r_wkv, r_wbra, r_wbrb, r_wout = _scatter_finish(rs_mix, r_wup, "mix")
    gq_p = r_wq.T
    gq = jnp.concatenate([gq_p[:, : 2 * MLA_NOPE].reshape(MLA_Q_LORA, 2, MLA_NOPE), gq_p[:, 2 * MLA_NOPE :].reshape(MLA_Q_LORA, 2, MLA_ROPE)], axis=2)
    grads = {
        "c_ctx": g_c_ctx.reshape(D), "w_ada": g_w_ada[None], "b_ada": g_b_ada, "norm1_g": t_n1g,
        "mla_q_norm_g": t_qg, "w_q_up": gq.reshape(1, MLA_Q_LORA, -1), "mla_kv_norm_g": t_kvg, "w_kv_up": r_wkv,
        "gqa_q_norm_g": t_gq, "gqa_k_norm_g": t_kg, "w_br_a": r_wbra, "w_br_b": r_wbrb, "w_out": r_wout[None],
        "norm2_g": t_n2g, "w_up": r_wup, "conv_w": g_cw[None], "conv_b": t_cb, "w_down": r_wdown[None],
        "final_norm_g": t_fg.reshape(D),
    }
    arrives_transposed = ("w_kv_up", "w_br_a", "w_br_b", "w_up")
    weights = dict(c_ctx=c_ctx, w_ada=w_ada, b_ada=b_ada, norm1_g=norm1_g, w_in=w_in, mla_q_norm_g=mla_q_norm_g, w_q_up=w_q_up,
                   mla_kv_norm_g=mla_kv_norm_g, w_kv_up=w_kv_up, gqa_q_norm_g=gqa_q_norm_g, gqa_k_norm_g=gqa_k_norm_g, w_br_a=w_br_a,
                   w_br_b=w_br_b, w_out=w_out, norm2_g=norm2_g, w_up=w_up, conv_w=conv_w, conv_b=conv_b, w_down=w_down,
                   final_norm_g=final_norm_g)
    m_in = dict(c_ctx=m_c_ctx, w_ada=m_w_ada, b_ada=m_b_ada, norm1_g=m_norm1_g, w_in=m_w_in, mla_q_norm_g=m_mla_q_norm_g,
                w_q_up=m_w_q_up, mla_kv_norm_g=m_mla_kv_norm_g, w_kv_up=m_w_kv_up, gqa_q_norm_g=m_gqa_q_norm_g,
                gqa_k_norm_g=m_gqa_k_norm_g, w_br_a=m_w_br_a, w_br_b=m_w_br_b, w_out=m_w_out, norm2_g=m_norm2_g, w_up=m_w_up,
                conv_w=m_conv_w, conv_b=m_conv_b, w_down=m_w_down, final_norm_g=m_final_norm_g)
    v_in = dict(c_ctx=v_c_ctx, w_ada=v_w_ada, b_ada=v_b_ada, norm1_g=v_norm1_g, w_in=v_w_in, mla_q_norm_g=v_mla_q_norm_g,
                w_q_up=v_w_q_up, mla_kv_norm_g=v_mla_kv_norm_g, w_kv_up=v_w_kv_up, gqa_q_norm_g=v_gqa_q_norm_g,
                gqa_k_norm_g=v_gqa_k_norm_g, w_br_a=v_w_br_a, w_br_b=v_w_br_b, w_out=v_w_out, norm2_g=v_norm2_g, w_up=v_w_up,
                conv_w=v_conv_w, conv_b=v_conv_b, w_down=v_w_down, final_norm_g=v_final_norm_g)
    names = list(weights)
    big = [n for n in names if weights[n].ndim == 3 and weights[n].shape[1] >= 8]
    small = [n for n in names if n not in big]
    delta, new_m, new_v = {}, {}, {}

    def update(n):
        shp = weights[n].shape
        two_d = lambda a: a.reshape(shp[1], shp[2])
        g_t = n in arrives_transposed
        g_in = grads[n] if g_t else two_d(grads[n].astype(F32))
        g_, d_, m_, v_ = _adamw(two_d(weights[n]), g_in, two_d(m_in[n]), two_d(v_in[n]), "adamw_" + n, g_transposed=g_t)
        grads[n], delta[n], new_m[n], new_v[n] = g_.reshape(shp), d_.reshape(shp), m_.reshape(shp), v_.reshape(shp)

    early = [n for n in big if n != "w_in"]
    for n in early:
        update(n)
    done = sum(delta[n][0, 0:1, 0:1] for n in early)
    (r_win,) = _scatter_finish(rs_in, done, "in")
    _, d_, m_, v_ = _adamw(w_in[0].T, r_win, m_w_in[0].T, v_w_in[0].T, "adamw_w_in")
    grads["w_in"], delta["w_in"], new_m["w_in"], new_v["w_in"] = r_win.T[None], d_.T[None], m_.T[None], v_.T[None]
    grads = {n: grads[n].reshape(weights[n].shape).astype(F32) for n in names}

    def pack(tree):
        flat_ = jnp.concatenate([tree[n].reshape(-1) for n in small])
        rows = -(-flat_.shape[0] // (8 * LANES)) * 8
        return jnp.pad(flat_, (0, rows * LANES - flat_.shape[0])).reshape(rows, LANES)

    _, d_, m_, v_ = _adamw(pack(weights), pack(grads), pack(m_in), pack(v_in), "adamw_small")
    off = 0
    for n in small:
        size = weights[n].size
        shp = weights[n].shape
        delta[n] = d_.reshape(-1)[off : off + size].reshape(shp)
        new_m[n] = m_.reshape(-1)[off : off + size].reshape(shp)
        new_v[n] = v_.reshape(-1)[off : off + size].reshape(shp)
        off += size

    return (loss, grad_x[None], *[grads[n] for n in names], *[delta[n] for n in names], *[new_m[n] for n in names],
            *[new_v[n] for n in names])
```

```python
import math

import jax
import jax.numpy as jnp
from jax import lax
from jax.experimental import pallas as pl
from jax.experimental.pallas import tpu as pltpu

F32 = jnp.float32
BF16 = jnp.bfloat16
MESH = pl.DeviceIdType.MESH

NORM_EPS = 1e-6
ROPE_THETA = 10000.0
GRID_W = 64
MLA_HEADS = 8
MLA_Q_LORA = 768
MLA_KV_LORA = 512
MLA_NOPE = 128
MLA_ROPE = 64
MLA_V = 128
GQA_HEADS = 8
GQA_KV_HEADS = 2
GQA_HEAD_DIM = 128
GQA_GROUP = GQA_HEADS // GQA_KV_HEADS
LANES = 128
KVP = MLA_KV_LORA + 2 * GQA_KV_HEADS * GQA_HEAD_DIM + LANES
QC = MLA_Q_LORA + GQA_HEADS * GQA_HEAD_DIM

ADAM_LR = 0.001
ADAM_B1 = 0.9
ADAM_B2 = 0.999
ADAM_EPS = 1e-08
ADAM_WD = 0.01
ADAM_STEP = 10

VMEM_LIMIT = 56 * 1024 * 1024


def _pick(dim, target, mult=LANES):
    t = (min(target, dim) // mult) * mult
    while t >= mult:
        if dim % t == 0:
            return t
        t -= mult
    return dim


def _params(sem):
    return pltpu.CompilerParams(dimension_semantics=sem, vmem_limit_bytes=VMEM_LIMIT)


_DIMS = {"NN": (((1,), (0,)), ((), ())), "NT": (((1,), (1,)), ((), ())), "TN": (((0,), (0,)), ((), ()))}


MM_VMEM_BUDGET = 36 * 1024 * 1024


def _mm_tiles(M, N, K, sa, sb, so, tm, tn, tk):
    tm, tn, tk = _pick(M, tm), _pick(N, tn), _pick(K, tk)

    def need(t):
        return 2 * (tm * t * sa + t * tn * sb) + 2 * tm * tn * so + (tm * tn * 4 if t < K else 0)

    while need(tk) > MM_VMEM_BUDGET and tk > LANES:
        smaller = _pick(K, tk - LANES)
        if smaller >= tk:
            break
        tk = smaller
    return tm, tn, tk


def _window(block, index, offsets):
    if not any(offsets):
        return pl.BlockSpec(block, index)
    for t, o in zip(block, offsets):
        assert o % 16 == 0 and t % 16 == 0, (block, offsets)

    def at(i, j, k):
        return tuple(pl.multiple_of(o + p * t, math.gcd(o, t)) for p, t, o in zip(index(i, j, k), block, offsets))

    return pl.BlockSpec(tuple(pl.Element(t) for t in block), at)


def _mm(a, b, mode, out_dtype, name, m=None, n=None, k=None, b_off=0, add=None, out_rows=None, out_base=None, out_off=0,
        tm=1024, tn=1024, tk=2304, act=None, bias=None):
    if mode == "NN":
        M, K, N = m or a.shape[0], k or a.shape[1], b.shape[1]
    elif mode == "NT":
        M, K, N = m or a.shape[0], a.shape[1], n or b.shape[0]
    else:
        M, K, N = a.shape[1], k or a.shape[0], b.shape[1]
    tm, tn, tk = _mm_tiles(M, N, K, a.dtype.itemsize, b.dtype.itemsize, jnp.dtype(out_dtype).itemsize, tm, tn, tk)
    nk = K // tk
    dims = _DIMS[mode]
    n_in = 2 + (bias is not None) + (add is not None) + (out_base is not None)

    def body(*refs):
        a_ref, b_ref = refs[:2]
        bias_ref = refs[2] if bias is not None else None
        add_ref = refs[2 + (bias is not None)] if add is not None else None
        o_ref = refs[n_in]
        av = a_ref[...]
        if act == "silu":
            av = av * jax.nn.sigmoid(av)
        part = lax.dot_general(av.astype(BF16), b_ref[...].astype(BF16), dims, preferred_element_type=F32)

        def finish(r):
            if bias is not None:
                r = r + bias_ref[...]
            if add is not None:
                r = r + add_ref[...]
            o_ref[...] = r.astype(out_dtype)

        if nk == 1:
            finish(part)
            return
        acc = refs[-1]
        k = pl.program_id(2)

        @pl.when(k == 0)
        def _():
            acc[...] = part

        @pl.when(jnp.logical_and(k > 0, k < nk - 1))
        def _():
            acc[...] += part

        @pl.when(k == nk - 1)
        def _():
            finish(acc[...] + part)

    a_spec = pl.BlockSpec((tk, tm), lambda i, j, k: (k, i)) if mode == "TN" else pl.BlockSpec((tm, tk), lambda i, j, k: (i, k))
    if mode == "NT":
        b_spec = _window((tn, tk), lambda i, j, k: (j, k), (b_off, 0))
    else:
        b_spec = _window((tk, tn), lambda i, j, k: (k, j), (b_off, 0))
    in_specs, args = [a_spec, b_spec], [a, b]
    if bias is not None:
        in_specs.append(pl.BlockSpec((1, tn), lambda i, j, k: (0, j)))
        args.append(bias)
    if add is not None:
        in_specs.append(pl.BlockSpec((tm, tn), lambda i, j, k: (i, j)))
        args.append(add)
    aliases = {}
    if out_base is not None:
        aliases = {len(args): 0}
        in_specs.append(ANY)
        args.append(out_base)
        out_rows = out_base.shape[0]
    return pl.pallas_call(
        body,
        name=name,
        grid=(M // tm, N // tn, nk),
        in_specs=in_specs,
        out_specs=_window((tm, tn), lambda i, j, k: (i, j), (out_off, 0)),
        out_shape=jax.ShapeDtypeStruct((out_rows or M, N), out_dtype),
        input_output_aliases=aliases,
        scratch_shapes=[pltpu.VMEM((tm, tn), F32)] if nk > 1 else [],
        compiler_params=_params(("parallel", "parallel", "arbitrary")),
    )(*args)


def _rms(x):
    r = lax.rsqrt(jnp.mean(x * x, axis=-1, keepdims=True) + NORM_EPS)
    return x * r, r


def _rms_bwd(xh, r, dxh):
    return r * (dxh - xh * jnp.mean(dxh * xh, axis=-1, keepdims=True))


def _swap(x, q):
    lane = lax.broadcasted_iota(jnp.int32, x.shape, 1)
    even = ((lane // q) % 2) == 0
    return jnp.where(even, pltpu.roll(x, LANES - q, 1), pltpu.roll(x, q, 1))


def _rope(x, cos, ss, q):
    return x * cos + _swap(x, q) * ss


def _rope_t(d, cos, ss, q):
    return d * cos + _swap(d * ss, q)


def _csum(x):
    return jnp.sum(x, axis=0, keepdims=True)


def _rows(tr, w, off=0):
    return pl.BlockSpec((tr, w), lambda i: (i + off, 0))


def _bcast(w):
    return pl.BlockSpec((1, w), lambda i: (0, 0))


def _acc_init(i, refs):
    @pl.when(i == 0)
    def _():
        for r in refs:
            r[...] = jnp.zeros_like(r)


def _rope_tables(n_ctx, n_lat, rot_dim):
    rows = n_lat // GRID_W
    row = jnp.repeat(jnp.arange(rows, dtype=F32), GRID_W)
    col = jnp.tile(jnp.arange(GRID_W, dtype=F32), rows)
    half = rot_dim // 2
    inv_freq = ROPE_THETA ** (-jnp.arange(0, half, 2, dtype=F32) / half)
    ar, ac = row[:, None] * inv_freq, col[:, None] * inv_freq
    cos = jnp.concatenate([jnp.cos(ar), jnp.cos(ar), jnp.cos(ac), jnp.cos(ac)], axis=-1)
    ss = jnp.concatenate([-jnp.sin(ar), jnp.sin(ar), -jnp.sin(ac), jnp.sin(ac)], axis=-1)
    cos = jnp.tile(cos, (1, LANES // rot_dim))
    ss = jnp.tile(ss, (1, LANES // rot_dim))
    cos = jnp.concatenate([cos, jnp.ones((n_ctx, LANES), F32)], axis=0)
    ss = jnp.concatenate([ss, jnp.zeros((n_ctx, LANES), F32)], axis=0)
    return cos, ss


def _norm_mod_fwd(x2d, g, sh, sc, name, tr, out_rows=None, base=None, out_off=0):
    n, d = x2d.shape

    def body(x_ref, g_ref, sh_ref, sc_ref, *rest):
        xh, _ = _rms(x_ref[...])
        rest[-1][...] = ((xh * g_ref[...]) * (1.0 + sc_ref[...]) + sh_ref[...]).astype(BF16)

    args, in_specs, aliases = [x2d, g, sh, sc], [_rows(tr, d), _bcast(d), _bcast(d), _bcast(d)], {}
    if base is not None:
        args.append(base)
        in_specs.append(ANY)
        aliases = {4: 0}
        out_rows = base.shape[0]
    return pl.pallas_call(
        body,
        name=name,
        grid=(n // tr,),
        in_specs=in_specs,
        out_specs=_rows(tr, d, out_off // tr),
        out_shape=jax.ShapeDtypeStruct((out_rows or n, d), BF16),
        input_output_aliases=aliases,
        compiler_params=_params(("parallel",)),
    )(*args)


def _norm_mod_bwd(dz, dz_off, x2d, g, sc, dres, name, tr):
    n, d = x2d.shape
    want_dx = dres is not None

    def body(*refs):
        if want_dx:
            dz_ref, x_ref, g_ref, sc_ref, dres_ref, dx_ref, dg_ref, dsh_ref, dsc_ref = refs
        else:
            dz_ref, x_ref, g_ref, sc_ref, dg_ref, dsh_ref, dsc_ref = refs
        _acc_init(pl.program_id(0), [dg_ref, dsh_ref, dsc_ref])
        xh, r = _rms(x_ref[...])
        dzv = dz_ref[...]
        gv = g_ref[...]
        dsc_ref[...] += _csum(dzv * (xh * gv))
        dsh_ref[...] += _csum(dzv)
        dh = dzv * (1.0 + sc_ref[...])
        dg_ref[...] += _csum(dh * xh)
        if want_dx:
            dx_ref[...] = _rms_bwd(xh, r, dh * gv) + dres_ref[...]

    in_specs = [_rows(tr, d, dz_off), _rows(tr, d), _bcast(d), _bcast(d)]
    args = [dz, x2d, g, sc]
    out_specs = [_bcast(d)] * 3
    out_shape = [jax.ShapeDtypeStruct((1, d), F32)] * 3
    if want_dx:
        in_specs.append(_rows(tr, d))
        args.append(dres)
        out_specs = [_rows(tr, d)] + out_specs
        out_shape = [jax.ShapeDtypeStruct((n, d), F32)] + out_shape
    res = pl.pallas_call(
        body,
        name=name,
        grid=(n // tr,),
        in_specs=in_specs,
        out_specs=out_specs,
        out_shape=out_shape,
        compiler_params=_params(("arbitrary",)),
    )(*args)
    return res if want_dx else (None, *res)


_QA, _QB = MLA_ROPE // 4, GQA_HEAD_DIM // 4


def _kprep_fwd(pkv, kvg, kg, cos_a, ss_a, cos_b, ss_b, tr):
    n = pkv.shape[0]
    nb = GQA_KV_HEADS * GQA_HEAD_DIM

    def body(p_ref, kvg_ref, kg_ref, ca, sa, cb, sb, ckv_ref, kb_ref, vb_ref, kpe_ref):
        p = p_ref[...]
        xh, _ = _rms(p[:, :MLA_KV_LORA])
        ckv_ref[...] = (xh * kvg_ref[...]).astype(BF16)
        for e in range(GQA_KV_HEADS):
            lo = MLA_KV_LORA + e * GQA_HEAD_DIM
            kh, _ = _rms(p[:, lo : lo + GQA_HEAD_DIM])
            kb_ref[:, e * GQA_HEAD_DIM : (e + 1) * GQA_HEAD_DIM] = _rope(kh * kg_ref[...], cb[...], sb[...], _QB).astype(BF16)
        vb_ref[...] = p[:, MLA_KV_LORA + nb : MLA_KV_LORA + 2 * nb].astype(BF16)
        kr = _rope(p[:, MLA_KV_LORA + 2 * nb :], ca[...], sa[...], _QA)
        kpe_ref[:, :LANES] = kr.astype(BF16)
        kpe_ref[:, LANES:] = pltpu.roll(kr, MLA_ROPE, 1).astype(BF16)

    return pl.pallas_call(
        body,
        name="kprep_fwd",
        grid=(n // tr,),
        in_specs=[_rows(tr, KVP), _bcast(MLA_KV_LORA), _bcast(GQA_HEAD_DIM)] + [_rows(tr, LANES)] * 4,
        out_specs=[_rows(tr, MLA_KV_LORA), _rows(tr, nb), _rows(tr, nb), _rows(tr, 2 * LANES)],
        out_shape=[jax.ShapeDtypeStruct((n, w), BF16) for w in (MLA_KV_LORA, nb, nb, 2 * LANES)],
        compiler_params=_params(("parallel",)),
    )(pkv, kvg, kg, cos_a, ss_a, cos_b, ss_b)


def _kprep_bwd(pkv, dckv, dkb, dvb, dkpe, kvg, kg, cos_b, ss_b, tr):
    n = pkv.shape[0]
    nb = GQA_KV_HEADS * GQA_HEAD_DIM

    def body(p_ref, dckv_ref, dkb_ref, dvb_ref, dkpe_ref, kvg_ref, kg_ref, cb, sb, dp_ref, dkvg_ref, dkg_ref):
        _acc_init(pl.program_id(0), [dkvg_ref, dkg_ref])
        p = p_ref[...]
        xh, r = _rms(p[:, :MLA_KV_LORA])
        dn = dckv_ref[...]
        dkvg_ref[...] += _csum(dn * xh)
        dp_ref[:, :MLA_KV_LORA] = _rms_bwd(xh, r, dn * kvg_ref[...]).astype(BF16)
        for e in range(GQA_KV_HEADS):
            lo = MLA_KV_LORA + e * GQA_HEAD_DIM
            kh, rk = _rms(p[:, lo : lo + GQA_HEAD_DIM])
            dk = _rope_t(dkb_ref[:, e * GQA_HEAD_DIM : (e + 1) * GQA_HEAD_DIM], cb[...], sb[...], _QB)
            dkg_ref[...] += _csum(dk * kh)
            dp_ref[:, lo : lo + GQA_HEAD_DIM] = _rms_bwd(kh, rk, dk * kg_ref[...]).astype(BF16)
        dp_ref[:, MLA_KV_LORA + nb : MLA_KV_LORA + 2 * nb] = dvb_ref[...].astype(BF16)
        dp_ref[:, MLA_KV_LORA + 2 * nb :] = dkpe_ref[...].astype(BF16)

    return pl.pallas_call(
        body,
        name="kprep_bwd",
        grid=(n // tr,),
        in_specs=[_rows(tr, KVP), _rows(tr, MLA_KV_LORA), _rows(tr, nb), _rows(tr, nb), _rows(tr, LANES),
                  _bcast(MLA_KV_LORA), _bcast(GQA_HEAD_DIM), _rows(tr, LANES), _rows(tr, LANES)],
        out_specs=[_rows(tr, KVP), _bcast(MLA_KV_LORA), _bcast(GQA_HEAD_DIM)],
        out_shape=[jax.ShapeDtypeStruct((n, KVP), BF16), jax.ShapeDtypeStruct((1, MLA_KV_LORA), F32),
                   jax.ShapeDtypeStruct((1, GQA_HEAD_DIM), F32)],
        compiler_params=_params(("arbitrary",)),
    )(pkv, dckv, dkb, dvb, dkpe, kvg, kg, cos_b, ss_b)


def _kgrad_split(dka, dva, cos_a, ss_a, tr):
    n = dka.shape[0]
    wk = MLA_HEADS * 2 * LANES

    def body(dk_ref, dv_ref, ca, sa, dkv_ref, dkpe_ref):
        even = jnp.zeros((tr, LANES), F32)
        odd = jnp.zeros((tr, LANES), F32)
        for h in range(MLA_HEADS):
            dkv_ref[:, 2 * h * LANES : (2 * h + 1) * LANES] = dk_ref[:, 2 * h * LANES : (2 * h + 1) * LANES].astype(BF16)
            dkv_ref[:, (2 * h + 1) * LANES : (2 * h + 2) * LANES] = dv_ref[:, h * MLA_V : (h + 1) * MLA_V].astype(BF16)
            part = dk_ref[:, (2 * h + 1) * LANES : (2 * h + 2) * LANES]
            if h % 2 == 0:
                even = even + part
            else:
                odd = odd + part
        lane = lax.broadcasted_iota(jnp.int32, (tr, LANES), 1)
        low = lane < MLA_ROPE
        both = jnp.where(low, even, odd)
        tot = jnp.where(low, both + pltpu.roll(both, MLA_ROPE, 1), 0.0)
        dkpe_ref[...] = _rope_t(tot, ca[...], sa[...], _QA)

    return pl.pallas_call(
        body,
        name="kgrad_split",
        grid=(n // tr,),
        in_specs=[_rows(tr, wk), _rows(tr, MLA_HEADS * MLA_V), _rows(tr, LANES), _rows(tr, LANES)],
        out_specs=[_rows(tr, wk), _rows(tr, LANES)],
        out_shape=[jax.ShapeDtypeStruct((n, wk), BF16), jax.ShapeDtypeStruct((n, LANES), F32)],
        compiler_params=_params(("parallel",)),
    )(dka, dva, cos_a, ss_a)


def _qprep_fwd(pq, qg, gq, cos_b, ss_b, tr):
    n = pq.shape[0]
    nq = GQA_HEADS * GQA_HEAD_DIM

    def body(p_ref, qg_ref, gq_ref, cb, sb, cq_ref, qb_ref):
        xh, _ = _rms(p_ref[:, :MLA_Q_LORA])
        cq_ref[...] = (xh * qg_ref[...]).astype(BF16)
        for h in range(GQA_HEADS):
            lo = MLA_Q_LORA + h * GQA_HEAD_DIM
            qh, _ = _rms(p_ref[:, lo : lo + GQA_HEAD_DIM])
            qb_ref[:, h * GQA_HEAD_DIM : (h + 1) * GQA_HEAD_DIM] = _rope(qh * gq_ref[...], cb[...], sb[...], _QB).astype(BF16)

    return pl.pallas_call(
        body,
        name="qprep_fwd",
        grid=(n // tr,),
        in_specs=[_rows(tr, QC), _bcast(MLA_Q_LORA), _bcast(GQA_HEAD_DIM), _rows(tr, LANES), _rows(tr, LANES)],
        out_specs=[_rows(tr, MLA_Q_LORA), _rows(tr, nq)],
        out_shape=[jax.ShapeDtypeStruct((n, MLA_Q_LORA), BF16), jax.ShapeDtypeStruct((n, nq), BF16)],
        compiler_params=_params(("parallel",)),
    )(pq, qg, gq, cos_b, ss_b)


def _qprep_bwd(pq, dcq, dqb, qg, gq, cos_b, ss_b, tr):
    n = pq.shape[0]
    nq = GQA_HEADS * GQA_HEAD_DIM

    def body(p_ref, dcq_ref, dqb_ref, qg_ref, gq_ref, cb, sb, dp_ref, dqg_ref, dgq_ref):
        _acc_init(pl.program_id(0), [dqg_ref, dgq_ref])
        xh, r = _rms(p_ref[:, :MLA_Q_LORA])
        dn = dcq_ref[...]
        dqg_ref[...] += _csum(dn * xh)
        dp_ref[:, :MLA_Q_LORA] = _rms_bwd(xh, r, dn * qg_ref[...]).astype(BF16)
        for h in range(GQA_HEADS):
            lo = MLA_Q_LORA + h * GQA_HEAD_DIM
            qh, rq = _rms(p_ref[:, lo : lo + GQA_HEAD_DIM])
            dq = _rope_t(dqb_ref[:, h * GQA_HEAD_DIM : (h + 1) * GQA_HEAD_DIM], cb[...], sb[...], _QB)
            dgq_ref[...] += _csum(dq * qh)
            dp_ref[:, lo : lo + GQA_HEAD_DIM] = _rms_bwd(qh, rq, dq * gq_ref[...]).astype(BF16)

    return pl.pallas_call(
        body,
        name="qprep_bwd",
        grid=(n // tr,),
        in_specs=[_rows(tr, QC), _rows(tr, MLA_Q_LORA), _rows(tr, nq), _bcast(MLA_Q_LORA), _bcast(GQA_HEAD_DIM),
                  _rows(tr, LANES), _rows(tr, LANES)],
        out_specs=[_rows(tr, QC), _bcast(MLA_Q_LORA), _bcast(GQA_HEAD_DIM)],
        out_shape=[jax.ShapeDtypeStruct((n, QC), BF16), jax.ShapeDtypeStruct((1, MLA_Q_LORA), F32),
                   jax.ShapeDtypeStruct((1, GQA_HEAD_DIM), F32)],
        compiler_params=_params(("arbitrary",)),
    )(pq, dcq, dqb, qg, gq, cos_b, ss_b)


_QA_COLS = MLA_HEADS * (MLA_NOPE + MLA_ROPE)


def _qrope_fwd(qa, cos_a, ss_a, tr):
    n = qa.shape[0]

    def body(q_ref, ca, sa, o_ref):
        for j in range(MLA_HEADS // 2):
            lo = 3 * j * LANES
            o_ref[:, lo : lo + 2 * LANES] = q_ref[:, lo : lo + 2 * LANES].astype(BF16)
            o_ref[:, lo + 2 * LANES : lo + 3 * LANES] = _rope(q_ref[:, lo + 2 * LANES : lo + 3 * LANES], ca[...], sa[...], _QA).astype(BF16)

    return pl.pallas_call(
        body,
        name="qrope_fwd",
        grid=(n // tr,),
        in_specs=[_rows(tr, _QA_COLS), _rows(tr, LANES), _rows(tr, LANES)],
        out_specs=_rows(tr, _QA_COLS),
        out_shape=jax.ShapeDtypeStruct((n, _QA_COLS), BF16),
        compiler_params=_params(("parallel",)),
    )(qa, cos_a, ss_a)


def _qrope_bwd(dq2, cos_a, ss_a, tr):
    n = dq2.shape[0]

    def body(d_ref, ca, sa, o_ref):
        for j in range(MLA_HEADS // 2):
            lo = 3 * j * LANES
            h0, h1 = 2 * j, 2 * j + 1
            o_ref[:, lo : lo + LANES] = d_ref[:, 2 * h0 * LANES : (2 * h0 + 1) * LANES].astype(BF16)
            o_ref[:, lo + LANES : lo + 2 * LANES] = d_ref[:, 2 * h1 * LANES : (2 * h1 + 1) * LANES].astype(BF16)
            pe = d_ref[:, (2 * h0 + 1) * LANES : (2 * h0 + 2) * LANES] + d_ref[:, (2 * h1 + 1) * LANES : (2 * h1 + 2) * LANES]
            o_ref[:, lo + 2 * LANES : lo + 3 * LANES] = _rope_t(pe, ca[...], sa[...], _QA).astype(BF16)

    return pl.pallas_call(
        body,
        name="qrope_bwd",
        grid=(n // tr,),
        in_specs=[_rows(tr, MLA_HEADS * 2 * LANES), _rows(tr, LANES), _rows(tr, LANES)],
        out_specs=_rows(tr, _QA_COLS),
        out_shape=jax.ShapeDtypeStruct((n, _QA_COLS), BF16),
        compiler_params=_params(("parallel",)),
    )(dq2, cos_a, ss_a)


def _cat(refs):
    vals = [r[...] for r in refs]
    return vals[0] if len(vals) == 1 else jnp.concatenate(vals, axis=-1)


def _attn_fwd(qparts, kparts, vpart, n_heads, group, dv, scale, name, tq):
    T, Tk = qparts[0][0].shape[0], kparts[0][0].shape[0]
    nq_, nk_ = len(qparts), len(kparts)

    def body(*refs):
        q = _cat(refs[:nq_])
        k = _cat(refs[nq_ : nq_ + nk_])
        v_ref, o_ref, lse_ref = refs[nq_ + nk_ :]
        s = lax.dot_general(q, k, _DIMS["NT"], preferred_element_type=F32) * scale
        m = jnp.max(s, axis=-1, keepdims=True)
        p = jnp.exp(s - m)
        l = jnp.sum(p, axis=-1, keepdims=True)
        pn = (p * (1.0 / l)).astype(BF16)
        o_ref[...] = jnp.dot(pn, v_ref[...], preferred_element_type=F32).astype(BF16)
        lse_ref[...] = m + jnp.log(l)

    in_specs = [pl.BlockSpec((tq, LANES), lambda h, i, f=f: (i, f(h))) for _, f in qparts]
    in_specs += [pl.BlockSpec((Tk, LANES), lambda h, i, f=f: (0, f(h // group))) for _, f in kparts]
    fv = vpart[1]
    in_specs.append(pl.BlockSpec((Tk, dv), lambda h, i: (0, fv(h // group))))
    return pl.pallas_call(
        body,
        name=name,
        grid=(n_heads, T // tq),
        in_specs=in_specs,
        out_specs=[pl.BlockSpec((tq, dv), lambda h, i: (i, h)), pl.BlockSpec((None, tq, 1), lambda h, i: (h, i, 0))],
        out_shape=[jax.ShapeDtypeStruct((T, n_heads * dv), BF16), jax.ShapeDtypeStruct((n_heads, T, 1), F32)],
        compiler_params=_params(("parallel", "parallel")),
    )(*[a for a, _ in qparts], *[a for a, _ in kparts], vpart[0])


def _attn_bwd(qparts, kparts, vpart, o, do, lse, n_heads, group, dv, scale, name, tq):
    T, Tk = qparts[0][0].shape[0], kparts[0][0].shape[0]
    nq_, nk_ = len(qparts), len(kparts)
    dk_ = LANES * nq_
    n_kv = n_heads // group
    nblk = T // tq

    def head(hk, i):
        return hk * group + i // nblk

    def body(*refs):
        q = _cat(refs[:nq_])
        k = _cat(refs[nq_ : nq_ + nk_])
        v_ref, o_ref, do_ref, lse_ref, dq_ref, dk_ref, dv_ref = refs[nq_ + nk_ :]
        _acc_init(pl.program_id(1), [dk_ref, dv_ref])
        s = lax.dot_general(q, k, _DIMS["NT"], preferred_element_type=F32) * scale
        p = jnp.exp(s - lse_ref[...])
        dov = do_ref[...]
        dp = lax.dot_general(dov, v_ref[...], _DIMS["NT"], preferred_element_type=F32)
        delta = jnp.sum(dov.astype(F32) * o_ref[...].astype(F32), axis=-1, keepdims=True)
        ds = (p * (dp - delta) * scale).astype(BF16)
        dq_ref[...] = jnp.dot(ds, k, preferred_element_type=F32)
        dk_ref[...] += lax.dot_general(ds, q, _DIMS["TN"], preferred_element_type=F32)
        dv_ref[...] += lax.dot_general(p.astype(BF16), dov, _DIMS["TN"], preferred_element_type=F32)

    in_specs = [pl.BlockSpec((tq, LANES), lambda hk, i, f=f: (i % nblk, f(head(hk, i)))) for _, f in qparts]
    in_specs += [pl.BlockSpec((Tk, LANES), lambda hk, i, f=f: (0, f(hk))) for _, f in kparts]
    fv = vpart[1]
    in_specs.append(pl.BlockSpec((Tk, dv), lambda hk, i: (0, fv(hk))))
    in_specs += [pl.BlockSpec((tq, dv), lambda hk, i: (i % nblk, head(hk, i)))] * 2
    in_specs.append(pl.BlockSpec((None, tq, 1), lambda hk, i: (head(hk, i), i % nblk, 0)))
    return pl.pallas_call(
        body,
        name=name,
        grid=(n_kv, group * nblk),
        in_specs=in_specs,
        out_specs=[pl.BlockSpec((tq, dk_), lambda hk, i: (i % nblk, head(hk, i))),
                   pl.BlockSpec((Tk, dk_), lambda hk, i: (0, hk)),
                   pl.BlockSpec((Tk, dv), lambda hk, i: (0, hk))],
        out_shape=[jax.ShapeDtypeStruct((T, n_heads * dk_), F32), jax.ShapeDtypeStruct((Tk, n_kv * dk_), F32),
                   jax.ShapeDtypeStruct((Tk, n_kv * dv), F32)],
        compiler_params=_params(("parallel", "arbitrary")),
    )(*[a for a, _ in qparts], *[a for a, _ in kparts], vpart[0], o, do, lse)


def _gates_fwd(pg, ya, yb, tr):
    n, d = ya.shape

    def body(pg_ref, ya_ref, yb_ref, o_ref):
        ga = jax.nn.sigmoid(pg_ref[:, :d])
        gb = jax.nn.sigmoid(pg_ref[:, d:])
        o_ref[...] = (ga * ya_ref[...] + gb * yb_ref[...]).astype(BF16)

    return pl.pallas_call(
        body,
        name="gates_fwd",
        grid=(n // tr,),
        in_specs=[_rows(tr, 2 * d), _rows(tr, d), _rows(tr, d)],
        out_specs=_rows(tr, d),
        out_shape=jax.ShapeDtypeStruct((n, d), BF16),
        compiler_params=_params(("parallel",)),
    )(pg, ya, yb)


def _gates_bwd(dm, pg, ya, yb, tr):
    n, d = ya.shape

    def body(dm_ref, pg_ref, ya_ref, yb_ref, dya_ref, dyb_ref, dpg_ref):
        dmv = dm_ref[...]
        ga = jax.nn.sigmoid(pg_ref[:, :d])
        gb = jax.nn.sigmoid(pg_ref[:, d:])
        dya_ref[...] = (dmv * ga).astype(BF16)
        dyb_ref[...] = (dmv * gb).astype(BF16)
        dpg_ref[:, :d] = (dmv * ya_ref[...] * ga * (1.0 - ga)).astype(BF16)
        dpg_ref[:, d:] = (dmv * yb_ref[...] * gb * (1.0 - gb)).astype(BF16)

    return pl.pallas_call(
        body,
        name="gates_bwd",
        grid=(n // tr,),
        in_specs=[_rows(tr, d), _rows(tr, 2 * d), _rows(tr, d), _rows(tr, d)],
        out_specs=[_rows(tr, d), _rows(tr, d), _rows(tr, 2 * d)],
        out_shape=[jax.ShapeDtypeStruct((n, d), BF16), jax.ShapeDtypeStruct((n, d), BF16), jax.ShapeDtypeStruct((n, 2 * d), BF16)],
        compiler_params=_params(("parallel",)),
    )(dm, pg, ya, yb)


def _resid_norm2_fwd(x2d, att, g1, n2g, sh2, sc2, tr):
    n, d = x2d.shape

    def body(x_ref, a_ref, g1_ref, g_ref, sh_ref, sc_ref, x1_ref, z_ref):
        x1 = x_ref[...] + g1_ref[...] * a_ref[...]
        x1_ref[...] = x1
        xh, _ = _rms(x1)
        z_ref[...] = ((xh * g_ref[...]) * (1.0 + sc_ref[...]) + sh_ref[...]).astype(BF16)

    return pl.pallas_call(
        body,
        name="resid_norm2_fwd",
        grid=(n // tr,),
        in_specs=[_rows(tr, d), _rows(tr, d)] + [_bcast(d)] * 4,
        out_specs=[_rows(tr, d), _rows(tr, d)],
        out_shape=[jax.ShapeDtypeStruct((n, d), F32), jax.ShapeDtypeStruct((n, d), BF16)],
        compiler_params=_params(("parallel",)),
    )(x2d, att, g1, n2g, sh2, sc2)


def _resid_norm2_bwd(dz2, x1, dx2, att, n2g, sc2, g1, tr):
    n, d = x1.shape

    def body(dz_ref, x1_ref, dx2_ref, a_ref, g_ref, sc_ref, g1_ref, dx1_ref, da_ref, dg_ref, dsh_ref, dsc_ref, dg1_ref):
        _acc_init(pl.program_id(0), [dg_ref, dsh_ref, dsc_ref, dg1_ref])
        xh, r = _rms(x1_ref[...])
        dzv = dz_ref[...]
        gv = g_ref[...]
        dsc_ref[...] += _csum(dzv * (xh * gv))
        dsh_ref[...] += _csum(dzv)
        dh = dzv * (1.0 + sc_ref[...])
        dg_ref[...] += _csum(dh * xh)
        dx1 = _rms_bwd(xh, r, dh * gv) + dx2_ref[...]
        dx1_ref[...] = dx1
        dg1_ref[...] += _csum(dx1 * a_ref[...])
        da_ref[...] = (dx1 * g1_ref[...]).astype(BF16)

    return pl.pallas_call(
        body,
        name="resid_norm2_bwd",
        grid=(n // tr,),
        in_specs=[_rows(tr, d)] * 4 + [_bcast(d)] * 3,
        out_specs=[_rows(tr, d), _rows(tr, d)] + [_bcast(d)] * 4,
        out_shape=[jax.ShapeDtypeStruct((n, d), F32), jax.ShapeDtypeStruct((n, d), BF16)] + [jax.ShapeDtypeStruct((1, d), F32)] * 4,
        compiler_params=_params(("arbitrary",)),
    )(dz2, x1, dx2, att, n2g, sc2, g1)


def _shift_prev(u):
    row = lax.broadcasted_iota(jnp.int32, u.shape, 0)
    return jnp.where(row == 0, 0.0, pltpu.roll(u, 1, 0))


def _shift_next(u):
    n = u.shape[0]
    row = lax.broadcasted_iota(jnp.int32, u.shape, 0)
    return jnp.where(row == n - 1, 0.0, pltpu.roll(u, n - 1, 0))


def _conv3(u, w_ref, b_ref):
    return b_ref[...] + w_ref[0:1, :] * _shift_prev(u) + w_ref[1:2, :] * u + w_ref[2:3, :] * _shift_next(u)


def _conv_fwd(u, cw, cb, tc):
    n, two_f = u.shape
    f = two_f // 2
    nb = f // tc

    def body(ua_ref, ub_ref, wa_ref, wb_ref, ba_ref, bb_ref, h_ref):
        a = _conv3(ua_ref[...].astype(F32), wa_ref, ba_ref)
        b = _conv3(ub_ref[...].astype(F32), wb_ref, bb_ref)
        h_ref[...] = (a * jax.nn.sigmoid(a) * b).astype(BF16)

    col = lambda rows, off: pl.BlockSpec((rows, tc), lambda i: (0, i + off))
    return pl.pallas_call(
        body,
        name="conv_fwd",
        grid=(nb,),
        in_specs=[col(n, 0), col(n, nb), col(3, 0), col(3, nb), col(1, 0), col(1, nb)],
        out_specs=col(n, 0),
        out_shape=jax.ShapeDtypeStruct((n, f), BF16),
        compiler_params=_params(("parallel",)),
    )(u, u, cw, cw, cb, cb)


def _conv_bwd(u, dh, cw, cb, tc):
    n, two_f = u.shape
    f = two_f // 2
    nb = f // tc

    def part(uv, duc, w_ref, du_ref, dw_ref, db_ref):
        db_ref[...] = _csum(duc)
        dw_ref[0:1, :] = _csum(duc * _shift_prev(uv))
        dw_ref[1:2, :] = _csum(duc * uv)
        dw_ref[2:3, :] = _csum(duc * _shift_next(uv))
        du_ref[...] = (w_ref[0:1, :] * _shift_next(duc) + w_ref[1:2, :] * duc + w_ref[2:3, :] * _shift_prev(duc)).astype(BF16)

    def body(ua_ref, ub_ref, dh_ref, wa_ref, wb_ref, ba_ref, bb_ref, dua_ref, dub_ref, dwa_ref, dwb_ref, dba_ref, dbb_ref):
        ua = ua_ref[...].astype(F32)
        ub = ub_ref[...].astype(F32)
        a = _conv3(ua, wa_ref, ba_ref)
        b = _conv3(ub, wb_ref, bb_ref)
        dhv = dh_ref[...].astype(F32)
        sg = jax.nn.sigmoid(a)
        da = dhv * b * (sg * (1.0 + a * (1.0 - sg)))
        db = dhv * (a * sg)
        part(ua, da, wa_ref, dua_ref, dwa_ref, dba_ref)
        part(ub, db, wb_ref, dub_ref, dwb_ref, dbb_ref)

    col = lambda rows, off: pl.BlockSpec((rows, tc), lambda i: (0, i + off))
    return pl.pallas_call(
        body,
        name="conv_bwd",
        grid=(nb,),
        in_specs=[col(n, 0), col(n, nb), col(n, 0), col(3, 0), col(3, nb), col(1, 0), col(1, nb)],
        out_specs=[col(n, 0), col(n, 0), col(3, 0), col(3, 0), col(1, 0), col(1, 0)],
        out_shape=[jax.ShapeDtypeStruct((n, f), BF16)] * 2 + [jax.ShapeDtypeStruct((3, f), F32)] * 2 + [jax.ShapeDtypeStruct((1, f), F32)] * 2,
        compiler_params=_params(("parallel",)),
    )(u, u, dh, cw, cw, cb, cb)


def _loss_head(x1, f, g2, fg, tgt, tr):
    n, d = x1.shape

    def body(x1_ref, f_ref, g2_ref, fg_ref, t_ref, sq_ref, dx2_ref, dfg_ref, dg2_ref, df_ref):
        _acc_init(pl.program_id(0), [sq_ref, dfg_ref, dg2_ref])
        fv = f_ref[...]
        xh, r = _rms(x1_ref[...] + g2_ref[...] * fv)
        err = xh * fg_ref[...] - t_ref[...]
        sq_ref[...] += _csum(err * err)
        dy = err * (1.0 / d)
        dfg_ref[...] += _csum(dy * xh)
        dx2 = _rms_bwd(xh, r, dy * fg_ref[...])
        dx2_ref[...] = dx2
        dg2_ref[...] += _csum(dx2 * fv)
        df_ref[...] = (dx2 * g2_ref[...]).astype(BF16)

    return pl.pallas_call(
        body,
        name="loss_head",
        grid=(n // tr,),
        in_specs=[_rows(tr, d), _rows(tr, d), _bcast(d), _bcast(d), _rows(tr, d)],
        out_specs=[_bcast(d), _rows(tr, d), _bcast(d), _bcast(d), _rows(tr, d)],
        out_shape=[jax.ShapeDtypeStruct((1, d), F32), jax.ShapeDtypeStruct((n, d), F32), jax.ShapeDtypeStruct((1, d), F32),
                   jax.ShapeDtypeStruct((1, d), F32), jax.ShapeDtypeStruct((n, d), BF16)],
        compiler_params=_params(("arbitrary",)),
    )(x1, f, g2, fg, tgt)


def _sum_slots(g, name):
    s, r, w = g.shape

    def body(g_ref, o_ref):
        acc = g_ref[0]
        for k in range(1, s):
            acc = acc + g_ref[k]
        o_ref[...] = acc

    return pl.pallas_call(body, name=name, out_shape=jax.ShapeDtypeStruct((r, w), F32))(g)


def _silu_grad_mul(ds, cvec):
    def body(d_ref, c_ref, o_ref):
        cv = c_ref[...]
        sg = jax.nn.sigmoid(cv)
        o_ref[...] = d_ref[...] * (sg * (1.0 + cv * (1.0 - sg)))

    return pl.pallas_call(body, name="silu_grad_mul", out_shape=jax.ShapeDtypeStruct(ds.shape, F32))(ds, cvec)


def _adamw(w, g, m, v, name, g_transposed=False):
    r, cdim = w.shape
    tr = _pick(r, 1024, LANES if g_transposed else 8)
    tc = _pick(cdim, max(LANES, (1 << 19) // tr))
    b1c = 1.0 - ADAM_B1**ADAM_STEP
    b2c = 1.0 - ADAM_B2**ADAM_STEP

    def body(w_ref, g_ref, m_ref, v_ref, *outs):
        d_ref, mo_ref, vo_ref = outs[-3:]
        gv = g_ref[...]
        if g_transposed:
            gv = gv.T
            outs[0][...] = gv
        mn = ADAM_B1 * m_ref[...] + (1.0 - ADAM_B1) * gv
        vn = ADAM_B2 * v_ref[...] + (1.0 - ADAM_B2) * (gv * gv)
        mo_ref[...] = mn
        vo_ref[...] = vn
        d_ref[...] = -ADAM_LR * ((mn / b1c) / (jnp.sqrt(vn / b2c) + ADAM_EPS) + ADAM_WD * w_ref[...])

    spec = pl.BlockSpec((tr, tc), lambda i, j: (i, j))
    g_spec = pl.BlockSpec((tc, tr), lambda i, j: (j, i)) if g_transposed else spec
    n_out = 4 if g_transposed else 3
    res = pl.pallas_call(
        body,
        name=name,
        grid=(r // tr, cdim // tc),
        in_specs=[spec, g_spec, spec, spec],
        out_specs=[spec] * n_out,
        out_shape=[jax.ShapeDtypeStruct((r, cdim), F32)] * n_out,
        compiler_params=_params(("parallel", "parallel")),
    )(w, g, m, v)
    return res if g_transposed else [g, *res]


def _place():
    return lax.axis_index("x"), lax.axis_index("y"), lax.axis_index("c")


def _remote(src, dst, send_sem, recv_sem, dev):
    return pltpu.make_async_remote_copy(src_ref=src, dst_ref=dst, send_sem=send_sem, recv_sem=recv_sem, device_id=dev, device_id_type=MESH)


ANY = pl.BlockSpec(memory_space=pl.ANY)


def _all_gather_small(v, name):
    r, w = v.shape

    def body(v_ref, o_ref, send, recv, lsem):
        x, y, c = _place()
        me = 4 * x + 2 * y + c
        mine = pltpu.make_async_copy(v_ref, o_ref.at[me], lsem)
        mine.start()
        sent = []
        for k in range(1, 8):
            px, py, pc = x ^ (k >> 2), y ^ ((k >> 1) & 1), c ^ (k & 1)
            cp = _remote(v_ref, o_ref.at[me], send.at[k - 1], recv.at[k - 1], (px, py, pc))
            cp.start()
            sent.append(cp)
        for k in range(1, 8):
            px, py, pc = x ^ (k >> 2), y ^ ((k >> 1) & 1), c ^ (k & 1)
            slot = o_ref.at[4 * px + 2 * py + pc]
            _remote(slot, slot, send.at[k - 1], recv.at[k - 1], (x, y, c)).wait_recv()
        for cp in sent:
            cp.wait_send()
        mine.wait()

    return pl.pallas_call(
        body,
        name=name,
        out_shape=jax.ShapeDtypeStruct((8, r, w), F32),
        in_specs=[pl.BlockSpec(memory_space=pltpu.VMEM)],
        out_specs=pl.BlockSpec(memory_space=pltpu.VMEM),
        scratch_shapes=[pltpu.SemaphoreType.DMA((7,)), pltpu.SemaphoreType.DMA((7,)), pltpu.SemaphoreType.DMA],
        compiler_params=pltpu.CompilerParams(vmem_limit_bytes=VMEM_LIMIT),
    )(v)


HBM = pl.BlockSpec(memory_space=pltpu.HBM)
SEM = pl.BlockSpec(memory_space=pltpu.SEMAPHORE)
EFFECT = pltpu.SideEffectType.DATAFLOW_SIDE_EFFECTING


def _other_chips(x, y):
    return [(1 - x, y), (x, 1 - y), (1 - x, 1 - y)]


def _bulk_start(name, srcs, land_shapes, n_copies, copies, after):
    n, m = len(srcs), len(land_shapes)

    def body(*refs):
        src_refs, land_refs = refs[:n], refs[n : n + m]
        send, recv = refs[n + m + 1], refs[n + m + 2]
        token = refs[-1]
        for k, (s, d, dev) in enumerate(copies(src_refs, land_refs)):
            _remote(s, d, send.at[k], recv.at[k], dev).start()
        token[...] = jnp.zeros_like(token)

    lands = [pltpu.with_memory_space_constraint(lax.empty(s.shape, s.dtype), pltpu.HBM) for s in land_shapes]
    out = pl.pallas_call(
        body,
        name=name,
        out_shape=(pltpu.SemaphoreType.DMA((n_copies,)), pltpu.SemaphoreType.DMA((n_copies,)),
                   *[pltpu.HBM(s.shape, s.dtype) for s in srcs], *[pltpu.HBM(s.shape, s.dtype) for s in land_shapes],
                   jax.ShapeDtypeStruct((8, LANES), F32)),
        in_specs=[HBM] * (n + m) + [ANY],
        out_specs=(SEM, SEM, *[HBM] * (n + m), pl.BlockSpec(memory_space=pltpu.VMEM)),
        input_output_aliases={i: 2 + i for i in range(n + m)},
        compiler_params=pltpu.CompilerParams(has_side_effects=EFFECT),
    )(*[pltpu.with_memory_space_constraint(s, pltpu.HBM) for s in srcs], *lands, after)
    return out[0], out[1], list(out[2 : 2 + n]), list(out[2 + n : 2 + n + m]), out[-1][0:1, 0:1]


def _bulk_wait(name, send, recv, srcs, lands, after, waits):
    n, m = len(srcs), len(lands)

    def body(*refs):
        src_refs, land_refs = refs[:n], refs[n : n + m]
        send_sem, recv_sem = refs[n + m], refs[n + m + 1]
        x, y, c = _place()
        for k, (s, d) in enumerate(waits(src_refs, land_refs)):
            cp = _remote(s, d, send_sem.at[k], recv_sem.at[k], (x, y, c))
            cp.wait_send()
            cp.wait_recv()

    out = pl.pallas_call(
        body,
        name=name,
        out_shape=tuple(pltpu.HBM(s.shape, s.dtype) for s in (*srcs, *lands)),
        in_specs=[HBM] * (n + m) + [SEM, SEM, ANY],
        out_specs=tuple([HBM] * (n + m)),
        input_output_aliases={i: i for i in range(n + m)},
        compiler_params=pltpu.CompilerParams(has_side_effects=EFFECT),
    )(*srcs, *lands, send, recv, after)
    return list(out[:n]), list(out[n:])


def _gather_start(shards, after, name):
    def copies(src, land):
        x, y, c = _place()
        j = 2 * x + y
        return [(src[a].at[c], land[a].at[j, c], (px, py, c)) for a in range(len(shards)) for px, py in _other_chips(x, y)]

    shapes = [jax.ShapeDtypeStruct((4,) + s.shape, s.dtype) for s in shards]
    return _bulk_start(name, shards, shapes, 3 * len(shards), copies, after)


def _gather_wait(started, after, name):
    send, recv, srcs, lands, _ = started

    def waits(src, land):
        x, y, c = _place()
        return [(src[a].at[c], land[a].at[2 * px + py, c]) for a in range(len(srcs)) for px, py in _other_chips(x, y)]

    return _bulk_wait(name, send, recv, srcs, lands, after, waits)


def _forward_halves(lands, name):
    n = len(lands)

    def body(*refs):
        bufs = refs[n : 2 * n]
        send, recv = refs[2 * n :]
        x, y, c = _place()
        started = []
        for a in range(n):
            for k, (px, py) in enumerate(_other_chips(x, y)):
                blk = bufs[a].at[2 * px + py, c]
                cp = _remote(blk, blk, send.at[3 * a + k], recv.at[3 * a + k], (x, y, 1 - c))
                cp.start()
                started.append(cp)
        for a in range(n):
            for k, (px, py) in enumerate(_other_chips(x, y)):
                blk = bufs[a].at[2 * px + py, 1 - c]
                _remote(blk, blk, send.at[3 * a + k], recv.at[3 * a + k], (x, y, c)).wait_recv()
        for cp in started:
            cp.wait_send()

    return pl.pallas_call(
        body,
        name=name,
        out_shape=[jax.ShapeDtypeStruct(b.shape, b.dtype) for b in lands],
        in_specs=[ANY] * n,
        out_specs=[ANY] * n,
        input_output_aliases={i: i for i in range(n)},
        scratch_shapes=[pltpu.SemaphoreType.DMA((3 * n,)), pltpu.SemaphoreType.DMA((3 * n,))],
    )(*lands)


def _gather_finish(started, after, tag):
    shards, lands = _gather_wait(started, after, "gather_wait_" + tag)
    lands = _forward_halves(lands, "gather_forward_" + tag)
    j = 2 * lax.axis_index("x") + lax.axis_index("y")
    full = [lax.dynamic_update_slice(b, s[None], (j, 0, 0, 0)) for b, s in zip(lands, shards)]
    return [f.reshape(4 * f.shape[2] * 2, f.shape[3]) for f in full]


def _swap_halves(grads, name):
    n = len(grads)

    def body(*refs):
        ins, outs = refs[:n], refs[n : 2 * n]
        send, recv = refs[2 * n :]
        x, y, c = _place()
        started = []
        for a in range(n):
            for s in range(4):
                cp = _remote(ins[a].at[s, 1 - c], outs[a].at[s], send.at[4 * a + s], recv.at[4 * a + s], (x, y, 1 - c))
                cp.start()
                started.append(cp)
        for cp in started:
            cp.wait_recv()
        for cp in started:
            cp.wait_send()

    return pl.pallas_call(
        body,
        name=name,
        out_shape=[jax.ShapeDtypeStruct((4,) + g.shape[2:], g.dtype) for g in grads],
        in_specs=[ANY] * n,
        out_specs=[ANY] * n,
        scratch_shapes=[pltpu.SemaphoreType.DMA((4 * n,)), pltpu.SemaphoreType.DMA((4 * n,))],
    )(*grads)


def _add_halves(grads, others, tag):
    outs = []
    for a, (g, o) in enumerate(zip(grads, others)):
        _, _, rh, cdim = g.shape
        tr = _pick(rh, 512, 16)

        def body(g_ref, o_ref, p_ref):
            c = lax.axis_index("c")
            own = jnp.where(c == 0, g_ref[0].astype(F32), g_ref[1].astype(F32))
            p_ref[...] = (own + o_ref[...].astype(F32)).astype(BF16)

        outs.append(
            pl.pallas_call(
                body,
                name=f"add_halves_{tag}{a}",
                grid=(4, rh // tr),
                in_specs=[pl.BlockSpec((None, 2, tr, cdim), lambda s, i: (s, 0, i, 0)), pl.BlockSpec((None, tr, cdim), lambda s, i: (s, i, 0))],
                out_specs=pl.BlockSpec((None, tr, cdim), lambda s, i: (s, i, 0)),
                out_shape=jax.ShapeDtypeStruct((4, rh, cdim), BF16),
                compiler_params=_params(("parallel", "parallel")),
            )(g, o)
        )
    return outs


def _exchange_start(parts, after, name):
    def copies(src, land):
        x, y, c = _place()
        j = 2 * x + y
        return [(src[a].at[2 * px + py], land[a].at[j], (px, py, c)) for a in range(len(parts)) for px, py in _other_chips(x, y)]

    return _bulk_start(name, parts, [jax.ShapeDtypeStruct(p.shape, p.dtype) for p in parts], 3 * len(parts), copies, after)


def _exchange_finish(started, after, name):
    send, recv, srcs, lands, _ = started

    def waits(src, land):
        x, y, _ = _place()
        return [(src[a].at[2 * px + py], land[a].at[2 * px + py]) for a in range(len(srcs)) for px, py in _other_chips(x, y)]

    srcs, lands = _bulk_wait(name, send, recv, srcs, lands, after, waits)
    j = 2 * lax.axis_index("x") + lax.axis_index("y")
    return [lax.dynamic_update_slice(b, lax.dynamic_slice(p, (j, 0, 0), (1,) + p.shape[1:]), (j, 0, 0)) for b, p in zip(lands, srcs)]


def _sum_chips(recvd, tag):
    outs = []
    for a, g in enumerate(recvd):
        _, rh, cdim = g.shape
        tr = _pick(rh, 512, 16)

        def body(g_ref, o_ref):
            o_ref[...] = ((g_ref[0].astype(F32) + g_ref[1].astype(F32)) + g_ref[2].astype(F32)) + g_ref[3].astype(F32)

        outs.append(
            pl.pallas_call(
                body,
                name=f"sum_chips_{tag}{a}",
                grid=(rh // tr,),
                in_specs=[pl.BlockSpec((4, tr, cdim), lambda i: (0, i, 0))],
                out_specs=pl.BlockSpec((tr, cdim), lambda i: (i, 0)),
                out_shape=jax.ShapeDtypeStruct((rh, cdim), F32),
                compiler_params=_params(("parallel",)),
            )(g)
        )
    return outs


def _join_halves(halves, name):
    n = len(halves)

    def body(*refs):
        ins, outs = refs[:n], refs[n : 2 * n]
        send, recv = refs[2 * n :]
        x, y, c = _place()
        started = []
        for a in range(n):
            cp = _remote(ins[a], outs[a], send.at[a], recv.at[a], (x, y, 1 - c))
            cp.start()
            started.append(cp)
        for cp in started:
            cp.wait_recv()
        for cp in started:
            cp.wait_send()

    others = pl.pallas_call(
        body,
        name=name,
        out_shape=[jax.ShapeDtypeStruct(h.shape, h.dtype) for h in halves],
        in_specs=[ANY] * n,
        out_specs=[ANY] * n,
        scratch_shapes=[pltpu.SemaphoreType.DMA((n,)), pltpu.SemaphoreType.DMA((n,))],
    )(*halves)
    first = lax.axis_index("c") == 0
    return [jnp.concatenate([jnp.where(first, h, o), jnp.where(first, o, h)], axis=0) for h, o in zip(halves, others)]


def _scatter_start(grads, tag):
    views = [g.reshape(4, 2, g.shape[0] // 8, g.shape[1]) for g in grads]
    mine = _add_halves(views, _swap_halves(views, "swap_halves_" + tag), tag)
    return _exchange_start(mine, mine[-1], "exchange_start_" + tag)


def _scatter_finish(started, after, tag):
    halves = _sum_chips(_exchange_finish(started, after, "exchange_wait_" + tag), tag)
    return _join_halves(halves, "join_halves_" + tag)


def _t_bf16(w):
    return w.T.astype(BF16)


def kernel(x, c, ctx, c_ctx, w_ada, b_ada, norm1_g, w_in, mla_q_norm_g, w_q_up, mla_kv_norm_g, w_kv_up, gqa_q_norm_g, gqa_k_norm_g, w_br_a, w_br_b, w_out, norm2_g, w_up, conv_w, conv_b, w_down, final_norm_g, loss_target, m_c_ctx, m_w_ada, m_b_ada, m_norm1_g, m_w_in, m_mla_q_norm_g, m_w_q_up, m_mla_kv_norm_g, m_w_kv_up, m_gqa_q_norm_g, m_gqa_k_norm_g, m_w_br_a, m_w_br_b, m_w_out, m_norm2_g, m_w_up, m_conv_w, m_conv_b, m_w_down, m_final_norm_g, v_c_ctx, v_w_ada, v_b_ada, v_norm1_g, v_w_in, v_mla_q_norm_g, v_w_q_up, v_mla_kv_norm_g, v_w_kv_up, v_gqa_q_norm_g, v_gqa_k_norm_g, v_w_br_a, v_w_br_b, v_w_out, v_norm2_g, v_w_up, v_conv_w, v_conv_b, v_w_down, v_final_norm_g):
    T, D = x.shape[1], x.shape[2]
    C = ctx.shape[1]
    NA = w_ada.shape[2]
    NW = w_up.shape[2]
    F2 = 4 * NW
    FF = F2 // 2
    xi, yi, ci = _place()
    j = 2 * xi + yi
    me = 4 * xi + 2 * yi + ci
    tr = _pick(C, 128, 8)
    tq = _pick(T, 256)

    x2d, tgt, ctx2d = x[0], loss_target[0], ctx[0]
    fg = final_norm_g.reshape(1, D)
    cc = c_ctx.reshape(1, D)

    w0 = max(D, NW)
    pay = jnp.zeros((8, w0), F32).at[0:1, :D].set(c).at[1:4, :NW].set(conv_w[0])
    got = _all_gather_small(pay, "gather_cond")
    c_all = got[:, 0, :D]
    cw = jnp.concatenate([got[2 * s, 1:4, :NW] for s in range(4)], axis=1)
    s16 = jnp.concatenate([c_all, cc, jnp.zeros((7, D), F32)], axis=0)
    b_cols = lax.dynamic_slice(b_ada, (0, j * NA), (1, NA))
    ada_part = _mm(s16, w_ada[0], "NN", F32, "ada_fwd", act="silu", bias=b_cols)
    got = _all_gather_small(ada_part, "gather_ada")
    ada = jnp.concatenate([got[2 * s] for s in range(4)], axis=1)
    lat = lax.dynamic_slice(ada, (me, 0), (1, 6 * D))
    sh1, sc1, g1, sh2, sc2, g2 = [lat[:, k * D : (k + 1) * D] for k in range(6)]
    csh, csc = ada[8:9, :D], ada[8:9, D : 2 * D]

    wq3 = w_q_up[0].reshape(MLA_Q_LORA, 2, MLA_NOPE + MLA_ROPE)
    wq_perm = jnp.concatenate([wq3[:, :, :MLA_NOPE].reshape(MLA_Q_LORA, -1), wq3[:, :, MLA_NOPE:].reshape(MLA_Q_LORA, -1)], axis=1)
    shards = [_t_bf16(w_in[0]), _t_bf16(wq_perm), _t_bf16(w_kv_up[0]), _t_bf16(w_br_a[0]), _t_bf16(w_br_b[0]),
              w_out[0].astype(BF16), _t_bf16(w_up[0]), w_down[0].astype(BF16)]
    halves = [s.reshape(2, s.shape[0] // 2, s.shape[1]) for s in shards]
    ag_in = _gather_start(halves[0:1], got, "gather_start_in")
    ag_mix = _gather_start(halves[1:6], ag_in[4], "gather_start_mix")
    ag_ffn = _gather_start(halves[6:8], ag_mix[4], "gather_start_ffn")
    sh1 = sh1 + ag_ffn[4]

    cos_a, ss_a = _rope_tables(C, T, MLA_ROPE)
    cos_b, ss_b = _rope_tables(C, T, GQA_HEAD_DIM)
    lcos_a, lss_a, lcos_b, lss_b = cos_a[:T], ss_a[:T], cos_b[:T], ss_b[:T]

    z_all = _norm_mod_fwd(x2d, norm1_g, sh1, sc1, "norm1_lat_fwd", tr, out_rows=T + C)
    z_all = _norm_mod_fwd(ctx2d, norm1_g, csh, csc, "norm1_ctx_fwd", tr, base=z_all, out_off=T)
    (win_t,) = _gather_finish(ag_in, z_all, "in")
    kv_cols = KVP - LANES + MLA_ROPE
    e_kpe = MLA_KV_LORA + MLA_ROPE
    w_kvp = jnp.concatenate([win_t[:MLA_KV_LORA], win_t[e_kpe:kv_cols], win_t[MLA_KV_LORA:e_kpe], jnp.zeros((LANES - MLA_ROPE, D), BF16)], axis=0)

    pkv = _mm(z_all, w_kvp, "NT", F32, "proj_kv")
    pq = _mm(z_all, win_t, "NT", F32, "proj_q", m=T, n=QC, b_off=kv_cols)
    pg = _mm(z_all, win_t, "NT", F32, "proj_g", m=T, n=2 * D, b_off=kv_cols + QC)
    wq_t, wkv_t, wbra_t, wbrb_t, wout = _gather_finish(ag_mix, pg, "mix")
    ckv_n, kb2, vb2, kpe2 = _kprep_fwd(pkv, mla_kv_norm_g, gqa_k_norm_g, cos_a, ss_a, cos_b, ss_b, tr)
    kv_up = _mm(ckv_n, wkv_t, "NT", BF16, "kv_up")
    cq_n, qb2 = _qprep_fwd(pq, mla_q_norm_g, gqa_q_norm_g, lcos_b, lss_b, tr)
    q_a = _mm(cq_n, wq_t, "NT", F32, "q_up")
    qar = _qrope_fwd(q_a, lcos_a, lss_a, tr)

    a_q = [(qar, lambda h: 3 * (h // 2) + h % 2), (qar, lambda h: 3 * (h // 2) + 2)]
    a_k = [(kv_up, lambda h: 2 * h), (kpe2, lambda h: h % 2)]
    a_v = (kv_up, lambda h: 2 * h + 1)
    a_scale = float(MLA_NOPE + MLA_ROPE) ** -0.5
    b_q = [(qb2, lambda h: h)]
    b_k = [(kb2, lambda h: h)]
    b_v = (vb2, lambda h: h)
    b_scale = float(GQA_HEAD_DIM) ** -0.5
    o_a, lse_a = _attn_fwd(a_q, a_k, a_v, MLA_HEADS, 1, MLA_V, a_scale, "attn_a_fwd", tq)
    o_b, lse_b = _attn_fwd(b_q, b_k, b_v, GQA_HEADS, GQA_GROUP, GQA_HEAD_DIM, b_scale, "attn_b_fwd", tq)
    ya = _mm(o_a, wbra_t, "NT", F32, "br_a")
    yb = _mm(o_b, wbrb_t, "NT", F32, "br_b")
    merged = _gates_fwd(pg, ya, yb, tr)
    att = _mm(merged, wout, "NN", F32, "out_proj")
    x1, z2 = _resid_norm2_fwd(x2d, att, g1, norm2_g, sh2, sc2, tr)
    wup_t, wdown = _gather_finish(ag_ffn, o_b, "ffn")
    u = _mm(z2, wup_t, "NT", BF16, "ffn_up")
    tc = _pick(FF, 128)
    hg = _conv_fwd(u, cw, conv_b, tc)
    f = _mm(hg, wdown, "NN", F32, "ffn_down")
    sq, dx2, d_fg, d_g2, df = _loss_head(x1, f, g2, fg, tgt, tr)
    loss = lax.psum(0.5 * jnp.sum(sq) / D, ("x", "y", "c"))

    dhg = _mm(df, wdown, "NT", BF16, "ffn_down_dx")
    g_wdown = _mm(hg, df, "TN", BF16, "ffn_down_dw")
    du_a, du_b, dcw_a, dcw_b, dcb_a, dcb_b = _conv_bwd(u, dhg, cw, conv_b, tc)
    dz2 = _mm(du_a, wup_t, "NN", F32, "ffn_up_dx_a")
    dz2 = _mm(du_b, wup_t, "NN", F32, "ffn_up_dx_b", b_off=FF, add=dz2)
    g_wup_t = _mm(du_a, z2, "TN", BF16, "ffn_up_dw_a", out_rows=F2, tm=FF // 4)
    g_wup_t = _mm(du_b, z2, "TN", BF16, "ffn_up_dw_b", out_base=g_wup_t, out_off=FF, tm=FF // 4)
    rs_ffn = _scatter_start([g_wdown, g_wup_t], "ffn")
    sc2 = sc2 + rs_ffn[4]
    dx1, datt, d_n2g, d_sh2, d_sc2, d_g1 = _resid_norm2_bwd(dz2, x1, dx2, att, norm2_g, sc2, g1, tr)

    dmerged = _mm(datt, wout, "NT", F32, "out_proj_dx")
    g_wout = _mm(merged, datt, "TN", BF16, "out_proj_dw")
    dya, dyb, dpg = _gates_bwd(dmerged, pg, ya, yb, tr)
    do_a = _mm(dya, wbra_t, "NN", BF16, "br_a_dx")
    g_wbra_t = _mm(dya, o_a, "TN", BF16, "br_a_dw")
    do_b = _mm(dyb, wbrb_t, "NN", BF16, "br_b_dx")
    g_wbrb_t = _mm(dyb, o_b, "TN", BF16, "br_b_dw")
    dqa2, dka2, dva2 = _attn_bwd(a_q, a_k, a_v, o_a, do_a, lse_a, MLA_HEADS, 1, MLA_V, a_scale, "attn_a_bwd", tq)
    dqb2, dkb2, dvb2 = _attn_bwd(b_q, b_k, b_v, o_b, do_b, lse_b, GQA_HEADS, GQA_GROUP, GQA_HEAD_DIM, b_scale, "attn_b_bwd", tq)
    dq_a = _qrope_bwd(dqa2, lcos_a, lss_a, tr)
    dcq_n = _mm(dq_a, wq_t, "NN", F32, "q_up_dx")
    g_wq_t = _mm(dq_a, cq_n, "TN", BF16, "q_up_dw")
    dpq, d_qg, d_gq = _qprep_bwd(pq, dcq_n, dqb2, mla_q_norm_g, gqa_q_norm_g, lcos_b, lss_b, tr)
    dkv_up, dkpe = _kgrad_split(dka2, dva2, cos_a, ss_a, tr)
    dckv_n = _mm(dkv_up, wkv_t, "NN", F32, "kv_up_dx")
    g_wkv_t = _mm(dkv_up, ckv_n, "TN", BF16, "kv_up_dw")
    rs_mix = _scatter_start([g_wq_t, g_wkv_t, g_wbra_t, g_wbrb_t, g_wout], "mix")
    dpkv, d_kvg, d_kg = _kprep_bwd(pkv, dckv_n, dkb2, dvb2, dkpe, mla_kv_norm_g + rs_mix[4], gqa_k_norm_g, cos_b, ss_b, tr)
    dz_kv = _mm(dpkv, w_kvp, "NN", F32, "proj_kv_dx")
    dz_lat = _mm(dpq, win_t, "NN", F32, "proj_q_dx", b_off=kv_cols, add=dz_kv)
    dz_lat = _mm(dpg, win_t, "NN", F32, "proj_g_dx", b_off=kv_cols + QC, add=dz_lat)
    g_kvp = _mm(dpkv, z_all, "TN", BF16, "proj_kv_dw")
    nk = MLA_KV_LORA + 2 * GQA_KV_HEADS * GQA_HEAD_DIM
    g_kv = jnp.concatenate([g_kvp[:MLA_KV_LORA], g_kvp[nk : nk + MLA_ROPE], g_kvp[MLA_KV_LORA:nk]], axis=0)
    g_win_t = _mm(dpq, z_all, "TN", BF16, "proj_q_dw", out_rows=kv_cols + QC + 2 * D, out_off=kv_cols, tm=QC // 2)
    g_win_t = _mm(dpg, z_all, "TN", BF16, "proj_g_dw", out_base=g_win_t, out_off=kv_cols + QC)
    g_win_t = lax.dynamic_update_slice(g_win_t, g_kv, (0, 0))
    rs_in = _scatter_start([g_win_t], "in")
    csc, sc1 = csc + rs_in[4], sc1 + rs_in[4]
    _, d_n1g_c, d_csh, d_csc = _norm_mod_bwd(dz_kv, T // tr, ctx2d, norm1_g, csc, None, "norm1_ctx_bwd", tr)
    grad_x, d_n1g_l, d_sh1, d_sc1 = _norm_mod_bwd(dz_lat, 0, x2d, norm1_g, sc1, dx1, "norm1_lat_bwd", tr)

    zeros_d = jnp.zeros((1, D), F32)
    d_lat = jnp.concatenate([d_sh1, d_sc1, d_g1, d_sh2, d_sc2, d_g2], axis=1)
    d_ctx_part = jnp.concatenate([d_csh, d_csc], axis=1)
    flat = jnp.concatenate(
        [d_n1g_c + d_n1g_l, d_qg, d_kvg, d_gq, d_kg, d_n2g, dcb_a, dcb_b, d_fg,
         dcw_a.reshape(1, -1), dcw_b.reshape(1, -1), d_ctx_part, d_lat], axis=1)
    n_flat = flat.shape[1]
    n_rows = -(-n_flat // (8 * LANES)) * 8
    flat = jnp.pad(flat, ((0, 0), (0, n_rows * LANES - n_flat))).reshape(n_rows, LANES)
    got = _all_gather_small(flat, "gather_small_grads")
    tot = _sum_slots(got, "sum_small_grads").reshape(1, -1)
    sizes = [D, MLA_Q_LORA, MLA_KV_LORA, GQA_HEAD_DIM, GQA_HEAD_DIM, D, F2, D, 3 * FF, 3 * FF, 2 * D]
    offs = [0]
    for s in sizes:
        offs.append(offs[-1] + s)
    t_n1g, t_qg, t_kvg, t_gq, t_kg, t_n2g, t_cb, t_fg, t_cwa, t_cwb, t_ctx = [tot[:, offs[k] : offs[k + 1]] for k in range(len(sizes))]
    g_cw_full = jnp.concatenate([t_cwa.reshape(3, FF), t_cwb.reshape(3, FF)], axis=1)
    g_cw = lax.dynamic_slice(g_cw_full, (0, j * NW), (3, NW))
    d_lat_all = got.reshape(8, -1)[:, offs[-1] : offs[-1] + 6 * D]
    g16 = jnp.concatenate([d_lat_all, jnp.pad(t_ctx, ((0, 0), (0, 4 * D))), jnp.zeros((7, 6 * D), F32)], axis=0)
    g_b_ada = _sum_slots(g16.reshape(16, 1, 6 * D), "sum_b_ada")
    g16_cols = lax.dynamic_slice(g16, (0, j * NA), (16, NA))
    g_w_ada = _mm(s16, g16_cols, "TN", F32, "ada_dw", act="silu")
    ds_part = _mm(g16_cols, w_ada[0], "NT", F32, "ada_dx")
    got = _all_gather_small(ds_part[8:16], "gather_ada_dx")
    ds_ctx = _sum_slots(jnp.stack([got[2 * s] for s in range(4)]), "sum_ada_dx")[0:1]
    g_c_ctx = _silu_grad_mul(ds_ctx, cc)

    r_wdown, r_wup = _scatter_finish(rs_ffn, grad_x, "ffn")
    r_wq, r_wkv, r_wbra, r_wbrb, r_wout = _scatter_finish(rs_mix, r_wup, "mix")
    gq_p = r_wq.T
    gq = jnp.concatenate([gq_p[:, : 2 * MLA_NOPE].reshape(MLA_Q_LORA, 2, MLA_NOPE), gq_p[:, 2 * MLA_NOPE :].reshape(MLA_Q_LORA, 2, MLA_ROPE)], axis=2)
    grads = {
        "c_ctx": g_c_ctx.reshape(D), "w_ada": g_w_ada[None], "b_ada": g_b_ada, "norm1_g": t_n1g,
        "mla_q_norm_g": t_qg, "w_q_up": gq.reshape(1, MLA_Q_LORA, -1), "mla_kv_norm_g": t_kvg, "w_kv_up": r_wkv,
        "gqa_q_norm_g": t_gq, "gqa_k_norm_g": t_kg, "w_br_a": r_wbra, "w_br_b": r_wbrb, "w_out": r_wout[None],
        "norm2_g": t_n2g, "w_up": r_wup, "conv_w": g_cw[None], "conv_b": t_cb, "w_down": r_wdown[None],
        "final_norm_g": t_fg.reshape(D),
    }
    arrives_transposed = ("w_kv_up", "w_br_a", "w_br_b", "w_up")
    weights = dict(c_ctx=c_ctx, w_ada=w_ada, b_ada=b_ada, norm1_g=norm1_g, w_in=w_in, mla_q_norm_g=mla_q_norm_g, w_q_up=w_q_up,
                   mla_kv_norm_g=mla_kv_norm_g, w_kv_up=w_kv_up, gqa_q_norm_g=gqa_q_norm_g, gqa_k_norm_g=gqa_k_norm_g, w_br_a=w_br_a,
                   w_br_b=w_br_b, w_out=w_out, norm2_g=norm2_g, w_up=w_up, conv_w=conv_w, conv_b=conv_b, w_down=w_down,
                   final_norm_g=final_norm_g)
    m_in = dict(c_ctx=m_c_ctx, w_ada=m_w_ada, b_ada=m_b_ada, norm1_g=m_norm1_g, w_in=m_w_in, mla_q_norm_g=m_mla_q_norm_g,
                w_q_up=m_w_q_up, mla_kv_norm_g=m_mla_kv_norm_g, w_kv_up=m_w_kv_up, gqa_q_norm_g=m_gqa_q_norm_g,
                gqa_k_norm_g=m_gqa_k_norm_g, w_br_a=m_w_br_a, w_br_b=m_w_br_b, w_out=m_w_out, norm2_g=m_norm2_g, w_up=m_w_up,
                conv_w=m_conv_w, conv_b=m_conv_b, w_down=m_w_down, final_norm_g=m_final_norm_g)
    v_in = dict(c_ctx=v_c_ctx, w_ada=v_w_ada, b_ada=v_b_ada, norm1_g=v_norm1_g, w_in=v_w_in, mla_q_norm_g=v_mla_q_norm_g,
                w_q_up=v_w_q_up, mla_kv_norm_g=v_mla_kv_norm_g, w_kv_up=v_w_kv_up, gqa_q_norm_g=v_gqa_q_norm_g,
                gqa_k_norm_g=v_gqa_k_norm_g, w_br_a=v_w_br_a, w_br_b=v_w_br_b, w_out=v_w_out, norm2_g=v_norm2_g, w_up=v_w_up,
                conv_w=v_conv_w, conv_b=v_conv_b, w_down=v_w_down, final_norm_g=v_final_norm_g)
    names = list(weights)
    big = [n for n in names if weights[n].ndim == 3 and weights[n].shape[1] >= 8]
    small = [n for n in names if n not in big]
    delta, new_m, new_v = {}, {}, {}

    def update(n):
        shp = weights[n].shape
        two_d = lambda a: a.reshape(shp[1], shp[2])
        g_t = n in arrives_transposed
        g_in = grads[n] if g_t else two_d(grads[n].astype(F32))
        g_, d_, m_, v_ = _adamw(two_d(weights[n]), g_in, two_d(m_in[n]), two_d(v_in[n]), "adamw_" + n, g_transposed=g_t)
        grads[n], delta[n], new_m[n], new_v[n] = g_.reshape(shp), d_.reshape(shp), m_.reshape(shp), v_.reshape(shp)

    early = [n for n in big if n != "w_in"]
    for n in early:
        update(n)
    done = sum(delta[n][0, 0:1, 0:1] for n in early)
    (r_win,) = _scatter_finish(rs_in, done, "in")
    _, d_, m_, v_ = _adamw(w_in[0].T, r_win, m_w_in[0].T, v_w_in[0].T, "adamw_w_in")
    grads["w_in"], delta["w_in"], new_m["w_in"], new_v["w_in"] = r_win.T[None], d_.T[None], m_.T[None], v_.T[None]
    grads = {n: grads[n].reshape(weights[n].shape).astype(F32) for n in names}

    def pack(tree):
        flat_ = jnp.concatenate([tree[n].reshape(-1) for n in small])
        rows = -(-flat_.shape[0] // (8 * LANES)) * 8
        return jnp.pad(flat_, (0, rows * LANES - flat_.shape[0])).reshape(rows, LANES)

    _, d_, m_, v_ = _adamw(pack(weights), pack(grads), pack(m_in), pack(v_in), "adamw_small")
    off = 0
    for n in small:
        size = weights[n].size
        shp = weights[n].shape
        delta[n] = d_.reshape(-1)[off : off + size].reshape(shp)
        new_m[n] = m_.reshape(-1)[off : off + size].reshape(shp)
        new_v[n] = v_.reshape(-1)[off : off + size].reshape(shp)
        off += size

    return (loss, grad_x[None], *[grads[n] for n in names], *[delta[n] for n in names], *[new_m[n] for n in names],
            *[new_v[n] for n in names])
```

```python
import math

import jax
import jax.numpy as jnp
from jax import lax
from jax.experimental import pallas as pl
from jax.experimental.pallas import tpu as pltpu

F32 = jnp.float32
BF16 = jnp.bfloat16
MESH = pl.DeviceIdType.MESH

NORM_EPS = 1e-6
ROPE_THETA = 10000.0
GRID_W = 64
MLA_HEADS = 8
MLA_Q_LORA = 768
MLA_KV_LORA = 512
MLA_NOPE = 128
MLA_ROPE = 64
MLA_V = 128
GQA_HEADS = 8
GQA_KV_HEADS = 2
GQA_HEAD_DIM = 128
GQA_GROUP = GQA_HEADS // GQA_KV_HEADS
LANES = 128
KVP = MLA_KV_LORA + 2 * GQA_KV_HEADS * GQA_HEAD_DIM + LANES
QC = MLA_Q_LORA + GQA_HEADS * GQA_HEAD_DIM

ADAM_LR = 0.001
ADAM_B1 = 0.9
ADAM_B2 = 0.999
ADAM_EPS = 1e-08
ADAM_WD = 0.01
ADAM_STEP = 10

VMEM_LIMIT = 56 * 1024 * 1024


def _pick(dim, target, mult=LANES):
    t = (min(target, dim) // mult) * mult
    while t >= mult:
        if dim % t == 0:
            return t
        t -= mult
    return dim


def _params(sem):
    return pltpu.CompilerParams(dimension_semantics=sem, vmem_limit_bytes=VMEM_LIMIT)


_DIMS = {"NN": (((1,), (0,)), ((), ())), "NT": (((1,), (1,)), ((), ())), "TN": (((0,), (0,)), ((), ()))}


MM_VMEM_BUDGET = 36 * 1024 * 1024


def _mm_tiles(M, N, K, sa, sb, so, tm, tn, tk):
    tm, tn, tk = _pick(M, tm), _pick(N, tn), _pick(K, tk)

    def need(t):
        return 2 * (tm * t * sa + t * tn * sb) + 2 * tm * tn * so + (tm * tn * 4 if t < K else 0)

    while need(tk) > MM_VMEM_BUDGET and tk > LANES:
        smaller = _pick(K, tk - LANES)
        if smaller >= tk:
            break
        tk = smaller
    return tm, tn, tk


def _window(block, index, offsets):
    if not any(offsets):
        return pl.BlockSpec(block, index)
    for t, o in zip(block, offsets):
        assert o % 16 == 0 and t % 16 == 0, (block, offsets)

    def at(i, j, k):
        return tuple(pl.multiple_of(o + p * t, math.gcd(o, t)) for p, t, o in zip(index(i, j, k), block, offsets))

    return pl.BlockSpec(tuple(pl.Element(t) for t in block), at)


def _mm(a, b, mode, out_dtype, name, m=None, n=None, k=None, b_off=0, add=None, out_rows=None, out_base=None, out_off=0,
        tm=1024, tn=1024, tk=2304, act=None, bias=None, after=None):
    if mode == "NN":
        M, K, N = m or a.shape[0], k or a.shape[1], b.shape[1]
    elif mode == "NT":
        M, K, N = m or a.shape[0], a.shape[1], n or b.shape[0]
    else:
        M, K, N = a.shape[1], k or a.shape[0], b.shape[1]
    tm, tn, tk = _mm_tiles(M, N, K, a.dtype.itemsize, b.dtype.itemsize, jnp.dtype(out_dtype).itemsize, tm, tn, tk)
    nk = K // tk
    dims = _DIMS[mode]
    n_in = 2 + (bias is not None) + (add is not None) + (out_base is not None) + (after is not None)

    def body(*refs):
        a_ref, b_ref = refs[:2]
        bias_ref = refs[2] if bias is not None else None
        add_ref = refs[2 + (bias is not None)] if add is not None else None
        o_ref = refs[n_in]
        av = a_ref[...]
        if act == "silu":
            av = av * jax.nn.sigmoid(av)
        part = lax.dot_general(av.astype(BF16), b_ref[...].astype(BF16), dims, preferred_element_type=F32)

        def finish(r):
            if bias is not None:
                r = r + bias_ref[...]
            if add is not None:
                r = r + add_ref[...]
            o_ref[...] = r.astype(out_dtype)

        if nk == 1:
            finish(part)
            return
        acc = refs[-1]
        k = pl.program_id(2)

        @pl.when(k == 0)
        def _():
            acc[...] = part

        @pl.when(jnp.logical_and(k > 0, k < nk - 1))
        def _():
            acc[...] += part

        @pl.when(k == nk - 1)
        def _():
            finish(acc[...] + part)

    a_spec = pl.BlockSpec((tk, tm), lambda i, j, k: (k, i)) if mode == "TN" else pl.BlockSpec((tm, tk), lambda i, j, k: (i, k))
    if mode == "NT":
        b_spec = _window((tn, tk), lambda i, j, k: (j, k), (b_off, 0))
    else:
        b_spec = _window((tk, tn), lambda i, j, k: (k, j), (b_off, 0))
    in_specs, args = [a_spec, b_spec], [a, b]
    if bias is not None:
        in_specs.append(pl.BlockSpec((1, tn), lambda i, j, k: (0, j)))
        args.append(bias)
    if add is not None:
        in_specs.append(pl.BlockSpec((tm, tn), lambda i, j, k: (i, j)))
        args.append(add)
    aliases = {}
    if after is not None:
        in_specs.append(pl.BlockSpec(after.shape, lambda i, j, k: (0, 0)))
        args.append(after)
    if out_base is not None:
        aliases = {len(args): 0}
        in_specs.append(ANY)
        args.append(out_base)
        out_rows = out_base.shape[0]
    return pl.pallas_call(
        body,
        name=name,
        grid=(M // tm, N // tn, nk),
        in_specs=in_specs,
        out_specs=_window((tm, tn), lambda i, j, k: (i, j), (out_off, 0)),
        out_shape=jax.ShapeDtypeStruct((out_rows or M, N), out_dtype),
        input_output_aliases=aliases,
        scratch_shapes=[pltpu.VMEM((tm, tn), F32)] if nk > 1 else [],
        compiler_params=_params(("parallel", "parallel", "arbitrary")),
    )(*args)


def _rms(x):
    r = lax.rsqrt(jnp.mean(x * x, axis=-1, keepdims=True) + NORM_EPS)
    return x * r, r


def _rms_bwd(xh, r, dxh):
    return r * (dxh - xh * jnp.mean(dxh * xh, axis=-1, keepdims=True))


def _swap(x, q):
    lane = lax.broadcasted_iota(jnp.int32, x.shape, 1)
    even = ((lane // q) % 2) == 0
    return jnp.where(even, pltpu.roll(x, LANES - q, 1), pltpu.roll(x, q, 1))


def _rope(x, cos, ss, q):
    return x * cos + _swap(x, q) * ss


def _rope_t(d, cos, ss, q):
    return d * cos + _swap(d * ss, q)


def _csum(x):
    return jnp.sum(x, axis=0, keepdims=True)


def _rows(tr, w, off=0):
    return pl.BlockSpec((tr, w), lambda i: (i + off, 0))


def _bcast(w):
    return pl.BlockSpec((1, w), lambda i: (0, 0))


def _acc_init(i, refs):
    @pl.when(i == 0)
    def _():
        for r in refs:
            r[...] = jnp.zeros_like(r)


def _rope_tables(n_ctx, n_lat, rot_dim):
    rows = n_lat // GRID_W
    row = jnp.repeat(jnp.arange(rows, dtype=F32), GRID_W)
    col = jnp.tile(jnp.arange(GRID_W, dtype=F32), rows)
    half = rot_dim // 2
    inv_freq = ROPE_THETA ** (-jnp.arange(0, half, 2, dtype=F32) / half)
    ar, ac = row[:, None] * inv_freq, col[:, None] * inv_freq
    cos = jnp.concatenate([jnp.cos(ar), jnp.cos(ar), jnp.cos(ac), jnp.cos(ac)], axis=-1)
    ss = jnp.concatenate([-jnp.sin(ar), jnp.sin(ar), -jnp.sin(ac), jnp.sin(ac)], axis=-1)
    cos = jnp.tile(cos, (1, LANES // rot_dim))
    ss = jnp.tile(ss, (1, LANES // rot_dim))
    cos = jnp.concatenate([cos, jnp.ones((n_ctx, LANES), F32)], axis=0)
    ss = jnp.concatenate([ss, jnp.zeros((n_ctx, LANES), F32)], axis=0)
    return cos, ss


def _norm_mod_fwd(x2d, g, sh, sc, name, tr, out_rows=None, base=None, out_off=0):
    n, d = x2d.shape

    def body(x_ref, g_ref, sh_ref, sc_ref, *rest):
        xh, _ = _rms(x_ref[...])
        rest[-1][...] = ((xh * g_ref[...]) * (1.0 + sc_ref[...]) + sh_ref[...]).astype(BF16)

    args, in_specs, aliases = [x2d, g, sh, sc], [_rows(tr, d), _bcast(d), _bcast(d), _bcast(d)], {}
    if base is not None:
        args.append(base)
        in_specs.append(ANY)
        aliases = {4: 0}
        out_rows = base.shape[0]
    return pl.pallas_call(
        body,
        name=name,
        grid=(n // tr,),
        in_specs=in_specs,
        out_specs=_rows(tr, d, out_off // tr),
        out_shape=jax.ShapeDtypeStruct((out_rows or n, d), BF16),
        input_output_aliases=aliases,
        compiler_params=_params(("parallel",)),
    )(*args)


def _norm_mod_bwd(dz, dz_off, x2d, g, sc, dres, name, tr):
    n, d = x2d.shape
    want_dx = dres is not None

    def body(*refs):
        if want_dx:
            dz_ref, x_ref, g_ref, sc_ref, dres_ref, dx_ref, dg_ref, dsh_ref, dsc_ref = refs
        else:
            dz_ref, x_ref, g_ref, sc_ref, dg_ref, dsh_ref, dsc_ref = refs
        _acc_init(pl.program_id(0), [dg_ref, dsh_ref, dsc_ref])
        xh, r = _rms(x_ref[...])
        dzv = dz_ref[...]
        gv = g_ref[...]
        dsc_ref[...] += _csum(dzv * (xh * gv))
        dsh_ref[...] += _csum(dzv)
        dh = dzv * (1.0 + sc_ref[...])
        dg_ref[...] += _csum(dh * xh)
        if want_dx:
            dx_ref[...] = _rms_bwd(xh, r, dh * gv) + dres_ref[...]

    in_specs = [_rows(tr, d, dz_off), _rows(tr, d), _bcast(d), _bcast(d)]
    args = [dz, x2d, g, sc]
    out_specs = [_bcast(d)] * 3
    out_shape = [jax.ShapeDtypeStruct((1, d), F32)] * 3
    if want_dx:
        in_specs.append(_rows(tr, d))
        args.append(dres)
        out_specs = [_rows(tr, d)] + out_specs
        out_shape = [jax.ShapeDtypeStruct((n, d), F32)] + out_shape
    res = pl.pallas_call(
        body,
        name=name,
        grid=(n // tr,),
        in_specs=in_specs,
        out_specs=out_specs,
        out_shape=out_shape,
        compiler_params=_params(("arbitrary",)),
    )(*args)
    return res if want_dx else (None, *res)


_QA, _QB = MLA_ROPE // 4, GQA_HEAD_DIM // 4


def _kprep_fwd(pkv, kvg, kg, cos_a, ss_a, cos_b, ss_b, tr):
    n = pkv.shape[0]
    nb = GQA_KV_HEADS * GQA_HEAD_DIM

    def body(p_ref, kvg_ref, kg_ref, ca, sa, cb, sb, ckv_ref, kb_ref, vb_ref, kpe_ref):
        p = p_ref[...]
        xh, _ = _rms(p[:, :MLA_KV_LORA])
        ckv_ref[...] = (xh * kvg_ref[...]).astype(BF16)
        for e in range(GQA_KV_HEADS):
            lo = MLA_KV_LORA + e * GQA_HEAD_DIM
            kh, _ = _rms(p[:, lo : lo + GQA_HEAD_DIM])
            kb_ref[:, e * GQA_HEAD_DIM : (e + 1) * GQA_HEAD_DIM] = _rope(kh * kg_ref[...], cb[...], sb[...], _QB).astype(BF16)
        vb_ref[...] = p[:, MLA_KV_LORA + nb : MLA_KV_LORA + 2 * nb].astype(BF16)
        kr = _rope(p[:, MLA_KV_LORA + 2 * nb :], ca[...], sa[...], _QA)
        kpe_ref[:, :LANES] = kr.astype(BF16)
        kpe_ref[:, LANES:] = pltpu.roll(kr, MLA_ROPE, 1).astype(BF16)

    return pl.pallas_call(
        body,
        name="kprep_fwd",
        grid=(n // tr,),
        in_specs=[_rows(tr, KVP), _bcast(MLA_KV_LORA), _bcast(GQA_HEAD_DIM)] + [_rows(tr, LANES)] * 4,
        out_specs=[_rows(tr, MLA_KV_LORA), _rows(tr, nb), _rows(tr, nb), _rows(tr, 2 * LANES)],
        out_shape=[jax.ShapeDtypeStruct((n, w), BF16) for w in (MLA_KV_LORA, nb, nb, 2 * LANES)],
        compiler_params=_params(("parallel",)),
    )(pkv, kvg, kg, cos_a, ss_a, cos_b, ss_b)


def _kprep_bwd(pkv, dckv, dkb, dvb, dkpe, kvg, kg, cos_b, ss_b, tr):
    n = pkv.shape[0]
    nb = GQA_KV_HEADS * GQA_HEAD_DIM

    def body(p_ref, dckv_ref, dkb_ref, dvb_ref, dkpe_ref, kvg_ref, kg_ref, cb, sb, dp_ref, dkvg_ref, dkg_ref):
        _acc_init(pl.program_id(0), [dkvg_ref, dkg_ref])
        p = p_ref[...]
        xh, r = _rms(p[:, :MLA_KV_LORA])
        dn = dckv_ref[...]
        dkvg_ref[...] += _csum(dn * xh)
        dp_ref[:, :MLA_KV_LORA] = _rms_bwd(xh, r, dn * kvg_ref[...]).astype(BF16)
        for e in range(GQA_KV_HEADS):
            lo = MLA_KV_LORA + e * GQA_HEAD_DIM
            kh, rk = _rms(p[:, lo : lo + GQA_HEAD_DIM])
            dk = _rope_t(dkb_ref[:, e * GQA_HEAD_DIM : (e + 1) * GQA_HEAD_DIM], cb[...], sb[...], _QB)
            dkg_ref[...] += _csum(dk * kh)
            dp_ref[:, lo : lo + GQA_HEAD_DIM] = _rms_bwd(kh, rk, dk * kg_ref[...]).astype(BF16)
        dp_ref[:, MLA_KV_LORA + nb : MLA_KV_LORA + 2 * nb] = dvb_ref[...].astype(BF16)
        dp_ref[:, MLA_KV_LORA + 2 * nb :] = dkpe_ref[...].astype(BF16)

    return pl.pallas_call(
        body,
        name="kprep_bwd",
        grid=(n // tr,),
        in_specs=[_rows(tr, KVP), _rows(tr, MLA_KV_LORA), _rows(tr, nb), _rows(tr, nb), _rows(tr, LANES),
                  _bcast(MLA_KV_LORA), _bcast(GQA_HEAD_DIM), _rows(tr, LANES), _rows(tr, LANES)],
        out_specs=[_rows(tr, KVP), _bcast(MLA_KV_LORA), _bcast(GQA_HEAD_DIM)],
        out_shape=[jax.ShapeDtypeStruct((n, KVP), BF16), jax.ShapeDtypeStruct((1, MLA_KV_LORA), F32),
                   jax.ShapeDtypeStruct((1, GQA_HEAD_DIM), F32)],
        compiler_params=_params(("arbitrary",)),
    )(pkv, dckv, dkb, dvb, dkpe, kvg, kg, cos_b, ss_b)


def _kgrad_split(dka, dva, cos_a, ss_a, tr):
    n = dka.shape[0]
    wk = MLA_HEADS * 2 * LANES

    def body(dk_ref, dv_ref, ca, sa, dkv_ref, dkpe_ref):
        even = jnp.zeros((tr, LANES), F32)
        odd = jnp.zeros((tr, LANES), F32)
        for h in range(MLA_HEADS):
            dkv_ref[:, 2 * h * LANES : (2 * h + 1) * LANES] = dk_ref[:, 2 * h * LANES : (2 * h + 1) * LANES].astype(BF16)
            dkv_ref[:, (2 * h + 1) * LANES : (2 * h + 2) * LANES] = dv_ref[:, h * MLA_V : (h + 1) * MLA_V].astype(BF16)
            part = dk_ref[:, (2 * h + 1) * LANES : (2 * h + 2) * LANES]
            if h % 2 == 0:
                even = even + part
            else:
                odd = odd + part
        lane = lax.broadcasted_iota(jnp.int32, (tr, LANES), 1)
        low = lane < MLA_ROPE
        both = jnp.where(low, even, odd)
        tot = jnp.where(low, both + pltpu.roll(both, MLA_ROPE, 1), 0.0)
        dkpe_ref[...] = _rope_t(tot, ca[...], sa[...], _QA)

    return pl.pallas_call(
        body,
        name="kgrad_split",
        grid=(n // tr,),
        in_specs=[_rows(tr, wk), _rows(tr, MLA_HEADS * MLA_V), _rows(tr, LANES), _rows(tr, LANES)],
        out_specs=[_rows(tr, wk), _rows(tr, LANES)],
        out_shape=[jax.ShapeDtypeStruct((n, wk), BF16), jax.ShapeDtypeStruct((n, LANES), F32)],
        compiler_params=_params(("parallel",)),
    )(dka, dva, cos_a, ss_a)


def _qprep_fwd(pq, qg, gq, cos_b, ss_b, tr):
    n = pq.shape[0]
    nq = GQA_HEADS * GQA_HEAD_DIM

    def body(p_ref, qg_ref, gq_ref, cb, sb, cq_ref, qb_ref):
        xh, _ = _rms(p_ref[:, :MLA_Q_LORA])
        cq_ref[...] = (xh * qg_ref[...]).astype(BF16)
        for h in range(GQA_HEADS):
            lo = MLA_Q_LORA + h * GQA_HEAD_DIM
            qh, _ = _rms(p_ref[:, lo : lo + GQA_HEAD_DIM])
            qb_ref[:, h * GQA_HEAD_DIM : (h + 1) * GQA_HEAD_DIM] = _rope(qh * gq_ref[...], cb[...], sb[...], _QB).astype(BF16)

    return pl.pallas_call(
        body,
        name="qprep_fwd",
        grid=(n // tr,),
        in_specs=[_rows(tr, QC), _bcast(MLA_Q_LORA), _bcast(GQA_HEAD_DIM), _rows(tr, LANES), _rows(tr, LANES)],
        out_specs=[_rows(tr, MLA_Q_LORA), _rows(tr, nq)],
        out_shape=[jax.ShapeDtypeStruct((n, MLA_Q_LORA), BF16), jax.ShapeDtypeStruct((n, nq), BF16)],
        compiler_params=_params(("parallel",)),
    )(pq, qg, gq, cos_b, ss_b)


def _qprep_bwd(pq, dcq, dqb, qg, gq, cos_b, ss_b, tr):
    n = pq.shape[0]
    nq = GQA_HEADS * GQA_HEAD_DIM

    def body(p_ref, dcq_ref, dqb_ref, qg_ref, gq_ref, cb, sb, dp_ref, dqg_ref, dgq_ref):
        _acc_init(pl.program_id(0), [dqg_ref, dgq_ref])
        xh, r = _rms(p_ref[:, :MLA_Q_LORA])
        dn = dcq_ref[...]
        dqg_ref[...] += _csum(dn * xh)
        dp_ref[:, :MLA_Q_LORA] = _rms_bwd(xh, r, dn * qg_ref[...]).astype(BF16)
        for h in range(GQA_HEADS):
            lo = MLA_Q_LORA + h * GQA_HEAD_DIM
            qh, rq = _rms(p_ref[:, lo : lo + GQA_HEAD_DIM])
            dq = _rope_t(dqb_ref[:, h * GQA_HEAD_DIM : (h + 1) * GQA_HEAD_DIM], cb[...], sb[...], _QB)
            dgq_ref[...] += _csum(dq * qh)
            dp_ref[:, lo : lo + GQA_HEAD_DIM] = _rms_bwd(qh, rq, dq * gq_ref[...]).astype(BF16)

    return pl.pallas_call(
        body,
        name="qprep_bwd",
        grid=(n // tr,),
        in_specs=[_rows(tr, QC), _rows(tr, MLA_Q_LORA), _rows(tr, nq), _bcast(MLA_Q_LORA), _bcast(GQA_HEAD_DIM),
                  _rows(tr, LANES), _rows(tr, LANES)],
        out_specs=[_rows(tr, QC), _bcast(MLA_Q_LORA), _bcast(GQA_HEAD_DIM)],
        out_shape=[jax.ShapeDtypeStruct((n, QC), BF16), jax.ShapeDtypeStruct((1, MLA_Q_LORA), F32),
                   jax.ShapeDtypeStruct((1, GQA_HEAD_DIM), F32)],
        compiler_params=_params(("arbitrary",)),
    )(pq, dcq, dqb, qg, gq, cos_b, ss_b)


_QA_COLS = MLA_HEADS * (MLA_NOPE + MLA_ROPE)


def _qrope_fwd(qa, cos_a, ss_a, tr):
    n = qa.shape[0]

    def body(q_ref, ca, sa, o_ref):
        for j in range(MLA_HEADS // 2):
            lo = 3 * j * LANES
            o_ref[:, lo : lo + 2 * LANES] = q_ref[:, lo : lo + 2 * LANES].astype(BF16)
            o_ref[:, lo + 2 * LANES : lo + 3 * LANES] = _rope(q_ref[:, lo + 2 * LANES : lo + 3 * LANES], ca[...], sa[...], _QA).astype(BF16)

    return pl.pallas_call(
        body,
        name="qrope_fwd",
        grid=(n // tr,),
        in_specs=[_rows(tr, _QA_COLS), _rows(tr, LANES), _rows(tr, LANES)],
        out_specs=_rows(tr, _QA_COLS),
        out_shape=jax.ShapeDtypeStruct((n, _QA_COLS), BF16),
        compiler_params=_params(("parallel",)),
    )(qa, cos_a, ss_a)


def _qrope_bwd(dq2, cos_a, ss_a, tr):
    n = dq2.shape[0]

    def body(d_ref, ca, sa, o_ref):
        for j in range(MLA_HEADS // 2):
            lo = 3 * j * LANES
            h0, h1 = 2 * j, 2 * j + 1
            o_ref[:, lo : lo + LANES] = d_ref[:, 2 * h0 * LANES : (2 * h0 + 1) * LANES].astype(BF16)
            o_ref[:, lo + LANES : lo + 2 * LANES] = d_ref[:, 2 * h1 * LANES : (2 * h1 + 1) * LANES].astype(BF16)
            pe = d_ref[:, (2 * h0 + 1) * LANES : (2 * h0 + 2) * LANES] + d_ref[:, (2 * h1 + 1) * LANES : (2 * h1 + 2) * LANES]
            o_ref[:, lo + 2 * LANES : lo + 3 * LANES] = _rope_t(pe, ca[...], sa[...], _QA).astype(BF16)

    return pl.pallas_call(
        body,
        name="qrope_bwd",
        grid=(n // tr,),
        in_specs=[_rows(tr, MLA_HEADS * 2 * LANES), _rows(tr, LANES), _rows(tr, LANES)],
        out_specs=_rows(tr, _QA_COLS),
        out_shape=jax.ShapeDtypeStruct((n, _QA_COLS), BF16),
        compiler_params=_params(("parallel",)),
    )(dq2, cos_a, ss_a)


def _cat(refs):
    vals = [r[...] for r in refs]
    return vals[0] if len(vals) == 1 else jnp.concatenate(vals, axis=-1)


LOG2E = 1.4426950408889634


def _attn_fwd(qparts, kparts, vpart, n_heads, group, dv, scale, name, tq, after=None):
    T, Tk = qparts[0][0].shape[0], kparts[0][0].shape[0]
    nq_, nk_ = len(qparts), len(kparts)
    sub = min(tq, 256)
    c2 = scale * LOG2E

    def body(*refs):
        q_refs, k_refs = refs[:nq_], refs[nq_ : nq_ + nk_]
        v_ref = refs[nq_ + nk_]
        o_ref, lse_ref = refs[-2:]
        k = _cat(k_refs)
        v = v_ref[...]
        for r0 in range(0, tq, sub):
            q = _cat([r.at[r0 : r0 + sub, :] for r in q_refs])
            s = lax.dot_general(q, k, _DIMS["NT"], preferred_element_type=F32)
            m = jnp.max(s, axis=-1, keepdims=True)
            p = jnp.exp2((s - m) * c2)
            l = jnp.sum(p, axis=-1, keepdims=True)
            acc = jnp.dot(p.astype(BF16), v, preferred_element_type=F32)
            o_ref[r0 : r0 + sub, :] = (acc * (1.0 / l)).astype(BF16)
            lse_ref[r0 : r0 + sub, :] = m * scale + jnp.log(l)

    in_specs = [pl.BlockSpec((tq, LANES), lambda h, i, f=f: (i, f(h))) for _, f in qparts]
    in_specs += [pl.BlockSpec((Tk, LANES), lambda h, i, f=f: (0, f(h // group))) for _, f in kparts]
    fv = vpart[1]
    in_specs.append(pl.BlockSpec((Tk, dv), lambda h, i: (0, fv(h // group))))
    args = [*[a for a, _ in qparts], *[a for a, _ in kparts], vpart[0]]
    if after is not None:
        in_specs.append(pl.BlockSpec(after.shape, lambda h, i: (0, 0)))
        args.append(after)
    return pl.pallas_call(
        body,
        name=name,
        grid=(n_heads, T // tq),
        in_specs=in_specs,
        out_specs=[pl.BlockSpec((tq, dv), lambda h, i: (i, h)), pl.BlockSpec((None, tq, 1), lambda h, i: (h, i, 0))],
        out_shape=[jax.ShapeDtypeStruct((T, n_heads * dv), BF16), jax.ShapeDtypeStruct((n_heads, T, 1), F32)],
        compiler_params=_params(("parallel", "parallel")),
    )(*args)


def _attn_bwd(qparts, kparts, vpart, o, do, lse, n_heads, group, dv, scale, name, tq):
    T, Tk = qparts[0][0].shape[0], kparts[0][0].shape[0]
    nq_, nk_ = len(qparts), len(kparts)
    dk_ = LANES * nq_
    n_kv = n_heads // group
    nblk = T // tq
    c2 = scale * LOG2E

    def head(hk, i):
        return hk * group + i // nblk

    def body(*refs):
        q = _cat(refs[:nq_])
        k = _cat(refs[nq_ : nq_ + nk_])
        v_ref, o_ref, do_ref, lse_ref, dq_ref, dk_ref, dv_ref = refs[nq_ + nk_ :]
        i = pl.program_id(1)
        _acc_init(i, [dk_ref, dv_ref])
        s = lax.dot_general(q, k, _DIMS["NT"], preferred_element_type=F32)
        p = jnp.exp2(s * c2 - lse_ref[...] * LOG2E)
        dov = do_ref[...]
        dp = lax.dot_general(dov, v_ref[...], _DIMS["NT"], preferred_element_type=F32)
        delta = jnp.sum(dov.astype(F32) * o_ref[...].astype(F32), axis=-1, keepdims=True)
        ds = (p * (dp - delta)).astype(BF16)
        dq_ref[...] = jnp.dot(ds, k, preferred_element_type=F32) * scale
        dk_ref[...] += lax.dot_general(ds, q, _DIMS["TN"], preferred_element_type=F32)
        dv_ref[...] += lax.dot_general(p.astype(BF16), dov, _DIMS["TN"], preferred_element_type=F32)

        @pl.when(i == group * nblk - 1)
        def _():
            dk_ref[...] *= scale

    in_specs = [pl.BlockSpec((tq, LANES), lambda hk, i, f=f: (i % nblk, f(head(hk, i)))) for _, f in qparts]
    in_specs += [pl.BlockSpec((Tk, LANES), lambda hk, i, f=f: (0, f(hk))) for _, f in kparts]
    fv = vpart[1]
    in_specs.append(pl.BlockSpec((Tk, dv), lambda hk, i: (0, fv(hk))))
    in_specs += [pl.BlockSpec((tq, dv), lambda hk, i: (i % nblk, head(hk, i)))] * 2
    in_specs.append(pl.BlockSpec((None, tq, 1), lambda hk, i: (head(hk, i), i % nblk, 0)))
    return pl.pallas_call(
        body,
        name=name,
        grid=(n_kv, group * nblk),
        in_specs=in_specs,
        out_specs=[pl.BlockSpec((tq, dk_), lambda hk, i: (i % nblk, head(hk, i))),
                   pl.BlockSpec((Tk, dk_), lambda hk, i: (0, hk)),
                   pl.BlockSpec((Tk, dv), lambda hk, i: (0, hk))],
        out_shape=[jax.ShapeDtypeStruct((T, n_heads * dk_), F32), jax.ShapeDtypeStruct((Tk, n_kv * dk_), F32),
                   jax.ShapeDtypeStruct((Tk, n_kv * dv), F32)],
        compiler_params=_params(("parallel", "arbitrary")),
    )(*[a for a, _ in qparts], *[a for a, _ in kparts], vpart[0], o, do, lse)


def _gates_fwd(pg, ya, yb, tr):
    n, d = ya.shape

    def body(pg_ref, ya_ref, yb_ref, o_ref):
        ga = jax.nn.sigmoid(pg_ref[:, :d])
        gb = jax.nn.sigmoid(pg_ref[:, d:])
        o_ref[...] = (ga * ya_ref[...] + gb * yb_ref[...]).astype(BF16)

    return pl.pallas_call(
        body,
        name="gates_fwd",
        grid=(n // tr,),
        in_specs=[_rows(tr, 2 * d), _rows(tr, d), _rows(tr, d)],
        out_specs=_rows(tr, d),
        out_shape=jax.ShapeDtypeStruct((n, d), BF16),
        compiler_params=_params(("parallel",)),
    )(pg, ya, yb)


def _gates_bwd(dm, pg, ya, yb, tr):
    n, d = ya.shape

    def body(dm_ref, pg_ref, ya_ref, yb_ref, dya_ref, dyb_ref, dpg_ref):
        dmv = dm_ref[...]
        ga = jax.nn.sigmoid(pg_ref[:, :d])
        gb = jax.nn.sigmoid(pg_ref[:, d:])
        dya_ref[...] = (dmv * ga).astype(BF16)
        dyb_ref[...] = (dmv * gb).astype(BF16)
        dpg_ref[:, :d] = (dmv * ya_ref[...] * ga * (1.0 - ga)).astype(BF16)
        dpg_ref[:, d:] = (dmv * yb_ref[...] * gb * (1.0 - gb)).astype(BF16)

    return pl.pallas_call(
        body,
        name="gates_bwd",
        grid=(n // tr,),
        in_specs=[_rows(tr, d), _rows(tr, 2 * d), _rows(tr, d), _rows(tr, d)],
        out_specs=[_rows(tr, d), _rows(tr, d), _rows(tr, 2 * d)],
        out_shape=[jax.ShapeDtypeStruct((n, d), BF16), jax.ShapeDtypeStruct((n, d), BF16), jax.ShapeDtypeStruct((n, 2 * d), BF16)],
        compiler_params=_params(("parallel",)),
    )(dm, pg, ya, yb)


def _resid_norm2_fwd(x2d, att, g1, n2g, sh2, sc2, tr):
    n, d = x2d.shape

    def body(x_ref, a_ref, g1_ref, g_ref, sh_ref, sc_ref, x1_ref, z_ref):
        x1 = x_ref[...] + g1_ref[...] * a_ref[...]
        x1_ref[...] = x1
        xh, _ = _rms(x1)
        z_ref[...] = ((xh * g_ref[...]) * (1.0 + sc_ref[...]) + sh_ref[...]).astype(BF16)

    return pl.pallas_call(
        body,
        name="resid_norm2_fwd",
        grid=(n // tr,),
        in_specs=[_rows(tr, d), _rows(tr, d)] + [_bcast(d)] * 4,
        out_specs=[_rows(tr, d), _rows(tr, d)],
        out_shape=[jax.ShapeDtypeStruct((n, d), F32), jax.ShapeDtypeStruct((n, d), BF16)],
        compiler_params=_params(("parallel",)),
    )(x2d, att, g1, n2g, sh2, sc2)


def _resid_norm2_bwd(dz2, x1, dx2, att, n2g, sc2, g1, tr):
    n, d = x1.shape

    def body(dz_ref, x1_ref, dx2_ref, a_ref, g_ref, sc_ref, g1_ref, dx1_ref, da_ref, dg_ref, dsh_ref, dsc_ref, dg1_ref):
        _acc_init(pl.program_id(0), [dg_ref, dsh_ref, dsc_ref, dg1_ref])
        xh, r = _rms(x1_ref[...])
        dzv = dz_ref[...]
        gv = g_ref[...]
        dsc_ref[...] += _csum(dzv * (xh * gv))
        dsh_ref[...] += _csum(dzv)
        dh = dzv * (1.0 + sc_ref[...])
        dg_ref[...] += _csum(dh * xh)
        dx1 = _rms_bwd(xh, r, dh * gv) + dx2_ref[...]
        dx1_ref[...] = dx1
        dg1_ref[...] += _csum(dx1 * a_ref[...])
        da_ref[...] = (dx1 * g1_ref[...]).astype(BF16)

    return pl.pallas_call(
        body,
        name="resid_norm2_bwd",
        grid=(n // tr,),
        in_specs=[_rows(tr, d)] * 4 + [_bcast(d)] * 3,
        out_specs=[_rows(tr, d), _rows(tr, d)] + [_bcast(d)] * 4,
        out_shape=[jax.ShapeDtypeStruct((n, d), F32), jax.ShapeDtypeStruct((n, d), BF16)] + [jax.ShapeDtypeStruct((1, d), F32)] * 4,
        compiler_params=_params(("arbitrary",)),
    )(dz2, x1, dx2, att, n2g, sc2, g1)


def _shift_prev(u):
    row = lax.broadcasted_iota(jnp.int32, u.shape, 0)
    return jnp.where(row == 0, 0.0, pltpu.roll(u, 1, 0))


def _shift_next(u):
    n = u.shape[0]
    row = lax.broadcasted_iota(jnp.int32, u.shape, 0)
    return jnp.where(row == n - 1, 0.0, pltpu.roll(u, n - 1, 0))


def _conv3(u, w_ref, b_ref):
    return b_ref[...] + w_ref[0:1, :] * _shift_prev(u) + w_ref[1:2, :] * u + w_ref[2:3, :] * _shift_next(u)


def _conv_fwd(u, cw, cb, tc):
    n, two_f = u.shape
    f = two_f // 2
    nb = f // tc

    def body(ua_ref, ub_ref, wa_ref, wb_ref, ba_ref, bb_ref, h_ref):
        a = _conv3(ua_ref[...].astype(F32), wa_ref, ba_ref)
        b = _conv3(ub_ref[...].astype(F32), wb_ref, bb_ref)
        h_ref[...] = (a * jax.nn.sigmoid(a) * b).astype(BF16)

    col = lambda rows, off: pl.BlockSpec((rows, tc), lambda i: (0, i + off))
    return pl.pallas_call(
        body,
        name="conv_fwd",
        grid=(nb,),
        in_specs=[col(n, 0), col(n, nb), col(3, 0), col(3, nb), col(1, 0), col(1, nb)],
        out_specs=col(n, 0),
        out_shape=jax.ShapeDtypeStruct((n, f), BF16),
        compiler_params=_params(("parallel",)),
    )(u, u, cw, cw, cb, cb)


def _conv_bwd(u, dh, cw, cb, tc):
    n, two_f = u.shape
    f = two_f // 2
    nb = f // tc

    def part(uv, duc, w_ref, du_ref, dw_ref, db_ref):
        db_ref[...] = _csum(duc)
        dw_ref[0:1, :] = _csum(duc * _shift_prev(uv))
        dw_ref[1:2, :] = _csum(duc * uv)
        dw_ref[2:3, :] = _csum(duc * _shift_next(uv))
        du_ref[...] = (w_ref[0:1, :] * _shift_next(duc) + w_ref[1:2, :] * duc + w_ref[2:3, :] * _shift_prev(duc)).astype(BF16)

    def body(ua_ref, ub_ref, dh_ref, wa_ref, wb_ref, ba_ref, bb_ref, dua_ref, dub_ref, dwa_ref, dwb_ref, dba_ref, dbb_ref):
        ua = ua_ref[...].astype(F32)
        ub = ub_ref[...].astype(F32)
        a = _conv3(ua, wa_ref, ba_ref)
        b = _conv3(ub, wb_ref, bb_ref)
        dhv = dh_ref[...].astype(F32)
        sg = jax.nn.sigmoid(a)
        da = dhv * b * (sg * (1.0 + a * (1.0 - sg)))
        db = dhv * (a * sg)
        part(ua, da, wa_ref, dua_ref, dwa_ref, dba_ref)
        part(ub, db, wb_ref, dub_ref, dwb_ref, dbb_ref)

    col = lambda rows, off: pl.BlockSpec((rows, tc), lambda i: (0, i + off))
    return pl.pallas_call(
        body,
        name="conv_bwd",
        grid=(nb,),
        in_specs=[col(n, 0), col(n, nb), col(n, 0), col(3, 0), col(3, nb), col(1, 0), col(1, nb)],
        out_specs=[col(n, 0), col(n, 0), col(3, 0), col(3, 0), col(1, 0), col(1, 0)],
        out_shape=[jax.ShapeDtypeStruct((n, f), BF16)] * 2 + [jax.ShapeDtypeStruct((3, f), F32)] * 2 + [jax.ShapeDtypeStruct((1, f), F32)] * 2,
        compiler_params=_params(("parallel",)),
    )(u, u, dh, cw, cw, cb, cb)


def _loss_head(x1, f, g2, fg, tgt, tr):
    n, d = x1.shape

    def body(x1_ref, f_ref, g2_ref, fg_ref, t_ref, sq_ref, dx2_ref, dfg_ref, dg2_ref, df_ref):
        _acc_init(pl.program_id(0), [sq_ref, dfg_ref, dg2_ref])
        fv = f_ref[...]
        xh, r = _rms(x1_ref[...] + g2_ref[...] * fv)
        err = xh * fg_ref[...] - t_ref[...]
        sq_ref[...] += _csum(err * err)
        dy = err * (1.0 / d)
        dfg_ref[...] += _csum(dy * xh)
        dx2 = _rms_bwd(xh, r, dy * fg_ref[...])
        dx2_ref[...] = dx2
        dg2_ref[...] += _csum(dx2 * fv)
        df_ref[...] = (dx2 * g2_ref[...]).astype(BF16)

    return pl.pallas_call(
        body,
        name="loss_head",
        grid=(n // tr,),
        in_specs=[_rows(tr, d), _rows(tr, d), _bcast(d), _bcast(d), _rows(tr, d)],
        out_specs=[_bcast(d), _rows(tr, d), _bcast(d), _bcast(d), _rows(tr, d)],
        out_shape=[jax.ShapeDtypeStruct((1, d), F32), jax.ShapeDtypeStruct((n, d), F32), jax.ShapeDtypeStruct((1, d), F32),
                   jax.ShapeDtypeStruct((1, d), F32), jax.ShapeDtypeStruct((n, d), BF16)],
        compiler_params=_params(("arbitrary",)),
    )(x1, f, g2, fg, tgt)


def _sum_slots(g, name):
    s, r, w = g.shape

    def body(g_ref, o_ref):
        acc = g_ref[0]
        for k in range(1, s):
            acc = acc + g_ref[k]
        o_ref[...] = acc

    return pl.pallas_call(body, name=name, out_shape=jax.ShapeDtypeStruct((r, w), F32))(g)


def _silu_grad_mul(ds, cvec):
    def body(d_ref, c_ref, o_ref):
        cv = c_ref[...]
        sg = jax.nn.sigmoid(cv)
        o_ref[...] = d_ref[...] * (sg * (1.0 + cv * (1.0 - sg)))

    return pl.pallas_call(body, name="silu_grad_mul", out_shape=jax.ShapeDtypeStruct(ds.shape, F32))(ds, cvec)


def _adamw(w, g, m, v, name, g_transposed=False):
    r, cdim = w.shape
    tr = _pick(r, 1024, LANES if g_transposed else 8)
    tc = _pick(cdim, max(LANES, (1 << 19) // tr))
    b1c = 1.0 - ADAM_B1**ADAM_STEP
    b2c = 1.0 - ADAM_B2**ADAM_STEP

    def body(w_ref, g_ref, m_ref, v_ref, *outs):
        d_ref, mo_ref, vo_ref = outs[-3:]
        gv = g_ref[...]
        if g_transposed:
            gv = gv.T
            outs[0][...] = gv
        mn = ADAM_B1 * m_ref[...] + (1.0 - ADAM_B1) * gv
        vn = ADAM_B2 * v_ref[...] + (1.0 - ADAM_B2) * (gv * gv)
        mo_ref[...] = mn
        vo_ref[...] = vn
        d_ref[...] = -ADAM_LR * ((mn / b1c) / (jnp.sqrt(vn / b2c) + ADAM_EPS) + ADAM_WD * w_ref[...])

    spec = pl.BlockSpec((tr, tc), lambda i, j: (i, j))
    g_spec = pl.BlockSpec((tc, tr), lambda i, j: (j, i)) if g_transposed else spec
    n_out = 4 if g_transposed else 3
    res = pl.pallas_call(
        body,
        name=name,
        grid=(r // tr, cdim // tc),
        in_specs=[spec, g_spec, spec, spec],
        out_specs=[spec] * n_out,
        out_shape=[jax.ShapeDtypeStruct((r, cdim), F32)] * n_out,
        compiler_params=_params(("parallel", "parallel")),
    )(w, g, m, v)
    return res if g_transposed else [g, *res]


def _place():
    return lax.axis_index("x"), lax.axis_index("y"), lax.axis_index("c")


def _remote(src, dst, send_sem, recv_sem, dev):
    return pltpu.make_async_remote_copy(src_ref=src, dst_ref=dst, send_sem=send_sem, recv_sem=recv_sem, device_id=dev, device_id_type=MESH)


ANY = pl.BlockSpec(memory_space=pl.ANY)


def _all_gather_small(v, name):
    r, w = v.shape

    def body(v_ref, o_ref, send, recv, lsem):
        x, y, c = _place()
        me = 4 * x + 2 * y + c
        mine = pltpu.make_async_copy(v_ref, o_ref.at[me], lsem)
        mine.start()
        sent = []
        for k in range(1, 8):
            px, py, pc = x ^ (k >> 2), y ^ ((k >> 1) & 1), c ^ (k & 1)
            cp = _remote(v_ref, o_ref.at[me], send.at[k - 1], recv.at[k - 1], (px, py, pc))
            cp.start()
            sent.append(cp)
        for k in range(1, 8):
            px, py, pc = x ^ (k >> 2), y ^ ((k >> 1) & 1), c ^ (k & 1)
            slot = o_ref.at[4 * px + 2 * py + pc]
            _remote(slot, slot, send.at[k - 1], recv.at[k - 1], (x, y, c)).wait_recv()
        for cp in sent:
            cp.wait_send()
        mine.wait()

    return pl.pallas_call(
        body,
        name=name,
        out_shape=jax.ShapeDtypeStruct((8, r, w), F32),
        in_specs=[pl.BlockSpec(memory_space=pltpu.VMEM)],
        out_specs=pl.BlockSpec(memory_space=pltpu.VMEM),
        scratch_shapes=[pltpu.SemaphoreType.DMA((7,)), pltpu.SemaphoreType.DMA((7,)), pltpu.SemaphoreType.DMA],
        compiler_params=pltpu.CompilerParams(vmem_limit_bytes=VMEM_LIMIT),
    )(v)


HBM = pl.BlockSpec(memory_space=pltpu.HBM)
SEM = pl.BlockSpec(memory_space=pltpu.SEMAPHORE)
EFFECT = pltpu.SideEffectType.DATAFLOW_SIDE_EFFECTING


def _other_chips(x, y):
    return [(1 - x, y), (x, 1 - y), (1 - x, 1 - y)]


def _bulk_start(name, srcs, land_shapes, n_copies, copies, after):
    n, m = len(srcs), len(land_shapes)

    def body(*refs):
        src_refs, land_refs = refs[:n], refs[n : n + m]
        send, recv = refs[n + m + 1], refs[n + m + 2]
        token = refs[-1]
        for k, (s, d, dev) in enumerate(copies(src_refs, land_refs)):
            _remote(s, d, send.at[k], recv.at[k], dev).start()
        token[...] = jnp.zeros_like(token)

    lands = [pltpu.with_memory_space_constraint(lax.empty(s.shape, s.dtype), pltpu.HBM) for s in land_shapes]
    out = pl.pallas_call(
        body,
        name=name,
        out_shape=(pltpu.SemaphoreType.DMA((n_copies,)), pltpu.SemaphoreType.DMA((n_copies,)),
                   *[pltpu.HBM(s.shape, s.dtype) for s in srcs], *[pltpu.HBM(s.shape, s.dtype) for s in land_shapes],
                   jax.ShapeDtypeStruct((8, LANES), F32)),
        in_specs=[HBM] * (n + m) + [ANY],
        out_specs=(SEM, SEM, *[HBM] * (n + m), pl.BlockSpec(memory_space=pltpu.VMEM)),
        input_output_aliases={i: 2 + i for i in range(n + m)},
        compiler_params=pltpu.CompilerParams(has_side_effects=EFFECT),
    )(*[pltpu.with_memory_space_constraint(s, pltpu.HBM) for s in srcs], *lands, after)
    return out[0], out[1], list(out[2 : 2 + n]), list(out[2 + n : 2 + n + m]), out[-1][0:1, 0:1]


def _bulk_wait(name, send, recv, srcs, lands, after, waits):
    n, m = len(srcs), len(lands)

    def body(*refs):
        src_refs, land_refs = refs[:n], refs[n : n + m]
        send_sem, recv_sem = refs[n + m], refs[n + m + 1]
        x, y, c = _place()
        for k, (s, d) in enumerate(waits(src_refs, land_refs)):
            cp = _remote(s, d, send_sem.at[k], recv_sem.at[k], (x, y, c))
            cp.wait_send()
            cp.wait_recv()

    out = pl.pallas_call(
        body,
        name=name,
        out_shape=tuple(pltpu.HBM(s.shape, s.dtype) for s in (*srcs, *lands)),
        in_specs=[HBM] * (n + m) + [SEM, SEM, ANY],
        out_specs=tuple([HBM] * (n + m)),
        input_output_aliases={i: i for i in range(n + m)},
        compiler_params=pltpu.CompilerParams(has_side_effects=EFFECT),
    )(*srcs, *lands, send, recv, after)
    return list(out[:n]), list(out[n:])


def _gather_start(shards, after, name):
    def copies(src, land):
        x, y, c = _place()
        j = 2 * x + y
        return [(src[a].at[c], land[a].at[j, c], (px, py, c)) for a in range(len(shards)) for px, py in _other_chips(x, y)]

    shapes = [jax.ShapeDtypeStruct((4,) + s.shape, s.dtype) for s in shards]
    return _bulk_start(name, shards, shapes, 3 * len(shards), copies, after)


def _gather_wait(started, after, name):
    send, recv, srcs, lands, _ = started

    def waits(src, land):
        x, y, c = _place()
        return [(src[a].at[c], land[a].at[2 * px + py, c]) for a in range(len(srcs)) for px, py in _other_chips(x, y)]

    return _bulk_wait(name, send, recv, srcs, lands, after, waits)


def _forward_halves(lands, name):
    n = len(lands)

    def body(*refs):
        bufs = refs[n : 2 * n]
        send, recv = refs[2 * n :]
        x, y, c = _place()
        started = []
        for a in range(n):
            for k, (px, py) in enumerate(_other_chips(x, y)):
                blk = bufs[a].at[2 * px + py, c]
                cp = _remote(blk, blk, send.at[3 * a + k], recv.at[3 * a + k], (x, y, 1 - c))
                cp.start()
                started.append(cp)
        for a in range(n):
            for k, (px, py) in enumerate(_other_chips(x, y)):
                blk = bufs[a].at[2 * px + py, 1 - c]
                _remote(blk, blk, send.at[3 * a + k], recv.at[3 * a + k], (x, y, c)).wait_recv()
        for cp in started:
            cp.wait_send()

    return pl.pallas_call(
        body,
        name=name,
        out_shape=[jax.ShapeDtypeStruct(b.shape, b.dtype) for b in lands],
        in_specs=[ANY] * n,
        out_specs=[ANY] * n,
        input_output_aliases={i: i for i in range(n)},
        scratch_shapes=[pltpu.SemaphoreType.DMA((3 * n,)), pltpu.SemaphoreType.DMA((3 * n,))],
    )(*lands)


def _forward_start(lands, after, name):
    def copies(src, _):
        x, y, c = _place()
        blocks = [src[a].at[2 * px + py, c] for a in range(len(lands)) for px, py in _other_chips(x, y)]
        return [(b, b, (x, y, 1 - c)) for b in blocks]

    return _bulk_start(name, lands, [], 3 * len(lands), copies, after)


def _forward_wait(started, after, name):
    send, recv, bufs, _, _ = started

    def waits(src, _):
        x, y, c = _place()
        return [(src[a].at[2 * px + py, c], src[a].at[2 * px + py, 1 - c]) for a in range(len(bufs)) for px, py in _other_chips(x, y)]

    return _bulk_wait(name, send, recv, bufs, [], after, waits)[0]


def _place_own(shards, lands):
    j = 2 * lax.axis_index("x") + lax.axis_index("y")
    full = [lax.dynamic_update_slice(b, s[None], (j, 0, 0, 0)) for b, s in zip(lands, shards)]
    return [f.reshape(4 * f.shape[2] * 2, f.shape[3]) for f in full]


def _gather_finish(started, after, tag):
    shards, lands = _gather_wait(started, after, "gather_wait_" + tag)
    return _place_own(shards, _forward_halves(lands, "gather_forward_" + tag))


def _gather_land(started, after, tag):
    shards, lands = _gather_wait(started, after, "gather_wait_" + tag)
    return shards, _forward_start(lands, shards[0], "forward_start_" + tag)


def _gather_done(landed, after, tag):
    shards, fwd = landed
    return _place_own(shards, _forward_wait(fwd, after, "forward_wait_" + tag))


def _swap_halves(grads, name):
    n = len(grads)

    def body(*refs):
        ins, outs = refs[:n], refs[n : 2 * n]
        send, recv = refs[2 * n :]
        x, y, c = _place()
        started = []
        for a in range(n):
            for s in range(4):
                cp = _remote(ins[a].at[s, 1 - c], outs[a].at[s], send.at[4 * a + s], recv.at[4 * a + s], (x, y, 1 - c))
                cp.start()
                started.append(cp)
        for cp in started:
            cp.wait_recv()
        for cp in started:
            cp.wait_send()

    return pl.pallas_call(
        body,
        name=name,
        out_shape=[jax.ShapeDtypeStruct((4,) + g.shape[2:], g.dtype) for g in grads],
        in_specs=[ANY] * n,
        out_specs=[ANY] * n,
        scratch_shapes=[pltpu.SemaphoreType.DMA((4 * n,)), pltpu.SemaphoreType.DMA((4 * n,))],
    )(*grads)


def _add_halves(grads, others, tag):
    outs = []
    for a, (g, o) in enumerate(zip(grads, others)):
        _, _, rh, cdim = g.shape
        tr = _pick(rh, 512, 16)

        def body(g_ref, o_ref, p_ref):
            c = lax.axis_index("c")
            own = jnp.where(c == 0, g_ref[0].astype(F32), g_ref[1].astype(F32))
            p_ref[...] = (own + o_ref[...].astype(F32)).astype(BF16)

        outs.append(
            pl.pallas_call(
                body,
                name=f"add_halves_{tag}{a}",
                grid=(4, rh // tr),
                in_specs=[pl.BlockSpec((None, 2, tr, cdim), lambda s, i: (s, 0, i, 0)), pl.BlockSpec((None, tr, cdim), lambda s, i: (s, i, 0))],
                out_specs=pl.BlockSpec((None, tr, cdim), lambda s, i: (s, i, 0)),
                out_shape=jax.ShapeDtypeStruct((4, rh, cdim), BF16),
                compiler_params=_params(("parallel", "parallel")),
            )(g, o)
        )
    return outs


def _exchange_start(parts, after, name):
    def copies(src, land):
        x, y, c = _place()
        j = 2 * x + y
        return [(src[a].at[2 * px + py], land[a].at[j], (px, py, c)) for a in range(len(parts)) for px, py in _other_chips(x, y)]

    return _bulk_start(name, parts, [jax.ShapeDtypeStruct(p.shape, p.dtype) for p in parts], 3 * len(parts), copies, after)


def _exchange_finish(started, after, name):
    send, recv, srcs, lands, _ = started

    def waits(src, land):
        x, y, _ = _place()
        return [(src[a].at[2 * px + py], land[a].at[2 * px + py]) for a in range(len(srcs)) for px, py in _other_chips(x, y)]

    srcs, lands = _bulk_wait(name, send, recv, srcs, lands, after, waits)
    j = 2 * lax.axis_index("x") + lax.axis_index("y")
    return [lax.dynamic_update_slice(b, lax.dynamic_slice(p, (j, 0, 0), (1,) + p.shape[1:]), (j, 0, 0)) for b, p in zip(lands, srcs)]


def _sum_chips(recvd, tag):
    outs = []
    for a, g in enumerate(recvd):
        _, rh, cdim = g.shape
        tr = _pick(rh, 512, 16)

        def body(g_ref, o_ref):
            o_ref[...] = ((g_ref[0].astype(F32) + g_ref[1].astype(F32)) + g_ref[2].astype(F32)) + g_ref[3].astype(F32)

        outs.append(
            pl.pallas_call(
                body,
                name=f"sum_chips_{tag}{a}",
                grid=(rh // tr,),
                in_specs=[pl.BlockSpec((4, tr, cdim), lambda i: (0, i, 0))],
                out_specs=pl.BlockSpec((tr, cdim), lambda i: (i, 0)),
                out_shape=jax.ShapeDtypeStruct((rh, cdim), F32),
                compiler_params=_params(("parallel",)),
            )(g)
        )
    return outs


def _join_halves(halves, name):
    n = len(halves)

    def body(*refs):
        ins, outs = refs[:n], refs[n : 2 * n]
        send, recv = refs[2 * n :]
        x, y, c = _place()
        started = []
        for a in range(n):
            cp = _remote(ins[a], outs[a], send.at[a], recv.at[a], (x, y, 1 - c))
            cp.start()
            started.append(cp)
        for cp in started:
            cp.wait_recv()
        for cp in started:
            cp.wait_send()

    others = pl.pallas_call(
        body,
        name=name,
        out_shape=[jax.ShapeDtypeStruct(h.shape, h.dtype) for h in halves],
        in_specs=[ANY] * n,
        out_specs=[ANY] * n,
        scratch_shapes=[pltpu.SemaphoreType.DMA((n,)), pltpu.SemaphoreType.DMA((n,))],
    )(*halves)
    first = lax.axis_index("c") == 0
    return [jnp.concatenate([jnp.where(first, h, o), jnp.where(first, o, h)], axis=0) for h, o in zip(halves, others)]


def _grad_views(grads):
    return [g.reshape(4, 2, g.shape[0] // 8, g.shape[1]) for g in grads]


def _scatter_start(grads, tag):
    views = _grad_views(grads)
    mine = _add_halves(views, _swap_halves(views, "swap_halves_" + tag), tag)
    return _exchange_start(mine, mine[-1], "exchange_start_" + tag)


def _swap_start(grads, after, tag):
    views = _grad_views(grads)

    def copies(src, land):
        x, y, c = _place()
        return [(src[a].at[s, 1 - c], land[a].at[s], (x, y, 1 - c)) for a in range(len(views)) for s in range(4)]

    shapes = [jax.ShapeDtypeStruct((4,) + v.shape[2:], v.dtype) for v in views]
    return _bulk_start("swap_start_" + tag, views, shapes, 4 * len(views), copies, after)


def _scatter_start_after_swap(swapped, after, tag):
    send, recv, views, lands, _ = swapped

    def waits(src, land):
        c = lax.axis_index("c")
        return [(src[a].at[s, 1 - c], land[a].at[s]) for a in range(len(views)) for s in range(4)]

    views, others = _bulk_wait("swap_wait_" + tag, send, recv, views, lands, after, waits)
    mine = _add_halves(views, others, tag)
    return _exchange_start(mine, mine[-1], "exchange_start_" + tag)


def _join_start(halves, after, tag):
    def copies(src, land):
        x, y, c = _place()
        return [(src[a], land[a], (x, y, 1 - c)) for a in range(len(halves))]

    return _bulk_start("join_start_" + tag, halves, [jax.ShapeDtypeStruct(h.shape, h.dtype) for h in halves], len(halves), copies, after)


def _join_wait(started, after, tag):
    send, recv, halves, lands, _ = started
    halves, others = _bulk_wait("join_wait_" + tag, send, recv, halves, lands, after, lambda src, land: list(zip(src, land)))
    first = lax.axis_index("c") == 0
    return [jnp.concatenate([jnp.where(first, h, o), jnp.where(first, o, h)], axis=0) for h, o in zip(halves, others)]


def _scatter_sums(started, after, tag):
    return _sum_chips(_exchange_finish(started, after, "exchange_wait_" + tag), tag)


def _scatter_finish(started, after, tag):
    return _join_halves(_scatter_sums(started, after, tag), "join_halves_" + tag)


def _t_bf16(w):
    return w.T.astype(BF16)


def kernel(x, c, ctx, c_ctx, w_ada, b_ada, norm1_g, w_in, mla_q_norm_g, w_q_up, mla_kv_norm_g, w_kv_up, gqa_q_norm_g, gqa_k_norm_g, w_br_a, w_br_b, w_out, norm2_g, w_up, conv_w, conv_b, w_down, final_norm_g, loss_target, m_c_ctx, m_w_ada, m_b_ada, m_norm1_g, m_w_in, m_mla_q_norm_g, m_w_q_up, m_mla_kv_norm_g, m_w_kv_up, m_gqa_q_norm_g, m_gqa_k_norm_g, m_w_br_a, m_w_br_b, m_w_out, m_norm2_g, m_w_up, m_conv_w, m_conv_b, m_w_down, m_final_norm_g, v_c_ctx, v_w_ada, v_b_ada, v_norm1_g, v_w_in, v_mla_q_norm_g, v_w_q_up, v_mla_kv_norm_g, v_w_kv_up, v_gqa_q_norm_g, v_gqa_k_norm_g, v_w_br_a, v_w_br_b, v_w_out, v_norm2_g, v_w_up, v_conv_w, v_conv_b, v_w_down, v_final_norm_g):
    T, D = x.shape[1], x.shape[2]
    C = ctx.shape[1]
    NA = w_ada.shape[2]
    NW = w_up.shape[2]
    F2 = 4 * NW
    FF = F2 // 2
    xi, yi, ci = _place()
    j = 2 * xi + yi
    me = 4 * xi + 2 * yi + ci
    tr = _pick(C, 128, 8)
    tq = _pick(T, 256)

    x2d, tgt, ctx2d = x[0], loss_target[0], ctx[0]
    fg = final_norm_g.reshape(1, D)
    cc = c_ctx.reshape(1, D)

    halve = lambda s: s.reshape(2, s.shape[0] // 2, s.shape[1])
    ag_in = _gather_start([halve(_t_bf16(w_in[0]))], c, "gather_start_in")

    w0 = max(D, NW)
    pay = jnp.zeros((8, w0), F32).at[0:1, :D].set(c + ag_in[4]).at[1:4, :NW].set(conv_w[0])
    got = _all_gather_small(pay, "gather_cond")
    c_all = got[:, 0, :D]
    cw = jnp.concatenate([got[2 * s, 1:4, :NW] for s in range(4)], axis=1)
    s16 = jnp.concatenate([c_all, cc, jnp.zeros((7, D), F32)], axis=0)
    b_cols = lax.dynamic_slice(b_ada, (0, j * NA), (1, NA))
    ada_part = _mm(s16, w_ada[0], "NN", F32, "ada_fwd", act="silu", bias=b_cols)
    got = _all_gather_small(ada_part, "gather_ada")
    ada = jnp.concatenate([got[2 * s] for s in range(4)], axis=1)
    lat = lax.dynamic_slice(ada, (me, 0), (1, 6 * D))
    sh1, sc1, g1, sh2, sc2, g2 = [lat[:, k * D : (k + 1) * D] for k in range(6)]
    csh, csc = ada[8:9, :D], ada[8:9, D : 2 * D]

    wq3 = w_q_up[0].reshape(MLA_Q_LORA, 2, MLA_NOPE + MLA_ROPE)
    wq_perm = jnp.concatenate([wq3[:, :, :MLA_NOPE].reshape(MLA_Q_LORA, -1), wq3[:, :, MLA_NOPE:].reshape(MLA_Q_LORA, -1)], axis=1)
    mix = [_t_bf16(wq_perm), _t_bf16(w_kv_up[0]), _t_bf16(w_br_a[0]), _t_bf16(w_br_b[0]), w_out[0].astype(BF16)]
    ffn = [_t_bf16(w_up[0]), w_down[0].astype(BF16)]
    ag_mix = _gather_start([halve(s) for s in mix], got, "gather_start_mix")
    ag_ffn = _gather_start([halve(s) for s in ffn], ag_mix[4], "gather_start_ffn")
    sh1 = sh1 + ag_ffn[4]

    cos_a, ss_a = _rope_tables(C, T, MLA_ROPE)
    cos_b, ss_b = _rope_tables(C, T, GQA_HEAD_DIM)
    lcos_a, lss_a, lcos_b, lss_b = cos_a[:T], ss_a[:T], cos_b[:T], ss_b[:T]

    z_all = _norm_mod_fwd(x2d, norm1_g, sh1, sc1, "norm1_lat_fwd", tr, out_rows=T + C)
    z_all = _norm_mod_fwd(ctx2d, norm1_g, csh, csc, "norm1_ctx_fwd", tr, base=z_all, out_off=T)
    (win_t,) = _gather_finish(ag_in, z_all, "in")
    kv_cols = KVP - LANES + MLA_ROPE
    e_kpe = MLA_KV_LORA + MLA_ROPE
    w_kvp = jnp.concatenate([win_t[:MLA_KV_LORA], win_t[e_kpe:kv_cols], win_t[MLA_KV_LORA:e_kpe], jnp.zeros((LANES - MLA_ROPE, D), BF16)], axis=0)

    pkv = _mm(z_all, w_kvp, "NT", F32, "proj_kv")
    mix_landed = _gather_land(ag_mix, pkv, "mix")
    pq = _mm(z_all, win_t, "NT", F32, "proj_q", m=T, n=QC, b_off=kv_cols, after=mix_landed[1][4])
    pg = _mm(z_all, win_t, "NT", F32, "proj_g", m=T, n=2 * D, b_off=kv_cols + QC)
    wq_t, wkv_t, wbra_t, wbrb_t, wout = _gather_done(mix_landed, pg, "mix")
    ckv_n, kb2, vb2, kpe2 = _kprep_fwd(pkv, mla_kv_norm_g, gqa_k_norm_g, cos_a, ss_a, cos_b, ss_b, tr)
    kv_up = _mm(ckv_n, wkv_t, "NT", BF16, "kv_up")
    cq_n, qb2 = _qprep_fwd(pq, mla_q_norm_g, gqa_q_norm_g, lcos_b, lss_b, tr)
    q_a = _mm(cq_n, wq_t, "NT", F32, "q_up")
    qar = _qrope_fwd(q_a, lcos_a, lss_a, tr)

    a_q = [(qar, lambda h: 3 * (h // 2) + h % 2), (qar, lambda h: 3 * (h // 2) + 2)]
    a_k = [(kv_up, lambda h: 2 * h), (kpe2, lambda h: h % 2)]
    a_v = (kv_up, lambda h: 2 * h + 1)
    a_scale = float(MLA_NOPE + MLA_ROPE) ** -0.5
    b_q = [(qb2, lambda h: h)]
    b_k = [(kb2, lambda h: h)]
    b_v = (vb2, lambda h: h)
    b_scale = float(GQA_HEAD_DIM) ** -0.5
    tq_f = _pick(T, 512)
    o_a, lse_a = _attn_fwd(a_q, a_k, a_v, MLA_HEADS, 1, MLA_V, a_scale, "attn_a_fwd", tq_f)
    ffn_landed = _gather_land(ag_ffn, o_a, "ffn")
    o_b, lse_b = _attn_fwd(b_q, b_k, b_v, GQA_HEADS, GQA_GROUP, GQA_HEAD_DIM, b_scale, "attn_b_fwd", tq_f, after=ffn_landed[1][4])
    ya = _mm(o_a, wbra_t, "NT", F32, "br_a")
    yb = _mm(o_b, wbrb_t, "NT", F32, "br_b")
    merged = _gates_fwd(pg, ya, yb, tr)
    att = _mm(merged, wout, "NN", F32, "out_proj")
    x1, z2 = _resid_norm2_fwd(x2d, att, g1, norm2_g, sh2, sc2, tr)
    wup_t, wdown = _gather_done(ffn_landed, att, "ffn")
    u = _mm(z2, wup_t, "NT", BF16, "ffn_up")
    tc = _pick(FF, 128)
    hg = _conv_fwd(u, cw, conv_b, tc)
    f = _mm(hg, wdown, "NN", F32, "ffn_down")
    sq, dx2, d_fg, d_g2, df = _loss_head(x1, f, g2, fg, tgt, tr)
    loss = lax.psum(0.5 * jnp.sum(sq) / D, ("x", "y", "c"))

    dhg = _mm(df, wdown, "NT", BF16, "ffn_down_dx")
    g_wdown = _mm(hg, df, "TN", BF16, "ffn_down_dw")
    du_a, du_b, dcw_a, dcw_b, dcb_a, dcb_b = _conv_bwd(u, dhg, cw, conv_b, tc)
    dz2 = _mm(du_a, wup_t, "NN", F32, "ffn_up_dx_a")
    dz2 = _mm(du_b, wup_t, "NN", F32, "ffn_up_dx_b", b_off=FF, add=dz2)
    g_wup_t = _mm(du_a, z2, "TN", BF16, "ffn_up_dw_a", out_rows=F2, tm=FF // 4)
    g_wup_t = _mm(du_b, z2, "TN", BF16, "ffn_up_dw_b", out_base=g_wup_t, out_off=FF, tm=FF // 4)
    sw_ffn = _swap_start([g_wdown, g_wup_t], sc2, "ffn")
    sc2 = sc2 + sw_ffn[4]
    dx1, datt, d_n2g, d_sh2, d_sc2, d_g1 = _resid_norm2_bwd(dz2, x1, dx2, att, norm2_g, sc2, g1, tr)

    dmerged = _mm(datt, wout, "NT", F32, "out_proj_dx")
    rs_ffn = _scatter_start_after_swap(sw_ffn, dmerged, "ffn")
    lse_a = lse_a + rs_ffn[4]
    g_wout = _mm(merged, datt, "TN", BF16, "out_proj_dw")
    dya, dyb, dpg = _gates_bwd(dmerged, pg, ya, yb, tr)
    do_a = _mm(dya, wbra_t, "NN", BF16, "br_a_dx")
    g_wbra_t = _mm(dya, o_a, "TN", BF16, "br_a_dw")
    do_b = _mm(dyb, wbrb_t, "NN", BF16, "br_b_dx")
    g_wbrb_t = _mm(dyb, o_b, "TN", BF16, "br_b_dw")
    dqa2, dka2, dva2 = _attn_bwd(a_q, a_k, a_v, o_a, do_a, lse_a, MLA_HEADS, 1, MLA_V, a_scale, "attn_a_bwd", tq)
    dqb2, dkb2, dvb2 = _attn_bwd(b_q, b_k, b_v, o_b, do_b, lse_b, GQA_HEADS, GQA_GROUP, GQA_HEAD_DIM, b_scale, "attn_b_bwd", tq)
    dq_a = _qrope_bwd(dqa2, lcos_a, lss_a, tr)
    dcq_n = _mm(dq_a, wq_t, "NN", F32, "q_up_dx")
    g_wq_t = _mm(dq_a, cq_n, "TN", BF16, "q_up_dw")
    dpq, d_qg, d_gq = _qprep_bwd(pq, dcq_n, dqb2, mla_q_norm_g, gqa_q_norm_g, lcos_b, lss_b, tr)
    dkv_up, dkpe = _kgrad_split(dka2, dva2, cos_a, ss_a, tr)
    dckv_n = _mm(dkv_up, wkv_t, "NN", F32, "kv_up_dx")
    g_wkv_t = _mm(dkv_up, ckv_n, "TN", BF16, "kv_up_dw")
    sw_mix = _swap_start([g_wq_t, g_wkv_t, g_wbra_t, g_wbrb_t, g_wout], mla_kv_norm_g, "mix")
    dpkv, d_kvg, d_kg = _kprep_bwd(pkv, dckv_n, dkb2, dvb2, dkpe, mla_kv_norm_g + sw_mix[4], gqa_k_norm_g, cos_b, ss_b, tr)
    rs_mix = _scatter_start_after_swap(sw_mix, dpkv, "mix")
    dz_kv = _mm(dpkv, w_kvp, "NN", F32, "proj_kv_dx", after=rs_mix[4])
    dz_lat = _mm(dpq, win_t, "NN", F32, "proj_q_dx", b_off=kv_cols, add=dz_kv)
    dz_lat = _mm(dpg, win_t, "NN", F32, "proj_g_dx", b_off=kv_cols + QC, add=dz_lat)
    g_kvp = _mm(dpkv, z_all, "TN", BF16, "proj_kv_dw")
    nk = MLA_KV_LORA + 2 * GQA_KV_HEADS * GQA_HEAD_DIM
    g_kv = jnp.concatenate([g_kvp[:MLA_KV_LORA], g_kvp[nk : nk + MLA_ROPE], g_kvp[MLA_KV_LORA:nk]], axis=0)
    g_win_t = _mm(dpq, z_all, "TN", BF16, "proj_q_dw", out_rows=kv_cols + QC + 2 * D, out_off=kv_cols, tm=QC // 2)
    g_win_t = _mm(dpg, z_all, "TN", BF16, "proj_g_dw", out_base=g_win_t, out_off=kv_cols + QC)
    g_win_t = lax.dynamic_update_slice(g_win_t, g_kv, (0, 0))
    rs_in = _scatter_start([g_win_t], "in")
    csc, sc1 = csc + rs_in[4], sc1 + rs_in[4]
    _, d_n1g_c, d_csh, d_csc = _norm_mod_bwd(dz_kv, T // tr, ctx2d, norm1_g, csc, None, "norm1_ctx_bwd", tr)
    grad_x, d_n1g_l, d_sh1, d_sc1 = _norm_mod_bwd(dz_lat, 0, x2d, norm1_g, sc1, dx1, "norm1_lat_bwd", tr)

    zeros_d = jnp.zeros((1, D), F32)
    d_lat = jnp.concatenate([d_sh1, d_sc1, d_g1, d_sh2, d_sc2, d_g2], axis=1)
    d_ctx_part = jnp.concatenate([d_csh, d_csc], axis=1)
    flat = jnp.concatenate(
        [d_n1g_c + d_n1g_l, d_qg, d_kvg, d_gq, d_kg, d_n2g, dcb_a, dcb_b, d_fg,
         dcw_a.reshape(1, -1), dcw_b.reshape(1, -1), d_ctx_part, d_lat], axis=1)
    n_flat = flat.shape[1]
    n_rows = -(-n_flat // (8 * LANES)) * 8
    flat = jnp.pad(flat, ((0, 0), (0, n_rows * LANES - n_flat))).reshape(n_rows, LANES)
    got = _all_gather_small(flat, "gather_small_grads")
    tot = _sum_slots(got, "sum_small_grads").reshape(1, -1)
    sizes = [D, MLA_Q_LORA, MLA_KV_LORA, GQA_HEAD_DIM, GQA_HEAD_DIM, D, F2, D, 3 * FF, 3 * FF, 2 * D]
    offs = [0]
    for s in sizes:
        offs.append(offs[-1] + s)
    t_n1g, t_qg, t_kvg, t_gq, t_kg, t_n2g, t_cb, t_fg, t_cwa, t_cwb, t_ctx = [tot[:, offs[k] : offs[k + 1]] for k in range(len(sizes))]
    g_cw_full = jnp.concatenate([t_cwa.reshape(3, FF), t_cwb.reshape(3, FF)], axis=1)
    g_cw = lax.dynamic_slice(g_cw_full, (0, j * NW), (3, NW))
    d_lat_all = got.reshape(8, -1)[:, offs[-1] : offs[-1] + 6 * D]
    g16 = jnp.concatenate([d_lat_all, jnp.pad(t_ctx, ((0, 0), (0, 4 * D))), jnp.zeros((7, 6 * D), F32)], axis=0)
    g_b_ada = _sum_slots(g16.reshape(16, 1, 6 * D), "sum_b_ada")
    g16_cols = lax.dynamic_slice(g16, (0, j * NA), (16, NA))
    ds_part = _mm(g16_cols, w_ada[0], "NT", F32, "ada_dx")
    got = _all_gather_small(ds_part[8:16], "gather_ada_dx")
    ds_ctx = _sum_slots(jnp.stack([got[2 * s] for s in range(4)]), "sum_ada_dx")[0:1]
    g_c_ctx = _silu_grad_mul(ds_ctx, cc)

    h_ffn = _scatter_sums(rs_ffn, grad_x, "ffn")
    j_ffn = _join_start(h_ffn, grad_x, "ffn")
    h_mix = _scatter_sums(rs_mix, j_ffn[2][0], "mix")
    j_mix = _join_start(h_mix, j_ffn[2][0], "mix")
    g_w_ada = _mm(s16, g16_cols, "TN", F32, "ada_dw", act="silu", after=j_mix[4])
    _, d_ada, m_ada, v_ada = _adamw(w_ada[0], g_w_ada, m_w_ada[0], v_w_ada[0], "adamw_w_ada")
    r_wdown, r_wup = _join_wait(j_ffn, d_ada, "ffn")
    r_wq, r_wkv, r_wbra, r_wbrb, r_wout = _join_wait(j_mix, d_ada, "mix")
    gq_p = r_wq.T
    gq = jnp.concatenate([gq_p[:, : 2 * MLA_NOPE].reshape(MLA_Q_LORA, 2, MLA_NOPE), gq_p[:, 2 * MLA_NOPE :].reshape(MLA_Q_LORA, 2, MLA_ROPE)], axis=2)
    grads = {
        "c_ctx": g_c_ctx.reshape(D), "w_ada": g_w_ada[None], "b_ada": g_b_ada, "norm1_g": t_n1g,
        "mla_q_norm_g": t_qg, "w_q_up": gq.reshape(1, MLA_Q_LORA, -1), "mla_kv_norm_g": t_kvg, "w_kv_up": r_wkv,
        "gqa_q_norm_g": t_gq, "gqa_k_norm_g": t_kg, "w_br_a": r_wbra, "w_br_b": r_wbrb, "w_out": r_wout[None],
        "norm2_g": t_n2g, "w_up": r_wup, "conv_w": g_cw[None], "conv_b": t_cb, "w_down": r_wdown[None],
        "final_norm_g": t_fg.reshape(D),
    }
    arrives_transposed = ("w_kv_up", "w_br_a", "w_br_b", "w_up")
    weights = dict(c_ctx=c_ctx, w_ada=w_ada, b_ada=b_ada, norm1_g=norm1_g, w_in=w_in, mla_q_norm_g=mla_q_norm_g, w_q_up=w_q_up,
                   mla_kv_norm_g=mla_kv_norm_g, w_kv_up=w_kv_up, gqa_q_norm_g=gqa_q_norm_g, gqa_k_norm_g=gqa_k_norm_g, w_br_a=w_br_a,
                   w_br_b=w_br_b, w_out=w_out, norm2_g=norm2_g, w_up=w_up, conv_w=conv_w, conv_b=conv_b, w_down=w_down,
                   final_norm_g=final_norm_g)
    m_in = dict(c_ctx=m_c_ctx, w_ada=m_w_ada, b_ada=m_b_ada, norm1_g=m_norm1_g, w_in=m_w_in, mla_q_norm_g=m_mla_q_norm_g,
                w_q_up=m_w_q_up, mla_kv_norm_g=m_mla_kv_norm_g, w_kv_up=m_w_kv_up, gqa_q_norm_g=m_gqa_q_norm_g,
                gqa_k_norm_g=m_gqa_k_norm_g, w_br_a=m_w_br_a, w_br_b=m_w_br_b, w_out=m_w_out, norm2_g=m_norm2_g, w_up=m_w_up,
                conv_w=m_conv_w, conv_b=m_conv_b, w_down=m_w_down, final_norm_g=m_final_norm_g)
    v_in = dict(c_ctx=v_c_ctx, w_ada=v_w_ada, b_ada=v_b_ada, norm1_g=v_norm1_g, w_in=v_w_in, mla_q_norm_g=v_mla_q_norm_g,
                w_q_up=v_w_q_up, mla_kv_norm_g=v_mla_kv_norm_g, w_kv_up=v_w_kv_up, gqa_q_norm_g=v_gqa_q_norm_g,
                gqa_k_norm_g=v_gqa_k_norm_g, w_br_a=v_w_br_a, w_br_b=v_w_br_b, w_out=v_w_out, norm2_g=v_norm2_g, w_up=v_w_up,
                conv_w=v_conv_w, conv_b=v_conv_b, w_down=v_w_down, final_norm_g=v_final_norm_g)
    names = list(weights)
    big = [n for n in names if weights[n].ndim == 3 and weights[n].shape[1] >= 8]
    small = [n for n in names if n not in big]
    delta, new_m, new_v = {}, {}, {}

    def update(n):
        shp = weights[n].shape
        two_d = lambda a: a.reshape(shp[1], shp[2])
        g_t = n in arrives_transposed
        g_in = grads[n] if g_t else two_d(grads[n].astype(F32))
        g_, d_, m_, v_ = _adamw(two_d(weights[n]), g_in, two_d(m_in[n]), two_d(v_in[n]), "adamw_" + n, g_transposed=g_t)
        grads[n], delta[n], new_m[n], new_v[n] = g_.reshape(shp), d_.reshape(shp), m_.reshape(shp), v_.reshape(shp)

    delta["w_ada"], new_m["w_ada"], new_v["w_ada"] = d_ada[None], m_ada[None], v_ada[None]
    early = [n for n in big if n not in ("w_in", "w_ada")]
    for n in early:
        update(n)
    done = sum(delta[n][0, 0:1, 0:1] for n in early)
    (r_win,) = _scatter_finish(rs_in, done, "in")
    _, d_, m_, v_ = _adamw(w_in[0].T, r_win, m_w_in[0].T, v_w_in[0].T, "adamw_w_in")
    grads["w_in"], delta["w_in"], new_m["w_in"], new_v["w_in"] = r_win.T[None], d_.T[None], m_.T[None], v_.T[None]
    grads = {n: grads[n].reshape(weights[n].shape).astype(F32) for n in names}

    def pack(tree):
        flat_ = jnp.concatenate([tree[n].reshape(-1) for n in small])
        rows = -(-flat_.shape[0] // (8 * LANES)) * 8
        return jnp.pad(flat_, (0, rows * LANES - flat_.shape[0])).reshape(rows, LANES)

    _, d_, m_, v_ = _adamw(pack(weights), pack(grads), pack(m_in), pack(v_in), "adamw_small")
    off = 0
    for n in small:
        size = weights[n].size
        shp = weights[n].shape
        delta[n] = d_.reshape(-1)[off : off + size].reshape(shp)
        new_m[n] = m_.reshape(-1)[off : off + size].reshape(shp)
        new_v[n] = v_.reshape(-1)[off : off + size].reshape(shp)
        off += size

    return (loss, grad_x[None], *[grads[n] for n in names], *[delta[n] for n in names], *[new_m[n] for n in names],
            *[new_v[n] for n in names])
```

```python
import math

import jax
import jax.numpy as jnp
from jax import lax
from jax.experimental import pallas as pl
from jax.experimental.pallas import tpu as pltpu

F32 = jnp.float32
BF16 = jnp.bfloat16
MESH = pl.DeviceIdType.MESH

NORM_EPS = 1e-6
ROPE_THETA = 10000.0
GRID_W = 64
MLA_HEADS = 8
MLA_Q_LORA = 768
MLA_KV_LORA = 512
MLA_NOPE = 128
MLA_ROPE = 64
MLA_V = 128
GQA_HEADS = 8
GQA_KV_HEADS = 2
GQA_HEAD_DIM = 128
GQA_GROUP = GQA_HEADS // GQA_KV_HEADS
LANES = 128
KVP = MLA_KV_LORA + 2 * GQA_KV_HEADS * GQA_HEAD_DIM + LANES
QC = MLA_Q_LORA + GQA_HEADS * GQA_HEAD_DIM

ADAM_LR = 0.001
ADAM_B1 = 0.9
ADAM_B2 = 0.999
ADAM_EPS = 1e-08
ADAM_WD = 0.01
ADAM_STEP = 10

VMEM_LIMIT = 56 * 1024 * 1024


def _pick(dim, target, mult=LANES):
    t = (min(target, dim) // mult) * mult
    while t >= mult:
        if dim % t == 0:
            return t
        t -= mult
    return dim


def _params(sem):
    return pltpu.CompilerParams(dimension_semantics=sem, vmem_limit_bytes=VMEM_LIMIT)


_DIMS = {"NN": (((1,), (0,)), ((), ())), "NT": (((1,), (1,)), ((), ())), "TN": (((0,), (0,)), ((), ()))}


MM_VMEM_BUDGET = 36 * 1024 * 1024


def _mm_tiles(M, N, K, sa, sb, so, tm, tn, tk):
    tm, tn, tk = _pick(M, tm), _pick(N, tn), _pick(K, tk)

    def need(t):
        return 2 * (tm * t * sa + t * tn * sb) + 2 * tm * tn * so + (tm * tn * 4 if t < K else 0)

    while need(tk) > MM_VMEM_BUDGET and tk > LANES:
        smaller = _pick(K, tk - LANES)
        if smaller >= tk:
            break
        tk = smaller
    return tm, tn, tk


def _window(block, index, offsets):
    if not any(offsets):
        return pl.BlockSpec(block, index)
    for t, o in zip(block, offsets):
        assert o % 16 == 0 and t % 16 == 0, (block, offsets)

    def at(i, j, k):
        return tuple(pl.multiple_of(o + p * t, math.gcd(o, t)) for p, t, o in zip(index(i, j, k), block, offsets))

    return pl.BlockSpec(tuple(pl.Element(t) for t in block), at)


def _mm(a, b, mode, out_dtype, name, m=None, n=None, k=None, b_off=0, add=None, out_rows=None, out_base=None, out_off=0,
        tm=1024, tn=1024, tk=2304, act=None, bias=None, after=None):
    if mode == "NN":
        M, K, N = m or a.shape[0], k or a.shape[1], b.shape[1]
    elif mode == "NT":
        M, K, N = m or a.shape[0], a.shape[1], n or b.shape[0]
    else:
        M, K, N = a.shape[1], k or a.shape[0], b.shape[1]
    tm, tn, tk = _mm_tiles(M, N, K, a.dtype.itemsize, b.dtype.itemsize, jnp.dtype(out_dtype).itemsize, tm, tn, tk)
    nk = K // tk
    dims = _DIMS[mode]
    n_in = 2 + (bias is not None) + (add is not None) + (out_base is not None) + (after is not None)

    def body(*refs):
        a_ref, b_ref = refs[:2]
        bias_ref = refs[2] if bias is not None else None
        add_ref = refs[2 + (bias is not None)] if add is not None else None
        o_ref = refs[n_in]
        av = a_ref[...]
        if act == "silu":
            av = av * jax.nn.sigmoid(av)
        part = lax.dot_general(av.astype(BF16), b_ref[...].astype(BF16), dims, preferred_element_type=F32)

        def finish(r):
            if bias is not None:
                r = r + bias_ref[...]
            if add is not None:
                r = r + add_ref[...]
            o_ref[...] = r.astype(out_dtype)

        if nk == 1:
            finish(part)
            return
        acc = refs[-1]
        k = pl.program_id(2)

        @pl.when(k == 0)
        def _():
            acc[...] = part

        @pl.when(jnp.logical_and(k > 0, k < nk - 1))
        def _():
            acc[...] += part

        @pl.when(k == nk - 1)
        def _():
            finish(acc[...] + part)

    a_spec = pl.BlockSpec((tk, tm), lambda i, j, k: (k, i)) if mode == "TN" else pl.BlockSpec((tm, tk), lambda i, j, k: (i, k))
    if mode == "NT":
        b_spec = _window((tn, tk), lambda i, j, k: (j, k), (b_off, 0))
    else:
        b_spec = _window((tk, tn), lambda i, j, k: (k, j), (b_off, 0))
    in_specs, args = [a_spec, b_spec], [a, b]
    if bias is not None:
        in_specs.append(pl.BlockSpec((1, tn), lambda i, j, k: (0, j)))
        args.append(bias)
    if add is not None:
        in_specs.append(pl.BlockSpec((tm, tn), lambda i, j, k: (i, j)))
        args.append(add)
    aliases = {}
    if after is not None:
        in_specs.append(pl.BlockSpec(after.shape, lambda i, j, k: (0, 0)))
        args.append(after)
    if out_base is not None:
        aliases = {len(args): 0}
        in_specs.append(ANY)
        args.append(out_base)
        out_rows = out_base.shape[0]
    return pl.pallas_call(
        body,
        name=name,
        grid=(M // tm, N // tn, nk),
        in_specs=in_specs,
        out_specs=_window((tm, tn), lambda i, j, k: (i, j), (out_off, 0)),
        out_shape=jax.ShapeDtypeStruct((out_rows or M, N), out_dtype),
        input_output_aliases=aliases,
        scratch_shapes=[pltpu.VMEM((tm, tn), F32)] if nk > 1 else [],
        compiler_params=_params(("parallel", "parallel", "arbitrary")),
    )(*args)


def _rms(x):
    r = lax.rsqrt(jnp.mean(x * x, axis=-1, keepdims=True) + NORM_EPS)
    return x * r, r


def _rms_bwd(xh, r, dxh):
    return r * (dxh - xh * jnp.mean(dxh * xh, axis=-1, keepdims=True))


def _swap(x, q):
    lane = lax.broadcasted_iota(jnp.int32, x.shape, 1)
    even = ((lane // q) % 2) == 0
    return jnp.where(even, pltpu.roll(x, LANES - q, 1), pltpu.roll(x, q, 1))


def _rope(x, cos, ss, q):
    return x * cos + _swap(x, q) * ss


def _rope_t(d, cos, ss, q):
    return d * cos + _swap(d * ss, q)


def _csum(x):
    return jnp.sum(x, axis=0, keepdims=True)


def _rows(tr, w, off=0):
    return pl.BlockSpec((tr, w), lambda i: (i + off, 0))


def _bcast(w):
    return pl.BlockSpec((1, w), lambda i: (0, 0))


def _acc_init(i, refs):
    @pl.when(i == 0)
    def _():
        for r in refs:
            r[...] = jnp.zeros_like(r)


def _rope_tables(n_ctx, n_lat, rot_dim):
    rows = n_lat // GRID_W
    row = jnp.repeat(jnp.arange(rows, dtype=F32), GRID_W)
    col = jnp.tile(jnp.arange(GRID_W, dtype=F32), rows)
    half = rot_dim // 2
    inv_freq = ROPE_THETA ** (-jnp.arange(0, half, 2, dtype=F32) / half)
    ar, ac = row[:, None] * inv_freq, col[:, None] * inv_freq
    cos = jnp.concatenate([jnp.cos(ar), jnp.cos(ar), jnp.cos(ac), jnp.cos(ac)], axis=-1)
    ss = jnp.concatenate([-jnp.sin(ar), jnp.sin(ar), -jnp.sin(ac), jnp.sin(ac)], axis=-1)
    cos = jnp.tile(cos, (1, LANES // rot_dim))
    ss = jnp.tile(ss, (1, LANES // rot_dim))
    cos = jnp.concatenate([cos, jnp.ones((n_ctx, LANES), F32)], axis=0)
    ss = jnp.concatenate([ss, jnp.zeros((n_ctx, LANES), F32)], axis=0)
    return cos, ss


def _norm_mod_fwd(x2d, g, sh, sc, name, tr, out_rows=None, base=None, out_off=0):
    n, d = x2d.shape

    def body(x_ref, g_ref, sh_ref, sc_ref, *rest):
        xh, _ = _rms(x_ref[...])
        rest[-1][...] = ((xh * g_ref[...]) * (1.0 + sc_ref[...]) + sh_ref[...]).astype(BF16)

    args, in_specs, aliases = [x2d, g, sh, sc], [_rows(tr, d), _bcast(d), _bcast(d), _bcast(d)], {}
    if base is not None:
        args.append(base)
        in_specs.append(ANY)
        aliases = {4: 0}
        out_rows = base.shape[0]
    return pl.pallas_call(
        body,
        name=name,
        grid=(n // tr,),
        in_specs=in_specs,
        out_specs=_rows(tr, d, out_off // tr),
        out_shape=jax.ShapeDtypeStruct((out_rows or n, d), BF16),
        input_output_aliases=aliases,
        compiler_params=_params(("parallel",)),
    )(*args)


def _norm_mod_bwd(dz, dz_off, x2d, g, sc, dres, name, tr):
    n, d = x2d.shape
    want_dx = dres is not None

    def body(*refs):
        if want_dx:
            dz_ref, x_ref, g_ref, sc_ref, dres_ref, dx_ref, dg_ref, dsh_ref, dsc_ref = refs
        else:
            dz_ref, x_ref, g_ref, sc_ref, dg_ref, dsh_ref, dsc_ref = refs
        _acc_init(pl.program_id(0), [dg_ref, dsh_ref, dsc_ref])
        xh, r = _rms(x_ref[...])
        dzv = dz_ref[...]
        gv = g_ref[...]
        dsc_ref[...] += _csum(dzv * (xh * gv))
        dsh_ref[...] += _csum(dzv)
        dh = dzv * (1.0 + sc_ref[...])
        dg_ref[...] += _csum(dh * xh)
        if want_dx:
            dx_ref[...] = _rms_bwd(xh, r, dh * gv) + dres_ref[...]

    in_specs = [_rows(tr, d, dz_off), _rows(tr, d), _bcast(d), _bcast(d)]
    args = [dz, x2d, g, sc]
    out_specs = [_bcast(d)] * 3
    out_shape = [jax.ShapeDtypeStruct((1, d), F32)] * 3
    if want_dx:
        in_specs.append(_rows(tr, d))
        args.append(dres)
        out_specs = [_rows(tr, d)] + out_specs
        out_shape = [jax.ShapeDtypeStruct((n, d), F32)] + out_shape
    res = pl.pallas_call(
        body,
        name=name,
        grid=(n // tr,),
        in_specs=in_specs,
        out_specs=out_specs,
        out_shape=out_shape,
        compiler_params=_params(("arbitrary",)),
    )(*args)
    return res if want_dx else (None, *res)


_QA, _QB = MLA_ROPE // 4, GQA_HEAD_DIM // 4


def _kprep_fwd(pkv, kvg, kg, cos_a, ss_a, cos_b, ss_b, tr):
    n = pkv.shape[0]
    nb = GQA_KV_HEADS * GQA_HEAD_DIM

    def body(p_ref, kvg_ref, kg_ref, ca, sa, cb, sb, ckv_ref, kb_ref, vb_ref, kpe_ref):
        p = p_ref[...]
        xh, _ = _rms(p[:, :MLA_KV_LORA])
        ckv_ref[...] = (xh * kvg_ref[...]).astype(BF16)
        for e in range(GQA_KV_HEADS):
            lo = MLA_KV_LORA + e * GQA_HEAD_DIM
            kh, _ = _rms(p[:, lo : lo + GQA_HEAD_DIM])
            kb_ref[:, e * GQA_HEAD_DIM : (e + 1) * GQA_HEAD_DIM] = _rope(kh * kg_ref[...], cb[...], sb[...], _QB).astype(BF16)
        vb_ref[...] = p[:, MLA_KV_LORA + nb : MLA_KV_LORA + 2 * nb].astype(BF16)
        kr = _rope(p[:, MLA_KV_LORA + 2 * nb :], ca[...], sa[...], _QA)
        kpe_ref[:, :LANES] = kr.astype(BF16)
        kpe_ref[:, LANES:] = pltpu.roll(kr, MLA_ROPE, 1).astype(BF16)

    return pl.pallas_call(
        body,
        name="kprep_fwd",
        grid=(n // tr,),
        in_specs=[_rows(tr, KVP), _bcast(MLA_KV_LORA), _bcast(GQA_HEAD_DIM)] + [_rows(tr, LANES)] * 4,
        out_specs=[_rows(tr, MLA_KV_LORA), _rows(tr, nb), _rows(tr, nb), _rows(tr, 2 * LANES)],
        out_shape=[jax.ShapeDtypeStruct((n, w), BF16) for w in (MLA_KV_LORA, nb, nb, 2 * LANES)],
        compiler_params=_params(("parallel",)),
    )(pkv, kvg, kg, cos_a, ss_a, cos_b, ss_b)


def _kprep_bwd(pkv, dckv, dkb, dvb, dkpe, kvg, kg, cos_b, ss_b, tr):
    n = pkv.shape[0]
    nb = GQA_KV_HEADS * GQA_HEAD_DIM

    def body(p_ref, dckv_ref, dkb_ref, dvb_ref, dkpe_ref, kvg_ref, kg_ref, cb, sb, dp_ref, dkvg_ref, dkg_ref):
        _acc_init(pl.program_id(0), [dkvg_ref, dkg_ref])
        p = p_ref[...]
        xh, r = _rms(p[:, :MLA_KV_LORA])
        dn = dckv_ref[...]
        dkvg_ref[...] += _csum(dn * xh)
        dp_ref[:, :MLA_KV_LORA] = _rms_bwd(xh, r, dn * kvg_ref[...]).astype(BF16)
        for e in range(GQA_KV_HEADS):
            lo = MLA_KV_LORA + e * GQA_HEAD_DIM
            kh, rk = _rms(p[:, lo : lo + GQA_HEAD_DIM])
            dk = _rope_t(dkb_ref[:, e * GQA_HEAD_DIM : (e + 1) * GQA_HEAD_DIM], cb[...], sb[...], _QB)
            dkg_ref[...] += _csum(dk * kh)
            dp_ref[:, lo : lo + GQA_HEAD_DIM] = _rms_bwd(kh, rk, dk * kg_ref[...]).astype(BF16)
        dp_ref[:, MLA_KV_LORA + nb : MLA_KV_LORA + 2 * nb] = dvb_ref[...].astype(BF16)
        dp_ref[:, MLA_KV_LORA + 2 * nb :] = dkpe_ref[...].astype(BF16)

    return pl.pallas_call(
        body,
        name="kprep_bwd",
        grid=(n // tr,),
        in_specs=[_rows(tr, KVP), _rows(tr, MLA_KV_LORA), _rows(tr, nb), _rows(tr, nb), _rows(tr, LANES),
                  _bcast(MLA_KV_LORA), _bcast(GQA_HEAD_DIM), _rows(tr, LANES), _rows(tr, LANES)],
        out_specs=[_rows(tr, KVP), _bcast(MLA_KV_LORA), _bcast(GQA_HEAD_DIM)],
        out_shape=[jax.ShapeDtypeStruct((n, KVP), BF16), jax.ShapeDtypeStruct((1, MLA_KV_LORA), F32),
                   jax.ShapeDtypeStruct((1, GQA_HEAD_DIM), F32)],
        compiler_params=_params(("arbitrary",)),
    )(pkv, dckv, dkb, dvb, dkpe, kvg, kg, cos_b, ss_b)


def _kgrad_split(dka, dva, cos_a, ss_a, tr):
    n = dka.shape[0]
    wk = MLA_HEADS * 2 * LANES

    def body(dk_ref, dv_ref, ca, sa, dkv_ref, dkpe_ref):
        even = jnp.zeros((tr, LANES), F32)
        odd = jnp.zeros((tr, LANES), F32)
        for h in range(MLA_HEADS):
            dkv_ref[:, 2 * h * LANES : (2 * h + 1) * LANES] = dk_ref[:, 2 * h * LANES : (2 * h + 1) * LANES].astype(BF16)
            dkv_ref[:, (2 * h + 1) * LANES : (2 * h + 2) * LANES] = dv_ref[:, h * MLA_V : (h + 1) * MLA_V].astype(BF16)
            part = dk_ref[:, (2 * h + 1) * LANES : (2 * h + 2) * LANES]
            if h % 2 == 0:
                even = even + part
            else:
                odd = odd + part
        lane = lax.broadcasted_iota(jnp.int32, (tr, LANES), 1)
        low = lane < MLA_ROPE
        both = jnp.where(low, even, odd)
        tot = jnp.where(low, both + pltpu.roll(both, MLA_ROPE, 1), 0.0)
        dkpe_ref[...] = _rope_t(tot, ca[...], sa[...], _QA)

    return pl.pallas_call(
        body,
        name="kgrad_split",
        grid=(n // tr,),
        in_specs=[_rows(tr, wk), _rows(tr, MLA_HEADS * MLA_V), _rows(tr, LANES), _rows(tr, LANES)],
        out_specs=[_rows(tr, wk), _rows(tr, LANES)],
        out_shape=[jax.ShapeDtypeStruct((n, wk), BF16), jax.ShapeDtypeStruct((n, LANES), F32)],
        compiler_params=_params(("parallel",)),
    )(dka, dva, cos_a, ss_a)


def _qprep_fwd(pq, qg, gq, cos_b, ss_b, tr):
    n = pq.shape[0]
    nq = GQA_HEADS * GQA_HEAD_DIM

    def body(p_ref, qg_ref, gq_ref, cb, sb, cq_ref, qb_ref):
        xh, _ = _rms(p_ref[:, :MLA_Q_LORA])
        cq_ref[...] = (xh * qg_ref[...]).astype(BF16)
        for h in range(GQA_HEADS):
            lo = MLA_Q_LORA + h * GQA_HEAD_DIM
            qh, _ = _rms(p_ref[:, lo : lo + GQA_HEAD_DIM])
            qb_ref[:, h * GQA_HEAD_DIM : (h + 1) * GQA_HEAD_DIM] = _rope(qh * gq_ref[...], cb[...], sb[...], _QB).astype(BF16)

    return pl.pallas_call(
        body,
        name="qprep_fwd",
        grid=(n // tr,),
        in_specs=[_rows(tr, QC), _bcast(MLA_Q_LORA), _bcast(GQA_HEAD_DIM), _rows(tr, LANES), _rows(tr, LANES)],
        out_specs=[_rows(tr, MLA_Q_LORA), _rows(tr, nq)],
        out_shape=[jax.ShapeDtypeStruct((n, MLA_Q_LORA), BF16), jax.ShapeDtypeStruct((n, nq), BF16)],
        compiler_params=_params(("parallel",)),
    )(pq, qg, gq, cos_b, ss_b)


def _qprep_bwd(pq, dcq, dqb, qg, gq, cos_b, ss_b, tr):
    n = pq.shape[0]
    nq = GQA_HEADS * GQA_HEAD_DIM

    def body(p_ref, dcq_ref, dqb_ref, qg_ref, gq_ref, cb, sb, dp_ref, dqg_ref, dgq_ref):
        _acc_init(pl.program_id(0), [dqg_ref, dgq_ref])
        xh, r = _rms(p_ref[:, :MLA_Q_LORA])
        dn = dcq_ref[...]
        dqg_ref[...] += _csum(dn * xh)
        dp_ref[:, :MLA_Q_LORA] = _rms_bwd(xh, r, dn * qg_ref[...]).astype(BF16)
        for h in range(GQA_HEADS):
            lo = MLA_Q_LORA + h * GQA_HEAD_DIM
            qh, rq = _rms(p_ref[:, lo : lo + GQA_HEAD_DIM])
            dq = _rope_t(dqb_ref[:, h * GQA_HEAD_DIM : (h + 1) * GQA_HEAD_DIM], cb[...], sb[...], _QB)
            dgq_ref[...] += _csum(dq * qh)
            dp_ref[:, lo : lo + GQA_HEAD_DIM] = _rms_bwd(qh, rq, dq * gq_ref[...]).astype(BF16)

    return pl.pallas_call(
        body,
        name="qprep_bwd",
        grid=(n // tr,),
        in_specs=[_rows(tr, QC), _rows(tr, MLA_Q_LORA), _rows(tr, nq), _bcast(MLA_Q_LORA), _bcast(GQA_HEAD_DIM),
                  _rows(tr, LANES), _rows(tr, LANES)],
        out_specs=[_rows(tr, QC), _bcast(MLA_Q_LORA), _bcast(GQA_HEAD_DIM)],
        out_shape=[jax.ShapeDtypeStruct((n, QC), BF16), jax.ShapeDtypeStruct((1, MLA_Q_LORA), F32),
                   jax.ShapeDtypeStruct((1, GQA_HEAD_DIM), F32)],
        compiler_params=_params(("arbitrary",)),
    )(pq, dcq, dqb, qg, gq, cos_b, ss_b)


_QA_COLS = MLA_HEADS * (MLA_NOPE + MLA_ROPE)


def _qrope_fwd(qa, cos_a, ss_a, tr):
    n = qa.shape[0]

    def body(q_ref, ca, sa, o_ref):
        for j in range(MLA_HEADS // 2):
            lo = 3 * j * LANES
            o_ref[:, lo : lo + 2 * LANES] = q_ref[:, lo : lo + 2 * LANES].astype(BF16)
            o_ref[:, lo + 2 * LANES : lo + 3 * LANES] = _rope(q_ref[:, lo + 2 * LANES : lo + 3 * LANES], ca[...], sa[...], _QA).astype(BF16)

    return pl.pallas_call(
        body,
        name="qrope_fwd",
        grid=(n // tr,),
        in_specs=[_rows(tr, _QA_COLS), _rows(tr, LANES), _rows(tr, LANES)],
        out_specs=_rows(tr, _QA_COLS),
        out_shape=jax.ShapeDtypeStruct((n, _QA_COLS), BF16),
        compiler_params=_params(("parallel",)),
    )(qa, cos_a, ss_a)


def _qrope_bwd(dq2, cos_a, ss_a, tr):
    n = dq2.shape[0]

    def body(d_ref, ca, sa, o_ref):
        for j in range(MLA_HEADS // 2):
            lo = 3 * j * LANES
            h0, h1 = 2 * j, 2 * j + 1
            o_ref[:, lo : lo + LANES] = d_ref[:, 2 * h0 * LANES : (2 * h0 + 1) * LANES].astype(BF16)
            o_ref[:, lo + LANES : lo + 2 * LANES] = d_ref[:, 2 * h1 * LANES : (2 * h1 + 1) * LANES].astype(BF16)
            pe = d_ref[:, (2 * h0 + 1) * LANES : (2 * h0 + 2) * LANES] + d_ref[:, (2 * h1 + 1) * LANES : (2 * h1 + 2) * LANES]
            o_ref[:, lo + 2 * LANES : lo + 3 * LANES] = _rope_t(pe, ca[...], sa[...], _QA).astype(BF16)

    return pl.pallas_call(
        body,
        name="qrope_bwd",
        grid=(n // tr,),
        in_specs=[_rows(tr, MLA_HEADS * 2 * LANES), _rows(tr, LANES), _rows(tr, LANES)],
        out_specs=_rows(tr, _QA_COLS),
        out_shape=jax.ShapeDtypeStruct((n, _QA_COLS), BF16),
        compiler_params=_params(("parallel",)),
    )(dq2, cos_a, ss_a)


def _cat(refs):
    vals = [r[...] for r in refs]
    return vals[0] if len(vals) == 1 else jnp.concatenate(vals, axis=-1)


LOG2E = 1.4426950408889634


def _attn_fwd(qparts, kparts, vpart, n_heads, group, dv, scale, name, tq, after=None):
    T, Tk = qparts[0][0].shape[0], kparts[0][0].shape[0]
    nq_, nk_ = len(qparts), len(kparts)
    sub = min(tq, 256)
    c2 = scale * LOG2E

    def body(*refs):
        q_refs, k_refs = refs[:nq_], refs[nq_ : nq_ + nk_]
        v_ref = refs[nq_ + nk_]
        o_ref, lse_ref = refs[-2:]
        k = _cat(k_refs)
        v = v_ref[...]
        for r0 in range(0, tq, sub):
            q = _cat([r.at[r0 : r0 + sub, :] for r in q_refs])
            s = lax.dot_general(q, k, _DIMS["NT"], preferred_element_type=F32)
            m = jnp.max(s, axis=-1, keepdims=True)
            p = jnp.exp2((s - m) * c2)
            l = jnp.sum(p, axis=-1, keepdims=True)
            acc = jnp.dot(p.astype(BF16), v, preferred_element_type=F32)
            o_ref[r0 : r0 + sub, :] = (acc * (1.0 / l)).astype(BF16)
            lse_ref[r0 : r0 + sub, :] = m * scale + jnp.log(l)

    in_specs = [pl.BlockSpec((tq, LANES), lambda h, i, f=f: (i, f(h))) for _, f in qparts]
    in_specs += [pl.BlockSpec((Tk, LANES), lambda h, i, f=f: (0, f(h // group))) for _, f in kparts]
    fv = vpart[1]
    in_specs.append(pl.BlockSpec((Tk, dv), lambda h, i: (0, fv(h // group))))
    args = [*[a for a, _ in qparts], *[a for a, _ in kparts], vpart[0]]
    if after is not None:
        in_specs.append(pl.BlockSpec(after.shape, lambda h, i: (0, 0)))
        args.append(after)
    return pl.pallas_call(
        body,
        name=name,
        grid=(n_heads, T // tq),
        in_specs=in_specs,
        out_specs=[pl.BlockSpec((tq, dv), lambda h, i: (i, h)), pl.BlockSpec((None, tq, 1), lambda h, i: (h, i, 0))],
        out_shape=[jax.ShapeDtypeStruct((T, n_heads * dv), BF16), jax.ShapeDtypeStruct((n_heads, T, 1), F32)],
        compiler_params=_params(("parallel", "parallel")),
    )(*args)


def _attn_bwd(qparts, kparts, vpart, o, do, lse, n_heads, group, dv, scale, name, tq):
    T, Tk = qparts[0][0].shape[0], kparts[0][0].shape[0]
    nq_, nk_ = len(qparts), len(kparts)
    dk_ = LANES * nq_
    n_kv = n_heads // group
    nblk = T // tq
    c2 = scale * LOG2E

    def head(hk, i):
        return hk * group + i // nblk

    def body(*refs):
        q = _cat(refs[:nq_])
        k = _cat(refs[nq_ : nq_ + nk_])
        v_ref, o_ref, do_ref, lse_ref, dq_ref, dk_ref, dv_ref = refs[nq_ + nk_ :]
        i = pl.program_id(1)
        _acc_init(i, [dk_ref, dv_ref])
        s = lax.dot_general(q, k, _DIMS["NT"], preferred_element_type=F32)
        p = jnp.exp2(s * c2 - lse_ref[...] * LOG2E)
        dov = do_ref[...]
        dp = lax.dot_general(dov, v_ref[...], _DIMS["NT"], preferred_element_type=F32)
        delta = jnp.sum(dov.astype(F32) * o_ref[...].astype(F32), axis=-1, keepdims=True)
        ds = (p * (dp - delta)).astype(BF16)
        dq_ref[...] = jnp.dot(ds, k, preferred_element_type=F32) * scale
        dk_ref[...] += lax.dot_general(ds, q, _DIMS["TN"], preferred_element_type=F32)
        dv_ref[...] += lax.dot_general(p.astype(BF16), dov, _DIMS["TN"], preferred_element_type=F32)

        @pl.when(i == group * nblk - 1)
        def _():
            dk_ref[...] *= scale

    in_specs = [pl.BlockSpec((tq, LANES), lambda hk, i, f=f: (i % nblk, f(head(hk, i)))) for _, f in qparts]
    in_specs += [pl.BlockSpec((Tk, LANES), lambda hk, i, f=f: (0, f(hk))) for _, f in kparts]
    fv = vpart[1]
    in_specs.append(pl.BlockSpec((Tk, dv), lambda hk, i: (0, fv(hk))))
    in_specs += [pl.BlockSpec((tq, dv), lambda hk, i: (i % nblk, head(hk, i)))] * 2
    in_specs.append(pl.BlockSpec((None, tq, 1), lambda hk, i: (head(hk, i), i % nblk, 0)))
    return pl.pallas_call(
        body,
        name=name,
        grid=(n_kv, group * nblk),
        in_specs=in_specs,
        out_specs=[pl.BlockSpec((tq, dk_), lambda hk, i: (i % nblk, head(hk, i))),
                   pl.BlockSpec((Tk, dk_), lambda hk, i: (0, hk)),
                   pl.BlockSpec((Tk, dv), lambda hk, i: (0, hk))],
        out_shape=[jax.ShapeDtypeStruct((T, n_heads * dk_), F32), jax.ShapeDtypeStruct((Tk, n_kv * dk_), F32),
                   jax.ShapeDtypeStruct((Tk, n_kv * dv), F32)],
        compiler_params=_params(("parallel", "arbitrary")),
    )(*[a for a, _ in qparts], *[a for a, _ in kparts], vpart[0], o, do, lse)


def _gates_fwd(pg, ya, yb, tr):
    n, d = ya.shape

    def body(pg_ref, ya_ref, yb_ref, o_ref):
        ga = jax.nn.sigmoid(pg_ref[:, :d])
        gb = jax.nn.sigmoid(pg_ref[:, d:])
        o_ref[...] = (ga * ya_ref[...] + gb * yb_ref[...]).astype(BF16)

    return pl.pallas_call(
        body,
        name="gates_fwd",
        grid=(n // tr,),
        in_specs=[_rows(tr, 2 * d), _rows(tr, d), _rows(tr, d)],
        out_specs=_rows(tr, d),
        out_shape=jax.ShapeDtypeStruct((n, d), BF16),
        compiler_params=_params(("parallel",)),
    )(pg, ya, yb)


def _gates_bwd(dm, pg, ya, yb, tr):
    n, d = ya.shape

    def body(dm_ref, pg_ref, ya_ref, yb_ref, dya_ref, dyb_ref, dpg_ref):
        dmv = dm_ref[...]
        ga = jax.nn.sigmoid(pg_ref[:, :d])
        gb = jax.nn.sigmoid(pg_ref[:, d:])
        dya_ref[...] = (dmv * ga).astype(BF16)
        dyb_ref[...] = (dmv * gb).astype(BF16)
        dpg_ref[:, :d] = (dmv * ya_ref[...] * ga * (1.0 - ga)).astype(BF16)
        dpg_ref[:, d:] = (dmv * yb_ref[...] * gb * (1.0 - gb)).astype(BF16)

    return pl.pallas_call(
        body,
        name="gates_bwd",
        grid=(n // tr,),
        in_specs=[_rows(tr, d), _rows(tr, 2 * d), _rows(tr, d), _rows(tr, d)],
        out_specs=[_rows(tr, d), _rows(tr, d), _rows(tr, 2 * d)],
        out_shape=[jax.ShapeDtypeStruct((n, d), BF16), jax.ShapeDtypeStruct((n, d), BF16), jax.ShapeDtypeStruct((n, 2 * d), BF16)],
        compiler_params=_params(("parallel",)),
    )(dm, pg, ya, yb)


def _resid_norm2_fwd(x2d, att, g1, n2g, sh2, sc2, tr):
    n, d = x2d.shape

    def body(x_ref, a_ref, g1_ref, g_ref, sh_ref, sc_ref, x1_ref, z_ref):
        x1 = x_ref[...] + g1_ref[...] * a_ref[...]
        x1_ref[...] = x1
        xh, _ = _rms(x1)
        z_ref[...] = ((xh * g_ref[...]) * (1.0 + sc_ref[...]) + sh_ref[...]).astype(BF16)

    return pl.pallas_call(
        body,
        name="resid_norm2_fwd",
        grid=(n // tr,),
        in_specs=[_rows(tr, d), _rows(tr, d)] + [_bcast(d)] * 4,
        out_specs=[_rows(tr, d), _rows(tr, d)],
        out_shape=[jax.ShapeDtypeStruct((n, d), F32), jax.ShapeDtypeStruct((n, d), BF16)],
        compiler_params=_params(("parallel",)),
    )(x2d, att, g1, n2g, sh2, sc2)


def _resid_norm2_bwd(dz2, x1, dx2, att, n2g, sc2, g1, tr):
    n, d = x1.shape

    def body(dz_ref, x1_ref, dx2_ref, a_ref, g_ref, sc_ref, g1_ref, dx1_ref, da_ref, dg_ref, dsh_ref, dsc_ref, dg1_ref):
        _acc_init(pl.program_id(0), [dg_ref, dsh_ref, dsc_ref, dg1_ref])
        xh, r = _rms(x1_ref[...])
        dzv = dz_ref[...]
        gv = g_ref[...]
        dsc_ref[...] += _csum(dzv * (xh * gv))
        dsh_ref[...] += _csum(dzv)
        dh = dzv * (1.0 + sc_ref[...])
        dg_ref[...] += _csum(dh * xh)
        dx1 = _rms_bwd(xh, r, dh * gv) + dx2_ref[...]
        dx1_ref[...] = dx1
        dg1_ref[...] += _csum(dx1 * a_ref[...])
        da_ref[...] = (dx1 * g1_ref[...]).astype(BF16)

    return pl.pallas_call(
        body,
        name="resid_norm2_bwd",
        grid=(n // tr,),
        in_specs=[_rows(tr, d)] * 4 + [_bcast(d)] * 3,
        out_specs=[_rows(tr, d), _rows(tr, d)] + [_bcast(d)] * 4,
        out_shape=[jax.ShapeDtypeStruct((n, d), F32), jax.ShapeDtypeStruct((n, d), BF16)] + [jax.ShapeDtypeStruct((1, d), F32)] * 4,
        compiler_params=_params(("arbitrary",)),
    )(dz2, x1, dx2, att, n2g, sc2, g1)


def _shift_prev(u):
    row = lax.broadcasted_iota(jnp.int32, u.shape, 0)
    return jnp.where(row == 0, 0.0, pltpu.roll(u, 1, 0))


def _shift_next(u):
    n = u.shape[0]
    row = lax.broadcasted_iota(jnp.int32, u.shape, 0)
    return jnp.where(row == n - 1, 0.0, pltpu.roll(u, n - 1, 0))


def _conv3(u, w_ref, b_ref):
    return b_ref[...] + w_ref[0:1, :] * _shift_prev(u) + w_ref[1:2, :] * u + w_ref[2:3, :] * _shift_next(u)


def _conv_fwd(u, cw, cb, tc):
    n, two_f = u.shape
    f = two_f // 2
    nb = f // tc

    def body(ua_ref, ub_ref, wa_ref, wb_ref, ba_ref, bb_ref, h_ref):
        a = _conv3(ua_ref[...].astype(F32), wa_ref, ba_ref)
        b = _conv3(ub_ref[...].astype(F32), wb_ref, bb_ref)
        h_ref[...] = (a * jax.nn.sigmoid(a) * b).astype(BF16)

    col = lambda rows, off: pl.BlockSpec((rows, tc), lambda i: (0, i + off))
    return pl.pallas_call(
        body,
        name="conv_fwd",
        grid=(nb,),
        in_specs=[col(n, 0), col(n, nb), col(3, 0), col(3, nb), col(1, 0), col(1, nb)],
        out_specs=col(n, 0),
        out_shape=jax.ShapeDtypeStruct((n, f), BF16),
        compiler_params=_params(("parallel",)),
    )(u, u, cw, cw, cb, cb)


def _conv_bwd(u, dh, cw, cb, tc):
    n, two_f = u.shape
    f = two_f // 2
    nb = f // tc

    def part(uv, duc, w_ref, du_ref, dw_ref, db_ref):
        db_ref[...] = _csum(duc)
        dw_ref[0:1, :] = _csum(duc * _shift_prev(uv))
        dw_ref[1:2, :] = _csum(duc * uv)
        dw_ref[2:3, :] = _csum(duc * _shift_next(uv))
        du_ref[...] = (w_ref[0:1, :] * _shift_next(duc) + w_ref[1:2, :] * duc + w_ref[2:3, :] * _shift_prev(duc)).astype(BF16)

    def body(ua_ref, ub_ref, dh_ref, wa_ref, wb_ref, ba_ref, bb_ref, dua_ref, dub_ref, dwa_ref, dwb_ref, dba_ref, dbb_ref):
        ua = ua_ref[...].astype(F32)
        ub = ub_ref[...].astype(F32)
        a = _conv3(ua, wa_ref, ba_ref)
        b = _conv3(ub, wb_ref, bb_ref)
        dhv = dh_ref[...].astype(F32)
        sg = jax.nn.sigmoid(a)
        da = dhv * b * (sg * (1.0 + a * (1.0 - sg)))
        db = dhv * (a * sg)
        part(ua, da, wa_ref, dua_ref, dwa_ref, dba_ref)
        part(ub, db, wb_ref, dub_ref, dwb_ref, dbb_ref)

    col = lambda rows, off: pl.BlockSpec((rows, tc), lambda i: (0, i + off))
    return pl.pallas_call(
        body,
        name="conv_bwd",
        grid=(nb,),
        in_specs=[col(n, 0), col(n, nb), col(n, 0), col(3, 0), col(3, nb), col(1, 0), col(1, nb)],
        out_specs=[col(n, 0), col(n, 0), col(3, 0), col(3, 0), col(1, 0), col(1, 0)],
        out_shape=[jax.ShapeDtypeStruct((n, f), BF16)] * 2 + [jax.ShapeDtypeStruct((3, f), F32)] * 2 + [jax.ShapeDtypeStruct((1, f), F32)] * 2,
        compiler_params=_params(("parallel",)),
    )(u, u, dh, cw, cw, cb, cb)


def _loss_head(x1, f, g2, fg, tgt, tr):
    n, d = x1.shape

    def body(x1_ref, f_ref, g2_ref, fg_ref, t_ref, sq_ref, dx2_ref, dfg_ref, dg2_ref, df_ref):
        _acc_init(pl.program_id(0), [sq_ref, dfg_ref, dg2_ref])
        fv = f_ref[...]
        xh, r = _rms(x1_ref[...] + g2_ref[...] * fv)
        err = xh * fg_ref[...] - t_ref[...]
        sq_ref[...] += _csum(err * err)
        dy = err * (1.0 / d)
        dfg_ref[...] += _csum(dy * xh)
        dx2 = _rms_bwd(xh, r, dy * fg_ref[...])
        dx2_ref[...] = dx2
        dg2_ref[...] += _csum(dx2 * fv)
        df_ref[...] = (dx2 * g2_ref[...]).astype(BF16)

    return pl.pallas_call(
        body,
        name="loss_head",
        grid=(n // tr,),
        in_specs=[_rows(tr, d), _rows(tr, d), _bcast(d), _bcast(d), _rows(tr, d)],
        out_specs=[_bcast(d), _rows(tr, d), _bcast(d), _bcast(d), _rows(tr, d)],
        out_shape=[jax.ShapeDtypeStruct((1, d), F32), jax.ShapeDtypeStruct((n, d), F32), jax.ShapeDtypeStruct((1, d), F32),
                   jax.ShapeDtypeStruct((1, d), F32), jax.ShapeDtypeStruct((n, d), BF16)],
        compiler_params=_params(("arbitrary",)),
    )(x1, f, g2, fg, tgt)


def _sum_slots(g, name):
    s, r, w = g.shape

    def body(g_ref, o_ref):
        acc = g_ref[0]
        for k in range(1, s):
            acc = acc + g_ref[k]
        o_ref[...] = acc

    return pl.pallas_call(body, name=name, out_shape=jax.ShapeDtypeStruct((r, w), F32))(g)


def _silu_grad_mul(ds, cvec):
    def body(d_ref, c_ref, o_ref):
        cv = c_ref[...]
        sg = jax.nn.sigmoid(cv)
        o_ref[...] = d_ref[...] * (sg * (1.0 + cv * (1.0 - sg)))

    return pl.pallas_call(body, name="silu_grad_mul", out_shape=jax.ShapeDtypeStruct(ds.shape, F32))(ds, cvec)


def _adamw(w, g, m, v, name, g_transposed=False):
    r, cdim = w.shape
    tr = _pick(r, 1024, LANES if g_transposed else 8)
    tc = _pick(cdim, max(LANES, (1 << 19) // tr))
    b1c = 1.0 - ADAM_B1**ADAM_STEP
    b2c = 1.0 - ADAM_B2**ADAM_STEP

    def body(w_ref, g_ref, m_ref, v_ref, *outs):
        d_ref, mo_ref, vo_ref = outs[-3:]
        gv = g_ref[...]
        if g_transposed:
            gv = gv.T
            outs[0][...] = gv
        mn = ADAM_B1 * m_ref[...] + (1.0 - ADAM_B1) * gv
        vn = ADAM_B2 * v_ref[...] + (1.0 - ADAM_B2) * (gv * gv)
        mo_ref[...] = mn
        vo_ref[...] = vn
        d_ref[...] = -ADAM_LR * ((mn / b1c) / (jnp.sqrt(vn / b2c) + ADAM_EPS) + ADAM_WD * w_ref[...])

    spec = pl.BlockSpec((tr, tc), lambda i, j: (i, j))
    g_spec = pl.BlockSpec((tc, tr), lambda i, j: (j, i)) if g_transposed else spec
    n_out = 4 if g_transposed else 3
    res = pl.pallas_call(
        body,
        name=name,
        grid=(r // tr, cdim // tc),
        in_specs=[spec, g_spec, spec, spec],
        out_specs=[spec] * n_out,
        out_shape=[jax.ShapeDtypeStruct((r, cdim), F32)] * n_out,
        compiler_params=_params(("parallel", "parallel")),
    )(w, g, m, v)
    return res if g_transposed else [g, *res]


def _place():
    return lax.axis_index("x"), lax.axis_index("y"), lax.axis_index("c")


def _remote(src, dst, send_sem, recv_sem, dev):
    return pltpu.make_async_remote_copy(src_ref=src, dst_ref=dst, send_sem=send_sem, recv_sem=recv_sem, device_id=dev, device_id_type=MESH)


ANY = pl.BlockSpec(memory_space=pl.ANY)


def _all_gather_small(v, name):
    r, w = v.shape

    def body(v_ref, o_ref, send, recv, lsem):
        x, y, c = _place()
        me = 4 * x + 2 * y + c
        mine = pltpu.make_async_copy(v_ref, o_ref.at[me], lsem)
        mine.start()
        sent = []
        for k in range(1, 8):
            px, py, pc = x ^ (k >> 2), y ^ ((k >> 1) & 1), c ^ (k & 1)
            cp = _remote(v_ref, o_ref.at[me], send.at[k - 1], recv.at[k - 1], (px, py, pc))
            cp.start()
            sent.append(cp)
        for k in range(1, 8):
            px, py, pc = x ^ (k >> 2), y ^ ((k >> 1) & 1), c ^ (k & 1)
            slot = o_ref.at[4 * px + 2 * py + pc]
            _remote(slot, slot, send.at[k - 1], recv.at[k - 1], (x, y, c)).wait_recv()
        for cp in sent:
            cp.wait_send()
        mine.wait()

    return pl.pallas_call(
        body,
        name=name,
        out_shape=jax.ShapeDtypeStruct((8, r, w), F32),
        in_specs=[pl.BlockSpec(memory_space=pltpu.VMEM)],
        out_specs=pl.BlockSpec(memory_space=pltpu.VMEM),
        scratch_shapes=[pltpu.SemaphoreType.DMA((7,)), pltpu.SemaphoreType.DMA((7,)), pltpu.SemaphoreType.DMA],
        compiler_params=pltpu.CompilerParams(vmem_limit_bytes=VMEM_LIMIT),
    )(v)


HBM = pl.BlockSpec(memory_space=pltpu.HBM)
SEM = pl.BlockSpec(memory_space=pltpu.SEMAPHORE)
EFFECT = pltpu.SideEffectType.DATAFLOW_SIDE_EFFECTING


def _other_chips(x, y):
    return [(1 - x, y), (x, 1 - y), (1 - x, 1 - y)]


def _bulk_start(name, srcs, land_shapes, n_copies, copies, after):
    n, m = len(srcs), len(land_shapes)

    def body(*refs):
        src_refs, land_refs = refs[:n], refs[n : n + m]
        send, recv = refs[n + m + 1], refs[n + m + 2]
        token = refs[-1]
        for k, (s, d, dev) in enumerate(copies(src_refs, land_refs)):
            _remote(s, d, send.at[k], recv.at[k], dev).start()
        token[...] = jnp.zeros_like(token)

    lands = [pltpu.with_memory_space_constraint(lax.empty(s.shape, s.dtype), pltpu.HBM) for s in land_shapes]
    out = pl.pallas_call(
        body,
        name=name,
        out_shape=(pltpu.SemaphoreType.DMA((n_copies,)), pltpu.SemaphoreType.DMA((n_copies,)),
                   *[pltpu.HBM(s.shape, s.dtype) for s in srcs], *[pltpu.HBM(s.shape, s.dtype) for s in land_shapes],
                   jax.ShapeDtypeStruct((8, LANES), F32)),
        in_specs=[HBM] * (n + m) + [ANY],
        out_specs=(SEM, SEM, *[HBM] * (n + m), pl.BlockSpec(memory_space=pltpu.VMEM)),
        input_output_aliases={i: 2 + i for i in range(n + m)},
        compiler_params=pltpu.CompilerParams(has_side_effects=EFFECT),
    )(*[pltpu.with_memory_space_constraint(s, pltpu.HBM) for s in srcs], *lands, after)
    return out[0], out[1], list(out[2 : 2 + n]), list(out[2 + n : 2 + n + m]), out[-1][0:1, 0:1]


def _bulk_wait(name, send, recv, srcs, lands, after, waits):
    n, m = len(srcs), len(lands)

    def body(*refs):
        src_refs, land_refs = refs[:n], refs[n : n + m]
        send_sem, recv_sem = refs[n + m], refs[n + m + 1]
        x, y, c = _place()
        for k, (s, d) in enumerate(waits(src_refs, land_refs)):
            cp = _remote(s, d, send_sem.at[k], recv_sem.at[k], (x, y, c))
            cp.wait_send()
            cp.wait_recv()

    out = pl.pallas_call(
        body,
        name=name,
        out_shape=tuple(pltpu.HBM(s.shape, s.dtype) for s in (*srcs, *lands)),
        in_specs=[HBM] * (n + m) + [SEM, SEM, ANY],
        out_specs=tuple([HBM] * (n + m)),
        input_output_aliases={i: i for i in range(n + m)},
        compiler_params=pltpu.CompilerParams(has_side_effects=EFFECT),
    )(*srcs, *lands, send, recv, after)
    return list(out[:n]), list(out[n:])


def _gather_start(shards, after, name):
    def copies(src, land):
        x, y, c = _place()
        j = 2 * x + y
        return [(src[a].at[c], land[a].at[j, c], (px, py, c)) for a in range(len(shards)) for px, py in _other_chips(x, y)]

    shapes = [jax.ShapeDtypeStruct((4,) + s.shape, s.dtype) for s in shards]
    return _bulk_start(name, shards, shapes, 3 * len(shards), copies, after)


def _gather_wait(started, after, name):
    send, recv, srcs, lands, _ = started

    def waits(src, land):
        x, y, c = _place()
        return [(src[a].at[c], land[a].at[2 * px + py, c]) for a in range(len(srcs)) for px, py in _other_chips(x, y)]

    return _bulk_wait(name, send, recv, srcs, lands, after, waits)


def _forward_halves(lands, name):
    n = len(lands)

    def body(*refs):
        bufs = refs[n : 2 * n]
        send, recv = refs[2 * n :]
        x, y, c = _place()
        started = []
        for a in range(n):
            for k, (px, py) in enumerate(_other_chips(x, y)):
                blk = bufs[a].at[2 * px + py, c]
                cp = _remote(blk, blk, send.at[3 * a + k], recv.at[3 * a + k], (x, y, 1 - c))
                cp.start()
                started.append(cp)
        for a in range(n):
            for k, (px, py) in enumerate(_other_chips(x, y)):
                blk = bufs[a].at[2 * px + py, 1 - c]
                _remote(blk, blk, send.at[3 * a + k], recv.at[3 * a + k], (x, y, c)).wait_recv()
        for cp in started:
            cp.wait_send()

    return pl.pallas_call(
        body,
        name=name,
        out_shape=[jax.ShapeDtypeStruct(b.shape, b.dtype) for b in lands],
        in_specs=[ANY] * n,
        out_specs=[ANY] * n,
        input_output_aliases={i: i for i in range(n)},
        scratch_shapes=[pltpu.SemaphoreType.DMA((3 * n,)), pltpu.SemaphoreType.DMA((3 * n,))],
    )(*lands)


def _forward_start(lands, after, name):
    def copies(src, _):
        x, y, c = _place()
        blocks = [src[a].at[2 * px + py, c] for a in range(len(lands)) for px, py in _other_chips(x, y)]
        return [(b, b, (x, y, 1 - c)) for b in blocks]

    return _bulk_start(name, lands, [], 3 * len(lands), copies, after)


def _forward_wait(started, after, name):
    send, recv, bufs, _, _ = started

    def waits(src, _):
        x, y, c = _place()
        return [(src[a].at[2 * px + py, c], src[a].at[2 * px + py, 1 - c]) for a in range(len(bufs)) for px, py in _other_chips(x, y)]

    return _bulk_wait(name, send, recv, bufs, [], after, waits)[0]


def _place_own(shards, lands):
    j = 2 * lax.axis_index("x") + lax.axis_index("y")
    full = [lax.dynamic_update_slice(b, s[None], (j, 0, 0, 0)) for b, s in zip(lands, shards)]
    return [f.reshape(4 * f.shape[2] * 2, f.shape[3]) for f in full]


def _gather_finish(started, after, tag):
    shards, lands = _gather_wait(started, after, "gather_wait_" + tag)
    return _place_own(shards, _forward_halves(lands, "gather_forward_" + tag))


def _gather_land(started, after, tag):
    shards, lands = _gather_wait(started, after, "gather_wait_" + tag)
    return shards, _forward_start(lands, shards[0], "forward_start_" + tag)


def _gather_done(landed, after, tag):
    shards, fwd = landed
    return _place_own(shards, _forward_wait(fwd, after, "forward_wait_" + tag))


def _swap_halves(grads, name):
    n = len(grads)

    def body(*refs):
        ins, outs = refs[:n], refs[n : 2 * n]
        send, recv = refs[2 * n :]
        x, y, c = _place()
        started = []
        for a in range(n):
            for s in range(4):
                cp = _remote(ins[a].at[s, 1 - c], outs[a].at[s], send.at[4 * a + s], recv.at[4 * a + s], (x, y, 1 - c))
                cp.start()
                started.append(cp)
        for cp in started:
            cp.wait_recv()
        for cp in started:
            cp.wait_send()

    return pl.pallas_call(
        body,
        name=name,
        out_shape=[jax.ShapeDtypeStruct((4,) + g.shape[2:], g.dtype) for g in grads],
        in_specs=[ANY] * n,
        out_specs=[ANY] * n,
        scratch_shapes=[pltpu.SemaphoreType.DMA((4 * n,)), pltpu.SemaphoreType.DMA((4 * n,))],
    )(*grads)


def _add_halves(grads, others, tag):
    outs = []
    for a, (g, o) in enumerate(zip(grads, others)):
        _, _, rh, cdim = g.shape
        tr = _pick(rh, 512, 16)

        def body(g_ref, o_ref, p_ref):
            c = lax.axis_index("c")
            own = jnp.where(c == 0, g_ref[0].astype(F32), g_ref[1].astype(F32))
            p_ref[...] = (own + o_ref[...].astype(F32)).astype(BF16)

        outs.append(
            pl.pallas_call(
                body,
                name=f"add_halves_{tag}{a}",
                grid=(4, rh // tr),
                in_specs=[pl.BlockSpec((None, 2, tr, cdim), lambda s, i: (s, 0, i, 0)), pl.BlockSpec((None, tr, cdim), lambda s, i: (s, i, 0))],
                out_specs=pl.BlockSpec((None, tr, cdim), lambda s, i: (s, i, 0)),
                out_shape=jax.ShapeDtypeStruct((4, rh, cdim), BF16),
                compiler_params=_params(("parallel", "parallel")),
            )(g, o)
        )
    return outs


def _exchange_start(parts, after, name):
    def copies(src, land):
        x, y, c = _place()
        j = 2 * x + y
        return [(src[a].at[2 * px + py], land[a].at[j], (px, py, c)) for a in range(len(parts)) for px, py in _other_chips(x, y)]

    return _bulk_start(name, parts, [jax.ShapeDtypeStruct(p.shape, p.dtype) for p in parts], 3 * len(parts), copies, after)


def _exchange_finish(started, after, name):
    send, recv, srcs, lands, _ = started

    def waits(src, land):
        x, y, _ = _place()
        return [(src[a].at[2 * px + py], land[a].at[2 * px + py]) for a in range(len(srcs)) for px, py in _other_chips(x, y)]

    srcs, lands = _bulk_wait(name, send, recv, srcs, lands, after, waits)
    j = 2 * lax.axis_index("x") + lax.axis_index("y")
    return [lax.dynamic_update_slice(b, lax.dynamic_slice(p, (j, 0, 0), (1,) + p.shape[1:]), (j, 0, 0)) for b, p in zip(lands, srcs)]


def _sum_chips(recvd, tag):
    outs = []
    for a, g in enumerate(recvd):
        _, rh, cdim = g.shape
        tr = _pick(rh, 512, 16)

        def body(g_ref, o_ref):
            o_ref[...] = ((g_ref[0].astype(F32) + g_ref[1].astype(F32)) + g_ref[2].astype(F32)) + g_ref[3].astype(F32)

        outs.append(
            pl.pallas_call(
                body,
                name=f"sum_chips_{tag}{a}",
                grid=(rh // tr,),
                in_specs=[pl.BlockSpec((4, tr, cdim), lambda i: (0, i, 0))],
                out_specs=pl.BlockSpec((tr, cdim), lambda i: (i, 0)),
                out_shape=jax.ShapeDtypeStruct((rh, cdim), F32),
                compiler_params=_params(("parallel",)),
            )(g)
        )
    return outs


def _join_halves(halves, name):
    n = len(halves)

    def body(*refs):
        ins, outs = refs[:n], refs[n : 2 * n]
        send, recv = refs[2 * n :]
        x, y, c = _place()
        started = []
        for a in range(n):
            cp = _remote(ins[a], outs[a], send.at[a], recv.at[a], (x, y, 1 - c))
            cp.start()
            started.append(cp)
        for cp in started:
            cp.wait_recv()
        for cp in started:
            cp.wait_send()

    others = pl.pallas_call(
        body,
        name=name,
        out_shape=[jax.ShapeDtypeStruct(h.shape, h.dtype) for h in halves],
        in_specs=[ANY] * n,
        out_specs=[ANY] * n,
        scratch_shapes=[pltpu.SemaphoreType.DMA((n,)), pltpu.SemaphoreType.DMA((n,))],
    )(*halves)
    first = lax.axis_index("c") == 0
    return [jnp.concatenate([jnp.where(first, h, o), jnp.where(first, o, h)], axis=0) for h, o in zip(halves, others)]


def _grad_views(grads):
    return [g.reshape(4, 2, g.shape[0] // 8, g.shape[1]) for g in grads]


def _scatter_start(grads, tag, after=None):
    views = _grad_views(grads)
    mine = _add_halves(views, _swap_halves(views, "swap_halves_" + tag), tag)
    return _exchange_start(mine, mine[-1] if after is None else after, "exchange_start_" + tag)


def _swap_start(grads, after, tag):
    views = _grad_views(grads)

    def copies(src, land):
        x, y, c = _place()
        return [(src[a].at[s, 1 - c], land[a].at[s], (x, y, 1 - c)) for a in range(len(views)) for s in range(4)]

    shapes = [jax.ShapeDtypeStruct((4,) + v.shape[2:], v.dtype) for v in views]
    return _bulk_start("swap_start_" + tag, views, shapes, 4 * len(views), copies, after)


def _scatter_start_after_swap(swapped, after, tag):
    send, recv, views, lands, _ = swapped

    def waits(src, land):
        c = lax.axis_index("c")
        return [(src[a].at[s, 1 - c], land[a].at[s]) for a in range(len(views)) for s in range(4)]

    views, others = _bulk_wait("swap_wait_" + tag, send, recv, views, lands, after, waits)
    mine = _add_halves(views, others, tag)
    return _exchange_start(mine, mine[-1], "exchange_start_" + tag)


def _join_start(halves, after, tag):
    def copies(src, land):
        x, y, c = _place()
        return [(src[a], land[a], (x, y, 1 - c)) for a in range(len(halves))]

    return _bulk_start("join_start_" + tag, halves, [jax.ShapeDtypeStruct(h.shape, h.dtype) for h in halves], len(halves), copies, after)


def _join_wait(started, after, tag):
    send, recv, halves, lands, _ = started
    halves, others = _bulk_wait("join_wait_" + tag, send, recv, halves, lands, after, lambda src, land: list(zip(src, land)))
    first = lax.axis_index("c") == 0
    return [jnp.concatenate([jnp.where(first, h, o), jnp.where(first, o, h)], axis=0) for h, o in zip(halves, others)]


def _scatter_sums(started, after, tag):
    return _sum_chips(_exchange_finish(started, after, "exchange_wait_" + tag), tag)


def _scatter_finish(started, after, tag):
    return _join_halves(_scatter_sums(started, after, tag), "join_halves_" + tag)


def _t_bf16(w):
    return w.T.astype(BF16)


def kernel(x, c, ctx, c_ctx, w_ada, b_ada, norm1_g, w_in, mla_q_norm_g, w_q_up, mla_kv_norm_g, w_kv_up, gqa_q_norm_g, gqa_k_norm_g, w_br_a, w_br_b, w_out, norm2_g, w_up, conv_w, conv_b, w_down, final_norm_g, loss_target, m_c_ctx, m_w_ada, m_b_ada, m_norm1_g, m_w_in, m_mla_q_norm_g, m_w_q_up, m_mla_kv_norm_g, m_w_kv_up, m_gqa_q_norm_g, m_gqa_k_norm_g, m_w_br_a, m_w_br_b, m_w_out, m_norm2_g, m_w_up, m_conv_w, m_conv_b, m_w_down, m_final_norm_g, v_c_ctx, v_w_ada, v_b_ada, v_norm1_g, v_w_in, v_mla_q_norm_g, v_w_q_up, v_mla_kv_norm_g, v_w_kv_up, v_gqa_q_norm_g, v_gqa_k_norm_g, v_w_br_a, v_w_br_b, v_w_out, v_norm2_g, v_w_up, v_conv_w, v_conv_b, v_w_down, v_final_norm_g):
    T, D = x.shape[1], x.shape[2]
    C = ctx.shape[1]
    NA = w_ada.shape[2]
    NW = w_up.shape[2]
    F2 = 4 * NW
    FF = F2 // 2
    xi, yi, ci = _place()
    j = 2 * xi + yi
    me = 4 * xi + 2 * yi + ci
    tr = _pick(C, 128, 8)
    tq = _pick(T, 256)

    x2d, tgt, ctx2d = x[0], loss_target[0], ctx[0]
    fg = final_norm_g.reshape(1, D)
    cc = c_ctx.reshape(1, D)

    halve = lambda s: s.reshape(2, s.shape[0] // 2, s.shape[1])
    win_shard = halve(_t_bf16(w_in[0]))
    w0 = max(D, NW)
    pay = jnp.zeros((8, w0), F32).at[0:1, :D].set(c).at[1:4, :NW].set(conv_w[0])
    got = _all_gather_small(pay, "gather_cond")
    c_all = got[:, 0, :D]
    cw = jnp.concatenate([got[2 * s, 1:4, :NW] for s in range(4)], axis=1)
    s16 = jnp.concatenate([c_all, cc, jnp.zeros((7, D), F32)], axis=0)
    b_cols = lax.dynamic_slice(b_ada, (0, j * NA), (1, NA))
    ada_part = _mm(s16, w_ada[0], "NN", F32, "ada_fwd", act="silu", bias=b_cols)
    got = _all_gather_small(ada_part, "gather_ada")
    ada = jnp.concatenate([got[2 * s] for s in range(4)], axis=1)
    lat = lax.dynamic_slice(ada, (me, 0), (1, 6 * D))
    sh1, sc1, g1, sh2, sc2, g2 = [lat[:, k * D : (k + 1) * D] for k in range(6)]
    csh, csc = ada[8:9, :D], ada[8:9, D : 2 * D]

    ag_in = _gather_start([win_shard], got, "gather_start_in")
    t_in = ag_in[4]
    wq3 = (w_q_up[0] + t_in).reshape(MLA_Q_LORA, 2, MLA_NOPE + MLA_ROPE)
    wq_perm = jnp.concatenate([wq3[:, :, :MLA_NOPE].reshape(MLA_Q_LORA, -1), wq3[:, :, MLA_NOPE:].reshape(MLA_Q_LORA, -1)], axis=1)
    mix = [_t_bf16(wq_perm), _t_bf16(w_kv_up[0] + t_in), _t_bf16(w_br_a[0] + t_in), _t_bf16(w_br_b[0] + t_in), (w_out[0] + t_in).astype(BF16)]
    ffn = [_t_bf16(w_up[0] + t_in), (w_down[0] + t_in).astype(BF16)]
    ag_mix = _gather_start([halve(s) for s in mix], t_in, "gather_start_mix")
    ag_ffn = _gather_start([halve(s) for s in ffn], ag_mix[4], "gather_start_ffn")
    sh1 = sh1 + ag_ffn[4]

    cos_a, ss_a = _rope_tables(C, T, MLA_ROPE)
    cos_b, ss_b = _rope_tables(C, T, GQA_HEAD_DIM)
    lcos_a, lss_a, lcos_b, lss_b = cos_a[:T], ss_a[:T], cos_b[:T], ss_b[:T]

    z_all = _norm_mod_fwd(x2d, norm1_g, sh1, sc1, "norm1_lat_fwd", tr, out_rows=T + C)
    z_all = _norm_mod_fwd(ctx2d, norm1_g, csh, csc, "norm1_ctx_fwd", tr, base=z_all, out_off=T)
    (win_t,) = _gather_finish(ag_in, z_all, "in")
    kv_cols = KVP - LANES + MLA_ROPE
    e_kpe = MLA_KV_LORA + MLA_ROPE
    w_kvp = jnp.concatenate([win_t[:MLA_KV_LORA], win_t[e_kpe:kv_cols], win_t[MLA_KV_LORA:e_kpe], jnp.zeros((LANES - MLA_ROPE, D), BF16)], axis=0)

    pkv = _mm(z_all, w_kvp, "NT", F32, "proj_kv")
    mix_landed = _gather_land(ag_mix, pkv, "mix")
    pq = _mm(z_all, win_t, "NT", F32, "proj_q", m=T, n=QC, b_off=kv_cols, after=mix_landed[1][4])
    pg = _mm(z_all, win_t, "NT", F32, "proj_g", m=T, n=2 * D, b_off=kv_cols + QC)
    wq_t, wkv_t, wbra_t, wbrb_t, wout = _gather_done(mix_landed, pg, "mix")
    ckv_n, kb2, vb2, kpe2 = _kprep_fwd(pkv, mla_kv_norm_g, gqa_k_norm_g, cos_a, ss_a, cos_b, ss_b, tr)
    kv_up = _mm(ckv_n, wkv_t, "NT", BF16, "kv_up")
    cq_n, qb2 = _qprep_fwd(pq, mla_q_norm_g, gqa_q_norm_g, lcos_b, lss_b, tr)
    q_a = _mm(cq_n, wq_t, "NT", F32, "q_up")
    qar = _qrope_fwd(q_a, lcos_a, lss_a, tr)

    a_q = [(qar, lambda h: 3 * (h // 2) + h % 2), (qar, lambda h: 3 * (h // 2) + 2)]
    a_k = [(kv_up, lambda h: 2 * h), (kpe2, lambda h: h % 2)]
    a_v = (kv_up, lambda h: 2 * h + 1)
    a_scale = float(MLA_NOPE + MLA_ROPE) ** -0.5
    b_q = [(qb2, lambda h: h)]
    b_k = [(kb2, lambda h: h)]
    b_v = (vb2, lambda h: h)
    b_scale = float(GQA_HEAD_DIM) ** -0.5
    tq_f = _pick(T, 512)
    o_a, lse_a = _attn_fwd(a_q, a_k, a_v, MLA_HEADS, 1, MLA_V, a_scale, "attn_a_fwd", tq_f)
    ffn_landed = _gather_land(ag_ffn, o_a, "ffn")
    o_b, lse_b = _attn_fwd(b_q, b_k, b_v, GQA_HEADS, GQA_GROUP, GQA_HEAD_DIM, b_scale, "attn_b_fwd", tq_f, after=ffn_landed[1][4])
    ya = _mm(o_a, wbra_t, "NT", F32, "br_a")
    yb = _mm(o_b, wbrb_t, "NT", F32, "br_b")
    merged = _gates_fwd(pg, ya, yb, tr)
    att = _mm(merged, wout, "NN", F32, "out_proj")
    x1, z2 = _resid_norm2_fwd(x2d, att, g1, norm2_g, sh2, sc2, tr)
    wup_t, wdown = _gather_done(ffn_landed, att, "ffn")
    u = _mm(z2, wup_t, "NT", BF16, "ffn_up")
    tc = _pick(FF, 128)
    hg = _conv_fwd(u, cw, conv_b, tc)
    f = _mm(hg, wdown, "NN", F32, "ffn_down")
    sq, dx2, d_fg, d_g2, df = _loss_head(x1, f, g2, fg, tgt, tr)
    loss = lax.psum(0.5 * jnp.sum(sq) / D, ("x", "y", "c"))

    dhg = _mm(df, wdown, "NT", BF16, "ffn_down_dx")
    g_wdown = _mm(hg, df, "TN", BF16, "ffn_down_dw")
    du_a, du_b, dcw_a, dcw_b, dcb_a, dcb_b = _conv_bwd(u, dhg, cw, conv_b, tc)
    dz2 = _mm(du_a, wup_t, "NN", F32, "ffn_up_dx_a")
    dz2 = _mm(du_b, wup_t, "NN", F32, "ffn_up_dx_b", b_off=FF, add=dz2)
    g_wup_t = _mm(du_a, z2, "TN", BF16, "ffn_up_dw_a", out_rows=F2, tm=FF // 4)
    g_wup_t = _mm(du_b, z2, "TN", BF16, "ffn_up_dw_b", out_base=g_wup_t, out_off=FF, tm=FF // 4)
    sw_ffn = _swap_start([g_wdown, g_wup_t], sc2, "ffn")
    sc2 = sc2 + sw_ffn[4]
    dx1, datt, d_n2g, d_sh2, d_sc2, d_g1 = _resid_norm2_bwd(dz2, x1, dx2, att, norm2_g, sc2, g1, tr)

    dmerged = _mm(datt, wout, "NT", F32, "out_proj_dx")
    rs_ffn = _scatter_start_after_swap(sw_ffn, dmerged, "ffn")
    lse_a = lse_a + rs_ffn[4]
    g_wout = _mm(merged, datt, "TN", BF16, "out_proj_dw")
    dya, dyb, dpg = _gates_bwd(dmerged, pg, ya, yb, tr)
    do_a = _mm(dya, wbra_t, "NN", BF16, "br_a_dx")
    g_wbra_t = _mm(dya, o_a, "TN", BF16, "br_a_dw")
    do_b = _mm(dyb, wbrb_t, "NN", BF16, "br_b_dx")
    g_wbrb_t = _mm(dyb, o_b, "TN", BF16, "br_b_dw")
    dqa2, dka2, dva2 = _attn_bwd(a_q, a_k, a_v, o_a, do_a, lse_a, MLA_HEADS, 1, MLA_V, a_scale, "attn_a_bwd", tq)
    dqb2, dkb2, dvb2 = _attn_bwd(b_q, b_k, b_v, o_b, do_b, lse_b, GQA_HEADS, GQA_GROUP, GQA_HEAD_DIM, b_scale, "attn_b_bwd", tq)
    dq_a = _qrope_bwd(dqa2, lcos_a, lss_a, tr)
    dcq_n = _mm(dq_a, wq_t, "NN", F32, "q_up_dx")
    g_wq_t = _mm(dq_a, cq_n, "TN", BF16, "q_up_dw")
    dpq, d_qg, d_gq = _qprep_bwd(pq, dcq_n, dqb2, mla_q_norm_g, gqa_q_norm_g, lcos_b, lss_b, tr)
    dkv_up, dkpe = _kgrad_split(dka2, dva2, cos_a, ss_a, tr)
    dckv_n = _mm(dkv_up, wkv_t, "NN", F32, "kv_up_dx")
    g_wkv_t = _mm(dkv_up, ckv_n, "TN", BF16, "kv_up_dw")
    rs_mix = _scatter_start([g_wq_t, g_wkv_t, g_wbra_t, g_wbrb_t, g_wout], "mix")
    dpkv, d_kvg, d_kg = _kprep_bwd(pkv, dckv_n, dkb2, dvb2, dkpe, mla_kv_norm_g + rs_mix[4], gqa_k_norm_g, cos_b, ss_b, tr)
    dz_kv = _mm(dpkv, w_kvp, "NN", F32, "proj_kv_dx")
    dz_lat = _mm(dpq, win_t, "NN", F32, "proj_q_dx", b_off=kv_cols, add=dz_kv)
    dz_lat = _mm(dpg, win_t, "NN", F32, "proj_g_dx", b_off=kv_cols + QC, add=dz_lat)
    _, d_n1g_c, d_csh, d_csc = _norm_mod_bwd(dz_kv, T // tr, ctx2d, norm1_g, csc, None, "norm1_ctx_bwd", tr)
    grad_x, d_n1g_l, d_sh1, d_sc1 = _norm_mod_bwd(dz_lat, 0, x2d, norm1_g, sc1, dx1, "norm1_lat_bwd", tr)

    zeros_d = jnp.zeros((1, D), F32)
    d_lat = jnp.concatenate([d_sh1, d_sc1, d_g1, d_sh2, d_sc2, d_g2], axis=1)
    d_ctx_part = jnp.concatenate([d_csh, d_csc], axis=1)
    flat = jnp.concatenate(
        [d_n1g_c + d_n1g_l, d_qg, d_kvg, d_gq, d_kg, d_n2g, dcb_a, dcb_b, d_fg,
         dcw_a.reshape(1, -1), dcw_b.reshape(1, -1), d_ctx_part, d_lat], axis=1)
    n_flat = flat.shape[1]
    n_rows = -(-n_flat // (8 * LANES)) * 8
    flat = jnp.pad(flat, ((0, 0), (0, n_rows * LANES - n_flat))).reshape(n_rows, LANES)
    got = _all_gather_small(flat, "gather_small_grads")
    tot = _sum_slots(got, "sum_small_grads").reshape(1, -1)
    sizes = [D, MLA_Q_LORA, MLA_KV_LORA, GQA_HEAD_DIM, GQA_HEAD_DIM, D, F2, D, 3 * FF, 3 * FF, 2 * D]
    offs = [0]
    for s in sizes:
        offs.append(offs[-1] + s)
    t_n1g, t_qg, t_kvg, t_gq, t_kg, t_n2g, t_cb, t_fg, t_cwa, t_cwb, t_ctx = [tot[:, offs[k] : offs[k + 1]] for k in range(len(sizes))]
    g_cw_full = jnp.concatenate([t_cwa.reshape(3, FF), t_cwb.reshape(3, FF)], axis=1)
    g_cw = lax.dynamic_slice(g_cw_full, (0, j * NW), (3, NW))
    d_lat_all = got.reshape(8, -1)[:, offs[-1] : offs[-1] + 6 * D]
    g16 = jnp.concatenate([d_lat_all, jnp.pad(t_ctx, ((0, 0), (0, 4 * D))), jnp.zeros((7, 6 * D), F32)], axis=0)
    g_b_ada = _sum_slots(g16.reshape(16, 1, 6 * D), "sum_b_ada")
    g16_cols = lax.dynamic_slice(g16, (0, j * NA), (16, NA))
    ds_part = _mm(g16_cols, w_ada[0], "NT", F32, "ada_dx")
    got = _all_gather_small(ds_part[8:16], "gather_ada_dx")
    ds_ctx = _sum_slots(jnp.stack([got[2 * s] for s in range(4)]), "sum_ada_dx")[0:1]
    g_c_ctx = _silu_grad_mul(ds_ctx, cc)

    g_kvp = _mm(dpkv, z_all, "TN", BF16, "proj_kv_dw")
    nk = MLA_KV_LORA + 2 * GQA_KV_HEADS * GQA_HEAD_DIM
    g_kv = jnp.concatenate([g_kvp[:MLA_KV_LORA], g_kvp[nk : nk + MLA_ROPE], g_kvp[MLA_KV_LORA:nk]], axis=0)
    g_win_t = _mm(dpq, z_all, "TN", BF16, "proj_q_dw", out_rows=kv_cols + QC + 2 * D, out_off=kv_cols, tm=QC // 2)
    g_win_t = _mm(dpg, z_all, "TN", BF16, "proj_g_dw", out_base=g_win_t, out_off=kv_cols + QC)
    g_win_t = lax.dynamic_update_slice(g_win_t, g_kv, (0, 0))
    rs_in = _scatter_start([g_win_t], "in", after=got)

    h_ffn = _scatter_sums(rs_ffn, rs_in[2][0], "ffn")
    j_ffn = _join_start(h_ffn, grad_x, "ffn")
    h_mix = _scatter_sums(rs_mix, j_ffn[2][0], "mix")
    j_mix = _join_start(h_mix, j_ffn[2][0], "mix")
    g_w_ada = _mm(s16, g16_cols, "TN", F32, "ada_dw", act="silu", after=j_mix[4])
    _, d_ada, m_ada, v_ada = _adamw(w_ada[0], g_w_ada, m_w_ada[0], v_w_ada[0], "adamw_w_ada")
    r_wdown, r_wup = _join_wait(j_ffn, d_ada, "ffn")
    r_wq, r_wkv, r_wbra, r_wbrb, r_wout = _join_wait(j_mix, d_ada, "mix")
    gq_p = r_wq.T
    gq = jnp.concatenate([gq_p[:, : 2 * MLA_NOPE].reshape(MLA_Q_LORA, 2, MLA_NOPE), gq_p[:, 2 * MLA_NOPE :].reshape(MLA_Q_LORA, 2, MLA_ROPE)], axis=2)
    grads = {
        "c_ctx": g_c_ctx.reshape(D), "w_ada": g_w_ada[None], "b_ada": g_b_ada, "norm1_g": t_n1g,
        "mla_q_norm_g": t_qg, "w_q_up": gq.reshape(1, MLA_Q_LORA, -1), "mla_kv_norm_g": t_kvg, "w_kv_up": r_wkv,
        "gqa_q_norm_g": t_gq, "gqa_k_norm_g": t_kg, "w_br_a": r_wbra, "w_br_b": r_wbrb, "w_out": r_wout[None],
        "norm2_g": t_n2g, "w_up": r_wup, "conv_w": g_cw[None], "conv_b": t_cb, "w_down": r_wdown[None],
        "final_norm_g": t_fg.reshape(D),
    }
    arrives_transposed = ("w_kv_up", "w_br_a", "w_br_b", "w_up")
    weights = dict(c_ctx=c_ctx, w_ada=w_ada, b_ada=b_ada, norm1_g=norm1_g, w_in=w_in, mla_q_norm_g=mla_q_norm_g, w_q_up=w_q_up,
                   mla_kv_norm_g=mla_kv_norm_g, w_kv_up=w_kv_up, gqa_q_norm_g=gqa_q_norm_g, gqa_k_norm_g=gqa_k_norm_g, w_br_a=w_br_a,
                   w_br_b=w_br_b, w_out=w_out, norm2_g=norm2_g, w_up=w_up, conv_w=conv_w, conv_b=conv_b, w_down=w_down,
                   final_norm_g=final_norm_g)
    m_in = dict(c_ctx=m_c_ctx, w_ada=m_w_ada, b_ada=m_b_ada, norm1_g=m_norm1_g, w_in=m_w_in, mla_q_norm_g=m_mla_q_norm_g,
                w_q_up=m_w_q_up, mla_kv_norm_g=m_mla_kv_norm_g, w_kv_up=m_w_kv_up, gqa_q_norm_g=m_gqa_q_norm_g,
                gqa_k_norm_g=m_gqa_k_norm_g, w_br_a=m_w_br_a, w_br_b=m_w_br_b, w_out=m_w_out, norm2_g=m_norm2_g, w_up=m_w_up,
                conv_w=m_conv_w, conv_b=m_conv_b, w_down=m_w_down, final_norm_g=m_final_norm_g)
    v_in = dict(c_ctx=v_c_ctx, w_ada=v_w_ada, b_ada=v_b_ada, norm1_g=v_norm1_g, w_in=v_w_in, mla_q_norm_g=v_mla_q_norm_g,
                w_q_up=v_w_q_up, mla_kv_norm_g=v_mla_kv_norm_g, w_kv_up=v_w_kv_up, gqa_q_norm_g=v_gqa_q_norm_g,
                gqa_k_norm_g=v_gqa_k_norm_g, w_br_a=v_w_br_a, w_br_b=v_w_br_b, w_out=v_w_out, norm2_g=v_norm2_g, w_up=v_w_up,
                conv_w=v_conv_w, conv_b=v_conv_b, w_down=v_w_down, final_norm_g=v_final_norm_g)
    names = list(weights)
    big = [n for n in names if weights[n].ndim == 3 and weights[n].shape[1] >= 8]
    small = [n for n in names if n not in big]
    delta, new_m, new_v = {}, {}, {}

    def update(n):
        shp = weights[n].shape
        two_d = lambda a: a.reshape(shp[1], shp[2])
        g_t = n in arrives_transposed
        g_in = grads[n] if g_t else two_d(grads[n].astype(F32))
        g_, d_, m_, v_ = _adamw(two_d(weights[n]), g_in, two_d(m_in[n]), two_d(v_in[n]), "adamw_" + n, g_transposed=g_t)
        grads[n], delta[n], new_m[n], new_v[n] = g_.reshape(shp), d_.reshape(shp), m_.reshape(shp), v_.reshape(shp)

    delta["w_ada"], new_m["w_ada"], new_v["w_ada"] = d_ada[None], m_ada[None], v_ada[None]
    early = [n for n in big if n not in ("w_in", "w_ada")]
    for n in early:
        update(n)
    done = sum(delta[n][0, 0:1, 0:1] for n in early)
    (r_win,) = _scatter_finish(rs_in, done, "in")
    _, d_, m_, v_ = _adamw(w_in[0].T, r_win, m_w_in[0].T, v_w_in[0].T, "adamw_w_in")
    grads["w_in"], delta["w_in"], new_m["w_in"], new_v["w_in"] = r_win.T[None], d_.T[None], m_.T[None], v_.T[None]
    grads = {n: grads[n].reshape(weights[n].shape).astype(F32) for n in names}

    def pack(tree):
        flat_ = jnp.concatenate([tree[n].reshape(-1) for n in small])
        rows = -(-flat_.shape[0] // (8 * LANES)) * 8
        return jnp.pad(flat_, (0, rows * LANES - flat_.shape[0])).reshape(rows, LANES)

    _, d_, m_, v_ = _adamw(pack(weights), pack(grads), pack(m_in), pack(v_in), "adamw_small")
    off = 0
    for n in small:
        size = weights[n].size
        shp = weights[n].shape
        delta[n] = d_.reshape(-1)[off : off + size].reshape(shp)
        new_m[n] = m_.reshape(-1)[off : off + size].reshape(shp)
        new_v[n] = v_.reshape(-1)[off : off + size].reshape(shp)
        off += size

    return (loss, grad_x[None], *[grads[n] for n in names], *[delta[n] for n in names], *[new_m[n] for n in names],
            *[new_v[n] for n in names])
```

```python
import math

import jax
import jax.numpy as jnp
from jax import lax
from jax.experimental import pallas as pl
from jax.experimental.pallas import tpu as pltpu

F32 = jnp.float32
BF16 = jnp.bfloat16
MESH = pl.DeviceIdType.MESH

NORM_EPS = 1e-6
ROPE_THETA = 10000.0
GRID_W = 64
MLA_HEADS = 8
MLA_Q_LORA = 768
MLA_KV_LORA = 512
MLA_NOPE = 128
MLA_ROPE = 64
MLA_V = 128
GQA_HEADS = 8
GQA_KV_HEADS = 2
GQA_HEAD_DIM = 128
GQA_GROUP = GQA_HEADS // GQA_KV_HEADS
LANES = 128
KVP = MLA_KV_LORA + 2 * GQA_KV_HEADS * GQA_HEAD_DIM + LANES
QC = MLA_Q_LORA + GQA_HEADS * GQA_HEAD_DIM

ADAM_LR = 0.001
ADAM_B1 = 0.9
ADAM_B2 = 0.999
ADAM_EPS = 1e-08
ADAM_WD = 0.01
ADAM_STEP = 10

VMEM_LIMIT = 56 * 1024 * 1024


def _pick(dim, target, mult=LANES):
    t = (min(target, dim) // mult) * mult
    while t >= mult:
        if dim % t == 0:
            return t
        t -= mult
    return dim


def _params(sem):
    return pltpu.CompilerParams(dimension_semantics=sem, vmem_limit_bytes=VMEM_LIMIT)


_DIMS = {"NN": (((1,), (0,)), ((), ())), "NT": (((1,), (1,)), ((), ())), "TN": (((0,), (0,)), ((), ()))}


MM_VMEM_BUDGET = 36 * 1024 * 1024


def _mm_tiles(M, N, K, sa, sb, so, tm, tn, tk):
    tm, tn, tk = _pick(M, tm), _pick(N, tn), _pick(K, tk)

    def need(t):
        return 2 * (tm * t * sa + t * tn * sb) + 2 * tm * tn * so + (tm * tn * 4 if t < K else 0)

    while need(tk) > MM_VMEM_BUDGET and tk > LANES:
        smaller = _pick(K, tk - LANES)
        if smaller >= tk:
            break
        tk = smaller
    return tm, tn, tk


def _window(block, index, offsets):
    if not any(offsets):
        return pl.BlockSpec(block, index)
    for t, o in zip(block, offsets):
        assert o % 16 == 0 and t % 16 == 0, (block, offsets)

    def at(i, j, k):
        return tuple(pl.multiple_of(o + p * t, math.gcd(o, t)) for p, t, o in zip(index(i, j, k), block, offsets))

    return pl.BlockSpec(tuple(pl.Element(t) for t in block), at)


def _mm(a, b, mode, out_dtype, name, m=None, n=None, k=None, b_off=0, add=None, out_rows=None, out_base=None, out_off=0,
        tm=1024, tn=1024, tk=2304, act=None, bias=None, after=None):
    if mode == "NN":
        M, K, N = m or a.shape[0], k or a.shape[1], b.shape[1]
    elif mode == "NT":
        M, K, N = m or a.shape[0], a.shape[1], n or b.shape[0]
    else:
        M, K, N = a.shape[1], k or a.shape[0], b.shape[1]
    tm, tn, tk = _mm_tiles(M, N, K, a.dtype.itemsize, b.dtype.itemsize, jnp.dtype(out_dtype).itemsize, tm, tn, tk)
    nk = K // tk
    dims = _DIMS[mode]
    n_in = 2 + (bias is not None) + (add is not None) + (out_base is not None) + (after is not None)

    def body(*refs):
        a_ref, b_ref = refs[:2]
        bias_ref = refs[2] if bias is not None else None
        add_ref = refs[2 + (bias is not None)] if add is not None else None
        o_ref = refs[n_in]
        av = a_ref[...]
        if act == "silu":
            av = av * jax.nn.sigmoid(av)
        part = lax.dot_general(av.astype(BF16), b_ref[...].astype(BF16), dims, preferred_element_type=F32)

        def finish(r):
            if bias is not None:
                r = r + bias_ref[...]
            if add is not None:
                r = r + add_ref[...]
            o_ref[...] = r.astype(out_dtype)

        if nk == 1:
            finish(part)
            return
        acc = refs[-1]
        k = pl.program_id(2)

        @pl.when(k == 0)
        def _():
            acc[...] = part

        @pl.when(jnp.logical_and(k > 0, k < nk - 1))
        def _():
            acc[...] += part

        @pl.when(k == nk - 1)
        def _():
            finish(acc[...] + part)

    a_spec = pl.BlockSpec((tk, tm), lambda i, j, k: (k, i)) if mode == "TN" else pl.BlockSpec((tm, tk), lambda i, j, k: (i, k))
    if mode == "NT":
        b_spec = _window((tn, tk), lambda i, j, k: (j, k), (b_off, 0))
    else:
        b_spec = _window((tk, tn), lambda i, j, k: (k, j), (b_off, 0))
    in_specs, args = [a_spec, b_spec], [a, b]
    if bias is not None:
        in_specs.append(pl.BlockSpec((1, tn), lambda i, j, k: (0, j)))
        args.append(bias)
    if add is not None:
        in_specs.append(pl.BlockSpec((tm, tn), lambda i, j, k: (i, j)))
        args.append(add)
    aliases = {}
    if after is not None:
        in_specs.append(pl.BlockSpec(after.shape, lambda i, j, k: (0, 0)))
        args.append(after)
    if out_base is not None:
        aliases = {len(args): 0}
        in_specs.append(ANY)
        args.append(out_base)
        out_rows = out_base.shape[0]
    return pl.pallas_call(
        body,
        name=name,
        grid=(M // tm, N // tn, nk),
        in_specs=in_specs,
        out_specs=_window((tm, tn), lambda i, j, k: (i, j), (out_off, 0)),
        out_shape=jax.ShapeDtypeStruct((out_rows or M, N), out_dtype),
        input_output_aliases=aliases,
        scratch_shapes=[pltpu.VMEM((tm, tn), F32)] if nk > 1 else [],
        compiler_params=_params(("parallel", "parallel", "arbitrary")),
    )(*args)


def _rms(x):
    r = lax.rsqrt(jnp.mean(x * x, axis=-1, keepdims=True) + NORM_EPS)
    return x * r, r


def _rms_bwd(xh, r, dxh):
    return r * (dxh - xh * jnp.mean(dxh * xh, axis=-1, keepdims=True))


def _swap(x, q):
    lane = lax.broadcasted_iota(jnp.int32, x.shape, 1)
    even = ((lane // q) % 2) == 0
    return jnp.where(even, pltpu.roll(x, LANES - q, 1), pltpu.roll(x, q, 1))


def _rope(x, cos, ss, q):
    return x * cos + _swap(x, q) * ss


def _rope_t(d, cos, ss, q):
    return d * cos + _swap(d * ss, q)


def _csum(x):
    return jnp.sum(x, axis=0, keepdims=True)


def _rows(tr, w, off=0):
    return pl.BlockSpec((tr, w), lambda i: (i + off, 0))


def _bcast(w):
    return pl.BlockSpec((1, w), lambda i: (0, 0))


def _acc_init(i, refs):
    @pl.when(i == 0)
    def _():
        for r in refs:
            r[...] = jnp.zeros_like(r)


def _rope_tables(n_ctx, n_lat, rot_dim):
    rows = n_lat // GRID_W
    row = jnp.repeat(jnp.arange(rows, dtype=F32), GRID_W)
    col = jnp.tile(jnp.arange(GRID_W, dtype=F32), rows)
    half = rot_dim // 2
    inv_freq = ROPE_THETA ** (-jnp.arange(0, half, 2, dtype=F32) / half)
    ar, ac = row[:, None] * inv_freq, col[:, None] * inv_freq
    cos = jnp.concatenate([jnp.cos(ar), jnp.cos(ar), jnp.cos(ac), jnp.cos(ac)], axis=-1)
    ss = jnp.concatenate([-jnp.sin(ar), jnp.sin(ar), -jnp.sin(ac), jnp.sin(ac)], axis=-1)
    cos = jnp.tile(cos, (1, LANES // rot_dim))
    ss = jnp.tile(ss, (1, LANES // rot_dim))
    cos = jnp.concatenate([cos, jnp.ones((n_ctx, LANES), F32)], axis=0)
    ss = jnp.concatenate([ss, jnp.zeros((n_ctx, LANES), F32)], axis=0)
    return cos, ss


def _norm_mod_fwd(x2d, g, sh, sc, name, tr, out_rows=None, base=None, out_off=0):
    n, d = x2d.shape

    def body(x_ref, g_ref, sh_ref, sc_ref, *rest):
        xh, _ = _rms(x_ref[...])
        rest[-1][...] = ((xh * g_ref[...]) * (1.0 + sc_ref[...]) + sh_ref[...]).astype(BF16)

    args, in_specs, aliases = [x2d, g, sh, sc], [_rows(tr, d), _bcast(d), _bcast(d), _bcast(d)], {}
    if base is not None:
        args.append(base)
        in_specs.append(ANY)
        aliases = {4: 0}
        out_rows = base.shape[0]
    return pl.pallas_call(
        body,
        name=name,
        grid=(n // tr,),
        in_specs=in_specs,
        out_specs=_rows(tr, d, out_off // tr),
        out_shape=jax.ShapeDtypeStruct((out_rows or n, d), BF16),
        input_output_aliases=aliases,
        compiler_params=_params(("parallel",)),
    )(*args)


def _norm_mod_bwd(dz, dz_off, x2d, g, sc, dres, name, tr):
    n, d = x2d.shape
    want_dx = dres is not None

    def body(*refs):
        if want_dx:
            dz_ref, x_ref, g_ref, sc_ref, dres_ref, dx_ref, dg_ref, dsh_ref, dsc_ref = refs
        else:
            dz_ref, x_ref, g_ref, sc_ref, dg_ref, dsh_ref, dsc_ref = refs
        _acc_init(pl.program_id(0), [dg_ref, dsh_ref, dsc_ref])
        xh, r = _rms(x_ref[...])
        dzv = dz_ref[...]
        gv = g_ref[...]
        dsc_ref[...] += _csum(dzv * (xh * gv))
        dsh_ref[...] += _csum(dzv)
        dh = dzv * (1.0 + sc_ref[...])
        dg_ref[...] += _csum(dh * xh)
        if want_dx:
            dx_ref[...] = _rms_bwd(xh, r, dh * gv) + dres_ref[...]

    in_specs = [_rows(tr, d, dz_off), _rows(tr, d), _bcast(d), _bcast(d)]
    args = [dz, x2d, g, sc]
    out_specs = [_bcast(d)] * 3
    out_shape = [jax.ShapeDtypeStruct((1, d), F32)] * 3
    if want_dx:
        in_specs.append(_rows(tr, d))
        args.append(dres)
        out_specs = [_rows(tr, d)] + out_specs
        out_shape = [jax.ShapeDtypeStruct((n, d), F32)] + out_shape
    res = pl.pallas_call(
        body,
        name=name,
        grid=(n // tr,),
        in_specs=in_specs,
        out_specs=out_specs,
        out_shape=out_shape,
        compiler_params=_params(("arbitrary",)),
    )(*args)
    return res if want_dx else (None, *res)


_QA, _QB = MLA_ROPE // 4, GQA_HEAD_DIM // 4


def _kprep_fwd(pkv, kvg, kg, cos_a, ss_a, cos_b, ss_b, tr):
    n = pkv.shape[0]
    nb = GQA_KV_HEADS * GQA_HEAD_DIM

    def body(p_ref, kvg_ref, kg_ref, ca, sa, cb, sb, ckv_ref, kb_ref, vb_ref, kpe_ref):
        p = p_ref[...]
        xh, _ = _rms(p[:, :MLA_KV_LORA])
        ckv_ref[...] = (xh * kvg_ref[...]).astype(BF16)
        for e in range(GQA_KV_HEADS):
            lo = MLA_KV_LORA + e * GQA_HEAD_DIM
            kh, _ = _rms(p[:, lo : lo + GQA_HEAD_DIM])
            kb_ref[:, e * GQA_HEAD_DIM : (e + 1) * GQA_HEAD_DIM] = _rope(kh * kg_ref[...], cb[...], sb[...], _QB).astype(BF16)
        vb_ref[...] = p[:, MLA_KV_LORA + nb : MLA_KV_LORA + 2 * nb].astype(BF16)
        kr = _rope(p[:, MLA_KV_LORA + 2 * nb :], ca[...], sa[...], _QA)
        kpe_ref[:, :LANES] = kr.astype(BF16)
        kpe_ref[:, LANES:] = pltpu.roll(kr, MLA_ROPE, 1).astype(BF16)

    return pl.pallas_call(
        body,
        name="kprep_fwd",
        grid=(n // tr,),
        in_specs=[_rows(tr, KVP), _bcast(MLA_KV_LORA), _bcast(GQA_HEAD_DIM)] + [_rows(tr, LANES)] * 4,
        out_specs=[_rows(tr, MLA_KV_LORA), _rows(tr, nb), _rows(tr, nb), _rows(tr, 2 * LANES)],
        out_shape=[jax.ShapeDtypeStruct((n, w), BF16) for w in (MLA_KV_LORA, nb, nb, 2 * LANES)],
        compiler_params=_params(("parallel",)),
    )(pkv, kvg, kg, cos_a, ss_a, cos_b, ss_b)


def _kprep_bwd(pkv, dckv, dkb, dvb, dkpe, kvg, kg, cos_b, ss_b, tr):
    n = pkv.shape[0]
    nb = GQA_KV_HEADS * GQA_HEAD_DIM

    def body(p_ref, dckv_ref, dkb_ref, dvb_ref, dkpe_ref, kvg_ref, kg_ref, cb, sb, dp_ref, dkvg_ref, dkg_ref):
        _acc_init(pl.program_id(0), [dkvg_ref, dkg_ref])
        p = p_ref[...]
        xh, r = _rms(p[:, :MLA_KV_LORA])
        dn = dckv_ref[...]
        dkvg_ref[...] += _csum(dn * xh)
        dp_ref[:, :MLA_KV_LORA] = _rms_bwd(xh, r, dn * kvg_ref[...]).astype(BF16)
        for e in range(GQA_KV_HEADS):
            lo = MLA_KV_LORA + e * GQA_HEAD_DIM
            kh, rk = _rms(p[:, lo : lo + GQA_HEAD_DIM])
            dk = _rope_t(dkb_ref[:, e * GQA_HEAD_DIM : (e + 1) * GQA_HEAD_DIM], cb[...], sb[...], _QB)
            dkg_ref[...] += _csum(dk * kh)
            dp_ref[:, lo : lo + GQA_HEAD_DIM] = _rms_bwd(kh, rk, dk * kg_ref[...]).astype(BF16)
        dp_ref[:, MLA_KV_LORA + nb : MLA_KV_LORA + 2 * nb] = dvb_ref[...].astype(BF16)
        dp_ref[:, MLA_KV_LORA + 2 * nb :] = dkpe_ref[...].astype(BF16)

    return pl.pallas_call(
        body,
        name="kprep_bwd",
        grid=(n // tr,),
        in_specs=[_rows(tr, KVP), _rows(tr, MLA_KV_LORA), _rows(tr, nb), _rows(tr, nb), _rows(tr, LANES),
                  _bcast(MLA_KV_LORA), _bcast(GQA_HEAD_DIM), _rows(tr, LANES), _rows(tr, LANES)],
        out_specs=[_rows(tr, KVP), _bcast(MLA_KV_LORA), _bcast(GQA_HEAD_DIM)],
        out_shape=[jax.ShapeDtypeStruct((n, KVP), BF16), jax.ShapeDtypeStruct((1, MLA_KV_LORA), F32),
                   jax.ShapeDtypeStruct((1, GQA_HEAD_DIM), F32)],
        compiler_params=_params(("arbitrary",)),
    )(pkv, dckv, dkb, dvb, dkpe, kvg, kg, cos_b, ss_b)


def _kgrad_split(dka, dva, cos_a, ss_a, tr):
    n = dka.shape[0]
    wk = MLA_HEADS * 2 * LANES

    def body(dk_ref, dv_ref, ca, sa, dkv_ref, dkpe_ref):
        even = jnp.zeros((tr, LANES), F32)
        odd = jnp.zeros((tr, LANES), F32)
        for h in range(MLA_HEADS):
            dkv_ref[:, 2 * h * LANES : (2 * h + 1) * LANES] = dk_ref[:, 2 * h * LANES : (2 * h + 1) * LANES].astype(BF16)
            dkv_ref[:, (2 * h + 1) * LANES : (2 * h + 2) * LANES] = dv_ref[:, h * MLA_V : (h + 1) * MLA_V].astype(BF16)
            part = dk_ref[:, (2 * h + 1) * LANES : (2 * h + 2) * LANES]
            if h % 2 == 0:
                even = even + part
            else:
                odd = odd + part
        lane = lax.broadcasted_iota(jnp.int32, (tr, LANES), 1)
        low = lane < MLA_ROPE
        both = jnp.where(low, even, odd)
        tot = jnp.where(low, both + pltpu.roll(both, MLA_ROPE, 1), 0.0)
        dkpe_ref[...] = _rope_t(tot, ca[...], sa[...], _QA)

    return pl.pallas_call(
        body,
        name="kgrad_split",
        grid=(n // tr,),
        in_specs=[_rows(tr, wk), _rows(tr, MLA_HEADS * MLA_V), _rows(tr, LANES), _rows(tr, LANES)],
        out_specs=[_rows(tr, wk), _rows(tr, LANES)],
        out_shape=[jax.ShapeDtypeStruct((n, wk), BF16), jax.ShapeDtypeStruct((n, LANES), F32)],
        compiler_params=_params(("parallel",)),
    )(dka, dva, cos_a, ss_a)


def _qprep_fwd(pq, qg, gq, cos_b, ss_b, tr):
    n = pq.shape[0]
    nq = GQA_HEADS * GQA_HEAD_DIM

    def body(p_ref, qg_ref, gq_ref, cb, sb, cq_ref, qb_ref):
        xh, _ = _rms(p_ref[:, :MLA_Q_LORA])
        cq_ref[...] = (xh * qg_ref[...]).astype(BF16)
        for h in range(GQA_HEADS):
            lo = MLA_Q_LORA + h * GQA_HEAD_DIM
            qh, _ = _rms(p_ref[:, lo : lo + GQA_HEAD_DIM])
            qb_ref[:, h * GQA_HEAD_DIM : (h + 1) * GQA_HEAD_DIM] = _rope(qh * gq_ref[...], cb[...], sb[...], _QB).astype(BF16)

    return pl.pallas_call(
        body,
        name="qprep_fwd",
        grid=(n // tr,),
        in_specs=[_rows(tr, QC), _bcast(MLA_Q_LORA), _bcast(GQA_HEAD_DIM), _rows(tr, LANES), _rows(tr, LANES)],
        out_specs=[_rows(tr, MLA_Q_LORA), _rows(tr, nq)],
        out_shape=[jax.ShapeDtypeStruct((n, MLA_Q_LORA), BF16), jax.ShapeDtypeStruct((n, nq), BF16)],
        compiler_params=_params(("parallel",)),
    )(pq, qg, gq, cos_b, ss_b)


def _qprep_bwd(pq, dcq, dqb, qg, gq, cos_b, ss_b, tr):
    n = pq.shape[0]
    nq = GQA_HEADS * GQA_HEAD_DIM

    def body(p_ref, dcq_ref, dqb_ref, qg_ref, gq_ref, cb, sb, dp_ref, dqg_ref, dgq_ref):
        _acc_init(pl.program_id(0), [dqg_ref, dgq_ref])
        xh, r = _rms(p_ref[:, :MLA_Q_LORA])
        dn = dcq_ref[...]
        dqg_ref[...] += _csum(dn * xh)
        dp_ref[:, :MLA_Q_LORA] = _rms_bwd(xh, r, dn * qg_ref[...]).astype(BF16)
        for h in range(GQA_HEADS):
            lo = MLA_Q_LORA + h * GQA_HEAD_DIM
            qh, rq = _rms(p_ref[:, lo : lo + GQA_HEAD_DIM])
            dq = _rope_t(dqb_ref[:, h * GQA_HEAD_DIM : (h + 1) * GQA_HEAD_DIM], cb[...], sb[...], _QB)
            dgq_ref[...] += _csum(dq * qh)
            dp_ref[:, lo : lo + GQA_HEAD_DIM] = _rms_bwd(qh, rq, dq * gq_ref[...]).astype(BF16)

    return pl.pallas_call(
        body,
        name="qprep_bwd",
        grid=(n // tr,),
        in_specs=[_rows(tr, QC), _rows(tr, MLA_Q_LORA), _rows(tr, nq), _bcast(MLA_Q_LORA), _bcast(GQA_HEAD_DIM),
                  _rows(tr, LANES), _rows(tr, LANES)],
        out_specs=[_rows(tr, QC), _bcast(MLA_Q_LORA), _bcast(GQA_HEAD_DIM)],
        out_shape=[jax.ShapeDtypeStruct((n, QC), BF16), jax.ShapeDtypeStruct((1, MLA_Q_LORA), F32),
                   jax.ShapeDtypeStruct((1, GQA_HEAD_DIM), F32)],
        compiler_params=_params(("arbitrary",)),
    )(pq, dcq, dqb, qg, gq, cos_b, ss_b)


_QA_COLS = MLA_HEADS * (MLA_NOPE + MLA_ROPE)


def _qrope_fwd(qa, cos_a, ss_a, tr):
    n = qa.shape[0]

    def body(q_ref, ca, sa, o_ref):
        for j in range(MLA_HEADS // 2):
            lo = 3 * j * LANES
            o_ref[:, lo : lo + 2 * LANES] = q_ref[:, lo : lo + 2 * LANES].astype(BF16)
            o_ref[:, lo + 2 * LANES : lo + 3 * LANES] = _rope(q_ref[:, lo + 2 * LANES : lo + 3 * LANES], ca[...], sa[...], _QA).astype(BF16)

    return pl.pallas_call(
        body,
        name="qrope_fwd",
        grid=(n // tr,),
        in_specs=[_rows(tr, _QA_COLS), _rows(tr, LANES), _rows(tr, LANES)],
        out_specs=_rows(tr, _QA_COLS),
        out_shape=jax.ShapeDtypeStruct((n, _QA_COLS), BF16),
        compiler_params=_params(("parallel",)),
    )(qa, cos_a, ss_a)


def _qrope_bwd(dq2, cos_a, ss_a, tr):
    n = dq2.shape[0]

    def body(d_ref, ca, sa, o_ref):
        for j in range(MLA_HEADS // 2):
            lo = 3 * j * LANES
            h0, h1 = 2 * j, 2 * j + 1
            o_ref[:, lo : lo + LANES] = d_ref[:, 2 * h0 * LANES : (2 * h0 + 1) * LANES].astype(BF16)
            o_ref[:, lo + LANES : lo + 2 * LANES] = d_ref[:, 2 * h1 * LANES : (2 * h1 + 1) * LANES].astype(BF16)
            pe = d_ref[:, (2 * h0 + 1) * LANES : (2 * h0 + 2) * LANES] + d_ref[:, (2 * h1 + 1) * LANES : (2 * h1 + 2) * LANES]
            o_ref[:, lo + 2 * LANES : lo + 3 * LANES] = _rope_t(pe, ca[...], sa[...], _QA).astype(BF16)

    return pl.pallas_call(
        body,
        name="qrope_bwd",
        grid=(n // tr,),
        in_specs=[_rows(tr, MLA_HEADS * 2 * LANES), _rows(tr, LANES), _rows(tr, LANES)],
        out_specs=_rows(tr, _QA_COLS),
        out_shape=jax.ShapeDtypeStruct((n, _QA_COLS), BF16),
        compiler_params=_params(("parallel",)),
    )(dq2, cos_a, ss_a)


def _cat(refs):
    vals = [r[...] for r in refs]
    return vals[0] if len(vals) == 1 else jnp.concatenate(vals, axis=-1)


LOG2E = 1.4426950408889634


def _attn_fwd(qparts, kparts, vpart, n_heads, group, dv, scale, name, tq, after=None):
    T, Tk = qparts[0][0].shape[0], kparts[0][0].shape[0]
    nq_, nk_ = len(qparts), len(kparts)
    sub = min(tq, 256)
    c2 = scale * LOG2E

    def body(*refs):
        q_refs, k_refs = refs[:nq_], refs[nq_ : nq_ + nk_]
        v_ref = refs[nq_ + nk_]
        o_ref, lse_ref = refs[-2:]
        k = _cat(k_refs)
        v = v_ref[...]
        for r0 in range(0, tq, sub):
            q = _cat([r.at[r0 : r0 + sub, :] for r in q_refs])
            s = lax.dot_general(q, k, _DIMS["NT"], preferred_element_type=F32)
            m = jnp.max(s, axis=-1, keepdims=True)
            p = jnp.exp2((s - m) * c2)
            l = jnp.sum(p, axis=-1, keepdims=True)
            acc = jnp.dot(p.astype(BF16), v, preferred_element_type=F32)
            o_ref[r0 : r0 + sub, :] = (acc * (1.0 / l)).astype(BF16)
            lse_ref[r0 : r0 + sub, :] = m * scale + jnp.log(l)

    in_specs = [pl.BlockSpec((tq, LANES), lambda h, i, f=f: (i, f(h))) for _, f in qparts]
    in_specs += [pl.BlockSpec((Tk, LANES), lambda h, i, f=f: (0, f(h // group))) for _, f in kparts]
    fv = vpart[1]
    in_specs.append(pl.BlockSpec((Tk, dv), lambda h, i: (0, fv(h // group))))
    args = [*[a for a, _ in qparts], *[a for a, _ in kparts], vpart[0]]
    if after is not None:
        in_specs.append(pl.BlockSpec(after.shape, lambda h, i: (0, 0)))
        args.append(after)
    return pl.pallas_call(
        body,
        name=name,
        grid=(n_heads, T // tq),
        in_specs=in_specs,
        out_specs=[pl.BlockSpec((tq, dv), lambda h, i: (i, h)), pl.BlockSpec((None, tq, 1), lambda h, i: (h, i, 0))],
        out_shape=[jax.ShapeDtypeStruct((T, n_heads * dv), BF16), jax.ShapeDtypeStruct((n_heads, T, 1), F32)],
        compiler_params=_params(("parallel", "parallel")),
    )(*args)


def _attn_bwd(qparts, kparts, vpart, o, do, lse, n_heads, group, dv, scale, name, tq):
    T, Tk = qparts[0][0].shape[0], kparts[0][0].shape[0]
    nq_, nk_ = len(qparts), len(kparts)
    dk_ = LANES * nq_
    n_kv = n_heads // group
    nblk = T // tq
    c2 = scale * LOG2E

    def head(hk, i):
        return hk * group + i // nblk

    def body(*refs):
        q = _cat(refs[:nq_])
        k = _cat(refs[nq_ : nq_ + nk_])
        v_ref, o_ref, do_ref, lse_ref, dq_ref, dk_ref, dv_ref = refs[nq_ + nk_ :]
        i = pl.program_id(1)
        _acc_init(i, [dk_ref, dv_ref])
        s = lax.dot_general(q, k, _DIMS["NT"], preferred_element_type=F32)
        p = jnp.exp2(s * c2 - lse_ref[...] * LOG2E)
        dov = do_ref[...]
        dp = lax.dot_general(dov, v_ref[...], _DIMS["NT"], preferred_element_type=F32)
        delta = jnp.sum(dov.astype(F32) * o_ref[...].astype(F32), axis=-1, keepdims=True)
        ds = (p * (dp - delta)).astype(BF16)
        dq_ref[...] = jnp.dot(ds, k, preferred_element_type=F32) * scale
        dk_ref[...] += lax.dot_general(ds, q, _DIMS["TN"], preferred_element_type=F32)
        dv_ref[...] += lax.dot_general(p.astype(BF16), dov, _DIMS["TN"], preferred_element_type=F32)

        @pl.when(i == group * nblk - 1)
        def _():
            dk_ref[...] *= scale

    in_specs = [pl.BlockSpec((tq, LANES), lambda hk, i, f=f: (i % nblk, f(head(hk, i)))) for _, f in qparts]
    in_specs += [pl.BlockSpec((Tk, LANES), lambda hk, i, f=f: (0, f(hk))) for _, f in kparts]
    fv = vpart[1]
    in_specs.append(pl.BlockSpec((Tk, dv), lambda hk, i: (0, fv(hk))))
    in_specs += [pl.BlockSpec((tq, dv), lambda hk, i: (i % nblk, head(hk, i)))] * 2
    in_specs.append(pl.BlockSpec((None, tq, 1), lambda hk, i: (head(hk, i), i % nblk, 0)))
    return pl.pallas_call(
        body,
        name=name,
        grid=(n_kv, group * nblk),
        in_specs=in_specs,
        out_specs=[pl.BlockSpec((tq, dk_), lambda hk, i: (i % nblk, head(hk, i))),
                   pl.BlockSpec((Tk, dk_), lambda hk, i: (0, hk)),
                   pl.BlockSpec((Tk, dv), lambda hk, i: (0, hk))],
        out_shape=[jax.ShapeDtypeStruct((T, n_heads * dk_), F32), jax.ShapeDtypeStruct((Tk, n_kv * dk_), F32),
                   jax.ShapeDtypeStruct((Tk, n_kv * dv), F32)],
        compiler_params=_params(("parallel", "arbitrary")),
    )(*[a for a, _ in qparts], *[a for a, _ in kparts], vpart[0], o, do, lse)


def _gates_fwd(pg, ya, yb, tr):
    n, d = ya.shape

    def body(pg_ref, ya_ref, yb_ref, o_ref):
        ga = jax.nn.sigmoid(pg_ref[:, :d])
        gb = jax.nn.sigmoid(pg_ref[:, d:])
        o_ref[...] = (ga * ya_ref[...] + gb * yb_ref[...]).astype(BF16)

    return pl.pallas_call(
        body,
        name="gates_fwd",
        grid=(n // tr,),
        in_specs=[_rows(tr, 2 * d), _rows(tr, d), _rows(tr, d)],
        out_specs=_rows(tr, d),
        out_shape=jax.ShapeDtypeStruct((n, d), BF16),
        compiler_params=_params(("parallel",)),
    )(pg, ya, yb)


def _gates_bwd(dm, pg, ya, yb, tr):
    n, d = ya.shape

    def body(dm_ref, pg_ref, ya_ref, yb_ref, dya_ref, dyb_ref, dpg_ref):
        dmv = dm_ref[...]
        ga = jax.nn.sigmoid(pg_ref[:, :d])
        gb = jax.nn.sigmoid(pg_ref[:, d:])
        dya_ref[...] = (dmv * ga).astype(BF16)
        dyb_ref[...] = (dmv * gb).astype(BF16)
        dpg_ref[:, :d] = (dmv * ya_ref[...] * ga * (1.0 - ga)).astype(BF16)
        dpg_ref[:, d:] = (dmv * yb_ref[...] * gb * (1.0 - gb)).astype(BF16)

    return pl.pallas_call(
        body,
        name="gates_bwd",
        grid=(n // tr,),
        in_specs=[_rows(tr, d), _rows(tr, 2 * d), _rows(tr, d), _rows(tr, d)],
        out_specs=[_rows(tr, d), _rows(tr, d), _rows(tr, 2 * d)],
        out_shape=[jax.ShapeDtypeStruct((n, d), BF16), jax.ShapeDtypeStruct((n, d), BF16), jax.ShapeDtypeStruct((n, 2 * d), BF16)],
        compiler_params=_params(("parallel",)),
    )(dm, pg, ya, yb)


def _resid_norm2_fwd(x2d, att, g1, n2g, sh2, sc2, tr):
    n, d = x2d.shape

    def body(x_ref, a_ref, g1_ref, g_ref, sh_ref, sc_ref, x1_ref, z_ref):
        x1 = x_ref[...] + g1_ref[...] * a_ref[...]
        x1_ref[...] = x1
        xh, _ = _rms(x1)
        z_ref[...] = ((xh * g_ref[...]) * (1.0 + sc_ref[...]) + sh_ref[...]).astype(BF16)

    return pl.pallas_call(
        body,
        name="resid_norm2_fwd",
        grid=(n // tr,),
        in_specs=[_rows(tr, d), _rows(tr, d)] + [_bcast(d)] * 4,
        out_specs=[_rows(tr, d), _rows(tr, d)],
        out_shape=[jax.ShapeDtypeStruct((n, d), F32), jax.ShapeDtypeStruct((n, d), BF16)],
        compiler_params=_params(("parallel",)),
    )(x2d, att, g1, n2g, sh2, sc2)


def _resid_norm2_bwd(dz2, x1, dx2, att, n2g, sc2, g1, tr):
    n, d = x1.shape

    def body(dz_ref, x1_ref, dx2_ref, a_ref, g_ref, sc_ref, g1_ref, dx1_ref, da_ref, dg_ref, dsh_ref, dsc_ref, dg1_ref):
        _acc_init(pl.program_id(0), [dg_ref, dsh_ref, dsc_ref, dg1_ref])
        xh, r = _rms(x1_ref[...])
        dzv = dz_ref[...]
        gv = g_ref[...]
        dsc_ref[...] += _csum(dzv * (xh * gv))
        dsh_ref[...] += _csum(dzv)
        dh = dzv * (1.0 + sc_ref[...])
        dg_ref[...] += _csum(dh * xh)
        dx1 = _rms_bwd(xh, r, dh * gv) + dx2_ref[...]
        dx1_ref[...] = dx1
        dg1_ref[...] += _csum(dx1 * a_ref[...])
        da_ref[...] = (dx1 * g1_ref[...]).astype(BF16)

    return pl.pallas_call(
        body,
        name="resid_norm2_bwd",
        grid=(n // tr,),
        in_specs=[_rows(tr, d)] * 4 + [_bcast(d)] * 3,
        out_specs=[_rows(tr, d), _rows(tr, d)] + [_bcast(d)] * 4,
        out_shape=[jax.ShapeDtypeStruct((n, d), F32), jax.ShapeDtypeStruct((n, d), BF16)] + [jax.ShapeDtypeStruct((1, d), F32)] * 4,
        compiler_params=_params(("arbitrary",)),
    )(dz2, x1, dx2, att, n2g, sc2, g1)


def _edges(shape):
    row = lax.broadcasted_iota(jnp.int32, shape, 0)
    return row == 0, row == shape[0] - 1


def _shifts(u, edges):
    n = u.shape[0]
    return jnp.where(edges[0], 0.0, pltpu.roll(u, 1, 0)), jnp.where(edges[1], 0.0, pltpu.roll(u, n - 1, 0))


def _conv3(u, prev, nxt, w_ref, b_ref):
    return b_ref[...] + w_ref[0:1, :] * prev + w_ref[1:2, :] * u + w_ref[2:3, :] * nxt


def _conv_fwd(u, cw, cb, tc):
    n, two_f = u.shape
    f = two_f // 2
    nb = f // tc

    def body(ua_ref, ub_ref, wa_ref, wb_ref, ba_ref, bb_ref, h_ref):
        edges = _edges((n, tc))
        ua = ua_ref[...].astype(F32)
        ub = ub_ref[...].astype(F32)
        a = _conv3(ua, *_shifts(ua, edges), wa_ref, ba_ref)
        b = _conv3(ub, *_shifts(ub, edges), wb_ref, bb_ref)
        h_ref[...] = (a * jax.nn.sigmoid(a) * b).astype(BF16)

    col = lambda rows, off: pl.BlockSpec((rows, tc), lambda i: (0, i + off))
    return pl.pallas_call(
        body,
        name="conv_fwd",
        grid=(nb,),
        in_specs=[col(n, 0), col(n, nb), col(3, 0), col(3, nb), col(1, 0), col(1, nb)],
        out_specs=col(n, 0),
        out_shape=jax.ShapeDtypeStruct((n, f), BF16),
        compiler_params=_params(("parallel",)),
    )(u, u, cw, cw, cb, cb)


def _conv_bwd(u, dh, cw, cb, tc):
    n, two_f = u.shape
    f = two_f // 2
    nb = f // tc

    def part(uv, prev, nxt, duc, edges, w_ref, du_ref, dw_ref, db_ref):
        db_ref[...] = _csum(duc)
        dw_ref[0:1, :] = _csum(duc * prev)
        dw_ref[1:2, :] = _csum(duc * uv)
        dw_ref[2:3, :] = _csum(duc * nxt)
        d_prev, d_next = _shifts(duc, edges)
        du_ref[...] = (w_ref[0:1, :] * d_next + w_ref[1:2, :] * duc + w_ref[2:3, :] * d_prev).astype(BF16)

    def body(ua_ref, ub_ref, dh_ref, wa_ref, wb_ref, ba_ref, bb_ref, dua_ref, dub_ref, dwa_ref, dwb_ref, dba_ref, dbb_ref):
        edges = _edges((n, tc))
        ua = ua_ref[...].astype(F32)
        ub = ub_ref[...].astype(F32)
        sa = _shifts(ua, edges)
        sb = _shifts(ub, edges)
        a = _conv3(ua, *sa, wa_ref, ba_ref)
        b = _conv3(ub, *sb, wb_ref, bb_ref)
        dhv = dh_ref[...].astype(F32)
        sg = jax.nn.sigmoid(a)
        da = dhv * b * (sg * (1.0 + a * (1.0 - sg)))
        db = dhv * (a * sg)
        part(ua, *sa, da, edges, wa_ref, dua_ref, dwa_ref, dba_ref)
        part(ub, *sb, db, edges, wb_ref, dub_ref, dwb_ref, dbb_ref)

    col = lambda rows, off: pl.BlockSpec((rows, tc), lambda i: (0, i + off))
    return pl.pallas_call(
        body,
        name="conv_bwd",
        grid=(nb,),
        in_specs=[col(n, 0), col(n, nb), col(n, 0), col(3, 0), col(3, nb), col(1, 0), col(1, nb)],
        out_specs=[col(n, 0), col(n, 0), col(3, 0), col(3, 0), col(1, 0), col(1, 0)],
        out_shape=[jax.ShapeDtypeStruct((n, f), BF16)] * 2 + [jax.ShapeDtypeStruct((3, f), F32)] * 2 + [jax.ShapeDtypeStruct((1, f), F32)] * 2,
        compiler_params=_params(("parallel",)),
    )(u, u, dh, cw, cw, cb, cb)


def _loss_head(x1, f, g2, fg, tgt, tr):
    n, d = x1.shape

    def body(x1_ref, f_ref, g2_ref, fg_ref, t_ref, sq_ref, dx2_ref, dfg_ref, dg2_ref, df_ref):
        _acc_init(pl.program_id(0), [sq_ref, dfg_ref, dg2_ref])
        fv = f_ref[...]
        xh, r = _rms(x1_ref[...] + g2_ref[...] * fv)
        err = xh * fg_ref[...] - t_ref[...]
        sq_ref[...] += _csum(err * err)
        dy = err * (1.0 / d)
        dfg_ref[...] += _csum(dy * xh)
        dx2 = _rms_bwd(xh, r, dy * fg_ref[...])
        dx2_ref[...] = dx2
        dg2_ref[...] += _csum(dx2 * fv)
        df_ref[...] = (dx2 * g2_ref[...]).astype(BF16)

    return pl.pallas_call(
        body,
        name="loss_head",
        grid=(n // tr,),
        in_specs=[_rows(tr, d), _rows(tr, d), _bcast(d), _bcast(d), _rows(tr, d)],
        out_specs=[_bcast(d), _rows(tr, d), _bcast(d), _bcast(d), _rows(tr, d)],
        out_shape=[jax.ShapeDtypeStruct((1, d), F32), jax.ShapeDtypeStruct((n, d), F32), jax.ShapeDtypeStruct((1, d), F32),
                   jax.ShapeDtypeStruct((1, d), F32), jax.ShapeDtypeStruct((n, d), BF16)],
        compiler_params=_params(("arbitrary",)),
    )(x1, f, g2, fg, tgt)


def _sum_slots(g, name):
    s, r, w = g.shape

    def body(g_ref, o_ref):
        acc = g_ref[0]
        for k in range(1, s):
            acc = acc + g_ref[k]
        o_ref[...] = acc

    return pl.pallas_call(body, name=name, out_shape=jax.ShapeDtypeStruct((r, w), F32))(g)


def _silu_grad_mul(ds, cvec):
    def body(d_ref, c_ref, o_ref):
        cv = c_ref[...]
        sg = jax.nn.sigmoid(cv)
        o_ref[...] = d_ref[...] * (sg * (1.0 + cv * (1.0 - sg)))

    return pl.pallas_call(body, name="silu_grad_mul", out_shape=jax.ShapeDtypeStruct(ds.shape, F32))(ds, cvec)


def _adamw(w, g, m, v, name, g_transposed=False):
    r, cdim = w.shape
    tr = _pick(r, 1024, LANES if g_transposed else 8)
    tc = _pick(cdim, max(LANES, (1 << 19) // tr))
    b1c = 1.0 - ADAM_B1**ADAM_STEP
    b2c = 1.0 - ADAM_B2**ADAM_STEP

    def body(w_ref, g_ref, m_ref, v_ref, *outs):
        d_ref, mo_ref, vo_ref = outs[-3:]
        gv = g_ref[...]
        if g_transposed:
            gv = gv.T
            outs[0][...] = gv
        mn = ADAM_B1 * m_ref[...] + (1.0 - ADAM_B1) * gv
        vn = ADAM_B2 * v_ref[...] + (1.0 - ADAM_B2) * (gv * gv)
        mo_ref[...] = mn
        vo_ref[...] = vn
        d_ref[...] = -ADAM_LR * ((mn / b1c) / (jnp.sqrt(vn / b2c) + ADAM_EPS) + ADAM_WD * w_ref[...])

    spec = pl.BlockSpec((tr, tc), lambda i, j: (i, j))
    g_spec = pl.BlockSpec((tc, tr), lambda i, j: (j, i)) if g_transposed else spec
    n_out = 4 if g_transposed else 3
    res = pl.pallas_call(
        body,
        name=name,
        grid=(r // tr, cdim // tc),
        in_specs=[spec, g_spec, spec, spec],
        out_specs=[spec] * n_out,
        out_shape=[jax.ShapeDtypeStruct((r, cdim), F32)] * n_out,
        compiler_params=_params(("parallel", "parallel")),
    )(w, g, m, v)
    return res if g_transposed else [g, *res]


def _place():
    return lax.axis_index("x"), lax.axis_index("y"), lax.axis_index("c")


def _remote(src, dst, send_sem, recv_sem, dev):
    return pltpu.make_async_remote_copy(src_ref=src, dst_ref=dst, send_sem=send_sem, recv_sem=recv_sem, device_id=dev, device_id_type=MESH)


ANY = pl.BlockSpec(memory_space=pl.ANY)


def _all_gather_small(v, name):
    r, w = v.shape

    def body(v_ref, o_ref, send, recv, lsem):
        x, y, c = _place()
        me = 4 * x + 2 * y + c
        mine = pltpu.make_async_copy(v_ref, o_ref.at[me], lsem)
        mine.start()
        sent = []
        for k in range(1, 8):
            px, py, pc = x ^ (k >> 2), y ^ ((k >> 1) & 1), c ^ (k & 1)
            cp = _remote(v_ref, o_ref.at[me], send.at[k - 1], recv.at[k - 1], (px, py, pc))
            cp.start()
            sent.append(cp)
        for k in range(1, 8):
            px, py, pc = x ^ (k >> 2), y ^ ((k >> 1) & 1), c ^ (k & 1)
            slot = o_ref.at[4 * px + 2 * py + pc]
            _remote(slot, slot, send.at[k - 1], recv.at[k - 1], (x, y, c)).wait_recv()
        for cp in sent:
            cp.wait_send()
        mine.wait()

    return pl.pallas_call(
        body,
        name=name,
        out_shape=jax.ShapeDtypeStruct((8, r, w), F32),
        in_specs=[pl.BlockSpec(memory_space=pltpu.VMEM)],
        out_specs=pl.BlockSpec(memory_space=pltpu.VMEM),
        scratch_shapes=[pltpu.SemaphoreType.DMA((7,)), pltpu.SemaphoreType.DMA((7,)), pltpu.SemaphoreType.DMA],
        compiler_params=pltpu.CompilerParams(vmem_limit_bytes=VMEM_LIMIT),
    )(v)


HBM = pl.BlockSpec(memory_space=pltpu.HBM)
SEM = pl.BlockSpec(memory_space=pltpu.SEMAPHORE)
EFFECT = pltpu.SideEffectType.DATAFLOW_SIDE_EFFECTING


def _other_chips(x, y):
    return [(1 - x, y), (x, 1 - y), (1 - x, 1 - y)]


def _bulk_start(name, srcs, land_shapes, n_copies, copies, after):
    n, m = len(srcs), len(land_shapes)

    def body(*refs):
        src_refs, land_refs = refs[:n], refs[n : n + m]
        send, recv = refs[n + m + 1], refs[n + m + 2]
        token = refs[-1]
        for k, (s, d, dev) in enumerate(copies(src_refs, land_refs)):
            _remote(s, d, send.at[k], recv.at[k], dev).start()
        token[...] = jnp.zeros_like(token)

    lands = [pltpu.with_memory_space_constraint(lax.empty(s.shape, s.dtype), pltpu.HBM) for s in land_shapes]
    out = pl.pallas_call(
        body,
        name=name,
        out_shape=(pltpu.SemaphoreType.DMA((n_copies,)), pltpu.SemaphoreType.DMA((n_copies,)),
                   *[pltpu.HBM(s.shape, s.dtype) for s in srcs], *[pltpu.HBM(s.shape, s.dtype) for s in land_shapes],
                   jax.ShapeDtypeStruct((8, LANES), F32)),
        in_specs=[HBM] * (n + m) + [ANY],
        out_specs=(SEM, SEM, *[HBM] * (n + m), pl.BlockSpec(memory_space=pltpu.VMEM)),
        input_output_aliases={i: 2 + i for i in range(n + m)},
        compiler_params=pltpu.CompilerParams(has_side_effects=EFFECT),
    )(*[pltpu.with_memory_space_constraint(s, pltpu.HBM) for s in srcs], *lands, after)
    return out[0], out[1], list(out[2 : 2 + n]), list(out[2 + n : 2 + n + m]), out[-1][0:1, 0:1]


def _bulk_wait(name, send, recv, srcs, lands, after, waits):
    n, m = len(srcs), len(lands)

    def body(*refs):
        src_refs, land_refs = refs[:n], refs[n : n + m]
        send_sem, recv_sem = refs[n + m], refs[n + m + 1]
        x, y, c = _place()
        for k, (s, d) in enumerate(waits(src_refs, land_refs)):
            cp = _remote(s, d, send_sem.at[k], recv_sem.at[k], (x, y, c))
            cp.wait_send()
            cp.wait_recv()

    out = pl.pallas_call(
        body,
        name=name,
        out_shape=tuple(pltpu.HBM(s.shape, s.dtype) for s in (*srcs, *lands)),
        in_specs=[HBM] * (n + m) + [SEM, SEM, ANY],
        out_specs=tuple([HBM] * (n + m)),
        input_output_aliases={i: i for i in range(n + m)},
        compiler_params=pltpu.CompilerParams(has_side_effects=EFFECT),
    )(*srcs, *lands, send, recv, after)
    return list(out[:n]), list(out[n:])


def _gather_start(shards, after, name):
    def copies(src, land):
        x, y, c = _place()
        j = 2 * x + y
        return [(src[a].at[c], land[a].at[j, c], (px, py, c)) for a in range(len(shards)) for px, py in _other_chips(x, y)]

    shapes = [jax.ShapeDtypeStruct((4,) + s.shape, s.dtype) for s in shards]
    return _bulk_start(name, shards, shapes, 3 * len(shards), copies, after)


def _gather_wait(started, after, name):
    send, recv, srcs, lands, _ = started

    def waits(src, land):
        x, y, c = _place()
        return [(src[a].at[c], land[a].at[2 * px + py, c]) for a in range(len(srcs)) for px, py in _other_chips(x, y)]

    return _bulk_wait(name, send, recv, srcs, lands, after, waits)


def _forward_halves(lands, name):
    n = len(lands)

    def body(*refs):
        bufs = refs[n : 2 * n]
        send, recv = refs[2 * n :]
        x, y, c = _place()
        started = []
        for a in range(n):
            for k, (px, py) in enumerate(_other_chips(x, y)):
                blk = bufs[a].at[2 * px + py, c]
                cp = _remote(blk, blk, send.at[3 * a + k], recv.at[3 * a + k], (x, y, 1 - c))
                cp.start()
                started.append(cp)
        for a in range(n):
            for k, (px, py) in enumerate(_other_chips(x, y)):
                blk = bufs[a].at[2 * px + py, 1 - c]
                _remote(blk, blk, send.at[3 * a + k], recv.at[3 * a + k], (x, y, c)).wait_recv()
        for cp in started:
            cp.wait_send()

    return pl.pallas_call(
        body,
        name=name,
        out_shape=[jax.ShapeDtypeStruct(b.shape, b.dtype) for b in lands],
        in_specs=[ANY] * n,
        out_specs=[ANY] * n,
        input_output_aliases={i: i for i in range(n)},
        scratch_shapes=[pltpu.SemaphoreType.DMA((3 * n,)), pltpu.SemaphoreType.DMA((3 * n,))],
    )(*lands)


def _forward_start(lands, after, name):
    def copies(src, _):
        x, y, c = _place()
        blocks = [src[a].at[2 * px + py, c] for a in range(len(lands)) for px, py in _other_chips(x, y)]
        return [(b, b, (x, y, 1 - c)) for b in blocks]

    return _bulk_start(name, lands, [], 3 * len(lands), copies, after)


def _forward_wait(started, after, name):
    send, recv, bufs, _, _ = started

    def waits(src, _):
        x, y, c = _place()
        return [(src[a].at[2 * px + py, c], src[a].at[2 * px + py, 1 - c]) for a in range(len(bufs)) for px, py in _other_chips(x, y)]

    return _bulk_wait(name, send, recv, bufs, [], after, waits)[0]


def _place_own(shards, lands):
    j = 2 * lax.axis_index("x") + lax.axis_index("y")
    full = [lax.dynamic_update_slice(b, s[None], (j, 0, 0, 0)) for b, s in zip(lands, shards)]
    return [f.reshape(4 * f.shape[2] * 2, f.shape[3]) for f in full]


def _gather_finish(started, after, tag):
    shards, lands = _gather_wait(started, after, "gather_wait_" + tag)
    return _place_own(shards, _forward_halves(lands, "gather_forward_" + tag))


def _gather_land(started, after, tag):
    shards, lands = _gather_wait(started, after, "gather_wait_" + tag)
    return shards, _forward_start(lands, shards[0], "forward_start_" + tag)


def _gather_done(landed, after, tag):
    shards, fwd = landed
    return _place_own(shards, _forward_wait(fwd, after, "forward_wait_" + tag))


def _swap_halves(grads, name):
    n = len(grads)

    def body(*refs):
        ins, outs = refs[:n], refs[n : 2 * n]
        send, recv = refs[2 * n :]
        x, y, c = _place()
        started = []
        for a in range(n):
            for s in range(4):
                cp = _remote(ins[a].at[s, 1 - c], outs[a].at[s], send.at[4 * a + s], recv.at[4 * a + s], (x, y, 1 - c))
                cp.start()
                started.append(cp)
        for cp in started:
            cp.wait_recv()
        for cp in started:
            cp.wait_send()

    return pl.pallas_call(
        body,
        name=name,
        out_shape=[jax.ShapeDtypeStruct((4,) + g.shape[2:], g.dtype) for g in grads],
        in_specs=[ANY] * n,
        out_specs=[ANY] * n,
        scratch_shapes=[pltpu.SemaphoreType.DMA((4 * n,)), pltpu.SemaphoreType.DMA((4 * n,))],
    )(*grads)


def _add_halves(grads, others, tag):
    outs = []
    for a, (g, o) in enumerate(zip(grads, others)):
        _, _, rh, cdim = g.shape
        tr = _pick(rh, 512, 16)

        def body(g_ref, o_ref, p_ref):
            c = lax.axis_index("c")
            own = jnp.where(c == 0, g_ref[0].astype(F32), g_ref[1].astype(F32))
            p_ref[...] = (own + o_ref[...].astype(F32)).astype(BF16)

        outs.append(
            pl.pallas_call(
                body,
                name=f"add_halves_{tag}{a}",
                grid=(4, rh // tr),
                in_specs=[pl.BlockSpec((None, 2, tr, cdim), lambda s, i: (s, 0, i, 0)), pl.BlockSpec((None, tr, cdim), lambda s, i: (s, i, 0))],
                out_specs=pl.BlockSpec((None, tr, cdim), lambda s, i: (s, i, 0)),
                out_shape=jax.ShapeDtypeStruct((4, rh, cdim), BF16),
                compiler_params=_params(("parallel", "parallel")),
            )(g, o)
        )
    return outs


def _exchange_start(parts, after, name):
    def copies(src, land):
        x, y, c = _place()
        j = 2 * x + y
        return [(src[a].at[2 * px + py], land[a].at[j], (px, py, c)) for a in range(len(parts)) for px, py in _other_chips(x, y)]

    return _bulk_start(name, parts, [jax.ShapeDtypeStruct(p.shape, p.dtype) for p in parts], 3 * len(parts), copies, after)


def _exchange_finish(started, after, name):
    send, recv, srcs, lands, _ = started

    def waits(src, land):
        x, y, _ = _place()
        return [(src[a].at[2 * px + py], land[a].at[2 * px + py]) for a in range(len(srcs)) for px, py in _other_chips(x, y)]

    srcs, lands = _bulk_wait(name, send, recv, srcs, lands, after, waits)
    j = 2 * lax.axis_index("x") + lax.axis_index("y")
    return [lax.dynamic_update_slice(b, lax.dynamic_slice(p, (j, 0, 0), (1,) + p.shape[1:]), (j, 0, 0)) for b, p in zip(lands, srcs)]


def _sum_chips(recvd, tag):
    outs = []
    for a, g in enumerate(recvd):
        _, rh, cdim = g.shape
        tr = _pick(rh, 512, 16)

        def body(g_ref, o_ref):
            o_ref[...] = ((g_ref[0].astype(F32) + g_ref[1].astype(F32)) + g_ref[2].astype(F32)) + g_ref[3].astype(F32)

        outs.append(
            pl.pallas_call(
                body,
                name=f"sum_chips_{tag}{a}",
                grid=(rh // tr,),
                in_specs=[pl.BlockSpec((4, tr, cdim), lambda i: (0, i, 0))],
                out_specs=pl.BlockSpec((tr, cdim), lambda i: (i, 0)),
                out_shape=jax.ShapeDtypeStruct((rh, cdim), F32),
                compiler_params=_params(("parallel",)),
            )(g)
        )
    return outs


def _join_halves(halves, name):
    n = len(halves)

    def body(*refs):
        ins, outs = refs[:n], refs[n : 2 * n]
        send, recv = refs[2 * n :]
        x, y, c = _place()
        started = []
        for a in range(n):
            cp = _remote(ins[a], outs[a], send.at[a], recv.at[a], (x, y, 1 - c))
            cp.start()
            started.append(cp)
        for cp in started:
            cp.wait_recv()
        for cp in started:
            cp.wait_send()

    others = pl.pallas_call(
        body,
        name=name,
        out_shape=[jax.ShapeDtypeStruct(h.shape, h.dtype) for h in halves],
        in_specs=[ANY] * n,
        out_specs=[ANY] * n,
        scratch_shapes=[pltpu.SemaphoreType.DMA((n,)), pltpu.SemaphoreType.DMA((n,))],
    )(*halves)
    first = lax.axis_index("c") == 0
    return [jnp.concatenate([jnp.where(first, h, o), jnp.where(first, o, h)], axis=0) for h, o in zip(halves, others)]


def _grad_views(grads):
    return [g.reshape(4, 2, g.shape[0] // 8, g.shape[1]) for g in grads]


def _scatter_start(grads, tag, after=None):
    views = _grad_views(grads)
    mine = _add_halves(views, _swap_halves(views, "swap_halves_" + tag), tag)
    return _exchange_start(mine, mine[-1] if after is None else after, "exchange_start_" + tag)


def _swap_start(grads, after, tag):
    views = _grad_views(grads)

    def copies(src, land):
        x, y, c = _place()
        return [(src[a].at[s, 1 - c], land[a].at[s], (x, y, 1 - c)) for a in range(len(views)) for s in range(4)]

    shapes = [jax.ShapeDtypeStruct((4,) + v.shape[2:], v.dtype) for v in views]
    return _bulk_start("swap_start_" + tag, views, shapes, 4 * len(views), copies, after)


def _scatter_start_after_swap(swapped, after, tag):
    send, recv, views, lands, _ = swapped

    def waits(src, land):
        c = lax.axis_index("c")
        return [(src[a].at[s, 1 - c], land[a].at[s]) for a in range(len(views)) for s in range(4)]

    views, others = _bulk_wait("swap_wait_" + tag, send, recv, views, lands, after, waits)
    mine = _add_halves(views, others, tag)
    return _exchange_start(mine, mine[-1], "exchange_start_" + tag)


def _join_start(halves, after, tag):
    def copies(src, land):
        x, y, c = _place()
        return [(src[a], land[a], (x, y, 1 - c)) for a in range(len(halves))]

    return _bulk_start("join_start_" + tag, halves, [jax.ShapeDtypeStruct(h.shape, h.dtype) for h in halves], len(halves), copies, after)


def _join_wait(started, after, tag):
    send, recv, halves, lands, _ = started
    halves, others = _bulk_wait("join_wait_" + tag, send, recv, halves, lands, after, lambda src, land: list(zip(src, land)))
    first = lax.axis_index("c") == 0
    return [jnp.concatenate([jnp.where(first, h, o), jnp.where(first, o, h)], axis=0) for h, o in zip(halves, others)]


def _scatter_sums(started, after, tag):
    return _sum_chips(_exchange_finish(started, after, "exchange_wait_" + tag), tag)


def _scatter_finish(started, after, tag):
    return _join_halves(_scatter_sums(started, after, tag), "join_halves_" + tag)


def _t_bf16(w):
    return w.T.astype(BF16)


def kernel(x, c, ctx, c_ctx, w_ada, b_ada, norm1_g, w_in, mla_q_norm_g, w_q_up, mla_kv_norm_g, w_kv_up, gqa_q_norm_g, gqa_k_norm_g, w_br_a, w_br_b, w_out, norm2_g, w_up, conv_w, conv_b, w_down, final_norm_g, loss_target, m_c_ctx, m_w_ada, m_b_ada, m_norm1_g, m_w_in, m_mla_q_norm_g, m_w_q_up, m_mla_kv_norm_g, m_w_kv_up, m_gqa_q_norm_g, m_gqa_k_norm_g, m_w_br_a, m_w_br_b, m_w_out, m_norm2_g, m_w_up, m_conv_w, m_conv_b, m_w_down, m_final_norm_g, v_c_ctx, v_w_ada, v_b_ada, v_norm1_g, v_w_in, v_mla_q_norm_g, v_w_q_up, v_mla_kv_norm_g, v_w_kv_up, v_gqa_q_norm_g, v_gqa_k_norm_g, v_w_br_a, v_w_br_b, v_w_out, v_norm2_g, v_w_up, v_conv_w, v_conv_b, v_w_down, v_final_norm_g):
    T, D = x.shape[1], x.shape[2]
    C = ctx.shape[1]
    NA = w_ada.shape[2]
    NW = w_up.shape[2]
    F2 = 4 * NW
    FF = F2 // 2
    xi, yi, ci = _place()
    j = 2 * xi + yi
    me = 4 * xi + 2 * yi + ci
    tr = _pick(C, 128, 8)
    tq = _pick(T, 256)

    x2d, tgt, ctx2d = x[0], loss_target[0], ctx[0]
    fg = final_norm_g.reshape(1, D)
    cc = c_ctx.reshape(1, D)

    halve = lambda s: s.reshape(2, s.shape[0] // 2, s.shape[1])
    win_shard = halve(_t_bf16(w_in[0]))
    w0 = max(D, NW)
    pay = jnp.zeros((8, w0), F32).at[0:1, :D].set(c).at[1:4, :NW].set(conv_w[0])
    got = _all_gather_small(pay, "gather_cond")
    c_all = got[:, 0, :D]
    cw = jnp.concatenate([got[2 * s, 1:4, :NW] for s in range(4)], axis=1)
    s16 = jnp.concatenate([c_all, cc, jnp.zeros((7, D), F32)], axis=0)
    b_cols = lax.dynamic_slice(b_ada, (0, j * NA), (1, NA))
    ada_part = _mm(s16, w_ada[0], "NN", F32, "ada_fwd", act="silu", bias=b_cols)
    got = _all_gather_small(ada_part, "gather_ada")
    ada = jnp.concatenate([got[2 * s] for s in range(4)], axis=1)
    lat = lax.dynamic_slice(ada, (me, 0), (1, 6 * D))
    sh1, sc1, g1, sh2, sc2, g2 = [lat[:, k * D : (k + 1) * D] for k in range(6)]
    csh, csc = ada[8:9, :D], ada[8:9, D : 2 * D]

    ag_in = _gather_start([win_shard], got, "gather_start_in")
    t_in = ag_in[4]
    wq3 = (w_q_up[0] + t_in).reshape(MLA_Q_LORA, 2, MLA_NOPE + MLA_ROPE)
    wq_perm = jnp.concatenate([wq3[:, :, :MLA_NOPE].reshape(MLA_Q_LORA, -1), wq3[:, :, MLA_NOPE:].reshape(MLA_Q_LORA, -1)], axis=1)
    mix = [_t_bf16(wq_perm), _t_bf16(w_kv_up[0] + t_in), _t_bf16(w_br_a[0] + t_in), _t_bf16(w_br_b[0] + t_in), (w_out[0] + t_in).astype(BF16)]
    ag_mix = _gather_start([halve(s) for s in mix], t_in, "gather_start_mix")
    ag_up = _gather_start([halve(_t_bf16(w_up[0] + t_in))], ag_mix[4], "gather_start_up")
    ag_down = _gather_start([halve((w_down[0] + t_in).astype(BF16))], ag_up[4], "gather_start_down")
    sh1 = sh1 + ag_down[4]

    cos_a, ss_a = _rope_tables(C, T, MLA_ROPE)
    cos_b, ss_b = _rope_tables(C, T, GQA_HEAD_DIM)
    lcos_a, lss_a, lcos_b, lss_b = cos_a[:T], ss_a[:T], cos_b[:T], ss_b[:T]

    z_all = _norm_mod_fwd(x2d, norm1_g, sh1, sc1, "norm1_lat_fwd", tr, out_rows=T + C)
    z_all = _norm_mod_fwd(ctx2d, norm1_g, csh, csc, "norm1_ctx_fwd", tr, base=z_all, out_off=T)
    (win_t,) = _gather_finish(ag_in, z_all, "in")
    kv_cols = KVP - LANES + MLA_ROPE
    e_kpe = MLA_KV_LORA + MLA_ROPE
    w_kvp = jnp.concatenate([win_t[:MLA_KV_LORA], win_t[e_kpe:kv_cols], win_t[MLA_KV_LORA:e_kpe], jnp.zeros((LANES - MLA_ROPE, D), BF16)], axis=0)

    pkv = _mm(z_all, w_kvp, "NT", F32, "proj_kv")
    pq = _mm(z_all, win_t, "NT", F32, "proj_q", m=T, n=QC, b_off=kv_cols)
    mix_landed = _gather_land(ag_mix, pq, "mix")
    pg = _mm(z_all, win_t, "NT", F32, "proj_g", m=T, n=2 * D, b_off=kv_cols + QC, after=mix_landed[1][4])
    wq_t, wkv_t, wbra_t, wbrb_t, wout = _gather_done(mix_landed, pg, "mix")
    ckv_n, kb2, vb2, kpe2 = _kprep_fwd(pkv, mla_kv_norm_g, gqa_k_norm_g, cos_a, ss_a, cos_b, ss_b, tr)
    kv_up = _mm(ckv_n, wkv_t, "NT", BF16, "kv_up")
    cq_n, qb2 = _qprep_fwd(pq, mla_q_norm_g, gqa_q_norm_g, lcos_b, lss_b, tr)
    q_a = _mm(cq_n, wq_t, "NT", F32, "q_up")
    qar = _qrope_fwd(q_a, lcos_a, lss_a, tr)

    a_q = [(qar, lambda h: 3 * (h // 2) + h % 2), (qar, lambda h: 3 * (h // 2) + 2)]
    a_k = [(kv_up, lambda h: 2 * h), (kpe2, lambda h: h % 2)]
    a_v = (kv_up, lambda h: 2 * h + 1)
    a_scale = float(MLA_NOPE + MLA_ROPE) ** -0.5
    b_q = [(qb2, lambda h: h)]
    b_k = [(kb2, lambda h: h)]
    b_v = (vb2, lambda h: h)
    b_scale = float(GQA_HEAD_DIM) ** -0.5
    tq_f = _pick(T, 512)
    o_a, lse_a = _attn_fwd(a_q, a_k, a_v, MLA_HEADS, 1, MLA_V, a_scale, "attn_a_fwd", tq_f)
    o_b, lse_b = _attn_fwd(b_q, b_k, b_v, GQA_HEADS, GQA_GROUP, GQA_HEAD_DIM, b_scale, "attn_b_fwd", tq_f)
    up_landed = _gather_land(ag_up, o_b, "up")
    ya = _mm(o_a, wbra_t, "NT", F32, "br_a", after=up_landed[1][4])
    yb = _mm(o_b, wbrb_t, "NT", F32, "br_b")
    merged = _gates_fwd(pg, ya, yb, tr)
    att = _mm(merged, wout, "NN", F32, "out_proj")
    x1, z2 = _resid_norm2_fwd(x2d, att, g1, norm2_g, sh2, sc2, tr)
    (wup_t,) = _gather_done(up_landed, z2, "up")
    down_landed = _gather_land(ag_down, z2, "down")
    u = _mm(z2, wup_t, "NT", BF16, "ffn_up", after=down_landed[1][4])
    tc = _pick(FF, 128)
    hg = _conv_fwd(u, cw, conv_b, tc)
    (wdown,) = _gather_done(down_landed, hg, "down")
    f = _mm(hg, wdown, "NN", F32, "ffn_down")
    sq, dx2, d_fg, d_g2, df = _loss_head(x1, f, g2, fg, tgt, tr)
    loss = lax.psum(0.5 * jnp.sum(sq) / D, ("x", "y", "c"))

    dhg = _mm(df, wdown, "NT", BF16, "ffn_down_dx")
    g_wdown = _mm(hg, df, "TN", BF16, "ffn_down_dw")
    du_a, du_b, dcw_a, dcw_b, dcb_a, dcb_b = _conv_bwd(u, dhg, cw, conv_b, tc)
    dz2 = _mm(du_a, wup_t, "NN", F32, "ffn_up_dx_a")
    dz2 = _mm(du_b, wup_t, "NN", F32, "ffn_up_dx_b", b_off=FF, add=dz2)
    g_wup_t = _mm(du_a, z2, "TN", BF16, "ffn_up_dw_a", out_rows=F2, tm=FF // 4)
    g_wup_t = _mm(du_b, z2, "TN", BF16, "ffn_up_dw_b", out_base=g_wup_t, out_off=FF, tm=FF // 4)
    sw_ffn = _swap_start([g_wdown, g_wup_t], sc2, "ffn")
    sc2 = sc2 + sw_ffn[4]
    dx1, datt, d_n2g, d_sh2, d_sc2, d_g1 = _resid_norm2_bwd(dz2, x1, dx2, att, norm2_g, sc2, g1, tr)

    dmerged = _mm(datt, wout, "NT", F32, "out_proj_dx")
    rs_ffn = _scatter_start_after_swap(sw_ffn, dmerged, "ffn")
    lse_a = lse_a + rs_ffn[4]
    g_wout = _mm(merged, datt, "TN", BF16, "out_proj_dw")
    dya, dyb, dpg = _gates_bwd(dmerged, pg, ya, yb, tr)
    do_a = _mm(dya, wbra_t, "NN", BF16, "br_a_dx")
    g_wbra_t = _mm(dya, o_a, "TN", BF16, "br_a_dw")
    do_b = _mm(dyb, wbrb_t, "NN", BF16, "br_b_dx")
    g_wbrb_t = _mm(dyb, o_b, "TN", BF16, "br_b_dw")
    dqa2, dka2, dva2 = _attn_bwd(a_q, a_k, a_v, o_a, do_a, lse_a, MLA_HEADS, 1, MLA_V, a_scale, "attn_a_bwd", tq)
    dqb2, dkb2, dvb2 = _attn_bwd(b_q, b_k, b_v, o_b, do_b, lse_b, GQA_HEADS, GQA_GROUP, GQA_HEAD_DIM, b_scale, "attn_b_bwd", tq)
    dq_a = _qrope_bwd(dqa2, lcos_a, lss_a, tr)
    dcq_n = _mm(dq_a, wq_t, "NN", F32, "q_up_dx")
    g_wq_t = _mm(dq_a, cq_n, "TN", BF16, "q_up_dw")
    dpq, d_qg, d_gq = _qprep_bwd(pq, dcq_n, dqb2, mla_q_norm_g, gqa_q_norm_g, lcos_b, lss_b, tr)
    dkv_up, dkpe = _kgrad_split(dka2, dva2, cos_a, ss_a, tr)
    dckv_n = _mm(dkv_up, wkv_t, "NN", F32, "kv_up_dx")
    g_wkv_t = _mm(dkv_up, ckv_n, "TN", BF16, "kv_up_dw")
    rs_mix = _scatter_start([g_wq_t, g_wkv_t, g_wbra_t, g_wbrb_t, g_wout], "mix")
    dpkv, d_kvg, d_kg = _kprep_bwd(pkv, dckv_n, dkb2, dvb2, dkpe, mla_kv_norm_g + rs_mix[4], gqa_k_norm_g, cos_b, ss_b, tr)
    dz_kv = _mm(dpkv, w_kvp, "NN", F32, "proj_kv_dx")
    dz_lat = _mm(dpq, win_t, "NN", F32, "proj_q_dx", b_off=kv_cols, add=dz_kv)
    dz_lat = _mm(dpg, win_t, "NN", F32, "proj_g_dx", b_off=kv_cols + QC, add=dz_lat)
    _, d_n1g_c, d_csh, d_csc = _norm_mod_bwd(dz_kv, T // tr, ctx2d, norm1_g, csc, None, "norm1_ctx_bwd", tr)
    grad_x, d_n1g_l, d_sh1, d_sc1 = _norm_mod_bwd(dz_lat, 0, x2d, norm1_g, sc1, dx1, "norm1_lat_bwd", tr)

    zeros_d = jnp.zeros((1, D), F32)
    d_lat = jnp.concatenate([d_sh1, d_sc1, d_g1, d_sh2, d_sc2, d_g2], axis=1)
    d_ctx_part = jnp.concatenate([d_csh, d_csc], axis=1)
    flat = jnp.concatenate(
        [d_n1g_c + d_n1g_l, d_qg, d_kvg, d_gq, d_kg, d_n2g, dcb_a, dcb_b, d_fg,
         dcw_a.reshape(1, -1), dcw_b.reshape(1, -1), d_ctx_part, d_lat], axis=1)
    n_flat = flat.shape[1]
    n_rows = -(-n_flat // (8 * LANES)) * 8
    flat = jnp.pad(flat, ((0, 0), (0, n_rows * LANES - n_flat))).reshape(n_rows, LANES)
    got = _all_gather_small(flat, "gather_small_grads")
    tot = _sum_slots(got, "sum_small_grads").reshape(1, -1)
    sizes = [D, MLA_Q_LORA, MLA_KV_LORA, GQA_HEAD_DIM, GQA_HEAD_DIM, D, F2, D, 3 * FF, 3 * FF, 2 * D]
    offs = [0]
    for s in sizes:
        offs.append(offs[-1] + s)
    t_n1g, t_qg, t_kvg, t_gq, t_kg, t_n2g, t_cb, t_fg, t_cwa, t_cwb, t_ctx = [tot[:, offs[k] : offs[k + 1]] for k in range(len(sizes))]
    g_cw_full = jnp.concatenate([t_cwa.reshape(3, FF), t_cwb.reshape(3, FF)], axis=1)
    g_cw = lax.dynamic_slice(g_cw_full, (0, j * NW), (3, NW))
    d_lat_all = got.reshape(8, -1)[:, offs[-1] : offs[-1] + 6 * D]
    g16 = jnp.concatenate([d_lat_all, jnp.pad(t_ctx, ((0, 0), (0, 4 * D))), jnp.zeros((7, 6 * D), F32)], axis=0)
    g_b_ada = _sum_slots(g16.reshape(16, 1, 6 * D), "sum_b_ada")
    g16_cols = lax.dynamic_slice(g16, (0, j * NA), (16, NA))
    ds_part = _mm(g16_cols, w_ada[0], "NT", F32, "ada_dx")
    got = _all_gather_small(ds_part[8:16], "gather_ada_dx")
    ds_ctx = _sum_slots(jnp.stack([got[2 * s] for s in range(4)]), "sum_ada_dx")[0:1]
    g_c_ctx = _silu_grad_mul(ds_ctx, cc)

    g_kvp = _mm(dpkv, z_all, "TN", BF16, "proj_kv_dw")
    nk = MLA_KV_LORA + 2 * GQA_KV_HEADS * GQA_HEAD_DIM
    g_kv = jnp.concatenate([g_kvp[:MLA_KV_LORA], g_kvp[nk : nk + MLA_ROPE], g_kvp[MLA_KV_LORA:nk]], axis=0)
    g_win_t = _mm(dpq, z_all, "TN", BF16, "proj_q_dw", out_rows=kv_cols + QC + 2 * D, out_off=kv_cols, tm=QC // 2)
    g_win_t = _mm(dpg, z_all, "TN", BF16, "proj_g_dw", out_base=g_win_t, out_off=kv_cols + QC)
    g_win_t = lax.dynamic_update_slice(g_win_t, g_kv, (0, 0))
    rs_in = _scatter_start([g_win_t], "in", after=got)

    h_ffn = _scatter_sums(rs_ffn, rs_in[2][0], "ffn")
    j_ffn = _join_start(h_ffn, grad_x, "ffn")
    h_mix = _scatter_sums(rs_mix, j_ffn[2][0], "mix")
    j_mix = _join_start(h_mix, j_ffn[2][0], "mix")
    g_w_ada = _mm(s16, g16_cols, "TN", F32, "ada_dw", act="silu", after=j_mix[4])
    _, d_ada, m_ada, v_ada = _adamw(w_ada[0], g_w_ada, m_w_ada[0], v_w_ada[0], "adamw_w_ada")
    r_wdown, r_wup = _join_wait(j_ffn, d_ada, "ffn")
    r_wq, r_wkv, r_wbra, r_wbrb, r_wout = _join_wait(j_mix, d_ada, "mix")
    gq_p = r_wq.T
    gq = jnp.concatenate([gq_p[:, : 2 * MLA_NOPE].reshape(MLA_Q_LORA, 2, MLA_NOPE), gq_p[:, 2 * MLA_NOPE :].reshape(MLA_Q_LORA, 2, MLA_ROPE)], axis=2)
    grads = {
        "c_ctx": g_c_ctx.reshape(D), "w_ada": g_w_ada[None], "b_ada": g_b_ada, "norm1_g": t_n1g,
        "mla_q_norm_g": t_qg, "w_q_up": gq.reshape(1, MLA_Q_LORA, -1), "mla_kv_norm_g": t_kvg, "w_kv_up": r_wkv,
        "gqa_q_norm_g": t_gq, "gqa_k_norm_g": t_kg, "w_br_a": r_wbra, "w_br_b": r_wbrb, "w_out": r_wout[None],
        "norm2_g": t_n2g, "w_up": r_wup, "conv_w": g_cw[None], "conv_b": t_cb, "w_down": r_wdown[None],
        "final_norm_g": t_fg.reshape(D),
    }
    arrives_transposed = ("w_kv_up", "w_br_a", "w_br_b", "w_up")
    weights = dict(c_ctx=c_ctx, w_ada=w_ada, b_ada=b_ada, norm1_g=norm1_g, w_in=w_in, mla_q_norm_g=mla_q_norm_g, w_q_up=w_q_up,
                   mla_kv_norm_g=mla_kv_norm_g, w_kv_up=w_kv_up, gqa_q_norm_g=gqa_q_norm_g, gqa_k_norm_g=gqa_k_norm_g, w_br_a=w_br_a,
                   w_br_b=w_br_b, w_out=w_out, norm2_g=norm2_g, w_up=w_up, conv_w=conv_w, conv_b=conv_b, w_down=w_down,
                   final_norm_g=final_norm_g)
    m_in = dict(c_ctx=m_c_ctx, w_ada=m_w_ada, b_ada=m_b_ada, norm1_g=m_norm1_g, w_in=m_w_in, mla_q_norm_g=m_mla_q_norm_g,
                w_q_up=m_w_q_up, mla_kv_norm_g=m_mla_kv_norm_g, w_kv_up=m_w_kv_up, gqa_q_norm_g=m_gqa_q_norm_g,
                gqa_k_norm_g=m_gqa_k_norm_g, w_br_a=m_w_br_a, w_br_b=m_w_br_b, w_out=m_w_out, norm2_g=m_norm2_g, w_up=m_w_up,
                conv_w=m_conv_w, conv_b=m_conv_b, w_down=m_w_down, final_norm_g=m_final_norm_g)
    v_in = dict(c_ctx=v_c_ctx, w_ada=v_w_ada, b_ada=v_b_ada, norm1_g=v_norm1_g, w_in=v_w_in, mla_q_norm_g=v_mla_q_norm_g,
                w_q_up=v_w_q_up, mla_kv_norm_g=v_mla_kv_norm_g, w_kv_up=v_w_kv_up, gqa_q_norm_g=v_gqa_q_norm_g,
                gqa_k_norm_g=v_gqa_k_norm_g, w_br_a=v_w_br_a, w_br_b=v_w_br_b, w_out=v_w_out, norm2_g=v_norm2_g, w_up=v_w_up,
                conv_w=v_conv_w, conv_b=v_conv_b, w_down=v_w_down, final_norm_g=v_final_norm_g)
    names = list(weights)
    big = [n for n in names if weights[n].ndim == 3 and weights[n].shape[1] >= 8]
    small = [n for n in names if n not in big]
    delta, new_m, new_v = {}, {}, {}

    def update(n):
        shp = weights[n].shape
        two_d = lambda a: a.reshape(shp[1], shp[2])
        g_t = n in arrives_transposed
        g_in = grads[n] if g_t else two_d(grads[n].astype(F32))
        g_, d_, m_, v_ = _adamw(two_d(weights[n]), g_in, two_d(m_in[n]), two_d(v_in[n]), "adamw_" + n, g_transposed=g_t)
        grads[n], delta[n], new_m[n], new_v[n] = g_.reshape(shp), d_.reshape(shp), m_.reshape(shp), v_.reshape(shp)

    delta["w_ada"], new_m["w_ada"], new_v["w_ada"] = d_ada[None], m_ada[None], v_ada[None]
    early = [n for n in big if n not in ("w_in", "w_ada")]
    for n in early:
        update(n)
    done = sum(delta[n][0, 0:1, 0:1] for n in early)
    (r_win,) = _scatter_finish(rs_in, done, "in")
    _, d_, m_, v_ = _adamw(w_in[0].T, r_win, m_w_in[0].T, v_w_in[0].T, "adamw_w_in")
    grads["w_in"], delta["w_in"], new_m["w_in"], new_v["w_in"] = r_win.T[None], d_.T[None], m_.T[None], v_.T[None]
    grads = {n: grads[n].reshape(weights[n].shape).astype(F32) for n in names}

    def pack(tree):
        flat_ = jnp.concatenate([tree[n].reshape(-1) for n in small])
        rows = -(-flat_.shape[0] // (8 * LANES)) * 8
        return jnp.pad(flat_, (0, rows * LANES - flat_.shape[0])).reshape(rows, LANES)

    _, d_, m_, v_ = _adamw(pack(weights), pack(grads), pack(m_in), pack(v_in), "adamw_small")
    off = 0
    for n in small:
        size = weights[n].size
        shp = weights[n].shape
        delta[n] = d_.reshape(-1)[off : off + size].reshape(shp)
        new_m[n] = m_.reshape(-1)[off : off + size].reshape(shp)
        new_v[n] = v_.reshape(-1)[off : off + size].reshape(shp)
        off += size

    return (loss, grad_x[None], *[grads[n] for n in names], *[delta[n] for n in names], *[new_m[n] for n in names],
            *[new_v[n] for n in names])
```

```python
import math

import jax
import jax.numpy as jnp
from jax import lax
from jax.experimental import pallas as pl
from jax.experimental.pallas import tpu as pltpu

F32 = jnp.float32
BF16 = jnp.bfloat16
MESH = pl.DeviceIdType.MESH

NORM_EPS = 1e-6
ROPE_THETA = 10000.0
GRID_W = 64
MLA_HEADS = 8
MLA_Q_LORA = 768
MLA_KV_LORA = 512
MLA_NOPE = 128
MLA_ROPE = 64
MLA_V = 128
GQA_HEADS = 8
GQA_KV_HEADS = 2
GQA_HEAD_DIM = 128
GQA_GROUP = GQA_HEADS // GQA_KV_HEADS
LANES = 128
KVP = MLA_KV_LORA + 2 * GQA_KV_HEADS * GQA_HEAD_DIM + LANES
QC = MLA_Q_LORA + GQA_HEADS * GQA_HEAD_DIM

ADAM_LR = 0.001
ADAM_B1 = 0.9
ADAM_B2 = 0.999
ADAM_EPS = 1e-08
ADAM_WD = 0.01
ADAM_STEP = 10

VMEM_LIMIT = 56 * 1024 * 1024


def _pick(dim, target, mult=LANES):
    t = (min(target, dim) // mult) * mult
    while t >= mult:
        if dim % t == 0:
            return t
        t -= mult
    return dim


def _params(sem):
    return pltpu.CompilerParams(dimension_semantics=sem, vmem_limit_bytes=VMEM_LIMIT)


_DIMS = {"NN": (((1,), (0,)), ((), ())), "NT": (((1,), (1,)), ((), ())), "TN": (((0,), (0,)), ((), ()))}


MM_VMEM_BUDGET = 36 * 1024 * 1024


def _mm_tiles(M, N, K, sa, sb, so, tm, tn, tk):
    tm, tn, tk = _pick(M, tm), _pick(N, tn), _pick(K, tk)

    def need(t):
        return 2 * (tm * t * sa + t * tn * sb) + 2 * tm * tn * so + (tm * tn * 4 if t < K else 0)

    while need(tk) > MM_VMEM_BUDGET and tk > LANES:
        smaller = _pick(K, tk - LANES)
        if smaller >= tk:
            break
        tk = smaller
    return tm, tn, tk


def _window(block, index, offsets):
    if not any(offsets):
        return pl.BlockSpec(block, index)
    for t, o in zip(block, offsets):
        assert o % 16 == 0 and t % 16 == 0, (block, offsets)

    def at(i, j, k):
        return tuple(pl.multiple_of(o + p * t, math.gcd(o, t)) for p, t, o in zip(index(i, j, k), block, offsets))

    return pl.BlockSpec(tuple(pl.Element(t) for t in block), at)


def _mm(a, b, mode, out_dtype, name, m=None, n=None, k=None, b_off=0, add=None, out_rows=None, out_base=None, out_off=0,
        tm=1024, tn=1024, tk=2304, act=None, bias=None, after=None):
    if mode == "NN":
        M, K, N = m or a.shape[0], k or a.shape[1], b.shape[1]
    elif mode == "NT":
        M, K, N = m or a.shape[0], a.shape[1], n or b.shape[0]
    else:
        M, K, N = a.shape[1], k or a.shape[0], b.shape[1]
    tm, tn, tk = _mm_tiles(M, N, K, a.dtype.itemsize, b.dtype.itemsize, jnp.dtype(out_dtype).itemsize, tm, tn, tk)
    nk = K // tk
    dims = _DIMS[mode]
    n_in = 2 + (bias is not None) + (add is not None) + (out_base is not None) + (after is not None)

    def body(*refs):
        a_ref, b_ref = refs[:2]
        bias_ref = refs[2] if bias is not None else None
        add_ref = refs[2 + (bias is not None)] if add is not None else None
        o_ref = refs[n_in]
        av = a_ref[...]
        if act == "silu":
            av = av * jax.nn.sigmoid(av)
        part = lax.dot_general(av.astype(BF16), b_ref[...].astype(BF16), dims, preferred_element_type=F32)

        def finish(r):
            if bias is not None:
                r = r + bias_ref[...]
            if add is not None:
                r = r + add_ref[...]
            o_ref[...] = r.astype(out_dtype)

        if nk == 1:
            finish(part)
            return
        acc = refs[-1]
        k = pl.program_id(2)

        @pl.when(k == 0)
        def _():
            acc[...] = part

        @pl.when(jnp.logical_and(k > 0, k < nk - 1))
        def _():
            acc[...] += part

        @pl.when(k == nk - 1)
        def _():
            finish(acc[...] + part)

    a_spec = pl.BlockSpec((tk, tm), lambda i, j, k: (k, i)) if mode == "TN" else pl.BlockSpec((tm, tk), lambda i, j, k: (i, k))
    if mode == "NT":
        b_spec = _window((tn, tk), lambda i, j, k: (j, k), (b_off, 0))
    else:
        b_spec = _window((tk, tn), lambda i, j, k: (k, j), (b_off, 0))
    in_specs, args = [a_spec, b_spec], [a, b]
    if bias is not None:
        in_specs.append(pl.BlockSpec((1, tn), lambda i, j, k: (0, j)))
        args.append(bias)
    if add is not None:
        in_specs.append(pl.BlockSpec((tm, tn), lambda i, j, k: (i, j)))
        args.append(add)
    aliases = {}
    if after is not None:
        in_specs.append(pl.BlockSpec(after.shape, lambda i, j, k: (0, 0)))
        args.append(after)
    if out_base is not None:
        aliases = {len(args): 0}
        in_specs.append(ANY)
        args.append(out_base)
        out_rows = out_base.shape[0]
    return pl.pallas_call(
        body,
        name=name,
        grid=(M // tm, N // tn, nk),
        in_specs=in_specs,
        out_specs=_window((tm, tn), lambda i, j, k: (i, j), (out_off, 0)),
        out_shape=jax.ShapeDtypeStruct((out_rows or M, N), out_dtype),
        input_output_aliases=aliases,
        scratch_shapes=[pltpu.VMEM((tm, tn), F32)] if nk > 1 else [],
        compiler_params=_params(("parallel", "parallel", "arbitrary")),
    )(*args)


def _rms(x):
    r = lax.rsqrt(jnp.mean(x * x, axis=-1, keepdims=True) + NORM_EPS)
    return x * r, r


def _rms_bwd(xh, r, dxh):
    return r * (dxh - xh * jnp.mean(dxh * xh, axis=-1, keepdims=True))


def _swap(x, q):
    lane = lax.broadcasted_iota(jnp.int32, x.shape, 1)
    even = ((lane // q) % 2) == 0
    return jnp.where(even, pltpu.roll(x, LANES - q, 1), pltpu.roll(x, q, 1))


def _rope(x, cos, ss, q):
    return x * cos + _swap(x, q) * ss


def _rope_t(d, cos, ss, q):
    return d * cos + _swap(d * ss, q)


def _csum(x):
    return jnp.sum(x, axis=0, keepdims=True)


def _rows(tr, w, off=0):
    return pl.BlockSpec((tr, w), lambda i: (i + off, 0))


def _bcast(w):
    return pl.BlockSpec((1, w), lambda i: (0, 0))


def _acc_init(i, refs):
    @pl.when(i == 0)
    def _():
        for r in refs:
            r[...] = jnp.zeros_like(r)


def _rope_tables(n_ctx, n_lat, rot_dim):
    rows = n_lat // GRID_W
    row = jnp.repeat(jnp.arange(rows, dtype=F32), GRID_W)
    col = jnp.tile(jnp.arange(GRID_W, dtype=F32), rows)
    half = rot_dim // 2
    inv_freq = ROPE_THETA ** (-jnp.arange(0, half, 2, dtype=F32) / half)
    ar, ac = row[:, None] * inv_freq, col[:, None] * inv_freq
    cos = jnp.concatenate([jnp.cos(ar), jnp.cos(ar), jnp.cos(ac), jnp.cos(ac)], axis=-1)
    ss = jnp.concatenate([-jnp.sin(ar), jnp.sin(ar), -jnp.sin(ac), jnp.sin(ac)], axis=-1)
    cos = jnp.tile(cos, (1, LANES // rot_dim))
    ss = jnp.tile(ss, (1, LANES // rot_dim))
    cos = jnp.concatenate([cos, jnp.ones((n_ctx, LANES), F32)], axis=0)
    ss = jnp.concatenate([ss, jnp.zeros((n_ctx, LANES), F32)], axis=0)
    return cos, ss


def _norm_mod_fwd(x2d, g, sh, sc, name, tr, out_rows=None, base=None, out_off=0):
    n, d = x2d.shape

    def body(x_ref, g_ref, sh_ref, sc_ref, *rest):
        xh, _ = _rms(x_ref[...])
        rest[-1][...] = ((xh * g_ref[...]) * (1.0 + sc_ref[...]) + sh_ref[...]).astype(BF16)

    args, in_specs, aliases = [x2d, g, sh, sc], [_rows(tr, d), _bcast(d), _bcast(d), _bcast(d)], {}
    if base is not None:
        args.append(base)
        in_specs.append(ANY)
        aliases = {4: 0}
        out_rows = base.shape[0]
    return pl.pallas_call(
        body,
        name=name,
        grid=(n // tr,),
        in_specs=in_specs,
        out_specs=_rows(tr, d, out_off // tr),
        out_shape=jax.ShapeDtypeStruct((out_rows or n, d), BF16),
        input_output_aliases=aliases,
        compiler_params=_params(("parallel",)),
    )(*args)


def _norm_mod_bwd(dz, dz_off, x2d, g, sc, dres, name, tr):
    n, d = x2d.shape
    want_dx = dres is not None

    def body(*refs):
        if want_dx:
            dz_ref, x_ref, g_ref, sc_ref, dres_ref, dx_ref, dg_ref, dsh_ref, dsc_ref = refs
        else:
            dz_ref, x_ref, g_ref, sc_ref, dg_ref, dsh_ref, dsc_ref = refs
        _acc_init(pl.program_id(0), [dg_ref, dsh_ref, dsc_ref])
        xh, r = _rms(x_ref[...])
        dzv = dz_ref[...]
        gv = g_ref[...]
        dsc_ref[...] += _csum(dzv * (xh * gv))
        dsh_ref[...] += _csum(dzv)
        dh = dzv * (1.0 + sc_ref[...])
        dg_ref[...] += _csum(dh * xh)
        if want_dx:
            dx_ref[...] = _rms_bwd(xh, r, dh * gv) + dres_ref[...]

    in_specs = [_rows(tr, d, dz_off), _rows(tr, d), _bcast(d), _bcast(d)]
    args = [dz, x2d, g, sc]
    out_specs = [_bcast(d)] * 3
    out_shape = [jax.ShapeDtypeStruct((1, d), F32)] * 3
    if want_dx:
        in_specs.append(_rows(tr, d))
        args.append(dres)
        out_specs = [_rows(tr, d)] + out_specs
        out_shape = [jax.ShapeDtypeStruct((n, d), F32)] + out_shape
    res = pl.pallas_call(
        body,
        name=name,
        grid=(n // tr,),
        in_specs=in_specs,
        out_specs=out_specs,
        out_shape=out_shape,
        compiler_params=_params(("arbitrary",)),
    )(*args)
    return res if want_dx else (None, *res)


_QA, _QB = MLA_ROPE // 4, GQA_HEAD_DIM // 4


def _kprep_fwd(pkv, kvg, kg, cos_a, ss_a, cos_b, ss_b, tr):
    n = pkv.shape[0]
    nb = GQA_KV_HEADS * GQA_HEAD_DIM

    def body(p_ref, kvg_ref, kg_ref, ca, sa, cb, sb, ckv_ref, kb_ref, vb_ref, kpe_ref):
        p = p_ref[...]
        xh, _ = _rms(p[:, :MLA_KV_LORA])
        ckv_ref[...] = (xh * kvg_ref[...]).astype(BF16)
        for e in range(GQA_KV_HEADS):
            lo = MLA_KV_LORA + e * GQA_HEAD_DIM
            kh, _ = _rms(p[:, lo : lo + GQA_HEAD_DIM])
            kb_ref[:, e * GQA_HEAD_DIM : (e + 1) * GQA_HEAD_DIM] = _rope(kh * kg_ref[...], cb[...], sb[...], _QB).astype(BF16)
        vb_ref[...] = p[:, MLA_KV_LORA + nb : MLA_KV_LORA + 2 * nb].astype(BF16)
        kr = _rope(p[:, MLA_KV_LORA + 2 * nb :], ca[...], sa[...], _QA)
        kpe_ref[:, :LANES] = kr.astype(BF16)
        kpe_ref[:, LANES:] = pltpu.roll(kr, MLA_ROPE, 1).astype(BF16)

    return pl.pallas_call(
        body,
        name="kprep_fwd",
        grid=(n // tr,),
        in_specs=[_rows(tr, KVP), _bcast(MLA_KV_LORA), _bcast(GQA_HEAD_DIM)] + [_rows(tr, LANES)] * 4,
        out_specs=[_rows(tr, MLA_KV_LORA), _rows(tr, nb), _rows(tr, nb), _rows(tr, 2 * LANES)],
        out_shape=[jax.ShapeDtypeStruct((n, w), BF16) for w in (MLA_KV_LORA, nb, nb, 2 * LANES)],
        compiler_params=_params(("parallel",)),
    )(pkv, kvg, kg, cos_a, ss_a, cos_b, ss_b)


def _kprep_bwd(pkv, dckv, dkb, dvb, dkpe, kvg, kg, cos_b, ss_b, tr):
    n = pkv.shape[0]
    nb = GQA_KV_HEADS * GQA_HEAD_DIM

    def body(p_ref, dckv_ref, dkb_ref, dvb_ref, dkpe_ref, kvg_ref, kg_ref, cb, sb, dp_ref, dkvg_ref, dkg_ref):
        _acc_init(pl.program_id(0), [dkvg_ref, dkg_ref])
        p = p_ref[...]
        xh, r = _rms(p[:, :MLA_KV_LORA])
        dn = dckv_ref[...]
        dkvg_ref[...] += _csum(dn * xh)
        dp_ref[:, :MLA_KV_LORA] = _rms_bwd(xh, r, dn * kvg_ref[...]).astype(BF16)
        for e in range(GQA_KV_HEADS):
            lo = MLA_KV_LORA + e * GQA_HEAD_DIM
            kh, rk = _rms(p[:, lo : lo + GQA_HEAD_DIM])
            dk = _rope_t(dkb_ref[:, e * GQA_HEAD_DIM : (e + 1) * GQA_HEAD_DIM], cb[...], sb[...], _QB)
            dkg_ref[...] += _csum(dk * kh)
            dp_ref[:, lo : lo + GQA_HEAD_DIM] = _rms_bwd(kh, rk, dk * kg_ref[...]).astype(BF16)
        dp_ref[:, MLA_KV_LORA + nb : MLA_KV_LORA + 2 * nb] = dvb_ref[...].astype(BF16)
        dp_ref[:, MLA_KV_LORA + 2 * nb :] = dkpe_ref[...].astype(BF16)

    return pl.pallas_call(
        body,
        name="kprep_bwd",
        grid=(n // tr,),
        in_specs=[_rows(tr, KVP), _rows(tr, MLA_KV_LORA), _rows(tr, nb), _rows(tr, nb), _rows(tr, LANES),
                  _bcast(MLA_KV_LORA), _bcast(GQA_HEAD_DIM), _rows(tr, LANES), _rows(tr, LANES)],
        out_specs=[_rows(tr, KVP), _bcast(MLA_KV_LORA), _bcast(GQA_HEAD_DIM)],
        out_shape=[jax.ShapeDtypeStruct((n, KVP), BF16), jax.ShapeDtypeStruct((1, MLA_KV_LORA), F32),
                   jax.ShapeDtypeStruct((1, GQA_HEAD_DIM), F32)],
        compiler_params=_params(("arbitrary",)),
    )(pkv, dckv, dkb, dvb, dkpe, kvg, kg, cos_b, ss_b)


def _kgrad_split(dka, dva, cos_a, ss_a, tr):
    n = dka.shape[0]
    wk = MLA_HEADS * 2 * LANES

    def body(dk_ref, dv_ref, ca, sa, dkv_ref, dkpe_ref):
        even = jnp.zeros((tr, LANES), F32)
        odd = jnp.zeros((tr, LANES), F32)
        for h in range(MLA_HEADS):
            dkv_ref[:, 2 * h * LANES : (2 * h + 1) * LANES] = dk_ref[:, 2 * h * LANES : (2 * h + 1) * LANES].astype(BF16)
            dkv_ref[:, (2 * h + 1) * LANES : (2 * h + 2) * LANES] = dv_ref[:, h * MLA_V : (h + 1) * MLA_V].astype(BF16)
            part = dk_ref[:, (2 * h + 1) * LANES : (2 * h + 2) * LANES]
            if h % 2 == 0:
                even = even + part
            else:
                odd = odd + part
        lane = lax.broadcasted_iota(jnp.int32, (tr, LANES), 1)
        low = lane < MLA_ROPE
        both = jnp.where(low, even, odd)
        tot = jnp.where(low, both + pltpu.roll(both, MLA_ROPE, 1), 0.0)
        dkpe_ref[...] = _rope_t(tot, ca[...], sa[...], _QA)

    return pl.pallas_call(
        body,
        name="kgrad_split",
        grid=(n // tr,),
        in_specs=[_rows(tr, wk), _rows(tr, MLA_HEADS * MLA_V), _rows(tr, LANES), _rows(tr, LANES)],
        out_specs=[_rows(tr, wk), _rows(tr, LANES)],
        out_shape=[jax.ShapeDtypeStruct((n, wk), BF16), jax.ShapeDtypeStruct((n, LANES), F32)],
        compiler_params=_params(("parallel",)),
    )(dka, dva, cos_a, ss_a)


def _qprep_fwd(pq, qg, gq, cos_b, ss_b, tr):
    n = pq.shape[0]
    nq = GQA_HEADS * GQA_HEAD_DIM

    def body(p_ref, qg_ref, gq_ref, cb, sb, cq_ref, qb_ref):
        xh, _ = _rms(p_ref[:, :MLA_Q_LORA])
        cq_ref[...] = (xh * qg_ref[...]).astype(BF16)
        for h in range(GQA_HEADS):
            lo = MLA_Q_LORA + h * GQA_HEAD_DIM
            qh, _ = _rms(p_ref[:, lo : lo + GQA_HEAD_DIM])
            qb_ref[:, h * GQA_HEAD_DIM : (h + 1) * GQA_HEAD_DIM] = _rope(qh * gq_ref[...], cb[...], sb[...], _QB).astype(BF16)

    return pl.pallas_call(
        body,
        name="qprep_fwd",
        grid=(n // tr,),
        in_specs=[_rows(tr, QC), _bcast(MLA_Q_LORA), _bcast(GQA_HEAD_DIM), _rows(tr, LANES), _rows(tr, LANES)],
        out_specs=[_rows(tr, MLA_Q_LORA), _rows(tr, nq)],
        out_shape=[jax.ShapeDtypeStruct((n, MLA_Q_LORA), BF16), jax.ShapeDtypeStruct((n, nq), BF16)],
        compiler_params=_params(("parallel",)),
    )(pq, qg, gq, cos_b, ss_b)


def _qprep_bwd(pq, dcq, dqb, qg, gq, cos_b, ss_b, tr):
    n = pq.shape[0]
    nq = GQA_HEADS * GQA_HEAD_DIM

    def body(p_ref, dcq_ref, dqb_ref, qg_ref, gq_ref, cb, sb, dp_ref, dqg_ref, dgq_ref):
        _acc_init(pl.program_id(0), [dqg_ref, dgq_ref])
        xh, r = _rms(p_ref[:, :MLA_Q_LORA])
        dn = dcq_ref[...]
        dqg_ref[...] += _csum(dn * xh)
        dp_ref[:, :MLA_Q_LORA] = _rms_bwd(xh, r, dn * qg_ref[...]).astype(BF16)
        for h in range(GQA_HEADS):
            lo = MLA_Q_LORA + h * GQA_HEAD_DIM
            qh, rq = _rms(p_ref[:, lo : lo + GQA_HEAD_DIM])
            dq = _rope_t(dqb_ref[:, h * GQA_HEAD_DIM : (h + 1) * GQA_HEAD_DIM], cb[...], sb[...], _QB)
            dgq_ref[...] += _csum(dq * qh)
            dp_ref[:, lo : lo + GQA_HEAD_DIM] = _rms_bwd(qh, rq, dq * gq_ref[...]).astype(BF16)

    return pl.pallas_call(
        body,
        name="qprep_bwd",
        grid=(n // tr,),
        in_specs=[_rows(tr, QC), _rows(tr, MLA_Q_LORA), _rows(tr, nq), _bcast(MLA_Q_LORA), _bcast(GQA_HEAD_DIM),
                  _rows(tr, LANES), _rows(tr, LANES)],
        out_specs=[_rows(tr, QC), _bcast(MLA_Q_LORA), _bcast(GQA_HEAD_DIM)],
        out_shape=[jax.ShapeDtypeStruct((n, QC), BF16), jax.ShapeDtypeStruct((1, MLA_Q_LORA), F32),
                   jax.ShapeDtypeStruct((1, GQA_HEAD_DIM), F32)],
        compiler_params=_params(("arbitrary",)),
    )(pq, dcq, dqb, qg, gq, cos_b, ss_b)


_QA_COLS = MLA_HEADS * (MLA_NOPE + MLA_ROPE)


def _qrope_fwd(qa, cos_a, ss_a, tr):
    n = qa.shape[0]

    def body(q_ref, ca, sa, o_ref):
        for j in range(MLA_HEADS // 2):
            lo = 3 * j * LANES
            o_ref[:, lo : lo + 2 * LANES] = q_ref[:, lo : lo + 2 * LANES].astype(BF16)
            o_ref[:, lo + 2 * LANES : lo + 3 * LANES] = _rope(q_ref[:, lo + 2 * LANES : lo + 3 * LANES], ca[...], sa[...], _QA).astype(BF16)

    return pl.pallas_call(
        body,
        name="qrope_fwd",
        grid=(n // tr,),
        in_specs=[_rows(tr, _QA_COLS), _rows(tr, LANES), _rows(tr, LANES)],
        out_specs=_rows(tr, _QA_COLS),
        out_shape=jax.ShapeDtypeStruct((n, _QA_COLS), BF16),
        compiler_params=_params(("parallel",)),
    )(qa, cos_a, ss_a)


def _qrope_bwd(dq2, cos_a, ss_a, tr):
    n = dq2.shape[0]

    def body(d_ref, ca, sa, o_ref):
        for j in range(MLA_HEADS // 2):
            lo = 3 * j * LANES
            h0, h1 = 2 * j, 2 * j + 1
            o_ref[:, lo : lo + LANES] = d_ref[:, 2 * h0 * LANES : (2 * h0 + 1) * LANES].astype(BF16)
            o_ref[:, lo + LANES : lo + 2 * LANES] = d_ref[:, 2 * h1 * LANES : (2 * h1 + 1) * LANES].astype(BF16)
            pe = d_ref[:, (2 * h0 + 1) * LANES : (2 * h0 + 2) * LANES] + d_ref[:, (2 * h1 + 1) * LANES : (2 * h1 + 2) * LANES]
            o_ref[:, lo + 2 * LANES : lo + 3 * LANES] = _rope_t(pe, ca[...], sa[...], _QA).astype(BF16)

    return pl.pallas_call(
        body,
        name="qrope_bwd",
        grid=(n // tr,),
        in_specs=[_rows(tr, MLA_HEADS * 2 * LANES), _rows(tr, LANES), _rows(tr, LANES)],
        out_specs=_rows(tr, _QA_COLS),
        out_shape=jax.ShapeDtypeStruct((n, _QA_COLS), BF16),
        compiler_params=_params(("parallel",)),
    )(dq2, cos_a, ss_a)


def _cat(refs):
    vals = [r[...] for r in refs]
    return vals[0] if len(vals) == 1 else jnp.concatenate(vals, axis=-1)


LOG2E = 1.4426950408889634


def _attn_fwd(qparts, kparts, vpart, n_heads, group, dv, scale, name, tq, after=None):
    T, Tk = qparts[0][0].shape[0], kparts[0][0].shape[0]
    nq_, nk_ = len(qparts), len(kparts)
    sub = min(tq, 256)
    c2 = scale * LOG2E

    def body(*refs):
        q_refs, k_refs = refs[:nq_], refs[nq_ : nq_ + nk_]
        v_ref = refs[nq_ + nk_]
        o_ref, lse_ref = refs[-2:]
        k = _cat(k_refs)
        v = v_ref[...]
        for r0 in range(0, tq, sub):
            q = _cat([r.at[r0 : r0 + sub, :] for r in q_refs])
            s = lax.dot_general(q, k, _DIMS["NT"], preferred_element_type=F32)
            m = jnp.max(s, axis=-1, keepdims=True)
            p = jnp.exp2((s - m) * c2)
            l = jnp.sum(p, axis=-1, keepdims=True)
            acc = jnp.dot(p.astype(BF16), v, preferred_element_type=F32)
            o_ref[r0 : r0 + sub, :] = (acc * (1.0 / l)).astype(BF16)
            lse_ref[r0 : r0 + sub, :] = m * scale + jnp.log(l)

    in_specs = [pl.BlockSpec((tq, LANES), lambda h, i, f=f: (i, f(h))) for _, f in qparts]
    in_specs += [pl.BlockSpec((Tk, LANES), lambda h, i, f=f: (0, f(h // group))) for _, f in kparts]
    fv = vpart[1]
    in_specs.append(pl.BlockSpec((Tk, dv), lambda h, i: (0, fv(h // group))))
    args = [*[a for a, _ in qparts], *[a for a, _ in kparts], vpart[0]]
    if after is not None:
        in_specs.append(pl.BlockSpec(after.shape, lambda h, i: (0, 0)))
        args.append(after)
    return pl.pallas_call(
        body,
        name=name,
        grid=(n_heads, T // tq),
        in_specs=in_specs,
        out_specs=[pl.BlockSpec((tq, dv), lambda h, i: (i, h)), pl.BlockSpec((None, tq, 1), lambda h, i: (h, i, 0))],
        out_shape=[jax.ShapeDtypeStruct((T, n_heads * dv), BF16), jax.ShapeDtypeStruct((n_heads, T, 1), F32)],
        compiler_params=_params(("parallel", "parallel")),
    )(*args)


def _attn_bwd(qparts, kparts, vpart, o, do, lse, n_heads, group, dv, scale, name, tq):
    T, Tk = qparts[0][0].shape[0], kparts[0][0].shape[0]
    nq_, nk_ = len(qparts), len(kparts)
    dk_ = LANES * nq_
    n_kv = n_heads // group
    nblk = T // tq
    c2 = scale * LOG2E

    def head(hk, i):
        return hk * group + i // nblk

    sub = min(tq, 256)

    def body(*refs):
        q_refs = refs[:nq_]
        k = _cat(refs[nq_ : nq_ + nk_])
        v_ref, o_ref, do_ref, lse_ref, dq_ref, dk_ref, dv_ref = refs[nq_ + nk_ :]
        i = pl.program_id(1)
        _acc_init(i, [dk_ref, dv_ref])
        v = v_ref[...]
        dk_acc, dv_acc = None, None
        for r0 in range(0, tq, sub):
            rows = slice(r0, r0 + sub)
            q = _cat([r.at[rows, :] for r in q_refs])
            s = lax.dot_general(q, k, _DIMS["NT"], preferred_element_type=F32)
            p = jnp.exp2(s * c2 - lse_ref[rows, :] * LOG2E)
            dov = do_ref[rows, :]
            dp = lax.dot_general(dov, v, _DIMS["NT"], preferred_element_type=F32)
            delta = jnp.sum(dov.astype(F32) * o_ref[rows, :].astype(F32), axis=-1, keepdims=True)
            ds = (p * (dp - delta)).astype(BF16)
            dq_ref[rows, :] = jnp.dot(ds, k, preferred_element_type=F32) * scale
            dk_part = lax.dot_general(ds, q, _DIMS["TN"], preferred_element_type=F32)
            dv_part = lax.dot_general(p.astype(BF16), dov, _DIMS["TN"], preferred_element_type=F32)
            dk_acc = dk_part if dk_acc is None else dk_acc + dk_part
            dv_acc = dv_part if dv_acc is None else dv_acc + dv_part
        dk_ref[...] += dk_acc
        dv_ref[...] += dv_acc

        @pl.when(i == group * nblk - 1)
        def _():
            dk_ref[...] *= scale

    in_specs = [pl.BlockSpec((tq, LANES), lambda hk, i, f=f: (i % nblk, f(head(hk, i)))) for _, f in qparts]
    in_specs += [pl.BlockSpec((Tk, LANES), lambda hk, i, f=f: (0, f(hk))) for _, f in kparts]
    fv = vpart[1]
    in_specs.append(pl.BlockSpec((Tk, dv), lambda hk, i: (0, fv(hk))))
    in_specs += [pl.BlockSpec((tq, dv), lambda hk, i: (i % nblk, head(hk, i)))] * 2
    in_specs.append(pl.BlockSpec((None, tq, 1), lambda hk, i: (head(hk, i), i % nblk, 0)))
    return pl.pallas_call(
        body,
        name=name,
        grid=(n_kv, group * nblk),
        in_specs=in_specs,
        out_specs=[pl.BlockSpec((tq, dk_), lambda hk, i: (i % nblk, head(hk, i))),
                   pl.BlockSpec((Tk, dk_), lambda hk, i: (0, hk)),
                   pl.BlockSpec((Tk, dv), lambda hk, i: (0, hk))],
        out_shape=[jax.ShapeDtypeStruct((T, n_heads * dk_), F32), jax.ShapeDtypeStruct((Tk, n_kv * dk_), F32),
                   jax.ShapeDtypeStruct((Tk, n_kv * dv), F32)],
        compiler_params=_params(("parallel", "arbitrary")),
    )(*[a for a, _ in qparts], *[a for a, _ in kparts], vpart[0], o, do, lse)


def _gates_fwd(pg, ya, yb, tr):
    n, d = ya.shape

    def body(pg_ref, ya_ref, yb_ref, o_ref):
        ga = jax.nn.sigmoid(pg_ref[:, :d].astype(F32))
        gb = jax.nn.sigmoid(pg_ref[:, d:].astype(F32))
        o_ref[...] = (ga * ya_ref[...].astype(F32) + gb * yb_ref[...].astype(F32)).astype(BF16)

    return pl.pallas_call(
        body,
        name="gates_fwd",
        grid=(n // tr,),
        in_specs=[_rows(tr, 2 * d), _rows(tr, d), _rows(tr, d)],
        out_specs=_rows(tr, d),
        out_shape=jax.ShapeDtypeStruct((n, d), BF16),
        compiler_params=_params(("parallel",)),
    )(pg, ya, yb)


def _gates_bwd(dm, pg, ya, yb, tr):
    n, d = ya.shape

    def body(dm_ref, pg_ref, ya_ref, yb_ref, dya_ref, dyb_ref, dpg_ref):
        dmv = dm_ref[...].astype(F32)
        ga = jax.nn.sigmoid(pg_ref[:, :d].astype(F32))
        gb = jax.nn.sigmoid(pg_ref[:, d:].astype(F32))
        dya_ref[...] = (dmv * ga).astype(BF16)
        dyb_ref[...] = (dmv * gb).astype(BF16)
        dpg_ref[:, :d] = (dmv * ya_ref[...].astype(F32) * ga * (1.0 - ga)).astype(BF16)
        dpg_ref[:, d:] = (dmv * yb_ref[...].astype(F32) * gb * (1.0 - gb)).astype(BF16)

    return pl.pallas_call(
        body,
        name="gates_bwd",
        grid=(n // tr,),
        in_specs=[_rows(tr, d), _rows(tr, 2 * d), _rows(tr, d), _rows(tr, d)],
        out_specs=[_rows(tr, d), _rows(tr, d), _rows(tr, 2 * d)],
        out_shape=[jax.ShapeDtypeStruct((n, d), BF16), jax.ShapeDtypeStruct((n, d), BF16), jax.ShapeDtypeStruct((n, 2 * d), BF16)],
        compiler_params=_params(("parallel",)),
    )(dm, pg, ya, yb)


def _resid_norm2_fwd(x2d, att, g1, n2g, sh2, sc2, tr):
    n, d = x2d.shape

    def body(x_ref, a_ref, g1_ref, g_ref, sh_ref, sc_ref, x1_ref, z_ref):
        x1 = x_ref[...] + g1_ref[...] * a_ref[...]
        x1_ref[...] = x1
        xh, _ = _rms(x1)
        z_ref[...] = ((xh * g_ref[...]) * (1.0 + sc_ref[...]) + sh_ref[...]).astype(BF16)

    return pl.pallas_call(
        body,
        name="resid_norm2_fwd",
        grid=(n // tr,),
        in_specs=[_rows(tr, d), _rows(tr, d)] + [_bcast(d)] * 4,
        out_specs=[_rows(tr, d), _rows(tr, d)],
        out_shape=[jax.ShapeDtypeStruct((n, d), F32), jax.ShapeDtypeStruct((n, d), BF16)],
        compiler_params=_params(("parallel",)),
    )(x2d, att, g1, n2g, sh2, sc2)


def _resid_norm2_bwd(dz2, x1, dx2, att, n2g, sc2, g1, tr):
    n, d = x1.shape

    def body(dz_ref, x1_ref, dx2_ref, a_ref, g_ref, sc_ref, g1_ref, dx1_ref, da_ref, dg_ref, dsh_ref, dsc_ref, dg1_ref):
        _acc_init(pl.program_id(0), [dg_ref, dsh_ref, dsc_ref, dg1_ref])
        xh, r = _rms(x1_ref[...])
        dzv = dz_ref[...]
        gv = g_ref[...]
        dsc_ref[...] += _csum(dzv * (xh * gv))
        dsh_ref[...] += _csum(dzv)
        dh = dzv * (1.0 + sc_ref[...])
        dg_ref[...] += _csum(dh * xh)
        dx1 = _rms_bwd(xh, r, dh * gv) + dx2_ref[...]
        dx1_ref[...] = dx1
        dg1_ref[...] += _csum(dx1 * a_ref[...])
        da_ref[...] = (dx1 * g1_ref[...]).astype(BF16)

    return pl.pallas_call(
        body,
        name="resid_norm2_bwd",
        grid=(n // tr,),
        in_specs=[_rows(tr, d)] * 4 + [_bcast(d)] * 3,
        out_specs=[_rows(tr, d), _rows(tr, d)] + [_bcast(d)] * 4,
        out_shape=[jax.ShapeDtypeStruct((n, d), F32), jax.ShapeDtypeStruct((n, d), BF16)] + [jax.ShapeDtypeStruct((1, d), F32)] * 4,
        compiler_params=_params(("arbitrary",)),
    )(dz2, x1, dx2, att, n2g, sc2, g1)


def _edges(shape):
    row = lax.broadcasted_iota(jnp.int32, shape, 0)
    return row == 0, row == shape[0] - 1


def _shifts(u, edges):
    n = u.shape[0]
    return jnp.where(edges[0], 0.0, pltpu.roll(u, 1, 0)), jnp.where(edges[1], 0.0, pltpu.roll(u, n - 1, 0))


def _conv3(u, prev, nxt, w_ref, b_ref):
    return b_ref[...] + w_ref[0:1, :] * prev + w_ref[1:2, :] * u + w_ref[2:3, :] * nxt


def _conv_fwd(u, cw, cb, tc):
    n, two_f = u.shape
    f = two_f // 2
    nb = f // tc

    def body(ua_ref, ub_ref, wa_ref, wb_ref, ba_ref, bb_ref, h_ref):
        edges = _edges((n, tc))
        ua = ua_ref[...].astype(F32)
        ub = ub_ref[...].astype(F32)
        a = _conv3(ua, *_shifts(ua, edges), wa_ref, ba_ref)
        b = _conv3(ub, *_shifts(ub, edges), wb_ref, bb_ref)
        h_ref[...] = (a * jax.nn.sigmoid(a) * b).astype(BF16)

    col = lambda rows, off: pl.BlockSpec((rows, tc), lambda i: (0, i + off))
    return pl.pallas_call(
        body,
        name="conv_fwd",
        grid=(nb,),
        in_specs=[col(n, 0), col(n, nb), col(3, 0), col(3, nb), col(1, 0), col(1, nb)],
        out_specs=col(n, 0),
        out_shape=jax.ShapeDtypeStruct((n, f), BF16),
        compiler_params=_params(("parallel",)),
    )(u, u, cw, cw, cb, cb)


def _conv_bwd(u, dh, cw, cb, tc):
    n, two_f = u.shape
    f = two_f // 2
    nb = f // tc

    def part(uv, prev, nxt, duc, edges, w_ref, du_ref, dw_ref, db_ref):
        db_ref[...] = _csum(duc)
        dw_ref[0:1, :] = _csum(duc * prev)
        dw_ref[1:2, :] = _csum(duc * uv)
        dw_ref[2:3, :] = _csum(duc * nxt)
        d_prev, d_next = _shifts(duc, edges)
        du_ref[...] = (w_ref[0:1, :] * d_next + w_ref[1:2, :] * duc + w_ref[2:3, :] * d_prev).astype(BF16)

    def body(ua_ref, ub_ref, dh_ref, wa_ref, wb_ref, ba_ref, bb_ref, dua_ref, dub_ref, dwa_ref, dwb_ref, dba_ref, dbb_ref):
        edges = _edges((n, tc))
        ua = ua_ref[...].astype(F32)
        ub = ub_ref[...].astype(F32)
        sa = _shifts(ua, edges)
        sb = _shifts(ub, edges)
        a = _conv3(ua, *sa, wa_ref, ba_ref)
        b = _conv3(ub, *sb, wb_ref, bb_ref)
        dhv = dh_ref[...].astype(F32)
        sg = jax.nn.sigmoid(a)
        da = dhv * b * (sg * (1.0 + a * (1.0 - sg)))
        db = dhv * (a * sg)
        part(ua, *sa, da, edges, wa_ref, dua_ref, dwa_ref, dba_ref)
        part(ub, *sb, db, edges, wb_ref, dub_ref, dwb_ref, dbb_ref)

    col = lambda rows, off: pl.BlockSpec((rows, tc), lambda i: (0, i + off))
    return pl.pallas_call(
        body,
        name="conv_bwd",
        grid=(nb,),
        in_specs=[col(n, 0), col(n, nb), col(n, 0), col(3, 0), col(3, nb), col(1, 0), col(1, nb)],
        out_specs=[col(n, 0), col(n, 0), col(3, 0), col(3, 0), col(1, 0), col(1, 0)],
        out_shape=[jax.ShapeDtypeStruct((n, f), BF16)] * 2 + [jax.ShapeDtypeStruct((3, f), F32)] * 2 + [jax.ShapeDtypeStruct((1, f), F32)] * 2,
        compiler_params=_params(("parallel",)),
    )(u, u, dh, cw, cw, cb, cb)


def _loss_head(x1, f, g2, fg, tgt, tr):
    n, d = x1.shape

    def body(x1_ref, f_ref, g2_ref, fg_ref, t_ref, sq_ref, dx2_ref, dfg_ref, dg2_ref, df_ref):
        _acc_init(pl.program_id(0), [sq_ref, dfg_ref, dg2_ref])
        fv = f_ref[...]
        xh, r = _rms(x1_ref[...] + g2_ref[...] * fv)
        err = xh * fg_ref[...] - t_ref[...]
        sq_ref[...] += _csum(err * err)
        dy = err * (1.0 / d)
        dfg_ref[...] += _csum(dy * xh)
        dx2 = _rms_bwd(xh, r, dy * fg_ref[...])
        dx2_ref[...] = dx2
        dg2_ref[...] += _csum(dx2 * fv)
        df_ref[...] = (dx2 * g2_ref[...]).astype(BF16)

    return pl.pallas_call(
        body,
        name="loss_head",
        grid=(n // tr,),
        in_specs=[_rows(tr, d), _rows(tr, d), _bcast(d), _bcast(d), _rows(tr, d)],
        out_specs=[_bcast(d), _rows(tr, d), _bcast(d), _bcast(d), _rows(tr, d)],
        out_shape=[jax.ShapeDtypeStruct((1, d), F32), jax.ShapeDtypeStruct((n, d), F32), jax.ShapeDtypeStruct((1, d), F32),
                   jax.ShapeDtypeStruct((1, d), F32), jax.ShapeDtypeStruct((n, d), BF16)],
        compiler_params=_params(("arbitrary",)),
    )(x1, f, g2, fg, tgt)


def _sum_slots(g, name):
    s, r, w = g.shape

    def body(g_ref, o_ref):
        acc = g_ref[0]
        for k in range(1, s):
            acc = acc + g_ref[k]
        o_ref[...] = acc

    return pl.pallas_call(body, name=name, out_shape=jax.ShapeDtypeStruct((r, w), F32))(g)


def _silu_grad_mul(ds, cvec):
    def body(d_ref, c_ref, o_ref):
        cv = c_ref[...]
        sg = jax.nn.sigmoid(cv)
        o_ref[...] = d_ref[...] * (sg * (1.0 + cv * (1.0 - sg)))

    return pl.pallas_call(body, name="silu_grad_mul", out_shape=jax.ShapeDtypeStruct(ds.shape, F32))(ds, cvec)


def _adamw_update(wv, gv, mv, vv, d_ref, mo_ref, vo_ref):
    mn = ADAM_B1 * mv + (1.0 - ADAM_B1) * gv
    vn = ADAM_B2 * vv + (1.0 - ADAM_B2) * (gv * gv)
    mo_ref[...] = mn
    vo_ref[...] = vn
    m_hat = mn / (1.0 - ADAM_B1**ADAM_STEP)
    v_hat = vn / (1.0 - ADAM_B2**ADAM_STEP)
    d_ref[...] = -ADAM_LR * (m_hat / (jnp.sqrt(v_hat) + ADAM_EPS) + ADAM_WD * wv)


def _adamw_many(ws, gs, ms, vs, name):
    n = len(ws)

    def body(*refs):
        for k in range(n):
            w_ref, g_ref, m_ref, v_ref = (refs[q * n + k] for q in range(4))
            d_ref, mo_ref, vo_ref = (refs[(4 + q) * n + k] for q in range(3))
            _adamw_update(w_ref[...], g_ref[...], m_ref[...], v_ref[...], d_ref, mo_ref, vo_ref)

    res = pl.pallas_call(body, name=name, out_shape=[jax.ShapeDtypeStruct(w.shape, F32) for w in ws] * 3)(*ws, *gs, *ms, *vs)
    return res[:n], res[n : 2 * n], res[2 * n :]


def _adamw(w, g, m, v, name, g_transposed=False):
    r, cdim = w.shape
    tr = _pick(r, 1024, LANES if g_transposed else 8)
    tc = _pick(cdim, max(LANES, (1 << 19) // tr))

    def body(w_ref, g_ref, m_ref, v_ref, *outs):
        gv = g_ref[...]
        if g_transposed:
            gv = gv.T
            outs[0][...] = gv
        _adamw_update(w_ref[...], gv, m_ref[...], v_ref[...], *outs[-3:])

    spec = pl.BlockSpec((tr, tc), lambda i, j: (i, j))
    g_spec = pl.BlockSpec((tc, tr), lambda i, j: (j, i)) if g_transposed else spec
    n_out = 4 if g_transposed else 3
    res = pl.pallas_call(
        body,
        name=name,
        grid=(r // tr, cdim // tc),
        in_specs=[spec, g_spec, spec, spec],
        out_specs=[spec] * n_out,
        out_shape=[jax.ShapeDtypeStruct((r, cdim), F32)] * n_out,
        compiler_params=_params(("parallel", "parallel")),
    )(w, g, m, v)
    return res if g_transposed else [g, *res]


def _place():
    return lax.axis_index("x"), lax.axis_index("y"), lax.axis_index("c")


def _remote(src, dst, send_sem, recv_sem, dev):
    return pltpu.make_async_remote_copy(src_ref=src, dst_ref=dst, send_sem=send_sem, recv_sem=recv_sem, device_id=dev, device_id_type=MESH)


ANY = pl.BlockSpec(memory_space=pl.ANY)


def _all_gather_small(v, name):
    r, w = v.shape

    def body(v_ref, o_ref, send, recv, lsem):
        x, y, c = _place()
        me = 4 * x + 2 * y + c
        mine = pltpu.make_async_copy(v_ref, o_ref.at[me], lsem)
        mine.start()
        sent = []
        for k in range(1, 8):
            px, py, pc = x ^ (k >> 2), y ^ ((k >> 1) & 1), c ^ (k & 1)
            cp = _remote(v_ref, o_ref.at[me], send.at[k - 1], recv.at[k - 1], (px, py, pc))
            cp.start()
            sent.append(cp)
        for k in range(1, 8):
            px, py, pc = x ^ (k >> 2), y ^ ((k >> 1) & 1), c ^ (k & 1)
            slot = o_ref.at[4 * px + 2 * py + pc]
            _remote(slot, slot, send.at[k - 1], recv.at[k - 1], (x, y, c)).wait_recv()
        for cp in sent:
            cp.wait_send()
        mine.wait()

    return pl.pallas_call(
        body,
        name=name,
        out_shape=jax.ShapeDtypeStruct((8, r, w), F32),
        in_specs=[pl.BlockSpec(memory_space=pltpu.VMEM)],
        out_specs=pl.BlockSpec(memory_space=pltpu.VMEM),
        scratch_shapes=[pltpu.SemaphoreType.DMA((7,)), pltpu.SemaphoreType.DMA((7,)), pltpu.SemaphoreType.DMA],
        compiler_params=pltpu.CompilerParams(vmem_limit_bytes=VMEM_LIMIT),
    )(v)


HBM = pl.BlockSpec(memory_space=pltpu.HBM)
SEM = pl.BlockSpec(memory_space=pltpu.SEMAPHORE)
EFFECT = pltpu.SideEffectType.DATAFLOW_SIDE_EFFECTING


def _other_chips(x, y):
    return [(1 - x, y), (x, 1 - y), (1 - x, 1 - y)]


def _bulk_start(name, srcs, land_shapes, n_copies, copies, after):
    n, m = len(srcs), len(land_shapes)

    def body(*refs):
        src_refs, land_refs = refs[:n], refs[n : n + m]
        send, recv = refs[n + m + 1], refs[n + m + 2]
        token = refs[-1]
        for k, (s, d, dev) in enumerate(copies(src_refs, land_refs)):
            _remote(s, d, send.at[k], recv.at[k], dev).start()
        token[...] = jnp.zeros_like(token)

    lands = [pltpu.with_memory_space_constraint(lax.empty(s.shape, s.dtype), pltpu.HBM) for s in land_shapes]
    out = pl.pallas_call(
        body,
        name=name,
        out_shape=(pltpu.SemaphoreType.DMA((n_copies,)), pltpu.SemaphoreType.DMA((n_copies,)),
                   *[pltpu.HBM(s.shape, s.dtype) for s in srcs], *[pltpu.HBM(s.shape, s.dtype) for s in land_shapes],
                   jax.ShapeDtypeStruct((8, LANES), F32)),
        in_specs=[HBM] * (n + m) + [ANY],
        out_specs=(SEM, SEM, *[HBM] * (n + m), pl.BlockSpec(memory_space=pltpu.VMEM)),
        input_output_aliases={i: 2 + i for i in range(n + m)},
        compiler_params=pltpu.CompilerParams(has_side_effects=EFFECT),
    )(*[pltpu.with_memory_space_constraint(s, pltpu.HBM) for s in srcs], *lands, after)
    return out[0], out[1], list(out[2 : 2 + n]), list(out[2 + n : 2 + n + m]), out[-1][0:1, 0:1]


def _bulk_wait(name, send, recv, srcs, lands, after, waits):
    n, m = len(srcs), len(lands)

    def body(*refs):
        src_refs, land_refs = refs[:n], refs[n : n + m]
        send_sem, recv_sem = refs[n + m], refs[n + m + 1]
        x, y, c = _place()
        for k, (s, d) in enumerate(waits(src_refs, land_refs)):
            cp = _remote(s, d, send_sem.at[k], recv_sem.at[k], (x, y, c))
            cp.wait_send()
            cp.wait_recv()

    out = pl.pallas_call(
        body,
        name=name,
        out_shape=tuple(pltpu.HBM(s.shape, s.dtype) for s in (*srcs, *lands)),
        in_specs=[HBM] * (n + m) + [SEM, SEM, ANY],
        out_specs=tuple([HBM] * (n + m)),
        input_output_aliases={i: i for i in range(n + m)},
        compiler_params=pltpu.CompilerParams(has_side_effects=EFFECT),
    )(*srcs, *lands, send, recv, after)
    return list(out[:n]), list(out[n:])


def _gather_start(shards, after, name):
    def copies(src, land):
        x, y, c = _place()
        j = 2 * x + y
        return [(src[a].at[c], land[a].at[j, c], (px, py, c)) for a in range(len(shards)) for px, py in _other_chips(x, y)]

    shapes = [jax.ShapeDtypeStruct((4,) + s.shape, s.dtype) for s in shards]
    return _bulk_start(name, shards, shapes, 3 * len(shards), copies, after)


def _gather_wait(started, after, name):
    send, recv, srcs, lands, _ = started

    def waits(src, land):
        x, y, c = _place()
        return [(src[a].at[c], land[a].at[2 * px + py, c]) for a in range(len(srcs)) for px, py in _other_chips(x, y)]

    return _bulk_wait(name, send, recv, srcs, lands, after, waits)


def _forward_halves(lands, name):
    n = len(lands)

    def body(*refs):
        bufs = refs[n : 2 * n]
        send, recv = refs[2 * n :]
        x, y, c = _place()
        started = []
        for a in range(n):
            for k, (px, py) in enumerate(_other_chips(x, y)):
                blk = bufs[a].at[2 * px + py, c]
                cp = _remote(blk, blk, send.at[3 * a + k], recv.at[3 * a + k], (x, y, 1 - c))
                cp.start()
                started.append(cp)
        for a in range(n):
            for k, (px, py) in enumerate(_other_chips(x, y)):
                blk = bufs[a].at[2 * px + py, 1 - c]
                _remote(blk, blk, send.at[3 * a + k], recv.at[3 * a + k], (x, y, c)).wait_recv()
        for cp in started:
            cp.wait_send()

    return pl.pallas_call(
        body,
        name=name,
        out_shape=[jax.ShapeDtypeStruct(b.shape, b.dtype) for b in lands],
        in_specs=[ANY] * n,
        out_specs=[ANY] * n,
        input_output_aliases={i: i for i in range(n)},
        scratch_shapes=[pltpu.SemaphoreType.DMA((3 * n,)), pltpu.SemaphoreType.DMA((3 * n,))],
    )(*lands)


def _forward_start(lands, after, name):
    def copies(src, _):
        x, y, c = _place()
        blocks = [src[a].at[2 * px + py, c] for a in range(len(lands)) for px, py in _other_chips(x, y)]
        return [(b, b, (x, y, 1 - c)) for b in blocks]

    return _bulk_start(name, lands, [], 3 * len(lands), copies, after)


def _forward_wait(started, after, name):
    send, recv, bufs, _, _ = started

    def waits(src, _):
        x, y, c = _place()
        return [(src[a].at[2 * px + py, c], src[a].at[2 * px + py, 1 - c]) for a in range(len(bufs)) for px, py in _other_chips(x, y)]

    return _bulk_wait(name, send, recv, bufs, [], after, waits)[0]


def _place_own(shards, lands):
    j = 2 * lax.axis_index("x") + lax.axis_index("y")
    full = [lax.dynamic_update_slice(b, s[None], (j, 0, 0, 0)) for b, s in zip(lands, shards)]
    return [f.reshape(4 * f.shape[2] * 2, f.shape[3]) for f in full]


def _gather_finish(started, after, tag):
    shards, lands = _gather_wait(started, after, "gather_wait_" + tag)
    return _place_own(shards, _forward_halves(lands, "gather_forward_" + tag))


def _gather_land(started, after, tag):
    shards, lands = _gather_wait(started, after, "gather_wait_" + tag)
    return shards, _forward_start(lands, shards[0], "forward_start_" + tag)


def _gather_done(landed, after, tag):
    shards, fwd = landed
    return _place_own(shards, _forward_wait(fwd, after, "forward_wait_" + tag))


def _swap_halves(grads, name):
    n = len(grads)

    def body(*refs):
        ins, outs = refs[:n], refs[n : 2 * n]
        send, recv = refs[2 * n :]
        x, y, c = _place()
        started = []
        for a in range(n):
            for s in range(4):
                cp = _remote(ins[a].at[s, 1 - c], outs[a].at[s], send.at[4 * a + s], recv.at[4 * a + s], (x, y, 1 - c))
                cp.start()
                started.append(cp)
        for cp in started:
            cp.wait_recv()
        for cp in started:
            cp.wait_send()

    return pl.pallas_call(
        body,
        name=name,
        out_shape=[jax.ShapeDtypeStruct((4,) + g.shape[2:], g.dtype) for g in grads],
        in_specs=[ANY] * n,
        out_specs=[ANY] * n,
        scratch_shapes=[pltpu.SemaphoreType.DMA((4 * n,)), pltpu.SemaphoreType.DMA((4 * n,))],
    )(*grads)


def _add_halves(grads, others, tag):
    outs = []
    for a, (g, o) in enumerate(zip(grads, others)):
        _, _, rh, cdim = g.shape
        tr = _pick(rh, 512, 16)

        def body(g_ref, o_ref, p_ref):
            p_ref[...] = (g_ref[...].astype(F32) + o_ref[...].astype(F32)).astype(BF16)

        outs.append(
            pl.pallas_call(
                body,
                name=f"add_halves_{tag}{a}",
                grid=(4, rh // tr),
                in_specs=[pl.BlockSpec((None, None, tr, cdim), lambda s, i: (s, lax.axis_index("c"), i, 0)),
                          pl.BlockSpec((None, tr, cdim), lambda s, i: (s, i, 0))],
                out_specs=pl.BlockSpec((None, tr, cdim), lambda s, i: (s, i, 0)),
                out_shape=jax.ShapeDtypeStruct((4, rh, cdim), BF16),
                compiler_params=_params(("parallel", "parallel")),
            )(g, o)
        )
    return outs


def _exchange_start(parts, after, name):
    def copies(src, land):
        x, y, c = _place()
        j = 2 * x + y
        return [(src[a].at[2 * px + py], land[a].at[j], (px, py, c)) for a in range(len(parts)) for px, py in _other_chips(x, y)]

    return _bulk_start(name, parts, [jax.ShapeDtypeStruct(p.shape, p.dtype) for p in parts], 3 * len(parts), copies, after)


def _exchange_finish(started, after, name):
    send, recv, srcs, lands, _ = started

    def waits(src, land):
        x, y, _ = _place()
        return [(src[a].at[2 * px + py], land[a].at[2 * px + py]) for a in range(len(srcs)) for px, py in _other_chips(x, y)]

    srcs, lands = _bulk_wait(name, send, recv, srcs, lands, after, waits)
    j = 2 * lax.axis_index("x") + lax.axis_index("y")
    return [lax.dynamic_update_slice(b, lax.dynamic_slice(p, (j, 0, 0), (1,) + p.shape[1:]), (j, 0, 0)) for b, p in zip(lands, srcs)]


def _sum_chips(recvd, tag):
    outs = []
    for a, g in enumerate(recvd):
        _, rh, cdim = g.shape
        tr = _pick(rh, 512, 16)

        def body(g_ref, o_ref):
            o_ref[...] = ((g_ref[0].astype(F32) + g_ref[1].astype(F32)) + g_ref[2].astype(F32)) + g_ref[3].astype(F32)

        outs.append(
            pl.pallas_call(
                body,
                name=f"sum_chips_{tag}{a}",
                grid=(rh // tr,),
                in_specs=[pl.BlockSpec((4, tr, cdim), lambda i: (0, i, 0))],
                out_specs=pl.BlockSpec((tr, cdim), lambda i: (i, 0)),
                out_shape=jax.ShapeDtypeStruct((rh, cdim), F32),
                compiler_params=_params(("parallel",)),
            )(g)
        )
    return outs


def _join_halves(halves, name):
    n = len(halves)

    def body(*refs):
        ins, outs = refs[:n], refs[n : 2 * n]
        send, recv = refs[2 * n :]
        x, y, c = _place()
        started = []
        for a in range(n):
            cp = _remote(ins[a], outs[a], send.at[a], recv.at[a], (x, y, 1 - c))
            cp.start()
            started.append(cp)
        for cp in started:
            cp.wait_recv()
        for cp in started:
            cp.wait_send()

    others = pl.pallas_call(
        body,
        name=name,
        out_shape=[jax.ShapeDtypeStruct(h.shape, h.dtype) for h in halves],
        in_specs=[ANY] * n,
        out_specs=[ANY] * n,
        scratch_shapes=[pltpu.SemaphoreType.DMA((n,)), pltpu.SemaphoreType.DMA((n,))],
    )(*halves)
    first = lax.axis_index("c") == 0
    return [jnp.concatenate([jnp.where(first, h, o), jnp.where(first, o, h)], axis=0) for h, o in zip(halves, others)]


def _grad_views(grads):
    return [g.reshape(4, 2, g.shape[0] // 8, g.shape[1]) for g in grads]


def _scatter_start(grads, tag, after=None):
    views = _grad_views(grads)
    mine = _add_halves(views, _swap_halves(views, "swap_halves_" + tag), tag)
    return _exchange_start(mine, mine[-1] if after is None else after, "exchange_start_" + tag)


def _swap_start(grads, after, tag):
    views = _grad_views(grads)

    def copies(src, land):
        x, y, c = _place()
        return [(src[a].at[s, 1 - c], land[a].at[s], (x, y, 1 - c)) for a in range(len(views)) for s in range(4)]

    shapes = [jax.ShapeDtypeStruct((4,) + v.shape[2:], v.dtype) for v in views]
    return _bulk_start("swap_start_" + tag, views, shapes, 4 * len(views), copies, after)


def _scatter_start_after_swap(swapped, after, tag):
    send, recv, views, lands, _ = swapped

    def waits(src, land):
        c = lax.axis_index("c")
        return [(src[a].at[s, 1 - c], land[a].at[s]) for a in range(len(views)) for s in range(4)]

    views, others = _bulk_wait("swap_wait_" + tag, send, recv, views, lands, after, waits)
    mine = _add_halves(views, others, tag)
    return _exchange_start(mine, mine[-1], "exchange_start_" + tag)


def _join_start(halves, after, tag):
    def copies(src, land):
        x, y, c = _place()
        return [(src[a], land[a], (x, y, 1 - c)) for a in range(len(halves))]

    return _bulk_start("join_start_" + tag, halves, [jax.ShapeDtypeStruct(h.shape, h.dtype) for h in halves], len(halves), copies, after)


def _join_wait(started, after, tag):
    send, recv, halves, lands, _ = started
    halves, others = _bulk_wait("join_wait_" + tag, send, recv, halves, lands, after, lambda src, land: list(zip(src, land)))
    first = lax.axis_index("c") == 0
    return [jnp.concatenate([jnp.where(first, h, o), jnp.where(first, o, h)], axis=0) for h, o in zip(halves, others)]


def _scatter_sums(started, after, tag):
    return _sum_chips(_exchange_finish(started, after, "exchange_wait_" + tag), tag)


def _scatter_finish(started, after, tag):
    return _join_halves(_scatter_sums(started, after, tag), "join_halves_" + tag)


def _t_bf16(w):
    return w.T.astype(BF16)


def kernel(x, c, ctx, c_ctx, w_ada, b_ada, norm1_g, w_in, mla_q_norm_g, w_q_up, mla_kv_norm_g, w_kv_up, gqa_q_norm_g, gqa_k_norm_g, w_br_a, w_br_b, w_out, norm2_g, w_up, conv_w, conv_b, w_down, final_norm_g, loss_target, m_c_ctx, m_w_ada, m_b_ada, m_norm1_g, m_w_in, m_mla_q_norm_g, m_w_q_up, m_mla_kv_norm_g, m_w_kv_up, m_gqa_q_norm_g, m_gqa_k_norm_g, m_w_br_a, m_w_br_b, m_w_out, m_norm2_g, m_w_up, m_conv_w, m_conv_b, m_w_down, m_final_norm_g, v_c_ctx, v_w_ada, v_b_ada, v_norm1_g, v_w_in, v_mla_q_norm_g, v_w_q_up, v_mla_kv_norm_g, v_w_kv_up, v_gqa_q_norm_g, v_gqa_k_norm_g, v_w_br_a, v_w_br_b, v_w_out, v_norm2_g, v_w_up, v_conv_w, v_conv_b, v_w_down, v_final_norm_g):
    T, D = x.shape[1], x.shape[2]
    C = ctx.shape[1]
    NA = w_ada.shape[2]
    NW = w_up.shape[2]
    F2 = 4 * NW
    FF = F2 // 2
    xi, yi, ci = _place()
    j = 2 * xi + yi
    me = 4 * xi + 2 * yi + ci
    tr = _pick(C, 128, 8)
    tq = _pick(T, 256)

    x2d, tgt, ctx2d = x[0], loss_target[0], ctx[0]
    fg = final_norm_g.reshape(1, D)
    cc = c_ctx.reshape(1, D)

    halve = lambda s: s.reshape(2, s.shape[0] // 2, s.shape[1])
    win_shard = halve(_t_bf16(w_in[0]))
    w0 = max(D, NW)
    pay = jnp.zeros((8, w0), F32).at[0:1, :D].set(c).at[1:4, :NW].set(conv_w[0])
    got = _all_gather_small(pay, "gather_cond")
    c_all = got[:, 0, :D]
    cw = jnp.concatenate([got[2 * s, 1:4, :NW] for s in range(4)], axis=1)
    s16 = jnp.concatenate([c_all, cc, jnp.zeros((7, D), F32)], axis=0)
    b_cols = lax.dynamic_slice(b_ada, (0, j * NA), (1, NA))
    ada_part = _mm(s16, w_ada[0], "NN", F32, "ada_fwd", act="silu", bias=b_cols)
    got = _all_gather_small(ada_part, "gather_ada")
    ada = jnp.concatenate([got[2 * s] for s in range(4)], axis=1)
    lat = lax.dynamic_slice(ada, (me, 0), (1, 6 * D))
    sh1, sc1, g1, sh2, sc2, g2 = [lat[:, k * D : (k + 1) * D] for k in range(6)]
    csh, csc = ada[8:9, :D], ada[8:9, D : 2 * D]

    ag_in = _gather_start([win_shard], got, "gather_start_in")
    t_in = ag_in[4]
    wq3 = (w_q_up[0] + t_in).reshape(MLA_Q_LORA, 2, MLA_NOPE + MLA_ROPE)
    wq_perm = jnp.concatenate([wq3[:, :, :MLA_NOPE].reshape(MLA_Q_LORA, -1), wq3[:, :, MLA_NOPE:].reshape(MLA_Q_LORA, -1)], axis=1)
    mix = [_t_bf16(wq_perm), _t_bf16(w_kv_up[0] + t_in), _t_bf16(w_br_a[0] + t_in), _t_bf16(w_br_b[0] + t_in), (w_out[0] + t_in).astype(BF16)]
    ag_mix = _gather_start([halve(s) for s in mix], t_in, "gather_start_mix")
    ag_up = _gather_start([halve(_t_bf16(w_up[0] + t_in))], ag_mix[4], "gather_start_up")
    ag_down = _gather_start([halve((w_down[0] + t_in).astype(BF16))], ag_up[4], "gather_start_down")
    sh1 = sh1 + ag_down[4]

    cos_a, ss_a = _rope_tables(C, T, MLA_ROPE)
    cos_b, ss_b = _rope_tables(C, T, GQA_HEAD_DIM)
    lcos_a, lss_a, lcos_b, lss_b = cos_a[:T], ss_a[:T], cos_b[:T], ss_b[:T]

    z_all = _norm_mod_fwd(x2d, norm1_g, sh1, sc1, "norm1_lat_fwd", tr, out_rows=T + C)
    z_all = _norm_mod_fwd(ctx2d, norm1_g, csh, csc, "norm1_ctx_fwd", tr, base=z_all, out_off=T)
    (win_t,) = _gather_finish(ag_in, z_all, "in")
    kv_cols = KVP - LANES + MLA_ROPE
    e_kpe = MLA_KV_LORA + MLA_ROPE
    w_kvp = jnp.concatenate([win_t[:MLA_KV_LORA], win_t[e_kpe:kv_cols], win_t[MLA_KV_LORA:e_kpe], jnp.zeros((LANES - MLA_ROPE, D), BF16)], axis=0)

    pkv = _mm(z_all, w_kvp, "NT", F32, "proj_kv")
    pq = _mm(z_all, win_t, "NT", F32, "proj_q", m=T, n=QC, b_off=kv_cols)
    mix_landed = _gather_land(ag_mix, pq, "mix")
    pg = _mm(z_all, win_t, "NT", BF16, "proj_g", m=T, n=2 * D, b_off=kv_cols + QC, after=mix_landed[1][4])
    wq_t, wkv_t, wbra_t, wbrb_t, wout = _gather_done(mix_landed, pg, "mix")
    ckv_n, kb2, vb2, kpe2 = _kprep_fwd(pkv, mla_kv_norm_g, gqa_k_norm_g, cos_a, ss_a, cos_b, ss_b, tr)
    kv_up = _mm(ckv_n, wkv_t, "NT", BF16, "kv_up")
    cq_n, qb2 = _qprep_fwd(pq, mla_q_norm_g, gqa_q_norm_g, lcos_b, lss_b, tr)
    q_a = _mm(cq_n, wq_t, "NT", F32, "q_up")
    qar = _qrope_fwd(q_a, lcos_a, lss_a, tr)

    a_q = [(qar, lambda h: 3 * (h // 2) + h % 2), (qar, lambda h: 3 * (h // 2) + 2)]
    a_k = [(kv_up, lambda h: 2 * h), (kpe2, lambda h: h % 2)]
    a_v = (kv_up, lambda h: 2 * h + 1)
    a_scale = float(MLA_NOPE + MLA_ROPE) ** -0.5
    b_q = [(qb2, lambda h: h)]
    b_k = [(kb2, lambda h: h)]
    b_v = (vb2, lambda h: h)
    b_scale = float(GQA_HEAD_DIM) ** -0.5
    tq_f = _pick(T, 512)
    o_a, lse_a = _attn_fwd(a_q, a_k, a_v, MLA_HEADS, 1, MLA_V, a_scale, "attn_a_fwd", tq_f)
    o_b, lse_b = _attn_fwd(b_q, b_k, b_v, GQA_HEADS, GQA_GROUP, GQA_HEAD_DIM, b_scale, "attn_b_fwd", tq_f)
    up_landed = _gather_land(ag_up, o_b, "up")
    ya = _mm(o_a, wbra_t, "NT", BF16, "br_a", after=up_landed[1][4])
    yb = _mm(o_b, wbrb_t, "NT", BF16, "br_b")
    merged = _gates_fwd(pg, ya, yb, tr)
    att = _mm(merged, wout, "NN", F32, "out_proj")
    x1, z2 = _resid_norm2_fwd(x2d, att, g1, norm2_g, sh2, sc2, tr)
    (wup_t,) = _gather_done(up_landed, z2, "up")
    down_landed = _gather_land(ag_down, z2, "down")
    u = _mm(z2, wup_t, "NT", BF16, "ffn_up", after=down_landed[1][4])
    tc = _pick(FF, 128)
    hg = _conv_fwd(u, cw, conv_b, tc)
    (wdown,) = _gather_done(down_landed, hg, "down")
    f = _mm(hg, wdown, "NN", F32, "ffn_down")
    sq, dx2, d_fg, d_g2, df = _loss_head(x1, f, g2, fg, tgt, tr)
    loss = lax.psum(0.5 * jnp.sum(sq) / D, ("x", "y", "c"))

    dhg = _mm(df, wdown, "NT", BF16, "ffn_down_dx")
    g_wdown = _mm(hg, df, "TN", BF16, "ffn_down_dw")
    du_a, du_b, dcw_a, dcw_b, dcb_a, dcb_b = _conv_bwd(u, dhg, cw, conv_b, tc)
    dz2 = _mm(du_a, wup_t, "NN", F32, "ffn_up_dx_a")
    dz2 = _mm(du_b, wup_t, "NN", F32, "ffn_up_dx_b", b_off=FF, add=dz2)
    g_wup_t = _mm(du_a, z2, "TN", BF16, "ffn_up_dw_a", out_rows=F2, tm=FF // 4)
    g_wup_t = _mm(du_b, z2, "TN", BF16, "ffn_up_dw_b", out_base=g_wup_t, out_off=FF, tm=FF // 4)
    sw_ffn = _swap_start([g_wdown, g_wup_t], sc2, "ffn")
    sc2 = sc2 + sw_ffn[4]
    dx1, datt, d_n2g, d_sh2, d_sc2, d_g1 = _resid_norm2_bwd(dz2, x1, dx2, att, norm2_g, sc2, g1, tr)

    dmerged = _mm(datt, wout, "NT", BF16, "out_proj_dx")
    rs_ffn = _scatter_start_after_swap(sw_ffn, dmerged, "ffn")
    lse_a = lse_a + rs_ffn[4]
    g_wout = _mm(merged, datt, "TN", BF16, "out_proj_dw")
    dya, dyb, dpg = _gates_bwd(dmerged, pg, ya, yb, tr)
    do_a = _mm(dya, wbra_t, "NN", BF16, "br_a_dx")
    g_wbra_t = _mm(dya, o_a, "TN", BF16, "br_a_dw")
    do_b = _mm(dyb, wbrb_t, "NN", BF16, "br_b_dx")
    g_wbrb_t = _mm(dyb, o_b, "TN", BF16, "br_b_dw")
    dqa2, dka2, dva2 = _attn_bwd(a_q, a_k, a_v, o_a, do_a, lse_a, MLA_HEADS, 1, MLA_V, a_scale, "attn_a_bwd", tq_f)
    dqb2, dkb2, dvb2 = _attn_bwd(b_q, b_k, b_v, o_b, do_b, lse_b, GQA_HEADS, GQA_GROUP, GQA_HEAD_DIM, b_scale, "attn_b_bwd", tq_f)
    dq_a = _qrope_bwd(dqa2, lcos_a, lss_a, tr)
    dcq_n = _mm(dq_a, wq_t, "NN", F32, "q_up_dx")
    g_wq_t = _mm(dq_a, cq_n, "TN", BF16, "q_up_dw")
    dpq, d_qg, d_gq = _qprep_bwd(pq, dcq_n, dqb2, mla_q_norm_g, gqa_q_norm_g, lcos_b, lss_b, tr)
    dkv_up, dkpe = _kgrad_split(dka2, dva2, cos_a, ss_a, tr)
    dckv_n = _mm(dkv_up, wkv_t, "NN", F32, "kv_up_dx")
    g_wkv_t = _mm(dkv_up, ckv_n, "TN", BF16, "kv_up_dw")
    rs_mix = _scatter_start([g_wq_t, g_wkv_t, g_wbra_t, g_wbrb_t, g_wout], "mix")
    dpkv, d_kvg, d_kg = _kprep_bwd(pkv, dckv_n, dkb2, dvb2, dkpe, mla_kv_norm_g + rs_mix[4], gqa_k_norm_g, cos_b, ss_b, tr)
    dz_kv = _mm(dpkv, w_kvp, "NN", F32, "proj_kv_dx")
    dz_lat = _mm(dpq, win_t, "NN", F32, "proj_q_dx", b_off=kv_cols, add=dz_kv)
    dz_lat = _mm(dpg, win_t, "NN", F32, "proj_g_dx", b_off=kv_cols + QC, add=dz_lat)
    _, d_n1g_c, d_csh, d_csc = _norm_mod_bwd(dz_kv, T // tr, ctx2d, norm1_g, csc, None, "norm1_ctx_bwd", tr)
    grad_x, d_n1g_l, d_sh1, d_sc1 = _norm_mod_bwd(dz_lat, 0, x2d, norm1_g, sc1, dx1, "norm1_lat_bwd", tr)

    zeros_d = jnp.zeros((1, D), F32)
    d_lat = jnp.concatenate([d_sh1, d_sc1, d_g1, d_sh2, d_sc2, d_g2], axis=1)
    d_ctx_part = jnp.concatenate([d_csh, d_csc], axis=1)
    flat = jnp.concatenate(
        [d_n1g_c + d_n1g_l, d_qg, d_kvg, d_gq, d_kg, d_n2g, dcb_a, dcb_b, d_fg,
         dcw_a.reshape(1, -1), dcw_b.reshape(1, -1), d_ctx_part, d_lat], axis=1)
    n_flat = flat.shape[1]
    n_rows = -(-n_flat // (8 * LANES)) * 8
    flat = jnp.pad(flat, ((0, 0), (0, n_rows * LANES - n_flat))).reshape(n_rows, LANES)
    got = _all_gather_small(flat, "gather_small_grads")
    tot = _sum_slots(got, "sum_small_grads").reshape(1, -1)
    sizes = [D, MLA_Q_LORA, MLA_KV_LORA, GQA_HEAD_DIM, GQA_HEAD_DIM, D, F2, D, 3 * FF, 3 * FF, 2 * D]
    offs = [0]
    for s in sizes:
        offs.append(offs[-1] + s)
    t_n1g, t_qg, t_kvg, t_gq, t_kg, t_n2g, t_cb, t_fg, t_cwa, t_cwb, t_ctx = [tot[:, offs[k] : offs[k + 1]] for k in range(len(sizes))]
    g_cw_full = jnp.concatenate([t_cwa.reshape(3, FF), t_cwb.reshape(3, FF)], axis=1)
    g_cw = lax.dynamic_slice(g_cw_full, (0, j * NW), (3, NW))
    d_lat_all = got.reshape(8, -1)[:, offs[-1] : offs[-1] + 6 * D]
    g16 = jnp.concatenate([d_lat_all, jnp.pad(t_ctx, ((0, 0), (0, 4 * D))), jnp.zeros((7, 6 * D), F32)], axis=0)
    g_b_ada = _sum_slots(g16.reshape(16, 1, 6 * D), "sum_b_ada")
    g16_cols = lax.dynamic_slice(g16, (0, j * NA), (16, NA))
    ds_part = _mm(g16_cols, w_ada[0], "NT", F32, "ada_dx")
    got = _all_gather_small(ds_part[8:16], "gather_ada_dx")
    ds_ctx = _sum_slots(jnp.stack([got[2 * s] for s in range(4)]), "sum_ada_dx")[0:1]
    g_c_ctx = _silu_grad_mul(ds_ctx, cc)

    g_kvp = _mm(dpkv, z_all, "TN", BF16, "proj_kv_dw")
    nk = MLA_KV_LORA + 2 * GQA_KV_HEADS * GQA_HEAD_DIM
    g_kv = jnp.concatenate([g_kvp[:MLA_KV_LORA], g_kvp[nk : nk + MLA_ROPE], g_kvp[MLA_KV_LORA:nk]], axis=0)
    g_win_t = _mm(dpq, z_all, "TN", BF16, "proj_q_dw", out_rows=kv_cols + QC + 2 * D, out_off=kv_cols, tm=QC // 2)
    g_win_t = _mm(dpg, z_all, "TN", BF16, "proj_g_dw", out_base=g_win_t, out_off=kv_cols + QC)
    g_win_t = lax.dynamic_update_slice(g_win_t, g_kv, (0, 0))
    rs_in = _scatter_start([g_win_t], "in", after=got)

    h_ffn = _scatter_sums(rs_ffn, rs_in[2][0], "ffn")
    j_ffn = _join_start(h_ffn, grad_x, "ffn")
    h_mix = _scatter_sums(rs_mix, j_ffn[2][0], "mix")
    j_mix = _join_start(h_mix, j_ffn[2][0], "mix")
    g_w_ada = _mm(s16, g16_cols, "TN", F32, "ada_dw", act="silu", after=j_mix[4])
    _, d_ada, m_ada, v_ada = _adamw(w_ada[0], g_w_ada, m_w_ada[0], v_w_ada[0], "adamw_w_ada")
    r_wdown, r_wup = _join_wait(j_ffn, d_ada, "ffn")
    r_wq, r_wkv, r_wbra, r_wbrb, r_wout = _join_wait(j_mix, d_ada, "mix")
    gq_p = r_wq.T
    gq = jnp.concatenate([gq_p[:, : 2 * MLA_NOPE].reshape(MLA_Q_LORA, 2, MLA_NOPE), gq_p[:, 2 * MLA_NOPE :].reshape(MLA_Q_LORA, 2, MLA_ROPE)], axis=2)
    grads = {
        "c_ctx": g_c_ctx.reshape(D), "w_ada": g_w_ada[None], "b_ada": g_b_ada, "norm1_g": t_n1g,
        "mla_q_norm_g": t_qg, "w_q_up": gq.reshape(1, MLA_Q_LORA, -1), "mla_kv_norm_g": t_kvg, "w_kv_up": r_wkv,
        "gqa_q_norm_g": t_gq, "gqa_k_norm_g": t_kg, "w_br_a": r_wbra, "w_br_b": r_wbrb, "w_out": r_wout[None],
        "norm2_g": t_n2g, "w_up": r_wup, "conv_w": g_cw[None], "conv_b": t_cb, "w_down": r_wdown[None],
        "final_norm_g": t_fg.reshape(D),
    }
    arrives_transposed = ("w_kv_up", "w_br_a", "w_br_b", "w_up")
    weights = dict(c_ctx=c_ctx, w_ada=w_ada, b_ada=b_ada, norm1_g=norm1_g, w_in=w_in, mla_q_norm_g=mla_q_norm_g, w_q_up=w_q_up,
                   mla_kv_norm_g=mla_kv_norm_g, w_kv_up=w_kv_up, gqa_q_norm_g=gqa_q_norm_g, gqa_k_norm_g=gqa_k_norm_g, w_br_a=w_br_a,
                   w_br_b=w_br_b, w_out=w_out, norm2_g=norm2_g, w_up=w_up, conv_w=conv_w, conv_b=conv_b, w_down=w_down,
                   final_norm_g=final_norm_g)
    m_in = dict(c_ctx=m_c_ctx, w_ada=m_w_ada, b_ada=m_b_ada, norm1_g=m_norm1_g, w_in=m_w_in, mla_q_norm_g=m_mla_q_norm_g,
                w_q_up=m_w_q_up, mla_kv_norm_g=m_mla_kv_norm_g, w_kv_up=m_w_kv_up, gqa_q_norm_g=m_gqa_q_norm_g,
                gqa_k_norm_g=m_gqa_k_norm_g, w_br_a=m_w_br_a, w_br_b=m_w_br_b, w_out=m_w_out, norm2_g=m_norm2_g, w_up=m_w_up,
                conv_w=m_conv_w, conv_b=m_conv_b, w_down=m_w_down, final_norm_g=m_final_norm_g)
    v_in = dict(c_ctx=v_c_ctx, w_ada=v_w_ada, b_ada=v_b_ada, norm1_g=v_norm1_g, w_in=v_w_in, mla_q_norm_g=v_mla_q_norm_g,
                w_q_up=v_w_q_up, mla_kv_norm_g=v_mla_kv_norm_g, w_kv_up=v_w_kv_up, gqa_q_norm_g=v_gqa_q_norm_g,
                gqa_k_norm_g=v_gqa_k_norm_g, w_br_a=v_w_br_a, w_br_b=v_w_br_b, w_out=v_w_out, norm2_g=v_norm2_g, w_up=v_w_up,
                conv_w=v_conv_w, conv_b=v_conv_b, w_down=v_w_down, final_norm_g=v_final_norm_g)
    names = list(weights)
    big = [n for n in names if weights[n].ndim == 3 and weights[n].shape[1] >= 8]
    small = [n for n in names if n not in big]
    delta, new_m, new_v = {}, {}, {}

    def update(n):
        shp = weights[n].shape
        two_d = lambda a: a.reshape(shp[1], shp[2])
        g_t = n in arrives_transposed
        g_in = grads[n] if g_t else two_d(grads[n].astype(F32))
        g_, d_, m_, v_ = _adamw(two_d(weights[n]), g_in, two_d(m_in[n]), two_d(v_in[n]), "adamw_" + n, g_transposed=g_t)
        grads[n], delta[n], new_m[n], new_v[n] = g_.reshape(shp), d_.reshape(shp), m_.reshape(shp), v_.reshape(shp)

    delta["w_ada"], new_m["w_ada"], new_v["w_ada"] = d_ada[None], m_ada[None], v_ada[None]
    early = [n for n in big if n not in ("w_in", "w_ada")]
    for n in early:
        update(n)
    done = sum(delta[n][0, 0:1, 0:1] for n in early)
    (r_win,) = _scatter_finish(rs_in, done, "in")
    _, d_, m_, v_ = _adamw(w_in[0].T, r_win, m_w_in[0].T, v_w_in[0].T, "adamw_w_in")
    grads["w_in"], delta["w_in"], new_m["w_in"], new_v["w_in"] = r_win.T[None], d_.T[None], m_.T[None], v_.T[None]
    grads = {n: grads[n].reshape(weights[n].shape).astype(F32) for n in names}

    slab = lambda tree: [tree[n].reshape(-1, LANES) for n in small]
    d_, m_, v_ = _adamw_many(slab(weights), slab(grads), slab(m_in), slab(v_in), "adamw_small")
    for k, n in enumerate(small):
        shp = weights[n].shape
        delta[n], new_m[n], new_v[n] = d_[k].reshape(shp), m_[k].reshape(shp), v_[k].reshape(shp)

    return (loss, grad_x[None], *[grads[n] for n in names], *[delta[n] for n in names], *[new_m[n] for n in names],
            *[new_v[n] for n in names])
```

```python
import math

import jax
import jax.numpy as jnp
from jax import lax
from jax.experimental import pallas as pl
from jax.experimental.pallas import tpu as pltpu

F32 = jnp.float32
BF16 = jnp.bfloat16
MESH = pl.DeviceIdType.MESH

NORM_EPS = 1e-6
ROPE_THETA = 10000.0
GRID_W = 64
MLA_HEADS = 8
MLA_Q_LORA = 768
MLA_KV_LORA = 512
MLA_NOPE = 128
MLA_ROPE = 64
MLA_V = 128
GQA_HEADS = 8
GQA_KV_HEADS = 2
GQA_HEAD_DIM = 128
GQA_GROUP = GQA_HEADS // GQA_KV_HEADS
LANES = 128
KVP = MLA_KV_LORA + 2 * GQA_KV_HEADS * GQA_HEAD_DIM + LANES
QC = MLA_Q_LORA + GQA_HEADS * GQA_HEAD_DIM

ADAM_LR = 0.001
ADAM_B1 = 0.9
ADAM_B2 = 0.999
ADAM_EPS = 1e-08
ADAM_WD = 0.01
ADAM_STEP = 10

VMEM_LIMIT = 56 * 1024 * 1024


def _pick(dim, target, mult=LANES):
    t = (min(target, dim) // mult) * mult
    while t >= mult:
        if dim % t == 0:
            return t
        t -= mult
    return dim


def _params(sem):
    return pltpu.CompilerParams(dimension_semantics=sem, vmem_limit_bytes=VMEM_LIMIT)


_DIMS = {"NN": (((1,), (0,)), ((), ())), "NT": (((1,), (1,)), ((), ())), "TN": (((0,), (0,)), ((), ()))}


MM_VMEM_BUDGET = 36 * 1024 * 1024


def _mm_tiles(M, N, K, sa, sb, so, tm, tn, tk):
    tm, tn, tk = _pick(M, tm), _pick(N, tn), _pick(K, tk)

    def need(t):
        return 2 * (tm * t * sa + t * tn * sb) + 2 * tm * tn * so + (tm * tn * 4 if t < K else 0)

    while need(tk) > MM_VMEM_BUDGET and tk > LANES:
        smaller = _pick(K, tk - LANES)
        if smaller >= tk:
            break
        tk = smaller
    return tm, tn, tk


def _window(block, index, offsets):
    if not any(offsets):
        return pl.BlockSpec(block, index)
    for t, o in zip(block, offsets):
        assert o % 16 == 0 and t % 16 == 0, (block, offsets)

    def at(i, j, k):
        return tuple(pl.multiple_of(o + p * t, math.gcd(o, t)) for p, t, o in zip(index(i, j, k), block, offsets))

    return pl.BlockSpec(tuple(pl.Element(t) for t in block), at)


def _mm(a, b, mode, out_dtype, name, m=None, n=None, k=None, b_off=0, add=None, out_rows=None, out_base=None, out_off=0,
        tm=1024, tn=1024, tk=2304, act=None, bias=None, after=None):
    if mode == "NN":
        M, K, N = m or a.shape[0], k or a.shape[1], b.shape[1]
    elif mode == "NT":
        M, K, N = m or a.shape[0], a.shape[1], n or b.shape[0]
    else:
        M, K, N = a.shape[1], k or a.shape[0], b.shape[1]
    tm, tn, tk = _mm_tiles(M, N, K, a.dtype.itemsize, b.dtype.itemsize, jnp.dtype(out_dtype).itemsize, tm, tn, tk)
    nk = K // tk
    dims = _DIMS[mode]
    n_in = 2 + (bias is not None) + (add is not None) + (out_base is not None) + (after is not None)

    def body(*refs):
        a_ref, b_ref = refs[:2]
        bias_ref = refs[2] if bias is not None else None
        add_ref = refs[2 + (bias is not None)] if add is not None else None
        o_ref = refs[n_in]
        av = a_ref[...]
        if act == "silu":
            av = av * jax.nn.sigmoid(av)
        part = lax.dot_general(av.astype(BF16), b_ref[...].astype(BF16), dims, preferred_element_type=F32)

        def finish(r):
            if bias is not None:
                r = r + bias_ref[...]
            if add is not None:
                r = r + add_ref[...]
            o_ref[...] = r.astype(out_dtype)

        if nk == 1:
            finish(part)
            return
        acc = refs[-1]
        k = pl.program_id(2)

        @pl.when(k == 0)
        def _():
            acc[...] = part

        @pl.when(jnp.logical_and(k > 0, k < nk - 1))
        def _():
            acc[...] += part

        @pl.when(k == nk - 1)
        def _():
            finish(acc[...] + part)

    a_spec = pl.BlockSpec((tk, tm), lambda i, j, k: (k, i)) if mode == "TN" else pl.BlockSpec((tm, tk), lambda i, j, k: (i, k))
    if mode == "NT":
        b_spec = _window((tn, tk), lambda i, j, k: (j, k), (b_off, 0))
    else:
        b_spec = _window((tk, tn), lambda i, j, k: (k, j), (b_off, 0))
    in_specs, args = [a_spec, b_spec], [a, b]
    if bias is not None:
        in_specs.append(pl.BlockSpec((1, tn), lambda i, j, k: (0, j)))
        args.append(bias)
    if add is not None:
        in_specs.append(pl.BlockSpec((tm, tn), lambda i, j, k: (i, j)))
        args.append(add)
    aliases = {}
    if after is not None:
        in_specs.append(pl.BlockSpec(after.shape, lambda i, j, k: (0, 0)))
        args.append(after)
    if out_base is not None:
        aliases = {len(args): 0}
        in_specs.append(ANY)
        args.append(out_base)
        out_rows = out_base.shape[0]
    return pl.pallas_call(
        body,
        name=name,
        grid=(M // tm, N // tn, nk),
        in_specs=in_specs,
        out_specs=_window((tm, tn), lambda i, j, k: (i, j), (out_off, 0)),
        out_shape=jax.ShapeDtypeStruct((out_rows or M, N), out_dtype),
        input_output_aliases=aliases,
        scratch_shapes=[pltpu.VMEM((tm, tn), F32)] if nk > 1 else [],
        compiler_params=_params(("parallel", "parallel", "arbitrary")),
    )(*args)


def _rms(x):
    r = lax.rsqrt(jnp.mean(x * x, axis=-1, keepdims=True) + NORM_EPS)
    return x * r, r


def _rms_bwd(xh, r, dxh):
    return r * (dxh - xh * jnp.mean(dxh * xh, axis=-1, keepdims=True))


def _swap(x, q):
    lane = lax.broadcasted_iota(jnp.int32, x.shape, 1)
    even = ((lane // q) % 2) == 0
    return jnp.where(even, pltpu.roll(x, LANES - q, 1), pltpu.roll(x, q, 1))


def _rope(x, cos, ss, q):
    return x * cos + _swap(x, q) * ss


def _rope_t(d, cos, ss, q):
    return d * cos + _swap(d * ss, q)


def _csum(x):
    return jnp.sum(x, axis=0, keepdims=True)


def _rows(tr, w, off=0):
    return pl.BlockSpec((tr, w), lambda i: (i + off, 0))


def _bcast(w):
    return pl.BlockSpec((1, w), lambda i: (0, 0))


def _acc_init(i, refs):
    @pl.when(i == 0)
    def _():
        for r in refs:
            r[...] = jnp.zeros_like(r)


def _rope_tables(n_ctx, n_lat, rot_dim):
    rows = n_lat // GRID_W
    row = jnp.repeat(jnp.arange(rows, dtype=F32), GRID_W)
    col = jnp.tile(jnp.arange(GRID_W, dtype=F32), rows)
    half = rot_dim // 2
    inv_freq = ROPE_THETA ** (-jnp.arange(0, half, 2, dtype=F32) / half)
    ar, ac = row[:, None] * inv_freq, col[:, None] * inv_freq
    cos = jnp.concatenate([jnp.cos(ar), jnp.cos(ar), jnp.cos(ac), jnp.cos(ac)], axis=-1)
    ss = jnp.concatenate([-jnp.sin(ar), jnp.sin(ar), -jnp.sin(ac), jnp.sin(ac)], axis=-1)
    cos = jnp.tile(cos, (1, LANES // rot_dim))
    ss = jnp.tile(ss, (1, LANES // rot_dim))
    cos = jnp.concatenate([cos, jnp.ones((n_ctx, LANES), F32)], axis=0)
    ss = jnp.concatenate([ss, jnp.zeros((n_ctx, LANES), F32)], axis=0)
    return cos, ss


def _norm_mod_fwd(x2d, g, sh, sc, name, tr, out_rows=None, base=None, out_off=0):
    n, d = x2d.shape

    def body(x_ref, g_ref, sh_ref, sc_ref, *rest):
        xh, _ = _rms(x_ref[...])
        rest[-1][...] = ((xh * g_ref[...]) * (1.0 + sc_ref[...]) + sh_ref[...]).astype(BF16)

    args, in_specs, aliases = [x2d, g, sh, sc], [_rows(tr, d), _bcast(d), _bcast(d), _bcast(d)], {}
    if base is not None:
        args.append(base)
        in_specs.append(ANY)
        aliases = {4: 0}
        out_rows = base.shape[0]
    return pl.pallas_call(
        body,
        name=name,
        grid=(n // tr,),
        in_specs=in_specs,
        out_specs=_rows(tr, d, out_off // tr),
        out_shape=jax.ShapeDtypeStruct((out_rows or n, d), BF16),
        input_output_aliases=aliases,
        compiler_params=_params(("parallel",)),
    )(*args)


def _norm_mod_bwd(dz, dz_off, x2d, g, sc, dres, name, tr):
    n, d = x2d.shape
    want_dx = dres is not None

    def body(*refs):
        if want_dx:
            dz_ref, x_ref, g_ref, sc_ref, dres_ref, dx_ref, dg_ref, dsh_ref, dsc_ref = refs
        else:
            dz_ref, x_ref, g_ref, sc_ref, dg_ref, dsh_ref, dsc_ref = refs
        _acc_init(pl.program_id(0), [dg_ref, dsh_ref, dsc_ref])
        xh, r = _rms(x_ref[...])
        dzv = dz_ref[...]
        gv = g_ref[...]
        dsc_ref[...] += _csum(dzv * (xh * gv))
        dsh_ref[...] += _csum(dzv)
        dh = dzv * (1.0 + sc_ref[...])
        dg_ref[...] += _csum(dh * xh)
        if want_dx:
            dx_ref[...] = _rms_bwd(xh, r, dh * gv) + dres_ref[...]

    in_specs = [_rows(tr, d, dz_off), _rows(tr, d), _bcast(d), _bcast(d)]
    args = [dz, x2d, g, sc]
    out_specs = [_bcast(d)] * 3
    out_shape = [jax.ShapeDtypeStruct((1, d), F32)] * 3
    if want_dx:
        in_specs.append(_rows(tr, d))
        args.append(dres)
        out_specs = [_rows(tr, d)] + out_specs
        out_shape = [jax.ShapeDtypeStruct((n, d), F32)] + out_shape
    res = pl.pallas_call(
        body,
        name=name,
        grid=(n // tr,),
        in_specs=in_specs,
        out_specs=out_specs,
        out_shape=out_shape,
        compiler_params=_params(("arbitrary",)),
    )(*args)
    return res if want_dx else (None, *res)


_QA, _QB = MLA_ROPE // 4, GQA_HEAD_DIM // 4


def _kprep_fwd(pkv, kvg, kg, cos_a, ss_a, cos_b, ss_b, tr):
    n = pkv.shape[0]
    nb = GQA_KV_HEADS * GQA_HEAD_DIM

    def body(p_ref, kvg_ref, kg_ref, ca, sa, cb, sb, ckv_ref, kb_ref, vb_ref, kpe_ref):
        p = p_ref[...]
        xh, _ = _rms(p[:, :MLA_KV_LORA])
        ckv_ref[...] = (xh * kvg_ref[...]).astype(BF16)
        for e in range(GQA_KV_HEADS):
            lo = MLA_KV_LORA + e * GQA_HEAD_DIM
            kh, _ = _rms(p[:, lo : lo + GQA_HEAD_DIM])
            kb_ref[:, e * GQA_HEAD_DIM : (e + 1) * GQA_HEAD_DIM] = _rope(kh * kg_ref[...], cb[...], sb[...], _QB).astype(BF16)
        vb_ref[...] = p[:, MLA_KV_LORA + nb : MLA_KV_LORA + 2 * nb].astype(BF16)
        kr = _rope(p[:, MLA_KV_LORA + 2 * nb :], ca[...], sa[...], _QA)
        kpe_ref[:, :LANES] = kr.astype(BF16)
        kpe_ref[:, LANES:] = pltpu.roll(kr, MLA_ROPE, 1).astype(BF16)

    return pl.pallas_call(
        body,
        name="kprep_fwd",
        grid=(n // tr,),
        in_specs=[_rows(tr, KVP), _bcast(MLA_KV_LORA), _bcast(GQA_HEAD_DIM)] + [_rows(tr, LANES)] * 4,
        out_specs=[_rows(tr, MLA_KV_LORA), _rows(tr, nb), _rows(tr, nb), _rows(tr, 2 * LANES)],
        out_shape=[jax.ShapeDtypeStruct((n, w), BF16) for w in (MLA_KV_LORA, nb, nb, 2 * LANES)],
        compiler_params=_params(("parallel",)),
    )(pkv, kvg, kg, cos_a, ss_a, cos_b, ss_b)


def _kprep_bwd(pkv, dckv, dkb, dvb, dkpe, kvg, kg, cos_b, ss_b, tr):
    n = pkv.shape[0]
    nb = GQA_KV_HEADS * GQA_HEAD_DIM

    def body(p_ref, dckv_ref, dkb_ref, dvb_ref, dkpe_ref, kvg_ref, kg_ref, cb, sb, dp_ref, dkvg_ref, dkg_ref):
        _acc_init(pl.program_id(0), [dkvg_ref, dkg_ref])
        p = p_ref[...]
        xh, r = _rms(p[:, :MLA_KV_LORA])
        dn = dckv_ref[...]
        dkvg_ref[...] += _csum(dn * xh)
        dp_ref[:, :MLA_KV_LORA] = _rms_bwd(xh, r, dn * kvg_ref[...]).astype(BF16)
        for e in range(GQA_KV_HEADS):
            lo = MLA_KV_LORA + e * GQA_HEAD_DIM
            kh, rk = _rms(p[:, lo : lo + GQA_HEAD_DIM])
            dk = _rope_t(dkb_ref[:, e * GQA_HEAD_DIM : (e + 1) * GQA_HEAD_DIM], cb[...], sb[...], _QB)
            dkg_ref[...] += _csum(dk * kh)
            dp_ref[:, lo : lo + GQA_HEAD_DIM] = _rms_bwd(kh, rk, dk * kg_ref[...]).astype(BF16)
        dp_ref[:, MLA_KV_LORA + nb : MLA_KV_LORA + 2 * nb] = dvb_ref[...].astype(BF16)
        dp_ref[:, MLA_KV_LORA + 2 * nb :] = dkpe_ref[...].astype(BF16)

    return pl.pallas_call(
        body,
        name="kprep_bwd",
        grid=(n // tr,),
        in_specs=[_rows(tr, KVP), _rows(tr, MLA_KV_LORA), _rows(tr, nb), _rows(tr, nb), _rows(tr, LANES),
                  _bcast(MLA_KV_LORA), _bcast(GQA_HEAD_DIM), _rows(tr, LANES), _rows(tr, LANES)],
        out_specs=[_rows(tr, KVP), _bcast(MLA_KV_LORA), _bcast(GQA_HEAD_DIM)],
        out_shape=[jax.ShapeDtypeStruct((n, KVP), BF16), jax.ShapeDtypeStruct((1, MLA_KV_LORA), F32),
                   jax.ShapeDtypeStruct((1, GQA_HEAD_DIM), F32)],
        compiler_params=_params(("arbitrary",)),
    )(pkv, dckv, dkb, dvb, dkpe, kvg, kg, cos_b, ss_b)


def _kgrad_split(dka, dva, cos_a, ss_a, tr):
    n = dka.shape[0]
    wk = MLA_HEADS * 2 * LANES

    def body(dk_ref, dv_ref, ca, sa, dkv_ref, dkpe_ref):
        even = jnp.zeros((tr, LANES), F32)
        odd = jnp.zeros((tr, LANES), F32)
        for h in range(MLA_HEADS):
            dkv_ref[:, 2 * h * LANES : (2 * h + 1) * LANES] = dk_ref[:, 2 * h * LANES : (2 * h + 1) * LANES].astype(BF16)
            dkv_ref[:, (2 * h + 1) * LANES : (2 * h + 2) * LANES] = dv_ref[:, h * MLA_V : (h + 1) * MLA_V].astype(BF16)
            part = dk_ref[:, (2 * h + 1) * LANES : (2 * h + 2) * LANES]
            if h % 2 == 0:
                even = even + part
            else:
                odd = odd + part
        lane = lax.broadcasted_iota(jnp.int32, (tr, LANES), 1)
        low = lane < MLA_ROPE
        both = jnp.where(low, even, odd)
        tot = jnp.where(low, both + pltpu.roll(both, MLA_ROPE, 1), 0.0)
        dkpe_ref[...] = _rope_t(tot, ca[...], sa[...], _QA)

    return pl.pallas_call(
        body,
        name="kgrad_split",
        grid=(n // tr,),
        in_specs=[_rows(tr, wk), _rows(tr, MLA_HEADS * MLA_V), _rows(tr, LANES), _rows(tr, LANES)],
        out_specs=[_rows(tr, wk), _rows(tr, LANES)],
        out_shape=[jax.ShapeDtypeStruct((n, wk), BF16), jax.ShapeDtypeStruct((n, LANES), F32)],
        compiler_params=_params(("parallel",)),
    )(dka, dva, cos_a, ss_a)


def _qprep_fwd(pq, qg, gq, cos_b, ss_b, tr):
    n = pq.shape[0]
    nq = GQA_HEADS * GQA_HEAD_DIM

    def body(p_ref, qg_ref, gq_ref, cb, sb, cq_ref, qb_ref):
        xh, _ = _rms(p_ref[:, :MLA_Q_LORA])
        cq_ref[...] = (xh * qg_ref[...]).astype(BF16)
        for h in range(GQA_HEADS):
            lo = MLA_Q_LORA + h * GQA_HEAD_DIM
            qh, _ = _rms(p_ref[:, lo : lo + GQA_HEAD_DIM])
            qb_ref[:, h * GQA_HEAD_DIM : (h + 1) * GQA_HEAD_DIM] = _rope(qh * gq_ref[...], cb[...], sb[...], _QB).astype(BF16)

    return pl.pallas_call(
        body,
        name="qprep_fwd",
        grid=(n // tr,),
        in_specs=[_rows(tr, QC), _bcast(MLA_Q_LORA), _bcast(GQA_HEAD_DIM), _rows(tr, LANES), _rows(tr, LANES)],
        out_specs=[_rows(tr, MLA_Q_LORA), _rows(tr, nq)],
        out_shape=[jax.ShapeDtypeStruct((n, MLA_Q_LORA), BF16), jax.ShapeDtypeStruct((n, nq), BF16)],
        compiler_params=_params(("parallel",)),
    )(pq, qg, gq, cos_b, ss_b)


def _qprep_bwd(pq, dcq, dqb, qg, gq, cos_b, ss_b, tr):
    n = pq.shape[0]
    nq = GQA_HEADS * GQA_HEAD_DIM

    def body(p_ref, dcq_ref, dqb_ref, qg_ref, gq_ref, cb, sb, dp_ref, dqg_ref, dgq_ref):
        _acc_init(pl.program_id(0), [dqg_ref, dgq_ref])
        xh, r = _rms(p_ref[:, :MLA_Q_LORA])
        dn = dcq_ref[...]
        dqg_ref[...] += _csum(dn * xh)
        dp_ref[:, :MLA_Q_LORA] = _rms_bwd(xh, r, dn * qg_ref[...]).astype(BF16)
        for h in range(GQA_HEADS):
            lo = MLA_Q_LORA + h * GQA_HEAD_DIM
            qh, rq = _rms(p_ref[:, lo : lo + GQA_HEAD_DIM])
            dq = _rope_t(dqb_ref[:, h * GQA_HEAD_DIM : (h + 1) * GQA_HEAD_DIM], cb[...], sb[...], _QB)
            dgq_ref[...] += _csum(dq * qh)
            dp_ref[:, lo : lo + GQA_HEAD_DIM] = _rms_bwd(qh, rq, dq * gq_ref[...]).astype(BF16)

    return pl.pallas_call(
        body,
        name="qprep_bwd",
        grid=(n // tr,),
        in_specs=[_rows(tr, QC), _rows(tr, MLA_Q_LORA), _rows(tr, nq), _bcast(MLA_Q_LORA), _bcast(GQA_HEAD_DIM),
                  _rows(tr, LANES), _rows(tr, LANES)],
        out_specs=[_rows(tr, QC), _bcast(MLA_Q_LORA), _bcast(GQA_HEAD_DIM)],
        out_shape=[jax.ShapeDtypeStruct((n, QC), BF16), jax.ShapeDtypeStruct((1, MLA_Q_LORA), F32),
                   jax.ShapeDtypeStruct((1, GQA_HEAD_DIM), F32)],
        compiler_params=_params(("arbitrary",)),
    )(pq, dcq, dqb, qg, gq, cos_b, ss_b)


_QA_COLS = MLA_HEADS * (MLA_NOPE + MLA_ROPE)


def _qrope_fwd(qa, cos_a, ss_a, tr):
    n = qa.shape[0]

    def body(q_ref, ca, sa, o_ref):
        for j in range(MLA_HEADS // 2):
            lo = 3 * j * LANES
            o_ref[:, lo : lo + 2 * LANES] = q_ref[:, lo : lo + 2 * LANES].astype(BF16)
            o_ref[:, lo + 2 * LANES : lo + 3 * LANES] = _rope(q_ref[:, lo + 2 * LANES : lo + 3 * LANES], ca[...], sa[...], _QA).astype(BF16)

    return pl.pallas_call(
        body,
        name="qrope_fwd",
        grid=(n // tr,),
        in_specs=[_rows(tr, _QA_COLS), _rows(tr, LANES), _rows(tr, LANES)],
        out_specs=_rows(tr, _QA_COLS),
        out_shape=jax.ShapeDtypeStruct((n, _QA_COLS), BF16),
        compiler_params=_params(("parallel",)),
    )(qa, cos_a, ss_a)


def _qrope_bwd(dq2, cos_a, ss_a, tr):
    n = dq2.shape[0]

    def body(d_ref, ca, sa, o_ref):
        for j in range(MLA_HEADS // 2):
            lo = 3 * j * LANES
            h0, h1 = 2 * j, 2 * j + 1
            o_ref[:, lo : lo + LANES] = d_ref[:, 2 * h0 * LANES : (2 * h0 + 1) * LANES].astype(BF16)
            o_ref[:, lo + LANES : lo + 2 * LANES] = d_ref[:, 2 * h1 * LANES : (2 * h1 + 1) * LANES].astype(BF16)
            pe = d_ref[:, (2 * h0 + 1) * LANES : (2 * h0 + 2) * LANES] + d_ref[:, (2 * h1 + 1) * LANES : (2 * h1 + 2) * LANES]
            o_ref[:, lo + 2 * LANES : lo + 3 * LANES] = _rope_t(pe, ca[...], sa[...], _QA).astype(BF16)

    return pl.pallas_call(
        body,
        name="qrope_bwd",
        grid=(n // tr,),
        in_specs=[_rows(tr, MLA_HEADS * 2 * LANES), _rows(tr, LANES), _rows(tr, LANES)],
        out_specs=_rows(tr, _QA_COLS),
        out_shape=jax.ShapeDtypeStruct((n, _QA_COLS), BF16),
        compiler_params=_params(("parallel",)),
    )(dq2, cos_a, ss_a)


def _cat(refs):
    vals = [r[...] for r in refs]
    return vals[0] if len(vals) == 1 else jnp.concatenate(vals, axis=-1)


LOG2E = 1.4426950408889634


def _attn_fwd(qparts, kparts, vpart, n_heads, group, dv, scale, name, tq, after=None):
    T, Tk = qparts[0][0].shape[0], kparts[0][0].shape[0]
    nq_, nk_ = len(qparts), len(kparts)
    sub = min(tq, 256)
    c2 = scale * LOG2E

    def body(*refs):
        q_refs, k_refs = refs[:nq_], refs[nq_ : nq_ + nk_]
        v_ref = refs[nq_ + nk_]
        o_ref, lse_ref = refs[-2:]
        k = _cat(k_refs)
        v = v_ref[...]
        for r0 in range(0, tq, sub):
            q = _cat([r.at[r0 : r0 + sub, :] for r in q_refs])
            s = lax.dot_general(q, k, _DIMS["NT"], preferred_element_type=F32)
            m = jnp.max(s, axis=-1, keepdims=True)
            p = jnp.exp2((s - m) * c2)
            l = jnp.sum(p, axis=-1, keepdims=True)
            acc = jnp.dot(p.astype(BF16), v, preferred_element_type=F32)
            o_ref[r0 : r0 + sub, :] = (acc * (1.0 / l)).astype(BF16)
            lse_ref[r0 : r0 + sub, :] = m * scale + jnp.log(l)

    in_specs = [pl.BlockSpec((tq, LANES), lambda h, i, f=f: (i, f(h))) for _, f in qparts]
    in_specs += [pl.BlockSpec((Tk, LANES), lambda h, i, f=f: (0, f(h // group))) for _, f in kparts]
    fv = vpart[1]
    in_specs.append(pl.BlockSpec((Tk, dv), lambda h, i: (0, fv(h // group))))
    args = [*[a for a, _ in qparts], *[a for a, _ in kparts], vpart[0]]
    if after is not None:
        in_specs.append(pl.BlockSpec(after.shape, lambda h, i: (0, 0)))
        args.append(after)
    return pl.pallas_call(
        body,
        name=name,
        grid=(n_heads, T // tq),
        in_specs=in_specs,
        out_specs=[pl.BlockSpec((tq, dv), lambda h, i: (i, h)), pl.BlockSpec((None, tq, 1), lambda h, i: (h, i, 0))],
        out_shape=[jax.ShapeDtypeStruct((T, n_heads * dv), BF16), jax.ShapeDtypeStruct((n_heads, T, 1), F32)],
        compiler_params=_params(("parallel", "parallel")),
    )(*args)


def _attn_bwd(qparts, kparts, vpart, o, do, lse, n_heads, group, dv, scale, name, tq):
    T, Tk = qparts[0][0].shape[0], kparts[0][0].shape[0]
    nq_, nk_ = len(qparts), len(kparts)
    dk_ = LANES * nq_
    n_kv = n_heads // group
    nblk = T // tq
    c2 = scale * LOG2E

    def head(hk, i):
        return hk * group + i // nblk

    sub = min(tq, 256)

    def body(*refs):
        q_refs = refs[:nq_]
        k = _cat(refs[nq_ : nq_ + nk_])
        v_ref, o_ref, do_ref, lse_ref, dq_ref, dk_ref, dv_ref = refs[nq_ + nk_ :]
        i = pl.program_id(1)
        _acc_init(i, [dk_ref, dv_ref])
        v = v_ref[...]
        dk_acc, dv_acc = None, None
        for r0 in range(0, tq, sub):
            rows = slice(r0, r0 + sub)
            q = _cat([r.at[rows, :] for r in q_refs])
            s = lax.dot_general(q, k, _DIMS["NT"], preferred_element_type=F32)
            p = jnp.exp2(s * c2 - lse_ref[rows, :] * LOG2E)
            dov = do_ref[rows, :]
            dp = lax.dot_general(dov, v, _DIMS["NT"], preferred_element_type=F32)
            delta = jnp.sum(dov.astype(F32) * o_ref[rows, :].astype(F32), axis=-1, keepdims=True)
            ds = (p * (dp - delta)).astype(BF16)
            dq_ref[rows, :] = jnp.dot(ds, k, preferred_element_type=F32) * scale
            dk_part = lax.dot_general(ds, q, _DIMS["TN"], preferred_element_type=F32)
            dv_part = lax.dot_general(p.astype(BF16), dov, _DIMS["TN"], preferred_element_type=F32)
            dk_acc = dk_part if dk_acc is None else dk_acc + dk_part
            dv_acc = dv_part if dv_acc is None else dv_acc + dv_part
        dk_ref[...] += dk_acc
        dv_ref[...] += dv_acc

        @pl.when(i == group * nblk - 1)
        def _():
            dk_ref[...] *= scale

    in_specs = [pl.BlockSpec((tq, LANES), lambda hk, i, f=f: (i % nblk, f(head(hk, i)))) for _, f in qparts]
    in_specs += [pl.BlockSpec((Tk, LANES), lambda hk, i, f=f: (0, f(hk))) for _, f in kparts]
    fv = vpart[1]
    in_specs.append(pl.BlockSpec((Tk, dv), lambda hk, i: (0, fv(hk))))
    in_specs += [pl.BlockSpec((tq, dv), lambda hk, i: (i % nblk, head(hk, i)))] * 2
    in_specs.append(pl.BlockSpec((None, tq, 1), lambda hk, i: (head(hk, i), i % nblk, 0)))
    return pl.pallas_call(
        body,
        name=name,
        grid=(n_kv, group * nblk),
        in_specs=in_specs,
        out_specs=[pl.BlockSpec((tq, dk_), lambda hk, i: (i % nblk, head(hk, i))),
                   pl.BlockSpec((Tk, dk_), lambda hk, i: (0, hk)),
                   pl.BlockSpec((Tk, dv), lambda hk, i: (0, hk))],
        out_shape=[jax.ShapeDtypeStruct((T, n_heads * dk_), F32), jax.ShapeDtypeStruct((Tk, n_kv * dk_), F32),
                   jax.ShapeDtypeStruct((Tk, n_kv * dv), F32)],
        compiler_params=_params(("parallel", "arbitrary")),
    )(*[a for a, _ in qparts], *[a for a, _ in kparts], vpart[0], o, do, lse)


def _gates_fwd(pg, ya, yb, tr):
    n, d = ya.shape

    def body(pg_ref, ya_ref, yb_ref, o_ref):
        ga = jax.nn.sigmoid(pg_ref[:, :d].astype(F32))
        gb = jax.nn.sigmoid(pg_ref[:, d:].astype(F32))
        o_ref[...] = (ga * ya_ref[...].astype(F32) + gb * yb_ref[...].astype(F32)).astype(BF16)

    return pl.pallas_call(
        body,
        name="gates_fwd",
        grid=(n // tr,),
        in_specs=[_rows(tr, 2 * d), _rows(tr, d), _rows(tr, d)],
        out_specs=_rows(tr, d),
        out_shape=jax.ShapeDtypeStruct((n, d), BF16),
        compiler_params=_params(("parallel",)),
    )(pg, ya, yb)


def _gates_bwd(dm, pg, ya, yb, tr):
    n, d = ya.shape

    def body(dm_ref, pg_ref, ya_ref, yb_ref, dya_ref, dyb_ref, dpg_ref):
        dmv = dm_ref[...].astype(F32)
        ga = jax.nn.sigmoid(pg_ref[:, :d].astype(F32))
        gb = jax.nn.sigmoid(pg_ref[:, d:].astype(F32))
        dya_ref[...] = (dmv * ga).astype(BF16)
        dyb_ref[...] = (dmv * gb).astype(BF16)
        dpg_ref[:, :d] = (dmv * ya_ref[...].astype(F32) * ga * (1.0 - ga)).astype(BF16)
        dpg_ref[:, d:] = (dmv * yb_ref[...].astype(F32) * gb * (1.0 - gb)).astype(BF16)

    return pl.pallas_call(
        body,
        name="gates_bwd",
        grid=(n // tr,),
        in_specs=[_rows(tr, d), _rows(tr, 2 * d), _rows(tr, d), _rows(tr, d)],
        out_specs=[_rows(tr, d), _rows(tr, d), _rows(tr, 2 * d)],
        out_shape=[jax.ShapeDtypeStruct((n, d), BF16), jax.ShapeDtypeStruct((n, d), BF16), jax.ShapeDtypeStruct((n, 2 * d), BF16)],
        compiler_params=_params(("parallel",)),
    )(dm, pg, ya, yb)


def _resid_norm2_fwd(x2d, att, g1, n2g, sh2, sc2, tr):
    n, d = x2d.shape

    def body(x_ref, a_ref, g1_ref, g_ref, sh_ref, sc_ref, x1_ref, z_ref):
        x1 = x_ref[...] + g1_ref[...] * a_ref[...]
        x1_ref[...] = x1
        xh, _ = _rms(x1)
        z_ref[...] = ((xh * g_ref[...]) * (1.0 + sc_ref[...]) + sh_ref[...]).astype(BF16)

    return pl.pallas_call(
        body,
        name="resid_norm2_fwd",
        grid=(n // tr,),
        in_specs=[_rows(tr, d), _rows(tr, d)] + [_bcast(d)] * 4,
        out_specs=[_rows(tr, d), _rows(tr, d)],
        out_shape=[jax.ShapeDtypeStruct((n, d), F32), jax.ShapeDtypeStruct((n, d), BF16)],
        compiler_params=_params(("parallel",)),
    )(x2d, att, g1, n2g, sh2, sc2)


def _resid_norm2_bwd(dz2, x1, dx2, att, n2g, sc2, g1, tr):
    n, d = x1.shape

    def body(dz_ref, x1_ref, dx2_ref, a_ref, g_ref, sc_ref, g1_ref, dx1_ref, da_ref, dg_ref, dsh_ref, dsc_ref, dg1_ref):
        _acc_init(pl.program_id(0), [dg_ref, dsh_ref, dsc_ref, dg1_ref])
        xh, r = _rms(x1_ref[...])
        dzv = dz_ref[...]
        gv = g_ref[...]
        dsc_ref[...] += _csum(dzv * (xh * gv))
        dsh_ref[...] += _csum(dzv)
        dh = dzv * (1.0 + sc_ref[...])
        dg_ref[...] += _csum(dh * xh)
        dx1 = _rms_bwd(xh, r, dh * gv) + dx2_ref[...]
        dx1_ref[...] = dx1
        dg1_ref[...] += _csum(dx1 * a_ref[...])
        da_ref[...] = (dx1 * g1_ref[...]).astype(BF16)

    return pl.pallas_call(
        body,
        name="resid_norm2_bwd",
        grid=(n // tr,),
        in_specs=[_rows(tr, d)] * 4 + [_bcast(d)] * 3,
        out_specs=[_rows(tr, d), _rows(tr, d)] + [_bcast(d)] * 4,
        out_shape=[jax.ShapeDtypeStruct((n, d), F32), jax.ShapeDtypeStruct((n, d), BF16)] + [jax.ShapeDtypeStruct((1, d), F32)] * 4,
        compiler_params=_params(("arbitrary",)),
    )(dz2, x1, dx2, att, n2g, sc2, g1)


def _edges(shape):
    row = lax.broadcasted_iota(jnp.int32, shape, 0)
    return row == 0, row == shape[0] - 1


def _shifts(u, edges):
    n = u.shape[0]
    return jnp.where(edges[0], 0.0, pltpu.roll(u, 1, 0)), jnp.where(edges[1], 0.0, pltpu.roll(u, n - 1, 0))


def _conv3(u, prev, nxt, w_ref, b_ref):
    return b_ref[...] + w_ref[0:1, :] * prev + w_ref[1:2, :] * u + w_ref[2:3, :] * nxt


def _conv_fwd(u, cw, cb, tc):
    n, two_f = u.shape
    f = two_f // 2
    nb = f // tc

    def body(ua_ref, ub_ref, wa_ref, wb_ref, ba_ref, bb_ref, h_ref):
        edges = _edges((n, tc))
        ua = ua_ref[...].astype(F32)
        ub = ub_ref[...].astype(F32)
        a = _conv3(ua, *_shifts(ua, edges), wa_ref, ba_ref)
        b = _conv3(ub, *_shifts(ub, edges), wb_ref, bb_ref)
        h_ref[...] = (a * jax.nn.sigmoid(a) * b).astype(BF16)

    col = lambda rows, off: pl.BlockSpec((rows, tc), lambda i: (0, i + off))
    return pl.pallas_call(
        body,
        name="conv_fwd",
        grid=(nb,),
        in_specs=[col(n, 0), col(n, nb), col(3, 0), col(3, nb), col(1, 0), col(1, nb)],
        out_specs=col(n, 0),
        out_shape=jax.ShapeDtypeStruct((n, f), BF16),
        compiler_params=_params(("parallel",)),
    )(u, u, cw, cw, cb, cb)


def _conv_bwd(u, dh, cw, cb, tc):
    n, two_f = u.shape
    f = two_f // 2
    nb = f // tc

    def part(uv, prev, nxt, duc, edges, w_ref, du_ref, dw_ref, db_ref):
        db_ref[...] = _csum(duc)
        dw_ref[0:1, :] = _csum(duc * prev)
        dw_ref[1:2, :] = _csum(duc * uv)
        dw_ref[2:3, :] = _csum(duc * nxt)
        d_prev, d_next = _shifts(duc, edges)
        du_ref[...] = (w_ref[0:1, :] * d_next + w_ref[1:2, :] * duc + w_ref[2:3, :] * d_prev).astype(BF16)

    def body(ua_ref, ub_ref, dh_ref, wa_ref, wb_ref, ba_ref, bb_ref, dua_ref, dub_ref, dwa_ref, dwb_ref, dba_ref, dbb_ref):
        edges = _edges((n, tc))
        ua = ua_ref[...].astype(F32)
        ub = ub_ref[...].astype(F32)
        sa = _shifts(ua, edges)
        sb = _shifts(ub, edges)
        a = _conv3(ua, *sa, wa_ref, ba_ref)
        b = _conv3(ub, *sb, wb_ref, bb_ref)
        dhv = dh_ref[...].astype(F32)
        sg = jax.nn.sigmoid(a)
        da = dhv * b * (sg * (1.0 + a * (1.0 - sg)))
        db = dhv * (a * sg)
        part(ua, *sa, da, edges, wa_ref, dua_ref, dwa_ref, dba_ref)
        part(ub, *sb, db, edges, wb_ref, dub_ref, dwb_ref, dbb_ref)

    col = lambda rows, off: pl.BlockSpec((rows, tc), lambda i: (0, i + off))
    return pl.pallas_call(
        body,
        name="conv_bwd",
        grid=(nb,),
        in_specs=[col(n, 0), col(n, nb), col(n, 0), col(3, 0), col(3, nb), col(1, 0), col(1, nb)],
        out_specs=[col(n, 0), col(n, 0), col(3, 0), col(3, 0), col(1, 0), col(1, 0)],
        out_shape=[jax.ShapeDtypeStruct((n, f), BF16)] * 2 + [jax.ShapeDtypeStruct((3, f), F32)] * 2 + [jax.ShapeDtypeStruct((1, f), F32)] * 2,
        compiler_params=_params(("parallel",)),
    )(u, u, dh, cw, cw, cb, cb)


def _loss_head(x1, f, g2, fg, tgt, tr):
    n, d = x1.shape

    def body(x1_ref, f_ref, g2_ref, fg_ref, t_ref, sq_ref, dx2_ref, dfg_ref, dg2_ref, df_ref):
        _acc_init(pl.program_id(0), [sq_ref, dfg_ref, dg2_ref])
        fv = f_ref[...]
        xh, r = _rms(x1_ref[...] + g2_ref[...] * fv)
        err = xh * fg_ref[...] - t_ref[...]
        sq_ref[...] += _csum(err * err)
        dy = err * (1.0 / d)
        dfg_ref[...] += _csum(dy * xh)
        dx2 = _rms_bwd(xh, r, dy * fg_ref[...])
        dx2_ref[...] = dx2
        dg2_ref[...] += _csum(dx2 * fv)
        df_ref[...] = (dx2 * g2_ref[...]).astype(BF16)

    return pl.pallas_call(
        body,
        name="loss_head",
        grid=(n // tr,),
        in_specs=[_rows(tr, d), _rows(tr, d), _bcast(d), _bcast(d), _rows(tr, d)],
        out_specs=[_bcast(d), _rows(tr, d), _bcast(d), _bcast(d), _rows(tr, d)],
        out_shape=[jax.ShapeDtypeStruct((1, d), F32), jax.ShapeDtypeStruct((n, d), F32), jax.ShapeDtypeStruct((1, d), F32),
                   jax.ShapeDtypeStruct((1, d), F32), jax.ShapeDtypeStruct((n, d), BF16)],
        compiler_params=_params(("arbitrary",)),
    )(x1, f, g2, fg, tgt)


def _sum_slots(g, name):
    s, r, w = g.shape

    def body(g_ref, o_ref):
        acc = g_ref[0]
        for k in range(1, s):
            acc = acc + g_ref[k]
        o_ref[...] = acc

    return pl.pallas_call(body, name=name, out_shape=jax.ShapeDtypeStruct((r, w), F32))(g)


def _silu_grad_mul(ds, cvec):
    def body(d_ref, c_ref, o_ref):
        cv = c_ref[...]
        sg = jax.nn.sigmoid(cv)
        o_ref[...] = d_ref[...] * (sg * (1.0 + cv * (1.0 - sg)))

    return pl.pallas_call(body, name="silu_grad_mul", out_shape=jax.ShapeDtypeStruct(ds.shape, F32))(ds, cvec)


def _adamw_update(wv, gv, mv, vv, d_ref, mo_ref, vo_ref):
    mn = ADAM_B1 * mv + (1.0 - ADAM_B1) * gv
    vn = ADAM_B2 * vv + (1.0 - ADAM_B2) * (gv * gv)
    mo_ref[...] = mn
    vo_ref[...] = vn
    m_hat = mn / (1.0 - ADAM_B1**ADAM_STEP)
    v_hat = vn / (1.0 - ADAM_B2**ADAM_STEP)
    d_ref[...] = -ADAM_LR * (m_hat / (jnp.sqrt(v_hat) + ADAM_EPS) + ADAM_WD * wv)


def _adamw_many(ws, gs, ms, vs, name):
    n = len(ws)

    def body(*refs):
        for k in range(n):
            w_ref, g_ref, m_ref, v_ref = (refs[q * n + k] for q in range(4))
            d_ref, mo_ref, vo_ref = (refs[(4 + q) * n + k] for q in range(3))
            _adamw_update(w_ref[...], g_ref[...], m_ref[...], v_ref[...], d_ref, mo_ref, vo_ref)

    res = pl.pallas_call(body, name=name, out_shape=[jax.ShapeDtypeStruct(w.shape, F32) for w in ws] * 3)(*ws, *gs, *ms, *vs)
    return res[:n], res[n : 2 * n], res[2 * n :]


def _adamw(w, g, m, v, name, g_transposed=False, g_sibling=None):
    r, cdim = w.shape
    halves = g_sibling is not None
    if g_transposed:
        tr = _pick(r, 1024, LANES)
        tc = _pick(cdim // 2 if halves else cdim, max(LANES, (1 << 19) // tr))
        per_half = (cdim // 2) // tc
    else:
        tr = _pick(r // 2 if halves else r, 1024, 8)
        tc = _pick(cdim, max(LANES, (1 << 19) // tr))
        per_half = (r // 2) // tr
    emit_g = g_transposed or halves

    def body(w_ref, g_ref, *rest):
        m_ref, v_ref = rest[halves : halves + 2]
        outs = rest[halves + 2 :]
        gv = g_ref[...]
        if halves:
            along = pl.program_id(1 if g_transposed else 0)
            gv = jnp.where(along // per_half == lax.axis_index("c"), gv, rest[0][...])
        if g_transposed:
            gv = gv.T
        if emit_g:
            outs[0][...] = gv
        _adamw_update(w_ref[...], gv, m_ref[...], v_ref[...], *outs[-3:])

    spec = pl.BlockSpec((tr, tc), lambda i, j: (i, j))
    if g_transposed:
        g_spec = pl.BlockSpec((tc, tr), lambda i, j: (j % per_half if halves else j, i))
    else:
        g_spec = pl.BlockSpec((tr, tc), lambda i, j: (i % per_half if halves else i, j))
    n_out = 3 + emit_g
    res = pl.pallas_call(
        body,
        name=name,
        grid=(r // tr, cdim // tc),
        in_specs=[spec, g_spec] + [g_spec] * halves + [spec, spec],
        out_specs=[spec] * n_out,
        out_shape=[jax.ShapeDtypeStruct((r, cdim), F32)] * n_out,
        compiler_params=_params(("parallel", "parallel")),
    )(w, g, *([g_sibling] if halves else []), m, v)
    return res if emit_g else [g, *res]


def _place():
    return lax.axis_index("x"), lax.axis_index("y"), lax.axis_index("c")


def _remote(src, dst, send_sem, recv_sem, dev):
    return pltpu.make_async_remote_copy(src_ref=src, dst_ref=dst, send_sem=send_sem, recv_sem=recv_sem, device_id=dev, device_id_type=MESH)


ANY = pl.BlockSpec(memory_space=pl.ANY)


def _all_gather_small(v, name):
    r, w = v.shape

    def body(v_ref, o_ref, send, recv, lsem):
        x, y, c = _place()
        me = 4 * x + 2 * y + c
        mine = pltpu.make_async_copy(v_ref, o_ref.at[me], lsem)
        mine.start()
        sent = []
        for k in range(1, 8):
            px, py, pc = x ^ (k >> 2), y ^ ((k >> 1) & 1), c ^ (k & 1)
            cp = _remote(v_ref, o_ref.at[me], send.at[k - 1], recv.at[k - 1], (px, py, pc))
            cp.start()
            sent.append(cp)
        for k in range(1, 8):
            px, py, pc = x ^ (k >> 2), y ^ ((k >> 1) & 1), c ^ (k & 1)
            slot = o_ref.at[4 * px + 2 * py + pc]
            _remote(slot, slot, send.at[k - 1], recv.at[k - 1], (x, y, c)).wait_recv()
        for cp in sent:
            cp.wait_send()
        mine.wait()

    return pl.pallas_call(
        body,
        name=name,
        out_shape=jax.ShapeDtypeStruct((8, r, w), F32),
        in_specs=[pl.BlockSpec(memory_space=pltpu.VMEM)],
        out_specs=pl.BlockSpec(memory_space=pltpu.VMEM),
        scratch_shapes=[pltpu.SemaphoreType.DMA((7,)), pltpu.SemaphoreType.DMA((7,)), pltpu.SemaphoreType.DMA],
        compiler_params=pltpu.CompilerParams(vmem_limit_bytes=VMEM_LIMIT),
    )(v)


HBM = pl.BlockSpec(memory_space=pltpu.HBM)
SEM = pl.BlockSpec(memory_space=pltpu.SEMAPHORE)
EFFECT = pltpu.SideEffectType.DATAFLOW_SIDE_EFFECTING


def _other_chips(x, y):
    return [(1 - x, y), (x, 1 - y), (1 - x, 1 - y)]


def _bulk_start(name, srcs, land_shapes, n_copies, copies, after):
    n, m = len(srcs), len(land_shapes)

    def body(*refs):
        src_refs, land_refs = refs[:n], refs[n : n + m]
        send, recv = refs[n + m + 1], refs[n + m + 2]
        token = refs[-1]
        for k, (s, d, dev) in enumerate(copies(src_refs, land_refs)):
            _remote(s, d, send.at[k], recv.at[k], dev).start()
        token[...] = jnp.zeros_like(token)

    lands = [pltpu.with_memory_space_constraint(lax.empty(s.shape, s.dtype), pltpu.HBM) for s in land_shapes]
    out = pl.pallas_call(
        body,
        name=name,
        out_shape=(pltpu.SemaphoreType.DMA((n_copies,)), pltpu.SemaphoreType.DMA((n_copies,)),
                   *[pltpu.HBM(s.shape, s.dtype) for s in srcs], *[pltpu.HBM(s.shape, s.dtype) for s in land_shapes],
                   jax.ShapeDtypeStruct((8, LANES), F32)),
        in_specs=[HBM] * (n + m) + [ANY],
        out_specs=(SEM, SEM, *[HBM] * (n + m), pl.BlockSpec(memory_space=pltpu.VMEM)),
        input_output_aliases={i: 2 + i for i in range(n + m)},
        compiler_params=pltpu.CompilerParams(has_side_effects=EFFECT),
    )(*[pltpu.with_memory_space_constraint(s, pltpu.HBM) for s in srcs], *lands, after)
    return out[0], out[1], list(out[2 : 2 + n]), list(out[2 + n : 2 + n + m]), out[-1][0:1, 0:1]


def _bulk_wait(name, send, recv, srcs, lands, after, waits):
    n, m = len(srcs), len(lands)

    def body(*refs):
        src_refs, land_refs = refs[:n], refs[n : n + m]
        send_sem, recv_sem = refs[n + m], refs[n + m + 1]
        x, y, c = _place()
        for k, (s, d) in enumerate(waits(src_refs, land_refs)):
            cp = _remote(s, d, send_sem.at[k], recv_sem.at[k], (x, y, c))
            cp.wait_send()
            cp.wait_recv()

    out = pl.pallas_call(
        body,
        name=name,
        out_shape=tuple(pltpu.HBM(s.shape, s.dtype) for s in (*srcs, *lands)),
        in_specs=[HBM] * (n + m) + [SEM, SEM, ANY],
        out_specs=tuple([HBM] * (n + m)),
        input_output_aliases={i: i for i in range(n + m)},
        compiler_params=pltpu.CompilerParams(has_side_effects=EFFECT),
    )(*srcs, *lands, send, recv, after)
    return list(out[:n]), list(out[n:])


def _gather_start(shards, after, name):
    def copies(src, land):
        x, y, c = _place()
        j = 2 * x + y
        return [(src[a].at[c], land[a].at[j, c], (px, py, c)) for a in range(len(shards)) for px, py in _other_chips(x, y)]

    shapes = [jax.ShapeDtypeStruct((4,) + s.shape, s.dtype) for s in shards]
    return _bulk_start(name, shards, shapes, 3 * len(shards), copies, after)


def _gather_wait(started, after, name):
    send, recv, srcs, lands, _ = started

    def waits(src, land):
        x, y, c = _place()
        return [(src[a].at[c], land[a].at[2 * px + py, c]) for a in range(len(srcs)) for px, py in _other_chips(x, y)]

    return _bulk_wait(name, send, recv, srcs, lands, after, waits)


def _forward_halves(lands, name):
    n = len(lands)

    def body(*refs):
        bufs = refs[n : 2 * n]
        send, recv = refs[2 * n :]
        x, y, c = _place()
        started = []
        for a in range(n):
            for k, (px, py) in enumerate(_other_chips(x, y)):
                blk = bufs[a].at[2 * px + py, c]
                cp = _remote(blk, blk, send.at[3 * a + k], recv.at[3 * a + k], (x, y, 1 - c))
                cp.start()
                started.append(cp)
        for a in range(n):
            for k, (px, py) in enumerate(_other_chips(x, y)):
                blk = bufs[a].at[2 * px + py, 1 - c]
                _remote(blk, blk, send.at[3 * a + k], recv.at[3 * a + k], (x, y, c)).wait_recv()
        for cp in started:
            cp.wait_send()

    return pl.pallas_call(
        body,
        name=name,
        out_shape=[jax.ShapeDtypeStruct(b.shape, b.dtype) for b in lands],
        in_specs=[ANY] * n,
        out_specs=[ANY] * n,
        input_output_aliases={i: i for i in range(n)},
        scratch_shapes=[pltpu.SemaphoreType.DMA((3 * n,)), pltpu.SemaphoreType.DMA((3 * n,))],
    )(*lands)


def _forward_start(lands, after, name):
    def copies(src, _):
        x, y, c = _place()
        blocks = [src[a].at[2 * px + py, c] for a in range(len(lands)) for px, py in _other_chips(x, y)]
        return [(b, b, (x, y, 1 - c)) for b in blocks]

    return _bulk_start(name, lands, [], 3 * len(lands), copies, after)


def _forward_wait(started, after, name):
    send, recv, bufs, _, _ = started

    def waits(src, _):
        x, y, c = _place()
        return [(src[a].at[2 * px + py, c], src[a].at[2 * px + py, 1 - c]) for a in range(len(bufs)) for px, py in _other_chips(x, y)]

    return _bulk_wait(name, send, recv, bufs, [], after, waits)[0]


def _place_own(shards, lands):
    j = 2 * lax.axis_index("x") + lax.axis_index("y")
    full = [lax.dynamic_update_slice(b, s[None], (j, 0, 0, 0)) for b, s in zip(lands, shards)]
    return [f.reshape(4 * f.shape[2] * 2, f.shape[3]) for f in full]


def _gather_finish(started, after, tag):
    shards, lands = _gather_wait(started, after, "gather_wait_" + tag)
    return _place_own(shards, _forward_halves(lands, "gather_forward_" + tag))


def _gather_land(started, after, tag):
    shards, lands = _gather_wait(started, after, "gather_wait_" + tag)
    return shards, _forward_start(lands, shards[0], "forward_start_" + tag)


def _gather_done(landed, after, tag):
    shards, fwd = landed
    return _place_own(shards, _forward_wait(fwd, after, "forward_wait_" + tag))


def _swap_halves(grads, name):
    n = len(grads)

    def body(*refs):
        ins, outs = refs[:n], refs[n : 2 * n]
        send, recv = refs[2 * n :]
        x, y, c = _place()
        started = []
        for a in range(n):
            for s in range(4):
                cp = _remote(ins[a].at[s, 1 - c], outs[a].at[s], send.at[4 * a + s], recv.at[4 * a + s], (x, y, 1 - c))
                cp.start()
                started.append(cp)
        for cp in started:
            cp.wait_recv()
        for cp in started:
            cp.wait_send()

    return pl.pallas_call(
        body,
        name=name,
        out_shape=[jax.ShapeDtypeStruct((4,) + g.shape[2:], g.dtype) for g in grads],
        in_specs=[ANY] * n,
        out_specs=[ANY] * n,
        scratch_shapes=[pltpu.SemaphoreType.DMA((4 * n,)), pltpu.SemaphoreType.DMA((4 * n,))],
    )(*grads)


def _add_halves(grads, others, tag):
    outs = []
    for a, (g, o) in enumerate(zip(grads, others)):
        _, _, rh, cdim = g.shape
        tr = _pick(rh, 512, 16)

        def body(g_ref, o_ref, p_ref):
            p_ref[...] = (g_ref[...].astype(F32) + o_ref[...].astype(F32)).astype(BF16)

        outs.append(
            pl.pallas_call(
                body,
                name=f"add_halves_{tag}{a}",
                grid=(4, rh // tr),
                in_specs=[pl.BlockSpec((None, None, tr, cdim), lambda s, i: (s, lax.axis_index("c"), i, 0)),
                          pl.BlockSpec((None, tr, cdim), lambda s, i: (s, i, 0))],
                out_specs=pl.BlockSpec((None, tr, cdim), lambda s, i: (s, i, 0)),
                out_shape=jax.ShapeDtypeStruct((4, rh, cdim), BF16),
                compiler_params=_params(("parallel", "parallel")),
            )(g, o)
        )
    return outs


def _exchange_start(parts, after, name):
    def copies(src, land):
        x, y, c = _place()
        j = 2 * x + y
        return [(src[a].at[2 * px + py], land[a].at[j], (px, py, c)) for a in range(len(parts)) for px, py in _other_chips(x, y)]

    return _bulk_start(name, parts, [jax.ShapeDtypeStruct(p.shape, p.dtype) for p in parts], 3 * len(parts), copies, after)


def _exchange_finish(started, after, name):
    send, recv, srcs, lands, _ = started

    def waits(src, land):
        x, y, _ = _place()
        return [(src[a].at[2 * px + py], land[a].at[2 * px + py]) for a in range(len(srcs)) for px, py in _other_chips(x, y)]

    srcs, lands = _bulk_wait(name, send, recv, srcs, lands, after, waits)
    j = 2 * lax.axis_index("x") + lax.axis_index("y")
    return [lax.dynamic_update_slice(b, lax.dynamic_slice(p, (j, 0, 0), (1,) + p.shape[1:]), (j, 0, 0)) for b, p in zip(lands, srcs)]


def _sum_chips(recvd, tag):
    outs = []
    for a, g in enumerate(recvd):
        _, rh, cdim = g.shape
        tr = _pick(rh, 512, 16)

        def body(g_ref, o_ref):
            o_ref[...] = ((g_ref[0].astype(F32) + g_ref[1].astype(F32)) + g_ref[2].astype(F32)) + g_ref[3].astype(F32)

        outs.append(
            pl.pallas_call(
                body,
                name=f"sum_chips_{tag}{a}",
                grid=(rh // tr,),
                in_specs=[pl.BlockSpec((4, tr, cdim), lambda i: (0, i, 0))],
                out_specs=pl.BlockSpec((tr, cdim), lambda i: (i, 0)),
                out_shape=jax.ShapeDtypeStruct((rh, cdim), F32),
                compiler_params=_params(("parallel",)),
            )(g)
        )
    return outs


def _join_halves(halves, name):
    n = len(halves)

    def body(*refs):
        ins, outs = refs[:n], refs[n : 2 * n]
        send, recv = refs[2 * n :]
        x, y, c = _place()
        started = []
        for a in range(n):
            cp = _remote(ins[a], outs[a], send.at[a], recv.at[a], (x, y, 1 - c))
            cp.start()
            started.append(cp)
        for cp in started:
            cp.wait_recv()
        for cp in started:
            cp.wait_send()

    others = pl.pallas_call(
        body,
        name=name,
        out_shape=[jax.ShapeDtypeStruct(h.shape, h.dtype) for h in halves],
        in_specs=[ANY] * n,
        out_specs=[ANY] * n,
        scratch_shapes=[pltpu.SemaphoreType.DMA((n,)), pltpu.SemaphoreType.DMA((n,))],
    )(*halves)
    return list(zip(halves, others))


def _joined(mine, other):
    first = lax.axis_index("c") == 0
    return jnp.concatenate([jnp.where(first, mine, other), jnp.where(first, other, mine)], axis=0)


def _grad_views(grads):
    return [g.reshape(4, 2, g.shape[0] // 8, g.shape[1]) for g in grads]


def _scatter_start(grads, tag, after=None):
    views = _grad_views(grads)
    mine = _add_halves(views, _swap_halves(views, "swap_halves_" + tag), tag)
    return _exchange_start(mine, mine[-1] if after is None else after, "exchange_start_" + tag)


def _swap_start(grads, after, tag):
    views = _grad_views(grads)

    def copies(src, land):
        x, y, c = _place()
        return [(src[a].at[s, 1 - c], land[a].at[s], (x, y, 1 - c)) for a in range(len(views)) for s in range(4)]

    shapes = [jax.ShapeDtypeStruct((4,) + v.shape[2:], v.dtype) for v in views]
    return _bulk_start("swap_start_" + tag, views, shapes, 4 * len(views), copies, after)


def _scatter_start_after_swap(swapped, after, tag):
    send, recv, views, lands, _ = swapped

    def waits(src, land):
        c = lax.axis_index("c")
        return [(src[a].at[s, 1 - c], land[a].at[s]) for a in range(len(views)) for s in range(4)]

    views, others = _bulk_wait("swap_wait_" + tag, send, recv, views, lands, after, waits)
    mine = _add_halves(views, others, tag)
    return _exchange_start(mine, mine[-1], "exchange_start_" + tag)


def _join_start(halves, after, tag):
    def copies(src, land):
        x, y, c = _place()
        return [(src[a], land[a], (x, y, 1 - c)) for a in range(len(halves))]

    return _bulk_start("join_start_" + tag, halves, [jax.ShapeDtypeStruct(h.shape, h.dtype) for h in halves], len(halves), copies, after)


def _join_wait(started, after, tag):
    send, recv, halves, lands, _ = started
    halves, others = _bulk_wait("join_wait_" + tag, send, recv, halves, lands, after, lambda src, land: list(zip(src, land)))
    return list(zip(halves, others))


def _scatter_sums(started, after, tag):
    return _sum_chips(_exchange_finish(started, after, "exchange_wait_" + tag), tag)


def _scatter_finish(started, after, tag):
    return _join_halves(_scatter_sums(started, after, tag), "join_halves_" + tag)


def _t_bf16(w):
    return w.T.astype(BF16)


def kernel(x, c, ctx, c_ctx, w_ada, b_ada, norm1_g, w_in, mla_q_norm_g, w_q_up, mla_kv_norm_g, w_kv_up, gqa_q_norm_g, gqa_k_norm_g, w_br_a, w_br_b, w_out, norm2_g, w_up, conv_w, conv_b, w_down, final_norm_g, loss_target, m_c_ctx, m_w_ada, m_b_ada, m_norm1_g, m_w_in, m_mla_q_norm_g, m_w_q_up, m_mla_kv_norm_g, m_w_kv_up, m_gqa_q_norm_g, m_gqa_k_norm_g, m_w_br_a, m_w_br_b, m_w_out, m_norm2_g, m_w_up, m_conv_w, m_conv_b, m_w_down, m_final_norm_g, v_c_ctx, v_w_ada, v_b_ada, v_norm1_g, v_w_in, v_mla_q_norm_g, v_w_q_up, v_mla_kv_norm_g, v_w_kv_up, v_gqa_q_norm_g, v_gqa_k_norm_g, v_w_br_a, v_w_br_b, v_w_out, v_norm2_g, v_w_up, v_conv_w, v_conv_b, v_w_down, v_final_norm_g):
    T, D = x.shape[1], x.shape[2]
    C = ctx.shape[1]
    NA = w_ada.shape[2]
    NW = w_up.shape[2]
    F2 = 4 * NW
    FF = F2 // 2
    xi, yi, ci = _place()
    j = 2 * xi + yi
    me = 4 * xi + 2 * yi + ci
    tr = _pick(C, 128, 8)
    tq = _pick(T, 256)

    x2d, tgt, ctx2d = x[0], loss_target[0], ctx[0]
    fg = final_norm_g.reshape(1, D)
    cc = c_ctx.reshape(1, D)

    halve = lambda s: s.reshape(2, s.shape[0] // 2, s.shape[1])
    win_shard = halve(_t_bf16(w_in[0]))
    w0 = max(D, NW)
    pay = jnp.zeros((8, w0), F32).at[0:1, :D].set(c).at[1:4, :NW].set(conv_w[0])
    got = _all_gather_small(pay, "gather_cond")
    c_all = got[:, 0, :D]
    cw = jnp.concatenate([got[2 * s, 1:4, :NW] for s in range(4)], axis=1)
    s16 = jnp.concatenate([c_all, cc, jnp.zeros((7, D), F32)], axis=0)
    b_cols = lax.dynamic_slice(b_ada, (0, j * NA), (1, NA))
    ada_part = _mm(s16, w_ada[0], "NN", F32, "ada_fwd", act="silu", bias=b_cols)
    got = _all_gather_small(ada_part, "gather_ada")
    ada = jnp.concatenate([got[2 * s] for s in range(4)], axis=1)
    lat = lax.dynamic_slice(ada, (me, 0), (1, 6 * D))
    sh1, sc1, g1, sh2, sc2, g2 = [lat[:, k * D : (k + 1) * D] for k in range(6)]
    csh, csc = ada[8:9, :D], ada[8:9, D : 2 * D]

    ag_in = _gather_start([win_shard], got, "gather_start_in")
    t_in = ag_in[4]
    wq3 = (w_q_up[0] + t_in).reshape(MLA_Q_LORA, 2, MLA_NOPE + MLA_ROPE)
    wq_perm = jnp.concatenate([wq3[:, :, :MLA_NOPE].reshape(MLA_Q_LORA, -1), wq3[:, :, MLA_NOPE:].reshape(MLA_Q_LORA, -1)], axis=1)
    low = [_t_bf16(wq_perm), _t_bf16(w_kv_up[0] + t_in)]
    br = [_t_bf16(w_br_a[0] + t_in), _t_bf16(w_br_b[0] + t_in), (w_out[0] + t_in).astype(BF16)]
    ag_low = _gather_start([halve(s) for s in low], t_in, "gather_start_low")
    ag_br = _gather_start([halve(s) for s in br], ag_low[4], "gather_start_br")
    ag_up = _gather_start([halve(_t_bf16(w_up[0] + t_in))], ag_br[4], "gather_start_up")
    ag_down = _gather_start([halve((w_down[0] + t_in).astype(BF16))], ag_up[4], "gather_start_down")
    sh1 = sh1 + ag_down[4]

    cos_a, ss_a = _rope_tables(C, T, MLA_ROPE)
    cos_b, ss_b = _rope_tables(C, T, GQA_HEAD_DIM)
    lcos_a, lss_a, lcos_b, lss_b = cos_a[:T], ss_a[:T], cos_b[:T], ss_b[:T]

    z_all = _norm_mod_fwd(x2d, norm1_g, sh1, sc1, "norm1_lat_fwd", tr, out_rows=T + C)
    z_all = _norm_mod_fwd(ctx2d, norm1_g, csh, csc, "norm1_ctx_fwd", tr, base=z_all, out_off=T)
    (win_t,) = _gather_finish(ag_in, z_all, "in")
    kv_cols = KVP - LANES + MLA_ROPE
    e_kpe = MLA_KV_LORA + MLA_ROPE
    w_kvp = jnp.concatenate([win_t[:MLA_KV_LORA], win_t[e_kpe:kv_cols], win_t[MLA_KV_LORA:e_kpe], jnp.zeros((LANES - MLA_ROPE, D), BF16)], axis=0)

    pkv = _mm(z_all, w_kvp, "NT", F32, "proj_kv")
    pq = _mm(z_all, win_t, "NT", F32, "proj_q", m=T, n=QC, b_off=kv_cols)
    low_landed = _gather_land(ag_low, pq, "low")
    pg = _mm(z_all, win_t, "NT", BF16, "proj_g", m=T, n=2 * D, b_off=kv_cols + QC, after=low_landed[1][4])
    wq_t, wkv_t = _gather_done(low_landed, pg, "low")
    ckv_n, kb2, vb2, kpe2 = _kprep_fwd(pkv, mla_kv_norm_g, gqa_k_norm_g, cos_a, ss_a, cos_b, ss_b, tr)
    kv_up = _mm(ckv_n, wkv_t, "NT", BF16, "kv_up")
    cq_n, qb2 = _qprep_fwd(pq, mla_q_norm_g, gqa_q_norm_g, lcos_b, lss_b, tr)
    q_a = _mm(cq_n, wq_t, "NT", F32, "q_up")
    qar = _qrope_fwd(q_a, lcos_a, lss_a, tr)

    a_q = [(qar, lambda h: 3 * (h // 2) + h % 2), (qar, lambda h: 3 * (h // 2) + 2)]
    a_k = [(kv_up, lambda h: 2 * h), (kpe2, lambda h: h % 2)]
    a_v = (kv_up, lambda h: 2 * h + 1)
    a_scale = float(MLA_NOPE + MLA_ROPE) ** -0.5
    b_q = [(qb2, lambda h: h)]
    b_k = [(kb2, lambda h: h)]
    b_v = (vb2, lambda h: h)
    b_scale = float(GQA_HEAD_DIM) ** -0.5
    tq_f = _pick(T, 512)
    o_a, lse_a = _attn_fwd(a_q, a_k, a_v, MLA_HEADS, 1, MLA_V, a_scale, "attn_a_fwd", tq_f)
    br_landed = _gather_land(ag_br, o_a, "br")
    o_b, lse_b = _attn_fwd(b_q, b_k, b_v, GQA_HEADS, GQA_GROUP, GQA_HEAD_DIM, b_scale, "attn_b_fwd", tq_f, after=br_landed[1][4])
    wbra_t, wbrb_t, wout = _gather_done(br_landed, o_b, "br")
    up_landed = _gather_land(ag_up, o_b, "up")
    ya = _mm(o_a, wbra_t, "NT", BF16, "br_a", after=up_landed[1][4])
    yb = _mm(o_b, wbrb_t, "NT", BF16, "br_b")
    merged = _gates_fwd(pg, ya, yb, tr)
    att = _mm(merged, wout, "NN", F32, "out_proj")
    x1, z2 = _resid_norm2_fwd(x2d, att, g1, norm2_g, sh2, sc2, tr)
    (wup_t,) = _gather_done(up_landed, z2, "up")
    down_landed = _gather_land(ag_down, z2, "down")
    u = _mm(z2, wup_t, "NT", BF16, "ffn_up", after=down_landed[1][4])
    tc = _pick(FF, 128)
    hg = _conv_fwd(u, cw, conv_b, tc)
    (wdown,) = _gather_done(down_landed, hg, "down")
    f = _mm(hg, wdown, "NN", F32, "ffn_down", tk=FF // 2)
    sq, dx2, d_fg, d_g2, df = _loss_head(x1, f, g2, fg, tgt, tr)
    loss = lax.psum(0.5 * jnp.sum(sq) / D, ("x", "y", "c"))

    dhg = _mm(df, wdown, "NT", BF16, "ffn_down_dx")
    g_wdown = _mm(hg, df, "TN", BF16, "ffn_down_dw")
    du_a, du_b, dcw_a, dcw_b, dcb_a, dcb_b = _conv_bwd(u, dhg, cw, conv_b, tc)
    dz2 = _mm(du_a, wup_t, "NN", F32, "ffn_up_dx_a", tk=FF // 2)
    dz2 = _mm(du_b, wup_t, "NN", F32, "ffn_up_dx_b", b_off=FF, add=dz2, tk=FF // 2)
    g_wup_t = _mm(du_a, z2, "TN", BF16, "ffn_up_dw_a", out_rows=F2, tm=FF // 4)
    g_wup_t = _mm(du_b, z2, "TN", BF16, "ffn_up_dw_b", out_base=g_wup_t, out_off=FF, tm=FF // 4)
    sw_ffn = _swap_start([g_wdown, g_wup_t], sc2, "ffn")
    sc2 = sc2 + sw_ffn[4]
    dx1, datt, d_n2g, d_sh2, d_sc2, d_g1 = _resid_norm2_bwd(dz2, x1, dx2, att, norm2_g, sc2, g1, tr)

    dmerged = _mm(datt, wout, "NT", BF16, "out_proj_dx")
    rs_ffn = _scatter_start_after_swap(sw_ffn, dmerged, "ffn")
    lse_a = lse_a + rs_ffn[4]
    g_wout = _mm(merged, datt, "TN", BF16, "out_proj_dw")
    dya, dyb, dpg = _gates_bwd(dmerged, pg, ya, yb, tr)
    do_a = _mm(dya, wbra_t, "NN", BF16, "br_a_dx")
    g_wbra_t = _mm(dya, o_a, "TN", BF16, "br_a_dw")
    do_b = _mm(dyb, wbrb_t, "NN", BF16, "br_b_dx")
    g_wbrb_t = _mm(dyb, o_b, "TN", BF16, "br_b_dw")
    dqa2, dka2, dva2 = _attn_bwd(a_q, a_k, a_v, o_a, do_a, lse_a, MLA_HEADS, 1, MLA_V, a_scale, "attn_a_bwd", tq_f)
    dqb2, dkb2, dvb2 = _attn_bwd(b_q, b_k, b_v, o_b, do_b, lse_b, GQA_HEADS, GQA_GROUP, GQA_HEAD_DIM, b_scale, "attn_b_bwd", tq_f)
    dq_a = _qrope_bwd(dqa2, lcos_a, lss_a, tr)
    dcq_n = _mm(dq_a, wq_t, "NN", F32, "q_up_dx")
    g_wq_t = _mm(dq_a, cq_n, "TN", BF16, "q_up_dw")
    dpq, d_qg, d_gq = _qprep_bwd(pq, dcq_n, dqb2, mla_q_norm_g, gqa_q_norm_g, lcos_b, lss_b, tr)
    dkv_up, dkpe = _kgrad_split(dka2, dva2, cos_a, ss_a, tr)
    dckv_n = _mm(dkv_up, wkv_t, "NN", F32, "kv_up_dx")
    g_wkv_t = _mm(dkv_up, ckv_n, "TN", BF16, "kv_up_dw")
    rs_mix = _scatter_start([g_wq_t, g_wkv_t, g_wbra_t, g_wbrb_t, g_wout], "mix")
    dpkv, d_kvg, d_kg = _kprep_bwd(pkv, dckv_n, dkb2, dvb2, dkpe, mla_kv_norm_g + rs_mix[4], gqa_k_norm_g, cos_b, ss_b, tr)
    dz_kv = _mm(dpkv, w_kvp, "NN", F32, "proj_kv_dx")
    dz_lat = _mm(dpq, win_t, "NN", F32, "proj_q_dx", b_off=kv_cols, add=dz_kv)
    dz_lat = _mm(dpg, win_t, "NN", F32, "proj_g_dx", b_off=kv_cols + QC, add=dz_lat)
    _, d_n1g_c, d_csh, d_csc = _norm_mod_bwd(dz_kv, T // tr, ctx2d, norm1_g, csc, None, "norm1_ctx_bwd", tr)
    grad_x, d_n1g_l, d_sh1, d_sc1 = _norm_mod_bwd(dz_lat, 0, x2d, norm1_g, sc1, dx1, "norm1_lat_bwd", tr)

    zeros_d = jnp.zeros((1, D), F32)
    d_lat = jnp.concatenate([d_sh1, d_sc1, d_g1, d_sh2, d_sc2, d_g2], axis=1)
    d_ctx_part = jnp.concatenate([d_csh, d_csc], axis=1)
    flat = jnp.concatenate(
        [d_n1g_c + d_n1g_l, d_qg, d_kvg, d_gq, d_kg, d_n2g, dcb_a, dcb_b, d_fg,
         dcw_a.reshape(1, -1), dcw_b.reshape(1, -1), d_ctx_part, d_lat], axis=1)
    n_flat = flat.shape[1]
    n_rows = -(-n_flat // (8 * LANES)) * 8
    flat = jnp.pad(flat, ((0, 0), (0, n_rows * LANES - n_flat))).reshape(n_rows, LANES)
    got = _all_gather_small(flat, "gather_small_grads")
    tot = _sum_slots(got, "sum_small_grads").reshape(1, -1)
    sizes = [D, MLA_Q_LORA, MLA_KV_LORA, GQA_HEAD_DIM, GQA_HEAD_DIM, D, F2, D, 3 * FF, 3 * FF, 2 * D]
    offs = [0]
    for s in sizes:
        offs.append(offs[-1] + s)
    t_n1g, t_qg, t_kvg, t_gq, t_kg, t_n2g, t_cb, t_fg, t_cwa, t_cwb, t_ctx = [tot[:, offs[k] : offs[k + 1]] for k in range(len(sizes))]
    g_cw_full = jnp.concatenate([t_cwa.reshape(3, FF), t_cwb.reshape(3, FF)], axis=1)
    g_cw = lax.dynamic_slice(g_cw_full, (0, j * NW), (3, NW))
    d_lat_all = got.reshape(8, -1)[:, offs[-1] : offs[-1] + 6 * D]
    g16 = jnp.concatenate([d_lat_all, jnp.pad(t_ctx, ((0, 0), (0, 4 * D))), jnp.zeros((7, 6 * D), F32)], axis=0)
    g_b_ada = _sum_slots(g16.reshape(16, 1, 6 * D), "sum_b_ada")
    g16_cols = lax.dynamic_slice(g16, (0, j * NA), (16, NA))
    ds_part = _mm(g16_cols, w_ada[0], "NT", F32, "ada_dx")
    got = _all_gather_small(ds_part[8:16], "gather_ada_dx")
    ds_ctx = _sum_slots(jnp.stack([got[2 * s] for s in range(4)]), "sum_ada_dx")[0:1]
    g_c_ctx = _silu_grad_mul(ds_ctx, cc)

    g_kvp = _mm(dpkv, z_all, "TN", BF16, "proj_kv_dw")
    nk = MLA_KV_LORA + 2 * GQA_KV_HEADS * GQA_HEAD_DIM
    g_kv = jnp.concatenate([g_kvp[:MLA_KV_LORA], g_kvp[nk : nk + MLA_ROPE], g_kvp[MLA_KV_LORA:nk]], axis=0)
    g_win_t = _mm(dpq, z_all, "TN", BF16, "proj_q_dw", out_rows=kv_cols + QC + 2 * D, out_off=kv_cols, tm=QC // 2)
    g_win_t = _mm(dpg, z_all, "TN", BF16, "proj_g_dw", out_base=g_win_t, out_off=kv_cols + QC)
    g_win_t = lax.dynamic_update_slice(g_win_t, g_kv, (0, 0))
    rs_in = _scatter_start([g_win_t], "in", after=got)

    h_ffn = _scatter_sums(rs_ffn, rs_in[2][0], "ffn")
    j_ffn = _join_start(h_ffn, grad_x, "ffn")
    h_mix = _scatter_sums(rs_mix, j_ffn[2][0], "mix")
    j_mix = _join_start(h_mix, j_ffn[2][0], "mix")
    g_w_ada = _mm(s16, g16_cols, "TN", F32, "ada_dw", act="silu", after=j_mix[4])
    _, d_ada, m_ada, v_ada = _adamw(w_ada[0], g_w_ada, m_w_ada[0], v_w_ada[0], "adamw_w_ada")
    r_wdown, r_wup = _join_wait(j_ffn, d_ada, "ffn")
    r_wq, r_wkv, r_wbra, r_wbrb, r_wout = _join_wait(j_mix, d_ada, "mix")
    gq_p = _joined(*r_wq).T
    gq = jnp.concatenate([gq_p[:, : 2 * MLA_NOPE].reshape(MLA_Q_LORA, 2, MLA_NOPE), gq_p[:, 2 * MLA_NOPE :].reshape(MLA_Q_LORA, 2, MLA_ROPE)], axis=2)
    grads = {
        "c_ctx": g_c_ctx.reshape(D), "w_ada": g_w_ada[None], "b_ada": g_b_ada, "norm1_g": t_n1g,
        "mla_q_norm_g": t_qg, "w_q_up": gq.reshape(1, MLA_Q_LORA, -1), "mla_kv_norm_g": t_kvg, "w_kv_up": r_wkv,
        "gqa_q_norm_g": t_gq, "gqa_k_norm_g": t_kg, "w_br_a": r_wbra, "w_br_b": r_wbrb, "w_out": r_wout,
        "norm2_g": t_n2g, "w_up": r_wup, "conv_w": g_cw[None], "conv_b": t_cb, "w_down": r_wdown,
        "final_norm_g": t_fg.reshape(D),
    }
    arrives_transposed = ("w_kv_up", "w_br_a", "w_br_b", "w_up")
    arrives_halved = arrives_transposed + ("w_out", "w_down")
    weights = dict(c_ctx=c_ctx, w_ada=w_ada, b_ada=b_ada, norm1_g=norm1_g, w_in=w_in, mla_q_norm_g=mla_q_norm_g, w_q_up=w_q_up,
                   mla_kv_norm_g=mla_kv_norm_g, w_kv_up=w_kv_up, gqa_q_norm_g=gqa_q_norm_g, gqa_k_norm_g=gqa_k_norm_g, w_br_a=w_br_a,
                   w_br_b=w_br_b, w_out=w_out, norm2_g=norm2_g, w_up=w_up, conv_w=conv_w, conv_b=conv_b, w_down=w_down,
                   final_norm_g=final_norm_g)
    m_in = dict(c_ctx=m_c_ctx, w_ada=m_w_ada, b_ada=m_b_ada, norm1_g=m_norm1_g, w_in=m_w_in, mla_q_norm_g=m_mla_q_norm_g,
                w_q_up=m_w_q_up, mla_kv_norm_g=m_mla_kv_norm_g, w_kv_up=m_w_kv_up, gqa_q_norm_g=m_gqa_q_norm_g,
                gqa_k_norm_g=m_gqa_k_norm_g, w_br_a=m_w_br_a, w_br_b=m_w_br_b, w_out=m_w_out, norm2_g=m_norm2_g, w_up=m_w_up,
                conv_w=m_conv_w, conv_b=m_conv_b, w_down=m_w_down, final_norm_g=m_final_norm_g)
    v_in = dict(c_ctx=v_c_ctx, w_ada=v_w_ada, b_ada=v_b_ada, norm1_g=v_norm1_g, w_in=v_w_in, mla_q_norm_g=v_mla_q_norm_g,
                w_q_up=v_w_q_up, mla_kv_norm_g=v_mla_kv_norm_g, w_kv_up=v_w_kv_up, gqa_q_norm_g=v_gqa_q_norm_g,
                gqa_k_norm_g=v_gqa_k_norm_g, w_br_a=v_w_br_a, w_br_b=v_w_br_b, w_out=v_w_out, norm2_g=v_norm2_g, w_up=v_w_up,
                conv_w=v_conv_w, conv_b=v_conv_b, w_down=v_w_down, final_norm_g=v_final_norm_g)
    names = list(weights)
    big = [n for n in names if weights[n].ndim == 3 and weights[n].shape[1] >= 8]
    small = [n for n in names if n not in big]
    delta, new_m, new_v = {}, {}, {}

    def update(n):
        shp = weights[n].shape
        two_d = lambda a: a.reshape(shp[1], shp[2])
        g_t = n in arrives_transposed
        if n in arrives_halved:
            g_in, g_sib = grads[n]
        else:
            g_in, g_sib = two_d(grads[n].astype(F32)), None
        g_, d_, m_, v_ = _adamw(two_d(weights[n]), g_in, two_d(m_in[n]), two_d(v_in[n]), "adamw_" + n, g_transposed=g_t, g_sibling=g_sib)
        grads[n], delta[n], new_m[n], new_v[n] = g_.reshape(shp), d_.reshape(shp), m_.reshape(shp), v_.reshape(shp)

    delta["w_ada"], new_m["w_ada"], new_v["w_ada"] = d_ada[None], m_ada[None], v_ada[None]
    early = [n for n in big if n not in ("w_in", "w_ada")]
    for n in early:
        update(n)
    done = sum(delta[n][0, 0:1, 0:1] for n in early)
    ((g_mine, g_sib),) = _scatter_finish(rs_in, done, "in")
    g_, d_, m_, v_ = _adamw(w_in[0].T, g_mine, m_w_in[0].T, v_w_in[0].T, "adamw_w_in", g_sibling=g_sib)
    grads["w_in"], delta["w_in"], new_m["w_in"], new_v["w_in"] = g_.T[None], d_.T[None], m_.T[None], v_.T[None]
    grads = {n: grads[n].reshape(weights[n].shape).astype(F32) for n in names}

    slab = lambda tree: [tree[n].reshape(-1, LANES) for n in small]
    d_, m_, v_ = _adamw_many(slab(weights), slab(grads), slab(m_in), slab(v_in), "adamw_small")
    for k, n in enumerate(small):
        shp = weights[n].shape
        delta[n], new_m[n], new_v[n] = d_[k].reshape(shp), m_[k].reshape(shp), v_[k].reshape(shp)

    return (loss, grad_x[None], *[grads[n] for n in names], *[delta[n] for n in names], *[new_m[n] for n in names],
            *[new_v[n] for n in names])
```

```python
import math

import jax
import jax.numpy as jnp
from jax import lax
from jax.experimental import pallas as pl
from jax.experimental.pallas import tpu as pltpu

F32 = jnp.float32
BF16 = jnp.bfloat16
MESH = pl.DeviceIdType.MESH

NORM_EPS = 1e-6
ROPE_THETA = 10000.0
GRID_W = 64
MLA_HEADS = 8
MLA_Q_LORA = 768
MLA_KV_LORA = 512
MLA_NOPE = 128
MLA_ROPE = 64
MLA_V = 128
GQA_HEADS = 8
GQA_KV_HEADS = 2
GQA_HEAD_DIM = 128
GQA_GROUP = GQA_HEADS // GQA_KV_HEADS
LANES = 128
KVP = MLA_KV_LORA + 2 * GQA_KV_HEADS * GQA_HEAD_DIM + LANES
QC = MLA_Q_LORA + GQA_HEADS * GQA_HEAD_DIM

ADAM_LR = 0.001
ADAM_B1 = 0.9
ADAM_B2 = 0.999
ADAM_EPS = 1e-08
ADAM_WD = 0.01
ADAM_STEP = 10

VMEM_LIMIT = 56 * 1024 * 1024


def _pick(dim, target, mult=LANES):
    t = (min(target, dim) // mult) * mult
    while t >= mult:
        if dim % t == 0:
            return t
        t -= mult
    return dim


def _params(sem):
    return pltpu.CompilerParams(dimension_semantics=sem, vmem_limit_bytes=VMEM_LIMIT)


_DIMS = {"NN": (((1,), (0,)), ((), ())), "NT": (((1,), (1,)), ((), ())), "TN": (((0,), (0,)), ((), ()))}


MM_VMEM_BUDGET = 36 * 1024 * 1024


def _mm_tiles(M, N, K, sa, sb, so, tm, tn, tk):
    tm, tn, tk = _pick(M, tm), _pick(N, tn), _pick(K, tk)

    def need(t):
        return 2 * (tm * t * sa + t * tn * sb) + 2 * tm * tn * so + (tm * tn * 4 if t < K else 0)

    while need(tk) > MM_VMEM_BUDGET and tk > LANES:
        smaller = _pick(K, tk - LANES)
        if smaller >= tk:
            break
        tk = smaller
    return tm, tn, tk


def _window(block, index, offsets):
    if not any(offsets):
        return pl.BlockSpec(block, index)
    for t, o in zip(block, offsets):
        assert o % 16 == 0 and t % 16 == 0, (block, offsets)

    def at(i, j, k):
        return tuple(pl.multiple_of(o + p * t, math.gcd(o, t)) for p, t, o in zip(index(i, j, k), block, offsets))

    return pl.BlockSpec(tuple(pl.Element(t) for t in block), at)


def _mm(a, b, mode, out_dtype, name, m=None, n=None, k=None, b_off=0, add=None, out_rows=None, out_base=None, out_off=0,
        tm=1024, tn=1024, tk=2304, act=None, bias=None, after=None):
    if mode == "NN":
        M, K, N = m or a.shape[0], k or a.shape[1], b.shape[1]
    elif mode == "NT":
        M, K, N = m or a.shape[0], a.shape[1], n or b.shape[0]
    else:
        M, K, N = a.shape[1], k or a.shape[0], b.shape[1]
    tm, tn, tk = _mm_tiles(M, N, K, a.dtype.itemsize, b.dtype.itemsize, jnp.dtype(out_dtype).itemsize, tm, tn, tk)
    nk = K // tk
    dims = _DIMS[mode]
    n_in = 2 + (bias is not None) + (add is not None) + (out_base is not None) + (after is not None)

    def body(*refs):
        a_ref, b_ref = refs[:2]
        bias_ref = refs[2] if bias is not None else None
        add_ref = refs[2 + (bias is not None)] if add is not None else None
        o_ref = refs[n_in]
        av = a_ref[...]
        if act == "silu":
            av = av * jax.nn.sigmoid(av)
        part = lax.dot_general(av.astype(BF16), b_ref[...].astype(BF16), dims, preferred_element_type=F32)

        def finish(r):
            if bias is not None:
                r = r + bias_ref[...]
            if add is not None:
                r = r + add_ref[...]
            o_ref[...] = r.astype(out_dtype)

        if nk == 1:
            finish(part)
            return
        acc = refs[-1]
        k = pl.program_id(2)

        @pl.when(k == 0)
        def _():
            acc[...] = part

        @pl.when(jnp.logical_and(k > 0, k < nk - 1))
        def _():
            acc[...] += part

        @pl.when(k == nk - 1)
        def _():
            finish(acc[...] + part)

    a_spec = pl.BlockSpec((tk, tm), lambda i, j, k: (k, i)) if mode == "TN" else pl.BlockSpec((tm, tk), lambda i, j, k: (i, k))
    if mode == "NT":
        b_spec = _window((tn, tk), lambda i, j, k: (j, k), (b_off, 0))
    else:
        b_spec = _window((tk, tn), lambda i, j, k: (k, j), (b_off, 0))
    in_specs, args = [a_spec, b_spec], [a, b]
    if bias is not None:
        in_specs.append(pl.BlockSpec((1, tn), lambda i, j, k: (0, j)))
        args.append(bias)
    if add is not None:
        in_specs.append(pl.BlockSpec((tm, tn), lambda i, j, k: (i, j)))
        args.append(add)
    aliases = {}
    if after is not None:
        in_specs.append(pl.BlockSpec(after.shape, lambda i, j, k: (0, 0)))
        args.append(after)
    if out_base is not None:
        aliases = {len(args): 0}
        in_specs.append(ANY)
        args.append(out_base)
        out_rows = out_base.shape[0]
    return pl.pallas_call(
        body,
        name=name,
        grid=(M // tm, N // tn, nk),
        in_specs=in_specs,
        out_specs=_window((tm, tn), lambda i, j, k: (i, j), (out_off, 0)),
        out_shape=jax.ShapeDtypeStruct((out_rows or M, N), out_dtype),
        input_output_aliases=aliases,
        scratch_shapes=[pltpu.VMEM((tm, tn), F32)] if nk > 1 else [],
        compiler_params=_params(("parallel", "parallel", "arbitrary")),
    )(*args)


def _rms(x):
    r = lax.rsqrt(jnp.mean(x * x, axis=-1, keepdims=True) + NORM_EPS)
    return x * r, r


def _rms_bwd(xh, r, dxh):
    return r * (dxh - xh * jnp.mean(dxh * xh, axis=-1, keepdims=True))


def _swap(x, q):
    lane = lax.broadcasted_iota(jnp.int32, x.shape, 1)
    even = ((lane // q) % 2) == 0
    return jnp.where(even, pltpu.roll(x, LANES - q, 1), pltpu.roll(x, q, 1))


def _rope(x, cos, ss, q):
    return x * cos + _swap(x, q) * ss


def _rope_t(d, cos, ss, q):
    return d * cos + _swap(d * ss, q)


def _csum(x):
    return jnp.sum(x, axis=0, keepdims=True)


def _rows(tr, w, off=0):
    return pl.BlockSpec((tr, w), lambda i: (i + off, 0))


def _bcast(w):
    return pl.BlockSpec((1, w), lambda i: (0, 0))


def _acc_init(i, refs):
    @pl.when(i == 0)
    def _():
        for r in refs:
            r[...] = jnp.zeros_like(r)


def _rope_tables(n_ctx, n_lat, rot_dim):
    rows = n_lat // GRID_W
    row = jnp.repeat(jnp.arange(rows, dtype=F32), GRID_W)
    col = jnp.tile(jnp.arange(GRID_W, dtype=F32), rows)
    half = rot_dim // 2
    inv_freq = ROPE_THETA ** (-jnp.arange(0, half, 2, dtype=F32) / half)
    ar, ac = row[:, None] * inv_freq, col[:, None] * inv_freq
    cos = jnp.concatenate([jnp.cos(ar), jnp.cos(ar), jnp.cos(ac), jnp.cos(ac)], axis=-1)
    ss = jnp.concatenate([-jnp.sin(ar), jnp.sin(ar), -jnp.sin(ac), jnp.sin(ac)], axis=-1)
    cos = jnp.tile(cos, (1, LANES // rot_dim))
    ss = jnp.tile(ss, (1, LANES // rot_dim))
    cos = jnp.concatenate([cos, jnp.ones((n_ctx, LANES), F32)], axis=0)
    ss = jnp.concatenate([ss, jnp.zeros((n_ctx, LANES), F32)], axis=0)
    return cos, ss


def _norm_mod_fwd(x2d, g, sh, sc, name, tr, out_rows=None, base=None, out_off=0):
    n, d = x2d.shape

    def body(x_ref, g_ref, sh_ref, sc_ref, *rest):
        xh, _ = _rms(x_ref[...])
        rest[-1][...] = ((xh * g_ref[...]) * (1.0 + sc_ref[...]) + sh_ref[...]).astype(BF16)

    args, in_specs, aliases = [x2d, g, sh, sc], [_rows(tr, d), _bcast(d), _bcast(d), _bcast(d)], {}
    if base is not None:
        args.append(base)
        in_specs.append(ANY)
        aliases = {4: 0}
        out_rows = base.shape[0]
    return pl.pallas_call(
        body,
        name=name,
        grid=(n // tr,),
        in_specs=in_specs,
        out_specs=_rows(tr, d, out_off // tr),
        out_shape=jax.ShapeDtypeStruct((out_rows or n, d), BF16),
        input_output_aliases=aliases,
        compiler_params=_params(("parallel",)),
    )(*args)


def _norm_mod_bwd(dz, dz_off, x2d, g, sc, dres, name, tr):
    n, d = x2d.shape
    want_dx = dres is not None

    def body(*refs):
        if want_dx:
            dz_ref, x_ref, g_ref, sc_ref, dres_ref, dx_ref, dg_ref, dsh_ref, dsc_ref = refs
        else:
            dz_ref, x_ref, g_ref, sc_ref, dg_ref, dsh_ref, dsc_ref = refs
        _acc_init(pl.program_id(0), [dg_ref, dsh_ref, dsc_ref])
        xh, r = _rms(x_ref[...])
        dzv = dz_ref[...]
        gv = g_ref[...]
        dsc_ref[...] += _csum(dzv * (xh * gv))
        dsh_ref[...] += _csum(dzv)
        dh = dzv * (1.0 + sc_ref[...])
        dg_ref[...] += _csum(dh * xh)
        if want_dx:
            dx_ref[...] = _rms_bwd(xh, r, dh * gv) + dres_ref[...]

    in_specs = [_rows(tr, d, dz_off), _rows(tr, d), _bcast(d), _bcast(d)]
    args = [dz, x2d, g, sc]
    out_specs = [_bcast(d)] * 3
    out_shape = [jax.ShapeDtypeStruct((1, d), F32)] * 3
    if want_dx:
        in_specs.append(_rows(tr, d))
        args.append(dres)
        out_specs = [_rows(tr, d)] + out_specs
        out_shape = [jax.ShapeDtypeStruct((n, d), F32)] + out_shape
    res = pl.pallas_call(
        body,
        name=name,
        grid=(n // tr,),
        in_specs=in_specs,
        out_specs=out_specs,
        out_shape=out_shape,
        compiler_params=_params(("arbitrary",)),
    )(*args)
    return res if want_dx else (None, *res)


_QA, _QB = MLA_ROPE // 4, GQA_HEAD_DIM // 4


def _kprep_fwd(pkv, kvg, kg, cos_a, ss_a, cos_b, ss_b, tr):
    n = pkv.shape[0]
    nb = GQA_KV_HEADS * GQA_HEAD_DIM

    def body(p_ref, kvg_ref, kg_ref, ca, sa, cb, sb, ckv_ref, kb_ref, vb_ref, kpe_ref):
        p = p_ref[...]
        xh, _ = _rms(p[:, :MLA_KV_LORA])
        ckv_ref[...] = (xh * kvg_ref[...]).astype(BF16)
        for e in range(GQA_KV_HEADS):
            lo = MLA_KV_LORA + e * GQA_HEAD_DIM
            kh, _ = _rms(p[:, lo : lo + GQA_HEAD_DIM])
            kb_ref[:, e * GQA_HEAD_DIM : (e + 1) * GQA_HEAD_DIM] = _rope(kh * kg_ref[...], cb[...], sb[...], _QB).astype(BF16)
        vb_ref[...] = p[:, MLA_KV_LORA + nb : MLA_KV_LORA + 2 * nb].astype(BF16)
        kr = _rope(p[:, MLA_KV_LORA + 2 * nb :], ca[...], sa[...], _QA)
        kpe_ref[:, :LANES] = kr.astype(BF16)
        kpe_ref[:, LANES:] = pltpu.roll(kr, MLA_ROPE, 1).astype(BF16)

    return pl.pallas_call(
        body,
        name="kprep_fwd",
        grid=(n // tr,),
        in_specs=[_rows(tr, KVP), _bcast(MLA_KV_LORA), _bcast(GQA_HEAD_DIM)] + [_rows(tr, LANES)] * 4,
        out_specs=[_rows(tr, MLA_KV_LORA), _rows(tr, nb), _rows(tr, nb), _rows(tr, 2 * LANES)],
        out_shape=[jax.ShapeDtypeStruct((n, w), BF16) for w in (MLA_KV_LORA, nb, nb, 2 * LANES)],
        compiler_params=_params(("parallel",)),
    )(pkv, kvg, kg, cos_a, ss_a, cos_b, ss_b)


def _kprep_bwd(pkv, dckv, dkb, dvb, dkpe, kvg, kg, cos_b, ss_b, tr):
    n = pkv.shape[0]
    nb = GQA_KV_HEADS * GQA_HEAD_DIM

    def body(p_ref, dckv_ref, dkb_ref, dvb_ref, dkpe_ref, kvg_ref, kg_ref, cb, sb, dp_ref, dkvg_ref, dkg_ref):
        _acc_init(pl.program_id(0), [dkvg_ref, dkg_ref])
        p = p_ref[...]
        xh, r = _rms(p[:, :MLA_KV_LORA])
        dn = dckv_ref[...]
        dkvg_ref[...] += _csum(dn * xh)
        dp_ref[:, :MLA_KV_LORA] = _rms_bwd(xh, r, dn * kvg_ref[...]).astype(BF16)
        for e in range(GQA_KV_HEADS):
            lo = MLA_KV_LORA + e * GQA_HEAD_DIM
            kh, rk = _rms(p[:, lo : lo + GQA_HEAD_DIM])
            dk = _rope_t(dkb_ref[:, e * GQA_HEAD_DIM : (e + 1) * GQA_HEAD_DIM], cb[...], sb[...], _QB)
            dkg_ref[...] += _csum(dk * kh)
            dp_ref[:, lo : lo + GQA_HEAD_DIM] = _rms_bwd(kh, rk, dk * kg_ref[...]).astype(BF16)
        dp_ref[:, MLA_KV_LORA + nb : MLA_KV_LORA + 2 * nb] = dvb_ref[...].astype(BF16)
        dp_ref[:, MLA_KV_LORA + 2 * nb :] = dkpe_ref[...].astype(BF16)

    return pl.pallas_call(
        body,
        name="kprep_bwd",
        grid=(n // tr,),
        in_specs=[_rows(tr, KVP), _rows(tr, MLA_KV_LORA), _rows(tr, nb), _rows(tr, nb), _rows(tr, LANES),
                  _bcast(MLA_KV_LORA), _bcast(GQA_HEAD_DIM), _rows(tr, LANES), _rows(tr, LANES)],
        out_specs=[_rows(tr, KVP), _bcast(MLA_KV_LORA), _bcast(GQA_HEAD_DIM)],
        out_shape=[jax.ShapeDtypeStruct((n, KVP), BF16), jax.ShapeDtypeStruct((1, MLA_KV_LORA), F32),
                   jax.ShapeDtypeStruct((1, GQA_HEAD_DIM), F32)],
        compiler_params=_params(("arbitrary",)),
    )(pkv, dckv, dkb, dvb, dkpe, kvg, kg, cos_b, ss_b)


def _kgrad_split(dka, dva, cos_a, ss_a, tr):
    n = dka.shape[0]
    wk = MLA_HEADS * 2 * LANES

    def body(dk_ref, dv_ref, ca, sa, dkv_ref, dkpe_ref):
        even = jnp.zeros((tr, LANES), F32)
        odd = jnp.zeros((tr, LANES), F32)
        for h in range(MLA_HEADS):
            dkv_ref[:, 2 * h * LANES : (2 * h + 1) * LANES] = dk_ref[:, 2 * h * LANES : (2 * h + 1) * LANES].astype(BF16)
            dkv_ref[:, (2 * h + 1) * LANES : (2 * h + 2) * LANES] = dv_ref[:, h * MLA_V : (h + 1) * MLA_V].astype(BF16)
            part = dk_ref[:, (2 * h + 1) * LANES : (2 * h + 2) * LANES]
            if h % 2 == 0:
                even = even + part
            else:
                odd = odd + part
        lane = lax.broadcasted_iota(jnp.int32, (tr, LANES), 1)
        low = lane < MLA_ROPE
        both = jnp.where(low, even, odd)
        tot = jnp.where(low, both + pltpu.roll(both, MLA_ROPE, 1), 0.0)
        dkpe_ref[...] = _rope_t(tot, ca[...], sa[...], _QA)

    return pl.pallas_call(
        body,
        name="kgrad_split",
        grid=(n // tr,),
        in_specs=[_rows(tr, wk), _rows(tr, MLA_HEADS * MLA_V), _rows(tr, LANES), _rows(tr, LANES)],
        out_specs=[_rows(tr, wk), _rows(tr, LANES)],
        out_shape=[jax.ShapeDtypeStruct((n, wk), BF16), jax.ShapeDtypeStruct((n, LANES), F32)],
        compiler_params=_params(("parallel",)),
    )(dka, dva, cos_a, ss_a)


def _qprep_fwd(pq, qg, gq, cos_b, ss_b, tr):
    n = pq.shape[0]
    nq = GQA_HEADS * GQA_HEAD_DIM

    def body(p_ref, qg_ref, gq_ref, cb, sb, cq_ref, qb_ref):
        xh, _ = _rms(p_ref[:, :MLA_Q_LORA])
        cq_ref[...] = (xh * qg_ref[...]).astype(BF16)
        for h in range(GQA_HEADS):
            lo = MLA_Q_LORA + h * GQA_HEAD_DIM
            qh, _ = _rms(p_ref[:, lo : lo + GQA_HEAD_DIM])
            qb_ref[:, h * GQA_HEAD_DIM : (h + 1) * GQA_HEAD_DIM] = _rope(qh * gq_ref[...], cb[...], sb[...], _QB).astype(BF16)

    return pl.pallas_call(
        body,
        name="qprep_fwd",
        grid=(n // tr,),
        in_specs=[_rows(tr, QC), _bcast(MLA_Q_LORA), _bcast(GQA_HEAD_DIM), _rows(tr, LANES), _rows(tr, LANES)],
        out_specs=[_rows(tr, MLA_Q_LORA), _rows(tr, nq)],
        out_shape=[jax.ShapeDtypeStruct((n, MLA_Q_LORA), BF16), jax.ShapeDtypeStruct((n, nq), BF16)],
        compiler_params=_params(("parallel",)),
    )(pq, qg, gq, cos_b, ss_b)


def _qprep_bwd(pq, dcq, dqb, qg, gq, cos_b, ss_b, tr):
    n = pq.shape[0]
    nq = GQA_HEADS * GQA_HEAD_DIM

    def body(p_ref, dcq_ref, dqb_ref, qg_ref, gq_ref, cb, sb, dp_ref, dqg_ref, dgq_ref):
        _acc_init(pl.program_id(0), [dqg_ref, dgq_ref])
        xh, r = _rms(p_ref[:, :MLA_Q_LORA])
        dn = dcq_ref[...]
        dqg_ref[...] += _csum(dn * xh)
        dp_ref[:, :MLA_Q_LORA] = _rms_bwd(xh, r, dn * qg_ref[...]).astype(BF16)
        for h in range(GQA_HEADS):
            lo = MLA_Q_LORA + h * GQA_HEAD_DIM
            qh, rq = _rms(p_ref[:, lo : lo + GQA_HEAD_DIM])
            dq = _rope_t(dqb_ref[:, h * GQA_HEAD_DIM : (h + 1) * GQA_HEAD_DIM], cb[...], sb[...], _QB)
            dgq_ref[...] += _csum(dq * qh)
            dp_ref[:, lo : lo + GQA_HEAD_DIM] = _rms_bwd(qh, rq, dq * gq_ref[...]).astype(BF16)

    return pl.pallas_call(
        body,
        name="qprep_bwd",
        grid=(n // tr,),
        in_specs=[_rows(tr, QC), _rows(tr, MLA_Q_LORA), _rows(tr, nq), _bcast(MLA_Q_LORA), _bcast(GQA_HEAD_DIM),
                  _rows(tr, LANES), _rows(tr, LANES)],
        out_specs=[_rows(tr, QC), _bcast(MLA_Q_LORA), _bcast(GQA_HEAD_DIM)],
        out_shape=[jax.ShapeDtypeStruct((n, QC), BF16), jax.ShapeDtypeStruct((1, MLA_Q_LORA), F32),
                   jax.ShapeDtypeStruct((1, GQA_HEAD_DIM), F32)],
        compiler_params=_params(("arbitrary",)),
    )(pq, dcq, dqb, qg, gq, cos_b, ss_b)


_QA_COLS = MLA_HEADS * (MLA_NOPE + MLA_ROPE)


def _qrope_fwd(qa, cos_a, ss_a, tr):
    n = qa.shape[0]

    def body(q_ref, ca, sa, o_ref):
        for j in range(MLA_HEADS // 2):
            lo = 3 * j * LANES
            o_ref[:, lo : lo + 2 * LANES] = q_ref[:, lo : lo + 2 * LANES].astype(BF16)
            o_ref[:, lo + 2 * LANES : lo + 3 * LANES] = _rope(q_ref[:, lo + 2 * LANES : lo + 3 * LANES], ca[...], sa[...], _QA).astype(BF16)

    return pl.pallas_call(
        body,
        name="qrope_fwd",
        grid=(n // tr,),
        in_specs=[_rows(tr, _QA_COLS), _rows(tr, LANES), _rows(tr, LANES)],
        out_specs=_rows(tr, _QA_COLS),
        out_shape=jax.ShapeDtypeStruct((n, _QA_COLS), BF16),
        compiler_params=_params(("parallel",)),
    )(qa, cos_a, ss_a)


def _qrope_bwd(dq2, cos_a, ss_a, tr):
    n = dq2.shape[0]

    def body(d_ref, ca, sa, o_ref):
        for j in range(MLA_HEADS // 2):
            lo = 3 * j * LANES
            h0, h1 = 2 * j, 2 * j + 1
            o_ref[:, lo : lo + LANES] = d_ref[:, 2 * h0 * LANES : (2 * h0 + 1) * LANES].astype(BF16)
            o_ref[:, lo + LANES : lo + 2 * LANES] = d_ref[:, 2 * h1 * LANES : (2 * h1 + 1) * LANES].astype(BF16)
            pe = d_ref[:, (2 * h0 + 1) * LANES : (2 * h0 + 2) * LANES] + d_ref[:, (2 * h1 + 1) * LANES : (2 * h1 + 2) * LANES]
            o_ref[:, lo + 2 * LANES : lo + 3 * LANES] = _rope_t(pe, ca[...], sa[...], _QA).astype(BF16)

    return pl.pallas_call(
        body,
        name="qrope_bwd",
        grid=(n // tr,),
        in_specs=[_rows(tr, MLA_HEADS * 2 * LANES), _rows(tr, LANES), _rows(tr, LANES)],
        out_specs=_rows(tr, _QA_COLS),
        out_shape=jax.ShapeDtypeStruct((n, _QA_COLS), BF16),
        compiler_params=_params(("parallel",)),
    )(dq2, cos_a, ss_a)


def _cat(refs):
    vals = [r[...] for r in refs]
    return vals[0] if len(vals) == 1 else jnp.concatenate(vals, axis=-1)


LOG2E = 1.4426950408889634


def _attn_fwd(qparts, kparts, vpart, n_heads, group, dv, scale, name, tq, after=None):
    T, Tk = qparts[0][0].shape[0], kparts[0][0].shape[0]
    nq_, nk_ = len(qparts), len(kparts)
    sub = min(tq, 256)
    c2 = scale * LOG2E

    def body(*refs):
        q_refs, k_refs = refs[:nq_], refs[nq_ : nq_ + nk_]
        v_ref = refs[nq_ + nk_]
        o_ref, lse_ref = refs[-2:]
        k = _cat(k_refs)
        v = v_ref[...]
        for r0 in range(0, tq, sub):
            q = _cat([r.at[r0 : r0 + sub, :] for r in q_refs])
            s = lax.dot_general(q, k, _DIMS["NT"], preferred_element_type=F32)
            m = jnp.max(s, axis=-1, keepdims=True)
            p = jnp.exp2((s - m) * c2)
            l = jnp.sum(p, axis=-1, keepdims=True)
            acc = jnp.dot(p.astype(BF16), v, preferred_element_type=F32)
            o_ref[r0 : r0 + sub, :] = (acc * (1.0 / l)).astype(BF16)
            lse_ref[r0 : r0 + sub, :] = m * scale + jnp.log(l)

    in_specs = [pl.BlockSpec((tq, LANES), lambda h, i, f=f: (i, f(h))) for _, f in qparts]
    in_specs += [pl.BlockSpec((Tk, LANES), lambda h, i, f=f: (0, f(h // group))) for _, f in kparts]
    fv = vpart[1]
    in_specs.append(pl.BlockSpec((Tk, dv), lambda h, i: (0, fv(h // group))))
    args = [*[a for a, _ in qparts], *[a for a, _ in kparts], vpart[0]]
    if after is not None:
        in_specs.append(pl.BlockSpec(after.shape, lambda h, i: (0, 0)))
        args.append(after)
    return pl.pallas_call(
        body,
        name=name,
        grid=(n_heads, T // tq),
        in_specs=in_specs,
        out_specs=[pl.BlockSpec((tq, dv), lambda h, i: (i, h)), pl.BlockSpec((None, tq, 1), lambda h, i: (h, i, 0))],
        out_shape=[jax.ShapeDtypeStruct((T, n_heads * dv), BF16), jax.ShapeDtypeStruct((n_heads, T, 1), F32)],
        compiler_params=_params(("parallel", "parallel")),
    )(*args)


def _attn_bwd(qparts, kparts, vpart, o, do, lse, n_heads, group, dv, scale, name, tq):
    T, Tk = qparts[0][0].shape[0], kparts[0][0].shape[0]
    nq_, nk_ = len(qparts), len(kparts)
    dk_ = LANES * nq_
    n_kv = n_heads // group
    nblk = T // tq
    c2 = scale * LOG2E

    def head(hk, i):
        return hk * group + i // nblk

    sub = min(tq, 256)

    def body(*refs):
        q_refs = refs[:nq_]
        k = _cat(refs[nq_ : nq_ + nk_])
        v_ref, o_ref, do_ref, lse_ref, dq_ref, dk_ref, dv_ref = refs[nq_ + nk_ :]
        i = pl.program_id(1)
        _acc_init(i, [dk_ref, dv_ref])
        v = v_ref[...]
        dk_acc, dv_acc = None, None
        for r0 in range(0, tq, sub):
            rows = slice(r0, r0 + sub)
            q = _cat([r.at[rows, :] for r in q_refs])
            s = lax.dot_general(q, k, _DIMS["NT"], preferred_element_type=F32)
            p = jnp.exp2(s * c2 - lse_ref[rows, :] * LOG2E)
            dov = do_ref[rows, :]
            dp = lax.dot_general(dov, v, _DIMS["NT"], preferred_element_type=F32)
            delta = jnp.sum(dov.astype(F32) * o_ref[rows, :].astype(F32), axis=-1, keepdims=True)
            ds = (p * (dp - delta)).astype(BF16)
            dq_ref[rows, :] = jnp.dot(ds, k, preferred_element_type=F32) * scale
            dk_part = lax.dot_general(ds, q, _DIMS["TN"], preferred_element_type=F32)
            dv_part = lax.dot_general(p.astype(BF16), dov, _DIMS["TN"], preferred_element_type=F32)
            dk_acc = dk_part if dk_acc is None else dk_acc + dk_part
            dv_acc = dv_part if dv_acc is None else dv_acc + dv_part
        dk_ref[...] += dk_acc
        dv_ref[...] += dv_acc

        @pl.when(i == group * nblk - 1)
        def _():
            dk_ref[...] *= scale

    in_specs = [pl.BlockSpec((tq, LANES), lambda hk, i, f=f: (i % nblk, f(head(hk, i)))) for _, f in qparts]
    in_specs += [pl.BlockSpec((Tk, LANES), lambda hk, i, f=f: (0, f(hk))) for _, f in kparts]
    fv = vpart[1]
    in_specs.append(pl.BlockSpec((Tk, dv), lambda hk, i: (0, fv(hk))))
    in_specs += [pl.BlockSpec((tq, dv), lambda hk, i: (i % nblk, head(hk, i)))] * 2
    in_specs.append(pl.BlockSpec((None, tq, 1), lambda hk, i: (head(hk, i), i % nblk, 0)))
    return pl.pallas_call(
        body,
        name=name,
        grid=(n_kv, group * nblk),
        in_specs=in_specs,
        out_specs=[pl.BlockSpec((tq, dk_), lambda hk, i: (i % nblk, head(hk, i))),
                   pl.BlockSpec((Tk, dk_), lambda hk, i: (0, hk)),
                   pl.BlockSpec((Tk, dv), lambda hk, i: (0, hk))],
        out_shape=[jax.ShapeDtypeStruct((T, n_heads * dk_), F32), jax.ShapeDtypeStruct((Tk, n_kv * dk_), F32),
                   jax.ShapeDtypeStruct((Tk, n_kv * dv), F32)],
        compiler_params=_params(("parallel", "arbitrary")),
    )(*[a for a, _ in qparts], *[a for a, _ in kparts], vpart[0], o, do, lse)


def _gates_fwd(pg, ya, yb, tr):
    n, d = ya.shape

    def body(pg_ref, ya_ref, yb_ref, o_ref):
        ga = jax.nn.sigmoid(pg_ref[:, :d].astype(F32))
        gb = jax.nn.sigmoid(pg_ref[:, d:].astype(F32))
        o_ref[...] = (ga * ya_ref[...].astype(F32) + gb * yb_ref[...].astype(F32)).astype(BF16)

    return pl.pallas_call(
        body,
        name="gates_fwd",
        grid=(n // tr,),
        in_specs=[_rows(tr, 2 * d), _rows(tr, d), _rows(tr, d)],
        out_specs=_rows(tr, d),
        out_shape=jax.ShapeDtypeStruct((n, d), BF16),
        compiler_params=_params(("parallel",)),
    )(pg, ya, yb)


def _gates_bwd(dm, pg, ya, yb, tr):
    n, d = ya.shape

    def body(dm_ref, pg_ref, ya_ref, yb_ref, dya_ref, dyb_ref, dpg_ref):
        dmv = dm_ref[...].astype(F32)
        ga = jax.nn.sigmoid(pg_ref[:, :d].astype(F32))
        gb = jax.nn.sigmoid(pg_ref[:, d:].astype(F32))
        dya_ref[...] = (dmv * ga).astype(BF16)
        dyb_ref[...] = (dmv * gb).astype(BF16)
        dpg_ref[:, :d] = (dmv * ya_ref[...].astype(F32) * ga * (1.0 - ga)).astype(BF16)
        dpg_ref[:, d:] = (dmv * yb_ref[...].astype(F32) * gb * (1.0 - gb)).astype(BF16)

    return pl.pallas_call(
        body,
        name="gates_bwd",
        grid=(n // tr,),
        in_specs=[_rows(tr, d), _rows(tr, 2 * d), _rows(tr, d), _rows(tr, d)],
        out_specs=[_rows(tr, d), _rows(tr, d), _rows(tr, 2 * d)],
        out_shape=[jax.ShapeDtypeStruct((n, d), BF16), jax.ShapeDtypeStruct((n, d), BF16), jax.ShapeDtypeStruct((n, 2 * d), BF16)],
        compiler_params=_params(("parallel",)),
    )(dm, pg, ya, yb)


def _resid_norm2_fwd(x2d, att, g1, n2g, sh2, sc2, tr):
    n, d = x2d.shape

    def body(x_ref, a_ref, g1_ref, g_ref, sh_ref, sc_ref, x1_ref, z_ref):
        x1 = x_ref[...] + g1_ref[...] * a_ref[...]
        x1_ref[...] = x1
        xh, _ = _rms(x1)
        z_ref[...] = ((xh * g_ref[...]) * (1.0 + sc_ref[...]) + sh_ref[...]).astype(BF16)

    return pl.pallas_call(
        body,
        name="resid_norm2_fwd",
        grid=(n // tr,),
        in_specs=[_rows(tr, d), _rows(tr, d)] + [_bcast(d)] * 4,
        out_specs=[_rows(tr, d), _rows(tr, d)],
        out_shape=[jax.ShapeDtypeStruct((n, d), F32), jax.ShapeDtypeStruct((n, d), BF16)],
        compiler_params=_params(("parallel",)),
    )(x2d, att, g1, n2g, sh2, sc2)


def _resid_norm2_bwd(dz2, x1, dx2, att, n2g, sc2, g1, tr):
    n, d = x1.shape

    def body(dz_ref, x1_ref, dx2_ref, a_ref, g_ref, sc_ref, g1_ref, dx1_ref, da_ref, dg_ref, dsh_ref, dsc_ref, dg1_ref):
        _acc_init(pl.program_id(0), [dg_ref, dsh_ref, dsc_ref, dg1_ref])
        xh, r = _rms(x1_ref[...])
        dzv = dz_ref[...]
        gv = g_ref[...]
        dsc_ref[...] += _csum(dzv * (xh * gv))
        dsh_ref[...] += _csum(dzv)
        dh = dzv * (1.0 + sc_ref[...])
        dg_ref[...] += _csum(dh * xh)
        dx1 = _rms_bwd(xh, r, dh * gv) + dx2_ref[...]
        dx1_ref[...] = dx1
        dg1_ref[...] += _csum(dx1 * a_ref[...])
        da_ref[...] = (dx1 * g1_ref[...]).astype(BF16)

    return pl.pallas_call(
        body,
        name="resid_norm2_bwd",
        grid=(n // tr,),
        in_specs=[_rows(tr, d)] * 4 + [_bcast(d)] * 3,
        out_specs=[_rows(tr, d), _rows(tr, d)] + [_bcast(d)] * 4,
        out_shape=[jax.ShapeDtypeStruct((n, d), F32), jax.ShapeDtypeStruct((n, d), BF16)] + [jax.ShapeDtypeStruct((1, d), F32)] * 4,
        compiler_params=_params(("arbitrary",)),
    )(dz2, x1, dx2, att, n2g, sc2, g1)


SUBLANES = 8


def _shifts(pad, u):
    n = u.shape[0]
    pad[SUBLANES : n + SUBLANES, :] = u
    return pad[SUBLANES - 1 : n + SUBLANES - 1, :], pad[SUBLANES + 1 : n + SUBLANES + 1, :]


def _zero_borders(pads, n):
    for pad in pads:
        zeros = jnp.zeros((SUBLANES, pad.shape[1]), F32)
        pad[0:SUBLANES, :] = zeros
        pad[n + SUBLANES : n + 2 * SUBLANES, :] = zeros


def _conv3(u, prev, nxt, w_ref, b_ref):
    return b_ref[...] + w_ref[0:1, :] * prev + w_ref[1:2, :] * u + w_ref[2:3, :] * nxt


def _conv_fwd(u, cw, cb, tc):
    n, two_f = u.shape
    f = two_f // 2
    nb = f // tc

    def body(ua_ref, ub_ref, wa_ref, wb_ref, ba_ref, bb_ref, h_ref, pad_a, pad_b):
        _zero_borders([pad_a, pad_b], n)
        ua = ua_ref[...].astype(F32)
        ub = ub_ref[...].astype(F32)
        a = _conv3(ua, *_shifts(pad_a, ua), wa_ref, ba_ref)
        b = _conv3(ub, *_shifts(pad_b, ub), wb_ref, bb_ref)
        h_ref[...] = (a * jax.nn.sigmoid(a) * b).astype(BF16)

    col = lambda rows, off: pl.BlockSpec((rows, tc), lambda i: (0, i + off))
    return pl.pallas_call(
        body,
        name="conv_fwd",
        grid=(nb,),
        in_specs=[col(n, 0), col(n, nb), col(3, 0), col(3, nb), col(1, 0), col(1, nb)],
        out_specs=col(n, 0),
        out_shape=jax.ShapeDtypeStruct((n, f), BF16),
        scratch_shapes=[pltpu.VMEM((n + 2 * SUBLANES, tc), F32)] * 2,
        compiler_params=_params(("parallel",)),
    )(u, u, cw, cw, cb, cb)


def _conv_bwd(u, dh, cw, cb, tc):
    n, two_f = u.shape
    f = two_f // 2
    nb = f // tc

    def part(uv, prev, nxt, duc, pad, w_ref, du_ref, dw_ref, db_ref):
        db_ref[...] = _csum(duc)
        dw_ref[0:1, :] = _csum(duc * prev)
        dw_ref[1:2, :] = _csum(duc * uv)
        dw_ref[2:3, :] = _csum(duc * nxt)
        d_prev, d_next = _shifts(pad, duc)
        du_ref[...] = (w_ref[0:1, :] * d_next + w_ref[1:2, :] * duc + w_ref[2:3, :] * d_prev).astype(BF16)

    def body(ua_ref, ub_ref, dh_ref, wa_ref, wb_ref, ba_ref, bb_ref, dua_ref, dub_ref, dwa_ref, dwb_ref, dba_ref, dbb_ref,
             pad_a, pad_b, pad_da, pad_db):
        _zero_borders([pad_a, pad_b, pad_da, pad_db], n)
        ua = ua_ref[...].astype(F32)
        ub = ub_ref[...].astype(F32)
        sa = _shifts(pad_a, ua)
        sb = _shifts(pad_b, ub)
        a = _conv3(ua, *sa, wa_ref, ba_ref)
        b = _conv3(ub, *sb, wb_ref, bb_ref)
        dhv = dh_ref[...].astype(F32)
        sg = jax.nn.sigmoid(a)
        da = dhv * b * (sg * (1.0 + a * (1.0 - sg)))
        db = dhv * (a * sg)
        part(ua, *sa, da, pad_da, wa_ref, dua_ref, dwa_ref, dba_ref)
        part(ub, *sb, db, pad_db, wb_ref, dub_ref, dwb_ref, dbb_ref)

    col = lambda rows, off: pl.BlockSpec((rows, tc), lambda i: (0, i + off))
    return pl.pallas_call(
        body,
        name="conv_bwd",
        grid=(nb,),
        in_specs=[col(n, 0), col(n, nb), col(n, 0), col(3, 0), col(3, nb), col(1, 0), col(1, nb)],
        out_specs=[col(n, 0), col(n, 0), col(3, 0), col(3, 0), col(1, 0), col(1, 0)],
        out_shape=[jax.ShapeDtypeStruct((n, f), BF16)] * 2 + [jax.ShapeDtypeStruct((3, f), F32)] * 2 + [jax.ShapeDtypeStruct((1, f), F32)] * 2,
        scratch_shapes=[pltpu.VMEM((n + 2 * SUBLANES, tc), F32)] * 4,
        compiler_params=_params(("parallel",)),
    )(u, u, dh, cw, cw, cb, cb)


def _loss_head(x1, f, g2, fg, tgt, tr):
    n, d = x1.shape

    def body(x1_ref, f_ref, g2_ref, fg_ref, t_ref, sq_ref, dx2_ref, dfg_ref, dg2_ref, df_ref):
        _acc_init(pl.program_id(0), [sq_ref, dfg_ref, dg2_ref])
        fv = f_ref[...]
        xh, r = _rms(x1_ref[...] + g2_ref[...] * fv)
        err = xh * fg_ref[...] - t_ref[...]
        sq_ref[...] += _csum(err * err)
        dy = err * (1.0 / d)
        dfg_ref[...] += _csum(dy * xh)
        dx2 = _rms_bwd(xh, r, dy * fg_ref[...])
        dx2_ref[...] = dx2
        dg2_ref[...] += _csum(dx2 * fv)
        df_ref[...] = (dx2 * g2_ref[...]).astype(BF16)

    return pl.pallas_call(
        body,
        name="loss_head",
        grid=(n // tr,),
        in_specs=[_rows(tr, d), _rows(tr, d), _bcast(d), _bcast(d), _rows(tr, d)],
        out_specs=[_bcast(d), _rows(tr, d), _bcast(d), _bcast(d), _rows(tr, d)],
        out_shape=[jax.ShapeDtypeStruct((1, d), F32), jax.ShapeDtypeStruct((n, d), F32), jax.ShapeDtypeStruct((1, d), F32),
                   jax.ShapeDtypeStruct((1, d), F32), jax.ShapeDtypeStruct((n, d), BF16)],
        compiler_params=_params(("arbitrary",)),
    )(x1, f, g2, fg, tgt)


def _sum_slots(g, name):
    s, r, w = g.shape

    def body(g_ref, o_ref):
        acc = g_ref[0]
        for k in range(1, s):
            acc = acc + g_ref[k]
        o_ref[...] = acc

    return pl.pallas_call(body, name=name, out_shape=jax.ShapeDtypeStruct((r, w), F32))(g)


def _silu_grad_mul(ds, cvec):
    def body(d_ref, c_ref, o_ref):
        cv = c_ref[...]
        sg = jax.nn.sigmoid(cv)
        o_ref[...] = d_ref[...] * (sg * (1.0 + cv * (1.0 - sg)))

    return pl.pallas_call(body, name="silu_grad_mul", out_shape=jax.ShapeDtypeStruct(ds.shape, F32))(ds, cvec)


def _adamw_update(wv, gv, mv, vv, d_ref, mo_ref, vo_ref):
    mn = ADAM_B1 * mv + (1.0 - ADAM_B1) * gv
    vn = ADAM_B2 * vv + (1.0 - ADAM_B2) * (gv * gv)
    mo_ref[...] = mn
    vo_ref[...] = vn
    m_hat = mn / (1.0 - ADAM_B1**ADAM_STEP)
    v_hat = vn / (1.0 - ADAM_B2**ADAM_STEP)
    d_ref[...] = -ADAM_LR * (m_hat / (jnp.sqrt(v_hat) + ADAM_EPS) + ADAM_WD * wv)


def _adamw_many(ws, gs, ms, vs, name):
    n = len(ws)

    def body(*refs):
        for k in range(n):
            w_ref, g_ref, m_ref, v_ref = (refs[q * n + k] for q in range(4))
            d_ref, mo_ref, vo_ref = (refs[(4 + q) * n + k] for q in range(3))
            _adamw_update(w_ref[...], g_ref[...], m_ref[...], v_ref[...], d_ref, mo_ref, vo_ref)

    res = pl.pallas_call(body, name=name, out_shape=[jax.ShapeDtypeStruct(w.shape, F32) for w in ws] * 3)(*ws, *gs, *ms, *vs)
    return res[:n], res[n : 2 * n], res[2 * n :]


def _adamw(w, g, m, v, name, g_transposed=False, g_sibling=None):
    r, cdim = w.shape
    halves = g_sibling is not None
    if g_transposed:
        tr = _pick(r, 1024, LANES)
        tc = _pick(cdim // 2 if halves else cdim, max(LANES, (1 << 19) // tr))
        per_half = (cdim // 2) // tc
    else:
        tr = _pick(r // 2 if halves else r, 1024, 8)
        tc = _pick(cdim, max(LANES, (1 << 19) // tr))
        per_half = (r // 2) // tr
    emit_g = g_transposed or halves

    def body(w_ref, g_ref, *rest):
        m_ref, v_ref = rest[halves : halves + 2]
        outs = rest[halves + 2 :]
        gv = g_ref[...]
        if halves:
            along = pl.program_id(1 if g_transposed else 0)
            gv = jnp.where(along // per_half == lax.axis_index("c"), gv, rest[0][...])
        if g_transposed:
            gv = gv.T
        if emit_g:
            outs[0][...] = gv
        _adamw_update(w_ref[...], gv, m_ref[...], v_ref[...], *outs[-3:])

    spec = pl.BlockSpec((tr, tc), lambda i, j: (i, j))
    if g_transposed:
        g_spec = pl.BlockSpec((tc, tr), lambda i, j: (j % per_half if halves else j, i))
    else:
        g_spec = pl.BlockSpec((tr, tc), lambda i, j: (i % per_half if halves else i, j))
    n_out = 3 + emit_g
    res = pl.pallas_call(
        body,
        name=name,
        grid=(r // tr, cdim // tc),
        in_specs=[spec, g_spec] + [g_spec] * halves + [spec, spec],
        out_specs=[spec] * n_out,
        out_shape=[jax.ShapeDtypeStruct((r, cdim), F32)] * n_out,
        compiler_params=_params(("parallel", "parallel")),
    )(w, g, *([g_sibling] if halves else []), m, v)
    return res if emit_g else [g, *res]


def _place():
    return lax.axis_index("x"), lax.axis_index("y"), lax.axis_index("c")


def _remote(src, dst, send_sem, recv_sem, dev):
    return pltpu.make_async_remote_copy(src_ref=src, dst_ref=dst, send_sem=send_sem, recv_sem=recv_sem, device_id=dev, device_id_type=MESH)


ANY = pl.BlockSpec(memory_space=pl.ANY)


def _all_gather_small(v, name):
    r, w = v.shape

    def body(v_ref, o_ref, send, recv, lsem):
        x, y, c = _place()
        me = 4 * x + 2 * y + c
        mine = pltpu.make_async_copy(v_ref, o_ref.at[me], lsem)
        mine.start()
        sent = []
        for k in range(1, 8):
            px, py, pc = x ^ (k >> 2), y ^ ((k >> 1) & 1), c ^ (k & 1)
            cp = _remote(v_ref, o_ref.at[me], send.at[k - 1], recv.at[k - 1], (px, py, pc))
            cp.start()
            sent.append(cp)
        for k in range(1, 8):
            px, py, pc = x ^ (k >> 2), y ^ ((k >> 1) & 1), c ^ (k & 1)
            slot = o_ref.at[4 * px + 2 * py + pc]
            _remote(slot, slot, send.at[k - 1], recv.at[k - 1], (x, y, c)).wait_recv()
        for cp in sent:
            cp.wait_send()
        mine.wait()

    return pl.pallas_call(
        body,
        name=name,
        out_shape=jax.ShapeDtypeStruct((8, r, w), F32),
        in_specs=[pl.BlockSpec(memory_space=pltpu.VMEM)],
        out_specs=pl.BlockSpec(memory_space=pltpu.VMEM),
        scratch_shapes=[pltpu.SemaphoreType.DMA((7,)), pltpu.SemaphoreType.DMA((7,)), pltpu.SemaphoreType.DMA],
        compiler_params=pltpu.CompilerParams(vmem_limit_bytes=VMEM_LIMIT),
    )(v)


HBM = pl.BlockSpec(memory_space=pltpu.HBM)
SEM = pl.BlockSpec(memory_space=pltpu.SEMAPHORE)
EFFECT = pltpu.SideEffectType.DATAFLOW_SIDE_EFFECTING


def _other_chips(x, y):
    return [(1 - x, y), (x, 1 - y), (1 - x, 1 - y)]


def _bulk_start(name, srcs, land_shapes, n_copies, copies, after):
    n, m = len(srcs), len(land_shapes)

    def body(*refs):
        src_refs, land_refs = refs[:n], refs[n : n + m]
        send, recv = refs[n + m + 1], refs[n + m + 2]
        token = refs[-1]
        for k, (s, d, dev) in enumerate(copies(src_refs, land_refs)):
            _remote(s, d, send.at[k], recv.at[k], dev).start()
        token[...] = jnp.zeros_like(token)

    lands = [pltpu.with_memory_space_constraint(lax.empty(s.shape, s.dtype), pltpu.HBM) for s in land_shapes]
    out = pl.pallas_call(
        body,
        name=name,
        out_shape=(pltpu.SemaphoreType.DMA((n_copies,)), pltpu.SemaphoreType.DMA((n_copies,)),
                   *[pltpu.HBM(s.shape, s.dtype) for s in srcs], *[pltpu.HBM(s.shape, s.dtype) for s in land_shapes],
                   jax.ShapeDtypeStruct((8, LANES), F32)),
        in_specs=[HBM] * (n + m) + [ANY],
        out_specs=(SEM, SEM, *[HBM] * (n + m), pl.BlockSpec(memory_space=pltpu.VMEM)),
        input_output_aliases={i: 2 + i for i in range(n + m)},
        compiler_params=pltpu.CompilerParams(has_side_effects=EFFECT),
    )(*[pltpu.with_memory_space_constraint(s, pltpu.HBM) for s in srcs], *lands, after)
    return out[0], out[1], list(out[2 : 2 + n]), list(out[2 + n : 2 + n + m]), out[-1][0:1, 0:1]


def _bulk_wait(name, send, recv, srcs, lands, after, waits):
    n, m = len(srcs), len(lands)

    def body(*refs):
        src_refs, land_refs = refs[:n], refs[n : n + m]
        send_sem, recv_sem = refs[n + m], refs[n + m + 1]
        x, y, c = _place()
        for k, (s, d) in enumerate(waits(src_refs, land_refs)):
            cp = _remote(s, d, send_sem.at[k], recv_sem.at[k], (x, y, c))
            cp.wait_send()
            cp.wait_recv()

    out = pl.pallas_call(
        body,
        name=name,
        out_shape=tuple(pltpu.HBM(s.shape, s.dtype) for s in (*srcs, *lands)),
        in_specs=[HBM] * (n + m) + [SEM, SEM, ANY],
        out_specs=tuple([HBM] * (n + m)),
        input_output_aliases={i: i for i in range(n + m)},
        compiler_params=pltpu.CompilerParams(has_side_effects=EFFECT),
    )(*srcs, *lands, send, recv, after)
    return list(out[:n]), list(out[n:])


def _gather_start(shards, after, name):
    def copies(src, land):
        x, y, c = _place()
        j = 2 * x + y
        return [(src[a].at[c], land[a].at[j, c], (px, py, c)) for a in range(len(shards)) for px, py in _other_chips(x, y)]

    shapes = [jax.ShapeDtypeStruct((4,) + s.shape, s.dtype) for s in shards]
    return _bulk_start(name, shards, shapes, 3 * len(shards), copies, after)


def _gather_wait(started, after, name):
    send, recv, srcs, lands, _ = started

    def waits(src, land):
        x, y, c = _place()
        return [(src[a].at[c], land[a].at[2 * px + py, c]) for a in range(len(srcs)) for px, py in _other_chips(x, y)]

    return _bulk_wait(name, send, recv, srcs, lands, after, waits)


def _forward_halves(lands, name):
    n = len(lands)

    def body(*refs):
        bufs = refs[n : 2 * n]
        send, recv = refs[2 * n :]
        x, y, c = _place()
        started = []
        for a in range(n):
            for k, (px, py) in enumerate(_other_chips(x, y)):
                blk = bufs[a].at[2 * px + py, c]
                cp = _remote(blk, blk, send.at[3 * a + k], recv.at[3 * a + k], (x, y, 1 - c))
                cp.start()
                started.append(cp)
        for a in range(n):
            for k, (px, py) in enumerate(_other_chips(x, y)):
                blk = bufs[a].at[2 * px + py, 1 - c]
                _remote(blk, blk, send.at[3 * a + k], recv.at[3 * a + k], (x, y, c)).wait_recv()
        for cp in started:
            cp.wait_send()

    return pl.pallas_call(
        body,
        name=name,
        out_shape=[jax.ShapeDtypeStruct(b.shape, b.dtype) for b in lands],
        in_specs=[ANY] * n,
        out_specs=[ANY] * n,
        input_output_aliases={i: i for i in range(n)},
        scratch_shapes=[pltpu.SemaphoreType.DMA((3 * n,)), pltpu.SemaphoreType.DMA((3 * n,))],
    )(*lands)


def _forward_start(lands, after, name):
    def copies(src, _):
        x, y, c = _place()
        blocks = [src[a].at[2 * px + py, c] for a in range(len(lands)) for px, py in _other_chips(x, y)]
        return [(b, b, (x, y, 1 - c)) for b in blocks]

    return _bulk_start(name, lands, [], 3 * len(lands), copies, after)


def _forward_wait(started, after, name):
    send, recv, bufs, _, _ = started

    def waits(src, _):
        x, y, c = _place()
        return [(src[a].at[2 * px + py, c], src[a].at[2 * px + py, 1 - c]) for a in range(len(bufs)) for px, py in _other_chips(x, y)]

    return _bulk_wait(name, send, recv, bufs, [], after, waits)[0]


def _place_own(shards, lands):
    j = 2 * lax.axis_index("x") + lax.axis_index("y")
    full = [lax.dynamic_update_slice(b, s[None], (j, 0, 0, 0)) for b, s in zip(lands, shards)]
    return [f.reshape(4 * f.shape[2] * 2, f.shape[3]) for f in full]


def _gather_finish(started, after, tag):
    shards, lands = _gather_wait(started, after, "gather_wait_" + tag)
    return _place_own(shards, _forward_halves(lands, "gather_forward_" + tag))


def _gather_land(started, after, tag):
    shards, lands = _gather_wait(started, after, "gather_wait_" + tag)
    return shards, _forward_start(lands, shards[0], "forward_start_" + tag)


def _gather_done(landed, after, tag):
    shards, fwd = landed
    return _place_own(shards, _forward_wait(fwd, after, "forward_wait_" + tag))


def _swap_halves(grads, name):
    n = len(grads)

    def body(*refs):
        ins, outs = refs[:n], refs[n : 2 * n]
        send, recv = refs[2 * n :]
        x, y, c = _place()
        started = []
        for a in range(n):
            for s in range(4):
                cp = _remote(ins[a].at[s, 1 - c], outs[a].at[s], send.at[4 * a + s], recv.at[4 * a + s], (x, y, 1 - c))
                cp.start()
                started.append(cp)
        for cp in started:
            cp.wait_recv()
        for cp in started:
            cp.wait_send()

    return pl.pallas_call(
        body,
        name=name,
        out_shape=[jax.ShapeDtypeStruct((4,) + g.shape[2:], g.dtype) for g in grads],
        in_specs=[ANY] * n,
        out_specs=[ANY] * n,
        scratch_shapes=[pltpu.SemaphoreType.DMA((4 * n,)), pltpu.SemaphoreType.DMA((4 * n,))],
    )(*grads)


def _add_halves(grads, others, tag):
    outs = []
    for a, (g, o) in enumerate(zip(grads, others)):
        _, _, rh, cdim = g.shape
        tr = _pick(rh, 512, 16)

        def body(g_ref, o_ref, p_ref):
            p_ref[...] = (g_ref[...].astype(F32) + o_ref[...].astype(F32)).astype(BF16)

        outs.append(
            pl.pallas_call(
                body,
                name=f"add_halves_{tag}{a}",
                grid=(4, rh // tr),
                in_specs=[pl.BlockSpec((None, None, tr, cdim), lambda s, i: (s, lax.axis_index("c"), i, 0)),
                          pl.BlockSpec((None, tr, cdim), lambda s, i: (s, i, 0))],
                out_specs=pl.BlockSpec((None, tr, cdim), lambda s, i: (s, i, 0)),
                out_shape=jax.ShapeDtypeStruct((4, rh, cdim), BF16),
                compiler_params=_params(("parallel", "parallel")),
            )(g, o)
        )
    return outs


def _exchange_start(parts, after, name):
    def copies(src, land):
        x, y, c = _place()
        j = 2 * x + y
        return [(src[a].at[2 * px + py], land[a].at[j], (px, py, c)) for a in range(len(parts)) for px, py in _other_chips(x, y)]

    return _bulk_start(name, parts, [jax.ShapeDtypeStruct(p.shape, p.dtype) for p in parts], 3 * len(parts), copies, after)


def _exchange_finish(started, after, name):
    send, recv, srcs, lands, _ = started

    def waits(src, land):
        x, y, _ = _place()
        return [(src[a].at[2 * px + py], land[a].at[2 * px + py]) for a in range(len(srcs)) for px, py in _other_chips(x, y)]

    srcs, lands = _bulk_wait(name, send, recv, srcs, lands, after, waits)
    j = 2 * lax.axis_index("x") + lax.axis_index("y")
    return [lax.dynamic_update_slice(b, lax.dynamic_slice(p, (j, 0, 0), (1,) + p.shape[1:]), (j, 0, 0)) for b, p in zip(lands, srcs)]


def _sum_chips(recvd, tag):
    outs = []
    for a, g in enumerate(recvd):
        _, rh, cdim = g.shape
        tr = _pick(rh, 512, 16)

        def body(g_ref, o_ref):
            o_ref[...] = ((g_ref[0].astype(F32) + g_ref[1].astype(F32)) + g_ref[2].astype(F32)) + g_ref[3].astype(F32)

        outs.append(
            pl.pallas_call(
                body,
                name=f"sum_chips_{tag}{a}",
                grid=(rh // tr,),
                in_specs=[pl.BlockSpec((4, tr, cdim), lambda i: (0, i, 0))],
                out_specs=pl.BlockSpec((tr, cdim), lambda i: (i, 0)),
                out_shape=jax.ShapeDtypeStruct((rh, cdim), F32),
                compiler_params=_params(("parallel",)),
            )(g)
        )
    return outs


def _join_halves(halves, name):
    n = len(halves)

    def body(*refs):
        ins, outs = refs[:n], refs[n : 2 * n]
        send, recv = refs[2 * n :]
        x, y, c = _place()
        started = []
        for a in range(n):
            cp = _remote(ins[a], outs[a], send.at[a], recv.at[a], (x, y, 1 - c))
            cp.start()
            started.append(cp)
        for cp in started:
            cp.wait_recv()
        for cp in started:
            cp.wait_send()

    others = pl.pallas_call(
        body,
        name=name,
        out_shape=[jax.ShapeDtypeStruct(h.shape, h.dtype) for h in halves],
        in_specs=[ANY] * n,
        out_specs=[ANY] * n,
        scratch_shapes=[pltpu.SemaphoreType.DMA((n,)), pltpu.SemaphoreType.DMA((n,))],
    )(*halves)
    return list(zip(halves, others))


def _joined(mine, other):
    first = lax.axis_index("c") == 0
    return jnp.concatenate([jnp.where(first, mine, other), jnp.where(first, other, mine)], axis=0)


def _grad_views(grads):
    return [g.reshape(4, 2, g.shape[0] // 8, g.shape[1]) for g in grads]


def _scatter_start(grads, tag, after=None):
    views = _grad_views(grads)
    mine = _add_halves(views, _swap_halves(views, "swap_halves_" + tag), tag)
    return _exchange_start(mine, mine[-1] if after is None else after, "exchange_start_" + tag)


def _swap_start(grads, after, tag):
    views = _grad_views(grads)

    def copies(src, land):
        x, y, c = _place()
        return [(src[a].at[s, 1 - c], land[a].at[s], (x, y, 1 - c)) for a in range(len(views)) for s in range(4)]

    shapes = [jax.ShapeDtypeStruct((4,) + v.shape[2:], v.dtype) for v in views]
    return _bulk_start("swap_start_" + tag, views, shapes, 4 * len(views), copies, after)


def _scatter_start_after_swap(swapped, after, tag):
    send, recv, views, lands, _ = swapped

    def waits(src, land):
        c = lax.axis_index("c")
        return [(src[a].at[s, 1 - c], land[a].at[s]) for a in range(len(views)) for s in range(4)]

    views, others = _bulk_wait("swap_wait_" + tag, send, recv, views, lands, after, waits)
    mine = _add_halves(views, others, tag)
    return _exchange_start(mine, mine[-1], "exchange_start_" + tag)


def _join_start(halves, after, tag):
    def copies(src, land):
        x, y, c = _place()
        return [(src[a], land[a], (x, y, 1 - c)) for a in range(len(halves))]

    return _bulk_start("join_start_" + tag, halves, [jax.ShapeDtypeStruct(h.shape, h.dtype) for h in halves], len(halves), copies, after)


def _join_wait(started, after, tag):
    send, recv, halves, lands, _ = started
    halves, others = _bulk_wait("join_wait_" + tag, send, recv, halves, lands, after, lambda src, land: list(zip(src, land)))
    return list(zip(halves, others))


def _scatter_sums(started, after, tag):
    return _sum_chips(_exchange_finish(started, after, "exchange_wait_" + tag), tag)


def _scatter_finish(started, after, tag):
    return _join_halves(_scatter_sums(started, after, tag), "join_halves_" + tag)


def _t_bf16(w):
    return w.T.astype(BF16)


def kernel(x, c, ctx, c_ctx, w_ada, b_ada, norm1_g, w_in, mla_q_norm_g, w_q_up, mla_kv_norm_g, w_kv_up, gqa_q_norm_g, gqa_k_norm_g, w_br_a, w_br_b, w_out, norm2_g, w_up, conv_w, conv_b, w_down, final_norm_g, loss_target, m_c_ctx, m_w_ada, m_b_ada, m_norm1_g, m_w_in, m_mla_q_norm_g, m_w_q_up, m_mla_kv_norm_g, m_w_kv_up, m_gqa_q_norm_g, m_gqa_k_norm_g, m_w_br_a, m_w_br_b, m_w_out, m_norm2_g, m_w_up, m_conv_w, m_conv_b, m_w_down, m_final_norm_g, v_c_ctx, v_w_ada, v_b_ada, v_norm1_g, v_w_in, v_mla_q_norm_g, v_w_q_up, v_mla_kv_norm_g, v_w_kv_up, v_gqa_q_norm_g, v_gqa_k_norm_g, v_w_br_a, v_w_br_b, v_w_out, v_norm2_g, v_w_up, v_conv_w, v_conv_b, v_w_down, v_final_norm_g):
    T, D = x.shape[1], x.shape[2]
    C = ctx.shape[1]
    NA = w_ada.shape[2]
    NW = w_up.shape[2]
    F2 = 4 * NW
    FF = F2 // 2
    xi, yi, ci = _place()
    j = 2 * xi + yi
    me = 4 * xi + 2 * yi + ci
    tr = _pick(C, 128, 8)
    tq = _pick(T, 256)

    x2d, tgt, ctx2d = x[0], loss_target[0], ctx[0]
    fg = final_norm_g.reshape(1, D)
    cc = c_ctx.reshape(1, D)

    halve = lambda s: s.reshape(2, s.shape[0] // 2, s.shape[1])
    win_shard = halve(_t_bf16(w_in[0]))
    w0 = max(D, NW)
    pay = jnp.zeros((8, w0), F32).at[0:1, :D].set(c).at[1:4, :NW].set(conv_w[0])
    got = _all_gather_small(pay, "gather_cond")
    c_all = got[:, 0, :D]
    cw = jnp.concatenate([got[2 * s, 1:4, :NW] for s in range(4)], axis=1)
    s16 = jnp.concatenate([c_all, cc, jnp.zeros((7, D), F32)], axis=0)
    b_cols = lax.dynamic_slice(b_ada, (0, j * NA), (1, NA))
    ada_part = _mm(s16, w_ada[0], "NN", F32, "ada_fwd", act="silu", bias=b_cols)
    got = _all_gather_small(ada_part, "gather_ada")
    ada = jnp.concatenate([got[2 * s] for s in range(4)], axis=1)
    lat = lax.dynamic_slice(ada, (me, 0), (1, 6 * D))
    sh1, sc1, g1, sh2, sc2, g2 = [lat[:, k * D : (k + 1) * D] for k in range(6)]
    csh, csc = ada[8:9, :D], ada[8:9, D : 2 * D]

    ag_in = _gather_start([win_shard], got, "gather_start_in")
    t_in = ag_in[4]
    wq3 = (w_q_up[0] + t_in).reshape(MLA_Q_LORA, 2, MLA_NOPE + MLA_ROPE)
    wq_perm = jnp.concatenate([wq3[:, :, :MLA_NOPE].reshape(MLA_Q_LORA, -1), wq3[:, :, MLA_NOPE:].reshape(MLA_Q_LORA, -1)], axis=1)
    low = [_t_bf16(wq_perm), _t_bf16(w_kv_up[0] + t_in)]
    br = [_t_bf16(w_br_a[0] + t_in), _t_bf16(w_br_b[0] + t_in), (w_out[0] + t_in).astype(BF16)]
    ag_low = _gather_start([halve(s) for s in low], t_in, "gather_start_low")
    ag_br = _gather_start([halve(s) for s in br], ag_low[4], "gather_start_br")
    ag_up = _gather_start([halve(_t_bf16(w_up[0] + t_in))], ag_br[4], "gather_start_up")
    ag_down = _gather_start([halve((w_down[0] + t_in).astype(BF16))], ag_up[4], "gather_start_down")
    sh1 = sh1 + ag_down[4]

    cos_a, ss_a = _rope_tables(C, T, MLA_ROPE)
    cos_b, ss_b = _rope_tables(C, T, GQA_HEAD_DIM)
    lcos_a, lss_a, lcos_b, lss_b = cos_a[:T], ss_a[:T], cos_b[:T], ss_b[:T]

    z_all = _norm_mod_fwd(x2d, norm1_g, sh1, sc1, "norm1_lat_fwd", tr, out_rows=T + C)
    z_all = _norm_mod_fwd(ctx2d, norm1_g, csh, csc, "norm1_ctx_fwd", tr, base=z_all, out_off=T)
    (win_t,) = _gather_finish(ag_in, z_all, "in")
    kv_cols = KVP - LANES + MLA_ROPE
    e_kpe = MLA_KV_LORA + MLA_ROPE
    w_kvp = jnp.concatenate([win_t[:MLA_KV_LORA], win_t[e_kpe:kv_cols], win_t[MLA_KV_LORA:e_kpe], jnp.zeros((LANES - MLA_ROPE, D), BF16)], axis=0)

    pkv = _mm(z_all, w_kvp, "NT", F32, "proj_kv")
    pq = _mm(z_all, win_t, "NT", F32, "proj_q", m=T, n=QC, b_off=kv_cols)
    low_landed = _gather_land(ag_low, pq, "low")
    pg = _mm(z_all, win_t, "NT", BF16, "proj_g", m=T, n=2 * D, b_off=kv_cols + QC, after=low_landed[1][4])
    wq_t, wkv_t = _gather_done(low_landed, pg, "low")
    ckv_n, kb2, vb2, kpe2 = _kprep_fwd(pkv, mla_kv_norm_g, gqa_k_norm_g, cos_a, ss_a, cos_b, ss_b, tr)
    kv_up = _mm(ckv_n, wkv_t, "NT", BF16, "kv_up")
    cq_n, qb2 = _qprep_fwd(pq, mla_q_norm_g, gqa_q_norm_g, lcos_b, lss_b, tr)
    q_a = _mm(cq_n, wq_t, "NT", F32, "q_up")
    qar = _qrope_fwd(q_a, lcos_a, lss_a, tr)

    a_q = [(qar, lambda h: 3 * (h // 2) + h % 2), (qar, lambda h: 3 * (h // 2) + 2)]
    a_k = [(kv_up, lambda h: 2 * h), (kpe2, lambda h: h % 2)]
    a_v = (kv_up, lambda h: 2 * h + 1)
    a_scale = float(MLA_NOPE + MLA_ROPE) ** -0.5
    b_q = [(qb2, lambda h: h)]
    b_k = [(kb2, lambda h: h)]
    b_v = (vb2, lambda h: h)
    b_scale = float(GQA_HEAD_DIM) ** -0.5
    tq_f = _pick(T, 512)
    o_a, lse_a = _attn_fwd(a_q, a_k, a_v, MLA_HEADS, 1, MLA_V, a_scale, "attn_a_fwd", tq_f)
    br_landed = _gather_land(ag_br, o_a, "br")
    o_b, lse_b = _attn_fwd(b_q, b_k, b_v, GQA_HEADS, GQA_GROUP, GQA_HEAD_DIM, b_scale, "attn_b_fwd", tq_f, after=br_landed[1][4])
    wbra_t, wbrb_t, wout = _gather_done(br_landed, o_b, "br")
    up_landed = _gather_land(ag_up, o_b, "up")
    ya = _mm(o_a, wbra_t, "NT", BF16, "br_a", after=up_landed[1][4])
    yb = _mm(o_b, wbrb_t, "NT", BF16, "br_b")
    merged = _gates_fwd(pg, ya, yb, tr)
    att = _mm(merged, wout, "NN", F32, "out_proj")
    x1, z2 = _resid_norm2_fwd(x2d, att, g1, norm2_g, sh2, sc2, tr)
    (wup_t,) = _gather_done(up_landed, z2, "up")
    down_landed = _gather_land(ag_down, z2, "down")
    u = _mm(z2, wup_t, "NT", BF16, "ffn_up", after=down_landed[1][4])
    tc = _pick(FF, 128)
    hg = _conv_fwd(u, cw, conv_b, tc)
    (wdown,) = _gather_done(down_landed, hg, "down")
    f = _mm(hg, wdown, "NN", F32, "ffn_down", tk=FF // 2)
    sq, dx2, d_fg, d_g2, df = _loss_head(x1, f, g2, fg, tgt, tr)
    loss = lax.psum(0.5 * jnp.sum(sq) / D, ("x", "y", "c"))

    dhg = _mm(df, wdown, "NT", BF16, "ffn_down_dx")
    g_wdown = _mm(hg, df, "TN", BF16, "ffn_down_dw", tm=FF // 4)
    du_a, du_b, dcw_a, dcw_b, dcb_a, dcb_b = _conv_bwd(u, dhg, cw, conv_b, tc)
    dz2 = _mm(du_a, wup_t, "NN", F32, "ffn_up_dx_a", tk=FF // 2)
    dz2 = _mm(du_b, wup_t, "NN", F32, "ffn_up_dx_b", b_off=FF, add=dz2, tk=FF // 2)
    g_wup_t = _mm(du_a, z2, "TN", BF16, "ffn_up_dw_a", out_rows=F2, tm=FF // 4)
    g_wup_t = _mm(du_b, z2, "TN", BF16, "ffn_up_dw_b", out_base=g_wup_t, out_off=FF, tm=FF // 4)
    sw_ffn = _swap_start([g_wdown, g_wup_t], sc2, "ffn")
    sc2 = sc2 + sw_ffn[4]
    dx1, datt, d_n2g, d_sh2, d_sc2, d_g1 = _resid_norm2_bwd(dz2, x1, dx2, att, norm2_g, sc2, g1, tr)

    dmerged = _mm(datt, wout, "NT", BF16, "out_proj_dx")
    rs_ffn = _scatter_start_after_swap(sw_ffn, dmerged, "ffn")
    lse_a = lse_a + rs_ffn[4]
    g_wout = _mm(merged, datt, "TN", BF16, "out_proj_dw")
    dya, dyb, dpg = _gates_bwd(dmerged, pg, ya, yb, tr)
    do_a = _mm(dya, wbra_t, "NN", BF16, "br_a_dx")
    g_wbra_t = _mm(dya, o_a, "TN", BF16, "br_a_dw")
    do_b = _mm(dyb, wbrb_t, "NN", BF16, "br_b_dx")
    g_wbrb_t = _mm(dyb, o_b, "TN", BF16, "br_b_dw")
    dqa2, dka2, dva2 = _attn_bwd(a_q, a_k, a_v, o_a, do_a, lse_a, MLA_HEADS, 1, MLA_V, a_scale, "attn_a_bwd", tq_f)
    dqb2, dkb2, dvb2 = _attn_bwd(b_q, b_k, b_v, o_b, do_b, lse_b, GQA_HEADS, GQA_GROUP, GQA_HEAD_DIM, b_scale, "attn_b_bwd", tq_f)
    dq_a = _qrope_bwd(dqa2, lcos_a, lss_a, tr)
    dcq_n = _mm(dq_a, wq_t, "NN", F32, "q_up_dx")
    g_wq_t = _mm(dq_a, cq_n, "TN", BF16, "q_up_dw")
    dpq, d_qg, d_gq = _qprep_bwd(pq, dcq_n, dqb2, mla_q_norm_g, gqa_q_norm_g, lcos_b, lss_b, tr)
    dkv_up, dkpe = _kgrad_split(dka2, dva2, cos_a, ss_a, tr)
    dckv_n = _mm(dkv_up, wkv_t, "NN", F32, "kv_up_dx")
    g_wkv_t = _mm(dkv_up, ckv_n, "TN", BF16, "kv_up_dw")
    rs_mix = _scatter_start([g_wq_t, g_wkv_t, g_wbra_t, g_wbrb_t, g_wout], "mix")
    dpkv, d_kvg, d_kg = _kprep_bwd(pkv, dckv_n, dkb2, dvb2, dkpe, mla_kv_norm_g + rs_mix[4], gqa_k_norm_g, cos_b, ss_b, tr)
    dz_kv = _mm(dpkv, w_kvp, "NN", F32, "proj_kv_dx")
    dz_lat = _mm(dpq, win_t, "NN", F32, "proj_q_dx", b_off=kv_cols, add=dz_kv)
    dz_lat = _mm(dpg, win_t, "NN", F32, "proj_g_dx", b_off=kv_cols + QC, add=dz_lat)
    _, d_n1g_c, d_csh, d_csc = _norm_mod_bwd(dz_kv, T // tr, ctx2d, norm1_g, csc, None, "norm1_ctx_bwd", tr)
    grad_x, d_n1g_l, d_sh1, d_sc1 = _norm_mod_bwd(dz_lat, 0, x2d, norm1_g, sc1, dx1, "norm1_lat_bwd", tr)

    zeros_d = jnp.zeros((1, D), F32)
    d_lat = jnp.concatenate([d_sh1, d_sc1, d_g1, d_sh2, d_sc2, d_g2], axis=1)
    d_ctx_part = jnp.concatenate([d_csh, d_csc], axis=1)
    flat = jnp.concatenate(
        [d_n1g_c + d_n1g_l, d_qg, d_kvg, d_gq, d_kg, d_n2g, dcb_a, dcb_b, d_fg,
         dcw_a.reshape(1, -1), dcw_b.reshape(1, -1), d_ctx_part, d_lat], axis=1)
    n_flat = flat.shape[1]
    n_rows = -(-n_flat // (8 * LANES)) * 8
    flat = jnp.pad(flat, ((0, 0), (0, n_rows * LANES - n_flat))).reshape(n_rows, LANES)
    got = _all_gather_small(flat, "gather_small_grads")
    tot = _sum_slots(got, "sum_small_grads").reshape(1, -1)
    sizes = [D, MLA_Q_LORA, MLA_KV_LORA, GQA_HEAD_DIM, GQA_HEAD_DIM, D, F2, D, 3 * FF, 3 * FF, 2 * D]
    offs = [0]
    for s in sizes:
        offs.append(offs[-1] + s)
    t_n1g, t_qg, t_kvg, t_gq, t_kg, t_n2g, t_cb, t_fg, t_cwa, t_cwb, t_ctx = [tot[:, offs[k] : offs[k + 1]] for k in range(len(sizes))]
    g_cw_full = jnp.concatenate([t_cwa.reshape(3, FF), t_cwb.reshape(3, FF)], axis=1)
    g_cw = lax.dynamic_slice(g_cw_full, (0, j * NW), (3, NW))
    d_lat_all = got.reshape(8, -1)[:, offs[-1] : offs[-1] + 6 * D]
    g16 = jnp.concatenate([d_lat_all, jnp.pad(t_ctx, ((0, 0), (0, 4 * D))), jnp.zeros((7, 6 * D), F32)], axis=0)
    g_b_ada = _sum_slots(g16.reshape(16, 1, 6 * D), "sum_b_ada")
    g16_cols = lax.dynamic_slice(g16, (0, j * NA), (16, NA))
    ds_part = _mm(g16_cols, w_ada[0], "NT", F32, "ada_dx")
    got = _all_gather_small(ds_part[8:16], "gather_ada_dx")
    ds_ctx = _sum_slots(jnp.stack([got[2 * s] for s in range(4)]), "sum_ada_dx")[0:1]
    g_c_ctx = _silu_grad_mul(ds_ctx, cc)

    g_kvp = _mm(dpkv, z_all, "TN", BF16, "proj_kv_dw")
    nk = MLA_KV_LORA + 2 * GQA_KV_HEADS * GQA_HEAD_DIM
    g_kv = jnp.concatenate([g_kvp[:MLA_KV_LORA], g_kvp[nk : nk + MLA_ROPE], g_kvp[MLA_KV_LORA:nk]], axis=0)
    g_win_t = _mm(dpq, z_all, "TN", BF16, "proj_q_dw", out_rows=kv_cols + QC + 2 * D, out_off=kv_cols, tm=QC // 2)
    g_win_t = _mm(dpg, z_all, "TN", BF16, "proj_g_dw", out_base=g_win_t, out_off=kv_cols + QC)
    g_win_t = lax.dynamic_update_slice(g_win_t, g_kv, (0, 0))
    rs_in = _scatter_start([g_win_t], "in", after=got)

    h_ffn = _scatter_sums(rs_ffn, rs_in[2][0], "ffn")
    j_ffn = _join_start(h_ffn, grad_x, "ffn")
    h_mix = _scatter_sums(rs_mix, j_ffn[2][0], "mix")
    j_mix = _join_start(h_mix, j_ffn[2][0], "mix")
    g_w_ada = _mm(s16, g16_cols, "TN", F32, "ada_dw", act="silu", after=j_mix[4])
    _, d_ada, m_ada, v_ada = _adamw(w_ada[0], g_w_ada, m_w_ada[0], v_w_ada[0], "adamw_w_ada")
    r_wdown, r_wup = _join_wait(j_ffn, d_ada, "ffn")
    r_wq, r_wkv, r_wbra, r_wbrb, r_wout = _join_wait(j_mix, d_ada, "mix")
    gq_p = _joined(*r_wq).T
    gq = jnp.concatenate([gq_p[:, : 2 * MLA_NOPE].reshape(MLA_Q_LORA, 2, MLA_NOPE), gq_p[:, 2 * MLA_NOPE :].reshape(MLA_Q_LORA, 2, MLA_ROPE)], axis=2)
    grads = {
        "c_ctx": g_c_ctx.reshape(D), "w_ada": g_w_ada[None], "b_ada": g_b_ada, "norm1_g": t_n1g,
        "mla_q_norm_g": t_qg, "w_q_up": gq.reshape(1, MLA_Q_LORA, -1), "mla_kv_norm_g": t_kvg, "w_kv_up": r_wkv,
        "gqa_q_norm_g": t_gq, "gqa_k_norm_g": t_kg, "w_br_a": r_wbra, "w_br_b": r_wbrb, "w_out": r_wout,
        "norm2_g": t_n2g, "w_up": r_wup, "conv_w": g_cw[None], "conv_b": t_cb, "w_down": r_wdown,
        "final_norm_g": t_fg.reshape(D),
    }
    arrives_transposed = ("w_kv_up", "w_br_a", "w_br_b", "w_up")
    arrives_halved = arrives_transposed + ("w_out", "w_down")
    weights = dict(c_ctx=c_ctx, w_ada=w_ada, b_ada=b_ada, norm1_g=norm1_g, w_in=w_in, mla_q_norm_g=mla_q_norm_g, w_q_up=w_q_up,
                   mla_kv_norm_g=mla_kv_norm_g, w_kv_up=w_kv_up, gqa_q_norm_g=gqa_q_norm_g, gqa_k_norm_g=gqa_k_norm_g, w_br_a=w_br_a,
                   w_br_b=w_br_b, w_out=w_out, norm2_g=norm2_g, w_up=w_up, conv_w=conv_w, conv_b=conv_b, w_down=w_down,
                   final_norm_g=final_norm_g)
    m_in = dict(c_ctx=m_c_ctx, w_ada=m_w_ada, b_ada=m_b_ada, norm1_g=m_norm1_g, w_in=m_w_in, mla_q_norm_g=m_mla_q_norm_g,
                w_q_up=m_w_q_up, mla_kv_norm_g=m_mla_kv_norm_g, w_kv_up=m_w_kv_up, gqa_q_norm_g=m_gqa_q_norm_g,
                gqa_k_norm_g=m_gqa_k_norm_g, w_br_a=m_w_br_a, w_br_b=m_w_br_b, w_out=m_w_out, norm2_g=m_norm2_g, w_up=m_w_up,
                conv_w=m_conv_w, conv_b=m_conv_b, w_down=m_w_down, final_norm_g=m_final_norm_g)
    v_in = dict(c_ctx=v_c_ctx, w_ada=v_w_ada, b_ada=v_b_ada, norm1_g=v_norm1_g, w_in=v_w_in, mla_q_norm_g=v_mla_q_norm_g,
                w_q_up=v_w_q_up, mla_kv_norm_g=v_mla_kv_norm_g, w_kv_up=v_w_kv_up, gqa_q_norm_g=v_gqa_q_norm_g,
                gqa_k_norm_g=v_gqa_k_norm_g, w_br_a=v_w_br_a, w_br_b=v_w_br_b, w_out=v_w_out, norm2_g=v_norm2_g, w_up=v_w_up,
                conv_w=v_conv_w, conv_b=v_conv_b, w_down=v_w_down, final_norm_g=v_final_norm_g)
    names = list(weights)
    big = [n for n in names if weights[n].ndim == 3 and weights[n].shape[1] >= 8]
    small = [n for n in names if n not in big]
    delta, new_m, new_v = {}, {}, {}

    def update(n):
        shp = weights[n].shape
        two_d = lambda a: a.reshape(shp[1], shp[2])
        g_t = n in arrives_transposed
        if n in arrives_halved:
            g_in, g_sib = grads[n]
        else:
            g_in, g_sib = two_d(grads[n].astype(F32)), None
        g_, d_, m_, v_ = _adamw(two_d(weights[n]), g_in, two_d(m_in[n]), two_d(v_in[n]), "adamw_" + n, g_transposed=g_t, g_sibling=g_sib)
        grads[n], delta[n], new_m[n], new_v[n] = g_.reshape(shp), d_.reshape(shp), m_.reshape(shp), v_.reshape(shp)

    delta["w_ada"], new_m["w_ada"], new_v["w_ada"] = d_ada[None], m_ada[None], v_ada[None]
    early = [n for n in big if n not in ("w_in", "w_ada")]
    for n in early:
        update(n)
    done = sum(delta[n][0, 0:1, 0:1] for n in early)
    ((g_mine, g_sib),) = _scatter_finish(rs_in, done, "in")
    g_, d_, m_, v_ = _adamw(w_in[0].T, g_mine, m_w_in[0].T, v_w_in[0].T, "adamw_w_in", g_sibling=g_sib)
    grads["w_in"], delta["w_in"], new_m["w_in"], new_v["w_in"] = g_.T[None], d_.T[None], m_.T[None], v_.T[None]
    grads = {n: grads[n].reshape(weights[n].shape).astype(F32) for n in names}

    slab = lambda tree: [tree[n].reshape(-1, LANES) for n in small]
    d_, m_, v_ = _adamw_many(slab(weights), slab(grads), slab(m_in), slab(v_in), "adamw_small")
    for k, n in enumerate(small):
        shp = weights[n].shape
        delta[n], new_m[n], new_v[n] = d_[k].reshape(shp), m_[k].reshape(shp), v_[k].reshape(shp)

    return (loss, grad_x[None], *[grads[n] for n in names], *[delta[n] for n in names], *[new_m[n] for n in names],
            *[new_v[n] for n in names])
```

```python
import math

import jax
import jax.numpy as jnp
from jax import lax
from jax.experimental import pallas as pl
from jax.experimental.pallas import tpu as pltpu

F32 = jnp.float32
BF16 = jnp.bfloat16
MESH = pl.DeviceIdType.MESH

NORM_EPS = 1e-6
ROPE_THETA = 10000.0
GRID_W = 64
MLA_HEADS = 8
MLA_Q_LORA = 768
MLA_KV_LORA = 512
MLA_NOPE = 128
MLA_ROPE = 64
MLA_V = 128
GQA_HEADS = 8
GQA_KV_HEADS = 2
GQA_HEAD_DIM = 128
GQA_GROUP = GQA_HEADS // GQA_KV_HEADS
LANES = 128
KVP = MLA_KV_LORA + 2 * GQA_KV_HEADS * GQA_HEAD_DIM + LANES
QC = MLA_Q_LORA + GQA_HEADS * GQA_HEAD_DIM

ADAM_LR = 0.001
ADAM_B1 = 0.9
ADAM_B2 = 0.999
ADAM_EPS = 1e-08
ADAM_WD = 0.01
ADAM_STEP = 10

VMEM_LIMIT = 56 * 1024 * 1024


def _pick(dim, target, mult=LANES):
    t = (min(target, dim) // mult) * mult
    while t >= mult:
        if dim % t == 0:
            return t
        t -= mult
    return dim


def _params(sem):
    return pltpu.CompilerParams(dimension_semantics=sem, vmem_limit_bytes=VMEM_LIMIT)


_DIMS = {"NN": (((1,), (0,)), ((), ())), "NT": (((1,), (1,)), ((), ())), "TN": (((0,), (0,)), ((), ()))}


MM_VMEM_BUDGET = 36 * 1024 * 1024


def _mm_tiles(M, N, K, sa, sb, so, tm, tn, tk):
    tm, tn, tk = _pick(M, tm), _pick(N, tn), _pick(K, tk)

    def need(t):
        return 2 * (tm * t * sa + t * tn * sb) + 2 * tm * tn * so + (tm * tn * 4 if t < K else 0)

    while need(tk) > MM_VMEM_BUDGET and tk > LANES:
        smaller = _pick(K, tk - LANES)
        if smaller >= tk:
            break
        tk = smaller
    return tm, tn, tk


def _window(block, index, offsets):
    if not any(offsets):
        return pl.BlockSpec(block, index)
    for t, o in zip(block, offsets):
        assert o % 16 == 0 and t % 16 == 0, (block, offsets)

    def at(i, j, k):
        return tuple(pl.multiple_of(o + p * t, math.gcd(o, t)) for p, t, o in zip(index(i, j, k), block, offsets))

    return pl.BlockSpec(tuple(pl.Element(t) for t in block), at)


def _mm(a, b, mode, out_dtype, name, m=None, n=None, k=None, b_off=0, add=None, out_rows=None, out_base=None, out_off=0,
        tm=1024, tn=1024, tk=2304, act=None, bias=None, after=None):
    if mode == "NN":
        M, K, N = m or a.shape[0], k or a.shape[1], b.shape[1]
    elif mode == "NT":
        M, K, N = m or a.shape[0], a.shape[1], n or b.shape[0]
    else:
        M, K, N = a.shape[1], k or a.shape[0], b.shape[1]
    tm, tn, tk = _mm_tiles(M, N, K, a.dtype.itemsize, b.dtype.itemsize, jnp.dtype(out_dtype).itemsize, tm, tn, tk)
    nk = K // tk
    dims = _DIMS[mode]
    n_in = 2 + (bias is not None) + (add is not None) + (out_base is not None) + (after is not None)

    def body(*refs):
        a_ref, b_ref = refs[:2]
        bias_ref = refs[2] if bias is not None else None
        add_ref = refs[2 + (bias is not None)] if add is not None else None
        o_ref = refs[n_in]
        av = a_ref[...]
        if act == "silu":
            av = av * jax.nn.sigmoid(av)
        part = lax.dot_general(av.astype(BF16), b_ref[...].astype(BF16), dims, preferred_element_type=F32)

        def finish(r):
            if bias is not None:
                r = r + bias_ref[...]
            if add is not None:
                r = r + add_ref[...]
            o_ref[...] = r.astype(out_dtype)

        if nk == 1:
            finish(part)
            return
        acc = refs[-1]
        k = pl.program_id(2)

        @pl.when(k == 0)
        def _():
            acc[...] = part

        @pl.when(jnp.logical_and(k > 0, k < nk - 1))
        def _():
            acc[...] += part

        @pl.when(k == nk - 1)
        def _():
            finish(acc[...] + part)

    a_spec = pl.BlockSpec((tk, tm), lambda i, j, k: (k, i)) if mode == "TN" else pl.BlockSpec((tm, tk), lambda i, j, k: (i, k))
    if mode == "NT":
        b_spec = _window((tn, tk), lambda i, j, k: (j, k), (b_off, 0))
    else:
        b_spec = _window((tk, tn), lambda i, j, k: (k, j), (b_off, 0))
    in_specs, args = [a_spec, b_spec], [a, b]
    if bias is not None:
        in_specs.append(pl.BlockSpec((1, tn), lambda i, j, k: (0, j)))
        args.append(bias)
    if add is not None:
        in_specs.append(pl.BlockSpec((tm, tn), lambda i, j, k: (i, j)))
        args.append(add)
    aliases = {}
    if after is not None:
        in_specs.append(pl.BlockSpec(after.shape, lambda i, j, k: (0, 0)))
        args.append(after)
    if out_base is not None:
        aliases = {len(args): 0}
        in_specs.append(ANY)
        args.append(out_base)
        out_rows = out_base.shape[0]
    return pl.pallas_call(
        body,
        name=name,
        grid=(M // tm, N // tn, nk),
        in_specs=in_specs,
        out_specs=_window((tm, tn), lambda i, j, k: (i, j), (out_off, 0)),
        out_shape=jax.ShapeDtypeStruct((out_rows or M, N), out_dtype),
        input_output_aliases=aliases,
        scratch_shapes=[pltpu.VMEM((tm, tn), F32)] if nk > 1 else [],
        compiler_params=_params(("parallel", "parallel", "arbitrary")),
    )(*args)


def _rms(x):
    r = lax.rsqrt(jnp.mean(x * x, axis=-1, keepdims=True) + NORM_EPS)
    return x * r, r


def _rms_bwd(xh, r, dxh):
    return r * (dxh - xh * jnp.mean(dxh * xh, axis=-1, keepdims=True))


def _swap(x, q):
    lane = lax.broadcasted_iota(jnp.int32, x.shape, 1)
    even = ((lane // q) % 2) == 0
    return jnp.where(even, pltpu.roll(x, LANES - q, 1), pltpu.roll(x, q, 1))


def _rope(x, cos, ss, q):
    return x * cos + _swap(x, q) * ss


def _rope_t(d, cos, ss, q):
    return d * cos + _swap(d * ss, q)


def _csum(x):
    return jnp.sum(x, axis=0, keepdims=True)


def _rows(tr, w, off=0):
    return pl.BlockSpec((tr, w), lambda i: (i + off, 0))


def _bcast(w):
    return pl.BlockSpec((1, w), lambda i: (0, 0))


def _acc_init(i, refs):
    @pl.when(i == 0)
    def _():
        for r in refs:
            r[...] = jnp.zeros_like(r)


def _rope_tables(n_ctx, n_lat, rot_dim):
    rows = n_lat // GRID_W
    row = jnp.repeat(jnp.arange(rows, dtype=F32), GRID_W)
    col = jnp.tile(jnp.arange(GRID_W, dtype=F32), rows)
    half = rot_dim // 2
    inv_freq = ROPE_THETA ** (-jnp.arange(0, half, 2, dtype=F32) / half)
    ar, ac = row[:, None] * inv_freq, col[:, None] * inv_freq
    cos = jnp.concatenate([jnp.cos(ar), jnp.cos(ar), jnp.cos(ac), jnp.cos(ac)], axis=-1)
    ss = jnp.concatenate([-jnp.sin(ar), jnp.sin(ar), -jnp.sin(ac), jnp.sin(ac)], axis=-1)
    cos = jnp.tile(cos, (1, LANES // rot_dim))
    ss = jnp.tile(ss, (1, LANES // rot_dim))
    cos = jnp.concatenate([cos, jnp.ones((n_ctx, LANES), F32)], axis=0)
    ss = jnp.concatenate([ss, jnp.zeros((n_ctx, LANES), F32)], axis=0)
    return cos, ss


def _norm_mod_fwd(x2d, g, sh, sc, name, tr, out_rows=None, base=None, out_off=0):
    n, d = x2d.shape

    def body(x_ref, g_ref, sh_ref, sc_ref, *rest):
        xh, _ = _rms(x_ref[...])
        rest[-1][...] = ((xh * g_ref[...]) * (1.0 + sc_ref[...]) + sh_ref[...]).astype(BF16)

    args, in_specs, aliases = [x2d, g, sh, sc], [_rows(tr, d), _bcast(d), _bcast(d), _bcast(d)], {}
    if base is not None:
        args.append(base)
        in_specs.append(ANY)
        aliases = {4: 0}
        out_rows = base.shape[0]
    return pl.pallas_call(
        body,
        name=name,
        grid=(n // tr,),
        in_specs=in_specs,
        out_specs=_rows(tr, d, out_off // tr),
        out_shape=jax.ShapeDtypeStruct((out_rows or n, d), BF16),
        input_output_aliases=aliases,
        compiler_params=_params(("parallel",)),
    )(*args)


def _norm_mod_bwd(dz, dz_off, x2d, g, sc, dres, name, tr):
    n, d = x2d.shape
    want_dx = dres is not None

    def body(*refs):
        if want_dx:
            dz_ref, x_ref, g_ref, sc_ref, dres_ref, dx_ref, dg_ref, dsh_ref, dsc_ref = refs
        else:
            dz_ref, x_ref, g_ref, sc_ref, dg_ref, dsh_ref, dsc_ref = refs
        _acc_init(pl.program_id(0), [dg_ref, dsh_ref, dsc_ref])
        xh, r = _rms(x_ref[...])
        dzv = dz_ref[...]
        gv = g_ref[...]
        dsc_ref[...] += _csum(dzv * (xh * gv))
        dsh_ref[...] += _csum(dzv)
        dh = dzv * (1.0 + sc_ref[...])
        dg_ref[...] += _csum(dh * xh)
        if want_dx:
            dx_ref[...] = _rms_bwd(xh, r, dh * gv) + dres_ref[...]

    in_specs = [_rows(tr, d, dz_off), _rows(tr, d), _bcast(d), _bcast(d)]
    args = [dz, x2d, g, sc]
    out_specs = [_bcast(d)] * 3
    out_shape = [jax.ShapeDtypeStruct((1, d), F32)] * 3
    if want_dx:
        in_specs.append(_rows(tr, d))
        args.append(dres)
        out_specs = [_rows(tr, d)] + out_specs
        out_shape = [jax.ShapeDtypeStruct((n, d), F32)] + out_shape
    res = pl.pallas_call(
        body,
        name=name,
        grid=(n // tr,),
        in_specs=in_specs,
        out_specs=out_specs,
        out_shape=out_shape,
        compiler_params=_params(("arbitrary",)),
    )(*args)
    return res if want_dx else (None, *res)


_QA, _QB = MLA_ROPE // 4, GQA_HEAD_DIM // 4


def _kprep_fwd(pkv, kvg, kg, cos_a, ss_a, cos_b, ss_b, tr):
    n = pkv.shape[0]
    nb = GQA_KV_HEADS * GQA_HEAD_DIM

    def body(p_ref, kvg_ref, kg_ref, ca, sa, cb, sb, ckv_ref, kb_ref, vb_ref, kpe_ref):
        p = p_ref[...]
        xh, _ = _rms(p[:, :MLA_KV_LORA])
        ckv_ref[...] = (xh * kvg_ref[...]).astype(BF16)
        for e in range(GQA_KV_HEADS):
            lo = MLA_KV_LORA + e * GQA_HEAD_DIM
            kh, _ = _rms(p[:, lo : lo + GQA_HEAD_DIM])
            kb_ref[:, e * GQA_HEAD_DIM : (e + 1) * GQA_HEAD_DIM] = _rope(kh * kg_ref[...], cb[...], sb[...], _QB).astype(BF16)
        vb_ref[...] = p[:, MLA_KV_LORA + nb : MLA_KV_LORA + 2 * nb].astype(BF16)
        kr = _rope(p[:, MLA_KV_LORA + 2 * nb :], ca[...], sa[...], _QA)
        kpe_ref[:, :LANES] = kr.astype(BF16)
        kpe_ref[:, LANES:] = pltpu.roll(kr, MLA_ROPE, 1).astype(BF16)

    return pl.pallas_call(
        body,
        name="kprep_fwd",
        grid=(n // tr,),
        in_specs=[_rows(tr, KVP), _bcast(MLA_KV_LORA), _bcast(GQA_HEAD_DIM)] + [_rows(tr, LANES)] * 4,
        out_specs=[_rows(tr, MLA_KV_LORA), _rows(tr, nb), _rows(tr, nb), _rows(tr, 2 * LANES)],
        out_shape=[jax.ShapeDtypeStruct((n, w), BF16) for w in (MLA_KV_LORA, nb, nb, 2 * LANES)],
        compiler_params=_params(("parallel",)),
    )(pkv, kvg, kg, cos_a, ss_a, cos_b, ss_b)


def _kprep_bwd(pkv, dckv, dkb, dvb, dkpe, kvg, kg, cos_b, ss_b, tr):
    n = pkv.shape[0]
    nb = GQA_KV_HEADS * GQA_HEAD_DIM

    def body(p_ref, dckv_ref, dkb_ref, dvb_ref, dkpe_ref, kvg_ref, kg_ref, cb, sb, dp_ref, dkvg_ref, dkg_ref):
        _acc_init(pl.program_id(0), [dkvg_ref, dkg_ref])
        p = p_ref[...]
        xh, r = _rms(p[:, :MLA_KV_LORA])
        dn = dckv_ref[...]
        dkvg_ref[...] += _csum(dn * xh)
        dp_ref[:, :MLA_KV_LORA] = _rms_bwd(xh, r, dn * kvg_ref[...]).astype(BF16)
        for e in range(GQA_KV_HEADS):
            lo = MLA_KV_LORA + e * GQA_HEAD_DIM
            kh, rk = _rms(p[:, lo : lo + GQA_HEAD_DIM])
            dk = _rope_t(dkb_ref[:, e * GQA_HEAD_DIM : (e + 1) * GQA_HEAD_DIM], cb[...], sb[...], _QB)
            dkg_ref[...] += _csum(dk * kh)
            dp_ref[:, lo : lo + GQA_HEAD_DIM] = _rms_bwd(kh, rk, dk * kg_ref[...]).astype(BF16)
        dp_ref[:, MLA_KV_LORA + nb : MLA_KV_LORA + 2 * nb] = dvb_ref[...].astype(BF16)
        dp_ref[:, MLA_KV_LORA + 2 * nb :] = dkpe_ref[...].astype(BF16)

    return pl.pallas_call(
        body,
        name="kprep_bwd",
        grid=(n // tr,),
        in_specs=[_rows(tr, KVP), _rows(tr, MLA_KV_LORA), _rows(tr, nb), _rows(tr, nb), _rows(tr, LANES),
                  _bcast(MLA_KV_LORA), _bcast(GQA_HEAD_DIM), _rows(tr, LANES), _rows(tr, LANES)],
        out_specs=[_rows(tr, KVP), _bcast(MLA_KV_LORA), _bcast(GQA_HEAD_DIM)],
        out_shape=[jax.ShapeDtypeStruct((n, KVP), BF16), jax.ShapeDtypeStruct((1, MLA_KV_LORA), F32),
                   jax.ShapeDtypeStruct((1, GQA_HEAD_DIM), F32)],
        compiler_params=_params(("arbitrary",)),
    )(pkv, dckv, dkb, dvb, dkpe, kvg, kg, cos_b, ss_b)


def _kgrad_split(dka, dva, cos_a, ss_a, tr):
    n = dka.shape[0]
    wk = MLA_HEADS * 2 * LANES

    def body(dk_ref, dv_ref, ca, sa, dkv_ref, dkpe_ref):
        even = jnp.zeros((tr, LANES), F32)
        odd = jnp.zeros((tr, LANES), F32)
        for h in range(MLA_HEADS):
            dkv_ref[:, 2 * h * LANES : (2 * h + 1) * LANES] = dk_ref[:, 2 * h * LANES : (2 * h + 1) * LANES].astype(BF16)
            dkv_ref[:, (2 * h + 1) * LANES : (2 * h + 2) * LANES] = dv_ref[:, h * MLA_V : (h + 1) * MLA_V].astype(BF16)
            part = dk_ref[:, (2 * h + 1) * LANES : (2 * h + 2) * LANES]
            if h % 2 == 0:
                even = even + part
            else:
                odd = odd + part
        lane = lax.broadcasted_iota(jnp.int32, (tr, LANES), 1)
        low = lane < MLA_ROPE
        both = jnp.where(low, even, odd)
        tot = jnp.where(low, both + pltpu.roll(both, MLA_ROPE, 1), 0.0)
        dkpe_ref[...] = _rope_t(tot, ca[...], sa[...], _QA)

    return pl.pallas_call(
        body,
        name="kgrad_split",
        grid=(n // tr,),
        in_specs=[_rows(tr, wk), _rows(tr, MLA_HEADS * MLA_V), _rows(tr, LANES), _rows(tr, LANES)],
        out_specs=[_rows(tr, wk), _rows(tr, LANES)],
        out_shape=[jax.ShapeDtypeStruct((n, wk), BF16), jax.ShapeDtypeStruct((n, LANES), F32)],
        compiler_params=_params(("parallel",)),
    )(dka, dva, cos_a, ss_a)


def _qprep_fwd(pq, qg, gq, cos_b, ss_b, tr):
    n = pq.shape[0]
    nq = GQA_HEADS * GQA_HEAD_DIM

    def body(p_ref, qg_ref, gq_ref, cb, sb, cq_ref, qb_ref):
        xh, _ = _rms(p_ref[:, :MLA_Q_LORA])
        cq_ref[...] = (xh * qg_ref[...]).astype(BF16)
        for h in range(GQA_HEADS):
            lo = MLA_Q_LORA + h * GQA_HEAD_DIM
            qh, _ = _rms(p_ref[:, lo : lo + GQA_HEAD_DIM])
            qb_ref[:, h * GQA_HEAD_DIM : (h + 1) * GQA_HEAD_DIM] = _rope(qh * gq_ref[...], cb[...], sb[...], _QB).astype(BF16)

    return pl.pallas_call(
        body,
        name="qprep_fwd",
        grid=(n // tr,),
        in_specs=[_rows(tr, QC), _bcast(MLA_Q_LORA), _bcast(GQA_HEAD_DIM), _rows(tr, LANES), _rows(tr, LANES)],
        out_specs=[_rows(tr, MLA_Q_LORA), _rows(tr, nq)],
        out_shape=[jax.ShapeDtypeStruct((n, MLA_Q_LORA), BF16), jax.ShapeDtypeStruct((n, nq), BF16)],
        compiler_params=_params(("parallel",)),
    )(pq, qg, gq, cos_b, ss_b)


def _qprep_bwd(pq, dcq, dqb, qg, gq, cos_b, ss_b, tr):
    n = pq.shape[0]
    nq = GQA_HEADS * GQA_HEAD_DIM

    def body(p_ref, dcq_ref, dqb_ref, qg_ref, gq_ref, cb, sb, dp_ref, dqg_ref, dgq_ref):
        _acc_init(pl.program_id(0), [dqg_ref, dgq_ref])
        xh, r = _rms(p_ref[:, :MLA_Q_LORA])
        dn = dcq_ref[...]
        dqg_ref[...] += _csum(dn * xh)
        dp_ref[:, :MLA_Q_LORA] = _rms_bwd(xh, r, dn * qg_ref[...]).astype(BF16)
        for h in range(GQA_HEADS):
            lo = MLA_Q_LORA + h * GQA_HEAD_DIM
            qh, rq = _rms(p_ref[:, lo : lo + GQA_HEAD_DIM])
            dq = _rope_t(dqb_ref[:, h * GQA_HEAD_DIM : (h + 1) * GQA_HEAD_DIM], cb[...], sb[...], _QB)
            dgq_ref[...] += _csum(dq * qh)
            dp_ref[:, lo : lo + GQA_HEAD_DIM] = _rms_bwd(qh, rq, dq * gq_ref[...]).astype(BF16)

    return pl.pallas_call(
        body,
        name="qprep_bwd",
        grid=(n // tr,),
        in_specs=[_rows(tr, QC), _rows(tr, MLA_Q_LORA), _rows(tr, nq), _bcast(MLA_Q_LORA), _bcast(GQA_HEAD_DIM),
                  _rows(tr, LANES), _rows(tr, LANES)],
        out_specs=[_rows(tr, QC), _bcast(MLA_Q_LORA), _bcast(GQA_HEAD_DIM)],
        out_shape=[jax.ShapeDtypeStruct((n, QC), BF16), jax.ShapeDtypeStruct((1, MLA_Q_LORA), F32),
                   jax.ShapeDtypeStruct((1, GQA_HEAD_DIM), F32)],
        compiler_params=_params(("arbitrary",)),
    )(pq, dcq, dqb, qg, gq, cos_b, ss_b)


_QA_COLS = MLA_HEADS * (MLA_NOPE + MLA_ROPE)


def _qrope_fwd(qa, cos_a, ss_a, tr):
    n = qa.shape[0]

    def body(q_ref, ca, sa, o_ref):
        for j in range(MLA_HEADS // 2):
            lo = 3 * j * LANES
            o_ref[:, lo : lo + 2 * LANES] = q_ref[:, lo : lo + 2 * LANES].astype(BF16)
            o_ref[:, lo + 2 * LANES : lo + 3 * LANES] = _rope(q_ref[:, lo + 2 * LANES : lo + 3 * LANES], ca[...], sa[...], _QA).astype(BF16)

    return pl.pallas_call(
        body,
        name="qrope_fwd",
        grid=(n // tr,),
        in_specs=[_rows(tr, _QA_COLS), _rows(tr, LANES), _rows(tr, LANES)],
        out_specs=_rows(tr, _QA_COLS),
        out_shape=jax.ShapeDtypeStruct((n, _QA_COLS), BF16),
        compiler_params=_params(("parallel",)),
    )(qa, cos_a, ss_a)


def _qrope_bwd(dq2, cos_a, ss_a, tr):
    n = dq2.shape[0]

    def body(d_ref, ca, sa, o_ref):
        for j in range(MLA_HEADS // 2):
            lo = 3 * j * LANES
            h0, h1 = 2 * j, 2 * j + 1
            o_ref[:, lo : lo + LANES] = d_ref[:, 2 * h0 * LANES : (2 * h0 + 1) * LANES].astype(BF16)
            o_ref[:, lo + LANES : lo + 2 * LANES] = d_ref[:, 2 * h1 * LANES : (2 * h1 + 1) * LANES].astype(BF16)
            pe = d_ref[:, (2 * h0 + 1) * LANES : (2 * h0 + 2) * LANES] + d_ref[:, (2 * h1 + 1) * LANES : (2 * h1 + 2) * LANES]
            o_ref[:, lo + 2 * LANES : lo + 3 * LANES] = _rope_t(pe, ca[...], sa[...], _QA).astype(BF16)

    return pl.pallas_call(
        body,
        name="qrope_bwd",
        grid=(n // tr,),
        in_specs=[_rows(tr, MLA_HEADS * 2 * LANES), _rows(tr, LANES), _rows(tr, LANES)],
        out_specs=_rows(tr, _QA_COLS),
        out_shape=jax.ShapeDtypeStruct((n, _QA_COLS), BF16),
        compiler_params=_params(("parallel",)),
    )(dq2, cos_a, ss_a)


def _cat(refs):
    vals = [r[...] for r in refs]
    return vals[0] if len(vals) == 1 else jnp.concatenate(vals, axis=-1)


LOG2E = 1.4426950408889634


def _attn_fwd(qparts, kparts, vpart, n_heads, group, dv, scale, name, tq, after=None):
    T, Tk = qparts[0][0].shape[0], kparts[0][0].shape[0]
    nq_, nk_ = len(qparts), len(kparts)
    sub = min(tq, 256)
    c2 = scale * LOG2E

    def body(*refs):
        q_refs, k_refs = refs[:nq_], refs[nq_ : nq_ + nk_]
        v_ref = refs[nq_ + nk_]
        o_ref, lse_ref = refs[-2:]
        k = _cat(k_refs)
        v = v_ref[...]
        for r0 in range(0, tq, sub):
            q = _cat([r.at[r0 : r0 + sub, :] for r in q_refs])
            s = lax.dot_general(q, k, _DIMS["NT"], preferred_element_type=F32)
            m = jnp.max(s, axis=-1, keepdims=True)
            p = jnp.exp2((s - m) * c2)
            l = jnp.sum(p, axis=-1, keepdims=True)
            acc = jnp.dot(p.astype(BF16), v, preferred_element_type=F32)
            o_ref[r0 : r0 + sub, :] = (acc * (1.0 / l)).astype(BF16)
            lse_ref[r0 : r0 + sub, :] = m * scale + jnp.log(l)

    in_specs = [pl.BlockSpec((tq, LANES), lambda h, i, f=f: (i, f(h))) for _, f in qparts]
    in_specs += [pl.BlockSpec((Tk, LANES), lambda h, i, f=f: (0, f(h // group))) for _, f in kparts]
    fv = vpart[1]
    in_specs.append(pl.BlockSpec((Tk, dv), lambda h, i: (0, fv(h // group))))
    args = [*[a for a, _ in qparts], *[a for a, _ in kparts], vpart[0]]
    if after is not None:
        in_specs.append(pl.BlockSpec(after.shape, lambda h, i: (0, 0)))
        args.append(after)
    return pl.pallas_call(
        body,
        name=name,
        grid=(n_heads, T // tq),
        in_specs=in_specs,
        out_specs=[pl.BlockSpec((tq, dv), lambda h, i: (i, h)), pl.BlockSpec((None, tq, 1), lambda h, i: (h, i, 0))],
        out_shape=[jax.ShapeDtypeStruct((T, n_heads * dv), BF16), jax.ShapeDtypeStruct((n_heads, T, 1), F32)],
        compiler_params=_params(("parallel", "parallel")),
    )(*args)


def _attn_bwd(qparts, kparts, vpart, o, do, lse, n_heads, group, dv, scale, name, tq):
    T, Tk = qparts[0][0].shape[0], kparts[0][0].shape[0]
    nq_, nk_ = len(qparts), len(kparts)
    dk_ = LANES * nq_
    n_kv = n_heads // group
    nblk = T // tq
    c2 = scale * LOG2E

    def head(hk, i):
        return hk * group + i // nblk

    sub = min(tq, 256)

    def body(*refs):
        q_refs = refs[:nq_]
        k = _cat(refs[nq_ : nq_ + nk_])
        v_ref, o_ref, do_ref, lse_ref, dq_ref, dk_ref, dv_ref = refs[nq_ + nk_ :]
        i = pl.program_id(1)
        _acc_init(i, [dk_ref, dv_ref])
        v = v_ref[...]
        dk_acc, dv_acc = None, None
        for r0 in range(0, tq, sub):
            rows = slice(r0, r0 + sub)
            q = _cat([r.at[rows, :] for r in q_refs])
            s = lax.dot_general(q, k, _DIMS["NT"], preferred_element_type=F32)
            p = jnp.exp2(s * c2 - lse_ref[rows, :] * LOG2E)
            dov = do_ref[rows, :]
            dp = lax.dot_general(dov, v, _DIMS["NT"], preferred_element_type=F32)
            delta = jnp.sum(dov.astype(F32) * o_ref[rows, :].astype(F32), axis=-1, keepdims=True)
            ds = (p * (dp - delta)).astype(BF16)
            dq_ref[rows, :] = jnp.dot(ds, k, preferred_element_type=F32) * scale
            dk_part = lax.dot_general(ds, q, _DIMS["TN"], preferred_element_type=F32)
            dv_part = lax.dot_general(p.astype(BF16), dov, _DIMS["TN"], preferred_element_type=F32)
            dk_acc = dk_part if dk_acc is None else dk_acc + dk_part
            dv_acc = dv_part if dv_acc is None else dv_acc + dv_part
        dk_ref[...] += dk_acc
        dv_ref[...] += dv_acc

        @pl.when(i == group * nblk - 1)
        def _():
            dk_ref[...] *= scale

    in_specs = [pl.BlockSpec((tq, LANES), lambda hk, i, f=f: (i % nblk, f(head(hk, i)))) for _, f in qparts]
    in_specs += [pl.BlockSpec((Tk, LANES), lambda hk, i, f=f: (0, f(hk))) for _, f in kparts]
    fv = vpart[1]
    in_specs.append(pl.BlockSpec((Tk, dv), lambda hk, i: (0, fv(hk))))
    in_specs += [pl.BlockSpec((tq, dv), lambda hk, i: (i % nblk, head(hk, i)))] * 2
    in_specs.append(pl.BlockSpec((None, tq, 1), lambda hk, i: (head(hk, i), i % nblk, 0)))
    return pl.pallas_call(
        body,
        name=name,
        grid=(n_kv, group * nblk),
        in_specs=in_specs,
        out_specs=[pl.BlockSpec((tq, dk_), lambda hk, i: (i % nblk, head(hk, i))),
                   pl.BlockSpec((Tk, dk_), lambda hk, i: (0, hk)),
                   pl.BlockSpec((Tk, dv), lambda hk, i: (0, hk))],
        out_shape=[jax.ShapeDtypeStruct((T, n_heads * dk_), F32), jax.ShapeDtypeStruct((Tk, n_kv * dk_), F32),
                   jax.ShapeDtypeStruct((Tk, n_kv * dv), F32)],
        compiler_params=_params(("parallel", "arbitrary")),
    )(*[a for a, _ in qparts], *[a for a, _ in kparts], vpart[0], o, do, lse)


def _gates_fwd(pg, ya, yb, tr):
    n, d = ya.shape

    def body(pg_ref, ya_ref, yb_ref, o_ref):
        ga = jax.nn.sigmoid(pg_ref[:, :d].astype(F32))
        gb = jax.nn.sigmoid(pg_ref[:, d:].astype(F32))
        o_ref[...] = (ga * ya_ref[...].astype(F32) + gb * yb_ref[...].astype(F32)).astype(BF16)

    return pl.pallas_call(
        body,
        name="gates_fwd",
        grid=(n // tr,),
        in_specs=[_rows(tr, 2 * d), _rows(tr, d), _rows(tr, d)],
        out_specs=_rows(tr, d),
        out_shape=jax.ShapeDtypeStruct((n, d), BF16),
        compiler_params=_params(("parallel",)),
    )(pg, ya, yb)


def _gates_bwd(dm, pg, ya, yb, tr):
    n, d = ya.shape

    def body(dm_ref, pg_ref, ya_ref, yb_ref, dya_ref, dyb_ref, dpg_ref):
        dmv = dm_ref[...].astype(F32)
        ga = jax.nn.sigmoid(pg_ref[:, :d].astype(F32))
        gb = jax.nn.sigmoid(pg_ref[:, d:].astype(F32))
        dya_ref[...] = (dmv * ga).astype(BF16)
        dyb_ref[...] = (dmv * gb).astype(BF16)
        dpg_ref[:, :d] = (dmv * ya_ref[...].astype(F32) * ga * (1.0 - ga)).astype(BF16)
        dpg_ref[:, d:] = (dmv * yb_ref[...].astype(F32) * gb * (1.0 - gb)).astype(BF16)

    return pl.pallas_call(
        body,
        name="gates_bwd",
        grid=(n // tr,),
        in_specs=[_rows(tr, d), _rows(tr, 2 * d), _rows(tr, d), _rows(tr, d)],
        out_specs=[_rows(tr, d), _rows(tr, d), _rows(tr, 2 * d)],
        out_shape=[jax.ShapeDtypeStruct((n, d), BF16), jax.ShapeDtypeStruct((n, d), BF16), jax.ShapeDtypeStruct((n, 2 * d), BF16)],
        compiler_params=_params(("parallel",)),
    )(dm, pg, ya, yb)


def _resid_norm2_fwd(x2d, att, g1, n2g, sh2, sc2, tr):
    n, d = x2d.shape

    def body(x_ref, a_ref, g1_ref, g_ref, sh_ref, sc_ref, x1_ref, z_ref):
        x1 = x_ref[...] + g1_ref[...] * a_ref[...]
        x1_ref[...] = x1
        xh, _ = _rms(x1)
        z_ref[...] = ((xh * g_ref[...]) * (1.0 + sc_ref[...]) + sh_ref[...]).astype(BF16)

    return pl.pallas_call(
        body,
        name="resid_norm2_fwd",
        grid=(n // tr,),
        in_specs=[_rows(tr, d), _rows(tr, d)] + [_bcast(d)] * 4,
        out_specs=[_rows(tr, d), _rows(tr, d)],
        out_shape=[jax.ShapeDtypeStruct((n, d), F32), jax.ShapeDtypeStruct((n, d), BF16)],
        compiler_params=_params(("parallel",)),
    )(x2d, att, g1, n2g, sh2, sc2)


def _resid_norm2_bwd(dz2, x1, dx2, att, n2g, sc2, g1, tr):
    n, d = x1.shape

    def body(dz_ref, x1_ref, dx2_ref, a_ref, g_ref, sc_ref, g1_ref, dx1_ref, da_ref, dg_ref, dsh_ref, dsc_ref, dg1_ref):
        _acc_init(pl.program_id(0), [dg_ref, dsh_ref, dsc_ref, dg1_ref])
        xh, r = _rms(x1_ref[...])
        dzv = dz_ref[...]
        gv = g_ref[...]
        dsc_ref[...] += _csum(dzv * (xh * gv))
        dsh_ref[...] += _csum(dzv)
        dh = dzv * (1.0 + sc_ref[...])
        dg_ref[...] += _csum(dh * xh)
        dx1 = _rms_bwd(xh, r, dh * gv) + dx2_ref[...]
        dx1_ref[...] = dx1
        dg1_ref[...] += _csum(dx1 * a_ref[...])
        da_ref[...] = (dx1 * g1_ref[...]).astype(BF16)

    return pl.pallas_call(
        body,
        name="resid_norm2_bwd",
        grid=(n // tr,),
        in_specs=[_rows(tr, d)] * 4 + [_bcast(d)] * 3,
        out_specs=[_rows(tr, d), _rows(tr, d)] + [_bcast(d)] * 4,
        out_shape=[jax.ShapeDtypeStruct((n, d), F32), jax.ShapeDtypeStruct((n, d), BF16)] + [jax.ShapeDtypeStruct((1, d), F32)] * 4,
        compiler_params=_params(("arbitrary",)),
    )(dz2, x1, dx2, att, n2g, sc2, g1)


def _edges(shape):
    row = lax.broadcasted_iota(jnp.int32, shape, 0)
    return row == 0, row == shape[0] - 1


def _shifts(u, edges):
    n = u.shape[0]
    return jnp.where(edges[0], 0.0, pltpu.roll(u, 1, 0)), jnp.where(edges[1], 0.0, pltpu.roll(u, n - 1, 0))


def _conv3(u, prev, nxt, w_ref, b_ref):
    return b_ref[...] + w_ref[0:1, :] * prev + w_ref[1:2, :] * u + w_ref[2:3, :] * nxt


def _ffn_up_conv(z, wup_t, cw, cb, tc, after):
    n, d = z.shape
    f = wup_t.shape[0] // 2
    nb = f // tc

    def body(z_ref, wa_ref, wb_ref, cwa, cwb, cba, cbb, after_ref, ua_ref, ub_ref, h_ref):
        w = jnp.concatenate([wa_ref[...], wb_ref[...]], axis=0)
        u = lax.dot_general(z_ref[...], w, _DIMS["NT"], preferred_element_type=F32).astype(BF16)
        ua_ref[...] = u[:, :tc]
        ub_ref[...] = u[:, tc:]
        edges = _edges((n, tc))
        ua = u[:, :tc].astype(F32)
        ub = u[:, tc:].astype(F32)
        a = _conv3(ua, *_shifts(ua, edges), cwa, cba)
        b = _conv3(ub, *_shifts(ub, edges), cwb, cbb)
        h_ref[...] = (a * jax.nn.sigmoid(a) * b).astype(BF16)

    col = lambda rows, off: pl.BlockSpec((rows, tc), lambda i: (0, i + off))
    w_rows = lambda off: pl.BlockSpec((tc, d), lambda i: (i + off, 0))
    return pl.pallas_call(
        body,
        name="ffn_up_conv",
        grid=(nb,),
        in_specs=[pl.BlockSpec((n, d), lambda i: (0, 0)), w_rows(0), w_rows(nb), col(3, 0), col(3, nb), col(1, 0), col(1, nb),
                  pl.BlockSpec(after.shape, lambda i: (0, 0))],
        out_specs=[col(n, 0)] * 3,
        out_shape=[jax.ShapeDtypeStruct((n, f), BF16)] * 3,
        compiler_params=_params(("parallel",)),
    )(z, wup_t, wup_t, cw, cw, cb, cb, after)


def _conv_bwd(u_a, u_b, dh, cw, cb, tc):
    n, f = u_a.shape
    nb = f // tc

    def part(uv, prev, nxt, duc, edges, w_ref, du_ref, dw_ref, db_ref):
        db_ref[...] = _csum(duc)
        dw_ref[0:1, :] = _csum(duc * prev)
        dw_ref[1:2, :] = _csum(duc * uv)
        dw_ref[2:3, :] = _csum(duc * nxt)
        d_prev, d_next = _shifts(duc, edges)
        du_ref[...] = (w_ref[0:1, :] * d_next + w_ref[1:2, :] * duc + w_ref[2:3, :] * d_prev).astype(BF16)

    def body(ua_ref, ub_ref, dh_ref, wa_ref, wb_ref, ba_ref, bb_ref, dua_ref, dub_ref, dwa_ref, dwb_ref, dba_ref, dbb_ref):
        edges = _edges((n, tc))
        ua = ua_ref[...].astype(F32)
        ub = ub_ref[...].astype(F32)
        sa = _shifts(ua, edges)
        sb = _shifts(ub, edges)
        a = _conv3(ua, *sa, wa_ref, ba_ref)
        b = _conv3(ub, *sb, wb_ref, bb_ref)
        dhv = dh_ref[...].astype(F32)
        sg = jax.nn.sigmoid(a)
        da = dhv * b * (sg * (1.0 + a * (1.0 - sg)))
        db = dhv * (a * sg)
        part(ua, *sa, da, edges, wa_ref, dua_ref, dwa_ref, dba_ref)
        part(ub, *sb, db, edges, wb_ref, dub_ref, dwb_ref, dbb_ref)

    col = lambda rows, off: pl.BlockSpec((rows, tc), lambda i: (0, i + off))
    return pl.pallas_call(
        body,
        name="conv_bwd",
        grid=(nb,),
        in_specs=[col(n, 0), col(n, 0), col(n, 0), col(3, 0), col(3, nb), col(1, 0), col(1, nb)],
        out_specs=[col(n, 0), col(n, 0), col(3, 0), col(3, 0), col(1, 0), col(1, 0)],
        out_shape=[jax.ShapeDtypeStruct((n, f), BF16)] * 2 + [jax.ShapeDtypeStruct((3, f), F32)] * 2 + [jax.ShapeDtypeStruct((1, f), F32)] * 2,
        compiler_params=_params(("parallel",)),
    )(u_a, u_b, dh, cw, cw, cb, cb)


def _loss_head(x1, f, g2, fg, tgt, tr):
    n, d = x1.shape

    def body(x1_ref, f_ref, g2_ref, fg_ref, t_ref, sq_ref, dx2_ref, dfg_ref, dg2_ref, df_ref):
        _acc_init(pl.program_id(0), [sq_ref, dfg_ref, dg2_ref])
        fv = f_ref[...]
        xh, r = _rms(x1_ref[...] + g2_ref[...] * fv)
        err = xh * fg_ref[...] - t_ref[...]
        sq_ref[...] += _csum(err * err)
        dy = err * (1.0 / d)
        dfg_ref[...] += _csum(dy * xh)
        dx2 = _rms_bwd(xh, r, dy * fg_ref[...])
        dx2_ref[...] = dx2
        dg2_ref[...] += _csum(dx2 * fv)
        df_ref[...] = (dx2 * g2_ref[...]).astype(BF16)

    return pl.pallas_call(
        body,
        name="loss_head",
        grid=(n // tr,),
        in_specs=[_rows(tr, d), _rows(tr, d), _bcast(d), _bcast(d), _rows(tr, d)],
        out_specs=[_bcast(d), _rows(tr, d), _bcast(d), _bcast(d), _rows(tr, d)],
        out_shape=[jax.ShapeDtypeStruct((1, d), F32), jax.ShapeDtypeStruct((n, d), F32), jax.ShapeDtypeStruct((1, d), F32),
                   jax.ShapeDtypeStruct((1, d), F32), jax.ShapeDtypeStruct((n, d), BF16)],
        compiler_params=_params(("arbitrary",)),
    )(x1, f, g2, fg, tgt)


def _sum_slots(g, name):
    s, r, w = g.shape

    def body(g_ref, o_ref):
        acc = g_ref[0]
        for k in range(1, s):
            acc = acc + g_ref[k]
        o_ref[...] = acc

    return pl.pallas_call(body, name=name, out_shape=jax.ShapeDtypeStruct((r, w), F32))(g)


def _silu_grad_mul(ds, cvec):
    def body(d_ref, c_ref, o_ref):
        cv = c_ref[...]
        sg = jax.nn.sigmoid(cv)
        o_ref[...] = d_ref[...] * (sg * (1.0 + cv * (1.0 - sg)))

    return pl.pallas_call(body, name="silu_grad_mul", out_shape=jax.ShapeDtypeStruct(ds.shape, F32))(ds, cvec)


def _adamw_update(wv, gv, mv, vv, d_ref, mo_ref, vo_ref):
    mn = ADAM_B1 * mv + (1.0 - ADAM_B1) * gv
    vn = ADAM_B2 * vv + (1.0 - ADAM_B2) * (gv * gv)
    mo_ref[...] = mn
    vo_ref[...] = vn
    m_hat = mn / (1.0 - ADAM_B1**ADAM_STEP)
    v_hat = vn / (1.0 - ADAM_B2**ADAM_STEP)
    d_ref[...] = -ADAM_LR * (m_hat / (jnp.sqrt(v_hat) + ADAM_EPS) + ADAM_WD * wv)


def _adamw_many(ws, gs, ms, vs, name):
    n = len(ws)

    def body(*refs):
        for k in range(n):
            w_ref, g_ref, m_ref, v_ref = (refs[q * n + k] for q in range(4))
            d_ref, mo_ref, vo_ref = (refs[(4 + q) * n + k] for q in range(3))
            _adamw_update(w_ref[...], g_ref[...], m_ref[...], v_ref[...], d_ref, mo_ref, vo_ref)

    res = pl.pallas_call(body, name=name, out_shape=[jax.ShapeDtypeStruct(w.shape, F32) for w in ws] * 3)(*ws, *gs, *ms, *vs)
    return res[:n], res[n : 2 * n], res[2 * n :]


def _adamw(w, g, m, v, name, g_transposed=False, g_sibling=None):
    r, cdim = w.shape
    halves = g_sibling is not None
    if g_transposed:
        tr = _pick(r, 1024, LANES)
        tc = _pick(cdim // 2 if halves else cdim, max(LANES, (1 << 19) // tr))
        per_half = (cdim // 2) // tc
    else:
        tr = _pick(r // 2 if halves else r, 1024, 8)
        tc = _pick(cdim, max(LANES, (1 << 19) // tr))
        per_half = (r // 2) // tr
    emit_g = g_transposed or halves

    def body(w_ref, g_ref, *rest):
        m_ref, v_ref = rest[halves : halves + 2]
        outs = rest[halves + 2 :]
        gv = g_ref[...]
        if halves:
            along = pl.program_id(1 if g_transposed else 0)
            gv = jnp.where(along // per_half == lax.axis_index("c"), gv, rest[0][...])
        if g_transposed:
            gv = gv.T
        if emit_g:
            outs[0][...] = gv
        _adamw_update(w_ref[...], gv, m_ref[...], v_ref[...], *outs[-3:])

    spec = pl.BlockSpec((tr, tc), lambda i, j: (i, j))
    if g_transposed:
        g_spec = pl.BlockSpec((tc, tr), lambda i, j: (j % per_half if halves else j, i))
    else:
        g_spec = pl.BlockSpec((tr, tc), lambda i, j: (i % per_half if halves else i, j))
    n_out = 3 + emit_g
    res = pl.pallas_call(
        body,
        name=name,
        grid=(r // tr, cdim // tc),
        in_specs=[spec, g_spec] + [g_spec] * halves + [spec, spec],
        out_specs=[spec] * n_out,
        out_shape=[jax.ShapeDtypeStruct((r, cdim), F32)] * n_out,
        compiler_params=_params(("parallel", "parallel")),
    )(w, g, *([g_sibling] if halves else []), m, v)
    return res if emit_g else [g, *res]


def _place():
    return lax.axis_index("x"), lax.axis_index("y"), lax.axis_index("c")


def _remote(src, dst, send_sem, recv_sem, dev):
    return pltpu.make_async_remote_copy(src_ref=src, dst_ref=dst, send_sem=send_sem, recv_sem=recv_sem, device_id=dev, device_id_type=MESH)


ANY = pl.BlockSpec(memory_space=pl.ANY)


def _all_gather_small(v, name):
    r, w = v.shape

    def body(v_ref, o_ref, send, recv, lsem):
        x, y, c = _place()
        me = 4 * x + 2 * y + c
        mine = pltpu.make_async_copy(v_ref, o_ref.at[me], lsem)
        mine.start()
        sent = []
        for k in range(1, 8):
            px, py, pc = x ^ (k >> 2), y ^ ((k >> 1) & 1), c ^ (k & 1)
            cp = _remote(v_ref, o_ref.at[me], send.at[k - 1], recv.at[k - 1], (px, py, pc))
            cp.start()
            sent.append(cp)
        for k in range(1, 8):
            px, py, pc = x ^ (k >> 2), y ^ ((k >> 1) & 1), c ^ (k & 1)
            slot = o_ref.at[4 * px + 2 * py + pc]
            _remote(slot, slot, send.at[k - 1], recv.at[k - 1], (x, y, c)).wait_recv()
        for cp in sent:
            cp.wait_send()
        mine.wait()

    return pl.pallas_call(
        body,
        name=name,
        out_shape=jax.ShapeDtypeStruct((8, r, w), F32),
        in_specs=[pl.BlockSpec(memory_space=pltpu.VMEM)],
        out_specs=pl.BlockSpec(memory_space=pltpu.VMEM),
        scratch_shapes=[pltpu.SemaphoreType.DMA((7,)), pltpu.SemaphoreType.DMA((7,)), pltpu.SemaphoreType.DMA],
        compiler_params=pltpu.CompilerParams(vmem_limit_bytes=VMEM_LIMIT),
    )(v)


HBM = pl.BlockSpec(memory_space=pltpu.HBM)
SEM = pl.BlockSpec(memory_space=pltpu.SEMAPHORE)
EFFECT = pltpu.SideEffectType.DATAFLOW_SIDE_EFFECTING


def _other_chips(x, y):
    return [(1 - x, y), (x, 1 - y), (1 - x, 1 - y)]


def _bulk_start(name, srcs, land_shapes, n_copies, copies, after):
    n, m = len(srcs), len(land_shapes)

    def body(*refs):
        src_refs, land_refs = refs[:n], refs[n : n + m]
        send, recv = refs[n + m + 1], refs[n + m + 2]
        token = refs[-1]
        for k, (s, d, dev) in enumerate(copies(src_refs, land_refs)):
            _remote(s, d, send.at[k], recv.at[k], dev).start()
        token[...] = jnp.zeros_like(token)

    lands = [pltpu.with_memory_space_constraint(lax.empty(s.shape, s.dtype), pltpu.HBM) for s in land_shapes]
    out = pl.pallas_call(
        body,
        name=name,
        out_shape=(pltpu.SemaphoreType.DMA((n_copies,)), pltpu.SemaphoreType.DMA((n_copies,)),
                   *[pltpu.HBM(s.shape, s.dtype) for s in srcs], *[pltpu.HBM(s.shape, s.dtype) for s in land_shapes],
                   jax.ShapeDtypeStruct((8, LANES), F32)),
        in_specs=[HBM] * (n + m) + [ANY],
        out_specs=(SEM, SEM, *[HBM] * (n + m), pl.BlockSpec(memory_space=pltpu.VMEM)),
        input_output_aliases={i: 2 + i for i in range(n + m)},
        compiler_params=pltpu.CompilerParams(has_side_effects=EFFECT),
    )(*[pltpu.with_memory_space_constraint(s, pltpu.HBM) for s in srcs], *lands, after)
    return out[0], out[1], list(out[2 : 2 + n]), list(out[2 + n : 2 + n + m]), out[-1][0:1, 0:1]


def _bulk_wait(name, send, recv, srcs, lands, after, waits):
    n, m = len(srcs), len(lands)

    def body(*refs):
        src_refs, land_refs = refs[:n], refs[n : n + m]
        send_sem, recv_sem = refs[n + m], refs[n + m + 1]
        x, y, c = _place()
        for k, (s, d) in enumerate(waits(src_refs, land_refs)):
            cp = _remote(s, d, send_sem.at[k], recv_sem.at[k], (x, y, c))
            cp.wait_send()
            cp.wait_recv()

    out = pl.pallas_call(
        body,
        name=name,
        out_shape=tuple(pltpu.HBM(s.shape, s.dtype) for s in (*srcs, *lands)),
        in_specs=[HBM] * (n + m) + [SEM, SEM, ANY],
        out_specs=tuple([HBM] * (n + m)),
        input_output_aliases={i: i for i in range(n + m)},
        compiler_params=pltpu.CompilerParams(has_side_effects=EFFECT),
    )(*srcs, *lands, send, recv, after)
    return list(out[:n]), list(out[n:])


def _gather_start(shards, after, name):
    def copies(src, land):
        x, y, c = _place()
        j = 2 * x + y
        return [(src[a].at[c], land[a].at[j, c], (px, py, c)) for a in range(len(shards)) for px, py in _other_chips(x, y)]

    shapes = [jax.ShapeDtypeStruct((4,) + s.shape, s.dtype) for s in shards]
    return _bulk_start(name, shards, shapes, 3 * len(shards), copies, after)


def _gather_wait(started, after, name):
    send, recv, srcs, lands, _ = started

    def waits(src, land):
        x, y, c = _place()
        return [(src[a].at[c], land[a].at[2 * px + py, c]) for a in range(len(srcs)) for px, py in _other_chips(x, y)]

    return _bulk_wait(name, send, recv, srcs, lands, after, waits)


def _forward_halves(lands, name):
    n = len(lands)

    def body(*refs):
        bufs = refs[n : 2 * n]
        send, recv = refs[2 * n :]
        x, y, c = _place()
        started = []
        for a in range(n):
            for k, (px, py) in enumerate(_other_chips(x, y)):
                blk = bufs[a].at[2 * px + py, c]
                cp = _remote(blk, blk, send.at[3 * a + k], recv.at[3 * a + k], (x, y, 1 - c))
                cp.start()
                started.append(cp)
        for a in range(n):
            for k, (px, py) in enumerate(_other_chips(x, y)):
                blk = bufs[a].at[2 * px + py, 1 - c]
                _remote(blk, blk, send.at[3 * a + k], recv.at[3 * a + k], (x, y, c)).wait_recv()
        for cp in started:
            cp.wait_send()

    return pl.pallas_call(
        body,
        name=name,
        out_shape=[jax.ShapeDtypeStruct(b.shape, b.dtype) for b in lands],
        in_specs=[ANY] * n,
        out_specs=[ANY] * n,
        input_output_aliases={i: i for i in range(n)},
        scratch_shapes=[pltpu.SemaphoreType.DMA((3 * n,)), pltpu.SemaphoreType.DMA((3 * n,))],
    )(*lands)


def _forward_start(lands, after, name):
    def copies(src, _):
        x, y, c = _place()
        blocks = [src[a].at[2 * px + py, c] for a in range(len(lands)) for px, py in _other_chips(x, y)]
        return [(b, b, (x, y, 1 - c)) for b in blocks]

    return _bulk_start(name, lands, [], 3 * len(lands), copies, after)


def _forward_wait(started, after, name):
    send, recv, bufs, _, _ = started

    def waits(src, _):
        x, y, c = _place()
        return [(src[a].at[2 * px + py, c], src[a].at[2 * px + py, 1 - c]) for a in range(len(bufs)) for px, py in _other_chips(x, y)]

    return _bulk_wait(name, send, recv, bufs, [], after, waits)[0]


def _place_own(shards, lands):
    j = 2 * lax.axis_index("x") + lax.axis_index("y")
    full = [lax.dynamic_update_slice(b, s[None], (j, 0, 0, 0)) for b, s in zip(lands, shards)]
    return [f.reshape(4 * f.shape[2] * 2, f.shape[3]) for f in full]


def _gather_finish(started, after, tag):
    shards, lands = _gather_wait(started, after, "gather_wait_" + tag)
    return _place_own(shards, _forward_halves(lands, "gather_forward_" + tag))


def _gather_land(started, after, tag):
    shards, lands = _gather_wait(started, after, "gather_wait_" + tag)
    return shards, _forward_start(lands, shards[0], "forward_start_" + tag)


def _gather_done(landed, after, tag):
    shards, fwd = landed
    return _place_own(shards, _forward_wait(fwd, after, "forward_wait_" + tag))


def _swap_halves(grads, name):
    n = len(grads)

    def body(*refs):
        ins, outs = refs[:n], refs[n : 2 * n]
        send, recv = refs[2 * n :]
        x, y, c = _place()
        started = []
        for a in range(n):
            for s in range(4):
                cp = _remote(ins[a].at[s, 1 - c], outs[a].at[s], send.at[4 * a + s], recv.at[4 * a + s], (x, y, 1 - c))
                cp.start()
                started.append(cp)
        for cp in started:
            cp.wait_recv()
        for cp in started:
            cp.wait_send()

    return pl.pallas_call(
        body,
        name=name,
        out_shape=[jax.ShapeDtypeStruct((4,) + g.shape[2:], g.dtype) for g in grads],
        in_specs=[ANY] * n,
        out_specs=[ANY] * n,
        scratch_shapes=[pltpu.SemaphoreType.DMA((4 * n,)), pltpu.SemaphoreType.DMA((4 * n,))],
    )(*grads)


def _add_halves(grads, others, tag):
    outs = []
    for a, (g, o) in enumerate(zip(grads, others)):
        _, _, rh, cdim = g.shape
        tr = _pick(rh, 512, 16)

        def body(g_ref, o_ref, p_ref):
            p_ref[...] = (g_ref[...].astype(F32) + o_ref[...].astype(F32)).astype(BF16)

        outs.append(
            pl.pallas_call(
                body,
                name=f"add_halves_{tag}{a}",
                grid=(4, rh // tr),
                in_specs=[pl.BlockSpec((None, None, tr, cdim), lambda s, i: (s, lax.axis_index("c"), i, 0)),
                          pl.BlockSpec((None, tr, cdim), lambda s, i: (s, i, 0))],
                out_specs=pl.BlockSpec((None, tr, cdim), lambda s, i: (s, i, 0)),
                out_shape=jax.ShapeDtypeStruct((4, rh, cdim), BF16),
                compiler_params=_params(("parallel", "parallel")),
            )(g, o)
        )
    return outs


def _exchange_start(parts, after, name):
    def copies(src, land):
        x, y, c = _place()
        j = 2 * x + y
        return [(src[a].at[2 * px + py], land[a].at[j], (px, py, c)) for a in range(len(parts)) for px, py in _other_chips(x, y)]

    return _bulk_start(name, parts, [jax.ShapeDtypeStruct(p.shape, p.dtype) for p in parts], 3 * len(parts), copies, after)


def _exchange_finish(started, after, name):
    send, recv, srcs, lands, _ = started

    def waits(src, land):
        x, y, _ = _place()
        return [(src[a].at[2 * px + py], land[a].at[2 * px + py]) for a in range(len(srcs)) for px, py in _other_chips(x, y)]

    srcs, lands = _bulk_wait(name, send, recv, srcs, lands, after, waits)
    j = 2 * lax.axis_index("x") + lax.axis_index("y")
    return [lax.dynamic_update_slice(b, lax.dynamic_slice(p, (j, 0, 0), (1,) + p.shape[1:]), (j, 0, 0)) for b, p in zip(lands, srcs)]


def _sum_chips(recvd, tag):
    outs = []
    for a, g in enumerate(recvd):
        _, rh, cdim = g.shape
        tr = _pick(rh, 512, 16)

        def body(g_ref, o_ref):
            o_ref[...] = ((g_ref[0].astype(F32) + g_ref[1].astype(F32)) + g_ref[2].astype(F32)) + g_ref[3].astype(F32)

        outs.append(
            pl.pallas_call(
                body,
                name=f"sum_chips_{tag}{a}",
                grid=(rh // tr,),
                in_specs=[pl.BlockSpec((4, tr, cdim), lambda i: (0, i, 0))],
                out_specs=pl.BlockSpec((tr, cdim), lambda i: (i, 0)),
                out_shape=jax.ShapeDtypeStruct((rh, cdim), F32),
                compiler_params=_params(("parallel",)),
            )(g)
        )
    return outs


def _join_halves(halves, name):
    n = len(halves)

    def body(*refs):
        ins, outs = refs[:n], refs[n : 2 * n]
        send, recv = refs[2 * n :]
        x, y, c = _place()
        started = []
        for a in range(n):
            cp = _remote(ins[a], outs[a], send.at[a], recv.at[a], (x, y, 1 - c))
            cp.start()
            started.append(cp)
        for cp in started:
            cp.wait_recv()
        for cp in started:
            cp.wait_send()

    others = pl.pallas_call(
        body,
        name=name,
        out_shape=[jax.ShapeDtypeStruct(h.shape, h.dtype) for h in halves],
        in_specs=[ANY] * n,
        out_specs=[ANY] * n,
        scratch_shapes=[pltpu.SemaphoreType.DMA((n,)), pltpu.SemaphoreType.DMA((n,))],
    )(*halves)
    return list(zip(halves, others))


def _joined(mine, other):
    first = lax.axis_index("c") == 0
    return jnp.concatenate([jnp.where(first, mine, other), jnp.where(first, other, mine)], axis=0)


def _grad_views(grads):
    return [g.reshape(4, 2, g.shape[0] // 8, g.shape[1]) for g in grads]


def _scatter_start(grads, tag, after=None):
    views = _grad_views(grads)
    mine = _add_halves(views, _swap_halves(views, "swap_halves_" + tag), tag)
    return _exchange_start(mine, mine[-1] if after is None else after, "exchange_start_" + tag)


def _swap_start(grads, after, tag):
    views = _grad_views(grads)

    def copies(src, land):
        x, y, c = _place()
        return [(src[a].at[s, 1 - c], land[a].at[s], (x, y, 1 - c)) for a in range(len(views)) for s in range(4)]

    shapes = [jax.ShapeDtypeStruct((4,) + v.shape[2:], v.dtype) for v in views]
    return _bulk_start("swap_start_" + tag, views, shapes, 4 * len(views), copies, after)


def _scatter_start_after_swap(swapped, after, tag):
    send, recv, views, lands, _ = swapped

    def waits(src, land):
        c = lax.axis_index("c")
        return [(src[a].at[s, 1 - c], land[a].at[s]) for a in range(len(views)) for s in range(4)]

    views, others = _bulk_wait("swap_wait_" + tag, send, recv, views, lands, after, waits)
    mine = _add_halves(views, others, tag)
    return _exchange_start(mine, mine[-1], "exchange_start_" + tag)


def _join_start(halves, after, tag):
    def copies(src, land):
        x, y, c = _place()
        return [(src[a], land[a], (x, y, 1 - c)) for a in range(len(halves))]

    return _bulk_start("join_start_" + tag, halves, [jax.ShapeDtypeStruct(h.shape, h.dtype) for h in halves], len(halves), copies, after)


def _join_wait(started, after, tag):
    send, recv, halves, lands, _ = started
    halves, others = _bulk_wait("join_wait_" + tag, send, recv, halves, lands, after, lambda src, land: list(zip(src, land)))
    return list(zip(halves, others))


def _scatter_sums(started, after, tag):
    return _sum_chips(_exchange_finish(started, after, "exchange_wait_" + tag), tag)


def _scatter_finish(started, after, tag):
    return _join_halves(_scatter_sums(started, after, tag), "join_halves_" + tag)


def _t_bf16(w):
    return w.T.astype(BF16)


def kernel(x, c, ctx, c_ctx, w_ada, b_ada, norm1_g, w_in, mla_q_norm_g, w_q_up, mla_kv_norm_g, w_kv_up, gqa_q_norm_g, gqa_k_norm_g, w_br_a, w_br_b, w_out, norm2_g, w_up, conv_w, conv_b, w_down, final_norm_g, loss_target, m_c_ctx, m_w_ada, m_b_ada, m_norm1_g, m_w_in, m_mla_q_norm_g, m_w_q_up, m_mla_kv_norm_g, m_w_kv_up, m_gqa_q_norm_g, m_gqa_k_norm_g, m_w_br_a, m_w_br_b, m_w_out, m_norm2_g, m_w_up, m_conv_w, m_conv_b, m_w_down, m_final_norm_g, v_c_ctx, v_w_ada, v_b_ada, v_norm1_g, v_w_in, v_mla_q_norm_g, v_w_q_up, v_mla_kv_norm_g, v_w_kv_up, v_gqa_q_norm_g, v_gqa_k_norm_g, v_w_br_a, v_w_br_b, v_w_out, v_norm2_g, v_w_up, v_conv_w, v_conv_b, v_w_down, v_final_norm_g):
    T, D = x.shape[1], x.shape[2]
    C = ctx.shape[1]
    NA = w_ada.shape[2]
    NW = w_up.shape[2]
    F2 = 4 * NW
    FF = F2 // 2
    xi, yi, ci = _place()
    j = 2 * xi + yi
    me = 4 * xi + 2 * yi + ci
    tr = _pick(C, 128, 8)
    tq = _pick(T, 256)

    x2d, tgt, ctx2d = x[0], loss_target[0], ctx[0]
    fg = final_norm_g.reshape(1, D)
    cc = c_ctx.reshape(1, D)

    halve = lambda s: s.reshape(2, s.shape[0] // 2, s.shape[1])
    win_shard = halve(_t_bf16(w_in[0]))
    w0 = max(D, NW)
    pay = jnp.zeros((8, w0), F32).at[0:1, :D].set(c).at[1:4, :NW].set(conv_w[0])
    got = _all_gather_small(pay, "gather_cond")
    c_all = got[:, 0, :D]
    cw = jnp.concatenate([got[2 * s, 1:4, :NW] for s in range(4)], axis=1)
    s16 = jnp.concatenate([c_all, cc, jnp.zeros((7, D), F32)], axis=0)
    b_cols = lax.dynamic_slice(b_ada, (0, j * NA), (1, NA))
    ada_part = _mm(s16, w_ada[0], "NN", F32, "ada_fwd", act="silu", bias=b_cols)
    got = _all_gather_small(ada_part, "gather_ada")
    ada = jnp.concatenate([got[2 * s] for s in range(4)], axis=1)
    lat = lax.dynamic_slice(ada, (me, 0), (1, 6 * D))
    sh1, sc1, g1, sh2, sc2, g2 = [lat[:, k * D : (k + 1) * D] for k in range(6)]
    csh, csc = ada[8:9, :D], ada[8:9, D : 2 * D]

    ag_in = _gather_start([win_shard], got, "gather_start_in")
    t_in = ag_in[4]
    wq3 = (w_q_up[0] + t_in).reshape(MLA_Q_LORA, 2, MLA_NOPE + MLA_ROPE)
    wq_perm = jnp.concatenate([wq3[:, :, :MLA_NOPE].reshape(MLA_Q_LORA, -1), wq3[:, :, MLA_NOPE:].reshape(MLA_Q_LORA, -1)], axis=1)
    low = [_t_bf16(wq_perm), _t_bf16(w_kv_up[0] + t_in)]
    br = [_t_bf16(w_br_a[0] + t_in), _t_bf16(w_br_b[0] + t_in), (w_out[0] + t_in).astype(BF16)]
    ag_low = _gather_start([halve(s) for s in low], t_in, "gather_start_low")
    ag_br = _gather_start([halve(s) for s in br], ag_low[4], "gather_start_br")
    ag_up = _gather_start([halve(_t_bf16(w_up[0] + t_in))], ag_br[4], "gather_start_up")
    ag_down = _gather_start([halve((w_down[0] + t_in).astype(BF16))], ag_up[4], "gather_start_down")
    sh1 = sh1 + ag_down[4]

    cos_a, ss_a = _rope_tables(C, T, MLA_ROPE)
    cos_b, ss_b = _rope_tables(C, T, GQA_HEAD_DIM)
    lcos_a, lss_a, lcos_b, lss_b = cos_a[:T], ss_a[:T], cos_b[:T], ss_b[:T]

    z_all = _norm_mod_fwd(x2d, norm1_g, sh1, sc1, "norm1_lat_fwd", tr, out_rows=T + C)
    z_all = _norm_mod_fwd(ctx2d, norm1_g, csh, csc, "norm1_ctx_fwd", tr, base=z_all, out_off=T)
    (win_t,) = _gather_finish(ag_in, z_all, "in")
    kv_cols = KVP - LANES + MLA_ROPE
    e_kpe = MLA_KV_LORA + MLA_ROPE
    w_kvp = jnp.concatenate([win_t[:MLA_KV_LORA], win_t[e_kpe:kv_cols], win_t[MLA_KV_LORA:e_kpe], jnp.zeros((LANES - MLA_ROPE, D), BF16)], axis=0)

    pkv = _mm(z_all, w_kvp, "NT", F32, "proj_kv")
    pq = _mm(z_all, win_t, "NT", F32, "proj_q", m=T, n=QC, b_off=kv_cols)
    low_landed = _gather_land(ag_low, pq, "low")
    pg = _mm(z_all, win_t, "NT", BF16, "proj_g", m=T, n=2 * D, b_off=kv_cols + QC, after=low_landed[1][4])
    wq_t, wkv_t = _gather_done(low_landed, pg, "low")
    ckv_n, kb2, vb2, kpe2 = _kprep_fwd(pkv, mla_kv_norm_g, gqa_k_norm_g, cos_a, ss_a, cos_b, ss_b, tr)
    kv_up = _mm(ckv_n, wkv_t, "NT", BF16, "kv_up")
    cq_n, qb2 = _qprep_fwd(pq, mla_q_norm_g, gqa_q_norm_g, lcos_b, lss_b, tr)
    q_a = _mm(cq_n, wq_t, "NT", F32, "q_up")
    qar = _qrope_fwd(q_a, lcos_a, lss_a, tr)

    a_q = [(qar, lambda h: 3 * (h // 2) + h % 2), (qar, lambda h: 3 * (h // 2) + 2)]
    a_k = [(kv_up, lambda h: 2 * h), (kpe2, lambda h: h % 2)]
    a_v = (kv_up, lambda h: 2 * h + 1)
    a_scale = float(MLA_NOPE + MLA_ROPE) ** -0.5
    b_q = [(qb2, lambda h: h)]
    b_k = [(kb2, lambda h: h)]
    b_v = (vb2, lambda h: h)
    b_scale = float(GQA_HEAD_DIM) ** -0.5
    tq_f = _pick(T, 512)
    o_a, lse_a = _attn_fwd(a_q, a_k, a_v, MLA_HEADS, 1, MLA_V, a_scale, "attn_a_fwd", tq_f)
    br_landed = _gather_land(ag_br, o_a, "br")
    o_b, lse_b = _attn_fwd(b_q, b_k, b_v, GQA_HEADS, GQA_GROUP, GQA_HEAD_DIM, b_scale, "attn_b_fwd", tq_f, after=br_landed[1][4])
    wbra_t, wbrb_t, wout = _gather_done(br_landed, o_b, "br")
    up_landed = _gather_land(ag_up, o_b, "up")
    ya = _mm(o_a, wbra_t, "NT", BF16, "br_a", after=up_landed[1][4])
    yb = _mm(o_b, wbrb_t, "NT", BF16, "br_b")
    merged = _gates_fwd(pg, ya, yb, tr)
    att = _mm(merged, wout, "NN", F32, "out_proj")
    x1, z2 = _resid_norm2_fwd(x2d, att, g1, norm2_g, sh2, sc2, tr)
    (wup_t,) = _gather_done(up_landed, z2, "up")
    down_landed = _gather_land(ag_down, z2, "down")
    tc = _pick(FF, 128)
    u_a, u_b, hg = _ffn_up_conv(z2, wup_t, cw, conv_b, tc, down_landed[1][4])
    (wdown,) = _gather_done(down_landed, hg, "down")
    f = _mm(hg, wdown, "NN", F32, "ffn_down", tk=FF // 2)
    sq, dx2, d_fg, d_g2, df = _loss_head(x1, f, g2, fg, tgt, tr)
    loss = lax.psum(0.5 * jnp.sum(sq) / D, ("x", "y", "c"))

    dhg = _mm(df, wdown, "NT", BF16, "ffn_down_dx")
    g_wdown = _mm(hg, df, "TN", BF16, "ffn_down_dw", tm=FF // 4)
    du_a, du_b, dcw_a, dcw_b, dcb_a, dcb_b = _conv_bwd(u_a, u_b, dhg, cw, conv_b, tc)
    dz2 = _mm(du_a, wup_t, "NN", F32, "ffn_up_dx_a", tk=FF // 2)
    dz2 = _mm(du_b, wup_t, "NN", F32, "ffn_up_dx_b", b_off=FF, add=dz2, tk=FF // 2)
    g_wup_t = _mm(du_a, z2, "TN", BF16, "ffn_up_dw_a", out_rows=F2, tm=FF // 4)
    g_wup_t = _mm(du_b, z2, "TN", BF16, "ffn_up_dw_b", out_base=g_wup_t, out_off=FF, tm=FF // 4)
    sw_ffn = _swap_start([g_wdown, g_wup_t], sc2, "ffn")
    sc2 = sc2 + sw_ffn[4]
    dx1, datt, d_n2g, d_sh2, d_sc2, d_g1 = _resid_norm2_bwd(dz2, x1, dx2, att, norm2_g, sc2, g1, tr)

    dmerged = _mm(datt, wout, "NT", BF16, "out_proj_dx")
    rs_ffn = _scatter_start_after_swap(sw_ffn, dmerged, "ffn")
    lse_a = lse_a + rs_ffn[4]
    g_wout = _mm(merged, datt, "TN", BF16, "out_proj_dw")
    dya, dyb, dpg = _gates_bwd(dmerged, pg, ya, yb, tr)
    do_a = _mm(dya, wbra_t, "NN", BF16, "br_a_dx")
    g_wbra_t = _mm(dya, o_a, "TN", BF16, "br_a_dw")
    do_b = _mm(dyb, wbrb_t, "NN", BF16, "br_b_dx")
    g_wbrb_t = _mm(dyb, o_b, "TN", BF16, "br_b_dw")
    dqa2, dka2, dva2 = _attn_bwd(a_q, a_k, a_v, o_a, do_a, lse_a, MLA_HEADS, 1, MLA_V, a_scale, "attn_a_bwd", tq_f)
    dqb2, dkb2, dvb2 = _attn_bwd(b_q, b_k, b_v, o_b, do_b, lse_b, GQA_HEADS, GQA_GROUP, GQA_HEAD_DIM, b_scale, "attn_b_bwd", tq_f)
    dq_a = _qrope_bwd(dqa2, lcos_a, lss_a, tr)
    dcq_n = _mm(dq_a, wq_t, "NN", F32, "q_up_dx")
    g_wq_t = _mm(dq_a, cq_n, "TN", BF16, "q_up_dw")
    dpq, d_qg, d_gq = _qprep_bwd(pq, dcq_n, dqb2, mla_q_norm_g, gqa_q_norm_g, lcos_b, lss_b, tr)
    dkv_up, dkpe = _kgrad_split(dka2, dva2, cos_a, ss_a, tr)
    dckv_n = _mm(dkv_up, wkv_t, "NN", F32, "kv_up_dx")
    g_wkv_t = _mm(dkv_up, ckv_n, "TN", BF16, "kv_up_dw")
    rs_mix = _scatter_start([g_wq_t, g_wkv_t, g_wbra_t, g_wbrb_t, g_wout], "mix")
    dpkv, d_kvg, d_kg = _kprep_bwd(pkv, dckv_n, dkb2, dvb2, dkpe, mla_kv_norm_g + rs_mix[4], gqa_k_norm_g, cos_b, ss_b, tr)
    dz_kv = _mm(dpkv, w_kvp, "NN", F32, "proj_kv_dx")
    dz_lat = _mm(dpq, win_t, "NN", F32, "proj_q_dx", b_off=kv_cols, add=dz_kv)
    dz_lat = _mm(dpg, win_t, "NN", F32, "proj_g_dx", b_off=kv_cols + QC, add=dz_lat)
    _, d_n1g_c, d_csh, d_csc = _norm_mod_bwd(dz_kv, T // tr, ctx2d, norm1_g, csc, None, "norm1_ctx_bwd", tr)
    grad_x, d_n1g_l, d_sh1, d_sc1 = _norm_mod_bwd(dz_lat, 0, x2d, norm1_g, sc1, dx1, "norm1_lat_bwd", tr)

    zeros_d = jnp.zeros((1, D), F32)
    d_lat = jnp.concatenate([d_sh1, d_sc1, d_g1, d_sh2, d_sc2, d_g2], axis=1)
    d_ctx_part = jnp.concatenate([d_csh, d_csc], axis=1)
    flat = jnp.concatenate(
        [d_n1g_c + d_n1g_l, d_qg, d_kvg, d_gq, d_kg, d_n2g, dcb_a, dcb_b, d_fg,
         dcw_a.reshape(1, -1), dcw_b.reshape(1, -1), d_ctx_part, d_lat], axis=1)
    n_flat = flat.shape[1]
    n_rows = -(-n_flat // (8 * LANES)) * 8
    flat = jnp.pad(flat, ((0, 0), (0, n_rows * LANES - n_flat))).reshape(n_rows, LANES)
    got = _all_gather_small(flat, "gather_small_grads")
    tot = _sum_slots(got, "sum_small_grads").reshape(1, -1)
    sizes = [D, MLA_Q_LORA, MLA_KV_LORA, GQA_HEAD_DIM, GQA_HEAD_DIM, D, F2, D, 3 * FF, 3 * FF, 2 * D]
    offs = [0]
    for s in sizes:
        offs.append(offs[-1] + s)
    t_n1g, t_qg, t_kvg, t_gq, t_kg, t_n2g, t_cb, t_fg, t_cwa, t_cwb, t_ctx = [tot[:, offs[k] : offs[k + 1]] for k in range(len(sizes))]
    g_cw_full = jnp.concatenate([t_cwa.reshape(3, FF), t_cwb.reshape(3, FF)], axis=1)
    g_cw = lax.dynamic_slice(g_cw_full, (0, j * NW), (3, NW))
    d_lat_all = got.reshape(8, -1)[:, offs[-1] : offs[-1] + 6 * D]
    g16 = jnp.concatenate([d_lat_all, jnp.pad(t_ctx, ((0, 0), (0, 4 * D))), jnp.zeros((7, 6 * D), F32)], axis=0)
    g_b_ada = _sum_slots(g16.reshape(16, 1, 6 * D), "sum_b_ada")
    g16_cols = lax.dynamic_slice(g16, (0, j * NA), (16, NA))
    ds_part = _mm(g16_cols, w_ada[0], "NT", F32, "ada_dx")
    got = _all_gather_small(ds_part[8:16], "gather_ada_dx")
    ds_ctx = _sum_slots(jnp.stack([got[2 * s] for s in range(4)]), "sum_ada_dx")[0:1]
    g_c_ctx = _silu_grad_mul(ds_ctx, cc)

    g_kvp = _mm(dpkv, z_all, "TN", BF16, "proj_kv_dw")
    nk = MLA_KV_LORA + 2 * GQA_KV_HEADS * GQA_HEAD_DIM
    g_kv = jnp.concatenate([g_kvp[:MLA_KV_LORA], g_kvp[nk : nk + MLA_ROPE], g_kvp[MLA_KV_LORA:nk]], axis=0)
    g_win_t = _mm(dpq, z_all, "TN", BF16, "proj_q_dw", out_rows=kv_cols + QC + 2 * D, out_off=kv_cols, tm=QC // 2)
    g_win_t = _mm(dpg, z_all, "TN", BF16, "proj_g_dw", out_base=g_win_t, out_off=kv_cols + QC)
    g_win_t = lax.dynamic_update_slice(g_win_t, g_kv, (0, 0))
    rs_in = _scatter_start([g_win_t], "in", after=got)

    h_ffn = _scatter_sums(rs_ffn, rs_in[2][0], "ffn")
    j_ffn = _join_start(h_ffn, grad_x, "ffn")
    h_mix = _scatter_sums(rs_mix, j_ffn[2][0], "mix")
    j_mix = _join_start(h_mix, j_ffn[2][0], "mix")
    g_w_ada = _mm(s16, g16_cols, "TN", F32, "ada_dw", act="silu", after=j_mix[4])
    _, d_ada, m_ada, v_ada = _adamw(w_ada[0], g_w_ada, m_w_ada[0], v_w_ada[0], "adamw_w_ada")
    r_wdown, r_wup = _join_wait(j_ffn, d_ada, "ffn")
    r_wq, r_wkv, r_wbra, r_wbrb, r_wout = _join_wait(j_mix, d_ada, "mix")
    gq_p = _joined(*r_wq).T
    gq = jnp.concatenate([gq_p[:, : 2 * MLA_NOPE].reshape(MLA_Q_LORA, 2, MLA_NOPE), gq_p[:, 2 * MLA_NOPE :].reshape(MLA_Q_LORA, 2, MLA_ROPE)], axis=2)
    grads = {
        "c_ctx": g_c_ctx.reshape(D), "w_ada": g_w_ada[None], "b_ada": g_b_ada, "norm1_g": t_n1g,
        "mla_q_norm_g": t_qg, "w_q_up": gq.reshape(1, MLA_Q_LORA, -1), "mla_kv_norm_g": t_kvg, "w_kv_up": r_wkv,
        "gqa_q_norm_g": t_gq, "gqa_k_norm_g": t_kg, "w_br_a": r_wbra, "w_br_b": r_wbrb, "w_out": r_wout,
        "norm2_g": t_n2g, "w_up": r_wup, "conv_w": g_cw[None], "conv_b": t_cb, "w_down": r_wdown,
        "final_norm_g": t_fg.reshape(D),
    }
    arrives_transposed = ("w_kv_up", "w_br_a", "w_br_b", "w_up")
    arrives_halved = arrives_transposed + ("w_out", "w_down")
    weights = dict(c_ctx=c_ctx, w_ada=w_ada, b_ada=b_ada, norm1_g=norm1_g, w_in=w_in, mla_q_norm_g=mla_q_norm_g, w_q_up=w_q_up,
                   mla_kv_norm_g=mla_kv_norm_g, w_kv_up=w_kv_up, gqa_q_norm_g=gqa_q_norm_g, gqa_k_norm_g=gqa_k_norm_g, w_br_a=w_br_a,
                   w_br_b=w_br_b, w_out=w_out, norm2_g=norm2_g, w_up=w_up, conv_w=conv_w, conv_b=conv_b, w_down=w_down,
                   final_norm_g=final_norm_g)
    m_in = dict(c_ctx=m_c_ctx, w_ada=m_w_ada, b_ada=m_b_ada, norm1_g=m_norm1_g, w_in=m_w_in, mla_q_norm_g=m_mla_q_norm_g,
                w_q_up=m_w_q_up, mla_kv_norm_g=m_mla_kv_norm_g, w_kv_up=m_w_kv_up, gqa_q_norm_g=m_gqa_q_norm_g,
                gqa_k_norm_g=m_gqa_k_norm_g, w_br_a=m_w_br_a, w_br_b=m_w_br_b, w_out=m_w_out, norm2_g=m_norm2_g, w_up=m_w_up,
                conv_w=m_conv_w, conv_b=m_conv_b, w_down=m_w_down, final_norm_g=m_final_norm_g)
    v_in = dict(c_ctx=v_c_ctx, w_ada=v_w_ada, b_ada=v_b_ada, norm1_g=v_norm1_g, w_in=v_w_in, mla_q_norm_g=v_mla_q_norm_g,
                w_q_up=v_w_q_up, mla_kv_norm_g=v_mla_kv_norm_g, w_kv_up=v_w_kv_up, gqa_q_norm_g=v_gqa_q_norm_g,
                gqa_k_norm_g=v_gqa_k_norm_g, w_br_a=v_w_br_a, w_br_b=v_w_br_b, w_out=v_w_out, norm2_g=v_norm2_g, w_up=v_w_up,
                conv_w=v_conv_w, conv_b=v_conv_b, w_down=v_w_down, final_norm_g=v_final_norm_g)
    names = list(weights)
    big = [n for n in names if weights[n].ndim == 3 and weights[n].shape[1] >= 8]
    small = [n for n in names if n not in big]
    delta, new_m, new_v = {}, {}, {}

    def update(n):
        shp = weights[n].shape
        two_d = lambda a: a.reshape(shp[1], shp[2])
        g_t = n in arrives_transposed
        if n in arrives_halved:
            g_in, g_sib = grads[n]
        else:
            g_in, g_sib = two_d(grads[n].astype(F32)), None
        g_, d_, m_, v_ = _adamw(two_d(weights[n]), g_in, two_d(m_in[n]), two_d(v_in[n]), "adamw_" + n, g_transposed=g_t, g_sibling=g_sib)
        grads[n], delta[n], new_m[n], new_v[n] = g_.reshape(shp), d_.reshape(shp), m_.reshape(shp), v_.reshape(shp)

    delta["w_ada"], new_m["w_ada"], new_v["w_ada"] = d_ada[None], m_ada[None], v_ada[None]
    early = [n for n in big if n not in ("w_in", "w_ada")]
    for n in early:
        update(n)
    done = sum(delta[n][0, 0:1, 0:1] for n in early)
    ((g_mine, g_sib),) = _scatter_finish(rs_in, done, "in")
    g_, d_, m_, v_ = _adamw(w_in[0].T, g_mine, m_w_in[0].T, v_w_in[0].T, "adamw_w_in", g_sibling=g_sib)
    grads["w_in"], delta["w_in"], new_m["w_in"], new_v["w_in"] = g_.T[None], d_.T[None], m_.T[None], v_.T[None]
    grads = {n: grads[n].reshape(weights[n].shape).astype(F32) for n in names}

    slab = lambda tree: [tree[n].reshape(-1, LANES) for n in small]
    d_, m_, v_ = _adamw_many(slab(weights), slab(grads), slab(m_in), slab(v_in), "adamw_small")
    for k, n in enumerate(small):
        shp = weights[n].shape
        delta[n], new_m[n], new_v[n] = d_[k].reshape(shp), m_[k].reshape(shp), v_[k].reshape(shp)

    return (loss, grad_x[None], *[grads[n] for n in names], *[delta[n] for n in names], *[new_m[n] for n in names],
            *[new_v[n] for n in names])
```

```python
import math

import jax
import jax.numpy as jnp
from jax import lax
from jax.experimental import pallas as pl
from jax.experimental.pallas import tpu as pltpu

F32 = jnp.float32
BF16 = jnp.bfloat16
MESH = pl.DeviceIdType.MESH

NORM_EPS = 1e-6
ROPE_THETA = 10000.0
GRID_W = 64
MLA_HEADS = 8
MLA_Q_LORA = 768
MLA_KV_LORA = 512
MLA_NOPE = 128
MLA_ROPE = 64
MLA_V = 128
GQA_HEADS = 8
GQA_KV_HEADS = 2
GQA_HEAD_DIM = 128
GQA_GROUP = GQA_HEADS // GQA_KV_HEADS
LANES = 128
KVP = MLA_KV_LORA + 2 * GQA_KV_HEADS * GQA_HEAD_DIM + LANES
QC = MLA_Q_LORA + GQA_HEADS * GQA_HEAD_DIM

ADAM_LR = 0.001
ADAM_B1 = 0.9
ADAM_B2 = 0.999
ADAM_EPS = 1e-08
ADAM_WD = 0.01
ADAM_STEP = 10

VMEM_LIMIT = 56 * 1024 * 1024


def _pick(dim, target, mult=LANES):
    t = (min(target, dim) // mult) * mult
    while t >= mult:
        if dim % t == 0:
            return t
        t -= mult
    return dim


def _params(sem):
    return pltpu.CompilerParams(dimension_semantics=sem, vmem_limit_bytes=VMEM_LIMIT)


_DIMS = {"NN": (((1,), (0,)), ((), ())), "NT": (((1,), (1,)), ((), ())), "TN": (((0,), (0,)), ((), ()))}


MM_VMEM_BUDGET = 36 * 1024 * 1024


def _mm_tiles(M, N, K, sa, sb, so, tm, tn, tk):
    tm, tn, tk = _pick(M, tm), _pick(N, tn), _pick(K, tk)

    def need(t):
        return 2 * (tm * t * sa + t * tn * sb) + 2 * tm * tn * so + (tm * tn * 4 if t < K else 0)

    while need(tk) > MM_VMEM_BUDGET and tk > LANES:
        smaller = _pick(K, tk - LANES)
        if smaller >= tk:
            break
        tk = smaller
    return tm, tn, tk


def _window(block, index, offsets):
    if not any(offsets):
        return pl.BlockSpec(block, index)
    for t, o in zip(block, offsets):
        assert o % 16 == 0 and t % 16 == 0, (block, offsets)

    def at(i, j, k):
        return tuple(pl.multiple_of(o + p * t, math.gcd(o, t)) for p, t, o in zip(index(i, j, k), block, offsets))

    return pl.BlockSpec(tuple(pl.Element(t) for t in block), at)


def _mm(a, b, mode, out_dtype, name, m=None, n=None, k=None, b_off=0, add=None, out_rows=None, out_base=None, out_off=0,
        tm=1024, tn=1024, tk=2304, act=None, bias=None, after=None):
    if mode == "NN":
        M, K, N = m or a.shape[0], k or a.shape[1], b.shape[1]
    elif mode == "NT":
        M, K, N = m or a.shape[0], a.shape[1], n or b.shape[0]
    else:
        M, K, N = a.shape[1], k or a.shape[0], b.shape[1]
    tm, tn, tk = _mm_tiles(M, N, K, a.dtype.itemsize, b.dtype.itemsize, jnp.dtype(out_dtype).itemsize, tm, tn, tk)
    nk = K // tk
    dims = _DIMS[mode]
    n_in = 2 + (bias is not None) + (add is not None) + (out_base is not None) + (after is not None)

    def body(*refs):
        a_ref, b_ref = refs[:2]
        bias_ref = refs[2] if bias is not None else None
        add_ref = refs[2 + (bias is not None)] if add is not None else None
        o_ref = refs[n_in]
        av = a_ref[...]
        if act == "silu":
            av = av * jax.nn.sigmoid(av)
        part = lax.dot_general(av.astype(BF16), b_ref[...].astype(BF16), dims, preferred_element_type=F32)

        def finish(r):
            if bias is not None:
                r = r + bias_ref[...]
            if add is not None:
                r = r + add_ref[...]
            o_ref[...] = r.astype(out_dtype)

        if nk == 1:
            finish(part)
            return
        acc = refs[-1]
        k = pl.program_id(2)

        @pl.when(k == 0)
        def _():
            acc[...] = part

        @pl.when(jnp.logical_and(k > 0, k < nk - 1))
        def _():
            acc[...] += part

        @pl.when(k == nk - 1)
        def _():
            finish(acc[...] + part)

    a_spec = pl.BlockSpec((tk, tm), lambda i, j, k: (k, i)) if mode == "TN" else pl.BlockSpec((tm, tk), lambda i, j, k: (i, k))
    if mode == "NT":
        b_spec = _window((tn, tk), lambda i, j, k: (j, k), (b_off, 0))
    else:
        b_spec = _window((tk, tn), lambda i, j, k: (k, j), (b_off, 0))
    in_specs, args = [a_spec, b_spec], [a, b]
    if bias is not None:
        in_specs.append(pl.BlockSpec((1, tn), lambda i, j, k: (0, j)))
        args.append(bias)
    if add is not None:
        in_specs.append(pl.BlockSpec((tm, tn), lambda i, j, k: (i, j)))
        args.append(add)
    aliases = {}
    if after is not None:
        in_specs.append(pl.BlockSpec(after.shape, lambda i, j, k: (0, 0)))
        args.append(after)
    if out_base is not None:
        aliases = {len(args): 0}
        in_specs.append(ANY)
        args.append(out_base)
        out_rows = out_base.shape[0]
    return pl.pallas_call(
        body,
        name=name,
        grid=(M // tm, N // tn, nk),
        in_specs=in_specs,
        out_specs=_window((tm, tn), lambda i, j, k: (i, j), (out_off, 0)),
        out_shape=jax.ShapeDtypeStruct((out_rows or M, N), out_dtype),
        input_output_aliases=aliases,
        scratch_shapes=[pltpu.VMEM((tm, tn), F32)] if nk > 1 else [],
        compiler_params=_params(("parallel", "parallel", "arbitrary")),
    )(*args)


def _rms(x):
    r = lax.rsqrt(jnp.mean(x * x, axis=-1, keepdims=True) + NORM_EPS)
    return x * r, r


def _rms_bwd(xh, r, dxh):
    return r * (dxh - xh * jnp.mean(dxh * xh, axis=-1, keepdims=True))


def _swap(x, q):
    lane = lax.broadcasted_iota(jnp.int32, x.shape, 1)
    even = ((lane // q) % 2) == 0
    return jnp.where(even, pltpu.roll(x, LANES - q, 1), pltpu.roll(x, q, 1))


def _rope(x, cos, ss, q):
    return x * cos + _swap(x, q) * ss


def _rope_t(d, cos, ss, q):
    return d * cos + _swap(d * ss, q)


def _csum(x):
    return jnp.sum(x, axis=0, keepdims=True)


def _rows(tr, w, off=0):
    return pl.BlockSpec((tr, w), lambda i: (i + off, 0))


def _bcast(w):
    return pl.BlockSpec((1, w), lambda i: (0, 0))


def _acc_init(i, refs):
    @pl.when(i == 0)
    def _():
        for r in refs:
            r[...] = jnp.zeros_like(r)


def _rope_tables(n_ctx, n_lat, rot_dim):
    rows = n_lat // GRID_W
    row = jnp.repeat(jnp.arange(rows, dtype=F32), GRID_W)
    col = jnp.tile(jnp.arange(GRID_W, dtype=F32), rows)
    half = rot_dim // 2
    inv_freq = ROPE_THETA ** (-jnp.arange(0, half, 2, dtype=F32) / half)
    ar, ac = row[:, None] * inv_freq, col[:, None] * inv_freq
    cos = jnp.concatenate([jnp.cos(ar), jnp.cos(ar), jnp.cos(ac), jnp.cos(ac)], axis=-1)
    ss = jnp.concatenate([-jnp.sin(ar), jnp.sin(ar), -jnp.sin(ac), jnp.sin(ac)], axis=-1)
    cos = jnp.tile(cos, (1, LANES // rot_dim))
    ss = jnp.tile(ss, (1, LANES // rot_dim))
    cos = jnp.concatenate([cos, jnp.ones((n_ctx, LANES), F32)], axis=0)
    ss = jnp.concatenate([ss, jnp.zeros((n_ctx, LANES), F32)], axis=0)
    return cos, ss


def _norm_mod_fwd(x2d, g, sh, sc, name, tr, out_rows=None, base=None, out_off=0):
    n, d = x2d.shape

    def body(x_ref, g_ref, sh_ref, sc_ref, *rest):
        xh, _ = _rms(x_ref[...])
        rest[-1][...] = ((xh * g_ref[...]) * (1.0 + sc_ref[...]) + sh_ref[...]).astype(BF16)

    args, in_specs, aliases = [x2d, g, sh, sc], [_rows(tr, d), _bcast(d), _bcast(d), _bcast(d)], {}
    if base is not None:
        args.append(base)
        in_specs.append(ANY)
        aliases = {4: 0}
        out_rows = base.shape[0]
    return pl.pallas_call(
        body,
        name=name,
        grid=(n // tr,),
        in_specs=in_specs,
        out_specs=_rows(tr, d, out_off // tr),
        out_shape=jax.ShapeDtypeStruct((out_rows or n, d), BF16),
        input_output_aliases=aliases,
        compiler_params=_params(("parallel",)),
    )(*args)


def _norm_mod_bwd(dz, dz_off, x2d, g, sc, dres, name, tr):
    n, d = x2d.shape
    want_dx = dres is not None

    def body(*refs):
        if want_dx:
            dz_ref, x_ref, g_ref, sc_ref, dres_ref, dx_ref, dg_ref, dsh_ref, dsc_ref = refs
        else:
            dz_ref, x_ref, g_ref, sc_ref, dg_ref, dsh_ref, dsc_ref = refs
        _acc_init(pl.program_id(0), [dg_ref, dsh_ref, dsc_ref])
        xh, r = _rms(x_ref[...])
        dzv = dz_ref[...]
        gv = g_ref[...]
        dsc_ref[...] += _csum(dzv * (xh * gv))
        dsh_ref[...] += _csum(dzv)
        dh = dzv * (1.0 + sc_ref[...])
        dg_ref[...] += _csum(dh * xh)
        if want_dx:
            dx_ref[...] = _rms_bwd(xh, r, dh * gv) + dres_ref[...]

    in_specs = [_rows(tr, d, dz_off), _rows(tr, d), _bcast(d), _bcast(d)]
    args = [dz, x2d, g, sc]
    out_specs = [_bcast(d)] * 3
    out_shape = [jax.ShapeDtypeStruct((1, d), F32)] * 3
    if want_dx:
        in_specs.append(_rows(tr, d))
        args.append(dres)
        out_specs = [_rows(tr, d)] + out_specs
        out_shape = [jax.ShapeDtypeStruct((n, d), F32)] + out_shape
    res = pl.pallas_call(
        body,
        name=name,
        grid=(n // tr,),
        in_specs=in_specs,
        out_specs=out_specs,
        out_shape=out_shape,
        compiler_params=_params(("arbitrary",)),
    )(*args)
    return res if want_dx else (None, *res)


_QA, _QB = MLA_ROPE // 4, GQA_HEAD_DIM // 4


def _kprep_fwd(pkv, kvg, kg, cos_a, ss_a, cos_b, ss_b, tr):
    n = pkv.shape[0]
    nb = GQA_KV_HEADS * GQA_HEAD_DIM

    def body(p_ref, kvg_ref, kg_ref, ca, sa, cb, sb, ckv_ref, kb_ref, vb_ref, kpe_ref):
        p = p_ref[...]
        xh, _ = _rms(p[:, :MLA_KV_LORA])
        ckv_ref[...] = (xh * kvg_ref[...]).astype(BF16)
        for e in range(GQA_KV_HEADS):
            lo = MLA_KV_LORA + e * GQA_HEAD_DIM
            kh, _ = _rms(p[:, lo : lo + GQA_HEAD_DIM])
            kb_ref[:, e * GQA_HEAD_DIM : (e + 1) * GQA_HEAD_DIM] = _rope(kh * kg_ref[...], cb[...], sb[...], _QB).astype(BF16)
        vb_ref[...] = p[:, MLA_KV_LORA + nb : MLA_KV_LORA + 2 * nb].astype(BF16)
        kr = _rope(p[:, MLA_KV_LORA + 2 * nb :], ca[...], sa[...], _QA)
        kpe_ref[:, :LANES] = kr.astype(BF16)
        kpe_ref[:, LANES:] = pltpu.roll(kr, MLA_ROPE, 1).astype(BF16)

    return pl.pallas_call(
        body,
        name="kprep_fwd",
        grid=(n // tr,),
        in_specs=[_rows(tr, KVP), _bcast(MLA_KV_LORA), _bcast(GQA_HEAD_DIM)] + [_rows(tr, LANES)] * 4,
        out_specs=[_rows(tr, MLA_KV_LORA), _rows(tr, nb), _rows(tr, nb), _rows(tr, 2 * LANES)],
        out_shape=[jax.ShapeDtypeStruct((n, w), BF16) for w in (MLA_KV_LORA, nb, nb, 2 * LANES)],
        compiler_params=_params(("parallel",)),
    )(pkv, kvg, kg, cos_a, ss_a, cos_b, ss_b)


def _kprep_bwd(pkv, dckv, dkb, dvb, dkpe, kvg, kg, cos_b, ss_b, tr):
    n = pkv.shape[0]
    nb = GQA_KV_HEADS * GQA_HEAD_DIM

    def body(p_ref, dckv_ref, dkb_ref, dvb_ref, dkpe_ref, kvg_ref, kg_ref, cb, sb, dp_ref, dkvg_ref, dkg_ref):
        _acc_init(pl.program_id(0), [dkvg_ref, dkg_ref])
        p = p_ref[...]
        xh, r = _rms(p[:, :MLA_KV_LORA])
        dn = dckv_ref[...]
        dkvg_ref[...] += _csum(dn * xh)
        dp_ref[:, :MLA_KV_LORA] = _rms_bwd(xh, r, dn * kvg_ref[...]).astype(BF16)
        for e in range(GQA_KV_HEADS):
            lo = MLA_KV_LORA + e * GQA_HEAD_DIM
            kh, rk = _rms(p[:, lo : lo + GQA_HEAD_DIM])
            dk = _rope_t(dkb_ref[:, e * GQA_HEAD_DIM : (e + 1) * GQA_HEAD_DIM], cb[...], sb[...], _QB)
            dkg_ref[...] += _csum(dk * kh)
            dp_ref[:, lo : lo + GQA_HEAD_DIM] = _rms_bwd(kh, rk, dk * kg_ref[...]).astype(BF16)
        dp_ref[:, MLA_KV_LORA + nb : MLA_KV_LORA + 2 * nb] = dvb_ref[...].astype(BF16)
        dp_ref[:, MLA_KV_LORA + 2 * nb :] = dkpe_ref[...].astype(BF16)

    return pl.pallas_call(
        body,
        name="kprep_bwd",
        grid=(n // tr,),
        in_specs=[_rows(tr, KVP), _rows(tr, MLA_KV_LORA), _rows(tr, nb), _rows(tr, nb), _rows(tr, LANES),
                  _bcast(MLA_KV_LORA), _bcast(GQA_HEAD_DIM), _rows(tr, LANES), _rows(tr, LANES)],
        out_specs=[_rows(tr, KVP), _bcast(MLA_KV_LORA), _bcast(GQA_HEAD_DIM)],
        out_shape=[jax.ShapeDtypeStruct((n, KVP), BF16), jax.ShapeDtypeStruct((1, MLA_KV_LORA), F32),
                   jax.ShapeDtypeStruct((1, GQA_HEAD_DIM), F32)],
        compiler_params=_params(("arbitrary",)),
    )(pkv, dckv, dkb, dvb, dkpe, kvg, kg, cos_b, ss_b)


def _kgrad_split(dka, dva, cos_a, ss_a, tr):
    n = dka.shape[0]
    wk = MLA_HEADS * 2 * LANES

    def body(dk_ref, dv_ref, ca, sa, dkv_ref, dkpe_ref):
        even = jnp.zeros((tr, LANES), F32)
        odd = jnp.zeros((tr, LANES), F32)
        for h in range(MLA_HEADS):
            dkv_ref[:, 2 * h * LANES : (2 * h + 1) * LANES] = dk_ref[:, 2 * h * LANES : (2 * h + 1) * LANES].astype(BF16)
            dkv_ref[:, (2 * h + 1) * LANES : (2 * h + 2) * LANES] = dv_ref[:, h * MLA_V : (h + 1) * MLA_V].astype(BF16)
            part = dk_ref[:, (2 * h + 1) * LANES : (2 * h + 2) * LANES]
            if h % 2 == 0:
                even = even + part
            else:
                odd = odd + part
        lane = lax.broadcasted_iota(jnp.int32, (tr, LANES), 1)
        low = lane < MLA_ROPE
        both = jnp.where(low, even, odd)
        tot = jnp.where(low, both + pltpu.roll(both, MLA_ROPE, 1), 0.0)
        dkpe_ref[...] = _rope_t(tot, ca[...], sa[...], _QA)

    return pl.pallas_call(
        body,
        name="kgrad_split",
        grid=(n // tr,),
        in_specs=[_rows(tr, wk), _rows(tr, MLA_HEADS * MLA_V), _rows(tr, LANES), _rows(tr, LANES)],
        out_specs=[_rows(tr, wk), _rows(tr, LANES)],
        out_shape=[jax.ShapeDtypeStruct((n, wk), BF16), jax.ShapeDtypeStruct((n, LANES), F32)],
        compiler_params=_params(("parallel",)),
    )(dka, dva, cos_a, ss_a)


def _qprep_fwd(pq, qg, gq, cos_b, ss_b, tr):
    n = pq.shape[0]
    nq = GQA_HEADS * GQA_HEAD_DIM

    def body(p_ref, qg_ref, gq_ref, cb, sb, cq_ref, qb_ref):
        xh, _ = _rms(p_ref[:, :MLA_Q_LORA])
        cq_ref[...] = (xh * qg_ref[...]).astype(BF16)
        for h in range(GQA_HEADS):
            lo = MLA_Q_LORA + h * GQA_HEAD_DIM
            qh, _ = _rms(p_ref[:, lo : lo + GQA_HEAD_DIM])
            qb_ref[:, h * GQA_HEAD_DIM : (h + 1) * GQA_HEAD_DIM] = _rope(qh * gq_ref[...], cb[...], sb[...], _QB).astype(BF16)

    return pl.pallas_call(
        body,
        name="qprep_fwd",
        grid=(n // tr,),
        in_specs=[_rows(tr, QC), _bcast(MLA_Q_LORA), _bcast(GQA_HEAD_DIM), _rows(tr, LANES), _rows(tr, LANES)],
        out_specs=[_rows(tr, MLA_Q_LORA), _rows(tr, nq)],
        out_shape=[jax.ShapeDtypeStruct((n, MLA_Q_LORA), BF16), jax.ShapeDtypeStruct((n, nq), BF16)],
        compiler_params=_params(("parallel",)),
    )(pq, qg, gq, cos_b, ss_b)


def _qprep_bwd(pq, dcq, dqb, qg, gq, cos_b, ss_b, tr):
    n = pq.shape[0]
    nq = GQA_HEADS * GQA_HEAD_DIM

    def body(p_ref, dcq_ref, dqb_ref, qg_ref, gq_ref, cb, sb, dp_ref, dqg_ref, dgq_ref):
        _acc_init(pl.program_id(0), [dqg_ref, dgq_ref])
        xh, r = _rms(p_ref[:, :MLA_Q_LORA])
        dn = dcq_ref[...]
        dqg_ref[...] += _csum(dn * xh)
        dp_ref[:, :MLA_Q_LORA] = _rms_bwd(xh, r, dn * qg_ref[...]).astype(BF16)
        for h in range(GQA_HEADS):
            lo = MLA_Q_LORA + h * GQA_HEAD_DIM
            qh, rq = _rms(p_ref[:, lo : lo + GQA_HEAD_DIM])
            dq = _rope_t(dqb_ref[:, h * GQA_HEAD_DIM : (h + 1) * GQA_HEAD_DIM], cb[...], sb[...], _QB)
            dgq_ref[...] += _csum(dq * qh)
            dp_ref[:, lo : lo + GQA_HEAD_DIM] = _rms_bwd(qh, rq, dq * gq_ref[...]).astype(BF16)

    return pl.pallas_call(
        body,
        name="qprep_bwd",
        grid=(n // tr,),
        in_specs=[_rows(tr, QC), _rows(tr, MLA_Q_LORA), _rows(tr, nq), _bcast(MLA_Q_LORA), _bcast(GQA_HEAD_DIM),
                  _rows(tr, LANES), _rows(tr, LANES)],
        out_specs=[_rows(tr, QC), _bcast(MLA_Q_LORA), _bcast(GQA_HEAD_DIM)],
        out_shape=[jax.ShapeDtypeStruct((n, QC), BF16), jax.ShapeDtypeStruct((1, MLA_Q_LORA), F32),
                   jax.ShapeDtypeStruct((1, GQA_HEAD_DIM), F32)],
        compiler_params=_params(("arbitrary",)),
    )(pq, dcq, dqb, qg, gq, cos_b, ss_b)


_QA_COLS = MLA_HEADS * (MLA_NOPE + MLA_ROPE)


def _qrope_fwd(qa, cos_a, ss_a, tr):
    n = qa.shape[0]

    def body(q_ref, ca, sa, o_ref):
        for j in range(MLA_HEADS // 2):
            lo = 3 * j * LANES
            o_ref[:, lo : lo + 2 * LANES] = q_ref[:, lo : lo + 2 * LANES].astype(BF16)
            o_ref[:, lo + 2 * LANES : lo + 3 * LANES] = _rope(q_ref[:, lo + 2 * LANES : lo + 3 * LANES], ca[...], sa[...], _QA).astype(BF16)

    return pl.pallas_call(
        body,
        name="qrope_fwd",
        grid=(n // tr,),
        in_specs=[_rows(tr, _QA_COLS), _rows(tr, LANES), _rows(tr, LANES)],
        out_specs=_rows(tr, _QA_COLS),
        out_shape=jax.ShapeDtypeStruct((n, _QA_COLS), BF16),
        compiler_params=_params(("parallel",)),
    )(qa, cos_a, ss_a)


def _qrope_bwd(dq2, cos_a, ss_a, tr):
    n = dq2.shape[0]

    def body(d_ref, ca, sa, o_ref):
        for j in range(MLA_HEADS // 2):
            lo = 3 * j * LANES
            h0, h1 = 2 * j, 2 * j + 1
            o_ref[:, lo : lo + LANES] = d_ref[:, 2 * h0 * LANES : (2 * h0 + 1) * LANES].astype(BF16)
            o_ref[:, lo + LANES : lo + 2 * LANES] = d_ref[:, 2 * h1 * LANES : (2 * h1 + 1) * LANES].astype(BF16)
            pe = d_ref[:, (2 * h0 + 1) * LANES : (2 * h0 + 2) * LANES] + d_ref[:, (2 * h1 + 1) * LANES : (2 * h1 + 2) * LANES]
            o_ref[:, lo + 2 * LANES : lo + 3 * LANES] = _rope_t(pe, ca[...], sa[...], _QA).astype(BF16)

    return pl.pallas_call(
        body,
        name="qrope_bwd",
        grid=(n // tr,),
        in_specs=[_rows(tr, MLA_HEADS * 2 * LANES), _rows(tr, LANES), _rows(tr, LANES)],
        out_specs=_rows(tr, _QA_COLS),
        out_shape=jax.ShapeDtypeStruct((n, _QA_COLS), BF16),
        compiler_params=_params(("parallel",)),
    )(dq2, cos_a, ss_a)


def _cat(refs):
    vals = [r[...] for r in refs]
    return vals[0] if len(vals) == 1 else jnp.concatenate(vals, axis=-1)


LOG2E = 1.4426950408889634


def _attn_fwd(qparts, kparts, vpart, n_heads, group, dv, scale, name, tq, after=None):
    T, Tk = qparts[0][0].shape[0], kparts[0][0].shape[0]
    nq_, nk_ = len(qparts), len(kparts)
    sub = min(tq, 256)
    c2 = scale * LOG2E

    def body(*refs):
        q_refs, k_refs = refs[:nq_], refs[nq_ : nq_ + nk_]
        v_ref = refs[nq_ + nk_]
        o_ref, lse_ref = refs[-2:]
        k = _cat(k_refs)
        v = v_ref[...]
        for r0 in range(0, tq, sub):
            q = _cat([r.at[r0 : r0 + sub, :] for r in q_refs])
            s = lax.dot_general(q, k, _DIMS["NT"], preferred_element_type=F32)
            m = jnp.max(s, axis=-1, keepdims=True)
            p = jnp.exp2((s - m) * c2)
            l = jnp.sum(p, axis=-1, keepdims=True)
            acc = jnp.dot(p.astype(BF16), v, preferred_element_type=F32)
            o_ref[r0 : r0 + sub, :] = (acc * (1.0 / l)).astype(BF16)
            lse_ref[r0 : r0 + sub, :] = m * scale + jnp.log(l)

    in_specs = [pl.BlockSpec((tq, LANES), lambda h, i, f=f: (i, f(h))) for _, f in qparts]
    in_specs += [pl.BlockSpec((Tk, LANES), lambda h, i, f=f: (0, f(h // group))) for _, f in kparts]
    fv = vpart[1]
    in_specs.append(pl.BlockSpec((Tk, dv), lambda h, i: (0, fv(h // group))))
    args = [*[a for a, _ in qparts], *[a for a, _ in kparts], vpart[0]]
    if after is not None:
        in_specs.append(pl.BlockSpec(after.shape, lambda h, i: (0, 0)))
        args.append(after)
    return pl.pallas_call(
        body,
        name=name,
        grid=(n_heads, T // tq),
        in_specs=in_specs,
        out_specs=[pl.BlockSpec((tq, dv), lambda h, i: (i, h)), pl.BlockSpec((None, tq, 1), lambda h, i: (h, i, 0))],
        out_shape=[jax.ShapeDtypeStruct((T, n_heads * dv), BF16), jax.ShapeDtypeStruct((n_heads, T, 1), F32)],
        compiler_params=_params(("parallel", "parallel")),
    )(*args)


def _attn_bwd(qparts, kparts, vpart, o, do, lse, n_heads, group, dv, scale, name, tq):
    T, Tk = qparts[0][0].shape[0], kparts[0][0].shape[0]
    nq_, nk_ = len(qparts), len(kparts)
    dk_ = LANES * nq_
    n_kv = n_heads // group
    nblk = T // tq
    c2 = scale * LOG2E

    def head(hk, i):
        return hk * group + i // nblk

    sub = min(tq, 256)

    def body(*refs):
        q_refs = refs[:nq_]
        k = _cat(refs[nq_ : nq_ + nk_])
        v_ref, o_ref, do_ref, lse_ref, dq_ref, dk_ref, dv_ref = refs[nq_ + nk_ :]
        i = pl.program_id(1)
        _acc_init(i, [dk_ref, dv_ref])
        v = v_ref[...]
        dk_acc, dv_acc = None, None
        for r0 in range(0, tq, sub):
            rows = slice(r0, r0 + sub)
            q = _cat([r.at[rows, :] for r in q_refs])
            s = lax.dot_general(q, k, _DIMS["NT"], preferred_element_type=F32)
            p = jnp.exp2(s * c2 - lse_ref[rows, :] * LOG2E)
            dov = do_ref[rows, :]
            dp = lax.dot_general(dov, v, _DIMS["NT"], preferred_element_type=F32)
            delta = jnp.sum(dov.astype(F32) * o_ref[rows, :].astype(F32), axis=-1, keepdims=True)
            ds = (p * (dp - delta)).astype(BF16)
            dq_ref[rows, :] = jnp.dot(ds, k, preferred_element_type=F32) * scale
            dk_part = lax.dot_general(ds, q, _DIMS["TN"], preferred_element_type=F32)
            dv_part = lax.dot_general(p.astype(BF16), dov, _DIMS["TN"], preferred_element_type=F32)
            dk_acc = dk_part if dk_acc is None else dk_acc + dk_part
            dv_acc = dv_part if dv_acc is None else dv_acc + dv_part
        dk_ref[...] += dk_acc
        dv_ref[...] += dv_acc

        @pl.when(i == group * nblk - 1)
        def _():
            dk_ref[...] *= scale

    in_specs = [pl.BlockSpec((tq, LANES), lambda hk, i, f=f: (i % nblk, f(head(hk, i)))) for _, f in qparts]
    in_specs += [pl.BlockSpec((Tk, LANES), lambda hk, i, f=f: (0, f(hk))) for _, f in kparts]
    fv = vpart[1]
    in_specs.append(pl.BlockSpec((Tk, dv), lambda hk, i: (0, fv(hk))))
    in_specs += [pl.BlockSpec((tq, dv), lambda hk, i: (i % nblk, head(hk, i)))] * 2
    in_specs.append(pl.BlockSpec((None, tq, 1), lambda hk, i: (head(hk, i), i % nblk, 0)))
    return pl.pallas_call(
        body,
        name=name,
        grid=(n_kv, group * nblk),
        in_specs=in_specs,
        out_specs=[pl.BlockSpec((tq, dk_), lambda hk, i: (i % nblk, head(hk, i))),
                   pl.BlockSpec((Tk, dk_), lambda hk, i: (0, hk)),
                   pl.BlockSpec((Tk, dv), lambda hk, i: (0, hk))],
        out_shape=[jax.ShapeDtypeStruct((T, n_heads * dk_), F32), jax.ShapeDtypeStruct((Tk, n_kv * dk_), F32),
                   jax.ShapeDtypeStruct((Tk, n_kv * dv), F32)],
        compiler_params=_params(("parallel", "arbitrary")),
    )(*[a for a, _ in qparts], *[a for a, _ in kparts], vpart[0], o, do, lse)


def _gates_fwd(pg, ya, yb, tr):
    n, d = ya.shape

    def body(pg_ref, ya_ref, yb_ref, o_ref):
        ga = jax.nn.sigmoid(pg_ref[:, :d].astype(F32))
        gb = jax.nn.sigmoid(pg_ref[:, d:].astype(F32))
        o_ref[...] = (ga * ya_ref[...].astype(F32) + gb * yb_ref[...].astype(F32)).astype(BF16)

    return pl.pallas_call(
        body,
        name="gates_fwd",
        grid=(n // tr,),
        in_specs=[_rows(tr, 2 * d), _rows(tr, d), _rows(tr, d)],
        out_specs=_rows(tr, d),
        out_shape=jax.ShapeDtypeStruct((n, d), BF16),
        compiler_params=_params(("parallel",)),
    )(pg, ya, yb)


def _gates_bwd(dm, pg, ya, yb, tr):
    n, d = ya.shape

    def body(dm_ref, pg_ref, ya_ref, yb_ref, dya_ref, dyb_ref, dpg_ref):
        dmv = dm_ref[...].astype(F32)
        ga = jax.nn.sigmoid(pg_ref[:, :d].astype(F32))
        gb = jax.nn.sigmoid(pg_ref[:, d:].astype(F32))
        dya_ref[...] = (dmv * ga).astype(BF16)
        dyb_ref[...] = (dmv * gb).astype(BF16)
        dpg_ref[:, :d] = (dmv * ya_ref[...].astype(F32) * ga * (1.0 - ga)).astype(BF16)
        dpg_ref[:, d:] = (dmv * yb_ref[...].astype(F32) * gb * (1.0 - gb)).astype(BF16)

    return pl.pallas_call(
        body,
        name="gates_bwd",
        grid=(n // tr,),
        in_specs=[_rows(tr, d), _rows(tr, 2 * d), _rows(tr, d), _rows(tr, d)],
        out_specs=[_rows(tr, d), _rows(tr, d), _rows(tr, 2 * d)],
        out_shape=[jax.ShapeDtypeStruct((n, d), BF16), jax.ShapeDtypeStruct((n, d), BF16), jax.ShapeDtypeStruct((n, 2 * d), BF16)],
        compiler_params=_params(("parallel",)),
    )(dm, pg, ya, yb)


def _resid_norm2_fwd(x2d, att, g1, n2g, sh2, sc2, tr):
    n, d = x2d.shape

    def body(x_ref, a_ref, g1_ref, g_ref, sh_ref, sc_ref, x1_ref, z_ref):
        x1 = x_ref[...] + g1_ref[...] * a_ref[...]
        x1_ref[...] = x1
        xh, _ = _rms(x1)
        z_ref[...] = ((xh * g_ref[...]) * (1.0 + sc_ref[...]) + sh_ref[...]).astype(BF16)

    return pl.pallas_call(
        body,
        name="resid_norm2_fwd",
        grid=(n // tr,),
        in_specs=[_rows(tr, d), _rows(tr, d)] + [_bcast(d)] * 4,
        out_specs=[_rows(tr, d), _rows(tr, d)],
        out_shape=[jax.ShapeDtypeStruct((n, d), F32), jax.ShapeDtypeStruct((n, d), BF16)],
        compiler_params=_params(("parallel",)),
    )(x2d, att, g1, n2g, sh2, sc2)


def _resid_norm2_bwd(dz2, x1, dx2, att, n2g, sc2, g1, tr):
    n, d = x1.shape

    def body(dz_ref, x1_ref, dx2_ref, a_ref, g_ref, sc_ref, g1_ref, dx1_ref, da_ref, dg_ref, dsh_ref, dsc_ref, dg1_ref):
        _acc_init(pl.program_id(0), [dg_ref, dsh_ref, dsc_ref, dg1_ref])
        xh, r = _rms(x1_ref[...])
        dzv = dz_ref[...]
        gv = g_ref[...]
        dsc_ref[...] += _csum(dzv * (xh * gv))
        dsh_ref[...] += _csum(dzv)
        dh = dzv * (1.0 + sc_ref[...])
        dg_ref[...] += _csum(dh * xh)
        dx1 = _rms_bwd(xh, r, dh * gv) + dx2_ref[...]
        dx1_ref[...] = dx1
        dg1_ref[...] += _csum(dx1 * a_ref[...])
        da_ref[...] = (dx1 * g1_ref[...]).astype(BF16)

    return pl.pallas_call(
        body,
        name="resid_norm2_bwd",
        grid=(n // tr,),
        in_specs=[_rows(tr, d)] * 4 + [_bcast(d)] * 3,
        out_specs=[_rows(tr, d), _rows(tr, d)] + [_bcast(d)] * 4,
        out_shape=[jax.ShapeDtypeStruct((n, d), F32), jax.ShapeDtypeStruct((n, d), BF16)] + [jax.ShapeDtypeStruct((1, d), F32)] * 4,
        compiler_params=_params(("arbitrary",)),
    )(dz2, x1, dx2, att, n2g, sc2, g1)


def _edges(shape):
    row = lax.broadcasted_iota(jnp.int32, shape, 0)
    return row == 0, row == shape[0] - 1


def _shifts(u, edges):
    n = u.shape[0]
    return jnp.where(edges[0], 0.0, pltpu.roll(u, 1, 0)), jnp.where(edges[1], 0.0, pltpu.roll(u, n - 1, 0))


def _conv3(u, prev, nxt, w_ref, b_ref):
    return b_ref[...] + w_ref[0:1, :] * prev + w_ref[1:2, :] * u + w_ref[2:3, :] * nxt


def _ffn_up_conv(z, wup_t, cw, cb, tc, after):
    n, d = z.shape
    f = wup_t.shape[0] // 2
    nb = f // tc

    def body(z_ref, wa_ref, wb_ref, cwa, cwb, cba, cbb, after_ref, ua_ref, ub_ref, h_ref):
        w = jnp.concatenate([wa_ref[...], wb_ref[...]], axis=0)
        u = lax.dot_general(z_ref[...], w, _DIMS["NT"], preferred_element_type=F32).astype(BF16)
        ua_ref[...] = u[:, :tc]
        ub_ref[...] = u[:, tc:]
        edges = _edges((n, tc))
        ua = u[:, :tc].astype(F32)
        ub = u[:, tc:].astype(F32)
        a = _conv3(ua, *_shifts(ua, edges), cwa, cba)
        b = _conv3(ub, *_shifts(ub, edges), cwb, cbb)
        h_ref[...] = (a * jax.nn.sigmoid(a) * b).astype(BF16)

    col = lambda rows, off: pl.BlockSpec((rows, tc), lambda i: (0, i + off))
    w_rows = lambda off: pl.BlockSpec((tc, d), lambda i: (i + off, 0))
    return pl.pallas_call(
        body,
        name="ffn_up_conv",
        grid=(nb,),
        in_specs=[pl.BlockSpec((n, d), lambda i: (0, 0)), w_rows(0), w_rows(nb), col(3, 0), col(3, nb), col(1, 0), col(1, nb),
                  pl.BlockSpec(after.shape, lambda i: (0, 0))],
        out_specs=[col(n, 0)] * 3,
        out_shape=[jax.ShapeDtypeStruct((n, f), BF16)] * 3,
        compiler_params=_params(("parallel",)),
    )(z, wup_t, wup_t, cw, cw, cb, cb, after)


def _conv_bwd(u_a, u_b, dh, cw, cb, tc):
    n, f = u_a.shape
    nb = f // tc

    def part(uv, prev, nxt, duc, edges, w_ref, du_ref, dw_ref, db_ref):
        db_ref[...] = _csum(duc)
        dw_ref[0:1, :] = _csum(duc * prev)
        dw_ref[1:2, :] = _csum(duc * uv)
        dw_ref[2:3, :] = _csum(duc * nxt)
        d_prev, d_next = _shifts(duc, edges)
        du_ref[...] = (w_ref[0:1, :] * d_next + w_ref[1:2, :] * duc + w_ref[2:3, :] * d_prev).astype(BF16)

    def body(ua_ref, ub_ref, dh_ref, wa_ref, wb_ref, ba_ref, bb_ref, dua_ref, dub_ref, dwa_ref, dwb_ref, dba_ref, dbb_ref):
        edges = _edges((n, tc))
        ua = ua_ref[...].astype(F32)
        ub = ub_ref[...].astype(F32)
        sa = _shifts(ua, edges)
        sb = _shifts(ub, edges)
        a = _conv3(ua, *sa, wa_ref, ba_ref)
        b = _conv3(ub, *sb, wb_ref, bb_ref)
        dhv = dh_ref[...].astype(F32)
        sg = jax.nn.sigmoid(a)
        da = dhv * b * (sg * (1.0 + a * (1.0 - sg)))
        db = dhv * (a * sg)
        part(ua, *sa, da, edges, wa_ref, dua_ref, dwa_ref, dba_ref)
        part(ub, *sb, db, edges, wb_ref, dub_ref, dwb_ref, dbb_ref)

    col = lambda rows, off: pl.BlockSpec((rows, tc), lambda i: (0, i + off))
    return pl.pallas_call(
        body,
        name="conv_bwd",
        grid=(nb,),
        in_specs=[col(n, 0), col(n, 0), col(n, 0), col(3, 0), col(3, nb), col(1, 0), col(1, nb)],
        out_specs=[col(n, 0), col(n, 0), col(3, 0), col(3, 0), col(1, 0), col(1, 0)],
        out_shape=[jax.ShapeDtypeStruct((n, f), BF16)] * 2 + [jax.ShapeDtypeStruct((3, f), F32)] * 2 + [jax.ShapeDtypeStruct((1, f), F32)] * 2,
        compiler_params=_params(("parallel",)),
    )(u_a, u_b, dh, cw, cw, cb, cb)


def _loss_head(x1, f, g2, fg, tgt, tr):
    n, d = x1.shape

    def body(x1_ref, f_ref, g2_ref, fg_ref, t_ref, sq_ref, dx2_ref, dfg_ref, dg2_ref, df_ref):
        _acc_init(pl.program_id(0), [sq_ref, dfg_ref, dg2_ref])
        fv = f_ref[...]
        xh, r = _rms(x1_ref[...] + g2_ref[...] * fv)
        err = xh * fg_ref[...] - t_ref[...]
        sq_ref[...] += _csum(err * err)
        dy = err * (1.0 / d)
        dfg_ref[...] += _csum(dy * xh)
        dx2 = _rms_bwd(xh, r, dy * fg_ref[...])
        dx2_ref[...] = dx2
        dg2_ref[...] += _csum(dx2 * fv)
        df_ref[...] = (dx2 * g2_ref[...]).astype(BF16)

    return pl.pallas_call(
        body,
        name="loss_head",
        grid=(n // tr,),
        in_specs=[_rows(tr, d), _rows(tr, d), _bcast(d), _bcast(d), _rows(tr, d)],
        out_specs=[_bcast(d), _rows(tr, d), _bcast(d), _bcast(d), _rows(tr, d)],
        out_shape=[jax.ShapeDtypeStruct((1, d), F32), jax.ShapeDtypeStruct((n, d), F32), jax.ShapeDtypeStruct((1, d), F32),
                   jax.ShapeDtypeStruct((1, d), F32), jax.ShapeDtypeStruct((n, d), BF16)],
        compiler_params=_params(("arbitrary",)),
    )(x1, f, g2, fg, tgt)


def _sum_slots(g, name):
    s, r, w = g.shape

    def body(g_ref, o_ref):
        acc = g_ref[0]
        for k in range(1, s):
            acc = acc + g_ref[k]
        o_ref[...] = acc

    return pl.pallas_call(body, name=name, out_shape=jax.ShapeDtypeStruct((r, w), F32))(g)


def _silu_grad_mul(ds, cvec):
    def body(d_ref, c_ref, o_ref):
        cv = c_ref[...]
        sg = jax.nn.sigmoid(cv)
        o_ref[...] = d_ref[...] * (sg * (1.0 + cv * (1.0 - sg)))

    return pl.pallas_call(body, name="silu_grad_mul", out_shape=jax.ShapeDtypeStruct(ds.shape, F32))(ds, cvec)


def _adamw_update(wv, gv, mv, vv, d_ref, mo_ref, vo_ref):
    mn = ADAM_B1 * mv + (1.0 - ADAM_B1) * gv
    vn = ADAM_B2 * vv + (1.0 - ADAM_B2) * (gv * gv)
    mo_ref[...] = mn
    vo_ref[...] = vn
    m_hat = mn / (1.0 - ADAM_B1**ADAM_STEP)
    v_hat = vn / (1.0 - ADAM_B2**ADAM_STEP)
    d_ref[...] = -ADAM_LR * (m_hat / (jnp.sqrt(v_hat) + ADAM_EPS) + ADAM_WD * wv)


def _adamw_many(ws, gs, ms, vs, name):
    n = len(ws)

    def body(*refs):
        for k in range(n):
            w_ref, g_ref, m_ref, v_ref = (refs[q * n + k] for q in range(4))
            d_ref, mo_ref, vo_ref = (refs[(4 + q) * n + k] for q in range(3))
            _adamw_update(w_ref[...], g_ref[...], m_ref[...], v_ref[...], d_ref, mo_ref, vo_ref)

    res = pl.pallas_call(body, name=name, out_shape=[jax.ShapeDtypeStruct(w.shape, F32) for w in ws] * 3)(*ws, *gs, *ms, *vs)
    return res[:n], res[n : 2 * n], res[2 * n :]


def _adamw(w, g, m, v, name, g_transposed=False, g_sibling=None):
    r, cdim = w.shape
    halves = g_sibling is not None
    block = 1 << 19
    if g_transposed:
        tc = _pick(cdim // 2 if halves else cdim, 2048)
        tr = _pick(r, max(LANES, block // tc), LANES)
        per_half = (cdim // 2) // tc
    else:
        rows = r // 2 if halves else r
        tc = _pick(cdim, 2048)
        tr = _pick(rows, max(8, block // tc), 8)
        if tr < 64 and rows > 64:
            tr, tc = _pick(rows, 1024, 8), _pick(cdim, 512)
        per_half = (r // 2) // tr
    emit_g = g_transposed or halves

    def body(w_ref, g_ref, *rest):
        m_ref, v_ref = rest[halves : halves + 2]
        outs = rest[halves + 2 :]
        gv = g_ref[...]
        if halves:
            along = pl.program_id(1 if g_transposed else 0)
            gv = jnp.where(along // per_half == lax.axis_index("c"), gv, rest[0][...])
        if g_transposed:
            gv = gv.T
        if emit_g:
            outs[0][...] = gv
        _adamw_update(w_ref[...], gv, m_ref[...], v_ref[...], *outs[-3:])

    spec = pl.BlockSpec((tr, tc), lambda i, j: (i, j))
    if g_transposed:
        g_spec = pl.BlockSpec((tc, tr), lambda i, j: (j % per_half if halves else j, i))
    else:
        g_spec = pl.BlockSpec((tr, tc), lambda i, j: (i % per_half if halves else i, j))
    n_out = 3 + emit_g
    res = pl.pallas_call(
        body,
        name=name,
        grid=(r // tr, cdim // tc),
        in_specs=[spec, g_spec] + [g_spec] * halves + [spec, spec],
        out_specs=[spec] * n_out,
        out_shape=[jax.ShapeDtypeStruct((r, cdim), F32)] * n_out,
        compiler_params=_params(("parallel", "parallel")),
    )(w, g, *([g_sibling] if halves else []), m, v)
    return res if emit_g else [g, *res]


def _place():
    return lax.axis_index("x"), lax.axis_index("y"), lax.axis_index("c")


def _remote(src, dst, send_sem, recv_sem, dev):
    return pltpu.make_async_remote_copy(src_ref=src, dst_ref=dst, send_sem=send_sem, recv_sem=recv_sem, device_id=dev, device_id_type=MESH)


ANY = pl.BlockSpec(memory_space=pl.ANY)


def _all_gather_small(v, name):
    r, w = v.shape

    def body(v_ref, o_ref, send, recv, lsem):
        x, y, c = _place()
        me = 4 * x + 2 * y + c
        mine = pltpu.make_async_copy(v_ref, o_ref.at[me], lsem)
        mine.start()
        sent = []
        for k in range(1, 8):
            px, py, pc = x ^ (k >> 2), y ^ ((k >> 1) & 1), c ^ (k & 1)
            cp = _remote(v_ref, o_ref.at[me], send.at[k - 1], recv.at[k - 1], (px, py, pc))
            cp.start()
            sent.append(cp)
        for k in range(1, 8):
            px, py, pc = x ^ (k >> 2), y ^ ((k >> 1) & 1), c ^ (k & 1)
            slot = o_ref.at[4 * px + 2 * py + pc]
            _remote(slot, slot, send.at[k - 1], recv.at[k - 1], (x, y, c)).wait_recv()
        for cp in sent:
            cp.wait_send()
        mine.wait()

    return pl.pallas_call(
        body,
        name=name,
        out_shape=jax.ShapeDtypeStruct((8, r, w), F32),
        in_specs=[pl.BlockSpec(memory_space=pltpu.VMEM)],
        out_specs=pl.BlockSpec(memory_space=pltpu.VMEM),
        scratch_shapes=[pltpu.SemaphoreType.DMA((7,)), pltpu.SemaphoreType.DMA((7,)), pltpu.SemaphoreType.DMA],
        compiler_params=pltpu.CompilerParams(vmem_limit_bytes=VMEM_LIMIT),
    )(v)


HBM = pl.BlockSpec(memory_space=pltpu.HBM)
SEM = pl.BlockSpec(memory_space=pltpu.SEMAPHORE)
EFFECT = pltpu.SideEffectType.DATAFLOW_SIDE_EFFECTING


def _other_chips(x, y):
    return [(1 - x, y), (x, 1 - y), (1 - x, 1 - y)]


def _bulk_start(name, srcs, land_shapes, n_copies, copies, after):
    n, m = len(srcs), len(land_shapes)

    def body(*refs):
        src_refs, land_refs = refs[:n], refs[n : n + m]
        send, recv = refs[n + m + 1], refs[n + m + 2]
        token = refs[-1]
        for k, (s, d, dev) in enumerate(copies(src_refs, land_refs)):
            _remote(s, d, send.at[k], recv.at[k], dev).start()
        token[...] = jnp.zeros_like(token)

    lands = [pltpu.with_memory_space_constraint(lax.empty(s.shape, s.dtype), pltpu.HBM) for s in land_shapes]
    out = pl.pallas_call(
        body,
        name=name,
        out_shape=(pltpu.SemaphoreType.DMA((n_copies,)), pltpu.SemaphoreType.DMA((n_copies,)),
                   *[pltpu.HBM(s.shape, s.dtype) for s in srcs], *[pltpu.HBM(s.shape, s.dtype) for s in land_shapes],
                   jax.ShapeDtypeStruct((8, LANES), F32)),
        in_specs=[HBM] * (n + m) + [ANY],
        out_specs=(SEM, SEM, *[HBM] * (n + m), pl.BlockSpec(memory_space=pltpu.VMEM)),
        input_output_aliases={i: 2 + i for i in range(n + m)},
        compiler_params=pltpu.CompilerParams(has_side_effects=EFFECT),
    )(*[pltpu.with_memory_space_constraint(s, pltpu.HBM) for s in srcs], *lands, after)
    return out[0], out[1], list(out[2 : 2 + n]), list(out[2 + n : 2 + n + m]), out[-1][0:1, 0:1]


def _bulk_wait(name, send, recv, srcs, lands, after, waits):
    n, m = len(srcs), len(lands)

    def body(*refs):
        src_refs, land_refs = refs[:n], refs[n : n + m]
        send_sem, recv_sem = refs[n + m], refs[n + m + 1]
        x, y, c = _place()
        for k, (s, d) in enumerate(waits(src_refs, land_refs)):
            cp = _remote(s, d, send_sem.at[k], recv_sem.at[k], (x, y, c))
            cp.wait_send()
            cp.wait_recv()

    out = pl.pallas_call(
        body,
        name=name,
        out_shape=tuple(pltpu.HBM(s.shape, s.dtype) for s in (*srcs, *lands)),
        in_specs=[HBM] * (n + m) + [SEM, SEM, ANY],
        out_specs=tuple([HBM] * (n + m)),
        input_output_aliases={i: i for i in range(n + m)},
        compiler_params=pltpu.CompilerParams(has_side_effects=EFFECT),
    )(*srcs, *lands, send, recv, after)
    return list(out[:n]), list(out[n:])


def _gather_start(shards, after, name):
    def copies(src, land):
        x, y, c = _place()
        j = 2 * x + y
        return [(src[a].at[c], land[a].at[j, c], (px, py, c)) for a in range(len(shards)) for px, py in _other_chips(x, y)]

    shapes = [jax.ShapeDtypeStruct((4,) + s.shape, s.dtype) for s in shards]
    return _bulk_start(name, shards, shapes, 3 * len(shards), copies, after)


def _gather_wait(started, after, name):
    send, recv, srcs, lands, _ = started

    def waits(src, land):
        x, y, c = _place()
        return [(src[a].at[c], land[a].at[2 * px + py, c]) for a in range(len(srcs)) for px, py in _other_chips(x, y)]

    return _bulk_wait(name, send, recv, srcs, lands, after, waits)


def _forward_halves(lands, name):
    n = len(lands)

    def body(*refs):
        bufs = refs[n : 2 * n]
        send, recv = refs[2 * n :]
        x, y, c = _place()
        started = []
        for a in range(n):
            for k, (px, py) in enumerate(_other_chips(x, y)):
                blk = bufs[a].at[2 * px + py, c]
                cp = _remote(blk, blk, send.at[3 * a + k], recv.at[3 * a + k], (x, y, 1 - c))
                cp.start()
                started.append(cp)
        for a in range(n):
            for k, (px, py) in enumerate(_other_chips(x, y)):
                blk = bufs[a].at[2 * px + py, 1 - c]
                _remote(blk, blk, send.at[3 * a + k], recv.at[3 * a + k], (x, y, c)).wait_recv()
        for cp in started:
            cp.wait_send()

    return pl.pallas_call(
        body,
        name=name,
        out_shape=[jax.ShapeDtypeStruct(b.shape, b.dtype) for b in lands],
        in_specs=[ANY] * n,
        out_specs=[ANY] * n,
        input_output_aliases={i: i for i in range(n)},
        scratch_shapes=[pltpu.SemaphoreType.DMA((3 * n,)), pltpu.SemaphoreType.DMA((3 * n,))],
    )(*lands)


def _forward_start(lands, after, name):
    def copies(src, _):
        x, y, c = _place()
        blocks = [src[a].at[2 * px + py, c] for a in range(len(lands)) for px, py in _other_chips(x, y)]
        return [(b, b, (x, y, 1 - c)) for b in blocks]

    return _bulk_start(name, lands, [], 3 * len(lands), copies, after)


def _forward_wait(started, after, name):
    send, recv, bufs, _, _ = started

    def waits(src, _):
        x, y, c = _place()
        return [(src[a].at[2 * px + py, c], src[a].at[2 * px + py, 1 - c]) for a in range(len(bufs)) for px, py in _other_chips(x, y)]

    return _bulk_wait(name, send, recv, bufs, [], after, waits)[0]


def _place_own(shards, lands):
    j = 2 * lax.axis_index("x") + lax.axis_index("y")
    full = [lax.dynamic_update_slice(b, s[None], (j, 0, 0, 0)) for b, s in zip(lands, shards)]
    return [f.reshape(4 * f.shape[2] * 2, f.shape[3]) for f in full]


def _gather_finish(started, after, tag):
    shards, lands = _gather_wait(started, after, "gather_wait_" + tag)
    return _place_own(shards, _forward_halves(lands, "gather_forward_" + tag))


def _gather_land(started, after, tag):
    shards, lands = _gather_wait(started, after, "gather_wait_" + tag)
    return shards, _forward_start(lands, shards[0], "forward_start_" + tag)


def _gather_done(landed, after, tag):
    shards, fwd = landed
    return _place_own(shards, _forward_wait(fwd, after, "forward_wait_" + tag))


def _swap_halves(grads, name):
    n = len(grads)

    def body(*refs):
        ins, outs = refs[:n], refs[n : 2 * n]
        send, recv = refs[2 * n :]
        x, y, c = _place()
        started = []
        for a in range(n):
            for s in range(4):
                cp = _remote(ins[a].at[s, 1 - c], outs[a].at[s], send.at[4 * a + s], recv.at[4 * a + s], (x, y, 1 - c))
                cp.start()
                started.append(cp)
        for cp in started:
            cp.wait_recv()
        for cp in started:
            cp.wait_send()

    return pl.pallas_call(
        body,
        name=name,
        out_shape=[jax.ShapeDtypeStruct((4,) + g.shape[2:], g.dtype) for g in grads],
        in_specs=[ANY] * n,
        out_specs=[ANY] * n,
        scratch_shapes=[pltpu.SemaphoreType.DMA((4 * n,)), pltpu.SemaphoreType.DMA((4 * n,))],
    )(*grads)


def _add_halves(grads, others, tag):
    outs = []
    for a, (g, o) in enumerate(zip(grads, others)):
        _, _, rh, cdim = g.shape
        tr = _pick(rh, 512, 16)

        def body(g_ref, o_ref, p_ref):
            p_ref[...] = (g_ref[...].astype(F32) + o_ref[...].astype(F32)).astype(BF16)

        outs.append(
            pl.pallas_call(
                body,
                name=f"add_halves_{tag}{a}",
                grid=(4, rh // tr),
                in_specs=[pl.BlockSpec((None, None, tr, cdim), lambda s, i: (s, lax.axis_index("c"), i, 0)),
                          pl.BlockSpec((None, tr, cdim), lambda s, i: (s, i, 0))],
                out_specs=pl.BlockSpec((None, tr, cdim), lambda s, i: (s, i, 0)),
                out_shape=jax.ShapeDtypeStruct((4, rh, cdim), BF16),
                compiler_params=_params(("parallel", "parallel")),
            )(g, o)
        )
    return outs


def _exchange_start(parts, after, name):
    def copies(src, land):
        x, y, c = _place()
        j = 2 * x + y
        return [(src[a].at[2 * px + py], land[a].at[j], (px, py, c)) for a in range(len(parts)) for px, py in _other_chips(x, y)]

    return _bulk_start(name, parts, [jax.ShapeDtypeStruct(p.shape, p.dtype) for p in parts], 3 * len(parts), copies, after)


def _exchange_finish(started, after, name):
    send, recv, srcs, lands, _ = started

    def waits(src, land):
        x, y, _ = _place()
        return [(src[a].at[2 * px + py], land[a].at[2 * px + py]) for a in range(len(srcs)) for px, py in _other_chips(x, y)]

    srcs, lands = _bulk_wait(name, send, recv, srcs, lands, after, waits)
    j = 2 * lax.axis_index("x") + lax.axis_index("y")
    return [lax.dynamic_update_slice(b, lax.dynamic_slice(p, (j, 0, 0), (1,) + p.shape[1:]), (j, 0, 0)) for b, p in zip(lands, srcs)]


def _sum_chips(recvd, tag):
    outs = []
    for a, g in enumerate(recvd):
        _, rh, cdim = g.shape
        tr = _pick(rh, 512, 16)

        def body(g_ref, o_ref):
            o_ref[...] = ((g_ref[0].astype(F32) + g_ref[1].astype(F32)) + g_ref[2].astype(F32)) + g_ref[3].astype(F32)

        outs.append(
            pl.pallas_call(
                body,
                name=f"sum_chips_{tag}{a}",
                grid=(rh // tr,),
                in_specs=[pl.BlockSpec((4, tr, cdim), lambda i: (0, i, 0))],
                out_specs=pl.BlockSpec((tr, cdim), lambda i: (i, 0)),
                out_shape=jax.ShapeDtypeStruct((rh, cdim), F32),
                compiler_params=_params(("parallel",)),
            )(g)
        )
    return outs


def _join_halves(halves, name):
    n = len(halves)

    def body(*refs):
        ins, outs = refs[:n], refs[n : 2 * n]
        send, recv = refs[2 * n :]
        x, y, c = _place()
        started = []
        for a in range(n):
            cp = _remote(ins[a], outs[a], send.at[a], recv.at[a], (x, y, 1 - c))
            cp.start()
            started.append(cp)
        for cp in started:
            cp.wait_recv()
        for cp in started:
            cp.wait_send()

    others = pl.pallas_call(
        body,
        name=name,
        out_shape=[jax.ShapeDtypeStruct(h.shape, h.dtype) for h in halves],
        in_specs=[ANY] * n,
        out_specs=[ANY] * n,
        scratch_shapes=[pltpu.SemaphoreType.DMA((n,)), pltpu.SemaphoreType.DMA((n,))],
    )(*halves)
    return list(zip(halves, others))


def _joined(mine, other):
    first = lax.axis_index("c") == 0
    return jnp.concatenate([jnp.where(first, mine, other), jnp.where(first, other, mine)], axis=0)


def _grad_views(grads):
    return [g.reshape(4, 2, g.shape[0] // 8, g.shape[1]) for g in grads]


def _scatter_start(grads, tag, after=None):
    views = _grad_views(grads)
    mine = _add_halves(views, _swap_halves(views, "swap_halves_" + tag), tag)
    return _exchange_start(mine, mine[-1] if after is None else after, "exchange_start_" + tag)


def _swap_start(grads, after, tag):
    views = _grad_views(grads)

    def copies(src, land):
        x, y, c = _place()
        return [(src[a].at[s, 1 - c], land[a].at[s], (x, y, 1 - c)) for a in range(len(views)) for s in range(4)]

    shapes = [jax.ShapeDtypeStruct((4,) + v.shape[2:], v.dtype) for v in views]
    return _bulk_start("swap_start_" + tag, views, shapes, 4 * len(views), copies, after)


def _scatter_start_after_swap(swapped, after, tag):
    send, recv, views, lands, _ = swapped

    def waits(src, land):
        c = lax.axis_index("c")
        return [(src[a].at[s, 1 - c], land[a].at[s]) for a in range(len(views)) for s in range(4)]

    views, others = _bulk_wait("swap_wait_" + tag, send, recv, views, lands, after, waits)
    mine = _add_halves(views, others, tag)
    return _exchange_start(mine, mine[-1], "exchange_start_" + tag)


def _join_start(halves, after, tag):
    def copies(src, land):
        x, y, c = _place()
        return [(src[a], land[a], (x, y, 1 - c)) for a in range(len(halves))]

    return _bulk_start("join_start_" + tag, halves, [jax.ShapeDtypeStruct(h.shape, h.dtype) for h in halves], len(halves), copies, after)


def _join_wait(started, after, tag):
    send, recv, halves, lands, _ = started
    halves, others = _bulk_wait("join_wait_" + tag, send, recv, halves, lands, after, lambda src, land: list(zip(src, land)))
    return list(zip(halves, others))


def _scatter_sums(started, after, tag):
    return _sum_chips(_exchange_finish(started, after, "exchange_wait_" + tag), tag)


def _scatter_finish(started, after, tag):
    return _join_halves(_scatter_sums(started, after, tag), "join_halves_" + tag)


def _t_bf16(w):
    return w.T.astype(BF16)


def kernel(x, c, ctx, c_ctx, w_ada, b_ada, norm1_g, w_in, mla_q_norm_g, w_q_up, mla_kv_norm_g, w_kv_up, gqa_q_norm_g, gqa_k_norm_g, w_br_a, w_br_b, w_out, norm2_g, w_up, conv_w, conv_b, w_down, final_norm_g, loss_target, m_c_ctx, m_w_ada, m_b_ada, m_norm1_g, m_w_in, m_mla_q_norm_g, m_w_q_up, m_mla_kv_norm_g, m_w_kv_up, m_gqa_q_norm_g, m_gqa_k_norm_g, m_w_br_a, m_w_br_b, m_w_out, m_norm2_g, m_w_up, m_conv_w, m_conv_b, m_w_down, m_final_norm_g, v_c_ctx, v_w_ada, v_b_ada, v_norm1_g, v_w_in, v_mla_q_norm_g, v_w_q_up, v_mla_kv_norm_g, v_w_kv_up, v_gqa_q_norm_g, v_gqa_k_norm_g, v_w_br_a, v_w_br_b, v_w_out, v_norm2_g, v_w_up, v_conv_w, v_conv_b, v_w_down, v_final_norm_g):
    T, D = x.shape[1], x.shape[2]
    C = ctx.shape[1]
    NA = w_ada.shape[2]
    NW = w_up.shape[2]
    F2 = 4 * NW
    FF = F2 // 2
    xi, yi, ci = _place()
    j = 2 * xi + yi
    me = 4 * xi + 2 * yi + ci
    tr = _pick(C, 128, 8)
    tq = _pick(T, 256)

    x2d, tgt, ctx2d = x[0], loss_target[0], ctx[0]
    fg = final_norm_g.reshape(1, D)
    cc = c_ctx.reshape(1, D)

    halve = lambda s: s.reshape(2, s.shape[0] // 2, s.shape[1])
    win_shard = halve(_t_bf16(w_in[0]))
    w0 = max(D, NW)
    pay = jnp.zeros((8, w0), F32).at[0:1, :D].set(c).at[1:4, :NW].set(conv_w[0])
    got = _all_gather_small(pay, "gather_cond")
    c_all = got[:, 0, :D]
    cw = jnp.concatenate([got[2 * s, 1:4, :NW] for s in range(4)], axis=1)
    s16 = jnp.concatenate([c_all, cc, jnp.zeros((7, D), F32)], axis=0)
    b_cols = lax.dynamic_slice(b_ada, (0, j * NA), (1, NA))
    ada_part = _mm(s16, w_ada[0], "NN", F32, "ada_fwd", act="silu", bias=b_cols)
    got = _all_gather_small(ada_part, "gather_ada")
    ada = jnp.concatenate([got[2 * s] for s in range(4)], axis=1)
    lat = lax.dynamic_slice(ada, (me, 0), (1, 6 * D))
    sh1, sc1, g1, sh2, sc2, g2 = [lat[:, k * D : (k + 1) * D] for k in range(6)]
    csh, csc = ada[8:9, :D], ada[8:9, D : 2 * D]

    ag_in = _gather_start([win_shard], got, "gather_start_in")
    t_in = ag_in[4]
    wq3 = (w_q_up[0] + t_in).reshape(MLA_Q_LORA, 2, MLA_NOPE + MLA_ROPE)
    wq_perm = jnp.concatenate([wq3[:, :, :MLA_NOPE].reshape(MLA_Q_LORA, -1), wq3[:, :, MLA_NOPE:].reshape(MLA_Q_LORA, -1)], axis=1)
    low = [_t_bf16(wq_perm), _t_bf16(w_kv_up[0] + t_in)]
    br = [_t_bf16(w_br_a[0] + t_in), _t_bf16(w_br_b[0] + t_in), (w_out[0] + t_in).astype(BF16)]
    ag_low = _gather_start([halve(s) for s in low], t_in, "gather_start_low")
    ag_br = _gather_start([halve(s) for s in br], ag_low[4], "gather_start_br")
    ag_up = _gather_start([halve(_t_bf16(w_up[0] + t_in))], ag_br[4], "gather_start_up")
    ag_down = _gather_start([halve((w_down[0] + t_in).astype(BF16))], ag_up[4], "gather_start_down")
    sh1 = sh1 + ag_down[4]

    cos_a, ss_a = _rope_tables(C, T, MLA_ROPE)
    cos_b, ss_b = _rope_tables(C, T, GQA_HEAD_DIM)
    lcos_a, lss_a, lcos_b, lss_b = cos_a[:T], ss_a[:T], cos_b[:T], ss_b[:T]

    z_all = _norm_mod_fwd(x2d, norm1_g, sh1, sc1, "norm1_lat_fwd", tr, out_rows=T + C)
    z_all = _norm_mod_fwd(ctx2d, norm1_g, csh, csc, "norm1_ctx_fwd", tr, base=z_all, out_off=T)
    (win_t,) = _gather_finish(ag_in, z_all, "in")
    kv_cols = KVP - LANES + MLA_ROPE
    e_kpe = MLA_KV_LORA + MLA_ROPE
    w_kvp = jnp.concatenate([win_t[:MLA_KV_LORA], win_t[e_kpe:kv_cols], win_t[MLA_KV_LORA:e_kpe], jnp.zeros((LANES - MLA_ROPE, D), BF16)], axis=0)

    pkv = _mm(z_all, w_kvp, "NT", F32, "proj_kv", tn=KVP)
    pq = _mm(z_all, win_t, "NT", F32, "proj_q", m=T, n=QC, b_off=kv_cols)
    low_landed = _gather_land(ag_low, pq, "low")
    pg = _mm(z_all, win_t, "NT", BF16, "proj_g", m=T, n=2 * D, b_off=kv_cols + QC, after=low_landed[1][4])
    wq_t, wkv_t = _gather_done(low_landed, pg, "low")
    ckv_n, kb2, vb2, kpe2 = _kprep_fwd(pkv, mla_kv_norm_g, gqa_k_norm_g, cos_a, ss_a, cos_b, ss_b, tr)
    kv_up = _mm(ckv_n, wkv_t, "NT", BF16, "kv_up")
    cq_n, qb2 = _qprep_fwd(pq, mla_q_norm_g, gqa_q_norm_g, lcos_b, lss_b, tr)
    q_a = _mm(cq_n, wq_t, "NT", F32, "q_up")
    qar = _qrope_fwd(q_a, lcos_a, lss_a, tr)

    a_q = [(qar, lambda h: 3 * (h // 2) + h % 2), (qar, lambda h: 3 * (h // 2) + 2)]
    a_k = [(kv_up, lambda h: 2 * h), (kpe2, lambda h: h % 2)]
    a_v = (kv_up, lambda h: 2 * h + 1)
    a_scale = float(MLA_NOPE + MLA_ROPE) ** -0.5
    b_q = [(qb2, lambda h: h)]
    b_k = [(kb2, lambda h: h)]
    b_v = (vb2, lambda h: h)
    b_scale = float(GQA_HEAD_DIM) ** -0.5
    tq_f = _pick(T, 512)
    o_a, lse_a = _attn_fwd(a_q, a_k, a_v, MLA_HEADS, 1, MLA_V, a_scale, "attn_a_fwd", tq_f)
    br_landed = _gather_land(ag_br, o_a, "br")
    o_b, lse_b = _attn_fwd(b_q, b_k, b_v, GQA_HEADS, GQA_GROUP, GQA_HEAD_DIM, b_scale, "attn_b_fwd", tq_f, after=br_landed[1][4])
    wbra_t, wbrb_t, wout = _gather_done(br_landed, o_b, "br")
    up_landed = _gather_land(ag_up, o_b, "up")
    ya = _mm(o_a, wbra_t, "NT", BF16, "br_a", after=up_landed[1][4])
    yb = _mm(o_b, wbrb_t, "NT", BF16, "br_b")
    merged = _gates_fwd(pg, ya, yb, tr)
    att = _mm(merged, wout, "NN", F32, "out_proj")
    x1, z2 = _resid_norm2_fwd(x2d, att, g1, norm2_g, sh2, sc2, tr)
    (wup_t,) = _gather_done(up_landed, z2, "up")
    down_landed = _gather_land(ag_down, z2, "down")
    tc = _pick(FF, 128)
    u_a, u_b, hg = _ffn_up_conv(z2, wup_t, cw, conv_b, tc, down_landed[1][4])
    (wdown,) = _gather_done(down_landed, hg, "down")
    f = _mm(hg, wdown, "NN", F32, "ffn_down", tk=FF // 2)
    sq, dx2, d_fg, d_g2, df = _loss_head(x1, f, g2, fg, tgt, tr)
    loss = lax.psum(0.5 * jnp.sum(sq) / D, ("x", "y", "c"))

    dhg = _mm(df, wdown, "NT", BF16, "ffn_down_dx")
    g_wdown = _mm(hg, df, "TN", BF16, "ffn_down_dw", tm=FF // 4)
    du_a, du_b, dcw_a, dcw_b, dcb_a, dcb_b = _conv_bwd(u_a, u_b, dhg, cw, conv_b, tc)
    dz2 = _mm(du_a, wup_t, "NN", F32, "ffn_up_dx_a", tk=FF // 2)
    dz2 = _mm(du_b, wup_t, "NN", F32, "ffn_up_dx_b", b_off=FF, add=dz2, tk=FF // 2)
    g_wup_t = _mm(du_a, z2, "TN", BF16, "ffn_up_dw_a", out_rows=F2, tm=FF // 4)
    g_wup_t = _mm(du_b, z2, "TN", BF16, "ffn_up_dw_b", out_base=g_wup_t, out_off=FF, tm=FF // 4)
    sw_ffn = _swap_start([g_wdown, g_wup_t], sc2, "ffn")
    sc2 = sc2 + sw_ffn[4]
    dx1, datt, d_n2g, d_sh2, d_sc2, d_g1 = _resid_norm2_bwd(dz2, x1, dx2, att, norm2_g, sc2, g1, tr)

    dmerged = _mm(datt, wout, "NT", BF16, "out_proj_dx")
    rs_ffn = _scatter_start_after_swap(sw_ffn, dmerged, "ffn")
    lse_a = lse_a + rs_ffn[4]
    g_wout = _mm(merged, datt, "TN", BF16, "out_proj_dw")
    dya, dyb, dpg = _gates_bwd(dmerged, pg, ya, yb, tr)
    do_a = _mm(dya, wbra_t, "NN", BF16, "br_a_dx")
    g_wbra_t = _mm(dya, o_a, "TN", BF16, "br_a_dw")
    do_b = _mm(dyb, wbrb_t, "NN", BF16, "br_b_dx")
    g_wbrb_t = _mm(dyb, o_b, "TN", BF16, "br_b_dw")
    dqa2, dka2, dva2 = _attn_bwd(a_q, a_k, a_v, o_a, do_a, lse_a, MLA_HEADS, 1, MLA_V, a_scale, "attn_a_bwd", tq_f)
    dqb2, dkb2, dvb2 = _attn_bwd(b_q, b_k, b_v, o_b, do_b, lse_b, GQA_HEADS, GQA_GROUP, GQA_HEAD_DIM, b_scale, "attn_b_bwd", tq_f)
    dq_a = _qrope_bwd(dqa2, lcos_a, lss_a, tr)
    dcq_n = _mm(dq_a, wq_t, "NN", F32, "q_up_dx")
    g_wq_t = _mm(dq_a, cq_n, "TN", BF16, "q_up_dw")
    dpq, d_qg, d_gq = _qprep_bwd(pq, dcq_n, dqb2, mla_q_norm_g, gqa_q_norm_g, lcos_b, lss_b, tr)
    dkv_up, dkpe = _kgrad_split(dka2, dva2, cos_a, ss_a, tr)
    dckv_n = _mm(dkv_up, wkv_t, "NN", F32, "kv_up_dx")
    g_wkv_t = _mm(dkv_up, ckv_n, "TN", BF16, "kv_up_dw")
    rs_mix = _scatter_start([g_wq_t, g_wkv_t, g_wbra_t, g_wbrb_t, g_wout], "mix")
    dpkv, d_kvg, d_kg = _kprep_bwd(pkv, dckv_n, dkb2, dvb2, dkpe, mla_kv_norm_g + rs_mix[4], gqa_k_norm_g, cos_b, ss_b, tr)
    dz_kv = _mm(dpkv, w_kvp, "NN", F32, "proj_kv_dx")
    dz_lat = _mm(dpq, win_t, "NN", F32, "proj_q_dx", b_off=kv_cols, add=dz_kv)
    dz_lat = _mm(dpg, win_t, "NN", F32, "proj_g_dx", b_off=kv_cols + QC, add=dz_lat)
    _, d_n1g_c, d_csh, d_csc = _norm_mod_bwd(dz_kv, T // tr, ctx2d, norm1_g, csc, None, "norm1_ctx_bwd", tr)
    grad_x, d_n1g_l, d_sh1, d_sc1 = _norm_mod_bwd(dz_lat, 0, x2d, norm1_g, sc1, dx1, "norm1_lat_bwd", tr)

    zeros_d = jnp.zeros((1, D), F32)
    d_lat = jnp.concatenate([d_sh1, d_sc1, d_g1, d_sh2, d_sc2, d_g2], axis=1)
    d_ctx_part = jnp.concatenate([d_csh, d_csc], axis=1)
    flat = jnp.concatenate(
        [d_n1g_c + d_n1g_l, d_qg, d_kvg, d_gq, d_kg, d_n2g, dcb_a, dcb_b, d_fg,
         dcw_a.reshape(1, -1), dcw_b.reshape(1, -1), d_ctx_part, d_lat], axis=1)
    n_flat = flat.shape[1]
    n_rows = -(-n_flat // (8 * LANES)) * 8
    flat = jnp.pad(flat, ((0, 0), (0, n_rows * LANES - n_flat))).reshape(n_rows, LANES)
    got = _all_gather_small(flat, "gather_small_grads")
    tot = _sum_slots(got, "sum_small_grads").reshape(1, -1)
    sizes = [D, MLA_Q_LORA, MLA_KV_LORA, GQA_HEAD_DIM, GQA_HEAD_DIM, D, F2, D, 3 * FF, 3 * FF, 2 * D]
    offs = [0]
    for s in sizes:
        offs.append(offs[-1] + s)
    t_n1g, t_qg, t_kvg, t_gq, t_kg, t_n2g, t_cb, t_fg, t_cwa, t_cwb, t_ctx = [tot[:, offs[k] : offs[k + 1]] for k in range(len(sizes))]
    g_cw_full = jnp.concatenate([t_cwa.reshape(3, FF), t_cwb.reshape(3, FF)], axis=1)
    g_cw = lax.dynamic_slice(g_cw_full, (0, j * NW), (3, NW))
    d_lat_all = got.reshape(8, -1)[:, offs[-1] : offs[-1] + 6 * D]
    g16 = jnp.concatenate([d_lat_all, jnp.pad(t_ctx, ((0, 0), (0, 4 * D))), jnp.zeros((7, 6 * D), F32)], axis=0)
    g_b_ada = _sum_slots(g16.reshape(16, 1, 6 * D), "sum_b_ada")
    g16_cols = lax.dynamic_slice(g16, (0, j * NA), (16, NA))
    ds_part = _mm(g16_cols, w_ada[0], "NT", F32, "ada_dx")
    got = _all_gather_small(ds_part[8:16], "gather_ada_dx")
    ds_ctx = _sum_slots(jnp.stack([got[2 * s] for s in range(4)]), "sum_ada_dx")[0:1]
    g_c_ctx = _silu_grad_mul(ds_ctx, cc)

    g_kvp = _mm(dpkv, z_all, "TN", BF16, "proj_kv_dw")
    nk = MLA_KV_LORA + 2 * GQA_KV_HEADS * GQA_HEAD_DIM
    g_kv = jnp.concatenate([g_kvp[:MLA_KV_LORA], g_kvp[nk : nk + MLA_ROPE], g_kvp[MLA_KV_LORA:nk]], axis=0)
    g_win_t = _mm(dpq, z_all, "TN", BF16, "proj_q_dw", out_rows=kv_cols + QC + 2 * D, out_off=kv_cols, tm=QC // 2)
    g_win_t = _mm(dpg, z_all, "TN", BF16, "proj_g_dw", out_base=g_win_t, out_off=kv_cols + QC)
    g_win_t = lax.dynamic_update_slice(g_win_t, g_kv, (0, 0))
    sw_in = _swap_start([g_win_t], got, "in")

    h_ffn = _scatter_sums(rs_ffn, sw_in[2][0], "ffn")
    j_ffn = _join_start(h_ffn, grad_x, "ffn")
    h_mix = _scatter_sums(rs_mix, j_ffn[2][0], "mix")
    j_mix = _join_start(h_mix, j_ffn[2][0], "mix")
    rs_in = _scatter_start_after_swap(sw_in, j_mix[2][0], "in")
    g_w_ada = _mm(s16, g16_cols, "TN", F32, "ada_dw", act="silu", after=rs_in[4])
    _, d_ada, m_ada, v_ada = _adamw(w_ada[0], g_w_ada, m_w_ada[0], v_w_ada[0], "adamw_w_ada")
    r_wdown, r_wup = _join_wait(j_ffn, d_ada, "ffn")
    r_wq, r_wkv, r_wbra, r_wbrb, r_wout = _join_wait(j_mix, d_ada, "mix")
    gq_p = _joined(*r_wq).T
    gq = jnp.concatenate([gq_p[:, : 2 * MLA_NOPE].reshape(MLA_Q_LORA, 2, MLA_NOPE), gq_p[:, 2 * MLA_NOPE :].reshape(MLA_Q_LORA, 2, MLA_ROPE)], axis=2)
    grads = {
        "c_ctx": g_c_ctx.reshape(D), "w_ada": g_w_ada[None], "b_ada": g_b_ada, "norm1_g": t_n1g,
        "mla_q_norm_g": t_qg, "w_q_up": gq.reshape(1, MLA_Q_LORA, -1), "mla_kv_norm_g": t_kvg, "w_kv_up": r_wkv,
        "gqa_q_norm_g": t_gq, "gqa_k_norm_g": t_kg, "w_br_a": r_wbra, "w_br_b": r_wbrb, "w_out": r_wout,
        "norm2_g": t_n2g, "w_up": r_wup, "conv_w": g_cw[None], "conv_b": t_cb, "w_down": r_wdown,
        "final_norm_g": t_fg.reshape(D),
    }
    arrives_transposed = ("w_kv_up", "w_br_a", "w_br_b", "w_up")
    arrives_halved = arrives_transposed + ("w_out", "w_down")
    weights = dict(c_ctx=c_ctx, w_ada=w_ada, b_ada=b_ada, norm1_g=norm1_g, w_in=w_in, mla_q_norm_g=mla_q_norm_g, w_q_up=w_q_up,
                   mla_kv_norm_g=mla_kv_norm_g, w_kv_up=w_kv_up, gqa_q_norm_g=gqa_q_norm_g, gqa_k_norm_g=gqa_k_norm_g, w_br_a=w_br_a,
                   w_br_b=w_br_b, w_out=w_out, norm2_g=norm2_g, w_up=w_up, conv_w=conv_w, conv_b=conv_b, w_down=w_down,
                   final_norm_g=final_norm_g)
    m_in = dict(c_ctx=m_c_ctx, w_ada=m_w_ada, b_ada=m_b_ada, norm1_g=m_norm1_g, w_in=m_w_in, mla_q_norm_g=m_mla_q_norm_g,
                w_q_up=m_w_q_up, mla_kv_norm_g=m_mla_kv_norm_g, w_kv_up=m_w_kv_up, gqa_q_norm_g=m_gqa_q_norm_g,
                gqa_k_norm_g=m_gqa_k_norm_g, w_br_a=m_w_br_a, w_br_b=m_w_br_b, w_out=m_w_out, norm2_g=m_norm2_g, w_up=m_w_up,
                conv_w=m_conv_w, conv_b=m_conv_b, w_down=m_w_down, final_norm_g=m_final_norm_g)
    v_in = dict(c_ctx=v_c_ctx, w_ada=v_w_ada, b_ada=v_b_ada, norm1_g=v_norm1_g, w_in=v_w_in, mla_q_norm_g=v_mla_q_norm_g,
                w_q_up=v_w_q_up, mla_kv_norm_g=v_mla_kv_norm_g, w_kv_up=v_w_kv_up, gqa_q_norm_g=v_gqa_q_norm_g,
                gqa_k_norm_g=v_gqa_k_norm_g, w_br_a=v_w_br_a, w_br_b=v_w_br_b, w_out=v_w_out, norm2_g=v_norm2_g, w_up=v_w_up,
                conv_w=v_conv_w, conv_b=v_conv_b, w_down=v_w_down, final_norm_g=v_final_norm_g)
    names = list(weights)
    big = [n for n in names if weights[n].ndim == 3 and weights[n].shape[1] >= 8]
    small = [n for n in names if n not in big]
    delta, new_m, new_v = {}, {}, {}

    def update(n):
        shp = weights[n].shape
        two_d = lambda a: a.reshape(shp[1], shp[2])
        g_t = n in arrives_transposed
        if n in arrives_halved:
            g_in, g_sib = grads[n]
        else:
            g_in, g_sib = two_d(grads[n].astype(F32)), None
        g_, d_, m_, v_ = _adamw(two_d(weights[n]), g_in, two_d(m_in[n]), two_d(v_in[n]), "adamw_" + n, g_transposed=g_t, g_sibling=g_sib)
        grads[n], delta[n], new_m[n], new_v[n] = g_.reshape(shp), d_.reshape(shp), m_.reshape(shp), v_.reshape(shp)

    delta["w_ada"], new_m["w_ada"], new_v["w_ada"] = d_ada[None], m_ada[None], v_ada[None]
    early = [n for n in big if n not in ("w_in", "w_ada")]
    for n in early[:-1]:
        update(n)
    done = sum(delta[n][0, 0:1, 0:1] for n in early[:-1])
    j_in = _join_start(_scatter_sums(rs_in, done, "in"), done, "in")
    last = early[-1]
    grads[last] = (grads[last][0] + j_in[4], grads[last][1])
    update(last)
    ((g_mine, g_sib),) = _join_wait(j_in, delta[last], "in")
    g_, d_, m_, v_ = _adamw(w_in[0].T, g_mine, m_w_in[0].T, v_w_in[0].T, "adamw_w_in", g_sibling=g_sib)
    grads["w_in"], delta["w_in"], new_m["w_in"], new_v["w_in"] = g_.T[None], d_.T[None], m_.T[None], v_.T[None]
    grads = {n: grads[n].reshape(weights[n].shape).astype(F32) for n in names}

    slab = lambda tree: [tree[n].reshape(-1, LANES) for n in small]
    d_, m_, v_ = _adamw_many(slab(weights), slab(grads), slab(m_in), slab(v_in), "adamw_small")
    for k, n in enumerate(small):
        shp = weights[n].shape
        delta[n], new_m[n], new_v[n] = d_[k].reshape(shp), m_[k].reshape(shp), v_[k].reshape(shp)

    return (loss, grad_x[None], *[grads[n] for n in names], *[delta[n] for n in names], *[new_m[n] for n in names],
            *[new_v[n] for n in names])
```

```python
import math

import jax
import jax.numpy as jnp
from jax import lax
from jax.experimental import pallas as pl
from jax.experimental.pallas import tpu as pltpu

F32 = jnp.float32
BF16 = jnp.bfloat16
MESH = pl.DeviceIdType.MESH

NORM_EPS = 1e-6
ROPE_THETA = 10000.0
GRID_W = 64
MLA_HEADS = 8
MLA_Q_LORA = 768
MLA_KV_LORA = 512
MLA_NOPE = 128
MLA_ROPE = 64
MLA_V = 128
GQA_HEADS = 8
GQA_KV_HEADS = 2
GQA_HEAD_DIM = 128
GQA_GROUP = GQA_HEADS // GQA_KV_HEADS
LANES = 128
KVP = MLA_KV_LORA + 2 * GQA_KV_HEADS * GQA_HEAD_DIM + LANES
QC = MLA_Q_LORA + GQA_HEADS * GQA_HEAD_DIM

ADAM_LR = 0.001
ADAM_B1 = 0.9
ADAM_B2 = 0.999
ADAM_EPS = 1e-08
ADAM_WD = 0.01
ADAM_STEP = 10

VMEM_LIMIT = 56 * 1024 * 1024


def _pick(dim, target, mult=LANES):
    t = (min(target, dim) // mult) * mult
    while t >= mult:
        if dim % t == 0:
            return t
        t -= mult
    return dim


def _params(sem):
    return pltpu.CompilerParams(dimension_semantics=sem, vmem_limit_bytes=VMEM_LIMIT)


_DIMS = {"NN": (((1,), (0,)), ((), ())), "NT": (((1,), (1,)), ((), ())), "TN": (((0,), (0,)), ((), ()))}


MM_VMEM_BUDGET = 36 * 1024 * 1024


def _mm_tiles(M, N, K, sa, sb, so, tm, tn, tk):
    tm, tn, tk = _pick(M, tm), _pick(N, tn), _pick(K, tk)

    def need(t):
        return 2 * (tm * t * sa + t * tn * sb) + 2 * tm * tn * so + (tm * tn * 4 if t < K else 0)

    while need(tk) > MM_VMEM_BUDGET and tk > LANES:
        smaller = _pick(K, tk - LANES)
        if smaller >= tk:
            break
        tk = smaller
    return tm, tn, tk


def _window(block, index, offsets):
    if not any(offsets):
        return pl.BlockSpec(block, index)
    for t, o in zip(block, offsets):
        assert o % 16 == 0 and t % 16 == 0, (block, offsets)

    def at(i, j, k):
        return tuple(pl.multiple_of(o + p * t, math.gcd(o, t)) for p, t, o in zip(index(i, j, k), block, offsets))

    return pl.BlockSpec(tuple(pl.Element(t) for t in block), at)


def _mm(a, b, mode, out_dtype, name, m=None, n=None, k=None, b_off=0, add=None, out_rows=None, out_base=None, out_off=0,
        tm=1024, tn=1024, tk=2304, act=None, bias=None, after=None):
    if mode == "NN":
        M, K, N = m or a.shape[0], k or a.shape[1], b.shape[1]
    elif mode == "NT":
        M, K, N = m or a.shape[0], a.shape[1], n or b.shape[0]
    else:
        M, K, N = a.shape[1], k or a.shape[0], b.shape[1]
    tm, tn, tk = _mm_tiles(M, N, K, a.dtype.itemsize, b.dtype.itemsize, jnp.dtype(out_dtype).itemsize, tm, tn, tk)
    nk = K // tk
    dims = _DIMS[mode]
    n_in = 2 + (bias is not None) + (add is not None) + (out_base is not None) + (after is not None)

    def body(*refs):
        a_ref, b_ref = refs[:2]
        bias_ref = refs[2] if bias is not None else None
        add_ref = refs[2 + (bias is not None)] if add is not None else None
        o_ref = refs[n_in]
        av = a_ref[...]
        if act == "silu":
            av = av * jax.nn.sigmoid(av)
        part = lax.dot_general(av.astype(BF16), b_ref[...].astype(BF16), dims, preferred_element_type=F32)

        def finish(r):
            if bias is not None:
                r = r + bias_ref[...]
            if add is not None:
                r = r + add_ref[...]
            o_ref[...] = r.astype(out_dtype)

        if nk == 1:
            finish(part)
            return
        acc = refs[-1]
        k = pl.program_id(2)

        @pl.when(k == 0)
        def _():
            acc[...] = part

        @pl.when(jnp.logical_and(k > 0, k < nk - 1))
        def _():
            acc[...] += part

        @pl.when(k == nk - 1)
        def _():
            finish(acc[...] + part)

    a_spec = pl.BlockSpec((tk, tm), lambda i, j, k: (k, i)) if mode == "TN" else pl.BlockSpec((tm, tk), lambda i, j, k: (i, k))
    if mode == "NT":
        b_spec = _window((tn, tk), lambda i, j, k: (j, k), (b_off, 0))
    else:
        b_spec = _window((tk, tn), lambda i, j, k: (k, j), (b_off, 0))
    in_specs, args = [a_spec, b_spec], [a, b]
    if bias is not None:
        in_specs.append(pl.BlockSpec((1, tn), lambda i, j, k: (0, j)))
        args.append(bias)
    if add is not None:
        in_specs.append(pl.BlockSpec((tm, tn), lambda i, j, k: (i, j)))
        args.append(add)
    aliases = {}
    if after is not None:
        in_specs.append(pl.BlockSpec(after.shape, lambda i, j, k: (0, 0)))
        args.append(after)
    if out_base is not None:
        aliases = {len(args): 0}
        in_specs.append(ANY)
        args.append(out_base)
        out_rows = out_base.shape[0]
    return pl.pallas_call(
        body,
        name=name,
        grid=(M // tm, N // tn, nk),
        in_specs=in_specs,
        out_specs=_window((tm, tn), lambda i, j, k: (i, j), (out_off, 0)),
        out_shape=jax.ShapeDtypeStruct((out_rows or M, N), out_dtype),
        input_output_aliases=aliases,
        scratch_shapes=[pltpu.VMEM((tm, tn), F32)] if nk > 1 else [],
        compiler_params=_params(("parallel", "parallel", "arbitrary")),
    )(*args)


def _rms(x):
    r = lax.rsqrt(jnp.mean(x * x, axis=-1, keepdims=True) + NORM_EPS)
    return x * r, r


def _rms_bwd(xh, r, dxh):
    return r * (dxh - xh * jnp.mean(dxh * xh, axis=-1, keepdims=True))


def _swap(x, q):
    lane = lax.broadcasted_iota(jnp.int32, x.shape, 1)
    even = ((lane // q) % 2) == 0
    return jnp.where(even, pltpu.roll(x, LANES - q, 1), pltpu.roll(x, q, 1))


def _rope(x, cos, ss, q):
    return x * cos + _swap(x, q) * ss


def _rope_t(d, cos, ss, q):
    return d * cos + _swap(d * ss, q)


def _csum(x):
    return jnp.sum(x, axis=0, keepdims=True)


def _rows(tr, w, off=0):
    return pl.BlockSpec((tr, w), lambda i: (i + off, 0))


def _bcast(w):
    return pl.BlockSpec((1, w), lambda i: (0, 0))


def _acc_init(i, refs):
    @pl.when(i == 0)
    def _():
        for r in refs:
            r[...] = jnp.zeros_like(r)


def _rope_tables(n_ctx, n_lat, rot_dim):
    rows = n_lat // GRID_W
    row = jnp.repeat(jnp.arange(rows, dtype=F32), GRID_W)
    col = jnp.tile(jnp.arange(GRID_W, dtype=F32), rows)
    half = rot_dim // 2
    inv_freq = ROPE_THETA ** (-jnp.arange(0, half, 2, dtype=F32) / half)
    ar, ac = row[:, None] * inv_freq, col[:, None] * inv_freq
    cos = jnp.concatenate([jnp.cos(ar), jnp.cos(ar), jnp.cos(ac), jnp.cos(ac)], axis=-1)
    ss = jnp.concatenate([-jnp.sin(ar), jnp.sin(ar), -jnp.sin(ac), jnp.sin(ac)], axis=-1)
    cos = jnp.tile(cos, (1, LANES // rot_dim))
    ss = jnp.tile(ss, (1, LANES // rot_dim))
    cos = jnp.concatenate([cos, jnp.ones((n_ctx, LANES), F32)], axis=0)
    ss = jnp.concatenate([ss, jnp.zeros((n_ctx, LANES), F32)], axis=0)
    return cos, ss


def _norm_mod_fwd(x2d, g, sh, sc, name, tr, out_rows=None, base=None, out_off=0):
    n, d = x2d.shape

    def body(x_ref, g_ref, sh_ref, sc_ref, *rest):
        xh, _ = _rms(x_ref[...])
        rest[-1][...] = ((xh * g_ref[...]) * (1.0 + sc_ref[...]) + sh_ref[...]).astype(BF16)

    args, in_specs, aliases = [x2d, g, sh, sc], [_rows(tr, d), _bcast(d), _bcast(d), _bcast(d)], {}
    if base is not None:
        args.append(base)
        in_specs.append(ANY)
        aliases = {4: 0}
        out_rows = base.shape[0]
    return pl.pallas_call(
        body,
        name=name,
        grid=(n // tr,),
        in_specs=in_specs,
        out_specs=_rows(tr, d, out_off // tr),
        out_shape=jax.ShapeDtypeStruct((out_rows or n, d), BF16),
        input_output_aliases=aliases,
        compiler_params=_params(("parallel",)),
    )(*args)


def _norm_mod_bwd(dz, dz_off, x2d, g, sc, dres, name, tr):
    n, d = x2d.shape
    want_dx = dres is not None

    def body(*refs):
        if want_dx:
            dz_ref, x_ref, g_ref, sc_ref, dres_ref, dx_ref, dg_ref, dsh_ref, dsc_ref = refs
        else:
            dz_ref, x_ref, g_ref, sc_ref, dg_ref, dsh_ref, dsc_ref = refs
        _acc_init(pl.program_id(0), [dg_ref, dsh_ref, dsc_ref])
        xh, r = _rms(x_ref[...])
        dzv = dz_ref[...]
        gv = g_ref[...]
        dsc_ref[...] += _csum(dzv * (xh * gv))
        dsh_ref[...] += _csum(dzv)
        dh = dzv * (1.0 + sc_ref[...])
        dg_ref[...] += _csum(dh * xh)
        if want_dx:
            dx_ref[...] = _rms_bwd(xh, r, dh * gv) + dres_ref[...]

    in_specs = [_rows(tr, d, dz_off), _rows(tr, d), _bcast(d), _bcast(d)]
    args = [dz, x2d, g, sc]
    out_specs = [_bcast(d)] * 3
    out_shape = [jax.ShapeDtypeStruct((1, d), F32)] * 3
    if want_dx:
        in_specs.append(_rows(tr, d))
        args.append(dres)
        out_specs = [_rows(tr, d)] + out_specs
        out_shape = [jax.ShapeDtypeStruct((n, d), F32)] + out_shape
    res = pl.pallas_call(
        body,
        name=name,
        grid=(n // tr,),
        in_specs=in_specs,
        out_specs=out_specs,
        out_shape=out_shape,
        compiler_params=_params(("arbitrary",)),
    )(*args)
    return res if want_dx else (None, *res)


_QA, _QB = MLA_ROPE // 4, GQA_HEAD_DIM // 4


def _kprep_fwd(pkv, kvg, kg, cos_a, ss_a, cos_b, ss_b, tr):
    n = pkv.shape[0]
    nb = GQA_KV_HEADS * GQA_HEAD_DIM

    def body(p_ref, kvg_ref, kg_ref, ca, sa, cb, sb, ckv_ref, kb_ref, vb_ref, kpe_ref):
        p = p_ref[...]
        xh, _ = _rms(p[:, :MLA_KV_LORA])
        ckv_ref[...] = (xh * kvg_ref[...]).astype(BF16)
        for e in range(GQA_KV_HEADS):
            lo = MLA_KV_LORA + e * GQA_HEAD_DIM
            kh, _ = _rms(p[:, lo : lo + GQA_HEAD_DIM])
            kb_ref[:, e * GQA_HEAD_DIM : (e + 1) * GQA_HEAD_DIM] = _rope(kh * kg_ref[...], cb[...], sb[...], _QB).astype(BF16)
        vb_ref[...] = p[:, MLA_KV_LORA + nb : MLA_KV_LORA + 2 * nb].astype(BF16)
        kr = _rope(p[:, MLA_KV_LORA + 2 * nb :], ca[...], sa[...], _QA)
        kpe_ref[:, :LANES] = kr.astype(BF16)
        kpe_ref[:, LANES:] = pltpu.roll(kr, MLA_ROPE, 1).astype(BF16)

    return pl.pallas_call(
        body,
        name="kprep_fwd",
        grid=(n // tr,),
        in_specs=[_rows(tr, KVP), _bcast(MLA_KV_LORA), _bcast(GQA_HEAD_DIM)] + [_rows(tr, LANES)] * 4,
        out_specs=[_rows(tr, MLA_KV_LORA), _rows(tr, nb), _rows(tr, nb), _rows(tr, 2 * LANES)],
        out_shape=[jax.ShapeDtypeStruct((n, w), BF16) for w in (MLA_KV_LORA, nb, nb, 2 * LANES)],
        compiler_params=_params(("parallel",)),
    )(pkv, kvg, kg, cos_a, ss_a, cos_b, ss_b)


def _kprep_bwd(pkv, dckv, dkb, dvb, dkpe, kvg, kg, cos_b, ss_b, tr):
    n = pkv.shape[0]
    nb = GQA_KV_HEADS * GQA_HEAD_DIM

    def body(p_ref, dckv_ref, dkb_ref, dvb_ref, dkpe_ref, kvg_ref, kg_ref, cb, sb, dp_ref, dkvg_ref, dkg_ref):
        _acc_init(pl.program_id(0), [dkvg_ref, dkg_ref])
        p = p_ref[...]
        xh, r = _rms(p[:, :MLA_KV_LORA])
        dn = dckv_ref[...]
        dkvg_ref[...] += _csum(dn * xh)
        dp_ref[:, :MLA_KV_LORA] = _rms_bwd(xh, r, dn * kvg_ref[...]).astype(BF16)
        for e in range(GQA_KV_HEADS):
            lo = MLA_KV_LORA + e * GQA_HEAD_DIM
            kh, rk = _rms(p[:, lo : lo + GQA_HEAD_DIM])
            dk = _rope_t(dkb_ref[:, e * GQA_HEAD_DIM : (e + 1) * GQA_HEAD_DIM], cb[...], sb[...], _QB)
            dkg_ref[...] += _csum(dk * kh)
            dp_ref[:, lo : lo + GQA_HEAD_DIM] = _rms_bwd(kh, rk, dk * kg_ref[...]).astype(BF16)
        dp_ref[:, MLA_KV_LORA + nb : MLA_KV_LORA + 2 * nb] = dvb_ref[...].astype(BF16)
        dp_ref[:, MLA_KV_LORA + 2 * nb :] = dkpe_ref[...].astype(BF16)

    return pl.pallas_call(
        body,
        name="kprep_bwd",
        grid=(n // tr,),
        in_specs=[_rows(tr, KVP), _rows(tr, MLA_KV_LORA), _rows(tr, nb), _rows(tr, nb), _rows(tr, LANES),
                  _bcast(MLA_KV_LORA), _bcast(GQA_HEAD_DIM), _rows(tr, LANES), _rows(tr, LANES)],
        out_specs=[_rows(tr, KVP), _bcast(MLA_KV_LORA), _bcast(GQA_HEAD_DIM)],
        out_shape=[jax.ShapeDtypeStruct((n, KVP), BF16), jax.ShapeDtypeStruct((1, MLA_KV_LORA), F32),
                   jax.ShapeDtypeStruct((1, GQA_HEAD_DIM), F32)],
        compiler_params=_params(("arbitrary",)),
    )(pkv, dckv, dkb, dvb, dkpe, kvg, kg, cos_b, ss_b)


def _kgrad_split(dka, dva, cos_a, ss_a, tr):
    n = dka.shape[0]
    wk = MLA_HEADS * 2 * LANES

    def body(dk_ref, dv_ref, ca, sa, dkv_ref, dkpe_ref):
        even = jnp.zeros((tr, LANES), F32)
        odd = jnp.zeros((tr, LANES), F32)
        for h in range(MLA_HEADS):
            dkv_ref[:, 2 * h * LANES : (2 * h + 1) * LANES] = dk_ref[:, 2 * h * LANES : (2 * h + 1) * LANES].astype(BF16)
            dkv_ref[:, (2 * h + 1) * LANES : (2 * h + 2) * LANES] = dv_ref[:, h * MLA_V : (h + 1) * MLA_V].astype(BF16)
            part = dk_ref[:, (2 * h + 1) * LANES : (2 * h + 2) * LANES]
            if h % 2 == 0:
                even = even + part
            else:
                odd = odd + part
        lane = lax.broadcasted_iota(jnp.int32, (tr, LANES), 1)
        low = lane < MLA_ROPE
        both = jnp.where(low, even, odd)
        tot = jnp.where(low, both + pltpu.roll(both, MLA_ROPE, 1), 0.0)
        dkpe_ref[...] = _rope_t(tot, ca[...], sa[...], _QA)

    return pl.pallas_call(
        body,
        name="kgrad_split",
        grid=(n // tr,),
        in_specs=[_rows(tr, wk), _rows(tr, MLA_HEADS * MLA_V), _rows(tr, LANES), _rows(tr, LANES)],
        out_specs=[_rows(tr, wk), _rows(tr, LANES)],
        out_shape=[jax.ShapeDtypeStruct((n, wk), BF16), jax.ShapeDtypeStruct((n, LANES), F32)],
        compiler_params=_params(("parallel",)),
    )(dka, dva, cos_a, ss_a)


def _qprep_fwd(pq, qg, gq, cos_b, ss_b, tr):
    n = pq.shape[0]
    nq = GQA_HEADS * GQA_HEAD_DIM

    def body(p_ref, qg_ref, gq_ref, cb, sb, cq_ref, qb_ref):
        xh, _ = _rms(p_ref[:, :MLA_Q_LORA])
        cq_ref[...] = (xh * qg_ref[...]).astype(BF16)
        for h in range(GQA_HEADS):
            lo = MLA_Q_LORA + h * GQA_HEAD_DIM
            qh, _ = _rms(p_ref[:, lo : lo + GQA_HEAD_DIM])
            qb_ref[:, h * GQA_HEAD_DIM : (h + 1) * GQA_HEAD_DIM] = _rope(qh * gq_ref[...], cb[...], sb[...], _QB).astype(BF16)

    return pl.pallas_call(
        body,
        name="qprep_fwd",
        grid=(n // tr,),
        in_specs=[_rows(tr, QC), _bcast(MLA_Q_LORA), _bcast(GQA_HEAD_DIM), _rows(tr, LANES), _rows(tr, LANES)],
        out_specs=[_rows(tr, MLA_Q_LORA), _rows(tr, nq)],
        out_shape=[jax.ShapeDtypeStruct((n, MLA_Q_LORA), BF16), jax.ShapeDtypeStruct((n, nq), BF16)],
        compiler_params=_params(("parallel",)),
    )(pq, qg, gq, cos_b, ss_b)


def _qprep_bwd(pq, dcq, dqb, qg, gq, cos_b, ss_b, tr):
    n = pq.shape[0]
    nq = GQA_HEADS * GQA_HEAD_DIM

    def body(p_ref, dcq_ref, dqb_ref, qg_ref, gq_ref, cb, sb, dp_ref, dqg_ref, dgq_ref):
        _acc_init(pl.program_id(0), [dqg_ref, dgq_ref])
        xh, r = _rms(p_ref[:, :MLA_Q_LORA])
        dn = dcq_ref[...]
        dqg_ref[...] += _csum(dn * xh)
        dp_ref[:, :MLA_Q_LORA] = _rms_bwd(xh, r, dn * qg_ref[...]).astype(BF16)
        for h in range(GQA_HEADS):
            lo = MLA_Q_LORA + h * GQA_HEAD_DIM
            qh, rq = _rms(p_ref[:, lo : lo + GQA_HEAD_DIM])
            dq = _rope_t(dqb_ref[:, h * GQA_HEAD_DIM : (h + 1) * GQA_HEAD_DIM], cb[...], sb[...], _QB)
            dgq_ref[...] += _csum(dq * qh)
            dp_ref[:, lo : lo + GQA_HEAD_DIM] = _rms_bwd(qh, rq, dq * gq_ref[...]).astype(BF16)

    return pl.pallas_call(
        body,
        name="qprep_bwd",
        grid=(n // tr,),
        in_specs=[_rows(tr, QC), _rows(tr, MLA_Q_LORA), _rows(tr, nq), _bcast(MLA_Q_LORA), _bcast(GQA_HEAD_DIM),
                  _rows(tr, LANES), _rows(tr, LANES)],
        out_specs=[_rows(tr, QC), _bcast(MLA_Q_LORA), _bcast(GQA_HEAD_DIM)],
        out_shape=[jax.ShapeDtypeStruct((n, QC), BF16), jax.ShapeDtypeStruct((1, MLA_Q_LORA), F32),
                   jax.ShapeDtypeStruct((1, GQA_HEAD_DIM), F32)],
        compiler_params=_params(("arbitrary",)),
    )(pq, dcq, dqb, qg, gq, cos_b, ss_b)


_QA_COLS = MLA_HEADS * (MLA_NOPE + MLA_ROPE)


def _qrope_fwd(qa, cos_a, ss_a, tr):
    n = qa.shape[0]

    def body(q_ref, ca, sa, o_ref):
        for j in range(MLA_HEADS // 2):
            lo = 3 * j * LANES
            o_ref[:, lo : lo + 2 * LANES] = q_ref[:, lo : lo + 2 * LANES].astype(BF16)
            o_ref[:, lo + 2 * LANES : lo + 3 * LANES] = _rope(q_ref[:, lo + 2 * LANES : lo + 3 * LANES], ca[...], sa[...], _QA).astype(BF16)

    return pl.pallas_call(
        body,
        name="qrope_fwd",
        grid=(n // tr,),
        in_specs=[_rows(tr, _QA_COLS), _rows(tr, LANES), _rows(tr, LANES)],
        out_specs=_rows(tr, _QA_COLS),
        out_shape=jax.ShapeDtypeStruct((n, _QA_COLS), BF16),
        compiler_params=_params(("parallel",)),
    )(qa, cos_a, ss_a)


def _qrope_bwd(dq2, cos_a, ss_a, tr):
    n = dq2.shape[0]

    def body(d_ref, ca, sa, o_ref):
        for j in range(MLA_HEADS // 2):
            lo = 3 * j * LANES
            h0, h1 = 2 * j, 2 * j + 1
            o_ref[:, lo : lo + LANES] = d_ref[:, 2 * h0 * LANES : (2 * h0 + 1) * LANES].astype(BF16)
            o_ref[:, lo + LANES : lo + 2 * LANES] = d_ref[:, 2 * h1 * LANES : (2 * h1 + 1) * LANES].astype(BF16)
            pe = d_ref[:, (2 * h0 + 1) * LANES : (2 * h0 + 2) * LANES] + d_ref[:, (2 * h1 + 1) * LANES : (2 * h1 + 2) * LANES]
            o_ref[:, lo + 2 * LANES : lo + 3 * LANES] = _rope_t(pe, ca[...], sa[...], _QA).astype(BF16)

    return pl.pallas_call(
        body,
        name="qrope_bwd",
        grid=(n // tr,),
        in_specs=[_rows(tr, MLA_HEADS * 2 * LANES), _rows(tr, LANES), _rows(tr, LANES)],
        out_specs=_rows(tr, _QA_COLS),
        out_shape=jax.ShapeDtypeStruct((n, _QA_COLS), BF16),
        compiler_params=_params(("parallel",)),
    )(dq2, cos_a, ss_a)


def _cat(refs):
    vals = [r[...] for r in refs]
    return vals[0] if len(vals) == 1 else jnp.concatenate(vals, axis=-1)


LOG2E = 1.4426950408889634


def _attn_fwd(qparts, kparts, vpart, n_heads, group, dv, scale, name, tq, after=None):
    T, Tk = qparts[0][0].shape[0], kparts[0][0].shape[0]
    nq_, nk_ = len(qparts), len(kparts)
    sub = min(tq, 256)
    c2 = scale * LOG2E

    def body(*refs):
        q_refs, k_refs = refs[:nq_], refs[nq_ : nq_ + nk_]
        v_ref = refs[nq_ + nk_]
        o_ref, lse_ref = refs[-2:]
        k = _cat(k_refs)
        v = v_ref[...]
        for r0 in range(0, tq, sub):
            q = _cat([r.at[r0 : r0 + sub, :] for r in q_refs])
            s = lax.dot_general(q, k, _DIMS["NT"], preferred_element_type=F32)
            m = jnp.max(s, axis=-1, keepdims=True)
            p = jnp.exp2((s - m) * c2)
            l = jnp.sum(p, axis=-1, keepdims=True)
            acc = jnp.dot(p.astype(BF16), v, preferred_element_type=F32)
            o_ref[r0 : r0 + sub, :] = (acc * (1.0 / l)).astype(BF16)
            lse_ref[r0 : r0 + sub, :] = m * scale + jnp.log(l)

    in_specs = [pl.BlockSpec((tq, LANES), lambda h, i, f=f: (i, f(h))) for _, f in qparts]
    in_specs += [pl.BlockSpec((Tk, LANES), lambda h, i, f=f: (0, f(h // group))) for _, f in kparts]
    fv = vpart[1]
    in_specs.append(pl.BlockSpec((Tk, dv), lambda h, i: (0, fv(h // group))))
    args = [*[a for a, _ in qparts], *[a for a, _ in kparts], vpart[0]]
    if after is not None:
        in_specs.append(pl.BlockSpec(after.shape, lambda h, i: (0, 0)))
        args.append(after)
    return pl.pallas_call(
        body,
        name=name,
        grid=(n_heads, T // tq),
        in_specs=in_specs,
        out_specs=[pl.BlockSpec((tq, dv), lambda h, i: (i, h)), pl.BlockSpec((None, tq, 1), lambda h, i: (h, i, 0))],
        out_shape=[jax.ShapeDtypeStruct((T, n_heads * dv), BF16), jax.ShapeDtypeStruct((n_heads, T, 1), F32)],
        compiler_params=_params(("parallel", "parallel")),
    )(*args)


def _attn_bwd(qparts, kparts, vpart, o, do, lse, n_heads, group, dv, scale, name, tq):
    T, Tk = qparts[0][0].shape[0], kparts[0][0].shape[0]
    nq_, nk_ = len(qparts), len(kparts)
    dk_ = LANES * nq_
    n_kv = n_heads // group
    nblk = T // tq
    c2 = scale * LOG2E

    def head(hk, i):
        return hk * group + i // nblk

    sub = min(tq, 256)

    def body(*refs):
        q_refs = refs[:nq_]
        k = _cat(refs[nq_ : nq_ + nk_])
        v_ref, o_ref, do_ref, lse_ref, dq_ref, dk_ref, dv_ref = refs[nq_ + nk_ :]
        i = pl.program_id(1)
        _acc_init(i, [dk_ref, dv_ref])
        v = v_ref[...]
        dk_acc, dv_acc = None, None
        for r0 in range(0, tq, sub):
            rows = slice(r0, r0 + sub)
            q = _cat([r.at[rows, :] for r in q_refs])
            s = lax.dot_general(q, k, _DIMS["NT"], preferred_element_type=F32)
            p = jnp.exp2(s * c2 - lse_ref[rows, :] * LOG2E)
            dov = do_ref[rows, :]
            dp = lax.dot_general(dov, v, _DIMS["NT"], preferred_element_type=F32)
            delta = jnp.sum(dov.astype(F32) * o_ref[rows, :].astype(F32), axis=-1, keepdims=True)
            ds = (p * (dp - delta)).astype(BF16)
            dq_ref[rows, :] = jnp.dot(ds, k, preferred_element_type=F32) * scale
            dk_part = lax.dot_general(ds, q, _DIMS["TN"], preferred_element_type=F32)
            dv_part = lax.dot_general(p.astype(BF16), dov, _DIMS["TN"], preferred_element_type=F32)
            dk_acc = dk_part if dk_acc is None else dk_acc + dk_part
            dv_acc = dv_part if dv_acc is None else dv_acc + dv_part
        dk_ref[...] += dk_acc
        dv_ref[...] += dv_acc

        @pl.when(i == group * nblk - 1)
        def _():
            dk_ref[...] *= scale

    in_specs = [pl.BlockSpec((tq, LANES), lambda hk, i, f=f: (i % nblk, f(head(hk, i)))) for _, f in qparts]
    in_specs += [pl.BlockSpec((Tk, LANES), lambda hk, i, f=f: (0, f(hk))) for _, f in kparts]
    fv = vpart[1]
    in_specs.append(pl.BlockSpec((Tk, dv), lambda hk, i: (0, fv(hk))))
    in_specs += [pl.BlockSpec((tq, dv), lambda hk, i: (i % nblk, head(hk, i)))] * 2
    in_specs.append(pl.BlockSpec((None, tq, 1), lambda hk, i: (head(hk, i), i % nblk, 0)))
    return pl.pallas_call(
        body,
        name=name,
        grid=(n_kv, group * nblk),
        in_specs=in_specs,
        out_specs=[pl.BlockSpec((tq, dk_), lambda hk, i: (i % nblk, head(hk, i))),
                   pl.BlockSpec((Tk, dk_), lambda hk, i: (0, hk)),
                   pl.BlockSpec((Tk, dv), lambda hk, i: (0, hk))],
        out_shape=[jax.ShapeDtypeStruct((T, n_heads * dk_), F32), jax.ShapeDtypeStruct((Tk, n_kv * dk_), F32),
                   jax.ShapeDtypeStruct((Tk, n_kv * dv), F32)],
        compiler_params=_params(("parallel", "arbitrary")),
    )(*[a for a, _ in qparts], *[a for a, _ in kparts], vpart[0], o, do, lse)


def _gates_fwd(pg, ya, yb, tr):
    n, d = ya.shape

    def body(pg_ref, ya_ref, yb_ref, o_ref):
        ga = jax.nn.sigmoid(pg_ref[:, :d].astype(F32))
        gb = jax.nn.sigmoid(pg_ref[:, d:].astype(F32))
        o_ref[...] = (ga * ya_ref[...].astype(F32) + gb * yb_ref[...].astype(F32)).astype(BF16)

    return pl.pallas_call(
        body,
        name="gates_fwd",
        grid=(n // tr,),
        in_specs=[_rows(tr, 2 * d), _rows(tr, d), _rows(tr, d)],
        out_specs=_rows(tr, d),
        out_shape=jax.ShapeDtypeStruct((n, d), BF16),
        compiler_params=_params(("parallel",)),
    )(pg, ya, yb)


def _gates_bwd(dm, pg, ya, yb, tr):
    n, d = ya.shape

    def body(dm_ref, pg_ref, ya_ref, yb_ref, dya_ref, dyb_ref, dpg_ref):
        dmv = dm_ref[...].astype(F32)
        ga = jax.nn.sigmoid(pg_ref[:, :d].astype(F32))
        gb = jax.nn.sigmoid(pg_ref[:, d:].astype(F32))
        dya_ref[...] = (dmv * ga).astype(BF16)
        dyb_ref[...] = (dmv * gb).astype(BF16)
        dpg_ref[:, :d] = (dmv * ya_ref[...].astype(F32) * ga * (1.0 - ga)).astype(BF16)
        dpg_ref[:, d:] = (dmv * yb_ref[...].astype(F32) * gb * (1.0 - gb)).astype(BF16)

    return pl.pallas_call(
        body,
        name="gates_bwd",
        grid=(n // tr,),
        in_specs=[_rows(tr, d), _rows(tr, 2 * d), _rows(tr, d), _rows(tr, d)],
        out_specs=[_rows(tr, d), _rows(tr, d), _rows(tr, 2 * d)],
        out_shape=[jax.ShapeDtypeStruct((n, d), BF16), jax.ShapeDtypeStruct((n, d), BF16), jax.ShapeDtypeStruct((n, 2 * d), BF16)],
        compiler_params=_params(("parallel",)),
    )(dm, pg, ya, yb)


def _resid_norm2_fwd(x2d, att, g1, n2g, sh2, sc2, tr):
    n, d = x2d.shape

    def body(x_ref, a_ref, g1_ref, g_ref, sh_ref, sc_ref, x1_ref, z_ref):
        x1 = x_ref[...] + g1_ref[...] * a_ref[...]
        x1_ref[...] = x1
        xh, _ = _rms(x1)
        z_ref[...] = ((xh * g_ref[...]) * (1.0 + sc_ref[...]) + sh_ref[...]).astype(BF16)

    return pl.pallas_call(
        body,
        name="resid_norm2_fwd",
        grid=(n // tr,),
        in_specs=[_rows(tr, d), _rows(tr, d)] + [_bcast(d)] * 4,
        out_specs=[_rows(tr, d), _rows(tr, d)],
        out_shape=[jax.ShapeDtypeStruct((n, d), F32), jax.ShapeDtypeStruct((n, d), BF16)],
        compiler_params=_params(("parallel",)),
    )(x2d, att, g1, n2g, sh2, sc2)


def _resid_norm2_bwd(dz2, x1, dx2, att, n2g, sc2, g1, tr):
    n, d = x1.shape

    def body(dz_ref, x1_ref, dx2_ref, a_ref, g_ref, sc_ref, g1_ref, dx1_ref, da_ref, dg_ref, dsh_ref, dsc_ref, dg1_ref):
        _acc_init(pl.program_id(0), [dg_ref, dsh_ref, dsc_ref, dg1_ref])
        xh, r = _rms(x1_ref[...])
        dzv = dz_ref[...]
        gv = g_ref[...]
        dsc_ref[...] += _csum(dzv * (xh * gv))
        dsh_ref[...] += _csum(dzv)
        dh = dzv * (1.0 + sc_ref[...])
        dg_ref[...] += _csum(dh * xh)
        dx1 = _rms_bwd(xh, r, dh * gv) + dx2_ref[...]
        dx1_ref[...] = dx1
        dg1_ref[...] += _csum(dx1 * a_ref[...])
        da_ref[...] = (dx1 * g1_ref[...]).astype(BF16)

    return pl.pallas_call(
        body,
        name="resid_norm2_bwd",
        grid=(n // tr,),
        in_specs=[_rows(tr, d)] * 4 + [_bcast(d)] * 3,
        out_specs=[_rows(tr, d), _rows(tr, d)] + [_bcast(d)] * 4,
        out_shape=[jax.ShapeDtypeStruct((n, d), F32), jax.ShapeDtypeStruct((n, d), BF16)] + [jax.ShapeDtypeStruct((1, d), F32)] * 4,
        compiler_params=_params(("arbitrary",)),
    )(dz2, x1, dx2, att, n2g, sc2, g1)


def _edges(shape):
    row = lax.broadcasted_iota(jnp.int32, shape, 0)
    return row == 0, row == shape[0] - 1


def _shifts(u, edges):
    n = u.shape[0]
    return jnp.where(edges[0], 0.0, pltpu.roll(u, 1, 0)), jnp.where(edges[1], 0.0, pltpu.roll(u, n - 1, 0))


def _conv3(u, prev, nxt, w_ref, b_ref):
    return b_ref[...] + w_ref[0:1, :] * prev + w_ref[1:2, :] * u + w_ref[2:3, :] * nxt


def _ffn_up_conv(z, wup_t, cw, cb, tc, after):
    n, d = z.shape
    f = wup_t.shape[0] // 2
    nb = f // tc

    def body(z_ref, wa_ref, wb_ref, cwa, cwb, cba, cbb, after_ref, ua_ref, ub_ref, h_ref):
        w = jnp.concatenate([wa_ref[...], wb_ref[...]], axis=0)
        u = lax.dot_general(z_ref[...], w, _DIMS["NT"], preferred_element_type=F32).astype(BF16)
        ua_ref[...] = u[:, :tc]
        ub_ref[...] = u[:, tc:]
        edges = _edges((n, tc))
        ua = u[:, :tc].astype(F32)
        ub = u[:, tc:].astype(F32)
        a = _conv3(ua, *_shifts(ua, edges), cwa, cba)
        b = _conv3(ub, *_shifts(ub, edges), cwb, cbb)
        h_ref[...] = (a * jax.nn.sigmoid(a) * b).astype(BF16)

    col = lambda rows, off: pl.BlockSpec((rows, tc), lambda i: (0, i + off))
    w_rows = lambda off: pl.BlockSpec((tc, d), lambda i: (i + off, 0))
    return pl.pallas_call(
        body,
        name="ffn_up_conv",
        grid=(nb,),
        in_specs=[pl.BlockSpec((n, d), lambda i: (0, 0)), w_rows(0), w_rows(nb), col(3, 0), col(3, nb), col(1, 0), col(1, nb),
                  pl.BlockSpec(after.shape, lambda i: (0, 0))],
        out_specs=[col(n, 0)] * 3,
        out_shape=[jax.ShapeDtypeStruct((n, f), BF16)] * 3,
        compiler_params=_params(("parallel",)),
    )(z, wup_t, wup_t, cw, cw, cb, cb, after)


def _conv_bwd(u_a, u_b, dh, cw, cb, tc):
    n, f = u_a.shape
    nb = f // tc

    def part(uv, prev, nxt, duc, edges, w_ref, du_ref, dw_ref, db_ref):
        db_ref[...] = _csum(duc)
        dw_ref[0:1, :] = _csum(duc * prev)
        dw_ref[1:2, :] = _csum(duc * uv)
        dw_ref[2:3, :] = _csum(duc * nxt)
        d_prev, d_next = _shifts(duc, edges)
        du_ref[...] = (w_ref[0:1, :] * d_next + w_ref[1:2, :] * duc + w_ref[2:3, :] * d_prev).astype(BF16)

    def body(ua_ref, ub_ref, dh_ref, wa_ref, wb_ref, ba_ref, bb_ref, dua_ref, dub_ref, dwa_ref, dwb_ref, dba_ref, dbb_ref):
        edges = _edges((n, tc))
        ua = ua_ref[...].astype(F32)
        ub = ub_ref[...].astype(F32)
        sa = _shifts(ua, edges)
        sb = _shifts(ub, edges)
        a = _conv3(ua, *sa, wa_ref, ba_ref)
        b = _conv3(ub, *sb, wb_ref, bb_ref)
        dhv = dh_ref[...].astype(F32)
        sg = jax.nn.sigmoid(a)
        da = dhv * b * (sg * (1.0 + a * (1.0 - sg)))
        db = dhv * (a * sg)
        part(ua, *sa, da, edges, wa_ref, dua_ref, dwa_ref, dba_ref)
        part(ub, *sb, db, edges, wb_ref, dub_ref, dwb_ref, dbb_ref)

    col = lambda rows, off: pl.BlockSpec((rows, tc), lambda i: (0, i + off))
    return pl.pallas_call(
        body,
        name="conv_bwd",
        grid=(nb,),
        in_specs=[col(n, 0), col(n, 0), col(n, 0), col(3, 0), col(3, nb), col(1, 0), col(1, nb)],
        out_specs=[col(n, 0), col(n, 0), col(3, 0), col(3, 0), col(1, 0), col(1, 0)],
        out_shape=[jax.ShapeDtypeStruct((n, f), BF16)] * 2 + [jax.ShapeDtypeStruct((3, f), F32)] * 2 + [jax.ShapeDtypeStruct((1, f), F32)] * 2,
        compiler_params=_params(("parallel",)),
    )(u_a, u_b, dh, cw, cw, cb, cb)


def _loss_head(x1, f, g2, fg, tgt, tr):
    n, d = x1.shape

    def body(x1_ref, f_ref, g2_ref, fg_ref, t_ref, sq_ref, dx2_ref, dfg_ref, dg2_ref, df_ref):
        _acc_init(pl.program_id(0), [sq_ref, dfg_ref, dg2_ref])
        fv = f_ref[...]
        xh, r = _rms(x1_ref[...] + g2_ref[...] * fv)
        err = xh * fg_ref[...] - t_ref[...]
        sq_ref[...] += _csum(err * err)
        dy = err * (1.0 / d)
        dfg_ref[...] += _csum(dy * xh)
        dx2 = _rms_bwd(xh, r, dy * fg_ref[...])
        dx2_ref[...] = dx2
        dg2_ref[...] += _csum(dx2 * fv)
        df_ref[...] = (dx2 * g2_ref[...]).astype(BF16)

    return pl.pallas_call(
        body,
        name="loss_head",
        grid=(n // tr,),
        in_specs=[_rows(tr, d), _rows(tr, d), _bcast(d), _bcast(d), _rows(tr, d)],
        out_specs=[_bcast(d), _rows(tr, d), _bcast(d), _bcast(d), _rows(tr, d)],
        out_shape=[jax.ShapeDtypeStruct((1, d), F32), jax.ShapeDtypeStruct((n, d), F32), jax.ShapeDtypeStruct((1, d), F32),
                   jax.ShapeDtypeStruct((1, d), F32), jax.ShapeDtypeStruct((n, d), BF16)],
        compiler_params=_params(("arbitrary",)),
    )(x1, f, g2, fg, tgt)


def _sum_slots(g, name):
    s, r, w = g.shape

    def body(g_ref, o_ref):
        acc = g_ref[0]
        for k in range(1, s):
            acc = acc + g_ref[k]
        o_ref[...] = acc

    return pl.pallas_call(body, name=name, out_shape=jax.ShapeDtypeStruct((r, w), F32))(g)


def _silu_grad_mul(ds, cvec):
    def body(d_ref, c_ref, o_ref):
        cv = c_ref[...]
        sg = jax.nn.sigmoid(cv)
        o_ref[...] = d_ref[...] * (sg * (1.0 + cv * (1.0 - sg)))

    return pl.pallas_call(body, name="silu_grad_mul", out_shape=jax.ShapeDtypeStruct(ds.shape, F32))(ds, cvec)


def _adamw_update(wv, gv, mv, vv, d_ref, mo_ref, vo_ref):
    mn = ADAM_B1 * mv + (1.0 - ADAM_B1) * gv
    vn = ADAM_B2 * vv + (1.0 - ADAM_B2) * (gv * gv)
    mo_ref[...] = mn
    vo_ref[...] = vn
    m_hat = mn / (1.0 - ADAM_B1**ADAM_STEP)
    v_hat = vn / (1.0 - ADAM_B2**ADAM_STEP)
    d_ref[...] = -ADAM_LR * (m_hat / (jnp.sqrt(v_hat) + ADAM_EPS) + ADAM_WD * wv)


def _adamw_many(ws, gs, ms, vs, name):
    n = len(ws)

    def body(*refs):
        for k in range(n):
            w_ref, g_ref, m_ref, v_ref = (refs[q * n + k] for q in range(4))
            d_ref, mo_ref, vo_ref = (refs[(4 + q) * n + k] for q in range(3))
            _adamw_update(w_ref[...], g_ref[...], m_ref[...], v_ref[...], d_ref, mo_ref, vo_ref)

    res = pl.pallas_call(body, name=name, out_shape=[jax.ShapeDtypeStruct(w.shape, F32) for w in ws] * 3)(*ws, *gs, *ms, *vs)
    return res[:n], res[n : 2 * n], res[2 * n :]


def _adamw(w, g, m, v, name, g_transposed=False, g_sibling=None):
    r, cdim = w.shape
    halves = g_sibling is not None
    block = 1 << 19
    if g_transposed:
        tc = _pick(cdim // 2 if halves else cdim, 2048)
        tr = _pick(r, max(LANES, block // tc), LANES)
        per_half = (cdim // 2) // tc
    else:
        rows = r // 2 if halves else r
        tc = _pick(cdim, 2048)
        tr = _pick(rows, max(8, block // tc), 8)
        if tr < 64 and rows > 64:
            tr, tc = _pick(rows, 1024, 8), _pick(cdim, 512)
        per_half = (r // 2) // tr
    emit_g = g_transposed or halves

    def body(w_ref, g_ref, *rest):
        m_ref, v_ref = rest[halves : halves + 2]
        outs = rest[halves + 2 :]
        gv = g_ref[...]
        if halves:
            along = pl.program_id(1 if g_transposed else 0)
            gv = jnp.where(along // per_half == lax.axis_index("c"), gv, rest[0][...])
        if g_transposed:
            gv = gv.T
        if emit_g:
            outs[0][...] = gv
        _adamw_update(w_ref[...], gv, m_ref[...], v_ref[...], *outs[-3:])

    spec = pl.BlockSpec((tr, tc), lambda i, j: (i, j))
    if g_transposed:
        g_spec = pl.BlockSpec((tc, tr), lambda i, j: (j % per_half if halves else j, i))
    else:
        g_spec = pl.BlockSpec((tr, tc), lambda i, j: (i % per_half if halves else i, j))
    n_out = 3 + emit_g
    res = pl.pallas_call(
        body,
        name=name,
        grid=(r // tr, cdim // tc),
        in_specs=[spec, g_spec] + [g_spec] * halves + [spec, spec],
        out_specs=[spec] * n_out,
        out_shape=[jax.ShapeDtypeStruct((r, cdim), F32)] * n_out,
        compiler_params=_params(("parallel", "parallel")),
    )(w, g, *([g_sibling] if halves else []), m, v)
    return res if emit_g else [g, *res]


def _place():
    return lax.axis_index("x"), lax.axis_index("y"), lax.axis_index("c")


def _remote(src, dst, send_sem, recv_sem, dev):
    return pltpu.make_async_remote_copy(src_ref=src, dst_ref=dst, send_sem=send_sem, recv_sem=recv_sem, device_id=dev, device_id_type=MESH)


ANY = pl.BlockSpec(memory_space=pl.ANY)


def _all_gather_small(v, name):
    r, w = v.shape

    def body(v_ref, o_ref, send, recv, lsem):
        x, y, c = _place()
        me = 4 * x + 2 * y + c
        mine = pltpu.make_async_copy(v_ref, o_ref.at[me], lsem)
        mine.start()
        sent = []
        for k in range(1, 8):
            px, py, pc = x ^ (k >> 2), y ^ ((k >> 1) & 1), c ^ (k & 1)
            cp = _remote(v_ref, o_ref.at[me], send.at[k - 1], recv.at[k - 1], (px, py, pc))
            cp.start()
            sent.append(cp)
        for k in range(1, 8):
            px, py, pc = x ^ (k >> 2), y ^ ((k >> 1) & 1), c ^ (k & 1)
            slot = o_ref.at[4 * px + 2 * py + pc]
            _remote(slot, slot, send.at[k - 1], recv.at[k - 1], (x, y, c)).wait_recv()
        for cp in sent:
            cp.wait_send()
        mine.wait()

    return pl.pallas_call(
        body,
        name=name,
        out_shape=jax.ShapeDtypeStruct((8, r, w), F32),
        in_specs=[pl.BlockSpec(memory_space=pltpu.VMEM)],
        out_specs=pl.BlockSpec(memory_space=pltpu.VMEM),
        scratch_shapes=[pltpu.SemaphoreType.DMA((7,)), pltpu.SemaphoreType.DMA((7,)), pltpu.SemaphoreType.DMA],
        compiler_params=pltpu.CompilerParams(vmem_limit_bytes=VMEM_LIMIT),
    )(v)


HBM = pl.BlockSpec(memory_space=pltpu.HBM)
SEM = pl.BlockSpec(memory_space=pltpu.SEMAPHORE)
EFFECT = pltpu.SideEffectType.DATAFLOW_SIDE_EFFECTING


def _other_chips(x, y):
    return [(1 - x, y), (x, 1 - y), (1 - x, 1 - y)]


def _bulk_start(name, srcs, land_shapes, n_copies, copies, after):
    n, m = len(srcs), len(land_shapes)

    def body(*refs):
        src_refs, land_refs = refs[:n], refs[n : n + m]
        send, recv = refs[n + m + 1], refs[n + m + 2]
        token = refs[-1]
        for k, (s, d, dev) in enumerate(copies(src_refs, land_refs)):
            _remote(s, d, send.at[k], recv.at[k], dev).start()
        token[...] = jnp.zeros_like(token)

    lands = [pltpu.with_memory_space_constraint(lax.empty(s.shape, s.dtype), pltpu.HBM) for s in land_shapes]
    out = pl.pallas_call(
        body,
        name=name,
        out_shape=(pltpu.SemaphoreType.DMA((n_copies,)), pltpu.SemaphoreType.DMA((n_copies,)),
                   *[pltpu.HBM(s.shape, s.dtype) for s in srcs], *[pltpu.HBM(s.shape, s.dtype) for s in land_shapes],
                   jax.ShapeDtypeStruct((8, LANES), F32)),
        in_specs=[HBM] * (n + m) + [ANY],
        out_specs=(SEM, SEM, *[HBM] * (n + m), pl.BlockSpec(memory_space=pltpu.VMEM)),
        input_output_aliases={i: 2 + i for i in range(n + m)},
        compiler_params=pltpu.CompilerParams(has_side_effects=EFFECT),
    )(*[pltpu.with_memory_space_constraint(s, pltpu.HBM) for s in srcs], *lands, after)
    return out[0], out[1], list(out[2 : 2 + n]), list(out[2 + n : 2 + n + m]), out[-1][0:1, 0:1]


def _bulk_wait(name, send, recv, srcs, lands, after, waits):
    n, m = len(srcs), len(lands)

    def body(*refs):
        src_refs, land_refs = refs[:n], refs[n : n + m]
        send_sem, recv_sem = refs[n + m], refs[n + m + 1]
        x, y, c = _place()
        for k, (s, d) in enumerate(waits(src_refs, land_refs)):
            cp = _remote(s, d, send_sem.at[k], recv_sem.at[k], (x, y, c))
            cp.wait_send()
            cp.wait_recv()

    out = pl.pallas_call(
        body,
        name=name,
        out_shape=tuple(pltpu.HBM(s.shape, s.dtype) for s in (*srcs, *lands)),
        in_specs=[HBM] * (n + m) + [SEM, SEM, ANY],
        out_specs=tuple([HBM] * (n + m)),
        input_output_aliases={i: i for i in range(n + m)},
        compiler_params=pltpu.CompilerParams(has_side_effects=EFFECT),
    )(*srcs, *lands, send, recv, after)
    return list(out[:n]), list(out[n:])


def _gather_start(shards, after, name):
    def copies(src, land):
        x, y, c = _place()
        j = 2 * x + y
        return [(src[a].at[c], land[a].at[j, c], (px, py, c)) for a in range(len(shards)) for px, py in _other_chips(x, y)]

    shapes = [jax.ShapeDtypeStruct((4,) + s.shape, s.dtype) for s in shards]
    return _bulk_start(name, shards, shapes, 3 * len(shards), copies, after)


def _gather_wait(started, after, name):
    send, recv, srcs, lands, _ = started

    def waits(src, land):
        x, y, c = _place()
        return [(src[a].at[c], land[a].at[2 * px + py, c]) for a in range(len(srcs)) for px, py in _other_chips(x, y)]

    return _bulk_wait(name, send, recv, srcs, lands, after, waits)


def _forward_halves(lands, name):
    n = len(lands)

    def body(*refs):
        bufs = refs[n : 2 * n]
        send, recv = refs[2 * n :]
        x, y, c = _place()
        started = []
        for a in range(n):
            for k, (px, py) in enumerate(_other_chips(x, y)):
                blk = bufs[a].at[2 * px + py, c]
                cp = _remote(blk, blk, send.at[3 * a + k], recv.at[3 * a + k], (x, y, 1 - c))
                cp.start()
                started.append(cp)
        for a in range(n):
            for k, (px, py) in enumerate(_other_chips(x, y)):
                blk = bufs[a].at[2 * px + py, 1 - c]
                _remote(blk, blk, send.at[3 * a + k], recv.at[3 * a + k], (x, y, c)).wait_recv()
        for cp in started:
            cp.wait_send()

    return pl.pallas_call(
        body,
        name=name,
        out_shape=[jax.ShapeDtypeStruct(b.shape, b.dtype) for b in lands],
        in_specs=[ANY] * n,
        out_specs=[ANY] * n,
        input_output_aliases={i: i for i in range(n)},
        scratch_shapes=[pltpu.SemaphoreType.DMA((3 * n,)), pltpu.SemaphoreType.DMA((3 * n,))],
    )(*lands)


def _forward_start(lands, after, name):
    def copies(src, _):
        x, y, c = _place()
        blocks = [src[a].at[2 * px + py, c] for a in range(len(lands)) for px, py in _other_chips(x, y)]
        return [(b, b, (x, y, 1 - c)) for b in blocks]

    return _bulk_start(name, lands, [], 3 * len(lands), copies, after)


def _forward_wait(started, after, name):
    send, recv, bufs, _, _ = started

    def waits(src, _):
        x, y, c = _place()
        return [(src[a].at[2 * px + py, c], src[a].at[2 * px + py, 1 - c]) for a in range(len(bufs)) for px, py in _other_chips(x, y)]

    return _bulk_wait(name, send, recv, bufs, [], after, waits)[0]


def _place_own(shards, lands):
    j = 2 * lax.axis_index("x") + lax.axis_index("y")
    full = [lax.dynamic_update_slice(b, s[None], (j, 0, 0, 0)) for b, s in zip(lands, shards)]
    return [f.reshape(4 * f.shape[2] * 2, f.shape[3]) for f in full]


def _gather_finish(started, after, tag):
    shards, lands = _gather_wait(started, after, "gather_wait_" + tag)
    return _place_own(shards, _forward_halves(lands, "gather_forward_" + tag))


def _gather_land(started, after, tag):
    shards, lands = _gather_wait(started, after, "gather_wait_" + tag)
    return shards, _forward_start(lands, shards[0], "forward_start_" + tag)


def _gather_done(landed, after, tag):
    shards, fwd = landed
    return _place_own(shards, _forward_wait(fwd, after, "forward_wait_" + tag))


def _swap_halves(grads, name):
    n = len(grads)

    def body(*refs):
        ins, outs = refs[:n], refs[n : 2 * n]
        send, recv = refs[2 * n :]
        x, y, c = _place()
        started = []
        for a in range(n):
            for s in range(4):
                cp = _remote(ins[a].at[s, 1 - c], outs[a].at[s], send.at[4 * a + s], recv.at[4 * a + s], (x, y, 1 - c))
                cp.start()
                started.append(cp)
        for cp in started:
            cp.wait_recv()
        for cp in started:
            cp.wait_send()

    return pl.pallas_call(
        body,
        name=name,
        out_shape=[jax.ShapeDtypeStruct((4,) + g.shape[2:], g.dtype) for g in grads],
        in_specs=[ANY] * n,
        out_specs=[ANY] * n,
        scratch_shapes=[pltpu.SemaphoreType.DMA((4 * n,)), pltpu.SemaphoreType.DMA((4 * n,))],
    )(*grads)


def _add_halves(grads, others, tag):
    outs = []
    for a, (g, o) in enumerate(zip(grads, others)):
        _, _, rh, cdim = g.shape
        tr = _pick(rh, 512, 16)

        def body(g_ref, o_ref, p_ref):
            p_ref[...] = (g_ref[...].astype(F32) + o_ref[...].astype(F32)).astype(BF16)

        outs.append(
            pl.pallas_call(
                body,
                name=f"add_halves_{tag}{a}",
                grid=(4, rh // tr),
                in_specs=[pl.BlockSpec((None, None, tr, cdim), lambda s, i: (s, lax.axis_index("c"), i, 0)),
                          pl.BlockSpec((None, tr, cdim), lambda s, i: (s, i, 0))],
                out_specs=pl.BlockSpec((None, tr, cdim), lambda s, i: (s, i, 0)),
                out_shape=jax.ShapeDtypeStruct((4, rh, cdim), BF16),
                compiler_params=_params(("parallel", "parallel")),
            )(g, o)
        )
    return outs


def _exchange_start(parts, after, name):
    def copies(src, land):
        x, y, c = _place()
        j = 2 * x + y
        return [(src[a].at[2 * px + py], land[a].at[j], (px, py, c)) for a in range(len(parts)) for px, py in _other_chips(x, y)]

    return _bulk_start(name, parts, [jax.ShapeDtypeStruct(p.shape, p.dtype) for p in parts], 3 * len(parts), copies, after)


def _exchange_finish(started, after, name):
    send, recv, srcs, lands, _ = started

    def waits(src, land):
        x, y, _ = _place()
        return [(src[a].at[2 * px + py], land[a].at[2 * px + py]) for a in range(len(srcs)) for px, py in _other_chips(x, y)]

    srcs, lands = _bulk_wait(name, send, recv, srcs, lands, after, waits)
    j = 2 * lax.axis_index("x") + lax.axis_index("y")
    return [lax.dynamic_update_slice(b, lax.dynamic_slice(p, (j, 0, 0), (1,) + p.shape[1:]), (j, 0, 0)) for b, p in zip(lands, srcs)]


def _sum_chips(recvd, tag):
    outs = []
    for a, g in enumerate(recvd):
        _, rh, cdim = g.shape
        tr = _pick(rh, 512, 16)

        def body(g_ref, o_ref):
            o_ref[...] = ((g_ref[0].astype(F32) + g_ref[1].astype(F32)) + g_ref[2].astype(F32)) + g_ref[3].astype(F32)

        outs.append(
            pl.pallas_call(
                body,
                name=f"sum_chips_{tag}{a}",
                grid=(rh // tr,),
                in_specs=[pl.BlockSpec((4, tr, cdim), lambda i: (0, i, 0))],
                out_specs=pl.BlockSpec((tr, cdim), lambda i: (i, 0)),
                out_shape=jax.ShapeDtypeStruct((rh, cdim), F32),
                compiler_params=_params(("parallel",)),
            )(g)
        )
    return outs


def _join_halves(halves, name):
    n = len(halves)

    def body(*refs):
        ins, outs = refs[:n], refs[n : 2 * n]
        send, recv = refs[2 * n :]
        x, y, c = _place()
        started = []
        for a in range(n):
            cp = _remote(ins[a], outs[a], send.at[a], recv.at[a], (x, y, 1 - c))
            cp.start()
            started.append(cp)
        for cp in started:
            cp.wait_recv()
        for cp in started:
            cp.wait_send()

    others = pl.pallas_call(
        body,
        name=name,
        out_shape=[jax.ShapeDtypeStruct(h.shape, h.dtype) for h in halves],
        in_specs=[ANY] * n,
        out_specs=[ANY] * n,
        scratch_shapes=[pltpu.SemaphoreType.DMA((n,)), pltpu.SemaphoreType.DMA((n,))],
    )(*halves)
    return list(zip(halves, others))


def _joined(mine, other):
    first = lax.axis_index("c") == 0
    return jnp.concatenate([jnp.where(first, mine, other), jnp.where(first, other, mine)], axis=0)


def _grad_views(grads):
    return [g.reshape(4, 2, g.shape[0] // 8, g.shape[1]) for g in grads]


def _scatter_start(grads, tag, after=None):
    views = _grad_views(grads)
    mine = _add_halves(views, _swap_halves(views, "swap_halves_" + tag), tag)
    return _exchange_start(mine, mine[-1] if after is None else after, "exchange_start_" + tag)


def _swap_start(grads, after, tag):
    views = _grad_views(grads)

    def copies(src, land):
        x, y, c = _place()
        return [(src[a].at[s, 1 - c], land[a].at[s], (x, y, 1 - c)) for a in range(len(views)) for s in range(4)]

    shapes = [jax.ShapeDtypeStruct((4,) + v.shape[2:], v.dtype) for v in views]
    return _bulk_start("swap_start_" + tag, views, shapes, 4 * len(views), copies, after)


def _scatter_start_after_swap(swapped, after, tag):
    send, recv, views, lands, _ = swapped

    def waits(src, land):
        c = lax.axis_index("c")
        return [(src[a].at[s, 1 - c], land[a].at[s]) for a in range(len(views)) for s in range(4)]

    views, others = _bulk_wait("swap_wait_" + tag, send, recv, views, lands, after, waits)
    mine = _add_halves(views, others, tag)
    return _exchange_start(mine, mine[-1], "exchange_start_" + tag)


def _join_start(halves, after, tag):
    def copies(src, land):
        x, y, c = _place()
        return [(src[a], land[a], (x, y, 1 - c)) for a in range(len(halves))]

    return _bulk_start("join_start_" + tag, halves, [jax.ShapeDtypeStruct(h.shape, h.dtype) for h in halves], len(halves), copies, after)


def _join_wait(started, after, tag):
    send, recv, halves, lands, _ = started
    halves, others = _bulk_wait("join_wait_" + tag, send, recv, halves, lands, after, lambda src, land: list(zip(src, land)))
    return list(zip(halves, others))


def _scatter_sums(started, after, tag):
    return _sum_chips(_exchange_finish(started, after, "exchange_wait_" + tag), tag)


def _scatter_finish(started, after, tag):
    return _join_halves(_scatter_sums(started, after, tag), "join_halves_" + tag)


def _t_bf16(w):
    return w.T.astype(BF16)


def kernel(x, c, ctx, c_ctx, w_ada, b_ada, norm1_g, w_in, mla_q_norm_g, w_q_up, mla_kv_norm_g, w_kv_up, gqa_q_norm_g, gqa_k_norm_g, w_br_a, w_br_b, w_out, norm2_g, w_up, conv_w, conv_b, w_down, final_norm_g, loss_target, m_c_ctx, m_w_ada, m_b_ada, m_norm1_g, m_w_in, m_mla_q_norm_g, m_w_q_up, m_mla_kv_norm_g, m_w_kv_up, m_gqa_q_norm_g, m_gqa_k_norm_g, m_w_br_a, m_w_br_b, m_w_out, m_norm2_g, m_w_up, m_conv_w, m_conv_b, m_w_down, m_final_norm_g, v_c_ctx, v_w_ada, v_b_ada, v_norm1_g, v_w_in, v_mla_q_norm_g, v_w_q_up, v_mla_kv_norm_g, v_w_kv_up, v_gqa_q_norm_g, v_gqa_k_norm_g, v_w_br_a, v_w_br_b, v_w_out, v_norm2_g, v_w_up, v_conv_w, v_conv_b, v_w_down, v_final_norm_g):
    T, D = x.shape[1], x.shape[2]
    C = ctx.shape[1]
    NA = w_ada.shape[2]
    NW = w_up.shape[2]
    F2 = 4 * NW
    FF = F2 // 2
    xi, yi, ci = _place()
    j = 2 * xi + yi
    me = 4 * xi + 2 * yi + ci
    tr = _pick(C, 128, 8)
    tq = _pick(T, 256)

    x2d, tgt, ctx2d = x[0], loss_target[0], ctx[0]
    fg = final_norm_g.reshape(1, D)
    cc = c_ctx.reshape(1, D)

    halve = lambda s: s.reshape(2, s.shape[0] // 2, s.shape[1])
    win_shard = halve(_t_bf16(w_in[0]))
    w0 = max(D, NW)
    pay = jnp.zeros((8, w0), F32).at[0:1, :D].set(c).at[1:4, :NW].set(conv_w[0])
    got = _all_gather_small(pay, "gather_cond")
    c_all = got[:, 0, :D]
    cw = jnp.concatenate([got[2 * s, 1:4, :NW] for s in range(4)], axis=1)
    s16 = jnp.concatenate([c_all, cc, jnp.zeros((7, D), F32)], axis=0)
    b_cols = lax.dynamic_slice(b_ada, (0, j * NA), (1, NA))
    ada_part = _mm(s16, w_ada[0], "NN", F32, "ada_fwd", act="silu", bias=b_cols)
    got = _all_gather_small(ada_part, "gather_ada")
    ada = jnp.concatenate([got[2 * s] for s in range(4)], axis=1)
    lat = lax.dynamic_slice(ada, (me, 0), (1, 6 * D))
    sh1, sc1, g1, sh2, sc2, g2 = [lat[:, k * D : (k + 1) * D] for k in range(6)]
    csh, csc = ada[8:9, :D], ada[8:9, D : 2 * D]

    ag_in = _gather_start([win_shard], got, "gather_start_in")
    t_in = ag_in[4]
    wq3 = (w_q_up[0] + t_in).reshape(MLA_Q_LORA, 2, MLA_NOPE + MLA_ROPE)
    wq_perm = jnp.concatenate([wq3[:, :, :MLA_NOPE].reshape(MLA_Q_LORA, -1), wq3[:, :, MLA_NOPE:].reshape(MLA_Q_LORA, -1)], axis=1)
    low = [_t_bf16(wq_perm), _t_bf16(w_kv_up[0] + t_in)]
    br = [_t_bf16(w_br_a[0] + t_in), _t_bf16(w_br_b[0] + t_in), (w_out[0] + t_in).astype(BF16)]
    ag_low = _gather_start([halve(s) for s in low], t_in, "gather_start_low")
    ag_br = _gather_start([halve(s) for s in br], ag_low[4], "gather_start_br")
    ag_up = _gather_start([halve(_t_bf16(w_up[0] + t_in))], ag_br[4], "gather_start_up")
    ag_down = _gather_start([halve((w_down[0] + t_in).astype(BF16))], ag_up[4], "gather_start_down")
    sh1 = sh1 + ag_down[4]

    cos_a, ss_a = _rope_tables(C, T, MLA_ROPE)
    cos_b, ss_b = _rope_tables(C, T, GQA_HEAD_DIM)
    lcos_a, lss_a, lcos_b, lss_b = cos_a[:T], ss_a[:T], cos_b[:T], ss_b[:T]

    z_all = _norm_mod_fwd(x2d, norm1_g, sh1, sc1, "norm1_lat_fwd", tr, out_rows=T + C)
    z_all = _norm_mod_fwd(ctx2d, norm1_g, csh, csc, "norm1_ctx_fwd", tr, base=z_all, out_off=T)
    (win_t,) = _gather_finish(ag_in, z_all, "in")
    kv_cols = KVP - LANES + MLA_ROPE
    e_kpe = MLA_KV_LORA + MLA_ROPE
    w_kvp = jnp.concatenate([win_t[:MLA_KV_LORA], win_t[e_kpe:kv_cols], win_t[MLA_KV_LORA:e_kpe], jnp.zeros((LANES - MLA_ROPE, D), BF16)], axis=0)

    pkv = _mm(z_all, w_kvp, "NT", F32, "proj_kv", tn=KVP)
    pq = _mm(z_all, win_t, "NT", F32, "proj_q", m=T, n=QC, b_off=kv_cols)
    low_landed = _gather_land(ag_low, pq, "low")
    pg = _mm(z_all, win_t, "NT", BF16, "proj_g", m=T, n=2 * D, b_off=kv_cols + QC, after=low_landed[1][4])
    wq_t, wkv_t = _gather_done(low_landed, pg, "low")
    ckv_n, kb2, vb2, kpe2 = _kprep_fwd(pkv, mla_kv_norm_g, gqa_k_norm_g, cos_a, ss_a, cos_b, ss_b, tr)
    kv_up = _mm(ckv_n, wkv_t, "NT", BF16, "kv_up")
    cq_n, qb2 = _qprep_fwd(pq, mla_q_norm_g, gqa_q_norm_g, lcos_b, lss_b, tr)
    q_a = _mm(cq_n, wq_t, "NT", F32, "q_up")
    qar = _qrope_fwd(q_a, lcos_a, lss_a, tr)

    a_q = [(qar, lambda h: 3 * (h // 2) + h % 2), (qar, lambda h: 3 * (h // 2) + 2)]
    a_k = [(kv_up, lambda h: 2 * h), (kpe2, lambda h: h % 2)]
    a_v = (kv_up, lambda h: 2 * h + 1)
    a_scale = float(MLA_NOPE + MLA_ROPE) ** -0.5
    b_q = [(qb2, lambda h: h)]
    b_k = [(kb2, lambda h: h)]
    b_v = (vb2, lambda h: h)
    b_scale = float(GQA_HEAD_DIM) ** -0.5
    tq_f = _pick(T, 512)
    o_a, lse_a = _attn_fwd(a_q, a_k, a_v, MLA_HEADS, 1, MLA_V, a_scale, "attn_a_fwd", tq_f)
    br_landed = _gather_land(ag_br, o_a, "br")
    o_b, lse_b = _attn_fwd(b_q, b_k, b_v, GQA_HEADS, GQA_GROUP, GQA_HEAD_DIM, b_scale, "attn_b_fwd", tq_f, after=br_landed[1][4])
    wbra_t, wbrb_t, wout = _gather_done(br_landed, o_b, "br")
    up_landed = _gather_land(ag_up, o_b, "up")
    ya = _mm(o_a, wbra_t, "NT", BF16, "br_a", after=up_landed[1][4])
    yb = _mm(o_b, wbrb_t, "NT", BF16, "br_b")
    merged = _gates_fwd(pg, ya, yb, tr)
    att = _mm(merged, wout, "NN", F32, "out_proj")
    x1, z2 = _resid_norm2_fwd(x2d, att, g1, norm2_g, sh2, sc2, tr)
    (wup_t,) = _gather_done(up_landed, z2, "up")
    down_landed = _gather_land(ag_down, z2, "down")
    tc = _pick(FF, 128)
    u_a, u_b, hg = _ffn_up_conv(z2, wup_t, cw, conv_b, tc, down_landed[1][4])
    (wdown,) = _gather_done(down_landed, hg, "down")
    f = _mm(hg, wdown, "NN", F32, "ffn_down", tk=FF // 2)
    sq, dx2, d_fg, d_g2, df = _loss_head(x1, f, g2, fg, tgt, tr)
    loss = lax.psum(0.5 * jnp.sum(sq) / D, ("x", "y", "c"))

    dhg = _mm(df, wdown, "NT", BF16, "ffn_down_dx", after=loss.reshape(1, 1))
    g_wdown = _mm(hg, df, "TN", BF16, "ffn_down_dw", tm=FF // 4)
    du_a, du_b, dcw_a, dcw_b, dcb_a, dcb_b = _conv_bwd(u_a, u_b, dhg, cw, conv_b, tc)
    dz2 = _mm(du_a, wup_t, "NN", F32, "ffn_up_dx_a", tk=FF // 2)
    dz2 = _mm(du_b, wup_t, "NN", F32, "ffn_up_dx_b", b_off=FF, add=dz2, tk=FF // 2)
    g_wup_t = _mm(du_a, z2, "TN", BF16, "ffn_up_dw_a", out_rows=F2, tm=FF // 4)
    g_wup_t = _mm(du_b, z2, "TN", BF16, "ffn_up_dw_b", out_base=g_wup_t, out_off=FF, tm=FF // 4)
    sw_ffn = _swap_start([g_wdown, g_wup_t], sc2, "ffn")
    sc2 = sc2 + sw_ffn[4]
    dx1, datt, d_n2g, d_sh2, d_sc2, d_g1 = _resid_norm2_bwd(dz2, x1, dx2, att, norm2_g, sc2, g1, tr)

    dmerged = _mm(datt, wout, "NT", BF16, "out_proj_dx")
    rs_ffn = _scatter_start_after_swap(sw_ffn, dmerged, "ffn")
    lse_a = lse_a + rs_ffn[4]
    g_wout = _mm(merged, datt, "TN", BF16, "out_proj_dw")
    dya, dyb, dpg = _gates_bwd(dmerged, pg, ya, yb, tr)
    do_a = _mm(dya, wbra_t, "NN", BF16, "br_a_dx")
    g_wbra_t = _mm(dya, o_a, "TN", BF16, "br_a_dw")
    do_b = _mm(dyb, wbrb_t, "NN", BF16, "br_b_dx")
    g_wbrb_t = _mm(dyb, o_b, "TN", BF16, "br_b_dw")
    dqa2, dka2, dva2 = _attn_bwd(a_q, a_k, a_v, o_a, do_a, lse_a, MLA_HEADS, 1, MLA_V, a_scale, "attn_a_bwd", tq_f)
    dqb2, dkb2, dvb2 = _attn_bwd(b_q, b_k, b_v, o_b, do_b, lse_b, GQA_HEADS, GQA_GROUP, GQA_HEAD_DIM, b_scale, "attn_b_bwd", tq_f)
    dq_a = _qrope_bwd(dqa2, lcos_a, lss_a, tr)
    dcq_n = _mm(dq_a, wq_t, "NN", F32, "q_up_dx")
    g_wq_t = _mm(dq_a, cq_n, "TN", BF16, "q_up_dw")
    dpq, d_qg, d_gq = _qprep_bwd(pq, dcq_n, dqb2, mla_q_norm_g, gqa_q_norm_g, lcos_b, lss_b, tr)
    dkv_up, dkpe = _kgrad_split(dka2, dva2, cos_a, ss_a, tr)
    dckv_n = _mm(dkv_up, wkv_t, "NN", F32, "kv_up_dx")
    g_wkv_t = _mm(dkv_up, ckv_n, "TN", BF16, "kv_up_dw")
    rs_mix = _scatter_start([g_wq_t, g_wkv_t, g_wbra_t, g_wbrb_t, g_wout], "mix")
    dpkv, d_kvg, d_kg = _kprep_bwd(pkv, dckv_n, dkb2, dvb2, dkpe, mla_kv_norm_g + rs_mix[4], gqa_k_norm_g, cos_b, ss_b, tr)
    dz_kv = _mm(dpkv, w_kvp, "NN", F32, "proj_kv_dx")
    dz_lat = _mm(dpq, win_t, "NN", F32, "proj_q_dx", b_off=kv_cols, add=dz_kv)
    dz_lat = _mm(dpg, win_t, "NN", F32, "proj_g_dx", b_off=kv_cols + QC, add=dz_lat)
    _, d_n1g_c, d_csh, d_csc = _norm_mod_bwd(dz_kv, T // tr, ctx2d, norm1_g, csc, None, "norm1_ctx_bwd", tr)
    grad_x, d_n1g_l, d_sh1, d_sc1 = _norm_mod_bwd(dz_lat, 0, x2d, norm1_g, sc1, dx1, "norm1_lat_bwd", tr)

    zeros_d = jnp.zeros((1, D), F32)
    d_lat = jnp.concatenate([d_sh1, d_sc1, d_g1, d_sh2, d_sc2, d_g2], axis=1)
    d_ctx_part = jnp.concatenate([d_csh, d_csc], axis=1)
    flat = jnp.concatenate(
        [d_n1g_c + d_n1g_l, d_qg, d_kvg, d_gq, d_kg, d_n2g, dcb_a, dcb_b, d_fg,
         dcw_a.reshape(1, -1), dcw_b.reshape(1, -1), d_ctx_part, d_lat], axis=1)
    n_flat = flat.shape[1]
    n_rows = -(-n_flat // (8 * LANES)) * 8
    flat = jnp.pad(flat, ((0, 0), (0, n_rows * LANES - n_flat))).reshape(n_rows, LANES)
    got = _all_gather_small(flat, "gather_small_grads")
    tot = _sum_slots(got, "sum_small_grads").reshape(1, -1)
    sizes = [D, MLA_Q_LORA, MLA_KV_LORA, GQA_HEAD_DIM, GQA_HEAD_DIM, D, F2, D, 3 * FF, 3 * FF, 2 * D]
    offs = [0]
    for s in sizes:
        offs.append(offs[-1] + s)
    t_n1g, t_qg, t_kvg, t_gq, t_kg, t_n2g, t_cb, t_fg, t_cwa, t_cwb, t_ctx = [tot[:, offs[k] : offs[k + 1]] for k in range(len(sizes))]
    g_cw_full = jnp.concatenate([t_cwa.reshape(3, FF), t_cwb.reshape(3, FF)], axis=1)
    g_cw = lax.dynamic_slice(g_cw_full, (0, j * NW), (3, NW))
    d_lat_all = got.reshape(8, -1)[:, offs[-1] : offs[-1] + 6 * D]
    g16 = jnp.concatenate([d_lat_all, jnp.pad(t_ctx, ((0, 0), (0, 4 * D))), jnp.zeros((7, 6 * D), F32)], axis=0)
    g_b_ada = _sum_slots(g16.reshape(16, 1, 6 * D), "sum_b_ada")
    g16_cols = lax.dynamic_slice(g16, (0, j * NA), (16, NA))
    ds_part = _mm(g16_cols, w_ada[0], "NT", F32, "ada_dx")
    got = _all_gather_small(ds_part[8:16], "gather_ada_dx")
    ds_ctx = _sum_slots(jnp.stack([got[2 * s] for s in range(4)]), "sum_ada_dx")[0:1]
    g_c_ctx = _silu_grad_mul(ds_ctx, cc)

    g_kvp = _mm(dpkv, z_all, "TN", BF16, "proj_kv_dw")
    nk = MLA_KV_LORA + 2 * GQA_KV_HEADS * GQA_HEAD_DIM
    g_kv = jnp.concatenate([g_kvp[:MLA_KV_LORA], g_kvp[nk : nk + MLA_ROPE], g_kvp[MLA_KV_LORA:nk]], axis=0)
    g_win_t = _mm(dpq, z_all, "TN", BF16, "proj_q_dw", out_rows=kv_cols + QC + 2 * D, out_off=kv_cols, tm=QC // 2)
    g_win_t = _mm(dpg, z_all, "TN", BF16, "proj_g_dw", out_base=g_win_t, out_off=kv_cols + QC)
    g_win_t = lax.dynamic_update_slice(g_win_t, g_kv, (0, 0))
    sw_in = _swap_start([g_win_t], got, "in")

    h_ffn = _scatter_sums(rs_ffn, sw_in[2][0], "ffn")
    j_ffn = _join_start(h_ffn, grad_x, "ffn")
    h_mix = _scatter_sums(rs_mix, j_ffn[2][0], "mix")
    j_mix = _join_start(h_mix, j_ffn[2][0], "mix")
    rs_in = _scatter_start_after_swap(sw_in, j_mix[2][0], "in")
    g_w_ada = _mm(s16, g16_cols, "TN", F32, "ada_dw", act="silu", after=rs_in[4])
    _, d_ada, m_ada, v_ada = _adamw(w_ada[0], g_w_ada, m_w_ada[0], v_w_ada[0], "adamw_w_ada")
    r_wdown, r_wup = _join_wait(j_ffn, d_ada, "ffn")
    r_wq, r_wkv, r_wbra, r_wbrb, r_wout = _join_wait(j_mix, d_ada, "mix")
    gq_p = _joined(*r_wq).T
    gq = jnp.concatenate([gq_p[:, : 2 * MLA_NOPE].reshape(MLA_Q_LORA, 2, MLA_NOPE), gq_p[:, 2 * MLA_NOPE :].reshape(MLA_Q_LORA, 2, MLA_ROPE)], axis=2)
    grads = {
        "c_ctx": g_c_ctx.reshape(D), "w_ada": g_w_ada[None], "b_ada": g_b_ada, "norm1_g": t_n1g,
        "mla_q_norm_g": t_qg, "w_q_up": gq.reshape(1, MLA_Q_LORA, -1), "mla_kv_norm_g": t_kvg, "w_kv_up": r_wkv,
        "gqa_q_norm_g": t_gq, "gqa_k_norm_g": t_kg, "w_br_a": r_wbra, "w_br_b": r_wbrb, "w_out": r_wout,
        "norm2_g": t_n2g, "w_up": r_wup, "conv_w": g_cw[None], "conv_b": t_cb, "w_down": r_wdown,
        "final_norm_g": t_fg.reshape(D),
    }
    arrives_transposed = ("w_kv_up", "w_br_a", "w_br_b", "w_up")
    arrives_halved = arrives_transposed + ("w_out", "w_down")
    weights = dict(c_ctx=c_ctx, w_ada=w_ada, b_ada=b_ada, norm1_g=norm1_g, w_in=w_in, mla_q_norm_g=mla_q_norm_g, w_q_up=w_q_up,
                   mla_kv_norm_g=mla_kv_norm_g, w_kv_up=w_kv_up, gqa_q_norm_g=gqa_q_norm_g, gqa_k_norm_g=gqa_k_norm_g, w_br_a=w_br_a,
                   w_br_b=w_br_b, w_out=w_out, norm2_g=norm2_g, w_up=w_up, conv_w=conv_w, conv_b=conv_b, w_down=w_down,
                   final_norm_g=final_norm_g)
    m_in = dict(c_ctx=m_c_ctx, w_ada=m_w_ada, b_ada=m_b_ada, norm1_g=m_norm1_g, w_in=m_w_in, mla_q_norm_g=m_mla_q_norm_g,
                w_q_up=m_w_q_up, mla_kv_norm_g=m_mla_kv_norm_g, w_kv_up=m_w_kv_up, gqa_q_norm_g=m_gqa_q_norm_g,
                gqa_k_norm_g=m_gqa_k_norm_g, w_br_a=m_w_br_a, w_br_b=m_w_br_b, w_out=m_w_out, norm2_g=m_norm2_g, w_up=m_w_up,
                conv_w=m_conv_w, conv_b=m_conv_b, w_down=m_w_down, final_norm_g=m_final_norm_g)
    v_in = dict(c_ctx=v_c_ctx, w_ada=v_w_ada, b_ada=v_b_ada, norm1_g=v_norm1_g, w_in=v_w_in, mla_q_norm_g=v_mla_q_norm_g,
                w_q_up=v_w_q_up, mla_kv_norm_g=v_mla_kv_norm_g, w_kv_up=v_w_kv_up, gqa_q_norm_g=v_gqa_q_norm_g,
                gqa_k_norm_g=v_gqa_k_norm_g, w_br_a=v_w_br_a, w_br_b=v_w_br_b, w_out=v_w_out, norm2_g=v_norm2_g, w_up=v_w_up,
                conv_w=v_conv_w, conv_b=v_conv_b, w_down=v_w_down, final_norm_g=v_final_norm_g)
    names = list(weights)
    big = [n for n in names if weights[n].ndim == 3 and weights[n].shape[1] >= 8]
    small = [n for n in names if n not in big]
    delta, new_m, new_v = {}, {}, {}

    def update(n):
        shp = weights[n].shape
        two_d = lambda a: a.reshape(shp[1], shp[2])
        g_t = n in arrives_transposed
        if n in arrives_halved:
            g_in, g_sib = grads[n]
        else:
            g_in, g_sib = two_d(grads[n].astype(F32)), None
        g_, d_, m_, v_ = _adamw(two_d(weights[n]), g_in, two_d(m_in[n]), two_d(v_in[n]), "adamw_" + n, g_transposed=g_t, g_sibling=g_sib)
        grads[n], delta[n], new_m[n], new_v[n] = g_.reshape(shp), d_.reshape(shp), m_.reshape(shp), v_.reshape(shp)

    delta["w_ada"], new_m["w_ada"], new_v["w_ada"] = d_ada[None], m_ada[None], v_ada[None]
    early = [n for n in big if n not in ("w_in", "w_ada")]
    for n in early[:-1]:
        update(n)
    done = sum(delta[n][0, 0:1, 0:1] for n in early[:-1])
    j_in = _join_start(_scatter_sums(rs_in, done, "in"), done, "in")
    last = early[-1]
    grads[last] = (grads[last][0] + j_in[4], grads[last][1])
    update(last)
    ((g_mine, g_sib),) = _join_wait(j_in, delta[last], "in")
    g_, d_, m_, v_ = _adamw(w_in[0].T, g_mine, m_w_in[0].T, v_w_in[0].T, "adamw_w_in", g_sibling=g_sib)
    grads["w_in"], delta["w_in"], new_m["w_in"], new_v["w_in"] = g_.T[None], d_.T[None], m_.T[None], v_.T[None]
    grads = {n: grads[n].reshape(weights[n].shape).astype(F32) for n in names}

    slab = lambda tree: [tree[n].reshape(-1, LANES) for n in small]
    d_, m_, v_ = _adamw_many(slab(weights), slab(grads), slab(m_in), slab(v_in), "adamw_small")
    for k, n in enumerate(small):
        shp = weights[n].shape
        delta[n], new_m[n], new_v[n] = d_[k].reshape(shp), m_[k].reshape(shp), v_[k].reshape(shp)

    return (loss, grad_x[None], *[grads[n] for n in names], *[delta[n] for n in names], *[new_m[n] for n in names],
            *[new_v[n] for n in names])
```

```python
import math

import jax
import jax.numpy as jnp
from jax import lax
from jax.experimental import pallas as pl
from jax.experimental.pallas import tpu as pltpu

F32 = jnp.float32
BF16 = jnp.bfloat16
MESH = pl.DeviceIdType.MESH

NORM_EPS = 1e-6
ROPE_THETA = 10000.0
GRID_W = 64
MLA_HEADS = 8
MLA_Q_LORA = 768
MLA_KV_LORA = 512
MLA_NOPE = 128
MLA_ROPE = 64
MLA_V = 128
GQA_HEADS = 8
GQA_KV_HEADS = 2
GQA_HEAD_DIM = 128
GQA_GROUP = GQA_HEADS // GQA_KV_HEADS
LANES = 128
KVP = MLA_KV_LORA + 2 * GQA_KV_HEADS * GQA_HEAD_DIM + LANES
QC = MLA_Q_LORA + GQA_HEADS * GQA_HEAD_DIM

ADAM_LR = 0.001
ADAM_B1 = 0.9
ADAM_B2 = 0.999
ADAM_EPS = 1e-08
ADAM_WD = 0.01
ADAM_STEP = 10

VMEM_LIMIT = 56 * 1024 * 1024


def _pick(dim, target, mult=LANES):
    t = (min(target, dim) // mult) * mult
    while t >= mult:
        if dim % t == 0:
            return t
        t -= mult
    return dim


def _params(sem):
    return pltpu.CompilerParams(dimension_semantics=sem, vmem_limit_bytes=VMEM_LIMIT)


_DIMS = {"NN": (((1,), (0,)), ((), ())), "NT": (((1,), (1,)), ((), ())), "TN": (((0,), (0,)), ((), ()))}


MM_VMEM_BUDGET = 36 * 1024 * 1024


def _mm_tiles(M, N, K, sa, sb, so, tm, tn, tk):
    tm, tn, tk = _pick(M, tm), _pick(N, tn), _pick(K, tk)

    def need(t):
        return 2 * (tm * t * sa + t * tn * sb) + 2 * tm * tn * so + (tm * tn * 4 if t < K else 0)

    while need(tk) > MM_VMEM_BUDGET and tk > LANES:
        smaller = _pick(K, tk - LANES)
        if smaller >= tk:
            break
        tk = smaller
    return tm, tn, tk


def _window(block, index, offsets):
    if not any(offsets):
        return pl.BlockSpec(block, index)
    for t, o in zip(block, offsets):
        assert o % 16 == 0 and t % 16 == 0, (block, offsets)

    def at(i, j, k):
        return tuple(pl.multiple_of(o + p * t, math.gcd(o, t)) for p, t, o in zip(index(i, j, k), block, offsets))

    return pl.BlockSpec(tuple(pl.Element(t) for t in block), at)


def _mm(a, b, mode, out_dtype, name, m=None, n=None, k=None, b_off=0, add=None, out_rows=None, out_base=None, out_off=0,
        tm=1024, tn=1024, tk=2304, act=None, bias=None, after=None):
    if mode == "NN":
        M, K, N = m or a.shape[0], k or a.shape[1], b.shape[1]
    elif mode == "NT":
        M, K, N = m or a.shape[0], a.shape[1], n or b.shape[0]
    else:
        M, K, N = a.shape[1], k or a.shape[0], b.shape[1]
    tm, tn, tk = _mm_tiles(M, N, K, a.dtype.itemsize, b.dtype.itemsize, jnp.dtype(out_dtype).itemsize, tm, tn, tk)
    nk = K // tk
    dims = _DIMS[mode]
    n_in = 2 + (bias is not None) + (add is not None) + (out_base is not None) + (after is not None)

    def body(*refs):
        a_ref, b_ref = refs[:2]
        bias_ref = refs[2] if bias is not None else None
        add_ref = refs[2 + (bias is not None)] if add is not None else None
        o_ref = refs[n_in]
        av = a_ref[...]
        if act == "silu":
            av = av * jax.nn.sigmoid(av)
        part = lax.dot_general(av.astype(BF16), b_ref[...].astype(BF16), dims, preferred_element_type=F32)

        def finish(r):
            if bias is not None:
                r = r + bias_ref[...]
            if add is not None:
                r = r + add_ref[...]
            o_ref[...] = r.astype(out_dtype)

        if nk == 1:
            finish(part)
            return
        acc = refs[-1]
        k = pl.program_id(2)

        @pl.when(k == 0)
        def _():
            acc[...] = part

        @pl.when(jnp.logical_and(k > 0, k < nk - 1))
        def _():
            acc[...] += part

        @pl.when(k == nk - 1)
        def _():
            finish(acc[...] + part)

    a_spec = pl.BlockSpec((tk, tm), lambda i, j, k: (k, i)) if mode == "TN" else pl.BlockSpec((tm, tk), lambda i, j, k: (i, k))
    if mode == "NT":
        b_spec = _window((tn, tk), lambda i, j, k: (j, k), (b_off, 0))
    else:
        b_spec = _window((tk, tn), lambda i, j, k: (k, j), (b_off, 0))
    in_specs, args = [a_spec, b_spec], [a, b]
    if bias is not None:
        in_specs.append(pl.BlockSpec((1, tn), lambda i, j, k: (0, j)))
        args.append(bias)
    if add is not None:
        in_specs.append(pl.BlockSpec((tm, tn), lambda i, j, k: (i, j)))
        args.append(add)
    aliases = {}
    if after is not None:
        in_specs.append(pl.BlockSpec(after.shape, lambda i, j, k: (0, 0)))
        args.append(after)
    if out_base is not None:
        aliases = {len(args): 0}
        in_specs.append(ANY)
        args.append(out_base)
        out_rows = out_base.shape[0]
    return pl.pallas_call(
        body,
        name=name,
        grid=(M // tm, N // tn, nk),
        in_specs=in_specs,
        out_specs=_window((tm, tn), lambda i, j, k: (i, j), (out_off, 0)),
        out_shape=jax.ShapeDtypeStruct((out_rows or M, N), out_dtype),
        input_output_aliases=aliases,
        scratch_shapes=[pltpu.VMEM((tm, tn), F32)] if nk > 1 else [],
        compiler_params=_params(("parallel", "parallel", "arbitrary")),
    )(*args)


def _rms(x):
    r = lax.rsqrt(jnp.mean(x * x, axis=-1, keepdims=True) + NORM_EPS)
    return x * r, r


def _rms_bwd(xh, r, dxh):
    return r * (dxh - xh * jnp.mean(dxh * xh, axis=-1, keepdims=True))


def _swap(x, q):
    lane = lax.broadcasted_iota(jnp.int32, x.shape, 1)
    even = ((lane // q) % 2) == 0
    return jnp.where(even, pltpu.roll(x, LANES - q, 1), pltpu.roll(x, q, 1))


def _rope(x, cos, ss, q):
    return x * cos + _swap(x, q) * ss


def _rope_t(d, cos, ss, q):
    return d * cos + _swap(d * ss, q)


def _csum(x):
    return jnp.sum(x, axis=0, keepdims=True)


def _rows(tr, w, off=0):
    return pl.BlockSpec((tr, w), lambda i: (i + off, 0))


def _bcast(w):
    return pl.BlockSpec((1, w), lambda i: (0, 0))


def _acc_init(i, refs):
    @pl.when(i == 0)
    def _():
        for r in refs:
            r[...] = jnp.zeros_like(r)


def _rope_tables(n_ctx, n_lat, rot_dim):
    rows = n_lat // GRID_W
    row = jnp.repeat(jnp.arange(rows, dtype=F32), GRID_W)
    col = jnp.tile(jnp.arange(GRID_W, dtype=F32), rows)
    half = rot_dim // 2
    inv_freq = ROPE_THETA ** (-jnp.arange(0, half, 2, dtype=F32) / half)
    ar, ac = row[:, None] * inv_freq, col[:, None] * inv_freq
    cos = jnp.concatenate([jnp.cos(ar), jnp.cos(ar), jnp.cos(ac), jnp.cos(ac)], axis=-1)
    ss = jnp.concatenate([-jnp.sin(ar), jnp.sin(ar), -jnp.sin(ac), jnp.sin(ac)], axis=-1)
    cos = jnp.tile(cos, (1, LANES // rot_dim))
    ss = jnp.tile(ss, (1, LANES // rot_dim))
    cos = jnp.concatenate([cos, jnp.ones((n_ctx, LANES), F32)], axis=0)
    ss = jnp.concatenate([ss, jnp.zeros((n_ctx, LANES), F32)], axis=0)
    return cos, ss


def _norm_mod_fwd(x2d, g, sh, sc, name, tr, out_rows=None, base=None, out_off=0):
    n, d = x2d.shape

    def body(x_ref, g_ref, sh_ref, sc_ref, *rest):
        xh, _ = _rms(x_ref[...])
        rest[-1][...] = ((xh * g_ref[...]) * (1.0 + sc_ref[...]) + sh_ref[...]).astype(BF16)

    args, in_specs, aliases = [x2d, g, sh, sc], [_rows(tr, d), _bcast(d), _bcast(d), _bcast(d)], {}
    if base is not None:
        args.append(base)
        in_specs.append(ANY)
        aliases = {4: 0}
        out_rows = base.shape[0]
    return pl.pallas_call(
        body,
        name=name,
        grid=(n // tr,),
        in_specs=in_specs,
        out_specs=_rows(tr, d, out_off // tr),
        out_shape=jax.ShapeDtypeStruct((out_rows or n, d), BF16),
        input_output_aliases=aliases,
        compiler_params=_params(("parallel",)),
    )(*args)


def _norm_mod_bwd(dz, dz_off, x2d, g, sc, dres, name, tr):
    n, d = x2d.shape
    want_dx = dres is not None

    def body(*refs):
        if want_dx:
            dz_ref, x_ref, g_ref, sc_ref, dres_ref, dx_ref, dg_ref, dsh_ref, dsc_ref = refs
        else:
            dz_ref, x_ref, g_ref, sc_ref, dg_ref, dsh_ref, dsc_ref = refs
        _acc_init(pl.program_id(0), [dg_ref, dsh_ref, dsc_ref])
        xh, r = _rms(x_ref[...])
        dzv = dz_ref[...]
        gv = g_ref[...]
        dsc_ref[...] += _csum(dzv * (xh * gv))
        dsh_ref[...] += _csum(dzv)
        dh = dzv * (1.0 + sc_ref[...])
        dg_ref[...] += _csum(dh * xh)
        if want_dx:
            dx_ref[...] = _rms_bwd(xh, r, dh * gv) + dres_ref[...]

    in_specs = [_rows(tr, d, dz_off), _rows(tr, d), _bcast(d), _bcast(d)]
    args = [dz, x2d, g, sc]
    out_specs = [_bcast(d)] * 3
    out_shape = [jax.ShapeDtypeStruct((1, d), F32)] * 3
    if want_dx:
        in_specs.append(_rows(tr, d))
        args.append(dres)
        out_specs = [_rows(tr, d)] + out_specs
        out_shape = [jax.ShapeDtypeStruct((n, d), F32)] + out_shape
    res = pl.pallas_call(
        body,
        name=name,
        grid=(n // tr,),
        in_specs=in_specs,
        out_specs=out_specs,
        out_shape=out_shape,
        compiler_params=_params(("arbitrary",)),
    )(*args)
    return res if want_dx else (None, *res)


_QA, _QB = MLA_ROPE // 4, GQA_HEAD_DIM // 4


def _kprep_fwd(pkv, kvg, kg, cos_a, ss_a, cos_b, ss_b, tr):
    n = pkv.shape[0]
    nb = GQA_KV_HEADS * GQA_HEAD_DIM

    def body(p_ref, kvg_ref, kg_ref, ca, sa, cb, sb, ckv_ref, kb_ref, vb_ref, kpe_ref):
        p = p_ref[...]
        xh, _ = _rms(p[:, :MLA_KV_LORA])
        ckv_ref[...] = (xh * kvg_ref[...]).astype(BF16)
        for e in range(GQA_KV_HEADS):
            lo = MLA_KV_LORA + e * GQA_HEAD_DIM
            kh, _ = _rms(p[:, lo : lo + GQA_HEAD_DIM])
            kb_ref[:, e * GQA_HEAD_DIM : (e + 1) * GQA_HEAD_DIM] = _rope(kh * kg_ref[...], cb[...], sb[...], _QB).astype(BF16)
        vb_ref[...] = p[:, MLA_KV_LORA + nb : MLA_KV_LORA + 2 * nb].astype(BF16)
        kr = _rope(p[:, MLA_KV_LORA + 2 * nb :], ca[...], sa[...], _QA)
        kpe_ref[:, :LANES] = kr.astype(BF16)
        kpe_ref[:, LANES:] = pltpu.roll(kr, MLA_ROPE, 1).astype(BF16)

    return pl.pallas_call(
        body,
        name="kprep_fwd",
        grid=(n // tr,),
        in_specs=[_rows(tr, KVP), _bcast(MLA_KV_LORA), _bcast(GQA_HEAD_DIM)] + [_rows(tr, LANES)] * 4,
        out_specs=[_rows(tr, MLA_KV_LORA), _rows(tr, nb), _rows(tr, nb), _rows(tr, 2 * LANES)],
        out_shape=[jax.ShapeDtypeStruct((n, w), BF16) for w in (MLA_KV_LORA, nb, nb, 2 * LANES)],
        compiler_params=_params(("parallel",)),
    )(pkv, kvg, kg, cos_a, ss_a, cos_b, ss_b)


def _kprep_bwd(pkv, dckv, dkb, dvb, dkpe, kvg, kg, cos_b, ss_b, tr):
    n = pkv.shape[0]
    nb = GQA_KV_HEADS * GQA_HEAD_DIM

    def body(p_ref, dckv_ref, dkb_ref, dvb_ref, dkpe_ref, kvg_ref, kg_ref, cb, sb, dp_ref, dkvg_ref, dkg_ref):
        _acc_init(pl.program_id(0), [dkvg_ref, dkg_ref])
        p = p_ref[...]
        xh, r = _rms(p[:, :MLA_KV_LORA])
        dn = dckv_ref[...]
        dkvg_ref[...] += _csum(dn * xh)
        dp_ref[:, :MLA_KV_LORA] = _rms_bwd(xh, r, dn * kvg_ref[...]).astype(BF16)
        for e in range(GQA_KV_HEADS):
            lo = MLA_KV_LORA + e * GQA_HEAD_DIM
            kh, rk = _rms(p[:, lo : lo + GQA_HEAD_DIM])
            dk = _rope_t(dkb_ref[:, e * GQA_HEAD_DIM : (e + 1) * GQA_HEAD_DIM], cb[...], sb[...], _QB)
            dkg_ref[...] += _csum(dk * kh)
            dp_ref[:, lo : lo + GQA_HEAD_DIM] = _rms_bwd(kh, rk, dk * kg_ref[...]).astype(BF16)
        dp_ref[:, MLA_KV_LORA + nb : MLA_KV_LORA + 2 * nb] = dvb_ref[...].astype(BF16)
        dp_ref[:, MLA_KV_LORA + 2 * nb :] = dkpe_ref[...].astype(BF16)

    return pl.pallas_call(
        body,
        name="kprep_bwd",
        grid=(n // tr,),
        in_specs=[_rows(tr, KVP), _rows(tr, MLA_KV_LORA), _rows(tr, nb), _rows(tr, nb), _rows(tr, LANES),
                  _bcast(MLA_KV_LORA), _bcast(GQA_HEAD_DIM), _rows(tr, LANES), _rows(tr, LANES)],
        out_specs=[_rows(tr, KVP), _bcast(MLA_KV_LORA), _bcast(GQA_HEAD_DIM)],
        out_shape=[jax.ShapeDtypeStruct((n, KVP), BF16), jax.ShapeDtypeStruct((1, MLA_KV_LORA), F32),
                   jax.ShapeDtypeStruct((1, GQA_HEAD_DIM), F32)],
        compiler_params=_params(("arbitrary",)),
    )(pkv, dckv, dkb, dvb, dkpe, kvg, kg, cos_b, ss_b)


def _kgrad_split(dka, dva, cos_a, ss_a, tr):
    n = dka.shape[0]
    wk = MLA_HEADS * 2 * LANES

    def body(dk_ref, dv_ref, ca, sa, dkv_ref, dkpe_ref):
        even = jnp.zeros((tr, LANES), F32)
        odd = jnp.zeros((tr, LANES), F32)
        for h in range(MLA_HEADS):
            dkv_ref[:, 2 * h * LANES : (2 * h + 1) * LANES] = dk_ref[:, 2 * h * LANES : (2 * h + 1) * LANES].astype(BF16)
            dkv_ref[:, (2 * h + 1) * LANES : (2 * h + 2) * LANES] = dv_ref[:, h * MLA_V : (h + 1) * MLA_V].astype(BF16)
            part = dk_ref[:, (2 * h + 1) * LANES : (2 * h + 2) * LANES]
            if h % 2 == 0:
                even = even + part
            else:
                odd = odd + part
        lane = lax.broadcasted_iota(jnp.int32, (tr, LANES), 1)
        low = lane < MLA_ROPE
        both = jnp.where(low, even, odd)
        tot = jnp.where(low, both + pltpu.roll(both, MLA_ROPE, 1), 0.0)
        dkpe_ref[...] = _rope_t(tot, ca[...], sa[...], _QA)

    return pl.pallas_call(
        body,
        name="kgrad_split",
        grid=(n // tr,),
        in_specs=[_rows(tr, wk), _rows(tr, MLA_HEADS * MLA_V), _rows(tr, LANES), _rows(tr, LANES)],
        out_specs=[_rows(tr, wk), _rows(tr, LANES)],
        out_shape=[jax.ShapeDtypeStruct((n, wk), BF16), jax.ShapeDtypeStruct((n, LANES), F32)],
        compiler_params=_params(("parallel",)),
    )(dka, dva, cos_a, ss_a)


def _qprep_fwd(pq, qg, gq, cos_b, ss_b, tr):
    n = pq.shape[0]
    nq = GQA_HEADS * GQA_HEAD_DIM

    def body(p_ref, qg_ref, gq_ref, cb, sb, cq_ref, qb_ref):
        xh, _ = _rms(p_ref[:, :MLA_Q_LORA])
        cq_ref[...] = (xh * qg_ref[...]).astype(BF16)
        for h in range(GQA_HEADS):
            lo = MLA_Q_LORA + h * GQA_HEAD_DIM
            qh, _ = _rms(p_ref[:, lo : lo + GQA_HEAD_DIM])
            qb_ref[:, h * GQA_HEAD_DIM : (h + 1) * GQA_HEAD_DIM] = _rope(qh * gq_ref[...], cb[...], sb[...], _QB).astype(BF16)

    return pl.pallas_call(
        body,
        name="qprep_fwd",
        grid=(n // tr,),
        in_specs=[_rows(tr, QC), _bcast(MLA_Q_LORA), _bcast(GQA_HEAD_DIM), _rows(tr, LANES), _rows(tr, LANES)],
        out_specs=[_rows(tr, MLA_Q_LORA), _rows(tr, nq)],
        out_shape=[jax.ShapeDtypeStruct((n, MLA_Q_LORA), BF16), jax.ShapeDtypeStruct((n, nq), BF16)],
        compiler_params=_params(("parallel",)),
    )(pq, qg, gq, cos_b, ss_b)


def _qprep_bwd(pq, dcq, dqb, qg, gq, cos_b, ss_b, tr):
    n = pq.shape[0]
    nq = GQA_HEADS * GQA_HEAD_DIM

    def body(p_ref, dcq_ref, dqb_ref, qg_ref, gq_ref, cb, sb, dp_ref, dqg_ref, dgq_ref):
        _acc_init(pl.program_id(0), [dqg_ref, dgq_ref])
        xh, r = _rms(p_ref[:, :MLA_Q_LORA])
        dn = dcq_ref[...]
        dqg_ref[...] += _csum(dn * xh)
        dp_ref[:, :MLA_Q_LORA] = _rms_bwd(xh, r, dn * qg_ref[...]).astype(BF16)
        for h in range(GQA_HEADS):
            lo = MLA_Q_LORA + h * GQA_HEAD_DIM
            qh, rq = _rms(p_ref[:, lo : lo + GQA_HEAD_DIM])
            dq = _rope_t(dqb_ref[:, h * GQA_HEAD_DIM : (h + 1) * GQA_HEAD_DIM], cb[...], sb[...], _QB)
            dgq_ref[...] += _csum(dq * qh)
            dp_ref[:, lo : lo + GQA_HEAD_DIM] = _rms_bwd(qh, rq, dq * gq_ref[...]).astype(BF16)

    return pl.pallas_call(
        body,
        name="qprep_bwd",
        grid=(n // tr,),
        in_specs=[_rows(tr, QC), _rows(tr, MLA_Q_LORA), _rows(tr, nq), _bcast(MLA_Q_LORA), _bcast(GQA_HEAD_DIM),
                  _rows(tr, LANES), _rows(tr, LANES)],
        out_specs=[_rows(tr, QC), _bcast(MLA_Q_LORA), _bcast(GQA_HEAD_DIM)],
        out_shape=[jax.ShapeDtypeStruct((n, QC), BF16), jax.ShapeDtypeStruct((1, MLA_Q_LORA), F32),
                   jax.ShapeDtypeStruct((1, GQA_HEAD_DIM), F32)],
        compiler_params=_params(("arbitrary",)),
    )(pq, dcq, dqb, qg, gq, cos_b, ss_b)


_QA_COLS = MLA_HEADS * (MLA_NOPE + MLA_ROPE)


def _qrope_fwd(qa, cos_a, ss_a, tr):
    n = qa.shape[0]

    def body(q_ref, ca, sa, o_ref):
        for j in range(MLA_HEADS // 2):
            lo = 3 * j * LANES
            o_ref[:, lo : lo + 2 * LANES] = q_ref[:, lo : lo + 2 * LANES].astype(BF16)
            o_ref[:, lo + 2 * LANES : lo + 3 * LANES] = _rope(q_ref[:, lo + 2 * LANES : lo + 3 * LANES], ca[...], sa[...], _QA).astype(BF16)

    return pl.pallas_call(
        body,
        name="qrope_fwd",
        grid=(n // tr,),
        in_specs=[_rows(tr, _QA_COLS), _rows(tr, LANES), _rows(tr, LANES)],
        out_specs=_rows(tr, _QA_COLS),
        out_shape=jax.ShapeDtypeStruct((n, _QA_COLS), BF16),
        compiler_params=_params(("parallel",)),
    )(qa, cos_a, ss_a)


def _qrope_bwd(dq2, cos_a, ss_a, tr):
    n = dq2.shape[0]

    def body(d_ref, ca, sa, o_ref):
        for j in range(MLA_HEADS // 2):
            lo = 3 * j * LANES
            h0, h1 = 2 * j, 2 * j + 1
            o_ref[:, lo : lo + LANES] = d_ref[:, 2 * h0 * LANES : (2 * h0 + 1) * LANES].astype(BF16)
            o_ref[:, lo + LANES : lo + 2 * LANES] = d_ref[:, 2 * h1 * LANES : (2 * h1 + 1) * LANES].astype(BF16)
            pe = d_ref[:, (2 * h0 + 1) * LANES : (2 * h0 + 2) * LANES] + d_ref[:, (2 * h1 + 1) * LANES : (2 * h1 + 2) * LANES]
            o_ref[:, lo + 2 * LANES : lo + 3 * LANES] = _rope_t(pe, ca[...], sa[...], _QA).astype(BF16)

    return pl.pallas_call(
        body,
        name="qrope_bwd",
        grid=(n // tr,),
        in_specs=[_rows(tr, MLA_HEADS * 2 * LANES), _rows(tr, LANES), _rows(tr, LANES)],
        out_specs=_rows(tr, _QA_COLS),
        out_shape=jax.ShapeDtypeStruct((n, _QA_COLS), BF16),
        compiler_params=_params(("parallel",)),
    )(dq2, cos_a, ss_a)


def _cat(refs):
    vals = [r[...] for r in refs]
    return vals[0] if len(vals) == 1 else jnp.concatenate(vals, axis=-1)


LOG2E = 1.4426950408889634


def _attn_fwd(qparts, kparts, vpart, n_heads, group, dv, scale, name, tq, after=None):
    T, Tk = qparts[0][0].shape[0], kparts[0][0].shape[0]
    nq_, nk_ = len(qparts), len(kparts)
    sub = min(tq, 256)
    c2 = scale * LOG2E

    def body(*refs):
        q_refs, k_refs = refs[:nq_], refs[nq_ : nq_ + nk_]
        v_ref = refs[nq_ + nk_]
        o_ref, lse_ref = refs[-2:]
        k = _cat(k_refs)
        v = v_ref[...]
        for r0 in range(0, tq, sub):
            q = _cat([r.at[r0 : r0 + sub, :] for r in q_refs])
            s = lax.dot_general(q, k, _DIMS["NT"], preferred_element_type=F32)
            m = jnp.max(s, axis=-1, keepdims=True)
            p = jnp.exp2((s - m) * c2)
            l = jnp.sum(p, axis=-1, keepdims=True)
            acc = jnp.dot(p.astype(BF16), v, preferred_element_type=F32)
            o_ref[r0 : r0 + sub, :] = (acc * (1.0 / l)).astype(BF16)
            lse_ref[r0 : r0 + sub, :] = m * scale + jnp.log(l)

    in_specs = [pl.BlockSpec((tq, LANES), lambda h, i, f=f: (i, f(h))) for _, f in qparts]
    in_specs += [pl.BlockSpec((Tk, LANES), lambda h, i, f=f: (0, f(h // group))) for _, f in kparts]
    fv = vpart[1]
    in_specs.append(pl.BlockSpec((Tk, dv), lambda h, i: (0, fv(h // group))))
    args = [*[a for a, _ in qparts], *[a for a, _ in kparts], vpart[0]]
    if after is not None:
        in_specs.append(pl.BlockSpec(after.shape, lambda h, i: (0, 0)))
        args.append(after)
    return pl.pallas_call(
        body,
        name=name,
        grid=(n_heads, T // tq),
        in_specs=in_specs,
        out_specs=[pl.BlockSpec((tq, dv), lambda h, i: (i, h)), pl.BlockSpec((None, tq, 1), lambda h, i: (h, i, 0))],
        out_shape=[jax.ShapeDtypeStruct((T, n_heads * dv), BF16), jax.ShapeDtypeStruct((n_heads, T, 1), F32)],
        compiler_params=_params(("parallel", "parallel")),
    )(*args)


def _attn_bwd(qparts, kparts, vpart, o, do, lse, n_heads, group, dv, scale, name, tq):
    T, Tk = qparts[0][0].shape[0], kparts[0][0].shape[0]
    nq_, nk_ = len(qparts), len(kparts)
    dk_ = LANES * nq_
    n_kv = n_heads // group
    nblk = T // tq
    c2 = scale * LOG2E

    def head(hk, i):
        return hk * group + i // nblk

    sub = min(tq, 256)

    def body(*refs):
        q_refs = refs[:nq_]
        k = _cat(refs[nq_ : nq_ + nk_])
        v_ref, o_ref, do_ref, lse_ref, dq_ref, dk_ref, dv_ref = refs[nq_ + nk_ :]
        i = pl.program_id(1)
        _acc_init(i, [dk_ref, dv_ref])
        v = v_ref[...]
        dk_acc, dv_acc = None, None
        for r0 in range(0, tq, sub):
            rows = slice(r0, r0 + sub)
            q = _cat([r.at[rows, :] for r in q_refs])
            s = lax.dot_general(q, k, _DIMS["NT"], preferred_element_type=F32)
            p = jnp.exp2(s * c2 - lse_ref[rows, :] * LOG2E)
            dov = do_ref[rows, :]
            dp = lax.dot_general(dov, v, _DIMS["NT"], preferred_element_type=F32)
            delta = jnp.sum(dov.astype(F32) * o_ref[rows, :].astype(F32), axis=-1, keepdims=True)
            ds = (p * (dp - delta)).astype(BF16)
            dq_ref[rows, :] = jnp.dot(ds, k, preferred_element_type=F32) * scale
            dk_part = lax.dot_general(ds, q, _DIMS["TN"], preferred_element_type=F32)
            dv_part = lax.dot_general(p.astype(BF16), dov, _DIMS["TN"], preferred_element_type=F32)
            dk_acc = dk_part if dk_acc is None else dk_acc + dk_part
            dv_acc = dv_part if dv_acc is None else dv_acc + dv_part
        dk_ref[...] += dk_acc
        dv_ref[...] += dv_acc

        @pl.when(i == group * nblk - 1)
        def _():
            dk_ref[...] *= scale

    in_specs = [pl.BlockSpec((tq, LANES), lambda hk, i, f=f: (i % nblk, f(head(hk, i)))) for _, f in qparts]
    in_specs += [pl.BlockSpec((Tk, LANES), lambda hk, i, f=f: (0, f(hk))) for _, f in kparts]
    fv = vpart[1]
    in_specs.append(pl.BlockSpec((Tk, dv), lambda hk, i: (0, fv(hk))))
    in_specs += [pl.BlockSpec((tq, dv), lambda hk, i: (i % nblk, head(hk, i)))] * 2
    in_specs.append(pl.BlockSpec((None, tq, 1), lambda hk, i: (head(hk, i), i % nblk, 0)))
    return pl.pallas_call(
        body,
        name=name,
        grid=(n_kv, group * nblk),
        in_specs=in_specs,
        out_specs=[pl.BlockSpec((tq, dk_), lambda hk, i: (i % nblk, head(hk, i))),
                   pl.BlockSpec((Tk, dk_), lambda hk, i: (0, hk)),
                   pl.BlockSpec((Tk, dv), lambda hk, i: (0, hk))],
        out_shape=[jax.ShapeDtypeStruct((T, n_heads * dk_), F32), jax.ShapeDtypeStruct((Tk, n_kv * dk_), F32),
                   jax.ShapeDtypeStruct((Tk, n_kv * dv), F32)],
        compiler_params=_params(("parallel", "arbitrary")),
    )(*[a for a, _ in qparts], *[a for a, _ in kparts], vpart[0], o, do, lse)


def _gates_fwd(pg, ya, yb, tr):
    n, d = ya.shape

    def body(pg_ref, ya_ref, yb_ref, o_ref):
        ga = jax.nn.sigmoid(pg_ref[:, :d].astype(F32))
        gb = jax.nn.sigmoid(pg_ref[:, d:].astype(F32))
        o_ref[...] = (ga * ya_ref[...].astype(F32) + gb * yb_ref[...].astype(F32)).astype(BF16)

    return pl.pallas_call(
        body,
        name="gates_fwd",
        grid=(n // tr,),
        in_specs=[_rows(tr, 2 * d), _rows(tr, d), _rows(tr, d)],
        out_specs=_rows(tr, d),
        out_shape=jax.ShapeDtypeStruct((n, d), BF16),
        compiler_params=_params(("parallel",)),
    )(pg, ya, yb)


def _gates_bwd(dm, pg, ya, yb, tr):
    n, d = ya.shape

    def body(dm_ref, pg_ref, ya_ref, yb_ref, dya_ref, dyb_ref, dpg_ref):
        dmv = dm_ref[...].astype(F32)
        ga = jax.nn.sigmoid(pg_ref[:, :d].astype(F32))
        gb = jax.nn.sigmoid(pg_ref[:, d:].astype(F32))
        dya_ref[...] = (dmv * ga).astype(BF16)
        dyb_ref[...] = (dmv * gb).astype(BF16)
        dpg_ref[:, :d] = (dmv * ya_ref[...].astype(F32) * ga * (1.0 - ga)).astype(BF16)
        dpg_ref[:, d:] = (dmv * yb_ref[...].astype(F32) * gb * (1.0 - gb)).astype(BF16)

    return pl.pallas_call(
        body,
        name="gates_bwd",
        grid=(n // tr,),
        in_specs=[_rows(tr, d), _rows(tr, 2 * d), _rows(tr, d), _rows(tr, d)],
        out_specs=[_rows(tr, d), _rows(tr, d), _rows(tr, 2 * d)],
        out_shape=[jax.ShapeDtypeStruct((n, d), BF16), jax.ShapeDtypeStruct((n, d), BF16), jax.ShapeDtypeStruct((n, 2 * d), BF16)],
        compiler_params=_params(("parallel",)),
    )(dm, pg, ya, yb)


def _resid_norm2_fwd(x2d, att, g1, n2g, sh2, sc2, tr):
    n, d = x2d.shape

    def body(x_ref, a_ref, g1_ref, g_ref, sh_ref, sc_ref, x1_ref, z_ref):
        x1 = x_ref[...] + g1_ref[...] * a_ref[...]
        x1_ref[...] = x1
        xh, _ = _rms(x1)
        z_ref[...] = ((xh * g_ref[...]) * (1.0 + sc_ref[...]) + sh_ref[...]).astype(BF16)

    return pl.pallas_call(
        body,
        name="resid_norm2_fwd",
        grid=(n // tr,),
        in_specs=[_rows(tr, d), _rows(tr, d)] + [_bcast(d)] * 4,
        out_specs=[_rows(tr, d), _rows(tr, d)],
        out_shape=[jax.ShapeDtypeStruct((n, d), F32), jax.ShapeDtypeStruct((n, d), BF16)],
        compiler_params=_params(("parallel",)),
    )(x2d, att, g1, n2g, sh2, sc2)


def _resid_norm2_bwd(dz2, x1, dx2, att, n2g, sc2, g1, tr):
    n, d = x1.shape

    def body(dz_ref, x1_ref, dx2_ref, a_ref, g_ref, sc_ref, g1_ref, dx1_ref, da_ref, dg_ref, dsh_ref, dsc_ref, dg1_ref):
        _acc_init(pl.program_id(0), [dg_ref, dsh_ref, dsc_ref, dg1_ref])
        xh, r = _rms(x1_ref[...])
        dzv = dz_ref[...]
        gv = g_ref[...]
        dsc_ref[...] += _csum(dzv * (xh * gv))
        dsh_ref[...] += _csum(dzv)
        dh = dzv * (1.0 + sc_ref[...])
        dg_ref[...] += _csum(dh * xh)
        dx1 = _rms_bwd(xh, r, dh * gv) + dx2_ref[...]
        dx1_ref[...] = dx1
        dg1_ref[...] += _csum(dx1 * a_ref[...])
        da_ref[...] = (dx1 * g1_ref[...]).astype(BF16)

    return pl.pallas_call(
        body,
        name="resid_norm2_bwd",
        grid=(n // tr,),
        in_specs=[_rows(tr, d)] * 4 + [_bcast(d)] * 3,
        out_specs=[_rows(tr, d), _rows(tr, d)] + [_bcast(d)] * 4,
        out_shape=[jax.ShapeDtypeStruct((n, d), F32), jax.ShapeDtypeStruct((n, d), BF16)] + [jax.ShapeDtypeStruct((1, d), F32)] * 4,
        compiler_params=_params(("arbitrary",)),
    )(dz2, x1, dx2, att, n2g, sc2, g1)


def _edges(shape):
    row = lax.broadcasted_iota(jnp.int32, shape, 0)
    return row == 0, row == shape[0] - 1


def _shifts(u, edges):
    n = u.shape[0]
    return jnp.where(edges[0], 0.0, pltpu.roll(u, 1, 0)), jnp.where(edges[1], 0.0, pltpu.roll(u, n - 1, 0))


def _conv3(u, prev, nxt, w_ref, b_ref):
    return b_ref[...] + w_ref[0:1, :] * prev + w_ref[1:2, :] * u + w_ref[2:3, :] * nxt


def _ffn_up_conv(z, wup_t, cw, cb, tc, after):
    n, d = z.shape
    f = wup_t.shape[0] // 2
    nb = f // tc

    def body(z_ref, wa_ref, wb_ref, cwa, cwb, cba, cbb, after_ref, ua_ref, ub_ref, h_ref):
        w = jnp.concatenate([wa_ref[...], wb_ref[...]], axis=0)
        u = lax.dot_general(z_ref[...], w, _DIMS["NT"], preferred_element_type=F32).astype(BF16)
        ua_ref[...] = u[:, :tc]
        ub_ref[...] = u[:, tc:]
        edges = _edges((n, tc))
        ua = u[:, :tc].astype(F32)
        ub = u[:, tc:].astype(F32)
        a = _conv3(ua, *_shifts(ua, edges), cwa, cba)
        b = _conv3(ub, *_shifts(ub, edges), cwb, cbb)
        h_ref[...] = (a * jax.nn.sigmoid(a) * b).astype(BF16)

    col = lambda rows, off: pl.BlockSpec((rows, tc), lambda i: (0, i + off))
    w_rows = lambda off: pl.BlockSpec((tc, d), lambda i: (i + off, 0))
    return pl.pallas_call(
        body,
        name="ffn_up_conv",
        grid=(nb,),
        in_specs=[pl.BlockSpec((n, d), lambda i: (0, 0)), w_rows(0), w_rows(nb), col(3, 0), col(3, nb), col(1, 0), col(1, nb),
                  pl.BlockSpec(after.shape, lambda i: (0, 0))],
        out_specs=[col(n, 0)] * 3,
        out_shape=[jax.ShapeDtypeStruct((n, f), BF16)] * 3,
        compiler_params=_params(("parallel",)),
    )(z, wup_t, wup_t, cw, cw, cb, cb, after)


def _ffn_down_dx_conv_bwd(df, wdown, u_a, u_b, cw, cb, tc, after):
    n, f = u_a.shape
    d = df.shape[1]
    nb = f // tc

    def part(uv, prev, nxt, duc, edges, w_ref, du_ref, dw_ref, db_ref):
        db_ref[...] = _csum(duc)
        dw_ref[0:1, :] = _csum(duc * prev)
        dw_ref[1:2, :] = _csum(duc * uv)
        dw_ref[2:3, :] = _csum(duc * nxt)
        d_prev, d_next = _shifts(duc, edges)
        du_ref[...] = (w_ref[0:1, :] * d_next + w_ref[1:2, :] * duc + w_ref[2:3, :] * d_prev).astype(BF16)

    def body(df_ref, wd_ref, ua_ref, ub_ref, wa_ref, wb_ref, ba_ref, bb_ref, after_ref,
             dua_ref, dub_ref, dwa_ref, dwb_ref, dba_ref, dbb_ref):
        dhv = lax.dot_general(df_ref[...], wd_ref[...], _DIMS["NT"], preferred_element_type=F32)
        dhv = dhv.astype(BF16).astype(F32)
        edges = _edges((n, tc))
        ua = ua_ref[...].astype(F32)
        ub = ub_ref[...].astype(F32)
        sa = _shifts(ua, edges)
        sb = _shifts(ub, edges)
        a = _conv3(ua, *sa, wa_ref, ba_ref)
        b = _conv3(ub, *sb, wb_ref, bb_ref)
        sg = jax.nn.sigmoid(a)
        da = dhv * b * (sg * (1.0 + a * (1.0 - sg)))
        db = dhv * (a * sg)
        part(ua, *sa, da, edges, wa_ref, dua_ref, dwa_ref, dba_ref)
        part(ub, *sb, db, edges, wb_ref, dub_ref, dwb_ref, dbb_ref)

    col = lambda rows, off: pl.BlockSpec((rows, tc), lambda i: (0, i + off))
    return pl.pallas_call(
        body,
        name="ffn_down_dx_conv_bwd",
        grid=(nb,),
        in_specs=[pl.BlockSpec((n, d), lambda i: (0, 0)), pl.BlockSpec((tc, d), lambda i: (i, 0)), col(n, 0), col(n, 0),
                  col(3, 0), col(3, nb), col(1, 0), col(1, nb), pl.BlockSpec(after.shape, lambda i: (0, 0))],
        out_specs=[col(n, 0), col(n, 0), col(3, 0), col(3, 0), col(1, 0), col(1, 0)],
        out_shape=[jax.ShapeDtypeStruct((n, f), BF16)] * 2 + [jax.ShapeDtypeStruct((3, f), F32)] * 2 + [jax.ShapeDtypeStruct((1, f), F32)] * 2,
        compiler_params=_params(("parallel",)),
    )(df, wdown, u_a, u_b, cw, cw, cb, cb, after)


def _loss_head(x1, f, g2, fg, tgt, tr):
    n, d = x1.shape

    def body(x1_ref, f_ref, g2_ref, fg_ref, t_ref, sq_ref, dx2_ref, dfg_ref, dg2_ref, df_ref):
        _acc_init(pl.program_id(0), [sq_ref, dfg_ref, dg2_ref])
        fv = f_ref[...]
        xh, r = _rms(x1_ref[...] + g2_ref[...] * fv)
        err = xh * fg_ref[...] - t_ref[...]
        sq_ref[...] += _csum(err * err)
        dy = err * (1.0 / d)
        dfg_ref[...] += _csum(dy * xh)
        dx2 = _rms_bwd(xh, r, dy * fg_ref[...])
        dx2_ref[...] = dx2
        dg2_ref[...] += _csum(dx2 * fv)
        df_ref[...] = (dx2 * g2_ref[...]).astype(BF16)

    return pl.pallas_call(
        body,
        name="loss_head",
        grid=(n // tr,),
        in_specs=[_rows(tr, d), _rows(tr, d), _bcast(d), _bcast(d), _rows(tr, d)],
        out_specs=[_bcast(d), _rows(tr, d), _bcast(d), _bcast(d), _rows(tr, d)],
        out_shape=[jax.ShapeDtypeStruct((1, d), F32), jax.ShapeDtypeStruct((n, d), F32), jax.ShapeDtypeStruct((1, d), F32),
                   jax.ShapeDtypeStruct((1, d), F32), jax.ShapeDtypeStruct((n, d), BF16)],
        compiler_params=_params(("arbitrary",)),
    )(x1, f, g2, fg, tgt)


def _sum_slots(g, name):
    s, r, w = g.shape

    def body(g_ref, o_ref):
        acc = g_ref[0]
        for k in range(1, s):
            acc = acc + g_ref[k]
        o_ref[...] = acc

    return pl.pallas_call(body, name=name, out_shape=jax.ShapeDtypeStruct((r, w), F32))(g)


def _silu_grad_mul(ds, cvec):
    def body(d_ref, c_ref, o_ref):
        cv = c_ref[...]
        sg = jax.nn.sigmoid(cv)
        o_ref[...] = d_ref[...] * (sg * (1.0 + cv * (1.0 - sg)))

    return pl.pallas_call(body, name="silu_grad_mul", out_shape=jax.ShapeDtypeStruct(ds.shape, F32))(ds, cvec)


def _adamw_update(wv, gv, mv, vv, d_ref, mo_ref, vo_ref):
    mn = ADAM_B1 * mv + (1.0 - ADAM_B1) * gv
    vn = ADAM_B2 * vv + (1.0 - ADAM_B2) * (gv * gv)
    mo_ref[...] = mn
    vo_ref[...] = vn
    m_hat = mn / (1.0 - ADAM_B1**ADAM_STEP)
    v_hat = vn / (1.0 - ADAM_B2**ADAM_STEP)
    d_ref[...] = -ADAM_LR * (m_hat / (jnp.sqrt(v_hat) + ADAM_EPS) + ADAM_WD * wv)


def _adamw_many(ws, gs, ms, vs, name):
    n = len(ws)

    def body(*refs):
        for k in range(n):
            w_ref, g_ref, m_ref, v_ref = (refs[q * n + k] for q in range(4))
            d_ref, mo_ref, vo_ref = (refs[(4 + q) * n + k] for q in range(3))
            _adamw_update(w_ref[...], g_ref[...], m_ref[...], v_ref[...], d_ref, mo_ref, vo_ref)

    res = pl.pallas_call(body, name=name, out_shape=[jax.ShapeDtypeStruct(w.shape, F32) for w in ws] * 3)(*ws, *gs, *ms, *vs)
    return res[:n], res[n : 2 * n], res[2 * n :]


def _adamw(w, g, m, v, name, g_transposed=False, g_sibling=None):
    r, cdim = w.shape
    halves = g_sibling is not None
    block = 1 << 19
    if g_transposed:
        tc = _pick(cdim // 2 if halves else cdim, 2048)
        tr = _pick(r, max(LANES, block // tc), LANES)
        per_half = (cdim // 2) // tc
    else:
        rows = r // 2 if halves else r
        tc = _pick(cdim, 2048)
        tr = _pick(rows, max(8, block // tc), 8)
        if tr < 64 and rows > 64:
            tr, tc = _pick(rows, 1024, 8), _pick(cdim, 512)
        per_half = (r // 2) // tr
    emit_g = g_transposed or halves

    def body(w_ref, g_ref, *rest):
        m_ref, v_ref = rest[halves : halves + 2]
        outs = rest[halves + 2 :]
        gv = g_ref[...]
        if halves:
            along = pl.program_id(1 if g_transposed else 0)
            gv = jnp.where(along // per_half == lax.axis_index("c"), gv, rest[0][...])
        if g_transposed:
            gv = gv.T
        if emit_g:
            outs[0][...] = gv
        _adamw_update(w_ref[...], gv, m_ref[...], v_ref[...], *outs[-3:])

    spec = pl.BlockSpec((tr, tc), lambda i, j: (i, j))
    if g_transposed:
        g_spec = pl.BlockSpec((tc, tr), lambda i, j: (j % per_half if halves else j, i))
    else:
        g_spec = pl.BlockSpec((tr, tc), lambda i, j: (i % per_half if halves else i, j))
    n_out = 3 + emit_g
    res = pl.pallas_call(
        body,
        name=name,
        grid=(r // tr, cdim // tc),
        in_specs=[spec, g_spec] + [g_spec] * halves + [spec, spec],
        out_specs=[spec] * n_out,
        out_shape=[jax.ShapeDtypeStruct((r, cdim), F32)] * n_out,
        compiler_params=_params(("parallel", "parallel")),
    )(w, g, *([g_sibling] if halves else []), m, v)
    return res if emit_g else [g, *res]


def _place():
    return lax.axis_index("x"), lax.axis_index("y"), lax.axis_index("c")


def _remote(src, dst, send_sem, recv_sem, dev):
    return pltpu.make_async_remote_copy(src_ref=src, dst_ref=dst, send_sem=send_sem, recv_sem=recv_sem, device_id=dev, device_id_type=MESH)


ANY = pl.BlockSpec(memory_space=pl.ANY)


def _all_gather_small(v, name):
    r, w = v.shape

    def body(v_ref, o_ref, send, recv, lsem):
        x, y, c = _place()
        me = 4 * x + 2 * y + c
        mine = pltpu.make_async_copy(v_ref, o_ref.at[me], lsem)
        mine.start()
        sent = []
        for k in range(1, 8):
            px, py, pc = x ^ (k >> 2), y ^ ((k >> 1) & 1), c ^ (k & 1)
            cp = _remote(v_ref, o_ref.at[me], send.at[k - 1], recv.at[k - 1], (px, py, pc))
            cp.start()
            sent.append(cp)
        for k in range(1, 8):
            px, py, pc = x ^ (k >> 2), y ^ ((k >> 1) & 1), c ^ (k & 1)
            slot = o_ref.at[4 * px + 2 * py + pc]
            _remote(slot, slot, send.at[k - 1], recv.at[k - 1], (x, y, c)).wait_recv()
        for cp in sent:
            cp.wait_send()
        mine.wait()

    return pl.pallas_call(
        body,
        name=name,
        out_shape=jax.ShapeDtypeStruct((8, r, w), F32),
        in_specs=[pl.BlockSpec(memory_space=pltpu.VMEM)],
        out_specs=pl.BlockSpec(memory_space=pltpu.VMEM),
        scratch_shapes=[pltpu.SemaphoreType.DMA((7,)), pltpu.SemaphoreType.DMA((7,)), pltpu.SemaphoreType.DMA],
        compiler_params=pltpu.CompilerParams(vmem_limit_bytes=VMEM_LIMIT),
    )(v)


HBM = pl.BlockSpec(memory_space=pltpu.HBM)
SEM = pl.BlockSpec(memory_space=pltpu.SEMAPHORE)
EFFECT = pltpu.SideEffectType.DATAFLOW_SIDE_EFFECTING


def _other_chips(x, y):
    return [(1 - x, y), (x, 1 - y), (1 - x, 1 - y)]


def _bulk_start(name, srcs, land_shapes, n_copies, copies, after):
    n, m = len(srcs), len(land_shapes)

    def body(*refs):
        src_refs, land_refs = refs[:n], refs[n : n + m]
        send, recv = refs[n + m + 1], refs[n + m + 2]
        token = refs[-1]
        for k, (s, d, dev) in enumerate(copies(src_refs, land_refs)):
            _remote(s, d, send.at[k], recv.at[k], dev).start()
        token[...] = jnp.zeros_like(token)

    lands = [pltpu.with_memory_space_constraint(lax.empty(s.shape, s.dtype), pltpu.HBM) for s in land_shapes]
    out = pl.pallas_call(
        body,
        name=name,
        out_shape=(pltpu.SemaphoreType.DMA((n_copies,)), pltpu.SemaphoreType.DMA((n_copies,)),
                   *[pltpu.HBM(s.shape, s.dtype) for s in srcs], *[pltpu.HBM(s.shape, s.dtype) for s in land_shapes],
                   jax.ShapeDtypeStruct((8, LANES), F32)),
        in_specs=[HBM] * (n + m) + [ANY],
        out_specs=(SEM, SEM, *[HBM] * (n + m), pl.BlockSpec(memory_space=pltpu.VMEM)),
        input_output_aliases={i: 2 + i for i in range(n + m)},
        compiler_params=pltpu.CompilerParams(has_side_effects=EFFECT),
    )(*[pltpu.with_memory_space_constraint(s, pltpu.HBM) for s in srcs], *lands, after)
    return out[0], out[1], list(out[2 : 2 + n]), list(out[2 + n : 2 + n + m]), out[-1][0:1, 0:1]


def _bulk_wait(name, send, recv, srcs, lands, after, waits):
    n, m = len(srcs), len(lands)

    def body(*refs):
        src_refs, land_refs = refs[:n], refs[n : n + m]
        send_sem, recv_sem = refs[n + m], refs[n + m + 1]
        x, y, c = _place()
        for k, (s, d) in enumerate(waits(src_refs, land_refs)):
            cp = _remote(s, d, send_sem.at[k], recv_sem.at[k], (x, y, c))
            cp.wait_send()
            cp.wait_recv()

    out = pl.pallas_call(
        body,
        name=name,
        out_shape=tuple(pltpu.HBM(s.shape, s.dtype) for s in (*srcs, *lands)),
        in_specs=[HBM] * (n + m) + [SEM, SEM, ANY],
        out_specs=tuple([HBM] * (n + m)),
        input_output_aliases={i: i for i in range(n + m)},
        compiler_params=pltpu.CompilerParams(has_side_effects=EFFECT),
    )(*srcs, *lands, send, recv, after)
    return list(out[:n]), list(out[n:])


def _gather_start(shards, after, name):
    def copies(src, land):
        x, y, c = _place()
        j = 2 * x + y
        return [(src[a].at[c], land[a].at[j, c], (px, py, c)) for a in range(len(shards)) for px, py in _other_chips(x, y)]

    shapes = [jax.ShapeDtypeStruct((4,) + s.shape, s.dtype) for s in shards]
    return _bulk_start(name, shards, shapes, 3 * len(shards), copies, after)


def _gather_wait(started, after, name):
    send, recv, srcs, lands, _ = started

    def waits(src, land):
        x, y, c = _place()
        return [(src[a].at[c], land[a].at[2 * px + py, c]) for a in range(len(srcs)) for px, py in _other_chips(x, y)]

    return _bulk_wait(name, send, recv, srcs, lands, after, waits)


def _forward_halves(lands, name):
    n = len(lands)

    def body(*refs):
        bufs = refs[n : 2 * n]
        send, recv = refs[2 * n :]
        x, y, c = _place()
        started = []
        for a in range(n):
            for k, (px, py) in enumerate(_other_chips(x, y)):
                blk = bufs[a].at[2 * px + py, c]
                cp = _remote(blk, blk, send.at[3 * a + k], recv.at[3 * a + k], (x, y, 1 - c))
                cp.start()
                started.append(cp)
        for a in range(n):
            for k, (px, py) in enumerate(_other_chips(x, y)):
                blk = bufs[a].at[2 * px + py, 1 - c]
                _remote(blk, blk, send.at[3 * a + k], recv.at[3 * a + k], (x, y, c)).wait_recv()
        for cp in started:
            cp.wait_send()

    return pl.pallas_call(
        body,
        name=name,
        out_shape=[jax.ShapeDtypeStruct(b.shape, b.dtype) for b in lands],
        in_specs=[ANY] * n,
        out_specs=[ANY] * n,
        input_output_aliases={i: i for i in range(n)},
        scratch_shapes=[pltpu.SemaphoreType.DMA((3 * n,)), pltpu.SemaphoreType.DMA((3 * n,))],
    )(*lands)


def _forward_start(lands, after, name):
    def copies(src, _):
        x, y, c = _place()
        blocks = [src[a].at[2 * px + py, c] for a in range(len(lands)) for px, py in _other_chips(x, y)]
        return [(b, b, (x, y, 1 - c)) for b in blocks]

    return _bulk_start(name, lands, [], 3 * len(lands), copies, after)


def _forward_wait(started, after, name):
    send, recv, bufs, _, _ = started

    def waits(src, _):
        x, y, c = _place()
        return [(src[a].at[2 * px + py, c], src[a].at[2 * px + py, 1 - c]) for a in range(len(bufs)) for px, py in _other_chips(x, y)]

    return _bulk_wait(name, send, recv, bufs, [], after, waits)[0]


def _place_own(shards, lands):
    j = 2 * lax.axis_index("x") + lax.axis_index("y")
    full = [lax.dynamic_update_slice(b, s[None], (j, 0, 0, 0)) for b, s in zip(lands, shards)]
    return [f.reshape(4 * f.shape[2] * 2, f.shape[3]) for f in full]


def _gather_finish(started, after, tag):
    shards, lands = _gather_wait(started, after, "gather_wait_" + tag)
    return _place_own(shards, _forward_halves(lands, "gather_forward_" + tag))


def _gather_land(started, after, tag):
    shards, lands = _gather_wait(started, after, "gather_wait_" + tag)
    return shards, _forward_start(lands, shards[0], "forward_start_" + tag)


def _gather_done(landed, after, tag):
    shards, fwd = landed
    return _place_own(shards, _forward_wait(fwd, after, "forward_wait_" + tag))


def _swap_halves(grads, name):
    n = len(grads)

    def body(*refs):
        ins, outs = refs[:n], refs[n : 2 * n]
        send, recv = refs[2 * n :]
        x, y, c = _place()
        started = []
        for a in range(n):
            for s in range(4):
                cp = _remote(ins[a].at[s, 1 - c], outs[a].at[s], send.at[4 * a + s], recv.at[4 * a + s], (x, y, 1 - c))
                cp.start()
                started.append(cp)
        for cp in started:
            cp.wait_recv()
        for cp in started:
            cp.wait_send()

    return pl.pallas_call(
        body,
        name=name,
        out_shape=[jax.ShapeDtypeStruct((4,) + g.shape[2:], g.dtype) for g in grads],
        in_specs=[ANY] * n,
        out_specs=[ANY] * n,
        scratch_shapes=[pltpu.SemaphoreType.DMA((4 * n,)), pltpu.SemaphoreType.DMA((4 * n,))],
    )(*grads)


def _add_halves(grads, others, tag):
    outs = []
    for a, (g, o) in enumerate(zip(grads, others)):
        _, _, rh, cdim = g.shape
        tr = _pick(rh, 512, 16)

        def body(g_ref, o_ref, p_ref):
            p_ref[...] = (g_ref[...].astype(F32) + o_ref[...].astype(F32)).astype(BF16)

        outs.append(
            pl.pallas_call(
                body,
                name=f"add_halves_{tag}{a}",
                grid=(4, rh // tr),
                in_specs=[pl.BlockSpec((None, None, tr, cdim), lambda s, i: (s, lax.axis_index("c"), i, 0)),
                          pl.BlockSpec((None, tr, cdim), lambda s, i: (s, i, 0))],
                out_specs=pl.BlockSpec((None, tr, cdim), lambda s, i: (s, i, 0)),
                out_shape=jax.ShapeDtypeStruct((4, rh, cdim), BF16),
                compiler_params=_params(("parallel", "parallel")),
            )(g, o)
        )
    return outs


def _exchange_start(parts, after, name):
    def copies(src, land):
        x, y, c = _place()
        j = 2 * x + y
        return [(src[a].at[2 * px + py], land[a].at[j], (px, py, c)) for a in range(len(parts)) for px, py in _other_chips(x, y)]

    return _bulk_start(name, parts, [jax.ShapeDtypeStruct(p.shape, p.dtype) for p in parts], 3 * len(parts), copies, after)


def _exchange_finish(started, after, name):
    send, recv, srcs, lands, _ = started

    def waits(src, land):
        x, y, _ = _place()
        return [(src[a].at[2 * px + py], land[a].at[2 * px + py]) for a in range(len(srcs)) for px, py in _other_chips(x, y)]

    srcs, lands = _bulk_wait(name, send, recv, srcs, lands, after, waits)
    j = 2 * lax.axis_index("x") + lax.axis_index("y")
    return [lax.dynamic_update_slice(b, lax.dynamic_slice(p, (j, 0, 0), (1,) + p.shape[1:]), (j, 0, 0)) for b, p in zip(lands, srcs)]


def _sum_chips(recvd, tag):
    outs = []
    for a, g in enumerate(recvd):
        _, rh, cdim = g.shape
        tr = _pick(rh, 512, 16)

        def body(g_ref, o_ref):
            o_ref[...] = ((g_ref[0].astype(F32) + g_ref[1].astype(F32)) + g_ref[2].astype(F32)) + g_ref[3].astype(F32)

        outs.append(
            pl.pallas_call(
                body,
                name=f"sum_chips_{tag}{a}",
                grid=(rh // tr,),
                in_specs=[pl.BlockSpec((4, tr, cdim), lambda i: (0, i, 0))],
                out_specs=pl.BlockSpec((tr, cdim), lambda i: (i, 0)),
                out_shape=jax.ShapeDtypeStruct((rh, cdim), F32),
                compiler_params=_params(("parallel",)),
            )(g)
        )
    return outs


def _join_halves(halves, name):
    n = len(halves)

    def body(*refs):
        ins, outs = refs[:n], refs[n : 2 * n]
        send, recv = refs[2 * n :]
        x, y, c = _place()
        started = []
        for a in range(n):
            cp = _remote(ins[a], outs[a], send.at[a], recv.at[a], (x, y, 1 - c))
            cp.start()
            started.append(cp)
        for cp in started:
            cp.wait_recv()
        for cp in started:
            cp.wait_send()

    others = pl.pallas_call(
        body,
        name=name,
        out_shape=[jax.ShapeDtypeStruct(h.shape, h.dtype) for h in halves],
        in_specs=[ANY] * n,
        out_specs=[ANY] * n,
        scratch_shapes=[pltpu.SemaphoreType.DMA((n,)), pltpu.SemaphoreType.DMA((n,))],
    )(*halves)
    return list(zip(halves, others))


def _joined(mine, other):
    first = lax.axis_index("c") == 0
    return jnp.concatenate([jnp.where(first, mine, other), jnp.where(first, other, mine)], axis=0)


def _grad_views(grads):
    return [g.reshape(4, 2, g.shape[0] // 8, g.shape[1]) for g in grads]


def _scatter_start(grads, tag, after=None):
    views = _grad_views(grads)
    others = _swap_halves(views, "swap_halves_" + tag)
    mine = _add_halves(views, others, tag)
    return _exchange_start(mine, others[-1] if after is None else after, "exchange_start_" + tag)


def _swap_start(grads, after, tag):
    views = _grad_views(grads)

    def copies(src, land):
        x, y, c = _place()
        return [(src[a].at[s, 1 - c], land[a].at[s], (x, y, 1 - c)) for a in range(len(views)) for s in range(4)]

    shapes = [jax.ShapeDtypeStruct((4,) + v.shape[2:], v.dtype) for v in views]
    return _bulk_start("swap_start_" + tag, views, shapes, 4 * len(views), copies, after)


def _scatter_start_after_swap(swapped, after, tag):
    send, recv, views, lands, _ = swapped

    def waits(src, land):
        c = lax.axis_index("c")
        return [(src[a].at[s, 1 - c], land[a].at[s]) for a in range(len(views)) for s in range(4)]

    views, others = _bulk_wait("swap_wait_" + tag, send, recv, views, lands, after, waits)
    mine = _add_halves(views, others, tag)
    return _exchange_start(mine, others[-1], "exchange_start_" + tag)


def _join_start(halves, after, tag):
    def copies(src, land):
        x, y, c = _place()
        return [(src[a], land[a], (x, y, 1 - c)) for a in range(len(halves))]

    return _bulk_start("join_start_" + tag, halves, [jax.ShapeDtypeStruct(h.shape, h.dtype) for h in halves], len(halves), copies, after)


def _join_wait(started, after, tag):
    send, recv, halves, lands, _ = started
    halves, others = _bulk_wait("join_wait_" + tag, send, recv, halves, lands, after, lambda src, land: list(zip(src, land)))
    return list(zip(halves, others))


def _scatter_sums(started, after, tag):
    return _sum_chips(_exchange_finish(started, after, "exchange_wait_" + tag), tag)


def _scatter_finish(started, after, tag):
    return _join_halves(_scatter_sums(started, after, tag), "join_halves_" + tag)


def _t_bf16(w):
    return w.T.astype(BF16)


def kernel(x, c, ctx, c_ctx, w_ada, b_ada, norm1_g, w_in, mla_q_norm_g, w_q_up, mla_kv_norm_g, w_kv_up, gqa_q_norm_g, gqa_k_norm_g, w_br_a, w_br_b, w_out, norm2_g, w_up, conv_w, conv_b, w_down, final_norm_g, loss_target, m_c_ctx, m_w_ada, m_b_ada, m_norm1_g, m_w_in, m_mla_q_norm_g, m_w_q_up, m_mla_kv_norm_g, m_w_kv_up, m_gqa_q_norm_g, m_gqa_k_norm_g, m_w_br_a, m_w_br_b, m_w_out, m_norm2_g, m_w_up, m_conv_w, m_conv_b, m_w_down, m_final_norm_g, v_c_ctx, v_w_ada, v_b_ada, v_norm1_g, v_w_in, v_mla_q_norm_g, v_w_q_up, v_mla_kv_norm_g, v_w_kv_up, v_gqa_q_norm_g, v_gqa_k_norm_g, v_w_br_a, v_w_br_b, v_w_out, v_norm2_g, v_w_up, v_conv_w, v_conv_b, v_w_down, v_final_norm_g):
    T, D = x.shape[1], x.shape[2]
    C = ctx.shape[1]
    NA = w_ada.shape[2]
    NW = w_up.shape[2]
    F2 = 4 * NW
    FF = F2 // 2
    xi, yi, ci = _place()
    j = 2 * xi + yi
    me = 4 * xi + 2 * yi + ci
    tr = _pick(C, 128, 8)
    tq = _pick(T, 256)

    x2d, tgt, ctx2d = x[0], loss_target[0], ctx[0]
    fg = final_norm_g.reshape(1, D)
    cc = c_ctx.reshape(1, D)

    halve = lambda s: s.reshape(2, s.shape[0] // 2, s.shape[1])
    win_shard = halve(_t_bf16(w_in[0]))
    w0 = max(D, NW)
    pay = jnp.zeros((8, w0), F32).at[0:1, :D].set(c).at[1:4, :NW].set(conv_w[0])
    got = _all_gather_small(pay, "gather_cond")
    c_all = got[:, 0, :D]
    cw = jnp.concatenate([got[2 * s, 1:4, :NW] for s in range(4)], axis=1)
    s16 = jnp.concatenate([c_all, cc, jnp.zeros((7, D), F32)], axis=0)
    b_cols = lax.dynamic_slice(b_ada, (0, j * NA), (1, NA))
    ada_part = _mm(s16, w_ada[0], "NN", F32, "ada_fwd", act="silu", bias=b_cols)
    got = _all_gather_small(ada_part, "gather_ada")
    ada = jnp.concatenate([got[2 * s] for s in range(4)], axis=1)
    lat = lax.dynamic_slice(ada, (me, 0), (1, 6 * D))
    sh1, sc1, g1, sh2, sc2, g2 = [lat[:, k * D : (k + 1) * D] for k in range(6)]
    csh, csc = ada[8:9, :D], ada[8:9, D : 2 * D]

    ag_in = _gather_start([win_shard], got, "gather_start_in")
    t_in = ag_in[4]
    wq3 = (w_q_up[0] + t_in).reshape(MLA_Q_LORA, 2, MLA_NOPE + MLA_ROPE)
    wq_perm = jnp.concatenate([wq3[:, :, :MLA_NOPE].reshape(MLA_Q_LORA, -1), wq3[:, :, MLA_NOPE:].reshape(MLA_Q_LORA, -1)], axis=1)
    low = [_t_bf16(wq_perm), _t_bf16(w_kv_up[0] + t_in)]
    br = [_t_bf16(w_br_a[0] + t_in), _t_bf16(w_br_b[0] + t_in), (w_out[0] + t_in).astype(BF16)]
    ag_low = _gather_start([halve(s) for s in low], t_in, "gather_start_low")
    ag_br = _gather_start([halve(s) for s in br], ag_low[4], "gather_start_br")
    ag_up = _gather_start([halve(_t_bf16(w_up[0] + t_in))], ag_br[4], "gather_start_up")
    ag_down = _gather_start([halve((w_down[0] + t_in).astype(BF16))], ag_up[4], "gather_start_down")
    sh1 = sh1 + ag_down[4]

    cos_a, ss_a = _rope_tables(C, T, MLA_ROPE)
    cos_b, ss_b = _rope_tables(C, T, GQA_HEAD_DIM)
    lcos_a, lss_a, lcos_b, lss_b = cos_a[:T], ss_a[:T], cos_b[:T], ss_b[:T]

    z_all = _norm_mod_fwd(x2d, norm1_g, sh1, sc1, "norm1_lat_fwd", tr, out_rows=T + C)
    z_all = _norm_mod_fwd(ctx2d, norm1_g, csh, csc, "norm1_ctx_fwd", tr, base=z_all, out_off=T)
    (win_t,) = _gather_finish(ag_in, z_all, "in")
    kv_cols = KVP - LANES + MLA_ROPE
    e_kpe = MLA_KV_LORA + MLA_ROPE
    w_kvp = jnp.concatenate([win_t[:MLA_KV_LORA], win_t[e_kpe:kv_cols], win_t[MLA_KV_LORA:e_kpe], jnp.zeros((LANES - MLA_ROPE, D), BF16)], axis=0)

    pkv = _mm(z_all, w_kvp, "NT", F32, "proj_kv", tn=KVP)
    pq = _mm(z_all, win_t, "NT", F32, "proj_q", m=T, n=QC, b_off=kv_cols)
    low_landed = _gather_land(ag_low, pq, "low")
    pg = _mm(z_all, win_t, "NT", BF16, "proj_g", m=T, n=2 * D, b_off=kv_cols + QC, after=low_landed[1][4])
    wq_t, wkv_t = _gather_done(low_landed, pg, "low")
    ckv_n, kb2, vb2, kpe2 = _kprep_fwd(pkv, mla_kv_norm_g, gqa_k_norm_g, cos_a, ss_a, cos_b, ss_b, tr)
    kv_up = _mm(ckv_n, wkv_t, "NT", BF16, "kv_up")
    cq_n, qb2 = _qprep_fwd(pq, mla_q_norm_g, gqa_q_norm_g, lcos_b, lss_b, tr)
    q_a = _mm(cq_n, wq_t, "NT", F32, "q_up")
    qar = _qrope_fwd(q_a, lcos_a, lss_a, tr)

    a_q = [(qar, lambda h: 3 * (h // 2) + h % 2), (qar, lambda h: 3 * (h // 2) + 2)]
    a_k = [(kv_up, lambda h: 2 * h), (kpe2, lambda h: h % 2)]
    a_v = (kv_up, lambda h: 2 * h + 1)
    a_scale = float(MLA_NOPE + MLA_ROPE) ** -0.5
    b_q = [(qb2, lambda h: h)]
    b_k = [(kb2, lambda h: h)]
    b_v = (vb2, lambda h: h)
    b_scale = float(GQA_HEAD_DIM) ** -0.5
    tq_f = _pick(T, 512)
    o_a, lse_a = _attn_fwd(a_q, a_k, a_v, MLA_HEADS, 1, MLA_V, a_scale, "attn_a_fwd", tq_f)
    br_landed = _gather_land(ag_br, o_a, "br")
    o_b, lse_b = _attn_fwd(b_q, b_k, b_v, GQA_HEADS, GQA_GROUP, GQA_HEAD_DIM, b_scale, "attn_b_fwd", tq_f, after=br_landed[1][4])
    wbra_t, wbrb_t, wout = _gather_done(br_landed, o_b, "br")
    up_landed = _gather_land(ag_up, o_b, "up")
    ya = _mm(o_a, wbra_t, "NT", BF16, "br_a", after=up_landed[1][4])
    yb = _mm(o_b, wbrb_t, "NT", BF16, "br_b")
    merged = _gates_fwd(pg, ya, yb, tr)
    att = _mm(merged, wout, "NN", F32, "out_proj")
    x1, z2 = _resid_norm2_fwd(x2d, att, g1, norm2_g, sh2, sc2, tr)
    (wup_t,) = _gather_done(up_landed, z2, "up")
    down_landed = _gather_land(ag_down, z2, "down")
    tc = _pick(FF, 128)
    u_a, u_b, hg = _ffn_up_conv(z2, wup_t, cw, conv_b, tc, down_landed[1][4])
    (wdown,) = _gather_done(down_landed, hg, "down")
    f = _mm(hg, wdown, "NN", F32, "ffn_down", tk=FF // 2)
    sq, dx2, d_fg, d_g2, df = _loss_head(x1, f, g2, fg, tgt, tr)
    loss = lax.psum(0.5 * jnp.sum(sq) / D, ("x", "y", "c"))

    du_a, du_b, dcw_a, dcw_b, dcb_a, dcb_b = _ffn_down_dx_conv_bwd(df, wdown, u_a, u_b, cw, conv_b, _pick(FF, 256), loss.reshape(1, 1))
    g_wdown = _mm(hg, df, "TN", BF16, "ffn_down_dw", tm=FF // 4)
    dz2 = _mm(du_a, wup_t, "NN", F32, "ffn_up_dx_a", tk=FF // 2)
    dz2 = _mm(du_b, wup_t, "NN", F32, "ffn_up_dx_b", b_off=FF, add=dz2, tk=FF // 2)
    g_wup_t = _mm(du_a, z2, "TN", BF16, "ffn_up_dw_a", out_rows=F2, tm=FF // 4)
    g_wup_t = _mm(du_b, z2, "TN", BF16, "ffn_up_dw_b", out_base=g_wup_t, out_off=FF, tm=FF // 4)
    sw_ffn = _swap_start([g_wdown, g_wup_t], sc2, "ffn")
    sc2 = sc2 + sw_ffn[4]
    dx1, datt, d_n2g, d_sh2, d_sc2, d_g1 = _resid_norm2_bwd(dz2, x1, dx2, att, norm2_g, sc2, g1, tr)

    dmerged = _mm(datt, wout, "NT", BF16, "out_proj_dx")
    rs_ffn = _scatter_start_after_swap(sw_ffn, dmerged, "ffn")
    lse_a = lse_a + rs_ffn[4]
    g_wout = _mm(merged, datt, "TN", BF16, "out_proj_dw")
    dya, dyb, dpg = _gates_bwd(dmerged, pg, ya, yb, tr)
    do_a = _mm(dya, wbra_t, "NN", BF16, "br_a_dx")
    g_wbra_t = _mm(dya, o_a, "TN", BF16, "br_a_dw")
    do_b = _mm(dyb, wbrb_t, "NN", BF16, "br_b_dx")
    g_wbrb_t = _mm(dyb, o_b, "TN", BF16, "br_b_dw")
    dqa2, dka2, dva2 = _attn_bwd(a_q, a_k, a_v, o_a, do_a, lse_a, MLA_HEADS, 1, MLA_V, a_scale, "attn_a_bwd", tq_f)
    dqb2, dkb2, dvb2 = _attn_bwd(b_q, b_k, b_v, o_b, do_b, lse_b, GQA_HEADS, GQA_GROUP, GQA_HEAD_DIM, b_scale, "attn_b_bwd", tq_f)
    dq_a = _qrope_bwd(dqa2, lcos_a, lss_a, tr)
    dcq_n = _mm(dq_a, wq_t, "NN", F32, "q_up_dx")
    g_wq_t = _mm(dq_a, cq_n, "TN", BF16, "q_up_dw")
    dpq, d_qg, d_gq = _qprep_bwd(pq, dcq_n, dqb2, mla_q_norm_g, gqa_q_norm_g, lcos_b, lss_b, tr)
    dkv_up, dkpe = _kgrad_split(dka2, dva2, cos_a, ss_a, tr)
    dckv_n = _mm(dkv_up, wkv_t, "NN", F32, "kv_up_dx")
    g_wkv_t = _mm(dkv_up, ckv_n, "TN", BF16, "kv_up_dw")
    rs_mix = _scatter_start([g_wq_t, g_wkv_t, g_wbra_t, g_wbrb_t, g_wout], "mix")
    dpkv, d_kvg, d_kg = _kprep_bwd(pkv, dckv_n, dkb2, dvb2, dkpe, mla_kv_norm_g + rs_mix[4], gqa_k_norm_g, cos_b, ss_b, tr)
    dz_kv = _mm(dpkv, w_kvp, "NN", F32, "proj_kv_dx")
    dz_lat = _mm(dpq, win_t, "NN", F32, "proj_q_dx", b_off=kv_cols, add=dz_kv)
    dz_lat = _mm(dpg, win_t, "NN", F32, "proj_g_dx", b_off=kv_cols + QC, add=dz_lat)
    _, d_n1g_c, d_csh, d_csc = _norm_mod_bwd(dz_kv, T // tr, ctx2d, norm1_g, csc, None, "norm1_ctx_bwd", tr)
    grad_x, d_n1g_l, d_sh1, d_sc1 = _norm_mod_bwd(dz_lat, 0, x2d, norm1_g, sc1, dx1, "norm1_lat_bwd", tr)

    zeros_d = jnp.zeros((1, D), F32)
    d_lat = jnp.concatenate([d_sh1, d_sc1, d_g1, d_sh2, d_sc2, d_g2], axis=1)
    d_ctx_part = jnp.concatenate([d_csh, d_csc], axis=1)
    flat = jnp.concatenate(
        [d_n1g_c + d_n1g_l, d_qg, d_kvg, d_gq, d_kg, d_n2g, dcb_a, dcb_b, d_fg,
         dcw_a.reshape(1, -1), dcw_b.reshape(1, -1), d_ctx_part, d_lat], axis=1)
    n_flat = flat.shape[1]
    n_rows = -(-n_flat // (8 * LANES)) * 8
    flat = jnp.pad(flat, ((0, 0), (0, n_rows * LANES - n_flat))).reshape(n_rows, LANES)
    got = _all_gather_small(flat, "gather_small_grads")
    tot = _sum_slots(got, "sum_small_grads").reshape(1, -1)
    sizes = [D, MLA_Q_LORA, MLA_KV_LORA, GQA_HEAD_DIM, GQA_HEAD_DIM, D, F2, D, 3 * FF, 3 * FF, 2 * D]
    offs = [0]
    for s in sizes:
        offs.append(offs[-1] + s)
    t_n1g, t_qg, t_kvg, t_gq, t_kg, t_n2g, t_cb, t_fg, t_cwa, t_cwb, t_ctx = [tot[:, offs[k] : offs[k + 1]] for k in range(len(sizes))]
    g_cw_full = jnp.concatenate([t_cwa.reshape(3, FF), t_cwb.reshape(3, FF)], axis=1)
    g_cw = lax.dynamic_slice(g_cw_full, (0, j * NW), (3, NW))
    d_lat_all = got.reshape(8, -1)[:, offs[-1] : offs[-1] + 6 * D]
    g16 = jnp.concatenate([d_lat_all, jnp.pad(t_ctx, ((0, 0), (0, 4 * D))), jnp.zeros((7, 6 * D), F32)], axis=0)
    g_b_ada = _sum_slots(g16.reshape(16, 1, 6 * D), "sum_b_ada")
    g16_cols = lax.dynamic_slice(g16, (0, j * NA), (16, NA))
    ds_part = _mm(g16_cols, w_ada[0], "NT", F32, "ada_dx")
    got = _all_gather_small(ds_part[8:16], "gather_ada_dx")
    ds_ctx = _sum_slots(jnp.stack([got[2 * s] for s in range(4)]), "sum_ada_dx")[0:1]
    g_c_ctx = _silu_grad_mul(ds_ctx, cc)

    g_kvp = _mm(dpkv, z_all, "TN", BF16, "proj_kv_dw")
    nk = MLA_KV_LORA + 2 * GQA_KV_HEADS * GQA_HEAD_DIM
    g_kv = jnp.concatenate([g_kvp[:MLA_KV_LORA], g_kvp[nk : nk + MLA_ROPE], g_kvp[MLA_KV_LORA:nk]], axis=0)
    g_win_t = _mm(dpq, z_all, "TN", BF16, "proj_q_dw", out_rows=kv_cols + QC + 2 * D, out_off=kv_cols, tm=QC // 2)
    g_win_t = _mm(dpg, z_all, "TN", BF16, "proj_g_dw", out_base=g_win_t, out_off=kv_cols + QC)
    g_win_t = lax.dynamic_update_slice(g_win_t, g_kv, (0, 0))
    sw_in = _swap_start([g_win_t], got, "in")

    h_ffn = _scatter_sums(rs_ffn, sw_in[2][0], "ffn")
    j_ffn = _join_start(h_ffn, grad_x, "ffn")
    h_mix = _scatter_sums(rs_mix, j_ffn[2][0], "mix")
    j_mix = _join_start(h_mix, j_ffn[2][0], "mix")
    rs_in = _scatter_start_after_swap(sw_in, j_mix[2][0], "in")
    g_w_ada = _mm(s16, g16_cols, "TN", F32, "ada_dw", act="silu", after=rs_in[4])
    _, d_ada, m_ada, v_ada = _adamw(w_ada[0], g_w_ada, m_w_ada[0], v_w_ada[0], "adamw_w_ada")
    r_wdown, r_wup = _join_wait(j_ffn, d_ada, "ffn")
    r_wq, r_wkv, r_wbra, r_wbrb, r_wout = _join_wait(j_mix, d_ada, "mix")
    gq_p = _joined(*r_wq).T
    gq = jnp.concatenate([gq_p[:, : 2 * MLA_NOPE].reshape(MLA_Q_LORA, 2, MLA_NOPE), gq_p[:, 2 * MLA_NOPE :].reshape(MLA_Q_LORA, 2, MLA_ROPE)], axis=2)
    grads = {
        "c_ctx": g_c_ctx.reshape(D), "w_ada": g_w_ada[None], "b_ada": g_b_ada, "norm1_g": t_n1g,
        "mla_q_norm_g": t_qg, "w_q_up": gq.reshape(1, MLA_Q_LORA, -1), "mla_kv_norm_g": t_kvg, "w_kv_up": r_wkv,
        "gqa_q_norm_g": t_gq, "gqa_k_norm_g": t_kg, "w_br_a": r_wbra, "w_br_b": r_wbrb, "w_out": r_wout,
        "norm2_g": t_n2g, "w_up": r_wup, "conv_w": g_cw[None], "conv_b": t_cb, "w_down": r_wdown,
        "final_norm_g": t_fg.reshape(D),
    }
    arrives_transposed = ("w_kv_up", "w_br_a", "w_br_b", "w_up")
    arrives_halved = arrives_transposed + ("w_out", "w_down")
    weights = dict(c_ctx=c_ctx, w_ada=w_ada, b_ada=b_ada, norm1_g=norm1_g, w_in=w_in, mla_q_norm_g=mla_q_norm_g, w_q_up=w_q_up,
                   mla_kv_norm_g=mla_kv_norm_g, w_kv_up=w_kv_up, gqa_q_norm_g=gqa_q_norm_g, gqa_k_norm_g=gqa_k_norm_g, w_br_a=w_br_a,
                   w_br_b=w_br_b, w_out=w_out, norm2_g=norm2_g, w_up=w_up, conv_w=conv_w, conv_b=conv_b, w_down=w_down,
                   final_norm_g=final_norm_g)
    m_in = dict(c_ctx=m_c_ctx, w_ada=m_w_ada, b_ada=m_b_ada, norm1_g=m_norm1_g, w_in=m_w_in, mla_q_norm_g=m_mla_q_norm_g,
                w_q_up=m_w_q_up, mla_kv_norm_g=m_mla_kv_norm_g, w_kv_up=m_w_kv_up, gqa_q_norm_g=m_gqa_q_norm_g,
                gqa_k_norm_g=m_gqa_k_norm_g, w_br_a=m_w_br_a, w_br_b=m_w_br_b, w_out=m_w_out, norm2_g=m_norm2_g, w_up=m_w_up,
                conv_w=m_conv_w, conv_b=m_conv_b, w_down=m_w_down, final_norm_g=m_final_norm_g)
    v_in = dict(c_ctx=v_c_ctx, w_ada=v_w_ada, b_ada=v_b_ada, norm1_g=v_norm1_g, w_in=v_w_in, mla_q_norm_g=v_mla_q_norm_g,
                w_q_up=v_w_q_up, mla_kv_norm_g=v_mla_kv_norm_g, w_kv_up=v_w_kv_up, gqa_q_norm_g=v_gqa_q_norm_g,
                gqa_k_norm_g=v_gqa_k_norm_g, w_br_a=v_w_br_a, w_br_b=v_w_br_b, w_out=v_w_out, norm2_g=v_norm2_g, w_up=v_w_up,
                conv_w=v_conv_w, conv_b=v_conv_b, w_down=v_w_down, final_norm_g=v_final_norm_g)
    names = list(weights)
    big = [n for n in names if weights[n].ndim == 3 and weights[n].shape[1] >= 8]
    small = [n for n in names if n not in big]
    delta, new_m, new_v = {}, {}, {}

    def update(n):
        shp = weights[n].shape
        two_d = lambda a: a.reshape(shp[1], shp[2])
        g_t = n in arrives_transposed
        if n in arrives_halved:
            g_in, g_sib = grads[n]
        else:
            g_in, g_sib = two_d(grads[n].astype(F32)), None
        g_, d_, m_, v_ = _adamw(two_d(weights[n]), g_in, two_d(m_in[n]), two_d(v_in[n]), "adamw_" + n, g_transposed=g_t, g_sibling=g_sib)
        grads[n], delta[n], new_m[n], new_v[n] = g_.reshape(shp), d_.reshape(shp), m_.reshape(shp), v_.reshape(shp)

    delta["w_ada"], new_m["w_ada"], new_v["w_ada"] = d_ada[None], m_ada[None], v_ada[None]
    early = [n for n in big if n not in ("w_in", "w_ada")]
    for n in early[:-1]:
        update(n)
    done = sum(delta[n][0, 0:1, 0:1] for n in early[:-1])
    j_in = _join_start(_scatter_sums(rs_in, done, "in"), done, "in")
    last = early[-1]
    grads[last] = (grads[last][0] + j_in[4], grads[last][1])
    update(last)
    ((g_mine, g_sib),) = _join_wait(j_in, delta[last], "in")
    g_, d_, m_, v_ = _adamw(w_in[0].T, g_mine, m_w_in[0].T, v_w_in[0].T, "adamw_w_in", g_sibling=g_sib)
    grads["w_in"], delta["w_in"], new_m["w_in"], new_v["w_in"] = g_.T[None], d_.T[None], m_.T[None], v_.T[None]
    grads = {n: grads[n].reshape(weights[n].shape).astype(F32) for n in names}

    slab = lambda tree: [tree[n].reshape(-1, LANES) for n in small]
    d_, m_, v_ = _adamw_many(slab(weights), slab(grads), slab(m_in), slab(v_in), "adamw_small")
    for k, n in enumerate(small):
        shp = weights[n].shape
        delta[n], new_m[n], new_v[n] = d_[k].reshape(shp), m_[k].reshape(shp), v_[k].reshape(shp)

    return (loss, grad_x[None], *[grads[n] for n in names], *[delta[n] for n in names], *[new_m[n] for n in names],
            *[new_v[n] for n in names])
```

```python
import math

import jax
import jax.numpy as jnp
from jax import lax
from jax.experimental import pallas as pl
from jax.experimental.pallas import tpu as pltpu

F32 = jnp.float32
BF16 = jnp.bfloat16
MESH = pl.DeviceIdType.MESH

NORM_EPS = 1e-6
ROPE_THETA = 10000.0
GRID_W = 64
MLA_HEADS = 8
MLA_Q_LORA = 768
MLA_KV_LORA = 512
MLA_NOPE = 128
MLA_ROPE = 64
MLA_V = 128
GQA_HEADS = 8
GQA_KV_HEADS = 2
GQA_HEAD_DIM = 128
GQA_GROUP = GQA_HEADS // GQA_KV_HEADS
LANES = 128
KVP = MLA_KV_LORA + 2 * GQA_KV_HEADS * GQA_HEAD_DIM + LANES
QC = MLA_Q_LORA + GQA_HEADS * GQA_HEAD_DIM

ADAM_LR = 0.001
ADAM_B1 = 0.9
ADAM_B2 = 0.999
ADAM_EPS = 1e-08
ADAM_WD = 0.01
ADAM_STEP = 10

VMEM_LIMIT = 56 * 1024 * 1024


def _pick(dim, target, mult=LANES):
    t = (min(target, dim) // mult) * mult
    while t >= mult:
        if dim % t == 0:
            return t
        t -= mult
    return dim


def _params(sem):
    return pltpu.CompilerParams(dimension_semantics=sem, vmem_limit_bytes=VMEM_LIMIT)


_DIMS = {"NN": (((1,), (0,)), ((), ())), "NT": (((1,), (1,)), ((), ())), "TN": (((0,), (0,)), ((), ()))}


MM_VMEM_BUDGET = 36 * 1024 * 1024


def _mm_tiles(M, N, K, sa, sb, so, tm, tn, tk):
    tm, tn, tk = _pick(M, tm), _pick(N, tn), _pick(K, tk)

    def need(t):
        return 2 * (tm * t * sa + t * tn * sb) + 2 * tm * tn * so + (tm * tn * 4 if t < K else 0)

    while need(tk) > MM_VMEM_BUDGET and tk > LANES:
        smaller = _pick(K, tk - LANES)
        if smaller >= tk:
            break
        tk = smaller
    return tm, tn, tk


def _window(block, index, offsets):
    if not any(offsets):
        return pl.BlockSpec(block, index)
    for t, o in zip(block, offsets):
        assert o % 16 == 0 and t % 16 == 0, (block, offsets)

    def at(i, j, k):
        return tuple(pl.multiple_of(o + p * t, math.gcd(o, t)) for p, t, o in zip(index(i, j, k), block, offsets))

    return pl.BlockSpec(tuple(pl.Element(t) for t in block), at)


def _mm(a, b, mode, out_dtype, name, m=None, n=None, k=None, b_off=0, add=None, out_rows=None, out_base=None, out_off=0,
        tm=1024, tn=1024, tk=2304, act=None, bias=None, after=None):
    if mode == "NN":
        M, K, N = m or a.shape[0], k or a.shape[1], b.shape[1]
    elif mode == "NT":
        M, K, N = m or a.shape[0], a.shape[1], n or b.shape[0]
    else:
        M, K, N = a.shape[1], k or a.shape[0], b.shape[1]
    tm, tn, tk = _mm_tiles(M, N, K, a.dtype.itemsize, b.dtype.itemsize, jnp.dtype(out_dtype).itemsize, tm, tn, tk)
    nk = K // tk
    dims = _DIMS[mode]
    n_in = 2 + (bias is not None) + (add is not None) + (out_base is not None) + (after is not None)

    def body(*refs):
        a_ref, b_ref = refs[:2]
        bias_ref = refs[2] if bias is not None else None
        add_ref = refs[2 + (bias is not None)] if add is not None else None
        o_ref = refs[n_in]
        av = a_ref[...]
        if act == "silu":
            av = av * jax.nn.sigmoid(av)
        part = lax.dot_general(av.astype(BF16), b_ref[...].astype(BF16), dims, preferred_element_type=F32)

        def finish(r):
            if bias is not None:
                r = r + bias_ref[...]
            if add is not None:
                r = r + add_ref[...]
            o_ref[...] = r.astype(out_dtype)

        if nk == 1:
            finish(part)
            return
        acc = refs[-1]
        k = pl.program_id(2)

        @pl.when(k == 0)
        def _():
            acc[...] = part

        @pl.when(jnp.logical_and(k > 0, k < nk - 1))
        def _():
            acc[...] += part

        @pl.when(k == nk - 1)
        def _():
            finish(acc[...] + part)

    a_spec = pl.BlockSpec((tk, tm), lambda i, j, k: (k, i)) if mode == "TN" else pl.BlockSpec((tm, tk), lambda i, j, k: (i, k))
    if mode == "NT":
        b_spec = _window((tn, tk), lambda i, j, k: (j, k), (b_off, 0))
    else:
        b_spec = _window((tk, tn), lambda i, j, k: (k, j), (b_off, 0))
    in_specs, args = [a_spec, b_spec], [a, b]
    if bias is not None:
        in_specs.append(pl.BlockSpec((1, tn), lambda i, j, k: (0, j)))
        args.append(bias)
    if add is not None:
        in_specs.append(pl.BlockSpec((tm, tn), lambda i, j, k: (i, j)))
        args.append(add)
    aliases = {}
    if after is not None:
        in_specs.append(pl.BlockSpec(after.shape, lambda i, j, k: (0, 0)))
        args.append(after)
    if out_base is not None:
        aliases = {len(args): 0}
        in_specs.append(ANY)
        args.append(out_base)
        out_rows = out_base.shape[0]
    return pl.pallas_call(
        body,
        name=name,
        grid=(M // tm, N // tn, nk),
        in_specs=in_specs,
        out_specs=_window((tm, tn), lambda i, j, k: (i, j), (out_off, 0)),
        out_shape=jax.ShapeDtypeStruct((out_rows or M, N), out_dtype),
        input_output_aliases=aliases,
        scratch_shapes=[pltpu.VMEM((tm, tn), F32)] if nk > 1 else [],
        compiler_params=_params(("parallel", "parallel", "arbitrary")),
    )(*args)


def _rms(x):
    r = lax.rsqrt(jnp.mean(x * x, axis=-1, keepdims=True) + NORM_EPS)
    return x * r, r


def _rms_bwd(xh, r, dxh):
    return r * (dxh - xh * jnp.mean(dxh * xh, axis=-1, keepdims=True))


def _swap(x, q):
    lane = lax.broadcasted_iota(jnp.int32, x.shape, 1)
    even = ((lane // q) % 2) == 0
    return jnp.where(even, pltpu.roll(x, LANES - q, 1), pltpu.roll(x, q, 1))


def _rope(x, cos, ss, q):
    return x * cos + _swap(x, q) * ss


def _rope_t(d, cos, ss, q):
    return d * cos + _swap(d * ss, q)


def _csum(x):
    return jnp.sum(x, axis=0, keepdims=True)


def _rows(tr, w, off=0):
    return pl.BlockSpec((tr, w), lambda i: (i + off, 0))


def _bcast(w):
    return pl.BlockSpec((1, w), lambda i: (0, 0))


def _acc_init(i, refs):
    @pl.when(i == 0)
    def _():
        for r in refs:
            r[...] = jnp.zeros_like(r)


def _rope_tables(n_ctx, n_lat, rot_dim):
    rows = n_lat // GRID_W
    row = jnp.repeat(jnp.arange(rows, dtype=F32), GRID_W)
    col = jnp.tile(jnp.arange(GRID_W, dtype=F32), rows)
    half = rot_dim // 2
    inv_freq = ROPE_THETA ** (-jnp.arange(0, half, 2, dtype=F32) / half)
    ar, ac = row[:, None] * inv_freq, col[:, None] * inv_freq
    cos = jnp.concatenate([jnp.cos(ar), jnp.cos(ar), jnp.cos(ac), jnp.cos(ac)], axis=-1)
    ss = jnp.concatenate([-jnp.sin(ar), jnp.sin(ar), -jnp.sin(ac), jnp.sin(ac)], axis=-1)
    cos = jnp.tile(cos, (1, LANES // rot_dim))
    ss = jnp.tile(ss, (1, LANES // rot_dim))
    cos = jnp.concatenate([cos, jnp.ones((n_ctx, LANES), F32)], axis=0)
    ss = jnp.concatenate([ss, jnp.zeros((n_ctx, LANES), F32)], axis=0)
    return cos, ss


def _norm_mod_fwd(x2d, g, sh, sc, name, tr, out_rows=None, base=None, out_off=0):
    n, d = x2d.shape

    def body(x_ref, g_ref, sh_ref, sc_ref, *rest):
        xh, _ = _rms(x_ref[...])
        rest[-1][...] = ((xh * g_ref[...]) * (1.0 + sc_ref[...]) + sh_ref[...]).astype(BF16)

    args, in_specs, aliases = [x2d, g, sh, sc], [_rows(tr, d), _bcast(d), _bcast(d), _bcast(d)], {}
    if base is not None:
        args.append(base)
        in_specs.append(ANY)
        aliases = {4: 0}
        out_rows = base.shape[0]
    return pl.pallas_call(
        body,
        name=name,
        grid=(n // tr,),
        in_specs=in_specs,
        out_specs=_rows(tr, d, out_off // tr),
        out_shape=jax.ShapeDtypeStruct((out_rows or n, d), BF16),
        input_output_aliases=aliases,
        compiler_params=_params(("parallel",)),
    )(*args)


def _norm_mod_bwd(dz, dz_off, x2d, g, sc, dres, name, tr):
    n, d = x2d.shape
    want_dx = dres is not None

    def body(*refs):
        if want_dx:
            dz_ref, x_ref, g_ref, sc_ref, dres_ref, dx_ref, dg_ref, dsh_ref, dsc_ref = refs
        else:
            dz_ref, x_ref, g_ref, sc_ref, dg_ref, dsh_ref, dsc_ref = refs
        _acc_init(pl.program_id(0), [dg_ref, dsh_ref, dsc_ref])
        xh, r = _rms(x_ref[...])
        dzv = dz_ref[...]
        gv = g_ref[...]
        dsc_ref[...] += _csum(dzv * (xh * gv))
        dsh_ref[...] += _csum(dzv)
        dh = dzv * (1.0 + sc_ref[...])
        dg_ref[...] += _csum(dh * xh)
        if want_dx:
            dx_ref[...] = _rms_bwd(xh, r, dh * gv) + dres_ref[...]

    in_specs = [_rows(tr, d, dz_off), _rows(tr, d), _bcast(d), _bcast(d)]
    args = [dz, x2d, g, sc]
    out_specs = [_bcast(d)] * 3
    out_shape = [jax.ShapeDtypeStruct((1, d), F32)] * 3
    if want_dx:
        in_specs.append(_rows(tr, d))
        args.append(dres)
        out_specs = [_rows(tr, d)] + out_specs
        out_shape = [jax.ShapeDtypeStruct((n, d), F32)] + out_shape
    res = pl.pallas_call(
        body,
        name=name,
        grid=(n // tr,),
        in_specs=in_specs,
        out_specs=out_specs,
        out_shape=out_shape,
        compiler_params=_params(("arbitrary",)),
    )(*args)
    return res if want_dx else (None, *res)


_QA, _QB = MLA_ROPE // 4, GQA_HEAD_DIM // 4


def _kprep_fwd(pkv, kvg, kg, cos_a, ss_a, cos_b, ss_b, tr):
    n = pkv.shape[0]
    nb = GQA_KV_HEADS * GQA_HEAD_DIM

    def body(p_ref, kvg_ref, kg_ref, ca, sa, cb, sb, ckv_ref, kb_ref, vb_ref, kpe_ref):
        p = p_ref[...]
        xh, _ = _rms(p[:, :MLA_KV_LORA])
        ckv_ref[...] = (xh * kvg_ref[...]).astype(BF16)
        for e in range(GQA_KV_HEADS):
            lo = MLA_KV_LORA + e * GQA_HEAD_DIM
            kh, _ = _rms(p[:, lo : lo + GQA_HEAD_DIM])
            kb_ref[:, e * GQA_HEAD_DIM : (e + 1) * GQA_HEAD_DIM] = _rope(kh * kg_ref[...], cb[...], sb[...], _QB).astype(BF16)
        vb_ref[...] = p[:, MLA_KV_LORA + nb : MLA_KV_LORA + 2 * nb].astype(BF16)
        kr = _rope(p[:, MLA_KV_LORA + 2 * nb :], ca[...], sa[...], _QA)
        kpe_ref[:, :LANES] = kr.astype(BF16)
        kpe_ref[:, LANES:] = pltpu.roll(kr, MLA_ROPE, 1).astype(BF16)

    return pl.pallas_call(
        body,
        name="kprep_fwd",
        grid=(n // tr,),
        in_specs=[_rows(tr, KVP), _bcast(MLA_KV_LORA), _bcast(GQA_HEAD_DIM)] + [_rows(tr, LANES)] * 4,
        out_specs=[_rows(tr, MLA_KV_LORA), _rows(tr, nb), _rows(tr, nb), _rows(tr, 2 * LANES)],
        out_shape=[jax.ShapeDtypeStruct((n, w), BF16) for w in (MLA_KV_LORA, nb, nb, 2 * LANES)],
        compiler_params=_params(("parallel",)),
    )(pkv, kvg, kg, cos_a, ss_a, cos_b, ss_b)


def _kprep_bwd(pkv, dckv, dkb, dvb, dkpe, kvg, kg, cos_b, ss_b, tr):
    n = pkv.shape[0]
    nb = GQA_KV_HEADS * GQA_HEAD_DIM

    def body(p_ref, dckv_ref, dkb_ref, dvb_ref, dkpe_ref, kvg_ref, kg_ref, cb, sb, dp_ref, dkvg_ref, dkg_ref):
        _acc_init(pl.program_id(0), [dkvg_ref, dkg_ref])
        p = p_ref[...]
        xh, r = _rms(p[:, :MLA_KV_LORA])
        dn = dckv_ref[...]
        dkvg_ref[...] += _csum(dn * xh)
        dp_ref[:, :MLA_KV_LORA] = _rms_bwd(xh, r, dn * kvg_ref[...]).astype(BF16)
        for e in range(GQA_KV_HEADS):
            lo = MLA_KV_LORA + e * GQA_HEAD_DIM
            kh, rk = _rms(p[:, lo : lo + GQA_HEAD_DIM])
            dk = _rope_t(dkb_ref[:, e * GQA_HEAD_DIM : (e + 1) * GQA_HEAD_DIM], cb[...], sb[...], _QB)
            dkg_ref[...] += _csum(dk * kh)
            dp_ref[:, lo : lo + GQA_HEAD_DIM] = _rms_bwd(kh, rk, dk * kg_ref[...]).astype(BF16)
        dp_ref[:, MLA_KV_LORA + nb : MLA_KV_LORA + 2 * nb] = dvb_ref[...].astype(BF16)
        dp_ref[:, MLA_KV_LORA + 2 * nb :] = dkpe_ref[...].astype(BF16)

    return pl.pallas_call(
        body,
        name="kprep_bwd",
        grid=(n // tr,),
        in_specs=[_rows(tr, KVP), _rows(tr, MLA_KV_LORA), _rows(tr, nb), _rows(tr, nb), _rows(tr, LANES),
                  _bcast(MLA_KV_LORA), _bcast(GQA_HEAD_DIM), _rows(tr, LANES), _rows(tr, LANES)],
        out_specs=[_rows(tr, KVP), _bcast(MLA_KV_LORA), _bcast(GQA_HEAD_DIM)],
        out_shape=[jax.ShapeDtypeStruct((n, KVP), BF16), jax.ShapeDtypeStruct((1, MLA_KV_LORA), F32),
                   jax.ShapeDtypeStruct((1, GQA_HEAD_DIM), F32)],
        compiler_params=_params(("arbitrary",)),
    )(pkv, dckv, dkb, dvb, dkpe, kvg, kg, cos_b, ss_b)


def _kgrad_split(dka, dva, cos_a, ss_a, tr):
    n = dka.shape[0]
    wk = MLA_HEADS * 2 * LANES

    def body(dk_ref, dv_ref, ca, sa, dkv_ref, dkpe_ref):
        even = jnp.zeros((tr, LANES), F32)
        odd = jnp.zeros((tr, LANES), F32)
        for h in range(MLA_HEADS):
            dkv_ref[:, 2 * h * LANES : (2 * h + 1) * LANES] = dk_ref[:, 2 * h * LANES : (2 * h + 1) * LANES].astype(BF16)
            dkv_ref[:, (2 * h + 1) * LANES : (2 * h + 2) * LANES] = dv_ref[:, h * MLA_V : (h + 1) * MLA_V].astype(BF16)
            part = dk_ref[:, (2 * h + 1) * LANES : (2 * h + 2) * LANES]
            if h % 2 == 0:
                even = even + part
            else:
                odd = odd + part
        lane = lax.broadcasted_iota(jnp.int32, (tr, LANES), 1)
        low = lane < MLA_ROPE
        both = jnp.where(low, even, odd)
        tot = jnp.where(low, both + pltpu.roll(both, MLA_ROPE, 1), 0.0)
        dkpe_ref[...] = _rope_t(tot, ca[...], sa[...], _QA)

    return pl.pallas_call(
        body,
        name="kgrad_split",
        grid=(n // tr,),
        in_specs=[_rows(tr, wk), _rows(tr, MLA_HEADS * MLA_V), _rows(tr, LANES), _rows(tr, LANES)],
        out_specs=[_rows(tr, wk), _rows(tr, LANES)],
        out_shape=[jax.ShapeDtypeStruct((n, wk), BF16), jax.ShapeDtypeStruct((n, LANES), F32)],
        compiler_params=_params(("parallel",)),
    )(dka, dva, cos_a, ss_a)


def _qprep_fwd(pq, qg, gq, cos_b, ss_b, tr):
    n = pq.shape[0]
    nq = GQA_HEADS * GQA_HEAD_DIM

    def body(p_ref, qg_ref, gq_ref, cb, sb, cq_ref, qb_ref):
        xh, _ = _rms(p_ref[:, :MLA_Q_LORA])
        cq_ref[...] = (xh * qg_ref[...]).astype(BF16)
        for h in range(GQA_HEADS):
            lo = MLA_Q_LORA + h * GQA_HEAD_DIM
            qh, _ = _rms(p_ref[:, lo : lo + GQA_HEAD_DIM])
            qb_ref[:, h * GQA_HEAD_DIM : (h + 1) * GQA_HEAD_DIM] = _rope(qh * gq_ref[...], cb[...], sb[...], _QB).astype(BF16)

    return pl.pallas_call(
        body,
        name="qprep_fwd",
        grid=(n // tr,),
        in_specs=[_rows(tr, QC), _bcast(MLA_Q_LORA), _bcast(GQA_HEAD_DIM), _rows(tr, LANES), _rows(tr, LANES)],
        out_specs=[_rows(tr, MLA_Q_LORA), _rows(tr, nq)],
        out_shape=[jax.ShapeDtypeStruct((n, MLA_Q_LORA), BF16), jax.ShapeDtypeStruct((n, nq), BF16)],
        compiler_params=_params(("parallel",)),
    )(pq, qg, gq, cos_b, ss_b)


def _qprep_bwd(pq, dcq, dqb, qg, gq, cos_b, ss_b, tr):
    n = pq.shape[0]
    nq = GQA_HEADS * GQA_HEAD_DIM

    def body(p_ref, dcq_ref, dqb_ref, qg_ref, gq_ref, cb, sb, dp_ref, dqg_ref, dgq_ref):
        _acc_init(pl.program_id(0), [dqg_ref, dgq_ref])
        xh, r = _rms(p_ref[:, :MLA_Q_LORA])
        dn = dcq_ref[...]
        dqg_ref[...] += _csum(dn * xh)
        dp_ref[:, :MLA_Q_LORA] = _rms_bwd(xh, r, dn * qg_ref[...]).astype(BF16)
        for h in range(GQA_HEADS):
            lo = MLA_Q_LORA + h * GQA_HEAD_DIM
            qh, rq = _rms(p_ref[:, lo : lo + GQA_HEAD_DIM])
            dq = _rope_t(dqb_ref[:, h * GQA_HEAD_DIM : (h + 1) * GQA_HEAD_DIM], cb[...], sb[...], _QB)
            dgq_ref[...] += _csum(dq * qh)
            dp_ref[:, lo : lo + GQA_HEAD_DIM] = _rms_bwd(qh, rq, dq * gq_ref[...]).astype(BF16)

    return pl.pallas_call(
        body,
        name="qprep_bwd",
        grid=(n // tr,),
        in_specs=[_rows(tr, QC), _rows(tr, MLA_Q_LORA), _rows(tr, nq), _bcast(MLA_Q_LORA), _bcast(GQA_HEAD_DIM),
                  _rows(tr, LANES), _rows(tr, LANES)],
        out_specs=[_rows(tr, QC), _bcast(MLA_Q_LORA), _bcast(GQA_HEAD_DIM)],
        out_shape=[jax.ShapeDtypeStruct((n, QC), BF16), jax.ShapeDtypeStruct((1, MLA_Q_LORA), F32),
                   jax.ShapeDtypeStruct((1, GQA_HEAD_DIM), F32)],
        compiler_params=_params(("arbitrary",)),
    )(pq, dcq, dqb, qg, gq, cos_b, ss_b)


_QA_COLS = MLA_HEADS * (MLA_NOPE + MLA_ROPE)


def _qrope_fwd(qa, cos_a, ss_a, tr):
    n = qa.shape[0]

    def body(q_ref, ca, sa, o_ref):
        for j in range(MLA_HEADS // 2):
            lo = 3 * j * LANES
            o_ref[:, lo : lo + 2 * LANES] = q_ref[:, lo : lo + 2 * LANES].astype(BF16)
            o_ref[:, lo + 2 * LANES : lo + 3 * LANES] = _rope(q_ref[:, lo + 2 * LANES : lo + 3 * LANES], ca[...], sa[...], _QA).astype(BF16)

    return pl.pallas_call(
        body,
        name="qrope_fwd",
        grid=(n // tr,),
        in_specs=[_rows(tr, _QA_COLS), _rows(tr, LANES), _rows(tr, LANES)],
        out_specs=_rows(tr, _QA_COLS),
        out_shape=jax.ShapeDtypeStruct((n, _QA_COLS), BF16),
        compiler_params=_params(("parallel",)),
    )(qa, cos_a, ss_a)


def _qrope_bwd(dq2, cos_a, ss_a, tr):
    n = dq2.shape[0]

    def body(d_ref, ca, sa, o_ref):
        for j in range(MLA_HEADS // 2):
            lo = 3 * j * LANES
            h0, h1 = 2 * j, 2 * j + 1
            o_ref[:, lo : lo + LANES] = d_ref[:, 2 * h0 * LANES : (2 * h0 + 1) * LANES].astype(BF16)
            o_ref[:, lo + LANES : lo + 2 * LANES] = d_ref[:, 2 * h1 * LANES : (2 * h1 + 1) * LANES].astype(BF16)
            pe = d_ref[:, (2 * h0 + 1) * LANES : (2 * h0 + 2) * LANES] + d_ref[:, (2 * h1 + 1) * LANES : (2 * h1 + 2) * LANES]
            o_ref[:, lo + 2 * LANES : lo + 3 * LANES] = _rope_t(pe, ca[...], sa[...], _QA).astype(BF16)

    return pl.pallas_call(
        body,
        name="qrope_bwd",
        grid=(n // tr,),
        in_specs=[_rows(tr, MLA_HEADS * 2 * LANES), _rows(tr, LANES), _rows(tr, LANES)],
        out_specs=_rows(tr, _QA_COLS),
        out_shape=jax.ShapeDtypeStruct((n, _QA_COLS), BF16),
        compiler_params=_params(("parallel",)),
    )(dq2, cos_a, ss_a)


def _cat(refs):
    vals = [r[...] for r in refs]
    return vals[0] if len(vals) == 1 else jnp.concatenate(vals, axis=-1)


LOG2E = 1.4426950408889634


def _attn_fwd(qparts, kparts, vpart, n_heads, group, dv, scale, name, tq, after=None):
    T, Tk = qparts[0][0].shape[0], kparts[0][0].shape[0]
    nq_, nk_ = len(qparts), len(kparts)
    sub = min(tq, 256)
    c2 = scale * LOG2E

    def body(*refs):
        q_refs, k_refs = refs[:nq_], refs[nq_ : nq_ + nk_]
        v_ref = refs[nq_ + nk_]
        o_ref, lse_ref = refs[-2:]
        k = _cat(k_refs)
        v = v_ref[...]
        for r0 in range(0, tq, sub):
            q = _cat([r.at[r0 : r0 + sub, :] for r in q_refs])
            s = lax.dot_general(q, k, _DIMS["NT"], preferred_element_type=F32)
            m = jnp.max(s, axis=-1, keepdims=True)
            p = jnp.exp2((s - m) * c2)
            l = jnp.sum(p, axis=-1, keepdims=True)
            acc = jnp.dot(p.astype(BF16), v, preferred_element_type=F32)
            o_ref[r0 : r0 + sub, :] = (acc * (1.0 / l)).astype(BF16)
            lse_ref[r0 : r0 + sub, :] = m * scale + jnp.log(l)

    in_specs = [pl.BlockSpec((tq, LANES), lambda h, i, f=f: (i, f(h))) for _, f in qparts]
    in_specs += [pl.BlockSpec((Tk, LANES), lambda h, i, f=f: (0, f(h // group))) for _, f in kparts]
    fv = vpart[1]
    in_specs.append(pl.BlockSpec((Tk, dv), lambda h, i: (0, fv(h // group))))
    args = [*[a for a, _ in qparts], *[a for a, _ in kparts], vpart[0]]
    if after is not None:
        in_specs.append(pl.BlockSpec(after.shape, lambda h, i: (0, 0)))
        args.append(after)
    return pl.pallas_call(
        body,
        name=name,
        grid=(n_heads, T // tq),
        in_specs=in_specs,
        out_specs=[pl.BlockSpec((tq, dv), lambda h, i: (i, h)), pl.BlockSpec((None, tq, 1), lambda h, i: (h, i, 0))],
        out_shape=[jax.ShapeDtypeStruct((T, n_heads * dv), BF16), jax.ShapeDtypeStruct((n_heads, T, 1), F32)],
        compiler_params=_params(("parallel", "parallel")),
    )(*args)


def _attn_bwd(qparts, kparts, vpart, o, do, lse, n_heads, group, dv, scale, name, tq):
    T, Tk = qparts[0][0].shape[0], kparts[0][0].shape[0]
    nq_, nk_ = len(qparts), len(kparts)
    dk_ = LANES * nq_
    n_kv = n_heads // group
    nblk = T // tq
    c2 = scale * LOG2E

    def head(hk, i):
        return hk * group + i // nblk

    sub = min(tq, 256)

    def body(*refs):
        q_refs = refs[:nq_]
        k = _cat(refs[nq_ : nq_ + nk_])
        v_ref, o_ref, do_ref, lse_ref, dq_ref, dk_ref, dv_ref = refs[nq_ + nk_ :]
        i = pl.program_id(1)
        _acc_init(i, [dk_ref, dv_ref])
        v = v_ref[...]
        dk_acc, dv_acc = None, None
        for r0 in range(0, tq, sub):
            rows = slice(r0, r0 + sub)
            q = _cat([r.at[rows, :] for r in q_refs])
            s = lax.dot_general(q, k, _DIMS["NT"], preferred_element_type=F32)
            p = jnp.exp2(s * c2 - lse_ref[rows, :] * LOG2E)
            dov = do_ref[rows, :]
            dp = lax.dot_general(dov, v, _DIMS["NT"], preferred_element_type=F32)
            delta = jnp.sum(dov.astype(F32) * o_ref[rows, :].astype(F32), axis=-1, keepdims=True)
            ds = (p * (dp - delta)).astype(BF16)
            dq_ref[rows, :] = jnp.dot(ds, k, preferred_element_type=F32) * scale
            dk_part = lax.dot_general(ds, q, _DIMS["TN"], preferred_element_type=F32)
            dv_part = lax.dot_general(p.astype(BF16), dov, _DIMS["TN"], preferred_element_type=F32)
            dk_acc = dk_part if dk_acc is None else dk_acc + dk_part
            dv_acc = dv_part if dv_acc is None else dv_acc + dv_part
        dk_ref[...] += dk_acc
        dv_ref[...] += dv_acc

        @pl.when(i == group * nblk - 1)
        def _():
            dk_ref[...] *= scale

    in_specs = [pl.BlockSpec((tq, LANES), lambda hk, i, f=f: (i % nblk, f(head(hk, i)))) for _, f in qparts]
    in_specs += [pl.BlockSpec((Tk, LANES), lambda hk, i, f=f: (0, f(hk))) for _, f in kparts]
    fv = vpart[1]
    in_specs.append(pl.BlockSpec((Tk, dv), lambda hk, i: (0, fv(hk))))
    in_specs += [pl.BlockSpec((tq, dv), lambda hk, i: (i % nblk, head(hk, i)))] * 2
    in_specs.append(pl.BlockSpec((None, tq, 1), lambda hk, i: (head(hk, i), i % nblk, 0)))
    return pl.pallas_call(
        body,
        name=name,
        grid=(n_kv, group * nblk),
        in_specs=in_specs,
        out_specs=[pl.BlockSpec((tq, dk_), lambda hk, i: (i % nblk, head(hk, i))),
                   pl.BlockSpec((Tk, dk_), lambda hk, i: (0, hk)),
                   pl.BlockSpec((Tk, dv), lambda hk, i: (0, hk))],
        out_shape=[jax.ShapeDtypeStruct((T, n_heads * dk_), F32), jax.ShapeDtypeStruct((Tk, n_kv * dk_), F32),
                   jax.ShapeDtypeStruct((Tk, n_kv * dv), F32)],
        compiler_params=_params(("parallel", "arbitrary")),
    )(*[a for a, _ in qparts], *[a for a, _ in kparts], vpart[0], o, do, lse)


def _gates_fwd(pg, ya, yb, tr):
    n, d = ya.shape

    def body(pg_ref, ya_ref, yb_ref, o_ref):
        ga = jax.nn.sigmoid(pg_ref[:, :d].astype(F32))
        gb = jax.nn.sigmoid(pg_ref[:, d:].astype(F32))
        o_ref[...] = (ga * ya_ref[...].astype(F32) + gb * yb_ref[...].astype(F32)).astype(BF16)

    return pl.pallas_call(
        body,
        name="gates_fwd",
        grid=(n // tr,),
        in_specs=[_rows(tr, 2 * d), _rows(tr, d), _rows(tr, d)],
        out_specs=_rows(tr, d),
        out_shape=jax.ShapeDtypeStruct((n, d), BF16),
        compiler_params=_params(("parallel",)),
    )(pg, ya, yb)


def _gates_bwd(dm, pg, ya, yb, tr):
    n, d = ya.shape

    def body(dm_ref, pg_ref, ya_ref, yb_ref, dya_ref, dyb_ref, dpg_ref):
        dmv = dm_ref[...].astype(F32)
        ga = jax.nn.sigmoid(pg_ref[:, :d].astype(F32))
        gb = jax.nn.sigmoid(pg_ref[:, d:].astype(F32))
        dya_ref[...] = (dmv * ga).astype(BF16)
        dyb_ref[...] = (dmv * gb).astype(BF16)
        dpg_ref[:, :d] = (dmv * ya_ref[...].astype(F32) * ga * (1.0 - ga)).astype(BF16)
        dpg_ref[:, d:] = (dmv * yb_ref[...].astype(F32) * gb * (1.0 - gb)).astype(BF16)

    return pl.pallas_call(
        body,
        name="gates_bwd",
        grid=(n // tr,),
        in_specs=[_rows(tr, d), _rows(tr, 2 * d), _rows(tr, d), _rows(tr, d)],
        out_specs=[_rows(tr, d), _rows(tr, d), _rows(tr, 2 * d)],
        out_shape=[jax.ShapeDtypeStruct((n, d), BF16), jax.ShapeDtypeStruct((n, d), BF16), jax.ShapeDtypeStruct((n, 2 * d), BF16)],
        compiler_params=_params(("parallel",)),
    )(dm, pg, ya, yb)


def _resid_norm2_fwd(x2d, att, g1, n2g, sh2, sc2, tr):
    n, d = x2d.shape

    def body(x_ref, a_ref, g1_ref, g_ref, sh_ref, sc_ref, x1_ref, z_ref):
        x1 = x_ref[...] + g1_ref[...] * a_ref[...]
        x1_ref[...] = x1
        xh, _ = _rms(x1)
        z_ref[...] = ((xh * g_ref[...]) * (1.0 + sc_ref[...]) + sh_ref[...]).astype(BF16)

    return pl.pallas_call(
        body,
        name="resid_norm2_fwd",
        grid=(n // tr,),
        in_specs=[_rows(tr, d), _rows(tr, d)] + [_bcast(d)] * 4,
        out_specs=[_rows(tr, d), _rows(tr, d)],
        out_shape=[jax.ShapeDtypeStruct((n, d), F32), jax.ShapeDtypeStruct((n, d), BF16)],
        compiler_params=_params(("parallel",)),
    )(x2d, att, g1, n2g, sh2, sc2)


def _resid_norm2_bwd(dz2, x1, dx2, att, n2g, sc2, g1, tr):
    n, d = x1.shape

    def body(dz_ref, x1_ref, dx2_ref, a_ref, g_ref, sc_ref, g1_ref, dx1_ref, da_ref, dg_ref, dsh_ref, dsc_ref, dg1_ref):
        _acc_init(pl.program_id(0), [dg_ref, dsh_ref, dsc_ref, dg1_ref])
        xh, r = _rms(x1_ref[...])
        dzv = dz_ref[...]
        gv = g_ref[...]
        dsc_ref[...] += _csum(dzv * (xh * gv))
        dsh_ref[...] += _csum(dzv)
        dh = dzv * (1.0 + sc_ref[...])
        dg_ref[...] += _csum(dh * xh)
        dx1 = _rms_bwd(xh, r, dh * gv) + dx2_ref[...]
        dx1_ref[...] = dx1
        dg1_ref[...] += _csum(dx1 * a_ref[...])
        da_ref[...] = (dx1 * g1_ref[...]).astype(BF16)

    return pl.pallas_call(
        body,
        name="resid_norm2_bwd",
        grid=(n // tr,),
        in_specs=[_rows(tr, d)] * 4 + [_bcast(d)] * 3,
        out_specs=[_rows(tr, d), _rows(tr, d)] + [_bcast(d)] * 4,
        out_shape=[jax.ShapeDtypeStruct((n, d), F32), jax.ShapeDtypeStruct((n, d), BF16)] + [jax.ShapeDtypeStruct((1, d), F32)] * 4,
        compiler_params=_params(("arbitrary",)),
    )(dz2, x1, dx2, att, n2g, sc2, g1)


def _edges(shape):
    row = lax.broadcasted_iota(jnp.int32, shape, 0)
    return row == 0, row == shape[0] - 1


def _shifts(u, edges):
    n = u.shape[0]
    return jnp.where(edges[0], 0.0, pltpu.roll(u, 1, 0)), jnp.where(edges[1], 0.0, pltpu.roll(u, n - 1, 0))


def _conv3(u, prev, nxt, w_ref, b_ref):
    return b_ref[...] + w_ref[0:1, :] * prev + w_ref[1:2, :] * u + w_ref[2:3, :] * nxt


def _ffn_up_conv(z, wup_t, cw, cb, tc, after):
    n, d = z.shape
    f = wup_t.shape[0] // 2
    nb = f // tc

    def body(z_ref, wa_ref, wb_ref, cwa, cwb, cba, cbb, after_ref, ua_ref, ub_ref, h_ref):
        w = jnp.concatenate([wa_ref[...], wb_ref[...]], axis=0)
        u = lax.dot_general(z_ref[...], w, _DIMS["NT"], preferred_element_type=F32).astype(BF16)
        ua_ref[...] = u[:, :tc]
        ub_ref[...] = u[:, tc:]
        edges = _edges((n, tc))
        ua = u[:, :tc].astype(F32)
        ub = u[:, tc:].astype(F32)
        a = _conv3(ua, *_shifts(ua, edges), cwa, cba)
        b = _conv3(ub, *_shifts(ub, edges), cwb, cbb)
        h_ref[...] = (a * jax.nn.sigmoid(a) * b).astype(BF16)

    col = lambda rows, off: pl.BlockSpec((rows, tc), lambda i: (0, i + off))
    w_rows = lambda off: pl.BlockSpec((tc, d), lambda i: (i + off, 0))
    return pl.pallas_call(
        body,
        name="ffn_up_conv",
        grid=(nb,),
        in_specs=[pl.BlockSpec((n, d), lambda i: (0, 0)), w_rows(0), w_rows(nb), col(3, 0), col(3, nb), col(1, 0), col(1, nb),
                  pl.BlockSpec(after.shape, lambda i: (0, 0))],
        out_specs=[col(n, 0)] * 3,
        out_shape=[jax.ShapeDtypeStruct((n, f), BF16)] * 3,
        compiler_params=_params(("parallel",)),
    )(z, wup_t, wup_t, cw, cw, cb, cb, after)


def _ffn_down_dx_conv_bwd(df, wdown, u_a, u_b, cw, cb, tc, after):
    n, f = u_a.shape
    d = df.shape[1]
    nb = f // tc

    def part(uv, prev, nxt, duc, edges, w_ref, du_ref, dw_ref, db_ref):
        db_ref[...] = _csum(duc)
        dw_ref[0:1, :] = _csum(duc * prev)
        dw_ref[1:2, :] = _csum(duc * uv)
        dw_ref[2:3, :] = _csum(duc * nxt)
        d_prev, d_next = _shifts(duc, edges)
        du_ref[...] = (w_ref[0:1, :] * d_next + w_ref[1:2, :] * duc + w_ref[2:3, :] * d_prev).astype(BF16)

    def body(df_ref, wd_ref, ua_ref, ub_ref, wa_ref, wb_ref, ba_ref, bb_ref, after_ref,
             dua_ref, dub_ref, dwa_ref, dwb_ref, dba_ref, dbb_ref):
        dhv = lax.dot_general(df_ref[...], wd_ref[...], _DIMS["NT"], preferred_element_type=F32)
        dhv = dhv.astype(BF16).astype(F32)
        edges = _edges((n, tc))
        ua = ua_ref[...].astype(F32)
        ub = ub_ref[...].astype(F32)
        sa = _shifts(ua, edges)
        sb = _shifts(ub, edges)
        a = _conv3(ua, *sa, wa_ref, ba_ref)
        b = _conv3(ub, *sb, wb_ref, bb_ref)
        sg = jax.nn.sigmoid(a)
        da = dhv * b * (sg * (1.0 + a * (1.0 - sg)))
        db = dhv * (a * sg)
        part(ua, *sa, da, edges, wa_ref, dua_ref, dwa_ref, dba_ref)
        part(ub, *sb, db, edges, wb_ref, dub_ref, dwb_ref, dbb_ref)

    col = lambda rows, off: pl.BlockSpec((rows, tc), lambda i: (0, i + off))
    return pl.pallas_call(
        body,
        name="ffn_down_dx_conv_bwd",
        grid=(nb,),
        in_specs=[pl.BlockSpec((n, d), lambda i: (0, 0)), pl.BlockSpec((tc, d), lambda i: (i, 0)), col(n, 0), col(n, 0),
                  col(3, 0), col(3, nb), col(1, 0), col(1, nb), pl.BlockSpec(after.shape, lambda i: (0, 0))],
        out_specs=[col(n, 0), col(n, 0), col(3, 0), col(3, 0), col(1, 0), col(1, 0)],
        out_shape=[jax.ShapeDtypeStruct((n, f), BF16)] * 2 + [jax.ShapeDtypeStruct((3, f), F32)] * 2 + [jax.ShapeDtypeStruct((1, f), F32)] * 2,
        compiler_params=_params(("parallel",)),
    )(df, wdown, u_a, u_b, cw, cw, cb, cb, after)


def _loss_head(x1, f, g2, fg, tgt, tr):
    n, d = x1.shape

    def body(x1_ref, f_ref, g2_ref, fg_ref, t_ref, sq_ref, dx2_ref, dfg_ref, dg2_ref, df_ref):
        _acc_init(pl.program_id(0), [sq_ref, dfg_ref, dg2_ref])
        fv = f_ref[...]
        xh, r = _rms(x1_ref[...] + g2_ref[...] * fv)
        err = xh * fg_ref[...] - t_ref[...]
        sq_ref[...] += _csum(err * err)
        dy = err * (1.0 / d)
        dfg_ref[...] += _csum(dy * xh)
        dx2 = _rms_bwd(xh, r, dy * fg_ref[...])
        dx2_ref[...] = dx2
        dg2_ref[...] += _csum(dx2 * fv)
        df_ref[...] = (dx2 * g2_ref[...]).astype(BF16)

    return pl.pallas_call(
        body,
        name="loss_head",
        grid=(n // tr,),
        in_specs=[_rows(tr, d), _rows(tr, d), _bcast(d), _bcast(d), _rows(tr, d)],
        out_specs=[_bcast(d), _rows(tr, d), _bcast(d), _bcast(d), _rows(tr, d)],
        out_shape=[jax.ShapeDtypeStruct((1, d), F32), jax.ShapeDtypeStruct((n, d), F32), jax.ShapeDtypeStruct((1, d), F32),
                   jax.ShapeDtypeStruct((1, d), F32), jax.ShapeDtypeStruct((n, d), BF16)],
        compiler_params=_params(("arbitrary",)),
    )(x1, f, g2, fg, tgt)


def _sum_slots(g, name):
    s, r, w = g.shape

    def body(g_ref, o_ref):
        acc = g_ref[0]
        for k in range(1, s):
            acc = acc + g_ref[k]
        o_ref[...] = acc

    return pl.pallas_call(body, name=name, out_shape=jax.ShapeDtypeStruct((r, w), F32))(g)


def _silu_grad_mul(ds, cvec):
    def body(d_ref, c_ref, o_ref):
        cv = c_ref[...]
        sg = jax.nn.sigmoid(cv)
        o_ref[...] = d_ref[...] * (sg * (1.0 + cv * (1.0 - sg)))

    return pl.pallas_call(body, name="silu_grad_mul", out_shape=jax.ShapeDtypeStruct(ds.shape, F32))(ds, cvec)


def _adamw_update(wv, gv, mv, vv, d_ref, mo_ref, vo_ref):
    mn = ADAM_B1 * mv + (1.0 - ADAM_B1) * gv
    vn = ADAM_B2 * vv + (1.0 - ADAM_B2) * (gv * gv)
    mo_ref[...] = mn
    vo_ref[...] = vn
    m_hat = mn / (1.0 - ADAM_B1**ADAM_STEP)
    v_hat = vn / (1.0 - ADAM_B2**ADAM_STEP)
    d_ref[...] = -ADAM_LR * (m_hat / (jnp.sqrt(v_hat) + ADAM_EPS) + ADAM_WD * wv)


def _adamw_many(ws, gs, ms, vs, name):
    n = len(ws)

    def body(*refs):
        for k in range(n):
            w_ref, g_ref, m_ref, v_ref = (refs[q * n + k] for q in range(4))
            d_ref, mo_ref, vo_ref = (refs[(4 + q) * n + k] for q in range(3))
            _adamw_update(w_ref[...], g_ref[...], m_ref[...], v_ref[...], d_ref, mo_ref, vo_ref)

    res = pl.pallas_call(body, name=name, out_shape=[jax.ShapeDtypeStruct(w.shape, F32) for w in ws] * 3)(*ws, *gs, *ms, *vs)
    return res[:n], res[n : 2 * n], res[2 * n :]


def _adamw(w, g, m, v, name, g_transposed=False, g_sibling=None):
    r, cdim = w.shape
    halves = g_sibling is not None
    block = 1 << 19
    if g_transposed:
        tc = _pick(cdim // 2 if halves else cdim, 2048)
        tr = _pick(r, max(LANES, block // tc), LANES)
        per_half = (cdim // 2) // tc
    else:
        rows = r // 2 if halves else r
        tc = _pick(cdim, 2048)
        tr = _pick(rows, max(8, block // tc), 8)
        if tr < 64 and rows > 64:
            tr, tc = _pick(rows, 1024, 8), _pick(cdim, 512)
        per_half = (r // 2) // tr
    emit_g = g_transposed or halves

    def body(w_ref, g_ref, *rest):
        m_ref, v_ref = rest[halves : halves + 2]
        outs = rest[halves + 2 :]
        gv = g_ref[...]
        if halves:
            along = pl.program_id(1 if g_transposed else 0)
            gv = jnp.where(along // per_half == lax.axis_index("c"), gv, rest[0][...])
        if g_transposed:
            gv = gv.T
        if emit_g:
            outs[0][...] = gv
        _adamw_update(w_ref[...], gv, m_ref[...], v_ref[...], *outs[-3:])

    spec = pl.BlockSpec((tr, tc), lambda i, j: (i, j))
    if g_transposed:
        g_spec = pl.BlockSpec((tc, tr), lambda i, j: (j % per_half if halves else j, i))
    else:
        g_spec = pl.BlockSpec((tr, tc), lambda i, j: (i % per_half if halves else i, j))
    n_out = 3 + emit_g
    res = pl.pallas_call(
        body,
        name=name,
        grid=(r // tr, cdim // tc),
        in_specs=[spec, g_spec] + [g_spec] * halves + [spec, spec],
        out_specs=[spec] * n_out,
        out_shape=[jax.ShapeDtypeStruct((r, cdim), F32)] * n_out,
        compiler_params=_params(("parallel", "parallel")),
    )(w, g, *([g_sibling] if halves else []), m, v)
    return res if emit_g else [g, *res]


def _place():
    return lax.axis_index("x"), lax.axis_index("y"), lax.axis_index("c")


def _remote(src, dst, send_sem, recv_sem, dev):
    return pltpu.make_async_remote_copy(src_ref=src, dst_ref=dst, send_sem=send_sem, recv_sem=recv_sem, device_id=dev, device_id_type=MESH)


ANY = pl.BlockSpec(memory_space=pl.ANY)


def _all_gather_small(v, name):
    r, w = v.shape

    def body(v_ref, o_ref, send, recv, lsem):
        x, y, c = _place()
        me = 4 * x + 2 * y + c
        mine = pltpu.make_async_copy(v_ref, o_ref.at[me], lsem)
        mine.start()
        sent = []
        for k in range(1, 8):
            px, py, pc = x ^ (k >> 2), y ^ ((k >> 1) & 1), c ^ (k & 1)
            cp = _remote(v_ref, o_ref.at[me], send.at[k - 1], recv.at[k - 1], (px, py, pc))
            cp.start()
            sent.append(cp)
        for k in range(1, 8):
            px, py, pc = x ^ (k >> 2), y ^ ((k >> 1) & 1), c ^ (k & 1)
            slot = o_ref.at[4 * px + 2 * py + pc]
            _remote(slot, slot, send.at[k - 1], recv.at[k - 1], (x, y, c)).wait_recv()
        for cp in sent:
            cp.wait_send()
        mine.wait()

    return pl.pallas_call(
        body,
        name=name,
        out_shape=jax.ShapeDtypeStruct((8, r, w), F32),
        in_specs=[pl.BlockSpec(memory_space=pltpu.VMEM)],
        out_specs=pl.BlockSpec(memory_space=pltpu.VMEM),
        scratch_shapes=[pltpu.SemaphoreType.DMA((7,)), pltpu.SemaphoreType.DMA((7,)), pltpu.SemaphoreType.DMA],
        compiler_params=pltpu.CompilerParams(vmem_limit_bytes=VMEM_LIMIT),
    )(v)


HBM = pl.BlockSpec(memory_space=pltpu.HBM)
SEM = pl.BlockSpec(memory_space=pltpu.SEMAPHORE)
EFFECT = pltpu.SideEffectType.DATAFLOW_SIDE_EFFECTING


def _other_chips(x, y):
    return [(1 - x, y), (x, 1 - y), (1 - x, 1 - y)]


def _bulk_start(name, srcs, land_shapes, n_copies, copies, after):
    n, m = len(srcs), len(land_shapes)

    def body(*refs):
        src_refs, land_refs = refs[:n], refs[n : n + m]
        send, recv = refs[n + m + 1], refs[n + m + 2]
        token = refs[-1]
        for k, (s, d, dev) in enumerate(copies(src_refs, land_refs)):
            _remote(s, d, send.at[k], recv.at[k], dev).start()
        token[...] = jnp.zeros_like(token)

    lands = [pltpu.with_memory_space_constraint(lax.empty(s.shape, s.dtype), pltpu.HBM) for s in land_shapes]
    out = pl.pallas_call(
        body,
        name=name,
        out_shape=(pltpu.SemaphoreType.DMA((n_copies,)), pltpu.SemaphoreType.DMA((n_copies,)),
                   *[pltpu.HBM(s.shape, s.dtype) for s in srcs], *[pltpu.HBM(s.shape, s.dtype) for s in land_shapes],
                   jax.ShapeDtypeStruct((8, LANES), F32)),
        in_specs=[HBM] * (n + m) + [ANY],
        out_specs=(SEM, SEM, *[HBM] * (n + m), pl.BlockSpec(memory_space=pltpu.VMEM)),
        input_output_aliases={i: 2 + i for i in range(n + m)},
        compiler_params=pltpu.CompilerParams(has_side_effects=EFFECT),
    )(*[pltpu.with_memory_space_constraint(s, pltpu.HBM) for s in srcs], *lands, after)
    return out[0], out[1], list(out[2 : 2 + n]), list(out[2 + n : 2 + n + m]), out[-1][0:1, 0:1]


def _bulk_wait(name, send, recv, srcs, lands, after, waits):
    n, m = len(srcs), len(lands)

    def body(*refs):
        src_refs, land_refs = refs[:n], refs[n : n + m]
        send_sem, recv_sem = refs[n + m], refs[n + m + 1]
        x, y, c = _place()
        for k, (s, d) in enumerate(waits(src_refs, land_refs)):
            cp = _remote(s, d, send_sem.at[k], recv_sem.at[k], (x, y, c))
            cp.wait_send()
            cp.wait_recv()

    out = pl.pallas_call(
        body,
        name=name,
        out_shape=tuple(pltpu.HBM(s.shape, s.dtype) for s in (*srcs, *lands)),
        in_specs=[HBM] * (n + m) + [SEM, SEM, ANY],
        out_specs=tuple([HBM] * (n + m)),
        input_output_aliases={i: i for i in range(n + m)},
        compiler_params=pltpu.CompilerParams(has_side_effects=EFFECT),
    )(*srcs, *lands, send, recv, after)
    return list(out[:n]), list(out[n:])


def _gather_start(shards, after, name):
    def copies(src, land):
        x, y, c = _place()
        j = 2 * x + y
        return [(src[a].at[c], land[a].at[j, c], (px, py, c)) for a in range(len(shards)) for px, py in _other_chips(x, y)]

    shapes = [jax.ShapeDtypeStruct((4,) + s.shape, s.dtype) for s in shards]
    return _bulk_start(name, shards, shapes, 3 * len(shards), copies, after)


def _gather_wait(started, after, name):
    send, recv, srcs, lands, _ = started

    def waits(src, land):
        x, y, c = _place()
        return [(src[a].at[c], land[a].at[2 * px + py, c]) for a in range(len(srcs)) for px, py in _other_chips(x, y)]

    return _bulk_wait(name, send, recv, srcs, lands, after, waits)


def _forward_halves(lands, name):
    n = len(lands)

    def body(*refs):
        bufs = refs[n : 2 * n]
        send, recv = refs[2 * n :]
        x, y, c = _place()
        started = []
        for a in range(n):
            for k, (px, py) in enumerate(_other_chips(x, y)):
                blk = bufs[a].at[2 * px + py, c]
                cp = _remote(blk, blk, send.at[3 * a + k], recv.at[3 * a + k], (x, y, 1 - c))
                cp.start()
                started.append(cp)
        for a in range(n):
            for k, (px, py) in enumerate(_other_chips(x, y)):
                blk = bufs[a].at[2 * px + py, 1 - c]
                _remote(blk, blk, send.at[3 * a + k], recv.at[3 * a + k], (x, y, c)).wait_recv()
        for cp in started:
            cp.wait_send()

    return pl.pallas_call(
        body,
        name=name,
        out_shape=[jax.ShapeDtypeStruct(b.shape, b.dtype) for b in lands],
        in_specs=[ANY] * n,
        out_specs=[ANY] * n,
        input_output_aliases={i: i for i in range(n)},
        scratch_shapes=[pltpu.SemaphoreType.DMA((3 * n,)), pltpu.SemaphoreType.DMA((3 * n,))],
    )(*lands)


def _forward_start(lands, after, name):
    def copies(src, _):
        x, y, c = _place()
        blocks = [src[a].at[2 * px + py, c] for a in range(len(lands)) for px, py in _other_chips(x, y)]
        return [(b, b, (x, y, 1 - c)) for b in blocks]

    return _bulk_start(name, lands, [], 3 * len(lands), copies, after)


def _forward_wait(started, after, name):
    send, recv, bufs, _, _ = started

    def waits(src, _):
        x, y, c = _place()
        return [(src[a].at[2 * px + py, c], src[a].at[2 * px + py, 1 - c]) for a in range(len(bufs)) for px, py in _other_chips(x, y)]

    return _bulk_wait(name, send, recv, bufs, [], after, waits)[0]


def _place_own(shards, lands):
    j = 2 * lax.axis_index("x") + lax.axis_index("y")
    full = [lax.dynamic_update_slice(b, s[None], (j, 0, 0, 0)) for b, s in zip(lands, shards)]
    return [f.reshape(4 * f.shape[2] * 2, f.shape[3]) for f in full]


def _gather_finish(started, after, tag):
    shards, lands = _gather_wait(started, after, "gather_wait_" + tag)
    return _place_own(shards, _forward_halves(lands, "gather_forward_" + tag))


def _gather_land(started, after, tag):
    shards, lands = _gather_wait(started, after, "gather_wait_" + tag)
    return shards, _forward_start(lands, shards[0], "forward_start_" + tag)


def _gather_done(landed, after, tag):
    shards, fwd = landed
    return _place_own(shards, _forward_wait(fwd, after, "forward_wait_" + tag))


def _swap_halves(grads, name):
    n = len(grads)

    def body(*refs):
        ins, outs = refs[:n], refs[n : 2 * n]
        send, recv = refs[2 * n :]
        x, y, c = _place()
        started = []
        for a in range(n):
            for s in range(4):
                cp = _remote(ins[a].at[s, 1 - c], outs[a].at[s], send.at[4 * a + s], recv.at[4 * a + s], (x, y, 1 - c))
                cp.start()
                started.append(cp)
        for cp in started:
            cp.wait_recv()
        for cp in started:
            cp.wait_send()

    return pl.pallas_call(
        body,
        name=name,
        out_shape=[jax.ShapeDtypeStruct((4,) + g.shape[2:], g.dtype) for g in grads],
        in_specs=[ANY] * n,
        out_specs=[ANY] * n,
        scratch_shapes=[pltpu.SemaphoreType.DMA((4 * n,)), pltpu.SemaphoreType.DMA((4 * n,))],
    )(*grads)


def _add_halves(grads, others, tag):
    outs = []
    for a, (g, o) in enumerate(zip(grads, others)):
        _, _, rh, cdim = g.shape
        tr = _pick(rh, 512, 16)

        def body(g_ref, o_ref, p_ref):
            p_ref[...] = (g_ref[...].astype(F32) + o_ref[...].astype(F32)).astype(BF16)

        outs.append(
            pl.pallas_call(
                body,
                name=f"add_halves_{tag}{a}",
                grid=(4, rh // tr),
                in_specs=[pl.BlockSpec((None, None, tr, cdim), lambda s, i: (s, lax.axis_index("c"), i, 0)),
                          pl.BlockSpec((None, tr, cdim), lambda s, i: (s, i, 0))],
                out_specs=pl.BlockSpec((None, tr, cdim), lambda s, i: (s, i, 0)),
                out_shape=jax.ShapeDtypeStruct((4, rh, cdim), BF16),
                compiler_params=_params(("parallel", "parallel")),
            )(g, o)
        )
    return outs


def _exchange_start(parts, after, name):
    def copies(src, land):
        x, y, c = _place()
        j = 2 * x + y
        return [(src[a].at[2 * px + py], land[a].at[j], (px, py, c)) for a in range(len(parts)) for px, py in _other_chips(x, y)]

    return _bulk_start(name, parts, [jax.ShapeDtypeStruct(p.shape, p.dtype) for p in parts], 3 * len(parts), copies, after)


def _exchange_finish(started, after, name):
    send, recv, srcs, lands, _ = started

    def waits(src, land):
        x, y, _ = _place()
        return [(src[a].at[2 * px + py], land[a].at[2 * px + py]) for a in range(len(srcs)) for px, py in _other_chips(x, y)]

    srcs, lands = _bulk_wait(name, send, recv, srcs, lands, after, waits)
    j = 2 * lax.axis_index("x") + lax.axis_index("y")
    return [lax.dynamic_update_slice(b, lax.dynamic_slice(p, (j, 0, 0), (1,) + p.shape[1:]), (j, 0, 0)) for b, p in zip(lands, srcs)]


def _sum_chips(recvd, tag):
    outs = []
    for a, g in enumerate(recvd):
        _, rh, cdim = g.shape
        tr = _pick(rh, 512, 16)

        def body(g_ref, o_ref):
            o_ref[...] = ((g_ref[0].astype(F32) + g_ref[1].astype(F32)) + g_ref[2].astype(F32)) + g_ref[3].astype(F32)

        outs.append(
            pl.pallas_call(
                body,
                name=f"sum_chips_{tag}{a}",
                grid=(rh // tr,),
                in_specs=[pl.BlockSpec((4, tr, cdim), lambda i: (0, i, 0))],
                out_specs=pl.BlockSpec((tr, cdim), lambda i: (i, 0)),
                out_shape=jax.ShapeDtypeStruct((rh, cdim), F32),
                compiler_params=_params(("parallel",)),
            )(g)
        )
    return outs


def _join_halves(halves, name):
    n = len(halves)

    def body(*refs):
        ins, outs = refs[:n], refs[n : 2 * n]
        send, recv = refs[2 * n :]
        x, y, c = _place()
        started = []
        for a in range(n):
            cp = _remote(ins[a], outs[a], send.at[a], recv.at[a], (x, y, 1 - c))
            cp.start()
            started.append(cp)
        for cp in started:
            cp.wait_recv()
        for cp in started:
            cp.wait_send()

    others = pl.pallas_call(
        body,
        name=name,
        out_shape=[jax.ShapeDtypeStruct(h.shape, h.dtype) for h in halves],
        in_specs=[ANY] * n,
        out_specs=[ANY] * n,
        scratch_shapes=[pltpu.SemaphoreType.DMA((n,)), pltpu.SemaphoreType.DMA((n,))],
    )(*halves)
    return list(zip(halves, others))


def _joined(mine, other):
    first = lax.axis_index("c") == 0
    return jnp.concatenate([jnp.where(first, mine, other), jnp.where(first, other, mine)], axis=0)


def _grad_views(grads):
    return [g.reshape(4, 2, g.shape[0] // 8, g.shape[1]) for g in grads]


def _scatter_start(grads, tag, after=None):
    views = _grad_views(grads)
    others = _swap_halves(views, "swap_halves_" + tag)
    mine = _add_halves(views, others, tag)
    return _exchange_start(mine, others[-1] if after is None else after, "exchange_start_" + tag)


def _swap_start(grads, after, tag):
    views = _grad_views(grads)

    def copies(src, land):
        x, y, c = _place()
        return [(src[a].at[s, 1 - c], land[a].at[s], (x, y, 1 - c)) for a in range(len(views)) for s in range(4)]

    shapes = [jax.ShapeDtypeStruct((4,) + v.shape[2:], v.dtype) for v in views]
    return _bulk_start("swap_start_" + tag, views, shapes, 4 * len(views), copies, after)


def _scatter_start_after_swap(swapped, after, tag):
    send, recv, views, lands, _ = swapped

    def waits(src, land):
        c = lax.axis_index("c")
        return [(src[a].at[s, 1 - c], land[a].at[s]) for a in range(len(views)) for s in range(4)]

    views, others = _bulk_wait("swap_wait_" + tag, send, recv, views, lands, after, waits)
    mine = _add_halves(views, others, tag)
    return _exchange_start(mine, others[-1], "exchange_start_" + tag)


def _join_start(halves, after, tag):
    def copies(src, land):
        x, y, c = _place()
        return [(src[a], land[a], (x, y, 1 - c)) for a in range(len(halves))]

    return _bulk_start("join_start_" + tag, halves, [jax.ShapeDtypeStruct(h.shape, h.dtype) for h in halves], len(halves), copies, after)


def _join_wait(started, after, tag):
    send, recv, halves, lands, _ = started
    halves, others = _bulk_wait("join_wait_" + tag, send, recv, halves, lands, after, lambda src, land: list(zip(src, land)))
    return list(zip(halves, others))


def _scatter_sums(started, after, tag):
    return _sum_chips(_exchange_finish(started, after, "exchange_wait_" + tag), tag)


def _scatter_finish(started, after, tag):
    return _join_halves(_scatter_sums(started, after, tag), "join_halves_" + tag)


def _t_bf16(w):
    return w.T.astype(BF16)


def kernel(x, c, ctx, c_ctx, w_ada, b_ada, norm1_g, w_in, mla_q_norm_g, w_q_up, mla_kv_norm_g, w_kv_up, gqa_q_norm_g, gqa_k_norm_g, w_br_a, w_br_b, w_out, norm2_g, w_up, conv_w, conv_b, w_down, final_norm_g, loss_target, m_c_ctx, m_w_ada, m_b_ada, m_norm1_g, m_w_in, m_mla_q_norm_g, m_w_q_up, m_mla_kv_norm_g, m_w_kv_up, m_gqa_q_norm_g, m_gqa_k_norm_g, m_w_br_a, m_w_br_b, m_w_out, m_norm2_g, m_w_up, m_conv_w, m_conv_b, m_w_down, m_final_norm_g, v_c_ctx, v_w_ada, v_b_ada, v_norm1_g, v_w_in, v_mla_q_norm_g, v_w_q_up, v_mla_kv_norm_g, v_w_kv_up, v_gqa_q_norm_g, v_gqa_k_norm_g, v_w_br_a, v_w_br_b, v_w_out, v_norm2_g, v_w_up, v_conv_w, v_conv_b, v_w_down, v_final_norm_g):
    T, D = x.shape[1], x.shape[2]
    C = ctx.shape[1]
    NA = w_ada.shape[2]
    NW = w_up.shape[2]
    F2 = 4 * NW
    FF = F2 // 2
    xi, yi, ci = _place()
    j = 2 * xi + yi
    me = 4 * xi + 2 * yi + ci
    tr = _pick(C, 256, 8)

    x2d, tgt, ctx2d = x[0], loss_target[0], ctx[0]
    fg = final_norm_g.reshape(1, D)
    cc = c_ctx.reshape(1, D)

    halve = lambda s: s.reshape(2, s.shape[0] // 2, s.shape[1])
    win_shard = halve(_t_bf16(w_in[0]))
    w0 = max(D, NW)
    pay = jnp.zeros((8, w0), F32).at[0:1, :D].set(c).at[1:4, :NW].set(conv_w[0])
    got = _all_gather_small(pay, "gather_cond")
    c_all = got[:, 0, :D]
    cw = jnp.concatenate([got[2 * s, 1:4, :NW] for s in range(4)], axis=1)
    s16 = jnp.concatenate([c_all, cc, jnp.zeros((7, D), F32)], axis=0)
    b_cols = lax.dynamic_slice(b_ada, (0, j * NA), (1, NA))
    ada_part = _mm(s16, w_ada[0], "NN", F32, "ada_fwd", act="silu", bias=b_cols)
    got = _all_gather_small(ada_part, "gather_ada")
    ada = jnp.concatenate([got[2 * s] for s in range(4)], axis=1)
    lat = lax.dynamic_slice(ada, (me, 0), (1, 6 * D))
    sh1, sc1, g1, sh2, sc2, g2 = [lat[:, k * D : (k + 1) * D] for k in range(6)]
    csh, csc = ada[8:9, :D], ada[8:9, D : 2 * D]

    ag_in = _gather_start([win_shard], got, "gather_start_in")
    t_in = ag_in[4]
    wq3 = (w_q_up[0] + t_in).reshape(MLA_Q_LORA, 2, MLA_NOPE + MLA_ROPE)
    wq_perm = jnp.concatenate([wq3[:, :, :MLA_NOPE].reshape(MLA_Q_LORA, -1), wq3[:, :, MLA_NOPE:].reshape(MLA_Q_LORA, -1)], axis=1)
    low = [_t_bf16(wq_perm), _t_bf16(w_kv_up[0] + t_in)]
    br = [_t_bf16(w_br_a[0] + t_in), _t_bf16(w_br_b[0] + t_in), (w_out[0] + t_in).astype(BF16)]
    ag_low = _gather_start([halve(s) for s in low], t_in, "gather_start_low")
    ag_br = _gather_start([halve(s) for s in br], ag_low[4], "gather_start_br")
    ag_up = _gather_start([halve(_t_bf16(w_up[0] + t_in))], ag_br[4], "gather_start_up")
    ag_down = _gather_start([halve((w_down[0] + t_in).astype(BF16))], ag_up[4], "gather_start_down")
    sh1 = sh1 + ag_down[4]

    cos_a, ss_a = _rope_tables(C, T, MLA_ROPE)
    cos_b, ss_b = _rope_tables(C, T, GQA_HEAD_DIM)
    lcos_a, lss_a, lcos_b, lss_b = cos_a[:T], ss_a[:T], cos_b[:T], ss_b[:T]

    z_all = _norm_mod_fwd(x2d, norm1_g, sh1, sc1, "norm1_lat_fwd", tr, out_rows=T + C)
    z_all = _norm_mod_fwd(ctx2d, norm1_g, csh, csc, "norm1_ctx_fwd", tr, base=z_all, out_off=T)
    (win_t,) = _gather_finish(ag_in, z_all, "in")
    kv_cols = KVP - LANES + MLA_ROPE
    e_kpe = MLA_KV_LORA + MLA_ROPE
    w_kvp = jnp.concatenate([win_t[:MLA_KV_LORA], win_t[e_kpe:kv_cols], win_t[MLA_KV_LORA:e_kpe], jnp.zeros((LANES - MLA_ROPE, D), BF16)], axis=0)

    pkv = _mm(z_all, w_kvp, "NT", F32, "proj_kv", tn=KVP)
    pq = _mm(z_all, win_t, "NT", F32, "proj_q", m=T, n=QC, b_off=kv_cols)
    low_landed = _gather_land(ag_low, pq, "low")
    pg = _mm(z_all, win_t, "NT", BF16, "proj_g", m=T, n=2 * D, b_off=kv_cols + QC, after=low_landed[1][4])
    wq_t, wkv_t = _gather_done(low_landed, pg, "low")
    ckv_n, kb2, vb2, kpe2 = _kprep_fwd(pkv, mla_kv_norm_g, gqa_k_norm_g, cos_a, ss_a, cos_b, ss_b, tr)
    kv_up = _mm(ckv_n, wkv_t, "NT", BF16, "kv_up")
    cq_n, qb2 = _qprep_fwd(pq, mla_q_norm_g, gqa_q_norm_g, lcos_b, lss_b, tr)
    q_a = _mm(cq_n, wq_t, "NT", F32, "q_up")
    qar = _qrope_fwd(q_a, lcos_a, lss_a, tr)

    a_q = [(qar, lambda h: 3 * (h // 2) + h % 2), (qar, lambda h: 3 * (h // 2) + 2)]
    a_k = [(kv_up, lambda h: 2 * h), (kpe2, lambda h: h % 2)]
    a_v = (kv_up, lambda h: 2 * h + 1)
    a_scale = float(MLA_NOPE + MLA_ROPE) ** -0.5
    b_q = [(qb2, lambda h: h)]
    b_k = [(kb2, lambda h: h)]
    b_v = (vb2, lambda h: h)
    b_scale = float(GQA_HEAD_DIM) ** -0.5
    tq_f = _pick(T, 1024)
    o_a, lse_a = _attn_fwd(a_q, a_k, a_v, MLA_HEADS, 1, MLA_V, a_scale, "attn_a_fwd", tq_f)
    br_landed = _gather_land(ag_br, o_a, "br")
    o_b, lse_b = _attn_fwd(b_q, b_k, b_v, GQA_HEADS, GQA_GROUP, GQA_HEAD_DIM, b_scale, "attn_b_fwd", tq_f, after=br_landed[1][4])
    wbra_t, wbrb_t, wout = _gather_done(br_landed, o_b, "br")
    up_landed = _gather_land(ag_up, o_b, "up")
    ya = _mm(o_a, wbra_t, "NT", BF16, "br_a", after=up_landed[1][4])
    yb = _mm(o_b, wbrb_t, "NT", BF16, "br_b")
    merged = _gates_fwd(pg, ya, yb, tr)
    att = _mm(merged, wout, "NN", F32, "out_proj")
    x1, z2 = _resid_norm2_fwd(x2d, att, g1, norm2_g, sh2, sc2, tr)
    (wup_t,) = _gather_done(up_landed, z2, "up")
    down_landed = _gather_land(ag_down, z2, "down")
    tc = _pick(FF, 128)
    u_a, u_b, hg = _ffn_up_conv(z2, wup_t, cw, conv_b, tc, down_landed[1][4])
    (wdown,) = _gather_done(down_landed, hg, "down")
    f = _mm(hg, wdown, "NN", F32, "ffn_down", tk=FF // 2)
    sq, dx2, d_fg, d_g2, df = _loss_head(x1, f, g2, fg, tgt, tr)
    loss = lax.psum(0.5 * jnp.sum(sq) / D, ("x", "y", "c"))

    du_a, du_b, dcw_a, dcw_b, dcb_a, dcb_b = _ffn_down_dx_conv_bwd(df, wdown, u_a, u_b, cw, conv_b, _pick(FF, 256), loss.reshape(1, 1))
    g_wdown = _mm(hg, df, "TN", BF16, "ffn_down_dw", tm=FF // 4)
    dz2 = _mm(du_a, wup_t, "NN", F32, "ffn_up_dx_a", tk=FF // 2)
    dz2 = _mm(du_b, wup_t, "NN", F32, "ffn_up_dx_b", b_off=FF, add=dz2, tk=FF // 2)
    g_wup_t = _mm(du_a, z2, "TN", BF16, "ffn_up_dw_a", out_rows=F2, tm=FF // 4)
    g_wup_t = _mm(du_b, z2, "TN", BF16, "ffn_up_dw_b", out_base=g_wup_t, out_off=FF, tm=FF // 4)
    sw_ffn = _swap_start([g_wdown, g_wup_t], sc2, "ffn")
    sc2 = sc2 + sw_ffn[4]
    dx1, datt, d_n2g, d_sh2, d_sc2, d_g1 = _resid_norm2_bwd(dz2, x1, dx2, att, norm2_g, sc2, g1, tr)

    dmerged = _mm(datt, wout, "NT", BF16, "out_proj_dx")
    rs_ffn = _scatter_start_after_swap(sw_ffn, dmerged, "ffn")
    lse_a = lse_a + rs_ffn[4]
    g_wout = _mm(merged, datt, "TN", BF16, "out_proj_dw")
    dya, dyb, dpg = _gates_bwd(dmerged, pg, ya, yb, tr)
    do_a = _mm(dya, wbra_t, "NN", BF16, "br_a_dx")
    g_wbra_t = _mm(dya, o_a, "TN", BF16, "br_a_dw")
    do_b = _mm(dyb, wbrb_t, "NN", BF16, "br_b_dx")
    g_wbrb_t = _mm(dyb, o_b, "TN", BF16, "br_b_dw")
    dqa2, dka2, dva2 = _attn_bwd(a_q, a_k, a_v, o_a, do_a, lse_a, MLA_HEADS, 1, MLA_V, a_scale, "attn_a_bwd", tq_f)
    dqb2, dkb2, dvb2 = _attn_bwd(b_q, b_k, b_v, o_b, do_b, lse_b, GQA_HEADS, GQA_GROUP, GQA_HEAD_DIM, b_scale, "attn_b_bwd", tq_f)
    dq_a = _qrope_bwd(dqa2, lcos_a, lss_a, tr)
    dcq_n = _mm(dq_a, wq_t, "NN", F32, "q_up_dx")
    g_wq_t = _mm(dq_a, cq_n, "TN", BF16, "q_up_dw")
    dpq, d_qg, d_gq = _qprep_bwd(pq, dcq_n, dqb2, mla_q_norm_g, gqa_q_norm_g, lcos_b, lss_b, tr)
    dkv_up, dkpe = _kgrad_split(dka2, dva2, cos_a, ss_a, tr)
    dckv_n = _mm(dkv_up, wkv_t, "NN", F32, "kv_up_dx")
    g_wkv_t = _mm(dkv_up, ckv_n, "TN", BF16, "kv_up_dw")
    rs_mix = _scatter_start([g_wq_t, g_wkv_t, g_wbra_t, g_wbrb_t, g_wout], "mix")
    dpkv, d_kvg, d_kg = _kprep_bwd(pkv, dckv_n, dkb2, dvb2, dkpe, mla_kv_norm_g + rs_mix[4], gqa_k_norm_g, cos_b, ss_b, tr)
    dz_kv = _mm(dpkv, w_kvp, "NN", F32, "proj_kv_dx")
    dz_lat = _mm(dpq, win_t, "NN", F32, "proj_q_dx", b_off=kv_cols, add=dz_kv)
    dz_lat = _mm(dpg, win_t, "NN", F32, "proj_g_dx", b_off=kv_cols + QC, add=dz_lat)
    _, d_n1g_c, d_csh, d_csc = _norm_mod_bwd(dz_kv, T // tr, ctx2d, norm1_g, csc, None, "norm1_ctx_bwd", tr)
    grad_x, d_n1g_l, d_sh1, d_sc1 = _norm_mod_bwd(dz_lat, 0, x2d, norm1_g, sc1, dx1, "norm1_lat_bwd", tr)

    zeros_d = jnp.zeros((1, D), F32)
    d_lat = jnp.concatenate([d_sh1, d_sc1, d_g1, d_sh2, d_sc2, d_g2], axis=1)
    d_ctx_part = jnp.concatenate([d_csh, d_csc], axis=1)
    flat = jnp.concatenate(
        [d_n1g_c + d_n1g_l, d_qg, d_kvg, d_gq, d_kg, d_n2g, dcb_a, dcb_b, d_fg,
         dcw_a.reshape(1, -1), dcw_b.reshape(1, -1), d_ctx_part, d_lat], axis=1)
    n_flat = flat.shape[1]
    n_rows = -(-n_flat // (8 * LANES)) * 8
    flat = jnp.pad(flat, ((0, 0), (0, n_rows * LANES - n_flat))).reshape(n_rows, LANES)
    got = _all_gather_small(flat, "gather_small_grads")
    tot = _sum_slots(got, "sum_small_grads").reshape(1, -1)
    sizes = [D, MLA_Q_LORA, MLA_KV_LORA, GQA_HEAD_DIM, GQA_HEAD_DIM, D, F2, D, 3 * FF, 3 * FF, 2 * D]
    offs = [0]
    for s in sizes:
        offs.append(offs[-1] + s)
    t_n1g, t_qg, t_kvg, t_gq, t_kg, t_n2g, t_cb, t_fg, t_cwa, t_cwb, t_ctx = [tot[:, offs[k] : offs[k + 1]] for k in range(len(sizes))]
    g_cw_full = jnp.concatenate([t_cwa.reshape(3, FF), t_cwb.reshape(3, FF)], axis=1)
    g_cw = lax.dynamic_slice(g_cw_full, (0, j * NW), (3, NW))
    d_lat_all = got.reshape(8, -1)[:, offs[-1] : offs[-1] + 6 * D]
    g16 = jnp.concatenate([d_lat_all, jnp.pad(t_ctx, ((0, 0), (0, 4 * D))), jnp.zeros((7, 6 * D), F32)], axis=0)
    g_b_ada = _sum_slots(g16.reshape(16, 1, 6 * D), "sum_b_ada")
    g16_cols = lax.dynamic_slice(g16, (0, j * NA), (16, NA))
    ds_part = _mm(g16_cols, w_ada[0], "NT", F32, "ada_dx")
    got = _all_gather_small(ds_part[8:16], "gather_ada_dx")
    ds_ctx = _sum_slots(jnp.stack([got[2 * s] for s in range(4)]), "sum_ada_dx")[0:1]
    g_c_ctx = _silu_grad_mul(ds_ctx, cc)

    g_kvp = _mm(dpkv, z_all, "TN", BF16, "proj_kv_dw")
    nk = MLA_KV_LORA + 2 * GQA_KV_HEADS * GQA_HEAD_DIM
    g_kv = jnp.concatenate([g_kvp[:MLA_KV_LORA], g_kvp[nk : nk + MLA_ROPE], g_kvp[MLA_KV_LORA:nk]], axis=0)
    g_win_t = _mm(dpq, z_all, "TN", BF16, "proj_q_dw", out_rows=kv_cols + QC + 2 * D, out_off=kv_cols, tm=QC // 2)
    g_win_t = _mm(dpg, z_all, "TN", BF16, "proj_g_dw", out_base=g_win_t, out_off=kv_cols + QC)
    g_win_t = lax.dynamic_update_slice(g_win_t, g_kv, (0, 0))
    sw_in = _swap_start([g_win_t], got, "in")

    h_ffn = _scatter_sums(rs_ffn, sw_in[2][0], "ffn")
    j_ffn = _join_start(h_ffn, grad_x, "ffn")
    h_mix = _scatter_sums(rs_mix, j_ffn[2][0], "mix")
    j_mix = _join_start(h_mix, j_ffn[2][0], "mix")
    rs_in = _scatter_start_after_swap(sw_in, j_mix[2][0], "in")
    g_w_ada = _mm(s16, g16_cols, "TN", F32, "ada_dw", act="silu", after=rs_in[4])
    _, d_ada, m_ada, v_ada = _adamw(w_ada[0], g_w_ada, m_w_ada[0], v_w_ada[0], "adamw_w_ada")
    r_wdown, r_wup = _join_wait(j_ffn, d_ada, "ffn")
    r_wq, r_wkv, r_wbra, r_wbrb, r_wout = _join_wait(j_mix, d_ada, "mix")
    gq_p = _joined(*r_wq).T
    gq = jnp.concatenate([gq_p[:, : 2 * MLA_NOPE].reshape(MLA_Q_LORA, 2, MLA_NOPE), gq_p[:, 2 * MLA_NOPE :].reshape(MLA_Q_LORA, 2, MLA_ROPE)], axis=2)
    grads = {
        "c_ctx": g_c_ctx.reshape(D), "w_ada": g_w_ada[None], "b_ada": g_b_ada, "norm1_g": t_n1g,
        "mla_q_norm_g": t_qg, "w_q_up": gq.reshape(1, MLA_Q_LORA, -1), "mla_kv_norm_g": t_kvg, "w_kv_up": r_wkv,
        "gqa_q_norm_g": t_gq, "gqa_k_norm_g": t_kg, "w_br_a": r_wbra, "w_br_b": r_wbrb, "w_out": r_wout,
        "norm2_g": t_n2g, "w_up": r_wup, "conv_w": g_cw[None], "conv_b": t_cb, "w_down": r_wdown,
        "final_norm_g": t_fg.reshape(D),
    }
    arrives_transposed = ("w_kv_up", "w_br_a", "w_br_b", "w_up")
    arrives_halved = arrives_transposed + ("w_out", "w_down")
    weights = dict(c_ctx=c_ctx, w_ada=w_ada, b_ada=b_ada, norm1_g=norm1_g, w_in=w_in, mla_q_norm_g=mla_q_norm_g, w_q_up=w_q_up,
                   mla_kv_norm_g=mla_kv_norm_g, w_kv_up=w_kv_up, gqa_q_norm_g=gqa_q_norm_g, gqa_k_norm_g=gqa_k_norm_g, w_br_a=w_br_a,
                   w_br_b=w_br_b, w_out=w_out, norm2_g=norm2_g, w_up=w_up, conv_w=conv_w, conv_b=conv_b, w_down=w_down,
                   final_norm_g=final_norm_g)
    m_in = dict(c_ctx=m_c_ctx, w_ada=m_w_ada, b_ada=m_b_ada, norm1_g=m_norm1_g, w_in=m_w_in, mla_q_norm_g=m_mla_q_norm_g,
                w_q_up=m_w_q_up, mla_kv_norm_g=m_mla_kv_norm_g, w_kv_up=m_w_kv_up, gqa_q_norm_g=m_gqa_q_norm_g,
                gqa_k_norm_g=m_gqa_k_norm_g, w_br_a=m_w_br_a, w_br_b=m_w_br_b, w_out=m_w_out, norm2_g=m_norm2_g, w_up=m_w_up,
                conv_w=m_conv_w, conv_b=m_conv_b, w_down=m_w_down, final_norm_g=m_final_norm_g)
    v_in = dict(c_ctx=v_c_ctx, w_ada=v_w_ada, b_ada=v_b_ada, norm1_g=v_norm1_g, w_in=v_w_in, mla_q_norm_g=v_mla_q_norm_g,
                w_q_up=v_w_q_up, mla_kv_norm_g=v_mla_kv_norm_g, w_kv_up=v_w_kv_up, gqa_q_norm_g=v_gqa_q_norm_g,
                gqa_k_norm_g=v_gqa_k_norm_g, w_br_a=v_w_br_a, w_br_b=v_w_br_b, w_out=v_w_out, norm2_g=v_norm2_g, w_up=v_w_up,
                conv_w=v_conv_w, conv_b=v_conv_b, w_down=v_w_down, final_norm_g=v_final_norm_g)
    names = list(weights)
    big = [n for n in names if weights[n].ndim == 3 and weights[n].shape[1] >= 8]
    small = [n for n in names if n not in big]
    delta, new_m, new_v = {}, {}, {}

    def update(n):
        shp = weights[n].shape
        two_d = lambda a: a.reshape(shp[1], shp[2])
        g_t = n in arrives_transposed
        if n in arrives_halved:
            g_in, g_sib = grads[n]
        else:
            g_in, g_sib = two_d(grads[n].astype(F32)), None
        g_, d_, m_, v_ = _adamw(two_d(weights[n]), g_in, two_d(m_in[n]), two_d(v_in[n]), "adamw_" + n, g_transposed=g_t, g_sibling=g_sib)
        grads[n], delta[n], new_m[n], new_v[n] = g_.reshape(shp), d_.reshape(shp), m_.reshape(shp), v_.reshape(shp)

    delta["w_ada"], new_m["w_ada"], new_v["w_ada"] = d_ada[None], m_ada[None], v_ada[None]
    early = [n for n in big if n not in ("w_in", "w_ada")]
    for n in early[:-1]:
        update(n)
    done = sum(delta[n][0, 0:1, 0:1] for n in early[:-1])
    j_in = _join_start(_scatter_sums(rs_in, done, "in"), done, "in")
    last = early[-1]
    grads[last] = (grads[last][0] + j_in[4], grads[last][1])
    update(last)
    ((g_mine, g_sib),) = _join_wait(j_in, delta[last], "in")
    g_, d_, m_, v_ = _adamw(w_in[0].T, g_mine, m_w_in[0].T, v_w_in[0].T, "adamw_w_in", g_sibling=g_sib)
    grads["w_in"], delta["w_in"], new_m["w_in"], new_v["w_in"] = g_.T[None], d_.T[None], m_.T[None], v_.T[None]
    grads = {n: grads[n].reshape(weights[n].shape).astype(F32) for n in names}

    slab = lambda tree: [tree[n].reshape(-1, LANES) for n in small]
    d_, m_, v_ = _adamw_many(slab(weights), slab(grads), slab(m_in), slab(v_in), "adamw_small")
    for k, n in enumerate(small):
        shp = weights[n].shape
        delta[n], new_m[n], new_v[n] = d_[k].reshape(shp), m_[k].reshape(shp), v_[k].reshape(shp)

    return (loss, grad_x[None], *[grads[n] for n in names], *[delta[n] for n in names], *[new_m[n] for n in names],
            *[new_v[n] for n in names])
```

```python
import math

import jax
import jax.numpy as jnp
from jax import lax
from jax.experimental import pallas as pl
from jax.experimental.pallas import tpu as pltpu

F32 = jnp.float32
BF16 = jnp.bfloat16
MESH = pl.DeviceIdType.MESH

NORM_EPS = 1e-6
ROPE_THETA = 10000.0
GRID_W = 64
MLA_HEADS = 8
MLA_Q_LORA = 768
MLA_KV_LORA = 512
MLA_NOPE = 128
MLA_ROPE = 64
MLA_V = 128
GQA_HEADS = 8
GQA_KV_HEADS = 2
GQA_HEAD_DIM = 128
GQA_GROUP = GQA_HEADS // GQA_KV_HEADS
LANES = 128
KVP = MLA_KV_LORA + 2 * GQA_KV_HEADS * GQA_HEAD_DIM + LANES
QC = MLA_Q_LORA + GQA_HEADS * GQA_HEAD_DIM

ADAM_LR = 0.001
ADAM_B1 = 0.9
ADAM_B2 = 0.999
ADAM_EPS = 1e-08
ADAM_WD = 0.01
ADAM_STEP = 10

VMEM_LIMIT = 56 * 1024 * 1024


def _pick(dim, target, mult=LANES):
    t = (min(target, dim) // mult) * mult
    while t >= mult:
        if dim % t == 0:
            return t
        t -= mult
    return dim


def _params(sem):
    return pltpu.CompilerParams(dimension_semantics=sem, vmem_limit_bytes=VMEM_LIMIT)


_DIMS = {"NN": (((1,), (0,)), ((), ())), "NT": (((1,), (1,)), ((), ())), "TN": (((0,), (0,)), ((), ()))}


MM_VMEM_BUDGET = 36 * 1024 * 1024


def _mm_tiles(M, N, K, sa, sb, so, tm, tn, tk):
    tm, tn, tk = _pick(M, tm), _pick(N, tn), _pick(K, tk)

    def need(t):
        return 2 * (tm * t * sa + t * tn * sb) + 2 * tm * tn * so + (tm * tn * 4 if t < K else 0)

    while need(tk) > MM_VMEM_BUDGET and tk > LANES:
        smaller = _pick(K, tk - LANES)
        if smaller >= tk:
            break
        tk = smaller
    return tm, tn, tk


def _window(block, index, offsets):
    if not any(offsets):
        return pl.BlockSpec(block, index)
    for t, o in zip(block, offsets):
        assert o % 16 == 0 and t % 16 == 0, (block, offsets)

    def at(i, j, k):
        return tuple(pl.multiple_of(o + p * t, math.gcd(o, t)) for p, t, o in zip(index(i, j, k), block, offsets))

    return pl.BlockSpec(tuple(pl.Element(t) for t in block), at)


def _mm(a, b, mode, out_dtype, name, m=None, n=None, k=None, b_off=0, add=None, out_rows=None, out_base=None, out_off=0,
        tm=1024, tn=1024, tk=2304, act=None, bias=None, after=None):
    if mode == "NN":
        M, K, N = m or a.shape[0], k or a.shape[1], b.shape[1]
    elif mode == "NT":
        M, K, N = m or a.shape[0], a.shape[1], n or b.shape[0]
    else:
        M, K, N = a.shape[1], k or a.shape[0], b.shape[1]
    tm, tn, tk = _mm_tiles(M, N, K, a.dtype.itemsize, b.dtype.itemsize, jnp.dtype(out_dtype).itemsize, tm, tn, tk)
    nk = K // tk
    dims = _DIMS[mode]
    n_in = 2 + (bias is not None) + (add is not None) + (out_base is not None) + (after is not None)

    def body(*refs):
        a_ref, b_ref = refs[:2]
        bias_ref = refs[2] if bias is not None else None
        add_ref = refs[2 + (bias is not None)] if add is not None else None
        o_ref = refs[n_in]
        av = a_ref[...]
        if act == "silu":
            av = av * jax.nn.sigmoid(av)
        part = lax.dot_general(av.astype(BF16), b_ref[...].astype(BF16), dims, preferred_element_type=F32)

        def finish(r):
            if bias is not None:
                r = r + bias_ref[...]
            if add is not None:
                r = r + add_ref[...]
            o_ref[...] = r.astype(out_dtype)

        if nk == 1:
            finish(part)
            return
        acc = refs[-1]
        k = pl.program_id(2)

        @pl.when(k == 0)
        def _():
            acc[...] = part

        @pl.when(jnp.logical_and(k > 0, k < nk - 1))
        def _():
            acc[...] += part

        @pl.when(k == nk - 1)
        def _():
            finish(acc[...] + part)

    a_spec = pl.BlockSpec((tk, tm), lambda i, j, k: (k, i)) if mode == "TN" else pl.BlockSpec((tm, tk), lambda i, j, k: (i, k))
    if mode == "NT":
        b_spec = _window((tn, tk), lambda i, j, k: (j, k), (b_off, 0))
    else:
        b_spec = _window((tk, tn), lambda i, j, k: (k, j), (b_off, 0))
    in_specs, args = [a_spec, b_spec], [a, b]
    if bias is not None:
        in_specs.append(pl.BlockSpec((1, tn), lambda i, j, k: (0, j)))
        args.append(bias)
    if add is not None:
        in_specs.append(pl.BlockSpec((tm, tn), lambda i, j, k: (i, j)))
        args.append(add)
    aliases = {}
    if after is not None:
        in_specs.append(pl.BlockSpec(after.shape, lambda i, j, k: (0, 0)))
        args.append(after)
    if out_base is not None:
        aliases = {len(args): 0}
        in_specs.append(ANY)
        args.append(out_base)
        out_rows = out_base.shape[0]
    return pl.pallas_call(
        body,
        name=name,
        grid=(M // tm, N // tn, nk),
        in_specs=in_specs,
        out_specs=_window((tm, tn), lambda i, j, k: (i, j), (out_off, 0)),
        out_shape=jax.ShapeDtypeStruct((out_rows or M, N), out_dtype),
        input_output_aliases=aliases,
        scratch_shapes=[pltpu.VMEM((tm, tn), F32)] if nk > 1 else [],
        compiler_params=_params(("parallel", "parallel", "arbitrary")),
    )(*args)


def _rms(x):
    r = lax.rsqrt(jnp.mean(x * x, axis=-1, keepdims=True) + NORM_EPS)
    return x * r, r


def _rms_bwd(xh, r, dxh):
    return r * (dxh - xh * jnp.mean(dxh * xh, axis=-1, keepdims=True))


def _swap(x, q):
    lane = lax.broadcasted_iota(jnp.int32, x.shape, 1)
    even = ((lane // q) % 2) == 0
    return jnp.where(even, pltpu.roll(x, LANES - q, 1), pltpu.roll(x, q, 1))


def _rope(x, cos, ss, q):
    return x * cos + _swap(x, q) * ss


def _rope_t(d, cos, ss, q):
    return d * cos + _swap(d * ss, q)


def _csum(x):
    return jnp.sum(x, axis=0, keepdims=True)


def _rows(tr, w, off=0):
    return pl.BlockSpec((tr, w), lambda i: (i + off, 0))


def _bcast(w):
    return pl.BlockSpec((1, w), lambda i: (0, 0))


def _acc_init(i, refs):
    @pl.when(i == 0)
    def _():
        for r in refs:
            r[...] = jnp.zeros_like(r)


def _rope_tables(n_ctx, n_lat, rot_dim):
    rows = n_lat // GRID_W
    row = jnp.repeat(jnp.arange(rows, dtype=F32), GRID_W)
    col = jnp.tile(jnp.arange(GRID_W, dtype=F32), rows)
    half = rot_dim // 2
    inv_freq = ROPE_THETA ** (-jnp.arange(0, half, 2, dtype=F32) / half)
    ar, ac = row[:, None] * inv_freq, col[:, None] * inv_freq
    cos = jnp.concatenate([jnp.cos(ar), jnp.cos(ar), jnp.cos(ac), jnp.cos(ac)], axis=-1)
    ss = jnp.concatenate([-jnp.sin(ar), jnp.sin(ar), -jnp.sin(ac), jnp.sin(ac)], axis=-1)
    cos = jnp.tile(cos, (1, LANES // rot_dim))
    ss = jnp.tile(ss, (1, LANES // rot_dim))
    cos = jnp.concatenate([cos, jnp.ones((n_ctx, LANES), F32)], axis=0)
    ss = jnp.concatenate([ss, jnp.zeros((n_ctx, LANES), F32)], axis=0)
    return cos, ss


def _norm_mod_fwd(x2d, g, sh, sc, name, tr, out_rows=None, base=None, out_off=0):
    n, d = x2d.shape

    def body(x_ref, g_ref, sh_ref, sc_ref, *rest):
        xh, _ = _rms(x_ref[...])
        rest[-1][...] = ((xh * g_ref[...]) * (1.0 + sc_ref[...]) + sh_ref[...]).astype(BF16)

    args, in_specs, aliases = [x2d, g, sh, sc], [_rows(tr, d), _bcast(d), _bcast(d), _bcast(d)], {}
    if base is not None:
        args.append(base)
        in_specs.append(ANY)
        aliases = {4: 0}
        out_rows = base.shape[0]
    return pl.pallas_call(
        body,
        name=name,
        grid=(n // tr,),
        in_specs=in_specs,
        out_specs=_rows(tr, d, out_off // tr),
        out_shape=jax.ShapeDtypeStruct((out_rows or n, d), BF16),
        input_output_aliases=aliases,
        compiler_params=_params(("parallel",)),
    )(*args)


def _norm_mod_bwd(dz, dz_off, x2d, g, sc, dres, name, tr):
    n, d = x2d.shape
    want_dx = dres is not None

    def body(*refs):
        if want_dx:
            dz_ref, x_ref, g_ref, sc_ref, dres_ref, dx_ref, dg_ref, dsh_ref, dsc_ref = refs
        else:
            dz_ref, x_ref, g_ref, sc_ref, dg_ref, dsh_ref, dsc_ref = refs
        _acc_init(pl.program_id(0), [dg_ref, dsh_ref, dsc_ref])
        xh, r = _rms(x_ref[...])
        dzv = dz_ref[...]
        gv = g_ref[...]
        dsc_ref[...] += _csum(dzv * (xh * gv))
        dsh_ref[...] += _csum(dzv)
        dh = dzv * (1.0 + sc_ref[...])
        dg_ref[...] += _csum(dh * xh)
        if want_dx:
            dx_ref[...] = _rms_bwd(xh, r, dh * gv) + dres_ref[...]

    in_specs = [_rows(tr, d, dz_off), _rows(tr, d), _bcast(d), _bcast(d)]
    args = [dz, x2d, g, sc]
    out_specs = [_bcast(d)] * 3
    out_shape = [jax.ShapeDtypeStruct((1, d), F32)] * 3
    if want_dx:
        in_specs.append(_rows(tr, d))
        args.append(dres)
        out_specs = [_rows(tr, d)] + out_specs
        out_shape = [jax.ShapeDtypeStruct((n, d), F32)] + out_shape
    res = pl.pallas_call(
        body,
        name=name,
        grid=(n // tr,),
        in_specs=in_specs,
        out_specs=out_specs,
        out_shape=out_shape,
        compiler_params=_params(("arbitrary",)),
    )(*args)
    return res if want_dx else (None, *res)


_QA, _QB = MLA_ROPE // 4, GQA_HEAD_DIM // 4


def _kprep_fwd(pkv, kvg, kg, cos_a, ss_a, cos_b, ss_b, tr):
    n = pkv.shape[0]
    nb = GQA_KV_HEADS * GQA_HEAD_DIM

    def body(p_ref, kvg_ref, kg_ref, ca, sa, cb, sb, ckv_ref, kb_ref, vb_ref, kpe_ref):
        p = p_ref[...]
        xh, _ = _rms(p[:, :MLA_KV_LORA])
        ckv_ref[...] = (xh * kvg_ref[...]).astype(BF16)
        for e in range(GQA_KV_HEADS):
            lo = MLA_KV_LORA + e * GQA_HEAD_DIM
            kh, _ = _rms(p[:, lo : lo + GQA_HEAD_DIM])
            kb_ref[:, e * GQA_HEAD_DIM : (e + 1) * GQA_HEAD_DIM] = _rope(kh * kg_ref[...], cb[...], sb[...], _QB).astype(BF16)
        vb_ref[...] = p[:, MLA_KV_LORA + nb : MLA_KV_LORA + 2 * nb].astype(BF16)
        kr = _rope(p[:, MLA_KV_LORA + 2 * nb :], ca[...], sa[...], _QA)
        kpe_ref[:, :LANES] = kr.astype(BF16)
        kpe_ref[:, LANES:] = pltpu.roll(kr, MLA_ROPE, 1).astype(BF16)

    return pl.pallas_call(
        body,
        name="kprep_fwd",
        grid=(n // tr,),
        in_specs=[_rows(tr, KVP), _bcast(MLA_KV_LORA), _bcast(GQA_HEAD_DIM)] + [_rows(tr, LANES)] * 4,
        out_specs=[_rows(tr, MLA_KV_LORA), _rows(tr, nb), _rows(tr, nb), _rows(tr, 2 * LANES)],
        out_shape=[jax.ShapeDtypeStruct((n, w), BF16) for w in (MLA_KV_LORA, nb, nb, 2 * LANES)],
        compiler_params=_params(("parallel",)),
    )(pkv, kvg, kg, cos_a, ss_a, cos_b, ss_b)


def _kprep_bwd(pkv, dckv, dkb, dvb, dkpe, kvg, kg, cos_b, ss_b, tr):
    n = pkv.shape[0]
    nb = GQA_KV_HEADS * GQA_HEAD_DIM

    def body(p_ref, dckv_ref, dkb_ref, dvb_ref, dkpe_ref, kvg_ref, kg_ref, cb, sb, dp_ref, dkvg_ref, dkg_ref):
        _acc_init(pl.program_id(0), [dkvg_ref, dkg_ref])
        p = p_ref[...]
        xh, r = _rms(p[:, :MLA_KV_LORA])
        dn = dckv_ref[...]
        dkvg_ref[...] += _csum(dn * xh)
        dp_ref[:, :MLA_KV_LORA] = _rms_bwd(xh, r, dn * kvg_ref[...]).astype(BF16)
        for e in range(GQA_KV_HEADS):
            lo = MLA_KV_LORA + e * GQA_HEAD_DIM
            kh, rk = _rms(p[:, lo : lo + GQA_HEAD_DIM])
            dk = _rope_t(dkb_ref[:, e * GQA_HEAD_DIM : (e + 1) * GQA_HEAD_DIM], cb[...], sb[...], _QB)
            dkg_ref[...] += _csum(dk * kh)
            dp_ref[:, lo : lo + GQA_HEAD_DIM] = _rms_bwd(kh, rk, dk * kg_ref[...]).astype(BF16)
        dp_ref[:, MLA_KV_LORA + nb : MLA_KV_LORA + 2 * nb] = dvb_ref[...].astype(BF16)
        dp_ref[:, MLA_KV_LORA + 2 * nb :] = dkpe_ref[...].astype(BF16)

    return pl.pallas_call(
        body,
        name="kprep_bwd",
        grid=(n // tr,),
        in_specs=[_rows(tr, KVP), _rows(tr, MLA_KV_LORA), _rows(tr, nb), _rows(tr, nb), _rows(tr, LANES),
                  _bcast(MLA_KV_LORA), _bcast(GQA_HEAD_DIM), _rows(tr, LANES), _rows(tr, LANES)],
        out_specs=[_rows(tr, KVP), _bcast(MLA_KV_LORA), _bcast(GQA_HEAD_DIM)],
        out_shape=[jax.ShapeDtypeStruct((n, KVP), BF16), jax.ShapeDtypeStruct((1, MLA_KV_LORA), F32),
                   jax.ShapeDtypeStruct((1, GQA_HEAD_DIM), F32)],
        compiler_params=_params(("arbitrary",)),
    )(pkv, dckv, dkb, dvb, dkpe, kvg, kg, cos_b, ss_b)


def _kgrad_split(dka, dva, cos_a, ss_a, tr):
    n = dka.shape[0]
    wk = MLA_HEADS * 2 * LANES

    def body(dk_ref, dv_ref, ca, sa, dkv_ref, dkpe_ref):
        even = jnp.zeros((tr, LANES), F32)
        odd = jnp.zeros((tr, LANES), F32)
        for h in range(MLA_HEADS):
            dkv_ref[:, 2 * h * LANES : (2 * h + 1) * LANES] = dk_ref[:, 2 * h * LANES : (2 * h + 1) * LANES].astype(BF16)
            dkv_ref[:, (2 * h + 1) * LANES : (2 * h + 2) * LANES] = dv_ref[:, h * MLA_V : (h + 1) * MLA_V].astype(BF16)
            part = dk_ref[:, (2 * h + 1) * LANES : (2 * h + 2) * LANES]
            if h % 2 == 0:
                even = even + part
            else:
                odd = odd + part
        lane = lax.broadcasted_iota(jnp.int32, (tr, LANES), 1)
        low = lane < MLA_ROPE
        both = jnp.where(low, even, odd)
        tot = jnp.where(low, both + pltpu.roll(both, MLA_ROPE, 1), 0.0)
        dkpe_ref[...] = _rope_t(tot, ca[...], sa[...], _QA)

    return pl.pallas_call(
        body,
        name="kgrad_split",
        grid=(n // tr,),
        in_specs=[_rows(tr, wk), _rows(tr, MLA_HEADS * MLA_V), _rows(tr, LANES), _rows(tr, LANES)],
        out_specs=[_rows(tr, wk), _rows(tr, LANES)],
        out_shape=[jax.ShapeDtypeStruct((n, wk), BF16), jax.ShapeDtypeStruct((n, LANES), F32)],
        compiler_params=_params(("parallel",)),
    )(dka, dva, cos_a, ss_a)


def _qprep_fwd(pq, qg, gq, cos_b, ss_b, tr):
    n = pq.shape[0]
    nq = GQA_HEADS * GQA_HEAD_DIM

    def body(p_ref, qg_ref, gq_ref, cb, sb, cq_ref, qb_ref):
        xh, _ = _rms(p_ref[:, :MLA_Q_LORA])
        cq_ref[...] = (xh * qg_ref[...]).astype(BF16)
        for h in range(GQA_HEADS):
            lo = MLA_Q_LORA + h * GQA_HEAD_DIM
            qh, _ = _rms(p_ref[:, lo : lo + GQA_HEAD_DIM])
            qb_ref[:, h * GQA_HEAD_DIM : (h + 1) * GQA_HEAD_DIM] = _rope(qh * gq_ref[...], cb[...], sb[...], _QB).astype(BF16)

    return pl.pallas_call(
        body,
        name="qprep_fwd",
        grid=(n // tr,),
        in_specs=[_rows(tr, QC), _bcast(MLA_Q_LORA), _bcast(GQA_HEAD_DIM), _rows(tr, LANES), _rows(tr, LANES)],
        out_specs=[_rows(tr, MLA_Q_LORA), _rows(tr, nq)],
        out_shape=[jax.ShapeDtypeStruct((n, MLA_Q_LORA), BF16), jax.ShapeDtypeStruct((n, nq), BF16)],
        compiler_params=_params(("parallel",)),
    )(pq, qg, gq, cos_b, ss_b)


def _qprep_bwd(pq, dcq, dqb, qg, gq, cos_b, ss_b, tr):
    n = pq.shape[0]
    nq = GQA_HEADS * GQA_HEAD_DIM

    def body(p_ref, dcq_ref, dqb_ref, qg_ref, gq_ref, cb, sb, dp_ref, dqg_ref, dgq_ref):
        _acc_init(pl.program_id(0), [dqg_ref, dgq_ref])
        xh, r = _rms(p_ref[:, :MLA_Q_LORA])
        dn = dcq_ref[...]
        dqg_ref[...] += _csum(dn * xh)
        dp_ref[:, :MLA_Q_LORA] = _rms_bwd(xh, r, dn * qg_ref[...]).astype(BF16)
        for h in range(GQA_HEADS):
            lo = MLA_Q_LORA + h * GQA_HEAD_DIM
            qh, rq = _rms(p_ref[:, lo : lo + GQA_HEAD_DIM])
            dq = _rope_t(dqb_ref[:, h * GQA_HEAD_DIM : (h + 1) * GQA_HEAD_DIM], cb[...], sb[...], _QB)
            dgq_ref[...] += _csum(dq * qh)
            dp_ref[:, lo : lo + GQA_HEAD_DIM] = _rms_bwd(qh, rq, dq * gq_ref[...]).astype(BF16)

    return pl.pallas_call(
        body,
        name="qprep_bwd",
        grid=(n // tr,),
        in_specs=[_rows(tr, QC), _rows(tr, MLA_Q_LORA), _rows(tr, nq), _bcast(MLA_Q_LORA), _bcast(GQA_HEAD_DIM),
                  _rows(tr, LANES), _rows(tr, LANES)],
        out_specs=[_rows(tr, QC), _bcast(MLA_Q_LORA), _bcast(GQA_HEAD_DIM)],
        out_shape=[jax.ShapeDtypeStruct((n, QC), BF16), jax.ShapeDtypeStruct((1, MLA_Q_LORA), F32),
                   jax.ShapeDtypeStruct((1, GQA_HEAD_DIM), F32)],
        compiler_params=_params(("arbitrary",)),
    )(pq, dcq, dqb, qg, gq, cos_b, ss_b)


_QA_COLS = MLA_HEADS * (MLA_NOPE + MLA_ROPE)


def _qrope_fwd(qa, cos_a, ss_a, tr):
    n = qa.shape[0]

    def body(q_ref, ca, sa, o_ref):
        for j in range(MLA_HEADS // 2):
            lo = 3 * j * LANES
            o_ref[:, lo : lo + 2 * LANES] = q_ref[:, lo : lo + 2 * LANES].astype(BF16)
            o_ref[:, lo + 2 * LANES : lo + 3 * LANES] = _rope(q_ref[:, lo + 2 * LANES : lo + 3 * LANES], ca[...], sa[...], _QA).astype(BF16)

    return pl.pallas_call(
        body,
        name="qrope_fwd",
        grid=(n // tr,),
        in_specs=[_rows(tr, _QA_COLS), _rows(tr, LANES), _rows(tr, LANES)],
        out_specs=_rows(tr, _QA_COLS),
        out_shape=jax.ShapeDtypeStruct((n, _QA_COLS), BF16),
        compiler_params=_params(("parallel",)),
    )(qa, cos_a, ss_a)


def _qrope_bwd(dq2, cos_a, ss_a, tr):
    n = dq2.shape[0]

    def body(d_ref, ca, sa, o_ref):
        for j in range(MLA_HEADS // 2):
            lo = 3 * j * LANES
            h0, h1 = 2 * j, 2 * j + 1
            o_ref[:, lo : lo + LANES] = d_ref[:, 2 * h0 * LANES : (2 * h0 + 1) * LANES].astype(BF16)
            o_ref[:, lo + LANES : lo + 2 * LANES] = d_ref[:, 2 * h1 * LANES : (2 * h1 + 1) * LANES].astype(BF16)
            pe = d_ref[:, (2 * h0 + 1) * LANES : (2 * h0 + 2) * LANES] + d_ref[:, (2 * h1 + 1) * LANES : (2 * h1 + 2) * LANES]
            o_ref[:, lo + 2 * LANES : lo + 3 * LANES] = _rope_t(pe, ca[...], sa[...], _QA).astype(BF16)

    return pl.pallas_call(
        body,
        name="qrope_bwd",
        grid=(n // tr,),
        in_specs=[_rows(tr, MLA_HEADS * 2 * LANES), _rows(tr, LANES), _rows(tr, LANES)],
        out_specs=_rows(tr, _QA_COLS),
        out_shape=jax.ShapeDtypeStruct((n, _QA_COLS), BF16),
        compiler_params=_params(("parallel",)),
    )(dq2, cos_a, ss_a)


def _cat(refs):
    vals = [r[...] for r in refs]
    return vals[0] if len(vals) == 1 else jnp.concatenate(vals, axis=-1)


LOG2E = 1.4426950408889634


def _attn_fwd(qparts, kparts, vpart, n_heads, group, dv, scale, name, tq, after=None):
    T, Tk = qparts[0][0].shape[0], kparts[0][0].shape[0]
    nq_, nk_ = len(qparts), len(kparts)
    sub = min(tq, 256)
    c2 = scale * LOG2E

    def body(*refs):
        q_refs, k_refs = refs[:nq_], refs[nq_ : nq_ + nk_]
        v_ref = refs[nq_ + nk_]
        o_ref, lse_ref = refs[-2:]
        k = _cat(k_refs)
        v = v_ref[...]
        for r0 in range(0, tq, sub):
            q = _cat([r.at[r0 : r0 + sub, :] for r in q_refs])
            s = lax.dot_general(q, k, _DIMS["NT"], preferred_element_type=F32)
            m = jnp.max(s, axis=-1, keepdims=True)
            p = jnp.exp2((s - m) * c2)
            l = jnp.sum(p, axis=-1, keepdims=True)
            acc = jnp.dot(p.astype(BF16), v, preferred_element_type=F32)
            o_ref[r0 : r0 + sub, :] = (acc * (1.0 / l)).astype(BF16)
            lse_ref[r0 : r0 + sub, :] = m * scale + jnp.log(l)

    in_specs = [pl.BlockSpec((tq, LANES), lambda h, i, f=f: (i, f(h))) for _, f in qparts]
    in_specs += [pl.BlockSpec((Tk, LANES), lambda h, i, f=f: (0, f(h // group))) for _, f in kparts]
    fv = vpart[1]
    in_specs.append(pl.BlockSpec((Tk, dv), lambda h, i: (0, fv(h // group))))
    args = [*[a for a, _ in qparts], *[a for a, _ in kparts], vpart[0]]
    if after is not None:
        in_specs.append(pl.BlockSpec(after.shape, lambda h, i: (0, 0)))
        args.append(after)
    return pl.pallas_call(
        body,
        name=name,
        grid=(n_heads, T // tq),
        in_specs=in_specs,
        out_specs=[pl.BlockSpec((tq, dv), lambda h, i: (i, h)), pl.BlockSpec((None, tq, 1), lambda h, i: (h, i, 0))],
        out_shape=[jax.ShapeDtypeStruct((T, n_heads * dv), BF16), jax.ShapeDtypeStruct((n_heads, T, 1), F32)],
        compiler_params=_params(("parallel", "parallel")),
    )(*args)


def _attn_bwd(qparts, kparts, vpart, o, do, lse, n_heads, group, dv, scale, name, tq):
    T, Tk = qparts[0][0].shape[0], kparts[0][0].shape[0]
    nq_, nk_ = len(qparts), len(kparts)
    dk_ = LANES * nq_
    n_kv = n_heads // group
    nblk = T // tq
    c2 = scale * LOG2E

    def head(hk, i):
        return hk * group + i // nblk

    sub = min(tq, 256)

    def body(*refs):
        q_refs = refs[:nq_]
        k = _cat(refs[nq_ : nq_ + nk_])
        v_ref, o_ref, do_ref, lse_ref, dq_ref, dk_ref, dv_ref = refs[nq_ + nk_ :]
        i = pl.program_id(1)
        _acc_init(i, [dk_ref, dv_ref])
        v = v_ref[...]
        dk_acc, dv_acc = None, None
        for r0 in range(0, tq, sub):
            rows = slice(r0, r0 + sub)
            q = _cat([r.at[rows, :] for r in q_refs])
            s = lax.dot_general(q, k, _DIMS["NT"], preferred_element_type=F32)
            p = jnp.exp2(s * c2 - lse_ref[rows, :] * LOG2E)
            dov = do_ref[rows, :]
            dp = lax.dot_general(dov, v, _DIMS["NT"], preferred_element_type=F32)
            delta = jnp.sum(dov.astype(F32) * o_ref[rows, :].astype(F32), axis=-1, keepdims=True)
            ds = (p * (dp - delta)).astype(BF16)
            dq_ref[rows, :] = jnp.dot(ds, k, preferred_element_type=F32) * scale
            dk_part = lax.dot_general(ds, q, _DIMS["TN"], preferred_element_type=F32)
            dv_part = lax.dot_general(p.astype(BF16), dov, _DIMS["TN"], preferred_element_type=F32)
            dk_acc = dk_part if dk_acc is None else dk_acc + dk_part
            dv_acc = dv_part if dv_acc is None else dv_acc + dv_part
        dk_ref[...] += dk_acc
        dv_ref[...] += dv_acc

        @pl.when(i == group * nblk - 1)
        def _():
            dk_ref[...] *= scale

    in_specs = [pl.BlockSpec((tq, LANES), lambda hk, i, f=f: (i % nblk, f(head(hk, i)))) for _, f in qparts]
    in_specs += [pl.BlockSpec((Tk, LANES), lambda hk, i, f=f: (0, f(hk))) for _, f in kparts]
    fv = vpart[1]
    in_specs.append(pl.BlockSpec((Tk, dv), lambda hk, i: (0, fv(hk))))
    in_specs += [pl.BlockSpec((tq, dv), lambda hk, i: (i % nblk, head(hk, i)))] * 2
    in_specs.append(pl.BlockSpec((None, tq, 1), lambda hk, i: (head(hk, i), i % nblk, 0)))
    return pl.pallas_call(
        body,
        name=name,
        grid=(n_kv, group * nblk),
        in_specs=in_specs,
        out_specs=[pl.BlockSpec((tq, dk_), lambda hk, i: (i % nblk, head(hk, i))),
                   pl.BlockSpec((Tk, dk_), lambda hk, i: (0, hk)),
                   pl.BlockSpec((Tk, dv), lambda hk, i: (0, hk))],
        out_shape=[jax.ShapeDtypeStruct((T, n_heads * dk_), F32), jax.ShapeDtypeStruct((Tk, n_kv * dk_), F32),
                   jax.ShapeDtypeStruct((Tk, n_kv * dv), F32)],
        compiler_params=_params(("parallel", "arbitrary")),
    )(*[a for a, _ in qparts], *[a for a, _ in kparts], vpart[0], o, do, lse)


def _gates_fwd(pg, ya, yb, tr):
    n, d = ya.shape

    def body(pg_ref, ya_ref, yb_ref, o_ref):
        ga = jax.nn.sigmoid(pg_ref[:, :d].astype(F32))
        gb = jax.nn.sigmoid(pg_ref[:, d:].astype(F32))
        o_ref[...] = (ga * ya_ref[...].astype(F32) + gb * yb_ref[...].astype(F32)).astype(BF16)

    return pl.pallas_call(
        body,
        name="gates_fwd",
        grid=(n // tr,),
        in_specs=[_rows(tr, 2 * d), _rows(tr, d), _rows(tr, d)],
        out_specs=_rows(tr, d),
        out_shape=jax.ShapeDtypeStruct((n, d), BF16),
        compiler_params=_params(("parallel",)),
    )(pg, ya, yb)


def _gates_bwd(dm, pg, ya, yb, tr):
    n, d = ya.shape

    def body(dm_ref, pg_ref, ya_ref, yb_ref, dya_ref, dyb_ref, dpg_ref):
        dmv = dm_ref[...].astype(F32)
        ga = jax.nn.sigmoid(pg_ref[:, :d].astype(F32))
        gb = jax.nn.sigmoid(pg_ref[:, d:].astype(F32))
        dya_ref[...] = (dmv * ga).astype(BF16)
        dyb_ref[...] = (dmv * gb).astype(BF16)
        dpg_ref[:, :d] = (dmv * ya_ref[...].astype(F32) * ga * (1.0 - ga)).astype(BF16)
        dpg_ref[:, d:] = (dmv * yb_ref[...].astype(F32) * gb * (1.0 - gb)).astype(BF16)

    return pl.pallas_call(
        body,
        name="gates_bwd",
        grid=(n // tr,),
        in_specs=[_rows(tr, d), _rows(tr, 2 * d), _rows(tr, d), _rows(tr, d)],
        out_specs=[_rows(tr, d), _rows(tr, d), _rows(tr, 2 * d)],
        out_shape=[jax.ShapeDtypeStruct((n, d), BF16), jax.ShapeDtypeStruct((n, d), BF16), jax.ShapeDtypeStruct((n, 2 * d), BF16)],
        compiler_params=_params(("parallel",)),
    )(dm, pg, ya, yb)


def _resid_norm2_fwd(x2d, att, g1, n2g, sh2, sc2, tr):
    n, d = x2d.shape

    def body(x_ref, a_ref, g1_ref, g_ref, sh_ref, sc_ref, x1_ref, z_ref):
        x1 = x_ref[...] + g1_ref[...] * a_ref[...]
        x1_ref[...] = x1
        xh, _ = _rms(x1)
        z_ref[...] = ((xh * g_ref[...]) * (1.0 + sc_ref[...]) + sh_ref[...]).astype(BF16)

    return pl.pallas_call(
        body,
        name="resid_norm2_fwd",
        grid=(n // tr,),
        in_specs=[_rows(tr, d), _rows(tr, d)] + [_bcast(d)] * 4,
        out_specs=[_rows(tr, d), _rows(tr, d)],
        out_shape=[jax.ShapeDtypeStruct((n, d), F32), jax.ShapeDtypeStruct((n, d), BF16)],
        compiler_params=_params(("parallel",)),
    )(x2d, att, g1, n2g, sh2, sc2)


def _resid_norm2_bwd(dz2, x1, dx2, att, n2g, sc2, g1, tr):
    n, d = x1.shape

    def body(dz_ref, x1_ref, dx2_ref, a_ref, g_ref, sc_ref, g1_ref, dx1_ref, da_ref, dg_ref, dsh_ref, dsc_ref, dg1_ref):
        _acc_init(pl.program_id(0), [dg_ref, dsh_ref, dsc_ref, dg1_ref])
        xh, r = _rms(x1_ref[...])
        dzv = dz_ref[...]
        gv = g_ref[...]
        dsc_ref[...] += _csum(dzv * (xh * gv))
        dsh_ref[...] += _csum(dzv)
        dh = dzv * (1.0 + sc_ref[...])
        dg_ref[...] += _csum(dh * xh)
        dx1 = _rms_bwd(xh, r, dh * gv) + dx2_ref[...]
        dx1_ref[...] = dx1
        dg1_ref[...] += _csum(dx1 * a_ref[...])
        da_ref[...] = (dx1 * g1_ref[...]).astype(BF16)

    return pl.pallas_call(
        body,
        name="resid_norm2_bwd",
        grid=(n // tr,),
        in_specs=[_rows(tr, d)] * 4 + [_bcast(d)] * 3,
        out_specs=[_rows(tr, d), _rows(tr, d)] + [_bcast(d)] * 4,
        out_shape=[jax.ShapeDtypeStruct((n, d), F32), jax.ShapeDtypeStruct((n, d), BF16)] + [jax.ShapeDtypeStruct((1, d), F32)] * 4,
        compiler_params=_params(("arbitrary",)),
    )(dz2, x1, dx2, att, n2g, sc2, g1)


def _edges(shape):
    row = lax.broadcasted_iota(jnp.int32, shape, 0)
    return row == 0, row == shape[0] - 1


def _shifts(u, edges):
    n = u.shape[0]
    return jnp.where(edges[0], 0.0, pltpu.roll(u, 1, 0)), jnp.where(edges[1], 0.0, pltpu.roll(u, n - 1, 0))


def _conv3(u, prev, nxt, w_ref, b_ref):
    return b_ref[...] + w_ref[0:1, :] * prev + w_ref[1:2, :] * u + w_ref[2:3, :] * nxt


def _ffn_up_conv(z, wup_t, cw, cb, tc, after):
    n, d = z.shape
    f = wup_t.shape[0] // 2
    nb = f // tc

    def body(z_ref, wa_ref, wb_ref, cwa, cwb, cba, cbb, after_ref, ua_ref, ub_ref, h_ref):
        w = jnp.concatenate([wa_ref[...], wb_ref[...]], axis=0)
        u = lax.dot_general(z_ref[...], w, _DIMS["NT"], preferred_element_type=F32).astype(BF16)
        ua_ref[...] = u[:, :tc]
        ub_ref[...] = u[:, tc:]
        edges = _edges((n, tc))
        ua = u[:, :tc].astype(F32)
        ub = u[:, tc:].astype(F32)
        a = _conv3(ua, *_shifts(ua, edges), cwa, cba)
        b = _conv3(ub, *_shifts(ub, edges), cwb, cbb)
        h_ref[...] = (a * jax.nn.sigmoid(a) * b).astype(BF16)

    col = lambda rows, off: pl.BlockSpec((rows, tc), lambda i: (0, i + off))
    w_rows = lambda off: pl.BlockSpec((tc, d), lambda i: (i + off, 0))
    return pl.pallas_call(
        body,
        name="ffn_up_conv",
        grid=(nb,),
        in_specs=[pl.BlockSpec((n, d), lambda i: (0, 0)), w_rows(0), w_rows(nb), col(3, 0), col(3, nb), col(1, 0), col(1, nb),
                  pl.BlockSpec(after.shape, lambda i: (0, 0))],
        out_specs=[col(n, 0)] * 3,
        out_shape=[jax.ShapeDtypeStruct((n, f), BF16)] * 3,
        compiler_params=_params(("parallel",)),
    )(z, wup_t, wup_t, cw, cw, cb, cb, after)


def _ffn_down_dx_conv_bwd(df, wdown, u_a, u_b, cw, cb, tc, after):
    n, f = u_a.shape
    d = df.shape[1]
    nb = f // tc

    def part(uv, prev, nxt, duc, edges, w_ref, du_ref, dw_ref, db_ref):
        db_ref[...] = _csum(duc)
        dw_ref[0:1, :] = _csum(duc * prev)
        dw_ref[1:2, :] = _csum(duc * uv)
        dw_ref[2:3, :] = _csum(duc * nxt)
        d_prev, d_next = _shifts(duc, edges)
        du_ref[...] = (w_ref[0:1, :] * d_next + w_ref[1:2, :] * duc + w_ref[2:3, :] * d_prev).astype(BF16)

    def body(df_ref, wd_ref, ua_ref, ub_ref, wa_ref, wb_ref, ba_ref, bb_ref, after_ref,
             dua_ref, dub_ref, dwa_ref, dwb_ref, dba_ref, dbb_ref):
        dhv = lax.dot_general(df_ref[...], wd_ref[...], _DIMS["NT"], preferred_element_type=F32)
        dhv = dhv.astype(BF16).astype(F32)
        edges = _edges((n, tc))
        ua = ua_ref[...].astype(F32)
        ub = ub_ref[...].astype(F32)
        sa = _shifts(ua, edges)
        sb = _shifts(ub, edges)
        a = _conv3(ua, *sa, wa_ref, ba_ref)
        b = _conv3(ub, *sb, wb_ref, bb_ref)
        sg = jax.nn.sigmoid(a)
        da = dhv * b * (sg * (1.0 + a * (1.0 - sg)))
        db = dhv * (a * sg)
        part(ua, *sa, da, edges, wa_ref, dua_ref, dwa_ref, dba_ref)
        part(ub, *sb, db, edges, wb_ref, dub_ref, dwb_ref, dbb_ref)

    col = lambda rows, off: pl.BlockSpec((rows, tc), lambda i: (0, i + off))
    return pl.pallas_call(
        body,
        name="ffn_down_dx_conv_bwd",
        grid=(nb,),
        in_specs=[pl.BlockSpec((n, d), lambda i: (0, 0)), pl.BlockSpec((tc, d), lambda i: (i, 0)), col(n, 0), col(n, 0),
                  col(3, 0), col(3, nb), col(1, 0), col(1, nb), pl.BlockSpec(after.shape, lambda i: (0, 0))],
        out_specs=[col(n, 0), col(n, 0), col(3, 0), col(3, 0), col(1, 0), col(1, 0)],
        out_shape=[jax.ShapeDtypeStruct((n, f), BF16)] * 2 + [jax.ShapeDtypeStruct((3, f), F32)] * 2 + [jax.ShapeDtypeStruct((1, f), F32)] * 2,
        compiler_params=_params(("parallel",)),
    )(df, wdown, u_a, u_b, cw, cw, cb, cb, after)


def _loss_head(x1, f, g2, fg, tgt, tr):
    n, d = x1.shape

    def body(x1_ref, f_ref, g2_ref, fg_ref, t_ref, sq_ref, dx2_ref, dfg_ref, dg2_ref, df_ref):
        _acc_init(pl.program_id(0), [sq_ref, dfg_ref, dg2_ref])
        fv = f_ref[...]
        xh, r = _rms(x1_ref[...] + g2_ref[...] * fv)
        err = xh * fg_ref[...] - t_ref[...]
        sq_ref[...] += _csum(err * err)
        dy = err * (1.0 / d)
        dfg_ref[...] += _csum(dy * xh)
        dx2 = _rms_bwd(xh, r, dy * fg_ref[...])
        dx2_ref[...] = dx2
        dg2_ref[...] += _csum(dx2 * fv)
        df_ref[...] = (dx2 * g2_ref[...]).astype(BF16)

    return pl.pallas_call(
        body,
        name="loss_head",
        grid=(n // tr,),
        in_specs=[_rows(tr, d), _rows(tr, d), _bcast(d), _bcast(d), _rows(tr, d)],
        out_specs=[_bcast(d), _rows(tr, d), _bcast(d), _bcast(d), _rows(tr, d)],
        out_shape=[jax.ShapeDtypeStruct((1, d), F32), jax.ShapeDtypeStruct((n, d), F32), jax.ShapeDtypeStruct((1, d), F32),
                   jax.ShapeDtypeStruct((1, d), F32), jax.ShapeDtypeStruct((n, d), BF16)],
        compiler_params=_params(("arbitrary",)),
    )(x1, f, g2, fg, tgt)


def _sum_slots(g, name):
    s, r, w = g.shape

    def body(g_ref, o_ref):
        acc = g_ref[0]
        for k in range(1, s):
            acc = acc + g_ref[k]
        o_ref[...] = acc

    return pl.pallas_call(body, name=name, out_shape=jax.ShapeDtypeStruct((r, w), F32))(g)


def _silu_grad_mul(ds, cvec):
    def body(d_ref, c_ref, o_ref):
        cv = c_ref[...]
        sg = jax.nn.sigmoid(cv)
        o_ref[...] = d_ref[...] * (sg * (1.0 + cv * (1.0 - sg)))

    return pl.pallas_call(body, name="silu_grad_mul", out_shape=jax.ShapeDtypeStruct(ds.shape, F32))(ds, cvec)


def _adamw_update(wv, gv, mv, vv, d_ref, mo_ref, vo_ref):
    mn = ADAM_B1 * mv + (1.0 - ADAM_B1) * gv
    vn = ADAM_B2 * vv + (1.0 - ADAM_B2) * (gv * gv)
    mo_ref[...] = mn
    vo_ref[...] = vn
    m_hat = mn / (1.0 - ADAM_B1**ADAM_STEP)
    v_hat = vn / (1.0 - ADAM_B2**ADAM_STEP)
    d_ref[...] = -ADAM_LR * (m_hat / (jnp.sqrt(v_hat) + ADAM_EPS) + ADAM_WD * wv)


def _adamw_many(ws, gs, ms, vs, name):
    n = len(ws)

    def body(*refs):
        for k in range(n):
            w_ref, g_ref, m_ref, v_ref = (refs[q * n + k] for q in range(4))
            d_ref, mo_ref, vo_ref = (refs[(4 + q) * n + k] for q in range(3))
            _adamw_update(w_ref[...], g_ref[...], m_ref[...], v_ref[...], d_ref, mo_ref, vo_ref)

    res = pl.pallas_call(body, name=name, out_shape=[jax.ShapeDtypeStruct(w.shape, F32) for w in ws] * 3)(*ws, *gs, *ms, *vs)
    return res[:n], res[n : 2 * n], res[2 * n :]


def _adamw(w, g, m, v, name, g_transposed=False, g_sibling=None):
    r, cdim = w.shape
    halves = g_sibling is not None
    block = 1 << 19
    if g_transposed:
        tc = _pick(cdim // 2 if halves else cdim, 2048)
        tr = _pick(r, max(LANES, block // tc), LANES)
        per_half = (cdim // 2) // tc
    else:
        rows = r // 2 if halves else r
        tc = _pick(cdim, 2048)
        tr = _pick(rows, max(8, block // tc), 8)
        if tr < 64 and rows > 64:
            tr, tc = _pick(rows, 1024, 8), _pick(cdim, 512)
        per_half = (r // 2) // tr
    emit_g = g_transposed or halves

    def body(w_ref, g_ref, *rest):
        m_ref, v_ref = rest[halves : halves + 2]
        outs = rest[halves + 2 :]
        gv = g_ref[...]
        if halves:
            along = pl.program_id(1 if g_transposed else 0)
            gv = jnp.where(along // per_half == lax.axis_index("c"), gv, rest[0][...])
        if g_transposed:
            gv = gv.T
        if emit_g:
            outs[0][...] = gv
        _adamw_update(w_ref[...], gv, m_ref[...], v_ref[...], *outs[-3:])

    spec = pl.BlockSpec((tr, tc), lambda i, j: (i, j))
    if g_transposed:
        g_spec = pl.BlockSpec((tc, tr), lambda i, j: (j % per_half if halves else j, i))
    else:
        g_spec = pl.BlockSpec((tr, tc), lambda i, j: (i % per_half if halves else i, j))
    n_out = 3 + emit_g
    res = pl.pallas_call(
        body,
        name=name,
        grid=(r // tr, cdim // tc),
        in_specs=[spec, g_spec] + [g_spec] * halves + [spec, spec],
        out_specs=[spec] * n_out,
        out_shape=[jax.ShapeDtypeStruct((r, cdim), F32)] * n_out,
        compiler_params=_params(("parallel", "parallel")),
    )(w, g, *([g_sibling] if halves else []), m, v)
    return res if emit_g else [g, *res]


def _place():
    return lax.axis_index("x"), lax.axis_index("y"), lax.axis_index("c")


def _remote(src, dst, send_sem, recv_sem, dev):
    return pltpu.make_async_remote_copy(src_ref=src, dst_ref=dst, send_sem=send_sem, recv_sem=recv_sem, device_id=dev, device_id_type=MESH)


ANY = pl.BlockSpec(memory_space=pl.ANY)


def _all_gather_small(v, name):
    r, w = v.shape

    def body(v_ref, o_ref, send, recv, lsem):
        x, y, c = _place()
        me = 4 * x + 2 * y + c
        mine = pltpu.make_async_copy(v_ref, o_ref.at[me], lsem)
        mine.start()
        sent = []
        for k in range(1, 8):
            px, py, pc = x ^ (k >> 2), y ^ ((k >> 1) & 1), c ^ (k & 1)
            cp = _remote(v_ref, o_ref.at[me], send.at[k - 1], recv.at[k - 1], (px, py, pc))
            cp.start()
            sent.append(cp)
        for k in range(1, 8):
            px, py, pc = x ^ (k >> 2), y ^ ((k >> 1) & 1), c ^ (k & 1)
            slot = o_ref.at[4 * px + 2 * py + pc]
            _remote(slot, slot, send.at[k - 1], recv.at[k - 1], (x, y, c)).wait_recv()
        for cp in sent:
            cp.wait_send()
        mine.wait()

    return pl.pallas_call(
        body,
        name=name,
        out_shape=jax.ShapeDtypeStruct((8, r, w), F32),
        in_specs=[pl.BlockSpec(memory_space=pltpu.VMEM)],
        out_specs=pl.BlockSpec(memory_space=pltpu.VMEM),
        scratch_shapes=[pltpu.SemaphoreType.DMA((7,)), pltpu.SemaphoreType.DMA((7,)), pltpu.SemaphoreType.DMA],
        compiler_params=pltpu.CompilerParams(vmem_limit_bytes=VMEM_LIMIT),
    )(v)


HBM = pl.BlockSpec(memory_space=pltpu.HBM)
SEM = pl.BlockSpec(memory_space=pltpu.SEMAPHORE)
EFFECT = pltpu.SideEffectType.DATAFLOW_SIDE_EFFECTING


def _other_chips(x, y):
    return [(1 - x, y), (x, 1 - y), (1 - x, 1 - y)]


def _bulk_start(name, srcs, land_shapes, n_copies, copies, after, lands_init=None):
    n, m = len(srcs), len(land_shapes)

    def body(*refs):
        src_refs, land_refs = refs[:n], refs[n : n + m]
        send, recv = refs[n + m + 1], refs[n + m + 2]
        token = refs[-1]
        for k, (s, d, dev) in enumerate(copies(src_refs, land_refs)):
            _remote(s, d, send.at[k], recv.at[k], dev).start()
        token[...] = jnp.zeros_like(token)

    lands = lands_init or [lax.empty(s.shape, s.dtype) for s in land_shapes]
    lands = [pltpu.with_memory_space_constraint(b, pltpu.HBM) for b in lands]
    out = pl.pallas_call(
        body,
        name=name,
        out_shape=(pltpu.SemaphoreType.DMA((n_copies,)), pltpu.SemaphoreType.DMA((n_copies,)),
                   *[pltpu.HBM(s.shape, s.dtype) for s in srcs], *[pltpu.HBM(s.shape, s.dtype) for s in land_shapes],
                   jax.ShapeDtypeStruct((8, LANES), F32)),
        in_specs=[HBM] * (n + m) + [ANY],
        out_specs=(SEM, SEM, *[HBM] * (n + m), pl.BlockSpec(memory_space=pltpu.VMEM)),
        input_output_aliases={i: 2 + i for i in range(n + m)},
        compiler_params=pltpu.CompilerParams(has_side_effects=EFFECT),
    )(*[pltpu.with_memory_space_constraint(s, pltpu.HBM) for s in srcs], *lands, after)
    return out[0], out[1], list(out[2 : 2 + n]), list(out[2 + n : 2 + n + m]), out[-1][0:1, 0:1]


def _bulk_wait(name, send, recv, srcs, lands, after, waits):
    n, m = len(srcs), len(lands)

    def body(*refs):
        src_refs, land_refs = refs[:n], refs[n : n + m]
        send_sem, recv_sem = refs[n + m], refs[n + m + 1]
        x, y, c = _place()
        for k, (s, d) in enumerate(waits(src_refs, land_refs)):
            cp = _remote(s, d, send_sem.at[k], recv_sem.at[k], (x, y, c))
            cp.wait_send()
            cp.wait_recv()

    out = pl.pallas_call(
        body,
        name=name,
        out_shape=tuple(pltpu.HBM(s.shape, s.dtype) for s in (*srcs, *lands)),
        in_specs=[HBM] * (n + m) + [SEM, SEM, ANY],
        out_specs=tuple([HBM] * (n + m)),
        input_output_aliases={i: i for i in range(n + m)},
        compiler_params=pltpu.CompilerParams(has_side_effects=EFFECT),
    )(*srcs, *lands, send, recv, after)
    return list(out[:n]), list(out[n:])


def _gather_start(shards, after, name, own_placed=False):
    def copies(src, land):
        x, y, c = _place()
        j = 2 * x + y
        return [(src[a].at[c], land[a].at[j, c], (px, py, c)) for a in range(len(shards)) for px, py in _other_chips(x, y)]

    shapes = [jax.ShapeDtypeStruct((4,) + s.shape, s.dtype) for s in shards]
    init = None
    if own_placed:
        j = 2 * lax.axis_index("x") + lax.axis_index("y")
        init = [lax.dynamic_update_slice(lax.empty(t.shape, t.dtype), s[None], (j, 0, 0, 0)) for t, s in zip(shapes, shards)]
    return _bulk_start(name, shards, shapes, 3 * len(shards), copies, after, init)


def _gather_wait(started, after, name):
    send, recv, srcs, lands, _ = started

    def waits(src, land):
        x, y, c = _place()
        return [(src[a].at[c], land[a].at[2 * px + py, c]) for a in range(len(srcs)) for px, py in _other_chips(x, y)]

    return _bulk_wait(name, send, recv, srcs, lands, after, waits)


def _forward_halves(lands, name):
    n = len(lands)

    def body(*refs):
        bufs = refs[n : 2 * n]
        send, recv = refs[2 * n :]
        x, y, c = _place()
        started = []
        for a in range(n):
            for k, (px, py) in enumerate(_other_chips(x, y)):
                blk = bufs[a].at[2 * px + py, c]
                cp = _remote(blk, blk, send.at[3 * a + k], recv.at[3 * a + k], (x, y, 1 - c))
                cp.start()
                started.append(cp)
        for a in range(n):
            for k, (px, py) in enumerate(_other_chips(x, y)):
                blk = bufs[a].at[2 * px + py, 1 - c]
                _remote(blk, blk, send.at[3 * a + k], recv.at[3 * a + k], (x, y, c)).wait_recv()
        for cp in started:
            cp.wait_send()

    return pl.pallas_call(
        body,
        name=name,
        out_shape=[jax.ShapeDtypeStruct(b.shape, b.dtype) for b in lands],
        in_specs=[ANY] * n,
        out_specs=[ANY] * n,
        input_output_aliases={i: i for i in range(n)},
        scratch_shapes=[pltpu.SemaphoreType.DMA((3 * n,)), pltpu.SemaphoreType.DMA((3 * n,))],
    )(*lands)


def _forward_start(lands, after, name):
    def copies(src, _):
        x, y, c = _place()
        blocks = [src[a].at[2 * px + py, c] for a in range(len(lands)) for px, py in _other_chips(x, y)]
        return [(b, b, (x, y, 1 - c)) for b in blocks]

    return _bulk_start(name, lands, [], 3 * len(lands), copies, after)


def _forward_wait(started, after, name):
    send, recv, bufs, _, _ = started

    def waits(src, _):
        x, y, c = _place()
        return [(src[a].at[2 * px + py, c], src[a].at[2 * px + py, 1 - c]) for a in range(len(bufs)) for px, py in _other_chips(x, y)]

    return _bulk_wait(name, send, recv, bufs, [], after, waits)[0]


def _place_own(shards, lands):
    j = 2 * lax.axis_index("x") + lax.axis_index("y")
    full = [lax.dynamic_update_slice(b, s[None], (j, 0, 0, 0)) for b, s in zip(lands, shards)]
    return [f.reshape(4 * f.shape[2] * 2, f.shape[3]) for f in full]


def _gather_finish(started, after, tag, own_placed=False):
    shards, lands = _gather_wait(started, after, "gather_wait_" + tag)
    lands = _forward_halves(lands, "gather_forward_" + tag)
    if own_placed:
        return [f.reshape(4 * f.shape[2] * 2, f.shape[3]) for f in lands]
    return _place_own(shards, lands)


def _gather_land(started, after, tag):
    shards, lands = _gather_wait(started, after, "gather_wait_" + tag)
    return shards, _forward_start(lands, shards[0], "forward_start_" + tag)


def _gather_done(landed, after, tag):
    shards, fwd = landed
    return _place_own(shards, _forward_wait(fwd, after, "forward_wait_" + tag))


def _swap_halves(grads, name):
    n = len(grads)

    def body(*refs):
        ins, outs = refs[:n], refs[n : 2 * n]
        send, recv = refs[2 * n :]
        x, y, c = _place()
        started = []
        for a in range(n):
            for s in range(4):
                cp = _remote(ins[a].at[s, 1 - c], outs[a].at[s], send.at[4 * a + s], recv.at[4 * a + s], (x, y, 1 - c))
                cp.start()
                started.append(cp)
        for cp in started:
            cp.wait_recv()
        for cp in started:
            cp.wait_send()

    return pl.pallas_call(
        body,
        name=name,
        out_shape=[jax.ShapeDtypeStruct((4,) + g.shape[2:], g.dtype) for g in grads],
        in_specs=[ANY] * n,
        out_specs=[ANY] * n,
        scratch_shapes=[pltpu.SemaphoreType.DMA((4 * n,)), pltpu.SemaphoreType.DMA((4 * n,))],
    )(*grads)


def _add_halves(grads, others, tag):
    outs = []
    for a, (g, o) in enumerate(zip(grads, others)):
        _, _, rh, cdim = g.shape
        tr = _pick(rh, 512, 16)

        def body(g_ref, o_ref, p_ref):
            p_ref[...] = (g_ref[...].astype(F32) + o_ref[...].astype(F32)).astype(BF16)

        outs.append(
            pl.pallas_call(
                body,
                name=f"add_halves_{tag}{a}",
                grid=(4, rh // tr),
                in_specs=[pl.BlockSpec((None, None, tr, cdim), lambda s, i: (s, lax.axis_index("c"), i, 0)),
                          pl.BlockSpec((None, tr, cdim), lambda s, i: (s, i, 0))],
                out_specs=pl.BlockSpec((None, tr, cdim), lambda s, i: (s, i, 0)),
                out_shape=jax.ShapeDtypeStruct((4, rh, cdim), BF16),
                compiler_params=_params(("parallel", "parallel")),
            )(g, o)
        )
    return outs


def _exchange_start(parts, after, name):
    def copies(src, land):
        x, y, c = _place()
        j = 2 * x + y
        return [(src[a].at[2 * px + py], land[a].at[j], (px, py, c)) for a in range(len(parts)) for px, py in _other_chips(x, y)]

    return _bulk_start(name, parts, [jax.ShapeDtypeStruct(p.shape, p.dtype) for p in parts], 3 * len(parts), copies, after)


def _exchange_finish(started, after, name):
    send, recv, srcs, lands, _ = started

    def waits(src, land):
        x, y, _ = _place()
        return [(src[a].at[2 * px + py], land[a].at[2 * px + py]) for a in range(len(srcs)) for px, py in _other_chips(x, y)]

    srcs, lands = _bulk_wait(name, send, recv, srcs, lands, after, waits)
    j = 2 * lax.axis_index("x") + lax.axis_index("y")
    return [lax.dynamic_update_slice(b, lax.dynamic_slice(p, (j, 0, 0), (1,) + p.shape[1:]), (j, 0, 0)) for b, p in zip(lands, srcs)]


def _sum_chips(recvd, tag):
    outs = []
    for a, g in enumerate(recvd):
        _, rh, cdim = g.shape
        tr = _pick(rh, 512, 16)

        def body(g_ref, o_ref):
            o_ref[...] = ((g_ref[0].astype(F32) + g_ref[1].astype(F32)) + g_ref[2].astype(F32)) + g_ref[3].astype(F32)

        outs.append(
            pl.pallas_call(
                body,
                name=f"sum_chips_{tag}{a}",
                grid=(rh // tr,),
                in_specs=[pl.BlockSpec((4, tr, cdim), lambda i: (0, i, 0))],
                out_specs=pl.BlockSpec((tr, cdim), lambda i: (i, 0)),
                out_shape=jax.ShapeDtypeStruct((rh, cdim), F32),
                compiler_params=_params(("parallel",)),
            )(g)
        )
    return outs


def _join_halves(halves, name):
    n = len(halves)

    def body(*refs):
        ins, outs = refs[:n], refs[n : 2 * n]
        send, recv = refs[2 * n :]
        x, y, c = _place()
        started = []
        for a in range(n):
            cp = _remote(ins[a], outs[a], send.at[a], recv.at[a], (x, y, 1 - c))
            cp.start()
            started.append(cp)
        for cp in started:
            cp.wait_recv()
        for cp in started:
            cp.wait_send()

    others = pl.pallas_call(
        body,
        name=name,
        out_shape=[jax.ShapeDtypeStruct(h.shape, h.dtype) for h in halves],
        in_specs=[ANY] * n,
        out_specs=[ANY] * n,
        scratch_shapes=[pltpu.SemaphoreType.DMA((n,)), pltpu.SemaphoreType.DMA((n,))],
    )(*halves)
    return list(zip(halves, others))


def _joined(mine, other):
    first = lax.axis_index("c") == 0
    return jnp.concatenate([jnp.where(first, mine, other), jnp.where(first, other, mine)], axis=0)


def _grad_views(grads):
    return [g.reshape(4, 2, g.shape[0] // 8, g.shape[1]) for g in grads]


def _scatter_start(grads, tag, after=None):
    views = _grad_views(grads)
    others = _swap_halves(views, "swap_halves_" + tag)
    mine = _add_halves(views, others, tag)
    return _exchange_start(mine, others[-1] if after is None else after, "exchange_start_" + tag)


def _swap_start(grads, after, tag):
    views = _grad_views(grads)

    def copies(src, land):
        x, y, c = _place()
        return [(src[a].at[s, 1 - c], land[a].at[s], (x, y, 1 - c)) for a in range(len(views)) for s in range(4)]

    shapes = [jax.ShapeDtypeStruct((4,) + v.shape[2:], v.dtype) for v in views]
    return _bulk_start("swap_start_" + tag, views, shapes, 4 * len(views), copies, after)


def _scatter_start_after_swap(swapped, after, tag):
    send, recv, views, lands, _ = swapped

    def waits(src, land):
        c = lax.axis_index("c")
        return [(src[a].at[s, 1 - c], land[a].at[s]) for a in range(len(views)) for s in range(4)]

    views, others = _bulk_wait("swap_wait_" + tag, send, recv, views, lands, after, waits)
    mine = _add_halves(views, others, tag)
    return _exchange_start(mine, others[-1], "exchange_start_" + tag)


def _join_start(halves, after, tag):
    def copies(src, land):
        x, y, c = _place()
        return [(src[a], land[a], (x, y, 1 - c)) for a in range(len(halves))]

    return _bulk_start("join_start_" + tag, halves, [jax.ShapeDtypeStruct(h.shape, h.dtype) for h in halves], len(halves), copies, after)


def _join_wait(started, after, tag):
    send, recv, halves, lands, _ = started
    halves, others = _bulk_wait("join_wait_" + tag, send, recv, halves, lands, after, lambda src, land: list(zip(src, land)))
    return list(zip(halves, others))


def _scatter_sums(started, after, tag):
    return _sum_chips(_exchange_finish(started, after, "exchange_wait_" + tag), tag)


def _scatter_finish(started, after, tag):
    return _join_halves(_scatter_sums(started, after, tag), "join_halves_" + tag)


def _t_bf16(w):
    return w.T.astype(BF16)


def kernel(x, c, ctx, c_ctx, w_ada, b_ada, norm1_g, w_in, mla_q_norm_g, w_q_up, mla_kv_norm_g, w_kv_up, gqa_q_norm_g, gqa_k_norm_g, w_br_a, w_br_b, w_out, norm2_g, w_up, conv_w, conv_b, w_down, final_norm_g, loss_target, m_c_ctx, m_w_ada, m_b_ada, m_norm1_g, m_w_in, m_mla_q_norm_g, m_w_q_up, m_mla_kv_norm_g, m_w_kv_up, m_gqa_q_norm_g, m_gqa_k_norm_g, m_w_br_a, m_w_br_b, m_w_out, m_norm2_g, m_w_up, m_conv_w, m_conv_b, m_w_down, m_final_norm_g, v_c_ctx, v_w_ada, v_b_ada, v_norm1_g, v_w_in, v_mla_q_norm_g, v_w_q_up, v_mla_kv_norm_g, v_w_kv_up, v_gqa_q_norm_g, v_gqa_k_norm_g, v_w_br_a, v_w_br_b, v_w_out, v_norm2_g, v_w_up, v_conv_w, v_conv_b, v_w_down, v_final_norm_g):
    T, D = x.shape[1], x.shape[2]
    C = ctx.shape[1]
    NA = w_ada.shape[2]
    NW = w_up.shape[2]
    F2 = 4 * NW
    FF = F2 // 2
    xi, yi, ci = _place()
    j = 2 * xi + yi
    me = 4 * xi + 2 * yi + ci
    tr = _pick(C, 256, 8)

    x2d, tgt, ctx2d = x[0], loss_target[0], ctx[0]
    fg = final_norm_g.reshape(1, D)
    cc = c_ctx.reshape(1, D)

    halve = lambda s: s.reshape(2, s.shape[0] // 2, s.shape[1])
    win_shard = halve(_t_bf16(w_in[0]))
    w0 = max(D, NW)
    pay = jnp.zeros((8, w0), F32).at[0:1, :D].set(c).at[1:4, :NW].set(conv_w[0])
    got = _all_gather_small(pay, "gather_cond")
    c_all = got[:, 0, :D]
    cw = jnp.concatenate([got[2 * s, 1:4, :NW] for s in range(4)], axis=1)
    s16 = jnp.concatenate([c_all, cc, jnp.zeros((7, D), F32)], axis=0)
    b_cols = lax.dynamic_slice(b_ada, (0, j * NA), (1, NA))
    ada_part = _mm(s16, w_ada[0], "NN", F32, "ada_fwd", act="silu", bias=b_cols)
    got = _all_gather_small(ada_part, "gather_ada")
    ada = jnp.concatenate([got[2 * s] for s in range(4)], axis=1)
    lat = lax.dynamic_slice(ada, (me, 0), (1, 6 * D))
    sh1, sc1, g1, sh2, sc2, g2 = [lat[:, k * D : (k + 1) * D] for k in range(6)]
    csh, csc = ada[8:9, :D], ada[8:9, D : 2 * D]

    ag_in = _gather_start([win_shard], got, "gather_start_in", own_placed=True)
    t_in = ag_in[4]
    wq3 = (w_q_up[0] + t_in).reshape(MLA_Q_LORA, 2, MLA_NOPE + MLA_ROPE)
    wq_perm = jnp.concatenate([wq3[:, :, :MLA_NOPE].reshape(MLA_Q_LORA, -1), wq3[:, :, MLA_NOPE:].reshape(MLA_Q_LORA, -1)], axis=1)
    low = [_t_bf16(wq_perm), _t_bf16(w_kv_up[0] + t_in)]
    br = [_t_bf16(w_br_a[0] + t_in), _t_bf16(w_br_b[0] + t_in), (w_out[0] + t_in).astype(BF16)]
    ag_low = _gather_start([halve(s) for s in low], t_in, "gather_start_low")
    ag_br = _gather_start([halve(s) for s in br], ag_low[4], "gather_start_br")
    ag_up = _gather_start([halve(_t_bf16(w_up[0] + t_in))], ag_br[4], "gather_start_up")
    ag_down = _gather_start([halve((w_down[0] + t_in).astype(BF16))], ag_up[4], "gather_start_down")
    sh1 = sh1 + ag_down[4]

    cos_a, ss_a = _rope_tables(C, T, MLA_ROPE)
    cos_b, ss_b = _rope_tables(C, T, GQA_HEAD_DIM)
    lcos_a, lss_a, lcos_b, lss_b = cos_a[:T], ss_a[:T], cos_b[:T], ss_b[:T]

    z_all = _norm_mod_fwd(x2d, norm1_g, sh1, sc1, "norm1_lat_fwd", tr, out_rows=T + C)
    z_all = _norm_mod_fwd(ctx2d, norm1_g, csh, csc, "norm1_ctx_fwd", tr, base=z_all, out_off=T)
    (win_t,) = _gather_finish(ag_in, z_all, "in", own_placed=True)
    kv_cols = KVP - LANES + MLA_ROPE
    e_kpe = MLA_KV_LORA + MLA_ROPE
    w_kvp = jnp.concatenate([win_t[:MLA_KV_LORA], win_t[e_kpe:kv_cols], win_t[MLA_KV_LORA:e_kpe], jnp.zeros((LANES - MLA_ROPE, D), BF16)], axis=0)

    pkv = _mm(z_all, w_kvp, "NT", F32, "proj_kv", tn=KVP)
    pq = _mm(z_all, win_t, "NT", F32, "proj_q", m=T, n=QC, b_off=kv_cols)
    low_landed = _gather_land(ag_low, pq, "low")
    pg = _mm(z_all, win_t, "NT", BF16, "proj_g", m=T, n=2 * D, b_off=kv_cols + QC, after=low_landed[1][4])
    wq_t, wkv_t = _gather_done(low_landed, pg, "low")
    ckv_n, kb2, vb2, kpe2 = _kprep_fwd(pkv, mla_kv_norm_g, gqa_k_norm_g, cos_a, ss_a, cos_b, ss_b, tr)
    kv_up = _mm(ckv_n, wkv_t, "NT", BF16, "kv_up")
    cq_n, qb2 = _qprep_fwd(pq, mla_q_norm_g, gqa_q_norm_g, lcos_b, lss_b, tr)
    q_a = _mm(cq_n, wq_t, "NT", F32, "q_up")
    qar = _qrope_fwd(q_a, lcos_a, lss_a, tr)

    a_q = [(qar, lambda h: 3 * (h // 2) + h % 2), (qar, lambda h: 3 * (h // 2) + 2)]
    a_k = [(kv_up, lambda h: 2 * h), (kpe2, lambda h: h % 2)]
    a_v = (kv_up, lambda h: 2 * h + 1)
    a_scale = float(MLA_NOPE + MLA_ROPE) ** -0.5
    b_q = [(qb2, lambda h: h)]
    b_k = [(kb2, lambda h: h)]
    b_v = (vb2, lambda h: h)
    b_scale = float(GQA_HEAD_DIM) ** -0.5
    tq_f = _pick(T, 2048)
    o_a, lse_a = _attn_fwd(a_q, a_k, a_v, MLA_HEADS, 1, MLA_V, a_scale, "attn_a_fwd", tq_f)
    br_landed = _gather_land(ag_br, o_a, "br")
    o_b, lse_b = _attn_fwd(b_q, b_k, b_v, GQA_HEADS, GQA_GROUP, GQA_HEAD_DIM, b_scale, "attn_b_fwd", tq_f, after=br_landed[1][4])
    wbra_t, wbrb_t, wout = _gather_done(br_landed, o_b, "br")
    up_landed = _gather_land(ag_up, o_b, "up")
    ya = _mm(o_a, wbra_t, "NT", BF16, "br_a", after=up_landed[1][4])
    yb = _mm(o_b, wbrb_t, "NT", BF16, "br_b")
    merged = _gates_fwd(pg, ya, yb, tr)
    att = _mm(merged, wout, "NN", F32, "out_proj")
    x1, z2 = _resid_norm2_fwd(x2d, att, g1, norm2_g, sh2, sc2, tr)
    (wup_t,) = _gather_done(up_landed, z2, "up")
    down_landed = _gather_land(ag_down, z2, "down")
    tc = _pick(FF, 128)
    u_a, u_b, hg = _ffn_up_conv(z2, wup_t, cw, conv_b, tc, down_landed[1][4])
    (wdown,) = _gather_done(down_landed, hg, "down")
    f = _mm(hg, wdown, "NN", F32, "ffn_down", tk=FF // 2)
    sq, dx2, d_fg, d_g2, df = _loss_head(x1, f, g2, fg, tgt, tr)
    loss = lax.psum(0.5 * jnp.sum(sq) / D, ("x", "y", "c"))

    du_a, du_b, dcw_a, dcw_b, dcb_a, dcb_b = _ffn_down_dx_conv_bwd(df, wdown, u_a, u_b, cw, conv_b, _pick(FF, 256), loss.reshape(1, 1))
    g_wdown = _mm(hg, df, "TN", BF16, "ffn_down_dw", tm=FF // 4)
    dz2 = _mm(du_a, wup_t, "NN", F32, "ffn_up_dx_a", tk=FF // 2)
    dz2 = _mm(du_b, wup_t, "NN", F32, "ffn_up_dx_b", b_off=FF, add=dz2, tk=FF // 2)
    g_wup_t = _mm(du_a, z2, "TN", BF16, "ffn_up_dw_a", out_rows=F2, tm=FF // 4)
    g_wup_t = _mm(du_b, z2, "TN", BF16, "ffn_up_dw_b", out_base=g_wup_t, out_off=FF, tm=FF // 4)
    sw_ffn = _swap_start([g_wdown, g_wup_t], sc2, "ffn")
    sc2 = sc2 + sw_ffn[4]
    dx1, datt, d_n2g, d_sh2, d_sc2, d_g1 = _resid_norm2_bwd(dz2, x1, dx2, att, norm2_g, sc2, g1, tr)

    dmerged = _mm(datt, wout, "NT", BF16, "out_proj_dx")
    rs_ffn = _scatter_start_after_swap(sw_ffn, dmerged, "ffn")
    lse_a = lse_a + rs_ffn[4]
    g_wout = _mm(merged, datt, "TN", BF16, "out_proj_dw")
    dya, dyb, dpg = _gates_bwd(dmerged, pg, ya, yb, tr)
    do_a = _mm(dya, wbra_t, "NN", BF16, "br_a_dx")
    g_wbra_t = _mm(dya, o_a, "TN", BF16, "br_a_dw")
    do_b = _mm(dyb, wbrb_t, "NN", BF16, "br_b_dx")
    g_wbrb_t = _mm(dyb, o_b, "TN", BF16, "br_b_dw")
    dqa2, dka2, dva2 = _attn_bwd(a_q, a_k, a_v, o_a, do_a, lse_a, MLA_HEADS, 1, MLA_V, a_scale, "attn_a_bwd", tq_f)
    dqb2, dkb2, dvb2 = _attn_bwd(b_q, b_k, b_v, o_b, do_b, lse_b, GQA_HEADS, GQA_GROUP, GQA_HEAD_DIM, b_scale, "attn_b_bwd", tq_f)
    dq_a = _qrope_bwd(dqa2, lcos_a, lss_a, tr)
    dcq_n = _mm(dq_a, wq_t, "NN", F32, "q_up_dx")
    g_wq_t = _mm(dq_a, cq_n, "TN", BF16, "q_up_dw")
    dpq, d_qg, d_gq = _qprep_bwd(pq, dcq_n, dqb2, mla_q_norm_g, gqa_q_norm_g, lcos_b, lss_b, tr)
    dkv_up, dkpe = _kgrad_split(dka2, dva2, cos_a, ss_a, tr)
    dckv_n = _mm(dkv_up, wkv_t, "NN", F32, "kv_up_dx")
    g_wkv_t = _mm(dkv_up, ckv_n, "TN", BF16, "kv_up_dw")
    rs_mix = _scatter_start([g_wq_t, g_wkv_t, g_wbra_t, g_wbrb_t, g_wout], "mix")
    dpkv, d_kvg, d_kg = _kprep_bwd(pkv, dckv_n, dkb2, dvb2, dkpe, mla_kv_norm_g + rs_mix[4], gqa_k_norm_g, cos_b, ss_b, tr)
    dz_kv = _mm(dpkv, w_kvp, "NN", F32, "proj_kv_dx")
    dz_lat = _mm(dpq, win_t, "NN", F32, "proj_q_dx", b_off=kv_cols, add=dz_kv)
    dz_lat = _mm(dpg, win_t, "NN", F32, "proj_g_dx", b_off=kv_cols + QC, add=dz_lat)
    _, d_n1g_c, d_csh, d_csc = _norm_mod_bwd(dz_kv, T // tr, ctx2d, norm1_g, csc, None, "norm1_ctx_bwd", tr)
    grad_x, d_n1g_l, d_sh1, d_sc1 = _norm_mod_bwd(dz_lat, 0, x2d, norm1_g, sc1, dx1, "norm1_lat_bwd", tr)

    zeros_d = jnp.zeros((1, D), F32)
    d_lat = jnp.concatenate([d_sh1, d_sc1, d_g1, d_sh2, d_sc2, d_g2], axis=1)
    d_ctx_part = jnp.concatenate([d_csh, d_csc], axis=1)
    flat = jnp.concatenate(
        [d_n1g_c + d_n1g_l, d_qg, d_kvg, d_gq, d_kg, d_n2g, dcb_a, dcb_b, d_fg,
         dcw_a.reshape(1, -1), dcw_b.reshape(1, -1), d_ctx_part, d_lat], axis=1)
    n_flat = flat.shape[1]
    n_rows = -(-n_flat // (8 * LANES)) * 8
    flat = jnp.pad(flat, ((0, 0), (0, n_rows * LANES - n_flat))).reshape(n_rows, LANES)
    got = _all_gather_small(flat, "gather_small_grads")
    tot = _sum_slots(got, "sum_small_grads").reshape(1, -1)
    sizes = [D, MLA_Q_LORA, MLA_KV_LORA, GQA_HEAD_DIM, GQA_HEAD_DIM, D, F2, D, 3 * FF, 3 * FF, 2 * D]
    offs = [0]
    for s in sizes:
        offs.append(offs[-1] + s)
    t_n1g, t_qg, t_kvg, t_gq, t_kg, t_n2g, t_cb, t_fg, t_cwa, t_cwb, t_ctx = [tot[:, offs[k] : offs[k + 1]] for k in range(len(sizes))]
    g_cw_full = jnp.concatenate([t_cwa.reshape(3, FF), t_cwb.reshape(3, FF)], axis=1)
    g_cw = lax.dynamic_slice(g_cw_full, (0, j * NW), (3, NW))
    d_lat_all = got.reshape(8, -1)[:, offs[-1] : offs[-1] + 6 * D]
    g16 = jnp.concatenate([d_lat_all, jnp.pad(t_ctx, ((0, 0), (0, 4 * D))), jnp.zeros((7, 6 * D), F32)], axis=0)
    g_b_ada = _sum_slots(g16.reshape(16, 1, 6 * D), "sum_b_ada")
    g16_cols = lax.dynamic_slice(g16, (0, j * NA), (16, NA))
    ds_part = _mm(g16_cols, w_ada[0], "NT", F32, "ada_dx")
    got = _all_gather_small(ds_part[8:16], "gather_ada_dx")
    ds_ctx = _sum_slots(jnp.stack([got[2 * s] for s in range(4)]), "sum_ada_dx")[0:1]
    g_c_ctx = _silu_grad_mul(ds_ctx, cc)

    g_kvp = _mm(dpkv, z_all, "TN", BF16, "proj_kv_dw")
    nk = MLA_KV_LORA + 2 * GQA_KV_HEADS * GQA_HEAD_DIM
    g_kv = jnp.concatenate([g_kvp[:MLA_KV_LORA], g_kvp[nk : nk + MLA_ROPE], g_kvp[MLA_KV_LORA:nk]], axis=0)
    g_win_t = _mm(dpq, z_all, "TN", BF16, "proj_q_dw", out_rows=kv_cols + QC + 2 * D, out_off=kv_cols, tm=QC // 2)
    g_win_t = _mm(dpg, z_all, "TN", BF16, "proj_g_dw", out_base=g_win_t, out_off=kv_cols + QC)
    g_win_t = lax.dynamic_update_slice(g_win_t, g_kv, (0, 0))
    sw_in = _swap_start([g_win_t], got, "in")

    h_ffn = _scatter_sums(rs_ffn, sw_in[2][0], "ffn")
    j_ffn = _join_start(h_ffn, grad_x, "ffn")
    h_mix = _scatter_sums(rs_mix, j_ffn[2][0], "mix")
    j_mix = _join_start(h_mix, j_ffn[2][0], "mix")
    rs_in = _scatter_start_after_swap(sw_in, j_mix[2][0], "in")
    g_w_ada = _mm(s16, g16_cols, "TN", F32, "ada_dw", act="silu", after=rs_in[4])
    _, d_ada, m_ada, v_ada = _adamw(w_ada[0], g_w_ada, m_w_ada[0], v_w_ada[0], "adamw_w_ada")
    r_wdown, r_wup = _join_wait(j_ffn, d_ada, "ffn")
    r_wq, r_wkv, r_wbra, r_wbrb, r_wout = _join_wait(j_mix, d_ada, "mix")
    gq_p = _joined(*r_wq).T
    gq = jnp.concatenate([gq_p[:, : 2 * MLA_NOPE].reshape(MLA_Q_LORA, 2, MLA_NOPE), gq_p[:, 2 * MLA_NOPE :].reshape(MLA_Q_LORA, 2, MLA_ROPE)], axis=2)
    grads = {
        "c_ctx": g_c_ctx.reshape(D), "w_ada": g_w_ada[None], "b_ada": g_b_ada, "norm1_g": t_n1g,
        "mla_q_norm_g": t_qg, "w_q_up": gq.reshape(1, MLA_Q_LORA, -1), "mla_kv_norm_g": t_kvg, "w_kv_up": r_wkv,
        "gqa_q_norm_g": t_gq, "gqa_k_norm_g": t_kg, "w_br_a": r_wbra, "w_br_b": r_wbrb, "w_out": r_wout,
        "norm2_g": t_n2g, "w_up": r_wup, "conv_w": g_cw[None], "conv_b": t_cb, "w_down": r_wdown,
        "final_norm_g": t_fg.reshape(D),
    }
    arrives_transposed = ("w_kv_up", "w_br_a", "w_br_b", "w_up")
    arrives_halved = arrives_transposed + ("w_out", "w_down")
    weights = dict(c_ctx=c_ctx, w_ada=w_ada, b_ada=b_ada, norm1_g=norm1_g, w_in=w_in, mla_q_norm_g=mla_q_norm_g, w_q_up=w_q_up,
                   mla_kv_norm_g=mla_kv_norm_g, w_kv_up=w_kv_up, gqa_q_norm_g=gqa_q_norm_g, gqa_k_norm_g=gqa_k_norm_g, w_br_a=w_br_a,
                   w_br_b=w_br_b, w_out=w_out, norm2_g=norm2_g, w_up=w_up, conv_w=conv_w, conv_b=conv_b, w_down=w_down,
                   final_norm_g=final_norm_g)
    m_in = dict(c_ctx=m_c_ctx, w_ada=m_w_ada, b_ada=m_b_ada, norm1_g=m_norm1_g, w_in=m_w_in, mla_q_norm_g=m_mla_q_norm_g,
                w_q_up=m_w_q_up, mla_kv_norm_g=m_mla_kv_norm_g, w_kv_up=m_w_kv_up, gqa_q_norm_g=m_gqa_q_norm_g,
                gqa_k_norm_g=m_gqa_k_norm_g, w_br_a=m_w_br_a, w_br_b=m_w_br_b, w_out=m_w_out, norm2_g=m_norm2_g, w_up=m_w_up,
                conv_w=m_conv_w, conv_b=m_conv_b, w_down=m_w_down, final_norm_g=m_final_norm_g)
    v_in = dict(c_ctx=v_c_ctx, w_ada=v_w_ada, b_ada=v_b_ada, norm1_g=v_norm1_g, w_in=v_w_in, mla_q_norm_g=v_mla_q_norm_g,
                w_q_up=v_w_q_up, mla_kv_norm_g=v_mla_kv_norm_g, w_kv_up=v_w_kv_up, gqa_q_norm_g=v_gqa_q_norm_g,
                gqa_k_norm_g=v_gqa_k_norm_g, w_br_a=v_w_br_a, w_br_b=v_w_br_b, w_out=v_w_out, norm2_g=v_norm2_g, w_up=v_w_up,
                conv_w=v_conv_w, conv_b=v_conv_b, w_down=v_w_down, final_norm_g=v_final_norm_g)
    names = list(weights)
    big = [n for n in names if weights[n].ndim == 3 and weights[n].shape[1] >= 8]
    small = [n for n in names if n not in big]
    delta, new_m, new_v = {}, {}, {}

    def update(n):
        shp = weights[n].shape
        two_d = lambda a: a.reshape(shp[1], shp[2])
        g_t = n in arrives_transposed
        if n in arrives_halved:
            g_in, g_sib = grads[n]
        else:
            g_in, g_sib = two_d(grads[n].astype(F32)), None
        g_, d_, m_, v_ = _adamw(two_d(weights[n]), g_in, two_d(m_in[n]), two_d(v_in[n]), "adamw_" + n, g_transposed=g_t, g_sibling=g_sib)
        grads[n], delta[n], new_m[n], new_v[n] = g_.reshape(shp), d_.reshape(shp), m_.reshape(shp), v_.reshape(shp)

    delta["w_ada"], new_m["w_ada"], new_v["w_ada"] = d_ada[None], m_ada[None], v_ada[None]
    early = [n for n in big if n not in ("w_in", "w_ada")]
    for n in early[:-1]:
        update(n)
    done = sum(delta[n][0, 0:1, 0:1] for n in early[:-1])
    j_in = _join_start(_scatter_sums(rs_in, done, "in"), done, "in")
    last = early[-1]
    grads[last] = (grads[last][0] + j_in[4], grads[last][1])
    update(last)
    ((g_mine, g_sib),) = _join_wait(j_in, delta[last], "in")
    g_, d_, m_, v_ = _adamw(w_in[0].T, g_mine, m_w_in[0].T, v_w_in[0].T, "adamw_w_in", g_sibling=g_sib)
    grads["w_in"], delta["w_in"], new_m["w_in"], new_v["w_in"] = g_.T[None], d_.T[None], m_.T[None], v_.T[None]
    grads = {n: grads[n].reshape(weights[n].shape).astype(F32) for n in names}

    slab = lambda tree: [tree[n].reshape(-1, LANES) for n in small]
    d_, m_, v_ = _adamw_many(slab(weights), slab(grads), slab(m_in), slab(v_in), "adamw_small")
    for k, n in enumerate(small):
        shp = weights[n].shape
        delta[n], new_m[n], new_v[n] = d_[k].reshape(shp), m_[k].reshape(shp), v_[k].reshape(shp)

    return (loss, grad_x[None], *[grads[n] for n in names], *[delta[n] for n in names], *[new_m[n] for n in names],
            *[new_v[n] for n in names])
```

```python
import math

import jax
import jax.numpy as jnp
from jax import lax
from jax.experimental import pallas as pl
from jax.experimental.pallas import tpu as pltpu

F32 = jnp.float32
BF16 = jnp.bfloat16
MESH = pl.DeviceIdType.MESH

NORM_EPS = 1e-6
ROPE_THETA = 10000.0
GRID_W = 64
MLA_HEADS = 8
MLA_Q_LORA = 768
MLA_KV_LORA = 512
MLA_NOPE = 128
MLA_ROPE = 64
MLA_V = 128
GQA_HEADS = 8
GQA_KV_HEADS = 2
GQA_HEAD_DIM = 128
GQA_GROUP = GQA_HEADS // GQA_KV_HEADS
LANES = 128
KVP = MLA_KV_LORA + 2 * GQA_KV_HEADS * GQA_HEAD_DIM + LANES
QC = MLA_Q_LORA + GQA_HEADS * GQA_HEAD_DIM

ADAM_LR = 0.001
ADAM_B1 = 0.9
ADAM_B2 = 0.999
ADAM_EPS = 1e-08
ADAM_WD = 0.01
ADAM_STEP = 10

VMEM_LIMIT = 56 * 1024 * 1024


def _pick(dim, target, mult=LANES):
    t = (min(target, dim) // mult) * mult
    while t >= mult:
        if dim % t == 0:
            return t
        t -= mult
    return dim


def _params(sem):
    return pltpu.CompilerParams(dimension_semantics=sem, vmem_limit_bytes=VMEM_LIMIT)


_DIMS = {"NN": (((1,), (0,)), ((), ())), "NT": (((1,), (1,)), ((), ())), "TN": (((0,), (0,)), ((), ()))}


MM_VMEM_BUDGET = 36 * 1024 * 1024


def _mm_tiles(M, N, K, sa, sb, so, tm, tn, tk):
    tm, tn, tk = _pick(M, tm), _pick(N, tn), _pick(K, tk)

    def need(t):
        return 2 * (tm * t * sa + t * tn * sb) + 2 * tm * tn * so + (tm * tn * 4 if t < K else 0)

    while need(tk) > MM_VMEM_BUDGET and tk > LANES:
        smaller = _pick(K, tk - LANES)
        if smaller >= tk:
            break
        tk = smaller
    return tm, tn, tk


def _window(block, index, offsets):
    if not any(offsets):
        return pl.BlockSpec(block, index)
    for t, o in zip(block, offsets):
        assert o % 16 == 0 and t % 16 == 0, (block, offsets)

    def at(i, j, k):
        return tuple(pl.multiple_of(o + p * t, math.gcd(o, t)) for p, t, o in zip(index(i, j, k), block, offsets))

    return pl.BlockSpec(tuple(pl.Element(t) for t in block), at)


def _mm(a, b, mode, out_dtype, name, m=None, n=None, k=None, b_off=0, add=None, out_rows=None, out_base=None, out_off=0,
        tm=1024, tn=1024, tk=2304, act=None, bias=None, after=None):
    if mode == "NN":
        M, K, N = m or a.shape[0], k or a.shape[1], b.shape[1]
    elif mode == "NT":
        M, K, N = m or a.shape[0], a.shape[1], n or b.shape[0]
    else:
        M, K, N = a.shape[1], k or a.shape[0], b.shape[1]
    tm, tn, tk = _mm_tiles(M, N, K, a.dtype.itemsize, b.dtype.itemsize, jnp.dtype(out_dtype).itemsize, tm, tn, tk)
    nk = K // tk
    dims = _DIMS[mode]
    n_in = 2 + (bias is not None) + (add is not None) + (out_base is not None) + (after is not None)

    def body(*refs):
        a_ref, b_ref = refs[:2]
        bias_ref = refs[2] if bias is not None else None
        add_ref = refs[2 + (bias is not None)] if add is not None else None
        o_ref = refs[n_in]
        av = a_ref[...]
        if act == "silu":
            av = av * jax.nn.sigmoid(av)
        part = lax.dot_general(av.astype(BF16), b_ref[...].astype(BF16), dims, preferred_element_type=F32)

        def finish(r):
            if bias is not None:
                r = r + bias_ref[...]
            if add is not None:
                r = r + add_ref[...]
            o_ref[...] = r.astype(out_dtype)

        if nk == 1:
            finish(part)
            return
        acc = refs[-1]
        k = pl.program_id(2)

        @pl.when(k == 0)
        def _():
            acc[...] = part

        @pl.when(jnp.logical_and(k > 0, k < nk - 1))
        def _():
            acc[...] += part

        @pl.when(k == nk - 1)
        def _():
            finish(acc[...] + part)

    a_spec = pl.BlockSpec((tk, tm), lambda i, j, k: (k, i)) if mode == "TN" else pl.BlockSpec((tm, tk), lambda i, j, k: (i, k))
    if mode == "NT":
        b_spec = _window((tn, tk), lambda i, j, k: (j, k), (b_off, 0))
    else:
        b_spec = _window((tk, tn), lambda i, j, k: (k, j), (b_off, 0))
    in_specs, args = [a_spec, b_spec], [a, b]
    if bias is not None:
        in_specs.append(pl.BlockSpec((1, tn), lambda i, j, k: (0, j)))
        args.append(bias)
    if add is not None:
        in_specs.append(pl.BlockSpec((tm, tn), lambda i, j, k: (i, j)))
        args.append(add)
    aliases = {}
    if after is not None:
        in_specs.append(pl.BlockSpec(after.shape, lambda i, j, k: (0, 0)))
        args.append(after)
    if out_base is not None:
        aliases = {len(args): 0}
        in_specs.append(ANY)
        args.append(out_base)
        out_rows = out_base.shape[0]
    return pl.pallas_call(
        body,
        name=name,
        grid=(M // tm, N // tn, nk),
        in_specs=in_specs,
        out_specs=_window((tm, tn), lambda i, j, k: (i, j), (out_off, 0)),
        out_shape=jax.ShapeDtypeStruct((out_rows or M, N), out_dtype),
        input_output_aliases=aliases,
        scratch_shapes=[pltpu.VMEM((tm, tn), F32)] if nk > 1 else [],
        compiler_params=_params(("parallel", "parallel", "arbitrary")),
    )(*args)


def _rms(x):
    r = lax.rsqrt(jnp.mean(x * x, axis=-1, keepdims=True) + NORM_EPS)
    return x * r, r


def _rms_bwd(xh, r, dxh):
    return r * (dxh - xh * jnp.mean(dxh * xh, axis=-1, keepdims=True))


def _swap(x, q):
    lane = lax.broadcasted_iota(jnp.int32, x.shape, 1)
    even = ((lane // q) % 2) == 0
    return jnp.where(even, pltpu.roll(x, LANES - q, 1), pltpu.roll(x, q, 1))


def _rope(x, cos, ss, q):
    return x * cos + _swap(x, q) * ss


def _rope_t(d, cos, ss, q):
    return d * cos + _swap(d * ss, q)


def _csum(x):
    return jnp.sum(x, axis=0, keepdims=True)


def _rows(tr, w, off=0):
    return pl.BlockSpec((tr, w), lambda i: (i + off, 0))


def _bcast(w):
    return pl.BlockSpec((1, w), lambda i: (0, 0))


def _acc_init(i, refs):
    @pl.when(i == 0)
    def _():
        for r in refs:
            r[...] = jnp.zeros_like(r)


def _rope_tables(n_ctx, n_lat, rot_dim):
    rows = n_lat // GRID_W
    row = jnp.repeat(jnp.arange(rows, dtype=F32), GRID_W)
    col = jnp.tile(jnp.arange(GRID_W, dtype=F32), rows)
    half = rot_dim // 2
    inv_freq = ROPE_THETA ** (-jnp.arange(0, half, 2, dtype=F32) / half)
    ar, ac = row[:, None] * inv_freq, col[:, None] * inv_freq
    cos = jnp.concatenate([jnp.cos(ar), jnp.cos(ar), jnp.cos(ac), jnp.cos(ac)], axis=-1)
    ss = jnp.concatenate([-jnp.sin(ar), jnp.sin(ar), -jnp.sin(ac), jnp.sin(ac)], axis=-1)
    cos = jnp.tile(cos, (1, LANES // rot_dim))
    ss = jnp.tile(ss, (1, LANES // rot_dim))
    cos = jnp.concatenate([cos, jnp.ones((n_ctx, LANES), F32)], axis=0)
    ss = jnp.concatenate([ss, jnp.zeros((n_ctx, LANES), F32)], axis=0)
    return cos, ss


def _norm_mod_fwd(x2d, g, sh, sc, name, tr, out_rows=None, base=None, out_off=0):
    n, d = x2d.shape

    def body(x_ref, g_ref, sh_ref, sc_ref, *rest):
        xh, _ = _rms(x_ref[...])
        rest[-1][...] = ((xh * g_ref[...]) * (1.0 + sc_ref[...]) + sh_ref[...]).astype(BF16)

    args, in_specs, aliases = [x2d, g, sh, sc], [_rows(tr, d), _bcast(d), _bcast(d), _bcast(d)], {}
    if base is not None:
        args.append(base)
        in_specs.append(ANY)
        aliases = {4: 0}
        out_rows = base.shape[0]
    return pl.pallas_call(
        body,
        name=name,
        grid=(n // tr,),
        in_specs=in_specs,
        out_specs=_rows(tr, d, out_off // tr),
        out_shape=jax.ShapeDtypeStruct((out_rows or n, d), BF16),
        input_output_aliases=aliases,
        compiler_params=_params(("parallel",)),
    )(*args)


def _norm_mod_bwd(dz, dz_off, x2d, g, sc, dres, name, tr):
    n, d = x2d.shape
    want_dx = dres is not None

    def body(*refs):
        if want_dx:
            dz_ref, x_ref, g_ref, sc_ref, dres_ref, dx_ref, dg_ref, dsh_ref, dsc_ref = refs
        else:
            dz_ref, x_ref, g_ref, sc_ref, dg_ref, dsh_ref, dsc_ref = refs
        _acc_init(pl.program_id(0), [dg_ref, dsh_ref, dsc_ref])
        xh, r = _rms(x_ref[...])
        dzv = dz_ref[...]
        gv = g_ref[...]
        dsc_ref[...] += _csum(dzv * (xh * gv))
        dsh_ref[...] += _csum(dzv)
        dh = dzv * (1.0 + sc_ref[...])
        dg_ref[...] += _csum(dh * xh)
        if want_dx:
            dx_ref[...] = _rms_bwd(xh, r, dh * gv) + dres_ref[...]

    in_specs = [_rows(tr, d, dz_off), _rows(tr, d), _bcast(d), _bcast(d)]
    args = [dz, x2d, g, sc]
    out_specs = [_bcast(d)] * 3
    out_shape = [jax.ShapeDtypeStruct((1, d), F32)] * 3
    if want_dx:
        in_specs.append(_rows(tr, d))
        args.append(dres)
        out_specs = [_rows(tr, d)] + out_specs
        out_shape = [jax.ShapeDtypeStruct((n, d), F32)] + out_shape
    res = pl.pallas_call(
        body,
        name=name,
        grid=(n // tr,),
        in_specs=in_specs,
        out_specs=out_specs,
        out_shape=out_shape,
        compiler_params=_params(("arbitrary",)),
    )(*args)
    return res if want_dx else (None, *res)


_QA, _QB = MLA_ROPE // 4, GQA_HEAD_DIM // 4


def _kprep_fwd(pkv, kvg, kg, cos_a, ss_a, cos_b, ss_b, tr):
    n = pkv.shape[0]
    nb = GQA_KV_HEADS * GQA_HEAD_DIM

    def body(p_ref, kvg_ref, kg_ref, ca, sa, cb, sb, ckv_ref, kb_ref, vb_ref, kpe_ref):
        p = p_ref[...]
        xh, _ = _rms(p[:, :MLA_KV_LORA])
        ckv_ref[...] = (xh * kvg_ref[...]).astype(BF16)
        for e in range(GQA_KV_HEADS):
            lo = MLA_KV_LORA + e * GQA_HEAD_DIM
            kh, _ = _rms(p[:, lo : lo + GQA_HEAD_DIM])
            kb_ref[:, e * GQA_HEAD_DIM : (e + 1) * GQA_HEAD_DIM] = _rope(kh * kg_ref[...], cb[...], sb[...], _QB).astype(BF16)
        vb_ref[...] = p[:, MLA_KV_LORA + nb : MLA_KV_LORA + 2 * nb].astype(BF16)
        kr = _rope(p[:, MLA_KV_LORA + 2 * nb :], ca[...], sa[...], _QA)
        kpe_ref[:, :LANES] = kr.astype(BF16)
        kpe_ref[:, LANES:] = pltpu.roll(kr, MLA_ROPE, 1).astype(BF16)

    return pl.pallas_call(
        body,
        name="kprep_fwd",
        grid=(n // tr,),
        in_specs=[_rows(tr, KVP), _bcast(MLA_KV_LORA), _bcast(GQA_HEAD_DIM)] + [_rows(tr, LANES)] * 4,
        out_specs=[_rows(tr, MLA_KV_LORA), _rows(tr, nb), _rows(tr, nb), _rows(tr, 2 * LANES)],
        out_shape=[jax.ShapeDtypeStruct((n, w), BF16) for w in (MLA_KV_LORA, nb, nb, 2 * LANES)],
        compiler_params=_params(("parallel",)),
    )(pkv, kvg, kg, cos_a, ss_a, cos_b, ss_b)


def _kprep_bwd(pkv, dckv, dkb, dvb, dkpe, kvg, kg, cos_b, ss_b, tr):
    n = pkv.shape[0]
    nb = GQA_KV_HEADS * GQA_HEAD_DIM

    def body(p_ref, dckv_ref, dkb_ref, dvb_ref, dkpe_ref, kvg_ref, kg_ref, cb, sb, dp_ref, dkvg_ref, dkg_ref):
        _acc_init(pl.program_id(0), [dkvg_ref, dkg_ref])
        p = p_ref[...]
        xh, r = _rms(p[:, :MLA_KV_LORA])
        dn = dckv_ref[...]
        dkvg_ref[...] += _csum(dn * xh)
        dp_ref[:, :MLA_KV_LORA] = _rms_bwd(xh, r, dn * kvg_ref[...]).astype(BF16)
        for e in range(GQA_KV_HEADS):
            lo = MLA_KV_LORA + e * GQA_HEAD_DIM
            kh, rk = _rms(p[:, lo : lo + GQA_HEAD_DIM])
            dk = _rope_t(dkb_ref[:, e * GQA_HEAD_DIM : (e + 1) * GQA_HEAD_DIM], cb[...], sb[...], _QB)
            dkg_ref[...] += _csum(dk * kh)
            dp_ref[:, lo : lo + GQA_HEAD_DIM] = _rms_bwd(kh, rk, dk * kg_ref[...]).astype(BF16)
        dp_ref[:, MLA_KV_LORA + nb : MLA_KV_LORA + 2 * nb] = dvb_ref[...].astype(BF16)
        dp_ref[:, MLA_KV_LORA + 2 * nb :] = dkpe_ref[...].astype(BF16)

    return pl.pallas_call(
        body,
        name="kprep_bwd",
        grid=(n // tr,),
        in_specs=[_rows(tr, KVP), _rows(tr, MLA_KV_LORA), _rows(tr, nb), _rows(tr, nb), _rows(tr, LANES),
                  _bcast(MLA_KV_LORA), _bcast(GQA_HEAD_DIM), _rows(tr, LANES), _rows(tr, LANES)],
        out_specs=[_rows(tr, KVP), _bcast(MLA_KV_LORA), _bcast(GQA_HEAD_DIM)],
        out_shape=[jax.ShapeDtypeStruct((n, KVP), BF16), jax.ShapeDtypeStruct((1, MLA_KV_LORA), F32),
                   jax.ShapeDtypeStruct((1, GQA_HEAD_DIM), F32)],
        compiler_params=_params(("arbitrary",)),
    )(pkv, dckv, dkb, dvb, dkpe, kvg, kg, cos_b, ss_b)


def _kgrad_split(dka, dva, cos_a, ss_a, tr):
    n = dka.shape[0]
    wk = MLA_HEADS * 2 * LANES

    def body(dk_ref, dv_ref, ca, sa, dkv_ref, dkpe_ref):
        even = jnp.zeros((tr, LANES), F32)
        odd = jnp.zeros((tr, LANES), F32)
        for h in range(MLA_HEADS):
            dkv_ref[:, 2 * h * LANES : (2 * h + 1) * LANES] = dk_ref[:, 2 * h * LANES : (2 * h + 1) * LANES].astype(BF16)
            dkv_ref[:, (2 * h + 1) * LANES : (2 * h + 2) * LANES] = dv_ref[:, h * MLA_V : (h + 1) * MLA_V].astype(BF16)
            part = dk_ref[:, (2 * h + 1) * LANES : (2 * h + 2) * LANES]
            if h % 2 == 0:
                even = even + part
            else:
                odd = odd + part
        lane = lax.broadcasted_iota(jnp.int32, (tr, LANES), 1)
        low = lane < MLA_ROPE
        both = jnp.where(low, even, odd)
        tot = jnp.where(low, both + pltpu.roll(both, MLA_ROPE, 1), 0.0)
        dkpe_ref[...] = _rope_t(tot, ca[...], sa[...], _QA)

    return pl.pallas_call(
        body,
        name="kgrad_split",
        grid=(n // tr,),
        in_specs=[_rows(tr, wk), _rows(tr, MLA_HEADS * MLA_V), _rows(tr, LANES), _rows(tr, LANES)],
        out_specs=[_rows(tr, wk), _rows(tr, LANES)],
        out_shape=[jax.ShapeDtypeStruct((n, wk), BF16), jax.ShapeDtypeStruct((n, LANES), F32)],
        compiler_params=_params(("parallel",)),
    )(dka, dva, cos_a, ss_a)


def _qprep_fwd(pq, qg, gq, cos_b, ss_b, tr):
    n = pq.shape[0]
    nq = GQA_HEADS * GQA_HEAD_DIM

    def body(p_ref, qg_ref, gq_ref, cb, sb, cq_ref, qb_ref):
        xh, _ = _rms(p_ref[:, :MLA_Q_LORA])
        cq_ref[...] = (xh * qg_ref[...]).astype(BF16)
        for h in range(GQA_HEADS):
            lo = MLA_Q_LORA + h * GQA_HEAD_DIM
            qh, _ = _rms(p_ref[:, lo : lo + GQA_HEAD_DIM])
            qb_ref[:, h * GQA_HEAD_DIM : (h + 1) * GQA_HEAD_DIM] = _rope(qh * gq_ref[...], cb[...], sb[...], _QB).astype(BF16)

    return pl.pallas_call(
        body,
        name="qprep_fwd",
        grid=(n // tr,),
        in_specs=[_rows(tr, QC), _bcast(MLA_Q_LORA), _bcast(GQA_HEAD_DIM), _rows(tr, LANES), _rows(tr, LANES)],
        out_specs=[_rows(tr, MLA_Q_LORA), _rows(tr, nq)],
        out_shape=[jax.ShapeDtypeStruct((n, MLA_Q_LORA), BF16), jax.ShapeDtypeStruct((n, nq), BF16)],
        compiler_params=_params(("parallel",)),
    )(pq, qg, gq, cos_b, ss_b)


def _qprep_bwd(pq, dcq, dqb, qg, gq, cos_b, ss_b, tr):
    n = pq.shape[0]
    nq = GQA_HEADS * GQA_HEAD_DIM

    def body(p_ref, dcq_ref, dqb_ref, qg_ref, gq_ref, cb, sb, dp_ref, dqg_ref, dgq_ref):
        _acc_init(pl.program_id(0), [dqg_ref, dgq_ref])
        xh, r = _rms(p_ref[:, :MLA_Q_LORA])
        dn = dcq_ref[...]
        dqg_ref[...] += _csum(dn * xh)
        dp_ref[:, :MLA_Q_LORA] = _rms_bwd(xh, r, dn * qg_ref[...]).astype(BF16)
        for h in range(GQA_HEADS):
            lo = MLA_Q_LORA + h * GQA_HEAD_DIM
            qh, rq = _rms(p_ref[:, lo : lo + GQA_HEAD_DIM])
            dq = _rope_t(dqb_ref[:, h * GQA_HEAD_DIM : (h + 1) * GQA_HEAD_DIM], cb[...], sb[...], _QB)
            dgq_ref[...] += _csum(dq * qh)
            dp_ref[:, lo : lo + GQA_HEAD_DIM] = _rms_bwd(qh, rq, dq * gq_ref[...]).astype(BF16)

    return pl.pallas_call(
        body,
        name="qprep_bwd",
        grid=(n // tr,),
        in_specs=[_rows(tr, QC), _rows(tr, MLA_Q_LORA), _rows(tr, nq), _bcast(MLA_Q_LORA), _bcast(GQA_HEAD_DIM),
                  _rows(tr, LANES), _rows(tr, LANES)],
        out_specs=[_rows(tr, QC), _bcast(MLA_Q_LORA), _bcast(GQA_HEAD_DIM)],
        out_shape=[jax.ShapeDtypeStruct((n, QC), BF16), jax.ShapeDtypeStruct((1, MLA_Q_LORA), F32),
                   jax.ShapeDtypeStruct((1, GQA_HEAD_DIM), F32)],
        compiler_params=_params(("arbitrary",)),
    )(pq, dcq, dqb, qg, gq, cos_b, ss_b)


_QA_COLS = MLA_HEADS * (MLA_NOPE + MLA_ROPE)


def _qrope_fwd(qa, cos_a, ss_a, tr):
    n = qa.shape[0]

    def body(q_ref, ca, sa, o_ref):
        for j in range(MLA_HEADS // 2):
            lo = 3 * j * LANES
            o_ref[:, lo : lo + 2 * LANES] = q_ref[:, lo : lo + 2 * LANES].astype(BF16)
            o_ref[:, lo + 2 * LANES : lo + 3 * LANES] = _rope(q_ref[:, lo + 2 * LANES : lo + 3 * LANES], ca[...], sa[...], _QA).astype(BF16)

    return pl.pallas_call(
        body,
        name="qrope_fwd",
        grid=(n // tr,),
        in_specs=[_rows(tr, _QA_COLS), _rows(tr, LANES), _rows(tr, LANES)],
        out_specs=_rows(tr, _QA_COLS),
        out_shape=jax.ShapeDtypeStruct((n, _QA_COLS), BF16),
        compiler_params=_params(("parallel",)),
    )(qa, cos_a, ss_a)


def _qrope_bwd(dq2, cos_a, ss_a, tr):
    n = dq2.shape[0]

    def body(d_ref, ca, sa, o_ref):
        for j in range(MLA_HEADS // 2):
            lo = 3 * j * LANES
            h0, h1 = 2 * j, 2 * j + 1
            o_ref[:, lo : lo + LANES] = d_ref[:, 2 * h0 * LANES : (2 * h0 + 1) * LANES].astype(BF16)
            o_ref[:, lo + LANES : lo + 2 * LANES] = d_ref[:, 2 * h1 * LANES : (2 * h1 + 1) * LANES].astype(BF16)
            pe = d_ref[:, (2 * h0 + 1) * LANES : (2 * h0 + 2) * LANES] + d_ref[:, (2 * h1 + 1) * LANES : (2 * h1 + 2) * LANES]
            o_ref[:, lo + 2 * LANES : lo + 3 * LANES] = _rope_t(pe, ca[...], sa[...], _QA).astype(BF16)

    return pl.pallas_call(
        body,
        name="qrope_bwd",
        grid=(n // tr,),
        in_specs=[_rows(tr, MLA_HEADS * 2 * LANES), _rows(tr, LANES), _rows(tr, LANES)],
        out_specs=_rows(tr, _QA_COLS),
        out_shape=jax.ShapeDtypeStruct((n, _QA_COLS), BF16),
        compiler_params=_params(("parallel",)),
    )(dq2, cos_a, ss_a)


def _cat(refs):
    vals = [r[...] for r in refs]
    return vals[0] if len(vals) == 1 else jnp.concatenate(vals, axis=-1)


LOG2E = 1.4426950408889634


def _attn_fwd(qparts, kparts, vpart, n_heads, group, dv, scale, name, tq, after=None):
    T, Tk = qparts[0][0].shape[0], kparts[0][0].shape[0]
    nq_, nk_ = len(qparts), len(kparts)
    sub = min(tq, 256)
    c2 = scale * LOG2E

    def body(*refs):
        q_refs, k_refs = refs[:nq_], refs[nq_ : nq_ + nk_]
        v_ref = refs[nq_ + nk_]
        o_ref, lse_ref = refs[-2:]
        k = _cat(k_refs)
        v = v_ref[...]
        for r0 in range(0, tq, sub):
            q = _cat([r.at[r0 : r0 + sub, :] for r in q_refs])
            s = lax.dot_general(q, k, _DIMS["NT"], preferred_element_type=F32)
            m = jnp.max(s, axis=-1, keepdims=True)
            p = jnp.exp2((s - m) * c2)
            l = jnp.sum(p, axis=-1, keepdims=True)
            acc = jnp.dot(p.astype(BF16), v, preferred_element_type=F32)
            o_ref[r0 : r0 + sub, :] = (acc * (1.0 / l)).astype(BF16)
            lse_ref[r0 : r0 + sub, :] = m * scale + jnp.log(l)

    in_specs = [pl.BlockSpec((tq, LANES), lambda h, i, f=f: (i, f(h))) for _, f in qparts]
    in_specs += [pl.BlockSpec((Tk, LANES), lambda h, i, f=f: (0, f(h // group))) for _, f in kparts]
    fv = vpart[1]
    in_specs.append(pl.BlockSpec((Tk, dv), lambda h, i: (0, fv(h // group))))
    args = [*[a for a, _ in qparts], *[a for a, _ in kparts], vpart[0]]
    if after is not None:
        in_specs.append(pl.BlockSpec(after.shape, lambda h, i: (0, 0)))
        args.append(after)
    return pl.pallas_call(
        body,
        name=name,
        grid=(n_heads, T // tq),
        in_specs=in_specs,
        out_specs=[pl.BlockSpec((tq, dv), lambda h, i: (i, h)), pl.BlockSpec((None, tq, 1), lambda h, i: (h, i, 0))],
        out_shape=[jax.ShapeDtypeStruct((T, n_heads * dv), BF16), jax.ShapeDtypeStruct((n_heads, T, 1), F32)],
        compiler_params=_params(("parallel", "parallel")),
    )(*args)


def _attn_bwd(qparts, kparts, vpart, o, do, lse, n_heads, group, dv, scale, name, tq):
    T, Tk = qparts[0][0].shape[0], kparts[0][0].shape[0]
    nq_, nk_ = len(qparts), len(kparts)
    dk_ = LANES * nq_
    n_kv = n_heads // group
    nblk = T // tq
    c2 = scale * LOG2E

    def head(hk, i):
        return hk * group + i // nblk

    sub = min(tq, 256)

    def body(*refs):
        q_refs = refs[:nq_]
        k = _cat(refs[nq_ : nq_ + nk_])
        v_ref, o_ref, do_ref, lse_ref, dq_ref, dk_ref, dv_ref = refs[nq_ + nk_ :]
        i = pl.program_id(1)
        _acc_init(i, [dk_ref, dv_ref])
        v = v_ref[...]
        dk_acc, dv_acc = None, None
        for r0 in range(0, tq, sub):
            rows = slice(r0, r0 + sub)
            q = _cat([r.at[rows, :] for r in q_refs])
            s = lax.dot_general(q, k, _DIMS["NT"], preferred_element_type=F32)
            p = jnp.exp2(s * c2 - lse_ref[rows, :] * LOG2E)
            dov = do_ref[rows, :]
            dp = lax.dot_general(dov, v, _DIMS["NT"], preferred_element_type=F32)
            delta = jnp.sum(dov.astype(F32) * o_ref[rows, :].astype(F32), axis=-1, keepdims=True)
            ds = (p * (dp - delta)).astype(BF16)
            dq_ref[rows, :] = jnp.dot(ds, k, preferred_element_type=F32) * scale
            dk_part = lax.dot_general(ds, q, _DIMS["TN"], preferred_element_type=F32)
            dv_part = lax.dot_general(p.astype(BF16), dov, _DIMS["TN"], preferred_element_type=F32)
            dk_acc = dk_part if dk_acc is None else dk_acc + dk_part
            dv_acc = dv_part if dv_acc is None else dv_acc + dv_part
        dk_ref[...] += dk_acc
        dv_ref[...] += dv_acc

        @pl.when(i == group * nblk - 1)
        def _():
            dk_ref[...] *= scale

    in_specs = [pl.BlockSpec((tq, LANES), lambda hk, i, f=f: (i % nblk, f(head(hk, i)))) for _, f in qparts]
    in_specs += [pl.BlockSpec((Tk, LANES), lambda hk, i, f=f: (0, f(hk))) for _, f in kparts]
    fv = vpart[1]
    in_specs.append(pl.BlockSpec((Tk, dv), lambda hk, i: (0, fv(hk))))
    in_specs += [pl.BlockSpec((tq, dv), lambda hk, i: (i % nblk, head(hk, i)))] * 2
    in_specs.append(pl.BlockSpec((None, tq, 1), lambda hk, i: (head(hk, i), i % nblk, 0)))
    return pl.pallas_call(
        body,
        name=name,
        grid=(n_kv, group * nblk),
        in_specs=in_specs,
        out_specs=[pl.BlockSpec((tq, dk_), lambda hk, i: (i % nblk, head(hk, i))),
                   pl.BlockSpec((Tk, dk_), lambda hk, i: (0, hk)),
                   pl.BlockSpec((Tk, dv), lambda hk, i: (0, hk))],
        out_shape=[jax.ShapeDtypeStruct((T, n_heads * dk_), F32), jax.ShapeDtypeStruct((Tk, n_kv * dk_), F32),
                   jax.ShapeDtypeStruct((Tk, n_kv * dv), F32)],
        compiler_params=_params(("parallel", "arbitrary")),
    )(*[a for a, _ in qparts], *[a for a, _ in kparts], vpart[0], o, do, lse)


def _gates_fwd(pg, ya, yb, tr):
    n, d = ya.shape

    def body(pg_ref, ya_ref, yb_ref, o_ref):
        ga = jax.nn.sigmoid(pg_ref[:, :d].astype(F32))
        gb = jax.nn.sigmoid(pg_ref[:, d:].astype(F32))
        o_ref[...] = (ga * ya_ref[...].astype(F32) + gb * yb_ref[...].astype(F32)).astype(BF16)

    return pl.pallas_call(
        body,
        name="gates_fwd",
        grid=(n // tr,),
        in_specs=[_rows(tr, 2 * d), _rows(tr, d), _rows(tr, d)],
        out_specs=_rows(tr, d),
        out_shape=jax.ShapeDtypeStruct((n, d), BF16),
        compiler_params=_params(("parallel",)),
    )(pg, ya, yb)


def _gates_bwd(dm, pg, ya, yb, tr):
    n, d = ya.shape

    def body(dm_ref, pg_ref, ya_ref, yb_ref, dya_ref, dyb_ref, dpg_ref):
        dmv = dm_ref[...].astype(F32)
        ga = jax.nn.sigmoid(pg_ref[:, :d].astype(F32))
        gb = jax.nn.sigmoid(pg_ref[:, d:].astype(F32))
        dya_ref[...] = (dmv * ga).astype(BF16)
        dyb_ref[...] = (dmv * gb).astype(BF16)
        dpg_ref[:, :d] = (dmv * ya_ref[...].astype(F32) * ga * (1.0 - ga)).astype(BF16)
        dpg_ref[:, d:] = (dmv * yb_ref[...].astype(F32) * gb * (1.0 - gb)).astype(BF16)

    return pl.pallas_call(
        body,
        name="gates_bwd",
        grid=(n // tr,),
        in_specs=[_rows(tr, d), _rows(tr, 2 * d), _rows(tr, d), _rows(tr, d)],
        out_specs=[_rows(tr, d), _rows(tr, d), _rows(tr, 2 * d)],
        out_shape=[jax.ShapeDtypeStruct((n, d), BF16), jax.ShapeDtypeStruct((n, d), BF16), jax.ShapeDtypeStruct((n, 2 * d), BF16)],
        compiler_params=_params(("parallel",)),
    )(dm, pg, ya, yb)


def _resid_norm2_fwd(x2d, att, g1, n2g, sh2, sc2, tr):
    n, d = x2d.shape

    def body(x_ref, a_ref, g1_ref, g_ref, sh_ref, sc_ref, x1_ref, z_ref):
        x1 = x_ref[...] + g1_ref[...] * a_ref[...]
        x1_ref[...] = x1
        xh, _ = _rms(x1)
        z_ref[...] = ((xh * g_ref[...]) * (1.0 + sc_ref[...]) + sh_ref[...]).astype(BF16)

    return pl.pallas_call(
        body,
        name="resid_norm2_fwd",
        grid=(n // tr,),
        in_specs=[_rows(tr, d), _rows(tr, d)] + [_bcast(d)] * 4,
        out_specs=[_rows(tr, d), _rows(tr, d)],
        out_shape=[jax.ShapeDtypeStruct((n, d), F32), jax.ShapeDtypeStruct((n, d), BF16)],
        compiler_params=_params(("parallel",)),
    )(x2d, att, g1, n2g, sh2, sc2)


def _resid_norm2_bwd(dz2, x1, dx2, att, n2g, sc2, g1, tr):
    n, d = x1.shape

    def body(dz_ref, x1_ref, dx2_ref, a_ref, g_ref, sc_ref, g1_ref, dx1_ref, da_ref, dg_ref, dsh_ref, dsc_ref, dg1_ref):
        _acc_init(pl.program_id(0), [dg_ref, dsh_ref, dsc_ref, dg1_ref])
        xh, r = _rms(x1_ref[...])
        dzv = dz_ref[...]
        gv = g_ref[...]
        dsc_ref[...] += _csum(dzv * (xh * gv))
        dsh_ref[...] += _csum(dzv)
        dh = dzv * (1.0 + sc_ref[...])
        dg_ref[...] += _csum(dh * xh)
        dx1 = _rms_bwd(xh, r, dh * gv) + dx2_ref[...]
        dx1_ref[...] = dx1
        dg1_ref[...] += _csum(dx1 * a_ref[...])
        da_ref[...] = (dx1 * g1_ref[...]).astype(BF16)

    return pl.pallas_call(
        body,
        name="resid_norm2_bwd",
        grid=(n // tr,),
        in_specs=[_rows(tr, d)] * 4 + [_bcast(d)] * 3,
        out_specs=[_rows(tr, d), _rows(tr, d)] + [_bcast(d)] * 4,
        out_shape=[jax.ShapeDtypeStruct((n, d), F32), jax.ShapeDtypeStruct((n, d), BF16)] + [jax.ShapeDtypeStruct((1, d), F32)] * 4,
        compiler_params=_params(("arbitrary",)),
    )(dz2, x1, dx2, att, n2g, sc2, g1)


def _edges(shape):
    row = lax.broadcasted_iota(jnp.int32, shape, 0)
    return row == 0, row == shape[0] - 1


def _shifts(u, edges):
    n = u.shape[0]
    return jnp.where(edges[0], 0.0, pltpu.roll(u, 1, 0)), jnp.where(edges[1], 0.0, pltpu.roll(u, n - 1, 0))


def _conv3(u, prev, nxt, w_ref, b_ref):
    return b_ref[...] + w_ref[0:1, :] * prev + w_ref[1:2, :] * u + w_ref[2:3, :] * nxt


def _ffn_up_conv(z, wup_t, cw, cb, tc, after):
    n, d = z.shape
    f = wup_t.shape[0] // 2
    nb = f // tc

    def body(z_ref, wa_ref, wb_ref, cwa, cwb, cba, cbb, after_ref, ua_ref, ub_ref, h_ref):
        w = jnp.concatenate([wa_ref[...], wb_ref[...]], axis=0)
        u = lax.dot_general(z_ref[...], w, _DIMS["NT"], preferred_element_type=F32).astype(BF16)
        ua_ref[...] = u[:, :tc]
        ub_ref[...] = u[:, tc:]
        edges = _edges((n, tc))
        ua = u[:, :tc].astype(F32)
        ub = u[:, tc:].astype(F32)
        a = _conv3(ua, *_shifts(ua, edges), cwa, cba)
        b = _conv3(ub, *_shifts(ub, edges), cwb, cbb)
        h_ref[...] = (a * jax.nn.sigmoid(a) * b).astype(BF16)

    col = lambda rows, off: pl.BlockSpec((rows, tc), lambda i: (0, i + off))
    w_rows = lambda off: pl.BlockSpec((tc, d), lambda i: (i + off, 0))
    return pl.pallas_call(
        body,
        name="ffn_up_conv",
        grid=(nb,),
        in_specs=[pl.BlockSpec((n, d), lambda i: (0, 0)), w_rows(0), w_rows(nb), col(3, 0), col(3, nb), col(1, 0), col(1, nb),
                  pl.BlockSpec(after.shape, lambda i: (0, 0))],
        out_specs=[col(n, 0)] * 3,
        out_shape=[jax.ShapeDtypeStruct((n, f), BF16)] * 3,
        compiler_params=_params(("parallel",)),
    )(z, wup_t, wup_t, cw, cw, cb, cb, after)


def _ffn_down_dx_conv_bwd(df, wdown, u_a, u_b, cw, cb, tc, after):
    n, f = u_a.shape
    d = df.shape[1]
    nb = f // tc

    def part(uv, prev, nxt, duc, edges, w_ref, du_ref, dw_ref, db_ref):
        db_ref[...] = _csum(duc)
        dw_ref[0:1, :] = _csum(duc * prev)
        dw_ref[1:2, :] = _csum(duc * uv)
        dw_ref[2:3, :] = _csum(duc * nxt)
        d_prev, d_next = _shifts(duc, edges)
        du_ref[...] = (w_ref[0:1, :] * d_next + w_ref[1:2, :] * duc + w_ref[2:3, :] * d_prev).astype(BF16)

    def body(df_ref, wd_ref, ua_ref, ub_ref, wa_ref, wb_ref, ba_ref, bb_ref, after_ref,
             dua_ref, dub_ref, dwa_ref, dwb_ref, dba_ref, dbb_ref):
        dhv = lax.dot_general(df_ref[...], wd_ref[...], _DIMS["NT"], preferred_element_type=F32)
        dhv = dhv.astype(BF16).astype(F32)
        edges = _edges((n, tc))
        ua = ua_ref[...].astype(F32)
        ub = ub_ref[...].astype(F32)
        sa = _shifts(ua, edges)
        sb = _shifts(ub, edges)
        a = _conv3(ua, *sa, wa_ref, ba_ref)
        b = _conv3(ub, *sb, wb_ref, bb_ref)
        sg = jax.nn.sigmoid(a)
        da = dhv * b * (sg * (1.0 + a * (1.0 - sg)))
        db = dhv * (a * sg)
        part(ua, *sa, da, edges, wa_ref, dua_ref, dwa_ref, dba_ref)
        part(ub, *sb, db, edges, wb_ref, dub_ref, dwb_ref, dbb_ref)

    col = lambda rows, off: pl.BlockSpec((rows, tc), lambda i: (0, i + off))
    return pl.pallas_call(
        body,
        name="ffn_down_dx_conv_bwd",
        grid=(nb,),
        in_specs=[pl.BlockSpec((n, d), lambda i: (0, 0)), pl.BlockSpec((tc, d), lambda i: (i, 0)), col(n, 0), col(n, 0),
                  col(3, 0), col(3, nb), col(1, 0), col(1, nb), pl.BlockSpec(after.shape, lambda i: (0, 0))],
        out_specs=[col(n, 0), col(n, 0), col(3, 0), col(3, 0), col(1, 0), col(1, 0)],
        out_shape=[jax.ShapeDtypeStruct((n, f), BF16)] * 2 + [jax.ShapeDtypeStruct((3, f), F32)] * 2 + [jax.ShapeDtypeStruct((1, f), F32)] * 2,
        compiler_params=_params(("parallel",)),
    )(df, wdown, u_a, u_b, cw, cw, cb, cb, after)


def _loss_head(x1, f, g2, fg, tgt, tr):
    n, d = x1.shape

    def body(x1_ref, f_ref, g2_ref, fg_ref, t_ref, sq_ref, dx2_ref, dfg_ref, dg2_ref, df_ref):
        _acc_init(pl.program_id(0), [sq_ref, dfg_ref, dg2_ref])
        fv = f_ref[...]
        xh, r = _rms(x1_ref[...] + g2_ref[...] * fv)
        err = xh * fg_ref[...] - t_ref[...]
        sq_ref[...] += _csum(err * err)
        dy = err * (1.0 / d)
        dfg_ref[...] += _csum(dy * xh)
        dx2 = _rms_bwd(xh, r, dy * fg_ref[...])
        dx2_ref[...] = dx2
        dg2_ref[...] += _csum(dx2 * fv)
        df_ref[...] = (dx2 * g2_ref[...]).astype(BF16)

    return pl.pallas_call(
        body,
        name="loss_head",
        grid=(n // tr,),
        in_specs=[_rows(tr, d), _rows(tr, d), _bcast(d), _bcast(d), _rows(tr, d)],
        out_specs=[_bcast(d), _rows(tr, d), _bcast(d), _bcast(d), _rows(tr, d)],
        out_shape=[jax.ShapeDtypeStruct((1, d), F32), jax.ShapeDtypeStruct((n, d), F32), jax.ShapeDtypeStruct((1, d), F32),
                   jax.ShapeDtypeStruct((1, d), F32), jax.ShapeDtypeStruct((n, d), BF16)],
        compiler_params=_params(("arbitrary",)),
    )(x1, f, g2, fg, tgt)


def _sum_slots(g, name):
    s, r, w = g.shape

    def body(g_ref, o_ref):
        acc = g_ref[0]
        for k in range(1, s):
            acc = acc + g_ref[k]
        o_ref[...] = acc

    return pl.pallas_call(body, name=name, out_shape=jax.ShapeDtypeStruct((r, w), F32))(g)


def _silu_grad_mul(ds, cvec):
    def body(d_ref, c_ref, o_ref):
        cv = c_ref[...]
        sg = jax.nn.sigmoid(cv)
        o_ref[...] = d_ref[...] * (sg * (1.0 + cv * (1.0 - sg)))

    return pl.pallas_call(body, name="silu_grad_mul", out_shape=jax.ShapeDtypeStruct(ds.shape, F32))(ds, cvec)


def _adamw_update(wv, gv, mv, vv, d_ref, mo_ref, vo_ref):
    mn = ADAM_B1 * mv + (1.0 - ADAM_B1) * gv
    vn = ADAM_B2 * vv + (1.0 - ADAM_B2) * (gv * gv)
    mo_ref[...] = mn
    vo_ref[...] = vn
    m_hat = mn / (1.0 - ADAM_B1**ADAM_STEP)
    v_hat = vn / (1.0 - ADAM_B2**ADAM_STEP)
    d_ref[...] = -ADAM_LR * (m_hat / (jnp.sqrt(v_hat) + ADAM_EPS) + ADAM_WD * wv)


def _adamw_many(ws, gs, ms, vs, name):
    n = len(ws)

    def body(*refs):
        for k in range(n):
            w_ref, g_ref, m_ref, v_ref = (refs[q * n + k] for q in range(4))
            d_ref, mo_ref, vo_ref = (refs[(4 + q) * n + k] for q in range(3))
            _adamw_update(w_ref[...], g_ref[...], m_ref[...], v_ref[...], d_ref, mo_ref, vo_ref)

    res = pl.pallas_call(body, name=name, out_shape=[jax.ShapeDtypeStruct(w.shape, F32) for w in ws] * 3)(*ws, *gs, *ms, *vs)
    return res[:n], res[n : 2 * n], res[2 * n :]


def _adamw(w, g, m, v, name, g_transposed=False, g_sibling=None):
    r, cdim = w.shape
    halves = g_sibling is not None
    block = 1 << 19
    if g_transposed:
        tc = _pick(cdim // 2 if halves else cdim, 2048)
        tr = _pick(r, max(LANES, block // tc), LANES)
        per_half = (cdim // 2) // tc
    else:
        rows = r // 2 if halves else r
        tc = _pick(cdim, 2048)
        tr = _pick(rows, max(8, block // tc), 8)
        if tr < 64 and rows > 64:
            tr, tc = _pick(rows, 1024, 8), _pick(cdim, 512)
        per_half = (r // 2) // tr
    emit_g = g_transposed or halves

    def body(w_ref, g_ref, *rest):
        m_ref, v_ref = rest[halves : halves + 2]
        outs = rest[halves + 2 :]
        gv = g_ref[...]
        if halves:
            along = pl.program_id(1 if g_transposed else 0)
            gv = jnp.where(along // per_half == lax.axis_index("c"), gv, rest[0][...])
        if g_transposed:
            gv = gv.T
        if emit_g:
            outs[0][...] = gv
        _adamw_update(w_ref[...], gv, m_ref[...], v_ref[...], *outs[-3:])

    spec = pl.BlockSpec((tr, tc), lambda i, j: (i, j))
    if g_transposed:
        g_spec = pl.BlockSpec((tc, tr), lambda i, j: (j % per_half if halves else j, i))
    else:
        g_spec = pl.BlockSpec((tr, tc), lambda i, j: (i % per_half if halves else i, j))
    n_out = 3 + emit_g
    res = pl.pallas_call(
        body,
        name=name,
        grid=(r // tr, cdim // tc),
        in_specs=[spec, g_spec] + [g_spec] * halves + [spec, spec],
        out_specs=[spec] * n_out,
        out_shape=[jax.ShapeDtypeStruct((r, cdim), F32)] * n_out,
        compiler_params=_params(("parallel", "parallel")),
    )(w, g, *([g_sibling] if halves else []), m, v)
    return res if emit_g else [g, *res]


def _place():
    return lax.axis_index("x"), lax.axis_index("y"), lax.axis_index("c")


def _remote(src, dst, send_sem, recv_sem, dev):
    return pltpu.make_async_remote_copy(src_ref=src, dst_ref=dst, send_sem=send_sem, recv_sem=recv_sem, device_id=dev, device_id_type=MESH)


ANY = pl.BlockSpec(memory_space=pl.ANY)


def _all_gather_small(v, name):
    r, w = v.shape

    def body(v_ref, o_ref, send, recv, lsem):
        x, y, c = _place()
        me = 4 * x + 2 * y + c
        mine = pltpu.make_async_copy(v_ref, o_ref.at[me], lsem)
        mine.start()
        sent = []
        for k in range(1, 8):
            px, py, pc = x ^ (k >> 2), y ^ ((k >> 1) & 1), c ^ (k & 1)
            cp = _remote(v_ref, o_ref.at[me], send.at[k - 1], recv.at[k - 1], (px, py, pc))
            cp.start()
            sent.append(cp)
        for k in range(1, 8):
            px, py, pc = x ^ (k >> 2), y ^ ((k >> 1) & 1), c ^ (k & 1)
            slot = o_ref.at[4 * px + 2 * py + pc]
            _remote(slot, slot, send.at[k - 1], recv.at[k - 1], (x, y, c)).wait_recv()
        for cp in sent:
            cp.wait_send()
        mine.wait()

    return pl.pallas_call(
        body,
        name=name,
        out_shape=jax.ShapeDtypeStruct((8, r, w), F32),
        in_specs=[pl.BlockSpec(memory_space=pltpu.VMEM)],
        out_specs=pl.BlockSpec(memory_space=pltpu.VMEM),
        scratch_shapes=[pltpu.SemaphoreType.DMA((7,)), pltpu.SemaphoreType.DMA((7,)), pltpu.SemaphoreType.DMA],
        compiler_params=pltpu.CompilerParams(vmem_limit_bytes=VMEM_LIMIT),
    )(v)


HBM = pl.BlockSpec(memory_space=pltpu.HBM)
SEM = pl.BlockSpec(memory_space=pltpu.SEMAPHORE)
EFFECT = pltpu.SideEffectType.DATAFLOW_SIDE_EFFECTING


def _other_chips(x, y):
    return [(1 - x, y), (x, 1 - y), (1 - x, 1 - y)]


def _bulk_start(name, srcs, land_shapes, n_copies, copies, after, lands_init=None):
    n, m = len(srcs), len(land_shapes)

    def body(*refs):
        src_refs, land_refs = refs[:n], refs[n : n + m]
        send, recv = refs[n + m + 1], refs[n + m + 2]
        token = refs[-1]
        for k, (s, d, dev) in enumerate(copies(src_refs, land_refs)):
            _remote(s, d, send.at[k], recv.at[k], dev).start()
        token[...] = jnp.zeros_like(token)

    lands = lands_init or [lax.empty(s.shape, s.dtype) for s in land_shapes]
    lands = [pltpu.with_memory_space_constraint(b, pltpu.HBM) for b in lands]
    out = pl.pallas_call(
        body,
        name=name,
        out_shape=(pltpu.SemaphoreType.DMA((n_copies,)), pltpu.SemaphoreType.DMA((n_copies,)),
                   *[pltpu.HBM(s.shape, s.dtype) for s in srcs], *[pltpu.HBM(s.shape, s.dtype) for s in land_shapes],
                   jax.ShapeDtypeStruct((8, LANES), F32)),
        in_specs=[HBM] * (n + m) + [ANY],
        out_specs=(SEM, SEM, *[HBM] * (n + m), pl.BlockSpec(memory_space=pltpu.VMEM)),
        input_output_aliases={i: 2 + i for i in range(n + m)},
        compiler_params=pltpu.CompilerParams(has_side_effects=EFFECT),
    )(*[pltpu.with_memory_space_constraint(s, pltpu.HBM) for s in srcs], *lands, after)
    return out[0], out[1], list(out[2 : 2 + n]), list(out[2 + n : 2 + n + m]), out[-1][0:1, 0:1]


def _bulk_wait(name, send, recv, srcs, lands, after, waits):
    n, m = len(srcs), len(lands)

    def body(*refs):
        src_refs, land_refs = refs[:n], refs[n : n + m]
        send_sem, recv_sem = refs[n + m], refs[n + m + 1]
        x, y, c = _place()
        for k, entry in enumerate(waits(src_refs, land_refs)):
            if len(entry) == 3:
                k, s, d = entry
            else:
                s, d = entry
            ref = s if s is not None else d
            cp = _remote(ref, ref if d is None else d, send_sem.at[k], recv_sem.at[k], (x, y, c))
            if s is not None:
                cp.wait_send()
            if d is not None:
                cp.wait_recv()

    out = pl.pallas_call(
        body,
        name=name,
        out_shape=tuple(pltpu.HBM(s.shape, s.dtype) for s in (*srcs, *lands)),
        in_specs=[HBM] * (n + m) + [SEM, SEM, ANY],
        out_specs=tuple([HBM] * (n + m)),
        input_output_aliases={i: i for i in range(n + m)},
        compiler_params=pltpu.CompilerParams(has_side_effects=EFFECT),
    )(*srcs, *lands, send, recv, after)
    return list(out[:n]), list(out[n:])


def _gather_start(shards, after, name, own_placed=False):
    def copies(src, land):
        x, y, c = _place()
        j = 2 * x + y
        return [(src[a].at[c], land[a].at[j, c], (px, py, c)) for a in range(len(shards)) for px, py in _other_chips(x, y)]

    shapes = [jax.ShapeDtypeStruct((4,) + s.shape, s.dtype) for s in shards]
    init = None
    if own_placed:
        j = 2 * lax.axis_index("x") + lax.axis_index("y")
        init = [lax.dynamic_update_slice(lax.empty(t.shape, t.dtype), s[None], (j, 0, 0, 0)) for t, s in zip(shapes, shards)]
    return _bulk_start(name, shards, shapes, 3 * len(shards), copies, after, init)


def _gather_wait(started, after, name):
    send, recv, srcs, lands, _ = started

    def waits(src, land):
        x, y, c = _place()
        return [(src[a].at[c], land[a].at[2 * px + py, c]) for a in range(len(srcs)) for px, py in _other_chips(x, y)]

    return _bulk_wait(name, send, recv, srcs, lands, after, waits)


def _forward_halves(lands, name):
    n = len(lands)

    def body(*refs):
        bufs = refs[n : 2 * n]
        send, recv = refs[2 * n :]
        x, y, c = _place()
        started = []
        for a in range(n):
            for k, (px, py) in enumerate(_other_chips(x, y)):
                blk = bufs[a].at[2 * px + py, c]
                cp = _remote(blk, blk, send.at[3 * a + k], recv.at[3 * a + k], (x, y, 1 - c))
                cp.start()
                started.append(cp)
        for a in range(n):
            for k, (px, py) in enumerate(_other_chips(x, y)):
                blk = bufs[a].at[2 * px + py, 1 - c]
                _remote(blk, blk, send.at[3 * a + k], recv.at[3 * a + k], (x, y, c)).wait_recv()
        for cp in started:
            cp.wait_send()

    return pl.pallas_call(
        body,
        name=name,
        out_shape=[jax.ShapeDtypeStruct(b.shape, b.dtype) for b in lands],
        in_specs=[ANY] * n,
        out_specs=[ANY] * n,
        input_output_aliases={i: i for i in range(n)},
        scratch_shapes=[pltpu.SemaphoreType.DMA((3 * n,)), pltpu.SemaphoreType.DMA((3 * n,))],
    )(*lands)


def _forward_start(lands, after, name):
    def copies(src, _):
        x, y, c = _place()
        blocks = [src[a].at[2 * px + py, c] for a in range(len(lands)) for px, py in _other_chips(x, y)]
        return [(b, b, (x, y, 1 - c)) for b in blocks]

    return _bulk_start(name, lands, [], 3 * len(lands), copies, after)


def _forward_wait(started, after, name):
    send, recv, bufs, _, _ = started

    def waits(src, _):
        x, y, c = _place()
        return [(src[a].at[2 * px + py, c], src[a].at[2 * px + py, 1 - c]) for a in range(len(bufs)) for px, py in _other_chips(x, y)]

    return _bulk_wait(name, send, recv, bufs, [], after, waits)[0]


def _place_own(shards, lands):
    j = 2 * lax.axis_index("x") + lax.axis_index("y")
    full = [lax.dynamic_update_slice(b, s[None], (j, 0, 0, 0)) for b, s in zip(lands, shards)]
    return [f.reshape(4 * f.shape[2] * 2, f.shape[3]) for f in full]


def _gather_finish(started, after, tag, own_placed=False):
    shards, lands = _gather_wait(started, after, "gather_wait_" + tag)
    lands = _forward_halves(lands, "gather_forward_" + tag)
    if own_placed:
        return [f.reshape(4 * f.shape[2] * 2, f.shape[3]) for f in lands]
    return _place_own(shards, lands)


def _relay_chips(x, y, c):
    return (x ^ (1 - c), y ^ c), (x ^ c, y ^ (1 - c))


def _relay_start(shards, after, name):
    def copies(src, land):
        x, y, c = _place()
        j = 2 * x + y
        return [(src[a].at[c], land[a].at[j, c], (*chip, c)) for a in range(len(shards)) for chip in _relay_chips(x, y, c)]

    shapes = [jax.ShapeDtypeStruct((4,) + s.shape, s.dtype) for s in shards]
    j = 2 * lax.axis_index("x") + lax.axis_index("y")
    init = [lax.dynamic_update_slice(lax.empty(t.shape, t.dtype), s[None], (j, 0, 0, 0)) for t, s in zip(shapes, shards)]
    return _bulk_start(name, shards, shapes, 2 * len(shards), copies, after, init)


def _relay_forward(started, after, tag):
    send, recv, shards, lands, _ = started
    na = len(shards)

    def landed(src, land):
        x, y, c = _place()
        _, (px, py) = _relay_chips(x, y, c)
        return [(2 * a + 1, None, land[a].at[2 * px + py, c]) for a in range(na)]

    shards, lands = _bulk_wait("relay_wait1_" + tag, send, recv, shards, lands, after, landed)

    def copies(src, _):
        x, y, c = _place()
        n1, (px, py) = _relay_chips(x, y, c)
        blocks = [src[a].at[2 * px + py, c] for a in range(na)]
        return [(b, b, (*n1, c)) for b in blocks]

    second = _bulk_start("relay_start2_" + tag, lands, [], na, copies, shards[0])
    return (send, recv, shards), second


def _relay_finish(relayed, after, tag):
    (send, recv, shards), (send2, recv2, lands, _, _) = relayed
    na = len(shards)

    def rest(src, land):
        x, y, c = _place()
        (px, py), _ = _relay_chips(x, y, c)
        waits = []
        for a in range(na):
            waits += [(2 * a, src[a].at[c], land[a].at[2 * px + py, c]), (2 * a + 1, src[a].at[c], None)]
        return waits

    shards, lands = _bulk_wait("relay_wait_" + tag, send, recv, shards, lands, after, rest)

    def stage2(src, _):
        x, y, c = _place()
        _, (px, py) = _relay_chips(x, y, c)
        return [(a, src[a].at[2 * px + py, c], src[a].at[2 * (1 - x) + (1 - y), c]) for a in range(na)]

    lands = _bulk_wait("relay_wait2_" + tag, send2, recv2, lands, [], shards[0], stage2)[0]
    lands = _forward_halves(lands, "gather_forward_" + tag)
    return [f.reshape(4 * f.shape[2] * 2, f.shape[3]) for f in lands]


def _gather_land(started, after, tag):
    shards, lands = _gather_wait(started, after, "gather_wait_" + tag)
    return shards, _forward_start(lands, shards[0], "forward_start_" + tag)


def _gather_done(landed, after, tag):
    shards, fwd = landed
    return _place_own(shards, _forward_wait(fwd, after, "forward_wait_" + tag))


def _swap_halves(grads, name):
    n = len(grads)

    def body(*refs):
        ins, outs = refs[:n], refs[n : 2 * n]
        send, recv = refs[2 * n :]
        x, y, c = _place()
        started = []
        for a in range(n):
            for s in range(4):
                cp = _remote(ins[a].at[s, 1 - c], outs[a].at[s], send.at[4 * a + s], recv.at[4 * a + s], (x, y, 1 - c))
                cp.start()
                started.append(cp)
        for cp in started:
            cp.wait_recv()
        for cp in started:
            cp.wait_send()

    return pl.pallas_call(
        body,
        name=name,
        out_shape=[jax.ShapeDtypeStruct((4,) + g.shape[2:], g.dtype) for g in grads],
        in_specs=[ANY] * n,
        out_specs=[ANY] * n,
        scratch_shapes=[pltpu.SemaphoreType.DMA((4 * n,)), pltpu.SemaphoreType.DMA((4 * n,))],
    )(*grads)


def _add_halves(grads, others, tag):
    outs = []
    for a, (g, o) in enumerate(zip(grads, others)):
        _, _, rh, cdim = g.shape
        tr = _pick(rh, 512, 16)

        def body(g_ref, o_ref, p_ref):
            p_ref[...] = (g_ref[...].astype(F32) + o_ref[...].astype(F32)).astype(BF16)

        outs.append(
            pl.pallas_call(
                body,
                name=f"add_halves_{tag}{a}",
                grid=(4, rh // tr),
                in_specs=[pl.BlockSpec((None, None, tr, cdim), lambda s, i: (s, lax.axis_index("c"), i, 0)),
                          pl.BlockSpec((None, tr, cdim), lambda s, i: (s, i, 0))],
                out_specs=pl.BlockSpec((None, tr, cdim), lambda s, i: (s, i, 0)),
                out_shape=jax.ShapeDtypeStruct((4, rh, cdim), BF16),
                compiler_params=_params(("parallel", "parallel")),
            )(g, o)
        )
    return outs


def _exchange_start(parts, after, name):
    def copies(src, land):
        x, y, c = _place()
        j = 2 * x + y
        return [(src[a].at[2 * px + py], land[a].at[j], (px, py, c)) for a in range(len(parts)) for px, py in _other_chips(x, y)]

    return _bulk_start(name, parts, [jax.ShapeDtypeStruct(p.shape, p.dtype) for p in parts], 3 * len(parts), copies, after)


def _exchange_finish(started, after, name):
    send, recv, srcs, lands, _ = started

    def waits(src, land):
        x, y, _ = _place()
        return [(src[a].at[2 * px + py], land[a].at[2 * px + py]) for a in range(len(srcs)) for px, py in _other_chips(x, y)]

    srcs, lands = _bulk_wait(name, send, recv, srcs, lands, after, waits)
    j = 2 * lax.axis_index("x") + lax.axis_index("y")
    return [lax.dynamic_update_slice(b, lax.dynamic_slice(p, (j, 0, 0), (1,) + p.shape[1:]), (j, 0, 0)) for b, p in zip(lands, srcs)]


def _sum_chips(recvd, tag):
    outs = []
    for a, g in enumerate(recvd):
        _, rh, cdim = g.shape
        tr = _pick(rh, 512, 16)

        def body(g_ref, o_ref):
            o_ref[...] = ((g_ref[0].astype(F32) + g_ref[1].astype(F32)) + g_ref[2].astype(F32)) + g_ref[3].astype(F32)

        outs.append(
            pl.pallas_call(
                body,
                name=f"sum_chips_{tag}{a}",
                grid=(rh // tr,),
                in_specs=[pl.BlockSpec((4, tr, cdim), lambda i: (0, i, 0))],
                out_specs=pl.BlockSpec((tr, cdim), lambda i: (i, 0)),
                out_shape=jax.ShapeDtypeStruct((rh, cdim), F32),
                compiler_params=_params(("parallel",)),
            )(g)
        )
    return outs


def _join_halves(halves, name):
    n = len(halves)

    def body(*refs):
        ins, outs = refs[:n], refs[n : 2 * n]
        send, recv = refs[2 * n :]
        x, y, c = _place()
        started = []
        for a in range(n):
            cp = _remote(ins[a], outs[a], send.at[a], recv.at[a], (x, y, 1 - c))
            cp.start()
            started.append(cp)
        for cp in started:
            cp.wait_recv()
        for cp in started:
            cp.wait_send()

    others = pl.pallas_call(
        body,
        name=name,
        out_shape=[jax.ShapeDtypeStruct(h.shape, h.dtype) for h in halves],
        in_specs=[ANY] * n,
        out_specs=[ANY] * n,
        scratch_shapes=[pltpu.SemaphoreType.DMA((n,)), pltpu.SemaphoreType.DMA((n,))],
    )(*halves)
    return list(zip(halves, others))


def _joined(mine, other):
    first = lax.axis_index("c") == 0
    return jnp.concatenate([jnp.where(first, mine, other), jnp.where(first, other, mine)], axis=0)


def _grad_views(grads):
    return [g.reshape(4, 2, g.shape[0] // 8, g.shape[1]) for g in grads]


def _scatter_start(grads, tag, after=None):
    views = _grad_views(grads)
    others = _swap_halves(views, "swap_halves_" + tag)
    mine = _add_halves(views, others, tag)
    return _exchange_start(mine, others[-1] if after is None else after, "exchange_start_" + tag)


def _swap_start(grads, after, tag):
    views = _grad_views(grads)

    def copies(src, land):
        x, y, c = _place()
        return [(src[a].at[s, 1 - c], land[a].at[s], (x, y, 1 - c)) for a in range(len(views)) for s in range(4)]

    shapes = [jax.ShapeDtypeStruct((4,) + v.shape[2:], v.dtype) for v in views]
    return _bulk_start("swap_start_" + tag, views, shapes, 4 * len(views), copies, after)


def _scatter_start_after_swap(swapped, after, tag):
    send, recv, views, lands, _ = swapped

    def waits(src, land):
        c = lax.axis_index("c")
        return [(src[a].at[s, 1 - c], land[a].at[s]) for a in range(len(views)) for s in range(4)]

    views, others = _bulk_wait("swap_wait_" + tag, send, recv, views, lands, after, waits)
    mine = _add_halves(views, others, tag)
    return _exchange_start(mine, others[-1], "exchange_start_" + tag)


def _join_start(halves, after, tag):
    def copies(src, land):
        x, y, c = _place()
        return [(src[a], land[a], (x, y, 1 - c)) for a in range(len(halves))]

    return _bulk_start("join_start_" + tag, halves, [jax.ShapeDtypeStruct(h.shape, h.dtype) for h in halves], len(halves), copies, after)


def _join_wait(started, after, tag):
    send, recv, halves, lands, _ = started
    halves, others = _bulk_wait("join_wait_" + tag, send, recv, halves, lands, after, lambda src, land: list(zip(src, land)))
    return list(zip(halves, others))


def _scatter_sums(started, after, tag):
    return _sum_chips(_exchange_finish(started, after, "exchange_wait_" + tag), tag)


def _scatter_finish(started, after, tag):
    return _join_halves(_scatter_sums(started, after, tag), "join_halves_" + tag)


def _t_bf16(w):
    return w.T.astype(BF16)


def kernel(x, c, ctx, c_ctx, w_ada, b_ada, norm1_g, w_in, mla_q_norm_g, w_q_up, mla_kv_norm_g, w_kv_up, gqa_q_norm_g, gqa_k_norm_g, w_br_a, w_br_b, w_out, norm2_g, w_up, conv_w, conv_b, w_down, final_norm_g, loss_target, m_c_ctx, m_w_ada, m_b_ada, m_norm1_g, m_w_in, m_mla_q_norm_g, m_w_q_up, m_mla_kv_norm_g, m_w_kv_up, m_gqa_q_norm_g, m_gqa_k_norm_g, m_w_br_a, m_w_br_b, m_w_out, m_norm2_g, m_w_up, m_conv_w, m_conv_b, m_w_down, m_final_norm_g, v_c_ctx, v_w_ada, v_b_ada, v_norm1_g, v_w_in, v_mla_q_norm_g, v_w_q_up, v_mla_kv_norm_g, v_w_kv_up, v_gqa_q_norm_g, v_gqa_k_norm_g, v_w_br_a, v_w_br_b, v_w_out, v_norm2_g, v_w_up, v_conv_w, v_conv_b, v_w_down, v_final_norm_g):
    T, D = x.shape[1], x.shape[2]
    C = ctx.shape[1]
    NA = w_ada.shape[2]
    NW = w_up.shape[2]
    F2 = 4 * NW
    FF = F2 // 2
    xi, yi, ci = _place()
    j = 2 * xi + yi
    me = 4 * xi + 2 * yi + ci
    tr = _pick(C, 256, 8)

    x2d, tgt, ctx2d = x[0], loss_target[0], ctx[0]
    fg = final_norm_g.reshape(1, D)
    cc = c_ctx.reshape(1, D)

    halve = lambda s: s.reshape(2, s.shape[0] // 2, s.shape[1])
    win_shard = halve(_t_bf16(w_in[0]))
    w0 = max(D, NW)
    pay = jnp.zeros((8, w0), F32).at[0:1, :D].set(c).at[1:4, :NW].set(conv_w[0])
    got = _all_gather_small(pay, "gather_cond")
    c_all = got[:, 0, :D]
    cw = jnp.concatenate([got[2 * s, 1:4, :NW] for s in range(4)], axis=1)
    s16 = jnp.concatenate([c_all, cc, jnp.zeros((7, D), F32)], axis=0)
    b_cols = lax.dynamic_slice(b_ada, (0, j * NA), (1, NA))
    ada_part = _mm(s16, w_ada[0], "NN", F32, "ada_fwd", act="silu", bias=b_cols)
    got = _all_gather_small(ada_part, "gather_ada")
    ada = jnp.concatenate([got[2 * s] for s in range(4)], axis=1)
    lat = lax.dynamic_slice(ada, (me, 0), (1, 6 * D))
    sh1, sc1, g1, sh2, sc2, g2 = [lat[:, k * D : (k + 1) * D] for k in range(6)]
    csh, csc = ada[8:9, :D], ada[8:9, D : 2 * D]

    ag_in = _relay_start([win_shard], got, "gather_start_in")
    t_in = ag_in[4]
    wq3 = (w_q_up[0] + t_in).reshape(MLA_Q_LORA, 2, MLA_NOPE + MLA_ROPE)
    wq_perm = jnp.concatenate([wq3[:, :, :MLA_NOPE].reshape(MLA_Q_LORA, -1), wq3[:, :, MLA_NOPE:].reshape(MLA_Q_LORA, -1)], axis=1)
    low = [_t_bf16(wq_perm), _t_bf16(w_kv_up[0] + t_in)]
    br = [_t_bf16(w_br_a[0] + t_in), _t_bf16(w_br_b[0] + t_in), (w_out[0] + t_in).astype(BF16)]
    ag_low = _gather_start([halve(s) for s in low], t_in, "gather_start_low")
    in_relayed = _relay_forward(ag_in, ag_low[2][0], "in")
    ag_br = _gather_start([halve(s) for s in br], in_relayed[1][4], "gather_start_br")
    ag_up = _gather_start([halve(_t_bf16(w_up[0] + t_in))], ag_br[4], "gather_start_up")
    ag_down = _gather_start([halve((w_down[0] + t_in).astype(BF16))], ag_up[4], "gather_start_down")
    sh1 = sh1 + ag_down[4]

    cos_a, ss_a = _rope_tables(C, T, MLA_ROPE)
    cos_b, ss_b = _rope_tables(C, T, GQA_HEAD_DIM)
    lcos_a, lss_a, lcos_b, lss_b = cos_a[:T], ss_a[:T], cos_b[:T], ss_b[:T]

    z_all = _norm_mod_fwd(x2d, norm1_g, sh1, sc1, "norm1_lat_fwd", tr, out_rows=T + C)
    z_all = _norm_mod_fwd(ctx2d, norm1_g, csh, csc, "norm1_ctx_fwd", tr, base=z_all, out_off=T)
    (win_t,) = _relay_finish(in_relayed, z_all, "in")
    kv_cols = KVP - LANES + MLA_ROPE
    e_kpe = MLA_KV_LORA + MLA_ROPE
    w_kvp = jnp.concatenate([win_t[:MLA_KV_LORA], win_t[e_kpe:kv_cols], win_t[MLA_KV_LORA:e_kpe], jnp.zeros((LANES - MLA_ROPE, D), BF16)], axis=0)

    pkv = _mm(z_all, w_kvp, "NT", F32, "proj_kv", tn=KVP)
    pq = _mm(z_all, win_t, "NT", F32, "proj_q", m=T, n=QC, b_off=kv_cols)
    low_landed = _gather_land(ag_low, pq, "low")
    pg = _mm(z_all, win_t, "NT", BF16, "proj_g", m=T, n=2 * D, b_off=kv_cols + QC, after=low_landed[1][4])
    wq_t, wkv_t = _gather_done(low_landed, pg, "low")
    ckv_n, kb2, vb2, kpe2 = _kprep_fwd(pkv, mla_kv_norm_g, gqa_k_norm_g, cos_a, ss_a, cos_b, ss_b, tr)
    kv_up = _mm(ckv_n, wkv_t, "NT", BF16, "kv_up")
    cq_n, qb2 = _qprep_fwd(pq, mla_q_norm_g, gqa_q_norm_g, lcos_b, lss_b, tr)
    q_a = _mm(cq_n, wq_t, "NT", F32, "q_up")
    qar = _qrope_fwd(q_a, lcos_a, lss_a, tr)

    a_q = [(qar, lambda h: 3 * (h // 2) + h % 2), (qar, lambda h: 3 * (h // 2) + 2)]
    a_k = [(kv_up, lambda h: 2 * h), (kpe2, lambda h: h % 2)]
    a_v = (kv_up, lambda h: 2 * h + 1)
    a_scale = float(MLA_NOPE + MLA_ROPE) ** -0.5
    b_q = [(qb2, lambda h: h)]
    b_k = [(kb2, lambda h: h)]
    b_v = (vb2, lambda h: h)
    b_scale = float(GQA_HEAD_DIM) ** -0.5
    tq_f = _pick(T, 2048)
    o_a, lse_a = _attn_fwd(a_q, a_k, a_v, MLA_HEADS, 1, MLA_V, a_scale, "attn_a_fwd", tq_f)
    br_landed = _gather_land(ag_br, o_a, "br")
    o_b, lse_b = _attn_fwd(b_q, b_k, b_v, GQA_HEADS, GQA_GROUP, GQA_HEAD_DIM, b_scale, "attn_b_fwd", tq_f, after=br_landed[1][4])
    wbra_t, wbrb_t, wout = _gather_done(br_landed, o_b, "br")
    up_landed = _gather_land(ag_up, o_b, "up")
    ya = _mm(o_a, wbra_t, "NT", BF16, "br_a", after=up_landed[1][4])
    yb = _mm(o_b, wbrb_t, "NT", BF16, "br_b")
    merged = _gates_fwd(pg, ya, yb, tr)
    att = _mm(merged, wout, "NN", F32, "out_proj")
    x1, z2 = _resid_norm2_fwd(x2d, att, g1, norm2_g, sh2, sc2, tr)
    (wup_t,) = _gather_done(up_landed, z2, "up")
    down_landed = _gather_land(ag_down, z2, "down")
    tc = _pick(FF, 128)
    u_a, u_b, hg = _ffn_up_conv(z2, wup_t, cw, conv_b, tc, down_landed[1][4])
    (wdown,) = _gather_done(down_landed, hg, "down")
    f = _mm(hg, wdown, "NN", F32, "ffn_down", tk=FF // 2)
    sq, dx2, d_fg, d_g2, df = _loss_head(x1, f, g2, fg, tgt, tr)
    loss = lax.psum(0.5 * jnp.sum(sq) / D, ("x", "y", "c"))

    du_a, du_b, dcw_a, dcw_b, dcb_a, dcb_b = _ffn_down_dx_conv_bwd(df, wdown, u_a, u_b, cw, conv_b, _pick(FF, 256), loss.reshape(1, 1))
    g_wdown = _mm(hg, df, "TN", BF16, "ffn_down_dw", tm=FF // 4)
    dz2 = _mm(du_a, wup_t, "NN", F32, "ffn_up_dx_a", tk=FF // 2)
    dz2 = _mm(du_b, wup_t, "NN", F32, "ffn_up_dx_b", b_off=FF, add=dz2, tk=FF // 2)
    g_wup_t = _mm(du_a, z2, "TN", BF16, "ffn_up_dw_a", out_rows=F2, tm=FF // 4)
    g_wup_t = _mm(du_b, z2, "TN", BF16, "ffn_up_dw_b", out_base=g_wup_t, out_off=FF, tm=FF // 4)
    sw_ffn = _swap_start([g_wdown, g_wup_t], sc2, "ffn")
    sc2 = sc2 + sw_ffn[4]
    dx1, datt, d_n2g, d_sh2, d_sc2, d_g1 = _resid_norm2_bwd(dz2, x1, dx2, att, norm2_g, sc2, g1, tr)

    dmerged = _mm(datt, wout, "NT", BF16, "out_proj_dx")
    rs_ffn = _scatter_start_after_swap(sw_ffn, dmerged, "ffn")
    lse_a = lse_a + rs_ffn[4]
    g_wout = _mm(merged, datt, "TN", BF16, "out_proj_dw")
    dya, dyb, dpg = _gates_bwd(dmerged, pg, ya, yb, tr)
    do_a = _mm(dya, wbra_t, "NN", BF16, "br_a_dx")
    g_wbra_t = _mm(dya, o_a, "TN", BF16, "br_a_dw")
    do_b = _mm(dyb, wbrb_t, "NN", BF16, "br_b_dx")
    g_wbrb_t = _mm(dyb, o_b, "TN", BF16, "br_b_dw")
    dqa2, dka2, dva2 = _attn_bwd(a_q, a_k, a_v, o_a, do_a, lse_a, MLA_HEADS, 1, MLA_V, a_scale, "attn_a_bwd", tq_f)
    dqb2, dkb2, dvb2 = _attn_bwd(b_q, b_k, b_v, o_b, do_b, lse_b, GQA_HEADS, GQA_GROUP, GQA_HEAD_DIM, b_scale, "attn_b_bwd", tq_f)
    dq_a = _qrope_bwd(dqa2, lcos_a, lss_a, tr)
    dcq_n = _mm(dq_a, wq_t, "NN", F32, "q_up_dx")
    g_wq_t = _mm(dq_a, cq_n, "TN", BF16, "q_up_dw")
    dpq, d_qg, d_gq = _qprep_bwd(pq, dcq_n, dqb2, mla_q_norm_g, gqa_q_norm_g, lcos_b, lss_b, tr)
    dkv_up, dkpe = _kgrad_split(dka2, dva2, cos_a, ss_a, tr)
    dckv_n = _mm(dkv_up, wkv_t, "NN", F32, "kv_up_dx")
    g_wkv_t = _mm(dkv_up, ckv_n, "TN", BF16, "kv_up_dw")
    rs_mix = _scatter_start([g_wq_t, g_wkv_t, g_wbra_t, g_wbrb_t, g_wout], "mix")
    dpkv, d_kvg, d_kg = _kprep_bwd(pkv, dckv_n, dkb2, dvb2, dkpe, mla_kv_norm_g + rs_mix[4], gqa_k_norm_g, cos_b, ss_b, tr)
    dz_kv = _mm(dpkv, w_kvp, "NN", F32, "proj_kv_dx")
    dz_lat = _mm(dpq, win_t, "NN", F32, "proj_q_dx", b_off=kv_cols, add=dz_kv)
    dz_lat = _mm(dpg, win_t, "NN", F32, "proj_g_dx", b_off=kv_cols + QC, add=dz_lat)
    _, d_n1g_c, d_csh, d_csc = _norm_mod_bwd(dz_kv, T // tr, ctx2d, norm1_g, csc, None, "norm1_ctx_bwd", tr)
    grad_x, d_n1g_l, d_sh1, d_sc1 = _norm_mod_bwd(dz_lat, 0, x2d, norm1_g, sc1, dx1, "norm1_lat_bwd", tr)

    zeros_d = jnp.zeros((1, D), F32)
    d_lat = jnp.concatenate([d_sh1, d_sc1, d_g1, d_sh2, d_sc2, d_g2], axis=1)
    d_ctx_part = jnp.concatenate([d_csh, d_csc], axis=1)
    flat = jnp.concatenate(
        [d_n1g_c + d_n1g_l, d_qg, d_kvg, d_gq, d_kg, d_n2g, dcb_a, dcb_b, d_fg,
         dcw_a.reshape(1, -1), dcw_b.reshape(1, -1), d_ctx_part, d_lat], axis=1)
    n_flat = flat.shape[1]
    n_rows = -(-n_flat // (8 * LANES)) * 8
    flat = jnp.pad(flat, ((0, 0), (0, n_rows * LANES - n_flat))).reshape(n_rows, LANES)
    got = _all_gather_small(flat, "gather_small_grads")
    tot = _sum_slots(got, "sum_small_grads").reshape(1, -1)
    sizes = [D, MLA_Q_LORA, MLA_KV_LORA, GQA_HEAD_DIM, GQA_HEAD_DIM, D, F2, D, 3 * FF, 3 * FF, 2 * D]
    offs = [0]
    for s in sizes:
        offs.append(offs[-1] + s)
    t_n1g, t_qg, t_kvg, t_gq, t_kg, t_n2g, t_cb, t_fg, t_cwa, t_cwb, t_ctx = [tot[:, offs[k] : offs[k + 1]] for k in range(len(sizes))]
    g_cw_full = jnp.concatenate([t_cwa.reshape(3, FF), t_cwb.reshape(3, FF)], axis=1)
    g_cw = lax.dynamic_slice(g_cw_full, (0, j * NW), (3, NW))
    d_lat_all = got.reshape(8, -1)[:, offs[-1] : offs[-1] + 6 * D]
    g16 = jnp.concatenate([d_lat_all, jnp.pad(t_ctx, ((0, 0), (0, 4 * D))), jnp.zeros((7, 6 * D), F32)], axis=0)
    g_b_ada = _sum_slots(g16.reshape(16, 1, 6 * D), "sum_b_ada")
    g16_cols = lax.dynamic_slice(g16, (0, j * NA), (16, NA))
    ds_part = _mm(g16_cols, w_ada[0], "NT", F32, "ada_dx")
    got = _all_gather_small(ds_part[8:16], "gather_ada_dx")
    ds_ctx = _sum_slots(jnp.stack([got[2 * s] for s in range(4)]), "sum_ada_dx")[0:1]
    g_c_ctx = _silu_grad_mul(ds_ctx, cc)

    g_kvp = _mm(dpkv, z_all, "TN", BF16, "proj_kv_dw")
    nk = MLA_KV_LORA + 2 * GQA_KV_HEADS * GQA_HEAD_DIM
    g_kv = jnp.concatenate([g_kvp[:MLA_KV_LORA], g_kvp[nk : nk + MLA_ROPE], g_kvp[MLA_KV_LORA:nk]], axis=0)
    g_win_t = _mm(dpq, z_all, "TN", BF16, "proj_q_dw", out_rows=kv_cols + QC + 2 * D, out_off=kv_cols, tm=QC // 2)
    g_win_t = _mm(dpg, z_all, "TN", BF16, "proj_g_dw", out_base=g_win_t, out_off=kv_cols + QC)
    g_win_t = lax.dynamic_update_slice(g_win_t, g_kv, (0, 0))
    sw_in = _swap_start([g_win_t], got, "in")

    h_ffn = _scatter_sums(rs_ffn, sw_in[2][0], "ffn")
    j_ffn = _join_start(h_ffn, grad_x, "ffn")
    h_mix = _scatter_sums(rs_mix, j_ffn[2][0], "mix")
    j_mix = _join_start(h_mix, j_ffn[2][0], "mix")
    rs_in = _scatter_start_after_swap(sw_in, j_mix[2][0], "in")
    g_w_ada = _mm(s16, g16_cols, "TN", F32, "ada_dw", act="silu", after=rs_in[4])
    _, d_ada, m_ada, v_ada = _adamw(w_ada[0], g_w_ada, m_w_ada[0], v_w_ada[0], "adamw_w_ada")
    r_wdown, r_wup = _join_wait(j_ffn, d_ada, "ffn")
    r_wq, r_wkv, r_wbra, r_wbrb, r_wout = _join_wait(j_mix, d_ada, "mix")
    gq_p = _joined(*r_wq).T
    gq = jnp.concatenate([gq_p[:, : 2 * MLA_NOPE].reshape(MLA_Q_LORA, 2, MLA_NOPE), gq_p[:, 2 * MLA_NOPE :].reshape(MLA_Q_LORA, 2, MLA_ROPE)], axis=2)
    grads = {
        "c_ctx": g_c_ctx.reshape(D), "w_ada": g_w_ada[None], "b_ada": g_b_ada, "norm1_g": t_n1g,
        "mla_q_norm_g": t_qg, "w_q_up": gq.reshape(1, MLA_Q_LORA, -1), "mla_kv_norm_g": t_kvg, "w_kv_up": r_wkv,
        "gqa_q_norm_g": t_gq, "gqa_k_norm_g": t_kg, "w_br_a": r_wbra, "w_br_b": r_wbrb, "w_out": r_wout,
        "norm2_g": t_n2g, "w_up": r_wup, "conv_w": g_cw[None], "conv_b": t_cb, "w_down": r_wdown,
        "final_norm_g": t_fg.reshape(D),
    }
    arrives_transposed = ("w_kv_up", "w_br_a", "w_br_b", "w_up")
    arrives_halved = arrives_transposed + ("w_out", "w_down")
    weights = dict(c_ctx=c_ctx, w_ada=w_ada, b_ada=b_ada, norm1_g=norm1_g, w_in=w_in, mla_q_norm_g=mla_q_norm_g, w_q_up=w_q_up,
                   mla_kv_norm_g=mla_kv_norm_g, w_kv_up=w_kv_up, gqa_q_norm_g=gqa_q_norm_g, gqa_k_norm_g=gqa_k_norm_g, w_br_a=w_br_a,
                   w_br_b=w_br_b, w_out=w_out, norm2_g=norm2_g, w_up=w_up, conv_w=conv_w, conv_b=conv_b, w_down=w_down,
                   final_norm_g=final_norm_g)
    m_in = dict(c_ctx=m_c_ctx, w_ada=m_w_ada, b_ada=m_b_ada, norm1_g=m_norm1_g, w_in=m_w_in, mla_q_norm_g=m_mla_q_norm_g,
                w_q_up=m_w_q_up, mla_kv_norm_g=m_mla_kv_norm_g, w_kv_up=m_w_kv_up, gqa_q_norm_g=m_gqa_q_norm_g,
                gqa_k_norm_g=m_gqa_k_norm_g, w_br_a=m_w_br_a, w_br_b=m_w_br_b, w_out=m_w_out, norm2_g=m_norm2_g, w_up=m_w_up,
                conv_w=m_conv_w, conv_b=m_conv_b, w_down=m_w_down, final_norm_g=m_final_norm_g)
    v_in = dict(c_ctx=v_c_ctx, w_ada=v_w_ada, b_ada=v_b_ada, norm1_g=v_norm1_g, w_in=v_w_in, mla_q_norm_g=v_mla_q_norm_g,
                w_q_up=v_w_q_up, mla_kv_norm_g=v_mla_kv_norm_g, w_kv_up=v_w_kv_up, gqa_q_norm_g=v_gqa_q_norm_g,
                gqa_k_norm_g=v_gqa_k_norm_g, w_br_a=v_w_br_a, w_br_b=v_w_br_b, w_out=v_w_out, norm2_g=v_norm2_g, w_up=v_w_up,
                conv_w=v_conv_w, conv_b=v_conv_b, w_down=v_w_down, final_norm_g=v_final_norm_g)
    names = list(weights)
    big = [n for n in names if weights[n].ndim == 3 and weights[n].shape[1] >= 8]
    small = [n for n in names if n not in big]
    delta, new_m, new_v = {}, {}, {}

    def update(n):
        shp = weights[n].shape
        two_d = lambda a: a.reshape(shp[1], shp[2])
        g_t = n in arrives_transposed
        if n in arrives_halved:
            g_in, g_sib = grads[n]
        else:
            g_in, g_sib = two_d(grads[n].astype(F32)), None
        g_, d_, m_, v_ = _adamw(two_d(weights[n]), g_in, two_d(m_in[n]), two_d(v_in[n]), "adamw_" + n, g_transposed=g_t, g_sibling=g_sib)
        grads[n], delta[n], new_m[n], new_v[n] = g_.reshape(shp), d_.reshape(shp), m_.reshape(shp), v_.reshape(shp)

    delta["w_ada"], new_m["w_ada"], new_v["w_ada"] = d_ada[None], m_ada[None], v_ada[None]
    early = [n for n in big if n not in ("w_in", "w_ada")]
    for n in early[:-1]:
        update(n)
    done = sum(delta[n][0, 0:1, 0:1] for n in early[:-1])
    j_in = _join_start(_scatter_sums(rs_in, done, "in"), done, "in")
    last = early[-1]
    grads[last] = (grads[last][0] + j_in[4], grads[last][1])
    update(last)
    ((g_mine, g_sib),) = _join_wait(j_in, delta[last], "in")
    g_, d_, m_, v_ = _adamw(w_in[0].T, g_mine, m_w_in[0].T, v_w_in[0].T, "adamw_w_in", g_sibling=g_sib)
    grads["w_in"], delta["w_in"], new_m["w_in"], new_v["w_in"] = g_.T[None], d_.T[None], m_.T[None], v_.T[None]
    grads = {n: grads[n].reshape(weights[n].shape).astype(F32) for n in names}

    slab = lambda tree: [tree[n].reshape(-1, LANES) for n in small]
    d_, m_, v_ = _adamw_many(slab(weights), slab(grads), slab(m_in), slab(v_in), "adamw_small")
    for k, n in enumerate(small):
        shp = weights[n].shape
        delta[n], new_m[n], new_v[n] = d_[k].reshape(shp), m_[k].reshape(shp), v_[k].reshape(shp)

    return (loss, grad_x[None], *[grads[n] for n in names], *[delta[n] for n in names], *[new_m[n] for n in names],
            *[new_v[n] for n in names])
```

```python
import math

import jax
import jax.numpy as jnp
from jax import lax
from jax.experimental import pallas as pl
from jax.experimental.pallas import tpu as pltpu

F32 = jnp.float32
BF16 = jnp.bfloat16
MESH = pl.DeviceIdType.MESH

NORM_EPS = 1e-6
ROPE_THETA = 10000.0
GRID_W = 64
MLA_HEADS = 8
MLA_Q_LORA = 768
MLA_KV_LORA = 512
MLA_NOPE = 128
MLA_ROPE = 64
MLA_V = 128
GQA_HEADS = 8
GQA_KV_HEADS = 2
GQA_HEAD_DIM = 128
GQA_GROUP = GQA_HEADS // GQA_KV_HEADS
LANES = 128
KVP = MLA_KV_LORA + 2 * GQA_KV_HEADS * GQA_HEAD_DIM + LANES
QC = MLA_Q_LORA + GQA_HEADS * GQA_HEAD_DIM

ADAM_LR = 0.001
ADAM_B1 = 0.9
ADAM_B2 = 0.999
ADAM_EPS = 1e-08
ADAM_WD = 0.01
ADAM_STEP = 10

VMEM_LIMIT = 56 * 1024 * 1024


def _pick(dim, target, mult=LANES):
    t = (min(target, dim) // mult) * mult
    while t >= mult:
        if dim % t == 0:
            return t
        t -= mult
    return dim


def _params(sem):
    return pltpu.CompilerParams(dimension_semantics=sem, vmem_limit_bytes=VMEM_LIMIT)


_DIMS = {"NN": (((1,), (0,)), ((), ())), "NT": (((1,), (1,)), ((), ())), "TN": (((0,), (0,)), ((), ()))}


MM_VMEM_BUDGET = 36 * 1024 * 1024


def _mm_tiles(M, N, K, sa, sb, so, tm, tn, tk):
    tm, tn, tk = _pick(M, tm), _pick(N, tn), _pick(K, tk)

    def need(t):
        return 2 * (tm * t * sa + t * tn * sb) + 2 * tm * tn * so + (tm * tn * 4 if t < K else 0)

    while need(tk) > MM_VMEM_BUDGET and tk > LANES:
        smaller = _pick(K, tk - LANES)
        if smaller >= tk:
            break
        tk = smaller
    return tm, tn, tk


def _window(block, index, offsets):
    if not any(offsets):
        return pl.BlockSpec(block, index)
    for t, o in zip(block, offsets):
        assert o % 16 == 0 and t % 16 == 0, (block, offsets)

    def at(i, j, k):
        return tuple(pl.multiple_of(o + p * t, math.gcd(o, t)) for p, t, o in zip(index(i, j, k), block, offsets))

    return pl.BlockSpec(tuple(pl.Element(t) for t in block), at)


def _mm(a, b, mode, out_dtype, name, m=None, n=None, k=None, b_off=0, add=None, out_rows=None, out_base=None, out_off=0,
        tm=1024, tn=1024, tk=2304, act=None, bias=None, after=None):
    if mode == "NN":
        M, K, N = m or a.shape[0], k or a.shape[1], b.shape[1]
    elif mode == "NT":
        M, K, N = m or a.shape[0], a.shape[1], n or b.shape[0]
    else:
        M, K, N = a.shape[1], k or a.shape[0], b.shape[1]
    tm, tn, tk = _mm_tiles(M, N, K, a.dtype.itemsize, b.dtype.itemsize, jnp.dtype(out_dtype).itemsize, tm, tn, tk)
    nk = K // tk
    dims = _DIMS[mode]
    n_in = 2 + (bias is not None) + (add is not None) + (out_base is not None) + (after is not None)

    def body(*refs):
        a_ref, b_ref = refs[:2]
        bias_ref = refs[2] if bias is not None else None
        add_ref = refs[2 + (bias is not None)] if add is not None else None
        o_ref = refs[n_in]
        av = a_ref[...]
        if act == "silu":
            av = av * jax.nn.sigmoid(av)
        part = lax.dot_general(av.astype(BF16), b_ref[...].astype(BF16), dims, preferred_element_type=F32)

        def finish(r):
            if bias is not None:
                r = r + bias_ref[...]
            if add is not None:
                r = r + add_ref[...]
            o_ref[...] = r.astype(out_dtype)

        if nk == 1:
            finish(part)
            return
        acc = refs[-1]
        k = pl.program_id(2)

        @pl.when(k == 0)
        def _():
            acc[...] = part

        @pl.when(jnp.logical_and(k > 0, k < nk - 1))
        def _():
            acc[...] += part

        @pl.when(k == nk - 1)
        def _():
            finish(acc[...] + part)

    a_spec = pl.BlockSpec((tk, tm), lambda i, j, k: (k, i)) if mode == "TN" else pl.BlockSpec((tm, tk), lambda i, j, k: (i, k))
    if mode == "NT":
        b_spec = _window((tn, tk), lambda i, j, k: (j, k), (b_off, 0))
    else:
        b_spec = _window((tk, tn), lambda i, j, k: (k, j), (b_off, 0))
    in_specs, args = [a_spec, b_spec], [a, b]
    if bias is not None:
        in_specs.append(pl.BlockSpec((1, tn), lambda i, j, k: (0, j)))
        args.append(bias)
    if add is not None:
        in_specs.append(pl.BlockSpec((tm, tn), lambda i, j, k: (i, j)))
        args.append(add)
    aliases = {}
    if after is not None:
        in_specs.append(pl.BlockSpec(after.shape, lambda i, j, k: (0, 0)))
        args.append(after)
    if out_base is not None:
        aliases = {len(args): 0}
        in_specs.append(ANY)
        args.append(out_base)
        out_rows = out_base.shape[0]
    return pl.pallas_call(
        body,
        name=name,
        grid=(M // tm, N // tn, nk),
        in_specs=in_specs,
        out_specs=_window((tm, tn), lambda i, j, k: (i, j), (out_off, 0)),
        out_shape=jax.ShapeDtypeStruct((out_rows or M, N), out_dtype),
        input_output_aliases=aliases,
        scratch_shapes=[pltpu.VMEM((tm, tn), F32)] if nk > 1 else [],
        compiler_params=_params(("parallel", "parallel", "arbitrary")),
    )(*args)


def _rms(x):
    r = lax.rsqrt(jnp.mean(x * x, axis=-1, keepdims=True) + NORM_EPS)
    return x * r, r


def _rms_bwd(xh, r, dxh):
    return r * (dxh - xh * jnp.mean(dxh * xh, axis=-1, keepdims=True))


def _swap(x, q):
    lane = lax.broadcasted_iota(jnp.int32, x.shape, 1)
    even = ((lane // q) % 2) == 0
    return jnp.where(even, pltpu.roll(x, LANES - q, 1), pltpu.roll(x, q, 1))


def _rope(x, cos, ss, q):
    return x * cos + _swap(x, q) * ss


def _rope_t(d, cos, ss, q):
    return d * cos + _swap(d * ss, q)


def _csum(x):
    return jnp.sum(x, axis=0, keepdims=True)


def _rows(tr, w, off=0):
    return pl.BlockSpec((tr, w), lambda i: (i + off, 0))


def _bcast(w):
    return pl.BlockSpec((1, w), lambda i: (0, 0))


def _acc_init(i, refs):
    @pl.when(i == 0)
    def _():
        for r in refs:
            r[...] = jnp.zeros_like(r)


def _rope_tables(n_ctx, n_lat, rot_dim):
    rows = n_lat // GRID_W
    row = jnp.repeat(jnp.arange(rows, dtype=F32), GRID_W)
    col = jnp.tile(jnp.arange(GRID_W, dtype=F32), rows)
    half = rot_dim // 2
    inv_freq = ROPE_THETA ** (-jnp.arange(0, half, 2, dtype=F32) / half)
    ar, ac = row[:, None] * inv_freq, col[:, None] * inv_freq
    cos = jnp.concatenate([jnp.cos(ar), jnp.cos(ar), jnp.cos(ac), jnp.cos(ac)], axis=-1)
    ss = jnp.concatenate([-jnp.sin(ar), jnp.sin(ar), -jnp.sin(ac), jnp.sin(ac)], axis=-1)
    cos = jnp.tile(cos, (1, LANES // rot_dim))
    ss = jnp.tile(ss, (1, LANES // rot_dim))
    cos = jnp.concatenate([cos, jnp.ones((n_ctx, LANES), F32)], axis=0)
    ss = jnp.concatenate([ss, jnp.zeros((n_ctx, LANES), F32)], axis=0)
    return cos, ss


def _norm_mod_fwd(x2d, g, sh, sc, name, tr, out_rows=None, base=None, out_off=0):
    n, d = x2d.shape

    def body(x_ref, g_ref, sh_ref, sc_ref, *rest):
        xh, _ = _rms(x_ref[...])
        rest[-1][...] = ((xh * g_ref[...]) * (1.0 + sc_ref[...]) + sh_ref[...]).astype(BF16)

    args, in_specs, aliases = [x2d, g, sh, sc], [_rows(tr, d), _bcast(d), _bcast(d), _bcast(d)], {}
    if base is not None:
        args.append(base)
        in_specs.append(ANY)
        aliases = {4: 0}
        out_rows = base.shape[0]
    return pl.pallas_call(
        body,
        name=name,
        grid=(n // tr,),
        in_specs=in_specs,
        out_specs=_rows(tr, d, out_off // tr),
        out_shape=jax.ShapeDtypeStruct((out_rows or n, d), BF16),
        input_output_aliases=aliases,
        compiler_params=_params(("parallel",)),
    )(*args)


def _norm_mod_bwd(dz, dz_off, x2d, g, sc, dres, name, tr):
    n, d = x2d.shape
    want_dx = dres is not None

    def body(*refs):
        if want_dx:
            dz_ref, x_ref, g_ref, sc_ref, dres_ref, dx_ref, dg_ref, dsh_ref, dsc_ref = refs
        else:
            dz_ref, x_ref, g_ref, sc_ref, dg_ref, dsh_ref, dsc_ref = refs
        _acc_init(pl.program_id(0), [dg_ref, dsh_ref, dsc_ref])
        xh, r = _rms(x_ref[...])
        dzv = dz_ref[...]
        gv = g_ref[...]
        dsc_ref[...] += _csum(dzv * (xh * gv))
        dsh_ref[...] += _csum(dzv)
        dh = dzv * (1.0 + sc_ref[...])
        dg_ref[...] += _csum(dh * xh)
        if want_dx:
            dx_ref[...] = _rms_bwd(xh, r, dh * gv) + dres_ref[...]

    in_specs = [_rows(tr, d, dz_off), _rows(tr, d), _bcast(d), _bcast(d)]
    args = [dz, x2d, g, sc]
    out_specs = [_bcast(d)] * 3
    out_shape = [jax.ShapeDtypeStruct((1, d), F32)] * 3
    if want_dx:
        in_specs.append(_rows(tr, d))
        args.append(dres)
        out_specs = [_rows(tr, d)] + out_specs
        out_shape = [jax.ShapeDtypeStruct((n, d), F32)] + out_shape
    res = pl.pallas_call(
        body,
        name=name,
        grid=(n // tr,),
        in_specs=in_specs,
        out_specs=out_specs,
        out_shape=out_shape,
        compiler_params=_params(("arbitrary",)),
    )(*args)
    return res if want_dx else (None, *res)


_QA, _QB = MLA_ROPE // 4, GQA_HEAD_DIM // 4


def _kprep_fwd(pkv, kvg, kg, cos_a, ss_a, cos_b, ss_b, tr):
    n = pkv.shape[0]
    nb = GQA_KV_HEADS * GQA_HEAD_DIM

    def body(p_ref, kvg_ref, kg_ref, ca, sa, cb, sb, ckv_ref, kb_ref, vb_ref, kpe_ref):
        p = p_ref[...]
        xh, _ = _rms(p[:, :MLA_KV_LORA])
        ckv_ref[...] = (xh * kvg_ref[...]).astype(BF16)
        for e in range(GQA_KV_HEADS):
            lo = MLA_KV_LORA + e * GQA_HEAD_DIM
            kh, _ = _rms(p[:, lo : lo + GQA_HEAD_DIM])
            kb_ref[:, e * GQA_HEAD_DIM : (e + 1) * GQA_HEAD_DIM] = _rope(kh * kg_ref[...], cb[...], sb[...], _QB).astype(BF16)
        vb_ref[...] = p[:, MLA_KV_LORA + nb : MLA_KV_LORA + 2 * nb].astype(BF16)
        kr = _rope(p[:, MLA_KV_LORA + 2 * nb :], ca[...], sa[...], _QA)
        kpe_ref[:, :LANES] = kr.astype(BF16)
        kpe_ref[:, LANES:] = pltpu.roll(kr, MLA_ROPE, 1).astype(BF16)

    return pl.pallas_call(
        body,
        name="kprep_fwd",
        grid=(n // tr,),
        in_specs=[_rows(tr, KVP), _bcast(MLA_KV_LORA), _bcast(GQA_HEAD_DIM)] + [_rows(tr, LANES)] * 4,
        out_specs=[_rows(tr, MLA_KV_LORA), _rows(tr, nb), _rows(tr, nb), _rows(tr, 2 * LANES)],
        out_shape=[jax.ShapeDtypeStruct((n, w), BF16) for w in (MLA_KV_LORA, nb, nb, 2 * LANES)],
        compiler_params=_params(("parallel",)),
    )(pkv, kvg, kg, cos_a, ss_a, cos_b, ss_b)


def _kprep_bwd(pkv, dckv, dkb, dvb, dkpe, kvg, kg, cos_b, ss_b, tr):
    n = pkv.shape[0]
    nb = GQA_KV_HEADS * GQA_HEAD_DIM

    def body(p_ref, dckv_ref, dkb_ref, dvb_ref, dkpe_ref, kvg_ref, kg_ref, cb, sb, dp_ref, dkvg_ref, dkg_ref):
        _acc_init(pl.program_id(0), [dkvg_ref, dkg_ref])
        p = p_ref[...]
        xh, r = _rms(p[:, :MLA_KV_LORA])
        dn = dckv_ref[...]
        dkvg_ref[...] += _csum(dn * xh)
        dp_ref[:, :MLA_KV_LORA] = _rms_bwd(xh, r, dn * kvg_ref[...]).astype(BF16)
        for e in range(GQA_KV_HEADS):
            lo = MLA_KV_LORA + e * GQA_HEAD_DIM
            kh, rk = _rms(p[:, lo : lo + GQA_HEAD_DIM])
            dk = _rope_t(dkb_ref[:, e * GQA_HEAD_DIM : (e + 1) * GQA_HEAD_DIM], cb[...], sb[...], _QB)
            dkg_ref[...] += _csum(dk * kh)
            dp_ref[:, lo : lo + GQA_HEAD_DIM] = _rms_bwd(kh, rk, dk * kg_ref[...]).astype(BF16)
        dp_ref[:, MLA_KV_LORA + nb : MLA_KV_LORA + 2 * nb] = dvb_ref[...].astype(BF16)
        dp_ref[:, MLA_KV_LORA + 2 * nb :] = dkpe_ref[...].astype(BF16)

    return pl.pallas_call(
        body,
        name="kprep_bwd",
        grid=(n // tr,),
        in_specs=[_rows(tr, KVP), _rows(tr, MLA_KV_LORA), _rows(tr, nb), _rows(tr, nb), _rows(tr, LANES),
                  _bcast(MLA_KV_LORA), _bcast(GQA_HEAD_DIM), _rows(tr, LANES), _rows(tr, LANES)],
        out_specs=[_rows(tr, KVP), _bcast(MLA_KV_LORA), _bcast(GQA_HEAD_DIM)],
        out_shape=[jax.ShapeDtypeStruct((n, KVP), BF16), jax.ShapeDtypeStruct((1, MLA_KV_LORA), F32),
                   jax.ShapeDtypeStruct((1, GQA_HEAD_DIM), F32)],
        compiler_params=_params(("arbitrary",)),
    )(pkv, dckv, dkb, dvb, dkpe, kvg, kg, cos_b, ss_b)


def _kgrad_split(dka, dva, cos_a, ss_a, tr):
    n = dka.shape[0]
    wk = MLA_HEADS * 2 * LANES

    def body(dk_ref, dv_ref, ca, sa, dkv_ref, dkpe_ref):
        even = jnp.zeros((tr, LANES), F32)
        odd = jnp.zeros((tr, LANES), F32)
        for h in range(MLA_HEADS):
            dkv_ref[:, 2 * h * LANES : (2 * h + 1) * LANES] = dk_ref[:, 2 * h * LANES : (2 * h + 1) * LANES].astype(BF16)
            dkv_ref[:, (2 * h + 1) * LANES : (2 * h + 2) * LANES] = dv_ref[:, h * MLA_V : (h + 1) * MLA_V].astype(BF16)
            part = dk_ref[:, (2 * h + 1) * LANES : (2 * h + 2) * LANES]
            if h % 2 == 0:
                even = even + part
            else:
                odd = odd + part
        lane = lax.broadcasted_iota(jnp.int32, (tr, LANES), 1)
        low = lane < MLA_ROPE
        both = jnp.where(low, even, odd)
        tot = jnp.where(low, both + pltpu.roll(both, MLA_ROPE, 1), 0.0)
        dkpe_ref[...] = _rope_t(tot, ca[...], sa[...], _QA)

    return pl.pallas_call(
        body,
        name="kgrad_split",
        grid=(n // tr,),
        in_specs=[_rows(tr, wk), _rows(tr, MLA_HEADS * MLA_V), _rows(tr, LANES), _rows(tr, LANES)],
        out_specs=[_rows(tr, wk), _rows(tr, LANES)],
        out_shape=[jax.ShapeDtypeStruct((n, wk), BF16), jax.ShapeDtypeStruct((n, LANES), F32)],
        compiler_params=_params(("parallel",)),
    )(dka, dva, cos_a, ss_a)


def _qprep_fwd(pq, qg, gq, cos_b, ss_b, tr):
    n = pq.shape[0]
    nq = GQA_HEADS * GQA_HEAD_DIM

    def body(p_ref, qg_ref, gq_ref, cb, sb, cq_ref, qb_ref):
        xh, _ = _rms(p_ref[:, :MLA_Q_LORA])
        cq_ref[...] = (xh * qg_ref[...]).astype(BF16)
        for h in range(GQA_HEADS):
            lo = MLA_Q_LORA + h * GQA_HEAD_DIM
            qh, _ = _rms(p_ref[:, lo : lo + GQA_HEAD_DIM])
            qb_ref[:, h * GQA_HEAD_DIM : (h + 1) * GQA_HEAD_DIM] = _rope(qh * gq_ref[...], cb[...], sb[...], _QB).astype(BF16)

    return pl.pallas_call(
        body,
        name="qprep_fwd",
        grid=(n // tr,),
        in_specs=[_rows(tr, QC), _bcast(MLA_Q_LORA), _bcast(GQA_HEAD_DIM), _rows(tr, LANES), _rows(tr, LANES)],
        out_specs=[_rows(tr, MLA_Q_LORA), _rows(tr, nq)],
        out_shape=[jax.ShapeDtypeStruct((n, MLA_Q_LORA), BF16), jax.ShapeDtypeStruct((n, nq), BF16)],
        compiler_params=_params(("parallel",)),
    )(pq, qg, gq, cos_b, ss_b)


def _qprep_bwd(pq, dcq, dqb, qg, gq, cos_b, ss_b, tr):
    n = pq.shape[0]
    nq = GQA_HEADS * GQA_HEAD_DIM

    def body(p_ref, dcq_ref, dqb_ref, qg_ref, gq_ref, cb, sb, dp_ref, dqg_ref, dgq_ref):
        _acc_init(pl.program_id(0), [dqg_ref, dgq_ref])
        xh, r = _rms(p_ref[:, :MLA_Q_LORA])
        dn = dcq_ref[...]
        dqg_ref[...] += _csum(dn * xh)
        dp_ref[:, :MLA_Q_LORA] = _rms_bwd(xh, r, dn * qg_ref[...]).astype(BF16)
        for h in range(GQA_HEADS):
            lo = MLA_Q_LORA + h * GQA_HEAD_DIM
            qh, rq = _rms(p_ref[:, lo : lo + GQA_HEAD_DIM])
            dq = _rope_t(dqb_ref[:, h * GQA_HEAD_DIM : (h + 1) * GQA_HEAD_DIM], cb[...], sb[...], _QB)
            dgq_ref[...] += _csum(dq * qh)
            dp_ref[:, lo : lo + GQA_HEAD_DIM] = _rms_bwd(qh, rq, dq * gq_ref[...]).astype(BF16)

    return pl.pallas_call(
        body,
        name="qprep_bwd",
        grid=(n // tr,),
        in_specs=[_rows(tr, QC), _rows(tr, MLA_Q_LORA), _rows(tr, nq), _bcast(MLA_Q_LORA), _bcast(GQA_HEAD_DIM),
                  _rows(tr, LANES), _rows(tr, LANES)],
        out_specs=[_rows(tr, QC), _bcast(MLA_Q_LORA), _bcast(GQA_HEAD_DIM)],
        out_shape=[jax.ShapeDtypeStruct((n, QC), BF16), jax.ShapeDtypeStruct((1, MLA_Q_LORA), F32),
                   jax.ShapeDtypeStruct((1, GQA_HEAD_DIM), F32)],
        compiler_params=_params(("arbitrary",)),
    )(pq, dcq, dqb, qg, gq, cos_b, ss_b)


_QA_COLS = MLA_HEADS * (MLA_NOPE + MLA_ROPE)


def _qrope_fwd(qa, cos_a, ss_a, tr):
    n = qa.shape[0]

    def body(q_ref, ca, sa, o_ref):
        for j in range(MLA_HEADS // 2):
            lo = 3 * j * LANES
            o_ref[:, lo : lo + 2 * LANES] = q_ref[:, lo : lo + 2 * LANES].astype(BF16)
            o_ref[:, lo + 2 * LANES : lo + 3 * LANES] = _rope(q_ref[:, lo + 2 * LANES : lo + 3 * LANES], ca[...], sa[...], _QA).astype(BF16)

    return pl.pallas_call(
        body,
        name="qrope_fwd",
        grid=(n // tr,),
        in_specs=[_rows(tr, _QA_COLS), _rows(tr, LANES), _rows(tr, LANES)],
        out_specs=_rows(tr, _QA_COLS),
        out_shape=jax.ShapeDtypeStruct((n, _QA_COLS), BF16),
        compiler_params=_params(("parallel",)),
    )(qa, cos_a, ss_a)


def _qrope_bwd(dq2, cos_a, ss_a, tr):
    n = dq2.shape[0]

    def body(d_ref, ca, sa, o_ref):
        for j in range(MLA_HEADS // 2):
            lo = 3 * j * LANES
            h0, h1 = 2 * j, 2 * j + 1
            o_ref[:, lo : lo + LANES] = d_ref[:, 2 * h0 * LANES : (2 * h0 + 1) * LANES].astype(BF16)
            o_ref[:, lo + LANES : lo + 2 * LANES] = d_ref[:, 2 * h1 * LANES : (2 * h1 + 1) * LANES].astype(BF16)
            pe = d_ref[:, (2 * h0 + 1) * LANES : (2 * h0 + 2) * LANES] + d_ref[:, (2 * h1 + 1) * LANES : (2 * h1 + 2) * LANES]
            o_ref[:, lo + 2 * LANES : lo + 3 * LANES] = _rope_t(pe, ca[...], sa[...], _QA).astype(BF16)

    return pl.pallas_call(
        body,
        name="qrope_bwd",
        grid=(n // tr,),
        in_specs=[_rows(tr, MLA_HEADS * 2 * LANES), _rows(tr, LANES), _rows(tr, LANES)],
        out_specs=_rows(tr, _QA_COLS),
        out_shape=jax.ShapeDtypeStruct((n, _QA_COLS), BF16),
        compiler_params=_params(("parallel",)),
    )(dq2, cos_a, ss_a)


def _cat(refs):
    vals = [r[...] for r in refs]
    return vals[0] if len(vals) == 1 else jnp.concatenate(vals, axis=-1)


LOG2E = 1.4426950408889634


def _attn_fwd(qparts, kparts, vpart, n_heads, group, dv, scale, name, tq, after=None):
    T, Tk = qparts[0][0].shape[0], kparts[0][0].shape[0]
    nq_, nk_ = len(qparts), len(kparts)
    sub = min(tq, 256)
    c2 = scale * LOG2E

    def body(*refs):
        q_refs, k_refs = refs[:nq_], refs[nq_ : nq_ + nk_]
        v_ref = refs[nq_ + nk_]
        o_ref, lse_ref = refs[-2:]
        k = _cat(k_refs)
        v = v_ref[...]
        for r0 in range(0, tq, sub):
            q = _cat([r.at[r0 : r0 + sub, :] for r in q_refs])
            s = lax.dot_general(q, k, _DIMS["NT"], preferred_element_type=F32)
            m = jnp.max(s, axis=-1, keepdims=True)
            p = jnp.exp2((s - m) * c2)
            l = jnp.sum(p, axis=-1, keepdims=True)
            acc = jnp.dot(p.astype(BF16), v, preferred_element_type=F32)
            o_ref[r0 : r0 + sub, :] = (acc * (1.0 / l)).astype(BF16)
            lse_ref[r0 : r0 + sub, :] = m * scale + jnp.log(l)

    in_specs = [pl.BlockSpec((tq, LANES), lambda h, i, f=f: (i, f(h))) for _, f in qparts]
    in_specs += [pl.BlockSpec((Tk, LANES), lambda h, i, f=f: (0, f(h // group))) for _, f in kparts]
    fv = vpart[1]
    in_specs.append(pl.BlockSpec((Tk, dv), lambda h, i: (0, fv(h // group))))
    args = [*[a for a, _ in qparts], *[a for a, _ in kparts], vpart[0]]
    if after is not None:
        in_specs.append(pl.BlockSpec(after.shape, lambda h, i: (0, 0)))
        args.append(after)
    return pl.pallas_call(
        body,
        name=name,
        grid=(n_heads, T // tq),
        in_specs=in_specs,
        out_specs=[pl.BlockSpec((tq, dv), lambda h, i: (i, h)), pl.BlockSpec((None, tq, 1), lambda h, i: (h, i, 0))],
        out_shape=[jax.ShapeDtypeStruct((T, n_heads * dv), BF16), jax.ShapeDtypeStruct((n_heads, T, 1), F32)],
        compiler_params=_params(("parallel", "parallel")),
    )(*args)


def _attn_bwd(qparts, kparts, vpart, o, do, lse, n_heads, group, dv, scale, name, tq):
    T, Tk = qparts[0][0].shape[0], kparts[0][0].shape[0]
    nq_, nk_ = len(qparts), len(kparts)
    dk_ = LANES * nq_
    n_kv = n_heads // group
    nblk = T // tq
    c2 = scale * LOG2E

    def head(hk, i):
        return hk * group + i // nblk

    sub = min(tq, 256)

    def body(*refs):
        q_refs = refs[:nq_]
        k = _cat(refs[nq_ : nq_ + nk_])
        v_ref, o_ref, do_ref, lse_ref, dq_ref, dk_ref, dv_ref = refs[nq_ + nk_ :]
        i = pl.program_id(1)
        _acc_init(i, [dk_ref, dv_ref])
        v = v_ref[...]
        dk_acc, dv_acc = None, None
        for r0 in range(0, tq, sub):
            rows = slice(r0, r0 + sub)
            q = _cat([r.at[rows, :] for r in q_refs])
            s = lax.dot_general(q, k, _DIMS["NT"], preferred_element_type=F32)
            p = jnp.exp2(s * c2 - lse_ref[rows, :] * LOG2E)
            dov = do_ref[rows, :]
            dp = lax.dot_general(dov, v, _DIMS["NT"], preferred_element_type=F32)
            delta = jnp.sum(dov.astype(F32) * o_ref[rows, :].astype(F32), axis=-1, keepdims=True)
            ds = (p * (dp - delta)).astype(BF16)
            dq_ref[rows, :] = jnp.dot(ds, k, preferred_element_type=F32) * scale
            dk_part = lax.dot_general(ds, q, _DIMS["TN"], preferred_element_type=F32)
            dv_part = lax.dot_general(p.astype(BF16), dov, _DIMS["TN"], preferred_element_type=F32)
            dk_acc = dk_part if dk_acc is None else dk_acc + dk_part
            dv_acc = dv_part if dv_acc is None else dv_acc + dv_part
        dk_ref[...] += dk_acc
        dv_ref[...] += dv_acc

        @pl.when(i == group * nblk - 1)
        def _():
            dk_ref[...] *= scale

    in_specs = [pl.BlockSpec((tq, LANES), lambda hk, i, f=f: (i % nblk, f(head(hk, i)))) for _, f in qparts]
    in_specs += [pl.BlockSpec((Tk, LANES), lambda hk, i, f=f: (0, f(hk))) for _, f in kparts]
    fv = vpart[1]
    in_specs.append(pl.BlockSpec((Tk, dv), lambda hk, i: (0, fv(hk))))
    in_specs += [pl.BlockSpec((tq, dv), lambda hk, i: (i % nblk, head(hk, i)))] * 2
    in_specs.append(pl.BlockSpec((None, tq, 1), lambda hk, i: (head(hk, i), i % nblk, 0)))
    return pl.pallas_call(
        body,
        name=name,
        grid=(n_kv, group * nblk),
        in_specs=in_specs,
        out_specs=[pl.BlockSpec((tq, dk_), lambda hk, i: (i % nblk, head(hk, i))),
                   pl.BlockSpec((Tk, dk_), lambda hk, i: (0, hk)),
                   pl.BlockSpec((Tk, dv), lambda hk, i: (0, hk))],
        out_shape=[jax.ShapeDtypeStruct((T, n_heads * dk_), F32), jax.ShapeDtypeStruct((Tk, n_kv * dk_), F32),
                   jax.ShapeDtypeStruct((Tk, n_kv * dv), F32)],
        compiler_params=_params(("parallel", "arbitrary")),
    )(*[a for a, _ in qparts], *[a for a, _ in kparts], vpart[0], o, do, lse)


def _gates_fwd(pg, ya, yb, tr):
    n, d = ya.shape

    def body(pg_ref, ya_ref, yb_ref, o_ref):
        ga = jax.nn.sigmoid(pg_ref[:, :d].astype(F32))
        gb = jax.nn.sigmoid(pg_ref[:, d:].astype(F32))
        o_ref[...] = (ga * ya_ref[...].astype(F32) + gb * yb_ref[...].astype(F32)).astype(BF16)

    return pl.pallas_call(
        body,
        name="gates_fwd",
        grid=(n // tr,),
        in_specs=[_rows(tr, 2 * d), _rows(tr, d), _rows(tr, d)],
        out_specs=_rows(tr, d),
        out_shape=jax.ShapeDtypeStruct((n, d), BF16),
        compiler_params=_params(("parallel",)),
    )(pg, ya, yb)


def _gates_bwd(dm, pg, ya, yb, tr):
    n, d = ya.shape

    def body(dm_ref, pg_ref, ya_ref, yb_ref, dya_ref, dyb_ref, dpg_ref):
        dmv = dm_ref[...].astype(F32)
        ga = jax.nn.sigmoid(pg_ref[:, :d].astype(F32))
        gb = jax.nn.sigmoid(pg_ref[:, d:].astype(F32))
        dya_ref[...] = (dmv * ga).astype(BF16)
        dyb_ref[...] = (dmv * gb).astype(BF16)
        dpg_ref[:, :d] = (dmv * ya_ref[...].astype(F32) * ga * (1.0 - ga)).astype(BF16)
        dpg_ref[:, d:] = (dmv * yb_ref[...].astype(F32) * gb * (1.0 - gb)).astype(BF16)

    return pl.pallas_call(
        body,
        name="gates_bwd",
        grid=(n // tr,),
        in_specs=[_rows(tr, d), _rows(tr, 2 * d), _rows(tr, d), _rows(tr, d)],
        out_specs=[_rows(tr, d), _rows(tr, d), _rows(tr, 2 * d)],
        out_shape=[jax.ShapeDtypeStruct((n, d), BF16), jax.ShapeDtypeStruct((n, d), BF16), jax.ShapeDtypeStruct((n, 2 * d), BF16)],
        compiler_params=_params(("parallel",)),
    )(dm, pg, ya, yb)


def _resid_norm2_fwd(x2d, att, g1, n2g, sh2, sc2, tr):
    n, d = x2d.shape

    def body(x_ref, a_ref, g1_ref, g_ref, sh_ref, sc_ref, x1_ref, z_ref):
        x1 = x_ref[...] + g1_ref[...] * a_ref[...]
        x1_ref[...] = x1
        xh, _ = _rms(x1)
        z_ref[...] = ((xh * g_ref[...]) * (1.0 + sc_ref[...]) + sh_ref[...]).astype(BF16)

    return pl.pallas_call(
        body,
        name="resid_norm2_fwd",
        grid=(n // tr,),
        in_specs=[_rows(tr, d), _rows(tr, d)] + [_bcast(d)] * 4,
        out_specs=[_rows(tr, d), _rows(tr, d)],
        out_shape=[jax.ShapeDtypeStruct((n, d), F32), jax.ShapeDtypeStruct((n, d), BF16)],
        compiler_params=_params(("parallel",)),
    )(x2d, att, g1, n2g, sh2, sc2)


def _resid_norm2_bwd(dz2, x1, dx2, att, n2g, sc2, g1, tr):
    n, d = x1.shape

    def body(dz_ref, x1_ref, dx2_ref, a_ref, g_ref, sc_ref, g1_ref, dx1_ref, da_ref, dg_ref, dsh_ref, dsc_ref, dg1_ref):
        _acc_init(pl.program_id(0), [dg_ref, dsh_ref, dsc_ref, dg1_ref])
        xh, r = _rms(x1_ref[...])
        dzv = dz_ref[...]
        gv = g_ref[...]
        dsc_ref[...] += _csum(dzv * (xh * gv))
        dsh_ref[...] += _csum(dzv)
        dh = dzv * (1.0 + sc_ref[...])
        dg_ref[...] += _csum(dh * xh)
        dx1 = _rms_bwd(xh, r, dh * gv) + dx2_ref[...]
        dx1_ref[...] = dx1
        dg1_ref[...] += _csum(dx1 * a_ref[...])
        da_ref[...] = (dx1 * g1_ref[...]).astype(BF16)

    return pl.pallas_call(
        body,
        name="resid_norm2_bwd",
        grid=(n // tr,),
        in_specs=[_rows(tr, d)] * 4 + [_bcast(d)] * 3,
        out_specs=[_rows(tr, d), _rows(tr, d)] + [_bcast(d)] * 4,
        out_shape=[jax.ShapeDtypeStruct((n, d), F32), jax.ShapeDtypeStruct((n, d), BF16)] + [jax.ShapeDtypeStruct((1, d), F32)] * 4,
        compiler_params=_params(("arbitrary",)),
    )(dz2, x1, dx2, att, n2g, sc2, g1)


def _edges(shape):
    row = lax.broadcasted_iota(jnp.int32, shape, 0)
    return row == 0, row == shape[0] - 1


def _shifts(u, edges):
    n = u.shape[0]
    return jnp.where(edges[0], 0.0, pltpu.roll(u, 1, 0)), jnp.where(edges[1], 0.0, pltpu.roll(u, n - 1, 0))


def _conv3(u, prev, nxt, w_ref, b_ref):
    return b_ref[...] + w_ref[0:1, :] * prev + w_ref[1:2, :] * u + w_ref[2:3, :] * nxt


def _ffn_up_conv(z, wup_t, cw, cb, tc, after):
    n, d = z.shape
    f = wup_t.shape[0] // 2
    nb = f // tc

    def body(z_ref, wa_ref, wb_ref, cwa, cwb, cba, cbb, after_ref, ua_ref, ub_ref, h_ref):
        w = jnp.concatenate([wa_ref[...], wb_ref[...]], axis=0)
        u = lax.dot_general(z_ref[...], w, _DIMS["NT"], preferred_element_type=F32).astype(BF16)
        ua_ref[...] = u[:, :tc]
        ub_ref[...] = u[:, tc:]
        edges = _edges((n, tc))
        ua = u[:, :tc].astype(F32)
        ub = u[:, tc:].astype(F32)
        a = _conv3(ua, *_shifts(ua, edges), cwa, cba)
        b = _conv3(ub, *_shifts(ub, edges), cwb, cbb)
        h_ref[...] = (a * jax.nn.sigmoid(a) * b).astype(BF16)

    col = lambda rows, off: pl.BlockSpec((rows, tc), lambda i: (0, i + off))
    w_rows = lambda off: pl.BlockSpec((tc, d), lambda i: (i + off, 0))
    return pl.pallas_call(
        body,
        name="ffn_up_conv",
        grid=(nb,),
        in_specs=[pl.BlockSpec((n, d), lambda i: (0, 0)), w_rows(0), w_rows(nb), col(3, 0), col(3, nb), col(1, 0), col(1, nb),
                  pl.BlockSpec(after.shape, lambda i: (0, 0))],
        out_specs=[col(n, 0)] * 3,
        out_shape=[jax.ShapeDtypeStruct((n, f), BF16)] * 3,
        compiler_params=_params(("parallel",)),
    )(z, wup_t, wup_t, cw, cw, cb, cb, after)


def _ffn_down_dx_conv_bwd(df, wdown, u_a, u_b, cw, cb, tc, after):
    n, f = u_a.shape
    d = df.shape[1]
    nb = f // tc

    def part(uv, prev, nxt, duc, edges, w_ref, du_ref, dw_ref, db_ref):
        db_ref[...] = _csum(duc)
        dw_ref[0:1, :] = _csum(duc * prev)
        dw_ref[1:2, :] = _csum(duc * uv)
        dw_ref[2:3, :] = _csum(duc * nxt)
        d_prev, d_next = _shifts(duc, edges)
        du_ref[...] = (w_ref[0:1, :] * d_next + w_ref[1:2, :] * duc + w_ref[2:3, :] * d_prev).astype(BF16)

    def body(df_ref, wd_ref, ua_ref, ub_ref, wa_ref, wb_ref, ba_ref, bb_ref, after_ref,
             dua_ref, dub_ref, dwa_ref, dwb_ref, dba_ref, dbb_ref):
        dhv = lax.dot_general(df_ref[...], wd_ref[...], _DIMS["NT"], preferred_element_type=F32)
        dhv = dhv.astype(BF16).astype(F32)
        edges = _edges((n, tc))
        ua = ua_ref[...].astype(F32)
        ub = ub_ref[...].astype(F32)
        sa = _shifts(ua, edges)
        sb = _shifts(ub, edges)
        a = _conv3(ua, *sa, wa_ref, ba_ref)
        b = _conv3(ub, *sb, wb_ref, bb_ref)
        sg = jax.nn.sigmoid(a)
        da = dhv * b * (sg * (1.0 + a * (1.0 - sg)))
        db = dhv * (a * sg)
        part(ua, *sa, da, edges, wa_ref, dua_ref, dwa_ref, dba_ref)
        part(ub, *sb, db, edges, wb_ref, dub_ref, dwb_ref, dbb_ref)

    col = lambda rows, off: pl.BlockSpec((rows, tc), lambda i: (0, i + off))
    return pl.pallas_call(
        body,
        name="ffn_down_dx_conv_bwd",
        grid=(nb,),
        in_specs=[pl.BlockSpec((n, d), lambda i: (0, 0)), pl.BlockSpec((tc, d), lambda i: (i, 0)), col(n, 0), col(n, 0),
                  col(3, 0), col(3, nb), col(1, 0), col(1, nb), pl.BlockSpec(after.shape, lambda i: (0, 0))],
        out_specs=[col(n, 0), col(n, 0), col(3, 0), col(3, 0), col(1, 0), col(1, 0)],
        out_shape=[jax.ShapeDtypeStruct((n, f), BF16)] * 2 + [jax.ShapeDtypeStruct((3, f), F32)] * 2 + [jax.ShapeDtypeStruct((1, f), F32)] * 2,
        compiler_params=_params(("parallel",)),
    )(df, wdown, u_a, u_b, cw, cw, cb, cb, after)


def _loss_head(x1, f, g2, fg, tgt, tr):
    n, d = x1.shape

    def body(x1_ref, f_ref, g2_ref, fg_ref, t_ref, sq_ref, dx2_ref, dfg_ref, dg2_ref, df_ref):
        _acc_init(pl.program_id(0), [sq_ref, dfg_ref, dg2_ref])
        fv = f_ref[...]
        xh, r = _rms(x1_ref[...] + g2_ref[...] * fv)
        err = xh * fg_ref[...] - t_ref[...]
        sq_ref[...] += _csum(err * err)
        dy = err * (1.0 / d)
        dfg_ref[...] += _csum(dy * xh)
        dx2 = _rms_bwd(xh, r, dy * fg_ref[...])
        dx2_ref[...] = dx2
        dg2_ref[...] += _csum(dx2 * fv)
        df_ref[...] = (dx2 * g2_ref[...]).astype(BF16)

    return pl.pallas_call(
        body,
        name="loss_head",
        grid=(n // tr,),
        in_specs=[_rows(tr, d), _rows(tr, d), _bcast(d), _bcast(d), _rows(tr, d)],
        out_specs=[_bcast(d), _rows(tr, d), _bcast(d), _bcast(d), _rows(tr, d)],
        out_shape=[jax.ShapeDtypeStruct((1, d), F32), jax.ShapeDtypeStruct((n, d), F32), jax.ShapeDtypeStruct((1, d), F32),
                   jax.ShapeDtypeStruct((1, d), F32), jax.ShapeDtypeStruct((n, d), BF16)],
        compiler_params=_params(("arbitrary",)),
    )(x1, f, g2, fg, tgt)


def _sum_slots(g, name):
    s, r, w = g.shape

    def body(g_ref, o_ref):
        acc = g_ref[0]
        for k in range(1, s):
            acc = acc + g_ref[k]
        o_ref[...] = acc

    return pl.pallas_call(body, name=name, out_shape=jax.ShapeDtypeStruct((r, w), F32))(g)


def _silu_grad_mul(ds, cvec):
    def body(d_ref, c_ref, o_ref):
        cv = c_ref[...]
        sg = jax.nn.sigmoid(cv)
        o_ref[...] = d_ref[...] * (sg * (1.0 + cv * (1.0 - sg)))

    return pl.pallas_call(body, name="silu_grad_mul", out_shape=jax.ShapeDtypeStruct(ds.shape, F32))(ds, cvec)


def _adamw_update(wv, gv, mv, vv, d_ref, mo_ref, vo_ref):
    mn = ADAM_B1 * mv + (1.0 - ADAM_B1) * gv
    vn = ADAM_B2 * vv + (1.0 - ADAM_B2) * (gv * gv)
    mo_ref[...] = mn
    vo_ref[...] = vn
    m_hat = mn / (1.0 - ADAM_B1**ADAM_STEP)
    v_hat = vn / (1.0 - ADAM_B2**ADAM_STEP)
    d_ref[...] = -ADAM_LR * (m_hat / (jnp.sqrt(v_hat) + ADAM_EPS) + ADAM_WD * wv)


def _adamw_many(ws, gs, ms, vs, name):
    n = len(ws)

    def body(*refs):
        for k in range(n):
            w_ref, g_ref, m_ref, v_ref = (refs[q * n + k] for q in range(4))
            d_ref, mo_ref, vo_ref = (refs[(4 + q) * n + k] for q in range(3))
            _adamw_update(w_ref[...], g_ref[...], m_ref[...], v_ref[...], d_ref, mo_ref, vo_ref)

    res = pl.pallas_call(body, name=name, out_shape=[jax.ShapeDtypeStruct(w.shape, F32) for w in ws] * 3)(*ws, *gs, *ms, *vs)
    return res[:n], res[n : 2 * n], res[2 * n :]


def _adamw(w, g, m, v, name, g_transposed=False, g_sibling=None):
    r, cdim = w.shape
    halves = g_sibling is not None
    block = 1 << 19
    if g_transposed:
        tc = _pick(cdim // 2 if halves else cdim, 2048)
        tr = _pick(r, max(LANES, block // tc), LANES)
        per_half = (cdim // 2) // tc
    else:
        rows = r // 2 if halves else r
        tc = _pick(cdim, 2048)
        tr = _pick(rows, max(8, block // tc), 8)
        if tr < 64 and rows > 64:
            tr, tc = _pick(rows, 1024, 8), _pick(cdim, 512)
        per_half = (r // 2) // tr
    emit_g = g_transposed or halves

    def body(w_ref, g_ref, *rest):
        m_ref, v_ref = rest[halves : halves + 2]
        outs = rest[halves + 2 :]
        gv = g_ref[...]
        if halves:
            along = pl.program_id(1 if g_transposed else 0)
            gv = jnp.where(along // per_half == lax.axis_index("c"), gv, rest[0][...])
        if g_transposed:
            gv = gv.T
        if emit_g:
            outs[0][...] = gv
        _adamw_update(w_ref[...], gv, m_ref[...], v_ref[...], *outs[-3:])

    spec = pl.BlockSpec((tr, tc), lambda i, j: (i, j))
    if g_transposed:
        g_spec = pl.BlockSpec((tc, tr), lambda i, j: (j % per_half if halves else j, i))
    else:
        g_spec = pl.BlockSpec((tr, tc), lambda i, j: (i % per_half if halves else i, j))
    n_out = 3 + emit_g
    res = pl.pallas_call(
        body,
        name=name,
        grid=(r // tr, cdim // tc),
        in_specs=[spec, g_spec] + [g_spec] * halves + [spec, spec],
        out_specs=[spec] * n_out,
        out_shape=[jax.ShapeDtypeStruct((r, cdim), F32)] * n_out,
        compiler_params=_params(("parallel", "parallel")),
    )(w, g, *([g_sibling] if halves else []), m, v)
    return res if emit_g else [g, *res]


def _place():
    return lax.axis_index("x"), lax.axis_index("y"), lax.axis_index("c")


def _remote(src, dst, send_sem, recv_sem, dev):
    return pltpu.make_async_remote_copy(src_ref=src, dst_ref=dst, send_sem=send_sem, recv_sem=recv_sem, device_id=dev, device_id_type=MESH)


ANY = pl.BlockSpec(memory_space=pl.ANY)


def _all_gather_small(v, name, after=()):
    r, w = v.shape

    def body(v_ref, *rest):
        o_ref, send, recv, lsem = rest[len(after) :]
        x, y, c = _place()
        me = 4 * x + 2 * y + c
        mine = pltpu.make_async_copy(v_ref, o_ref.at[me], lsem)
        mine.start()
        sent = []
        for k in range(1, 8):
            px, py, pc = x ^ (k >> 2), y ^ ((k >> 1) & 1), c ^ (k & 1)
            cp = _remote(v_ref, o_ref.at[me], send.at[k - 1], recv.at[k - 1], (px, py, pc))
            cp.start()
            sent.append(cp)
        for k in range(1, 8):
            px, py, pc = x ^ (k >> 2), y ^ ((k >> 1) & 1), c ^ (k & 1)
            slot = o_ref.at[4 * px + 2 * py + pc]
            _remote(slot, slot, send.at[k - 1], recv.at[k - 1], (x, y, c)).wait_recv()
        for cp in sent:
            cp.wait_send()
        mine.wait()

    return pl.pallas_call(
        body,
        name=name,
        out_shape=jax.ShapeDtypeStruct((8, r, w), F32),
        in_specs=[pl.BlockSpec(memory_space=pltpu.VMEM)] + [ANY] * len(after),
        out_specs=pl.BlockSpec(memory_space=pltpu.VMEM),
        scratch_shapes=[pltpu.SemaphoreType.DMA((7,)), pltpu.SemaphoreType.DMA((7,)), pltpu.SemaphoreType.DMA],
        compiler_params=pltpu.CompilerParams(vmem_limit_bytes=VMEM_LIMIT),
    )(v, *after)


HBM = pl.BlockSpec(memory_space=pltpu.HBM)
SEM = pl.BlockSpec(memory_space=pltpu.SEMAPHORE)
EFFECT = pltpu.SideEffectType.DATAFLOW_SIDE_EFFECTING


def _other_chips(x, y):
    return [(1 - x, y), (x, 1 - y), (1 - x, 1 - y)]


def _bulk_start(name, srcs, land_shapes, n_copies, copies, after, lands_init=None):
    n, m = len(srcs), len(land_shapes)

    def body(*refs):
        src_refs, land_refs = refs[:n], refs[n : n + m]
        send, recv = refs[n + m + 1], refs[n + m + 2]
        token = refs[-1]
        for k, (s, d, dev) in enumerate(copies(src_refs, land_refs)):
            _remote(s, d, send.at[k], recv.at[k], dev).start()
        token[...] = jnp.zeros_like(token)

    lands = lands_init or [lax.empty(s.shape, s.dtype) for s in land_shapes]
    lands = [pltpu.with_memory_space_constraint(b, pltpu.HBM) for b in lands]
    out = pl.pallas_call(
        body,
        name=name,
        out_shape=(pltpu.SemaphoreType.DMA((n_copies,)), pltpu.SemaphoreType.DMA((n_copies,)),
                   *[pltpu.HBM(s.shape, s.dtype) for s in srcs], *[pltpu.HBM(s.shape, s.dtype) for s in land_shapes],
                   jax.ShapeDtypeStruct((8, LANES), F32)),
        in_specs=[HBM] * (n + m) + [ANY],
        out_specs=(SEM, SEM, *[HBM] * (n + m), pl.BlockSpec(memory_space=pltpu.VMEM)),
        input_output_aliases={i: 2 + i for i in range(n + m)},
        compiler_params=pltpu.CompilerParams(has_side_effects=EFFECT),
    )(*[pltpu.with_memory_space_constraint(s, pltpu.HBM) for s in srcs], *lands, after)
    return out[0], out[1], list(out[2 : 2 + n]), list(out[2 + n : 2 + n + m]), out[-1][0:1, 0:1]


def _bulk_wait(name, send, recv, srcs, lands, after, waits):
    n, m = len(srcs), len(lands)

    def body(*refs):
        src_refs, land_refs = refs[:n], refs[n : n + m]
        send_sem, recv_sem = refs[n + m], refs[n + m + 1]
        x, y, c = _place()
        for k, (s, d) in enumerate(waits(src_refs, land_refs)):
            cp = _remote(s, d, send_sem.at[k], recv_sem.at[k], (x, y, c))
            cp.wait_send()
            cp.wait_recv()

    out = pl.pallas_call(
        body,
        name=name,
        out_shape=tuple(pltpu.HBM(s.shape, s.dtype) for s in (*srcs, *lands)),
        in_specs=[HBM] * (n + m) + [SEM, SEM, ANY],
        out_specs=tuple([HBM] * (n + m)),
        input_output_aliases={i: i for i in range(n + m)},
        compiler_params=pltpu.CompilerParams(has_side_effects=EFFECT),
    )(*srcs, *lands, send, recv, after)
    return list(out[:n]), list(out[n:])


def _gather_start(shards, after, name, own_placed=False):
    def copies(src, land):
        x, y, c = _place()
        j = 2 * x + y
        return [(src[a].at[c], land[a].at[j, c], (px, py, c)) for a in range(len(shards)) for px, py in _other_chips(x, y)]

    shapes = [jax.ShapeDtypeStruct((4,) + s.shape, s.dtype) for s in shards]
    init = None
    if own_placed:
        j = 2 * lax.axis_index("x") + lax.axis_index("y")
        init = [lax.dynamic_update_slice(lax.empty(t.shape, t.dtype), s[None], (j, 0, 0, 0)) for t, s in zip(shapes, shards)]
    return _bulk_start(name, shards, shapes, 3 * len(shards), copies, after, init)


def _gather_wait(started, after, name):
    send, recv, srcs, lands, _ = started

    def waits(src, land):
        x, y, c = _place()
        return [(src[a].at[c], land[a].at[2 * px + py, c]) for a in range(len(srcs)) for px, py in _other_chips(x, y)]

    return _bulk_wait(name, send, recv, srcs, lands, after, waits)


def _forward_halves(lands, name):
    n = len(lands)

    def body(*refs):
        bufs = refs[n : 2 * n]
        send, recv = refs[2 * n :]
        x, y, c = _place()
        started = []
        for a in range(n):
            for k, (px, py) in enumerate(_other_chips(x, y)):
                blk = bufs[a].at[2 * px + py, c]
                cp = _remote(blk, blk, send.at[3 * a + k], recv.at[3 * a + k], (x, y, 1 - c))
                cp.start()
                started.append(cp)
        for a in range(n):
            for k, (px, py) in enumerate(_other_chips(x, y)):
                blk = bufs[a].at[2 * px + py, 1 - c]
                _remote(blk, blk, send.at[3 * a + k], recv.at[3 * a + k], (x, y, c)).wait_recv()
        for cp in started:
            cp.wait_send()

    return pl.pallas_call(
        body,
        name=name,
        out_shape=[jax.ShapeDtypeStruct(b.shape, b.dtype) for b in lands],
        in_specs=[ANY] * n,
        out_specs=[ANY] * n,
        input_output_aliases={i: i for i in range(n)},
        scratch_shapes=[pltpu.SemaphoreType.DMA((3 * n,)), pltpu.SemaphoreType.DMA((3 * n,))],
    )(*lands)


def _forward_start(lands, after, name):
    def copies(src, _):
        x, y, c = _place()
        blocks = [src[a].at[2 * px + py, c] for a in range(len(lands)) for px, py in _other_chips(x, y)]
        return [(b, b, (x, y, 1 - c)) for b in blocks]

    return _bulk_start(name, lands, [], 3 * len(lands), copies, after)


def _forward_wait(started, after, name):
    send, recv, bufs, _, _ = started

    def waits(src, _):
        x, y, c = _place()
        return [(src[a].at[2 * px + py, c], src[a].at[2 * px + py, 1 - c]) for a in range(len(bufs)) for px, py in _other_chips(x, y)]

    return _bulk_wait(name, send, recv, bufs, [], after, waits)[0]


def _place_own(shards, lands):
    j = 2 * lax.axis_index("x") + lax.axis_index("y")
    full = [lax.dynamic_update_slice(b, s[None], (j, 0, 0, 0)) for b, s in zip(lands, shards)]
    return [f.reshape(4 * f.shape[2] * 2, f.shape[3]) for f in full]


def _gather_finish(started, after, tag, own_placed=False):
    shards, lands = _gather_wait(started, after, "gather_wait_" + tag)
    lands = _forward_halves(lands, "gather_forward_" + tag)
    if own_placed:
        return [f.reshape(4 * f.shape[2] * 2, f.shape[3]) for f in lands]
    return _place_own(shards, lands)


def _gather_land(started, after, tag):
    shards, lands = _gather_wait(started, after, "gather_wait_" + tag)
    return shards, _forward_start(lands, shards[0], "forward_start_" + tag)


def _gather_done(landed, after, tag):
    shards, fwd = landed
    return _place_own(shards, _forward_wait(fwd, after, "forward_wait_" + tag))


def _swap_halves(grads, name):
    n = len(grads)

    def body(*refs):
        ins, outs = refs[:n], refs[n : 2 * n]
        send, recv = refs[2 * n :]
        x, y, c = _place()
        started = []
        for a in range(n):
            for s in range(4):
                cp = _remote(ins[a].at[s, 1 - c], outs[a].at[s], send.at[4 * a + s], recv.at[4 * a + s], (x, y, 1 - c))
                cp.start()
                started.append(cp)
        for cp in started:
            cp.wait_recv()
        for cp in started:
            cp.wait_send()

    return pl.pallas_call(
        body,
        name=name,
        out_shape=[jax.ShapeDtypeStruct((4,) + g.shape[2:], g.dtype) for g in grads],
        in_specs=[ANY] * n,
        out_specs=[ANY] * n,
        scratch_shapes=[pltpu.SemaphoreType.DMA((4 * n,)), pltpu.SemaphoreType.DMA((4 * n,))],
    )(*grads)


def _add_halves(grads, others, tag):
    outs = []
    for a, (g, o) in enumerate(zip(grads, others)):
        _, _, rh, cdim = g.shape
        tr = _pick(rh, 512, 16)

        def body(g_ref, o_ref, p_ref):
            p_ref[...] = (g_ref[...].astype(F32) + o_ref[...].astype(F32)).astype(BF16)

        outs.append(
            pl.pallas_call(
                body,
                name=f"add_halves_{tag}{a}",
                grid=(4, rh // tr),
                in_specs=[pl.BlockSpec((None, None, tr, cdim), lambda s, i: (s, lax.axis_index("c"), i, 0)),
                          pl.BlockSpec((None, tr, cdim), lambda s, i: (s, i, 0))],
                out_specs=pl.BlockSpec((None, tr, cdim), lambda s, i: (s, i, 0)),
                out_shape=jax.ShapeDtypeStruct((4, rh, cdim), BF16),
                compiler_params=_params(("parallel", "parallel")),
            )(g, o)
        )
    return outs


def _exchange_start(parts, after, name):
    def copies(src, land):
        x, y, c = _place()
        j = 2 * x + y
        return [(src[a].at[2 * px + py], land[a].at[j], (px, py, c)) for a in range(len(parts)) for px, py in _other_chips(x, y)]

    return _bulk_start(name, parts, [jax.ShapeDtypeStruct(p.shape, p.dtype) for p in parts], 3 * len(parts), copies, after)


def _exchange_finish(started, after, name):
    send, recv, srcs, lands, _ = started

    def waits(src, land):
        x, y, _ = _place()
        return [(src[a].at[2 * px + py], land[a].at[2 * px + py]) for a in range(len(srcs)) for px, py in _other_chips(x, y)]

    srcs, lands = _bulk_wait(name, send, recv, srcs, lands, after, waits)
    j = 2 * lax.axis_index("x") + lax.axis_index("y")
    return [lax.dynamic_update_slice(b, lax.dynamic_slice(p, (j, 0, 0), (1,) + p.shape[1:]), (j, 0, 0)) for b, p in zip(lands, srcs)]


def _sum_chips(recvd, tag):
    outs = []
    for a, g in enumerate(recvd):
        _, rh, cdim = g.shape
        tr = _pick(rh, 512, 16)

        def body(g_ref, o_ref):
            o_ref[...] = ((g_ref[0].astype(F32) + g_ref[1].astype(F32)) + g_ref[2].astype(F32)) + g_ref[3].astype(F32)

        outs.append(
            pl.pallas_call(
                body,
                name=f"sum_chips_{tag}{a}",
                grid=(rh // tr,),
                in_specs=[pl.BlockSpec((4, tr, cdim), lambda i: (0, i, 0))],
                out_specs=pl.BlockSpec((tr, cdim), lambda i: (i, 0)),
                out_shape=jax.ShapeDtypeStruct((rh, cdim), F32),
                compiler_params=_params(("parallel",)),
            )(g)
        )
    return outs


def _join_halves(halves, name):
    n = len(halves)

    def body(*refs):
        ins, outs = refs[:n], refs[n : 2 * n]
        send, recv = refs[2 * n :]
        x, y, c = _place()
        started = []
        for a in range(n):
            cp = _remote(ins[a], outs[a], send.at[a], recv.at[a], (x, y, 1 - c))
            cp.start()
            started.append(cp)
        for cp in started:
            cp.wait_recv()
        for cp in started:
            cp.wait_send()

    others = pl.pallas_call(
        body,
        name=name,
        out_shape=[jax.ShapeDtypeStruct(h.shape, h.dtype) for h in halves],
        in_specs=[ANY] * n,
        out_specs=[ANY] * n,
        scratch_shapes=[pltpu.SemaphoreType.DMA((n,)), pltpu.SemaphoreType.DMA((n,))],
    )(*halves)
    return list(zip(halves, others))


def _joined(mine, other):
    first = lax.axis_index("c") == 0
    return jnp.concatenate([jnp.where(first, mine, other), jnp.where(first, other, mine)], axis=0)


def _grad_views(grads):
    return [g.reshape(4, 2, g.shape[0] // 8, g.shape[1]) for g in grads]


def _scatter_start(grads, tag, after=None):
    views = _grad_views(grads)
    others = _swap_halves(views, "swap_halves_" + tag)
    mine = _add_halves(views, others, tag)
    return _exchange_start(mine, others[-1] if after is None else after, "exchange_start_" + tag)


def _swap_start(grads, after, tag):
    views = _grad_views(grads)

    def copies(src, land):
        x, y, c = _place()
        return [(src[a].at[s, 1 - c], land[a].at[s], (x, y, 1 - c)) for a in range(len(views)) for s in range(4)]

    shapes = [jax.ShapeDtypeStruct((4,) + v.shape[2:], v.dtype) for v in views]
    return _bulk_start("swap_start_" + tag, views, shapes, 4 * len(views), copies, after)


def _scatter_start_after_swap(swapped, after, tag):
    send, recv, views, lands, _ = swapped

    def waits(src, land):
        c = lax.axis_index("c")
        return [(src[a].at[s, 1 - c], land[a].at[s]) for a in range(len(views)) for s in range(4)]

    views, others = _bulk_wait("swap_wait_" + tag, send, recv, views, lands, after, waits)
    mine = _add_halves(views, others, tag)
    return _exchange_start(mine, others[-1], "exchange_start_" + tag)


def _join_start(halves, after, tag):
    def copies(src, land):
        x, y, c = _place()
        return [(src[a], land[a], (x, y, 1 - c)) for a in range(len(halves))]

    return _bulk_start("join_start_" + tag, halves, [jax.ShapeDtypeStruct(h.shape, h.dtype) for h in halves], len(halves), copies, after)


def _join_wait(started, after, tag):
    send, recv, halves, lands, _ = started
    halves, others = _bulk_wait("join_wait_" + tag, send, recv, halves, lands, after, lambda src, land: list(zip(src, land)))
    return list(zip(halves, others))


def _scatter_sums(started, after, tag):
    return _sum_chips(_exchange_finish(started, after, "exchange_wait_" + tag), tag)


def _scatter_finish(started, after, tag):
    return _join_halves(_scatter_sums(started, after, tag), "join_halves_" + tag)


def _t_bf16(w):
    return w.T.astype(BF16)


def kernel(x, c, ctx, c_ctx, w_ada, b_ada, norm1_g, w_in, mla_q_norm_g, w_q_up, mla_kv_norm_g, w_kv_up, gqa_q_norm_g, gqa_k_norm_g, w_br_a, w_br_b, w_out, norm2_g, w_up, conv_w, conv_b, w_down, final_norm_g, loss_target, m_c_ctx, m_w_ada, m_b_ada, m_norm1_g, m_w_in, m_mla_q_norm_g, m_w_q_up, m_mla_kv_norm_g, m_w_kv_up, m_gqa_q_norm_g, m_gqa_k_norm_g, m_w_br_a, m_w_br_b, m_w_out, m_norm2_g, m_w_up, m_conv_w, m_conv_b, m_w_down, m_final_norm_g, v_c_ctx, v_w_ada, v_b_ada, v_norm1_g, v_w_in, v_mla_q_norm_g, v_w_q_up, v_mla_kv_norm_g, v_w_kv_up, v_gqa_q_norm_g, v_gqa_k_norm_g, v_w_br_a, v_w_br_b, v_w_out, v_norm2_g, v_w_up, v_conv_w, v_conv_b, v_w_down, v_final_norm_g):
    T, D = x.shape[1], x.shape[2]
    C = ctx.shape[1]
    NA = w_ada.shape[2]
    NW = w_up.shape[2]
    F2 = 4 * NW
    FF = F2 // 2
    xi, yi, ci = _place()
    j = 2 * xi + yi
    me = 4 * xi + 2 * yi + ci
    tr = _pick(C, 256, 8)

    x2d, tgt, ctx2d = x[0], loss_target[0], ctx[0]
    fg = final_norm_g.reshape(1, D)
    cc = c_ctx.reshape(1, D)

    halve = lambda s: s.reshape(2, s.shape[0] // 2, s.shape[1])
    win_shard = halve(_t_bf16(w_in[0]))
    w0 = max(D, NW)
    pay = jnp.zeros((8, w0), F32).at[0:1, :D].set(c).at[1:4, :NW].set(conv_w[0])
    got = _all_gather_small(pay, "gather_cond")
    ag_in = _gather_start([win_shard], got, "gather_start_in", own_placed=True)
    t_in = ag_in[4]
    c_all = got[:, 0, :D]
    cw = jnp.concatenate([got[2 * s, 1:4, :NW] for s in range(4)], axis=1)
    s16 = jnp.concatenate([c_all, cc, jnp.zeros((7, D), F32)], axis=0) + t_in
    b_cols = lax.dynamic_slice(b_ada, (0, j * NA), (1, NA))
    ada_part = _mm(s16, w_ada[0], "NN", F32, "ada_fwd", act="silu", bias=b_cols)

    wq3 = (w_q_up[0] + t_in).reshape(MLA_Q_LORA, 2, MLA_NOPE + MLA_ROPE)
    wq_perm = jnp.concatenate([wq3[:, :, :MLA_NOPE].reshape(MLA_Q_LORA, -1), wq3[:, :, MLA_NOPE:].reshape(MLA_Q_LORA, -1)], axis=1)
    low = [halve(_t_bf16(wq_perm)), halve(_t_bf16(w_kv_up[0] + t_in))]
    br = [halve(_t_bf16(w_br_a[0] + t_in)), halve(_t_bf16(w_br_b[0] + t_in)), halve((w_out[0] + t_in).astype(BF16))]
    up = [halve(_t_bf16(w_up[0] + t_in))]
    down = [halve((w_down[0] + t_in).astype(BF16))]

    got = _all_gather_small(ada_part, "gather_ada", after=(*low, *br, *up, *down))
    ada = jnp.concatenate([got[2 * s] for s in range(4)], axis=1)
    lat = lax.dynamic_slice(ada, (me, 0), (1, 6 * D))
    sh1, sc1, g1, sh2, sc2, g2 = [lat[:, k * D : (k + 1) * D] for k in range(6)]
    csh, csc = ada[8:9, :D], ada[8:9, D : 2 * D]
    ag_low = _gather_start(low, got, "gather_start_low")
    ag_br = _gather_start(br, ag_low[4], "gather_start_br")
    ag_up = _gather_start(up, ag_br[4], "gather_start_up")
    ag_down = _gather_start(down, ag_up[4], "gather_start_down")
    sh1 = sh1 + ag_down[4]

    cos_a, ss_a = _rope_tables(C, T, MLA_ROPE)
    cos_b, ss_b = _rope_tables(C, T, GQA_HEAD_DIM)
    lcos_a, lss_a, lcos_b, lss_b = cos_a[:T], ss_a[:T], cos_b[:T], ss_b[:T]

    z_all = _norm_mod_fwd(x2d, norm1_g, sh1, sc1, "norm1_lat_fwd", tr, out_rows=T + C)
    z_all = _norm_mod_fwd(ctx2d, norm1_g, csh, csc, "norm1_ctx_fwd", tr, base=z_all, out_off=T)
    (win_t,) = _gather_finish(ag_in, z_all, "in", own_placed=True)
    kv_cols = KVP - LANES + MLA_ROPE
    e_kpe = MLA_KV_LORA + MLA_ROPE
    w_kvp = jnp.concatenate([win_t[:MLA_KV_LORA], win_t[e_kpe:kv_cols], win_t[MLA_KV_LORA:e_kpe], jnp.zeros((LANES - MLA_ROPE, D), BF16)], axis=0)

    pkv = _mm(z_all, w_kvp, "NT", F32, "proj_kv", tn=KVP)
    pq = _mm(z_all, win_t, "NT", F32, "proj_q", m=T, n=QC, b_off=kv_cols)
    low_landed = _gather_land(ag_low, pq, "low")
    pg = _mm(z_all, win_t, "NT", BF16, "proj_g", m=T, n=2 * D, b_off=kv_cols + QC, after=low_landed[1][4])
    wq_t, wkv_t = _gather_done(low_landed, pg, "low")
    ckv_n, kb2, vb2, kpe2 = _kprep_fwd(pkv, mla_kv_norm_g, gqa_k_norm_g, cos_a, ss_a, cos_b, ss_b, tr)
    kv_up = _mm(ckv_n, wkv_t, "NT", BF16, "kv_up")
    cq_n, qb2 = _qprep_fwd(pq, mla_q_norm_g, gqa_q_norm_g, lcos_b, lss_b, tr)
    q_a = _mm(cq_n, wq_t, "NT", F32, "q_up")
    qar = _qrope_fwd(q_a, lcos_a, lss_a, tr)

    a_q = [(qar, lambda h: 3 * (h // 2) + h % 2), (qar, lambda h: 3 * (h // 2) + 2)]
    a_k = [(kv_up, lambda h: 2 * h), (kpe2, lambda h: h % 2)]
    a_v = (kv_up, lambda h: 2 * h + 1)
    a_scale = float(MLA_NOPE + MLA_ROPE) ** -0.5
    b_q = [(qb2, lambda h: h)]
    b_k = [(kb2, lambda h: h)]
    b_v = (vb2, lambda h: h)
    b_scale = float(GQA_HEAD_DIM) ** -0.5
    tq_f = _pick(T, 2048)
    o_a, lse_a = _attn_fwd(a_q, a_k, a_v, MLA_HEADS, 1, MLA_V, a_scale, "attn_a_fwd", tq_f)
    br_landed = _gather_land(ag_br, o_a, "br")
    o_b, lse_b = _attn_fwd(b_q, b_k, b_v, GQA_HEADS, GQA_GROUP, GQA_HEAD_DIM, b_scale, "attn_b_fwd", tq_f, after=br_landed[1][4])
    wbra_t, wbrb_t, wout = _gather_done(br_landed, o_b, "br")
    up_landed = _gather_land(ag_up, o_b, "up")
    ya = _mm(o_a, wbra_t, "NT", BF16, "br_a", after=up_landed[1][4])
    yb = _mm(o_b, wbrb_t, "NT", BF16, "br_b")
    merged = _gates_fwd(pg, ya, yb, tr)
    att = _mm(merged, wout, "NN", F32, "out_proj")
    x1, z2 = _resid_norm2_fwd(x2d, att, g1, norm2_g, sh2, sc2, tr)
    (wup_t,) = _gather_done(up_landed, z2, "up")
    down_landed = _gather_land(ag_down, z2, "down")
    tc = _pick(FF, 128)
    u_a, u_b, hg = _ffn_up_conv(z2, wup_t, cw, conv_b, tc, down_landed[1][4])
    (wdown,) = _gather_done(down_landed, hg, "down")
    f = _mm(hg, wdown, "NN", F32, "ffn_down", tk=FF // 2)
    sq, dx2, d_fg, d_g2, df = _loss_head(x1, f, g2, fg, tgt, tr)
    loss = lax.psum(0.5 * jnp.sum(sq) / D, ("x", "y", "c"))

    du_a, du_b, dcw_a, dcw_b, dcb_a, dcb_b = _ffn_down_dx_conv_bwd(df, wdown, u_a, u_b, cw, conv_b, _pick(FF, 256), loss.reshape(1, 1))
    g_wdown = _mm(hg, df, "TN", BF16, "ffn_down_dw", tm=FF // 4)
    dz2 = _mm(du_a, wup_t, "NN", F32, "ffn_up_dx_a", tk=FF // 2)
    dz2 = _mm(du_b, wup_t, "NN", F32, "ffn_up_dx_b", b_off=FF, add=dz2, tk=FF // 2)
    g_wup_t = _mm(du_a, z2, "TN", BF16, "ffn_up_dw_a", out_rows=F2, tm=FF // 4)
    g_wup_t = _mm(du_b, z2, "TN", BF16, "ffn_up_dw_b", out_base=g_wup_t, out_off=FF, tm=FF // 4)
    sw_ffn = _swap_start([g_wdown, g_wup_t], sc2, "ffn")
    sc2 = sc2 + sw_ffn[4]
    dx1, datt, d_n2g, d_sh2, d_sc2, d_g1 = _resid_norm2_bwd(dz2, x1, dx2, att, norm2_g, sc2, g1, tr)

    dmerged = _mm(datt, wout, "NT", BF16, "out_proj_dx")
    rs_ffn = _scatter_start_after_swap(sw_ffn, dmerged, "ffn")
    lse_a = lse_a + rs_ffn[4]
    g_wout = _mm(merged, datt, "TN", BF16, "out_proj_dw")
    dya, dyb, dpg = _gates_bwd(dmerged, pg, ya, yb, tr)
    do_a = _mm(dya, wbra_t, "NN", BF16, "br_a_dx")
    g_wbra_t = _mm(dya, o_a, "TN", BF16, "br_a_dw")
    do_b = _mm(dyb, wbrb_t, "NN", BF16, "br_b_dx")
    g_wbrb_t = _mm(dyb, o_b, "TN", BF16, "br_b_dw")
    dqa2, dka2, dva2 = _attn_bwd(a_q, a_k, a_v, o_a, do_a, lse_a, MLA_HEADS, 1, MLA_V, a_scale, "attn_a_bwd", tq_f)
    dqb2, dkb2, dvb2 = _attn_bwd(b_q, b_k, b_v, o_b, do_b, lse_b, GQA_HEADS, GQA_GROUP, GQA_HEAD_DIM, b_scale, "attn_b_bwd", tq_f)
    dq_a = _qrope_bwd(dqa2, lcos_a, lss_a, tr)
    dcq_n = _mm(dq_a, wq_t, "NN", F32, "q_up_dx")
    g_wq_t = _mm(dq_a, cq_n, "TN", BF16, "q_up_dw")
    dpq, d_qg, d_gq = _qprep_bwd(pq, dcq_n, dqb2, mla_q_norm_g, gqa_q_norm_g, lcos_b, lss_b, tr)
    dkv_up, dkpe = _kgrad_split(dka2, dva2, cos_a, ss_a, tr)
    dckv_n = _mm(dkv_up, wkv_t, "NN", F32, "kv_up_dx")
    g_wkv_t = _mm(dkv_up, ckv_n, "TN", BF16, "kv_up_dw")
    rs_mix = _scatter_start([g_wq_t, g_wkv_t, g_wbra_t, g_wbrb_t, g_wout], "mix")
    dpkv, d_kvg, d_kg = _kprep_bwd(pkv, dckv_n, dkb2, dvb2, dkpe, mla_kv_norm_g + rs_mix[4], gqa_k_norm_g, cos_b, ss_b, tr)
    dz_kv = _mm(dpkv, w_kvp, "NN", F32, "proj_kv_dx")
    dz_lat = _mm(dpq, win_t, "NN", F32, "proj_q_dx", b_off=kv_cols, add=dz_kv)
    dz_lat = _mm(dpg, win_t, "NN", F32, "proj_g_dx", b_off=kv_cols + QC, add=dz_lat)
    _, d_n1g_c, d_csh, d_csc = _norm_mod_bwd(dz_kv, T // tr, ctx2d, norm1_g, csc, None, "norm1_ctx_bwd", tr)
    grad_x, d_n1g_l, d_sh1, d_sc1 = _norm_mod_bwd(dz_lat, 0, x2d, norm1_g, sc1, dx1, "norm1_lat_bwd", tr)

    zeros_d = jnp.zeros((1, D), F32)
    d_lat = jnp.concatenate([d_sh1, d_sc1, d_g1, d_sh2, d_sc2, d_g2], axis=1)
    d_ctx_part = jnp.concatenate([d_csh, d_csc], axis=1)
    flat = jnp.concatenate(
        [d_n1g_c + d_n1g_l, d_qg, d_kvg, d_gq, d_kg, d_n2g, dcb_a, dcb_b, d_fg,
         dcw_a.reshape(1, -1), dcw_b.reshape(1, -1), d_ctx_part, d_lat], axis=1)
    n_flat = flat.shape[1]
    n_rows = -(-n_flat // (8 * LANES)) * 8
    flat = jnp.pad(flat, ((0, 0), (0, n_rows * LANES - n_flat))).reshape(n_rows, LANES)
    got = _all_gather_small(flat, "gather_small_grads")
    tot = _sum_slots(got, "sum_small_grads").reshape(1, -1)
    sizes = [D, MLA_Q_LORA, MLA_KV_LORA, GQA_HEAD_DIM, GQA_HEAD_DIM, D, F2, D, 3 * FF, 3 * FF, 2 * D]
    offs = [0]
    for s in sizes:
        offs.append(offs[-1] + s)
    t_n1g, t_qg, t_kvg, t_gq, t_kg, t_n2g, t_cb, t_fg, t_cwa, t_cwb, t_ctx = [tot[:, offs[k] : offs[k + 1]] for k in range(len(sizes))]
    g_cw_full = jnp.concatenate([t_cwa.reshape(3, FF), t_cwb.reshape(3, FF)], axis=1)
    g_cw = lax.dynamic_slice(g_cw_full, (0, j * NW), (3, NW))
    d_lat_all = got.reshape(8, -1)[:, offs[-1] : offs[-1] + 6 * D]
    g16 = jnp.concatenate([d_lat_all, jnp.pad(t_ctx, ((0, 0), (0, 4 * D))), jnp.zeros((7, 6 * D), F32)], axis=0)
    g_b_ada = _sum_slots(g16.reshape(16, 1, 6 * D), "sum_b_ada")
    g16_cols = lax.dynamic_slice(g16, (0, j * NA), (16, NA))
    ds_part = _mm(g16_cols, w_ada[0], "NT", F32, "ada_dx")
    got = _all_gather_small(ds_part[8:16], "gather_ada_dx")
    ds_ctx = _sum_slots(jnp.stack([got[2 * s] for s in range(4)]), "sum_ada_dx")[0:1]
    g_c_ctx = _silu_grad_mul(ds_ctx, cc)

    g_kvp = _mm(dpkv, z_all, "TN", BF16, "proj_kv_dw")
    nk = MLA_KV_LORA + 2 * GQA_KV_HEADS * GQA_HEAD_DIM
    g_kv = jnp.concatenate([g_kvp[:MLA_KV_LORA], g_kvp[nk : nk + MLA_ROPE], g_kvp[MLA_KV_LORA:nk]], axis=0)
    g_win_t = _mm(dpq, z_all, "TN", BF16, "proj_q_dw", out_rows=kv_cols + QC + 2 * D, out_off=kv_cols, tm=QC // 2)
    g_win_t = _mm(dpg, z_all, "TN", BF16, "proj_g_dw", out_base=g_win_t, out_off=kv_cols + QC)
    g_win_t = lax.dynamic_update_slice(g_win_t, g_kv, (0, 0))
    sw_in = _swap_start([g_win_t], got, "in")

    h_ffn = _scatter_sums(rs_ffn, sw_in[2][0], "ffn")
    j_ffn = _join_start(h_ffn, grad_x, "ffn")
    h_mix = _scatter_sums(rs_mix, j_ffn[2][0], "mix")
    j_mix = _join_start(h_mix, j_ffn[2][0], "mix")
    rs_in = _scatter_start_after_swap(sw_in, j_mix[2][0], "in")
    g_w_ada = _mm(s16, g16_cols, "TN", F32, "ada_dw", act="silu", after=rs_in[4])
    _, d_ada, m_ada, v_ada = _adamw(w_ada[0], g_w_ada, m_w_ada[0], v_w_ada[0], "adamw_w_ada")
    r_wdown, r_wup = _join_wait(j_ffn, d_ada, "ffn")
    r_wq, r_wkv, r_wbra, r_wbrb, r_wout = _join_wait(j_mix, d_ada, "mix")
    gq_p = _joined(*r_wq).T
    gq = jnp.concatenate([gq_p[:, : 2 * MLA_NOPE].reshape(MLA_Q_LORA, 2, MLA_NOPE), gq_p[:, 2 * MLA_NOPE :].reshape(MLA_Q_LORA, 2, MLA_ROPE)], axis=2)
    grads = {
        "c_ctx": g_c_ctx.reshape(D), "w_ada": g_w_ada[None], "b_ada": g_b_ada, "norm1_g": t_n1g,
        "mla_q_norm_g": t_qg, "w_q_up": gq.reshape(1, MLA_Q_LORA, -1), "mla_kv_norm_g": t_kvg, "w_kv_up": r_wkv,
        "gqa_q_norm_g": t_gq, "gqa_k_norm_g": t_kg, "w_br_a": r_wbra, "w_br_b": r_wbrb, "w_out": r_wout,
        "norm2_g": t_n2g, "w_up": r_wup, "conv_w": g_cw[None], "conv_b": t_cb, "w_down": r_wdown,
        "final_norm_g": t_fg.reshape(D),
    }
    arrives_transposed = ("w_kv_up", "w_br_a", "w_br_b", "w_up")
    arrives_halved = arrives_transposed + ("w_out", "w_down")
    weights = dict(c_ctx=c_ctx, w_ada=w_ada, b_ada=b_ada, norm1_g=norm1_g, w_in=w_in, mla_q_norm_g=mla_q_norm_g, w_q_up=w_q_up,
                   mla_kv_norm_g=mla_kv_norm_g, w_kv_up=w_kv_up, gqa_q_norm_g=gqa_q_norm_g, gqa_k_norm_g=gqa_k_norm_g, w_br_a=w_br_a,
                   w_br_b=w_br_b, w_out=w_out, norm2_g=norm2_g, w_up=w_up, conv_w=conv_w, conv_b=conv_b, w_down=w_down,
                   final_norm_g=final_norm_g)
    m_in = dict(c_ctx=m_c_ctx, w_ada=m_w_ada, b_ada=m_b_ada, norm1_g=m_norm1_g, w_in=m_w_in, mla_q_norm_g=m_mla_q_norm_g,
                w_q_up=m_w_q_up, mla_kv_norm_g=m_mla_kv_norm_g, w_kv_up=m_w_kv_up, gqa_q_norm_g=m_gqa_q_norm_g,
                gqa_k_norm_g=m_gqa_k_norm_g, w_br_a=m_w_br_a, w_br_b=m_w_br_b, w_out=m_w_out, norm2_g=m_norm2_g, w_up=m_w_up,
                conv_w=m_conv_w, conv_b=m_conv_b, w_down=m_w_down, final_norm_g=m_final_norm_g)
    v_in = dict(c_ctx=v_c_ctx, w_ada=v_w_ada, b_ada=v_b_ada, norm1_g=v_norm1_g, w_in=v_w_in, mla_q_norm_g=v_mla_q_norm_g,
                w_q_up=v_w_q_up, mla_kv_norm_g=v_mla_kv_norm_g, w_kv_up=v_w_kv_up, gqa_q_norm_g=v_gqa_q_norm_g,
                gqa_k_norm_g=v_gqa_k_norm_g, w_br_a=v_w_br_a, w_br_b=v_w_br_b, w_out=v_w_out, norm2_g=v_norm2_g, w_up=v_w_up,
                conv_w=v_conv_w, conv_b=v_conv_b, w_down=v_w_down, final_norm_g=v_final_norm_g)
    names = list(weights)
    big = [n for n in names if weights[n].ndim == 3 and weights[n].shape[1] >= 8]
    small = [n for n in names if n not in big]
    delta, new_m, new_v = {}, {}, {}

    def update(n):
        shp = weights[n].shape
        two_d = lambda a: a.reshape(shp[1], shp[2])
        g_t = n in arrives_transposed
        if n in arrives_halved:
            g_in, g_sib = grads[n]
        else:
            g_in, g_sib = two_d(grads[n].astype(F32)), None
        g_, d_, m_, v_ = _adamw(two_d(weights[n]), g_in, two_d(m_in[n]), two_d(v_in[n]), "adamw_" + n, g_transposed=g_t, g_sibling=g_sib)
        grads[n], delta[n], new_m[n], new_v[n] = g_.reshape(shp), d_.reshape(shp), m_.reshape(shp), v_.reshape(shp)

    delta["w_ada"], new_m["w_ada"], new_v["w_ada"] = d_ada[None], m_ada[None], v_ada[None]
    early = [n for n in big if n not in ("w_in", "w_ada")]
    for n in early[:-1]:
        update(n)
    done = sum(delta[n][0, 0:1, 0:1] for n in early[:-1])
    j_in = _join_start(_scatter_sums(rs_in, done, "in"), done, "in")
    last = early[-1]
    grads[last] = (grads[last][0] + j_in[4], grads[last][1])
    update(last)
    ((g_mine, g_sib),) = _join_wait(j_in, delta[last], "in")
    g_, d_, m_, v_ = _adamw(w_in[0].T, g_mine, m_w_in[0].T, v_w_in[0].T, "adamw_w_in", g_sibling=g_sib)
    grads["w_in"], delta["w_in"], new_m["w_in"], new_v["w_in"] = g_.T[None], d_.T[None], m_.T[None], v_.T[None]
    grads = {n: grads[n].reshape(weights[n].shape).astype(F32) for n in names}

    slab = lambda tree: [tree[n].reshape(-1, LANES) for n in small]
    d_, m_, v_ = _adamw_many(slab(weights), slab(grads), slab(m_in), slab(v_in), "adamw_small")
    for k, n in enumerate(small):
        shp = weights[n].shape
        delta[n], new_m[n], new_v[n] = d_[k].reshape(shp), m_[k].reshape(shp), v_[k].reshape(shp)

    return (loss, grad_x[None], *[grads[n] for n in names], *[delta[n] for n in names], *[new_m[n] for n in names],
            *[new_v[n] for n in names])
```

```python
import math

import jax
import jax.numpy as jnp
from jax import lax
from jax.experimental import pallas as pl
from jax.experimental.pallas import tpu as pltpu

F32 = jnp.float32
BF16 = jnp.bfloat16
MESH = pl.DeviceIdType.MESH

NORM_EPS = 1e-6
ROPE_THETA = 10000.0
GRID_W = 64
MLA_HEADS = 8
MLA_Q_LORA = 768
MLA_KV_LORA = 512
MLA_NOPE = 128
MLA_ROPE = 64
MLA_V = 128
GQA_HEADS = 8
GQA_KV_HEADS = 2
GQA_HEAD_DIM = 128
GQA_GROUP = GQA_HEADS // GQA_KV_HEADS
LANES = 128
KVP = MLA_KV_LORA + 2 * GQA_KV_HEADS * GQA_HEAD_DIM + LANES
QC = MLA_Q_LORA + GQA_HEADS * GQA_HEAD_DIM

ADAM_LR = 0.001
ADAM_B1 = 0.9
ADAM_B2 = 0.999
ADAM_EPS = 1e-08
ADAM_WD = 0.01
ADAM_STEP = 10

VMEM_LIMIT = 56 * 1024 * 1024


def _pick(dim, target, mult=LANES):
    t = (min(target, dim) // mult) * mult
    while t >= mult:
        if dim % t == 0:
            return t
        t -= mult
    return dim


def _params(sem):
    return pltpu.CompilerParams(dimension_semantics=sem, vmem_limit_bytes=VMEM_LIMIT)


_DIMS = {"NN": (((1,), (0,)), ((), ())), "NT": (((1,), (1,)), ((), ())), "TN": (((0,), (0,)), ((), ()))}


MM_VMEM_BUDGET = 36 * 1024 * 1024


def _mm_tiles(M, N, K, sa, sb, so, tm, tn, tk):
    tm, tn, tk = _pick(M, tm), _pick(N, tn), _pick(K, tk)

    def need(t):
        return 2 * (tm * t * sa + t * tn * sb) + 2 * tm * tn * so + (tm * tn * 4 if t < K else 0)

    while need(tk) > MM_VMEM_BUDGET and tk > LANES:
        smaller = _pick(K, tk - LANES)
        if smaller >= tk:
            break
        tk = smaller
    return tm, tn, tk


def _window(block, index, offsets):
    if not any(offsets):
        return pl.BlockSpec(block, index)
    for t, o in zip(block, offsets):
        assert o % 16 == 0 and t % 16 == 0, (block, offsets)

    def at(i, j, k):
        return tuple(pl.multiple_of(o + p * t, math.gcd(o, t)) for p, t, o in zip(index(i, j, k), block, offsets))

    return pl.BlockSpec(tuple(pl.Element(t) for t in block), at)


def _mm(a, b, mode, out_dtype, name, m=None, n=None, k=None, b_off=0, add=None, out_rows=None, out_base=None, out_off=0,
        tm=1024, tn=1024, tk=2304, act=None, bias=None, after=None):
    if mode == "NN":
        M, K, N = m or a.shape[0], k or a.shape[1], b.shape[1]
    elif mode == "NT":
        M, K, N = m or a.shape[0], a.shape[1], n or b.shape[0]
    else:
        M, K, N = a.shape[1], k or a.shape[0], b.shape[1]
    tm, tn, tk = _mm_tiles(M, N, K, a.dtype.itemsize, b.dtype.itemsize, jnp.dtype(out_dtype).itemsize, tm, tn, tk)
    nk = K // tk
    dims = _DIMS[mode]
    n_in = 2 + (bias is not None) + (add is not None) + (out_base is not None) + (after is not None)

    def body(*refs):
        a_ref, b_ref = refs[:2]
        bias_ref = refs[2] if bias is not None else None
        add_ref = refs[2 + (bias is not None)] if add is not None else None
        o_ref = refs[n_in]
        av = a_ref[...]
        if act == "silu":
            av = av * jax.nn.sigmoid(av)
        part = lax.dot_general(av.astype(BF16), b_ref[...].astype(BF16), dims, preferred_element_type=F32)

        def finish(r):
            if bias is not None:
                r = r + bias_ref[...]
            if add is not None:
                r = r + add_ref[...]
            o_ref[...] = r.astype(out_dtype)

        if nk == 1:
            finish(part)
            return
        acc = refs[-1]
        k = pl.program_id(2)

        @pl.when(k == 0)
        def _():
            acc[...] = part

        @pl.when(jnp.logical_and(k > 0, k < nk - 1))
        def _():
            acc[...] += part

        @pl.when(k == nk - 1)
        def _():
            finish(acc[...] + part)

    a_spec = pl.BlockSpec((tk, tm), lambda i, j, k: (k, i)) if mode == "TN" else pl.BlockSpec((tm, tk), lambda i, j, k: (i, k))
    if mode == "NT":
        b_spec = _window((tn, tk), lambda i, j, k: (j, k), (b_off, 0))
    else:
        b_spec = _window((tk, tn), lambda i, j, k: (k, j), (b_off, 0))
    in_specs, args = [a_spec, b_spec], [a, b]
    if bias is not None:
        in_specs.append(pl.BlockSpec((1, tn), lambda i, j, k: (0, j)))
        args.append(bias)
    if add is not None:
        in_specs.append(pl.BlockSpec((tm, tn), lambda i, j, k: (i, j)))
        args.append(add)
    aliases = {}
    if after is not None:
        in_specs.append(pl.BlockSpec(after.shape, lambda i, j, k: (0, 0)))
        args.append(after)
    if out_base is not None:
        aliases = {len(args): 0}
        in_specs.append(ANY)
        args.append(out_base)
        out_rows = out_base.shape[0]
    return pl.pallas_call(
        body,
        name=name,
        grid=(M // tm, N // tn, nk),
        in_specs=in_specs,
        out_specs=_window((tm, tn), lambda i, j, k: (i, j), (out_off, 0)),
        out_shape=jax.ShapeDtypeStruct((out_rows or M, N), out_dtype),
        input_output_aliases=aliases,
        scratch_shapes=[pltpu.VMEM((tm, tn), F32)] if nk > 1 else [],
        compiler_params=_params(("parallel", "parallel", "arbitrary")),
    )(*args)


def _rms(x):
    r = lax.rsqrt(jnp.mean(x * x, axis=-1, keepdims=True) + NORM_EPS)
    return x * r, r


def _rms_bwd(xh, r, dxh):
    return r * (dxh - xh * jnp.mean(dxh * xh, axis=-1, keepdims=True))


def _swap(x, q):
    lane = lax.broadcasted_iota(jnp.int32, x.shape, 1)
    even = ((lane // q) % 2) == 0
    return jnp.where(even, pltpu.roll(x, LANES - q, 1), pltpu.roll(x, q, 1))


def _rope(x, cos, ss, q):
    return x * cos + _swap(x, q) * ss


def _rope_t(d, cos, ss, q):
    return d * cos + _swap(d * ss, q)


def _csum(x):
    return jnp.sum(x, axis=0, keepdims=True)


def _rows(tr, w, off=0):
    return pl.BlockSpec((tr, w), lambda i: (i + off, 0))


def _bcast(w):
    return pl.BlockSpec((1, w), lambda i: (0, 0))


def _acc_init(i, refs):
    @pl.when(i == 0)
    def _():
        for r in refs:
            r[...] = jnp.zeros_like(r)


def _rope_tables(n_ctx, n_lat, rot_dim):
    rows = n_lat // GRID_W
    row = jnp.repeat(jnp.arange(rows, dtype=F32), GRID_W)
    col = jnp.tile(jnp.arange(GRID_W, dtype=F32), rows)
    half = rot_dim // 2
    inv_freq = ROPE_THETA ** (-jnp.arange(0, half, 2, dtype=F32) / half)
    ar, ac = row[:, None] * inv_freq, col[:, None] * inv_freq
    cos = jnp.concatenate([jnp.cos(ar), jnp.cos(ar), jnp.cos(ac), jnp.cos(ac)], axis=-1)
    ss = jnp.concatenate([-jnp.sin(ar), jnp.sin(ar), -jnp.sin(ac), jnp.sin(ac)], axis=-1)
    cos = jnp.tile(cos, (1, LANES // rot_dim))
    ss = jnp.tile(ss, (1, LANES // rot_dim))
    cos = jnp.concatenate([cos, jnp.ones((n_ctx, LANES), F32)], axis=0)
    ss = jnp.concatenate([ss, jnp.zeros((n_ctx, LANES), F32)], axis=0)
    return cos, ss


def _norm_mod_fwd(x2d, g, sh, sc, name, tr, out_rows=None, base=None, out_off=0):
    n, d = x2d.shape

    def body(x_ref, g_ref, sh_ref, sc_ref, *rest):
        xh, _ = _rms(x_ref[...])
        rest[-1][...] = ((xh * g_ref[...]) * (1.0 + sc_ref[...]) + sh_ref[...]).astype(BF16)

    args, in_specs, aliases = [x2d, g, sh, sc], [_rows(tr, d), _bcast(d), _bcast(d), _bcast(d)], {}
    if base is not None:
        args.append(base)
        in_specs.append(ANY)
        aliases = {4: 0}
        out_rows = base.shape[0]
    return pl.pallas_call(
        body,
        name=name,
        grid=(n // tr,),
        in_specs=in_specs,
        out_specs=_rows(tr, d, out_off // tr),
        out_shape=jax.ShapeDtypeStruct((out_rows or n, d), BF16),
        input_output_aliases=aliases,
        compiler_params=_params(("parallel",)),
    )(*args)


def _norm_mod_bwd(dz, dz_off, x2d, g, sc, dres, name, tr):
    n, d = x2d.shape
    want_dx = dres is not None

    def body(*refs):
        if want_dx:
            dz_ref, x_ref, g_ref, sc_ref, dres_ref, dx_ref, dg_ref, dsh_ref, dsc_ref = refs
        else:
            dz_ref, x_ref, g_ref, sc_ref, dg_ref, dsh_ref, dsc_ref = refs
        _acc_init(pl.program_id(0), [dg_ref, dsh_ref, dsc_ref])
        xh, r = _rms(x_ref[...])
        dzv = dz_ref[...]
        gv = g_ref[...]
        dsc_ref[...] += _csum(dzv * (xh * gv))
        dsh_ref[...] += _csum(dzv)
        dh = dzv * (1.0 + sc_ref[...])
        dg_ref[...] += _csum(dh * xh)
        if want_dx:
            dx_ref[...] = _rms_bwd(xh, r, dh * gv) + dres_ref[...]

    in_specs = [_rows(tr, d, dz_off), _rows(tr, d), _bcast(d), _bcast(d)]
    args = [dz, x2d, g, sc]
    out_specs = [_bcast(d)] * 3
    out_shape = [jax.ShapeDtypeStruct((1, d), F32)] * 3
    if want_dx:
        in_specs.append(_rows(tr, d))
        args.append(dres)
        out_specs = [_rows(tr, d)] + out_specs
        out_shape = [jax.ShapeDtypeStruct((n, d), F32)] + out_shape
    res = pl.pallas_call(
        body,
        name=name,
        grid=(n // tr,),
        in_specs=in_specs,
        out_specs=out_specs,
        out_shape=out_shape,
        compiler_params=_params(("arbitrary",)),
    )(*args)
    return res if want_dx else (None, *res)


_QA, _QB = MLA_ROPE // 4, GQA_HEAD_DIM // 4


def _kprep_fwd(pkv, kvg, kg, cos_a, ss_a, cos_b, ss_b, tr):
    n = pkv.shape[0]
    nb = GQA_KV_HEADS * GQA_HEAD_DIM

    def body(p_ref, kvg_ref, kg_ref, ca, sa, cb, sb, ckv_ref, kb_ref, vb_ref, kpe_ref):
        p = p_ref[...]
        xh, _ = _rms(p[:, :MLA_KV_LORA])
        ckv_ref[...] = (xh * kvg_ref[...]).astype(BF16)
        for e in range(GQA_KV_HEADS):
            lo = MLA_KV_LORA + e * GQA_HEAD_DIM
            kh, _ = _rms(p[:, lo : lo + GQA_HEAD_DIM])
            kb_ref[:, e * GQA_HEAD_DIM : (e + 1) * GQA_HEAD_DIM] = _rope(kh * kg_ref[...], cb[...], sb[...], _QB).astype(BF16)
        vb_ref[...] = p[:, MLA_KV_LORA + nb : MLA_KV_LORA + 2 * nb].astype(BF16)
        kr = _rope(p[:, MLA_KV_LORA + 2 * nb :], ca[...], sa[...], _QA)
        kpe_ref[:, :LANES] = kr.astype(BF16)
        kpe_ref[:, LANES:] = pltpu.roll(kr, MLA_ROPE, 1).astype(BF16)

    return pl.pallas_call(
        body,
        name="kprep_fwd",
        grid=(n // tr,),
        in_specs=[_rows(tr, KVP), _bcast(MLA_KV_LORA), _bcast(GQA_HEAD_DIM)] + [_rows(tr, LANES)] * 4,
        out_specs=[_rows(tr, MLA_KV_LORA), _rows(tr, nb), _rows(tr, nb), _rows(tr, 2 * LANES)],
        out_shape=[jax.ShapeDtypeStruct((n, w), BF16) for w in (MLA_KV_LORA, nb, nb, 2 * LANES)],
        compiler_params=_params(("parallel",)),
    )(pkv, kvg, kg, cos_a, ss_a, cos_b, ss_b)


def _kprep_bwd(pkv, dckv, dkb, dvb, dkpe, kvg, kg, cos_b, ss_b, tr):
    n = pkv.shape[0]
    nb = GQA_KV_HEADS * GQA_HEAD_DIM

    def body(p_ref, dckv_ref, dkb_ref, dvb_ref, dkpe_ref, kvg_ref, kg_ref, cb, sb, dp_ref, dkvg_ref, dkg_ref):
        _acc_init(pl.program_id(0), [dkvg_ref, dkg_ref])
        p = p_ref[...]
        xh, r = _rms(p[:, :MLA_KV_LORA])
        dn = dckv_ref[...]
        dkvg_ref[...] += _csum(dn * xh)
        dp_ref[:, :MLA_KV_LORA] = _rms_bwd(xh, r, dn * kvg_ref[...]).astype(BF16)
        for e in range(GQA_KV_HEADS):
            lo = MLA_KV_LORA + e * GQA_HEAD_DIM
            kh, rk = _rms(p[:, lo : lo + GQA_HEAD_DIM])
            dk = _rope_t(dkb_ref[:, e * GQA_HEAD_DIM : (e + 1) * GQA_HEAD_DIM], cb[...], sb[...], _QB)
            dkg_ref[...] += _csum(dk * kh)
            dp_ref[:, lo : lo + GQA_HEAD_DIM] = _rms_bwd(kh, rk, dk * kg_ref[...]).astype(BF16)
        dp_ref[:, MLA_KV_LORA + nb : MLA_KV_LORA + 2 * nb] = dvb_ref[...].astype(BF16)
        dp_ref[:, MLA_KV_LORA + 2 * nb :] = dkpe_ref[...].astype(BF16)

    return pl.pallas_call(
        body,
        name="kprep_bwd",
        grid=(n // tr,),
        in_specs=[_rows(tr, KVP), _rows(tr, MLA_KV_LORA), _rows(tr, nb), _rows(tr, nb), _rows(tr, LANES),
                  _bcast(MLA_KV_LORA), _bcast(GQA_HEAD_DIM), _rows(tr, LANES), _rows(tr, LANES)],
        out_specs=[_rows(tr, KVP), _bcast(MLA_KV_LORA), _bcast(GQA_HEAD_DIM)],
        out_shape=[jax.ShapeDtypeStruct((n, KVP), BF16), jax.ShapeDtypeStruct((1, MLA_KV_LORA), F32),
                   jax.ShapeDtypeStruct((1, GQA_HEAD_DIM), F32)],
        compiler_params=_params(("arbitrary",)),
    )(pkv, dckv, dkb, dvb, dkpe, kvg, kg, cos_b, ss_b)


def _kgrad_split(dka, dva, cos_a, ss_a, tr):
    n = dka.shape[0]
    wk = MLA_HEADS * 2 * LANES

    def body(dk_ref, dv_ref, ca, sa, dkv_ref, dkpe_ref):
        even = jnp.zeros((tr, LANES), F32)
        odd = jnp.zeros((tr, LANES), F32)
        for h in range(MLA_HEADS):
            dkv_ref[:, 2 * h * LANES : (2 * h + 1) * LANES] = dk_ref[:, 2 * h * LANES : (2 * h + 1) * LANES].astype(BF16)
            dkv_ref[:, (2 * h + 1) * LANES : (2 * h + 2) * LANES] = dv_ref[:, h * MLA_V : (h + 1) * MLA_V].astype(BF16)
            part = dk_ref[:, (2 * h + 1) * LANES : (2 * h + 2) * LANES]
            if h % 2 == 0:
                even = even + part
            else:
                odd = odd + part
        lane = lax.broadcasted_iota(jnp.int32, (tr, LANES), 1)
        low = lane < MLA_ROPE
        both = jnp.where(low, even, odd)
        tot = jnp.where(low, both + pltpu.roll(both, MLA_ROPE, 1), 0.0)
        dkpe_ref[...] = _rope_t(tot, ca[...], sa[...], _QA)

    return pl.pallas_call(
        body,
        name="kgrad_split",
        grid=(n // tr,),
        in_specs=[_rows(tr, wk), _rows(tr, MLA_HEADS * MLA_V), _rows(tr, LANES), _rows(tr, LANES)],
        out_specs=[_rows(tr, wk), _rows(tr, LANES)],
        out_shape=[jax.ShapeDtypeStruct((n, wk), BF16), jax.ShapeDtypeStruct((n, LANES), F32)],
        compiler_params=_params(("parallel",)),
    )(dka, dva, cos_a, ss_a)


def _qprep_fwd(pq, qg, gq, cos_b, ss_b, tr):
    n = pq.shape[0]
    nq = GQA_HEADS * GQA_HEAD_DIM

    def body(p_ref, qg_ref, gq_ref, cb, sb, cq_ref, qb_ref):
        xh, _ = _rms(p_ref[:, :MLA_Q_LORA])
        cq_ref[...] = (xh * qg_ref[...]).astype(BF16)
        for h in range(GQA_HEADS):
            lo = MLA_Q_LORA + h * GQA_HEAD_DIM
            qh, _ = _rms(p_ref[:, lo : lo + GQA_HEAD_DIM])
            qb_ref[:, h * GQA_HEAD_DIM : (h + 1) * GQA_HEAD_DIM] = _rope(qh * gq_ref[...], cb[...], sb[...], _QB).astype(BF16)

    return pl.pallas_call(
        body,
        name="qprep_fwd",
        grid=(n // tr,),
        in_specs=[_rows(tr, QC), _bcast(MLA_Q_LORA), _bcast(GQA_HEAD_DIM), _rows(tr, LANES), _rows(tr, LANES)],
        out_specs=[_rows(tr, MLA_Q_LORA), _rows(tr, nq)],
        out_shape=[jax.ShapeDtypeStruct((n, MLA_Q_LORA), BF16), jax.ShapeDtypeStruct((n, nq), BF16)],
        compiler_params=_params(("parallel",)),
    )(pq, qg, gq, cos_b, ss_b)


def _qprep_bwd(pq, dcq, dqb, qg, gq, cos_b, ss_b, tr):
    n = pq.shape[0]
    nq = GQA_HEADS * GQA_HEAD_DIM

    def body(p_ref, dcq_ref, dqb_ref, qg_ref, gq_ref, cb, sb, dp_ref, dqg_ref, dgq_ref):
        _acc_init(pl.program_id(0), [dqg_ref, dgq_ref])
        xh, r = _rms(p_ref[:, :MLA_Q_LORA])
        dn = dcq_ref[...]
        dqg_ref[...] += _csum(dn * xh)
        dp_ref[:, :MLA_Q_LORA] = _rms_bwd(xh, r, dn * qg_ref[...]).astype(BF16)
        for h in range(GQA_HEADS):
            lo = MLA_Q_LORA + h * GQA_HEAD_DIM
            qh, rq = _rms(p_ref[:, lo : lo + GQA_HEAD_DIM])
            dq = _rope_t(dqb_ref[:, h * GQA_HEAD_DIM : (h + 1) * GQA_HEAD_DIM], cb[...], sb[...], _QB)
            dgq_ref[...] += _csum(dq * qh)
            dp_ref[:, lo : lo + GQA_HEAD_DIM] = _rms_bwd(qh, rq, dq * gq_ref[...]).astype(BF16)

    return pl.pallas_call(
        body,
        name="qprep_bwd",
        grid=(n // tr,),
        in_specs=[_rows(tr, QC), _rows(tr, MLA_Q_LORA), _rows(tr, nq), _bcast(MLA_Q_LORA), _bcast(GQA_HEAD_DIM),
                  _rows(tr, LANES), _rows(tr, LANES)],
        out_specs=[_rows(tr, QC), _bcast(MLA_Q_LORA), _bcast(GQA_HEAD_DIM)],
        out_shape=[jax.ShapeDtypeStruct((n, QC), BF16), jax.ShapeDtypeStruct((1, MLA_Q_LORA), F32),
                   jax.ShapeDtypeStruct((1, GQA_HEAD_DIM), F32)],
        compiler_params=_params(("arbitrary",)),
    )(pq, dcq, dqb, qg, gq, cos_b, ss_b)


_QA_COLS = MLA_HEADS * (MLA_NOPE + MLA_ROPE)


def _qrope_fwd(qa, cos_a, ss_a, tr):
    n = qa.shape[0]

    def body(q_ref, ca, sa, o_ref):
        for j in range(MLA_HEADS // 2):
            lo = 3 * j * LANES
            o_ref[:, lo : lo + 2 * LANES] = q_ref[:, lo : lo + 2 * LANES].astype(BF16)
            o_ref[:, lo + 2 * LANES : lo + 3 * LANES] = _rope(q_ref[:, lo + 2 * LANES : lo + 3 * LANES], ca[...], sa[...], _QA).astype(BF16)

    return pl.pallas_call(
        body,
        name="qrope_fwd",
        grid=(n // tr,),
        in_specs=[_rows(tr, _QA_COLS), _rows(tr, LANES), _rows(tr, LANES)],
        out_specs=_rows(tr, _QA_COLS),
        out_shape=jax.ShapeDtypeStruct((n, _QA_COLS), BF16),
        compiler_params=_params(("parallel",)),
    )(qa, cos_a, ss_a)


def _qrope_bwd(dq2, cos_a, ss_a, tr):
    n = dq2.shape[0]

    def body(d_ref, ca, sa, o_ref):
        for j in range(MLA_HEADS // 2):
            lo = 3 * j * LANES
            h0, h1 = 2 * j, 2 * j + 1
            o_ref[:, lo : lo + LANES] = d_ref[:, 2 * h0 * LANES : (2 * h0 + 1) * LANES].astype(BF16)
            o_ref[:, lo + LANES : lo + 2 * LANES] = d_ref[:, 2 * h1 * LANES : (2 * h1 + 1) * LANES].astype(BF16)
            pe = d_ref[:, (2 * h0 + 1) * LANES : (2 * h0 + 2) * LANES] + d_ref[:, (2 * h1 + 1) * LANES : (2 * h1 + 2) * LANES]
            o_ref[:, lo + 2 * LANES : lo + 3 * LANES] = _rope_t(pe, ca[...], sa[...], _QA).astype(BF16)

    return pl.pallas_call(
        body,
        name="qrope_bwd",
        grid=(n // tr,),
        in_specs=[_rows(tr, MLA_HEADS * 2 * LANES), _rows(tr, LANES), _rows(tr, LANES)],
        out_specs=_rows(tr, _QA_COLS),
        out_shape=jax.ShapeDtypeStruct((n, _QA_COLS), BF16),
        compiler_params=_params(("parallel",)),
    )(dq2, cos_a, ss_a)


def _cat(refs):
    vals = [r[...] for r in refs]
    return vals[0] if len(vals) == 1 else jnp.concatenate(vals, axis=-1)


LOG2E = 1.4426950408889634


def _attn_fwd(qparts, kparts, vpart, n_heads, group, dv, scale, name, tq, after=None):
    T, Tk = qparts[0][0].shape[0], kparts[0][0].shape[0]
    nq_, nk_ = len(qparts), len(kparts)
    sub = min(tq, 256)
    c2 = scale * LOG2E

    def body(*refs):
        q_refs, k_refs = refs[:nq_], refs[nq_ : nq_ + nk_]
        v_ref = refs[nq_ + nk_]
        o_ref, lse_ref = refs[-2:]
        k = _cat(k_refs)
        v = v_ref[...]
        for r0 in range(0, tq, sub):
            q = _cat([r.at[r0 : r0 + sub, :] for r in q_refs])
            s = lax.dot_general(q, k, _DIMS["NT"], preferred_element_type=F32)
            m = jnp.max(s, axis=-1, keepdims=True)
            p = jnp.exp2((s - m) * c2)
            l = jnp.sum(p, axis=-1, keepdims=True)
            acc = jnp.dot(p.astype(BF16), v, preferred_element_type=F32)
            o_ref[r0 : r0 + sub, :] = (acc * (1.0 / l)).astype(BF16)
            lse_ref[r0 : r0 + sub, :] = m * scale + jnp.log(l)

    in_specs = [pl.BlockSpec((tq, LANES), lambda h, i, f=f: (i, f(h))) for _, f in qparts]
    in_specs += [pl.BlockSpec((Tk, LANES), lambda h, i, f=f: (0, f(h // group))) for _, f in kparts]
    fv = vpart[1]
    in_specs.append(pl.BlockSpec((Tk, dv), lambda h, i: (0, fv(h // group))))
    args = [*[a for a, _ in qparts], *[a for a, _ in kparts], vpart[0]]
    if after is not None:
        in_specs.append(pl.BlockSpec(after.shape, lambda h, i: (0, 0)))
        args.append(after)
    return pl.pallas_call(
        body,
        name=name,
        grid=(n_heads, T // tq),
        in_specs=in_specs,
        out_specs=[pl.BlockSpec((tq, dv), lambda h, i: (i, h)), pl.BlockSpec((None, tq, 1), lambda h, i: (h, i, 0))],
        out_shape=[jax.ShapeDtypeStruct((T, n_heads * dv), BF16), jax.ShapeDtypeStruct((n_heads, T, 1), F32)],
        compiler_params=_params(("parallel", "parallel")),
    )(*args)


def _attn_bwd(qparts, kparts, vpart, o, do, lse, n_heads, group, dv, scale, name, tq):
    T, Tk = qparts[0][0].shape[0], kparts[0][0].shape[0]
    nq_, nk_ = len(qparts), len(kparts)
    dk_ = LANES * nq_
    n_kv = n_heads // group
    nblk = T // tq
    c2 = scale * LOG2E

    def head(hk, i):
        return hk * group + i // nblk

    sub = min(tq, 256)

    def body(*refs):
        q_refs = refs[:nq_]
        k = _cat(refs[nq_ : nq_ + nk_])
        v_ref, o_ref, do_ref, lse_ref, dq_ref, dk_ref, dv_ref = refs[nq_ + nk_ :]
        i = pl.program_id(1)
        _acc_init(i, [dk_ref, dv_ref])
        v = v_ref[...]
        dk_acc, dv_acc = None, None
        for r0 in range(0, tq, sub):
            rows = slice(r0, r0 + sub)
            q = _cat([r.at[rows, :] for r in q_refs])
            s = lax.dot_general(q, k, _DIMS["NT"], preferred_element_type=F32)
            p = jnp.exp2(s * c2 - lse_ref[rows, :] * LOG2E)
            dov = do_ref[rows, :]
            dp = lax.dot_general(dov, v, _DIMS["NT"], preferred_element_type=F32)
            delta = jnp.sum(dov.astype(F32) * o_ref[rows, :].astype(F32), axis=-1, keepdims=True)
            ds = (p * (dp - delta)).astype(BF16)
            dq_ref[rows, :] = jnp.dot(ds, k, preferred_element_type=F32) * scale
            dk_part = lax.dot_general(ds, q, _DIMS["TN"], preferred_element_type=F32)
            dv_part = lax.dot_general(p.astype(BF16), dov, _DIMS["TN"], preferred_element_type=F32)
            dk_acc = dk_part if dk_acc is None else dk_acc + dk_part
            dv_acc = dv_part if dv_acc is None else dv_acc + dv_part
        dk_ref[...] += dk_acc
        dv_ref[...] += dv_acc

        @pl.when(i == group * nblk - 1)
        def _():
            dk_ref[...] *= scale

    in_specs = [pl.BlockSpec((tq, LANES), lambda hk, i, f=f: (i % nblk, f(head(hk, i)))) for _, f in qparts]
    in_specs += [pl.BlockSpec((Tk, LANES), lambda hk, i, f=f: (0, f(hk))) for _, f in kparts]
    fv = vpart[1]
    in_specs.append(pl.BlockSpec((Tk, dv), lambda hk, i: (0, fv(hk))))
    in_specs += [pl.BlockSpec((tq, dv), lambda hk, i: (i % nblk, head(hk, i)))] * 2
    in_specs.append(pl.BlockSpec((None, tq, 1), lambda hk, i: (head(hk, i), i % nblk, 0)))
    return pl.pallas_call(
        body,
        name=name,
        grid=(n_kv, group * nblk),
        in_specs=in_specs,
        out_specs=[pl.BlockSpec((tq, dk_), lambda hk, i: (i % nblk, head(hk, i))),
                   pl.BlockSpec((Tk, dk_), lambda hk, i: (0, hk)),
                   pl.BlockSpec((Tk, dv), lambda hk, i: (0, hk))],
        out_shape=[jax.ShapeDtypeStruct((T, n_heads * dk_), F32), jax.ShapeDtypeStruct((Tk, n_kv * dk_), F32),
                   jax.ShapeDtypeStruct((Tk, n_kv * dv), F32)],
        compiler_params=_params(("parallel", "arbitrary")),
    )(*[a for a, _ in qparts], *[a for a, _ in kparts], vpart[0], o, do, lse)


def _gates_fwd(pg, ya, yb, tr):
    n, d = ya.shape

    def body(pg_ref, ya_ref, yb_ref, o_ref):
        ga = jax.nn.sigmoid(pg_ref[:, :d].astype(F32))
        gb = jax.nn.sigmoid(pg_ref[:, d:].astype(F32))
        o_ref[...] = (ga * ya_ref[...].astype(F32) + gb * yb_ref[...].astype(F32)).astype(BF16)

    return pl.pallas_call(
        body,
        name="gates_fwd",
        grid=(n // tr,),
        in_specs=[_rows(tr, 2 * d), _rows(tr, d), _rows(tr, d)],
        out_specs=_rows(tr, d),
        out_shape=jax.ShapeDtypeStruct((n, d), BF16),
        compiler_params=_params(("parallel",)),
    )(pg, ya, yb)


def _gates_bwd(dm, pg, ya, yb, tr):
    n, d = ya.shape

    def body(dm_ref, pg_ref, ya_ref, yb_ref, dya_ref, dyb_ref, dpg_ref):
        dmv = dm_ref[...].astype(F32)
        ga = jax.nn.sigmoid(pg_ref[:, :d].astype(F32))
        gb = jax.nn.sigmoid(pg_ref[:, d:].astype(F32))
        dya_ref[...] = (dmv * ga).astype(BF16)
        dyb_ref[...] = (dmv * gb).astype(BF16)
        dpg_ref[:, :d] = (dmv * ya_ref[...].astype(F32) * ga * (1.0 - ga)).astype(BF16)
        dpg_ref[:, d:] = (dmv * yb_ref[...].astype(F32) * gb * (1.0 - gb)).astype(BF16)

    return pl.pallas_call(
        body,
        name="gates_bwd",
        grid=(n // tr,),
        in_specs=[_rows(tr, d), _rows(tr, 2 * d), _rows(tr, d), _rows(tr, d)],
        out_specs=[_rows(tr, d), _rows(tr, d), _rows(tr, 2 * d)],
        out_shape=[jax.ShapeDtypeStruct((n, d), BF16), jax.ShapeDtypeStruct((n, d), BF16), jax.ShapeDtypeStruct((n, 2 * d), BF16)],
        compiler_params=_params(("parallel",)),
    )(dm, pg, ya, yb)


def _resid_norm2_fwd(x2d, att, g1, n2g, sh2, sc2, tr):
    n, d = x2d.shape

    def body(x_ref, a_ref, g1_ref, g_ref, sh_ref, sc_ref, x1_ref, z_ref):
        x1 = x_ref[...] + g1_ref[...] * a_ref[...]
        x1_ref[...] = x1
        xh, _ = _rms(x1)
        z_ref[...] = ((xh * g_ref[...]) * (1.0 + sc_ref[...]) + sh_ref[...]).astype(BF16)

    return pl.pallas_call(
        body,
        name="resid_norm2_fwd",
        grid=(n // tr,),
        in_specs=[_rows(tr, d), _rows(tr, d)] + [_bcast(d)] * 4,
        out_specs=[_rows(tr, d), _rows(tr, d)],
        out_shape=[jax.ShapeDtypeStruct((n, d), F32), jax.ShapeDtypeStruct((n, d), BF16)],
        compiler_params=_params(("parallel",)),
    )(x2d, att, g1, n2g, sh2, sc2)


def _resid_norm2_bwd(dz2, x1, dx2, att, n2g, sc2, g1, tr):
    n, d = x1.shape

    def body(dz_ref, x1_ref, dx2_ref, a_ref, g_ref, sc_ref, g1_ref, dx1_ref, da_ref, dg_ref, dsh_ref, dsc_ref, dg1_ref):
        _acc_init(pl.program_id(0), [dg_ref, dsh_ref, dsc_ref, dg1_ref])
        xh, r = _rms(x1_ref[...])
        dzv = dz_ref[...]
        gv = g_ref[...]
        dsc_ref[...] += _csum(dzv * (xh * gv))
        dsh_ref[...] += _csum(dzv)
        dh = dzv * (1.0 + sc_ref[...])
        dg_ref[...] += _csum(dh * xh)
        dx1 = _rms_bwd(xh, r, dh * gv) + dx2_ref[...]
        dx1_ref[...] = dx1
        dg1_ref[...] += _csum(dx1 * a_ref[...])
        da_ref[...] = (dx1 * g1_ref[...]).astype(BF16)

    return pl.pallas_call(
        body,
        name="resid_norm2_bwd",
        grid=(n // tr,),
        in_specs=[_rows(tr, d)] * 4 + [_bcast(d)] * 3,
        out_specs=[_rows(tr, d), _rows(tr, d)] + [_bcast(d)] * 4,
        out_shape=[jax.ShapeDtypeStruct((n, d), F32), jax.ShapeDtypeStruct((n, d), BF16)] + [jax.ShapeDtypeStruct((1, d), F32)] * 4,
        compiler_params=_params(("arbitrary",)),
    )(dz2, x1, dx2, att, n2g, sc2, g1)


def _edges(shape):
    row = lax.broadcasted_iota(jnp.int32, shape, 0)
    return row == 0, row == shape[0] - 1


def _shifts(u, edges):
    n = u.shape[0]
    return jnp.where(edges[0], 0.0, pltpu.roll(u, 1, 0)), jnp.where(edges[1], 0.0, pltpu.roll(u, n - 1, 0))


def _conv3(u, prev, nxt, w_ref, b_ref):
    return b_ref[...] + w_ref[0:1, :] * prev + w_ref[1:2, :] * u + w_ref[2:3, :] * nxt


def _ffn_up_conv(z, wup_t, cw, cb, tc, after):
    n, d = z.shape
    f = wup_t.shape[0] // 2
    nb = f // tc

    def body(z_ref, wa_ref, wb_ref, cwa, cwb, cba, cbb, after_ref, ua_ref, ub_ref, h_ref):
        w = jnp.concatenate([wa_ref[...], wb_ref[...]], axis=0)
        u = lax.dot_general(z_ref[...], w, _DIMS["NT"], preferred_element_type=F32).astype(BF16)
        ua_ref[...] = u[:, :tc]
        ub_ref[...] = u[:, tc:]
        edges = _edges((n, tc))
        ua = u[:, :tc].astype(F32)
        ub = u[:, tc:].astype(F32)
        a = _conv3(ua, *_shifts(ua, edges), cwa, cba)
        b = _conv3(ub, *_shifts(ub, edges), cwb, cbb)
        h_ref[...] = (a * jax.nn.sigmoid(a) * b).astype(BF16)

    col = lambda rows, off: pl.BlockSpec((rows, tc), lambda i: (0, i + off))
    w_rows = lambda off: pl.BlockSpec((tc, d), lambda i: (i + off, 0))
    return pl.pallas_call(
        body,
        name="ffn_up_conv",
        grid=(nb,),
        in_specs=[pl.BlockSpec((n, d), lambda i: (0, 0)), w_rows(0), w_rows(nb), col(3, 0), col(3, nb), col(1, 0), col(1, nb),
                  pl.BlockSpec(after.shape, lambda i: (0, 0))],
        out_specs=[col(n, 0)] * 3,
        out_shape=[jax.ShapeDtypeStruct((n, f), BF16)] * 3,
        compiler_params=_params(("parallel",)),
    )(z, wup_t, wup_t, cw, cw, cb, cb, after)


def _ffn_down_dx_conv_bwd(df, wdown, u_a, u_b, cw, cb, tc, after):
    n, f = u_a.shape
    d = df.shape[1]
    nb = f // tc

    def part(uv, prev, nxt, duc, edges, w_ref, du_ref, dw_ref, db_ref):
        db_ref[...] = _csum(duc)
        dw_ref[0:1, :] = _csum(duc * prev)
        dw_ref[1:2, :] = _csum(duc * uv)
        dw_ref[2:3, :] = _csum(duc * nxt)
        d_prev, d_next = _shifts(duc, edges)
        du_ref[...] = (w_ref[0:1, :] * d_next + w_ref[1:2, :] * duc + w_ref[2:3, :] * d_prev).astype(BF16)

    def body(df_ref, wd_ref, ua_ref, ub_ref, wa_ref, wb_ref, ba_ref, bb_ref, after_ref,
             dua_ref, dub_ref, dwa_ref, dwb_ref, dba_ref, dbb_ref):
        dhv = lax.dot_general(df_ref[...], wd_ref[...], _DIMS["NT"], preferred_element_type=F32)
        dhv = dhv.astype(BF16).astype(F32)
        edges = _edges((n, tc))
        ua = ua_ref[...].astype(F32)
        ub = ub_ref[...].astype(F32)
        sa = _shifts(ua, edges)
        sb = _shifts(ub, edges)
        a = _conv3(ua, *sa, wa_ref, ba_ref)
        b = _conv3(ub, *sb, wb_ref, bb_ref)
        sg = jax.nn.sigmoid(a)
        da = dhv * b * (sg * (1.0 + a * (1.0 - sg)))
        db = dhv * (a * sg)
        part(ua, *sa, da, edges, wa_ref, dua_ref, dwa_ref, dba_ref)
        part(ub, *sb, db, edges, wb_ref, dub_ref, dwb_ref, dbb_ref)

    col = lambda rows, off: pl.BlockSpec((rows, tc), lambda i: (0, i + off))
    return pl.pallas_call(
        body,
        name="ffn_down_dx_conv_bwd",
        grid=(nb,),
        in_specs=[pl.BlockSpec((n, d), lambda i: (0, 0)), pl.BlockSpec((tc, d), lambda i: (i, 0)), col(n, 0), col(n, 0),
                  col(3, 0), col(3, nb), col(1, 0), col(1, nb), pl.BlockSpec(after.shape, lambda i: (0, 0))],
        out_specs=[col(n, 0), col(n, 0), col(3, 0), col(3, 0), col(1, 0), col(1, 0)],
        out_shape=[jax.ShapeDtypeStruct((n, f), BF16)] * 2 + [jax.ShapeDtypeStruct((3, f), F32)] * 2 + [jax.ShapeDtypeStruct((1, f), F32)] * 2,
        compiler_params=_params(("parallel",)),
    )(df, wdown, u_a, u_b, cw, cw, cb, cb, after)


def _loss_head(x1, f, g2, fg, tgt, tr):
    n, d = x1.shape

    def body(x1_ref, f_ref, g2_ref, fg_ref, t_ref, sq_ref, dx2_ref, dfg_ref, dg2_ref, df_ref):
        _acc_init(pl.program_id(0), [sq_ref, dfg_ref, dg2_ref])
        fv = f_ref[...]
        xh, r = _rms(x1_ref[...] + g2_ref[...] * fv)
        err = xh * fg_ref[...] - t_ref[...]
        sq_ref[...] += _csum(err * err)
        dy = err * (1.0 / d)
        dfg_ref[...] += _csum(dy * xh)
        dx2 = _rms_bwd(xh, r, dy * fg_ref[...])
        dx2_ref[...] = dx2
        dg2_ref[...] += _csum(dx2 * fv)
        df_ref[...] = (dx2 * g2_ref[...]).astype(BF16)

    return pl.pallas_call(
        body,
        name="loss_head",
        grid=(n // tr,),
        in_specs=[_rows(tr, d), _rows(tr, d), _bcast(d), _bcast(d), _rows(tr, d)],
        out_specs=[_bcast(d), _rows(tr, d), _bcast(d), _bcast(d), _rows(tr, d)],
        out_shape=[jax.ShapeDtypeStruct((1, d), F32), jax.ShapeDtypeStruct((n, d), F32), jax.ShapeDtypeStruct((1, d), F32),
                   jax.ShapeDtypeStruct((1, d), F32), jax.ShapeDtypeStruct((n, d), BF16)],
        compiler_params=_params(("arbitrary",)),
    )(x1, f, g2, fg, tgt)


def _sum_slots(g, name):
    s, r, w = g.shape

    def body(g_ref, o_ref):
        acc = g_ref[0]
        for k in range(1, s):
            acc = acc + g_ref[k]
        o_ref[...] = acc

    return pl.pallas_call(body, name=name, out_shape=jax.ShapeDtypeStruct((r, w), F32))(g)


def _silu_grad_mul(ds, cvec):
    def body(d_ref, c_ref, o_ref):
        cv = c_ref[...]
        sg = jax.nn.sigmoid(cv)
        o_ref[...] = d_ref[...] * (sg * (1.0 + cv * (1.0 - sg)))

    return pl.pallas_call(body, name="silu_grad_mul", out_shape=jax.ShapeDtypeStruct(ds.shape, F32))(ds, cvec)


def _adamw_update(wv, gv, mv, vv, d_ref, mo_ref, vo_ref):
    mn = ADAM_B1 * mv + (1.0 - ADAM_B1) * gv
    vn = ADAM_B2 * vv + (1.0 - ADAM_B2) * (gv * gv)
    mo_ref[...] = mn
    vo_ref[...] = vn
    m_hat = mn / (1.0 - ADAM_B1**ADAM_STEP)
    v_hat = vn / (1.0 - ADAM_B2**ADAM_STEP)
    d_ref[...] = -ADAM_LR * (m_hat / (jnp.sqrt(v_hat) + ADAM_EPS) + ADAM_WD * wv)


def _adamw_many(ws, gs, ms, vs, name):
    n = len(ws)

    def body(*refs):
        for k in range(n):
            w_ref, g_ref, m_ref, v_ref = (refs[q * n + k] for q in range(4))
            d_ref, mo_ref, vo_ref = (refs[(4 + q) * n + k] for q in range(3))
            _adamw_update(w_ref[...], g_ref[...], m_ref[...], v_ref[...], d_ref, mo_ref, vo_ref)

    res = pl.pallas_call(body, name=name, out_shape=[jax.ShapeDtypeStruct(w.shape, F32) for w in ws] * 3)(*ws, *gs, *ms, *vs)
    return res[:n], res[n : 2 * n], res[2 * n :]


def _adamw(w, g, m, v, name, g_transposed=False, g_sibling=None):
    r, cdim = w.shape
    halves = g_sibling is not None
    block = 1 << 19
    if g_transposed:
        tc = _pick(cdim // 2 if halves else cdim, 2048)
        tr = _pick(r, max(LANES, block // tc), LANES)
        per_half = (cdim // 2) // tc
    else:
        rows = r // 2 if halves else r
        tc = _pick(cdim, 2048)
        tr = _pick(rows, max(8, block // tc), 8)
        if tr < 64 and rows > 64:
            tr, tc = _pick(rows, 1024, 8), _pick(cdim, 512)
        per_half = (r // 2) // tr
    emit_g = g_transposed or halves

    def body(w_ref, g_ref, *rest):
        m_ref, v_ref = rest[halves : halves + 2]
        outs = rest[halves + 2 :]
        gv = g_ref[...]
        if halves:
            along = pl.program_id(1 if g_transposed else 0)
            gv = jnp.where(along // per_half == lax.axis_index("c"), gv, rest[0][...])
        if g_transposed:
            gv = gv.T
        if emit_g:
            outs[0][...] = gv
        _adamw_update(w_ref[...], gv, m_ref[...], v_ref[...], *outs[-3:])

    spec = pl.BlockSpec((tr, tc), lambda i, j: (i, j))
    if g_transposed:
        g_spec = pl.BlockSpec((tc, tr), lambda i, j: (j % per_half if halves else j, i))
    else:
        g_spec = pl.BlockSpec((tr, tc), lambda i, j: (i % per_half if halves else i, j))
    n_out = 3 + emit_g
    res = pl.pallas_call(
        body,
        name=name,
        grid=(r // tr, cdim // tc),
        in_specs=[spec, g_spec] + [g_spec] * halves + [spec, spec],
        out_specs=[spec] * n_out,
        out_shape=[jax.ShapeDtypeStruct((r, cdim), F32)] * n_out,
        compiler_params=_params(("parallel", "parallel")),
    )(w, g, *([g_sibling] if halves else []), m, v)
    return res if emit_g else [g, *res]


def _place():
    return lax.axis_index("x"), lax.axis_index("y"), lax.axis_index("c")


def _remote(src, dst, send_sem, recv_sem, dev):
    return pltpu.make_async_remote_copy(src_ref=src, dst_ref=dst, send_sem=send_sem, recv_sem=recv_sem, device_id=dev, device_id_type=MESH)


ANY = pl.BlockSpec(memory_space=pl.ANY)


def _all_gather_small(v, name, after=()):
    r, w = v.shape

    def body(v_ref, *rest):
        o_ref, send, recv, lsem = rest[len(after) :]
        x, y, c = _place()
        me = 4 * x + 2 * y + c
        mine = pltpu.make_async_copy(v_ref, o_ref.at[me], lsem)
        mine.start()
        sent = []
        for k in range(1, 8):
            px, py, pc = x ^ (k >> 2), y ^ ((k >> 1) & 1), c ^ (k & 1)
            cp = _remote(v_ref, o_ref.at[me], send.at[k - 1], recv.at[k - 1], (px, py, pc))
            cp.start()
            sent.append(cp)
        for k in range(1, 8):
            px, py, pc = x ^ (k >> 2), y ^ ((k >> 1) & 1), c ^ (k & 1)
            slot = o_ref.at[4 * px + 2 * py + pc]
            _remote(slot, slot, send.at[k - 1], recv.at[k - 1], (x, y, c)).wait_recv()
        for cp in sent:
            cp.wait_send()
        mine.wait()

    return pl.pallas_call(
        body,
        name=name,
        out_shape=jax.ShapeDtypeStruct((8, r, w), F32),
        in_specs=[pl.BlockSpec(memory_space=pltpu.VMEM)] + [ANY] * len(after),
        out_specs=pl.BlockSpec(memory_space=pltpu.VMEM),
        scratch_shapes=[pltpu.SemaphoreType.DMA((7,)), pltpu.SemaphoreType.DMA((7,)), pltpu.SemaphoreType.DMA],
        compiler_params=pltpu.CompilerParams(vmem_limit_bytes=VMEM_LIMIT),
    )(v, *after)


HBM = pl.BlockSpec(memory_space=pltpu.HBM)
SEM = pl.BlockSpec(memory_space=pltpu.SEMAPHORE)
EFFECT = pltpu.SideEffectType.DATAFLOW_SIDE_EFFECTING


def _other_chips(x, y):
    return [(1 - x, y), (x, 1 - y), (1 - x, 1 - y)]


def _bulk_start(name, srcs, land_shapes, n_copies, copies, after, lands_init=None):
    n, m = len(srcs), len(land_shapes)

    def body(*refs):
        src_refs, land_refs = refs[:n], refs[n : n + m]
        send, recv = refs[n + m + 1], refs[n + m + 2]
        token = refs[-1]
        for k, (s, d, dev) in enumerate(copies(src_refs, land_refs)):
            _remote(s, d, send.at[k], recv.at[k], dev).start()
        token[...] = jnp.zeros_like(token)

    lands = lands_init or [lax.empty(s.shape, s.dtype) for s in land_shapes]
    lands = [pltpu.with_memory_space_constraint(b, pltpu.HBM) for b in lands]
    out = pl.pallas_call(
        body,
        name=name,
        out_shape=(pltpu.SemaphoreType.DMA((n_copies,)), pltpu.SemaphoreType.DMA((n_copies,)),
                   *[pltpu.HBM(s.shape, s.dtype) for s in srcs], *[pltpu.HBM(s.shape, s.dtype) for s in land_shapes],
                   jax.ShapeDtypeStruct((8, LANES), F32)),
        in_specs=[HBM] * (n + m) + [ANY],
        out_specs=(SEM, SEM, *[HBM] * (n + m), pl.BlockSpec(memory_space=pltpu.VMEM)),
        input_output_aliases={i: 2 + i for i in range(n + m)},
        compiler_params=pltpu.CompilerParams(has_side_effects=EFFECT),
    )(*[pltpu.with_memory_space_constraint(s, pltpu.HBM) for s in srcs], *lands, after)
    return out[0], out[1], list(out[2 : 2 + n]), list(out[2 + n : 2 + n + m]), out[-1][0:1, 0:1]


def _bulk_wait(name, send, recv, srcs, lands, after, waits):
    n, m = len(srcs), len(lands)

    def body(*refs):
        src_refs, land_refs = refs[:n], refs[n : n + m]
        send_sem, recv_sem = refs[n + m], refs[n + m + 1]
        x, y, c = _place()
        for k, (s, d) in enumerate(waits(src_refs, land_refs)):
            cp = _remote(s, d, send_sem.at[k], recv_sem.at[k], (x, y, c))
            cp.wait_send()
            cp.wait_recv()

    out = pl.pallas_call(
        body,
        name=name,
        out_shape=tuple(pltpu.HBM(s.shape, s.dtype) for s in (*srcs, *lands)),
        in_specs=[HBM] * (n + m) + [SEM, SEM, ANY],
        out_specs=tuple([HBM] * (n + m)),
        input_output_aliases={i: i for i in range(n + m)},
        compiler_params=pltpu.CompilerParams(has_side_effects=EFFECT),
    )(*srcs, *lands, send, recv, after)
    return list(out[:n]), list(out[n:])


def _gather_start(shards, after, name):
    def copies(src, land):
        x, y, c = _place()
        j = 2 * x + y
        return [(src[a].at[c], land[a].at[j, c], (px, py, c)) for a in range(len(shards)) for px, py in _other_chips(x, y)]

    shapes = [jax.ShapeDtypeStruct((4,) + s.shape, s.dtype) for s in shards]
    j = 2 * lax.axis_index("x") + lax.axis_index("y")
    init = [lax.dynamic_update_slice(lax.empty(t.shape, t.dtype), s[None], (j, 0, 0, 0)) for t, s in zip(shapes, shards)]
    return _bulk_start(name, shards, shapes, 3 * len(shards), copies, after, init)


def _gather_wait(started, after, name):
    send, recv, srcs, lands, _ = started

    def waits(src, land):
        x, y, c = _place()
        return [(src[a].at[c], land[a].at[2 * px + py, c]) for a in range(len(srcs)) for px, py in _other_chips(x, y)]

    return _bulk_wait(name, send, recv, srcs, lands, after, waits)


def _forward_halves(lands, name):
    n = len(lands)

    def body(*refs):
        bufs = refs[n : 2 * n]
        send, recv = refs[2 * n :]
        x, y, c = _place()
        started = []
        for a in range(n):
            for k, (px, py) in enumerate(_other_chips(x, y)):
                blk = bufs[a].at[2 * px + py, c]
                cp = _remote(blk, blk, send.at[3 * a + k], recv.at[3 * a + k], (x, y, 1 - c))
                cp.start()
                started.append(cp)
        for a in range(n):
            for k, (px, py) in enumerate(_other_chips(x, y)):
                blk = bufs[a].at[2 * px + py, 1 - c]
                _remote(blk, blk, send.at[3 * a + k], recv.at[3 * a + k], (x, y, c)).wait_recv()
        for cp in started:
            cp.wait_send()

    return pl.pallas_call(
        body,
        name=name,
        out_shape=[jax.ShapeDtypeStruct(b.shape, b.dtype) for b in lands],
        in_specs=[ANY] * n,
        out_specs=[ANY] * n,
        input_output_aliases={i: i for i in range(n)},
        scratch_shapes=[pltpu.SemaphoreType.DMA((3 * n,)), pltpu.SemaphoreType.DMA((3 * n,))],
    )(*lands)


def _forward_start(lands, after, name):
    def copies(src, _):
        x, y, c = _place()
        blocks = [src[a].at[2 * px + py, c] for a in range(len(lands)) for px, py in _other_chips(x, y)]
        return [(b, b, (x, y, 1 - c)) for b in blocks]

    return _bulk_start(name, lands, [], 3 * len(lands), copies, after)


def _forward_wait(started, after, name):
    send, recv, bufs, _, _ = started

    def waits(src, _):
        x, y, c = _place()
        return [(src[a].at[2 * px + py, c], src[a].at[2 * px + py, 1 - c]) for a in range(len(bufs)) for px, py in _other_chips(x, y)]

    return _bulk_wait(name, send, recv, bufs, [], after, waits)[0]


def _as_rows(lands):
    return [f.reshape(4 * f.shape[2] * 2, f.shape[3]) for f in lands]


def _gather_finish(started, after, tag):
    _, lands = _gather_wait(started, after, "gather_wait_" + tag)
    return _as_rows(_forward_halves(lands, "gather_forward_" + tag))


def _gather_land(started, after, tag):
    shards, lands = _gather_wait(started, after, "gather_wait_" + tag)
    return shards, _forward_start(lands, shards[0], "forward_start_" + tag)


def _gather_done(landed, after, tag):
    _, fwd = landed
    return _as_rows(_forward_wait(fwd, after, "forward_wait_" + tag))


def _swap_halves(grads, name):
    n = len(grads)

    def body(*refs):
        ins, outs = refs[:n], refs[n : 2 * n]
        send, recv = refs[2 * n :]
        x, y, c = _place()
        started = []
        for a in range(n):
            for s in range(4):
                cp = _remote(ins[a].at[s, 1 - c], outs[a].at[s], send.at[4 * a + s], recv.at[4 * a + s], (x, y, 1 - c))
                cp.start()
                started.append(cp)
        for cp in started:
            cp.wait_recv()
        for cp in started:
            cp.wait_send()

    return pl.pallas_call(
        body,
        name=name,
        out_shape=[jax.ShapeDtypeStruct((4,) + g.shape[2:], g.dtype) for g in grads],
        in_specs=[ANY] * n,
        out_specs=[ANY] * n,
        scratch_shapes=[pltpu.SemaphoreType.DMA((4 * n,)), pltpu.SemaphoreType.DMA((4 * n,))],
    )(*grads)


def _add_halves(grads, others, tag):
    outs = []
    for a, (g, o) in enumerate(zip(grads, others)):
        _, _, rh, cdim = g.shape
        tr = _pick(rh, 512, 16)

        def body(g_ref, o_ref, p_ref):
            p_ref[...] = (g_ref[...].astype(F32) + o_ref[...].astype(F32)).astype(BF16)

        outs.append(
            pl.pallas_call(
                body,
                name=f"add_halves_{tag}{a}",
                grid=(4, rh // tr),
                in_specs=[pl.BlockSpec((None, None, tr, cdim), lambda s, i: (s, lax.axis_index("c"), i, 0)),
                          pl.BlockSpec((None, tr, cdim), lambda s, i: (s, i, 0))],
                out_specs=pl.BlockSpec((None, tr, cdim), lambda s, i: (s, i, 0)),
                out_shape=jax.ShapeDtypeStruct((4, rh, cdim), BF16),
                compiler_params=_params(("parallel", "parallel")),
            )(g, o)
        )
    return outs


def _exchange_start(parts, after, name):
    def copies(src, land):
        x, y, c = _place()
        j = 2 * x + y
        return [(src[a].at[2 * px + py], land[a].at[j], (px, py, c)) for a in range(len(parts)) for px, py in _other_chips(x, y)]

    return _bulk_start(name, parts, [jax.ShapeDtypeStruct(p.shape, p.dtype) for p in parts], 3 * len(parts), copies, after)


def _exchange_finish(started, after, name):
    send, recv, srcs, lands, _ = started

    def waits(src, land):
        x, y, _ = _place()
        return [(src[a].at[2 * px + py], land[a].at[2 * px + py]) for a in range(len(srcs)) for px, py in _other_chips(x, y)]

    srcs, lands = _bulk_wait(name, send, recv, srcs, lands, after, waits)
    return lands, srcs


def _sum_chips(recvd, parts, tag):
    outs = []
    for a, (g, p) in enumerate(zip(recvd, parts)):
        _, rh, cdim = g.shape
        tr = _pick(rh, 512, 16)

        def body(g_ref, p_ref, o_ref):
            j = 2 * lax.axis_index("x") + lax.axis_index("y")
            own = p_ref[...].astype(F32)
            term = [jnp.where(j == s, own, g_ref[s].astype(F32)) for s in range(4)]
            o_ref[...] = ((term[0] + term[1]) + term[2]) + term[3]

        outs.append(
            pl.pallas_call(
                body,
                name=f"sum_chips_{tag}{a}",
                grid=(rh // tr,),
                in_specs=[pl.BlockSpec((4, tr, cdim), lambda i: (0, i, 0)),
                          pl.BlockSpec((None, tr, cdim), lambda i: (2 * lax.axis_index("x") + lax.axis_index("y"), i, 0))],
                out_specs=pl.BlockSpec((tr, cdim), lambda i: (i, 0)),
                out_shape=jax.ShapeDtypeStruct((rh, cdim), F32),
                compiler_params=_params(("parallel",)),
            )(g, p)
        )
    return outs


def _join_halves(halves, name):
    n = len(halves)

    def body(*refs):
        ins, outs = refs[:n], refs[n : 2 * n]
        send, recv = refs[2 * n :]
        x, y, c = _place()
        started = []
        for a in range(n):
            cp = _remote(ins[a], outs[a], send.at[a], recv.at[a], (x, y, 1 - c))
            cp.start()
            started.append(cp)
        for cp in started:
            cp.wait_recv()
        for cp in started:
            cp.wait_send()

    others = pl.pallas_call(
        body,
        name=name,
        out_shape=[jax.ShapeDtypeStruct(h.shape, h.dtype) for h in halves],
        in_specs=[ANY] * n,
        out_specs=[ANY] * n,
        scratch_shapes=[pltpu.SemaphoreType.DMA((n,)), pltpu.SemaphoreType.DMA((n,))],
    )(*halves)
    return list(zip(halves, others))


def _joined(mine, other):
    first = lax.axis_index("c") == 0
    return jnp.concatenate([jnp.where(first, mine, other), jnp.where(first, other, mine)], axis=0)


def _grad_views(grads):
    return [g.reshape(4, 2, g.shape[0] // 8, g.shape[1]) for g in grads]


def _scatter_start(grads, tag, after=None):
    views = _grad_views(grads)
    others = _swap_halves(views, "swap_halves_" + tag)
    mine = _add_halves(views, others, tag)
    return _exchange_start(mine, others[-1] if after is None else after, "exchange_start_" + tag)


def _swap_start(grads, after, tag):
    views = _grad_views(grads)

    def copies(src, land):
        x, y, c = _place()
        return [(src[a].at[s, 1 - c], land[a].at[s], (x, y, 1 - c)) for a in range(len(views)) for s in range(4)]

    shapes = [jax.ShapeDtypeStruct((4,) + v.shape[2:], v.dtype) for v in views]
    return _bulk_start("swap_start_" + tag, views, shapes, 4 * len(views), copies, after)


def _scatter_start_after_swap(swapped, after, tag):
    send, recv, views, lands, _ = swapped

    def waits(src, land):
        c = lax.axis_index("c")
        return [(src[a].at[s, 1 - c], land[a].at[s]) for a in range(len(views)) for s in range(4)]

    views, others = _bulk_wait("swap_wait_" + tag, send, recv, views, lands, after, waits)
    mine = _add_halves(views, others, tag)
    return _exchange_start(mine, others[-1], "exchange_start_" + tag)


def _join_start(halves, after, tag):
    def copies(src, land):
        x, y, c = _place()
        return [(src[a], land[a], (x, y, 1 - c)) for a in range(len(halves))]

    return _bulk_start("join_start_" + tag, halves, [jax.ShapeDtypeStruct(h.shape, h.dtype) for h in halves], len(halves), copies, after)


def _join_wait(started, after, tag):
    send, recv, halves, lands, _ = started
    halves, others = _bulk_wait("join_wait_" + tag, send, recv, halves, lands, after, lambda src, land: list(zip(src, land)))
    return list(zip(halves, others))


def _scatter_sums(started, after, tag):
    return _sum_chips(*_exchange_finish(started, after, "exchange_wait_" + tag), tag)


def _scatter_finish(started, after, tag):
    return _join_halves(_scatter_sums(started, after, tag), "join_halves_" + tag)


def _t_bf16(w):
    return w.T.astype(BF16)


def kernel(x, c, ctx, c_ctx, w_ada, b_ada, norm1_g, w_in, mla_q_norm_g, w_q_up, mla_kv_norm_g, w_kv_up, gqa_q_norm_g, gqa_k_norm_g, w_br_a, w_br_b, w_out, norm2_g, w_up, conv_w, conv_b, w_down, final_norm_g, loss_target, m_c_ctx, m_w_ada, m_b_ada, m_norm1_g, m_w_in, m_mla_q_norm_g, m_w_q_up, m_mla_kv_norm_g, m_w_kv_up, m_gqa_q_norm_g, m_gqa_k_norm_g, m_w_br_a, m_w_br_b, m_w_out, m_norm2_g, m_w_up, m_conv_w, m_conv_b, m_w_down, m_final_norm_g, v_c_ctx, v_w_ada, v_b_ada, v_norm1_g, v_w_in, v_mla_q_norm_g, v_w_q_up, v_mla_kv_norm_g, v_w_kv_up, v_gqa_q_norm_g, v_gqa_k_norm_g, v_w_br_a, v_w_br_b, v_w_out, v_norm2_g, v_w_up, v_conv_w, v_conv_b, v_w_down, v_final_norm_g):
    T, D = x.shape[1], x.shape[2]
    C = ctx.shape[1]
    NA = w_ada.shape[2]
    NW = w_up.shape[2]
    F2 = 4 * NW
    FF = F2 // 2
    xi, yi, ci = _place()
    j = 2 * xi + yi
    me = 4 * xi + 2 * yi + ci
    tr = _pick(C, 256, 8)

    x2d, tgt, ctx2d = x[0], loss_target[0], ctx[0]
    fg = final_norm_g.reshape(1, D)
    cc = c_ctx.reshape(1, D)

    halve = lambda s: s.reshape(2, s.shape[0] // 2, s.shape[1])
    win_shard = halve(_t_bf16(w_in[0]))
    w0 = max(D, NW)
    pay = jnp.zeros((8, w0), F32).at[0:1, :D].set(c).at[1:4, :NW].set(conv_w[0])
    got = _all_gather_small(pay, "gather_cond")
    ag_in = _gather_start([win_shard], got, "gather_start_in")
    t_in = ag_in[4]
    c_all = got[:, 0, :D]
    cw = jnp.concatenate([got[2 * s, 1:4, :NW] for s in range(4)], axis=1)
    s16 = jnp.concatenate([c_all, cc, jnp.zeros((7, D), F32)], axis=0) + t_in
    b_cols = lax.dynamic_slice(b_ada, (0, j * NA), (1, NA))
    ada_part = _mm(s16, w_ada[0], "NN", F32, "ada_fwd", act="silu", bias=b_cols)

    wq3 = (w_q_up[0] + t_in).reshape(MLA_Q_LORA, 2, MLA_NOPE + MLA_ROPE)
    wq_perm = jnp.concatenate([wq3[:, :, :MLA_NOPE].reshape(MLA_Q_LORA, -1), wq3[:, :, MLA_NOPE:].reshape(MLA_Q_LORA, -1)], axis=1)
    low = [halve(_t_bf16(wq_perm)), halve(_t_bf16(w_kv_up[0] + t_in))]
    br = [halve(_t_bf16(w_br_a[0] + t_in)), halve(_t_bf16(w_br_b[0] + t_in)), halve((w_out[0] + t_in).astype(BF16))]
    up = [halve(_t_bf16(w_up[0] + t_in))]
    down = [halve((w_down[0] + t_in).astype(BF16))]

    got = _all_gather_small(ada_part, "gather_ada", after=(*low, *br, *up, *down))
    ada = jnp.concatenate([got[2 * s] for s in range(4)], axis=1)
    lat = lax.dynamic_slice(ada, (me, 0), (1, 6 * D))
    sh1, sc1, g1, sh2, sc2, g2 = [lat[:, k * D : (k + 1) * D] for k in range(6)]
    csh, csc = ada[8:9, :D], ada[8:9, D : 2 * D]
    ag_low = _gather_start(low, got, "gather_start_low")
    ag_br = _gather_start(br, ag_low[4], "gather_start_br")
    ag_up = _gather_start(up, ag_br[4], "gather_start_up")
    ag_down = _gather_start(down, ag_up[4], "gather_start_down")
    sh1 = sh1 + ag_down[4]

    cos_a, ss_a = _rope_tables(C, T, MLA_ROPE)
    cos_b, ss_b = _rope_tables(C, T, GQA_HEAD_DIM)
    lcos_a, lss_a, lcos_b, lss_b = cos_a[:T], ss_a[:T], cos_b[:T], ss_b[:T]

    z_all = _norm_mod_fwd(x2d, norm1_g, sh1, sc1, "norm1_lat_fwd", tr, out_rows=T + C)
    z_all = _norm_mod_fwd(ctx2d, norm1_g, csh, csc, "norm1_ctx_fwd", tr, base=z_all, out_off=T)
    (win_t,) = _gather_finish(ag_in, z_all, "in")
    kv_cols = KVP - LANES + MLA_ROPE
    e_kpe = MLA_KV_LORA + MLA_ROPE
    w_kvp = jnp.concatenate([win_t[:MLA_KV_LORA], win_t[e_kpe:kv_cols], win_t[MLA_KV_LORA:e_kpe], jnp.zeros((LANES - MLA_ROPE, D), BF16)], axis=0)

    pkv = _mm(z_all, w_kvp, "NT", F32, "proj_kv", tn=KVP)
    pq = _mm(z_all, win_t, "NT", F32, "proj_q", m=T, n=QC, b_off=kv_cols)
    low_landed = _gather_land(ag_low, pq, "low")
    pg = _mm(z_all, win_t, "NT", BF16, "proj_g", m=T, n=2 * D, b_off=kv_cols + QC, after=low_landed[1][4])
    wq_t, wkv_t = _gather_done(low_landed, pg, "low")
    ckv_n, kb2, vb2, kpe2 = _kprep_fwd(pkv, mla_kv_norm_g, gqa_k_norm_g, cos_a, ss_a, cos_b, ss_b, tr)
    kv_up = _mm(ckv_n, wkv_t, "NT", BF16, "kv_up")
    cq_n, qb2 = _qprep_fwd(pq, mla_q_norm_g, gqa_q_norm_g, lcos_b, lss_b, tr)
    q_a = _mm(cq_n, wq_t, "NT", F32, "q_up")
    qar = _qrope_fwd(q_a, lcos_a, lss_a, tr)

    a_q = [(qar, lambda h: 3 * (h // 2) + h % 2), (qar, lambda h: 3 * (h // 2) + 2)]
    a_k = [(kv_up, lambda h: 2 * h), (kpe2, lambda h: h % 2)]
    a_v = (kv_up, lambda h: 2 * h + 1)
    a_scale = float(MLA_NOPE + MLA_ROPE) ** -0.5
    b_q = [(qb2, lambda h: h)]
    b_k = [(kb2, lambda h: h)]
    b_v = (vb2, lambda h: h)
    b_scale = float(GQA_HEAD_DIM) ** -0.5
    tq_f = _pick(T, 2048)
    o_a, lse_a = _attn_fwd(a_q, a_k, a_v, MLA_HEADS, 1, MLA_V, a_scale, "attn_a_fwd", tq_f)
    br_landed = _gather_land(ag_br, o_a, "br")
    o_b, lse_b = _attn_fwd(b_q, b_k, b_v, GQA_HEADS, GQA_GROUP, GQA_HEAD_DIM, b_scale, "attn_b_fwd", tq_f, after=br_landed[1][4])
    wbra_t, wbrb_t, wout = _gather_done(br_landed, o_b, "br")
    up_landed = _gather_land(ag_up, o_b, "up")
    ya = _mm(o_a, wbra_t, "NT", BF16, "br_a", after=up_landed[1][4])
    yb = _mm(o_b, wbrb_t, "NT", BF16, "br_b")
    merged = _gates_fwd(pg, ya, yb, tr)
    att = _mm(merged, wout, "NN", F32, "out_proj")
    x1, z2 = _resid_norm2_fwd(x2d, att, g1, norm2_g, sh2, sc2, tr)
    (wup_t,) = _gather_done(up_landed, z2, "up")
    down_landed = _gather_land(ag_down, z2, "down")
    tc = _pick(FF, 128)
    u_a, u_b, hg = _ffn_up_conv(z2, wup_t, cw, conv_b, tc, down_landed[1][4])
    (wdown,) = _gather_done(down_landed, hg, "down")
    f = _mm(hg, wdown, "NN", F32, "ffn_down", tk=FF // 2)
    sq, dx2, d_fg, d_g2, df = _loss_head(x1, f, g2, fg, tgt, tr)
    loss = lax.psum(0.5 * jnp.sum(sq) / D, ("x", "y", "c"))

    du_a, du_b, dcw_a, dcw_b, dcb_a, dcb_b = _ffn_down_dx_conv_bwd(df, wdown, u_a, u_b, cw, conv_b, _pick(FF, 256), loss.reshape(1, 1))
    g_wdown = _mm(hg, df, "TN", BF16, "ffn_down_dw", tm=FF // 4)
    dz2 = _mm(du_a, wup_t, "NN", F32, "ffn_up_dx_a", tk=FF // 2)
    dz2 = _mm(du_b, wup_t, "NN", F32, "ffn_up_dx_b", b_off=FF, add=dz2, tk=FF // 2)
    g_wup_t = _mm(du_a, z2, "TN", BF16, "ffn_up_dw_a", out_rows=F2, tm=FF // 4)
    g_wup_t = _mm(du_b, z2, "TN", BF16, "ffn_up_dw_b", out_base=g_wup_t, out_off=FF, tm=FF // 4)
    sw_ffn = _swap_start([g_wdown, g_wup_t], sc2, "ffn")
    sc2 = sc2 + sw_ffn[4]
    dx1, datt, d_n2g, d_sh2, d_sc2, d_g1 = _resid_norm2_bwd(dz2, x1, dx2, att, norm2_g, sc2, g1, tr)

    dmerged = _mm(datt, wout, "NT", BF16, "out_proj_dx")
    rs_ffn = _scatter_start_after_swap(sw_ffn, dmerged, "ffn")
    lse_a = lse_a + rs_ffn[4]
    g_wout = _mm(merged, datt, "TN", BF16, "out_proj_dw")
    dya, dyb, dpg = _gates_bwd(dmerged, pg, ya, yb, tr)
    do_a = _mm(dya, wbra_t, "NN", BF16, "br_a_dx")
    g_wbra_t = _mm(dya, o_a, "TN", BF16, "br_a_dw")
    do_b = _mm(dyb, wbrb_t, "NN", BF16, "br_b_dx")
    g_wbrb_t = _mm(dyb, o_b, "TN", BF16, "br_b_dw")
    dqa2, dka2, dva2 = _attn_bwd(a_q, a_k, a_v, o_a, do_a, lse_a, MLA_HEADS, 1, MLA_V, a_scale, "attn_a_bwd", tq_f)
    dqb2, dkb2, dvb2 = _attn_bwd(b_q, b_k, b_v, o_b, do_b, lse_b, GQA_HEADS, GQA_GROUP, GQA_HEAD_DIM, b_scale, "attn_b_bwd", tq_f)
    dq_a = _qrope_bwd(dqa2, lcos_a, lss_a, tr)
    dcq_n = _mm(dq_a, wq_t, "NN", F32, "q_up_dx")
    g_wq_t = _mm(dq_a, cq_n, "TN", BF16, "q_up_dw")
    dpq, d_qg, d_gq = _qprep_bwd(pq, dcq_n, dqb2, mla_q_norm_g, gqa_q_norm_g, lcos_b, lss_b, tr)
    dkv_up, dkpe = _kgrad_split(dka2, dva2, cos_a, ss_a, tr)
    dckv_n = _mm(dkv_up, wkv_t, "NN", F32, "kv_up_dx")
    g_wkv_t = _mm(dkv_up, ckv_n, "TN", BF16, "kv_up_dw")
    rs_mix = _scatter_start([g_wq_t, g_wkv_t, g_wbra_t, g_wbrb_t, g_wout], "mix")
    dpkv, d_kvg, d_kg = _kprep_bwd(pkv, dckv_n, dkb2, dvb2, dkpe, mla_kv_norm_g + rs_mix[4], gqa_k_norm_g, cos_b, ss_b, tr)
    dz_kv = _mm(dpkv, w_kvp, "NN", F32, "proj_kv_dx")
    dz_lat = _mm(dpq, win_t, "NN", F32, "proj_q_dx", b_off=kv_cols, add=dz_kv)
    dz_lat = _mm(dpg, win_t, "NN", F32, "proj_g_dx", b_off=kv_cols + QC, add=dz_lat)
    _, d_n1g_c, d_csh, d_csc = _norm_mod_bwd(dz_kv, T // tr, ctx2d, norm1_g, csc, None, "norm1_ctx_bwd", tr)
    grad_x, d_n1g_l, d_sh1, d_sc1 = _norm_mod_bwd(dz_lat, 0, x2d, norm1_g, sc1, dx1, "norm1_lat_bwd", tr)

    zeros_d = jnp.zeros((1, D), F32)
    d_lat = jnp.concatenate([d_sh1, d_sc1, d_g1, d_sh2, d_sc2, d_g2], axis=1)
    d_ctx_part = jnp.concatenate([d_csh, d_csc], axis=1)
    flat = jnp.concatenate(
        [d_n1g_c + d_n1g_l, d_qg, d_kvg, d_gq, d_kg, d_n2g, dcb_a, dcb_b, d_fg,
         dcw_a.reshape(1, -1), dcw_b.reshape(1, -1), d_ctx_part, d_lat], axis=1)
    n_flat = flat.shape[1]
    n_rows = -(-n_flat // (8 * LANES)) * 8
    flat = jnp.pad(flat, ((0, 0), (0, n_rows * LANES - n_flat))).reshape(n_rows, LANES)
    got = _all_gather_small(flat, "gather_small_grads")
    tot = _sum_slots(got, "sum_small_grads").reshape(1, -1)
    sizes = [D, MLA_Q_LORA, MLA_KV_LORA, GQA_HEAD_DIM, GQA_HEAD_DIM, D, F2, D, 3 * FF, 3 * FF, 2 * D]
    offs = [0]
    for s in sizes:
        offs.append(offs[-1] + s)
    t_n1g, t_qg, t_kvg, t_gq, t_kg, t_n2g, t_cb, t_fg, t_cwa, t_cwb, t_ctx = [tot[:, offs[k] : offs[k + 1]] for k in range(len(sizes))]
    g_cw_full = jnp.concatenate([t_cwa.reshape(3, FF), t_cwb.reshape(3, FF)], axis=1)
    g_cw = lax.dynamic_slice(g_cw_full, (0, j * NW), (3, NW))
    d_lat_all = got.reshape(8, -1)[:, offs[-1] : offs[-1] + 6 * D]
    g16 = jnp.concatenate([d_lat_all, jnp.pad(t_ctx, ((0, 0), (0, 4 * D))), jnp.zeros((7, 6 * D), F32)], axis=0)
    g_b_ada = _sum_slots(g16.reshape(16, 1, 6 * D), "sum_b_ada")
    g16_cols = lax.dynamic_slice(g16, (0, j * NA), (16, NA))
    ds_part = _mm(g16_cols, w_ada[0], "NT", F32, "ada_dx")
    got = _all_gather_small(ds_part[8:16], "gather_ada_dx")
    ds_ctx = _sum_slots(jnp.stack([got[2 * s] for s in range(4)]), "sum_ada_dx")[0:1]
    g_c_ctx = _silu_grad_mul(ds_ctx, cc)

    g_kvp = _mm(dpkv, z_all, "TN", BF16, "proj_kv_dw")
    nk = MLA_KV_LORA + 2 * GQA_KV_HEADS * GQA_HEAD_DIM
    g_kv = jnp.concatenate([g_kvp[:MLA_KV_LORA], g_kvp[nk : nk + MLA_ROPE], g_kvp[MLA_KV_LORA:nk]], axis=0)
    g_win_t = _mm(dpq, z_all, "TN", BF16, "proj_q_dw", out_rows=kv_cols + QC + 2 * D, out_off=kv_cols, tm=QC // 2)
    g_win_t = _mm(dpg, z_all, "TN", BF16, "proj_g_dw", out_base=g_win_t, out_off=kv_cols + QC)
    g_win_t = lax.dynamic_update_slice(g_win_t, g_kv, (0, 0))
    sw_in = _swap_start([g_win_t], got, "in")

    h_ffn = _scatter_sums(rs_ffn, sw_in[2][0], "ffn")
    j_ffn = _join_start(h_ffn, grad_x, "ffn")
    h_mix = _scatter_sums(rs_mix, j_ffn[2][0], "mix")
    j_mix = _join_start(h_mix, j_ffn[2][0], "mix")
    rs_in = _scatter_start_after_swap(sw_in, j_mix[2][0], "in")
    g_w_ada = _mm(s16, g16_cols, "TN", F32, "ada_dw", act="silu", after=rs_in[4])
    _, d_ada, m_ada, v_ada = _adamw(w_ada[0], g_w_ada, m_w_ada[0], v_w_ada[0], "adamw_w_ada")
    r_wdown, r_wup = _join_wait(j_ffn, d_ada, "ffn")
    r_wq, r_wkv, r_wbra, r_wbrb, r_wout = _join_wait(j_mix, d_ada, "mix")
    gq_p = _joined(*r_wq).T
    gq = jnp.concatenate([gq_p[:, : 2 * MLA_NOPE].reshape(MLA_Q_LORA, 2, MLA_NOPE), gq_p[:, 2 * MLA_NOPE :].reshape(MLA_Q_LORA, 2, MLA_ROPE)], axis=2)
    grads = {
        "c_ctx": g_c_ctx.reshape(D), "w_ada": g_w_ada[None], "b_ada": g_b_ada, "norm1_g": t_n1g,
        "mla_q_norm_g": t_qg, "w_q_up": gq.reshape(1, MLA_Q_LORA, -1), "mla_kv_norm_g": t_kvg, "w_kv_up": r_wkv,
        "gqa_q_norm_g": t_gq, "gqa_k_norm_g": t_kg, "w_br_a": r_wbra, "w_br_b": r_wbrb, "w_out": r_wout,
        "norm2_g": t_n2g, "w_up": r_wup, "conv_w": g_cw[None], "conv_b": t_cb, "w_down": r_wdown,
        "final_norm_g": t_fg.reshape(D),
    }
    arrives_transposed = ("w_kv_up", "w_br_a", "w_br_b", "w_up")
    arrives_halved = arrives_transposed + ("w_out", "w_down")
    weights = dict(c_ctx=c_ctx, w_ada=w_ada, b_ada=b_ada, norm1_g=norm1_g, w_in=w_in, mla_q_norm_g=mla_q_norm_g, w_q_up=w_q_up,
                   mla_kv_norm_g=mla_kv_norm_g, w_kv_up=w_kv_up, gqa_q_norm_g=gqa_q_norm_g, gqa_k_norm_g=gqa_k_norm_g, w_br_a=w_br_a,
                   w_br_b=w_br_b, w_out=w_out, norm2_g=norm2_g, w_up=w_up, conv_w=conv_w, conv_b=conv_b, w_down=w_down,
                   final_norm_g=final_norm_g)
    m_in = dict(c_ctx=m_c_ctx, w_ada=m_w_ada, b_ada=m_b_ada, norm1_g=m_norm1_g, w_in=m_w_in, mla_q_norm_g=m_mla_q_norm_g,
                w_q_up=m_w_q_up, mla_kv_norm_g=m_mla_kv_norm_g, w_kv_up=m_w_kv_up, gqa_q_norm_g=m_gqa_q_norm_g,
                gqa_k_norm_g=m_gqa_k_norm_g, w_br_a=m_w_br_a, w_br_b=m_w_br_b, w_out=m_w_out, norm2_g=m_norm2_g, w_up=m_w_up,
                conv_w=m_conv_w, conv_b=m_conv_b, w_down=m_w_down, final_norm_g=m_final_norm_g)
    v_in = dict(c_ctx=v_c_ctx, w_ada=v_w_ada, b_ada=v_b_ada, norm1_g=v_norm1_g, w_in=v_w_in, mla_q_norm_g=v_mla_q_norm_g,
                w_q_up=v_w_q_up, mla_kv_norm_g=v_mla_kv_norm_g, w_kv_up=v_w_kv_up, gqa_q_norm_g=v_gqa_q_norm_g,
                gqa_k_norm_g=v_gqa_k_norm_g, w_br_a=v_w_br_a, w_br_b=v_w_br_b, w_out=v_w_out, norm2_g=v_norm2_g, w_up=v_w_up,
                conv_w=v_conv_w, conv_b=v_conv_b, w_down=v_w_down, final_norm_g=v_final_norm_g)
    names = list(weights)
    big = [n for n in names if weights[n].ndim == 3 and weights[n].shape[1] >= 8]
    small = [n for n in names if n not in big]
    delta, new_m, new_v = {}, {}, {}

    def update(n):
        shp = weights[n].shape
        two_d = lambda a: a.reshape(shp[1], shp[2])
        g_t = n in arrives_transposed
        if n in arrives_halved:
            g_in, g_sib = grads[n]
        else:
            g_in, g_sib = two_d(grads[n].astype(F32)), None
        g_, d_, m_, v_ = _adamw(two_d(weights[n]), g_in, two_d(m_in[n]), two_d(v_in[n]), "adamw_" + n, g_transposed=g_t, g_sibling=g_sib)
        grads[n], delta[n], new_m[n], new_v[n] = g_.reshape(shp), d_.reshape(shp), m_.reshape(shp), v_.reshape(shp)

    delta["w_ada"], new_m["w_ada"], new_v["w_ada"] = d_ada[None], m_ada[None], v_ada[None]
    early = [n for n in big if n not in ("w_in", "w_ada")]
    for n in early[:-1]:
        update(n)
    done = sum(delta[n][0, 0:1, 0:1] for n in early[:-1])
    j_in = _join_start(_scatter_sums(rs_in, done, "in"), done, "in")
    last = early[-1]
    grads[last] = (grads[last][0] + j_in[4], grads[last][1])
    update(last)
    ((g_mine, g_sib),) = _join_wait(j_in, delta[last], "in")
    g_, d_, m_, v_ = _adamw(w_in[0].T, g_mine, m_w_in[0].T, v_w_in[0].T, "adamw_w_in", g_sibling=g_sib)
    grads["w_in"], delta["w_in"], new_m["w_in"], new_v["w_in"] = g_.T[None], d_.T[None], m_.T[None], v_.T[None]
    grads = {n: grads[n].reshape(weights[n].shape).astype(F32) for n in names}

    slab = lambda tree: [tree[n].reshape(-1, LANES) for n in small]
    d_, m_, v_ = _adamw_many(slab(weights), slab(grads), slab(m_in), slab(v_in), "adamw_small")
    for k, n in enumerate(small):
        shp = weights[n].shape
        delta[n], new_m[n], new_v[n] = d_[k].reshape(shp), m_[k].reshape(shp), v_[k].reshape(shp)

    return (loss, grad_x[None], *[grads[n] for n in names], *[delta[n] for n in names], *[new_m[n] for n in names],
            *[new_v[n] for n in names])
```

```python
import math

import jax
import jax.numpy as jnp
from jax import lax
from jax.experimental import pallas as pl
from jax.experimental.pallas import tpu as pltpu

F32 = jnp.float32
BF16 = jnp.bfloat16
MESH = pl.DeviceIdType.MESH

NORM_EPS = 1e-6
ROPE_THETA = 10000.0
GRID_W = 64
MLA_HEADS = 8
MLA_Q_LORA = 768
MLA_KV_LORA = 512
MLA_NOPE = 128
MLA_ROPE = 64
MLA_V = 128
GQA_HEADS = 8
GQA_KV_HEADS = 2
GQA_HEAD_DIM = 128
GQA_GROUP = GQA_HEADS // GQA_KV_HEADS
LANES = 128
KVP = MLA_KV_LORA + 2 * GQA_KV_HEADS * GQA_HEAD_DIM + LANES
QC = MLA_Q_LORA + GQA_HEADS * GQA_HEAD_DIM

ADAM_LR = 0.001
ADAM_B1 = 0.9
ADAM_B2 = 0.999
ADAM_EPS = 1e-08
ADAM_WD = 0.01
ADAM_STEP = 10

VMEM_LIMIT = 56 * 1024 * 1024


def _pick(dim, target, mult=LANES):
    t = (min(target, dim) // mult) * mult
    while t >= mult:
        if dim % t == 0:
            return t
        t -= mult
    return dim


def _params(sem):
    return pltpu.CompilerParams(dimension_semantics=sem, vmem_limit_bytes=VMEM_LIMIT)


_DIMS = {"NN": (((1,), (0,)), ((), ())), "NT": (((1,), (1,)), ((), ())), "TN": (((0,), (0,)), ((), ()))}


MM_VMEM_BUDGET = 36 * 1024 * 1024


def _mm_tiles(M, N, K, sa, sb, so, tm, tn, tk):
    tm, tn, tk = _pick(M, tm), _pick(N, tn), _pick(K, tk)

    def need(t):
        return 2 * (tm * t * sa + t * tn * sb) + 2 * tm * tn * so + (tm * tn * 4 if t < K else 0)

    while need(tk) > MM_VMEM_BUDGET and tk > LANES:
        smaller = _pick(K, tk - LANES)
        if smaller >= tk:
            break
        tk = smaller
    return tm, tn, tk


def _window(block, index, offsets):
    if not any(offsets):
        return pl.BlockSpec(block, index)
    for t, o in zip(block, offsets):
        assert o % 16 == 0 and t % 16 == 0, (block, offsets)

    def at(i, j, k):
        return tuple(pl.multiple_of(o + p * t, math.gcd(o, t)) for p, t, o in zip(index(i, j, k), block, offsets))

    return pl.BlockSpec(tuple(pl.Element(t) for t in block), at)


def _mm(a, b, mode, out_dtype, name, m=None, n=None, k=None, b_off=0, add=None, out_rows=None, out_base=None, out_off=0,
        tm=1024, tn=1024, tk=2304, act=None, bias=None, after=None):
    if mode == "NN":
        M, K, N = m or a.shape[0], k or a.shape[1], b.shape[1]
    elif mode == "NT":
        M, K, N = m or a.shape[0], a.shape[1], n or b.shape[0]
    else:
        M, K, N = a.shape[1], k or a.shape[0], b.shape[1]
    tm, tn, tk = _mm_tiles(M, N, K, a.dtype.itemsize, b.dtype.itemsize, jnp.dtype(out_dtype).itemsize, tm, tn, tk)
    nk = K // tk
    dims = _DIMS[mode]
    n_in = 2 + (bias is not None) + (add is not None) + (out_base is not None) + (after is not None)

    def body(*refs):
        a_ref, b_ref = refs[:2]
        bias_ref = refs[2] if bias is not None else None
        add_ref = refs[2 + (bias is not None)] if add is not None else None
        o_ref = refs[n_in]
        av = a_ref[...]
        if act == "silu":
            av = av * jax.nn.sigmoid(av)
        part = lax.dot_general(av.astype(BF16), b_ref[...].astype(BF16), dims, preferred_element_type=F32)

        def finish(r):
            if bias is not None:
                r = r + bias_ref[...]
            if add is not None:
                r = r + add_ref[...]
            o_ref[...] = r.astype(out_dtype)

        if nk == 1:
            finish(part)
            return
        acc = refs[-1]
        k = pl.program_id(2)

        @pl.when(k == 0)
        def _():
            acc[...] = part

        @pl.when(jnp.logical_and(k > 0, k < nk - 1))
        def _():
            acc[...] += part

        @pl.when(k == nk - 1)
        def _():
            finish(acc[...] + part)

    a_spec = pl.BlockSpec((tk, tm), lambda i, j, k: (k, i)) if mode == "TN" else pl.BlockSpec((tm, tk), lambda i, j, k: (i, k))
    if mode == "NT":
        b_spec = _window((tn, tk), lambda i, j, k: (j, k), (b_off, 0))
    else:
        b_spec = _window((tk, tn), lambda i, j, k: (k, j), (b_off, 0))
    in_specs, args = [a_spec, b_spec], [a, b]
    if bias is not None:
        in_specs.append(pl.BlockSpec((1, tn), lambda i, j, k: (0, j)))
        args.append(bias)
    if add is not None:
        in_specs.append(pl.BlockSpec((tm, tn), lambda i, j, k: (i, j)))
        args.append(add)
    aliases = {}
    if after is not None:
        in_specs.append(pl.BlockSpec(after.shape, lambda i, j, k: (0, 0)))
        args.append(after)
    if out_base is not None:
        aliases = {len(args): 0}
        in_specs.append(ANY)
        args.append(out_base)
        out_rows = out_base.shape[0]
    return pl.pallas_call(
        body,
        name=name,
        grid=(M // tm, N // tn, nk),
        in_specs=in_specs,
        out_specs=_window((tm, tn), lambda i, j, k: (i, j), (out_off, 0)),
        out_shape=jax.ShapeDtypeStruct((out_rows or M, N), out_dtype),
        input_output_aliases=aliases,
        scratch_shapes=[pltpu.VMEM((tm, tn), F32)] if nk > 1 else [],
        compiler_params=_params(("parallel", "parallel", "arbitrary")),
    )(*args)


def _rms(x):
    r = lax.rsqrt(jnp.mean(x * x, axis=-1, keepdims=True) + NORM_EPS)
    return x * r, r


def _rms_bwd(xh, r, dxh):
    return r * (dxh - xh * jnp.mean(dxh * xh, axis=-1, keepdims=True))


def _swap(x, q):
    lane = lax.broadcasted_iota(jnp.int32, x.shape, 1)
    even = ((lane // q) % 2) == 0
    return jnp.where(even, pltpu.roll(x, LANES - q, 1), pltpu.roll(x, q, 1))


def _rope(x, cos, ss, q):
    return x * cos + _swap(x, q) * ss


def _rope_t(d, cos, ss, q):
    return d * cos + _swap(d * ss, q)


def _csum(x):
    return jnp.sum(x, axis=0, keepdims=True)


def _rows(tr, w, off=0):
    return pl.BlockSpec((tr, w), lambda i: (i + off, 0))


def _bcast(w):
    return pl.BlockSpec((1, w), lambda i: (0, 0))


def _acc_init(i, refs):
    @pl.when(i == 0)
    def _():
        for r in refs:
            r[...] = jnp.zeros_like(r)


def _rope_tables(n_ctx, n_lat, rot_dim):
    rows = n_lat // GRID_W
    row = jnp.repeat(jnp.arange(rows, dtype=F32), GRID_W)
    col = jnp.tile(jnp.arange(GRID_W, dtype=F32), rows)
    half = rot_dim // 2
    inv_freq = ROPE_THETA ** (-jnp.arange(0, half, 2, dtype=F32) / half)
    ar, ac = row[:, None] * inv_freq, col[:, None] * inv_freq
    cos = jnp.concatenate([jnp.cos(ar), jnp.cos(ar), jnp.cos(ac), jnp.cos(ac)], axis=-1)
    ss = jnp.concatenate([-jnp.sin(ar), jnp.sin(ar), -jnp.sin(ac), jnp.sin(ac)], axis=-1)
    cos = jnp.tile(cos, (1, LANES // rot_dim))
    ss = jnp.tile(ss, (1, LANES // rot_dim))
    cos = jnp.concatenate([cos, jnp.ones((n_ctx, LANES), F32)], axis=0)
    ss = jnp.concatenate([ss, jnp.zeros((n_ctx, LANES), F32)], axis=0)
    return cos, ss


def _norm_mod_fwd(x2d, g, sh, sc, name, tr, out_rows=None, base=None, out_off=0):
    n, d = x2d.shape

    def body(x_ref, g_ref, sh_ref, sc_ref, *rest):
        xh, _ = _rms(x_ref[...])
        rest[-1][...] = ((xh * g_ref[...]) * (1.0 + sc_ref[...]) + sh_ref[...]).astype(BF16)

    args, in_specs, aliases = [x2d, g, sh, sc], [_rows(tr, d), _bcast(d), _bcast(d), _bcast(d)], {}
    if base is not None:
        args.append(base)
        in_specs.append(ANY)
        aliases = {4: 0}
        out_rows = base.shape[0]
    return pl.pallas_call(
        body,
        name=name,
        grid=(n // tr,),
        in_specs=in_specs,
        out_specs=_rows(tr, d, out_off // tr),
        out_shape=jax.ShapeDtypeStruct((out_rows or n, d), BF16),
        input_output_aliases=aliases,
        compiler_params=_params(("parallel",)),
    )(*args)


def _norm_mod_bwd(dz, dz_off, x2d, g, sc, dres, name, tr):
    n, d = x2d.shape
    want_dx = dres is not None

    def body(*refs):
        if want_dx:
            dz_ref, x_ref, g_ref, sc_ref, dres_ref, dx_ref, dg_ref, dsh_ref, dsc_ref = refs
        else:
            dz_ref, x_ref, g_ref, sc_ref, dg_ref, dsh_ref, dsc_ref = refs
        _acc_init(pl.program_id(0), [dg_ref, dsh_ref, dsc_ref])
        xh, r = _rms(x_ref[...])
        dzv = dz_ref[...]
        gv = g_ref[...]
        dsc_ref[...] += _csum(dzv * (xh * gv))
        dsh_ref[...] += _csum(dzv)
        dh = dzv * (1.0 + sc_ref[...])
        dg_ref[...] += _csum(dh * xh)
        if want_dx:
            dx_ref[...] = _rms_bwd(xh, r, dh * gv) + dres_ref[...]

    in_specs = [_rows(tr, d, dz_off), _rows(tr, d), _bcast(d), _bcast(d)]
    args = [dz, x2d, g, sc]
    out_specs = [_bcast(d)] * 3
    out_shape = [jax.ShapeDtypeStruct((1, d), F32)] * 3
    if want_dx:
        in_specs.append(_rows(tr, d))
        args.append(dres)
        out_specs = [_rows(tr, d)] + out_specs
        out_shape = [jax.ShapeDtypeStruct((n, d), F32)] + out_shape
    res = pl.pallas_call(
        body,
        name=name,
        grid=(n // tr,),
        in_specs=in_specs,
        out_specs=out_specs,
        out_shape=out_shape,
        compiler_params=_params(("arbitrary",)),
    )(*args)
    return res if want_dx else (None, *res)


_QA, _QB = MLA_ROPE // 4, GQA_HEAD_DIM // 4


def _kprep_fwd(pkv, kvg, kg, cos_a, ss_a, cos_b, ss_b, tr):
    n = pkv.shape[0]
    nb = GQA_KV_HEADS * GQA_HEAD_DIM

    def body(p_ref, kvg_ref, kg_ref, ca, sa, cb, sb, ckv_ref, kb_ref, vb_ref, kpe_ref):
        p = p_ref[...]
        xh, _ = _rms(p[:, :MLA_KV_LORA])
        ckv_ref[...] = (xh * kvg_ref[...]).astype(BF16)
        for e in range(GQA_KV_HEADS):
            lo = MLA_KV_LORA + e * GQA_HEAD_DIM
            kh, _ = _rms(p[:, lo : lo + GQA_HEAD_DIM])
            kb_ref[:, e * GQA_HEAD_DIM : (e + 1) * GQA_HEAD_DIM] = _rope(kh * kg_ref[...], cb[...], sb[...], _QB).astype(BF16)
        vb_ref[...] = p[:, MLA_KV_LORA + nb : MLA_KV_LORA + 2 * nb].astype(BF16)
        kr = _rope(p[:, MLA_KV_LORA + 2 * nb :], ca[...], sa[...], _QA)
        kpe_ref[:, :LANES] = kr.astype(BF16)
        kpe_ref[:, LANES:] = pltpu.roll(kr, MLA_ROPE, 1).astype(BF16)

    return pl.pallas_call(
        body,
        name="kprep_fwd",
        grid=(n // tr,),
        in_specs=[_rows(tr, KVP), _bcast(MLA_KV_LORA), _bcast(GQA_HEAD_DIM)] + [_rows(tr, LANES)] * 4,
        out_specs=[_rows(tr, MLA_KV_LORA), _rows(tr, nb), _rows(tr, nb), _rows(tr, 2 * LANES)],
        out_shape=[jax.ShapeDtypeStruct((n, w), BF16) for w in (MLA_KV_LORA, nb, nb, 2 * LANES)],
        compiler_params=_params(("parallel",)),
    )(pkv, kvg, kg, cos_a, ss_a, cos_b, ss_b)


def _kprep_bwd(pkv, dckv, dkb, dvb, dkpe, kvg, kg, cos_b, ss_b, tr):
    n = pkv.shape[0]
    nb = GQA_KV_HEADS * GQA_HEAD_DIM

    def body(p_ref, dckv_ref, dkb_ref, dvb_ref, dkpe_ref, kvg_ref, kg_ref, cb, sb, dp_ref, dkvg_ref, dkg_ref):
        _acc_init(pl.program_id(0), [dkvg_ref, dkg_ref])
        p = p_ref[...]
        xh, r = _rms(p[:, :MLA_KV_LORA])
        dn = dckv_ref[...]
        dkvg_ref[...] += _csum(dn * xh)
        dp_ref[:, :MLA_KV_LORA] = _rms_bwd(xh, r, dn * kvg_ref[...]).astype(BF16)
        for e in range(GQA_KV_HEADS):
            lo = MLA_KV_LORA + e * GQA_HEAD_DIM
            kh, rk = _rms(p[:, lo : lo + GQA_HEAD_DIM])
            dk = _rope_t(dkb_ref[:, e * GQA_HEAD_DIM : (e + 1) * GQA_HEAD_DIM], cb[...], sb[...], _QB)
            dkg_ref[...] += _csum(dk * kh)
            dp_ref[:, lo : lo + GQA_HEAD_DIM] = _rms_bwd(kh, rk, dk * kg_ref[...]).astype(BF16)
        dp_ref[:, MLA_KV_LORA + nb : MLA_KV_LORA + 2 * nb] = dvb_ref[...].astype(BF16)
        dp_ref[:, MLA_KV_LORA + 2 * nb :] = dkpe_ref[...].astype(BF16)

    return pl.pallas_call(
        body,
        name="kprep_bwd",
        grid=(n // tr,),
        in_specs=[_rows(tr, KVP), _rows(tr, MLA_KV_LORA), _rows(tr, nb), _rows(tr, nb), _rows(tr, LANES),
                  _bcast(MLA_KV_LORA), _bcast(GQA_HEAD_DIM), _rows(tr, LANES), _rows(tr, LANES)],
        out_specs=[_rows(tr, KVP), _bcast(MLA_KV_LORA), _bcast(GQA_HEAD_DIM)],
        out_shape=[jax.ShapeDtypeStruct((n, KVP), BF16), jax.ShapeDtypeStruct((1, MLA_KV_LORA), F32),
                   jax.ShapeDtypeStruct((1, GQA_HEAD_DIM), F32)],
        compiler_params=_params(("arbitrary",)),
    )(pkv, dckv, dkb, dvb, dkpe, kvg, kg, cos_b, ss_b)


def _kgrad_split(dka, dva, cos_a, ss_a, tr):
    n = dka.shape[0]
    wk = MLA_HEADS * 2 * LANES

    def body(dk_ref, dv_ref, ca, sa, dkv_ref, dkpe_ref):
        even = jnp.zeros((tr, LANES), F32)
        odd = jnp.zeros((tr, LANES), F32)
        for h in range(MLA_HEADS):
            dkv_ref[:, 2 * h * LANES : (2 * h + 1) * LANES] = dk_ref[:, 2 * h * LANES : (2 * h + 1) * LANES].astype(BF16)
            dkv_ref[:, (2 * h + 1) * LANES : (2 * h + 2) * LANES] = dv_ref[:, h * MLA_V : (h + 1) * MLA_V].astype(BF16)
            part = dk_ref[:, (2 * h + 1) * LANES : (2 * h + 2) * LANES]
            if h % 2 == 0:
                even = even + part
            else:
                odd = odd + part
        lane = lax.broadcasted_iota(jnp.int32, (tr, LANES), 1)
        low = lane < MLA_ROPE
        both = jnp.where(low, even, odd)
        tot = jnp.where(low, both + pltpu.roll(both, MLA_ROPE, 1), 0.0)
        dkpe_ref[...] = _rope_t(tot, ca[...], sa[...], _QA)

    return pl.pallas_call(
        body,
        name="kgrad_split",
        grid=(n // tr,),
        in_specs=[_rows(tr, wk), _rows(tr, MLA_HEADS * MLA_V), _rows(tr, LANES), _rows(tr, LANES)],
        out_specs=[_rows(tr, wk), _rows(tr, LANES)],
        out_shape=[jax.ShapeDtypeStruct((n, wk), BF16), jax.ShapeDtypeStruct((n, LANES), F32)],
        compiler_params=_params(("parallel",)),
    )(dka, dva, cos_a, ss_a)


def _qprep_fwd(pq, qg, gq, cos_b, ss_b, tr):
    n = pq.shape[0]
    nq = GQA_HEADS * GQA_HEAD_DIM

    def body(p_ref, qg_ref, gq_ref, cb, sb, cq_ref, qb_ref):
        xh, _ = _rms(p_ref[:, :MLA_Q_LORA])
        cq_ref[...] = (xh * qg_ref[...]).astype(BF16)
        for h in range(GQA_HEADS):
            lo = MLA_Q_LORA + h * GQA_HEAD_DIM
            qh, _ = _rms(p_ref[:, lo : lo + GQA_HEAD_DIM])
            qb_ref[:, h * GQA_HEAD_DIM : (h + 1) * GQA_HEAD_DIM] = _rope(qh * gq_ref[...], cb[...], sb[...], _QB).astype(BF16)

    return pl.pallas_call(
        body,
        name="qprep_fwd",
        grid=(n // tr,),
        in_specs=[_rows(tr, QC), _bcast(MLA_Q_LORA), _bcast(GQA_HEAD_DIM), _rows(tr, LANES), _rows(tr, LANES)],
        out_specs=[_rows(tr, MLA_Q_LORA), _rows(tr, nq)],
        out_shape=[jax.ShapeDtypeStruct((n, MLA_Q_LORA), BF16), jax.ShapeDtypeStruct((n, nq), BF16)],
        compiler_params=_params(("parallel",)),
    )(pq, qg, gq, cos_b, ss_b)


def _qprep_bwd(pq, dcq, dqb, qg, gq, cos_b, ss_b, tr):
    n = pq.shape[0]
    nq = GQA_HEADS * GQA_HEAD_DIM

    def body(p_ref, dcq_ref, dqb_ref, qg_ref, gq_ref, cb, sb, dp_ref, dqg_ref, dgq_ref):
        _acc_init(pl.program_id(0), [dqg_ref, dgq_ref])
        xh, r = _rms(p_ref[:, :MLA_Q_LORA])
        dn = dcq_ref[...]
        dqg_ref[...] += _csum(dn * xh)
        dp_ref[:, :MLA_Q_LORA] = _rms_bwd(xh, r, dn * qg_ref[...]).astype(BF16)
        for h in range(GQA_HEADS):
            lo = MLA_Q_LORA + h * GQA_HEAD_DIM
            qh, rq = _rms(p_ref[:, lo : lo + GQA_HEAD_DIM])
            dq = _rope_t(dqb_ref[:, h * GQA_HEAD_DIM : (h + 1) * GQA_HEAD_DIM], cb[...], sb[...], _QB)
            dgq_ref[...] += _csum(dq * qh)
            dp_ref[:, lo : lo + GQA_HEAD_DIM] = _rms_bwd(qh, rq, dq * gq_ref[...]).astype(BF16)

    return pl.pallas_call(
        body,
        name="qprep_bwd",
        grid=(n // tr,),
        in_specs=[_rows(tr, QC), _rows(tr, MLA_Q_LORA), _rows(tr, nq), _bcast(MLA_Q_LORA), _bcast(GQA_HEAD_DIM),
                  _rows(tr, LANES), _rows(tr, LANES)],
        out_specs=[_rows(tr, QC), _bcast(MLA_Q_LORA), _bcast(GQA_HEAD_DIM)],
        out_shape=[jax.ShapeDtypeStruct((n, QC), BF16), jax.ShapeDtypeStruct((1, MLA_Q_LORA), F32),
                   jax.ShapeDtypeStruct((1, GQA_HEAD_DIM), F32)],
        compiler_params=_params(("arbitrary",)),
    )(pq, dcq, dqb, qg, gq, cos_b, ss_b)


_QA_COLS = MLA_HEADS * (MLA_NOPE + MLA_ROPE)


def _qrope_fwd(qa, cos_a, ss_a, tr):
    n = qa.shape[0]

    def body(q_ref, ca, sa, o_ref):
        for j in range(MLA_HEADS // 2):
            lo = 3 * j * LANES
            o_ref[:, lo : lo + 2 * LANES] = q_ref[:, lo : lo + 2 * LANES].astype(BF16)
            o_ref[:, lo + 2 * LANES : lo + 3 * LANES] = _rope(q_ref[:, lo + 2 * LANES : lo + 3 * LANES], ca[...], sa[...], _QA).astype(BF16)

    return pl.pallas_call(
        body,
        name="qrope_fwd",
        grid=(n // tr,),
        in_specs=[_rows(tr, _QA_COLS), _rows(tr, LANES), _rows(tr, LANES)],
        out_specs=_rows(tr, _QA_COLS),
        out_shape=jax.ShapeDtypeStruct((n, _QA_COLS), BF16),
        compiler_params=_params(("parallel",)),
    )(qa, cos_a, ss_a)


def _qrope_bwd(dq2, cos_a, ss_a, tr):
    n = dq2.shape[0]

    def body(d_ref, ca, sa, o_ref):
        for j in range(MLA_HEADS // 2):
            lo = 3 * j * LANES
            h0, h1 = 2 * j, 2 * j + 1
            o_ref[:, lo : lo + LANES] = d_ref[:, 2 * h0 * LANES : (2 * h0 + 1) * LANES].astype(BF16)
            o_ref[:, lo + LANES : lo + 2 * LANES] = d_ref[:, 2 * h1 * LANES : (2 * h1 + 1) * LANES].astype(BF16)
            pe = d_ref[:, (2 * h0 + 1) * LANES : (2 * h0 + 2) * LANES] + d_ref[:, (2 * h1 + 1) * LANES : (2 * h1 + 2) * LANES]
            o_ref[:, lo + 2 * LANES : lo + 3 * LANES] = _rope_t(pe, ca[...], sa[...], _QA).astype(BF16)

    return pl.pallas_call(
        body,
        name="qrope_bwd",
        grid=(n // tr,),
        in_specs=[_rows(tr, MLA_HEADS * 2 * LANES), _rows(tr, LANES), _rows(tr, LANES)],
        out_specs=_rows(tr, _QA_COLS),
        out_shape=jax.ShapeDtypeStruct((n, _QA_COLS), BF16),
        compiler_params=_params(("parallel",)),
    )(dq2, cos_a, ss_a)


def _cat(refs):
    vals = [r[...] for r in refs]
    return vals[0] if len(vals) == 1 else jnp.concatenate(vals, axis=-1)


LOG2E = 1.4426950408889634


def _attn_fwd(qparts, kparts, vpart, n_heads, group, dv, scale, name, tq, after=None):
    T, Tk = qparts[0][0].shape[0], kparts[0][0].shape[0]
    nq_, nk_ = len(qparts), len(kparts)
    sub = min(tq, 256)
    c2 = scale * LOG2E

    def body(*refs):
        q_refs, k_refs = refs[:nq_], refs[nq_ : nq_ + nk_]
        v_ref = refs[nq_ + nk_]
        o_ref, lse_ref = refs[-2:]
        k = _cat(k_refs)
        v = v_ref[...]
        for r0 in range(0, tq, sub):
            q = _cat([r.at[r0 : r0 + sub, :] for r in q_refs])
            s = lax.dot_general(q, k, _DIMS["NT"], preferred_element_type=F32)
            m = jnp.max(s, axis=-1, keepdims=True)
            p = jnp.exp2((s - m) * c2)
            l = jnp.sum(p, axis=-1, keepdims=True)
            acc = jnp.dot(p.astype(BF16), v, preferred_element_type=F32)
            o_ref[r0 : r0 + sub, :] = (acc * (1.0 / l)).astype(BF16)
            lse_ref[r0 : r0 + sub, :] = m * scale + jnp.log(l)

    in_specs = [pl.BlockSpec((tq, LANES), lambda h, i, f=f: (i, f(h))) for _, f in qparts]
    in_specs += [pl.BlockSpec((Tk, LANES), lambda h, i, f=f: (0, f(h // group))) for _, f in kparts]
    fv = vpart[1]
    in_specs.append(pl.BlockSpec((Tk, dv), lambda h, i: (0, fv(h // group))))
    args = [*[a for a, _ in qparts], *[a for a, _ in kparts], vpart[0]]
    if after is not None:
        in_specs.append(pl.BlockSpec(after.shape, lambda h, i: (0, 0)))
        args.append(after)
    return pl.pallas_call(
        body,
        name=name,
        grid=(n_heads, T // tq),
        in_specs=in_specs,
        out_specs=[pl.BlockSpec((tq, dv), lambda h, i: (i, h)), pl.BlockSpec((None, tq, 1), lambda h, i: (h, i, 0))],
        out_shape=[jax.ShapeDtypeStruct((T, n_heads * dv), BF16), jax.ShapeDtypeStruct((n_heads, T, 1), F32)],
        compiler_params=_params(("parallel", "parallel")),
    )(*args)


def _attn_bwd(qparts, kparts, vpart, o, do, lse, n_heads, group, dv, scale, name, tq):
    T, Tk = qparts[0][0].shape[0], kparts[0][0].shape[0]
    nq_, nk_ = len(qparts), len(kparts)
    dk_ = LANES * nq_
    n_kv = n_heads // group
    nblk = T // tq
    c2 = scale * LOG2E

    def head(hk, i):
        return hk * group + i // nblk

    sub = min(tq, 256)

    def body(*refs):
        q_refs = refs[:nq_]
        k = _cat(refs[nq_ : nq_ + nk_])
        v_ref, o_ref, do_ref, lse_ref, dq_ref, dk_ref, dv_ref = refs[nq_ + nk_ :]
        i = pl.program_id(1)
        _acc_init(i, [dk_ref, dv_ref])
        v = v_ref[...]
        dk_acc, dv_acc = None, None
        for r0 in range(0, tq, sub):
            rows = slice(r0, r0 + sub)
            q = _cat([r.at[rows, :] for r in q_refs])
            s = lax.dot_general(q, k, _DIMS["NT"], preferred_element_type=F32)
            p = jnp.exp2(s * c2 - lse_ref[rows, :] * LOG2E)
            dov = do_ref[rows, :]
            dp = lax.dot_general(dov, v, _DIMS["NT"], preferred_element_type=F32)
            delta = jnp.sum(dov.astype(F32) * o_ref[rows, :].astype(F32), axis=-1, keepdims=True)
            ds = (p * (dp - delta)).astype(BF16)
            dq_ref[rows, :] = jnp.dot(ds, k, preferred_element_type=F32) * scale
            dk_part = lax.dot_general(ds, q, _DIMS["TN"], preferred_element_type=F32)
            dv_part = lax.dot_general(p.astype(BF16), dov, _DIMS["TN"], preferred_element_type=F32)
            dk_acc = dk_part if dk_acc is None else dk_acc + dk_part
            dv_acc = dv_part if dv_acc is None else dv_acc + dv_part
        dk_ref[...] += dk_acc
        dv_ref[...] += dv_acc

        @pl.when(i == group * nblk - 1)
        def _():
            dk_ref[...] *= scale

    in_specs = [pl.BlockSpec((tq, LANES), lambda hk, i, f=f: (i % nblk, f(head(hk, i)))) for _, f in qparts]
    in_specs += [pl.BlockSpec((Tk, LANES), lambda hk, i, f=f: (0, f(hk))) for _, f in kparts]
    fv = vpart[1]
    in_specs.append(pl.BlockSpec((Tk, dv), lambda hk, i: (0, fv(hk))))
    in_specs += [pl.BlockSpec((tq, dv), lambda hk, i: (i % nblk, head(hk, i)))] * 2
    in_specs.append(pl.BlockSpec((None, tq, 1), lambda hk, i: (head(hk, i), i % nblk, 0)))
    return pl.pallas_call(
        body,
        name=name,
        grid=(n_kv, group * nblk),
        in_specs=in_specs,
        out_specs=[pl.BlockSpec((tq, dk_), lambda hk, i: (i % nblk, head(hk, i))),
                   pl.BlockSpec((Tk, dk_), lambda hk, i: (0, hk)),
                   pl.BlockSpec((Tk, dv), lambda hk, i: (0, hk))],
        out_shape=[jax.ShapeDtypeStruct((T, n_heads * dk_), F32), jax.ShapeDtypeStruct((Tk, n_kv * dk_), F32),
                   jax.ShapeDtypeStruct((Tk, n_kv * dv), F32)],
        compiler_params=_params(("parallel", "arbitrary")),
    )(*[a for a, _ in qparts], *[a for a, _ in kparts], vpart[0], o, do, lse)


def _gates_fwd(pg, ya, yb, tr):
    n, d = ya.shape

    def body(pg_ref, ya_ref, yb_ref, o_ref):
        ga = jax.nn.sigmoid(pg_ref[:, :d].astype(F32))
        gb = jax.nn.sigmoid(pg_ref[:, d:].astype(F32))
        o_ref[...] = (ga * ya_ref[...].astype(F32) + gb * yb_ref[...].astype(F32)).astype(BF16)

    return pl.pallas_call(
        body,
        name="gates_fwd",
        grid=(n // tr,),
        in_specs=[_rows(tr, 2 * d), _rows(tr, d), _rows(tr, d)],
        out_specs=_rows(tr, d),
        out_shape=jax.ShapeDtypeStruct((n, d), BF16),
        compiler_params=_params(("parallel",)),
    )(pg, ya, yb)


def _gates_bwd(dm, pg, ya, yb, tr):
    n, d = ya.shape

    def body(dm_ref, pg_ref, ya_ref, yb_ref, dya_ref, dyb_ref, dpg_ref):
        dmv = dm_ref[...].astype(F32)
        ga = jax.nn.sigmoid(pg_ref[:, :d].astype(F32))
        gb = jax.nn.sigmoid(pg_ref[:, d:].astype(F32))
        dya_ref[...] = (dmv * ga).astype(BF16)
        dyb_ref[...] = (dmv * gb).astype(BF16)
        dpg_ref[:, :d] = (dmv * ya_ref[...].astype(F32) * ga * (1.0 - ga)).astype(BF16)
        dpg_ref[:, d:] = (dmv * yb_ref[...].astype(F32) * gb * (1.0 - gb)).astype(BF16)

    return pl.pallas_call(
        body,
        name="gates_bwd",
        grid=(n // tr,),
        in_specs=[_rows(tr, d), _rows(tr, 2 * d), _rows(tr, d), _rows(tr, d)],
        out_specs=[_rows(tr, d), _rows(tr, d), _rows(tr, 2 * d)],
        out_shape=[jax.ShapeDtypeStruct((n, d), BF16), jax.ShapeDtypeStruct((n, d), BF16), jax.ShapeDtypeStruct((n, 2 * d), BF16)],
        compiler_params=_params(("parallel",)),
    )(dm, pg, ya, yb)


def _resid_norm2_fwd(x2d, att, g1, n2g, sh2, sc2, tr):
    n, d = x2d.shape

    def body(x_ref, a_ref, g1_ref, g_ref, sh_ref, sc_ref, x1_ref, z_ref):
        x1 = x_ref[...] + g1_ref[...] * a_ref[...]
        x1_ref[...] = x1
        xh, _ = _rms(x1)
        z_ref[...] = ((xh * g_ref[...]) * (1.0 + sc_ref[...]) + sh_ref[...]).astype(BF16)

    return pl.pallas_call(
        body,
        name="resid_norm2_fwd",
        grid=(n // tr,),
        in_specs=[_rows(tr, d), _rows(tr, d)] + [_bcast(d)] * 4,
        out_specs=[_rows(tr, d), _rows(tr, d)],
        out_shape=[jax.ShapeDtypeStruct((n, d), F32), jax.ShapeDtypeStruct((n, d), BF16)],
        compiler_params=_params(("parallel",)),
    )(x2d, att, g1, n2g, sh2, sc2)


def _resid_norm2_bwd(dz2, x1, dx2, att, n2g, sc2, g1, tr):
    n, d = x1.shape

    def body(dz_ref, x1_ref, dx2_ref, a_ref, g_ref, sc_ref, g1_ref, dx1_ref, da_ref, dg_ref, dsh_ref, dsc_ref, dg1_ref):
        _acc_init(pl.program_id(0), [dg_ref, dsh_ref, dsc_ref, dg1_ref])
        xh, r = _rms(x1_ref[...])
        dzv = dz_ref[...]
        gv = g_ref[...]
        dsc_ref[...] += _csum(dzv * (xh * gv))
        dsh_ref[...] += _csum(dzv)
        dh = dzv * (1.0 + sc_ref[...])
        dg_ref[...] += _csum(dh * xh)
        dx1 = _rms_bwd(xh, r, dh * gv) + dx2_ref[...]
        dx1_ref[...] = dx1
        dg1_ref[...] += _csum(dx1 * a_ref[...])
        da_ref[...] = (dx1 * g1_ref[...]).astype(BF16)

    return pl.pallas_call(
        body,
        name="resid_norm2_bwd",
        grid=(n // tr,),
        in_specs=[_rows(tr, d)] * 4 + [_bcast(d)] * 3,
        out_specs=[_rows(tr, d), _rows(tr, d)] + [_bcast(d)] * 4,
        out_shape=[jax.ShapeDtypeStruct((n, d), F32), jax.ShapeDtypeStruct((n, d), BF16)] + [jax.ShapeDtypeStruct((1, d), F32)] * 4,
        compiler_params=_params(("arbitrary",)),
    )(dz2, x1, dx2, att, n2g, sc2, g1)


def _edges(shape):
    row = lax.broadcasted_iota(jnp.int32, shape, 0)
    return row == 0, row == shape[0] - 1


def _shifts(u, edges):
    n = u.shape[0]
    return jnp.where(edges[0], 0.0, pltpu.roll(u, 1, 0)), jnp.where(edges[1], 0.0, pltpu.roll(u, n - 1, 0))


def _conv3(u, prev, nxt, w_ref, b_ref):
    return b_ref[...] + w_ref[0:1, :] * prev + w_ref[1:2, :] * u + w_ref[2:3, :] * nxt


def _ffn_up_conv(z, wup_t, cw, cb, tc, after):
    n, d = z.shape
    f = wup_t.shape[0] // 2
    nb = f // tc

    def body(z_ref, wa_ref, wb_ref, cwa, cwb, cba, cbb, after_ref, ua_ref, ub_ref, h_ref):
        w = jnp.concatenate([wa_ref[...], wb_ref[...]], axis=0)
        u = lax.dot_general(z_ref[...], w, _DIMS["NT"], preferred_element_type=F32).astype(BF16)
        ua_ref[...] = u[:, :tc]
        ub_ref[...] = u[:, tc:]
        edges = _edges((n, tc))
        ua = u[:, :tc].astype(F32)
        ub = u[:, tc:].astype(F32)
        a = _conv3(ua, *_shifts(ua, edges), cwa, cba)
        b = _conv3(ub, *_shifts(ub, edges), cwb, cbb)
        h_ref[...] = (a * jax.nn.sigmoid(a) * b).astype(BF16)

    col = lambda rows, off: pl.BlockSpec((rows, tc), lambda i: (0, i + off))
    w_rows = lambda off: pl.BlockSpec((tc, d), lambda i: (i + off, 0))
    return pl.pallas_call(
        body,
        name="ffn_up_conv",
        grid=(nb,),
        in_specs=[pl.BlockSpec((n, d), lambda i: (0, 0)), w_rows(0), w_rows(nb), col(3, 0), col(3, nb), col(1, 0), col(1, nb),
                  pl.BlockSpec(after.shape, lambda i: (0, 0))],
        out_specs=[col(n, 0)] * 3,
        out_shape=[jax.ShapeDtypeStruct((n, f), BF16)] * 3,
        compiler_params=_params(("parallel",)),
    )(z, wup_t, wup_t, cw, cw, cb, cb, after)


def _ffn_down_dx_conv_bwd(df, wdown, u_a, u_b, cw, cb, tc, after):
    n, f = u_a.shape
    d = df.shape[1]
    nb = f // tc

    def part(uv, prev, nxt, duc, edges, w_ref, du_ref, dw_ref, db_ref):
        db_ref[...] = _csum(duc)
        dw_ref[0:1, :] = _csum(duc * prev)
        dw_ref[1:2, :] = _csum(duc * uv)
        dw_ref[2:3, :] = _csum(duc * nxt)
        d_prev, d_next = _shifts(duc, edges)
        du_ref[...] = (w_ref[0:1, :] * d_next + w_ref[1:2, :] * duc + w_ref[2:3, :] * d_prev).astype(BF16)

    def body(df_ref, wd_ref, ua_ref, ub_ref, wa_ref, wb_ref, ba_ref, bb_ref, after_ref,
             dua_ref, dub_ref, dwa_ref, dwb_ref, dba_ref, dbb_ref):
        dhv = lax.dot_general(df_ref[...], wd_ref[...], _DIMS["NT"], preferred_element_type=F32)
        dhv = dhv.astype(BF16).astype(F32)
        edges = _edges((n, tc))
        ua = ua_ref[...].astype(F32)
        ub = ub_ref[...].astype(F32)
        sa = _shifts(ua, edges)
        sb = _shifts(ub, edges)
        a = _conv3(ua, *sa, wa_ref, ba_ref)
        b = _conv3(ub, *sb, wb_ref, bb_ref)
        sg = jax.nn.sigmoid(a)
        da = dhv * b * (sg * (1.0 + a * (1.0 - sg)))
        db = dhv * (a * sg)
        part(ua, *sa, da, edges, wa_ref, dua_ref, dwa_ref, dba_ref)
        part(ub, *sb, db, edges, wb_ref, dub_ref, dwb_ref, dbb_ref)

    col = lambda rows, off: pl.BlockSpec((rows, tc), lambda i: (0, i + off))
    return pl.pallas_call(
        body,
        name="ffn_down_dx_conv_bwd",
        grid=(nb,),
        in_specs=[pl.BlockSpec((n, d), lambda i: (0, 0)), pl.BlockSpec((tc, d), lambda i: (i, 0)), col(n, 0), col(n, 0),
                  col(3, 0), col(3, nb), col(1, 0), col(1, nb), pl.BlockSpec(after.shape, lambda i: (0, 0))],
        out_specs=[col(n, 0), col(n, 0), col(3, 0), col(3, 0), col(1, 0), col(1, 0)],
        out_shape=[jax.ShapeDtypeStruct((n, f), BF16)] * 2 + [jax.ShapeDtypeStruct((3, f), F32)] * 2 + [jax.ShapeDtypeStruct((1, f), F32)] * 2,
        compiler_params=_params(("parallel",)),
    )(df, wdown, u_a, u_b, cw, cw, cb, cb, after)


def _loss_head(x1, f, g2, fg, tgt, tr):
    n, d = x1.shape

    def body(x1_ref, f_ref, g2_ref, fg_ref, t_ref, sq_ref, dx2_ref, dfg_ref, dg2_ref, df_ref):
        _acc_init(pl.program_id(0), [sq_ref, dfg_ref, dg2_ref])
        fv = f_ref[...]
        xh, r = _rms(x1_ref[...] + g2_ref[...] * fv)
        err = xh * fg_ref[...] - t_ref[...]
        sq_ref[...] += _csum(err * err)
        dy = err * (1.0 / d)
        dfg_ref[...] += _csum(dy * xh)
        dx2 = _rms_bwd(xh, r, dy * fg_ref[...])
        dx2_ref[...] = dx2
        dg2_ref[...] += _csum(dx2 * fv)
        df_ref[...] = (dx2 * g2_ref[...]).astype(BF16)

    return pl.pallas_call(
        body,
        name="loss_head",
        grid=(n // tr,),
        in_specs=[_rows(tr, d), _rows(tr, d), _bcast(d), _bcast(d), _rows(tr, d)],
        out_specs=[_bcast(d), _rows(tr, d), _bcast(d), _bcast(d), _rows(tr, d)],
        out_shape=[jax.ShapeDtypeStruct((1, d), F32), jax.ShapeDtypeStruct((n, d), F32), jax.ShapeDtypeStruct((1, d), F32),
                   jax.ShapeDtypeStruct((1, d), F32), jax.ShapeDtypeStruct((n, d), BF16)],
        compiler_params=_params(("arbitrary",)),
    )(x1, f, g2, fg, tgt)


def _sum_slots(g, name):
    s, r, w = g.shape

    def body(g_ref, o_ref):
        acc = g_ref[0]
        for k in range(1, s):
            acc = acc + g_ref[k]
        o_ref[...] = acc

    return pl.pallas_call(body, name=name, out_shape=jax.ShapeDtypeStruct((r, w), F32))(g)


def _silu_grad_mul(ds, cvec):
    def body(d_ref, c_ref, o_ref):
        cv = c_ref[...]
        sg = jax.nn.sigmoid(cv)
        o_ref[...] = d_ref[...] * (sg * (1.0 + cv * (1.0 - sg)))

    return pl.pallas_call(body, name="silu_grad_mul", out_shape=jax.ShapeDtypeStruct(ds.shape, F32))(ds, cvec)


def _adamw_update(wv, gv, mv, vv, d_ref, mo_ref, vo_ref):
    mn = ADAM_B1 * mv + (1.0 - ADAM_B1) * gv
    vn = ADAM_B2 * vv + (1.0 - ADAM_B2) * (gv * gv)
    mo_ref[...] = mn
    vo_ref[...] = vn
    m_hat = mn / (1.0 - ADAM_B1**ADAM_STEP)
    v_hat = vn / (1.0 - ADAM_B2**ADAM_STEP)
    d_ref[...] = -ADAM_LR * (m_hat / (jnp.sqrt(v_hat) + ADAM_EPS) + ADAM_WD * wv)


def _adamw_many(ws, gs, ms, vs, name):
    n = len(ws)

    def body(*refs):
        for k in range(n):
            w_ref, g_ref, m_ref, v_ref = (refs[q * n + k] for q in range(4))
            d_ref, mo_ref, vo_ref = (refs[(4 + q) * n + k] for q in range(3))
            _adamw_update(w_ref[...], g_ref[...], m_ref[...], v_ref[...], d_ref, mo_ref, vo_ref)

    res = pl.pallas_call(body, name=name, out_shape=[jax.ShapeDtypeStruct(w.shape, F32) for w in ws] * 3)(*ws, *gs, *ms, *vs)
    return res[:n], res[n : 2 * n], res[2 * n :]


def _adamw(w, g, m, v, name, g_transposed=False, g_sibling=None):
    r, cdim = w.shape
    halves = g_sibling is not None
    block = 1 << 19
    if g_transposed:
        tc = _pick(cdim // 2 if halves else cdim, 2048)
        tr = _pick(r, max(LANES, block // tc), LANES)
        per_half = (cdim // 2) // tc
    else:
        rows = r // 2 if halves else r
        tc = _pick(cdim, 2048)
        tr = _pick(rows, max(8, block // tc), 8)
        if tr < 64 and rows > 64:
            tr, tc = _pick(rows, 1024, 8), _pick(cdim, 512)
        per_half = (r // 2) // tr
    emit_g = g_transposed or halves

    def body(w_ref, g_ref, *rest):
        m_ref, v_ref = rest[halves : halves + 2]
        outs = rest[halves + 2 :]
        gv = g_ref[...]
        if halves:
            along = pl.program_id(1 if g_transposed else 0)
            gv = jnp.where(along // per_half == lax.axis_index("c"), gv, rest[0][...])
        if g_transposed:
            gv = gv.T
        if emit_g:
            outs[0][...] = gv
        _adamw_update(w_ref[...], gv, m_ref[...], v_ref[...], *outs[-3:])

    spec = pl.BlockSpec((tr, tc), lambda i, j: (i, j))
    if g_transposed:
        g_spec = pl.BlockSpec((tc, tr), lambda i, j: (j % per_half if halves else j, i))
    else:
        g_spec = pl.BlockSpec((tr, tc), lambda i, j: (i % per_half if halves else i, j))
    n_out = 3 + emit_g
    res = pl.pallas_call(
        body,
        name=name,
        grid=(r // tr, cdim // tc),
        in_specs=[spec, g_spec] + [g_spec] * halves + [spec, spec],
        out_specs=[spec] * n_out,
        out_shape=[jax.ShapeDtypeStruct((r, cdim), F32)] * n_out,
        compiler_params=_params(("parallel", "parallel")),
    )(w, g, *([g_sibling] if halves else []), m, v)
    return res if emit_g else [g, *res]


def _place():
    return lax.axis_index("x"), lax.axis_index("y"), lax.axis_index("c")


def _remote(src, dst, send_sem, recv_sem, dev):
    return pltpu.make_async_remote_copy(src_ref=src, dst_ref=dst, send_sem=send_sem, recv_sem=recv_sem, device_id=dev, device_id_type=MESH)


ANY = pl.BlockSpec(memory_space=pl.ANY)


def _all_gather_small(v, name, after=()):
    r, w = v.shape

    def body(v_ref, *rest):
        o_ref, send, recv, lsem = rest[len(after) :]
        x, y, c = _place()
        me = 4 * x + 2 * y + c
        mine = pltpu.make_async_copy(v_ref, o_ref.at[me], lsem)
        mine.start()
        sent = []
        for k in range(1, 8):
            px, py, pc = x ^ (k >> 2), y ^ ((k >> 1) & 1), c ^ (k & 1)
            cp = _remote(v_ref, o_ref.at[me], send.at[k - 1], recv.at[k - 1], (px, py, pc))
            cp.start()
            sent.append(cp)
        for k in range(1, 8):
            px, py, pc = x ^ (k >> 2), y ^ ((k >> 1) & 1), c ^ (k & 1)
            slot = o_ref.at[4 * px + 2 * py + pc]
            _remote(slot, slot, send.at[k - 1], recv.at[k - 1], (x, y, c)).wait_recv()
        for cp in sent:
            cp.wait_send()
        mine.wait()

    return pl.pallas_call(
        body,
        name=name,
        out_shape=jax.ShapeDtypeStruct((8, r, w), F32),
        in_specs=[pl.BlockSpec(memory_space=pltpu.VMEM)] + [ANY] * len(after),
        out_specs=pl.BlockSpec(memory_space=pltpu.VMEM),
        scratch_shapes=[pltpu.SemaphoreType.DMA((7,)), pltpu.SemaphoreType.DMA((7,)), pltpu.SemaphoreType.DMA],
        compiler_params=pltpu.CompilerParams(vmem_limit_bytes=VMEM_LIMIT),
    )(v, *after)


HBM = pl.BlockSpec(memory_space=pltpu.HBM)
SEM = pl.BlockSpec(memory_space=pltpu.SEMAPHORE)
EFFECT = pltpu.SideEffectType.DATAFLOW_SIDE_EFFECTING


def _other_chips(x, y):
    return [(1 - x, y), (x, 1 - y), (1 - x, 1 - y)]


def _bulk_start(name, srcs, land_shapes, n_copies, copies, after, lands_init=None):
    n, m = len(srcs), len(land_shapes)

    def body(*refs):
        src_refs, land_refs = refs[:n], refs[n : n + m]
        send, recv = refs[n + m + 1], refs[n + m + 2]
        token = refs[-1]
        for k, (s, d, dev) in enumerate(copies(src_refs, land_refs)):
            _remote(s, d, send.at[k], recv.at[k], dev).start()
        token[...] = jnp.zeros_like(token)

    lands = lands_init or [lax.empty(s.shape, s.dtype) for s in land_shapes]
    lands = [pltpu.with_memory_space_constraint(b, pltpu.HBM) for b in lands]
    out = pl.pallas_call(
        body,
        name=name,
        out_shape=(pltpu.SemaphoreType.DMA((n_copies,)), pltpu.SemaphoreType.DMA((n_copies,)),
                   *[pltpu.HBM(s.shape, s.dtype) for s in srcs], *[pltpu.HBM(s.shape, s.dtype) for s in land_shapes],
                   jax.ShapeDtypeStruct((8, LANES), F32)),
        in_specs=[HBM] * (n + m) + [ANY],
        out_specs=(SEM, SEM, *[HBM] * (n + m), pl.BlockSpec(memory_space=pltpu.VMEM)),
        input_output_aliases={i: 2 + i for i in range(n + m)},
        compiler_params=pltpu.CompilerParams(has_side_effects=EFFECT),
    )(*[pltpu.with_memory_space_constraint(s, pltpu.HBM) for s in srcs], *lands, after)
    return out[0], out[1], list(out[2 : 2 + n]), list(out[2 + n : 2 + n + m]), out[-1][0:1, 0:1]


def _bulk_wait(name, send, recv, srcs, lands, after, waits):
    n, m = len(srcs), len(lands)

    def body(*refs):
        src_refs, land_refs = refs[:n], refs[n : n + m]
        send_sem, recv_sem = refs[n + m], refs[n + m + 1]
        x, y, c = _place()
        for k, (s, d) in enumerate(waits(src_refs, land_refs)):
            cp = _remote(s, d, send_sem.at[k], recv_sem.at[k], (x, y, c))
            cp.wait_send()
            cp.wait_recv()

    out = pl.pallas_call(
        body,
        name=name,
        out_shape=tuple(pltpu.HBM(s.shape, s.dtype) for s in (*srcs, *lands)),
        in_specs=[HBM] * (n + m) + [SEM, SEM, ANY],
        out_specs=tuple([HBM] * (n + m)),
        input_output_aliases={i: i for i in range(n + m)},
        compiler_params=pltpu.CompilerParams(has_side_effects=EFFECT),
    )(*srcs, *lands, send, recv, after)
    return list(out[:n]), list(out[n:])


def _peers(x, y, c):
    return [(x ^ (k >> 2), y ^ ((k >> 1) & 1), c ^ (k & 1)) for k in range(1, 8)]


def _small_gather_start(v, after, name):
    r, w = v.shape

    def copies(src, land):
        x, y, c = _place()
        return [(src[0], land[0].at[4 * x + 2 * y + c], peer) for peer in _peers(x, y, c)]

    me = 4 * lax.axis_index("x") + 2 * lax.axis_index("y") + lax.axis_index("c")
    init = [lax.dynamic_update_slice(lax.empty((8, r, w), F32), v[None], (me, 0, 0))]
    return _bulk_start(name, [v], [jax.ShapeDtypeStruct((8, r, w), F32)], 7, copies, after, init)


def _small_gather_wait(started, after, name):
    send, recv, srcs, lands, _ = started

    def waits(src, land):
        x, y, c = _place()
        return [(src[0], land[0].at[4 * px + 2 * py + pc]) for px, py, pc in _peers(x, y, c)]

    return _bulk_wait(name, send, recv, srcs, lands, after, waits)[1][0]


def _gather_start(shards, after, name):
    def copies(src, land):
        x, y, c = _place()
        j = 2 * x + y
        return [(src[a].at[c], land[a].at[j, c], (px, py, c)) for a in range(len(shards)) for px, py in _other_chips(x, y)]

    shapes = [jax.ShapeDtypeStruct((4,) + s.shape, s.dtype) for s in shards]
    j = 2 * lax.axis_index("x") + lax.axis_index("y")
    init = [lax.dynamic_update_slice(lax.empty(t.shape, t.dtype), s[None], (j, 0, 0, 0)) for t, s in zip(shapes, shards)]
    return _bulk_start(name, shards, shapes, 3 * len(shards), copies, after, init)


def _gather_wait(started, after, name):
    send, recv, srcs, lands, _ = started

    def waits(src, land):
        x, y, c = _place()
        return [(src[a].at[c], land[a].at[2 * px + py, c]) for a in range(len(srcs)) for px, py in _other_chips(x, y)]

    return _bulk_wait(name, send, recv, srcs, lands, after, waits)


def _forward_start(lands, after, name):
    def copies(src, _):
        x, y, c = _place()
        blocks = [src[a].at[2 * px + py, c] for a in range(len(lands)) for px, py in _other_chips(x, y)]
        return [(b, b, (x, y, 1 - c)) for b in blocks]

    return _bulk_start(name, lands, [], 3 * len(lands), copies, after)


def _forward_wait(started, after, name):
    send, recv, bufs, _, _ = started

    def waits(src, _):
        x, y, c = _place()
        return [(src[a].at[2 * px + py, c], src[a].at[2 * px + py, 1 - c]) for a in range(len(bufs)) for px, py in _other_chips(x, y)]

    return _bulk_wait(name, send, recv, bufs, [], after, waits)[0]


def _as_rows(lands):
    return [f.reshape(4 * f.shape[2] * 2, f.shape[3]) for f in lands]


def _gather_land(started, after, tag):
    shards, lands = _gather_wait(started, after, "gather_wait_" + tag)
    return shards, _forward_start(lands, shards[0], "forward_start_" + tag)


def _gather_done(landed, after, tag):
    _, fwd = landed
    return _as_rows(_forward_wait(fwd, after, "forward_wait_" + tag))


def _swap_halves(grads, name):
    n = len(grads)

    def body(*refs):
        ins, outs = refs[:n], refs[n : 2 * n]
        send, recv = refs[2 * n :]
        x, y, c = _place()
        started = []
        for a in range(n):
            for s in range(4):
                cp = _remote(ins[a].at[s, 1 - c], outs[a].at[s], send.at[4 * a + s], recv.at[4 * a + s], (x, y, 1 - c))
                cp.start()
                started.append(cp)
        for cp in started:
            cp.wait_recv()
        for cp in started:
            cp.wait_send()

    return pl.pallas_call(
        body,
        name=name,
        out_shape=[jax.ShapeDtypeStruct((4,) + g.shape[2:], g.dtype) for g in grads],
        in_specs=[ANY] * n,
        out_specs=[ANY] * n,
        scratch_shapes=[pltpu.SemaphoreType.DMA((4 * n,)), pltpu.SemaphoreType.DMA((4 * n,))],
    )(*grads)


def _add_halves(grads, others, tag):
    outs = []
    for a, (g, o) in enumerate(zip(grads, others)):
        _, _, rh, cdim = g.shape
        tr = _pick(rh, 512, 16)

        def body(g_ref, o_ref, p_ref):
            p_ref[...] = (g_ref[...].astype(F32) + o_ref[...].astype(F32)).astype(BF16)

        outs.append(
            pl.pallas_call(
                body,
                name=f"add_halves_{tag}{a}",
                grid=(4, rh // tr),
                in_specs=[pl.BlockSpec((None, None, tr, cdim), lambda s, i: (s, lax.axis_index("c"), i, 0)),
                          pl.BlockSpec((None, tr, cdim), lambda s, i: (s, i, 0))],
                out_specs=pl.BlockSpec((None, tr, cdim), lambda s, i: (s, i, 0)),
                out_shape=jax.ShapeDtypeStruct((4, rh, cdim), BF16),
                compiler_params=_params(("parallel", "parallel")),
            )(g, o)
        )
    return outs


def _exchange_start(parts, after, name):
    def copies(src, land):
        x, y, c = _place()
        j = 2 * x + y
        return [(src[a].at[2 * px + py], land[a].at[j], (px, py, c)) for a in range(len(parts)) for px, py in _other_chips(x, y)]

    return _bulk_start(name, parts, [jax.ShapeDtypeStruct(p.shape, p.dtype) for p in parts], 3 * len(parts), copies, after)


def _exchange_finish(started, after, name):
    send, recv, srcs, lands, _ = started

    def waits(src, land):
        x, y, _ = _place()
        return [(src[a].at[2 * px + py], land[a].at[2 * px + py]) for a in range(len(srcs)) for px, py in _other_chips(x, y)]

    srcs, lands = _bulk_wait(name, send, recv, srcs, lands, after, waits)
    return lands, srcs


def _sum_chips(recvd, parts, tag):
    outs = []
    for a, (g, p) in enumerate(zip(recvd, parts)):
        _, rh, cdim = g.shape
        tr = _pick(rh, 512, 16)

        def body(g_ref, p_ref, o_ref):
            j = 2 * lax.axis_index("x") + lax.axis_index("y")
            own = p_ref[...].astype(F32)
            term = [jnp.where(j == s, own, g_ref[s].astype(F32)) for s in range(4)]
            o_ref[...] = ((term[0] + term[1]) + term[2]) + term[3]

        outs.append(
            pl.pallas_call(
                body,
                name=f"sum_chips_{tag}{a}",
                grid=(rh // tr,),
                in_specs=[pl.BlockSpec((4, tr, cdim), lambda i: (0, i, 0)),
                          pl.BlockSpec((None, tr, cdim), lambda i: (2 * lax.axis_index("x") + lax.axis_index("y"), i, 0))],
                out_specs=pl.BlockSpec((tr, cdim), lambda i: (i, 0)),
                out_shape=jax.ShapeDtypeStruct((rh, cdim), F32),
                compiler_params=_params(("parallel",)),
            )(g, p)
        )
    return outs


def _joined(mine, other):
    first = lax.axis_index("c") == 0
    return jnp.concatenate([jnp.where(first, mine, other), jnp.where(first, other, mine)], axis=0)


def _grad_views(grads):
    return [g.reshape(4, 2, g.shape[0] // 8, g.shape[1]) for g in grads]


def _scatter_start(grads, tag, after=None):
    views = _grad_views(grads)
    others = _swap_halves(views, "swap_halves_" + tag)
    mine = _add_halves(views, others, tag)
    return _exchange_start(mine, others[-1] if after is None else after, "exchange_start_" + tag)


def _swap_start(grads, after, tag):
    views = _grad_views(grads)

    def copies(src, land):
        x, y, c = _place()
        return [(src[a].at[s, 1 - c], land[a].at[s], (x, y, 1 - c)) for a in range(len(views)) for s in range(4)]

    shapes = [jax.ShapeDtypeStruct((4,) + v.shape[2:], v.dtype) for v in views]
    return _bulk_start("swap_start_" + tag, views, shapes, 4 * len(views), copies, after)


def _scatter_start_after_swap(swapped, after, tag):
    send, recv, views, lands, _ = swapped

    def waits(src, land):
        c = lax.axis_index("c")
        return [(src[a].at[s, 1 - c], land[a].at[s]) for a in range(len(views)) for s in range(4)]

    views, others = _bulk_wait("swap_wait_" + tag, send, recv, views, lands, after, waits)
    mine = _add_halves(views, others, tag)
    return _exchange_start(mine, others[-1], "exchange_start_" + tag)


def _join_start(halves, after, tag):
    def copies(src, land):
        x, y, c = _place()
        return [(src[a], land[a], (x, y, 1 - c)) for a in range(len(halves))]

    return _bulk_start("join_start_" + tag, halves, [jax.ShapeDtypeStruct(h.shape, h.dtype) for h in halves], len(halves), copies, after)


def _join_wait(started, after, tag):
    send, recv, halves, lands, _ = started
    halves, others = _bulk_wait("join_wait_" + tag, send, recv, halves, lands, after, lambda src, land: list(zip(src, land)))
    return list(zip(halves, others))


def _scatter_sums(started, after, tag):
    return _sum_chips(*_exchange_finish(started, after, "exchange_wait_" + tag), tag)


def _t_bf16(w):
    return w.T.astype(BF16)


def kernel(x, c, ctx, c_ctx, w_ada, b_ada, norm1_g, w_in, mla_q_norm_g, w_q_up, mla_kv_norm_g, w_kv_up, gqa_q_norm_g, gqa_k_norm_g, w_br_a, w_br_b, w_out, norm2_g, w_up, conv_w, conv_b, w_down, final_norm_g, loss_target, m_c_ctx, m_w_ada, m_b_ada, m_norm1_g, m_w_in, m_mla_q_norm_g, m_w_q_up, m_mla_kv_norm_g, m_w_kv_up, m_gqa_q_norm_g, m_gqa_k_norm_g, m_w_br_a, m_w_br_b, m_w_out, m_norm2_g, m_w_up, m_conv_w, m_conv_b, m_w_down, m_final_norm_g, v_c_ctx, v_w_ada, v_b_ada, v_norm1_g, v_w_in, v_mla_q_norm_g, v_w_q_up, v_mla_kv_norm_g, v_w_kv_up, v_gqa_q_norm_g, v_gqa_k_norm_g, v_w_br_a, v_w_br_b, v_w_out, v_norm2_g, v_w_up, v_conv_w, v_conv_b, v_w_down, v_final_norm_g):
    T, D = x.shape[1], x.shape[2]
    C = ctx.shape[1]
    NA = w_ada.shape[2]
    NW = w_up.shape[2]
    F2 = 4 * NW
    FF = F2 // 2
    xi, yi, ci = _place()
    j = 2 * xi + yi
    me = 4 * xi + 2 * yi + ci
    tr = _pick(C, 256, 8)

    x2d, tgt, ctx2d = x[0], loss_target[0], ctx[0]
    fg = final_norm_g.reshape(1, D)
    cc = c_ctx.reshape(1, D)

    halve = lambda s: s.reshape(2, s.shape[0] // 2, s.shape[1])
    win_shard = halve(_t_bf16(w_in[0]))
    w0 = max(D, NW)
    pay = jnp.zeros((8, w0), F32).at[0:1, :D].set(c).at[1:4, :NW].set(conv_w[0])
    got = _all_gather_small(pay, "gather_cond")
    ag_in = _gather_start([win_shard], got, "gather_start_in")
    t_in = ag_in[4]
    c_all = got[:, 0, :D]
    cw = jnp.concatenate([got[2 * s, 1:4, :NW] for s in range(4)], axis=1)
    s16 = jnp.concatenate([c_all, cc, jnp.zeros((7, D), F32)], axis=0) + t_in
    b_cols = lax.dynamic_slice(b_ada, (0, j * NA), (1, NA))
    ada_part = _mm(s16, w_ada[0], "NN", F32, "ada_fwd", act="silu", bias=b_cols)

    wq3 = (w_q_up[0] + t_in).reshape(MLA_Q_LORA, 2, MLA_NOPE + MLA_ROPE)
    wq_perm = jnp.concatenate([wq3[:, :, :MLA_NOPE].reshape(MLA_Q_LORA, -1), wq3[:, :, MLA_NOPE:].reshape(MLA_Q_LORA, -1)], axis=1)
    low = [halve(_t_bf16(wq_perm)), halve(_t_bf16(w_kv_up[0] + t_in))]
    br = [halve(_t_bf16(w_br_a[0] + t_in)), halve(_t_bf16(w_br_b[0] + t_in)), halve((w_out[0] + t_in).astype(BF16))]
    up = [halve(_t_bf16(w_up[0] + t_in))]
    down = [halve((w_down[0] + t_in).astype(BF16))]

    got = _all_gather_small(ada_part, "gather_ada", after=(*low, *br, *up, *down))
    ada = jnp.concatenate([got[2 * s] for s in range(4)], axis=1)
    lat = lax.dynamic_slice(ada, (me, 0), (1, 6 * D))
    sh1, sc1, g1, sh2, sc2, g2 = [lat[:, k * D : (k + 1) * D] for k in range(6)]
    csh, csc = ada[8:9, :D], ada[8:9, D : 2 * D]
    ag_low = _gather_start(low, got, "gather_start_low")
    ag_br = _gather_start(br, ag_low[4], "gather_start_br")
    ag_up = _gather_start(up, ag_br[4], "gather_start_up")
    ag_down = _gather_start(down, ag_up[4], "gather_start_down")
    sh1 = sh1 + ag_down[4]

    cos_a, ss_a = _rope_tables(C, T, MLA_ROPE)
    cos_b, ss_b = _rope_tables(C, T, GQA_HEAD_DIM)
    lcos_a, lss_a, lcos_b, lss_b = cos_a[:T], ss_a[:T], cos_b[:T], ss_b[:T]

    in_landed = _gather_land(ag_in, down[0], "in")
    z_all = _norm_mod_fwd(x2d, norm1_g, sh1 + in_landed[1][4], sc1, "norm1_lat_fwd", tr, out_rows=T + C)
    z_all = _norm_mod_fwd(ctx2d, norm1_g, csh, csc, "norm1_ctx_fwd", tr, base=z_all, out_off=T)
    (win_t,) = _gather_done(in_landed, z_all, "in")
    kv_cols = KVP - LANES + MLA_ROPE
    e_kpe = MLA_KV_LORA + MLA_ROPE
    w_kvp = jnp.concatenate([win_t[:MLA_KV_LORA], win_t[e_kpe:kv_cols], win_t[MLA_KV_LORA:e_kpe], jnp.zeros((LANES - MLA_ROPE, D), BF16)], axis=0)

    pkv = _mm(z_all, w_kvp, "NT", F32, "proj_kv", tn=KVP)
    pq = _mm(z_all, win_t, "NT", F32, "proj_q", m=T, n=QC, b_off=kv_cols)
    low_landed = _gather_land(ag_low, pq, "low")
    pg = _mm(z_all, win_t, "NT", BF16, "proj_g", m=T, n=2 * D, b_off=kv_cols + QC, after=low_landed[1][4])
    wq_t, wkv_t = _gather_done(low_landed, pg, "low")
    ckv_n, kb2, vb2, kpe2 = _kprep_fwd(pkv, mla_kv_norm_g, gqa_k_norm_g, cos_a, ss_a, cos_b, ss_b, tr)
    kv_up = _mm(ckv_n, wkv_t, "NT", BF16, "kv_up")
    cq_n, qb2 = _qprep_fwd(pq, mla_q_norm_g, gqa_q_norm_g, lcos_b, lss_b, tr)
    q_a = _mm(cq_n, wq_t, "NT", F32, "q_up")
    qar = _qrope_fwd(q_a, lcos_a, lss_a, tr)

    a_q = [(qar, lambda h: 3 * (h // 2) + h % 2), (qar, lambda h: 3 * (h // 2) + 2)]
    a_k = [(kv_up, lambda h: 2 * h), (kpe2, lambda h: h % 2)]
    a_v = (kv_up, lambda h: 2 * h + 1)
    a_scale = float(MLA_NOPE + MLA_ROPE) ** -0.5
    b_q = [(qb2, lambda h: h)]
    b_k = [(kb2, lambda h: h)]
    b_v = (vb2, lambda h: h)
    b_scale = float(GQA_HEAD_DIM) ** -0.5
    tq_f = _pick(T, 2048)
    o_a, lse_a = _attn_fwd(a_q, a_k, a_v, MLA_HEADS, 1, MLA_V, a_scale, "attn_a_fwd", tq_f)
    br_landed = _gather_land(ag_br, o_a, "br")
    o_b, lse_b = _attn_fwd(b_q, b_k, b_v, GQA_HEADS, GQA_GROUP, GQA_HEAD_DIM, b_scale, "attn_b_fwd", tq_f, after=br_landed[1][4])
    wbra_t, wbrb_t, wout = _gather_done(br_landed, o_b, "br")
    up_landed = _gather_land(ag_up, o_b, "up")
    ya = _mm(o_a, wbra_t, "NT", BF16, "br_a", after=up_landed[1][4])
    yb = _mm(o_b, wbrb_t, "NT", BF16, "br_b")
    merged = _gates_fwd(pg, ya, yb, tr)
    att = _mm(merged, wout, "NN", F32, "out_proj")
    x1, z2 = _resid_norm2_fwd(x2d, att, g1, norm2_g, sh2, sc2, tr)
    (wup_t,) = _gather_done(up_landed, z2, "up")
    down_landed = _gather_land(ag_down, z2, "down")
    tc = _pick(FF, 128)
    u_a, u_b, hg = _ffn_up_conv(z2, wup_t, cw, conv_b, tc, down_landed[1][4])
    (wdown,) = _gather_done(down_landed, hg, "down")
    f = _mm(hg, wdown, "NN", F32, "ffn_down", tk=FF // 2)
    sq, dx2, d_fg, d_g2, df = _loss_head(x1, f, g2, fg, tgt, tr)
    loss = lax.psum(0.5 * jnp.sum(sq) / D, ("x", "y", "c"))

    du_a, du_b, dcw_a, dcw_b, dcb_a, dcb_b = _ffn_down_dx_conv_bwd(df, wdown, u_a, u_b, cw, conv_b, _pick(FF, 256), loss.reshape(1, 1))
    g_wdown = _mm(hg, df, "TN", BF16, "ffn_down_dw", tm=FF // 4)
    dz2 = _mm(du_a, wup_t, "NN", F32, "ffn_up_dx_a", tk=FF // 2)
    dz2 = _mm(du_b, wup_t, "NN", F32, "ffn_up_dx_b", b_off=FF, add=dz2, tk=FF // 2)
    g_wup_t = _mm(du_a, z2, "TN", BF16, "ffn_up_dw_a", out_rows=F2, tm=FF // 4)
    g_wup_t = _mm(du_b, z2, "TN", BF16, "ffn_up_dw_b", out_base=g_wup_t, out_off=FF, tm=FF // 4)
    sw_ffn = _swap_start([g_wdown, g_wup_t], sc2, "ffn")
    sc2 = sc2 + sw_ffn[4]
    dx1, datt, d_n2g, d_sh2, d_sc2, d_g1 = _resid_norm2_bwd(dz2, x1, dx2, att, norm2_g, sc2, g1, tr)

    dmerged = _mm(datt, wout, "NT", BF16, "out_proj_dx")
    rs_ffn = _scatter_start_after_swap(sw_ffn, dmerged, "ffn")
    lse_a = lse_a + rs_ffn[4]
    g_wout = _mm(merged, datt, "TN", BF16, "out_proj_dw")
    dya, dyb, dpg = _gates_bwd(dmerged, pg, ya, yb, tr)
    do_a = _mm(dya, wbra_t, "NN", BF16, "br_a_dx")
    g_wbra_t = _mm(dya, o_a, "TN", BF16, "br_a_dw")
    do_b = _mm(dyb, wbrb_t, "NN", BF16, "br_b_dx")
    g_wbrb_t = _mm(dyb, o_b, "TN", BF16, "br_b_dw")
    dqa2, dka2, dva2 = _attn_bwd(a_q, a_k, a_v, o_a, do_a, lse_a, MLA_HEADS, 1, MLA_V, a_scale, "attn_a_bwd", tq_f)
    dqb2, dkb2, dvb2 = _attn_bwd(b_q, b_k, b_v, o_b, do_b, lse_b, GQA_HEADS, GQA_GROUP, GQA_HEAD_DIM, b_scale, "attn_b_bwd", tq_f)
    dq_a = _qrope_bwd(dqa2, lcos_a, lss_a, tr)
    dcq_n = _mm(dq_a, wq_t, "NN", F32, "q_up_dx")
    g_wq_t = _mm(dq_a, cq_n, "TN", BF16, "q_up_dw")
    dpq, d_qg, d_gq = _qprep_bwd(pq, dcq_n, dqb2, mla_q_norm_g, gqa_q_norm_g, lcos_b, lss_b, tr)
    dkv_up, dkpe = _kgrad_split(dka2, dva2, cos_a, ss_a, tr)
    dckv_n = _mm(dkv_up, wkv_t, "NN", F32, "kv_up_dx")
    g_wkv_t = _mm(dkv_up, ckv_n, "TN", BF16, "kv_up_dw")
    rs_mix = _scatter_start([g_wq_t, g_wkv_t, g_wbra_t, g_wbrb_t, g_wout], "mix")
    dpkv, d_kvg, d_kg = _kprep_bwd(pkv, dckv_n, dkb2, dvb2, dkpe, mla_kv_norm_g + rs_mix[4], gqa_k_norm_g, cos_b, ss_b, tr)
    dz_kv = _mm(dpkv, w_kvp, "NN", F32, "proj_kv_dx")
    dz_lat = _mm(dpq, win_t, "NN", F32, "proj_q_dx", b_off=kv_cols, add=dz_kv)
    dz_lat = _mm(dpg, win_t, "NN", F32, "proj_g_dx", b_off=kv_cols + QC, add=dz_lat)
    _, d_n1g_c, d_csh, d_csc = _norm_mod_bwd(dz_kv, T // tr, ctx2d, norm1_g, csc, None, "norm1_ctx_bwd", tr)
    grad_x, d_n1g_l, d_sh1, d_sc1 = _norm_mod_bwd(dz_lat, 0, x2d, norm1_g, sc1, dx1, "norm1_lat_bwd", tr)

    zeros_d = jnp.zeros((1, D), F32)
    d_lat = jnp.concatenate([d_sh1, d_sc1, d_g1, d_sh2, d_sc2, d_g2], axis=1)
    d_ctx_part = jnp.concatenate([d_csh, d_csc], axis=1)
    flat = jnp.concatenate(
        [d_n1g_c + d_n1g_l, d_qg, d_kvg, d_gq, d_kg, d_n2g, dcb_a, dcb_b, d_fg,
         dcw_a.reshape(1, -1), dcw_b.reshape(1, -1), d_ctx_part, d_lat], axis=1)
    n_flat = flat.shape[1]
    n_rows = -(-n_flat // (8 * LANES)) * 8
    flat = jnp.pad(flat, ((0, 0), (0, n_rows * LANES - n_flat))).reshape(n_rows, LANES)
    small = _small_gather_start(flat, grad_x, "small_grads_start")

    g_kvp = _mm(dpkv, z_all, "TN", BF16, "proj_kv_dw", after=small[4])
    nk = MLA_KV_LORA + 2 * GQA_KV_HEADS * GQA_HEAD_DIM
    g_kv = jnp.concatenate([g_kvp[:MLA_KV_LORA], g_kvp[nk : nk + MLA_ROPE], g_kvp[MLA_KV_LORA:nk]], axis=0)
    g_win_t = _mm(dpq, z_all, "TN", BF16, "proj_q_dw", out_rows=kv_cols + QC + 2 * D, out_off=kv_cols, tm=QC // 2)
    g_win_t = _mm(dpg, z_all, "TN", BF16, "proj_g_dw", out_base=g_win_t, out_off=kv_cols + QC)
    g_win_t = lax.dynamic_update_slice(g_win_t, g_kv, (0, 0))

    got = _small_gather_wait(small, g_win_t, "small_grads_wait")
    tot = _sum_slots(got, "sum_small_grads").reshape(1, -1)
    sizes = [D, MLA_Q_LORA, MLA_KV_LORA, GQA_HEAD_DIM, GQA_HEAD_DIM, D, F2, D, 3 * FF, 3 * FF, 2 * D]
    offs = [0]
    for s in sizes:
        offs.append(offs[-1] + s)
    t_n1g, t_qg, t_kvg, t_gq, t_kg, t_n2g, t_cb, t_fg, t_cwa, t_cwb, t_ctx = [tot[:, offs[k] : offs[k + 1]] for k in range(len(sizes))]
    g_cw_full = jnp.concatenate([t_cwa.reshape(3, FF), t_cwb.reshape(3, FF)], axis=1)
    g_cw = lax.dynamic_slice(g_cw_full, (0, j * NW), (3, NW))
    d_lat_all = got.reshape(8, -1)[:, offs[-1] : offs[-1] + 6 * D]
    g16 = jnp.concatenate([d_lat_all, jnp.pad(t_ctx, ((0, 0), (0, 4 * D))), jnp.zeros((7, 6 * D), F32)], axis=0)
    g_b_ada = _sum_slots(g16.reshape(16, 1, 6 * D), "sum_b_ada")
    g16_cols = lax.dynamic_slice(g16, (0, j * NA), (16, NA))
    ds_part = _mm(g16_cols, w_ada[0], "NT", F32, "ada_dx")
    got = _all_gather_small(ds_part[8:16], "gather_ada_dx")
    ds_ctx = _sum_slots(jnp.stack([got[2 * s] for s in range(4)]), "sum_ada_dx")[0:1]
    g_c_ctx = _silu_grad_mul(ds_ctx, cc)

    sw_in = _swap_start([g_win_t], got, "in")

    h_ffn = _scatter_sums(rs_ffn, sw_in[2][0], "ffn")
    j_ffn = _join_start(h_ffn, grad_x, "ffn")
    h_mix = _scatter_sums(rs_mix, j_ffn[2][0], "mix")
    j_mix = _join_start(h_mix, j_ffn[2][0], "mix")
    rs_in = _scatter_start_after_swap(sw_in, j_mix[2][0], "in")
    g_w_ada = _mm(s16, g16_cols, "TN", F32, "ada_dw", act="silu", after=rs_in[4])
    _, d_ada, m_ada, v_ada = _adamw(w_ada[0], g_w_ada, m_w_ada[0], v_w_ada[0], "adamw_w_ada")
    r_wdown, r_wup = _join_wait(j_ffn, d_ada, "ffn")
    r_wq, r_wkv, r_wbra, r_wbrb, r_wout = _join_wait(j_mix, d_ada, "mix")
    gq_p = _joined(*r_wq).T
    gq = jnp.concatenate([gq_p[:, : 2 * MLA_NOPE].reshape(MLA_Q_LORA, 2, MLA_NOPE), gq_p[:, 2 * MLA_NOPE :].reshape(MLA_Q_LORA, 2, MLA_ROPE)], axis=2)
    grads = {
        "c_ctx": g_c_ctx.reshape(D), "w_ada": g_w_ada[None], "b_ada": g_b_ada, "norm1_g": t_n1g,
        "mla_q_norm_g": t_qg, "w_q_up": gq.reshape(1, MLA_Q_LORA, -1), "mla_kv_norm_g": t_kvg, "w_kv_up": r_wkv,
        "gqa_q_norm_g": t_gq, "gqa_k_norm_g": t_kg, "w_br_a": r_wbra, "w_br_b": r_wbrb, "w_out": r_wout,
        "norm2_g": t_n2g, "w_up": r_wup, "conv_w": g_cw[None], "conv_b": t_cb, "w_down": r_wdown,
        "final_norm_g": t_fg.reshape(D),
    }
    arrives_transposed = ("w_kv_up", "w_br_a", "w_br_b", "w_up")
    arrives_halved = arrives_transposed + ("w_out", "w_down")
    weights = dict(c_ctx=c_ctx, w_ada=w_ada, b_ada=b_ada, norm1_g=norm1_g, w_in=w_in, mla_q_norm_g=mla_q_norm_g, w_q_up=w_q_up,
                   mla_kv_norm_g=mla_kv_norm_g, w_kv_up=w_kv_up, gqa_q_norm_g=gqa_q_norm_g, gqa_k_norm_g=gqa_k_norm_g, w_br_a=w_br_a,
                   w_br_b=w_br_b, w_out=w_out, norm2_g=norm2_g, w_up=w_up, conv_w=conv_w, conv_b=conv_b, w_down=w_down,
                   final_norm_g=final_norm_g)
    m_in = dict(c_ctx=m_c_ctx, w_ada=m_w_ada, b_ada=m_b_ada, norm1_g=m_norm1_g, w_in=m_w_in, mla_q_norm_g=m_mla_q_norm_g,
                w_q_up=m_w_q_up, mla_kv_norm_g=m_mla_kv_norm_g, w_kv_up=m_w_kv_up, gqa_q_norm_g=m_gqa_q_norm_g,
                gqa_k_norm_g=m_gqa_k_norm_g, w_br_a=m_w_br_a, w_br_b=m_w_br_b, w_out=m_w_out, norm2_g=m_norm2_g, w_up=m_w_up,
                conv_w=m_conv_w, conv_b=m_conv_b, w_down=m_w_down, final_norm_g=m_final_norm_g)
    v_in = dict(c_ctx=v_c_ctx, w_ada=v_w_ada, b_ada=v_b_ada, norm1_g=v_norm1_g, w_in=v_w_in, mla_q_norm_g=v_mla_q_norm_g,
                w_q_up=v_w_q_up, mla_kv_norm_g=v_mla_kv_norm_g, w_kv_up=v_w_kv_up, gqa_q_norm_g=v_gqa_q_norm_g,
                gqa_k_norm_g=v_gqa_k_norm_g, w_br_a=v_w_br_a, w_br_b=v_w_br_b, w_out=v_w_out, norm2_g=v_norm2_g, w_up=v_w_up,
                conv_w=v_conv_w, conv_b=v_conv_b, w_down=v_w_down, final_norm_g=v_final_norm_g)
    names = list(weights)
    big = [n for n in names if weights[n].ndim == 3 and weights[n].shape[1] >= 8]
    small = [n for n in names if n not in big]
    delta, new_m, new_v = {}, {}, {}

    def update(n):
        shp = weights[n].shape
        two_d = lambda a: a.reshape(shp[1], shp[2])
        g_t = n in arrives_transposed
        if n in arrives_halved:
            g_in, g_sib = grads[n]
        else:
            g_in, g_sib = two_d(grads[n].astype(F32)), None
        g_, d_, m_, v_ = _adamw(two_d(weights[n]), g_in, two_d(m_in[n]), two_d(v_in[n]), "adamw_" + n, g_transposed=g_t, g_sibling=g_sib)
        grads[n], delta[n], new_m[n], new_v[n] = g_.reshape(shp), d_.reshape(shp), m_.reshape(shp), v_.reshape(shp)

    delta["w_ada"], new_m["w_ada"], new_v["w_ada"] = d_ada[None], m_ada[None], v_ada[None]
    early = [n for n in big if n not in ("w_in", "w_ada")]
    for n in early[:-1]:
        update(n)
    done = sum(delta[n][0, 0:1, 0:1] for n in early[:-1])
    j_in = _join_start(_scatter_sums(rs_in, done, "in"), done, "in")
    last = early[-1]
    grads[last] = (grads[last][0] + j_in[4], grads[last][1])
    update(last)
    ((g_mine, g_sib),) = _join_wait(j_in, delta[last], "in")
    g_, d_, m_, v_ = _adamw(w_in[0].T, g_mine, m_w_in[0].T, v_w_in[0].T, "adamw_w_in", g_sibling=g_sib)
    grads["w_in"], delta["w_in"], new_m["w_in"], new_v["w_in"] = g_.T[None], d_.T[None], m_.T[None], v_.T[None]
    grads = {n: grads[n].reshape(weights[n].shape).astype(F32) for n in names}

    slab = lambda tree: [tree[n].reshape(-1, LANES) for n in small]
    d_, m_, v_ = _adamw_many(slab(weights), slab(grads), slab(m_in), slab(v_in), "adamw_small")
    for k, n in enumerate(small):
        shp = weights[n].shape
        delta[n], new_m[n], new_v[n] = d_[k].reshape(shp), m_[k].reshape(shp), v_[k].reshape(shp)

    return (loss, grad_x[None], *[grads[n] for n in names], *[delta[n] for n in names], *[new_m[n] for n in names],
            *[new_v[n] for n in names])
```

```python
import math

import jax
import jax.numpy as jnp
from jax import lax
from jax.experimental import pallas as pl
from jax.experimental.pallas import tpu as pltpu

F32 = jnp.float32
BF16 = jnp.bfloat16
MESH = pl.DeviceIdType.MESH

NORM_EPS = 1e-6
ROPE_THETA = 10000.0
GRID_W = 64
MLA_HEADS = 8
MLA_Q_LORA = 768
MLA_KV_LORA = 512
MLA_NOPE = 128
MLA_ROPE = 64
MLA_V = 128
GQA_HEADS = 8
GQA_KV_HEADS = 2
GQA_HEAD_DIM = 128
GQA_GROUP = GQA_HEADS // GQA_KV_HEADS
LANES = 128
KVP = MLA_KV_LORA + 2 * GQA_KV_HEADS * GQA_HEAD_DIM + LANES
QC = MLA_Q_LORA + GQA_HEADS * GQA_HEAD_DIM

ADAM_LR = 0.001
ADAM_B1 = 0.9
ADAM_B2 = 0.999
ADAM_EPS = 1e-08
ADAM_WD = 0.01
ADAM_STEP = 10

VMEM_LIMIT = 56 * 1024 * 1024


def _pick(dim, target, mult=LANES):
    t = (min(target, dim) // mult) * mult
    while t >= mult:
        if dim % t == 0:
            return t
        t -= mult
    return dim


def _params(sem):
    return pltpu.CompilerParams(dimension_semantics=sem, vmem_limit_bytes=VMEM_LIMIT)


_DIMS = {"NN": (((1,), (0,)), ((), ())), "NT": (((1,), (1,)), ((), ())), "TN": (((0,), (0,)), ((), ()))}


MM_VMEM_BUDGET = 36 * 1024 * 1024


def _mm_tiles(M, N, K, sa, sb, so, tm, tn, tk):
    tm, tn, tk = _pick(M, tm), _pick(N, tn), _pick(K, tk)

    def need(t):
        return 2 * (tm * t * sa + t * tn * sb) + 2 * tm * tn * so + (tm * tn * 4 if t < K else 0)

    while need(tk) > MM_VMEM_BUDGET and tk > LANES:
        smaller = _pick(K, tk - LANES)
        if smaller >= tk:
            break
        tk = smaller
    return tm, tn, tk


def _window(block, index, offsets):
    if not any(offsets):
        return pl.BlockSpec(block, index)
    for t, o in zip(block, offsets):
        assert o % 16 == 0 and t % 16 == 0, (block, offsets)

    def at(i, j, k):
        return tuple(pl.multiple_of(o + p * t, math.gcd(o, t)) for p, t, o in zip(index(i, j, k), block, offsets))

    return pl.BlockSpec(tuple(pl.Element(t) for t in block), at)


def _mm(a, b, mode, out_dtype, name, m=None, n=None, k=None, b_off=0, add=None, out_rows=None, out_base=None, out_off=0,
        tm=1024, tn=1024, tk=2304, act=None, bias=None, after=None):
    if mode == "NN":
        M, K, N = m or a.shape[0], k or a.shape[1], b.shape[1]
    elif mode == "NT":
        M, K, N = m or a.shape[0], a.shape[1], n or b.shape[0]
    else:
        M, K, N = a.shape[1], k or a.shape[0], b.shape[1]
    tm, tn, tk = _mm_tiles(M, N, K, a.dtype.itemsize, b.dtype.itemsize, jnp.dtype(out_dtype).itemsize, tm, tn, tk)
    nk = K // tk
    dims = _DIMS[mode]
    n_in = 2 + (bias is not None) + (add is not None) + (out_base is not None) + (after is not None)

    def body(*refs):
        a_ref, b_ref = refs[:2]
        bias_ref = refs[2] if bias is not None else None
        add_ref = refs[2 + (bias is not None)] if add is not None else None
        o_ref = refs[n_in]
        av = a_ref[...]
        if act == "silu":
            av = av * jax.nn.sigmoid(av)
        part = lax.dot_general(av.astype(BF16), b_ref[...].astype(BF16), dims, preferred_element_type=F32)

        def finish(r):
            if bias is not None:
                r = r + bias_ref[...]
            if add is not None:
                r = r + add_ref[...]
            o_ref[...] = r.astype(out_dtype)

        if nk == 1:
            finish(part)
            return
        acc = refs[-1]
        k = pl.program_id(2)

        @pl.when(k == 0)
        def _():
            acc[...] = part

        @pl.when(jnp.logical_and(k > 0, k < nk - 1))
        def _():
            acc[...] += part

        @pl.when(k == nk - 1)
        def _():
            finish(acc[...] + part)

    a_spec = pl.BlockSpec((tk, tm), lambda i, j, k: (k, i)) if mode == "TN" else pl.BlockSpec((tm, tk), lambda i, j, k: (i, k))
    if mode == "NT":
        b_spec = _window((tn, tk), lambda i, j, k: (j, k), (b_off, 0))
    else:
        b_spec = _window((tk, tn), lambda i, j, k: (k, j), (b_off, 0))
    in_specs, args = [a_spec, b_spec], [a, b]
    if bias is not None:
        in_specs.append(pl.BlockSpec((1, tn), lambda i, j, k: (0, j)))
        args.append(bias)
    if add is not None:
        in_specs.append(pl.BlockSpec((tm, tn), lambda i, j, k: (i, j)))
        args.append(add)
    aliases = {}
    if after is not None:
        in_specs.append(pl.BlockSpec(after.shape, lambda i, j, k: (0, 0)))
        args.append(after)
    if out_base is not None:
        aliases = {len(args): 0}
        in_specs.append(ANY)
        args.append(out_base)
        out_rows = out_base.shape[0]
    return pl.pallas_call(
        body,
        name=name,
        grid=(M // tm, N // tn, nk),
        in_specs=in_specs,
        out_specs=_window((tm, tn), lambda i, j, k: (i, j), (out_off, 0)),
        out_shape=jax.ShapeDtypeStruct((out_rows or M, N), out_dtype),
        input_output_aliases=aliases,
        scratch_shapes=[pltpu.VMEM((tm, tn), F32)] if nk > 1 else [],
        compiler_params=_params(("parallel", "parallel", "arbitrary")),
    )(*args)


def _rms(x):
    r = lax.rsqrt(jnp.mean(x * x, axis=-1, keepdims=True) + NORM_EPS)
    return x * r, r


def _rms_bwd(xh, r, dxh):
    return r * (dxh - xh * jnp.mean(dxh * xh, axis=-1, keepdims=True))


def _swap(x, q):
    lane = lax.broadcasted_iota(jnp.int32, x.shape, 1)
    even = ((lane // q) % 2) == 0
    return jnp.where(even, pltpu.roll(x, LANES - q, 1), pltpu.roll(x, q, 1))


def _rope(x, cos, ss, q):
    return x * cos + _swap(x, q) * ss


def _rope_t(d, cos, ss, q):
    return d * cos + _swap(d * ss, q)


def _csum(x):
    return jnp.sum(x, axis=0, keepdims=True)


def _rows(tr, w, off=0):
    return pl.BlockSpec((tr, w), lambda i: (i + off, 0))


def _bcast(w):
    return pl.BlockSpec((1, w), lambda i: (0, 0))


def _acc_init(i, refs):
    @pl.when(i == 0)
    def _():
        for r in refs:
            r[...] = jnp.zeros_like(r)


def _rope_tables(n_ctx, n_lat, rot_dim):
    rows = n_lat // GRID_W
    row = jnp.repeat(jnp.arange(rows, dtype=F32), GRID_W)
    col = jnp.tile(jnp.arange(GRID_W, dtype=F32), rows)
    half = rot_dim // 2
    inv_freq = ROPE_THETA ** (-jnp.arange(0, half, 2, dtype=F32) / half)
    ar, ac = row[:, None] * inv_freq, col[:, None] * inv_freq
    cos = jnp.concatenate([jnp.cos(ar), jnp.cos(ar), jnp.cos(ac), jnp.cos(ac)], axis=-1)
    ss = jnp.concatenate([-jnp.sin(ar), jnp.sin(ar), -jnp.sin(ac), jnp.sin(ac)], axis=-1)
    cos = jnp.tile(cos, (1, LANES // rot_dim))
    ss = jnp.tile(ss, (1, LANES // rot_dim))
    cos = jnp.concatenate([cos, jnp.ones((n_ctx, LANES), F32)], axis=0)
    ss = jnp.concatenate([ss, jnp.zeros((n_ctx, LANES), F32)], axis=0)
    return cos, ss


def _norm_mod_fwd(x2d, g, sh, sc, name, tr, out_rows=None, base=None, out_off=0):
    n, d = x2d.shape

    def body(x_ref, g_ref, sh_ref, sc_ref, *rest):
        xh, _ = _rms(x_ref[...])
        rest[-1][...] = ((xh * g_ref[...]) * (1.0 + sc_ref[...]) + sh_ref[...]).astype(BF16)

    args, in_specs, aliases = [x2d, g, sh, sc], [_rows(tr, d), _bcast(d), _bcast(d), _bcast(d)], {}
    if base is not None:
        args.append(base)
        in_specs.append(ANY)
        aliases = {4: 0}
        out_rows = base.shape[0]
    return pl.pallas_call(
        body,
        name=name,
        grid=(n // tr,),
        in_specs=in_specs,
        out_specs=_rows(tr, d, out_off // tr),
        out_shape=jax.ShapeDtypeStruct((out_rows or n, d), BF16),
        input_output_aliases=aliases,
        compiler_params=_params(("parallel",)),
    )(*args)


def _norm_mod_bwd(dz, dz_off, x2d, g, sc, dres, name, tr):
    n, d = x2d.shape
    want_dx = dres is not None

    def body(*refs):
        if want_dx:
            dz_ref, x_ref, g_ref, sc_ref, dres_ref, dx_ref, dg_ref, dsh_ref, dsc_ref = refs
        else:
            dz_ref, x_ref, g_ref, sc_ref, dg_ref, dsh_ref, dsc_ref = refs
        _acc_init(pl.program_id(0), [dg_ref, dsh_ref, dsc_ref])
        xh, r = _rms(x_ref[...])
        dzv = dz_ref[...]
        gv = g_ref[...]
        dsc_ref[...] += _csum(dzv * (xh * gv))
        dsh_ref[...] += _csum(dzv)
        dh = dzv * (1.0 + sc_ref[...])
        dg_ref[...] += _csum(dh * xh)
        if want_dx:
            dx_ref[...] = _rms_bwd(xh, r, dh * gv) + dres_ref[...]

    in_specs = [_rows(tr, d, dz_off), _rows(tr, d), _bcast(d), _bcast(d)]
    args = [dz, x2d, g, sc]
    out_specs = [_bcast(d)] * 3
    out_shape = [jax.ShapeDtypeStruct((1, d), F32)] * 3
    if want_dx:
        in_specs.append(_rows(tr, d))
        args.append(dres)
        out_specs = [_rows(tr, d)] + out_specs
        out_shape = [jax.ShapeDtypeStruct((n, d), F32)] + out_shape
    res = pl.pallas_call(
        body,
        name=name,
        grid=(n // tr,),
        in_specs=in_specs,
        out_specs=out_specs,
        out_shape=out_shape,
        compiler_params=_params(("arbitrary",)),
    )(*args)
    return res if want_dx else (None, *res)


_QA, _QB = MLA_ROPE // 4, GQA_HEAD_DIM // 4


def _kprep_fwd(pkv, kvg, kg, cos_a, ss_a, cos_b, ss_b, tr):
    n = pkv.shape[0]
    nb = GQA_KV_HEADS * GQA_HEAD_DIM

    def body(p_ref, kvg_ref, kg_ref, ca, sa, cb, sb, ckv_ref, kb_ref, vb_ref, kpe_ref):
        p = p_ref[...]
        xh, _ = _rms(p[:, :MLA_KV_LORA])
        ckv_ref[...] = (xh * kvg_ref[...]).astype(BF16)
        for e in range(GQA_KV_HEADS):
            lo = MLA_KV_LORA + e * GQA_HEAD_DIM
            kh, _ = _rms(p[:, lo : lo + GQA_HEAD_DIM])
            kb_ref[:, e * GQA_HEAD_DIM : (e + 1) * GQA_HEAD_DIM] = _rope(kh * kg_ref[...], cb[...], sb[...], _QB).astype(BF16)
        vb_ref[...] = p[:, MLA_KV_LORA + nb : MLA_KV_LORA + 2 * nb].astype(BF16)
        kr = _rope(p[:, MLA_KV_LORA + 2 * nb :], ca[...], sa[...], _QA)
        kpe_ref[:, :LANES] = kr.astype(BF16)
        kpe_ref[:, LANES:] = pltpu.roll(kr, MLA_ROPE, 1).astype(BF16)

    return pl.pallas_call(
        body,
        name="kprep_fwd",
        grid=(n // tr,),
        in_specs=[_rows(tr, KVP), _bcast(MLA_KV_LORA), _bcast(GQA_HEAD_DIM)] + [_rows(tr, LANES)] * 4,
        out_specs=[_rows(tr, MLA_KV_LORA), _rows(tr, nb), _rows(tr, nb), _rows(tr, 2 * LANES)],
        out_shape=[jax.ShapeDtypeStruct((n, w), BF16) for w in (MLA_KV_LORA, nb, nb, 2 * LANES)],
        compiler_params=_params(("parallel",)),
    )(pkv, kvg, kg, cos_a, ss_a, cos_b, ss_b)


def _kprep_bwd(pkv, dckv, dkb, dvb, dkpe, kvg, kg, cos_b, ss_b, tr):
    n = pkv.shape[0]
    nb = GQA_KV_HEADS * GQA_HEAD_DIM

    def body(p_ref, dckv_ref, dkb_ref, dvb_ref, dkpe_ref, kvg_ref, kg_ref, cb, sb, dp_ref, dkvg_ref, dkg_ref):
        _acc_init(pl.program_id(0), [dkvg_ref, dkg_ref])
        p = p_ref[...]
        xh, r = _rms(p[:, :MLA_KV_LORA])
        dn = dckv_ref[...]
        dkvg_ref[...] += _csum(dn * xh)
        dp_ref[:, :MLA_KV_LORA] = _rms_bwd(xh, r, dn * kvg_ref[...]).astype(BF16)
        for e in range(GQA_KV_HEADS):
            lo = MLA_KV_LORA + e * GQA_HEAD_DIM
            kh, rk = _rms(p[:, lo : lo + GQA_HEAD_DIM])
            dk = _rope_t(dkb_ref[:, e * GQA_HEAD_DIM : (e + 1) * GQA_HEAD_DIM], cb[...], sb[...], _QB)
            dkg_ref[...] += _csum(dk * kh)
            dp_ref[:, lo : lo + GQA_HEAD_DIM] = _rms_bwd(kh, rk, dk * kg_ref[...]).astype(BF16)
        dp_ref[:, MLA_KV_LORA + nb : MLA_KV_LORA + 2 * nb] = dvb_ref[...].astype(BF16)
        dp_ref[:, MLA_KV_LORA + 2 * nb :] = dkpe_ref[...].astype(BF16)

    return pl.pallas_call(
        body,
        name="kprep_bwd",
        grid=(n // tr,),
        in_specs=[_rows(tr, KVP), _rows(tr, MLA_KV_LORA), _rows(tr, nb), _rows(tr, nb), _rows(tr, LANES),
                  _bcast(MLA_KV_LORA), _bcast(GQA_HEAD_DIM), _rows(tr, LANES), _rows(tr, LANES)],
        out_specs=[_rows(tr, KVP), _bcast(MLA_KV_LORA), _bcast(GQA_HEAD_DIM)],
        out_shape=[jax.ShapeDtypeStruct((n, KVP), BF16), jax.ShapeDtypeStruct((1, MLA_KV_LORA), F32),
                   jax.ShapeDtypeStruct((1, GQA_HEAD_DIM), F32)],
        compiler_params=_params(("arbitrary",)),
    )(pkv, dckv, dkb, dvb, dkpe, kvg, kg, cos_b, ss_b)


def _kgrad_split(dka, dva, cos_a, ss_a, tr):
    n = dka.shape[0]
    wk = MLA_HEADS * 2 * LANES

    def body(dk_ref, dv_ref, ca, sa, dkv_ref, dkpe_ref):
        even = jnp.zeros((tr, LANES), F32)
        odd = jnp.zeros((tr, LANES), F32)
        for h in range(MLA_HEADS):
            dkv_ref[:, 2 * h * LANES : (2 * h + 1) * LANES] = dk_ref[:, 2 * h * LANES : (2 * h + 1) * LANES].astype(BF16)
            dkv_ref[:, (2 * h + 1) * LANES : (2 * h + 2) * LANES] = dv_ref[:, h * MLA_V : (h + 1) * MLA_V].astype(BF16)
            part = dk_ref[:, (2 * h + 1) * LANES : (2 * h + 2) * LANES]
            if h % 2 == 0:
                even = even + part
            else:
                odd = odd + part
        lane = lax.broadcasted_iota(jnp.int32, (tr, LANES), 1)
        low = lane < MLA_ROPE
        both = jnp.where(low, even, odd)
        tot = jnp.where(low, both + pltpu.roll(both, MLA_ROPE, 1), 0.0)
        dkpe_ref[...] = _rope_t(tot, ca[...], sa[...], _QA)

    return pl.pallas_call(
        body,
        name="kgrad_split",
        grid=(n // tr,),
        in_specs=[_rows(tr, wk), _rows(tr, MLA_HEADS * MLA_V), _rows(tr, LANES), _rows(tr, LANES)],
        out_specs=[_rows(tr, wk), _rows(tr, LANES)],
        out_shape=[jax.ShapeDtypeStruct((n, wk), BF16), jax.ShapeDtypeStruct((n, LANES), F32)],
        compiler_params=_params(("parallel",)),
    )(dka, dva, cos_a, ss_a)


def _qprep_fwd(pq, qg, gq, cos_b, ss_b, tr):
    n = pq.shape[0]
    nq = GQA_HEADS * GQA_HEAD_DIM

    def body(p_ref, qg_ref, gq_ref, cb, sb, cq_ref, qb_ref):
        xh, _ = _rms(p_ref[:, :MLA_Q_LORA])
        cq_ref[...] = (xh * qg_ref[...]).astype(BF16)
        for h in range(GQA_HEADS):
            lo = MLA_Q_LORA + h * GQA_HEAD_DIM
            qh, _ = _rms(p_ref[:, lo : lo + GQA_HEAD_DIM])
            qb_ref[:, h * GQA_HEAD_DIM : (h + 1) * GQA_HEAD_DIM] = _rope(qh * gq_ref[...], cb[...], sb[...], _QB).astype(BF16)

    return pl.pallas_call(
        body,
        name="qprep_fwd",
        grid=(n // tr,),
        in_specs=[_rows(tr, QC), _bcast(MLA_Q_LORA), _bcast(GQA_HEAD_DIM), _rows(tr, LANES), _rows(tr, LANES)],
        out_specs=[_rows(tr, MLA_Q_LORA), _rows(tr, nq)],
        out_shape=[jax.ShapeDtypeStruct((n, MLA_Q_LORA), BF16), jax.ShapeDtypeStruct((n, nq), BF16)],
        compiler_params=_params(("parallel",)),
    )(pq, qg, gq, cos_b, ss_b)


def _qprep_bwd(pq, dcq, dqb, qg, gq, cos_b, ss_b, tr):
    n = pq.shape[0]
    nq = GQA_HEADS * GQA_HEAD_DIM

    def body(p_ref, dcq_ref, dqb_ref, qg_ref, gq_ref, cb, sb, dp_ref, dqg_ref, dgq_ref):
        _acc_init(pl.program_id(0), [dqg_ref, dgq_ref])
        xh, r = _rms(p_ref[:, :MLA_Q_LORA])
        dn = dcq_ref[...]
        dqg_ref[...] += _csum(dn * xh)
        dp_ref[:, :MLA_Q_LORA] = _rms_bwd(xh, r, dn * qg_ref[...]).astype(BF16)
        for h in range(GQA_HEADS):
            lo = MLA_Q_LORA + h * GQA_HEAD_DIM
            qh, rq = _rms(p_ref[:, lo : lo + GQA_HEAD_DIM])
            dq = _rope_t(dqb_ref[:, h * GQA_HEAD_DIM : (h + 1) * GQA_HEAD_DIM], cb[...], sb[...], _QB)
            dgq_ref[...] += _csum(dq * qh)
            dp_ref[:, lo : lo + GQA_HEAD_DIM] = _rms_bwd(qh, rq, dq * gq_ref[...]).astype(BF16)

    return pl.pallas_call(
        body,
        name="qprep_bwd",
        grid=(n // tr,),
        in_specs=[_rows(tr, QC), _rows(tr, MLA_Q_LORA), _rows(tr, nq), _bcast(MLA_Q_LORA), _bcast(GQA_HEAD_DIM),
                  _rows(tr, LANES), _rows(tr, LANES)],
        out_specs=[_rows(tr, QC), _bcast(MLA_Q_LORA), _bcast(GQA_HEAD_DIM)],
        out_shape=[jax.ShapeDtypeStruct((n, QC), BF16), jax.ShapeDtypeStruct((1, MLA_Q_LORA), F32),
                   jax.ShapeDtypeStruct((1, GQA_HEAD_DIM), F32)],
        compiler_params=_params(("arbitrary",)),
    )(pq, dcq, dqb, qg, gq, cos_b, ss_b)


_QA_COLS = MLA_HEADS * (MLA_NOPE + MLA_ROPE)


def _qrope_fwd(qa, cos_a, ss_a, tr):
    n = qa.shape[0]

    def body(q_ref, ca, sa, o_ref):
        for j in range(MLA_HEADS // 2):
            lo = 3 * j * LANES
            o_ref[:, lo : lo + 2 * LANES] = q_ref[:, lo : lo + 2 * LANES].astype(BF16)
            o_ref[:, lo + 2 * LANES : lo + 3 * LANES] = _rope(q_ref[:, lo + 2 * LANES : lo + 3 * LANES], ca[...], sa[...], _QA).astype(BF16)

    return pl.pallas_call(
        body,
        name="qrope_fwd",
        grid=(n // tr,),
        in_specs=[_rows(tr, _QA_COLS), _rows(tr, LANES), _rows(tr, LANES)],
        out_specs=_rows(tr, _QA_COLS),
        out_shape=jax.ShapeDtypeStruct((n, _QA_COLS), BF16),
        compiler_params=_params(("parallel",)),
    )(qa, cos_a, ss_a)


def _qrope_bwd(dq2, cos_a, ss_a, tr):
    n = dq2.shape[0]

    def body(d_ref, ca, sa, o_ref):
        for j in range(MLA_HEADS // 2):
            lo = 3 * j * LANES
            h0, h1 = 2 * j, 2 * j + 1
            o_ref[:, lo : lo + LANES] = d_ref[:, 2 * h0 * LANES : (2 * h0 + 1) * LANES].astype(BF16)
            o_ref[:, lo + LANES : lo + 2 * LANES] = d_ref[:, 2 * h1 * LANES : (2 * h1 + 1) * LANES].astype(BF16)
            pe = d_ref[:, (2 * h0 + 1) * LANES : (2 * h0 + 2) * LANES] + d_ref[:, (2 * h1 + 1) * LANES : (2 * h1 + 2) * LANES]
            o_ref[:, lo + 2 * LANES : lo + 3 * LANES] = _rope_t(pe, ca[...], sa[...], _QA).astype(BF16)

    return pl.pallas_call(
        body,
        name="qrope_bwd",
        grid=(n // tr,),
        in_specs=[_rows(tr, MLA_HEADS * 2 * LANES), _rows(tr, LANES), _rows(tr, LANES)],
        out_specs=_rows(tr, _QA_COLS),
        out_shape=jax.ShapeDtypeStruct((n, _QA_COLS), BF16),
        compiler_params=_params(("parallel",)),
    )(dq2, cos_a, ss_a)


def _cat(refs):
    vals = [r[...] for r in refs]
    return vals[0] if len(vals) == 1 else jnp.concatenate(vals, axis=-1)


LOG2E = 1.4426950408889634


def _attn_fwd(qparts, kparts, vpart, n_heads, group, dv, scale, name, tq, after=None):
    T, Tk = qparts[0][0].shape[0], kparts[0][0].shape[0]
    nq_, nk_ = len(qparts), len(kparts)
    sub = min(tq, 256)
    c2 = scale * LOG2E

    def body(*refs):
        q_refs, k_refs = refs[:nq_], refs[nq_ : nq_ + nk_]
        v_ref = refs[nq_ + nk_]
        o_ref, lse_ref = refs[-2:]
        k = _cat(k_refs)
        v = v_ref[...]
        for r0 in range(0, tq, sub):
            q = _cat([r.at[r0 : r0 + sub, :] for r in q_refs])
            s = lax.dot_general(q, k, _DIMS["NT"], preferred_element_type=F32)
            m = jnp.max(s, axis=-1, keepdims=True)
            p = jnp.exp2((s - m) * c2)
            l = jnp.sum(p, axis=-1, keepdims=True)
            acc = jnp.dot(p.astype(BF16), v, preferred_element_type=F32)
            o_ref[r0 : r0 + sub, :] = (acc * (1.0 / l)).astype(BF16)
            lse_ref[r0 : r0 + sub, :] = m * scale + jnp.log(l)

    in_specs = [pl.BlockSpec((tq, LANES), lambda h, i, f=f: (i, f(h))) for _, f in qparts]
    in_specs += [pl.BlockSpec((Tk, LANES), lambda h, i, f=f: (0, f(h // group))) for _, f in kparts]
    fv = vpart[1]
    in_specs.append(pl.BlockSpec((Tk, dv), lambda h, i: (0, fv(h // group))))
    args = [*[a for a, _ in qparts], *[a for a, _ in kparts], vpart[0]]
    if after is not None:
        in_specs.append(pl.BlockSpec(after.shape, lambda h, i: (0, 0)))
        args.append(after)
    return pl.pallas_call(
        body,
        name=name,
        grid=(n_heads, T // tq),
        in_specs=in_specs,
        out_specs=[pl.BlockSpec((tq, dv), lambda h, i: (i, h)), pl.BlockSpec((None, tq, 1), lambda h, i: (h, i, 0))],
        out_shape=[jax.ShapeDtypeStruct((T, n_heads * dv), BF16), jax.ShapeDtypeStruct((n_heads, T, 1), F32)],
        compiler_params=_params(("parallel", "parallel")),
    )(*args)


def _attn_bwd(qparts, kparts, vpart, o, do, lse, n_heads, group, dv, scale, name, tq):
    T, Tk = qparts[0][0].shape[0], kparts[0][0].shape[0]
    nq_, nk_ = len(qparts), len(kparts)
    dk_ = LANES * nq_
    n_kv = n_heads // group
    nblk = T // tq
    c2 = scale * LOG2E

    def head(hk, i):
        return hk * group + i // nblk

    sub = min(tq, 256)

    def body(*refs):
        q_refs = refs[:nq_]
        k = _cat(refs[nq_ : nq_ + nk_])
        v_ref, o_ref, do_ref, lse_ref, dq_ref, dk_ref, dv_ref = refs[nq_ + nk_ :]
        i = pl.program_id(1)
        _acc_init(i, [dk_ref, dv_ref])
        v = v_ref[...]
        dk_acc, dv_acc = None, None
        for r0 in range(0, tq, sub):
            rows = slice(r0, r0 + sub)
            q = _cat([r.at[rows, :] for r in q_refs])
            s = lax.dot_general(q, k, _DIMS["NT"], preferred_element_type=F32)
            p = jnp.exp2(s * c2 - lse_ref[rows, :] * LOG2E)
            dov = do_ref[rows, :]
            dp = lax.dot_general(dov, v, _DIMS["NT"], preferred_element_type=F32)
            delta = jnp.sum(dov.astype(F32) * o_ref[rows, :].astype(F32), axis=-1, keepdims=True)
            ds = (p * (dp - delta)).astype(BF16)
            dq_ref[rows, :] = jnp.dot(ds, k, preferred_element_type=F32) * scale
            dk_part = lax.dot_general(ds, q, _DIMS["TN"], preferred_element_type=F32)
            dv_part = lax.dot_general(p.astype(BF16), dov, _DIMS["TN"], preferred_element_type=F32)
            dk_acc = dk_part if dk_acc is None else dk_acc + dk_part
            dv_acc = dv_part if dv_acc is None else dv_acc + dv_part
        dk_ref[...] += dk_acc
        dv_ref[...] += dv_acc

        @pl.when(i == group * nblk - 1)
        def _():
            dk_ref[...] *= scale

    in_specs = [pl.BlockSpec((tq, LANES), lambda hk, i, f=f: (i % nblk, f(head(hk, i)))) for _, f in qparts]
    in_specs += [pl.BlockSpec((Tk, LANES), lambda hk, i, f=f: (0, f(hk))) for _, f in kparts]
    fv = vpart[1]
    in_specs.append(pl.BlockSpec((Tk, dv), lambda hk, i: (0, fv(hk))))
    in_specs += [pl.BlockSpec((tq, dv), lambda hk, i: (i % nblk, head(hk, i)))] * 2
    in_specs.append(pl.BlockSpec((None, tq, 1), lambda hk, i: (head(hk, i), i % nblk, 0)))
    return pl.pallas_call(
        body,
        name=name,
        grid=(n_kv, group * nblk),
        in_specs=in_specs,
        out_specs=[pl.BlockSpec((tq, dk_), lambda hk, i: (i % nblk, head(hk, i))),
                   pl.BlockSpec((Tk, dk_), lambda hk, i: (0, hk)),
                   pl.BlockSpec((Tk, dv), lambda hk, i: (0, hk))],
        out_shape=[jax.ShapeDtypeStruct((T, n_heads * dk_), F32), jax.ShapeDtypeStruct((Tk, n_kv * dk_), F32),
                   jax.ShapeDtypeStruct((Tk, n_kv * dv), F32)],
        compiler_params=_params(("parallel", "arbitrary")),
    )(*[a for a, _ in qparts], *[a for a, _ in kparts], vpart[0], o, do, lse)


def _gates_fwd(pg, ya, yb, tr):
    n, d = ya.shape

    def body(pg_ref, ya_ref, yb_ref, o_ref):
        ga = jax.nn.sigmoid(pg_ref[:, :d].astype(F32))
        gb = jax.nn.sigmoid(pg_ref[:, d:].astype(F32))
        o_ref[...] = (ga * ya_ref[...].astype(F32) + gb * yb_ref[...].astype(F32)).astype(BF16)

    return pl.pallas_call(
        body,
        name="gates_fwd",
        grid=(n // tr,),
        in_specs=[_rows(tr, 2 * d), _rows(tr, d), _rows(tr, d)],
        out_specs=_rows(tr, d),
        out_shape=jax.ShapeDtypeStruct((n, d), BF16),
        compiler_params=_params(("parallel",)),
    )(pg, ya, yb)


def _gates_bwd(dm, pg, ya, yb, tr):
    n, d = ya.shape

    def body(dm_ref, pg_ref, ya_ref, yb_ref, dya_ref, dyb_ref, dpg_ref):
        dmv = dm_ref[...].astype(F32)
        ga = jax.nn.sigmoid(pg_ref[:, :d].astype(F32))
        gb = jax.nn.sigmoid(pg_ref[:, d:].astype(F32))
        dya_ref[...] = (dmv * ga).astype(BF16)
        dyb_ref[...] = (dmv * gb).astype(BF16)
        dpg_ref[:, :d] = (dmv * ya_ref[...].astype(F32) * ga * (1.0 - ga)).astype(BF16)
        dpg_ref[:, d:] = (dmv * yb_ref[...].astype(F32) * gb * (1.0 - gb)).astype(BF16)

    return pl.pallas_call(
        body,
        name="gates_bwd",
        grid=(n // tr,),
        in_specs=[_rows(tr, d), _rows(tr, 2 * d), _rows(tr, d), _rows(tr, d)],
        out_specs=[_rows(tr, d), _rows(tr, d), _rows(tr, 2 * d)],
        out_shape=[jax.ShapeDtypeStruct((n, d), BF16), jax.ShapeDtypeStruct((n, d), BF16), jax.ShapeDtypeStruct((n, 2 * d), BF16)],
        compiler_params=_params(("parallel",)),
    )(dm, pg, ya, yb)


def _resid_norm2_fwd(x2d, att, g1, n2g, sh2, sc2, tr):
    n, d = x2d.shape

    def body(x_ref, a_ref, g1_ref, g_ref, sh_ref, sc_ref, x1_ref, z_ref):
        x1 = x_ref[...] + g1_ref[...] * a_ref[...]
        x1_ref[...] = x1
        xh, _ = _rms(x1)
        z_ref[...] = ((xh * g_ref[...]) * (1.0 + sc_ref[...]) + sh_ref[...]).astype(BF16)

    return pl.pallas_call(
        body,
        name="resid_norm2_fwd",
        grid=(n // tr,),
        in_specs=[_rows(tr, d), _rows(tr, d)] + [_bcast(d)] * 4,
        out_specs=[_rows(tr, d), _rows(tr, d)],
        out_shape=[jax.ShapeDtypeStruct((n, d), F32), jax.ShapeDtypeStruct((n, d), BF16)],
        compiler_params=_params(("parallel",)),
    )(x2d, att, g1, n2g, sh2, sc2)


def _resid_norm2_bwd(dz2, x1, dx2, att, n2g, sc2, g1, tr):
    n, d = x1.shape

    def body(dz_ref, x1_ref, dx2_ref, a_ref, g_ref, sc_ref, g1_ref, dx1_ref, da_ref, dg_ref, dsh_ref, dsc_ref, dg1_ref):
        _acc_init(pl.program_id(0), [dg_ref, dsh_ref, dsc_ref, dg1_ref])
        xh, r = _rms(x1_ref[...])
        dzv = dz_ref[...]
        gv = g_ref[...]
        dsc_ref[...] += _csum(dzv * (xh * gv))
        dsh_ref[...] += _csum(dzv)
        dh = dzv * (1.0 + sc_ref[...])
        dg_ref[...] += _csum(dh * xh)
        dx1 = _rms_bwd(xh, r, dh * gv) + dx2_ref[...]
        dx1_ref[...] = dx1
        dg1_ref[...] += _csum(dx1 * a_ref[...])
        da_ref[...] = (dx1 * g1_ref[...]).astype(BF16)

    return pl.pallas_call(
        body,
        name="resid_norm2_bwd",
        grid=(n // tr,),
        in_specs=[_rows(tr, d)] * 4 + [_bcast(d)] * 3,
        out_specs=[_rows(tr, d), _rows(tr, d)] + [_bcast(d)] * 4,
        out_shape=[jax.ShapeDtypeStruct((n, d), F32), jax.ShapeDtypeStruct((n, d), BF16)] + [jax.ShapeDtypeStruct((1, d), F32)] * 4,
        compiler_params=_params(("arbitrary",)),
    )(dz2, x1, dx2, att, n2g, sc2, g1)


def _edges(shape):
    row = lax.broadcasted_iota(jnp.int32, shape, 0)
    return row == 0, row == shape[0] - 1


def _shifts(u, edges):
    n = u.shape[0]
    return jnp.where(edges[0], 0.0, pltpu.roll(u, 1, 0)), jnp.where(edges[1], 0.0, pltpu.roll(u, n - 1, 0))


def _conv3(u, prev, nxt, w_ref, b_ref):
    return b_ref[...] + w_ref[0:1, :] * prev + w_ref[1:2, :] * u + w_ref[2:3, :] * nxt


def _ffn_up_conv(z, wup_t, cw, cb, tc, after):
    n, d = z.shape
    f = wup_t.shape[0] // 2
    nb = f // tc

    def body(z_ref, wa_ref, wb_ref, cwa, cwb, cba, cbb, after_ref, ua_ref, ub_ref, h_ref):
        w = jnp.concatenate([wa_ref[...], wb_ref[...]], axis=0)
        u = lax.dot_general(z_ref[...], w, _DIMS["NT"], preferred_element_type=F32).astype(BF16)
        ua_ref[...] = u[:, :tc]
        ub_ref[...] = u[:, tc:]
        edges = _edges((n, tc))
        ua = u[:, :tc].astype(F32)
        ub = u[:, tc:].astype(F32)
        a = _conv3(ua, *_shifts(ua, edges), cwa, cba)
        b = _conv3(ub, *_shifts(ub, edges), cwb, cbb)
        h_ref[...] = (a * jax.nn.sigmoid(a) * b).astype(BF16)

    col = lambda rows, off: pl.BlockSpec((rows, tc), lambda i: (0, i + off))
    w_rows = lambda off: pl.BlockSpec((tc, d), lambda i: (i + off, 0))
    return pl.pallas_call(
        body,
        name="ffn_up_conv",
        grid=(nb,),
        in_specs=[pl.BlockSpec((n, d), lambda i: (0, 0)), w_rows(0), w_rows(nb), col(3, 0), col(3, nb), col(1, 0), col(1, nb),
                  pl.BlockSpec(after.shape, lambda i: (0, 0))],
        out_specs=[col(n, 0)] * 3,
        out_shape=[jax.ShapeDtypeStruct((n, f), BF16)] * 3,
        compiler_params=_params(("parallel",)),
    )(z, wup_t, wup_t, cw, cw, cb, cb, after)


def _ffn_down_dx_conv_bwd(df, wdown, u_a, u_b, cw, cb, tc, after):
    n, f = u_a.shape
    d = df.shape[1]
    nb = f // tc

    def part(uv, prev, nxt, duc, edges, w_ref, du_ref, dw_ref, db_ref):
        db_ref[...] = _csum(duc)
        dw_ref[0:1, :] = _csum(duc * prev)
        dw_ref[1:2, :] = _csum(duc * uv)
        dw_ref[2:3, :] = _csum(duc * nxt)
        d_prev, d_next = _shifts(duc, edges)
        du_ref[...] = (w_ref[0:1, :] * d_next + w_ref[1:2, :] * duc + w_ref[2:3, :] * d_prev).astype(BF16)

    def body(df_ref, wd_ref, ua_ref, ub_ref, wa_ref, wb_ref, ba_ref, bb_ref, after_ref,
             dua_ref, dub_ref, dwa_ref, dwb_ref, dba_ref, dbb_ref):
        dhv = lax.dot_general(df_ref[...], wd_ref[...], _DIMS["NT"], preferred_element_type=F32)
        dhv = dhv.astype(BF16).astype(F32)
        edges = _edges((n, tc))
        ua = ua_ref[...].astype(F32)
        ub = ub_ref[...].astype(F32)
        sa = _shifts(ua, edges)
        sb = _shifts(ub, edges)
        a = _conv3(ua, *sa, wa_ref, ba_ref)
        b = _conv3(ub, *sb, wb_ref, bb_ref)
        sg = jax.nn.sigmoid(a)
        da = dhv * b * (sg * (1.0 + a * (1.0 - sg)))
        db = dhv * (a * sg)
        part(ua, *sa, da, edges, wa_ref, dua_ref, dwa_ref, dba_ref)
        part(ub, *sb, db, edges, wb_ref, dub_ref, dwb_ref, dbb_ref)

    col = lambda rows, off: pl.BlockSpec((rows, tc), lambda i: (0, i + off))
    return pl.pallas_call(
        body,
        name="ffn_down_dx_conv_bwd",
        grid=(nb,),
        in_specs=[pl.BlockSpec((n, d), lambda i: (0, 0)), pl.BlockSpec((tc, d), lambda i: (i, 0)), col(n, 0), col(n, 0),
                  col(3, 0), col(3, nb), col(1, 0), col(1, nb), pl.BlockSpec(after.shape, lambda i: (0, 0))],
        out_specs=[col(n, 0), col(n, 0), col(3, 0), col(3, 0), col(1, 0), col(1, 0)],
        out_shape=[jax.ShapeDtypeStruct((n, f), BF16)] * 2 + [jax.ShapeDtypeStruct((3, f), F32)] * 2 + [jax.ShapeDtypeStruct((1, f), F32)] * 2,
        compiler_params=_params(("parallel",)),
    )(df, wdown, u_a, u_b, cw, cw, cb, cb, after)


def _loss_head(x1, f, g2, fg, tgt, tr):
    n, d = x1.shape

    def body(x1_ref, f_ref, g2_ref, fg_ref, t_ref, sq_ref, dx2_ref, dfg_ref, dg2_ref, df_ref):
        _acc_init(pl.program_id(0), [sq_ref, dfg_ref, dg2_ref])
        fv = f_ref[...]
        xh, r = _rms(x1_ref[...] + g2_ref[...] * fv)
        err = xh * fg_ref[...] - t_ref[...]
        sq_ref[...] += _csum(err * err)
        dy = err * (1.0 / d)
        dfg_ref[...] += _csum(dy * xh)
        dx2 = _rms_bwd(xh, r, dy * fg_ref[...])
        dx2_ref[...] = dx2
        dg2_ref[...] += _csum(dx2 * fv)
        df_ref[...] = (dx2 * g2_ref[...]).astype(BF16)

    return pl.pallas_call(
        body,
        name="loss_head",
        grid=(n // tr,),
        in_specs=[_rows(tr, d), _rows(tr, d), _bcast(d), _bcast(d), _rows(tr, d)],
        out_specs=[_bcast(d), _rows(tr, d), _bcast(d), _bcast(d), _rows(tr, d)],
        out_shape=[jax.ShapeDtypeStruct((1, d), F32), jax.ShapeDtypeStruct((n, d), F32), jax.ShapeDtypeStruct((1, d), F32),
                   jax.ShapeDtypeStruct((1, d), F32), jax.ShapeDtypeStruct((n, d), BF16)],
        compiler_params=_params(("arbitrary",)),
    )(x1, f, g2, fg, tgt)


def _sum_slots(g, name):
    s, r, w = g.shape

    def body(g_ref, o_ref):
        acc = g_ref[0]
        for k in range(1, s):
            acc = acc + g_ref[k]
        o_ref[...] = acc

    return pl.pallas_call(body, name=name, out_shape=jax.ShapeDtypeStruct((r, w), F32))(g)


def _silu_grad_mul(ds, cvec):
    def body(d_ref, c_ref, o_ref):
        cv = c_ref[...]
        sg = jax.nn.sigmoid(cv)
        o_ref[...] = d_ref[...] * (sg * (1.0 + cv * (1.0 - sg)))

    return pl.pallas_call(body, name="silu_grad_mul", out_shape=jax.ShapeDtypeStruct(ds.shape, F32))(ds, cvec)


def _adamw_update(wv, gv, mv, vv, d_ref, mo_ref, vo_ref):
    mn = ADAM_B1 * mv + (1.0 - ADAM_B1) * gv
    vn = ADAM_B2 * vv + (1.0 - ADAM_B2) * (gv * gv)
    mo_ref[...] = mn
    vo_ref[...] = vn
    m_hat = mn / (1.0 - ADAM_B1**ADAM_STEP)
    v_hat = vn / (1.0 - ADAM_B2**ADAM_STEP)
    d_ref[...] = -ADAM_LR * (m_hat / (jnp.sqrt(v_hat) + ADAM_EPS) + ADAM_WD * wv)


def _adamw_many(ws, gs, ms, vs, name):
    n = len(ws)

    def body(*refs):
        for k in range(n):
            w_ref, g_ref, m_ref, v_ref = (refs[q * n + k] for q in range(4))
            d_ref, mo_ref, vo_ref = (refs[(4 + q) * n + k] for q in range(3))
            _adamw_update(w_ref[...], g_ref[...], m_ref[...], v_ref[...], d_ref, mo_ref, vo_ref)

    res = pl.pallas_call(body, name=name, out_shape=[jax.ShapeDtypeStruct(w.shape, F32) for w in ws] * 3)(*ws, *gs, *ms, *vs)
    return res[:n], res[n : 2 * n], res[2 * n :]


def _adamw(w, g, m, v, name, g_transposed=False, g_sibling=None):
    r, cdim = w.shape
    halves = g_sibling is not None
    block = 1 << 19
    if g_transposed:
        tc = _pick(cdim // 2 if halves else cdim, 2048)
        tr = _pick(r, max(LANES, block // tc), LANES)
        per_half = (cdim // 2) // tc
    else:
        rows = r // 2 if halves else r
        tc = _pick(cdim, 2048)
        tr = _pick(rows, max(8, block // tc), 8)
        if tr < 64 and rows > 64:
            tr, tc = _pick(rows, 1024, 8), _pick(cdim, 512)
        per_half = (r // 2) // tr
    emit_g = g_transposed or halves

    def body(w_ref, g_ref, *rest):
        m_ref, v_ref = rest[halves : halves + 2]
        outs = rest[halves + 2 :]
        gv = g_ref[...]
        if halves:
            along = pl.program_id(1 if g_transposed else 0)
            gv = jnp.where(along // per_half == lax.axis_index("c"), gv, rest[0][...])
        if g_transposed:
            gv = gv.T
        if emit_g:
            outs[0][...] = gv
        _adamw_update(w_ref[...], gv, m_ref[...], v_ref[...], *outs[-3:])

    spec = pl.BlockSpec((tr, tc), lambda i, j: (i, j))
    if g_transposed:
        g_spec = pl.BlockSpec((tc, tr), lambda i, j: (j % per_half if halves else j, i))
    else:
        g_spec = pl.BlockSpec((tr, tc), lambda i, j: (i % per_half if halves else i, j))
    n_out = 3 + emit_g
    res = pl.pallas_call(
        body,
        name=name,
        grid=(r // tr, cdim // tc),
        in_specs=[spec, g_spec] + [g_spec] * halves + [spec, spec],
        out_specs=[spec] * n_out,
        out_shape=[jax.ShapeDtypeStruct((r, cdim), F32)] * n_out,
        compiler_params=_params(("parallel", "parallel")),
    )(w, g, *([g_sibling] if halves else []), m, v)
    return res if emit_g else [g, *res]


def _place():
    return lax.axis_index("x"), lax.axis_index("y"), lax.axis_index("c")


def _remote(src, dst, send_sem, recv_sem, dev):
    return pltpu.make_async_remote_copy(src_ref=src, dst_ref=dst, send_sem=send_sem, recv_sem=recv_sem, device_id=dev, device_id_type=MESH)


ANY = pl.BlockSpec(memory_space=pl.ANY)


def _all_gather_small(v, name, after=()):
    r, w = v.shape

    def body(v_ref, *rest):
        o_ref, send, recv, lsem = rest[len(after) :]
        x, y, c = _place()
        me = 4 * x + 2 * y + c
        mine = pltpu.make_async_copy(v_ref, o_ref.at[me], lsem)
        mine.start()
        sent = []
        for k in range(1, 8):
            px, py, pc = x ^ (k >> 2), y ^ ((k >> 1) & 1), c ^ (k & 1)
            cp = _remote(v_ref, o_ref.at[me], send.at[k - 1], recv.at[k - 1], (px, py, pc))
            cp.start()
            sent.append(cp)
        for k in range(1, 8):
            px, py, pc = x ^ (k >> 2), y ^ ((k >> 1) & 1), c ^ (k & 1)
            slot = o_ref.at[4 * px + 2 * py + pc]
            _remote(slot, slot, send.at[k - 1], recv.at[k - 1], (x, y, c)).wait_recv()
        for cp in sent:
            cp.wait_send()
        mine.wait()

    return pl.pallas_call(
        body,
        name=name,
        out_shape=jax.ShapeDtypeStruct((8, r, w), F32),
        in_specs=[pl.BlockSpec(memory_space=pltpu.VMEM)] + [ANY] * len(after),
        out_specs=pl.BlockSpec(memory_space=pltpu.VMEM),
        scratch_shapes=[pltpu.SemaphoreType.DMA((7,)), pltpu.SemaphoreType.DMA((7,)), pltpu.SemaphoreType.DMA],
        compiler_params=pltpu.CompilerParams(vmem_limit_bytes=VMEM_LIMIT),
    )(v, *after)


HBM = pl.BlockSpec(memory_space=pltpu.HBM)
SEM = pl.BlockSpec(memory_space=pltpu.SEMAPHORE)
EFFECT = pltpu.SideEffectType.DATAFLOW_SIDE_EFFECTING


def _other_chips(x, y):
    return [(1 - x, y), (x, 1 - y), (1 - x, 1 - y)]


def _bulk_start(name, srcs, land_shapes, n_copies, copies, after, lands_init=None):
    n, m = len(srcs), len(land_shapes)

    def body(*refs):
        src_refs, land_refs = refs[:n], refs[n : n + m]
        send, recv = refs[n + m + 1], refs[n + m + 2]
        token = refs[-1]
        for k, (s, d, dev) in enumerate(copies(src_refs, land_refs)):
            _remote(s, d, send.at[k], recv.at[k], dev).start()
        token[...] = jnp.zeros_like(token)

    lands = lands_init or [lax.empty(s.shape, s.dtype) for s in land_shapes]
    lands = [pltpu.with_memory_space_constraint(b, pltpu.HBM) for b in lands]
    out = pl.pallas_call(
        body,
        name=name,
        out_shape=(pltpu.SemaphoreType.DMA((n_copies,)), pltpu.SemaphoreType.DMA((n_copies,)),
                   *[pltpu.HBM(s.shape, s.dtype) for s in srcs], *[pltpu.HBM(s.shape, s.dtype) for s in land_shapes],
                   jax.ShapeDtypeStruct((8, LANES), F32)),
        in_specs=[HBM] * (n + m) + [ANY],
        out_specs=(SEM, SEM, *[HBM] * (n + m), pl.BlockSpec(memory_space=pltpu.VMEM)),
        input_output_aliases={i: 2 + i for i in range(n + m)},
        compiler_params=pltpu.CompilerParams(has_side_effects=EFFECT),
    )(*[pltpu.with_memory_space_constraint(s, pltpu.HBM) for s in srcs], *lands, after)
    return out[0], out[1], list(out[2 : 2 + n]), list(out[2 + n : 2 + n + m]), out[-1][0:1, 0:1]


def _bulk_wait(name, send, recv, srcs, lands, after, waits):
    n, m = len(srcs), len(lands)

    def body(*refs):
        src_refs, land_refs = refs[:n], refs[n : n + m]
        send_sem, recv_sem = refs[n + m], refs[n + m + 1]
        x, y, c = _place()
        for k, (s, d) in enumerate(waits(src_refs, land_refs)):
            cp = _remote(s, d, send_sem.at[k], recv_sem.at[k], (x, y, c))
            cp.wait_send()
            cp.wait_recv()

    out = pl.pallas_call(
        body,
        name=name,
        out_shape=tuple(pltpu.HBM(s.shape, s.dtype) for s in (*srcs, *lands)),
        in_specs=[HBM] * (n + m) + [SEM, SEM, ANY],
        out_specs=tuple([HBM] * (n + m)),
        input_output_aliases={i: i for i in range(n + m)},
        compiler_params=pltpu.CompilerParams(has_side_effects=EFFECT),
    )(*srcs, *lands, send, recv, after)
    return list(out[:n]), list(out[n:])


def _peers(x, y, c):
    return [(x ^ (k >> 2), y ^ ((k >> 1) & 1), c ^ (k & 1)) for k in range(1, 8)]


def _small_gather_start(v, after, name):
    r, w = v.shape

    def copies(src, land):
        x, y, c = _place()
        return [(src[0], land[0].at[4 * x + 2 * y + c], peer) for peer in _peers(x, y, c)]

    me = 4 * lax.axis_index("x") + 2 * lax.axis_index("y") + lax.axis_index("c")
    init = [lax.dynamic_update_slice(lax.empty((8, r, w), F32), v[None], (me, 0, 0))]
    return _bulk_start(name, [v], [jax.ShapeDtypeStruct((8, r, w), F32)], 7, copies, after, init)


def _small_gather_wait(started, after, name):
    send, recv, srcs, lands, _ = started

    def waits(src, land):
        x, y, c = _place()
        return [(src[0], land[0].at[4 * px + 2 * py + pc]) for px, py, pc in _peers(x, y, c)]

    return _bulk_wait(name, send, recv, srcs, lands, after, waits)[1][0]


def _gather_start(shards, after, name):
    def copies(src, land):
        x, y, c = _place()
        j = 2 * x + y
        return [(src[a].at[c], land[a].at[j, c], (px, py, c)) for a in range(len(shards)) for px, py in _other_chips(x, y)]

    shapes = [jax.ShapeDtypeStruct((4,) + s.shape, s.dtype) for s in shards]
    j = 2 * lax.axis_index("x") + lax.axis_index("y")
    init = [lax.dynamic_update_slice(lax.empty(t.shape, t.dtype), s[None], (j, 0, 0, 0)) for t, s in zip(shapes, shards)]
    return _bulk_start(name, shards, shapes, 3 * len(shards), copies, after, init)


def _gather_wait(started, after, name):
    send, recv, srcs, lands, _ = started

    def waits(src, land):
        x, y, c = _place()
        return [(src[a].at[c], land[a].at[2 * px + py, c]) for a in range(len(srcs)) for px, py in _other_chips(x, y)]

    return _bulk_wait(name, send, recv, srcs, lands, after, waits)


def _forward_start(lands, after, name):
    def copies(src, _):
        x, y, c = _place()
        blocks = [src[a].at[2 * px + py, c] for a in range(len(lands)) for px, py in _other_chips(x, y)]
        return [(b, b, (x, y, 1 - c)) for b in blocks]

    return _bulk_start(name, lands, [], 3 * len(lands), copies, after)


def _forward_wait(started, after, name):
    send, recv, bufs, _, _ = started

    def waits(src, _):
        x, y, c = _place()
        return [(src[a].at[2 * px + py, c], src[a].at[2 * px + py, 1 - c]) for a in range(len(bufs)) for px, py in _other_chips(x, y)]

    return _bulk_wait(name, send, recv, bufs, [], after, waits)[0]


def _as_rows(lands):
    return [f.reshape(4 * f.shape[2] * 2, f.shape[3]) for f in lands]


def _gather_land(started, after, tag):
    shards, lands = _gather_wait(started, after, "gather_wait_" + tag)
    return shards, _forward_start(lands, shards[0], "forward_start_" + tag)


def _gather_done(landed, after, tag):
    _, fwd = landed
    return _as_rows(_forward_wait(fwd, after, "forward_wait_" + tag))


def _swap_halves(grads, name):
    n = len(grads)

    def body(*refs):
        ins, outs = refs[:n], refs[n : 2 * n]
        send, recv = refs[2 * n :]
        x, y, c = _place()
        started = []
        for a in range(n):
            for s in range(4):
                cp = _remote(ins[a].at[s, 1 - c], outs[a].at[s], send.at[4 * a + s], recv.at[4 * a + s], (x, y, 1 - c))
                cp.start()
                started.append(cp)
        for cp in started:
            cp.wait_recv()
        for cp in started:
            cp.wait_send()

    return pl.pallas_call(
        body,
        name=name,
        out_shape=[jax.ShapeDtypeStruct((4,) + g.shape[2:], g.dtype) for g in grads],
        in_specs=[ANY] * n,
        out_specs=[ANY] * n,
        scratch_shapes=[pltpu.SemaphoreType.DMA((4 * n,)), pltpu.SemaphoreType.DMA((4 * n,))],
    )(*grads)


def _add_halves(grads, others, tag):
    outs = []
    for a, (g, o) in enumerate(zip(grads, others)):
        _, _, rh, cdim = g.shape
        tr = _pick(rh, 512, 16)

        def body(g_ref, o_ref, p_ref):
            p_ref[...] = (g_ref[...].astype(F32) + o_ref[...].astype(F32)).astype(BF16)

        outs.append(
            pl.pallas_call(
                body,
                name=f"add_halves_{tag}{a}",
                grid=(4, rh // tr),
                in_specs=[pl.BlockSpec((None, None, tr, cdim), lambda s, i: (s, lax.axis_index("c"), i, 0)),
                          pl.BlockSpec((None, tr, cdim), lambda s, i: (s, i, 0))],
                out_specs=pl.BlockSpec((None, tr, cdim), lambda s, i: (s, i, 0)),
                out_shape=jax.ShapeDtypeStruct((4, rh, cdim), BF16),
                compiler_params=_params(("parallel", "parallel")),
            )(g, o)
        )
    return outs


def _exchange_start(parts, after, name):
    def copies(src, land):
        x, y, c = _place()
        j = 2 * x + y
        return [(src[a].at[2 * px + py], land[a].at[j], (px, py, c)) for a in range(len(parts)) for px, py in _other_chips(x, y)]

    return _bulk_start(name, parts, [jax.ShapeDtypeStruct(p.shape, p.dtype) for p in parts], 3 * len(parts), copies, after)


def _exchange_finish(started, after, name):
    send, recv, srcs, lands, _ = started

    def waits(src, land):
        x, y, _ = _place()
        return [(src[a].at[2 * px + py], land[a].at[2 * px + py]) for a in range(len(srcs)) for px, py in _other_chips(x, y)]

    srcs, lands = _bulk_wait(name, send, recv, srcs, lands, after, waits)
    return lands, srcs


def _sum_chips(recvd, parts, tag):
    outs = []
    for a, (g, p) in enumerate(zip(recvd, parts)):
        _, rh, cdim = g.shape
        tr = _pick(rh, 512, 16)

        def body(g_ref, p_ref, o_ref):
            j = 2 * lax.axis_index("x") + lax.axis_index("y")
            own = p_ref[...].astype(F32)
            term = [jnp.where(j == s, own, g_ref[s].astype(F32)) for s in range(4)]
            o_ref[...] = ((term[0] + term[1]) + term[2]) + term[3]

        outs.append(
            pl.pallas_call(
                body,
                name=f"sum_chips_{tag}{a}",
                grid=(rh // tr,),
                in_specs=[pl.BlockSpec((4, tr, cdim), lambda i: (0, i, 0)),
                          pl.BlockSpec((None, tr, cdim), lambda i: (2 * lax.axis_index("x") + lax.axis_index("y"), i, 0))],
                out_specs=pl.BlockSpec((tr, cdim), lambda i: (i, 0)),
                out_shape=jax.ShapeDtypeStruct((rh, cdim), F32),
                compiler_params=_params(("parallel",)),
            )(g, p)
        )
    return outs


def _joined(mine, other):
    first = lax.axis_index("c") == 0
    return jnp.concatenate([jnp.where(first, mine, other), jnp.where(first, other, mine)], axis=0)


def _grad_views(grads):
    return [g.reshape(4, 2, g.shape[0] // 8, g.shape[1]) for g in grads]


def _scatter_start(grads, tag, after=None):
    views = _grad_views(grads)
    others = _swap_halves(views, "swap_halves_" + tag)
    mine = _add_halves(views, others, tag)
    return _exchange_start(mine, others[-1] if after is None else after, "exchange_start_" + tag)


def _swap_start(grads, after, tag):
    views = _grad_views(grads)

    def copies(src, land):
        x, y, c = _place()
        return [(src[a].at[s, 1 - c], land[a].at[s], (x, y, 1 - c)) for a in range(len(views)) for s in range(4)]

    shapes = [jax.ShapeDtypeStruct((4,) + v.shape[2:], v.dtype) for v in views]
    return _bulk_start("swap_start_" + tag, views, shapes, 4 * len(views), copies, after)


def _scatter_start_after_swap(swapped, after, tag):
    send, recv, views, lands, _ = swapped

    def waits(src, land):
        c = lax.axis_index("c")
        return [(src[a].at[s, 1 - c], land[a].at[s]) for a in range(len(views)) for s in range(4)]

    views, others = _bulk_wait("swap_wait_" + tag, send, recv, views, lands, after, waits)
    mine = _add_halves(views, others, tag)
    return _exchange_start(mine, others[-1], "exchange_start_" + tag)


def _join_start(halves, after, tag):
    def copies(src, land):
        x, y, c = _place()
        return [(src[a], land[a], (x, y, 1 - c)) for a in range(len(halves))]

    return _bulk_start("join_start_" + tag, halves, [jax.ShapeDtypeStruct(h.shape, h.dtype) for h in halves], len(halves), copies, after)


def _join_wait(started, after, tag):
    send, recv, halves, lands, _ = started
    halves, others = _bulk_wait("join_wait_" + tag, send, recv, halves, lands, after, lambda src, land: list(zip(src, land)))
    return list(zip(halves, others))


def _scatter_sums(started, after, tag):
    return _sum_chips(*_exchange_finish(started, after, "exchange_wait_" + tag), tag)


def _t_bf16(w):
    return w.T.astype(BF16)


def kernel(x, c, ctx, c_ctx, w_ada, b_ada, norm1_g, w_in, mla_q_norm_g, w_q_up, mla_kv_norm_g, w_kv_up, gqa_q_norm_g, gqa_k_norm_g, w_br_a, w_br_b, w_out, norm2_g, w_up, conv_w, conv_b, w_down, final_norm_g, loss_target, m_c_ctx, m_w_ada, m_b_ada, m_norm1_g, m_w_in, m_mla_q_norm_g, m_w_q_up, m_mla_kv_norm_g, m_w_kv_up, m_gqa_q_norm_g, m_gqa_k_norm_g, m_w_br_a, m_w_br_b, m_w_out, m_norm2_g, m_w_up, m_conv_w, m_conv_b, m_w_down, m_final_norm_g, v_c_ctx, v_w_ada, v_b_ada, v_norm1_g, v_w_in, v_mla_q_norm_g, v_w_q_up, v_mla_kv_norm_g, v_w_kv_up, v_gqa_q_norm_g, v_gqa_k_norm_g, v_w_br_a, v_w_br_b, v_w_out, v_norm2_g, v_w_up, v_conv_w, v_conv_b, v_w_down, v_final_norm_g):
    T, D = x.shape[1], x.shape[2]
    C = ctx.shape[1]
    NA = w_ada.shape[2]
    NW = w_up.shape[2]
    F2 = 4 * NW
    FF = F2 // 2
    xi, yi, ci = _place()
    j = 2 * xi + yi
    me = 4 * xi + 2 * yi + ci
    tr = _pick(C, 256, 8)

    x2d, tgt, ctx2d = x[0], loss_target[0], ctx[0]
    fg = final_norm_g.reshape(1, D)
    cc = c_ctx.reshape(1, D)

    halve = lambda s: s.reshape(2, s.shape[0] // 2, s.shape[1])
    win_shard = halve(_t_bf16(w_in[0]))
    w0 = max(D, NW)
    pay = jnp.zeros((8, w0), F32).at[0:1, :D].set(c).at[1:4, :NW].set(conv_w[0])
    got = _all_gather_small(pay, "gather_cond")
    ag_in = _gather_start([win_shard], got, "gather_start_in")
    t_in = ag_in[4]
    c_all = got[:, 0, :D]
    cw = jnp.concatenate([got[2 * s, 1:4, :NW] for s in range(4)], axis=1)
    s16 = jnp.concatenate([c_all, cc, jnp.zeros((7, D), F32)], axis=0) + t_in
    b_cols = lax.dynamic_slice(b_ada, (0, j * NA), (1, NA))
    ada_part = _mm(s16, w_ada[0], "NN", F32, "ada_fwd", act="silu", bias=b_cols)

    wq3 = (w_q_up[0] + t_in).reshape(MLA_Q_LORA, 2, MLA_NOPE + MLA_ROPE)
    wq_perm = jnp.concatenate([wq3[:, :, :MLA_NOPE].reshape(MLA_Q_LORA, -1), wq3[:, :, MLA_NOPE:].reshape(MLA_Q_LORA, -1)], axis=1)
    low = [halve(_t_bf16(wq_perm)), halve(_t_bf16(w_kv_up[0] + t_in))]
    br = [halve(_t_bf16(w_br_a[0] + t_in)), halve(_t_bf16(w_br_b[0] + t_in)), halve((w_out[0] + t_in).astype(BF16))]
    up = [halve(_t_bf16(w_up[0] + t_in))]
    down = [halve((w_down[0] + t_in).astype(BF16))]

    got = _all_gather_small(ada_part, "gather_ada", after=(*low, *br, *up, *down))
    ada = jnp.concatenate([got[2 * s] for s in range(4)], axis=1)
    lat = lax.dynamic_slice(ada, (me, 0), (1, 6 * D))
    sh1, sc1, g1, sh2, sc2, g2 = [lat[:, k * D : (k + 1) * D] for k in range(6)]
    csh, csc = ada[8:9, :D], ada[8:9, D : 2 * D]
    ag_low = _gather_start(low, got, "gather_start_low")
    ag_br = _gather_start(br, ag_low[4], "gather_start_br")
    ag_up = _gather_start(up, ag_br[4], "gather_start_up")
    ag_down = _gather_start(down, ag_up[4], "gather_start_down")
    sh1 = sh1 + ag_down[4]

    cos_a, ss_a = _rope_tables(C, T, MLA_ROPE)
    cos_b, ss_b = _rope_tables(C, T, GQA_HEAD_DIM)
    lcos_a, lss_a, lcos_b, lss_b = cos_a[:T], ss_a[:T], cos_b[:T], ss_b[:T]

    in_landed = _gather_land(ag_in, down[0], "in")
    z_all = _norm_mod_fwd(x2d, norm1_g, sh1 + in_landed[1][4], sc1, "norm1_lat_fwd", tr, out_rows=T + C)
    z_all = _norm_mod_fwd(ctx2d, norm1_g, csh, csc, "norm1_ctx_fwd", tr, base=z_all, out_off=T)
    (win_t,) = _gather_done(in_landed, z_all, "in")
    kv_cols = KVP - LANES + MLA_ROPE
    e_kpe = MLA_KV_LORA + MLA_ROPE
    w_kvp = jnp.concatenate([win_t[:MLA_KV_LORA], win_t[e_kpe:kv_cols], win_t[MLA_KV_LORA:e_kpe], jnp.zeros((LANES - MLA_ROPE, D), BF16)], axis=0)

    pkv = _mm(z_all, w_kvp, "NT", F32, "proj_kv", tn=KVP)
    pq = _mm(z_all, win_t, "NT", F32, "proj_q", m=T, n=QC, b_off=kv_cols)
    low_landed = _gather_land(ag_low, pq, "low")
    pg = _mm(z_all, win_t, "NT", BF16, "proj_g", m=T, n=2 * D, b_off=kv_cols + QC, after=low_landed[1][4])
    wq_t, wkv_t = _gather_done(low_landed, pg, "low")
    ckv_n, kb2, vb2, kpe2 = _kprep_fwd(pkv, mla_kv_norm_g, gqa_k_norm_g, cos_a, ss_a, cos_b, ss_b, tr)
    kv_up = _mm(ckv_n, wkv_t, "NT", BF16, "kv_up")
    cq_n, qb2 = _qprep_fwd(pq, mla_q_norm_g, gqa_q_norm_g, lcos_b, lss_b, tr)
    q_a = _mm(cq_n, wq_t, "NT", F32, "q_up")
    qar = _qrope_fwd(q_a, lcos_a, lss_a, tr)

    a_q = [(qar, lambda h: 3 * (h // 2) + h % 2), (qar, lambda h: 3 * (h // 2) + 2)]
    a_k = [(kv_up, lambda h: 2 * h), (kpe2, lambda h: h % 2)]
    a_v = (kv_up, lambda h: 2 * h + 1)
    a_scale = float(MLA_NOPE + MLA_ROPE) ** -0.5
    b_q = [(qb2, lambda h: h)]
    b_k = [(kb2, lambda h: h)]
    b_v = (vb2, lambda h: h)
    b_scale = float(GQA_HEAD_DIM) ** -0.5
    tq_f = _pick(T, 2048)
    o_a, lse_a = _attn_fwd(a_q, a_k, a_v, MLA_HEADS, 1, MLA_V, a_scale, "attn_a_fwd", tq_f)
    br_landed = _gather_land(ag_br, o_a, "br")
    o_b, lse_b = _attn_fwd(b_q, b_k, b_v, GQA_HEADS, GQA_GROUP, GQA_HEAD_DIM, b_scale, "attn_b_fwd", tq_f, after=br_landed[1][4])
    wbra_t, wbrb_t, wout = _gather_done(br_landed, o_b, "br")
    up_landed = _gather_land(ag_up, o_b, "up")
    ya = _mm(o_a, wbra_t, "NT", BF16, "br_a", after=up_landed[1][4])
    yb = _mm(o_b, wbrb_t, "NT", BF16, "br_b")
    merged = _gates_fwd(pg, ya, yb, tr)
    att = _mm(merged, wout, "NN", F32, "out_proj")
    x1, z2 = _resid_norm2_fwd(x2d, att, g1, norm2_g, sh2, sc2, tr)
    (wup_t,) = _gather_done(up_landed, z2, "up")
    down_landed = _gather_land(ag_down, z2, "down")
    tc = _pick(FF, 128)
    u_a, u_b, hg = _ffn_up_conv(z2, wup_t, cw, conv_b, tc, down_landed[1][4])
    (wdown,) = _gather_done(down_landed, hg, "down")
    f = _mm(hg, wdown, "NN", F32, "ffn_down", tk=FF // 2)
    sq, dx2, d_fg, d_g2, df = _loss_head(x1, f, g2, fg, tgt, tr)
    loss = lax.psum(0.5 * jnp.sum(sq) / D, ("x", "y", "c"))

    du_a, du_b, dcw_a, dcw_b, dcb_a, dcb_b = _ffn_down_dx_conv_bwd(df, wdown, u_a, u_b, cw, conv_b, _pick(FF, 256), loss.reshape(1, 1))
    g_wdown = _mm(hg, df, "TN", BF16, "ffn_down_dw", tm=FF // 4)
    dz2 = _mm(du_a, wup_t, "NN", F32, "ffn_up_dx_a", tk=FF // 2)
    dz2 = _mm(du_b, wup_t, "NN", F32, "ffn_up_dx_b", b_off=FF, add=dz2, tk=FF // 2)
    g_wup_t = _mm(du_a, z2, "TN", BF16, "ffn_up_dw_a", out_rows=F2, tm=FF // 4)
    g_wup_t = _mm(du_b, z2, "TN", BF16, "ffn_up_dw_b", out_base=g_wup_t, out_off=FF, tm=FF // 4)
    sw_ffn = _swap_start([g_wdown, g_wup_t], sc2, "ffn")
    sc2 = sc2 + sw_ffn[4]
    dx1, datt, d_n2g, d_sh2, d_sc2, d_g1 = _resid_norm2_bwd(dz2, x1, dx2, att, norm2_g, sc2, g1, tr)

    dmerged = _mm(datt, wout, "NT", BF16, "out_proj_dx")
    rs_ffn = _scatter_start_after_swap(sw_ffn, dmerged, "ffn")
    lse_a = lse_a + rs_ffn[4]
    g_wout = _mm(merged, datt, "TN", BF16, "out_proj_dw")
    dya, dyb, dpg = _gates_bwd(dmerged, pg, ya, yb, tr)
    do_a = _mm(dya, wbra_t, "NN", BF16, "br_a_dx")
    g_wbra_t = _mm(dya, o_a, "TN", BF16, "br_a_dw")
    do_b = _mm(dyb, wbrb_t, "NN", BF16, "br_b_dx")
    g_wbrb_t = _mm(dyb, o_b, "TN", BF16, "br_b_dw")
    dqa2, dka2, dva2 = _attn_bwd(a_q, a_k, a_v, o_a, do_a, lse_a, MLA_HEADS, 1, MLA_V, a_scale, "attn_a_bwd", tq_f)
    dqb2, dkb2, dvb2 = _attn_bwd(b_q, b_k, b_v, o_b, do_b, lse_b, GQA_HEADS, GQA_GROUP, GQA_HEAD_DIM, b_scale, "attn_b_bwd", tq_f)
    dq_a = _qrope_bwd(dqa2, lcos_a, lss_a, tr)
    dcq_n = _mm(dq_a, wq_t, "NN", F32, "q_up_dx")
    g_wq_t = _mm(dq_a, cq_n, "TN", BF16, "q_up_dw")
    dpq, d_qg, d_gq = _qprep_bwd(pq, dcq_n, dqb2, mla_q_norm_g, gqa_q_norm_g, lcos_b, lss_b, tr)
    dkv_up, dkpe = _kgrad_split(dka2, dva2, cos_a, ss_a, tr)
    dckv_n = _mm(dkv_up, wkv_t, "NN", F32, "kv_up_dx")
    g_wkv_t = _mm(dkv_up, ckv_n, "TN", BF16, "kv_up_dw")
    rs_mix = _scatter_start([g_wq_t, g_wkv_t, g_wbra_t, g_wbrb_t, g_wout], "mix")
    dpkv, d_kvg, d_kg = _kprep_bwd(pkv, dckv_n, dkb2, dvb2, dkpe, mla_kv_norm_g + rs_mix[4], gqa_k_norm_g, cos_b, ss_b, tr)
    dz_kv = _mm(dpkv, w_kvp, "NN", F32, "proj_kv_dx")
    dz_lat = _mm(dpq, win_t, "NN", F32, "proj_q_dx", b_off=kv_cols, add=dz_kv)
    dz_lat = _mm(dpg, win_t, "NN", F32, "proj_g_dx", b_off=kv_cols + QC, add=dz_lat)
    _, d_n1g_c, d_csh, d_csc = _norm_mod_bwd(dz_kv, T // tr, ctx2d, norm1_g, csc, None, "norm1_ctx_bwd", tr)
    grad_x, d_n1g_l, d_sh1, d_sc1 = _norm_mod_bwd(dz_lat, 0, x2d, norm1_g, sc1, dx1, "norm1_lat_bwd", tr)

    zeros_d = jnp.zeros((1, D), F32)
    d_lat = jnp.concatenate([d_sh1, d_sc1, d_g1, d_sh2, d_sc2, d_g2], axis=1)
    d_ctx_part = jnp.concatenate([d_csh, d_csc], axis=1)
    flat = jnp.concatenate(
        [d_n1g_c + d_n1g_l, d_qg, d_kvg, d_gq, d_kg, d_n2g, dcb_a, dcb_b, d_fg,
         dcw_a.reshape(1, -1), dcw_b.reshape(1, -1), d_ctx_part, d_lat], axis=1)
    n_flat = flat.shape[1]
    n_rows = -(-n_flat // (8 * LANES)) * 8
    flat = jnp.pad(flat, ((0, 0), (0, n_rows * LANES - n_flat))).reshape(n_rows, LANES)
    small = _small_gather_start(flat, grad_x, "small_grads_start")

    g_kvp = _mm(dpkv, z_all, "TN", BF16, "proj_kv_dw", after=small[4])
    nk = MLA_KV_LORA + 2 * GQA_KV_HEADS * GQA_HEAD_DIM
    g_kv = jnp.concatenate([g_kvp[:MLA_KV_LORA], g_kvp[nk : nk + MLA_ROPE], g_kvp[MLA_KV_LORA:nk]], axis=0)
    g_win_t = _mm(dpq, z_all, "TN", BF16, "proj_q_dw", out_rows=kv_cols + QC + 2 * D, out_off=kv_cols, tm=QC // 2)
    g_win_t = _mm(dpg, z_all, "TN", BF16, "proj_g_dw", out_base=g_win_t, out_off=kv_cols + QC)
    g_win_t = lax.dynamic_update_slice(g_win_t, g_kv, (0, 0))

    got = _small_gather_wait(small, g_win_t, "small_grads_wait")
    tot = _sum_slots(got, "sum_small_grads").reshape(1, -1)
    sizes = [D, MLA_Q_LORA, MLA_KV_LORA, GQA_HEAD_DIM, GQA_HEAD_DIM, D, F2, D, 3 * FF, 3 * FF, 2 * D]
    offs = [0]
    for s in sizes:
        offs.append(offs[-1] + s)
    t_n1g, t_qg, t_kvg, t_gq, t_kg, t_n2g, t_cb, t_fg, t_cwa, t_cwb, t_ctx = [tot[:, offs[k] : offs[k + 1]] for k in range(len(sizes))]
    g_cw_full = jnp.concatenate([t_cwa.reshape(3, FF), t_cwb.reshape(3, FF)], axis=1)
    g_cw = lax.dynamic_slice(g_cw_full, (0, j * NW), (3, NW))
    d_lat_all = got.reshape(8, -1)[:, offs[-1] : offs[-1] + 6 * D]
    g16 = jnp.concatenate([d_lat_all, jnp.pad(t_ctx, ((0, 0), (0, 4 * D))), jnp.zeros((7, 6 * D), F32)], axis=0)
    g_b_ada = _sum_slots(g16.reshape(16, 1, 6 * D), "sum_b_ada")
    g16_cols = lax.dynamic_slice(g16, (0, j * NA), (16, NA))
    ds_part = _mm(g16_cols, w_ada[0], "NT", F32, "ada_dx")
    ada_dx = _small_gather_start(ds_part[8:16], got, "ada_dx_start")
    sw_in = _swap_start([g_win_t], ada_dx[4], "in")

    h_ffn = _scatter_sums(rs_ffn, sw_in[2][0], "ffn")
    got = _small_gather_wait(ada_dx, h_ffn[0], "ada_dx_wait")
    ds_ctx = _sum_slots(jnp.stack([got[2 * s] for s in range(4)]), "sum_ada_dx")[0:1]
    g_c_ctx = _silu_grad_mul(ds_ctx, cc)
    j_ffn = _join_start(h_ffn, grad_x, "ffn")
    h_mix = _scatter_sums(rs_mix, j_ffn[2][0], "mix")
    j_mix = _join_start(h_mix, j_ffn[2][0], "mix")
    rs_in = _scatter_start_after_swap(sw_in, j_mix[2][0], "in")
    g_w_ada = _mm(s16, g16_cols, "TN", F32, "ada_dw", act="silu", after=rs_in[4])
    _, d_ada, m_ada, v_ada = _adamw(w_ada[0], g_w_ada, m_w_ada[0], v_w_ada[0], "adamw_w_ada")
    r_wdown, r_wup = _join_wait(j_ffn, d_ada, "ffn")
    r_wq, r_wkv, r_wbra, r_wbrb, r_wout = _join_wait(j_mix, d_ada, "mix")
    gq_p = _joined(*r_wq).T
    gq = jnp.concatenate([gq_p[:, : 2 * MLA_NOPE].reshape(MLA_Q_LORA, 2, MLA_NOPE), gq_p[:, 2 * MLA_NOPE :].reshape(MLA_Q_LORA, 2, MLA_ROPE)], axis=2)
    grads = {
        "c_ctx": g_c_ctx.reshape(D), "w_ada": g_w_ada[None], "b_ada": g_b_ada, "norm1_g": t_n1g,
        "mla_q_norm_g": t_qg, "w_q_up": gq.reshape(1, MLA_Q_LORA, -1), "mla_kv_norm_g": t_kvg, "w_kv_up": r_wkv,
        "gqa_q_norm_g": t_gq, "gqa_k_norm_g": t_kg, "w_br_a": r_wbra, "w_br_b": r_wbrb, "w_out": r_wout,
        "norm2_g": t_n2g, "w_up": r_wup, "conv_w": g_cw[None], "conv_b": t_cb, "w_down": r_wdown,
        "final_norm_g": t_fg.reshape(D),
    }
    arrives_transposed = ("w_kv_up", "w_br_a", "w_br_b", "w_up")
    arrives_halved = arrives_transposed + ("w_out", "w_down")
    weights = dict(c_ctx=c_ctx, w_ada=w_ada, b_ada=b_ada, norm1_g=norm1_g, w_in=w_in, mla_q_norm_g=mla_q_norm_g, w_q_up=w_q_up,
                   mla_kv_norm_g=mla_kv_norm_g, w_kv_up=w_kv_up, gqa_q_norm_g=gqa_q_norm_g, gqa_k_norm_g=gqa_k_norm_g, w_br_a=w_br_a,
                   w_br_b=w_br_b, w_out=w_out, norm2_g=norm2_g, w_up=w_up, conv_w=conv_w, conv_b=conv_b, w_down=w_down,
                   final_norm_g=final_norm_g)
    m_in = dict(c_ctx=m_c_ctx, w_ada=m_w_ada, b_ada=m_b_ada, norm1_g=m_norm1_g, w_in=m_w_in, mla_q_norm_g=m_mla_q_norm_g,
                w_q_up=m_w_q_up, mla_kv_norm_g=m_mla_kv_norm_g, w_kv_up=m_w_kv_up, gqa_q_norm_g=m_gqa_q_norm_g,
                gqa_k_norm_g=m_gqa_k_norm_g, w_br_a=m_w_br_a, w_br_b=m_w_br_b, w_out=m_w_out, norm2_g=m_norm2_g, w_up=m_w_up,
                conv_w=m_conv_w, conv_b=m_conv_b, w_down=m_w_down, final_norm_g=m_final_norm_g)
    v_in = dict(c_ctx=v_c_ctx, w_ada=v_w_ada, b_ada=v_b_ada, norm1_g=v_norm1_g, w_in=v_w_in, mla_q_norm_g=v_mla_q_norm_g,
                w_q_up=v_w_q_up, mla_kv_norm_g=v_mla_kv_norm_g, w_kv_up=v_w_kv_up, gqa_q_norm_g=v_gqa_q_norm_g,
                gqa_k_norm_g=v_gqa_k_norm_g, w_br_a=v_w_br_a, w_br_b=v_w_br_b, w_out=v_w_out, norm2_g=v_norm2_g, w_up=v_w_up,
                conv_w=v_conv_w, conv_b=v_conv_b, w_down=v_w_down, final_norm_g=v_final_norm_g)
    names = list(weights)
    big = [n for n in names if weights[n].ndim == 3 and weights[n].shape[1] >= 8]
    small = [n for n in names if n not in big]
    delta, new_m, new_v = {}, {}, {}

    def update(n):
        shp = weights[n].shape
        two_d = lambda a: a.reshape(shp[1], shp[2])
        g_t = n in arrives_transposed
        if n in arrives_halved:
            g_in, g_sib = grads[n]
        else:
            g_in, g_sib = two_d(grads[n].astype(F32)), None
        g_, d_, m_, v_ = _adamw(two_d(weights[n]), g_in, two_d(m_in[n]), two_d(v_in[n]), "adamw_" + n, g_transposed=g_t, g_sibling=g_sib)
        grads[n], delta[n], new_m[n], new_v[n] = g_.reshape(shp), d_.reshape(shp), m_.reshape(shp), v_.reshape(shp)

    delta["w_ada"], new_m["w_ada"], new_v["w_ada"] = d_ada[None], m_ada[None], v_ada[None]
    early = [n for n in big if n not in ("w_in", "w_ada")]
    for n in early[:-1]:
        update(n)
    done = sum(delta[n][0, 0:1, 0:1] for n in early[:-1])
    j_in = _join_start(_scatter_sums(rs_in, done, "in"), done, "in")
    last = early[-1]
    grads[last] = (grads[last][0] + j_in[4], grads[last][1])
    update(last)
    ((g_mine, g_sib),) = _join_wait(j_in, delta[last], "in")
    g_, d_, m_, v_ = _adamw(w_in[0].T, g_mine, m_w_in[0].T, v_w_in[0].T, "adamw_w_in", g_sibling=g_sib)
    grads["w_in"], delta["w_in"], new_m["w_in"], new_v["w_in"] = g_.T[None], d_.T[None], m_.T[None], v_.T[None]
    grads = {n: grads[n].reshape(weights[n].shape).astype(F32) for n in names}

    slab = lambda tree: [tree[n].reshape(-1, LANES) for n in small]
    d_, m_, v_ = _adamw_many(slab(weights), slab(grads), slab(m_in), slab(v_in), "adamw_small")
    for k, n in enumerate(small):
        shp = weights[n].shape
        delta[n], new_m[n], new_v[n] = d_[k].reshape(shp), m_[k].reshape(shp), v_[k].reshape(shp)

    return (loss, grad_x[None], *[grads[n] for n in names], *[delta[n] for n in names], *[new_m[n] for n in names],
            *[new_v[n] for n in names])
```

```python
import math

import jax
import jax.numpy as jnp
from jax import lax
from jax.experimental import pallas as pl
from jax.experimental.pallas import tpu as pltpu

F32 = jnp.float32
BF16 = jnp.bfloat16
MESH = pl.DeviceIdType.MESH

NORM_EPS = 1e-6
ROPE_THETA = 10000.0
GRID_W = 64
MLA_HEADS = 8
MLA_Q_LORA = 768
MLA_KV_LORA = 512
MLA_NOPE = 128
MLA_ROPE = 64
MLA_V = 128
GQA_HEADS = 8
GQA_KV_HEADS = 2
GQA_HEAD_DIM = 128
GQA_GROUP = GQA_HEADS // GQA_KV_HEADS
LANES = 128
KVP = MLA_KV_LORA + 2 * GQA_KV_HEADS * GQA_HEAD_DIM + LANES
QC = MLA_Q_LORA + GQA_HEADS * GQA_HEAD_DIM

ADAM_LR = 0.001
ADAM_B1 = 0.9
ADAM_B2 = 0.999
ADAM_EPS = 1e-08
ADAM_WD = 0.01
ADAM_STEP = 10

VMEM_LIMIT = 56 * 1024 * 1024


def _pick(dim, target, mult=LANES):
    t = (min(target, dim) // mult) * mult
    while t >= mult:
        if dim % t == 0:
            return t
        t -= mult
    return dim


def _params(sem):
    return pltpu.CompilerParams(dimension_semantics=sem, vmem_limit_bytes=VMEM_LIMIT)


_DIMS = {"NN": (((1,), (0,)), ((), ())), "NT": (((1,), (1,)), ((), ())), "TN": (((0,), (0,)), ((), ()))}


MM_VMEM_BUDGET = 36 * 1024 * 1024


def _mm_tiles(M, N, K, sa, sb, so, tm, tn, tk):
    tm, tn, tk = _pick(M, tm), _pick(N, tn), _pick(K, tk)

    def need(t):
        return 2 * (tm * t * sa + t * tn * sb) + 2 * tm * tn * so + (tm * tn * 4 if t < K else 0)

    while need(tk) > MM_VMEM_BUDGET and tk > LANES:
        smaller = _pick(K, tk - LANES)
        if smaller >= tk:
            break
        tk = smaller
    return tm, tn, tk


def _window(block, index, offsets):
    if not any(offsets):
        return pl.BlockSpec(block, index)
    for t, o in zip(block, offsets):
        assert o % 16 == 0 and t % 16 == 0, (block, offsets)

    def at(i, j, k):
        return tuple(pl.multiple_of(o + p * t, math.gcd(o, t)) for p, t, o in zip(index(i, j, k), block, offsets))

    return pl.BlockSpec(tuple(pl.Element(t) for t in block), at)


def _mm(a, b, mode, out_dtype, name, m=None, n=None, k=None, b_off=0, add=None, out_rows=None, out_base=None, out_off=0,
        tm=1024, tn=1024, tk=2304, act=None, bias=None, after=None):
    if mode == "NN":
        M, K, N = m or a.shape[0], k or a.shape[1], b.shape[1]
    elif mode == "NT":
        M, K, N = m or a.shape[0], a.shape[1], n or b.shape[0]
    else:
        M, K, N = a.shape[1], k or a.shape[0], b.shape[1]
    tm, tn, tk = _mm_tiles(M, N, K, a.dtype.itemsize, b.dtype.itemsize, jnp.dtype(out_dtype).itemsize, tm, tn, tk)
    nk = K // tk
    dims = _DIMS[mode]
    n_in = 2 + (bias is not None) + (add is not None) + (out_base is not None) + (after is not None)

    def body(*refs):
        a_ref, b_ref = refs[:2]
        bias_ref = refs[2] if bias is not None else None
        add_ref = refs[2 + (bias is not None)] if add is not None else None
        o_ref = refs[n_in]
        av = a_ref[...]
        if act == "silu":
            av = av * jax.nn.sigmoid(av)
        part = lax.dot_general(av.astype(BF16), b_ref[...].astype(BF16), dims, preferred_element_type=F32)

        def finish(r):
            if bias is not None:
                r = r + bias_ref[...]
            if add is not None:
                r = r + add_ref[...]
            o_ref[...] = r.astype(out_dtype)

        if nk == 1:
            finish(part)
            return
        acc = refs[-1]
        k = pl.program_id(2)

        @pl.when(k == 0)
        def _():
            acc[...] = part

        @pl.when(jnp.logical_and(k > 0, k < nk - 1))
        def _():
            acc[...] += part

        @pl.when(k == nk - 1)
        def _():
            finish(acc[...] + part)

    a_spec = pl.BlockSpec((tk, tm), lambda i, j, k: (k, i)) if mode == "TN" else pl.BlockSpec((tm, tk), lambda i, j, k: (i, k))
    if mode == "NT":
        b_spec = _window((tn, tk), lambda i, j, k: (j, k), (b_off, 0))
    else:
        b_spec = _window((tk, tn), lambda i, j, k: (k, j), (b_off, 0))
    in_specs, args = [a_spec, b_spec], [a, b]
    if bias is not None:
        in_specs.append(pl.BlockSpec((1, tn), lambda i, j, k: (0, j)))
        args.append(bias)
    if add is not None:
        in_specs.append(pl.BlockSpec((tm, tn), lambda i, j, k: (i, j)))
        args.append(add)
    aliases = {}
    if after is not None:
        in_specs.append(pl.BlockSpec(after.shape, lambda i, j, k: (0, 0)))
        args.append(after)
    if out_base is not None:
        aliases = {len(args): 0}
        in_specs.append(ANY)
        args.append(out_base)
        out_rows = out_base.shape[0]
    return pl.pallas_call(
        body,
        name=name,
        grid=(M // tm, N // tn, nk),
        in_specs=in_specs,
        out_specs=_window((tm, tn), lambda i, j, k: (i, j), (out_off, 0)),
        out_shape=jax.ShapeDtypeStruct((out_rows or M, N), out_dtype),
        input_output_aliases=aliases,
        scratch_shapes=[pltpu.VMEM((tm, tn), F32)] if nk > 1 else [],
        compiler_params=_params(("parallel", "parallel", "arbitrary")),
    )(*args)


def _rms(x):
    r = lax.rsqrt(jnp.mean(x * x, axis=-1, keepdims=True) + NORM_EPS)
    return x * r, r


def _rms_bwd(xh, r, dxh):
    return r * (dxh - xh * jnp.mean(dxh * xh, axis=-1, keepdims=True))


def _swap(x, q):
    lane = lax.broadcasted_iota(jnp.int32, x.shape, 1)
    even = ((lane // q) % 2) == 0
    return jnp.where(even, pltpu.roll(x, LANES - q, 1), pltpu.roll(x, q, 1))


def _rope(x, cos, ss, q):
    return x * cos + _swap(x, q) * ss


def _rope_t(d, cos, ss, q):
    return d * cos + _swap(d * ss, q)


def _csum(x):
    return jnp.sum(x, axis=0, keepdims=True)


def _rows(tr, w, off=0):
    return pl.BlockSpec((tr, w), lambda i: (i + off, 0))


def _bcast(w):
    return pl.BlockSpec((1, w), lambda i: (0, 0))


def _acc_init(i, refs):
    @pl.when(i == 0)
    def _():
        for r in refs:
            r[...] = jnp.zeros_like(r)


def _rope_tables(n_ctx, n_lat, rot_dim):
    rows = n_lat // GRID_W
    row = jnp.repeat(jnp.arange(rows, dtype=F32), GRID_W)
    col = jnp.tile(jnp.arange(GRID_W, dtype=F32), rows)
    half = rot_dim // 2
    inv_freq = ROPE_THETA ** (-jnp.arange(0, half, 2, dtype=F32) / half)
    ar, ac = row[:, None] * inv_freq, col[:, None] * inv_freq
    cos = jnp.concatenate([jnp.cos(ar), jnp.cos(ar), jnp.cos(ac), jnp.cos(ac)], axis=-1)
    ss = jnp.concatenate([-jnp.sin(ar), jnp.sin(ar), -jnp.sin(ac), jnp.sin(ac)], axis=-1)
    cos = jnp.tile(cos, (1, LANES // rot_dim))
    ss = jnp.tile(ss, (1, LANES // rot_dim))
    cos = jnp.concatenate([cos, jnp.ones((n_ctx, LANES), F32)], axis=0)
    ss = jnp.concatenate([ss, jnp.zeros((n_ctx, LANES), F32)], axis=0)
    return cos, ss


def _norm_mod_fwd(x2d, g, sh, sc, name, tr, out_rows=None, base=None, out_off=0):
    n, d = x2d.shape

    def body(x_ref, g_ref, sh_ref, sc_ref, *rest):
        xh, _ = _rms(x_ref[...])
        rest[-1][...] = ((xh * g_ref[...]) * (1.0 + sc_ref[...]) + sh_ref[...]).astype(BF16)

    args, in_specs, aliases = [x2d, g, sh, sc], [_rows(tr, d), _bcast(d), _bcast(d), _bcast(d)], {}
    if base is not None:
        args.append(base)
        in_specs.append(ANY)
        aliases = {4: 0}
        out_rows = base.shape[0]
    return pl.pallas_call(
        body,
        name=name,
        grid=(n // tr,),
        in_specs=in_specs,
        out_specs=_rows(tr, d, out_off // tr),
        out_shape=jax.ShapeDtypeStruct((out_rows or n, d), BF16),
        input_output_aliases=aliases,
        compiler_params=_params(("parallel",)),
    )(*args)


def _norm_mod_bwd(dz, dz_off, x2d, g, sc, dres, name, tr):
    n, d = x2d.shape
    want_dx = dres is not None

    def body(*refs):
        if want_dx:
            dz_ref, x_ref, g_ref, sc_ref, dres_ref, dx_ref, dg_ref, dsh_ref, dsc_ref = refs
        else:
            dz_ref, x_ref, g_ref, sc_ref, dg_ref, dsh_ref, dsc_ref = refs
        _acc_init(pl.program_id(0), [dg_ref, dsh_ref, dsc_ref])
        xh, r = _rms(x_ref[...])
        dzv = dz_ref[...]
        gv = g_ref[...]
        dsc_ref[...] += _csum(dzv * (xh * gv))
        dsh_ref[...] += _csum(dzv)
        dh = dzv * (1.0 + sc_ref[...])
        dg_ref[...] += _csum(dh * xh)
        if want_dx:
            dx_ref[...] = _rms_bwd(xh, r, dh * gv) + dres_ref[...]

    in_specs = [_rows(tr, d, dz_off), _rows(tr, d), _bcast(d), _bcast(d)]
    args = [dz, x2d, g, sc]
    out_specs = [_bcast(d)] * 3
    out_shape = [jax.ShapeDtypeStruct((1, d), F32)] * 3
    if want_dx:
        in_specs.append(_rows(tr, d))
        args.append(dres)
        out_specs = [_rows(tr, d)] + out_specs
        out_shape = [jax.ShapeDtypeStruct((n, d), F32)] + out_shape
    res = pl.pallas_call(
        body,
        name=name,
        grid=(n // tr,),
        in_specs=in_specs,
        out_specs=out_specs,
        out_shape=out_shape,
        compiler_params=_params(("arbitrary",)),
    )(*args)
    return res if want_dx else (None, *res)


_QA, _QB = MLA_ROPE // 4, GQA_HEAD_DIM // 4


def _kprep_fwd(pkv, kvg, kg, cos_a, ss_a, cos_b, ss_b, tr):
    n = pkv.shape[0]
    nb = GQA_KV_HEADS * GQA_HEAD_DIM

    def body(p_ref, kvg_ref, kg_ref, ca, sa, cb, sb, ckv_ref, kb_ref, vb_ref, kpe_ref):
        p = p_ref[...]
        xh, _ = _rms(p[:, :MLA_KV_LORA])
        ckv_ref[...] = (xh * kvg_ref[...]).astype(BF16)
        for e in range(GQA_KV_HEADS):
            lo = MLA_KV_LORA + e * GQA_HEAD_DIM
            kh, _ = _rms(p[:, lo : lo + GQA_HEAD_DIM])
            kb_ref[:, e * GQA_HEAD_DIM : (e + 1) * GQA_HEAD_DIM] = _rope(kh * kg_ref[...], cb[...], sb[...], _QB).astype(BF16)
        vb_ref[...] = p[:, MLA_KV_LORA + nb : MLA_KV_LORA + 2 * nb].astype(BF16)
        kr = _rope(p[:, MLA_KV_LORA + 2 * nb :], ca[...], sa[...], _QA)
        kpe_ref[:, :LANES] = kr.astype(BF16)
        kpe_ref[:, LANES:] = pltpu.roll(kr, MLA_ROPE, 1).astype(BF16)

    return pl.pallas_call(
        body,
        name="kprep_fwd",
        grid=(n // tr,),
        in_specs=[_rows(tr, KVP), _bcast(MLA_KV_LORA), _bcast(GQA_HEAD_DIM)] + [_rows(tr, LANES)] * 4,
        out_specs=[_rows(tr, MLA_KV_LORA), _rows(tr, nb), _rows(tr, nb), _rows(tr, 2 * LANES)],
        out_shape=[jax.ShapeDtypeStruct((n, w), BF16) for w in (MLA_KV_LORA, nb, nb, 2 * LANES)],
        compiler_params=_params(("parallel",)),
    )(pkv, kvg, kg, cos_a, ss_a, cos_b, ss_b)


def _kprep_bwd(pkv, dckv, dkb, dvb, dkpe, kvg, kg, cos_b, ss_b, tr):
    n = pkv.shape[0]
    nb = GQA_KV_HEADS * GQA_HEAD_DIM

    def body(p_ref, dckv_ref, dkb_ref, dvb_ref, dkpe_ref, kvg_ref, kg_ref, cb, sb, dp_ref, dkvg_ref, dkg_ref):
        _acc_init(pl.program_id(0), [dkvg_ref, dkg_ref])
        p = p_ref[...]
        xh, r = _rms(p[:, :MLA_KV_LORA])
        dn = dckv_ref[...]
        dkvg_ref[...] += _csum(dn * xh)
        dp_ref[:, :MLA_KV_LORA] = _rms_bwd(xh, r, dn * kvg_ref[...]).astype(BF16)
        for e in range(GQA_KV_HEADS):
            lo = MLA_KV_LORA + e * GQA_HEAD_DIM
            kh, rk = _rms(p[:, lo : lo + GQA_HEAD_DIM])
            dk = _rope_t(dkb_ref[:, e * GQA_HEAD_DIM : (e + 1) * GQA_HEAD_DIM], cb[...], sb[...], _QB)
            dkg_ref[...] += _csum(dk * kh)
            dp_ref[:, lo : lo + GQA_HEAD_DIM] = _rms_bwd(kh, rk, dk * kg_ref[...]).astype(BF16)
        dp_ref[:, MLA_KV_LORA + nb : MLA_KV_LORA + 2 * nb] = dvb_ref[...].astype(BF16)
        dp_ref[:, MLA_KV_LORA + 2 * nb :] = dkpe_ref[...].astype(BF16)

    return pl.pallas_call(
        body,
        name="kprep_bwd",
        grid=(n // tr,),
        in_specs=[_rows(tr, KVP), _rows(tr, MLA_KV_LORA), _rows(tr, nb), _rows(tr, nb), _rows(tr, LANES),
                  _bcast(MLA_KV_LORA), _bcast(GQA_HEAD_DIM), _rows(tr, LANES), _rows(tr, LANES)],
        out_specs=[_rows(tr, KVP), _bcast(MLA_KV_LORA), _bcast(GQA_HEAD_DIM)],
        out_shape=[jax.ShapeDtypeStruct((n, KVP), BF16), jax.ShapeDtypeStruct((1, MLA_KV_LORA), F32),
                   jax.ShapeDtypeStruct((1, GQA_HEAD_DIM), F32)],
        compiler_params=_params(("arbitrary",)),
    )(pkv, dckv, dkb, dvb, dkpe, kvg, kg, cos_b, ss_b)


def _kgrad_split(dka, dva, cos_a, ss_a, tr):
    n = dka.shape[0]
    wk = MLA_HEADS * 2 * LANES

    def body(dk_ref, dv_ref, ca, sa, dkv_ref, dkpe_ref):
        even = jnp.zeros((tr, LANES), F32)
        odd = jnp.zeros((tr, LANES), F32)
        for h in range(MLA_HEADS):
            dkv_ref[:, 2 * h * LANES : (2 * h + 1) * LANES] = dk_ref[:, 2 * h * LANES : (2 * h + 1) * LANES].astype(BF16)
            dkv_ref[:, (2 * h + 1) * LANES : (2 * h + 2) * LANES] = dv_ref[:, h * MLA_V : (h + 1) * MLA_V].astype(BF16)
            part = dk_ref[:, (2 * h + 1) * LANES : (2 * h + 2) * LANES]
            if h % 2 == 0:
                even = even + part
            else:
                odd = odd + part
        lane = lax.broadcasted_iota(jnp.int32, (tr, LANES), 1)
        low = lane < MLA_ROPE
        both = jnp.where(low, even, odd)
        tot = jnp.where(low, both + pltpu.roll(both, MLA_ROPE, 1), 0.0)
        dkpe_ref[...] = _rope_t(tot, ca[...], sa[...], _QA)

    return pl.pallas_call(
        body,
        name="kgrad_split",
        grid=(n // tr,),
        in_specs=[_rows(tr, wk), _rows(tr, MLA_HEADS * MLA_V), _rows(tr, LANES), _rows(tr, LANES)],
        out_specs=[_rows(tr, wk), _rows(tr, LANES)],
        out_shape=[jax.ShapeDtypeStruct((n, wk), BF16), jax.ShapeDtypeStruct((n, LANES), F32)],
        compiler_params=_params(("parallel",)),
    )(dka, dva, cos_a, ss_a)


def _qprep_fwd(pq, qg, gq, cos_b, ss_b, tr):
    n = pq.shape[0]
    nq = GQA_HEADS * GQA_HEAD_DIM

    def body(p_ref, qg_ref, gq_ref, cb, sb, cq_ref, qb_ref):
        xh, _ = _rms(p_ref[:, :MLA_Q_LORA])
        cq_ref[...] = (xh * qg_ref[...]).astype(BF16)
        for h in range(GQA_HEADS):
            lo = MLA_Q_LORA + h * GQA_HEAD_DIM
            qh, _ = _rms(p_ref[:, lo : lo + GQA_HEAD_DIM])
            qb_ref[:, h * GQA_HEAD_DIM : (h + 1) * GQA_HEAD_DIM] = _rope(qh * gq_ref[...], cb[...], sb[...], _QB).astype(BF16)

    return pl.pallas_call(
        body,
        name="qprep_fwd",
        grid=(n // tr,),
        in_specs=[_rows(tr, QC), _bcast(MLA_Q_LORA), _bcast(GQA_HEAD_DIM), _rows(tr, LANES), _rows(tr, LANES)],
        out_specs=[_rows(tr, MLA_Q_LORA), _rows(tr, nq)],
        out_shape=[jax.ShapeDtypeStruct((n, MLA_Q_LORA), BF16), jax.ShapeDtypeStruct((n, nq), BF16)],
        compiler_params=_params(("parallel",)),
    )(pq, qg, gq, cos_b, ss_b)


def _qprep_bwd(pq, dcq, dqb, qg, gq, cos_b, ss_b, tr):
    n = pq.shape[0]
    nq = GQA_HEADS * GQA_HEAD_DIM

    def body(p_ref, dcq_ref, dqb_ref, qg_ref, gq_ref, cb, sb, dp_ref, dqg_ref, dgq_ref):
        _acc_init(pl.program_id(0), [dqg_ref, dgq_ref])
        xh, r = _rms(p_ref[:, :MLA_Q_LORA])
        dn = dcq_ref[...]
        dqg_ref[...] += _csum(dn * xh)
        dp_ref[:, :MLA_Q_LORA] = _rms_bwd(xh, r, dn * qg_ref[...]).astype(BF16)
        for h in range(GQA_HEADS):
            lo = MLA_Q_LORA + h * GQA_HEAD_DIM
            qh, rq = _rms(p_ref[:, lo : lo + GQA_HEAD_DIM])
            dq = _rope_t(dqb_ref[:, h * GQA_HEAD_DIM : (h + 1) * GQA_HEAD_DIM], cb[...], sb[...], _QB)
            dgq_ref[...] += _csum(dq * qh)
            dp_ref[:, lo : lo + GQA_HEAD_DIM] = _rms_bwd(qh, rq, dq * gq_ref[...]).astype(BF16)

    return pl.pallas_call(
        body,
        name="qprep_bwd",
        grid=(n // tr,),
        in_specs=[_rows(tr, QC), _rows(tr, MLA_Q_LORA), _rows(tr, nq), _bcast(MLA_Q_LORA), _bcast(GQA_HEAD_DIM),
                  _rows(tr, LANES), _rows(tr, LANES)],
        out_specs=[_rows(tr, QC), _bcast(MLA_Q_LORA), _bcast(GQA_HEAD_DIM)],
        out_shape=[jax.ShapeDtypeStruct((n, QC), BF16), jax.ShapeDtypeStruct((1, MLA_Q_LORA), F32),
                   jax.ShapeDtypeStruct((1, GQA_HEAD_DIM), F32)],
        compiler_params=_params(("arbitrary",)),
    )(pq, dcq, dqb, qg, gq, cos_b, ss_b)


_QA_COLS = MLA_HEADS * (MLA_NOPE + MLA_ROPE)


def _qrope_fwd(qa, cos_a, ss_a, tr):
    n = qa.shape[0]

    def body(q_ref, ca, sa, o_ref):
        for j in range(MLA_HEADS // 2):
            lo = 3 * j * LANES
            o_ref[:, lo : lo + 2 * LANES] = q_ref[:, lo : lo + 2 * LANES].astype(BF16)
            o_ref[:, lo + 2 * LANES : lo + 3 * LANES] = _rope(q_ref[:, lo + 2 * LANES : lo + 3 * LANES], ca[...], sa[...], _QA).astype(BF16)

    return pl.pallas_call(
        body,
        name="qrope_fwd",
        grid=(n // tr,),
        in_specs=[_rows(tr, _QA_COLS), _rows(tr, LANES), _rows(tr, LANES)],
        out_specs=_rows(tr, _QA_COLS),
        out_shape=jax.ShapeDtypeStruct((n, _QA_COLS), BF16),
        compiler_params=_params(("parallel",)),
    )(qa, cos_a, ss_a)


def _qrope_bwd(dq2, cos_a, ss_a, tr):
    n = dq2.shape[0]

    def body(d_ref, ca, sa, o_ref):
        for j in range(MLA_HEADS // 2):
            lo = 3 * j * LANES
            h0, h1 = 2 * j, 2 * j + 1
            o_ref[:, lo : lo + LANES] = d_ref[:, 2 * h0 * LANES : (2 * h0 + 1) * LANES].astype(BF16)
            o_ref[:, lo + LANES : lo + 2 * LANES] = d_ref[:, 2 * h1 * LANES : (2 * h1 + 1) * LANES].astype(BF16)
            pe = d_ref[:, (2 * h0 + 1) * LANES : (2 * h0 + 2) * LANES] + d_ref[:, (2 * h1 + 1) * LANES : (2 * h1 + 2) * LANES]
            o_ref[:, lo + 2 * LANES : lo + 3 * LANES] = _rope_t(pe, ca[...], sa[...], _QA).astype(BF16)

    return pl.pallas_call(
        body,
        name="qrope_bwd",
        grid=(n // tr,),
        in_specs=[_rows(tr, MLA_HEADS * 2 * LANES), _rows(tr, LANES), _rows(tr, LANES)],
        out_specs=_rows(tr, _QA_COLS),
        out_shape=jax.ShapeDtypeStruct((n, _QA_COLS), BF16),
        compiler_params=_params(("parallel",)),
    )(dq2, cos_a, ss_a)


def _cat(refs):
    vals = [r[...] for r in refs]
    return vals[0] if len(vals) == 1 else jnp.concatenate(vals, axis=-1)


LOG2E = 1.4426950408889634


def _attn_fwd(qparts, kparts, vpart, n_heads, group, dv, scale, name, tq, after=None):
    T, Tk = qparts[0][0].shape[0], kparts[0][0].shape[0]
    nq_, nk_ = len(qparts), len(kparts)
    sub = min(tq, 256)
    c2 = scale * LOG2E

    def body(*refs):
        q_refs, k_refs = refs[:nq_], refs[nq_ : nq_ + nk_]
        v_ref = refs[nq_ + nk_]
        o_ref, lse_ref = refs[-2:]
        k = _cat(k_refs)
        v = v_ref[...]
        for r0 in range(0, tq, sub):
            q = _cat([r.at[r0 : r0 + sub, :] for r in q_refs])
            s = lax.dot_general(q, k, _DIMS["NT"], preferred_element_type=F32)
            m = jnp.max(s, axis=-1, keepdims=True)
            p = jnp.exp2((s - m) * c2)
            l = jnp.sum(p, axis=-1, keepdims=True)
            acc = jnp.dot(p.astype(BF16), v, preferred_element_type=F32)
            o_ref[r0 : r0 + sub, :] = (acc * (1.0 / l)).astype(BF16)
            lse_ref[r0 : r0 + sub, :] = m * scale + jnp.log(l)

    in_specs = [pl.BlockSpec((tq, LANES), lambda h, i, f=f: (i, f(h))) for _, f in qparts]
    in_specs += [pl.BlockSpec((Tk, LANES), lambda h, i, f=f: (0, f(h // group))) for _, f in kparts]
    fv = vpart[1]
    in_specs.append(pl.BlockSpec((Tk, dv), lambda h, i: (0, fv(h // group))))
    args = [*[a for a, _ in qparts], *[a for a, _ in kparts], vpart[0]]
    if after is not None:
        in_specs.append(pl.BlockSpec(after.shape, lambda h, i: (0, 0)))
        args.append(after)
    return pl.pallas_call(
        body,
        name=name,
        grid=(n_heads, T // tq),
        in_specs=in_specs,
        out_specs=[pl.BlockSpec((tq, dv), lambda h, i: (i, h)), pl.BlockSpec((None, tq, 1), lambda h, i: (h, i, 0))],
        out_shape=[jax.ShapeDtypeStruct((T, n_heads * dv), BF16), jax.ShapeDtypeStruct((n_heads, T, 1), F32)],
        compiler_params=_params(("parallel", "parallel")),
    )(*args)


def _attn_bwd(qparts, kparts, vpart, o, do, lse, n_heads, group, dv, scale, name, tq):
    T, Tk = qparts[0][0].shape[0], kparts[0][0].shape[0]
    nq_, nk_ = len(qparts), len(kparts)
    dk_ = LANES * nq_
    n_kv = n_heads // group
    nblk = T // tq
    c2 = scale * LOG2E

    def head(hk, i):
        return hk * group + i // nblk

    sub = min(tq, 256)

    def body(*refs):
        q_refs = refs[:nq_]
        k = _cat(refs[nq_ : nq_ + nk_])
        v_ref, o_ref, do_ref, lse_ref, dq_ref, dk_ref, dv_ref = refs[nq_ + nk_ :]
        i = pl.program_id(1)
        _acc_init(i, [dk_ref, dv_ref])
        v = v_ref[...]
        dk_acc, dv_acc = None, None
        for r0 in range(0, tq, sub):
            rows = slice(r0, r0 + sub)
            q = _cat([r.at[rows, :] for r in q_refs])
            s = lax.dot_general(q, k, _DIMS["NT"], preferred_element_type=F32)
            p = jnp.exp2(s * c2 - lse_ref[rows, :] * LOG2E)
            dov = do_ref[rows, :]
            dp = lax.dot_general(dov, v, _DIMS["NT"], preferred_element_type=F32)
            delta = jnp.sum(dov.astype(F32) * o_ref[rows, :].astype(F32), axis=-1, keepdims=True)
            ds = (p * (dp - delta)).astype(BF16)
            dq_ref[rows, :] = jnp.dot(ds, k, preferred_element_type=F32) * scale
            dk_part = lax.dot_general(ds, q, _DIMS["TN"], preferred_element_type=F32)
            dv_part = lax.dot_general(p.astype(BF16), dov, _DIMS["TN"], preferred_element_type=F32)
            dk_acc = dk_part if dk_acc is None else dk_acc + dk_part
            dv_acc = dv_part if dv_acc is None else dv_acc + dv_part
        dk_ref[...] += dk_acc
        dv_ref[...] += dv_acc

        @pl.when(i == group * nblk - 1)
        def _():
            dk_ref[...] *= scale

    in_specs = [pl.BlockSpec((tq, LANES), lambda hk, i, f=f: (i % nblk, f(head(hk, i)))) for _, f in qparts]
    in_specs += [pl.BlockSpec((Tk, LANES), lambda hk, i, f=f: (0, f(hk))) for _, f in kparts]
    fv = vpart[1]
    in_specs.append(pl.BlockSpec((Tk, dv), lambda hk, i: (0, fv(hk))))
    in_specs += [pl.BlockSpec((tq, dv), lambda hk, i: (i % nblk, head(hk, i)))] * 2
    in_specs.append(pl.BlockSpec((None, tq, 1), lambda hk, i: (head(hk, i), i % nblk, 0)))
    return pl.pallas_call(
        body,
        name=name,
        grid=(n_kv, group * nblk),
        in_specs=in_specs,
        out_specs=[pl.BlockSpec((tq, dk_), lambda hk, i: (i % nblk, head(hk, i))),
                   pl.BlockSpec((Tk, dk_), lambda hk, i: (0, hk)),
                   pl.BlockSpec((Tk, dv), lambda hk, i: (0, hk))],
        out_shape=[jax.ShapeDtypeStruct((T, n_heads * dk_), F32), jax.ShapeDtypeStruct((Tk, n_kv * dk_), F32),
                   jax.ShapeDtypeStruct((Tk, n_kv * dv), F32)],
        compiler_params=_params(("parallel", "arbitrary")),
    )(*[a for a, _ in qparts], *[a for a, _ in kparts], vpart[0], o, do, lse)


def _gates_fwd(pg, ya, yb, tr):
    n, d = ya.shape

    def body(pg_ref, ya_ref, yb_ref, o_ref):
        ga = jax.nn.sigmoid(pg_ref[:, :d].astype(F32))
        gb = jax.nn.sigmoid(pg_ref[:, d:].astype(F32))
        o_ref[...] = (ga * ya_ref[...].astype(F32) + gb * yb_ref[...].astype(F32)).astype(BF16)

    return pl.pallas_call(
        body,
        name="gates_fwd",
        grid=(n // tr,),
        in_specs=[_rows(tr, 2 * d), _rows(tr, d), _rows(tr, d)],
        out_specs=_rows(tr, d),
        out_shape=jax.ShapeDtypeStruct((n, d), BF16),
        compiler_params=_params(("parallel",)),
    )(pg, ya, yb)


def _gates_bwd(dm, pg, ya, yb, tr):
    n, d = ya.shape

    def body(dm_ref, pg_ref, ya_ref, yb_ref, dya_ref, dyb_ref, dpg_ref):
        dmv = dm_ref[...].astype(F32)
        ga = jax.nn.sigmoid(pg_ref[:, :d].astype(F32))
        gb = jax.nn.sigmoid(pg_ref[:, d:].astype(F32))
        dya_ref[...] = (dmv * ga).astype(BF16)
        dyb_ref[...] = (dmv * gb).astype(BF16)
        dpg_ref[:, :d] = (dmv * ya_ref[...].astype(F32) * ga * (1.0 - ga)).astype(BF16)
        dpg_ref[:, d:] = (dmv * yb_ref[...].astype(F32) * gb * (1.0 - gb)).astype(BF16)

    return pl.pallas_call(
        body,
        name="gates_bwd",
        grid=(n // tr,),
        in_specs=[_rows(tr, d), _rows(tr, 2 * d), _rows(tr, d), _rows(tr, d)],
        out_specs=[_rows(tr, d), _rows(tr, d), _rows(tr, 2 * d)],
        out_shape=[jax.ShapeDtypeStruct((n, d), BF16), jax.ShapeDtypeStruct((n, d), BF16), jax.ShapeDtypeStruct((n, 2 * d), BF16)],
        compiler_params=_params(("parallel",)),
    )(dm, pg, ya, yb)


def _resid_norm2_fwd(x2d, att, g1, n2g, sh2, sc2, tr):
    n, d = x2d.shape

    def body(x_ref, a_ref, g1_ref, g_ref, sh_ref, sc_ref, x1_ref, z_ref):
        x1 = x_ref[...] + g1_ref[...] * a_ref[...]
        x1_ref[...] = x1
        xh, _ = _rms(x1)
        z_ref[...] = ((xh * g_ref[...]) * (1.0 + sc_ref[...]) + sh_ref[...]).astype(BF16)

    return pl.pallas_call(
        body,
        name="resid_norm2_fwd",
        grid=(n // tr,),
        in_specs=[_rows(tr, d), _rows(tr, d)] + [_bcast(d)] * 4,
        out_specs=[_rows(tr, d), _rows(tr, d)],
        out_shape=[jax.ShapeDtypeStruct((n, d), F32), jax.ShapeDtypeStruct((n, d), BF16)],
        compiler_params=_params(("parallel",)),
    )(x2d, att, g1, n2g, sh2, sc2)


def _resid_norm2_bwd(dz2, x1, dx2, att, n2g, sc2, g1, tr):
    n, d = x1.shape

    def body(dz_ref, x1_ref, dx2_ref, a_ref, g_ref, sc_ref, g1_ref, dx1_ref, da_ref, dg_ref, dsh_ref, dsc_ref, dg1_ref):
        _acc_init(pl.program_id(0), [dg_ref, dsh_ref, dsc_ref, dg1_ref])
        xh, r = _rms(x1_ref[...])
        dzv = dz_ref[...]
        gv = g_ref[...]
        dsc_ref[...] += _csum(dzv * (xh * gv))
        dsh_ref[...] += _csum(dzv)
        dh = dzv * (1.0 + sc_ref[...])
        dg_ref[...] += _csum(dh * xh)
        dx1 = _rms_bwd(xh, r, dh * gv) + dx2_ref[...]
        dx1_ref[...] = dx1
        dg1_ref[...] += _csum(dx1 * a_ref[...])
        da_ref[...] = (dx1 * g1_ref[...]).astype(BF16)

    return pl.pallas_call(
        body,
        name="resid_norm2_bwd",
        grid=(n // tr,),
        in_specs=[_rows(tr, d)] * 4 + [_bcast(d)] * 3,
        out_specs=[_rows(tr, d), _rows(tr, d)] + [_bcast(d)] * 4,
        out_shape=[jax.ShapeDtypeStruct((n, d), F32), jax.ShapeDtypeStruct((n, d), BF16)] + [jax.ShapeDtypeStruct((1, d), F32)] * 4,
        compiler_params=_params(("arbitrary",)),
    )(dz2, x1, dx2, att, n2g, sc2, g1)


def _edges(shape):
    row = lax.broadcasted_iota(jnp.int32, shape, 0)
    return row == 0, row == shape[0] - 1


def _shifts(u, edges):
    n = u.shape[0]
    return jnp.where(edges[0], 0.0, pltpu.roll(u, 1, 0)), jnp.where(edges[1], 0.0, pltpu.roll(u, n - 1, 0))


def _conv3(u, prev, nxt, w_ref, b_ref):
    return b_ref[...] + w_ref[0:1, :] * prev + w_ref[1:2, :] * u + w_ref[2:3, :] * nxt


def _ffn_up_conv(z, wup_t, cw, cb, tc, after):
    n, d = z.shape
    f = wup_t.shape[0] // 2
    nb = f // tc

    def body(z_ref, wa_ref, wb_ref, cwa, cwb, cba, cbb, after_ref, ua_ref, ub_ref, h_ref):
        w = jnp.concatenate([wa_ref[...], wb_ref[...]], axis=0)
        u = lax.dot_general(z_ref[...], w, _DIMS["NT"], preferred_element_type=F32).astype(BF16)
        ua_ref[...] = u[:, :tc]
        ub_ref[...] = u[:, tc:]
        edges = _edges((n, tc))
        ua = u[:, :tc].astype(F32)
        ub = u[:, tc:].astype(F32)
        a = _conv3(ua, *_shifts(ua, edges), cwa, cba)
        b = _conv3(ub, *_shifts(ub, edges), cwb, cbb)
        h_ref[...] = (a * jax.nn.sigmoid(a) * b).astype(BF16)

    col = lambda rows, off: pl.BlockSpec((rows, tc), lambda i: (0, i + off))
    w_rows = lambda off: pl.BlockSpec((tc, d), lambda i: (i + off, 0))
    return pl.pallas_call(
        body,
        name="ffn_up_conv",
        grid=(nb,),
        in_specs=[pl.BlockSpec((n, d), lambda i: (0, 0)), w_rows(0), w_rows(nb), col(3, 0), col(3, nb), col(1, 0), col(1, nb),
                  pl.BlockSpec(after.shape, lambda i: (0, 0))],
        out_specs=[col(n, 0)] * 3,
        out_shape=[jax.ShapeDtypeStruct((n, f), BF16)] * 3,
        compiler_params=_params(("parallel",)),
    )(z, wup_t, wup_t, cw, cw, cb, cb, after)


def _ffn_down_dx_conv_bwd(df, wdown, u_a, u_b, cw, cb, tc, after):
    n, f = u_a.shape
    d = df.shape[1]
    nb = f // tc

    def part(uv, prev, nxt, duc, edges, w_ref, du_ref, dw_ref, db_ref):
        db_ref[...] = _csum(duc)
        dw_ref[0:1, :] = _csum(duc * prev)
        dw_ref[1:2, :] = _csum(duc * uv)
        dw_ref[2:3, :] = _csum(duc * nxt)
        d_prev, d_next = _shifts(duc, edges)
        du_ref[...] = (w_ref[0:1, :] * d_next + w_ref[1:2, :] * duc + w_ref[2:3, :] * d_prev).astype(BF16)

    def body(df_ref, wd_ref, ua_ref, ub_ref, wa_ref, wb_ref, ba_ref, bb_ref, after_ref,
             dua_ref, dub_ref, dwa_ref, dwb_ref, dba_ref, dbb_ref):
        dhv = lax.dot_general(df_ref[...], wd_ref[...], _DIMS["NT"], preferred_element_type=F32)
        dhv = dhv.astype(BF16).astype(F32)
        edges = _edges((n, tc))
        ua = ua_ref[...].astype(F32)
        ub = ub_ref[...].astype(F32)
        sa = _shifts(ua, edges)
        sb = _shifts(ub, edges)
        a = _conv3(ua, *sa, wa_ref, ba_ref)
        b = _conv3(ub, *sb, wb_ref, bb_ref)
        sg = jax.nn.sigmoid(a)
        da = dhv * b * (sg * (1.0 + a * (1.0 - sg)))
        db = dhv * (a * sg)
        part(ua, *sa, da, edges, wa_ref, dua_ref, dwa_ref, dba_ref)
        part(ub, *sb, db, edges, wb_ref, dub_ref, dwb_ref, dbb_ref)

    col = lambda rows, off: pl.BlockSpec((rows, tc), lambda i: (0, i + off))
    return pl.pallas_call(
        body,
        name="ffn_down_dx_conv_bwd",
        grid=(nb,),
        in_specs=[pl.BlockSpec((n, d), lambda i: (0, 0)), pl.BlockSpec((tc, d), lambda i: (i, 0)), col(n, 0), col(n, 0),
                  col(3, 0), col(3, nb), col(1, 0), col(1, nb), pl.BlockSpec(after.shape, lambda i: (0, 0))],
        out_specs=[col(n, 0), col(n, 0), col(3, 0), col(3, 0), col(1, 0), col(1, 0)],
        out_shape=[jax.ShapeDtypeStruct((n, f), BF16)] * 2 + [jax.ShapeDtypeStruct((3, f), F32)] * 2 + [jax.ShapeDtypeStruct((1, f), F32)] * 2,
        compiler_params=_params(("parallel",)),
    )(df, wdown, u_a, u_b, cw, cw, cb, cb, after)


def _loss_head(x1, f, g2, fg, tgt, tr):
    n, d = x1.shape

    def body(x1_ref, f_ref, g2_ref, fg_ref, t_ref, sq_ref, dx2_ref, dfg_ref, dg2_ref, df_ref):
        _acc_init(pl.program_id(0), [sq_ref, dfg_ref, dg2_ref])
        fv = f_ref[...]
        xh, r = _rms(x1_ref[...] + g2_ref[...] * fv)
        err = xh * fg_ref[...] - t_ref[...]
        sq_ref[...] += _csum(err * err)
        dy = err * (1.0 / d)
        dfg_ref[...] += _csum(dy * xh)
        dx2 = _rms_bwd(xh, r, dy * fg_ref[...])
        dx2_ref[...] = dx2
        dg2_ref[...] += _csum(dx2 * fv)
        df_ref[...] = (dx2 * g2_ref[...]).astype(BF16)

    return pl.pallas_call(
        body,
        name="loss_head",
        grid=(n // tr,),
        in_specs=[_rows(tr, d), _rows(tr, d), _bcast(d), _bcast(d), _rows(tr, d)],
        out_specs=[_bcast(d), _rows(tr, d), _bcast(d), _bcast(d), _rows(tr, d)],
        out_shape=[jax.ShapeDtypeStruct((1, d), F32), jax.ShapeDtypeStruct((n, d), F32), jax.ShapeDtypeStruct((1, d), F32),
                   jax.ShapeDtypeStruct((1, d), F32), jax.ShapeDtypeStruct((n, d), BF16)],
        compiler_params=_params(("arbitrary",)),
    )(x1, f, g2, fg, tgt)


def _sum_slots(g, name):
    s, r, w = g.shape

    def body(g_ref, o_ref):
        acc = g_ref[0]
        for k in range(1, s):
            acc = acc + g_ref[k]
        o_ref[...] = acc

    return pl.pallas_call(body, name=name, out_shape=jax.ShapeDtypeStruct((r, w), F32))(g)


def _silu_grad_mul(ds, cvec):
    def body(d_ref, c_ref, o_ref):
        cv = c_ref[...]
        sg = jax.nn.sigmoid(cv)
        o_ref[...] = d_ref[...] * (sg * (1.0 + cv * (1.0 - sg)))

    return pl.pallas_call(body, name="silu_grad_mul", out_shape=jax.ShapeDtypeStruct(ds.shape, F32))(ds, cvec)


def _adamw_update(wv, gv, mv, vv, d_ref, mo_ref, vo_ref):
    mn = ADAM_B1 * mv + (1.0 - ADAM_B1) * gv
    vn = ADAM_B2 * vv + (1.0 - ADAM_B2) * (gv * gv)
    mo_ref[...] = mn
    vo_ref[...] = vn
    m_hat = mn / (1.0 - ADAM_B1**ADAM_STEP)
    v_hat = vn / (1.0 - ADAM_B2**ADAM_STEP)
    d_ref[...] = -ADAM_LR * (m_hat / (jnp.sqrt(v_hat) + ADAM_EPS) + ADAM_WD * wv)


def _adamw_many(ws, gs, ms, vs, name):
    n = len(ws)

    def body(*refs):
        for k in range(n):
            w_ref, g_ref, m_ref, v_ref = (refs[q * n + k] for q in range(4))
            d_ref, mo_ref, vo_ref = (refs[(4 + q) * n + k] for q in range(3))
            _adamw_update(w_ref[...], g_ref[...], m_ref[...], v_ref[...], d_ref, mo_ref, vo_ref)

    res = pl.pallas_call(body, name=name, out_shape=[jax.ShapeDtypeStruct(w.shape, F32) for w in ws] * 3)(*ws, *gs, *ms, *vs)
    return res[:n], res[n : 2 * n], res[2 * n :]


def _adamw(w, g, m, v, name, g_transposed=False, g_sibling=None):
    r, cdim = w.shape
    halves = g_sibling is not None
    block = 1 << 19
    if g_transposed:
        tc = _pick(cdim // 2 if halves else cdim, 2048)
        tr = _pick(r, max(LANES, block // tc), LANES)
        per_half = (cdim // 2) // tc
    else:
        rows = r // 2 if halves else r
        tc = _pick(cdim, 2048)
        tr = _pick(rows, max(8, block // tc), 8)
        if tr < 64 and rows > 64:
            tr, tc = _pick(rows, 1024, 8), _pick(cdim, 512)
        per_half = (r // 2) // tr
    emit_g = g_transposed or halves

    def body(w_ref, g_ref, *rest):
        m_ref, v_ref = rest[halves : halves + 2]
        outs = rest[halves + 2 :]
        gv = g_ref[...]
        if halves:
            along = pl.program_id(1 if g_transposed else 0)
            gv = jnp.where(along // per_half == lax.axis_index("c"), gv, rest[0][...])
        if g_transposed:
            gv = gv.T
        if emit_g:
            outs[0][...] = gv
        _adamw_update(w_ref[...], gv, m_ref[...], v_ref[...], *outs[-3:])

    spec = pl.BlockSpec((tr, tc), lambda i, j: (i, j))
    if g_transposed:
        g_spec = pl.BlockSpec((tc, tr), lambda i, j: (j % per_half if halves else j, i))
    else:
        g_spec = pl.BlockSpec((tr, tc), lambda i, j: (i % per_half if halves else i, j))
    n_out = 3 + emit_g
    res = pl.pallas_call(
        body,
        name=name,
        grid=(r // tr, cdim // tc),
        in_specs=[spec, g_spec] + [g_spec] * halves + [spec, spec],
        out_specs=[spec] * n_out,
        out_shape=[jax.ShapeDtypeStruct((r, cdim), F32)] * n_out,
        compiler_params=_params(("parallel", "parallel")),
    )(w, g, *([g_sibling] if halves else []), m, v)
    return res if emit_g else [g, *res]


def _place():
    return lax.axis_index("x"), lax.axis_index("y"), lax.axis_index("c")


def _remote(src, dst, send_sem, recv_sem, dev):
    return pltpu.make_async_remote_copy(src_ref=src, dst_ref=dst, send_sem=send_sem, recv_sem=recv_sem, device_id=dev, device_id_type=MESH)


ANY = pl.BlockSpec(memory_space=pl.ANY)


def _all_gather_small(v, name, after=()):
    r, w = v.shape

    def body(v_ref, *rest):
        o_ref, send, recv, lsem = rest[len(after) :]
        x, y, c = _place()
        me = 4 * x + 2 * y + c
        mine = pltpu.make_async_copy(v_ref, o_ref.at[me], lsem)
        mine.start()
        sent = []
        for k in range(1, 8):
            px, py, pc = x ^ (k >> 2), y ^ ((k >> 1) & 1), c ^ (k & 1)
            cp = _remote(v_ref, o_ref.at[me], send.at[k - 1], recv.at[k - 1], (px, py, pc))
            cp.start()
            sent.append(cp)
        for k in range(1, 8):
            px, py, pc = x ^ (k >> 2), y ^ ((k >> 1) & 1), c ^ (k & 1)
            slot = o_ref.at[4 * px + 2 * py + pc]
            _remote(slot, slot, send.at[k - 1], recv.at[k - 1], (x, y, c)).wait_recv()
        for cp in sent:
            cp.wait_send()
        mine.wait()

    return pl.pallas_call(
        body,
        name=name,
        out_shape=jax.ShapeDtypeStruct((8, r, w), F32),
        in_specs=[pl.BlockSpec(memory_space=pltpu.VMEM)] + [ANY] * len(after),
        out_specs=pl.BlockSpec(memory_space=pltpu.VMEM),
        scratch_shapes=[pltpu.SemaphoreType.DMA((7,)), pltpu.SemaphoreType.DMA((7,)), pltpu.SemaphoreType.DMA],
        compiler_params=pltpu.CompilerParams(vmem_limit_bytes=VMEM_LIMIT),
    )(v, *after)


HBM = pl.BlockSpec(memory_space=pltpu.HBM)
SEM = pl.BlockSpec(memory_space=pltpu.SEMAPHORE)
EFFECT = pltpu.SideEffectType.DATAFLOW_SIDE_EFFECTING


def _other_chips(x, y):
    return [(1 - x, y), (x, 1 - y), (1 - x, 1 - y)]


def _bulk_start(name, srcs, land_shapes, n_copies, copies, after, lands_init=None):
    n, m = len(srcs), len(land_shapes)

    def body(*refs):
        src_refs, land_refs = refs[:n], refs[n : n + m]
        send, recv = refs[n + m + 1], refs[n + m + 2]
        token = refs[-1]
        for k, (s, d, dev) in enumerate(copies(src_refs, land_refs)):
            _remote(s, d, send.at[k], recv.at[k], dev).start()
        token[...] = jnp.zeros_like(token)

    lands = lands_init or [lax.empty(s.shape, s.dtype) for s in land_shapes]
    lands = [pltpu.with_memory_space_constraint(b, pltpu.HBM) for b in lands]
    out = pl.pallas_call(
        body,
        name=name,
        out_shape=(pltpu.SemaphoreType.DMA((n_copies,)), pltpu.SemaphoreType.DMA((n_copies,)),
                   *[pltpu.HBM(s.shape, s.dtype) for s in srcs], *[pltpu.HBM(s.shape, s.dtype) for s in land_shapes],
                   jax.ShapeDtypeStruct((8, LANES), F32)),
        in_specs=[HBM] * (n + m) + [ANY],
        out_specs=(SEM, SEM, *[HBM] * (n + m), pl.BlockSpec(memory_space=pltpu.VMEM)),
        input_output_aliases={i: 2 + i for i in range(n + m)},
        compiler_params=pltpu.CompilerParams(has_side_effects=EFFECT),
    )(*[pltpu.with_memory_space_constraint(s, pltpu.HBM) for s in srcs], *lands, after)
    return out[0], out[1], list(out[2 : 2 + n]), list(out[2 + n : 2 + n + m]), out[-1][0:1, 0:1]


def _bulk_wait(name, send, recv, srcs, lands, after, waits):
    n, m = len(srcs), len(lands)

    def body(*refs):
        src_refs, land_refs = refs[:n], refs[n : n + m]
        send_sem, recv_sem = refs[n + m], refs[n + m + 1]
        x, y, c = _place()
        for k, (s, d) in enumerate(waits(src_refs, land_refs)):
            cp = _remote(s, d, send_sem.at[k], recv_sem.at[k], (x, y, c))
            cp.wait_send()
            cp.wait_recv()

    out = pl.pallas_call(
        body,
        name=name,
        out_shape=tuple(pltpu.HBM(s.shape, s.dtype) for s in (*srcs, *lands)),
        in_specs=[HBM] * (n + m) + [SEM, SEM, ANY],
        out_specs=tuple([HBM] * (n + m)),
        input_output_aliases={i: i for i in range(n + m)},
        compiler_params=pltpu.CompilerParams(has_side_effects=EFFECT),
    )(*srcs, *lands, send, recv, after)
    return list(out[:n]), list(out[n:])


def _peers(x, y, c):
    return [(x ^ (k >> 2), y ^ ((k >> 1) & 1), c ^ (k & 1)) for k in range(1, 8)]


def _small_gather_start(v, after, name):
    r, w = v.shape

    def copies(src, land):
        x, y, c = _place()
        return [(src[0], land[0].at[4 * x + 2 * y + c], peer) for peer in _peers(x, y, c)]

    me = 4 * lax.axis_index("x") + 2 * lax.axis_index("y") + lax.axis_index("c")
    init = [lax.dynamic_update_slice(lax.empty((8, r, w), F32), v[None], (me, 0, 0))]
    return _bulk_start(name, [v], [jax.ShapeDtypeStruct((8, r, w), F32)], 7, copies, after, init)


def _small_gather_wait(started, after, name):
    send, recv, srcs, lands, _ = started

    def waits(src, land):
        x, y, c = _place()
        return [(src[0], land[0].at[4 * px + 2 * py + pc]) for px, py, pc in _peers(x, y, c)]

    return _bulk_wait(name, send, recv, srcs, lands, after, waits)[1][0]


def _gather_start(shards, after, name):
    def copies(src, land):
        x, y, c = _place()
        j = 2 * x + y
        return [(src[a].at[c], land[a].at[j, c], (px, py, c)) for a in range(len(shards)) for px, py in _other_chips(x, y)]

    shapes = [jax.ShapeDtypeStruct((4,) + s.shape, s.dtype) for s in shards]
    j = 2 * lax.axis_index("x") + lax.axis_index("y")
    init = [lax.dynamic_update_slice(lax.empty(t.shape, t.dtype), s[None], (j, 0, 0, 0)) for t, s in zip(shapes, shards)]
    return _bulk_start(name, shards, shapes, 3 * len(shards), copies, after, init)


def _gather_wait(started, after, name):
    send, recv, srcs, lands, _ = started

    def waits(src, land):
        x, y, c = _place()
        return [(src[a].at[c], land[a].at[2 * px + py, c]) for a in range(len(srcs)) for px, py in _other_chips(x, y)]

    return _bulk_wait(name, send, recv, srcs, lands, after, waits)


def _forward_start(lands, after, name):
    def copies(src, _):
        x, y, c = _place()
        blocks = [src[a].at[2 * px + py, c] for a in range(len(lands)) for px, py in _other_chips(x, y)]
        return [(b, b, (x, y, 1 - c)) for b in blocks]

    return _bulk_start(name, lands, [], 3 * len(lands), copies, after)


def _forward_wait(started, after, name):
    send, recv, bufs, _, _ = started

    def waits(src, _):
        x, y, c = _place()
        return [(src[a].at[2 * px + py, c], src[a].at[2 * px + py, 1 - c]) for a in range(len(bufs)) for px, py in _other_chips(x, y)]

    return _bulk_wait(name, send, recv, bufs, [], after, waits)[0]


def _as_rows(lands):
    return [f.reshape(4 * f.shape[2] * 2, f.shape[3]) for f in lands]


def _gather_land(started, after, tag):
    shards, lands = _gather_wait(started, after, "gather_wait_" + tag)
    return shards, _forward_start(lands, shards[0], "forward_start_" + tag)


def _gather_done(landed, after, tag):
    _, fwd = landed
    return _as_rows(_forward_wait(fwd, after, "forward_wait_" + tag))


def _swap_halves(grads, name):
    n = len(grads)

    def body(*refs):
        ins, outs = refs[:n], refs[n : 2 * n]
        send, recv = refs[2 * n :]
        x, y, c = _place()
        started = []
        for a in range(n):
            for s in range(4):
                cp = _remote(ins[a].at[s, 1 - c], outs[a].at[s], send.at[4 * a + s], recv.at[4 * a + s], (x, y, 1 - c))
                cp.start()
                started.append(cp)
        for cp in started:
            cp.wait_recv()
        for cp in started:
            cp.wait_send()

    return pl.pallas_call(
        body,
        name=name,
        out_shape=[jax.ShapeDtypeStruct((4,) + g.shape[2:], g.dtype) for g in grads],
        in_specs=[ANY] * n,
        out_specs=[ANY] * n,
        scratch_shapes=[pltpu.SemaphoreType.DMA((4 * n,)), pltpu.SemaphoreType.DMA((4 * n,))],
    )(*grads)


def _add_halves(grads, others, tag):
    outs = []
    for a, (g, o) in enumerate(zip(grads, others)):
        _, _, rh, cdim = g.shape
        tr = _pick(rh, 512, 16)

        def body(g_ref, o_ref, p_ref):
            p_ref[...] = (g_ref[...].astype(F32) + o_ref[...].astype(F32)).astype(BF16)

        outs.append(
            pl.pallas_call(
                body,
                name=f"add_halves_{tag}{a}",
                grid=(4, rh // tr),
                in_specs=[pl.BlockSpec((None, None, tr, cdim), lambda s, i: (s, lax.axis_index("c"), i, 0)),
                          pl.BlockSpec((None, tr, cdim), lambda s, i: (s, i, 0))],
                out_specs=pl.BlockSpec((None, tr, cdim), lambda s, i: (s, i, 0)),
                out_shape=jax.ShapeDtypeStruct((4, rh, cdim), BF16),
                compiler_params=_params(("parallel", "parallel")),
            )(g, o)
        )
    return outs


def _exchange_start(parts, after, name):
    def copies(src, land):
        x, y, c = _place()
        j = 2 * x + y
        return [(src[a].at[2 * px + py], land[a].at[j], (px, py, c)) for a in range(len(parts)) for px, py in _other_chips(x, y)]

    return _bulk_start(name, parts, [jax.ShapeDtypeStruct(p.shape, p.dtype) for p in parts], 3 * len(parts), copies, after)


def _exchange_finish(started, after, name):
    send, recv, srcs, lands, _ = started

    def waits(src, land):
        x, y, _ = _place()
        return [(src[a].at[2 * px + py], land[a].at[2 * px + py]) for a in range(len(srcs)) for px, py in _other_chips(x, y)]

    srcs, lands = _bulk_wait(name, send, recv, srcs, lands, after, waits)
    return lands, srcs


def _sum_chips(recvd, parts, tag):
    outs = []
    for a, (g, p) in enumerate(zip(recvd, parts)):
        _, rh, cdim = g.shape
        tr = _pick(rh, 512, 16)

        def body(g_ref, p_ref, o_ref):
            j = 2 * lax.axis_index("x") + lax.axis_index("y")
            own = p_ref[...].astype(F32)
            term = [jnp.where(j == s, own, g_ref[s].astype(F32)) for s in range(4)]
            o_ref[...] = ((term[0] + term[1]) + term[2]) + term[3]

        outs.append(
            pl.pallas_call(
                body,
                name=f"sum_chips_{tag}{a}",
                grid=(rh // tr,),
                in_specs=[pl.BlockSpec((4, tr, cdim), lambda i: (0, i, 0)),
                          pl.BlockSpec((None, tr, cdim), lambda i: (2 * lax.axis_index("x") + lax.axis_index("y"), i, 0))],
                out_specs=pl.BlockSpec((tr, cdim), lambda i: (i, 0)),
                out_shape=jax.ShapeDtypeStruct((rh, cdim), F32),
                compiler_params=_params(("parallel",)),
            )(g, p)
        )
    return outs


def _joined(mine, other):
    first = lax.axis_index("c") == 0
    return jnp.concatenate([jnp.where(first, mine, other), jnp.where(first, other, mine)], axis=0)


def _grad_views(grads):
    return [g.reshape(4, 2, g.shape[0] // 8, g.shape[1]) for g in grads]


def _scatter_start(grads, tag, after=None):
    views = _grad_views(grads)
    others = _swap_halves(views, "swap_halves_" + tag)
    mine = _add_halves(views, others, tag)
    return _exchange_start(mine, others[-1] if after is None else after, "exchange_start_" + tag)


def _swap_start(grads, after, tag):
    views = _grad_views(grads)

    def copies(src, land):
        x, y, c = _place()
        return [(src[a].at[s, 1 - c], land[a].at[s], (x, y, 1 - c)) for a in range(len(views)) for s in range(4)]

    shapes = [jax.ShapeDtypeStruct((4,) + v.shape[2:], v.dtype) for v in views]
    return _bulk_start("swap_start_" + tag, views, shapes, 4 * len(views), copies, after)


def _scatter_start_after_swap(swapped, after, tag):
    send, recv, views, lands, _ = swapped

    def waits(src, land):
        c = lax.axis_index("c")
        return [(src[a].at[s, 1 - c], land[a].at[s]) for a in range(len(views)) for s in range(4)]

    views, others = _bulk_wait("swap_wait_" + tag, send, recv, views, lands, after, waits)
    mine = _add_halves(views, others, tag)
    return _exchange_start(mine, others[-1], "exchange_start_" + tag)


def _join_start(halves, after, tag):
    def copies(src, land):
        x, y, c = _place()
        return [(src[a], land[a], (x, y, 1 - c)) for a in range(len(halves))]

    return _bulk_start("join_start_" + tag, halves, [jax.ShapeDtypeStruct(h.shape, h.dtype) for h in halves], len(halves), copies, after)


def _join_wait(started, after, tag):
    send, recv, halves, lands, _ = started
    halves, others = _bulk_wait("join_wait_" + tag, send, recv, halves, lands, after, lambda src, land: list(zip(src, land)))
    return list(zip(halves, others))


def _scatter_sums(started, after, tag):
    return _sum_chips(*_exchange_finish(started, after, "exchange_wait_" + tag), tag)


def _t_bf16(w):
    return w.T.astype(BF16)


def kernel(x, c, ctx, c_ctx, w_ada, b_ada, norm1_g, w_in, mla_q_norm_g, w_q_up, mla_kv_norm_g, w_kv_up, gqa_q_norm_g, gqa_k_norm_g, w_br_a, w_br_b, w_out, norm2_g, w_up, conv_w, conv_b, w_down, final_norm_g, loss_target, m_c_ctx, m_w_ada, m_b_ada, m_norm1_g, m_w_in, m_mla_q_norm_g, m_w_q_up, m_mla_kv_norm_g, m_w_kv_up, m_gqa_q_norm_g, m_gqa_k_norm_g, m_w_br_a, m_w_br_b, m_w_out, m_norm2_g, m_w_up, m_conv_w, m_conv_b, m_w_down, m_final_norm_g, v_c_ctx, v_w_ada, v_b_ada, v_norm1_g, v_w_in, v_mla_q_norm_g, v_w_q_up, v_mla_kv_norm_g, v_w_kv_up, v_gqa_q_norm_g, v_gqa_k_norm_g, v_w_br_a, v_w_br_b, v_w_out, v_norm2_g, v_w_up, v_conv_w, v_conv_b, v_w_down, v_final_norm_g):
    T, D = x.shape[1], x.shape[2]
    C = ctx.shape[1]
    NA = w_ada.shape[2]
    NW = w_up.shape[2]
    F2 = 4 * NW
    FF = F2 // 2
    xi, yi, ci = _place()
    j = 2 * xi + yi
    me = 4 * xi + 2 * yi + ci
    tr = _pick(C, 256, 8)

    x2d, tgt, ctx2d = x[0], loss_target[0], ctx[0]
    fg = final_norm_g.reshape(1, D)
    cc = c_ctx.reshape(1, D)

    halve = lambda s: s.reshape(2, s.shape[0] // 2, s.shape[1])
    win_shard = halve(_t_bf16(w_in[0]))
    w0 = max(D, NW)
    pay = jnp.zeros((8, w0), F32).at[0:1, :D].set(c).at[1:4, :NW].set(conv_w[0])
    got = _all_gather_small(pay, "gather_cond")
    ag_in = _gather_start([win_shard], got, "gather_start_in")
    t_in = ag_in[4]
    c_all = got[:, 0, :D]
    cw = jnp.concatenate([got[2 * s, 1:4, :NW] for s in range(4)], axis=1)
    s16 = jnp.concatenate([c_all, cc, jnp.zeros((7, D), F32)], axis=0) + t_in
    b_cols = lax.dynamic_slice(b_ada, (0, j * NA), (1, NA))
    ada_part = _mm(s16, w_ada[0], "NN", F32, "ada_fwd", act="silu", bias=b_cols)

    wq3 = (w_q_up[0] + t_in).reshape(MLA_Q_LORA, 2, MLA_NOPE + MLA_ROPE)
    wq_perm = jnp.concatenate([wq3[:, :, :MLA_NOPE].reshape(MLA_Q_LORA, -1), wq3[:, :, MLA_NOPE:].reshape(MLA_Q_LORA, -1)], axis=1)
    low = [halve(_t_bf16(wq_perm)), halve(_t_bf16(w_kv_up[0] + t_in))]
    br = [halve(_t_bf16(w_br_a[0] + t_in)), halve(_t_bf16(w_br_b[0] + t_in)), halve((w_out[0] + t_in).astype(BF16))]
    up = [halve(_t_bf16(w_up[0] + t_in))]
    down = [halve((w_down[0] + t_in).astype(BF16))]

    got = _all_gather_small(ada_part, "gather_ada", after=(*low, *br, *up, *down))
    ada = jnp.concatenate([got[2 * s] for s in range(4)], axis=1)
    lat = lax.dynamic_slice(ada, (me, 0), (1, 6 * D))
    sh1, sc1, g1, sh2, sc2, g2 = [lat[:, k * D : (k + 1) * D] for k in range(6)]
    csh, csc = ada[8:9, :D], ada[8:9, D : 2 * D]
    ag_low = _gather_start(low, got, "gather_start_low")
    ag_br = _gather_start(br, ag_low[4], "gather_start_br")
    ag_up = _gather_start(up, ag_br[4], "gather_start_up")
    ag_down = _gather_start(down, ag_up[4], "gather_start_down")
    sh1 = sh1 + ag_down[4]

    cos_a, ss_a = _rope_tables(C, T, MLA_ROPE)
    cos_b, ss_b = _rope_tables(C, T, GQA_HEAD_DIM)
    lcos_a, lss_a, lcos_b, lss_b = cos_a[:T], ss_a[:T], cos_b[:T], ss_b[:T]

    in_landed = _gather_land(ag_in, down[0], "in")
    z_all = _norm_mod_fwd(x2d, norm1_g, sh1 + in_landed[1][4], sc1, "norm1_lat_fwd", tr, out_rows=T + C)
    z_all = _norm_mod_fwd(ctx2d, norm1_g, csh, csc, "norm1_ctx_fwd", tr, base=z_all, out_off=T)
    (win_t,) = _gather_done(in_landed, z_all, "in")
    kv_cols = KVP - LANES + MLA_ROPE
    e_kpe = MLA_KV_LORA + MLA_ROPE
    w_kvp = jnp.concatenate([win_t[:MLA_KV_LORA], win_t[e_kpe:kv_cols], win_t[MLA_KV_LORA:e_kpe], jnp.zeros((LANES - MLA_ROPE, D), BF16)], axis=0)

    pkv = _mm(z_all, w_kvp, "NT", F32, "proj_kv", tn=KVP)
    pq = _mm(z_all, win_t, "NT", F32, "proj_q", m=T, n=QC, b_off=kv_cols)
    low_landed = _gather_land(ag_low, pq, "low")
    pg = _mm(z_all, win_t, "NT", BF16, "proj_g", m=T, n=2 * D, b_off=kv_cols + QC, after=low_landed[1][4])
    wq_t, wkv_t = _gather_done(low_landed, pg, "low")
    ckv_n, kb2, vb2, kpe2 = _kprep_fwd(pkv, mla_kv_norm_g, gqa_k_norm_g, cos_a, ss_a, cos_b, ss_b, tr)
    kv_up = _mm(ckv_n, wkv_t, "NT", BF16, "kv_up")
    cq_n, qb2 = _qprep_fwd(pq, mla_q_norm_g, gqa_q_norm_g, lcos_b, lss_b, tr)
    q_a = _mm(cq_n, wq_t, "NT", F32, "q_up")
    qar = _qrope_fwd(q_a, lcos_a, lss_a, tr)

    a_q = [(qar, lambda h: 3 * (h // 2) + h % 2), (qar, lambda h: 3 * (h // 2) + 2)]
    a_k = [(kv_up, lambda h: 2 * h), (kpe2, lambda h: h % 2)]
    a_v = (kv_up, lambda h: 2 * h + 1)
    a_scale = float(MLA_NOPE + MLA_ROPE) ** -0.5
    b_q = [(qb2, lambda h: h)]
    b_k = [(kb2, lambda h: h)]
    b_v = (vb2, lambda h: h)
    b_scale = float(GQA_HEAD_DIM) ** -0.5
    tq_f = _pick(T, 2048)
    o_a, lse_a = _attn_fwd(a_q, a_k, a_v, MLA_HEADS, 1, MLA_V, a_scale, "attn_a_fwd", tq_f)
    br_landed = _gather_land(ag_br, o_a, "br")
    o_b, lse_b = _attn_fwd(b_q, b_k, b_v, GQA_HEADS, GQA_GROUP, GQA_HEAD_DIM, b_scale, "attn_b_fwd", tq_f, after=br_landed[1][4])
    wbra_t, wbrb_t, wout = _gather_done(br_landed, o_b, "br")
    up_landed = _gather_land(ag_up, o_b, "up")
    ya = _mm(o_a, wbra_t, "NT", BF16, "br_a", after=up_landed[1][4])
    yb = _mm(o_b, wbrb_t, "NT", BF16, "br_b")
    merged = _gates_fwd(pg, ya, yb, tr)
    att = _mm(merged, wout, "NN", F32, "out_proj")
    x1, z2 = _resid_norm2_fwd(x2d, att, g1, norm2_g, sh2, sc2, tr)
    (wup_t,) = _gather_done(up_landed, z2, "up")
    down_landed = _gather_land(ag_down, z2, "down")
    tc = _pick(FF, 128)
    u_a, u_b, hg = _ffn_up_conv(z2, wup_t, cw, conv_b, tc, down_landed[1][4])
    (wdown,) = _gather_done(down_landed, hg, "down")
    f = _mm(hg, wdown, "NN", F32, "ffn_down", tk=FF // 2)
    sq, dx2, d_fg, d_g2, df = _loss_head(x1, f, g2, fg, tgt, tr)
    loss = lax.psum(0.5 * jnp.sum(sq) / D, ("x", "y", "c"))

    du_a, du_b, dcw_a, dcw_b, dcb_a, dcb_b = _ffn_down_dx_conv_bwd(df, wdown, u_a, u_b, cw, conv_b, _pick(FF, 256), loss.reshape(1, 1))
    g_wdown = _mm(hg, df, "TN", BF16, "ffn_down_dw", tm=FF // 4)
    dz2 = _mm(du_a, wup_t, "NN", F32, "ffn_up_dx_a", tk=FF // 2)
    dz2 = _mm(du_b, wup_t, "NN", F32, "ffn_up_dx_b", b_off=FF, add=dz2, tk=FF // 2)
    g_wup_t = _mm(du_a, z2, "TN", BF16, "ffn_up_dw_a", out_rows=F2, tm=FF // 4)
    g_wup_t = _mm(du_b, z2, "TN", BF16, "ffn_up_dw_b", out_base=g_wup_t, out_off=FF, tm=FF // 4)
    sw_ffn = _swap_start([g_wdown, g_wup_t], sc2, "ffn")
    sc2 = sc2 + sw_ffn[4]
    dx1, datt, d_n2g, d_sh2, d_sc2, d_g1 = _resid_norm2_bwd(dz2, x1, dx2, att, norm2_g, sc2, g1, tr)

    dmerged = _mm(datt, wout, "NT", BF16, "out_proj_dx")
    rs_ffn = _scatter_start_after_swap(sw_ffn, dmerged, "ffn")
    lse_a = lse_a + rs_ffn[4]
    g_wout = _mm(merged, datt, "TN", BF16, "out_proj_dw")
    dya, dyb, dpg = _gates_bwd(dmerged, pg, ya, yb, tr)
    do_a = _mm(dya, wbra_t, "NN", BF16, "br_a_dx")
    g_wbra_t = _mm(dya, o_a, "TN", BF16, "br_a_dw")
    do_b = _mm(dyb, wbrb_t, "NN", BF16, "br_b_dx")
    g_wbrb_t = _mm(dyb, o_b, "TN", BF16, "br_b_dw")
    dqa2, dka2, dva2 = _attn_bwd(a_q, a_k, a_v, o_a, do_a, lse_a, MLA_HEADS, 1, MLA_V, a_scale, "attn_a_bwd", tq_f)
    dqb2, dkb2, dvb2 = _attn_bwd(b_q, b_k, b_v, o_b, do_b, lse_b, GQA_HEADS, GQA_GROUP, GQA_HEAD_DIM, b_scale, "attn_b_bwd", tq_f)
    dq_a = _qrope_bwd(dqa2, lcos_a, lss_a, tr)
    dcq_n = _mm(dq_a, wq_t, "NN", F32, "q_up_dx")
    g_wq_t = _mm(dq_a, cq_n, "TN", BF16, "q_up_dw")
    dpq, d_qg, d_gq = _qprep_bwd(pq, dcq_n, dqb2, mla_q_norm_g, gqa_q_norm_g, lcos_b, lss_b, tr)
    dkv_up, dkpe = _kgrad_split(dka2, dva2, cos_a, ss_a, tr)
    dckv_n = _mm(dkv_up, wkv_t, "NN", F32, "kv_up_dx")
    g_wkv_t = _mm(dkv_up, ckv_n, "TN", BF16, "kv_up_dw")
    sw_mix = _swap_start([g_wq_t, g_wkv_t, g_wbra_t, g_wbrb_t, g_wout], mla_kv_norm_g, "mix")
    dpkv, d_kvg, d_kg = _kprep_bwd(pkv, dckv_n, dkb2, dvb2, dkpe, mla_kv_norm_g + sw_mix[4], gqa_k_norm_g, cos_b, ss_b, tr)
    dz_kv = _mm(dpkv, w_kvp, "NN", F32, "proj_kv_dx")
    rs_mix = _scatter_start_after_swap(sw_mix, dz_kv, "mix")
    dz_lat = _mm(dpq, win_t, "NN", F32, "proj_q_dx", b_off=kv_cols, add=dz_kv, after=rs_mix[4])
    dz_lat = _mm(dpg, win_t, "NN", F32, "proj_g_dx", b_off=kv_cols + QC, add=dz_lat)
    _, d_n1g_c, d_csh, d_csc = _norm_mod_bwd(dz_kv, T // tr, ctx2d, norm1_g, csc, None, "norm1_ctx_bwd", tr)
    grad_x, d_n1g_l, d_sh1, d_sc1 = _norm_mod_bwd(dz_lat, 0, x2d, norm1_g, sc1, dx1, "norm1_lat_bwd", tr)

    zeros_d = jnp.zeros((1, D), F32)
    d_lat = jnp.concatenate([d_sh1, d_sc1, d_g1, d_sh2, d_sc2, d_g2], axis=1)
    d_ctx_part = jnp.concatenate([d_csh, d_csc], axis=1)
    flat = jnp.concatenate(
        [d_n1g_c + d_n1g_l, d_qg, d_kvg, d_gq, d_kg, d_n2g, dcb_a, dcb_b, d_fg,
         dcw_a.reshape(1, -1), dcw_b.reshape(1, -1), d_ctx_part, d_lat], axis=1)
    n_flat = flat.shape[1]
    n_rows = -(-n_flat // (8 * LANES)) * 8
    flat = jnp.pad(flat, ((0, 0), (0, n_rows * LANES - n_flat))).reshape(n_rows, LANES)
    small = _small_gather_start(flat, grad_x, "small_grads_start")

    g_kvp = _mm(dpkv, z_all, "TN", BF16, "proj_kv_dw", after=small[4], tm=KVP)
    nk = MLA_KV_LORA + 2 * GQA_KV_HEADS * GQA_HEAD_DIM
    g_kv = jnp.concatenate([g_kvp[:MLA_KV_LORA], g_kvp[nk : nk + MLA_ROPE], g_kvp[MLA_KV_LORA:nk]], axis=0)
    g_win_t = _mm(dpq, z_all, "TN", BF16, "proj_q_dw", out_rows=kv_cols + QC + 2 * D, out_off=kv_cols, tm=QC // 2)
    g_win_t = _mm(dpg, z_all, "TN", BF16, "proj_g_dw", out_base=g_win_t, out_off=kv_cols + QC)
    g_win_t = lax.dynamic_update_slice(g_win_t, g_kv, (0, 0))

    got = _small_gather_wait(small, g_win_t, "small_grads_wait")
    tot = _sum_slots(got, "sum_small_grads").reshape(1, -1)
    sizes = [D, MLA_Q_LORA, MLA_KV_LORA, GQA_HEAD_DIM, GQA_HEAD_DIM, D, F2, D, 3 * FF, 3 * FF, 2 * D]
    offs = [0]
    for s in sizes:
        offs.append(offs[-1] + s)
    t_n1g, t_qg, t_kvg, t_gq, t_kg, t_n2g, t_cb, t_fg, t_cwa, t_cwb, t_ctx = [tot[:, offs[k] : offs[k + 1]] for k in range(len(sizes))]
    g_cw_full = jnp.concatenate([t_cwa.reshape(3, FF), t_cwb.reshape(3, FF)], axis=1)
    g_cw = lax.dynamic_slice(g_cw_full, (0, j * NW), (3, NW))
    d_lat_all = got.reshape(8, -1)[:, offs[-1] : offs[-1] + 6 * D]
    g16 = jnp.concatenate([d_lat_all, jnp.pad(t_ctx, ((0, 0), (0, 4 * D))), jnp.zeros((7, 6 * D), F32)], axis=0)
    g_b_ada = _sum_slots(g16.reshape(16, 1, 6 * D), "sum_b_ada")
    g16_cols = lax.dynamic_slice(g16, (0, j * NA), (16, NA))
    ds_part = _mm(g16_cols, w_ada[0], "NT", F32, "ada_dx")
    ada_dx = _small_gather_start(ds_part[8:16], got, "ada_dx_start")
    sw_in = _swap_start([g_win_t], ada_dx[4], "in")

    h_ffn = _scatter_sums(rs_ffn, sw_in[2][0], "ffn")
    got = _small_gather_wait(ada_dx, h_ffn[0], "ada_dx_wait")
    ds_ctx = _sum_slots(jnp.stack([got[2 * s] for s in range(4)]), "sum_ada_dx")[0:1]
    g_c_ctx = _silu_grad_mul(ds_ctx, cc)
    j_ffn = _join_start(h_ffn, grad_x, "ffn")
    h_mix = _scatter_sums(rs_mix, j_ffn[2][0], "mix")
    j_mix = _join_start(h_mix, j_ffn[2][0], "mix")
    rs_in = _scatter_start_after_swap(sw_in, j_mix[2][0], "in")
    g_w_ada = _mm(s16, g16_cols, "TN", F32, "ada_dw", act="silu", after=rs_in[4])
    _, d_ada, m_ada, v_ada = _adamw(w_ada[0], g_w_ada, m_w_ada[0], v_w_ada[0], "adamw_w_ada")
    r_wdown, r_wup = _join_wait(j_ffn, d_ada, "ffn")
    r_wq, r_wkv, r_wbra, r_wbrb, r_wout = _join_wait(j_mix, d_ada, "mix")
    gq_p = _joined(*r_wq).T
    gq = jnp.concatenate([gq_p[:, : 2 * MLA_NOPE].reshape(MLA_Q_LORA, 2, MLA_NOPE), gq_p[:, 2 * MLA_NOPE :].reshape(MLA_Q_LORA, 2, MLA_ROPE)], axis=2)
    grads = {
        "c_ctx": g_c_ctx.reshape(D), "w_ada": g_w_ada[None], "b_ada": g_b_ada, "norm1_g": t_n1g,
        "mla_q_norm_g": t_qg, "w_q_up": gq.reshape(1, MLA_Q_LORA, -1), "mla_kv_norm_g": t_kvg, "w_kv_up": r_wkv,
        "gqa_q_norm_g": t_gq, "gqa_k_norm_g": t_kg, "w_br_a": r_wbra, "w_br_b": r_wbrb, "w_out": r_wout,
        "norm2_g": t_n2g, "w_up": r_wup, "conv_w": g_cw[None], "conv_b": t_cb, "w_down": r_wdown,
        "final_norm_g": t_fg.reshape(D),
    }
    arrives_transposed = ("w_kv_up", "w_br_a", "w_br_b", "w_up")
    arrives_halved = arrives_transposed + ("w_out", "w_down")
    weights = dict(c_ctx=c_ctx, w_ada=w_ada, b_ada=b_ada, norm1_g=norm1_g, w_in=w_in, mla_q_norm_g=mla_q_norm_g, w_q_up=w_q_up,
                   mla_kv_norm_g=mla_kv_norm_g, w_kv_up=w_kv_up, gqa_q_norm_g=gqa_q_norm_g, gqa_k_norm_g=gqa_k_norm_g, w_br_a=w_br_a,
                   w_br_b=w_br_b, w_out=w_out, norm2_g=norm2_g, w_up=w_up, conv_w=conv_w, conv_b=conv_b, w_down=w_down,
                   final_norm_g=final_norm_g)
    m_in = dict(c_ctx=m_c_ctx, w_ada=m_w_ada, b_ada=m_b_ada, norm1_g=m_norm1_g, w_in=m_w_in, mla_q_norm_g=m_mla_q_norm_g,
                w_q_up=m_w_q_up, mla_kv_norm_g=m_mla_kv_norm_g, w_kv_up=m_w_kv_up, gqa_q_norm_g=m_gqa_q_norm_g,
                gqa_k_norm_g=m_gqa_k_norm_g, w_br_a=m_w_br_a, w_br_b=m_w_br_b, w_out=m_w_out, norm2_g=m_norm2_g, w_up=m_w_up,
                conv_w=m_conv_w, conv_b=m_conv_b, w_down=m_w_down, final_norm_g=m_final_norm_g)
    v_in = dict(c_ctx=v_c_ctx, w_ada=v_w_ada, b_ada=v_b_ada, norm1_g=v_norm1_g, w_in=v_w_in, mla_q_norm_g=v_mla_q_norm_g,
                w_q_up=v_w_q_up, mla_kv_norm_g=v_mla_kv_norm_g, w_kv_up=v_w_kv_up, gqa_q_norm_g=v_gqa_q_norm_g,
                gqa_k_norm_g=v_gqa_k_norm_g, w_br_a=v_w_br_a, w_br_b=v_w_br_b, w_out=v_w_out, norm2_g=v_norm2_g, w_up=v_w_up,
                conv_w=v_conv_w, conv_b=v_conv_b, w_down=v_w_down, final_norm_g=v_final_norm_g)
    names = list(weights)
    big = [n for n in names if weights[n].ndim == 3 and weights[n].shape[1] >= 8]
    small = [n for n in names if n not in big]
    delta, new_m, new_v = {}, {}, {}

    def update(n):
        shp = weights[n].shape
        two_d = lambda a: a.reshape(shp[1], shp[2])
        g_t = n in arrives_transposed
        if n in arrives_halved:
            g_in, g_sib = grads[n]
        else:
            g_in, g_sib = two_d(grads[n].astype(F32)), None
        g_, d_, m_, v_ = _adamw(two_d(weights[n]), g_in, two_d(m_in[n]), two_d(v_in[n]), "adamw_" + n, g_transposed=g_t, g_sibling=g_sib)
        grads[n], delta[n], new_m[n], new_v[n] = g_.reshape(shp), d_.reshape(shp), m_.reshape(shp), v_.reshape(shp)

    delta["w_ada"], new_m["w_ada"], new_v["w_ada"] = d_ada[None], m_ada[None], v_ada[None]
    early = [n for n in big if n not in ("w_in", "w_ada")]
    for n in early[:-1]:
        update(n)
    done = sum(delta[n][0, 0:1, 0:1] for n in early[:-1])
    j_in = _join_start(_scatter_sums(rs_in, done, "in"), done, "in")
    last = early[-1]
    grads[last] = (grads[last][0] + j_in[4], grads[last][1])
    update(last)
    ((g_mine, g_sib),) = _join_wait(j_in, delta[last], "in")
    g_, d_, m_, v_ = _adamw(w_in[0].T, g_mine, m_w_in[0].T, v_w_in[0].T, "adamw_w_in", g_sibling=g_sib)
    grads["w_in"], delta["w_in"], new_m["w_in"], new_v["w_in"] = g_.T[None], d_.T[None], m_.T[None], v_.T[None]
    grads = {n: grads[n].reshape(weights[n].shape).astype(F32) for n in names}

    slab = lambda tree: [tree[n].reshape(-1, LANES) for n in small]
    d_, m_, v_ = _adamw_many(slab(weights), slab(grads), slab(m_in), slab(v_in), "adamw_small")
    for k, n in enumerate(small):
        shp = weights[n].shape
        delta[n], new_m[n], new_v[n] = d_[k].reshape(shp), m_[k].reshape(shp), v_[k].reshape(shp)

    return (loss, grad_x[None], *[grads[n] for n in names], *[delta[n] for n in names], *[new_m[n] for n in names],
            *[new_v[n] for n in names])
```

```python
import math

import jax
import jax.numpy as jnp
from jax import lax
from jax.experimental import pallas as pl
from jax.experimental.pallas import tpu as pltpu

F32 = jnp.float32
BF16 = jnp.bfloat16
MESH = pl.DeviceIdType.MESH

NORM_EPS = 1e-6
ROPE_THETA = 10000.0
GRID_W = 64
MLA_HEADS = 8
MLA_Q_LORA = 768
MLA_KV_LORA = 512
MLA_NOPE = 128
MLA_ROPE = 64
MLA_V = 128
GQA_HEADS = 8
GQA_KV_HEADS = 2
GQA_HEAD_DIM = 128
GQA_GROUP = GQA_HEADS // GQA_KV_HEADS
LANES = 128
KVP = MLA_KV_LORA + 2 * GQA_KV_HEADS * GQA_HEAD_DIM + LANES
QC = MLA_Q_LORA + GQA_HEADS * GQA_HEAD_DIM

ADAM_LR = 0.001
ADAM_B1 = 0.9
ADAM_B2 = 0.999
ADAM_EPS = 1e-08
ADAM_WD = 0.01
ADAM_STEP = 10

VMEM_LIMIT = 56 * 1024 * 1024


def _pick(dim, target, mult=LANES):
    t = (min(target, dim) // mult) * mult
    while t >= mult:
        if dim % t == 0:
            return t
        t -= mult
    return dim


def _params(sem):
    return pltpu.CompilerParams(dimension_semantics=sem, vmem_limit_bytes=VMEM_LIMIT)


_DIMS = {"NN": (((1,), (0,)), ((), ())), "NT": (((1,), (1,)), ((), ())), "TN": (((0,), (0,)), ((), ()))}


MM_VMEM_BUDGET = 36 * 1024 * 1024


def _mm_tiles(M, N, K, sa, sb, so, tm, tn, tk):
    tm, tn, tk = _pick(M, tm), _pick(N, tn), _pick(K, tk)

    def need(t):
        return 2 * (tm * t * sa + t * tn * sb) + 2 * tm * tn * so + (tm * tn * 4 if t < K else 0)

    while need(tk) > MM_VMEM_BUDGET and tk > LANES:
        smaller = _pick(K, tk - LANES)
        if smaller >= tk:
            break
        tk = smaller
    return tm, tn, tk


def _window(block, index, offsets):
    if not any(offsets):
        return pl.BlockSpec(block, index)
    for t, o in zip(block, offsets):
        assert o % 16 == 0 and t % 16 == 0, (block, offsets)

    def at(i, j, k):
        return tuple(pl.multiple_of(o + p * t, math.gcd(o, t)) for p, t, o in zip(index(i, j, k), block, offsets))

    return pl.BlockSpec(tuple(pl.Element(t) for t in block), at)


def _mm(a, b, mode, out_dtype, name, m=None, n=None, k=None, b_off=0, add=None, out_rows=None, out_base=None, out_off=0,
        tm=1024, tn=1024, tk=2304, act=None, bias=None, after=None):
    if mode == "NN":
        M, K, N = m or a.shape[0], k or a.shape[1], b.shape[1]
    elif mode == "NT":
        M, K, N = m or a.shape[0], a.shape[1], n or b.shape[0]
    else:
        M, K, N = a.shape[1], k or a.shape[0], b.shape[1]
    tm, tn, tk = _mm_tiles(M, N, K, a.dtype.itemsize, b.dtype.itemsize, jnp.dtype(out_dtype).itemsize, tm, tn, tk)
    nk = K // tk
    dims = _DIMS[mode]
    n_in = 2 + (bias is not None) + (add is not None) + (out_base is not None) + (after is not None)

    def body(*refs):
        a_ref, b_ref = refs[:2]
        bias_ref = refs[2] if bias is not None else None
        add_ref = refs[2 + (bias is not None)] if add is not None else None
        o_ref = refs[n_in]
        av = a_ref[...]
        if act == "silu":
            av = av * jax.nn.sigmoid(av)
        part = lax.dot_general(av.astype(BF16), b_ref[...].astype(BF16), dims, preferred_element_type=F32)

        def finish(r):
            if bias is not None:
                r = r + bias_ref[...]
            if add is not None:
                r = r + add_ref[...]
            o_ref[...] = r.astype(out_dtype)

        if nk == 1:
            finish(part)
            return
        acc = refs[-1]
        k = pl.program_id(2)

        @pl.when(k == 0)
        def _():
            acc[...] = part

        @pl.when(jnp.logical_and(k > 0, k < nk - 1))
        def _():
            acc[...] += part

        @pl.when(k == nk - 1)
        def _():
            finish(acc[...] + part)

    a_spec = pl.BlockSpec((tk, tm), lambda i, j, k: (k, i)) if mode == "TN" else pl.BlockSpec((tm, tk), lambda i, j, k: (i, k))
    if mode == "NT":
        b_spec = _window((tn, tk), lambda i, j, k: (j, k), (b_off, 0))
    else:
        b_spec = _window((tk, tn), lambda i, j, k: (k, j), (b_off, 0))
    in_specs, args = [a_spec, b_spec], [a, b]
    if bias is not None:
        in_specs.append(pl.BlockSpec((1, tn), lambda i, j, k: (0, j)))
        args.append(bias)
    if add is not None:
        in_specs.append(pl.BlockSpec((tm, tn), lambda i, j, k: (i, j)))
        args.append(add)
    aliases = {}
    if after is not None:
        in_specs.append(pl.BlockSpec(after.shape, lambda i, j, k: (0, 0)))
        args.append(after)
    if out_base is not None:
        aliases = {len(args): 0}
        in_specs.append(ANY)
        args.append(out_base)
        out_rows = out_base.shape[0]
    return pl.pallas_call(
        body,
        name=name,
        grid=(M // tm, N // tn, nk),
        in_specs=in_specs,
        out_specs=_window((tm, tn), lambda i, j, k: (i, j), (out_off, 0)),
        out_shape=jax.ShapeDtypeStruct((out_rows or M, N), out_dtype),
        input_output_aliases=aliases,
        scratch_shapes=[pltpu.VMEM((tm, tn), F32)] if nk > 1 else [],
        compiler_params=_params(("parallel", "parallel", "arbitrary")),
    )(*args)


def _rms(x):
    r = lax.rsqrt(jnp.mean(x * x, axis=-1, keepdims=True) + NORM_EPS)
    return x * r, r


def _rms_bwd(xh, r, dxh):
    return r * (dxh - xh * jnp.mean(dxh * xh, axis=-1, keepdims=True))


def _swap(x, q):
    lane = lax.broadcasted_iota(jnp.int32, x.shape, 1)
    even = ((lane // q) % 2) == 0
    return jnp.where(even, pltpu.roll(x, LANES - q, 1), pltpu.roll(x, q, 1))


def _rope(x, cos, ss, q):
    return x * cos + _swap(x, q) * ss


def _rope_t(d, cos, ss, q):
    return d * cos + _swap(d * ss, q)


def _csum(x):
    return jnp.sum(x, axis=0, keepdims=True)


def _rows(tr, w, off=0):
    return pl.BlockSpec((tr, w), lambda i: (i + off, 0))


def _bcast(w):
    return pl.BlockSpec((1, w), lambda i: (0, 0))


def _acc_init(i, refs):
    @pl.when(i == 0)
    def _():
        for r in refs:
            r[...] = jnp.zeros_like(r)


def _rope_tables(n_ctx, n_lat, rot_dim):
    rows = n_lat // GRID_W
    row = jnp.repeat(jnp.arange(rows, dtype=F32), GRID_W)
    col = jnp.tile(jnp.arange(GRID_W, dtype=F32), rows)
    half = rot_dim // 2
    inv_freq = ROPE_THETA ** (-jnp.arange(0, half, 2, dtype=F32) / half)
    ar, ac = row[:, None] * inv_freq, col[:, None] * inv_freq
    cos = jnp.concatenate([jnp.cos(ar), jnp.cos(ar), jnp.cos(ac), jnp.cos(ac)], axis=-1)
    ss = jnp.concatenate([-jnp.sin(ar), jnp.sin(ar), -jnp.sin(ac), jnp.sin(ac)], axis=-1)
    cos = jnp.tile(cos, (1, LANES // rot_dim))
    ss = jnp.tile(ss, (1, LANES // rot_dim))
    cos = jnp.concatenate([cos, jnp.ones((n_ctx, LANES), F32)], axis=0)
    ss = jnp.concatenate([ss, jnp.zeros((n_ctx, LANES), F32)], axis=0)
    return cos, ss


def _norm_mod_fwd(x2d, g, sh, sc, name, tr, out_rows=None, base=None, out_off=0):
    n, d = x2d.shape

    def body(x_ref, g_ref, sh_ref, sc_ref, *rest):
        xh, _ = _rms(x_ref[...])
        rest[-1][...] = ((xh * g_ref[...]) * (1.0 + sc_ref[...]) + sh_ref[...]).astype(BF16)

    args, in_specs, aliases = [x2d, g, sh, sc], [_rows(tr, d), _bcast(d), _bcast(d), _bcast(d)], {}
    if base is not None:
        args.append(base)
        in_specs.append(ANY)
        aliases = {4: 0}
        out_rows = base.shape[0]
    return pl.pallas_call(
        body,
        name=name,
        grid=(n // tr,),
        in_specs=in_specs,
        out_specs=_rows(tr, d, out_off // tr),
        out_shape=jax.ShapeDtypeStruct((out_rows or n, d), BF16),
        input_output_aliases=aliases,
        compiler_params=_params(("parallel",)),
    )(*args)


def _norm_mod_bwd(dz, dz_off, x2d, g, sc, dres, name, tr):
    n, d = x2d.shape
    want_dx = dres is not None

    def body(*refs):
        if want_dx:
            dz_ref, x_ref, g_ref, sc_ref, dres_ref, dx_ref, dg_ref, dsh_ref, dsc_ref = refs
        else:
            dz_ref, x_ref, g_ref, sc_ref, dg_ref, dsh_ref, dsc_ref = refs
        _acc_init(pl.program_id(0), [dg_ref, dsh_ref, dsc_ref])
        xh, r = _rms(x_ref[...])
        dzv = dz_ref[...]
        gv = g_ref[...]
        dsc_ref[...] += _csum(dzv * (xh * gv))
        dsh_ref[...] += _csum(dzv)
        dh = dzv * (1.0 + sc_ref[...])
        dg_ref[...] += _csum(dh * xh)
        if want_dx:
            dx_ref[...] = _rms_bwd(xh, r, dh * gv) + dres_ref[...]

    in_specs = [_rows(tr, d, dz_off), _rows(tr, d), _bcast(d), _bcast(d)]
    args = [dz, x2d, g, sc]
    out_specs = [_bcast(d)] * 3
    out_shape = [jax.ShapeDtypeStruct((1, d), F32)] * 3
    if want_dx:
        in_specs.append(_rows(tr, d))
        args.append(dres)
        out_specs = [_rows(tr, d)] + out_specs
        out_shape = [jax.ShapeDtypeStruct((n, d), F32)] + out_shape
    res = pl.pallas_call(
        body,
        name=name,
        grid=(n // tr,),
        in_specs=in_specs,
        out_specs=out_specs,
        out_shape=out_shape,
        compiler_params=_params(("arbitrary",)),
    )(*args)
    return res if want_dx else (None, *res)


_QA, _QB = MLA_ROPE // 4, GQA_HEAD_DIM // 4


def _kprep_fwd(pkv, kvg, kg, cos_a, ss_a, cos_b, ss_b, tr):
    n = pkv.shape[0]
    nb = GQA_KV_HEADS * GQA_HEAD_DIM

    def body(p_ref, kvg_ref, kg_ref, ca, sa, cb, sb, ckv_ref, kb_ref, vb_ref, kpe_ref):
        p = p_ref[...]
        xh, _ = _rms(p[:, :MLA_KV_LORA])
        ckv_ref[...] = (xh * kvg_ref[...]).astype(BF16)
        for e in range(GQA_KV_HEADS):
            lo = MLA_KV_LORA + e * GQA_HEAD_DIM
            kh, _ = _rms(p[:, lo : lo + GQA_HEAD_DIM])
            kb_ref[:, e * GQA_HEAD_DIM : (e + 1) * GQA_HEAD_DIM] = _rope(kh * kg_ref[...], cb[...], sb[...], _QB).astype(BF16)
        vb_ref[...] = p[:, MLA_KV_LORA + nb : MLA_KV_LORA + 2 * nb].astype(BF16)
        kr = _rope(p[:, MLA_KV_LORA + 2 * nb :], ca[...], sa[...], _QA)
        kpe_ref[:, :LANES] = kr.astype(BF16)
        kpe_ref[:, LANES:] = pltpu.roll(kr, MLA_ROPE, 1).astype(BF16)

    return pl.pallas_call(
        body,
        name="kprep_fwd",
        grid=(n // tr,),
        in_specs=[_rows(tr, KVP), _bcast(MLA_KV_LORA), _bcast(GQA_HEAD_DIM)] + [_rows(tr, LANES)] * 4,
        out_specs=[_rows(tr, MLA_KV_LORA), _rows(tr, nb), _rows(tr, nb), _rows(tr, 2 * LANES)],
        out_shape=[jax.ShapeDtypeStruct((n, w), BF16) for w in (MLA_KV_LORA, nb, nb, 2 * LANES)],
        compiler_params=_params(("parallel",)),
    )(pkv, kvg, kg, cos_a, ss_a, cos_b, ss_b)


def _kprep_bwd(pkv, dckv, dkb, dvb, dkpe, kvg, kg, cos_b, ss_b, tr):
    n = pkv.shape[0]
    nb = GQA_KV_HEADS * GQA_HEAD_DIM

    def body(p_ref, dckv_ref, dkb_ref, dvb_ref, dkpe_ref, kvg_ref, kg_ref, cb, sb, dp_ref, dkvg_ref, dkg_ref):
        _acc_init(pl.program_id(0), [dkvg_ref, dkg_ref])
        p = p_ref[...]
        xh, r = _rms(p[:, :MLA_KV_LORA])
        dn = dckv_ref[...]
        dkvg_ref[...] += _csum(dn * xh)
        dp_ref[:, :MLA_KV_LORA] = _rms_bwd(xh, r, dn * kvg_ref[...]).astype(BF16)
        for e in range(GQA_KV_HEADS):
            lo = MLA_KV_LORA + e * GQA_HEAD_DIM
            kh, rk = _rms(p[:, lo : lo + GQA_HEAD_DIM])
            dk = _rope_t(dkb_ref[:, e * GQA_HEAD_DIM : (e + 1) * GQA_HEAD_DIM], cb[...], sb[...], _QB)
            dkg_ref[...] += _csum(dk * kh)
            dp_ref[:, lo : lo + GQA_HEAD_DIM] = _rms_bwd(kh, rk, dk * kg_ref[...]).astype(BF16)
        dp_ref[:, MLA_KV_LORA + nb : MLA_KV_LORA + 2 * nb] = dvb_ref[...].astype(BF16)
        dp_ref[:, MLA_KV_LORA + 2 * nb :] = dkpe_ref[...].astype(BF16)

    return pl.pallas_call(
        body,
        name="kprep_bwd",
        grid=(n // tr,),
        in_specs=[_rows(tr, KVP), _rows(tr, MLA_KV_LORA), _rows(tr, nb), _rows(tr, nb), _rows(tr, LANES),
                  _bcast(MLA_KV_LORA), _bcast(GQA_HEAD_DIM), _rows(tr, LANES), _rows(tr, LANES)],
        out_specs=[_rows(tr, KVP), _bcast(MLA_KV_LORA), _bcast(GQA_HEAD_DIM)],
        out_shape=[jax.ShapeDtypeStruct((n, KVP), BF16), jax.ShapeDtypeStruct((1, MLA_KV_LORA), F32),
                   jax.ShapeDtypeStruct((1, GQA_HEAD_DIM), F32)],
        compiler_params=_params(("arbitrary",)),
    )(pkv, dckv, dkb, dvb, dkpe, kvg, kg, cos_b, ss_b)


def _kgrad_split(dka, dva, cos_a, ss_a, tr):
    n = dka.shape[0]
    wk = MLA_HEADS * 2 * LANES

    def body(dk_ref, dv_ref, ca, sa, dkv_ref, dkpe_ref):
        even = jnp.zeros((tr, LANES), F32)
        odd = jnp.zeros((tr, LANES), F32)
        for h in range(MLA_HEADS):
            dkv_ref[:, 2 * h * LANES : (2 * h + 1) * LANES] = dk_ref[:, 2 * h * LANES : (2 * h + 1) * LANES].astype(BF16)
            dkv_ref[:, (2 * h + 1) * LANES : (2 * h + 2) * LANES] = dv_ref[:, h * MLA_V : (h + 1) * MLA_V].astype(BF16)
            part = dk_ref[:, (2 * h + 1) * LANES : (2 * h + 2) * LANES]
            if h % 2 == 0:
                even = even + part
            else:
                odd = odd + part
        lane = lax.broadcasted_iota(jnp.int32, (tr, LANES), 1)
        low = lane < MLA_ROPE
        both = jnp.where(low, even, odd)
        tot = jnp.where(low, both + pltpu.roll(both, MLA_ROPE, 1), 0.0)
        dkpe_ref[...] = _rope_t(tot, ca[...], sa[...], _QA)

    return pl.pallas_call(
        body,
        name="kgrad_split",
        grid=(n // tr,),
        in_specs=[_rows(tr, wk), _rows(tr, MLA_HEADS * MLA_V), _rows(tr, LANES), _rows(tr, LANES)],
        out_specs=[_rows(tr, wk), _rows(tr, LANES)],
        out_shape=[jax.ShapeDtypeStruct((n, wk), BF16), jax.ShapeDtypeStruct((n, LANES), F32)],
        compiler_params=_params(("parallel",)),
    )(dka, dva, cos_a, ss_a)


def _qprep_fwd(pq, qg, gq, cos_b, ss_b, tr):
    n = pq.shape[0]
    nq = GQA_HEADS * GQA_HEAD_DIM

    def body(p_ref, qg_ref, gq_ref, cb, sb, cq_ref, qb_ref):
        xh, _ = _rms(p_ref[:, :MLA_Q_LORA])
        cq_ref[...] = (xh * qg_ref[...]).astype(BF16)
        for h in range(GQA_HEADS):
            lo = MLA_Q_LORA + h * GQA_HEAD_DIM
            qh, _ = _rms(p_ref[:, lo : lo + GQA_HEAD_DIM])
            qb_ref[:, h * GQA_HEAD_DIM : (h + 1) * GQA_HEAD_DIM] = _rope(qh * gq_ref[...], cb[...], sb[...], _QB).astype(BF16)

    return pl.pallas_call(
        body,
        name="qprep_fwd",
        grid=(n // tr,),
        in_specs=[_rows(tr, QC), _bcast(MLA_Q_LORA), _bcast(GQA_HEAD_DIM), _rows(tr, LANES), _rows(tr, LANES)],
        out_specs=[_rows(tr, MLA_Q_LORA), _rows(tr, nq)],
        out_shape=[jax.ShapeDtypeStruct((n, MLA_Q_LORA), BF16), jax.ShapeDtypeStruct((n, nq), BF16)],
        compiler_params=_params(("parallel",)),
    )(pq, qg, gq, cos_b, ss_b)


def _qprep_bwd(pq, dcq, dqb, qg, gq, cos_b, ss_b, tr):
    n = pq.shape[0]
    nq = GQA_HEADS * GQA_HEAD_DIM

    def body(p_ref, dcq_ref, dqb_ref, qg_ref, gq_ref, cb, sb, dp_ref, dqg_ref, dgq_ref):
        _acc_init(pl.program_id(0), [dqg_ref, dgq_ref])
        xh, r = _rms(p_ref[:, :MLA_Q_LORA])
        dn = dcq_ref[...]
        dqg_ref[...] += _csum(dn * xh)
        dp_ref[:, :MLA_Q_LORA] = _rms_bwd(xh, r, dn * qg_ref[...]).astype(BF16)
        for h in range(GQA_HEADS):
            lo = MLA_Q_LORA + h * GQA_HEAD_DIM
            qh, rq = _rms(p_ref[:, lo : lo + GQA_HEAD_DIM])
            dq = _rope_t(dqb_ref[:, h * GQA_HEAD_DIM : (h + 1) * GQA_HEAD_DIM], cb[...], sb[...], _QB)
            dgq_ref[...] += _csum(dq * qh)
            dp_ref[:, lo : lo + GQA_HEAD_DIM] = _rms_bwd(qh, rq, dq * gq_ref[...]).astype(BF16)

    return pl.pallas_call(
        body,
        name="qprep_bwd",
        grid=(n // tr,),
        in_specs=[_rows(tr, QC), _rows(tr, MLA_Q_LORA), _rows(tr, nq), _bcast(MLA_Q_LORA), _bcast(GQA_HEAD_DIM),
                  _rows(tr, LANES), _rows(tr, LANES)],
        out_specs=[_rows(tr, QC), _bcast(MLA_Q_LORA), _bcast(GQA_HEAD_DIM)],
        out_shape=[jax.ShapeDtypeStruct((n, QC), BF16), jax.ShapeDtypeStruct((1, MLA_Q_LORA), F32),
                   jax.ShapeDtypeStruct((1, GQA_HEAD_DIM), F32)],
        compiler_params=_params(("arbitrary",)),
    )(pq, dcq, dqb, qg, gq, cos_b, ss_b)


_QA_COLS = MLA_HEADS * (MLA_NOPE + MLA_ROPE)


def _qrope_fwd(qa, cos_a, ss_a, tr):
    n = qa.shape[0]

    def body(q_ref, ca, sa, o_ref):
        for j in range(MLA_HEADS // 2):
            lo = 3 * j * LANES
            o_ref[:, lo : lo + 2 * LANES] = q_ref[:, lo : lo + 2 * LANES].astype(BF16)
            o_ref[:, lo + 2 * LANES : lo + 3 * LANES] = _rope(q_ref[:, lo + 2 * LANES : lo + 3 * LANES], ca[...], sa[...], _QA).astype(BF16)

    return pl.pallas_call(
        body,
        name="qrope_fwd",
        grid=(n // tr,),
        in_specs=[_rows(tr, _QA_COLS), _rows(tr, LANES), _rows(tr, LANES)],
        out_specs=_rows(tr, _QA_COLS),
        out_shape=jax.ShapeDtypeStruct((n, _QA_COLS), BF16),
        compiler_params=_params(("parallel",)),
    )(qa, cos_a, ss_a)


def _qrope_bwd(dq2, cos_a, ss_a, tr):
    n = dq2.shape[0]

    def body(d_ref, ca, sa, o_ref):
        for j in range(MLA_HEADS // 2):
            lo = 3 * j * LANES
            h0, h1 = 2 * j, 2 * j + 1
            o_ref[:, lo : lo + LANES] = d_ref[:, 2 * h0 * LANES : (2 * h0 + 1) * LANES].astype(BF16)
            o_ref[:, lo + LANES : lo + 2 * LANES] = d_ref[:, 2 * h1 * LANES : (2 * h1 + 1) * LANES].astype(BF16)
            pe = d_ref[:, (2 * h0 + 1) * LANES : (2 * h0 + 2) * LANES] + d_ref[:, (2 * h1 + 1) * LANES : (2 * h1 + 2) * LANES]
            o_ref[:, lo + 2 * LANES : lo + 3 * LANES] = _rope_t(pe, ca[...], sa[...], _QA).astype(BF16)

    return pl.pallas_call(
        body,
        name="qrope_bwd",
        grid=(n // tr,),
        in_specs=[_rows(tr, MLA_HEADS * 2 * LANES), _rows(tr, LANES), _rows(tr, LANES)],
        out_specs=_rows(tr, _QA_COLS),
        out_shape=jax.ShapeDtypeStruct((n, _QA_COLS), BF16),
        compiler_params=_params(("parallel",)),
    )(dq2, cos_a, ss_a)


def _cat(refs):
    vals = [r[...] for r in refs]
    return vals[0] if len(vals) == 1 else jnp.concatenate(vals, axis=-1)


LOG2E = 1.4426950408889634


def _attn_fwd(qparts, kparts, vpart, n_heads, group, dv, scale, name, tq, after=None):
    T, Tk = qparts[0][0].shape[0], kparts[0][0].shape[0]
    nq_, nk_ = len(qparts), len(kparts)
    sub = min(tq, 256)
    c2 = scale * LOG2E

    def body(*refs):
        q_refs, k_refs = refs[:nq_], refs[nq_ : nq_ + nk_]
        v_ref = refs[nq_ + nk_]
        o_ref, lse_ref = refs[-2:]
        k = _cat(k_refs)
        v = v_ref[...]
        for r0 in range(0, tq, sub):
            q = _cat([r.at[r0 : r0 + sub, :] for r in q_refs])
            s = lax.dot_general(q, k, _DIMS["NT"], preferred_element_type=F32)
            m = jnp.max(s, axis=-1, keepdims=True)
            p = jnp.exp2((s - m) * c2)
            l = jnp.sum(p, axis=-1, keepdims=True)
            acc = jnp.dot(p.astype(BF16), v, preferred_element_type=F32)
            o_ref[r0 : r0 + sub, :] = (acc * (1.0 / l)).astype(BF16)
            lse_ref[r0 : r0 + sub, :] = m * scale + jnp.log(l)

    in_specs = [pl.BlockSpec((tq, LANES), lambda h, i, f=f: (i, f(h))) for _, f in qparts]
    in_specs += [pl.BlockSpec((Tk, LANES), lambda h, i, f=f: (0, f(h // group))) for _, f in kparts]
    fv = vpart[1]
    in_specs.append(pl.BlockSpec((Tk, dv), lambda h, i: (0, fv(h // group))))
    args = [*[a for a, _ in qparts], *[a for a, _ in kparts], vpart[0]]
    if after is not None:
        in_specs.append(pl.BlockSpec(after.shape, lambda h, i: (0, 0)))
        args.append(after)
    return pl.pallas_call(
        body,
        name=name,
        grid=(n_heads, T // tq),
        in_specs=in_specs,
        out_specs=[pl.BlockSpec((tq, dv), lambda h, i: (i, h)), pl.BlockSpec((None, tq, 1), lambda h, i: (h, i, 0))],
        out_shape=[jax.ShapeDtypeStruct((T, n_heads * dv), BF16), jax.ShapeDtypeStruct((n_heads, T, 1), F32)],
        compiler_params=_params(("parallel", "parallel")),
    )(*args)


def _attn_bwd(qparts, kparts, vpart, o, do, lse, n_heads, group, dv, scale, name, tq):
    T, Tk = qparts[0][0].shape[0], kparts[0][0].shape[0]
    nq_, nk_ = len(qparts), len(kparts)
    dk_ = LANES * nq_
    n_kv = n_heads // group
    nblk = T // tq
    c2 = scale * LOG2E

    def head(hk, i):
        return hk * group + i // nblk

    sub = min(tq, 256)

    def body(*refs):
        q_refs = refs[:nq_]
        k = _cat(refs[nq_ : nq_ + nk_])
        v_ref, o_ref, do_ref, lse_ref, dq_ref, dk_ref, dv_ref = refs[nq_ + nk_ :]
        i = pl.program_id(1)
        _acc_init(i, [dk_ref, dv_ref])
        v = v_ref[...]
        dk_acc, dv_acc = None, None
        for r0 in range(0, tq, sub):
            rows = slice(r0, r0 + sub)
            q = _cat([r.at[rows, :] for r in q_refs])
            s = lax.dot_general(q, k, _DIMS["NT"], preferred_element_type=F32)
            p = jnp.exp2(s * c2 - lse_ref[rows, :] * LOG2E)
            dov = do_ref[rows, :]
            dp = lax.dot_general(dov, v, _DIMS["NT"], preferred_element_type=F32)
            delta = jnp.sum(dov.astype(F32) * o_ref[rows, :].astype(F32), axis=-1, keepdims=True)
            ds = (p * (dp - delta)).astype(BF16)
            dq_ref[rows, :] = jnp.dot(ds, k, preferred_element_type=F32) * scale
            dk_part = lax.dot_general(ds, q, _DIMS["TN"], preferred_element_type=F32)
            dv_part = lax.dot_general(p.astype(BF16), dov, _DIMS["TN"], preferred_element_type=F32)
            dk_acc = dk_part if dk_acc is None else dk_acc + dk_part
            dv_acc = dv_part if dv_acc is None else dv_acc + dv_part
        dk_ref[...] += dk_acc
        dv_ref[...] += dv_acc

        @pl.when(i == group * nblk - 1)
        def _():
            dk_ref[...] *= scale

    in_specs = [pl.BlockSpec((tq, LANES), lambda hk, i, f=f: (i % nblk, f(head(hk, i)))) for _, f in qparts]
    in_specs += [pl.BlockSpec((Tk, LANES), lambda hk, i, f=f: (0, f(hk))) for _, f in kparts]
    fv = vpart[1]
    in_specs.append(pl.BlockSpec((Tk, dv), lambda hk, i: (0, fv(hk))))
    in_specs += [pl.BlockSpec((tq, dv), lambda hk, i: (i % nblk, head(hk, i)))] * 2
    in_specs.append(pl.BlockSpec((None, tq, 1), lambda hk, i: (head(hk, i), i % nblk, 0)))
    return pl.pallas_call(
        body,
        name=name,
        grid=(n_kv, group * nblk),
        in_specs=in_specs,
        out_specs=[pl.BlockSpec((tq, dk_), lambda hk, i: (i % nblk, head(hk, i))),
                   pl.BlockSpec((Tk, dk_), lambda hk, i: (0, hk)),
                   pl.BlockSpec((Tk, dv), lambda hk, i: (0, hk))],
        out_shape=[jax.ShapeDtypeStruct((T, n_heads * dk_), F32), jax.ShapeDtypeStruct((Tk, n_kv * dk_), F32),
                   jax.ShapeDtypeStruct((Tk, n_kv * dv), F32)],
        compiler_params=_params(("parallel", "arbitrary")),
    )(*[a for a, _ in qparts], *[a for a, _ in kparts], vpart[0], o, do, lse)


def _gates_fwd(pg, ya, yb, tr):
    n, d = ya.shape

    def body(pg_ref, ya_ref, yb_ref, o_ref):
        ga = jax.nn.sigmoid(pg_ref[:, :d].astype(F32))
        gb = jax.nn.sigmoid(pg_ref[:, d:].astype(F32))
        o_ref[...] = (ga * ya_ref[...].astype(F32) + gb * yb_ref[...].astype(F32)).astype(BF16)

    return pl.pallas_call(
        body,
        name="gates_fwd",
        grid=(n // tr,),
        in_specs=[_rows(tr, 2 * d), _rows(tr, d), _rows(tr, d)],
        out_specs=_rows(tr, d),
        out_shape=jax.ShapeDtypeStruct((n, d), BF16),
        compiler_params=_params(("parallel",)),
    )(pg, ya, yb)


def _gates_bwd(dm, pg, ya, yb, tr):
    n, d = ya.shape

    def body(dm_ref, pg_ref, ya_ref, yb_ref, dya_ref, dyb_ref, dpg_ref):
        dmv = dm_ref[...].astype(F32)
        ga = jax.nn.sigmoid(pg_ref[:, :d].astype(F32))
        gb = jax.nn.sigmoid(pg_ref[:, d:].astype(F32))
        dya_ref[...] = (dmv * ga).astype(BF16)
        dyb_ref[...] = (dmv * gb).astype(BF16)
        dpg_ref[:, :d] = (dmv * ya_ref[...].astype(F32) * ga * (1.0 - ga)).astype(BF16)
        dpg_ref[:, d:] = (dmv * yb_ref[...].astype(F32) * gb * (1.0 - gb)).astype(BF16)

    return pl.pallas_call(
        body,
        name="gates_bwd",
        grid=(n // tr,),
        in_specs=[_rows(tr, d), _rows(tr, 2 * d), _rows(tr, d), _rows(tr, d)],
        out_specs=[_rows(tr, d), _rows(tr, d), _rows(tr, 2 * d)],
        out_shape=[jax.ShapeDtypeStruct((n, d), BF16), jax.ShapeDtypeStruct((n, d), BF16), jax.ShapeDtypeStruct((n, 2 * d), BF16)],
        compiler_params=_params(("parallel",)),
    )(dm, pg, ya, yb)


def _resid_norm2_fwd(x2d, att, g1, n2g, sh2, sc2, tr):
    n, d = x2d.shape

    def body(x_ref, a_ref, g1_ref, g_ref, sh_ref, sc_ref, x1_ref, z_ref):
        x1 = x_ref[...] + g1_ref[...] * a_ref[...]
        x1_ref[...] = x1
        xh, _ = _rms(x1)
        z_ref[...] = ((xh * g_ref[...]) * (1.0 + sc_ref[...]) + sh_ref[...]).astype(BF16)

    return pl.pallas_call(
        body,
        name="resid_norm2_fwd",
        grid=(n // tr,),
        in_specs=[_rows(tr, d), _rows(tr, d)] + [_bcast(d)] * 4,
        out_specs=[_rows(tr, d), _rows(tr, d)],
        out_shape=[jax.ShapeDtypeStruct((n, d), F32), jax.ShapeDtypeStruct((n, d), BF16)],
        compiler_params=_params(("parallel",)),
    )(x2d, att, g1, n2g, sh2, sc2)


def _resid_norm2_bwd(dz2, x1, dx2, att, n2g, sc2, g1, tr):
    n, d = x1.shape

    def body(dz_ref, x1_ref, dx2_ref, a_ref, g_ref, sc_ref, g1_ref, dx1_ref, da_ref, dg_ref, dsh_ref, dsc_ref, dg1_ref):
        _acc_init(pl.program_id(0), [dg_ref, dsh_ref, dsc_ref, dg1_ref])
        xh, r = _rms(x1_ref[...])
        dzv = dz_ref[...]
        gv = g_ref[...]
        dsc_ref[...] += _csum(dzv * (xh * gv))
        dsh_ref[...] += _csum(dzv)
        dh = dzv * (1.0 + sc_ref[...])
        dg_ref[...] += _csum(dh * xh)
        dx1 = _rms_bwd(xh, r, dh * gv) + dx2_ref[...]
        dx1_ref[...] = dx1
        dg1_ref[...] += _csum(dx1 * a_ref[...])
        da_ref[...] = (dx1 * g1_ref[...]).astype(BF16)

    return pl.pallas_call(
        body,
        name="resid_norm2_bwd",
        grid=(n // tr,),
        in_specs=[_rows(tr, d)] * 4 + [_bcast(d)] * 3,
        out_specs=[_rows(tr, d), _rows(tr, d)] + [_bcast(d)] * 4,
        out_shape=[jax.ShapeDtypeStruct((n, d), F32), jax.ShapeDtypeStruct((n, d), BF16)] + [jax.ShapeDtypeStruct((1, d), F32)] * 4,
        compiler_params=_params(("arbitrary",)),
    )(dz2, x1, dx2, att, n2g, sc2, g1)


def _edges(shape):
    row = lax.broadcasted_iota(jnp.int32, shape, 0)
    return row == 0, row == shape[0] - 1


def _shifts(u, edges):
    n = u.shape[0]
    return jnp.where(edges[0], 0.0, pltpu.roll(u, 1, 0)), jnp.where(edges[1], 0.0, pltpu.roll(u, n - 1, 0))


def _conv3(u, prev, nxt, w_ref, b_ref):
    return b_ref[...] + w_ref[0:1, :] * prev + w_ref[1:2, :] * u + w_ref[2:3, :] * nxt


def _ffn_up_conv(z, wup_t, cw, cb, tc, after):
    n, d = z.shape
    f = wup_t.shape[0] // 2
    nb = f // tc

    def body(z_ref, wa_ref, wb_ref, cwa, cwb, cba, cbb, after_ref, ua_ref, ub_ref, h_ref):
        w = jnp.concatenate([wa_ref[...], wb_ref[...]], axis=0)
        u = lax.dot_general(z_ref[...], w, _DIMS["NT"], preferred_element_type=F32).astype(BF16)
        ua_ref[...] = u[:, :tc]
        ub_ref[...] = u[:, tc:]
        edges = _edges((n, tc))
        ua = u[:, :tc].astype(F32)
        ub = u[:, tc:].astype(F32)
        a = _conv3(ua, *_shifts(ua, edges), cwa, cba)
        b = _conv3(ub, *_shifts(ub, edges), cwb, cbb)
        h_ref[...] = (a * jax.nn.sigmoid(a) * b).astype(BF16)

    col = lambda rows, off: pl.BlockSpec((rows, tc), lambda i: (0, i + off))
    w_rows = lambda off: pl.BlockSpec((tc, d), lambda i: (i + off, 0))
    return pl.pallas_call(
        body,
        name="ffn_up_conv",
        grid=(nb,),
        in_specs=[pl.BlockSpec((n, d), lambda i: (0, 0)), w_rows(0), w_rows(nb), col(3, 0), col(3, nb), col(1, 0), col(1, nb),
                  pl.BlockSpec(after.shape, lambda i: (0, 0))],
        out_specs=[col(n, 0)] * 3,
        out_shape=[jax.ShapeDtypeStruct((n, f), BF16)] * 3,
        compiler_params=_params(("parallel",)),
    )(z, wup_t, wup_t, cw, cw, cb, cb, after)


def _ffn_down_dx_conv_bwd(df, wdown, u_a, u_b, cw, cb, tc, after):
    n, f = u_a.shape
    d = df.shape[1]
    nb = f // tc

    def part(uv, prev, nxt, duc, edges, w_ref, du_ref, dw_ref, db_ref):
        db_ref[...] = _csum(duc)
        dw_ref[0:1, :] = _csum(duc * prev)
        dw_ref[1:2, :] = _csum(duc * uv)
        dw_ref[2:3, :] = _csum(duc * nxt)
        d_prev, d_next = _shifts(duc, edges)
        du_ref[...] = (w_ref[0:1, :] * d_next + w_ref[1:2, :] * duc + w_ref[2:3, :] * d_prev).astype(BF16)

    def body(df_ref, wd_ref, ua_ref, ub_ref, wa_ref, wb_ref, ba_ref, bb_ref, after_ref,
             dua_ref, dub_ref, dwa_ref, dwb_ref, dba_ref, dbb_ref):
        dhv = lax.dot_general(df_ref[...], wd_ref[...], _DIMS["NT"], preferred_element_type=F32)
        dhv = dhv.astype(BF16).astype(F32)
        edges = _edges((n, tc))
        ua = ua_ref[...].astype(F32)
        ub = ub_ref[...].astype(F32)
        sa = _shifts(ua, edges)
        sb = _shifts(ub, edges)
        a = _conv3(ua, *sa, wa_ref, ba_ref)
        b = _conv3(ub, *sb, wb_ref, bb_ref)
        sg = jax.nn.sigmoid(a)
        da = dhv * b * (sg * (1.0 + a * (1.0 - sg)))
        db = dhv * (a * sg)
        part(ua, *sa, da, edges, wa_ref, dua_ref, dwa_ref, dba_ref)
        part(ub, *sb, db, edges, wb_ref, dub_ref, dwb_ref, dbb_ref)

    col = lambda rows, off: pl.BlockSpec((rows, tc), lambda i: (0, i + off))
    return pl.pallas_call(
        body,
        name="ffn_down_dx_conv_bwd",
        grid=(nb,),
        in_specs=[pl.BlockSpec((n, d), lambda i: (0, 0)), pl.BlockSpec((tc, d), lambda i: (i, 0)), col(n, 0), col(n, 0),
                  col(3, 0), col(3, nb), col(1, 0), col(1, nb), pl.BlockSpec(after.shape, lambda i: (0, 0))],
        out_specs=[col(n, 0), col(n, 0), col(3, 0), col(3, 0), col(1, 0), col(1, 0)],
        out_shape=[jax.ShapeDtypeStruct((n, f), BF16)] * 2 + [jax.ShapeDtypeStruct((3, f), F32)] * 2 + [jax.ShapeDtypeStruct((1, f), F32)] * 2,
        compiler_params=_params(("parallel",)),
    )(df, wdown, u_a, u_b, cw, cw, cb, cb, after)


def _loss_head(x1, f, g2, fg, tgt, tr):
    n, d = x1.shape

    def body(x1_ref, f_ref, g2_ref, fg_ref, t_ref, sq_ref, dx2_ref, dfg_ref, dg2_ref, df_ref):
        _acc_init(pl.program_id(0), [sq_ref, dfg_ref, dg2_ref])
        fv = f_ref[...]
        xh, r = _rms(x1_ref[...] + g2_ref[...] * fv)
        err = xh * fg_ref[...] - t_ref[...]
        sq_ref[...] += _csum(err * err)
        dy = err * (1.0 / d)
        dfg_ref[...] += _csum(dy * xh)
        dx2 = _rms_bwd(xh, r, dy * fg_ref[...])
        dx2_ref[...] = dx2
        dg2_ref[...] += _csum(dx2 * fv)
        df_ref[...] = (dx2 * g2_ref[...]).astype(BF16)

    return pl.pallas_call(
        body,
        name="loss_head",
        grid=(n // tr,),
        in_specs=[_rows(tr, d), _rows(tr, d), _bcast(d), _bcast(d), _rows(tr, d)],
        out_specs=[_bcast(d), _rows(tr, d), _bcast(d), _bcast(d), _rows(tr, d)],
        out_shape=[jax.ShapeDtypeStruct((1, d), F32), jax.ShapeDtypeStruct((n, d), F32), jax.ShapeDtypeStruct((1, d), F32),
                   jax.ShapeDtypeStruct((1, d), F32), jax.ShapeDtypeStruct((n, d), BF16)],
        compiler_params=_params(("arbitrary",)),
    )(x1, f, g2, fg, tgt)


def _sum_slots(g, name):
    s, r, w = g.shape

    def body(g_ref, o_ref):
        acc = g_ref[0]
        for k in range(1, s):
            acc = acc + g_ref[k]
        o_ref[...] = acc

    return pl.pallas_call(body, name=name, out_shape=jax.ShapeDtypeStruct((r, w), F32))(g)


def _silu_grad_mul(ds, cvec):
    def body(d_ref, c_ref, o_ref):
        cv = c_ref[...]
        sg = jax.nn.sigmoid(cv)
        o_ref[...] = d_ref[...] * (sg * (1.0 + cv * (1.0 - sg)))

    return pl.pallas_call(body, name="silu_grad_mul", out_shape=jax.ShapeDtypeStruct(ds.shape, F32))(ds, cvec)


def _adamw_update(wv, gv, mv, vv, d_ref, mo_ref, vo_ref):
    mn = ADAM_B1 * mv + (1.0 - ADAM_B1) * gv
    vn = ADAM_B2 * vv + (1.0 - ADAM_B2) * (gv * gv)
    mo_ref[...] = mn
    vo_ref[...] = vn
    m_hat = mn / (1.0 - ADAM_B1**ADAM_STEP)
    v_hat = vn / (1.0 - ADAM_B2**ADAM_STEP)
    d_ref[...] = -ADAM_LR * (m_hat / (jnp.sqrt(v_hat) + ADAM_EPS) + ADAM_WD * wv)


def _adamw_many(ws, gs, ms, vs, name):
    n = len(ws)

    def body(*refs):
        for k in range(n):
            w_ref, g_ref, m_ref, v_ref = (refs[q * n + k] for q in range(4))
            d_ref, mo_ref, vo_ref = (refs[(4 + q) * n + k] for q in range(3))
            _adamw_update(w_ref[...], g_ref[...], m_ref[...], v_ref[...], d_ref, mo_ref, vo_ref)

    res = pl.pallas_call(body, name=name, out_shape=[jax.ShapeDtypeStruct(w.shape, F32) for w in ws] * 3)(*ws, *gs, *ms, *vs)
    return res[:n], res[n : 2 * n], res[2 * n :]


def _adamw(w, g, m, v, name, g_transposed=False, g_sibling=None):
    r, cdim = w.shape
    halves = g_sibling is not None
    block = 1 << 19
    if g_transposed:
        tc = _pick(cdim // 2 if halves else cdim, 2048)
        tr = _pick(r, max(LANES, block // tc), LANES)
        per_half = (cdim // 2) // tc
    else:
        rows = r // 2 if halves else r
        tc = _pick(cdim, 2048)
        tr = _pick(rows, max(8, block // tc), 8)
        if tr < 64 and rows > 64:
            tr, tc = _pick(rows, 1024, 8), _pick(cdim, 512)
        per_half = (r // 2) // tr
    emit_g = g_transposed or halves

    def body(w_ref, g_ref, *rest):
        m_ref, v_ref = rest[halves : halves + 2]
        outs = rest[halves + 2 :]
        gv = g_ref[...]
        if halves:
            along = pl.program_id(1 if g_transposed else 0)
            gv = jnp.where(along // per_half == lax.axis_index("c"), gv, rest[0][...])
        if g_transposed:
            gv = gv.T
        if emit_g:
            outs[0][...] = gv
        _adamw_update(w_ref[...], gv, m_ref[...], v_ref[...], *outs[-3:])

    spec = pl.BlockSpec((tr, tc), lambda i, j: (i, j))
    if g_transposed:
        g_spec = pl.BlockSpec((tc, tr), lambda i, j: (j % per_half if halves else j, i))
    else:
        g_spec = pl.BlockSpec((tr, tc), lambda i, j: (i % per_half if halves else i, j))
    n_out = 3 + emit_g
    res = pl.pallas_call(
        body,
        name=name,
        grid=(r // tr, cdim // tc),
        in_specs=[spec, g_spec] + [g_spec] * halves + [spec, spec],
        out_specs=[spec] * n_out,
        out_shape=[jax.ShapeDtypeStruct((r, cdim), F32)] * n_out,
        compiler_params=_params(("parallel", "parallel")),
    )(w, g, *([g_sibling] if halves else []), m, v)
    return res if emit_g else [g, *res]


def _place():
    return lax.axis_index("x"), lax.axis_index("y"), lax.axis_index("c")


def _remote(src, dst, send_sem, recv_sem, dev):
    return pltpu.make_async_remote_copy(src_ref=src, dst_ref=dst, send_sem=send_sem, recv_sem=recv_sem, device_id=dev, device_id_type=MESH)


ANY = pl.BlockSpec(memory_space=pl.ANY)


def _all_gather_small(v, name, after=()):
    r, w = v.shape

    def body(v_ref, *rest):
        o_ref, send, recv, lsem = rest[len(after) :]
        x, y, c = _place()
        me = 4 * x + 2 * y + c
        mine = pltpu.make_async_copy(v_ref, o_ref.at[me], lsem)
        mine.start()
        sent = []
        for k in range(1, 8):
            px, py, pc = x ^ (k >> 2), y ^ ((k >> 1) & 1), c ^ (k & 1)
            cp = _remote(v_ref, o_ref.at[me], send.at[k - 1], recv.at[k - 1], (px, py, pc))
            cp.start()
            sent.append(cp)
        for k in range(1, 8):
            px, py, pc = x ^ (k >> 2), y ^ ((k >> 1) & 1), c ^ (k & 1)
            slot = o_ref.at[4 * px + 2 * py + pc]
            _remote(slot, slot, send.at[k - 1], recv.at[k - 1], (x, y, c)).wait_recv()
        for cp in sent:
            cp.wait_send()
        mine.wait()

    return pl.pallas_call(
        body,
        name=name,
        out_shape=jax.ShapeDtypeStruct((8, r, w), F32),
        in_specs=[pl.BlockSpec(memory_space=pltpu.VMEM)] + [ANY] * len(after),
        out_specs=pl.BlockSpec(memory_space=pltpu.VMEM),
        scratch_shapes=[pltpu.SemaphoreType.DMA((7,)), pltpu.SemaphoreType.DMA((7,)), pltpu.SemaphoreType.DMA],
        compiler_params=pltpu.CompilerParams(vmem_limit_bytes=VMEM_LIMIT),
    )(v, *after)


HBM = pl.BlockSpec(memory_space=pltpu.HBM)
SEM = pl.BlockSpec(memory_space=pltpu.SEMAPHORE)
EFFECT = pltpu.SideEffectType.DATAFLOW_SIDE_EFFECTING


def _other_chips(x, y):
    return [(1 - x, y), (x, 1 - y), (1 - x, 1 - y)]


def _bulk_start(name, srcs, land_shapes, n_copies, copies, after, lands_init=None):
    n, m = len(srcs), len(land_shapes)

    def body(*refs):
        src_refs, land_refs = refs[:n], refs[n : n + m]
        send, recv = refs[n + m + 1], refs[n + m + 2]
        token = refs[-1]
        for k, (s, d, dev) in enumerate(copies(src_refs, land_refs)):
            _remote(s, d, send.at[k], recv.at[k], dev).start()
        token[...] = jnp.zeros_like(token)

    lands = lands_init or [lax.empty(s.shape, s.dtype) for s in land_shapes]
    lands = [pltpu.with_memory_space_constraint(b, pltpu.HBM) for b in lands]
    out = pl.pallas_call(
        body,
        name=name,
        out_shape=(pltpu.SemaphoreType.DMA((n_copies,)), pltpu.SemaphoreType.DMA((n_copies,)),
                   *[pltpu.HBM(s.shape, s.dtype) for s in srcs], *[pltpu.HBM(s.shape, s.dtype) for s in land_shapes],
                   jax.ShapeDtypeStruct((8, LANES), F32)),
        in_specs=[HBM] * (n + m) + [ANY],
        out_specs=(SEM, SEM, *[HBM] * (n + m), pl.BlockSpec(memory_space=pltpu.VMEM)),
        input_output_aliases={i: 2 + i for i in range(n + m)},
        compiler_params=pltpu.CompilerParams(has_side_effects=EFFECT),
    )(*[pltpu.with_memory_space_constraint(s, pltpu.HBM) for s in srcs], *lands, after)
    return out[0], out[1], list(out[2 : 2 + n]), list(out[2 + n : 2 + n + m]), out[-1][0:1, 0:1]


def _bulk_wait(name, send, recv, srcs, lands, after, waits):
    n, m = len(srcs), len(lands)

    def body(*refs):
        src_refs, land_refs = refs[:n], refs[n : n + m]
        send_sem, recv_sem = refs[n + m], refs[n + m + 1]
        x, y, c = _place()
        for k, (s, d) in enumerate(waits(src_refs, land_refs)):
            cp = _remote(s, d, send_sem.at[k], recv_sem.at[k], (x, y, c))
            cp.wait_send()
            cp.wait_recv()

    out = pl.pallas_call(
        body,
        name=name,
        out_shape=tuple(pltpu.HBM(s.shape, s.dtype) for s in (*srcs, *lands)),
        in_specs=[HBM] * (n + m) + [SEM, SEM, ANY],
        out_specs=tuple([HBM] * (n + m)),
        input_output_aliases={i: i for i in range(n + m)},
        compiler_params=pltpu.CompilerParams(has_side_effects=EFFECT),
    )(*srcs, *lands, send, recv, after)
    return list(out[:n]), list(out[n:])


def _peers(x, y, c):
    return [(x ^ (k >> 2), y ^ ((k >> 1) & 1), c ^ (k & 1)) for k in range(1, 8)]


def _small_gather_start(v, after, name):
    r, w = v.shape

    def copies(src, land):
        x, y, c = _place()
        return [(src[0], land[0].at[4 * x + 2 * y + c], peer) for peer in _peers(x, y, c)]

    me = 4 * lax.axis_index("x") + 2 * lax.axis_index("y") + lax.axis_index("c")
    init = [lax.dynamic_update_slice(lax.empty((8, r, w), F32), v[None], (me, 0, 0))]
    return _bulk_start(name, [v], [jax.ShapeDtypeStruct((8, r, w), F32)], 7, copies, after, init)


def _small_gather_wait(started, after, name):
    send, recv, srcs, lands, _ = started

    def waits(src, land):
        x, y, c = _place()
        return [(src[0], land[0].at[4 * px + 2 * py + pc]) for px, py, pc in _peers(x, y, c)]

    return _bulk_wait(name, send, recv, srcs, lands, after, waits)[1][0]


def _gather_start(shards, after, name):
    def copies(src, land):
        x, y, c = _place()
        j = 2 * x + y
        return [(src[a].at[c], land[a].at[j, c], (px, py, c)) for a in range(len(shards)) for px, py in _other_chips(x, y)]

    shapes = [jax.ShapeDtypeStruct((4,) + s.shape, s.dtype) for s in shards]
    j = 2 * lax.axis_index("x") + lax.axis_index("y")
    init = [lax.dynamic_update_slice(lax.empty(t.shape, t.dtype), s[None], (j, 0, 0, 0)) for t, s in zip(shapes, shards)]
    return _bulk_start(name, shards, shapes, 3 * len(shards), copies, after, init)


def _gather_wait(started, after, name):
    send, recv, srcs, lands, _ = started

    def waits(src, land):
        x, y, c = _place()
        return [(src[a].at[c], land[a].at[2 * px + py, c]) for a in range(len(srcs)) for px, py in _other_chips(x, y)]

    return _bulk_wait(name, send, recv, srcs, lands, after, waits)


def _forward_start(lands, after, name):
    def copies(src, _):
        x, y, c = _place()
        blocks = [src[a].at[2 * px + py, c] for a in range(len(lands)) for px, py in _other_chips(x, y)]
        return [(b, b, (x, y, 1 - c)) for b in blocks]

    return _bulk_start(name, lands, [], 3 * len(lands), copies, after)


def _forward_wait(started, after, name):
    send, recv, bufs, _, _ = started

    def waits(src, _):
        x, y, c = _place()
        return [(src[a].at[2 * px + py, c], src[a].at[2 * px + py, 1 - c]) for a in range(len(bufs)) for px, py in _other_chips(x, y)]

    return _bulk_wait(name, send, recv, bufs, [], after, waits)[0]


def _as_rows(lands):
    return [f.reshape(4 * f.shape[2] * 2, f.shape[3]) for f in lands]


def _gather_land(started, after, tag):
    shards, lands = _gather_wait(started, after, "gather_wait_" + tag)
    return shards, _forward_start(lands, shards[0], "forward_start_" + tag)


def _gather_done(landed, after, tag):
    _, fwd = landed
    return _as_rows(_forward_wait(fwd, after, "forward_wait_" + tag))


def _swap_halves(grads, name):
    n = len(grads)

    def body(*refs):
        ins, outs = refs[:n], refs[n : 2 * n]
        send, recv = refs[2 * n :]
        x, y, c = _place()
        started = []
        for a in range(n):
            for s in range(4):
                cp = _remote(ins[a].at[s, 1 - c], outs[a].at[s], send.at[4 * a + s], recv.at[4 * a + s], (x, y, 1 - c))
                cp.start()
                started.append(cp)
        for cp in started:
            cp.wait_recv()
        for cp in started:
            cp.wait_send()

    return pl.pallas_call(
        body,
        name=name,
        out_shape=[jax.ShapeDtypeStruct((4,) + g.shape[2:], g.dtype) for g in grads],
        in_specs=[ANY] * n,
        out_specs=[ANY] * n,
        scratch_shapes=[pltpu.SemaphoreType.DMA((4 * n,)), pltpu.SemaphoreType.DMA((4 * n,))],
    )(*grads)


def _add_halves(grads, others, tag):
    outs = []
    for a, (g, o) in enumerate(zip(grads, others)):
        _, _, rh, cdim = g.shape
        tr = _pick(rh, 512, 16)

        def body(g_ref, o_ref, p_ref):
            p_ref[...] = (g_ref[...].astype(F32) + o_ref[...].astype(F32)).astype(BF16)

        outs.append(
            pl.pallas_call(
                body,
                name=f"add_halves_{tag}{a}",
                grid=(4, rh // tr),
                in_specs=[pl.BlockSpec((None, None, tr, cdim), lambda s, i: (s, lax.axis_index("c"), i, 0)),
                          pl.BlockSpec((None, tr, cdim), lambda s, i: (s, i, 0))],
                out_specs=pl.BlockSpec((None, tr, cdim), lambda s, i: (s, i, 0)),
                out_shape=jax.ShapeDtypeStruct((4, rh, cdim), BF16),
                compiler_params=_params(("parallel", "parallel")),
            )(g, o)
        )
    return outs


def _exchange_start(parts, after, name):
    def copies(src, land):
        x, y, c = _place()
        j = 2 * x + y
        return [(src[a].at[2 * px + py], land[a].at[j], (px, py, c)) for a in range(len(parts)) for px, py in _other_chips(x, y)]

    return _bulk_start(name, parts, [jax.ShapeDtypeStruct(p.shape, p.dtype) for p in parts], 3 * len(parts), copies, after)


def _exchange_finish(started, after, name):
    send, recv, srcs, lands, _ = started

    def waits(src, land):
        x, y, _ = _place()
        return [(src[a].at[2 * px + py], land[a].at[2 * px + py]) for a in range(len(srcs)) for px, py in _other_chips(x, y)]

    srcs, lands = _bulk_wait(name, send, recv, srcs, lands, after, waits)
    return lands, srcs


def _sum_chips(recvd, parts, tag):
    outs = []
    for a, (g, p) in enumerate(zip(recvd, parts)):
        _, rh, cdim = g.shape
        tr = _pick(rh, 512, 16)

        def body(g_ref, p_ref, o_ref):
            j = 2 * lax.axis_index("x") + lax.axis_index("y")
            own = p_ref[...].astype(F32)
            term = [jnp.where(j == s, own, g_ref[s].astype(F32)) for s in range(4)]
            o_ref[...] = ((term[0] + term[1]) + term[2]) + term[3]

        outs.append(
            pl.pallas_call(
                body,
                name=f"sum_chips_{tag}{a}",
                grid=(rh // tr,),
                in_specs=[pl.BlockSpec((4, tr, cdim), lambda i: (0, i, 0)),
                          pl.BlockSpec((None, tr, cdim), lambda i: (2 * lax.axis_index("x") + lax.axis_index("y"), i, 0))],
                out_specs=pl.BlockSpec((tr, cdim), lambda i: (i, 0)),
                out_shape=jax.ShapeDtypeStruct((rh, cdim), F32),
                compiler_params=_params(("parallel",)),
            )(g, p)
        )
    return outs


def _joined(mine, other):
    first = lax.axis_index("c") == 0
    return jnp.concatenate([jnp.where(first, mine, other), jnp.where(first, other, mine)], axis=0)


def _grad_views(grads):
    return [g.reshape(4, 2, g.shape[0] // 8, g.shape[1]) for g in grads]


def _scatter_start(grads, tag, after=None):
    views = _grad_views(grads)
    others = _swap_halves(views, "swap_halves_" + tag)
    mine = _add_halves(views, others, tag)
    return _exchange_start(mine, others[-1] if after is None else after, "exchange_start_" + tag)


def _swap_start(grads, after, tag):
    views = _grad_views(grads)

    def copies(src, land):
        x, y, c = _place()
        return [(src[a].at[s, 1 - c], land[a].at[s], (x, y, 1 - c)) for a in range(len(views)) for s in range(4)]

    shapes = [jax.ShapeDtypeStruct((4,) + v.shape[2:], v.dtype) for v in views]
    return _bulk_start("swap_start_" + tag, views, shapes, 4 * len(views), copies, after)


def _scatter_start_after_swap(swapped, after, tag):
    send, recv, views, lands, _ = swapped

    def waits(src, land):
        c = lax.axis_index("c")
        return [(src[a].at[s, 1 - c], land[a].at[s]) for a in range(len(views)) for s in range(4)]

    views, others = _bulk_wait("swap_wait_" + tag, send, recv, views, lands, after, waits)
    mine = _add_halves(views, others, tag)
    return _exchange_start(mine, others[-1], "exchange_start_" + tag)


def _join_start(halves, after, tag):
    def copies(src, land):
        x, y, c = _place()
        return [(src[a], land[a], (x, y, 1 - c)) for a in range(len(halves))]

    return _bulk_start("join_start_" + tag, halves, [jax.ShapeDtypeStruct(h.shape, h.dtype) for h in halves], len(halves), copies, after)


def _join_wait(started, after, tag):
    send, recv, halves, lands, _ = started
    halves, others = _bulk_wait("join_wait_" + tag, send, recv, halves, lands, after, lambda src, land: list(zip(src, land)))
    return list(zip(halves, others))


def _scatter_sums(started, after, tag):
    return _sum_chips(*_exchange_finish(started, after, "exchange_wait_" + tag), tag)


def _t_bf16(w):
    return w.T.astype(BF16)


def kernel(x, c, ctx, c_ctx, w_ada, b_ada, norm1_g, w_in, mla_q_norm_g, w_q_up, mla_kv_norm_g, w_kv_up, gqa_q_norm_g, gqa_k_norm_g, w_br_a, w_br_b, w_out, norm2_g, w_up, conv_w, conv_b, w_down, final_norm_g, loss_target, m_c_ctx, m_w_ada, m_b_ada, m_norm1_g, m_w_in, m_mla_q_norm_g, m_w_q_up, m_mla_kv_norm_g, m_w_kv_up, m_gqa_q_norm_g, m_gqa_k_norm_g, m_w_br_a, m_w_br_b, m_w_out, m_norm2_g, m_w_up, m_conv_w, m_conv_b, m_w_down, m_final_norm_g, v_c_ctx, v_w_ada, v_b_ada, v_norm1_g, v_w_in, v_mla_q_norm_g, v_w_q_up, v_mla_kv_norm_g, v_w_kv_up, v_gqa_q_norm_g, v_gqa_k_norm_g, v_w_br_a, v_w_br_b, v_w_out, v_norm2_g, v_w_up, v_conv_w, v_conv_b, v_w_down, v_final_norm_g):
    T, D = x.shape[1], x.shape[2]
    C = ctx.shape[1]
    NA = w_ada.shape[2]
    NW = w_up.shape[2]
    F2 = 4 * NW
    FF = F2 // 2
    xi, yi, ci = _place()
    j = 2 * xi + yi
    me = 4 * xi + 2 * yi + ci
    tr = _pick(C, 256, 8)

    x2d, tgt, ctx2d = x[0], loss_target[0], ctx[0]
    fg = final_norm_g.reshape(1, D)
    cc = c_ctx.reshape(1, D)

    halve = lambda s: s.reshape(2, s.shape[0] // 2, s.shape[1])
    win_shard = halve(_t_bf16(w_in[0]))
    w0 = max(D, NW)
    pay = jnp.zeros((8, w0), F32).at[0:1, :D].set(c).at[1:4, :NW].set(conv_w[0])
    got = _all_gather_small(pay, "gather_cond")
    ag_in = _gather_start([win_shard], got, "gather_start_in")
    t_in = ag_in[4]
    c_all = got[:, 0, :D]
    cw = jnp.concatenate([got[2 * s, 1:4, :NW] for s in range(4)], axis=1)
    s16 = jnp.concatenate([c_all, cc, jnp.zeros((7, D), F32)], axis=0) + t_in
    b_cols = lax.dynamic_slice(b_ada, (0, j * NA), (1, NA))
    ada_part = _mm(s16, w_ada[0], "NN", F32, "ada_fwd", act="silu", bias=b_cols)

    wq3 = (w_q_up[0] + t_in).reshape(MLA_Q_LORA, 2, MLA_NOPE + MLA_ROPE)
    wq_perm = jnp.concatenate([wq3[:, :, :MLA_NOPE].reshape(MLA_Q_LORA, -1), wq3[:, :, MLA_NOPE:].reshape(MLA_Q_LORA, -1)], axis=1)
    low = [halve(_t_bf16(wq_perm)), halve(_t_bf16(w_kv_up[0] + t_in))]
    br = [halve(_t_bf16(w_br_a[0] + t_in)), halve(_t_bf16(w_br_b[0] + t_in)), halve((w_out[0] + t_in).astype(BF16))]
    up = [halve(_t_bf16(w_up[0] + t_in))]
    down = [halve((w_down[0] + t_in).astype(BF16))]

    got = _all_gather_small(ada_part, "gather_ada", after=(*low, *br, *up, *down))
    ada = jnp.concatenate([got[2 * s] for s in range(4)], axis=1)
    lat = lax.dynamic_slice(ada, (me, 0), (1, 6 * D))
    sh1, sc1, g1, sh2, sc2, g2 = [lat[:, k * D : (k + 1) * D] for k in range(6)]
    csh, csc = ada[8:9, :D], ada[8:9, D : 2 * D]
    ag_low = _gather_start(low, got, "gather_start_low")
    ag_br = _gather_start(br, ag_low[4], "gather_start_br")
    ag_up = _gather_start(up, ag_br[4], "gather_start_up")
    ag_down = _gather_start(down, ag_up[4], "gather_start_down")
    sh1 = sh1 + ag_down[4]

    cos_a, ss_a = _rope_tables(C, T, MLA_ROPE)
    cos_b, ss_b = _rope_tables(C, T, GQA_HEAD_DIM)
    lcos_a, lss_a, lcos_b, lss_b = cos_a[:T], ss_a[:T], cos_b[:T], ss_b[:T]

    in_landed = _gather_land(ag_in, down[0], "in")
    z_all = _norm_mod_fwd(x2d, norm1_g, sh1 + in_landed[1][4], sc1, "norm1_lat_fwd", tr, out_rows=T + C)
    z_all = _norm_mod_fwd(ctx2d, norm1_g, csh, csc, "norm1_ctx_fwd", tr, base=z_all, out_off=T)
    (win_t,) = _gather_done(in_landed, z_all, "in")
    kv_cols = KVP - LANES + MLA_ROPE
    e_kpe = MLA_KV_LORA + MLA_ROPE
    w_kvp = jnp.concatenate([win_t[:MLA_KV_LORA], win_t[e_kpe:kv_cols], win_t[MLA_KV_LORA:e_kpe], jnp.zeros((LANES - MLA_ROPE, D), BF16)], axis=0)

    pkv = _mm(z_all, w_kvp, "NT", F32, "proj_kv", tn=KVP)
    pq = _mm(z_all, win_t, "NT", F32, "proj_q", m=T, n=QC, b_off=kv_cols)
    low_landed = _gather_land(ag_low, pq, "low")
    pg = _mm(z_all, win_t, "NT", BF16, "proj_g", m=T, n=2 * D, b_off=kv_cols + QC, after=low_landed[1][4])
    wq_t, wkv_t = _gather_done(low_landed, pg, "low")
    ckv_n, kb2, vb2, kpe2 = _kprep_fwd(pkv, mla_kv_norm_g, gqa_k_norm_g, cos_a, ss_a, cos_b, ss_b, tr)
    kv_up = _mm(ckv_n, wkv_t, "NT", BF16, "kv_up")
    cq_n, qb2 = _qprep_fwd(pq, mla_q_norm_g, gqa_q_norm_g, lcos_b, lss_b, tr)
    q_a = _mm(cq_n, wq_t, "NT", F32, "q_up")
    qar = _qrope_fwd(q_a, lcos_a, lss_a, tr)

    a_q = [(qar, lambda h: 3 * (h // 2) + h % 2), (qar, lambda h: 3 * (h // 2) + 2)]
    a_k = [(kv_up, lambda h: 2 * h), (kpe2, lambda h: h % 2)]
    a_v = (kv_up, lambda h: 2 * h + 1)
    a_scale = float(MLA_NOPE + MLA_ROPE) ** -0.5
    b_q = [(qb2, lambda h: h)]
    b_k = [(kb2, lambda h: h)]
    b_v = (vb2, lambda h: h)
    b_scale = float(GQA_HEAD_DIM) ** -0.5
    tq_f = _pick(T, 2048)
    o_a, lse_a = _attn_fwd(a_q, a_k, a_v, MLA_HEADS, 1, MLA_V, a_scale, "attn_a_fwd", tq_f)
    br_landed = _gather_land(ag_br, o_a, "br")
    o_b, lse_b = _attn_fwd(b_q, b_k, b_v, GQA_HEADS, GQA_GROUP, GQA_HEAD_DIM, b_scale, "attn_b_fwd", tq_f, after=br_landed[1][4])
    wbra_t, wbrb_t, wout = _gather_done(br_landed, o_b, "br")
    up_landed = _gather_land(ag_up, o_b, "up")
    ya = _mm(o_a, wbra_t, "NT", BF16, "br_a", after=up_landed[1][4])
    yb = _mm(o_b, wbrb_t, "NT", BF16, "br_b")
    merged = _gates_fwd(pg, ya, yb, tr)
    att = _mm(merged, wout, "NN", F32, "out_proj")
    x1, z2 = _resid_norm2_fwd(x2d, att, g1, norm2_g, sh2, sc2, tr)
    (wup_t,) = _gather_done(up_landed, z2, "up")
    down_landed = _gather_land(ag_down, z2, "down")
    tc = _pick(FF, 128)
    u_a, u_b, hg = _ffn_up_conv(z2, wup_t, cw, conv_b, tc, down_landed[1][4])
    (wdown,) = _gather_done(down_landed, hg, "down")
    f = _mm(hg, wdown, "NN", F32, "ffn_down", tk=FF // 2)
    sq, dx2, d_fg, d_g2, df = _loss_head(x1, f, g2, fg, tgt, tr)
    loss_part = (0.5 * jnp.sum(sq) / D).reshape(1, 1)

    du_a, du_b, dcw_a, dcw_b, dcb_a, dcb_b = _ffn_down_dx_conv_bwd(df, wdown, u_a, u_b, cw, conv_b, _pick(FF, 256), loss_part)
    g_wdown = _mm(hg, df, "TN", BF16, "ffn_down_dw", tm=FF // 4)
    dz2 = _mm(du_a, wup_t, "NN", F32, "ffn_up_dx_a", tk=FF // 2)
    dz2 = _mm(du_b, wup_t, "NN", F32, "ffn_up_dx_b", b_off=FF, add=dz2, tk=FF // 2)
    g_wup_t = _mm(du_a, z2, "TN", BF16, "ffn_up_dw_a", out_rows=F2, tm=FF // 4)
    g_wup_t = _mm(du_b, z2, "TN", BF16, "ffn_up_dw_b", out_base=g_wup_t, out_off=FF, tm=FF // 4)
    sw_ffn = _swap_start([g_wdown, g_wup_t], sc2, "ffn")
    sc2 = sc2 + sw_ffn[4]
    dx1, datt, d_n2g, d_sh2, d_sc2, d_g1 = _resid_norm2_bwd(dz2, x1, dx2, att, norm2_g, sc2, g1, tr)

    dmerged = _mm(datt, wout, "NT", BF16, "out_proj_dx")
    rs_ffn = _scatter_start_after_swap(sw_ffn, dmerged, "ffn")
    lse_a = lse_a + rs_ffn[4]
    g_wout = _mm(merged, datt, "TN", BF16, "out_proj_dw")
    dya, dyb, dpg = _gates_bwd(dmerged, pg, ya, yb, tr)
    do_a = _mm(dya, wbra_t, "NN", BF16, "br_a_dx")
    g_wbra_t = _mm(dya, o_a, "TN", BF16, "br_a_dw")
    do_b = _mm(dyb, wbrb_t, "NN", BF16, "br_b_dx")
    g_wbrb_t = _mm(dyb, o_b, "TN", BF16, "br_b_dw")
    dqa2, dka2, dva2 = _attn_bwd(a_q, a_k, a_v, o_a, do_a, lse_a, MLA_HEADS, 1, MLA_V, a_scale, "attn_a_bwd", tq_f)
    dqb2, dkb2, dvb2 = _attn_bwd(b_q, b_k, b_v, o_b, do_b, lse_b, GQA_HEADS, GQA_GROUP, GQA_HEAD_DIM, b_scale, "attn_b_bwd", tq_f)
    dq_a = _qrope_bwd(dqa2, lcos_a, lss_a, tr)
    dcq_n = _mm(dq_a, wq_t, "NN", F32, "q_up_dx")
    g_wq_t = _mm(dq_a, cq_n, "TN", BF16, "q_up_dw")
    dpq, d_qg, d_gq = _qprep_bwd(pq, dcq_n, dqb2, mla_q_norm_g, gqa_q_norm_g, lcos_b, lss_b, tr)
    dkv_up, dkpe = _kgrad_split(dka2, dva2, cos_a, ss_a, tr)
    dckv_n = _mm(dkv_up, wkv_t, "NN", F32, "kv_up_dx")
    g_wkv_t = _mm(dkv_up, ckv_n, "TN", BF16, "kv_up_dw")
    sw_mix = _swap_start([g_wq_t, g_wkv_t, g_wbra_t, g_wbrb_t, g_wout], mla_kv_norm_g, "mix")
    dpkv, d_kvg, d_kg = _kprep_bwd(pkv, dckv_n, dkb2, dvb2, dkpe, mla_kv_norm_g + sw_mix[4], gqa_k_norm_g, cos_b, ss_b, tr)
    dz_kv = _mm(dpkv, w_kvp, "NN", F32, "proj_kv_dx")
    rs_mix = _scatter_start_after_swap(sw_mix, dz_kv, "mix")
    dz_lat = _mm(dpq, win_t, "NN", F32, "proj_q_dx", b_off=kv_cols, add=dz_kv, after=rs_mix[4])
    dz_lat = _mm(dpg, win_t, "NN", F32, "proj_g_dx", b_off=kv_cols + QC, add=dz_lat)
    _, d_n1g_c, d_csh, d_csc = _norm_mod_bwd(dz_kv, T // tr, ctx2d, norm1_g, csc, None, "norm1_ctx_bwd", tr)
    grad_x, d_n1g_l, d_sh1, d_sc1 = _norm_mod_bwd(dz_lat, 0, x2d, norm1_g, sc1, dx1, "norm1_lat_bwd", tr)

    zeros_d = jnp.zeros((1, D), F32)
    d_lat = jnp.concatenate([d_sh1, d_sc1, d_g1, d_sh2, d_sc2, d_g2], axis=1)
    d_ctx_part = jnp.concatenate([d_csh, d_csc], axis=1)
    flat = jnp.concatenate(
        [d_n1g_c + d_n1g_l, d_qg, d_kvg, d_gq, d_kg, d_n2g, dcb_a, dcb_b, d_fg,
         dcw_a.reshape(1, -1), dcw_b.reshape(1, -1), d_ctx_part, d_lat, loss_part], axis=1)
    n_flat = flat.shape[1]
    n_rows = -(-n_flat // (8 * LANES)) * 8
    flat = jnp.pad(flat, ((0, 0), (0, n_rows * LANES - n_flat))).reshape(n_rows, LANES)
    small = _small_gather_start(flat, grad_x, "small_grads_start")

    g_kvp = _mm(dpkv, z_all, "TN", BF16, "proj_kv_dw", after=small[4], tm=KVP)
    nk = MLA_KV_LORA + 2 * GQA_KV_HEADS * GQA_HEAD_DIM
    g_kv = jnp.concatenate([g_kvp[:MLA_KV_LORA], g_kvp[nk : nk + MLA_ROPE], g_kvp[MLA_KV_LORA:nk]], axis=0)
    g_win_t = _mm(dpq, z_all, "TN", BF16, "proj_q_dw", out_rows=kv_cols + QC + 2 * D, out_off=kv_cols, tm=QC // 2)
    g_win_t = _mm(dpg, z_all, "TN", BF16, "proj_g_dw", out_base=g_win_t, out_off=kv_cols + QC)
    g_win_t = lax.dynamic_update_slice(g_win_t, g_kv, (0, 0))

    got = _small_gather_wait(small, g_win_t, "small_grads_wait")
    tot = _sum_slots(got, "sum_small_grads").reshape(1, -1)
    sizes = [D, MLA_Q_LORA, MLA_KV_LORA, GQA_HEAD_DIM, GQA_HEAD_DIM, D, F2, D, 3 * FF, 3 * FF, 2 * D]
    offs = [0]
    for s in sizes:
        offs.append(offs[-1] + s)
    t_n1g, t_qg, t_kvg, t_gq, t_kg, t_n2g, t_cb, t_fg, t_cwa, t_cwb, t_ctx = [tot[:, offs[k] : offs[k + 1]] for k in range(len(sizes))]
    loss = tot[0, offs[-1] + 6 * D]
    g_cw_full = jnp.concatenate([t_cwa.reshape(3, FF), t_cwb.reshape(3, FF)], axis=1)
    g_cw = lax.dynamic_slice(g_cw_full, (0, j * NW), (3, NW))
    d_lat_all = got.reshape(8, -1)[:, offs[-1] : offs[-1] + 6 * D]
    g16 = jnp.concatenate([d_lat_all, jnp.pad(t_ctx, ((0, 0), (0, 4 * D))), jnp.zeros((7, 6 * D), F32)], axis=0)
    g_b_ada = _sum_slots(g16.reshape(16, 1, 6 * D), "sum_b_ada")
    g16_cols = lax.dynamic_slice(g16, (0, j * NA), (16, NA))
    ds_part = _mm(g16_cols, w_ada[0], "NT", F32, "ada_dx")
    ada_dx = _small_gather_start(ds_part[8:16], got, "ada_dx_start")
    sw_in = _swap_start([g_win_t], ada_dx[4], "in")

    h_ffn = _scatter_sums(rs_ffn, sw_in[2][0], "ffn")
    got = _small_gather_wait(ada_dx, h_ffn[0], "ada_dx_wait")
    ds_ctx = _sum_slots(jnp.stack([got[2 * s] for s in range(4)]), "sum_ada_dx")[0:1]
    g_c_ctx = _silu_grad_mul(ds_ctx, cc)
    j_ffn = _join_start(h_ffn, grad_x, "ffn")
    h_mix = _scatter_sums(rs_mix, j_ffn[2][0], "mix")
    j_mix = _join_start(h_mix, j_ffn[2][0], "mix")
    rs_in = _scatter_start_after_swap(sw_in, j_mix[2][0], "in")
    g_w_ada = _mm(s16, g16_cols, "TN", F32, "ada_dw", act="silu", after=rs_in[4])
    _, d_ada, m_ada, v_ada = _adamw(w_ada[0], g_w_ada, m_w_ada[0], v_w_ada[0], "adamw_w_ada")
    r_wdown, r_wup = _join_wait(j_ffn, d_ada, "ffn")
    r_wq, r_wkv, r_wbra, r_wbrb, r_wout = _join_wait(j_mix, d_ada, "mix")
    gq_p = _joined(*r_wq).T
    gq = jnp.concatenate([gq_p[:, : 2 * MLA_NOPE].reshape(MLA_Q_LORA, 2, MLA_NOPE), gq_p[:, 2 * MLA_NOPE :].reshape(MLA_Q_LORA, 2, MLA_ROPE)], axis=2)
    grads = {
        "c_ctx": g_c_ctx.reshape(D), "w_ada": g_w_ada[None], "b_ada": g_b_ada, "norm1_g": t_n1g,
        "mla_q_norm_g": t_qg, "w_q_up": gq.reshape(1, MLA_Q_LORA, -1), "mla_kv_norm_g": t_kvg, "w_kv_up": r_wkv,
        "gqa_q_norm_g": t_gq, "gqa_k_norm_g": t_kg, "w_br_a": r_wbra, "w_br_b": r_wbrb, "w_out": r_wout,
        "norm2_g": t_n2g, "w_up": r_wup, "conv_w": g_cw[None], "conv_b": t_cb, "w_down": r_wdown,
        "final_norm_g": t_fg.reshape(D),
    }
    arrives_transposed = ("w_kv_up", "w_br_a", "w_br_b", "w_up")
    arrives_halved = arrives_transposed + ("w_out", "w_down")
    weights = dict(c_ctx=c_ctx, w_ada=w_ada, b_ada=b_ada, norm1_g=norm1_g, w_in=w_in, mla_q_norm_g=mla_q_norm_g, w_q_up=w_q_up,
                   mla_kv_norm_g=mla_kv_norm_g, w_kv_up=w_kv_up, gqa_q_norm_g=gqa_q_norm_g, gqa_k_norm_g=gqa_k_norm_g, w_br_a=w_br_a,
                   w_br_b=w_br_b, w_out=w_out, norm2_g=norm2_g, w_up=w_up, conv_w=conv_w, conv_b=conv_b, w_down=w_down,
                   final_norm_g=final_norm_g)
    m_in = dict(c_ctx=m_c_ctx, w_ada=m_w_ada, b_ada=m_b_ada, norm1_g=m_norm1_g, w_in=m_w_in, mla_q_norm_g=m_mla_q_norm_g,
                w_q_up=m_w_q_up, mla_kv_norm_g=m_mla_kv_norm_g, w_kv_up=m_w_kv_up, gqa_q_norm_g=m_gqa_q_norm_g,
                gqa_k_norm_g=m_gqa_k_norm_g, w_br_a=m_w_br_a, w_br_b=m_w_br_b, w_out=m_w_out, norm2_g=m_norm2_g, w_up=m_w_up,
                conv_w=m_conv_w, conv_b=m_conv_b, w_down=m_w_down, final_norm_g=m_final_norm_g)
    v_in = dict(c_ctx=v_c_ctx, w_ada=v_w_ada, b_ada=v_b_ada, norm1_g=v_norm1_g, w_in=v_w_in, mla_q_norm_g=v_mla_q_norm_g,
                w_q_up=v_w_q_up, mla_kv_norm_g=v_mla_kv_norm_g, w_kv_up=v_w_kv_up, gqa_q_norm_g=v_gqa_q_norm_g,
                gqa_k_norm_g=v_gqa_k_norm_g, w_br_a=v_w_br_a, w_br_b=v_w_br_b, w_out=v_w_out, norm2_g=v_norm2_g, w_up=v_w_up,
                conv_w=v_conv_w, conv_b=v_conv_b, w_down=v_w_down, final_norm_g=v_final_norm_g)
    names = list(weights)
    big = [n for n in names if weights[n].ndim == 3 and weights[n].shape[1] >= 8]
    small = [n for n in names if n not in big]
    delta, new_m, new_v = {}, {}, {}

    def update(n):
        shp = weights[n].shape
        two_d = lambda a: a.reshape(shp[1], shp[2])
        g_t = n in arrives_transposed
        if n in arrives_halved:
            g_in, g_sib = grads[n]
        else:
            g_in, g_sib = two_d(grads[n].astype(F32)), None
        g_, d_, m_, v_ = _adamw(two_d(weights[n]), g_in, two_d(m_in[n]), two_d(v_in[n]), "adamw_" + n, g_transposed=g_t, g_sibling=g_sib)
        grads[n], delta[n], new_m[n], new_v[n] = g_.reshape(shp), d_.reshape(shp), m_.reshape(shp), v_.reshape(shp)

    delta["w_ada"], new_m["w_ada"], new_v["w_ada"] = d_ada[None], m_ada[None], v_ada[None]
    early = [n for n in big if n not in ("w_in", "w_ada")]
    for n in early[:-1]:
        update(n)
    done = sum(delta[n][0, 0:1, 0:1] for n in early[:-1])
    j_in = _join_start(_scatter_sums(rs_in, done, "in"), done, "in")
    last = early[-1]
    grads[last] = (grads[last][0] + j_in[4], grads[last][1])
    update(last)
    ((g_mine, g_sib),) = _join_wait(j_in, delta[last], "in")
    g_, d_, m_, v_ = _adamw(w_in[0].T, g_mine, m_w_in[0].T, v_w_in[0].T, "adamw_w_in", g_sibling=g_sib)
    grads["w_in"], delta["w_in"], new_m["w_in"], new_v["w_in"] = g_.T[None], d_.T[None], m_.T[None], v_.T[None]
    grads = {n: grads[n].reshape(weights[n].shape).astype(F32) for n in names}

    slab = lambda tree: [tree[n].reshape(-1, LANES) for n in small]
    d_, m_, v_ = _adamw_many(slab(weights), slab(grads), slab(m_in), slab(v_in), "adamw_small")
    for k, n in enumerate(small):
        shp = weights[n].shape
        delta[n], new_m[n], new_v[n] = d_[k].reshape(shp), m_[k].reshape(shp), v_[k].reshape(shp)

    return (loss, grad_x[None], *[grads[n] for n in names], *[delta[n] for n in names], *[new_m[n] for n in names],
            *[new_v[n] for n in names])
```

```python
import math

import jax
import jax.numpy as jnp
from jax import lax
from jax.experimental import pallas as pl
from jax.experimental.pallas import tpu as pltpu

F32 = jnp.float32
BF16 = jnp.bfloat16
MESH = pl.DeviceIdType.MESH

NORM_EPS = 1e-6
ROPE_THETA = 10000.0
GRID_W = 64
MLA_HEADS = 8
MLA_Q_LORA = 768
MLA_KV_LORA = 512
MLA_NOPE = 128
MLA_ROPE = 64
MLA_V = 128
GQA_HEADS = 8
GQA_KV_HEADS = 2
GQA_HEAD_DIM = 128
GQA_GROUP = GQA_HEADS // GQA_KV_HEADS
LANES = 128
KVP = MLA_KV_LORA + 2 * GQA_KV_HEADS * GQA_HEAD_DIM + LANES
QC = MLA_Q_LORA + GQA_HEADS * GQA_HEAD_DIM

ADAM_LR = 0.001
ADAM_B1 = 0.9
ADAM_B2 = 0.999
ADAM_EPS = 1e-08
ADAM_WD = 0.01
ADAM_STEP = 10

VMEM_LIMIT = 56 * 1024 * 1024


def _pick(dim, target, mult=LANES):
    t = (min(target, dim) // mult) * mult
    while t >= mult:
        if dim % t == 0:
            return t
        t -= mult
    return dim


def _params(sem):
    return pltpu.CompilerParams(dimension_semantics=sem, vmem_limit_bytes=VMEM_LIMIT)


_DIMS = {"NN": (((1,), (0,)), ((), ())), "NT": (((1,), (1,)), ((), ())), "TN": (((0,), (0,)), ((), ()))}


MM_VMEM_BUDGET = 36 * 1024 * 1024


def _mm_tiles(M, N, K, sa, sb, so, tm, tn, tk):
    tm, tn, tk = _pick(M, tm), _pick(N, tn), _pick(K, tk)

    def need(t):
        return 2 * (tm * t * sa + t * tn * sb) + 2 * tm * tn * so + (tm * tn * 4 if t < K else 0)

    while need(tk) > MM_VMEM_BUDGET and tk > LANES:
        smaller = _pick(K, tk - LANES)
        if smaller >= tk:
            break
        tk = smaller
    return tm, tn, tk


def _window(block, index, offsets):
    if not any(offsets):
        return pl.BlockSpec(block, index)
    for t, o in zip(block, offsets):
        assert o % 16 == 0 and t % 16 == 0, (block, offsets)

    def at(i, j, k):
        return tuple(pl.multiple_of(o + p * t, math.gcd(o, t)) for p, t, o in zip(index(i, j, k), block, offsets))

    return pl.BlockSpec(tuple(pl.Element(t) for t in block), at)


def _mm(a, b, mode, out_dtype, name, m=None, n=None, k=None, b_off=0, add=None, out_rows=None, out_base=None, out_off=0,
        tm=1024, tn=1024, tk=2304, act=None, bias=None, after=None):
    if mode == "NN":
        M, K, N = m or a.shape[0], k or a.shape[1], b.shape[1]
    elif mode == "NT":
        M, K, N = m or a.shape[0], a.shape[1], n or b.shape[0]
    else:
        M, K, N = a.shape[1], k or a.shape[0], b.shape[1]
    tm, tn, tk = _mm_tiles(M, N, K, a.dtype.itemsize, b.dtype.itemsize, jnp.dtype(out_dtype).itemsize, tm, tn, tk)
    nk = K // tk
    dims = _DIMS[mode]
    n_in = 2 + (bias is not None) + (add is not None) + (out_base is not None) + (after is not None)

    def body(*refs):
        a_ref, b_ref = refs[:2]
        bias_ref = refs[2] if bias is not None else None
        add_ref = refs[2 + (bias is not None)] if add is not None else None
        o_ref = refs[n_in]
        av = a_ref[...]
        if act == "silu":
            av = av * jax.nn.sigmoid(av)
        part = lax.dot_general(av.astype(BF16), b_ref[...].astype(BF16), dims, preferred_element_type=F32)

        def finish(r):
            if bias is not None:
                r = r + bias_ref[...]
            if add is not None:
                r = r + add_ref[...]
            o_ref[...] = r.astype(out_dtype)

        if nk == 1:
            finish(part)
            return
        acc = refs[-1]
        k = pl.program_id(2)

        @pl.when(k == 0)
        def _():
            acc[...] = part

        @pl.when(jnp.logical_and(k > 0, k < nk - 1))
        def _():
            acc[...] += part

        @pl.when(k == nk - 1)
        def _():
            finish(acc[...] + part)

    a_spec = pl.BlockSpec((tk, tm), lambda i, j, k: (k, i)) if mode == "TN" else pl.BlockSpec((tm, tk), lambda i, j, k: (i, k))
    if mode == "NT":
        b_spec = _window((tn, tk), lambda i, j, k: (j, k), (b_off, 0))
    else:
        b_spec = _window((tk, tn), lambda i, j, k: (k, j), (b_off, 0))
    in_specs, args = [a_spec, b_spec], [a, b]
    if bias is not None:
        in_specs.append(pl.BlockSpec((1, tn), lambda i, j, k: (0, j)))
        args.append(bias)
    if add is not None:
        in_specs.append(pl.BlockSpec((tm, tn), lambda i, j, k: (i, j)))
        args.append(add)
    aliases = {}
    if after is not None:
        in_specs.append(pl.BlockSpec(after.shape, lambda i, j, k: (0, 0)))
        args.append(after)
    if out_base is not None:
        aliases = {len(args): 0}
        in_specs.append(ANY)
        args.append(out_base)
        out_rows = out_base.shape[0]
    return pl.pallas_call(
        body,
        name=name,
        grid=(M // tm, N // tn, nk),
        in_specs=in_specs,
        out_specs=_window((tm, tn), lambda i, j, k: (i, j), (out_off, 0)),
        out_shape=jax.ShapeDtypeStruct((out_rows or M, N), out_dtype),
        input_output_aliases=aliases,
        scratch_shapes=[pltpu.VMEM((tm, tn), F32)] if nk > 1 else [],
        compiler_params=_params(("parallel", "parallel", "arbitrary")),
    )(*args)


def _rms(x):
    r = lax.rsqrt(jnp.mean(x * x, axis=-1, keepdims=True) + NORM_EPS)
    return x * r, r


def _rms_bwd(xh, r, dxh):
    return r * (dxh - xh * jnp.mean(dxh * xh, axis=-1, keepdims=True))


def _swap(x, q):
    lane = lax.broadcasted_iota(jnp.int32, x.shape, 1)
    even = ((lane // q) % 2) == 0
    return jnp.where(even, pltpu.roll(x, LANES - q, 1), pltpu.roll(x, q, 1))


def _rope(x, cos, ss, q):
    return x * cos + _swap(x, q) * ss


def _rope_t(d, cos, ss, q):
    return d * cos + _swap(d * ss, q)


def _csum(x):
    return jnp.sum(x, axis=0, keepdims=True)


def _rows(tr, w, off=0):
    return pl.BlockSpec((tr, w), lambda i: (i + off, 0))


def _bcast(w):
    return pl.BlockSpec((1, w), lambda i: (0, 0))


def _acc_init(i, refs):
    @pl.when(i == 0)
    def _():
        for r in refs:
            r[...] = jnp.zeros_like(r)


def _rope_tables(n_ctx, n_lat, rot_dim):
    rows = n_lat // GRID_W
    row = jnp.repeat(jnp.arange(rows, dtype=F32), GRID_W)
    col = jnp.tile(jnp.arange(GRID_W, dtype=F32), rows)
    half = rot_dim // 2
    inv_freq = ROPE_THETA ** (-jnp.arange(0, half, 2, dtype=F32) / half)
    ar, ac = row[:, None] * inv_freq, col[:, None] * inv_freq
    cos = jnp.concatenate([jnp.cos(ar), jnp.cos(ar), jnp.cos(ac), jnp.cos(ac)], axis=-1)
    ss = jnp.concatenate([-jnp.sin(ar), jnp.sin(ar), -jnp.sin(ac), jnp.sin(ac)], axis=-1)
    cos = jnp.tile(cos, (1, LANES // rot_dim))
    ss = jnp.tile(ss, (1, LANES // rot_dim))
    cos = jnp.concatenate([cos, jnp.ones((n_ctx, LANES), F32)], axis=0)
    ss = jnp.concatenate([ss, jnp.zeros((n_ctx, LANES), F32)], axis=0)
    return cos, ss


def _norm_mod_fwd(x2d, g, sh, sc, name, tr, out_rows=None, base=None, out_off=0):
    n, d = x2d.shape

    def body(x_ref, g_ref, sh_ref, sc_ref, *rest):
        xh, _ = _rms(x_ref[...])
        rest[-1][...] = ((xh * g_ref[...]) * (1.0 + sc_ref[...]) + sh_ref[...]).astype(BF16)

    args, in_specs, aliases = [x2d, g, sh, sc], [_rows(tr, d), _bcast(d), _bcast(d), _bcast(d)], {}
    if base is not None:
        args.append(base)
        in_specs.append(ANY)
        aliases = {4: 0}
        out_rows = base.shape[0]
    return pl.pallas_call(
        body,
        name=name,
        grid=(n // tr,),
        in_specs=in_specs,
        out_specs=_rows(tr, d, out_off // tr),
        out_shape=jax.ShapeDtypeStruct((out_rows or n, d), BF16),
        input_output_aliases=aliases,
        compiler_params=_params(("parallel",)),
    )(*args)


def _norm_mod_bwd(dz, dz_off, x2d, g, sc, dres, name, tr):
    n, d = x2d.shape
    want_dx = dres is not None

    def body(*refs):
        if want_dx:
            dz_ref, x_ref, g_ref, sc_ref, dres_ref, dx_ref, dg_ref, dsh_ref, dsc_ref = refs
        else:
            dz_ref, x_ref, g_ref, sc_ref, dg_ref, dsh_ref, dsc_ref = refs
        _acc_init(pl.program_id(0), [dg_ref, dsh_ref, dsc_ref])
        xh, r = _rms(x_ref[...])
        dzv = dz_ref[...]
        gv = g_ref[...]
        dsc_ref[...] += _csum(dzv * (xh * gv))
        dsh_ref[...] += _csum(dzv)
        dh = dzv * (1.0 + sc_ref[...])
        dg_ref[...] += _csum(dh * xh)
        if want_dx:
            dx_ref[...] = _rms_bwd(xh, r, dh * gv) + dres_ref[...]

    in_specs = [_rows(tr, d, dz_off), _rows(tr, d), _bcast(d), _bcast(d)]
    args = [dz, x2d, g, sc]
    out_specs = [_bcast(d)] * 3
    out_shape = [jax.ShapeDtypeStruct((1, d), F32)] * 3
    if want_dx:
        in_specs.append(_rows(tr, d))
        args.append(dres)
        out_specs = [_rows(tr, d)] + out_specs
        out_shape = [jax.ShapeDtypeStruct((n, d), F32)] + out_shape
    res = pl.pallas_call(
        body,
        name=name,
        grid=(n // tr,),
        in_specs=in_specs,
        out_specs=out_specs,
        out_shape=out_shape,
        compiler_params=_params(("arbitrary",)),
    )(*args)
    return res if want_dx else (None, *res)


_QA, _QB = MLA_ROPE // 4, GQA_HEAD_DIM // 4


def _kprep_fwd(pkv, kvg, kg, cos_a, ss_a, cos_b, ss_b, tr):
    n = pkv.shape[0]
    nb = GQA_KV_HEADS * GQA_HEAD_DIM

    def body(p_ref, kvg_ref, kg_ref, ca, sa, cb, sb, ckv_ref, kb_ref, vb_ref, kpe_ref):
        p = p_ref[...]
        xh, _ = _rms(p[:, :MLA_KV_LORA])
        ckv_ref[...] = (xh * kvg_ref[...]).astype(BF16)
        for e in range(GQA_KV_HEADS):
            lo = MLA_KV_LORA + e * GQA_HEAD_DIM
            kh, _ = _rms(p[:, lo : lo + GQA_HEAD_DIM])
            kb_ref[:, e * GQA_HEAD_DIM : (e + 1) * GQA_HEAD_DIM] = _rope(kh * kg_ref[...], cb[...], sb[...], _QB).astype(BF16)
        vb_ref[...] = p[:, MLA_KV_LORA + nb : MLA_KV_LORA + 2 * nb].astype(BF16)
        kr = _rope(p[:, MLA_KV_LORA + 2 * nb :], ca[...], sa[...], _QA)
        kpe_ref[:, :LANES] = kr.astype(BF16)
        kpe_ref[:, LANES:] = pltpu.roll(kr, MLA_ROPE, 1).astype(BF16)

    return pl.pallas_call(
        body,
        name="kprep_fwd",
        grid=(n // tr,),
        in_specs=[_rows(tr, KVP), _bcast(MLA_KV_LORA), _bcast(GQA_HEAD_DIM)] + [_rows(tr, LANES)] * 4,
        out_specs=[_rows(tr, MLA_KV_LORA), _rows(tr, nb), _rows(tr, nb), _rows(tr, 2 * LANES)],
        out_shape=[jax.ShapeDtypeStruct((n, w), BF16) for w in (MLA_KV_LORA, nb, nb, 2 * LANES)],
        compiler_params=_params(("parallel",)),
    )(pkv, kvg, kg, cos_a, ss_a, cos_b, ss_b)


def _kprep_bwd(pkv, dckv, dkb, dvb, dkpe, kvg, kg, cos_b, ss_b, tr):
    n = pkv.shape[0]
    nb = GQA_KV_HEADS * GQA_HEAD_DIM

    def body(p_ref, dckv_ref, dkb_ref, dvb_ref, dkpe_ref, kvg_ref, kg_ref, cb, sb, dp_ref, dkvg_ref, dkg_ref):
        _acc_init(pl.program_id(0), [dkvg_ref, dkg_ref])
        p = p_ref[...]
        xh, r = _rms(p[:, :MLA_KV_LORA])
        dn = dckv_ref[...]
        dkvg_ref[...] += _csum(dn * xh)
        dp_ref[:, :MLA_KV_LORA] = _rms_bwd(xh, r, dn * kvg_ref[...]).astype(BF16)
        for e in range(GQA_KV_HEADS):
            lo = MLA_KV_LORA + e * GQA_HEAD_DIM
            kh, rk = _rms(p[:, lo : lo + GQA_HEAD_DIM])
            dk = _rope_t(dkb_ref[:, e * GQA_HEAD_DIM : (e + 1) * GQA_HEAD_DIM], cb[...], sb[...], _QB)
            dkg_ref[...] += _csum(dk * kh)
            dp_ref[:, lo : lo + GQA_HEAD_DIM] = _rms_bwd(kh, rk, dk * kg_ref[...]).astype(BF16)
        dp_ref[:, MLA_KV_LORA + nb : MLA_KV_LORA + 2 * nb] = dvb_ref[...].astype(BF16)
        dp_ref[:, MLA_KV_LORA + 2 * nb :] = dkpe_ref[...].astype(BF16)

    return pl.pallas_call(
        body,
        name="kprep_bwd",
        grid=(n // tr,),
        in_specs=[_rows(tr, KVP), _rows(tr, MLA_KV_LORA), _rows(tr, nb), _rows(tr, nb), _rows(tr, LANES),
                  _bcast(MLA_KV_LORA), _bcast(GQA_HEAD_DIM), _rows(tr, LANES), _rows(tr, LANES)],
        out_specs=[_rows(tr, KVP), _bcast(MLA_KV_LORA), _bcast(GQA_HEAD_DIM)],
        out_shape=[jax.ShapeDtypeStruct((n, KVP), BF16), jax.ShapeDtypeStruct((1, MLA_KV_LORA), F32),
                   jax.ShapeDtypeStruct((1, GQA_HEAD_DIM), F32)],
        compiler_params=_params(("arbitrary",)),
    )(pkv, dckv, dkb, dvb, dkpe, kvg, kg, cos_b, ss_b)


def _kgrad_split(dka, dva, cos_a, ss_a, tr):
    n = dka.shape[0]
    wk = MLA_HEADS * 2 * LANES

    def body(dk_ref, dv_ref, ca, sa, dkv_ref, dkpe_ref):
        even = jnp.zeros((tr, LANES), F32)
        odd = jnp.zeros((tr, LANES), F32)
        for h in range(MLA_HEADS):
            dkv_ref[:, 2 * h * LANES : (2 * h + 1) * LANES] = dk_ref[:, 2 * h * LANES : (2 * h + 1) * LANES].astype(BF16)
            dkv_ref[:, (2 * h + 1) * LANES : (2 * h + 2) * LANES] = dv_ref[:, h * MLA_V : (h + 1) * MLA_V].astype(BF16)
            part = dk_ref[:, (2 * h + 1) * LANES : (2 * h + 2) * LANES]
            if h % 2 == 0:
                even = even + part
            else:
                odd = odd + part
        lane = lax.broadcasted_iota(jnp.int32, (tr, LANES), 1)
        low = lane < MLA_ROPE
        both = jnp.where(low, even, odd)
        tot = jnp.where(low, both + pltpu.roll(both, MLA_ROPE, 1), 0.0)
        dkpe_ref[...] = _rope_t(tot, ca[...], sa[...], _QA)

    return pl.pallas_call(
        body,
        name="kgrad_split",
        grid=(n // tr,),
        in_specs=[_rows(tr, wk), _rows(tr, MLA_HEADS * MLA_V), _rows(tr, LANES), _rows(tr, LANES)],
        out_specs=[_rows(tr, wk), _rows(tr, LANES)],
        out_shape=[jax.ShapeDtypeStruct((n, wk), BF16), jax.ShapeDtypeStruct((n, LANES), F32)],
        compiler_params=_params(("parallel",)),
    )(dka, dva, cos_a, ss_a)


def _qprep_fwd(pq, qg, gq, cos_b, ss_b, tr):
    n = pq.shape[0]
    nq = GQA_HEADS * GQA_HEAD_DIM

    def body(p_ref, qg_ref, gq_ref, cb, sb, cq_ref, qb_ref):
        xh, _ = _rms(p_ref[:, :MLA_Q_LORA])
        cq_ref[...] = (xh * qg_ref[...]).astype(BF16)
        for h in range(GQA_HEADS):
            lo = MLA_Q_LORA + h * GQA_HEAD_DIM
            qh, _ = _rms(p_ref[:, lo : lo + GQA_HEAD_DIM])
            qb_ref[:, h * GQA_HEAD_DIM : (h + 1) * GQA_HEAD_DIM] = _rope(qh * gq_ref[...], cb[...], sb[...], _QB).astype(BF16)

    return pl.pallas_call(
        body,
        name="qprep_fwd",
        grid=(n // tr,),
        in_specs=[_rows(tr, QC), _bcast(MLA_Q_LORA), _bcast(GQA_HEAD_DIM), _rows(tr, LANES), _rows(tr, LANES)],
        out_specs=[_rows(tr, MLA_Q_LORA), _rows(tr, nq)],
        out_shape=[jax.ShapeDtypeStruct((n, MLA_Q_LORA), BF16), jax.ShapeDtypeStruct((n, nq), BF16)],
        compiler_params=_params(("parallel",)),
    )(pq, qg, gq, cos_b, ss_b)


def _qprep_bwd(pq, dcq, dqb, qg, gq, cos_b, ss_b, tr):
    n = pq.shape[0]
    nq = GQA_HEADS * GQA_HEAD_DIM

    def body(p_ref, dcq_ref, dqb_ref, qg_ref, gq_ref, cb, sb, dp_ref, dqg_ref, dgq_ref):
        _acc_init(pl.program_id(0), [dqg_ref, dgq_ref])
        xh, r = _rms(p_ref[:, :MLA_Q_LORA])
        dn = dcq_ref[...]
        dqg_ref[...] += _csum(dn * xh)
        dp_ref[:, :MLA_Q_LORA] = _rms_bwd(xh, r, dn * qg_ref[...]).astype(BF16)
        for h in range(GQA_HEADS):
            lo = MLA_Q_LORA + h * GQA_HEAD_DIM
            qh, rq = _rms(p_ref[:, lo : lo + GQA_HEAD_DIM])
            dq = _rope_t(dqb_ref[:, h * GQA_HEAD_DIM : (h + 1) * GQA_HEAD_DIM], cb[...], sb[...], _QB)
            dgq_ref[...] += _csum(dq * qh)
            dp_ref[:, lo : lo + GQA_HEAD_DIM] = _rms_bwd(qh, rq, dq * gq_ref[...]).astype(BF16)

    return pl.pallas_call(
        body,
        name="qprep_bwd",
        grid=(n // tr,),
        in_specs=[_rows(tr, QC), _rows(tr, MLA_Q_LORA), _rows(tr, nq), _bcast(MLA_Q_LORA), _bcast(GQA_HEAD_DIM),
                  _rows(tr, LANES), _rows(tr, LANES)],
        out_specs=[_rows(tr, QC), _bcast(MLA_Q_LORA), _bcast(GQA_HEAD_DIM)],
        out_shape=[jax.ShapeDtypeStruct((n, QC), BF16), jax.ShapeDtypeStruct((1, MLA_Q_LORA), F32),
                   jax.ShapeDtypeStruct((1, GQA_HEAD_DIM), F32)],
        compiler_params=_params(("arbitrary",)),
    )(pq, dcq, dqb, qg, gq, cos_b, ss_b)


_QA_COLS = MLA_HEADS * (MLA_NOPE + MLA_ROPE)


def _qrope_fwd(qa, cos_a, ss_a, tr):
    n = qa.shape[0]

    def body(q_ref, ca, sa, o_ref):
        for j in range(MLA_HEADS // 2):
            lo = 3 * j * LANES
            o_ref[:, lo : lo + 2 * LANES] = q_ref[:, lo : lo + 2 * LANES].astype(BF16)
            o_ref[:, lo + 2 * LANES : lo + 3 * LANES] = _rope(q_ref[:, lo + 2 * LANES : lo + 3 * LANES], ca[...], sa[...], _QA).astype(BF16)

    return pl.pallas_call(
        body,
        name="qrope_fwd",
        grid=(n // tr,),
        in_specs=[_rows(tr, _QA_COLS), _rows(tr, LANES), _rows(tr, LANES)],
        out_specs=_rows(tr, _QA_COLS),
        out_shape=jax.ShapeDtypeStruct((n, _QA_COLS), BF16),
        compiler_params=_params(("parallel",)),
    )(qa, cos_a, ss_a)


def _qrope_bwd(dq2, cos_a, ss_a, tr):
    n = dq2.shape[0]

    def body(d_ref, ca, sa, o_ref):
        for j in range(MLA_HEADS // 2):
            lo = 3 * j * LANES
            h0, h1 = 2 * j, 2 * j + 1
            o_ref[:, lo : lo + LANES] = d_ref[:, 2 * h0 * LANES : (2 * h0 + 1) * LANES].astype(BF16)
            o_ref[:, lo + LANES : lo + 2 * LANES] = d_ref[:, 2 * h1 * LANES : (2 * h1 + 1) * LANES].astype(BF16)
            pe = d_ref[:, (2 * h0 + 1) * LANES : (2 * h0 + 2) * LANES] + d_ref[:, (2 * h1 + 1) * LANES : (2 * h1 + 2) * LANES]
            o_ref[:, lo + 2 * LANES : lo + 3 * LANES] = _rope_t(pe, ca[...], sa[...], _QA).astype(BF16)

    return pl.pallas_call(
        body,
        name="qrope_bwd",
        grid=(n // tr,),
        in_specs=[_rows(tr, MLA_HEADS * 2 * LANES), _rows(tr, LANES), _rows(tr, LANES)],
        out_specs=_rows(tr, _QA_COLS),
        out_shape=jax.ShapeDtypeStruct((n, _QA_COLS), BF16),
        compiler_params=_params(("parallel",)),
    )(dq2, cos_a, ss_a)


def _cat(refs):
    vals = [r[...] for r in refs]
    return vals[0] if len(vals) == 1 else jnp.concatenate(vals, axis=-1)


LOG2E = 1.4426950408889634


def _attn_fwd(qparts, kparts, vpart, n_heads, group, dv, scale, name, tq, after=None):
    T, Tk = qparts[0][0].shape[0], kparts[0][0].shape[0]
    nq_, nk_ = len(qparts), len(kparts)
    sub = min(tq, 256)
    c2 = scale * LOG2E

    def body(*refs):
        q_refs, k_refs = refs[:nq_], refs[nq_ : nq_ + nk_]
        v_ref = refs[nq_ + nk_]
        o_ref, lse_ref = refs[-2:]
        k = _cat(k_refs)
        v = v_ref[...]
        for r0 in range(0, tq, sub):
            q = _cat([r.at[r0 : r0 + sub, :] for r in q_refs])
            s = lax.dot_general(q, k, _DIMS["NT"], preferred_element_type=F32)
            m = jnp.max(s, axis=-1, keepdims=True)
            p = jnp.exp2((s - m) * c2)
            l = jnp.sum(p, axis=-1, keepdims=True)
            acc = jnp.dot(p.astype(BF16), v, preferred_element_type=F32)
            o_ref[r0 : r0 + sub, :] = (acc * (1.0 / l)).astype(BF16)
            lse_ref[r0 : r0 + sub, :] = m * scale + jnp.log(l)

    in_specs = [pl.BlockSpec((tq, LANES), lambda h, i, f=f: (i, f(h))) for _, f in qparts]
    in_specs += [pl.BlockSpec((Tk, LANES), lambda h, i, f=f: (0, f(h // group))) for _, f in kparts]
    fv = vpart[1]
    in_specs.append(pl.BlockSpec((Tk, dv), lambda h, i: (0, fv(h // group))))
    args = [*[a for a, _ in qparts], *[a for a, _ in kparts], vpart[0]]
    if after is not None:
        in_specs.append(pl.BlockSpec(after.shape, lambda h, i: (0, 0)))
        args.append(after)
    return pl.pallas_call(
        body,
        name=name,
        grid=(n_heads, T // tq),
        in_specs=in_specs,
        out_specs=[pl.BlockSpec((tq, dv), lambda h, i: (i, h)), pl.BlockSpec((None, tq, 1), lambda h, i: (h, i, 0))],
        out_shape=[jax.ShapeDtypeStruct((T, n_heads * dv), BF16), jax.ShapeDtypeStruct((n_heads, T, 1), F32)],
        compiler_params=_params(("parallel", "parallel")),
    )(*args)


def _attn_bwd(qparts, kparts, vpart, o, do, lse, n_heads, group, dv, scale, name, tq):
    T, Tk = qparts[0][0].shape[0], kparts[0][0].shape[0]
    nq_, nk_ = len(qparts), len(kparts)
    dk_ = LANES * nq_
    n_kv = n_heads // group
    nblk = T // tq
    c2 = scale * LOG2E

    def head(hk, i):
        return hk * group + i // nblk

    sub = min(tq, 256)

    def body(*refs):
        q_refs = refs[:nq_]
        k = _cat(refs[nq_ : nq_ + nk_])
        v_ref, o_ref, do_ref, lse_ref, dq_ref, dk_ref, dv_ref = refs[nq_ + nk_ :]
        i = pl.program_id(1)
        _acc_init(i, [dk_ref, dv_ref])
        v = v_ref[...]
        dk_acc, dv_acc = None, None
        for r0 in range(0, tq, sub):
            rows = slice(r0, r0 + sub)
            q = _cat([r.at[rows, :] for r in q_refs])
            s = lax.dot_general(q, k, _DIMS["NT"], preferred_element_type=F32)
            p = jnp.exp2(s * c2 - lse_ref[rows, :] * LOG2E)
            dov = do_ref[rows, :]
            dp = lax.dot_general(dov, v, _DIMS["NT"], preferred_element_type=F32)
            delta = jnp.sum(dov.astype(F32) * o_ref[rows, :].astype(F32), axis=-1, keepdims=True)
            ds = (p * (dp - delta)).astype(BF16)
            dq_ref[rows, :] = jnp.dot(ds, k, preferred_element_type=F32) * scale
            dk_part = lax.dot_general(ds, q, _DIMS["TN"], preferred_element_type=F32)
            dv_part = lax.dot_general(p.astype(BF16), dov, _DIMS["TN"], preferred_element_type=F32)
            dk_acc = dk_part if dk_acc is None else dk_acc + dk_part
            dv_acc = dv_part if dv_acc is None else dv_acc + dv_part
        dk_ref[...] += dk_acc
        dv_ref[...] += dv_acc

        @pl.when(i == group * nblk - 1)
        def _():
            dk_ref[...] *= scale

    in_specs = [pl.BlockSpec((tq, LANES), lambda hk, i, f=f: (i % nblk, f(head(hk, i)))) for _, f in qparts]
    in_specs += [pl.BlockSpec((Tk, LANES), lambda hk, i, f=f: (0, f(hk))) for _, f in kparts]
    fv = vpart[1]
    in_specs.append(pl.BlockSpec((Tk, dv), lambda hk, i: (0, fv(hk))))
    in_specs += [pl.BlockSpec((tq, dv), lambda hk, i: (i % nblk, head(hk, i)))] * 2
    in_specs.append(pl.BlockSpec((None, tq, 1), lambda hk, i: (head(hk, i), i % nblk, 0)))
    return pl.pallas_call(
        body,
        name=name,
        grid=(n_kv, group * nblk),
        in_specs=in_specs,
        out_specs=[pl.BlockSpec((tq, dk_), lambda hk, i: (i % nblk, head(hk, i))),
                   pl.BlockSpec((Tk, dk_), lambda hk, i: (0, hk)),
                   pl.BlockSpec((Tk, dv), lambda hk, i: (0, hk))],
        out_shape=[jax.ShapeDtypeStruct((T, n_heads * dk_), F32), jax.ShapeDtypeStruct((Tk, n_kv * dk_), F32),
                   jax.ShapeDtypeStruct((Tk, n_kv * dv), F32)],
        compiler_params=_params(("parallel", "arbitrary")),
    )(*[a for a, _ in qparts], *[a for a, _ in kparts], vpart[0], o, do, lse)


def _gates_fwd(pg, ya, yb, tr):
    n, d = ya.shape

    def body(pg_ref, ya_ref, yb_ref, o_ref):
        ga = jax.nn.sigmoid(pg_ref[:, :d].astype(F32))
        gb = jax.nn.sigmoid(pg_ref[:, d:].astype(F32))
        o_ref[...] = (ga * ya_ref[...].astype(F32) + gb * yb_ref[...].astype(F32)).astype(BF16)

    return pl.pallas_call(
        body,
        name="gates_fwd",
        grid=(n // tr,),
        in_specs=[_rows(tr, 2 * d), _rows(tr, d), _rows(tr, d)],
        out_specs=_rows(tr, d),
        out_shape=jax.ShapeDtypeStruct((n, d), BF16),
        compiler_params=_params(("parallel",)),
    )(pg, ya, yb)


def _gates_bwd(dm, pg, ya, yb, tr):
    n, d = ya.shape

    def body(dm_ref, pg_ref, ya_ref, yb_ref, dya_ref, dyb_ref, dpg_ref):
        dmv = dm_ref[...].astype(F32)
        ga = jax.nn.sigmoid(pg_ref[:, :d].astype(F32))
        gb = jax.nn.sigmoid(pg_ref[:, d:].astype(F32))
        dya_ref[...] = (dmv * ga).astype(BF16)
        dyb_ref[...] = (dmv * gb).astype(BF16)
        dpg_ref[:, :d] = (dmv * ya_ref[...].astype(F32) * ga * (1.0 - ga)).astype(BF16)
        dpg_ref[:, d:] = (dmv * yb_ref[...].astype(F32) * gb * (1.0 - gb)).astype(BF16)

    return pl.pallas_call(
        body,
        name="gates_bwd",
        grid=(n // tr,),
        in_specs=[_rows(tr, d), _rows(tr, 2 * d), _rows(tr, d), _rows(tr, d)],
        out_specs=[_rows(tr, d), _rows(tr, d), _rows(tr, 2 * d)],
        out_shape=[jax.ShapeDtypeStruct((n, d), BF16), jax.ShapeDtypeStruct((n, d), BF16), jax.ShapeDtypeStruct((n, 2 * d), BF16)],
        compiler_params=_params(("parallel",)),
    )(dm, pg, ya, yb)


def _resid_norm2_fwd(x2d, att, g1, n2g, sh2, sc2, tr):
    n, d = x2d.shape

    def body(x_ref, a_ref, g1_ref, g_ref, sh_ref, sc_ref, x1_ref, z_ref):
        x1 = x_ref[...] + g1_ref[...] * a_ref[...]
        x1_ref[...] = x1
        xh, _ = _rms(x1)
        z_ref[...] = ((xh * g_ref[...]) * (1.0 + sc_ref[...]) + sh_ref[...]).astype(BF16)

    return pl.pallas_call(
        body,
        name="resid_norm2_fwd",
        grid=(n // tr,),
        in_specs=[_rows(tr, d), _rows(tr, d)] + [_bcast(d)] * 4,
        out_specs=[_rows(tr, d), _rows(tr, d)],
        out_shape=[jax.ShapeDtypeStruct((n, d), F32), jax.ShapeDtypeStruct((n, d), BF16)],
        compiler_params=_params(("parallel",)),
    )(x2d, att, g1, n2g, sh2, sc2)


def _resid_norm2_bwd(dz2, x1, dx2, att, n2g, sc2, g1, tr):
    n, d = x1.shape

    def body(dz_ref, x1_ref, dx2_ref, a_ref, g_ref, sc_ref, g1_ref, dx1_ref, da_ref, dg_ref, dsh_ref, dsc_ref, dg1_ref):
        _acc_init(pl.program_id(0), [dg_ref, dsh_ref, dsc_ref, dg1_ref])
        xh, r = _rms(x1_ref[...])
        dzv = dz_ref[...]
        gv = g_ref[...]
        dsc_ref[...] += _csum(dzv * (xh * gv))
        dsh_ref[...] += _csum(dzv)
        dh = dzv * (1.0 + sc_ref[...])
        dg_ref[...] += _csum(dh * xh)
        dx1 = _rms_bwd(xh, r, dh * gv) + dx2_ref[...]
        dx1_ref[...] = dx1
        dg1_ref[...] += _csum(dx1 * a_ref[...])
        da_ref[...] = (dx1 * g1_ref[...]).astype(BF16)

    return pl.pallas_call(
        body,
        name="resid_norm2_bwd",
        grid=(n // tr,),
        in_specs=[_rows(tr, d)] * 4 + [_bcast(d)] * 3,
        out_specs=[_rows(tr, d), _rows(tr, d)] + [_bcast(d)] * 4,
        out_shape=[jax.ShapeDtypeStruct((n, d), F32), jax.ShapeDtypeStruct((n, d), BF16)] + [jax.ShapeDtypeStruct((1, d), F32)] * 4,
        compiler_params=_params(("arbitrary",)),
    )(dz2, x1, dx2, att, n2g, sc2, g1)


def _edges(shape):
    row = lax.broadcasted_iota(jnp.int32, shape, 0)
    return row == 0, row == shape[0] - 1


def _shifts(u, edges):
    n = u.shape[0]
    return jnp.where(edges[0], 0.0, pltpu.roll(u, 1, 0)), jnp.where(edges[1], 0.0, pltpu.roll(u, n - 1, 0))


def _conv3(u, prev, nxt, w_ref, b_ref):
    return b_ref[...] + w_ref[0:1, :] * prev + w_ref[1:2, :] * u + w_ref[2:3, :] * nxt


def _ffn_up_conv(z, wup_t, cw, cb, tc, after):
    n, d = z.shape
    f = wup_t.shape[0] // 2
    nb = f // tc

    def body(z_ref, wa_ref, wb_ref, cwa, cwb, cba, cbb, after_ref, ua_ref, ub_ref, h_ref):
        w = jnp.concatenate([wa_ref[...], wb_ref[...]], axis=0)
        u = lax.dot_general(z_ref[...], w, _DIMS["NT"], preferred_element_type=F32).astype(BF16)
        ua_ref[...] = u[:, :tc]
        ub_ref[...] = u[:, tc:]
        edges = _edges((n, tc))
        ua = u[:, :tc].astype(F32)
        ub = u[:, tc:].astype(F32)
        a = _conv3(ua, *_shifts(ua, edges), cwa, cba)
        b = _conv3(ub, *_shifts(ub, edges), cwb, cbb)
        h_ref[...] = (a * jax.nn.sigmoid(a) * b).astype(BF16)

    col = lambda rows, off: pl.BlockSpec((rows, tc), lambda i: (0, i + off))
    w_rows = lambda off: pl.BlockSpec((tc, d), lambda i: (i + off, 0))
    return pl.pallas_call(
        body,
        name="ffn_up_conv",
        grid=(nb,),
        in_specs=[pl.BlockSpec((n, d), lambda i: (0, 0)), w_rows(0), w_rows(nb), col(3, 0), col(3, nb), col(1, 0), col(1, nb),
                  pl.BlockSpec(after.shape, lambda i: (0, 0))],
        out_specs=[col(n, 0)] * 3,
        out_shape=[jax.ShapeDtypeStruct((n, f), BF16)] * 3,
        compiler_params=_params(("parallel",)),
    )(z, wup_t, wup_t, cw, cw, cb, cb, after)


def _ffn_down_dx_conv_bwd(df, wdown, u_a, u_b, cw, cb, tc, after):
    n, f = u_a.shape
    d = df.shape[1]
    nb = f // tc

    def part(uv, prev, nxt, duc, edges, w_ref, du_ref, dw_ref, db_ref):
        db_ref[...] = _csum(duc)
        dw_ref[0:1, :] = _csum(duc * prev)
        dw_ref[1:2, :] = _csum(duc * uv)
        dw_ref[2:3, :] = _csum(duc * nxt)
        d_prev, d_next = _shifts(duc, edges)
        du_ref[...] = (w_ref[0:1, :] * d_next + w_ref[1:2, :] * duc + w_ref[2:3, :] * d_prev).astype(BF16)

    def body(df_ref, wd_ref, ua_ref, ub_ref, wa_ref, wb_ref, ba_ref, bb_ref, after_ref,
             dua_ref, dub_ref, dwa_ref, dwb_ref, dba_ref, dbb_ref):
        dhv = lax.dot_general(df_ref[...], wd_ref[...], _DIMS["NT"], preferred_element_type=F32)
        dhv = dhv.astype(BF16).astype(F32)
        edges = _edges((n, tc))
        ua = ua_ref[...].astype(F32)
        ub = ub_ref[...].astype(F32)
        sa = _shifts(ua, edges)
        sb = _shifts(ub, edges)
        a = _conv3(ua, *sa, wa_ref, ba_ref)
        b = _conv3(ub, *sb, wb_ref, bb_ref)
        sg = jax.nn.sigmoid(a)
        da = dhv * b * (sg * (1.0 + a * (1.0 - sg)))
        db = dhv * (a * sg)
        part(ua, *sa, da, edges, wa_ref, dua_ref, dwa_ref, dba_ref)
        part(ub, *sb, db, edges, wb_ref, dub_ref, dwb_ref, dbb_ref)

    col = lambda rows, off: pl.BlockSpec((rows, tc), lambda i: (0, i + off))
    return pl.pallas_call(
        body,
        name="ffn_down_dx_conv_bwd",
        grid=(nb,),
        in_specs=[pl.BlockSpec((n, d), lambda i: (0, 0)), pl.BlockSpec((tc, d), lambda i: (i, 0)), col(n, 0), col(n, 0),
                  col(3, 0), col(3, nb), col(1, 0), col(1, nb), pl.BlockSpec(after.shape, lambda i: (0, 0))],
        out_specs=[col(n, 0), col(n, 0), col(3, 0), col(3, 0), col(1, 0), col(1, 0)],
        out_shape=[jax.ShapeDtypeStruct((n, f), BF16)] * 2 + [jax.ShapeDtypeStruct((3, f), F32)] * 2 + [jax.ShapeDtypeStruct((1, f), F32)] * 2,
        compiler_params=_params(("parallel",)),
    )(df, wdown, u_a, u_b, cw, cw, cb, cb, after)


def _loss_head(x1, f, g2, fg, tgt, tr):
    n, d = x1.shape

    def body(x1_ref, f_ref, g2_ref, fg_ref, t_ref, sq_ref, dx2_ref, dfg_ref, dg2_ref, df_ref):
        _acc_init(pl.program_id(0), [sq_ref, dfg_ref, dg2_ref])
        fv = f_ref[...]
        xh, r = _rms(x1_ref[...] + g2_ref[...] * fv)
        err = xh * fg_ref[...] - t_ref[...]
        sq_ref[...] += _csum(err * err)
        dy = err * (1.0 / d)
        dfg_ref[...] += _csum(dy * xh)
        dx2 = _rms_bwd(xh, r, dy * fg_ref[...])
        dx2_ref[...] = dx2
        dg2_ref[...] += _csum(dx2 * fv)
        df_ref[...] = (dx2 * g2_ref[...]).astype(BF16)

    return pl.pallas_call(
        body,
        name="loss_head",
        grid=(n // tr,),
        in_specs=[_rows(tr, d), _rows(tr, d), _bcast(d), _bcast(d), _rows(tr, d)],
        out_specs=[_bcast(d), _rows(tr, d), _bcast(d), _bcast(d), _rows(tr, d)],
        out_shape=[jax.ShapeDtypeStruct((1, d), F32), jax.ShapeDtypeStruct((n, d), F32), jax.ShapeDtypeStruct((1, d), F32),
                   jax.ShapeDtypeStruct((1, d), F32), jax.ShapeDtypeStruct((n, d), BF16)],
        compiler_params=_params(("arbitrary",)),
    )(x1, f, g2, fg, tgt)


def _sum_slots(g, name):
    s, r, w = g.shape

    def body(g_ref, o_ref):
        acc = g_ref[0]
        for k in range(1, s):
            acc = acc + g_ref[k]
        o_ref[...] = acc

    return pl.pallas_call(body, name=name, out_shape=jax.ShapeDtypeStruct((r, w), F32))(g)


def _silu_grad_mul(ds, cvec):
    def body(d_ref, c_ref, o_ref):
        cv = c_ref[...]
        sg = jax.nn.sigmoid(cv)
        o_ref[...] = d_ref[...] * (sg * (1.0 + cv * (1.0 - sg)))

    return pl.pallas_call(body, name="silu_grad_mul", out_shape=jax.ShapeDtypeStruct(ds.shape, F32))(ds, cvec)


def _adamw_update(wv, gv, mv, vv, d_ref, mo_ref, vo_ref):
    mn = ADAM_B1 * mv + (1.0 - ADAM_B1) * gv
    vn = ADAM_B2 * vv + (1.0 - ADAM_B2) * (gv * gv)
    mo_ref[...] = mn
    vo_ref[...] = vn
    m_hat = mn / (1.0 - ADAM_B1**ADAM_STEP)
    v_hat = vn / (1.0 - ADAM_B2**ADAM_STEP)
    d_ref[...] = -ADAM_LR * (m_hat / (jnp.sqrt(v_hat) + ADAM_EPS) + ADAM_WD * wv)


def _adamw_many(ws, gs, ms, vs, name):
    n = len(ws)

    def body(*refs):
        for k in range(n):
            w_ref, g_ref, m_ref, v_ref = (refs[q * n + k] for q in range(4))
            d_ref, mo_ref, vo_ref = (refs[(4 + q) * n + k] for q in range(3))
            _adamw_update(w_ref[...], g_ref[...], m_ref[...], v_ref[...], d_ref, mo_ref, vo_ref)

    res = pl.pallas_call(body, name=name, out_shape=[jax.ShapeDtypeStruct(w.shape, F32) for w in ws] * 3)(*ws, *gs, *ms, *vs)
    return res[:n], res[n : 2 * n], res[2 * n :]


def _adamw(w, g, m, v, name, g_transposed=False, g_sibling=None, after=None):
    r, cdim = w.shape
    halves = g_sibling is not None
    block = 1 << 19
    if g_transposed:
        tc = _pick(cdim // 2 if halves else cdim, 2048)
        tr = _pick(r, max(LANES, block // tc), LANES)
        per_half = (cdim // 2) // tc
    else:
        rows = r // 2 if halves else r
        tc = _pick(cdim, 2048)
        tr = _pick(rows, max(8, block // tc), 8)
        if tr < 64 and rows > 64:
            tr, tc = _pick(rows, 1024, 8), _pick(cdim, 512)
        per_half = (r // 2) // tr
    emit_g = g_transposed or halves

    def body(w_ref, g_ref, *rest):
        m_ref, v_ref = rest[halves : halves + 2]
        outs = rest[halves + 2 + (after is not None) :]
        gv = g_ref[...]
        if halves:
            along = pl.program_id(1 if g_transposed else 0)
            gv = jnp.where(along // per_half == lax.axis_index("c"), gv, rest[0][...])
        if g_transposed:
            gv = gv.T
        if emit_g:
            outs[0][...] = gv
        _adamw_update(w_ref[...], gv, m_ref[...], v_ref[...], *outs[-3:])

    spec = pl.BlockSpec((tr, tc), lambda i, j: (i, j))
    if g_transposed:
        g_spec = pl.BlockSpec((tc, tr), lambda i, j: (j % per_half if halves else j, i))
    else:
        g_spec = pl.BlockSpec((tr, tc), lambda i, j: (i % per_half if halves else i, j))
    n_out = 3 + emit_g
    token_spec = [] if after is None else [pl.BlockSpec(after.shape, lambda i, j: (0, 0))]
    res = pl.pallas_call(
        body,
        name=name,
        grid=(r // tr, cdim // tc),
        in_specs=[spec, g_spec] + [g_spec] * halves + [spec, spec] + token_spec,
        out_specs=[spec] * n_out,
        out_shape=[jax.ShapeDtypeStruct((r, cdim), F32)] * n_out,
        compiler_params=_params(("parallel", "parallel")),
    )(w, g, *([g_sibling] if halves else []), m, v, *([] if after is None else [after]))
    return res if emit_g else [g, *res]


def _place():
    return lax.axis_index("x"), lax.axis_index("y"), lax.axis_index("c")


def _remote(src, dst, send_sem, recv_sem, dev):
    return pltpu.make_async_remote_copy(src_ref=src, dst_ref=dst, send_sem=send_sem, recv_sem=recv_sem, device_id=dev, device_id_type=MESH)


ANY = pl.BlockSpec(memory_space=pl.ANY)


def _all_gather_small(v, name, after=()):
    r, w = v.shape

    def body(v_ref, *rest):
        o_ref, send, recv, lsem = rest[len(after) :]
        x, y, c = _place()
        me = 4 * x + 2 * y + c
        mine = pltpu.make_async_copy(v_ref, o_ref.at[me], lsem)
        mine.start()
        sent = []
        for k in range(1, 8):
            px, py, pc = x ^ (k >> 2), y ^ ((k >> 1) & 1), c ^ (k & 1)
            cp = _remote(v_ref, o_ref.at[me], send.at[k - 1], recv.at[k - 1], (px, py, pc))
            cp.start()
            sent.append(cp)
        for k in range(1, 8):
            px, py, pc = x ^ (k >> 2), y ^ ((k >> 1) & 1), c ^ (k & 1)
            slot = o_ref.at[4 * px + 2 * py + pc]
            _remote(slot, slot, send.at[k - 1], recv.at[k - 1], (x, y, c)).wait_recv()
        for cp in sent:
            cp.wait_send()
        mine.wait()

    return pl.pallas_call(
        body,
        name=name,
        out_shape=jax.ShapeDtypeStruct((8, r, w), F32),
        in_specs=[pl.BlockSpec(memory_space=pltpu.VMEM)] + [ANY] * len(after),
        out_specs=pl.BlockSpec(memory_space=pltpu.VMEM),
        scratch_shapes=[pltpu.SemaphoreType.DMA((7,)), pltpu.SemaphoreType.DMA((7,)), pltpu.SemaphoreType.DMA],
        compiler_params=pltpu.CompilerParams(vmem_limit_bytes=VMEM_LIMIT),
    )(v, *after)


HBM = pl.BlockSpec(memory_space=pltpu.HBM)
SEM = pl.BlockSpec(memory_space=pltpu.SEMAPHORE)
EFFECT = pltpu.SideEffectType.DATAFLOW_SIDE_EFFECTING


def _other_chips(x, y):
    return [(1 - x, y), (x, 1 - y), (1 - x, 1 - y)]


def _bulk_start(name, srcs, land_shapes, n_copies, copies, after, lands_init=None):
    n, m = len(srcs), len(land_shapes)

    def body(*refs):
        src_refs, land_refs = refs[:n], refs[n : n + m]
        send, recv = refs[n + m + 1], refs[n + m + 2]
        token = refs[-1]
        for k, (s, d, dev) in enumerate(copies(src_refs, land_refs)):
            _remote(s, d, send.at[k], recv.at[k], dev).start()
        token[...] = jnp.zeros_like(token)

    lands = lands_init or [lax.empty(s.shape, s.dtype) for s in land_shapes]
    lands = [pltpu.with_memory_space_constraint(b, pltpu.HBM) for b in lands]
    out = pl.pallas_call(
        body,
        name=name,
        out_shape=(pltpu.SemaphoreType.DMA((n_copies,)), pltpu.SemaphoreType.DMA((n_copies,)),
                   *[pltpu.HBM(s.shape, s.dtype) for s in srcs], *[pltpu.HBM(s.shape, s.dtype) for s in land_shapes],
                   jax.ShapeDtypeStruct((8, LANES), F32)),
        in_specs=[HBM] * (n + m) + [ANY],
        out_specs=(SEM, SEM, *[HBM] * (n + m), pl.BlockSpec(memory_space=pltpu.VMEM)),
        input_output_aliases={i: 2 + i for i in range(n + m)},
        compiler_params=pltpu.CompilerParams(has_side_effects=EFFECT),
    )(*[pltpu.with_memory_space_constraint(s, pltpu.HBM) for s in srcs], *lands, after)
    return out[0], out[1], list(out[2 : 2 + n]), list(out[2 + n : 2 + n + m]), out[-1][0:1, 0:1]


def _bulk_wait(name, send, recv, srcs, lands, after, waits):
    n, m = len(srcs), len(lands)

    def body(*refs):
        src_refs, land_refs = refs[:n], refs[n : n + m]
        send_sem, recv_sem = refs[n + m], refs[n + m + 1]
        x, y, c = _place()
        for k, (s, d) in enumerate(waits(src_refs, land_refs)):
            cp = _remote(s, d, send_sem.at[k], recv_sem.at[k], (x, y, c))
            cp.wait_send()
            cp.wait_recv()

    out = pl.pallas_call(
        body,
        name=name,
        out_shape=tuple(pltpu.HBM(s.shape, s.dtype) for s in (*srcs, *lands)),
        in_specs=[HBM] * (n + m) + [SEM, SEM, ANY],
        out_specs=tuple([HBM] * (n + m)),
        input_output_aliases={i: i for i in range(n + m)},
        compiler_params=pltpu.CompilerParams(has_side_effects=EFFECT),
    )(*srcs, *lands, send, recv, after)
    return list(out[:n]), list(out[n:])


def _peers(x, y, c):
    return [(x ^ (k >> 2), y ^ ((k >> 1) & 1), c ^ (k & 1)) for k in range(1, 8)]


def _small_gather_start(v, after, name):
    r, w = v.shape

    def copies(src, land):
        x, y, c = _place()
        return [(src[0], land[0].at[4 * x + 2 * y + c], peer) for peer in _peers(x, y, c)]

    me = 4 * lax.axis_index("x") + 2 * lax.axis_index("y") + lax.axis_index("c")
    init = [lax.dynamic_update_slice(lax.empty((8, r, w), F32), v[None], (me, 0, 0))]
    return _bulk_start(name, [v], [jax.ShapeDtypeStruct((8, r, w), F32)], 7, copies, after, init)


def _small_gather_wait(started, after, name):
    send, recv, srcs, lands, _ = started

    def waits(src, land):
        x, y, c = _place()
        return [(src[0], land[0].at[4 * px + 2 * py + pc]) for px, py, pc in _peers(x, y, c)]

    return _bulk_wait(name, send, recv, srcs, lands, after, waits)[1][0]


def _gather_start(shards, after, name):
    def copies(src, land):
        x, y, c = _place()
        j = 2 * x + y
        return [(src[a].at[c], land[a].at[j, c], (px, py, c)) for a in range(len(shards)) for px, py in _other_chips(x, y)]

    shapes = [jax.ShapeDtypeStruct((4,) + s.shape, s.dtype) for s in shards]
    j = 2 * lax.axis_index("x") + lax.axis_index("y")
    init = [lax.dynamic_update_slice(lax.empty(t.shape, t.dtype), s[None], (j, 0, 0, 0)) for t, s in zip(shapes, shards)]
    return _bulk_start(name, shards, shapes, 3 * len(shards), copies, after, init)


def _gather_wait(started, after, name):
    send, recv, srcs, lands, _ = started

    def waits(src, land):
        x, y, c = _place()
        return [(src[a].at[c], land[a].at[2 * px + py, c]) for a in range(len(srcs)) for px, py in _other_chips(x, y)]

    return _bulk_wait(name, send, recv, srcs, lands, after, waits)


def _forward_start(lands, after, name):
    def copies(src, _):
        x, y, c = _place()
        blocks = [src[a].at[2 * px + py, c] for a in range(len(lands)) for px, py in _other_chips(x, y)]
        return [(b, b, (x, y, 1 - c)) for b in blocks]

    return _bulk_start(name, lands, [], 3 * len(lands), copies, after)


def _forward_wait(started, after, name):
    send, recv, bufs, _, _ = started

    def waits(src, _):
        x, y, c = _place()
        return [(src[a].at[2 * px + py, c], src[a].at[2 * px + py, 1 - c]) for a in range(len(bufs)) for px, py in _other_chips(x, y)]

    return _bulk_wait(name, send, recv, bufs, [], after, waits)[0]


def _as_rows(lands):
    return [f.reshape(4 * f.shape[2] * 2, f.shape[3]) for f in lands]


def _gather_land(started, after, tag):
    shards, lands = _gather_wait(started, after, "gather_wait_" + tag)
    return shards, _forward_start(lands, shards[0], "forward_start_" + tag)


def _gather_done(landed, after, tag):
    _, fwd = landed
    return _as_rows(_forward_wait(fwd, after, "forward_wait_" + tag))


def _swap_halves(grads, name):
    n = len(grads)

    def body(*refs):
        ins, outs = refs[:n], refs[n : 2 * n]
        send, recv = refs[2 * n :]
        x, y, c = _place()
        started = []
        for a in range(n):
            for s in range(4):
                cp = _remote(ins[a].at[s, 1 - c], outs[a].at[s], send.at[4 * a + s], recv.at[4 * a + s], (x, y, 1 - c))
                cp.start()
                started.append(cp)
        for cp in started:
            cp.wait_recv()
        for cp in started:
            cp.wait_send()

    return pl.pallas_call(
        body,
        name=name,
        out_shape=[jax.ShapeDtypeStruct((4,) + g.shape[2:], g.dtype) for g in grads],
        in_specs=[ANY] * n,
        out_specs=[ANY] * n,
        scratch_shapes=[pltpu.SemaphoreType.DMA((4 * n,)), pltpu.SemaphoreType.DMA((4 * n,))],
    )(*grads)


def _add_halves(grads, others, tag):
    outs = []
    for a, (g, o) in enumerate(zip(grads, others)):
        _, _, rh, cdim = g.shape
        tr = _pick(rh, 512, 16)

        def body(g_ref, o_ref, p_ref):
            p_ref[...] = (g_ref[...].astype(F32) + o_ref[...].astype(F32)).astype(BF16)

        outs.append(
            pl.pallas_call(
                body,
                name=f"add_halves_{tag}{a}",
                grid=(4, rh // tr),
                in_specs=[pl.BlockSpec((None, None, tr, cdim), lambda s, i: (s, lax.axis_index("c"), i, 0)),
                          pl.BlockSpec((None, tr, cdim), lambda s, i: (s, i, 0))],
                out_specs=pl.BlockSpec((None, tr, cdim), lambda s, i: (s, i, 0)),
                out_shape=jax.ShapeDtypeStruct((4, rh, cdim), BF16),
                compiler_params=_params(("parallel", "parallel")),
            )(g, o)
        )
    return outs


def _exchange_start(parts, after, name):
    def copies(src, land):
        x, y, c = _place()
        j = 2 * x + y
        return [(src[a].at[2 * px + py], land[a].at[j], (px, py, c)) for a in range(len(parts)) for px, py in _other_chips(x, y)]

    return _bulk_start(name, parts, [jax.ShapeDtypeStruct(p.shape, p.dtype) for p in parts], 3 * len(parts), copies, after)


def _exchange_finish(started, after, name):
    send, recv, srcs, lands, _ = started

    def waits(src, land):
        x, y, _ = _place()
        return [(src[a].at[2 * px + py], land[a].at[2 * px + py]) for a in range(len(srcs)) for px, py in _other_chips(x, y)]

    srcs, lands = _bulk_wait(name, send, recv, srcs, lands, after, waits)
    return lands, srcs


def _sum_chips(recvd, parts, tag):
    outs = []
    for a, (g, p) in enumerate(zip(recvd, parts)):
        _, rh, cdim = g.shape
        tr = _pick(rh, 512, 16)

        def body(g_ref, p_ref, o_ref):
            j = 2 * lax.axis_index("x") + lax.axis_index("y")
            own = p_ref[...].astype(F32)
            term = [jnp.where(j == s, own, g_ref[s].astype(F32)) for s in range(4)]
            o_ref[...] = ((term[0] + term[1]) + term[2]) + term[3]

        outs.append(
            pl.pallas_call(
                body,
                name=f"sum_chips_{tag}{a}",
                grid=(rh // tr,),
                in_specs=[pl.BlockSpec((4, tr, cdim), lambda i: (0, i, 0)),
                          pl.BlockSpec((None, tr, cdim), lambda i: (2 * lax.axis_index("x") + lax.axis_index("y"), i, 0))],
                out_specs=pl.BlockSpec((tr, cdim), lambda i: (i, 0)),
                out_shape=jax.ShapeDtypeStruct((rh, cdim), F32),
                compiler_params=_params(("parallel",)),
            )(g, p)
        )
    return outs


def _joined(mine, other):
    first = lax.axis_index("c") == 0
    return jnp.concatenate([jnp.where(first, mine, other), jnp.where(first, other, mine)], axis=0)


def _grad_views(grads):
    return [g.reshape(4, 2, g.shape[0] // 8, g.shape[1]) for g in grads]


def _scatter_start(grads, tag, after=None):
    views = _grad_views(grads)
    others = _swap_halves(views, "swap_halves_" + tag)
    mine = _add_halves(views, others, tag)
    return _exchange_start(mine, others[-1] if after is None else after, "exchange_start_" + tag)


def _swap_start(grads, after, tag):
    views = _grad_views(grads)

    def copies(src, land):
        x, y, c = _place()
        return [(src[a].at[s, 1 - c], land[a].at[s], (x, y, 1 - c)) for a in range(len(views)) for s in range(4)]

    shapes = [jax.ShapeDtypeStruct((4,) + v.shape[2:], v.dtype) for v in views]
    return _bulk_start("swap_start_" + tag, views, shapes, 4 * len(views), copies, after)


def _scatter_start_after_swap(swapped, after, tag):
    send, recv, views, lands, _ = swapped

    def waits(src, land):
        c = lax.axis_index("c")
        return [(src[a].at[s, 1 - c], land[a].at[s]) for a in range(len(views)) for s in range(4)]

    views, others = _bulk_wait("swap_wait_" + tag, send, recv, views, lands, after, waits)
    mine = _add_halves(views, others, tag)
    return _exchange_start(mine, others[-1], "exchange_start_" + tag)


def _join_start(halves, after, tag):
    def copies(src, land):
        x, y, c = _place()
        return [(src[a], land[a], (x, y, 1 - c)) for a in range(len(halves))]

    return _bulk_start("join_start_" + tag, halves, [jax.ShapeDtypeStruct(h.shape, h.dtype) for h in halves], len(halves), copies, after)


def _join_wait(started, after, tag):
    send, recv, halves, lands, _ = started
    halves, others = _bulk_wait("join_wait_" + tag, send, recv, halves, lands, after, lambda src, land: list(zip(src, land)))
    return list(zip(halves, others))


def _scatter_sums(started, after, tag):
    return _sum_chips(*_exchange_finish(started, after, "exchange_wait_" + tag), tag)


def _t_bf16(w):
    return w.T.astype(BF16)


def kernel(x, c, ctx, c_ctx, w_ada, b_ada, norm1_g, w_in, mla_q_norm_g, w_q_up, mla_kv_norm_g, w_kv_up, gqa_q_norm_g, gqa_k_norm_g, w_br_a, w_br_b, w_out, norm2_g, w_up, conv_w, conv_b, w_down, final_norm_g, loss_target, m_c_ctx, m_w_ada, m_b_ada, m_norm1_g, m_w_in, m_mla_q_norm_g, m_w_q_up, m_mla_kv_norm_g, m_w_kv_up, m_gqa_q_norm_g, m_gqa_k_norm_g, m_w_br_a, m_w_br_b, m_w_out, m_norm2_g, m_w_up, m_conv_w, m_conv_b, m_w_down, m_final_norm_g, v_c_ctx, v_w_ada, v_b_ada, v_norm1_g, v_w_in, v_mla_q_norm_g, v_w_q_up, v_mla_kv_norm_g, v_w_kv_up, v_gqa_q_norm_g, v_gqa_k_norm_g, v_w_br_a, v_w_br_b, v_w_out, v_norm2_g, v_w_up, v_conv_w, v_conv_b, v_w_down, v_final_norm_g):
    T, D = x.shape[1], x.shape[2]
    C = ctx.shape[1]
    NA = w_ada.shape[2]
    NW = w_up.shape[2]
    F2 = 4 * NW
    FF = F2 // 2
    xi, yi, ci = _place()
    j = 2 * xi + yi
    me = 4 * xi + 2 * yi + ci
    tr = _pick(C, 256, 8)

    x2d, tgt, ctx2d = x[0], loss_target[0], ctx[0]
    fg = final_norm_g.reshape(1, D)
    cc = c_ctx.reshape(1, D)

    halve = lambda s: s.reshape(2, s.shape[0] // 2, s.shape[1])
    win_shard = halve(_t_bf16(w_in[0]))
    w0 = max(D, NW)
    pay = jnp.zeros((8, w0), F32).at[0:1, :D].set(c).at[1:4, :NW].set(conv_w[0])
    got = _all_gather_small(pay, "gather_cond")
    ag_in = _gather_start([win_shard], got, "gather_start_in")
    t_in = ag_in[4]
    c_all = got[:, 0, :D]
    cw = jnp.concatenate([got[2 * s, 1:4, :NW] for s in range(4)], axis=1)
    s16 = jnp.concatenate([c_all, cc, jnp.zeros((7, D), F32)], axis=0) + t_in
    b_cols = lax.dynamic_slice(b_ada, (0, j * NA), (1, NA))
    ada_part = _mm(s16, w_ada[0], "NN", F32, "ada_fwd", act="silu", bias=b_cols)

    wq3 = (w_q_up[0] + t_in).reshape(MLA_Q_LORA, 2, MLA_NOPE + MLA_ROPE)
    wq_perm = jnp.concatenate([wq3[:, :, :MLA_NOPE].reshape(MLA_Q_LORA, -1), wq3[:, :, MLA_NOPE:].reshape(MLA_Q_LORA, -1)], axis=1)
    low = [halve(_t_bf16(wq_perm)), halve(_t_bf16(w_kv_up[0] + t_in))]
    br = [halve(_t_bf16(w_br_a[0] + t_in)), halve(_t_bf16(w_br_b[0] + t_in)), halve((w_out[0] + t_in).astype(BF16))]
    up = [halve(_t_bf16(w_up[0] + t_in))]
    down = [halve((w_down[0] + t_in).astype(BF16))]

    got = _all_gather_small(ada_part, "gather_ada", after=(*low, *br, *up, *down))
    ada = jnp.concatenate([got[2 * s] for s in range(4)], axis=1)
    lat = lax.dynamic_slice(ada, (me, 0), (1, 6 * D))
    sh1, sc1, g1, sh2, sc2, g2 = [lat[:, k * D : (k + 1) * D] for k in range(6)]
    csh, csc = ada[8:9, :D], ada[8:9, D : 2 * D]
    ag_low = _gather_start(low, got, "gather_start_low")
    ag_br = _gather_start(br, ag_low[4], "gather_start_br")
    ag_up = _gather_start(up, ag_br[4], "gather_start_up")
    ag_down = _gather_start(down, ag_up[4], "gather_start_down")
    sh1 = sh1 + ag_down[4]

    cos_a, ss_a = _rope_tables(C, T, MLA_ROPE)
    cos_b, ss_b = _rope_tables(C, T, GQA_HEAD_DIM)
    lcos_a, lss_a, lcos_b, lss_b = cos_a[:T], ss_a[:T], cos_b[:T], ss_b[:T]

    in_landed = _gather_land(ag_in, down[0], "in")
    z_all = _norm_mod_fwd(x2d, norm1_g, sh1 + in_landed[1][4], sc1, "norm1_lat_fwd", tr, out_rows=T + C)
    z_all = _norm_mod_fwd(ctx2d, norm1_g, csh, csc, "norm1_ctx_fwd", tr, base=z_all, out_off=T)
    (win_t,) = _gather_done(in_landed, z_all, "in")
    kv_cols = KVP - LANES + MLA_ROPE
    e_kpe = MLA_KV_LORA + MLA_ROPE
    w_kvp = jnp.concatenate([win_t[:MLA_KV_LORA], win_t[e_kpe:kv_cols], win_t[MLA_KV_LORA:e_kpe], jnp.zeros((LANES - MLA_ROPE, D), BF16)], axis=0)

    pkv = _mm(z_all, w_kvp, "NT", F32, "proj_kv", tn=KVP)
    pq = _mm(z_all, win_t, "NT", F32, "proj_q", m=T, n=QC, b_off=kv_cols)
    low_landed = _gather_land(ag_low, pq, "low")
    pg = _mm(z_all, win_t, "NT", BF16, "proj_g", m=T, n=2 * D, b_off=kv_cols + QC, after=low_landed[1][4])
    wq_t, wkv_t = _gather_done(low_landed, pg, "low")
    ckv_n, kb2, vb2, kpe2 = _kprep_fwd(pkv, mla_kv_norm_g, gqa_k_norm_g, cos_a, ss_a, cos_b, ss_b, tr)
    kv_up = _mm(ckv_n, wkv_t, "NT", BF16, "kv_up")
    cq_n, qb2 = _qprep_fwd(pq, mla_q_norm_g, gqa_q_norm_g, lcos_b, lss_b, tr)
    q_a = _mm(cq_n, wq_t, "NT", F32, "q_up")
    qar = _qrope_fwd(q_a, lcos_a, lss_a, tr)

    a_q = [(qar, lambda h: 3 * (h // 2) + h % 2), (qar, lambda h: 3 * (h // 2) + 2)]
    a_k = [(kv_up, lambda h: 2 * h), (kpe2, lambda h: h % 2)]
    a_v = (kv_up, lambda h: 2 * h + 1)
    a_scale = float(MLA_NOPE + MLA_ROPE) ** -0.5
    b_q = [(qb2, lambda h: h)]
    b_k = [(kb2, lambda h: h)]
    b_v = (vb2, lambda h: h)
    b_scale = float(GQA_HEAD_DIM) ** -0.5
    tq_f = _pick(T, 2048)
    o_a, lse_a = _attn_fwd(a_q, a_k, a_v, MLA_HEADS, 1, MLA_V, a_scale, "attn_a_fwd", tq_f)
    br_landed = _gather_land(ag_br, o_a, "br")
    o_b, lse_b = _attn_fwd(b_q, b_k, b_v, GQA_HEADS, GQA_GROUP, GQA_HEAD_DIM, b_scale, "attn_b_fwd", tq_f, after=br_landed[1][4])
    wbra_t, wbrb_t, wout = _gather_done(br_landed, o_b, "br")
    up_landed = _gather_land(ag_up, o_b, "up")
    ya = _mm(o_a, wbra_t, "NT", BF16, "br_a", after=up_landed[1][4])
    yb = _mm(o_b, wbrb_t, "NT", BF16, "br_b")
    merged = _gates_fwd(pg, ya, yb, tr)
    att = _mm(merged, wout, "NN", F32, "out_proj")
    x1, z2 = _resid_norm2_fwd(x2d, att, g1, norm2_g, sh2, sc2, tr)
    (wup_t,) = _gather_done(up_landed, z2, "up")
    down_landed = _gather_land(ag_down, z2, "down")
    tc = _pick(FF, 128)
    u_a, u_b, hg = _ffn_up_conv(z2, wup_t, cw, conv_b, tc, down_landed[1][4])
    (wdown,) = _gather_done(down_landed, hg, "down")
    f = _mm(hg, wdown, "NN", F32, "ffn_down", tk=FF // 2)
    sq, dx2, d_fg, d_g2, df = _loss_head(x1, f, g2, fg, tgt, tr)
    loss_part = (0.5 * jnp.sum(sq) / D).reshape(1, 1)

    du_a, du_b, dcw_a, dcw_b, dcb_a, dcb_b = _ffn_down_dx_conv_bwd(df, wdown, u_a, u_b, cw, conv_b, _pick(FF, 256), loss_part)
    g_wdown = _mm(hg, df, "TN", BF16, "ffn_down_dw", tm=FF // 4)
    dz2 = _mm(du_a, wup_t, "NN", F32, "ffn_up_dx_a", tk=FF // 2)
    dz2 = _mm(du_b, wup_t, "NN", F32, "ffn_up_dx_b", b_off=FF, add=dz2, tk=FF // 2)
    g_wup_t = _mm(du_a, z2, "TN", BF16, "ffn_up_dw_a", out_rows=F2, tm=FF // 4)
    g_wup_t = _mm(du_b, z2, "TN", BF16, "ffn_up_dw_b", out_base=g_wup_t, out_off=FF, tm=FF // 4)
    sw_ffn = _swap_start([g_wdown, g_wup_t], sc2, "ffn")
    sc2 = sc2 + sw_ffn[4]
    dx1, datt, d_n2g, d_sh2, d_sc2, d_g1 = _resid_norm2_bwd(dz2, x1, dx2, att, norm2_g, sc2, g1, tr)

    dmerged = _mm(datt, wout, "NT", BF16, "out_proj_dx")
    rs_ffn = _scatter_start_after_swap(sw_ffn, dmerged, "ffn")
    lse_a = lse_a + rs_ffn[4]
    g_wout = _mm(merged, datt, "TN", BF16, "out_proj_dw")
    dya, dyb, dpg = _gates_bwd(dmerged, pg, ya, yb, tr)
    do_a = _mm(dya, wbra_t, "NN", BF16, "br_a_dx")
    g_wbra_t = _mm(dya, o_a, "TN", BF16, "br_a_dw")
    do_b = _mm(dyb, wbrb_t, "NN", BF16, "br_b_dx")
    g_wbrb_t = _mm(dyb, o_b, "TN", BF16, "br_b_dw")
    dqa2, dka2, dva2 = _attn_bwd(a_q, a_k, a_v, o_a, do_a, lse_a, MLA_HEADS, 1, MLA_V, a_scale, "attn_a_bwd", tq_f)
    dqb2, dkb2, dvb2 = _attn_bwd(b_q, b_k, b_v, o_b, do_b, lse_b, GQA_HEADS, GQA_GROUP, GQA_HEAD_DIM, b_scale, "attn_b_bwd", tq_f)
    dq_a = _qrope_bwd(dqa2, lcos_a, lss_a, tr)
    dcq_n = _mm(dq_a, wq_t, "NN", F32, "q_up_dx")
    g_wq_t = _mm(dq_a, cq_n, "TN", BF16, "q_up_dw")
    dpq, d_qg, d_gq = _qprep_bwd(pq, dcq_n, dqb2, mla_q_norm_g, gqa_q_norm_g, lcos_b, lss_b, tr)
    dkv_up, dkpe = _kgrad_split(dka2, dva2, cos_a, ss_a, tr)
    dckv_n = _mm(dkv_up, wkv_t, "NN", F32, "kv_up_dx")
    g_wkv_t = _mm(dkv_up, ckv_n, "TN", BF16, "kv_up_dw")
    sw_mix = _swap_start([g_wq_t, g_wkv_t, g_wbra_t, g_wbrb_t, g_wout], mla_kv_norm_g, "mix")
    dpkv, d_kvg, d_kg = _kprep_bwd(pkv, dckv_n, dkb2, dvb2, dkpe, mla_kv_norm_g + sw_mix[4], gqa_k_norm_g, cos_b, ss_b, tr)
    dz_kv = _mm(dpkv, w_kvp, "NN", F32, "proj_kv_dx")
    rs_mix = _scatter_start_after_swap(sw_mix, dz_kv, "mix")
    dz_lat = _mm(dpq, win_t, "NN", F32, "proj_q_dx", b_off=kv_cols, add=dz_kv, after=rs_mix[4])
    dz_lat = _mm(dpg, win_t, "NN", F32, "proj_g_dx", b_off=kv_cols + QC, add=dz_lat)
    _, d_n1g_c, d_csh, d_csc = _norm_mod_bwd(dz_kv, T // tr, ctx2d, norm1_g, csc, None, "norm1_ctx_bwd", tr)
    grad_x, d_n1g_l, d_sh1, d_sc1 = _norm_mod_bwd(dz_lat, 0, x2d, norm1_g, sc1, dx1, "norm1_lat_bwd", tr)

    zeros_d = jnp.zeros((1, D), F32)
    d_lat = jnp.concatenate([d_sh1, d_sc1, d_g1, d_sh2, d_sc2, d_g2], axis=1)
    d_ctx_part = jnp.concatenate([d_csh, d_csc], axis=1)
    flat = jnp.concatenate(
        [d_n1g_c + d_n1g_l, d_qg, d_kvg, d_gq, d_kg, d_n2g, dcb_a, dcb_b, d_fg,
         dcw_a.reshape(1, -1), dcw_b.reshape(1, -1), d_ctx_part, d_lat, loss_part], axis=1)
    n_flat = flat.shape[1]
    n_rows = -(-n_flat // (8 * LANES)) * 8
    flat = jnp.pad(flat, ((0, 0), (0, n_rows * LANES - n_flat))).reshape(n_rows, LANES)
    small = _small_gather_start(flat, grad_x, "small_grads_start")

    g_kvp = _mm(dpkv, z_all, "TN", BF16, "proj_kv_dw", after=small[4], tm=KVP)
    nk = MLA_KV_LORA + 2 * GQA_KV_HEADS * GQA_HEAD_DIM
    g_kv = jnp.concatenate([g_kvp[:MLA_KV_LORA], g_kvp[nk : nk + MLA_ROPE], g_kvp[MLA_KV_LORA:nk]], axis=0)
    g_win_t = _mm(dpq, z_all, "TN", BF16, "proj_q_dw", out_rows=kv_cols + QC + 2 * D, out_off=kv_cols, tm=QC // 2)
    g_win_t = _mm(dpg, z_all, "TN", BF16, "proj_g_dw", out_base=g_win_t, out_off=kv_cols + QC)
    g_win_t = lax.dynamic_update_slice(g_win_t, g_kv, (0, 0))

    got = _small_gather_wait(small, g_win_t, "small_grads_wait")
    tot = _sum_slots(got, "sum_small_grads").reshape(1, -1)
    sizes = [D, MLA_Q_LORA, MLA_KV_LORA, GQA_HEAD_DIM, GQA_HEAD_DIM, D, F2, D, 3 * FF, 3 * FF, 2 * D]
    offs = [0]
    for s in sizes:
        offs.append(offs[-1] + s)
    t_n1g, t_qg, t_kvg, t_gq, t_kg, t_n2g, t_cb, t_fg, t_cwa, t_cwb, t_ctx = [tot[:, offs[k] : offs[k + 1]] for k in range(len(sizes))]
    loss = tot[0, offs[-1] + 6 * D]
    g_cw_full = jnp.concatenate([t_cwa.reshape(3, FF), t_cwb.reshape(3, FF)], axis=1)
    g_cw = lax.dynamic_slice(g_cw_full, (0, j * NW), (3, NW))
    d_lat_all = got.reshape(8, -1)[:, offs[-1] : offs[-1] + 6 * D]
    g16 = jnp.concatenate([d_lat_all, jnp.pad(t_ctx, ((0, 0), (0, 4 * D))), jnp.zeros((7, 6 * D), F32)], axis=0)
    g_b_ada = _sum_slots(g16.reshape(16, 1, 6 * D), "sum_b_ada")
    g16_cols = lax.dynamic_slice(g16, (0, j * NA), (16, NA))
    ds_part = _mm(g16_cols, w_ada[0], "NT", F32, "ada_dx")
    ada_dx = _small_gather_start(ds_part[8:16], got, "ada_dx_start")
    sw_in = _swap_start([g_win_t], ada_dx[4], "in")

    h_ffn = _scatter_sums(rs_ffn, sw_in[2][0], "ffn")
    got = _small_gather_wait(ada_dx, h_ffn[0], "ada_dx_wait")
    ds_ctx = _sum_slots(jnp.stack([got[2 * s] for s in range(4)]), "sum_ada_dx")[0:1]
    g_c_ctx = _silu_grad_mul(ds_ctx, cc)
    j_ffn = _join_start(h_ffn, grad_x, "ffn")
    h_mix = _scatter_sums(rs_mix, j_ffn[2][0], "mix")
    j_mix = _join_start(h_mix, j_ffn[2][0], "mix")
    rs_in = _scatter_start_after_swap(sw_in, j_mix[2][0], "in")
    g_w_ada = _mm(s16, g16_cols, "TN", F32, "ada_dw", act="silu", after=rs_in[4])
    _, d_ada, m_ada, v_ada = _adamw(w_ada[0], g_w_ada, m_w_ada[0], v_w_ada[0], "adamw_w_ada")
    r_wdown, r_wup = _join_wait(j_ffn, d_ada, "ffn")
    r_wq, r_wkv, r_wbra, r_wbrb, r_wout = _join_wait(j_mix, d_ada, "mix")
    gq_p = _joined(*r_wq).T
    gq = jnp.concatenate([gq_p[:, : 2 * MLA_NOPE].reshape(MLA_Q_LORA, 2, MLA_NOPE), gq_p[:, 2 * MLA_NOPE :].reshape(MLA_Q_LORA, 2, MLA_ROPE)], axis=2)
    grads = {
        "c_ctx": g_c_ctx.reshape(D), "w_ada": g_w_ada[None], "b_ada": g_b_ada, "norm1_g": t_n1g,
        "mla_q_norm_g": t_qg, "w_q_up": gq.reshape(1, MLA_Q_LORA, -1), "mla_kv_norm_g": t_kvg, "w_kv_up": r_wkv,
        "gqa_q_norm_g": t_gq, "gqa_k_norm_g": t_kg, "w_br_a": r_wbra, "w_br_b": r_wbrb, "w_out": r_wout,
        "norm2_g": t_n2g, "w_up": r_wup, "conv_w": g_cw[None], "conv_b": t_cb, "w_down": r_wdown,
        "final_norm_g": t_fg.reshape(D),
    }
    arrives_transposed = ("w_kv_up", "w_br_a", "w_br_b", "w_up")
    arrives_halved = arrives_transposed + ("w_out", "w_down")
    weights = dict(c_ctx=c_ctx, w_ada=w_ada, b_ada=b_ada, norm1_g=norm1_g, w_in=w_in, mla_q_norm_g=mla_q_norm_g, w_q_up=w_q_up,
                   mla_kv_norm_g=mla_kv_norm_g, w_kv_up=w_kv_up, gqa_q_norm_g=gqa_q_norm_g, gqa_k_norm_g=gqa_k_norm_g, w_br_a=w_br_a,
                   w_br_b=w_br_b, w_out=w_out, norm2_g=norm2_g, w_up=w_up, conv_w=conv_w, conv_b=conv_b, w_down=w_down,
                   final_norm_g=final_norm_g)
    m_in = dict(c_ctx=m_c_ctx, w_ada=m_w_ada, b_ada=m_b_ada, norm1_g=m_norm1_g, w_in=m_w_in, mla_q_norm_g=m_mla_q_norm_g,
                w_q_up=m_w_q_up, mla_kv_norm_g=m_mla_kv_norm_g, w_kv_up=m_w_kv_up, gqa_q_norm_g=m_gqa_q_norm_g,
                gqa_k_norm_g=m_gqa_k_norm_g, w_br_a=m_w_br_a, w_br_b=m_w_br_b, w_out=m_w_out, norm2_g=m_norm2_g, w_up=m_w_up,
                conv_w=m_conv_w, conv_b=m_conv_b, w_down=m_w_down, final_norm_g=m_final_norm_g)
    v_in = dict(c_ctx=v_c_ctx, w_ada=v_w_ada, b_ada=v_b_ada, norm1_g=v_norm1_g, w_in=v_w_in, mla_q_norm_g=v_mla_q_norm_g,
                w_q_up=v_w_q_up, mla_kv_norm_g=v_mla_kv_norm_g, w_kv_up=v_w_kv_up, gqa_q_norm_g=v_gqa_q_norm_g,
                gqa_k_norm_g=v_gqa_k_norm_g, w_br_a=v_w_br_a, w_br_b=v_w_br_b, w_out=v_w_out, norm2_g=v_norm2_g, w_up=v_w_up,
                conv_w=v_conv_w, conv_b=v_conv_b, w_down=v_w_down, final_norm_g=v_final_norm_g)
    names = list(weights)
    big = [n for n in names if weights[n].ndim == 3 and weights[n].shape[1] >= 8]
    small = [n for n in names if n not in big]
    delta, new_m, new_v = {}, {}, {}

    def update(n, after=None):
        shp = weights[n].shape
        two_d = lambda a: a.reshape(shp[1], shp[2])
        g_t = n in arrives_transposed
        if n in arrives_halved:
            g_in, g_sib = grads[n]
        else:
            g_in, g_sib = two_d(grads[n].astype(F32)), None
        g_, d_, m_, v_ = _adamw(two_d(weights[n]), g_in, two_d(m_in[n]), two_d(v_in[n]), "adamw_" + n, g_transposed=g_t, g_sibling=g_sib,
                                after=after)
        grads[n], delta[n], new_m[n], new_v[n] = g_.reshape(shp), d_.reshape(shp), m_.reshape(shp), v_.reshape(shp)

    delta["w_ada"], new_m["w_ada"], new_v["w_ada"] = d_ada[None], m_ada[None], v_ada[None]
    early = [n for n in big if n not in ("w_in", "w_ada")]
    for n in early[:-1]:
        update(n)
    done = sum(delta[n][0, 0:1, 0:1] for n in early[:-1])
    j_in = _join_start(_scatter_sums(rs_in, done, "in"), done, "in")
    last = early[-1]
    update(last, after=j_in[4])
    ((g_mine, g_sib),) = _join_wait(j_in, delta[last], "in")
    g_, d_, m_, v_ = _adamw(w_in[0].T, g_mine, m_w_in[0].T, v_w_in[0].T, "adamw_w_in", g_sibling=g_sib)
    grads["w_in"], delta["w_in"], new_m["w_in"], new_v["w_in"] = g_.T[None], d_.T[None], m_.T[None], v_.T[None]
    grads = {n: grads[n].reshape(weights[n].shape).astype(F32) for n in names}

    slab = lambda tree: [tree[n].reshape(-1, LANES) for n in small]
    d_, m_, v_ = _adamw_many(slab(weights), slab(grads), slab(m_in), slab(v_in), "adamw_small")
    for k, n in enumerate(small):
        shp = weights[n].shape
        delta[n], new_m[n], new_v[n] = d_[k].reshape(shp), m_[k].reshape(shp), v_[k].reshape(shp)

    return (loss, grad_x[None], *[grads[n] for n in names], *[delta[n] for n in names], *[new_m[n] for n in names],
            *[new_v[n] for n in names])
```

```python
import math

import jax
import jax.numpy as jnp
from jax import lax
from jax.experimental import pallas as pl
from jax.experimental.pallas import tpu as pltpu

F32 = jnp.float32
BF16 = jnp.bfloat16
MESH = pl.DeviceIdType.MESH

NORM_EPS = 1e-6
ROPE_THETA = 10000.0
GRID_W = 64
MLA_HEADS = 8
MLA_Q_LORA = 768
MLA_KV_LORA = 512
MLA_NOPE = 128
MLA_ROPE = 64
MLA_V = 128
GQA_HEADS = 8
GQA_KV_HEADS = 2
GQA_HEAD_DIM = 128
GQA_GROUP = GQA_HEADS // GQA_KV_HEADS
LANES = 128
KVP = MLA_KV_LORA + 2 * GQA_KV_HEADS * GQA_HEAD_DIM + LANES
QC = MLA_Q_LORA + GQA_HEADS * GQA_HEAD_DIM

ADAM_LR = 0.001
ADAM_B1 = 0.9
ADAM_B2 = 0.999
ADAM_EPS = 1e-08
ADAM_WD = 0.01
ADAM_STEP = 10

VMEM_LIMIT = 56 * 1024 * 1024


def _pick(dim, target, mult=LANES):
    t = (min(target, dim) // mult) * mult
    while t >= mult:
        if dim % t == 0:
            return t
        t -= mult
    return dim


def _params(sem):
    return pltpu.CompilerParams(dimension_semantics=sem, vmem_limit_bytes=VMEM_LIMIT)


_DIMS = {"NN": (((1,), (0,)), ((), ())), "NT": (((1,), (1,)), ((), ())), "TN": (((0,), (0,)), ((), ()))}


MM_VMEM_BUDGET = 36 * 1024 * 1024


def _mm_tiles(M, N, K, sa, sb, so, tm, tn, tk):
    tm, tn, tk = _pick(M, tm), _pick(N, tn), _pick(K, tk)

    def need(t):
        return 2 * (tm * t * sa + t * tn * sb) + 2 * tm * tn * so + (tm * tn * 4 if t < K else 0)

    while need(tk) > MM_VMEM_BUDGET and tk > LANES:
        smaller = _pick(K, tk - LANES)
        if smaller >= tk:
            break
        tk = smaller
    return tm, tn, tk


def _window(block, index, offsets):
    if not any(offsets):
        return pl.BlockSpec(block, index)
    for t, o in zip(block, offsets):
        assert o % 16 == 0 and t % 16 == 0, (block, offsets)

    def at(i, j, k):
        return tuple(pl.multiple_of(o + p * t, math.gcd(o, t)) for p, t, o in zip(index(i, j, k), block, offsets))

    return pl.BlockSpec(tuple(pl.Element(t) for t in block), at)


def _mm(a, b, mode, out_dtype, name, m=None, n=None, k=None, b_off=0, add=None, out_rows=None, out_base=None, out_off=0,
        tm=1024, tn=1024, tk=2304, act=None, bias=None, after=None):
    if mode == "NN":
        M, K, N = m or a.shape[0], k or a.shape[1], b.shape[1]
    elif mode == "NT":
        M, K, N = m or a.shape[0], a.shape[1], n or b.shape[0]
    else:
        M, K, N = a.shape[1], k or a.shape[0], b.shape[1]
    tm, tn, tk = _mm_tiles(M, N, K, a.dtype.itemsize, b.dtype.itemsize, jnp.dtype(out_dtype).itemsize, tm, tn, tk)
    nk = K // tk
    dims = _DIMS[mode]
    n_in = 2 + (bias is not None) + (add is not None) + (out_base is not None) + (after is not None)

    def body(*refs):
        a_ref, b_ref = refs[:2]
        bias_ref = refs[2] if bias is not None else None
        add_ref = refs[2 + (bias is not None)] if add is not None else None
        o_ref = refs[n_in]
        av = a_ref[...]
        if act == "silu":
            av = av * jax.nn.sigmoid(av)
        part = lax.dot_general(av.astype(BF16), b_ref[...].astype(BF16), dims, preferred_element_type=F32)

        def finish(r):
            if bias is not None:
                r = r + bias_ref[...]
            if add is not None:
                r = r + add_ref[...]
            o_ref[...] = r.astype(out_dtype)

        if nk == 1:
            finish(part)
            return
        acc = refs[-1]
        k = pl.program_id(2)

        @pl.when(k == 0)
        def _():
            acc[...] = part

        @pl.when(jnp.logical_and(k > 0, k < nk - 1))
        def _():
            acc[...] += part

        @pl.when(k == nk - 1)
        def _():
            finish(acc[...] + part)

    a_spec = pl.BlockSpec((tk, tm), lambda i, j, k: (k, i)) if mode == "TN" else pl.BlockSpec((tm, tk), lambda i, j, k: (i, k))
    if mode == "NT":
        b_spec = _window((tn, tk), lambda i, j, k: (j, k), (b_off, 0))
    else:
        b_spec = _window((tk, tn), lambda i, j, k: (k, j), (b_off, 0))
    in_specs, args = [a_spec, b_spec], [a, b]
    if bias is not None:
        in_specs.append(pl.BlockSpec((1, tn), lambda i, j, k: (0, j)))
        args.append(bias)
    if add is not None:
        in_specs.append(pl.BlockSpec((tm, tn), lambda i, j, k: (i, j)))
        args.append(add)
    aliases = {}
    if after is not None:
        in_specs.append(pl.BlockSpec(after.shape, lambda i, j, k: (0, 0)))
        args.append(after)
    if out_base is not None:
        aliases = {len(args): 0}
        in_specs.append(ANY)
        args.append(out_base)
        out_rows = out_base.shape[0]
    return pl.pallas_call(
        body,
        name=name,
        grid=(M // tm, N // tn, nk),
        in_specs=in_specs,
        out_specs=_window((tm, tn), lambda i, j, k: (i, j), (out_off, 0)),
        out_shape=jax.ShapeDtypeStruct((out_rows or M, N), out_dtype),
        input_output_aliases=aliases,
        scratch_shapes=[pltpu.VMEM((tm, tn), F32)] if nk > 1 else [],
        compiler_params=_params(("parallel", "parallel", "arbitrary")),
    )(*args)


def _rms(x):
    r = lax.rsqrt(jnp.mean(x * x, axis=-1, keepdims=True) + NORM_EPS)
    return x * r, r


def _rms_bwd(xh, r, dxh):
    return r * (dxh - xh * jnp.mean(dxh * xh, axis=-1, keepdims=True))


def _swap(x, q):
    lane = lax.broadcasted_iota(jnp.int32, x.shape, 1)
    even = ((lane // q) % 2) == 0
    return jnp.where(even, pltpu.roll(x, LANES - q, 1), pltpu.roll(x, q, 1))


def _rope(x, cos, ss, q):
    return x * cos + _swap(x, q) * ss


def _rope_t(d, cos, ss, q):
    return d * cos + _swap(d * ss, q)


def _csum(x):
    return jnp.sum(x, axis=0, keepdims=True)


def _rows(tr, w, off=0):
    return pl.BlockSpec((tr, w), lambda i: (i + off, 0))


def _bcast(w):
    return pl.BlockSpec((1, w), lambda i: (0, 0))


def _acc_init(i, refs):
    @pl.when(i == 0)
    def _():
        for r in refs:
            r[...] = jnp.zeros_like(r)


def _rope_tables(n_ctx, n_lat, rot_dim):
    rows = n_lat // GRID_W
    row = jnp.repeat(jnp.arange(rows, dtype=F32), GRID_W)
    col = jnp.tile(jnp.arange(GRID_W, dtype=F32), rows)
    half = rot_dim // 2
    inv_freq = ROPE_THETA ** (-jnp.arange(0, half, 2, dtype=F32) / half)
    ar, ac = row[:, None] * inv_freq, col[:, None] * inv_freq
    cos = jnp.concatenate([jnp.cos(ar), jnp.cos(ar), jnp.cos(ac), jnp.cos(ac)], axis=-1)
    ss = jnp.concatenate([-jnp.sin(ar), jnp.sin(ar), -jnp.sin(ac), jnp.sin(ac)], axis=-1)
    cos = jnp.tile(cos, (1, LANES // rot_dim))
    ss = jnp.tile(ss, (1, LANES // rot_dim))
    cos = jnp.concatenate([cos, jnp.ones((n_ctx, LANES), F32)], axis=0)
    ss = jnp.concatenate([ss, jnp.zeros((n_ctx, LANES), F32)], axis=0)
    return cos, ss


def _norm_mod_fwd(x2d, g, sh, sc, name, tr, out_rows=None, base=None, out_off=0):
    n, d = x2d.shape

    def body(x_ref, g_ref, sh_ref, sc_ref, *rest):
        xh, _ = _rms(x_ref[...])
        rest[-1][...] = ((xh * g_ref[...]) * (1.0 + sc_ref[...]) + sh_ref[...]).astype(BF16)

    args, in_specs, aliases = [x2d, g, sh, sc], [_rows(tr, d), _bcast(d), _bcast(d), _bcast(d)], {}
    if base is not None:
        args.append(base)
        in_specs.append(ANY)
        aliases = {4: 0}
        out_rows = base.shape[0]
    return pl.pallas_call(
        body,
        name=name,
        grid=(n // tr,),
        in_specs=in_specs,
        out_specs=_rows(tr, d, out_off // tr),
        out_shape=jax.ShapeDtypeStruct((out_rows or n, d), BF16),
        input_output_aliases=aliases,
        compiler_params=_params(("parallel",)),
    )(*args)


def _norm_mod_bwd(dz, dz_off, x2d, g, sc, dres, name, tr):
    n, d = x2d.shape
    want_dx = dres is not None

    def body(*refs):
        if want_dx:
            dz_ref, x_ref, g_ref, sc_ref, dres_ref, dx_ref, dg_ref, dsh_ref, dsc_ref = refs
        else:
            dz_ref, x_ref, g_ref, sc_ref, dg_ref, dsh_ref, dsc_ref = refs
        _acc_init(pl.program_id(0), [dg_ref, dsh_ref, dsc_ref])
        xh, r = _rms(x_ref[...])
        dzv = dz_ref[...]
        gv = g_ref[...]
        dsc_ref[...] += _csum(dzv * (xh * gv))
        dsh_ref[...] += _csum(dzv)
        dh = dzv * (1.0 + sc_ref[...])
        dg_ref[...] += _csum(dh * xh)
        if want_dx:
            dx_ref[...] = _rms_bwd(xh, r, dh * gv) + dres_ref[...]

    in_specs = [_rows(tr, d, dz_off), _rows(tr, d), _bcast(d), _bcast(d)]
    args = [dz, x2d, g, sc]
    out_specs = [_bcast(d)] * 3
    out_shape = [jax.ShapeDtypeStruct((1, d), F32)] * 3
    if want_dx:
        in_specs.append(_rows(tr, d))
        args.append(dres)
        out_specs = [_rows(tr, d)] + out_specs
        out_shape = [jax.ShapeDtypeStruct((n, d), F32)] + out_shape
    res = pl.pallas_call(
        body,
        name=name,
        grid=(n // tr,),
        in_specs=in_specs,
        out_specs=out_specs,
        out_shape=out_shape,
        compiler_params=_params(("arbitrary",)),
    )(*args)
    return res if want_dx else (None, *res)


_QA, _QB = MLA_ROPE // 4, GQA_HEAD_DIM // 4


def _kprep_fwd(pkv, kvg, kg, cos_a, ss_a, cos_b, ss_b, tr):
    n = pkv.shape[0]
    nb = GQA_KV_HEADS * GQA_HEAD_DIM

    def body(p_ref, kvg_ref, kg_ref, ca, sa, cb, sb, ckv_ref, kb_ref, vb_ref, kpe_ref):
        p = p_ref[...]
        xh, _ = _rms(p[:, :MLA_KV_LORA])
        ckv_ref[...] = (xh * kvg_ref[...]).astype(BF16)
        for e in range(GQA_KV_HEADS):
            lo = MLA_KV_LORA + e * GQA_HEAD_DIM
            kh, _ = _rms(p[:, lo : lo + GQA_HEAD_DIM])
            kb_ref[:, e * GQA_HEAD_DIM : (e + 1) * GQA_HEAD_DIM] = _rope(kh * kg_ref[...], cb[...], sb[...], _QB).astype(BF16)
        vb_ref[...] = p[:, MLA_KV_LORA + nb : MLA_KV_LORA + 2 * nb].astype(BF16)
        kr = _rope(p[:, MLA_KV_LORA + 2 * nb :], ca[...], sa[...], _QA)
        kpe_ref[:, :LANES] = kr.astype(BF16)
        kpe_ref[:, LANES:] = pltpu.roll(kr, MLA_ROPE, 1).astype(BF16)

    return pl.pallas_call(
        body,
        name="kprep_fwd",
        grid=(n // tr,),
        in_specs=[_rows(tr, KVP), _bcast(MLA_KV_LORA), _bcast(GQA_HEAD_DIM)] + [_rows(tr, LANES)] * 4,
        out_specs=[_rows(tr, MLA_KV_LORA), _rows(tr, nb), _rows(tr, nb), _rows(tr, 2 * LANES)],
        out_shape=[jax.ShapeDtypeStruct((n, w), BF16) for w in (MLA_KV_LORA, nb, nb, 2 * LANES)],
        compiler_params=_params(("parallel",)),
    )(pkv, kvg, kg, cos_a, ss_a, cos_b, ss_b)


def _kprep_bwd(pkv, dckv, dkb, dvb, dkpe, kvg, kg, cos_b, ss_b, tr):
    n = pkv.shape[0]
    nb = GQA_KV_HEADS * GQA_HEAD_DIM

    def body(p_ref, dckv_ref, dkb_ref, dvb_ref, dkpe_ref, kvg_ref, kg_ref, cb, sb, dp_ref, dkvg_ref, dkg_ref):
        _acc_init(pl.program_id(0), [dkvg_ref, dkg_ref])
        p = p_ref[...]
        xh, r = _rms(p[:, :MLA_KV_LORA])
        dn = dckv_ref[...]
        dkvg_ref[...] += _csum(dn * xh)
        dp_ref[:, :MLA_KV_LORA] = _rms_bwd(xh, r, dn * kvg_ref[...]).astype(BF16)
        for e in range(GQA_KV_HEADS):
            lo = MLA_KV_LORA + e * GQA_HEAD_DIM
            kh, rk = _rms(p[:, lo : lo + GQA_HEAD_DIM])
            dk = _rope_t(dkb_ref[:, e * GQA_HEAD_DIM : (e + 1) * GQA_HEAD_DIM], cb[...], sb[...], _QB)
            dkg_ref[...] += _csum(dk * kh)
            dp_ref[:, lo : lo + GQA_HEAD_DIM] = _rms_bwd(kh, rk, dk * kg_ref[...]).astype(BF16)
        dp_ref[:, MLA_KV_LORA + nb : MLA_KV_LORA + 2 * nb] = dvb_ref[...].astype(BF16)
        dp_ref[:, MLA_KV_LORA + 2 * nb :] = dkpe_ref[...].astype(BF16)

    return pl.pallas_call(
        body,
        name="kprep_bwd",
        grid=(n // tr,),
        in_specs=[_rows(tr, KVP), _rows(tr, MLA_KV_LORA), _rows(tr, nb), _rows(tr, nb), _rows(tr, LANES),
                  _bcast(MLA_KV_LORA), _bcast(GQA_HEAD_DIM), _rows(tr, LANES), _rows(tr, LANES)],
        out_specs=[_rows(tr, KVP), _bcast(MLA_KV_LORA), _bcast(GQA_HEAD_DIM)],
        out_shape=[jax.ShapeDtypeStruct((n, KVP), BF16), jax.ShapeDtypeStruct((1, MLA_KV_LORA), F32),
                   jax.ShapeDtypeStruct((1, GQA_HEAD_DIM), F32)],
        compiler_params=_params(("arbitrary",)),
    )(pkv, dckv, dkb, dvb, dkpe, kvg, kg, cos_b, ss_b)


def _kgrad_split(dka, dva, cos_a, ss_a, tr):
    n = dka.shape[0]
    wk = MLA_HEADS * 2 * LANES

    def body(dk_ref, dv_ref, ca, sa, dkv_ref, dkpe_ref):
        even = jnp.zeros((tr, LANES), F32)
        odd = jnp.zeros((tr, LANES), F32)
        for h in range(MLA_HEADS):
            dkv_ref[:, 2 * h * LANES : (2 * h + 1) * LANES] = dk_ref[:, 2 * h * LANES : (2 * h + 1) * LANES].astype(BF16)
            dkv_ref[:, (2 * h + 1) * LANES : (2 * h + 2) * LANES] = dv_ref[:, h * MLA_V : (h + 1) * MLA_V].astype(BF16)
            part = dk_ref[:, (2 * h + 1) * LANES : (2 * h + 2) * LANES]
            if h % 2 == 0:
                even = even + part
            else:
                odd = odd + part
        lane = lax.broadcasted_iota(jnp.int32, (tr, LANES), 1)
        low = lane < MLA_ROPE
        both = jnp.where(low, even, odd)
        tot = jnp.where(low, both + pltpu.roll(both, MLA_ROPE, 1), 0.0)
        dkpe_ref[...] = _rope_t(tot, ca[...], sa[...], _QA)

    return pl.pallas_call(
        body,
        name="kgrad_split",
        grid=(n // tr,),
        in_specs=[_rows(tr, wk), _rows(tr, MLA_HEADS * MLA_V), _rows(tr, LANES), _rows(tr, LANES)],
        out_specs=[_rows(tr, wk), _rows(tr, LANES)],
        out_shape=[jax.ShapeDtypeStruct((n, wk), BF16), jax.ShapeDtypeStruct((n, LANES), F32)],
        compiler_params=_params(("parallel",)),
    )(dka, dva, cos_a, ss_a)


def _qprep_fwd(pq, qg, gq, cos_b, ss_b, tr):
    n = pq.shape[0]
    nq = GQA_HEADS * GQA_HEAD_DIM

    def body(p_ref, qg_ref, gq_ref, cb, sb, cq_ref, qb_ref):
        xh, _ = _rms(p_ref[:, :MLA_Q_LORA])
        cq_ref[...] = (xh * qg_ref[...]).astype(BF16)
        for h in range(GQA_HEADS):
            lo = MLA_Q_LORA + h * GQA_HEAD_DIM
            qh, _ = _rms(p_ref[:, lo : lo + GQA_HEAD_DIM])
            qb_ref[:, h * GQA_HEAD_DIM : (h + 1) * GQA_HEAD_DIM] = _rope(qh * gq_ref[...], cb[...], sb[...], _QB).astype(BF16)

    return pl.pallas_call(
        body,
        name="qprep_fwd",
        grid=(n // tr,),
        in_specs=[_rows(tr, QC), _bcast(MLA_Q_LORA), _bcast(GQA_HEAD_DIM), _rows(tr, LANES), _rows(tr, LANES)],
        out_specs=[_rows(tr, MLA_Q_LORA), _rows(tr, nq)],
        out_shape=[jax.ShapeDtypeStruct((n, MLA_Q_LORA), BF16), jax.ShapeDtypeStruct((n, nq), BF16)],
        compiler_params=_params(("parallel",)),
    )(pq, qg, gq, cos_b, ss_b)


def _qprep_bwd(pq, dcq, dqb, qg, gq, cos_b, ss_b, tr):
    n = pq.shape[0]
    nq = GQA_HEADS * GQA_HEAD_DIM

    def body(p_ref, dcq_ref, dqb_ref, qg_ref, gq_ref, cb, sb, dp_ref, dqg_ref, dgq_ref):
        _acc_init(pl.program_id(0), [dqg_ref, dgq_ref])
        xh, r = _rms(p_ref[:, :MLA_Q_LORA])
        dn = dcq_ref[...]
        dqg_ref[...] += _csum(dn * xh)
        dp_ref[:, :MLA_Q_LORA] = _rms_bwd(xh, r, dn * qg_ref[...]).astype(BF16)
        for h in range(GQA_HEADS):
            lo = MLA_Q_LORA + h * GQA_HEAD_DIM
            qh, rq = _rms(p_ref[:, lo : lo + GQA_HEAD_DIM])
            dq = _rope_t(dqb_ref[:, h * GQA_HEAD_DIM : (h + 1) * GQA_HEAD_DIM], cb[...], sb[...], _QB)
            dgq_ref[...] += _csum(dq * qh)
            dp_ref[:, lo : lo + GQA_HEAD_DIM] = _rms_bwd(qh, rq, dq * gq_ref[...]).astype(BF16)

    return pl.pallas_call(
        body,
        name="qprep_bwd",
        grid=(n // tr,),
        in_specs=[_rows(tr, QC), _rows(tr, MLA_Q_LORA), _rows(tr, nq), _bcast(MLA_Q_LORA), _bcast(GQA_HEAD_DIM),
                  _rows(tr, LANES), _rows(tr, LANES)],
        out_specs=[_rows(tr, QC), _bcast(MLA_Q_LORA), _bcast(GQA_HEAD_DIM)],
        out_shape=[jax.ShapeDtypeStruct((n, QC), BF16), jax.ShapeDtypeStruct((1, MLA_Q_LORA), F32),
                   jax.ShapeDtypeStruct((1, GQA_HEAD_DIM), F32)],
        compiler_params=_params(("arbitrary",)),
    )(pq, dcq, dqb, qg, gq, cos_b, ss_b)


_QA_COLS = MLA_HEADS * (MLA_NOPE + MLA_ROPE)


def _qrope_fwd(qa, cos_a, ss_a, tr):
    n = qa.shape[0]

    def body(q_ref, ca, sa, o_ref):
        for j in range(MLA_HEADS // 2):
            lo = 3 * j * LANES
            o_ref[:, lo : lo + 2 * LANES] = q_ref[:, lo : lo + 2 * LANES].astype(BF16)
            o_ref[:, lo + 2 * LANES : lo + 3 * LANES] = _rope(q_ref[:, lo + 2 * LANES : lo + 3 * LANES], ca[...], sa[...], _QA).astype(BF16)

    return pl.pallas_call(
        body,
        name="qrope_fwd",
        grid=(n // tr,),
        in_specs=[_rows(tr, _QA_COLS), _rows(tr, LANES), _rows(tr, LANES)],
        out_specs=_rows(tr, _QA_COLS),
        out_shape=jax.ShapeDtypeStruct((n, _QA_COLS), BF16),
        compiler_params=_params(("parallel",)),
    )(qa, cos_a, ss_a)


def _qrope_bwd(dq2, cos_a, ss_a, tr):
    n = dq2.shape[0]

    def body(d_ref, ca, sa, o_ref):
        for j in range(MLA_HEADS // 2):
            lo = 3 * j * LANES
            h0, h1 = 2 * j, 2 * j + 1
            o_ref[:, lo : lo + LANES] = d_ref[:, 2 * h0 * LANES : (2 * h0 + 1) * LANES].astype(BF16)
            o_ref[:, lo + LANES : lo + 2 * LANES] = d_ref[:, 2 * h1 * LANES : (2 * h1 + 1) * LANES].astype(BF16)
            pe = d_ref[:, (2 * h0 + 1) * LANES : (2 * h0 + 2) * LANES] + d_ref[:, (2 * h1 + 1) * LANES : (2 * h1 + 2) * LANES]
            o_ref[:, lo + 2 * LANES : lo + 3 * LANES] = _rope_t(pe, ca[...], sa[...], _QA).astype(BF16)

    return pl.pallas_call(
        body,
        name="qrope_bwd",
        grid=(n // tr,),
        in_specs=[_rows(tr, MLA_HEADS * 2 * LANES), _rows(tr, LANES), _rows(tr, LANES)],
        out_specs=_rows(tr, _QA_COLS),
        out_shape=jax.ShapeDtypeStruct((n, _QA_COLS), BF16),
        compiler_params=_params(("parallel",)),
    )(dq2, cos_a, ss_a)


def _cat(refs):
    vals = [r[...] for r in refs]
    return vals[0] if len(vals) == 1 else jnp.concatenate(vals, axis=-1)


LOG2E = 1.4426950408889634


def _attn_fwd(qparts, kparts, vpart, n_heads, group, dv, scale, name, tq, after=None):
    T, Tk = qparts[0][0].shape[0], kparts[0][0].shape[0]
    nq_, nk_ = len(qparts), len(kparts)
    sub = min(tq, 256)
    c2 = scale * LOG2E

    def body(*refs):
        q_refs, k_refs = refs[:nq_], refs[nq_ : nq_ + nk_]
        v_ref = refs[nq_ + nk_]
        o_ref, lse_ref = refs[-2:]
        k = _cat(k_refs)
        v = v_ref[...]
        for r0 in range(0, tq, sub):
            q = _cat([r.at[r0 : r0 + sub, :] for r in q_refs])
            s = lax.dot_general(q, k, _DIMS["NT"], preferred_element_type=F32)
            m = jnp.max(s, axis=-1, keepdims=True)
            p = jnp.exp2((s - m) * c2)
            l = jnp.sum(p, axis=-1, keepdims=True)
            acc = jnp.dot(p.astype(BF16), v, preferred_element_type=F32)
            o_ref[r0 : r0 + sub, :] = (acc * (1.0 / l)).astype(BF16)
            lse_ref[r0 : r0 + sub, :] = m * scale + jnp.log(l)

    in_specs = [pl.BlockSpec((tq, LANES), lambda h, i, f=f: (i, f(h))) for _, f in qparts]
    in_specs += [pl.BlockSpec((Tk, LANES), lambda h, i, f=f: (0, f(h // group))) for _, f in kparts]
    fv = vpart[1]
    in_specs.append(pl.BlockSpec((Tk, dv), lambda h, i: (0, fv(h // group))))
    args = [*[a for a, _ in qparts], *[a for a, _ in kparts], vpart[0]]
    if after is not None:
        in_specs.append(pl.BlockSpec(after.shape, lambda h, i: (0, 0)))
        args.append(after)
    return pl.pallas_call(
        body,
        name=name,
        grid=(n_heads, T // tq),
        in_specs=in_specs,
        out_specs=[pl.BlockSpec((tq, dv), lambda h, i: (i, h)), pl.BlockSpec((None, tq, 1), lambda h, i: (h, i, 0))],
        out_shape=[jax.ShapeDtypeStruct((T, n_heads * dv), BF16), jax.ShapeDtypeStruct((n_heads, T, 1), F32)],
        compiler_params=_params(("parallel", "parallel")),
    )(*args)


def _attn_bwd(qparts, kparts, vpart, o, do, lse, n_heads, group, dv, scale, name, tq):
    T, Tk = qparts[0][0].shape[0], kparts[0][0].shape[0]
    nq_, nk_ = len(qparts), len(kparts)
    dk_ = LANES * nq_
    n_kv = n_heads // group
    nblk = T // tq
    c2 = scale * LOG2E

    def head(hk, i):
        return hk * group + i // nblk

    sub = min(tq, 256)

    def body(*refs):
        q_refs = refs[:nq_]
        k = _cat(refs[nq_ : nq_ + nk_])
        v_ref, o_ref, do_ref, lse_ref, dq_ref, dk_ref, dv_ref = refs[nq_ + nk_ :]
        i = pl.program_id(1)
        _acc_init(i, [dk_ref, dv_ref])
        v = v_ref[...]
        dk_acc, dv_acc = None, None
        for r0 in range(0, tq, sub):
            rows = slice(r0, r0 + sub)
            q = _cat([r.at[rows, :] for r in q_refs])
            s = lax.dot_general(q, k, _DIMS["NT"], preferred_element_type=F32)
            p = jnp.exp2(s * c2 - lse_ref[rows, :] * LOG2E)
            dov = do_ref[rows, :]
            dp = lax.dot_general(dov, v, _DIMS["NT"], preferred_element_type=F32)
            delta = jnp.sum(dov.astype(F32) * o_ref[rows, :].astype(F32), axis=-1, keepdims=True)
            ds = (p * (dp - delta)).astype(BF16)
            dq_ref[rows, :] = jnp.dot(ds, k, preferred_element_type=F32) * scale
            dk_part = lax.dot_general(ds, q, _DIMS["TN"], preferred_element_type=F32)
            dv_part = lax.dot_general(p.astype(BF16), dov, _DIMS["TN"], preferred_element_type=F32)
            dk_acc = dk_part if dk_acc is None else dk_acc + dk_part
            dv_acc = dv_part if dv_acc is None else dv_acc + dv_part
        dk_ref[...] += dk_acc
        dv_ref[...] += dv_acc

        @pl.when(i == group * nblk - 1)
        def _():
            dk_ref[...] *= scale

    in_specs = [pl.BlockSpec((tq, LANES), lambda hk, i, f=f: (i % nblk, f(head(hk, i)))) for _, f in qparts]
    in_specs += [pl.BlockSpec((Tk, LANES), lambda hk, i, f=f: (0, f(hk))) for _, f in kparts]
    fv = vpart[1]
    in_specs.append(pl.BlockSpec((Tk, dv), lambda hk, i: (0, fv(hk))))
    in_specs += [pl.BlockSpec((tq, dv), lambda hk, i: (i % nblk, head(hk, i)))] * 2
    in_specs.append(pl.BlockSpec((None, tq, 1), lambda hk, i: (head(hk, i), i % nblk, 0)))
    return pl.pallas_call(
        body,
        name=name,
        grid=(n_kv, group * nblk),
        in_specs=in_specs,
        out_specs=[pl.BlockSpec((tq, dk_), lambda hk, i: (i % nblk, head(hk, i))),
                   pl.BlockSpec((Tk, dk_), lambda hk, i: (0, hk)),
                   pl.BlockSpec((Tk, dv), lambda hk, i: (0, hk))],
        out_shape=[jax.ShapeDtypeStruct((T, n_heads * dk_), F32), jax.ShapeDtypeStruct((Tk, n_kv * dk_), F32),
                   jax.ShapeDtypeStruct((Tk, n_kv * dv), F32)],
        compiler_params=_params(("parallel", "arbitrary")),
    )(*[a for a, _ in qparts], *[a for a, _ in kparts], vpart[0], o, do, lse)


def _gates_fwd(pg, ya, yb, tr):
    n, d = ya.shape

    def body(pg_ref, ya_ref, yb_ref, o_ref):
        ga = jax.nn.sigmoid(pg_ref[:, :d].astype(F32))
        gb = jax.nn.sigmoid(pg_ref[:, d:].astype(F32))
        o_ref[...] = (ga * ya_ref[...].astype(F32) + gb * yb_ref[...].astype(F32)).astype(BF16)

    return pl.pallas_call(
        body,
        name="gates_fwd",
        grid=(n // tr,),
        in_specs=[_rows(tr, 2 * d), _rows(tr, d), _rows(tr, d)],
        out_specs=_rows(tr, d),
        out_shape=jax.ShapeDtypeStruct((n, d), BF16),
        compiler_params=_params(("parallel",)),
    )(pg, ya, yb)


def _gates_bwd(dm, pg, ya, yb, tr):
    n, d = ya.shape

    def body(dm_ref, pg_ref, ya_ref, yb_ref, dya_ref, dyb_ref, dpg_ref):
        dmv = dm_ref[...].astype(F32)
        ga = jax.nn.sigmoid(pg_ref[:, :d].astype(F32))
        gb = jax.nn.sigmoid(pg_ref[:, d:].astype(F32))
        dya_ref[...] = (dmv * ga).astype(BF16)
        dyb_ref[...] = (dmv * gb).astype(BF16)
        dpg_ref[:, :d] = (dmv * ya_ref[...].astype(F32) * ga * (1.0 - ga)).astype(BF16)
        dpg_ref[:, d:] = (dmv * yb_ref[...].astype(F32) * gb * (1.0 - gb)).astype(BF16)

    return pl.pallas_call(
        body,
        name="gates_bwd",
        grid=(n // tr,),
        in_specs=[_rows(tr, d), _rows(tr, 2 * d), _rows(tr, d), _rows(tr, d)],
        out_specs=[_rows(tr, d), _rows(tr, d), _rows(tr, 2 * d)],
        out_shape=[jax.ShapeDtypeStruct((n, d), BF16), jax.ShapeDtypeStruct((n, d), BF16), jax.ShapeDtypeStruct((n, 2 * d), BF16)],
        compiler_params=_params(("parallel",)),
    )(dm, pg, ya, yb)


def _resid_norm2_fwd(x2d, att, g1, n2g, sh2, sc2, tr):
    n, d = x2d.shape

    def body(x_ref, a_ref, g1_ref, g_ref, sh_ref, sc_ref, x1_ref, z_ref):
        x1 = x_ref[...] + g1_ref[...] * a_ref[...]
        x1_ref[...] = x1
        xh, _ = _rms(x1)
        z_ref[...] = ((xh * g_ref[...]) * (1.0 + sc_ref[...]) + sh_ref[...]).astype(BF16)

    return pl.pallas_call(
        body,
        name="resid_norm2_fwd",
        grid=(n // tr,),
        in_specs=[_rows(tr, d), _rows(tr, d)] + [_bcast(d)] * 4,
        out_specs=[_rows(tr, d), _rows(tr, d)],
        out_shape=[jax.ShapeDtypeStruct((n, d), F32), jax.ShapeDtypeStruct((n, d), BF16)],
        compiler_params=_params(("parallel",)),
    )(x2d, att, g1, n2g, sh2, sc2)


def _resid_norm2_bwd(dz2, x1, dx2, att, n2g, sc2, g1, tr):
    n, d = x1.shape

    def body(dz_ref, x1_ref, dx2_ref, a_ref, g_ref, sc_ref, g1_ref, dx1_ref, da_ref, dg_ref, dsh_ref, dsc_ref, dg1_ref):
        _acc_init(pl.program_id(0), [dg_ref, dsh_ref, dsc_ref, dg1_ref])
        xh, r = _rms(x1_ref[...])
        dzv = dz_ref[...]
        gv = g_ref[...]
        dsc_ref[...] += _csum(dzv * (xh * gv))
        dsh_ref[...] += _csum(dzv)
        dh = dzv * (1.0 + sc_ref[...])
        dg_ref[...] += _csum(dh * xh)
        dx1 = _rms_bwd(xh, r, dh * gv) + dx2_ref[...]
        dx1_ref[...] = dx1
        dg1_ref[...] += _csum(dx1 * a_ref[...])
        da_ref[...] = (dx1 * g1_ref[...]).astype(BF16)

    return pl.pallas_call(
        body,
        name="resid_norm2_bwd",
        grid=(n // tr,),
        in_specs=[_rows(tr, d)] * 4 + [_bcast(d)] * 3,
        out_specs=[_rows(tr, d), _rows(tr, d)] + [_bcast(d)] * 4,
        out_shape=[jax.ShapeDtypeStruct((n, d), F32), jax.ShapeDtypeStruct((n, d), BF16)] + [jax.ShapeDtypeStruct((1, d), F32)] * 4,
        compiler_params=_params(("arbitrary",)),
    )(dz2, x1, dx2, att, n2g, sc2, g1)


def _edges(shape):
    row = lax.broadcasted_iota(jnp.int32, shape, 0)
    return row == 0, row == shape[0] - 1


def _shifts(u, edges):
    n = u.shape[0]
    return jnp.where(edges[0], 0.0, pltpu.roll(u, 1, 0)), jnp.where(edges[1], 0.0, pltpu.roll(u, n - 1, 0))


def _conv3(u, prev, nxt, w_ref, b_ref):
    return b_ref[...] + w_ref[0:1, :] * prev + w_ref[1:2, :] * u + w_ref[2:3, :] * nxt


def _ffn_up_conv(z, wup_t, cw, cb, tc, after):
    n, d = z.shape
    f = wup_t.shape[0] // 2
    nb = f // tc

    def body(z_ref, wa_ref, wb_ref, cwa, cwb, cba, cbb, after_ref, ua_ref, ub_ref, h_ref):
        w = jnp.concatenate([wa_ref[...], wb_ref[...]], axis=0)
        u = lax.dot_general(z_ref[...], w, _DIMS["NT"], preferred_element_type=F32).astype(BF16)
        ua_ref[...] = u[:, :tc]
        ub_ref[...] = u[:, tc:]
        edges = _edges((n, tc))
        ua = u[:, :tc].astype(F32)
        ub = u[:, tc:].astype(F32)
        a = _conv3(ua, *_shifts(ua, edges), cwa, cba)
        b = _conv3(ub, *_shifts(ub, edges), cwb, cbb)
        h_ref[...] = (a * jax.nn.sigmoid(a) * b).astype(BF16)

    col = lambda rows, off: pl.BlockSpec((rows, tc), lambda i: (0, i + off))
    w_rows = lambda off: pl.BlockSpec((tc, d), lambda i: (i + off, 0))
    return pl.pallas_call(
        body,
        name="ffn_up_conv",
        grid=(nb,),
        in_specs=[pl.BlockSpec((n, d), lambda i: (0, 0)), w_rows(0), w_rows(nb), col(3, 0), col(3, nb), col(1, 0), col(1, nb),
                  pl.BlockSpec(after.shape, lambda i: (0, 0))],
        out_specs=[col(n, 0)] * 3,
        out_shape=[jax.ShapeDtypeStruct((n, f), BF16)] * 3,
        compiler_params=_params(("parallel",)),
    )(z, wup_t, wup_t, cw, cw, cb, cb, after)


def _ffn_down_dx_conv_bwd(df, wdown, u_a, u_b, cw, cb, tc, after):
    n, f = u_a.shape
    d = df.shape[1]
    nb = f // tc

    def part(uv, prev, nxt, duc, edges, w_ref, du_ref, dw_ref, db_ref):
        db_ref[...] = _csum(duc)
        dw_ref[0:1, :] = _csum(duc * prev)
        dw_ref[1:2, :] = _csum(duc * uv)
        dw_ref[2:3, :] = _csum(duc * nxt)
        d_prev, d_next = _shifts(duc, edges)
        du_ref[...] = (w_ref[0:1, :] * d_next + w_ref[1:2, :] * duc + w_ref[2:3, :] * d_prev).astype(BF16)

    def body(df_ref, wd_ref, ua_ref, ub_ref, wa_ref, wb_ref, ba_ref, bb_ref, after_ref,
             dua_ref, dub_ref, dwa_ref, dwb_ref, dba_ref, dbb_ref):
        dhv = lax.dot_general(df_ref[...], wd_ref[...], _DIMS["NT"], preferred_element_type=F32)
        dhv = dhv.astype(BF16).astype(F32)
        edges = _edges((n, tc))
        ua = ua_ref[...].astype(F32)
        ub = ub_ref[...].astype(F32)
        sa = _shifts(ua, edges)
        sb = _shifts(ub, edges)
        a = _conv3(ua, *sa, wa_ref, ba_ref)
        b = _conv3(ub, *sb, wb_ref, bb_ref)
        sg = jax.nn.sigmoid(a)
        da = dhv * b * (sg * (1.0 + a * (1.0 - sg)))
        db = dhv * (a * sg)
        part(ua, *sa, da, edges, wa_ref, dua_ref, dwa_ref, dba_ref)
        part(ub, *sb, db, edges, wb_ref, dub_ref, dwb_ref, dbb_ref)

    col = lambda rows, off: pl.BlockSpec((rows, tc), lambda i: (0, i + off))
    return pl.pallas_call(
        body,
        name="ffn_down_dx_conv_bwd",
        grid=(nb,),
        in_specs=[pl.BlockSpec((n, d), lambda i: (0, 0)), pl.BlockSpec((tc, d), lambda i: (i, 0)), col(n, 0), col(n, 0),
                  col(3, 0), col(3, nb), col(1, 0), col(1, nb), pl.BlockSpec(after.shape, lambda i: (0, 0))],
        out_specs=[col(n, 0), col(n, 0), col(3, 0), col(3, 0), col(1, 0), col(1, 0)],
        out_shape=[jax.ShapeDtypeStruct((n, f), BF16)] * 2 + [jax.ShapeDtypeStruct((3, f), F32)] * 2 + [jax.ShapeDtypeStruct((1, f), F32)] * 2,
        compiler_params=_params(("parallel",)),
    )(df, wdown, u_a, u_b, cw, cw, cb, cb, after)


def _loss_head(x1, f, g2, fg, tgt, tr):
    n, d = x1.shape

    def body(x1_ref, f_ref, g2_ref, fg_ref, t_ref, sq_ref, dx2_ref, dfg_ref, dg2_ref, df_ref):
        _acc_init(pl.program_id(0), [sq_ref, dfg_ref, dg2_ref])
        fv = f_ref[...]
        xh, r = _rms(x1_ref[...] + g2_ref[...] * fv)
        err = xh * fg_ref[...] - t_ref[...]
        sq_ref[...] += _csum(err * err)
        dy = err * (1.0 / d)
        dfg_ref[...] += _csum(dy * xh)
        dx2 = _rms_bwd(xh, r, dy * fg_ref[...])
        dx2_ref[...] = dx2
        dg2_ref[...] += _csum(dx2 * fv)
        df_ref[...] = (dx2 * g2_ref[...]).astype(BF16)

    return pl.pallas_call(
        body,
        name="loss_head",
        grid=(n // tr,),
        in_specs=[_rows(tr, d), _rows(tr, d), _bcast(d), _bcast(d), _rows(tr, d)],
        out_specs=[_bcast(d), _rows(tr, d), _bcast(d), _bcast(d), _rows(tr, d)],
        out_shape=[jax.ShapeDtypeStruct((1, d), F32), jax.ShapeDtypeStruct((n, d), F32), jax.ShapeDtypeStruct((1, d), F32),
                   jax.ShapeDtypeStruct((1, d), F32), jax.ShapeDtypeStruct((n, d), BF16)],
        compiler_params=_params(("arbitrary",)),
    )(x1, f, g2, fg, tgt)


def _sum_slots(g, name):
    s, r, w = g.shape

    def body(g_ref, o_ref):
        acc = g_ref[0]
        for k in range(1, s):
            acc = acc + g_ref[k]
        o_ref[...] = acc

    return pl.pallas_call(body, name=name, out_shape=jax.ShapeDtypeStruct((r, w), F32))(g)


def _silu_grad_mul(ds, cvec):
    def body(d_ref, c_ref, o_ref):
        cv = c_ref[...]
        sg = jax.nn.sigmoid(cv)
        o_ref[...] = d_ref[...] * (sg * (1.0 + cv * (1.0 - sg)))

    return pl.pallas_call(body, name="silu_grad_mul", out_shape=jax.ShapeDtypeStruct(ds.shape, F32))(ds, cvec)


def _adamw_update(wv, gv, mv, vv, d_ref, mo_ref, vo_ref):
    mn = ADAM_B1 * mv + (1.0 - ADAM_B1) * gv
    vn = ADAM_B2 * vv + (1.0 - ADAM_B2) * (gv * gv)
    mo_ref[...] = mn
    vo_ref[...] = vn
    m_hat = mn / (1.0 - ADAM_B1**ADAM_STEP)
    v_hat = vn / (1.0 - ADAM_B2**ADAM_STEP)
    d_ref[...] = -ADAM_LR * (m_hat / (jnp.sqrt(v_hat) + ADAM_EPS) + ADAM_WD * wv)


def _adamw_many(ws, gs, ms, vs, name):
    n = len(ws)

    def body(*refs):
        for k in range(n):
            w_ref, g_ref, m_ref, v_ref = (refs[q * n + k] for q in range(4))
            d_ref, mo_ref, vo_ref = (refs[(4 + q) * n + k] for q in range(3))
            _adamw_update(w_ref[...], g_ref[...], m_ref[...], v_ref[...], d_ref, mo_ref, vo_ref)

    res = pl.pallas_call(body, name=name, out_shape=[jax.ShapeDtypeStruct(w.shape, F32) for w in ws] * 3)(*ws, *gs, *ms, *vs)
    return res[:n], res[n : 2 * n], res[2 * n :]


def _adamw(w, g, m, v, name, g_transposed=False, g_sibling=None, after=None):
    r, cdim = w.shape
    halves = g_sibling is not None
    block = 1 << 19
    if g_transposed:
        tc = _pick(cdim // 2 if halves else cdim, 2048)
        tr = _pick(r, max(LANES, block // tc), LANES)
        per_half = (cdim // 2) // tc
    else:
        rows = r // 2 if halves else r
        tc = _pick(cdim, 2048)
        tr = _pick(rows, max(8, block // tc), 8)
        if tr < 64 and rows > 64:
            tr, tc = _pick(rows, 1024, 8), _pick(cdim, 512)
        per_half = (r // 2) // tr
    emit_g = g_transposed or halves

    def body(w_ref, g_ref, *rest):
        m_ref, v_ref = rest[halves : halves + 2]
        outs = rest[halves + 2 + (after is not None) :]
        gv = g_ref[...]
        if halves:
            along = pl.program_id(1 if g_transposed else 0)
            gv = jnp.where(along // per_half == lax.axis_index("c"), gv, rest[0][...])
        if g_transposed:
            gv = gv.T
        if emit_g:
            outs[0][...] = gv
        _adamw_update(w_ref[...], gv, m_ref[...], v_ref[...], *outs[-3:])

    spec = pl.BlockSpec((tr, tc), lambda i, j: (i, j))
    if g_transposed:
        g_spec = pl.BlockSpec((tc, tr), lambda i, j: (j % per_half if halves else j, i))
    else:
        g_spec = pl.BlockSpec((tr, tc), lambda i, j: (i % per_half if halves else i, j))
    n_out = 3 + emit_g
    token_spec = [] if after is None else [pl.BlockSpec(after.shape, lambda i, j: (0, 0))]
    res = pl.pallas_call(
        body,
        name=name,
        grid=(r // tr, cdim // tc),
        in_specs=[spec, g_spec] + [g_spec] * halves + [spec, spec] + token_spec,
        out_specs=[spec] * n_out,
        out_shape=[jax.ShapeDtypeStruct((r, cdim), F32)] * n_out,
        compiler_params=_params(("parallel", "parallel")),
    )(w, g, *([g_sibling] if halves else []), m, v, *([] if after is None else [after]))
    return res if emit_g else [g, *res]


def _place():
    return lax.axis_index("x"), lax.axis_index("y"), lax.axis_index("c")


def _remote(src, dst, send_sem, recv_sem, dev):
    return pltpu.make_async_remote_copy(src_ref=src, dst_ref=dst, send_sem=send_sem, recv_sem=recv_sem, device_id=dev, device_id_type=MESH)


ANY = pl.BlockSpec(memory_space=pl.ANY)


def _all_gather_small(v, name, after=()):
    r, w = v.shape

    def body(v_ref, *rest):
        o_ref, send, recv, lsem = rest[len(after) :]
        x, y, c = _place()
        me = 4 * x + 2 * y + c
        mine = pltpu.make_async_copy(v_ref, o_ref.at[me], lsem)
        mine.start()
        sent = []
        for k in range(1, 8):
            px, py, pc = x ^ (k >> 2), y ^ ((k >> 1) & 1), c ^ (k & 1)
            cp = _remote(v_ref, o_ref.at[me], send.at[k - 1], recv.at[k - 1], (px, py, pc))
            cp.start()
            sent.append(cp)
        for k in range(1, 8):
            px, py, pc = x ^ (k >> 2), y ^ ((k >> 1) & 1), c ^ (k & 1)
            slot = o_ref.at[4 * px + 2 * py + pc]
            _remote(slot, slot, send.at[k - 1], recv.at[k - 1], (x, y, c)).wait_recv()
        for cp in sent:
            cp.wait_send()
        mine.wait()

    return pl.pallas_call(
        body,
        name=name,
        out_shape=jax.ShapeDtypeStruct((8, r, w), F32),
        in_specs=[pl.BlockSpec(memory_space=pltpu.VMEM)] + [ANY] * len(after),
        out_specs=pl.BlockSpec(memory_space=pltpu.VMEM),
        scratch_shapes=[pltpu.SemaphoreType.DMA((7,)), pltpu.SemaphoreType.DMA((7,)), pltpu.SemaphoreType.DMA],
        compiler_params=pltpu.CompilerParams(vmem_limit_bytes=VMEM_LIMIT),
    )(v, *after)


HBM = pl.BlockSpec(memory_space=pltpu.HBM)
SEM = pl.BlockSpec(memory_space=pltpu.SEMAPHORE)
EFFECT = pltpu.SideEffectType.DATAFLOW_SIDE_EFFECTING


def _other_chips(x, y):
    return [(1 - x, y), (x, 1 - y), (1 - x, 1 - y)]


def _bulk_start(name, srcs, land_shapes, n_copies, copies, after, lands_init=None):
    n, m = len(srcs), len(land_shapes)

    def body(*refs):
        src_refs, land_refs = refs[:n], refs[n : n + m]
        send, recv = refs[n + m + 1], refs[n + m + 2]
        token = refs[-1]
        for k, (s, d, dev) in enumerate(copies(src_refs, land_refs)):
            _remote(s, d, send.at[k], recv.at[k], dev).start()
        token[...] = jnp.zeros_like(token)

    lands = lands_init or [lax.empty(s.shape, s.dtype) for s in land_shapes]
    lands = [pltpu.with_memory_space_constraint(b, pltpu.HBM) for b in lands]
    out = pl.pallas_call(
        body,
        name=name,
        out_shape=(pltpu.SemaphoreType.DMA((n_copies,)), pltpu.SemaphoreType.DMA((n_copies,)),
                   *[pltpu.HBM(s.shape, s.dtype) for s in srcs], *[pltpu.HBM(s.shape, s.dtype) for s in land_shapes],
                   jax.ShapeDtypeStruct((8, LANES), F32)),
        in_specs=[HBM] * (n + m) + [ANY],
        out_specs=(SEM, SEM, *[HBM] * (n + m), pl.BlockSpec(memory_space=pltpu.VMEM)),
        input_output_aliases={i: 2 + i for i in range(n + m)},
        compiler_params=pltpu.CompilerParams(has_side_effects=EFFECT),
    )(*[pltpu.with_memory_space_constraint(s, pltpu.HBM) for s in srcs], *lands, after)
    return out[0], out[1], list(out[2 : 2 + n]), list(out[2 + n : 2 + n + m]), out[-1][0:1, 0:1]


def _bulk_wait(name, send, recv, srcs, lands, after, waits):
    n, m = len(srcs), len(lands)

    def body(*refs):
        src_refs, land_refs = refs[:n], refs[n : n + m]
        send_sem, recv_sem = refs[n + m], refs[n + m + 1]
        x, y, c = _place()
        for k, (s, d) in enumerate(waits(src_refs, land_refs)):
            cp = _remote(s, d, send_sem.at[k], recv_sem.at[k], (x, y, c))
            cp.wait_send()
            cp.wait_recv()

    out = pl.pallas_call(
        body,
        name=name,
        out_shape=tuple(pltpu.HBM(s.shape, s.dtype) for s in (*srcs, *lands)),
        in_specs=[HBM] * (n + m) + [SEM, SEM, ANY],
        out_specs=tuple([HBM] * (n + m)),
        input_output_aliases={i: i for i in range(n + m)},
        compiler_params=pltpu.CompilerParams(has_side_effects=EFFECT),
    )(*srcs, *lands, send, recv, after)
    return list(out[:n]), list(out[n:])


def _peers(x, y, c):
    return [(x ^ (k >> 2), y ^ ((k >> 1) & 1), c ^ (k & 1)) for k in range(1, 8)]


def _small_gather_start(v, after, name):
    r, w = v.shape

    def copies(src, land):
        x, y, c = _place()
        return [(src[0], land[0].at[4 * x + 2 * y + c], peer) for peer in _peers(x, y, c)]

    me = 4 * lax.axis_index("x") + 2 * lax.axis_index("y") + lax.axis_index("c")
    init = [lax.dynamic_update_slice(lax.empty((8, r, w), F32), v[None], (me, 0, 0))]
    return _bulk_start(name, [v], [jax.ShapeDtypeStruct((8, r, w), F32)], 7, copies, after, init)


def _small_gather_wait(started, after, name):
    send, recv, srcs, lands, _ = started

    def waits(src, land):
        x, y, c = _place()
        return [(src[0], land[0].at[4 * px + 2 * py + pc]) for px, py, pc in _peers(x, y, c)]

    return _bulk_wait(name, send, recv, srcs, lands, after, waits)[1][0]


def _gather_start(shards, after, name):
    def copies(src, land):
        x, y, c = _place()
        j = 2 * x + y
        return [(src[a].at[c], land[a].at[j, c], (px, py, c)) for a in range(len(shards)) for px, py in _other_chips(x, y)]

    shapes = [jax.ShapeDtypeStruct((4,) + s.shape, s.dtype) for s in shards]
    j = 2 * lax.axis_index("x") + lax.axis_index("y")
    init = [lax.dynamic_update_slice(lax.empty(t.shape, t.dtype), s[None], (j, 0, 0, 0)) for t, s in zip(shapes, shards)]
    return _bulk_start(name, shards, shapes, 3 * len(shards), copies, after, init)


def _gather_wait(started, after, name):
    send, recv, srcs, lands, _ = started

    def waits(src, land):
        x, y, c = _place()
        return [(src[a].at[c], land[a].at[2 * px + py, c]) for a in range(len(srcs)) for px, py in _other_chips(x, y)]

    return _bulk_wait(name, send, recv, srcs, lands, after, waits)


def _forward_start(lands, after, name):
    def copies(src, _):
        x, y, c = _place()
        blocks = [src[a].at[2 * px + py, c] for a in range(len(lands)) for px, py in _other_chips(x, y)]
        return [(b, b, (x, y, 1 - c)) for b in blocks]

    return _bulk_start(name, lands, [], 3 * len(lands), copies, after)


def _forward_wait(started, after, name):
    send, recv, bufs, _, _ = started

    def waits(src, _):
        x, y, c = _place()
        return [(src[a].at[2 * px + py, c], src[a].at[2 * px + py, 1 - c]) for a in range(len(bufs)) for px, py in _other_chips(x, y)]

    return _bulk_wait(name, send, recv, bufs, [], after, waits)[0]


def _as_rows(lands):
    return [f.reshape(4 * f.shape[2] * 2, f.shape[3]) for f in lands]


def _gather_land(started, after, tag):
    shards, lands = _gather_wait(started, after, "gather_wait_" + tag)
    return shards, _forward_start(lands, shards[0], "forward_start_" + tag)


def _gather_done(landed, after, tag):
    _, fwd = landed
    return _as_rows(_forward_wait(fwd, after, "forward_wait_" + tag))


def _swap_halves(grads, name):
    n = len(grads)

    def body(*refs):
        ins, outs = refs[:n], refs[n : 2 * n]
        send, recv = refs[2 * n :]
        x, y, c = _place()
        started = []
        for a in range(n):
            for s in range(4):
                cp = _remote(ins[a].at[s, 1 - c], outs[a].at[s], send.at[4 * a + s], recv.at[4 * a + s], (x, y, 1 - c))
                cp.start()
                started.append(cp)
        for cp in started:
            cp.wait_recv()
        for cp in started:
            cp.wait_send()

    return pl.pallas_call(
        body,
        name=name,
        out_shape=[jax.ShapeDtypeStruct((4,) + g.shape[2:], g.dtype) for g in grads],
        in_specs=[ANY] * n,
        out_specs=[ANY] * n,
        scratch_shapes=[pltpu.SemaphoreType.DMA((4 * n,)), pltpu.SemaphoreType.DMA((4 * n,))],
    )(*grads)


def _add_halves(grads, others, tag):
    outs = []
    for a, (g, o) in enumerate(zip(grads, others)):
        _, _, rh, cdim = g.shape
        tr = _pick(rh, 512, 16)

        def body(g_ref, o_ref, p_ref):
            p_ref[...] = (g_ref[...].astype(F32) + o_ref[...].astype(F32)).astype(BF16)

        outs.append(
            pl.pallas_call(
                body,
                name=f"add_halves_{tag}{a}",
                grid=(4, rh // tr),
                in_specs=[pl.BlockSpec((None, None, tr, cdim), lambda s, i: (s, lax.axis_index("c"), i, 0)),
                          pl.BlockSpec((None, tr, cdim), lambda s, i: (s, i, 0))],
                out_specs=pl.BlockSpec((None, tr, cdim), lambda s, i: (s, i, 0)),
                out_shape=jax.ShapeDtypeStruct((4, rh, cdim), BF16),
                compiler_params=_params(("parallel", "parallel")),
            )(g, o)
        )
    return outs


def _exchange_start(parts, after, name):
    def copies(src, land):
        x, y, c = _place()
        j = 2 * x + y
        return [(src[a].at[2 * px + py], land[a].at[j], (px, py, c)) for a in range(len(parts)) for px, py in _other_chips(x, y)]

    return _bulk_start(name, parts, [jax.ShapeDtypeStruct(p.shape, p.dtype) for p in parts], 3 * len(parts), copies, after)


def _exchange_finish(started, after, name):
    send, recv, srcs, lands, _ = started

    def waits(src, land):
        x, y, _ = _place()
        return [(src[a].at[2 * px + py], land[a].at[2 * px + py]) for a in range(len(srcs)) for px, py in _other_chips(x, y)]

    srcs, lands = _bulk_wait(name, send, recv, srcs, lands, after, waits)
    return lands, srcs


def _sum_chips(recvd, parts, tag):
    outs = []
    for a, (g, p) in enumerate(zip(recvd, parts)):
        _, rh, cdim = g.shape
        tr = _pick(rh, 512, 16)

        def body(g_ref, p_ref, o_ref):
            j = 2 * lax.axis_index("x") + lax.axis_index("y")
            own = p_ref[...].astype(F32)
            term = [jnp.where(j == s, own, g_ref[s].astype(F32)) for s in range(4)]
            o_ref[...] = ((term[0] + term[1]) + term[2]) + term[3]

        outs.append(
            pl.pallas_call(
                body,
                name=f"sum_chips_{tag}{a}",
                grid=(rh // tr,),
                in_specs=[pl.BlockSpec((4, tr, cdim), lambda i: (0, i, 0)),
                          pl.BlockSpec((None, tr, cdim), lambda i: (2 * lax.axis_index("x") + lax.axis_index("y"), i, 0))],
                out_specs=pl.BlockSpec((tr, cdim), lambda i: (i, 0)),
                out_shape=jax.ShapeDtypeStruct((rh, cdim), F32),
                compiler_params=_params(("parallel",)),
            )(g, p)
        )
    return outs


def _joined(mine, other):
    first = lax.axis_index("c") == 0
    return jnp.concatenate([jnp.where(first, mine, other), jnp.where(first, other, mine)], axis=0)


def _grad_views(grads):
    return [g.reshape(4, 2, g.shape[0] // 8, g.shape[1]) for g in grads]


def _scatter_start(grads, tag, after=None):
    views = _grad_views(grads)
    others = _swap_halves(views, "swap_halves_" + tag)
    mine = _add_halves(views, others, tag)
    return _exchange_start(mine, others[-1] if after is None else after, "exchange_start_" + tag)


def _swap_start(grads, after, tag):
    views = _grad_views(grads)

    def copies(src, land):
        x, y, c = _place()
        return [(src[a].at[s, 1 - c], land[a].at[s], (x, y, 1 - c)) for a in range(len(views)) for s in range(4)]

    shapes = [jax.ShapeDtypeStruct((4,) + v.shape[2:], v.dtype) for v in views]
    return _bulk_start("swap_start_" + tag, views, shapes, 4 * len(views), copies, after)


def _scatter_start_after_swap(swapped, after, tag):
    send, recv, views, lands, _ = swapped

    def waits(src, land):
        c = lax.axis_index("c")
        return [(src[a].at[s, 1 - c], land[a].at[s]) for a in range(len(views)) for s in range(4)]

    views, others = _bulk_wait("swap_wait_" + tag, send, recv, views, lands, after, waits)
    mine = _add_halves(views, others, tag)
    return _exchange_start(mine, others[-1], "exchange_start_" + tag)


def _join_start(halves, after, tag):
    def copies(src, land):
        x, y, c = _place()
        return [(src[a], land[a], (x, y, 1 - c)) for a in range(len(halves))]

    return _bulk_start("join_start_" + tag, halves, [jax.ShapeDtypeStruct(h.shape, h.dtype) for h in halves], len(halves), copies, after)


def _join_wait(started, after, tag):
    send, recv, halves, lands, _ = started
    halves, others = _bulk_wait("join_wait_" + tag, send, recv, halves, lands, after, lambda src, land: list(zip(src, land)))
    return list(zip(halves, others))


def _scatter_sums(started, after, tag):
    return _sum_chips(*_exchange_finish(started, after, "exchange_wait_" + tag), tag)


def _t_bf16(w):
    return w.T.astype(BF16)


def kernel(x, c, ctx, c_ctx, w_ada, b_ada, norm1_g, w_in, mla_q_norm_g, w_q_up, mla_kv_norm_g, w_kv_up, gqa_q_norm_g, gqa_k_norm_g, w_br_a, w_br_b, w_out, norm2_g, w_up, conv_w, conv_b, w_down, final_norm_g, loss_target, m_c_ctx, m_w_ada, m_b_ada, m_norm1_g, m_w_in, m_mla_q_norm_g, m_w_q_up, m_mla_kv_norm_g, m_w_kv_up, m_gqa_q_norm_g, m_gqa_k_norm_g, m_w_br_a, m_w_br_b, m_w_out, m_norm2_g, m_w_up, m_conv_w, m_conv_b, m_w_down, m_final_norm_g, v_c_ctx, v_w_ada, v_b_ada, v_norm1_g, v_w_in, v_mla_q_norm_g, v_w_q_up, v_mla_kv_norm_g, v_w_kv_up, v_gqa_q_norm_g, v_gqa_k_norm_g, v_w_br_a, v_w_br_b, v_w_out, v_norm2_g, v_w_up, v_conv_w, v_conv_b, v_w_down, v_final_norm_g):
    T, D = x.shape[1], x.shape[2]
    C = ctx.shape[1]
    NA = w_ada.shape[2]
    NW = w_up.shape[2]
    F2 = 4 * NW
    FF = F2 // 2
    xi, yi, ci = _place()
    j = 2 * xi + yi
    me = 4 * xi + 2 * yi + ci
    tr = _pick(C, 256, 8)

    x2d, tgt, ctx2d = x[0], loss_target[0], ctx[0]
    fg = final_norm_g.reshape(1, D)
    cc = c_ctx.reshape(1, D)

    halve = lambda s: s.reshape(2, s.shape[0] // 2, s.shape[1])
    win_shard = halve(_t_bf16(w_in[0]))
    w0 = max(D, NW)
    pay = jnp.zeros((8, w0), F32).at[0:1, :D].set(c).at[1:4, :NW].set(conv_w[0])
    got = _all_gather_small(pay, "gather_cond")
    ag_in = _gather_start([win_shard], got, "gather_start_in")
    t_in = ag_in[4]
    c_all = got[:, 0, :D]
    cw = jnp.concatenate([got[2 * s, 1:4, :NW] for s in range(4)], axis=1)
    s16 = jnp.concatenate([c_all, cc, jnp.zeros((7, D), F32)], axis=0) + t_in
    b_cols = lax.dynamic_slice(b_ada, (0, j * NA), (1, NA))
    ada_part = _mm(s16, w_ada[0], "NN", F32, "ada_fwd", act="silu", bias=b_cols)

    wq3 = (w_q_up[0] + t_in).reshape(MLA_Q_LORA, 2, MLA_NOPE + MLA_ROPE)
    wq_perm = jnp.concatenate([wq3[:, :, :MLA_NOPE].reshape(MLA_Q_LORA, -1), wq3[:, :, MLA_NOPE:].reshape(MLA_Q_LORA, -1)], axis=1)
    low = [halve(_t_bf16(wq_perm)), halve(_t_bf16(w_kv_up[0] + t_in))]
    br = [halve(_t_bf16(w_br_a[0] + t_in)), halve(_t_bf16(w_br_b[0] + t_in)), halve((w_out[0] + t_in).astype(BF16))]
    up = [halve(_t_bf16(w_up[0] + t_in))]
    down = [halve((w_down[0] + t_in).astype(BF16))]

    got = _all_gather_small(ada_part, "gather_ada", after=(*low, *br, *up, *down))
    ada = jnp.concatenate([got[2 * s] for s in range(4)], axis=1)
    lat = lax.dynamic_slice(ada, (me, 0), (1, 6 * D))
    sh1, sc1, g1, sh2, sc2, g2 = [lat[:, k * D : (k + 1) * D] for k in range(6)]
    csh, csc = ada[8:9, :D], ada[8:9, D : 2 * D]
    ag_low = _gather_start(low, got, "gather_start_low")
    ag_br = _gather_start(br, ag_low[4], "gather_start_br")
    ag_up = _gather_start(up, ag_br[4], "gather_start_up")
    ag_down = _gather_start(down, ag_up[4], "gather_start_down")
    sh1 = sh1 + ag_down[4]

    cos_a, ss_a = _rope_tables(C, T, MLA_ROPE)
    cos_b, ss_b = _rope_tables(C, T, GQA_HEAD_DIM)
    lcos_a, lss_a, lcos_b, lss_b = cos_a[:T], ss_a[:T], cos_b[:T], ss_b[:T]

    in_landed = _gather_land(ag_in, down[0], "in")
    z_all = _norm_mod_fwd(x2d, norm1_g, sh1 + in_landed[1][4], sc1, "norm1_lat_fwd", tr, out_rows=T + C)
    z_all = _norm_mod_fwd(ctx2d, norm1_g, csh, csc, "norm1_ctx_fwd", tr, base=z_all, out_off=T)
    (win_t,) = _gather_done(in_landed, z_all, "in")
    kv_cols = KVP - LANES + MLA_ROPE
    e_kpe = MLA_KV_LORA + MLA_ROPE
    w_kvp = jnp.concatenate([win_t[:MLA_KV_LORA], win_t[e_kpe:kv_cols], win_t[MLA_KV_LORA:e_kpe], jnp.zeros((LANES - MLA_ROPE, D), BF16)], axis=0)

    pkv = _mm(z_all, w_kvp, "NT", F32, "proj_kv", tn=KVP)
    pq = _mm(z_all, win_t, "NT", F32, "proj_q", m=T, n=QC, b_off=kv_cols)
    low_landed = _gather_land(ag_low, pq, "low")
    pg = _mm(z_all, win_t, "NT", BF16, "proj_g", m=T, n=2 * D, b_off=kv_cols + QC, after=low_landed[1][4])
    wq_t, wkv_t = _gather_done(low_landed, pg, "low")
    ckv_n, kb2, vb2, kpe2 = _kprep_fwd(pkv, mla_kv_norm_g, gqa_k_norm_g, cos_a, ss_a, cos_b, ss_b, tr)
    kv_up = _mm(ckv_n, wkv_t, "NT", BF16, "kv_up")
    cq_n, qb2 = _qprep_fwd(pq, mla_q_norm_g, gqa_q_norm_g, lcos_b, lss_b, tr)
    q_a = _mm(cq_n, wq_t, "NT", F32, "q_up")
    qar = _qrope_fwd(q_a, lcos_a, lss_a, tr)

    a_q = [(qar, lambda h: 3 * (h // 2) + h % 2), (qar, lambda h: 3 * (h // 2) + 2)]
    a_k = [(kv_up, lambda h: 2 * h), (kpe2, lambda h: h % 2)]
    a_v = (kv_up, lambda h: 2 * h + 1)
    a_scale = float(MLA_NOPE + MLA_ROPE) ** -0.5
    b_q = [(qb2, lambda h: h)]
    b_k = [(kb2, lambda h: h)]
    b_v = (vb2, lambda h: h)
    b_scale = float(GQA_HEAD_DIM) ** -0.5
    tq_f = _pick(T, 2048)
    o_a, lse_a = _attn_fwd(a_q, a_k, a_v, MLA_HEADS, 1, MLA_V, a_scale, "attn_a_fwd", tq_f)
    br_landed = _gather_land(ag_br, o_a, "br")
    o_b, lse_b = _attn_fwd(b_q, b_k, b_v, GQA_HEADS, GQA_GROUP, GQA_HEAD_DIM, b_scale, "attn_b_fwd", tq_f, after=br_landed[1][4])
    wbra_t, wbrb_t, wout = _gather_done(br_landed, o_b, "br")
    up_landed = _gather_land(ag_up, o_b, "up")
    ya = _mm(o_a, wbra_t, "NT", BF16, "br_a", after=up_landed[1][4])
    yb = _mm(o_b, wbrb_t, "NT", BF16, "br_b")
    merged = _gates_fwd(pg, ya, yb, tr)
    att = _mm(merged, wout, "NN", F32, "out_proj")
    x1, z2 = _resid_norm2_fwd(x2d, att, g1, norm2_g, sh2, sc2, tr)
    (wup_t,) = _gather_done(up_landed, z2, "up")
    down_landed = _gather_land(ag_down, z2, "down")
    tc = _pick(FF, 128)
    u_a, u_b, hg = _ffn_up_conv(z2, wup_t, cw, conv_b, tc, down_landed[1][4])
    (wdown,) = _gather_done(down_landed, hg, "down")
    f = _mm(hg, wdown, "NN", F32, "ffn_down", tk=FF // 2)
    sq, dx2, d_fg, d_g2, df = _loss_head(x1, f, g2, fg, tgt, tr)
    loss_part = (0.5 * jnp.sum(sq) / D).reshape(1, 1)

    du_a, du_b, dcw_a, dcw_b, dcb_a, dcb_b = _ffn_down_dx_conv_bwd(df, wdown, u_a, u_b, cw, conv_b, _pick(FF, 256), loss_part)
    g_wdown = _mm(hg, df, "TN", BF16, "ffn_down_dw", tm=FF // 4)
    dz2 = _mm(du_a, wup_t, "NN", F32, "ffn_up_dx_a", tk=FF // 2)
    dz2 = _mm(du_b, wup_t, "NN", F32, "ffn_up_dx_b", b_off=FF, add=dz2, tk=FF // 2)
    g_wup_t = _mm(du_a, z2, "TN", BF16, "ffn_up_dw_a", out_rows=F2, tm=FF // 4)
    g_wup_t = _mm(du_b, z2, "TN", BF16, "ffn_up_dw_b", out_base=g_wup_t, out_off=FF, tm=FF // 4)
    sw_ffn = _swap_start([g_wdown, g_wup_t], sc2, "ffn")
    sc2 = sc2 + sw_ffn[4]
    dx1, datt, d_n2g, d_sh2, d_sc2, d_g1 = _resid_norm2_bwd(dz2, x1, dx2, att, norm2_g, sc2, g1, tr)

    dmerged = _mm(datt, wout, "NT", BF16, "out_proj_dx")
    rs_ffn = _scatter_start_after_swap(sw_ffn, dmerged, "ffn")
    lse_a = lse_a + rs_ffn[4]
    g_wout = _mm(merged, datt, "TN", BF16, "out_proj_dw")
    dya, dyb, dpg = _gates_bwd(dmerged, pg, ya, yb, tr)
    do_a = _mm(dya, wbra_t, "NN", BF16, "br_a_dx")
    g_wbra_t = _mm(dya, o_a, "TN", BF16, "br_a_dw")
    do_b = _mm(dyb, wbrb_t, "NN", BF16, "br_b_dx")
    g_wbrb_t = _mm(dyb, o_b, "TN", BF16, "br_b_dw")
    dqa2, dka2, dva2 = _attn_bwd(a_q, a_k, a_v, o_a, do_a, lse_a, MLA_HEADS, 1, MLA_V, a_scale, "attn_a_bwd", tq_f)
    dqb2, dkb2, dvb2 = _attn_bwd(b_q, b_k, b_v, o_b, do_b, lse_b, GQA_HEADS, GQA_GROUP, GQA_HEAD_DIM, b_scale, "attn_b_bwd", tq_f)
    dq_a = _qrope_bwd(dqa2, lcos_a, lss_a, tr)
    dcq_n = _mm(dq_a, wq_t, "NN", F32, "q_up_dx")
    g_wq_t = _mm(dq_a, cq_n, "TN", BF16, "q_up_dw")
    dpq, d_qg, d_gq = _qprep_bwd(pq, dcq_n, dqb2, mla_q_norm_g, gqa_q_norm_g, lcos_b, lss_b, tr)
    dkv_up, dkpe = _kgrad_split(dka2, dva2, cos_a, ss_a, tr)
    dckv_n = _mm(dkv_up, wkv_t, "NN", F32, "kv_up_dx")
    g_wkv_t = _mm(dkv_up, ckv_n, "TN", BF16, "kv_up_dw")
    sw_mix = _swap_start([g_wq_t, g_wkv_t, g_wbra_t, g_wbrb_t, g_wout], mla_kv_norm_g, "mix")
    dpkv, d_kvg, d_kg = _kprep_bwd(pkv, dckv_n, dkb2, dvb2, dkpe, mla_kv_norm_g + sw_mix[4], gqa_k_norm_g, cos_b, ss_b, tr)
    dz_kv = _mm(dpkv, w_kvp, "NN", F32, "proj_kv_dx")
    rs_mix = _scatter_start_after_swap(sw_mix, dz_kv, "mix")
    dz_lat = _mm(dpq, win_t, "NN", F32, "proj_q_dx", b_off=kv_cols, add=dz_kv, after=rs_mix[4])
    dz_lat = _mm(dpg, win_t, "NN", F32, "proj_g_dx", b_off=kv_cols + QC, add=dz_lat)
    _, d_n1g_c, d_csh, d_csc = _norm_mod_bwd(dz_kv, T // tr, ctx2d, norm1_g, csc, None, "norm1_ctx_bwd", tr)
    grad_x, d_n1g_l, d_sh1, d_sc1 = _norm_mod_bwd(dz_lat, 0, x2d, norm1_g, sc1, dx1, "norm1_lat_bwd", tr)

    zeros_d = jnp.zeros((1, D), F32)
    d_lat = jnp.concatenate([d_sh1, d_sc1, d_g1, d_sh2, d_sc2, d_g2], axis=1)
    d_ctx_part = jnp.concatenate([d_csh, d_csc], axis=1)
    flat = jnp.concatenate(
        [d_n1g_c + d_n1g_l, d_qg, d_kvg, d_gq, d_kg, d_n2g, dcb_a, dcb_b, d_fg,
         dcw_a.reshape(1, -1), dcw_b.reshape(1, -1), d_ctx_part, d_lat, loss_part], axis=1)
    n_flat = flat.shape[1]
    n_rows = -(-n_flat // (8 * LANES)) * 8
    flat = jnp.pad(flat, ((0, 0), (0, n_rows * LANES - n_flat))).reshape(n_rows, LANES)
    small = _small_gather_start(flat, grad_x, "small_grads_start")

    g_kvp = _mm(dpkv, z_all, "TN", BF16, "proj_kv_dw", after=small[4], tm=KVP)
    nk = MLA_KV_LORA + 2 * GQA_KV_HEADS * GQA_HEAD_DIM
    g_kv = jnp.concatenate([g_kvp[:MLA_KV_LORA], g_kvp[nk : nk + MLA_ROPE], g_kvp[MLA_KV_LORA:nk]], axis=0)
    g_win_t = _mm(dpq, z_all, "TN", BF16, "proj_q_dw", out_rows=kv_cols + QC + 2 * D, out_off=kv_cols, tm=QC // 2)
    g_win_t = _mm(dpg, z_all, "TN", BF16, "proj_g_dw", out_base=g_win_t, out_off=kv_cols + QC)
    g_win_t = lax.dynamic_update_slice(g_win_t, g_kv, (0, 0))

    got = _small_gather_wait(small, g_win_t, "small_grads_wait")
    tot = _sum_slots(got, "sum_small_grads").reshape(1, -1)
    sizes = [D, MLA_Q_LORA, MLA_KV_LORA, GQA_HEAD_DIM, GQA_HEAD_DIM, D, F2, D, 3 * FF, 3 * FF, 2 * D]
    offs = [0]
    for s in sizes:
        offs.append(offs[-1] + s)
    t_n1g, t_qg, t_kvg, t_gq, t_kg, t_n2g, t_cb, t_fg, t_cwa, t_cwb, t_ctx = [tot[:, offs[k] : offs[k + 1]] for k in range(len(sizes))]
    loss = tot[0, offs[-1] + 6 * D]
    g_cw_full = jnp.concatenate([t_cwa.reshape(3, FF), t_cwb.reshape(3, FF)], axis=1)
    g_cw = lax.dynamic_slice(g_cw_full, (0, j * NW), (3, NW))
    d_lat_all = got.reshape(8, -1)[:, offs[-1] : offs[-1] + 6 * D]
    g16 = jnp.concatenate([d_lat_all, jnp.pad(t_ctx, ((0, 0), (0, 4 * D))), jnp.zeros((7, 6 * D), F32)], axis=0)
    g_b_ada = _sum_slots(g16.reshape(16, 1, 6 * D), "sum_b_ada")
    g16_cols = lax.dynamic_slice(g16, (0, j * NA), (16, NA))
    ds_part = _mm(g16_cols, w_ada[0], "NT", F32, "ada_dx")
    ada_dx = _small_gather_start(ds_part[8:16], got, "ada_dx_start")
    sw_in = _swap_start([g_win_t], ada_dx[4], "in")

    h_ffn = _scatter_sums(rs_ffn, sw_in[2][0], "ffn")
    got = _small_gather_wait(ada_dx, h_ffn[0], "ada_dx_wait")
    ds_ctx = _sum_slots(jnp.stack([got[2 * s] for s in range(4)]), "sum_ada_dx")[0:1]
    g_c_ctx = _silu_grad_mul(ds_ctx, cc)
    h_mix = _scatter_sums(rs_mix, h_ffn[0], "mix")
    j_big = _join_start(h_ffn + h_mix, grad_x, "big")
    rs_in = _scatter_start_after_swap(sw_in, j_big[2][0], "in")
    g_w_ada = _mm(s16, g16_cols, "TN", F32, "ada_dw", act="silu", after=rs_in[4])
    _, d_ada, m_ada, v_ada = _adamw(w_ada[0], g_w_ada, m_w_ada[0], v_w_ada[0], "adamw_w_ada")
    r_wdown, r_wup, r_wq, r_wkv, r_wbra, r_wbrb, r_wout = _join_wait(j_big, d_ada, "big")
    gq_p = _joined(*r_wq).T
    gq = jnp.concatenate([gq_p[:, : 2 * MLA_NOPE].reshape(MLA_Q_LORA, 2, MLA_NOPE), gq_p[:, 2 * MLA_NOPE :].reshape(MLA_Q_LORA, 2, MLA_ROPE)], axis=2)
    grads = {
        "c_ctx": g_c_ctx.reshape(D), "w_ada": g_w_ada[None], "b_ada": g_b_ada, "norm1_g": t_n1g,
        "mla_q_norm_g": t_qg, "w_q_up": gq.reshape(1, MLA_Q_LORA, -1), "mla_kv_norm_g": t_kvg, "w_kv_up": r_wkv,
        "gqa_q_norm_g": t_gq, "gqa_k_norm_g": t_kg, "w_br_a": r_wbra, "w_br_b": r_wbrb, "w_out": r_wout,
        "norm2_g": t_n2g, "w_up": r_wup, "conv_w": g_cw[None], "conv_b": t_cb, "w_down": r_wdown,
        "final_norm_g": t_fg.reshape(D),
    }
    arrives_transposed = ("w_kv_up", "w_br_a", "w_br_b", "w_up")
    arrives_halved = arrives_transposed + ("w_out", "w_down")
    weights = dict(c_ctx=c_ctx, w_ada=w_ada, b_ada=b_ada, norm1_g=norm1_g, w_in=w_in, mla_q_norm_g=mla_q_norm_g, w_q_up=w_q_up,
                   mla_kv_norm_g=mla_kv_norm_g, w_kv_up=w_kv_up, gqa_q_norm_g=gqa_q_norm_g, gqa_k_norm_g=gqa_k_norm_g, w_br_a=w_br_a,
                   w_br_b=w_br_b, w_out=w_out, norm2_g=norm2_g, w_up=w_up, conv_w=conv_w, conv_b=conv_b, w_down=w_down,
                   final_norm_g=final_norm_g)
    m_in = dict(c_ctx=m_c_ctx, w_ada=m_w_ada, b_ada=m_b_ada, norm1_g=m_norm1_g, w_in=m_w_in, mla_q_norm_g=m_mla_q_norm_g,
                w_q_up=m_w_q_up, mla_kv_norm_g=m_mla_kv_norm_g, w_kv_up=m_w_kv_up, gqa_q_norm_g=m_gqa_q_norm_g,
                gqa_k_norm_g=m_gqa_k_norm_g, w_br_a=m_w_br_a, w_br_b=m_w_br_b, w_out=m_w_out, norm2_g=m_norm2_g, w_up=m_w_up,
                conv_w=m_conv_w, conv_b=m_conv_b, w_down=m_w_down, final_norm_g=m_final_norm_g)
    v_in = dict(c_ctx=v_c_ctx, w_ada=v_w_ada, b_ada=v_b_ada, norm1_g=v_norm1_g, w_in=v_w_in, mla_q_norm_g=v_mla_q_norm_g,
                w_q_up=v_w_q_up, mla_kv_norm_g=v_mla_kv_norm_g, w_kv_up=v_w_kv_up, gqa_q_norm_g=v_gqa_q_norm_g,
                gqa_k_norm_g=v_gqa_k_norm_g, w_br_a=v_w_br_a, w_br_b=v_w_br_b, w_out=v_w_out, norm2_g=v_norm2_g, w_up=v_w_up,
                conv_w=v_conv_w, conv_b=v_conv_b, w_down=v_w_down, final_norm_g=v_final_norm_g)
    names = list(weights)
    big = [n for n in names if weights[n].ndim == 3 and weights[n].shape[1] >= 8]
    small = [n for n in names if n not in big]
    delta, new_m, new_v = {}, {}, {}

    def update(n, after=None):
        shp = weights[n].shape
        two_d = lambda a: a.reshape(shp[1], shp[2])
        g_t = n in arrives_transposed
        if n in arrives_halved:
            g_in, g_sib = grads[n]
        else:
            g_in, g_sib = two_d(grads[n].astype(F32)), None
        g_, d_, m_, v_ = _adamw(two_d(weights[n]), g_in, two_d(m_in[n]), two_d(v_in[n]), "adamw_" + n, g_transposed=g_t, g_sibling=g_sib,
                                after=after)
        grads[n], delta[n], new_m[n], new_v[n] = g_.reshape(shp), d_.reshape(shp), m_.reshape(shp), v_.reshape(shp)

    delta["w_ada"], new_m["w_ada"], new_v["w_ada"] = d_ada[None], m_ada[None], v_ada[None]
    early = [n for n in big if n not in ("w_in", "w_ada")]
    for n in early[:-1]:
        update(n)
    done = sum(delta[n][0, 0:1, 0:1] for n in early[:-1])
    j_in = _join_start(_scatter_sums(rs_in, done, "in"), done, "in")
    last = early[-1]
    update(last, after=j_in[4])
    ((g_mine, g_sib),) = _join_wait(j_in, delta[last], "in")
    g_, d_, m_, v_ = _adamw(w_in[0].T, g_mine, m_w_in[0].T, v_w_in[0].T, "adamw_w_in", g_sibling=g_sib)
    grads["w_in"], delta["w_in"], new_m["w_in"], new_v["w_in"] = g_.T[None], d_.T[None], m_.T[None], v_.T[None]
    grads = {n: grads[n].reshape(weights[n].shape).astype(F32) for n in names}

    slab = lambda tree: [tree[n].reshape(-1, LANES) for n in small]
    d_, m_, v_ = _adamw_many(slab(weights), slab(grads), slab(m_in), slab(v_in), "adamw_small")
    for k, n in enumerate(small):
        shp = weights[n].shape
        delta[n], new_m[n], new_v[n] = d_[k].reshape(shp), m_[k].reshape(shp), v_[k].reshape(shp)

    return (loss, grad_x[None], *[grads[n] for n in names], *[delta[n] for n in names], *[new_m[n] for n in names],
            *[new_v[n] for n in names])
```
